```python
import math
import jax, jax.numpy as jnp
from jax import lax
import numpy as np

D_MODEL = 2048
BATCH = 8
SEQ = 4096
DEPTH = 1

CONV_WIDTH = D_MODEL // 2
SSM_WIDTH = D_MODEL // 2
CONV_KERNEL = 31
SSM_GROUP = 16
SSM_GROUPS = SSM_WIDTH // SSM_GROUP
SSM_STATE = 64
D_FF = 4 * D_MODEL
N_MOD = 6
IN_COLS = 2 * CONV_WIDTH + SSM_WIDTH + 2 * D_MODEL
EPS = 1e-6
DT_MIN = 1e-3
DT_MAX = 1e-1

kernel_name = "hybrid_conformer_s5_gated_block"


def rmsnorm(x, g):
    xf = x.astype(jnp.float32)
    y = xf * lax.rsqrt(jnp.mean(jnp.square(xf), axis=-1, keepdims=True) + EPS)
    return (y * g.astype(jnp.float32)).astype(x.dtype)


def layernorm(x, g, b):
    xf = x.astype(jnp.float32)
    mu = jnp.mean(xf, axis=-1, keepdims=True)
    var = jnp.mean(jnp.square(xf - mu), axis=-1, keepdims=True)
    y = (xf - mu) * lax.rsqrt(var + EPS)
    return (y * g.astype(jnp.float32) + b.astype(jnp.float32)).astype(x.dtype)


def conformer_conv(v_glu, w_dw, b_dw, ln_g, ln_b, w_conv_out):
    a, gate = jnp.split(v_glu, 2, axis=-1)
    v = a * jax.nn.sigmoid(gate)
    v = lax.conv_general_dilated(
        v, w_dw[:, None, :], window_strides=(1,), padding=[(CONV_KERNEL - 1, 0)],
        dimension_numbers=("NWC", "WIO", "NWC"), feature_group_count=CONV_WIDTH,
    ) + b_dw
    v = jax.nn.silu(layernorm(v, ln_g, ln_b))
    return v @ w_conv_out


def s5_ssm(u, a_re, a_im, log_dt, b_re, b_im, c_re, c_im, d_skip, w_glu_a, w_glu_b):
    bsz, seq, _ = u.shape
    uf = u.astype(jnp.float32)
    ug = uf.reshape(bsz, seq, SSM_GROUPS, SSM_GROUP)
    lam = lax.complex(a_re.astype(jnp.float32), a_im.astype(jnp.float32))
    dt = jnp.exp(log_dt.astype(jnp.float32))[:, None]
    lam_bar = jnp.exp(lam * dt)
    b_c = lax.complex(b_re.astype(jnp.float32), b_im.astype(jnp.float32))
    b_bar = ((lam_bar - 1.0) / lam)[..., None] * b_c
    bu = jnp.einsum("blgh,gph->blgp", ug.astype(b_bar.dtype), b_bar)
    a_seq = jnp.broadcast_to(lam_bar, bu.shape)

    def combine(e1, e2):
        a1, s1 = e1
        a2, s2 = e2
        return a1 * a2, a2 * s1 + s2

    _, states = lax.associative_scan(combine, (a_seq, bu), axis=1)
    c_c = lax.complex(c_re.astype(jnp.float32), c_im.astype(jnp.float32))
    y = jnp.real(jnp.einsum("blgp,ghp->blgh", states, c_c))
    y = y.reshape(bsz, seq, SSM_WIDTH) + d_skip.astype(jnp.float32) * uf
    y = jax.nn.gelu(y).astype(u.dtype)
    return (y @ w_glu_a) * jax.nn.sigmoid(y @ w_glu_b)


def _fwd_setup_inputs(seed: int = 0) -> dict:
    key = jax.random.key(seed)
    ks = jax.random.split(key, 32)
    f32 = jnp.float32

    def nrm(k, shape, scale):
        return jax.random.normal(k, shape, f32) * scale

    x = jax.random.normal(ks[0], (BATCH, SEQ, D_MODEL), f32)
    c = jax.random.normal(ks[1], (BATCH, D_MODEL), f32)
    w_ada = nrm(ks[2], (DEPTH, D_MODEL, N_MOD * D_MODEL), 0.5 * D_MODEL ** -0.5)
    b_ada = nrm(ks[3], (DEPTH, N_MOD * D_MODEL), 0.02)
    norm1_g = 1.0 + nrm(ks[4], (DEPTH, D_MODEL), 0.02)
    w_in = nrm(ks[5], (DEPTH, D_MODEL, IN_COLS), D_MODEL ** -0.5)
    w_dw = nrm(ks[6], (DEPTH, CONV_KERNEL, CONV_WIDTH), CONV_KERNEL ** -0.5)
    b_dw = nrm(ks[7], (DEPTH, CONV_WIDTH), 0.02)
    ln_g = 1.0 + nrm(ks[8], (DEPTH, CONV_WIDTH), 0.02)
    ln_b = nrm(ks[9], (DEPTH, CONV_WIDTH), 0.02)
    w_conv_out = nrm(ks[10], (DEPTH, CONV_WIDTH, D_MODEL), CONV_WIDTH ** -0.5)
    n_idx = jnp.arange(SSM_STATE, dtype=f32)
    a_re = -0.5 + nrm(ks[11], (DEPTH, SSM_GROUPS, SSM_STATE), 0.01)
    a_im = math.pi * n_idx + nrm(ks[12], (DEPTH, SSM_GROUPS, SSM_STATE), 0.01)
    log_dt = jax.random.uniform(ks[13], (DEPTH, SSM_GROUPS), f32, math.log(DT_MIN), math.log(DT_MAX))
    b_scale = (2.0 * SSM_GROUP) ** -0.5
    b_re = nrm(ks[14], (DEPTH, SSM_GROUPS, SSM_STATE, SSM_GROUP), b_scale)
    b_im = nrm(ks[15], (DEPTH, SSM_GROUPS, SSM_STATE, SSM_GROUP), b_scale)
    c_scale = (2.0 * SSM_STATE) ** -0.5
    c_re = nrm(ks[16], (DEPTH, SSM_GROUPS, SSM_GROUP, SSM_STATE), c_scale)
    c_im = nrm(ks[17], (DEPTH, SSM_GROUPS, SSM_GROUP, SSM_STATE), c_scale)
    d_skip = nrm(ks[18], (DEPTH, SSM_WIDTH), 1.0)
    w_glu_a = nrm(ks[19], (DEPTH, SSM_WIDTH, D_MODEL), SSM_WIDTH ** -0.5)
    w_glu_b = nrm(ks[20], (DEPTH, SSM_WIDTH, D_MODEL), SSM_WIDTH ** -0.5)
    w_out = nrm(ks[21], (DEPTH, D_MODEL, D_MODEL), D_MODEL ** -0.5)
    norm2_g = 1.0 + nrm(ks[22], (DEPTH, D_MODEL), 0.02)
    w_ff1 = nrm(ks[23], (DEPTH, D_MODEL, D_FF), D_MODEL ** -0.5)
    w_ff2 = nrm(ks[24], (DEPTH, D_FF, D_MODEL), D_FF ** -0.5)
    final_g = 1.0 + nrm(ks[25], (D_MODEL,), 0.02)
    return {
        "x": x, "c": c, "w_ada": w_ada, "b_ada": b_ada, "norm1_g": norm1_g, "w_in": w_in,
        "w_dw": w_dw, "b_dw": b_dw, "ln_g": ln_g, "ln_b": ln_b, "w_conv_out": w_conv_out,
        "a_re": a_re, "a_im": a_im, "log_dt": log_dt, "b_re": b_re, "b_im": b_im,
        "c_re": c_re, "c_im": c_im, "d_skip": d_skip, "w_glu_a": w_glu_a, "w_glu_b": w_glu_b,
        "w_out": w_out, "norm2_g": norm2_g, "w_ff1": w_ff1, "w_ff2": w_ff2, "final_g": final_g,
    }


def _fwd_reference(x, c, w_ada, b_ada, norm1_g, w_in, w_dw, b_dw, ln_g, ln_b, w_conv_out,
              a_re, a_im, log_dt, b_re, b_im, c_re, c_im, d_skip, w_glu_a, w_glu_b,
              w_out, norm2_g, w_ff1, w_ff2, final_g):
    h = x
    c_act = jax.nn.silu(c)
    split_pts = [2 * CONV_WIDTH, 2 * CONV_WIDTH + SSM_WIDTH, 2 * CONV_WIDTH + SSM_WIDTH + D_MODEL]
    for l in range(DEPTH):
        mod = c_act @ w_ada[l] + b_ada[l]
        shift1, scale1, gate1, shift2, scale2, gate2 = [m[:, None, :] for m in jnp.split(mod, N_MOD, axis=-1)]

        u = rmsnorm(h, norm1_g[l]) * (1.0 + scale1) + shift1
        proj = u @ w_in[l]
        v_conv, v_ssm, g_conv, g_ssm = jnp.split(proj, split_pts, axis=-1)
        y_conv = conformer_conv(v_conv, w_dw[l], b_dw[l], ln_g[l], ln_b[l], w_conv_out[l])
        y_ssm = s5_ssm(v_ssm, a_re[l], a_im[l], log_dt[l], b_re[l], b_im[l], c_re[l], c_im[l],
                       d_skip[l], w_glu_a[l], w_glu_b[l])
        merged = jax.nn.sigmoid(g_conv) * y_conv + jax.nn.sigmoid(g_ssm) * y_ssm
        h = h + gate1 * (merged @ w_out[l])

        z = rmsnorm(h, norm2_g[l]) * (1.0 + scale2) + shift2
        ff = jnp.square(jax.nn.relu(z @ w_ff1[l])) @ w_ff2[l]
        h = h + gate2 * ff
    return rmsnorm(h, final_g)


import jax as _jax
import jax.numpy as _jnp

TWIN_FORMAT = 'train_step'
FWD_PARAMS = ['x', 'c', 'w_ada', 'b_ada', 'norm1_g', 'w_in', 'w_dw', 'b_dw', 'ln_g', 'ln_b', 'w_conv_out', 'a_re', 'a_im', 'log_dt', 'b_re', 'b_im', 'c_re', 'c_im', 'd_skip', 'w_glu_a', 'w_glu_b', 'w_out', 'norm2_g', 'w_ff1', 'w_ff2', 'final_g']
TWIN_WEIGHTS = ['w_ada', 'b_ada', 'norm1_g', 'w_in', 'w_dw', 'b_dw', 'ln_g', 'ln_b', 'w_conv_out', 'a_re', 'a_im', 'log_dt', 'b_re', 'b_im', 'c_re', 'c_im', 'd_skip', 'w_glu_a', 'w_glu_b', 'w_out', 'norm2_g', 'w_ff1', 'w_ff2', 'final_g']
TWIN_DIFF_INPUT = 'x'
TWIN_INPUTS = ['x', 'c', 'w_ada', 'b_ada', 'norm1_g', 'w_in', 'w_dw', 'b_dw', 'ln_g', 'ln_b', 'w_conv_out', 'a_re', 'a_im', 'log_dt', 'b_re', 'b_im', 'c_re', 'c_im', 'd_skip', 'w_glu_a', 'w_glu_b', 'w_out', 'norm2_g', 'w_ff1', 'w_ff2', 'final_g', 'loss_target', 'm_w_ada', 'm_b_ada', 'm_norm1_g', 'm_w_in', 'm_w_dw', 'm_b_dw', 'm_ln_g', 'm_ln_b', 'm_w_conv_out', 'm_a_re', 'm_a_im', 'm_log_dt', 'm_b_re', 'm_b_im', 'm_c_re', 'm_c_im', 'm_d_skip', 'm_w_glu_a', 'm_w_glu_b', 'm_w_out', 'm_norm2_g', 'm_w_ff1', 'm_w_ff2', 'm_final_g', 'v_w_ada', 'v_b_ada', 'v_norm1_g', 'v_w_in', 'v_w_dw', 'v_b_dw', 'v_ln_g', 'v_ln_b', 'v_w_conv_out', 'v_a_re', 'v_a_im', 'v_log_dt', 'v_b_re', 'v_b_im', 'v_c_re', 'v_c_im', 'v_d_skip', 'v_w_glu_a', 'v_w_glu_b', 'v_w_out', 'v_norm2_g', 'v_w_ff1', 'v_w_ff2', 'v_final_g']
TWIN_OUTPUTS = ['loss', 'grad_x', 'grad_w_ada', 'grad_b_ada', 'grad_norm1_g', 'grad_w_in', 'grad_w_dw', 'grad_b_dw', 'grad_ln_g', 'grad_ln_b', 'grad_w_conv_out', 'grad_a_re', 'grad_a_im', 'grad_log_dt', 'grad_b_re', 'grad_b_im', 'grad_c_re', 'grad_c_im', 'grad_d_skip', 'grad_w_glu_a', 'grad_w_glu_b', 'grad_w_out', 'grad_norm2_g', 'grad_w_ff1', 'grad_w_ff2', 'grad_final_g', 'delta_w_ada', 'delta_b_ada', 'delta_norm1_g', 'delta_w_in', 'delta_w_dw', 'delta_b_dw', 'delta_ln_g', 'delta_ln_b', 'delta_w_conv_out', 'delta_a_re', 'delta_a_im', 'delta_log_dt', 'delta_b_re', 'delta_b_im', 'delta_c_re', 'delta_c_im', 'delta_d_skip', 'delta_w_glu_a', 'delta_w_glu_b', 'delta_w_out', 'delta_norm2_g', 'delta_w_ff1', 'delta_w_ff2', 'delta_final_g', 'new_m_w_ada', 'new_m_b_ada', 'new_m_norm1_g', 'new_m_w_in', 'new_m_w_dw', 'new_m_b_dw', 'new_m_ln_g', 'new_m_ln_b', 'new_m_w_conv_out', 'new_m_a_re', 'new_m_a_im', 'new_m_log_dt', 'new_m_b_re', 'new_m_b_im', 'new_m_c_re', 'new_m_c_im', 'new_m_d_skip', 'new_m_w_glu_a', 'new_m_w_glu_b', 'new_m_w_out', 'new_m_norm2_g', 'new_m_w_ff1', 'new_m_w_ff2', 'new_m_final_g', 'new_v_w_ada', 'new_v_b_ada', 'new_v_norm1_g', 'new_v_w_in', 'new_v_w_dw', 'new_v_b_dw', 'new_v_ln_g', 'new_v_ln_b', 'new_v_w_conv_out', 'new_v_a_re', 'new_v_a_im', 'new_v_log_dt', 'new_v_b_re', 'new_v_b_im', 'new_v_c_re', 'new_v_c_im', 'new_v_d_skip', 'new_v_w_glu_a', 'new_v_w_glu_b', 'new_v_w_out', 'new_v_norm2_g', 'new_v_w_ff1', 'new_v_w_ff2', 'new_v_final_g']
TWIN_LEAF_KINDS = {'loss': 'loss', 'grad_x': 'grad_x', 'grad_w_ada': 'grad_w', 'grad_b_ada': 'grad_w', 'grad_norm1_g': 'grad_w', 'grad_w_in': 'grad_w', 'grad_w_dw': 'grad_w', 'grad_b_dw': 'grad_w', 'grad_ln_g': 'grad_w', 'grad_ln_b': 'grad_w', 'grad_w_conv_out': 'grad_w', 'grad_a_re': 'grad_w', 'grad_a_im': 'grad_w', 'grad_log_dt': 'grad_w', 'grad_b_re': 'grad_w', 'grad_b_im': 'grad_w', 'grad_c_re': 'grad_w', 'grad_c_im': 'grad_w', 'grad_d_skip': 'grad_w', 'grad_w_glu_a': 'grad_w', 'grad_w_glu_b': 'grad_w', 'grad_w_out': 'grad_w', 'grad_norm2_g': 'grad_w', 'grad_w_ff1': 'grad_w', 'grad_w_ff2': 'grad_w', 'grad_final_g': 'grad_w', 'delta_w_ada': 'delta_w', 'delta_b_ada': 'delta_w', 'delta_norm1_g': 'delta_w', 'delta_w_in': 'delta_w', 'delta_w_dw': 'delta_w', 'delta_b_dw': 'delta_w', 'delta_ln_g': 'delta_w', 'delta_ln_b': 'delta_w', 'delta_w_conv_out': 'delta_w', 'delta_a_re': 'delta_w', 'delta_a_im': 'delta_w', 'delta_log_dt': 'delta_w', 'delta_b_re': 'delta_w', 'delta_b_im': 'delta_w', 'delta_c_re': 'delta_w', 'delta_c_im': 'delta_w', 'delta_d_skip': 'delta_w', 'delta_w_glu_a': 'delta_w', 'delta_w_glu_b': 'delta_w', 'delta_w_out': 'delta_w', 'delta_norm2_g': 'delta_w', 'delta_w_ff1': 'delta_w', 'delta_w_ff2': 'delta_w', 'delta_final_g': 'delta_w', 'new_m_w_ada': 'new_m', 'new_m_b_ada': 'new_m', 'new_m_norm1_g': 'new_m', 'new_m_w_in': 'new_m', 'new_m_w_dw': 'new_m', 'new_m_b_dw': 'new_m', 'new_m_ln_g': 'new_m', 'new_m_ln_b': 'new_m', 'new_m_w_conv_out': 'new_m', 'new_m_a_re': 'new_m', 'new_m_a_im': 'new_m', 'new_m_log_dt': 'new_m', 'new_m_b_re': 'new_m', 'new_m_b_im': 'new_m', 'new_m_c_re': 'new_m', 'new_m_c_im': 'new_m', 'new_m_d_skip': 'new_m', 'new_m_w_glu_a': 'new_m', 'new_m_w_glu_b': 'new_m', 'new_m_w_out': 'new_m', 'new_m_norm2_g': 'new_m', 'new_m_w_ff1': 'new_m', 'new_m_w_ff2': 'new_m', 'new_m_final_g': 'new_m', 'new_v_w_ada': 'new_v', 'new_v_b_ada': 'new_v', 'new_v_norm1_g': 'new_v', 'new_v_w_in': 'new_v', 'new_v_w_dw': 'new_v', 'new_v_b_dw': 'new_v', 'new_v_ln_g': 'new_v', 'new_v_ln_b': 'new_v', 'new_v_w_conv_out': 'new_v', 'new_v_a_re': 'new_v', 'new_v_a_im': 'new_v', 'new_v_log_dt': 'new_v', 'new_v_b_re': 'new_v', 'new_v_b_im': 'new_v', 'new_v_c_re': 'new_v', 'new_v_c_im': 'new_v', 'new_v_d_skip': 'new_v', 'new_v_w_glu_a': 'new_v', 'new_v_w_glu_b': 'new_v', 'new_v_w_out': 'new_v', 'new_v_norm2_g': 'new_v', 'new_v_w_ff1': 'new_v', 'new_v_w_ff2': 'new_v', 'new_v_final_g': 'new_v'}


def _forward(args):
    return _fwd_reference(*[args[k] for k in FWD_PARAMS])


def _output_shape():
    def fwd():
        inp = _fwd_setup_inputs(0)
        return _fwd_reference(*[inp[k] for k in FWD_PARAMS])
    out = _jax.eval_shape(fwd)
    return out.shape, out.dtype

N_MICROBATCH = 1
ADAM_LR = 0.001
ADAM_B1 = 0.9
ADAM_B2 = 0.999
ADAM_EPS = 1e-08
ADAM_WD = 0.01
ADAM_STEP = 10
PER_EXAMPLE_BATCH_AXIS = {'x': 0, 'c': 0, 'loss_target': 0}
SHARED_INPUTS = []
_WEIGHT_DTYPES = {'w_ada': _jnp.float32, 'b_ada': _jnp.float32, 'norm1_g': _jnp.float32, 'w_in': _jnp.float32, 'w_dw': _jnp.float32, 'b_dw': _jnp.float32, 'ln_g': _jnp.float32, 'ln_b': _jnp.float32, 'w_conv_out': _jnp.float32, 'a_re': _jnp.float32, 'a_im': _jnp.float32, 'log_dt': _jnp.float32, 'b_re': _jnp.float32, 'b_im': _jnp.float32, 'c_re': _jnp.float32, 'c_im': _jnp.float32, 'd_skip': _jnp.float32, 'w_glu_a': _jnp.float32, 'w_glu_b': _jnp.float32, 'w_out': _jnp.float32, 'norm2_g': _jnp.float32, 'w_ff1': _jnp.float32, 'w_ff2': _jnp.float32, 'final_g': _jnp.float32}
MOMENT_SCALE = {'w_ada': 3.994320e-02, 'b_ada': 7.516844e-02, 'norm1_g': 1.189404e-02, 'w_in': 6.637727e-03, 'w_dw': 1.304986e-02, 'b_dw': 2.458042e-02, 'ln_g': 1.672848e-02, 'ln_b': 1.453161e-02, 'w_conv_out': 8.938688e-03, 'a_re': 5.895284e-04, 'a_im': 5.055427e-04, 'log_dt': 2.330574e-01, 'b_re': 3.271449e-04, 'b_im': 3.373532e-04, 'c_re': 6.844372e-04, 'c_im': 6.676144e-04, 'd_skip': 8.733821e-03, 'w_glu_a': 5.634198e-03, 'w_glu_b': 1.766321e-03, 'w_out': 1.057877e-02, 'norm2_g': 3.910201e-02, 'w_ff1': 2.049348e-02, 'w_ff2': 3.845225e-02, 'final_g': 1.605123e+01}


def _to_microbatches(a, axis):
    t = _jnp.moveaxis(a, axis, 0)
    t = t.reshape((N_MICROBATCH, t.shape[0] // N_MICROBATCH) + t.shape[1:])
    return _jnp.moveaxis(t, 1, axis + 1)


def setup_inputs(seed: int = 0) -> dict:
    inp = _fwd_setup_inputs(seed)
    key = _jax.random.fold_in(_jax.random.key(seed), 7919)
    shape, _ = _output_shape()
    out = dict(inp)
    out["loss_target"] = _jax.random.normal(_jax.random.fold_in(key, 0), shape, _jnp.float32)
    for i, name in enumerate(TWIN_WEIGHTS):
        w = inp[name].astype(_jnp.float32)
        if MOMENT_SCALE is None:
            s = _jnp.sqrt(_jnp.mean(_jnp.square(w)) + 1e-30)
        else:
            s = MOMENT_SCALE[name]
        km, kv = _jax.random.split(_jax.random.fold_in(key, i + 1))
        out[name] = w
        out["m_" + name] = s * _jax.random.normal(km, w.shape, _jnp.float32)
        out["v_" + name] = (s * s) * _jax.random.uniform(kv, w.shape, _jnp.float32, 0.5, 1.5)
    if N_MICROBATCH > 1:
        for name, axis in PER_EXAMPLE_BATCH_AXIS.items():
            out[name] = _to_microbatches(out[name], axis)
    return {'x': out['x'], 'c': out['c'], 'w_ada': out['w_ada'], 'b_ada': out['b_ada'], 'norm1_g': out['norm1_g'], 'w_in': out['w_in'], 'w_dw': out['w_dw'], 'b_dw': out['b_dw'], 'ln_g': out['ln_g'], 'ln_b': out['ln_b'], 'w_conv_out': out['w_conv_out'], 'a_re': out['a_re'], 'a_im': out['a_im'], 'log_dt': out['log_dt'], 'b_re': out['b_re'], 'b_im': out['b_im'], 'c_re': out['c_re'], 'c_im': out['c_im'], 'd_skip': out['d_skip'], 'w_glu_a': out['w_glu_a'], 'w_glu_b': out['w_glu_b'], 'w_out': out['w_out'], 'norm2_g': out['norm2_g'], 'w_ff1': out['w_ff1'], 'w_ff2': out['w_ff2'], 'final_g': out['final_g'], 'loss_target': out['loss_target'], 'm_w_ada': out['m_w_ada'], 'm_b_ada': out['m_b_ada'], 'm_norm1_g': out['m_norm1_g'], 'm_w_in': out['m_w_in'], 'm_w_dw': out['m_w_dw'], 'm_b_dw': out['m_b_dw'], 'm_ln_g': out['m_ln_g'], 'm_ln_b': out['m_ln_b'], 'm_w_conv_out': out['m_w_conv_out'], 'm_a_re': out['m_a_re'], 'm_a_im': out['m_a_im'], 'm_log_dt': out['m_log_dt'], 'm_b_re': out['m_b_re'], 'm_b_im': out['m_b_im'], 'm_c_re': out['m_c_re'], 'm_c_im': out['m_c_im'], 'm_d_skip': out['m_d_skip'], 'm_w_glu_a': out['m_w_glu_a'], 'm_w_glu_b': out['m_w_glu_b'], 'm_w_out': out['m_w_out'], 'm_norm2_g': out['m_norm2_g'], 'm_w_ff1': out['m_w_ff1'], 'm_w_ff2': out['m_w_ff2'], 'm_final_g': out['m_final_g'], 'v_w_ada': out['v_w_ada'], 'v_b_ada': out['v_b_ada'], 'v_norm1_g': out['v_norm1_g'], 'v_w_in': out['v_w_in'], 'v_w_dw': out['v_w_dw'], 'v_b_dw': out['v_b_dw'], 'v_ln_g': out['v_ln_g'], 'v_ln_b': out['v_ln_b'], 'v_w_conv_out': out['v_w_conv_out'], 'v_a_re': out['v_a_re'], 'v_a_im': out['v_a_im'], 'v_log_dt': out['v_log_dt'], 'v_b_re': out['v_b_re'], 'v_b_im': out['v_b_im'], 'v_c_re': out['v_c_re'], 'v_c_im': out['v_c_im'], 'v_d_skip': out['v_d_skip'], 'v_w_glu_a': out['v_w_glu_a'], 'v_w_glu_b': out['v_w_glu_b'], 'v_w_out': out['v_w_out'], 'v_norm2_g': out['v_norm2_g'], 'v_w_ff1': out['v_w_ff1'], 'v_w_ff2': out['v_w_ff2'], 'v_final_g': out['v_final_g']}


def _loss(weights, diff, rest, loss_target):
    with _jax.named_scope("forward"):
        args = {**rest, TWIN_DIFF_INPUT: diff, **{k: w.astype(_WEIGHT_DTYPES[k]) for k, w in weights.items()}}
        y = _forward(args)
    with _jax.named_scope("loss_head"):
        err = _jnp.square(y.astype(_jnp.float32) - loss_target)
        return 0.5 * _jnp.sum(_jnp.mean(err, axis=-1)) if err.ndim else 0.5 * err


def _adamw(w, g, m, v):
    m = ADAM_B1 * m + (1.0 - ADAM_B1) * g
    v = ADAM_B2 * v + (1.0 - ADAM_B2) * _jnp.square(g)
    m_hat = m / (1.0 - ADAM_B1 ** ADAM_STEP)
    v_hat = v / (1.0 - ADAM_B2 ** ADAM_STEP)
    delta = -ADAM_LR * (m_hat / (_jnp.sqrt(v_hat) + ADAM_EPS) + ADAM_WD * w)
    return delta, m, v


def reference(x, c, w_ada, b_ada, norm1_g, w_in, w_dw, b_dw, ln_g, ln_b, w_conv_out, a_re, a_im, log_dt, b_re, b_im, c_re, c_im, d_skip, w_glu_a, w_glu_b, w_out, norm2_g, w_ff1, w_ff2, final_g, loss_target, m_w_ada, m_b_ada, m_norm1_g, m_w_in, m_w_dw, m_b_dw, m_ln_g, m_ln_b, m_w_conv_out, m_a_re, m_a_im, m_log_dt, m_b_re, m_b_im, m_c_re, m_c_im, m_d_skip, m_w_glu_a, m_w_glu_b, m_w_out, m_norm2_g, m_w_ff1, m_w_ff2, m_final_g, v_w_ada, v_b_ada, v_norm1_g, v_w_in, v_w_dw, v_b_dw, v_ln_g, v_ln_b, v_w_conv_out, v_a_re, v_a_im, v_log_dt, v_b_re, v_b_im, v_c_re, v_c_im, v_d_skip, v_w_glu_a, v_w_glu_b, v_w_out, v_norm2_g, v_w_ff1, v_w_ff2, v_final_g):
    given = dict(x=x, c=c, w_ada=w_ada, b_ada=b_ada, norm1_g=norm1_g, w_in=w_in, w_dw=w_dw, b_dw=b_dw, ln_g=ln_g, ln_b=ln_b, w_conv_out=w_conv_out, a_re=a_re, a_im=a_im, log_dt=log_dt, b_re=b_re, b_im=b_im, c_re=c_re, c_im=c_im, d_skip=d_skip, w_glu_a=w_glu_a, w_glu_b=w_glu_b, w_out=w_out, norm2_g=norm2_g, w_ff1=w_ff1, w_ff2=w_ff2, final_g=final_g, loss_target=loss_target, m_w_ada=m_w_ada, m_b_ada=m_b_ada, m_norm1_g=m_norm1_g, m_w_in=m_w_in, m_w_dw=m_w_dw, m_b_dw=m_b_dw, m_ln_g=m_ln_g, m_ln_b=m_ln_b, m_w_conv_out=m_w_conv_out, m_a_re=m_a_re, m_a_im=m_a_im, m_log_dt=m_log_dt, m_b_re=m_b_re, m_b_im=m_b_im, m_c_re=m_c_re, m_c_im=m_c_im, m_d_skip=m_d_skip, m_w_glu_a=m_w_glu_a, m_w_glu_b=m_w_glu_b, m_w_out=m_w_out, m_norm2_g=m_norm2_g, m_w_ff1=m_w_ff1, m_w_ff2=m_w_ff2, m_final_g=m_final_g, v_w_ada=v_w_ada, v_b_ada=v_b_ada, v_norm1_g=v_norm1_g, v_w_in=v_w_in, v_w_dw=v_w_dw, v_b_dw=v_b_dw, v_ln_g=v_ln_g, v_ln_b=v_ln_b, v_w_conv_out=v_w_conv_out, v_a_re=v_a_re, v_a_im=v_a_im, v_log_dt=v_log_dt, v_b_re=v_b_re, v_b_im=v_b_im, v_c_re=v_c_re, v_c_im=v_c_im, v_d_skip=v_d_skip, v_w_glu_a=v_w_glu_a, v_w_glu_b=v_w_glu_b, v_w_out=v_w_out, v_norm2_g=v_norm2_g, v_w_ff1=v_w_ff1, v_w_ff2=v_w_ff2, v_final_g=v_final_g)
    weights = {n: given[n] for n in TWIN_WEIGHTS}
    shared = {n: given[n] for n in SHARED_INPUTS}
    per_example = {n: given[n] for n in ['x', 'c']}
    grad_fn = _jax.value_and_grad(_loss, argnums=(0, 1))

    def one_microbatch(ex, loss_target):
        ex = dict(ex)
        diff = ex.pop(TWIN_DIFF_INPUT)
        return grad_fn(weights, diff, {**shared, **ex}, loss_target)

    if N_MICROBATCH == 1:
        loss, (grad_w, grad_x) = one_microbatch(per_example, given["loss_target"])
    else:
        def body(carry, xs):
            loss_sum, grad_sum = carry
            l_k, (gw_k, gx_k) = one_microbatch(xs[0], xs[1])
            with _jax.named_scope("update"):
                return (loss_sum + l_k, _jax.tree.map(_jnp.add, grad_sum, gw_k)), gx_k

        init = (_jnp.zeros((), _jnp.float32), _jax.tree.map(_jnp.zeros_like, weights))
        (loss, grad_w), grad_x = _jax.lax.scan(body, init, (per_example, given["loss_target"]))
    with _jax.named_scope("update"):
        delta_w, new_m, new_v = {}, {}, {}
        for n in TWIN_WEIGHTS:
            delta_w[n], new_m[n], new_v[n] = _adamw(weights[n], grad_w[n], given["m_" + n], given["v_" + n])
    return (loss, grad_x, *[grad_w[n] for n in TWIN_WEIGHTS], *[delta_w[n] for n in TWIN_WEIGHTS],
            *[new_m[n] for n in TWIN_WEIGHTS], *[new_v[n] for n in TWIN_WEIGHTS])
```

```python
import functools
import math

import jax
import jax.numpy as jnp
from jax import lax
from jax.experimental import pallas as pl
from jax.experimental.pallas import tpu as pltpu

F32 = jnp.float32
BF16 = jnp.bfloat16
EPS = 1e-6
CONV_KERNEL = 31
SSM_GROUP = 16
SSM_STATE = 64
ADAM_LR = 0.001
ADAM_B1 = 0.9
ADAM_B2 = 0.999
ADAM_EPS = 1e-08
ADAM_WD = 0.01
ADAM_STEP = 10

N_CHIPS = 4
N_DEV = 8
VMEM_LIMIT_BYTES = 56 * 1024 * 1024
LANES = 128
SUBLANES = 8
HALO = 32
GROUPS_PER_BLOCK = LANES // SSM_GROUP
STATE_LANES = GROUPS_PER_BLOCK * SSM_STATE
MESH = pl.DeviceIdType.MESH


def _cparams(sem):
    return pltpu.CompilerParams(dimension_semantics=sem, vmem_limit_bytes=VMEM_LIMIT_BYTES)


def _pick(n, pref, mult=LANES):
    if n <= pref:
        return n
    best = None
    for d in range(mult, pref + 1, mult):
        if n % d == 0:
            best = d
    assert best is not None, (n, pref)
    return best


def _sigmoid(v):
    return 1.0 / (1.0 + jnp.exp(-v))


def _gelu_parts(v):
    k0 = math.sqrt(2.0 / math.pi)
    inner = k0 * (v + 0.044715 * v * v * v)
    t = jnp.tanh(inner)
    return k0, t


def _gelu(v):
    _, t = _gelu_parts(v)
    return 0.5 * v * (1.0 + t)


def _gelu_grad(v):
    k0, t = _gelu_parts(v)
    return 0.5 * (1.0 + t) + 0.5 * v * (1.0 - t * t) * k0 * (1.0 + 3.0 * 0.044715 * v * v)


def _relu2_bf16(a):
    t = jnp.maximum(a.astype(F32), 0.0)
    return (t * t).astype(BF16)


def _mm(a, b, *, mode, out_dtype, name, out_gathered=False, a_fn=None, epi=None, extra=None,
        bm_pref=1024, bn_pref=1024, bk_pref=512):
    gathered = (b.ndim == 3)
    if mode == "nn":
        m, kdim = a.shape
        ns = b.shape[-1]
        n = ns * (N_CHIPS if gathered else 1)
        bm, bn, bk = _pick(m, bm_pref), _pick(ns, bn_pref), _pick(kdim, bk_pref)
        npb = ns // bn
        grid = (m // bm, n // bn, kdim // bk)
        a_spec = pl.BlockSpec((bm, bk), lambda i, j, k: (i, k))
        if gathered:
            b_spec = pl.BlockSpec((None, bk, bn), lambda i, j, k: (j // npb, k, j % npb))
        else:
            b_spec = pl.BlockSpec((bk, bn), lambda i, j, k: (k, j))
        o_spec = pl.BlockSpec((bm, bn), lambda i, j, k: (i, j))
        e_spec = pl.BlockSpec((bm, bn), lambda i, j, k: (i, j))
        out_shape = (m, n)
        acc_shape = (bm, bn)
        dims = (((1,), (0,)), ((), ()))
    elif mode == "nt":
        m = a.shape[0]
        kdim, ns = b.shape[-2], b.shape[-1]
        n = ns * (N_CHIPS if gathered else 1)
        assert a.shape[1] == n
        bm, bko, bnr = _pick(m, bm_pref), _pick(kdim, bn_pref), _pick(ns, bk_pref)
        npb = ns // bnr
        grid = (m // bm, kdim // bko, n // bnr)
        a_spec = pl.BlockSpec((bm, bnr), lambda i, j, k: (i, k))
        if gathered:
            b_spec = pl.BlockSpec((None, bko, bnr), lambda i, j, k: (k // npb, j, k % npb))
        else:
            b_spec = pl.BlockSpec((bko, bnr), lambda i, j, k: (j, k))
        o_spec = pl.BlockSpec((bm, bko), lambda i, j, k: (i, j))
        e_spec = pl.BlockSpec((bm, bko), lambda i, j, k: (i, j))
        out_shape = (m, kdim)
        acc_shape = (bm, bko)
        dims = (((1,), (1,)), ((), ()))
    else:
        m, kdim = a.shape
        n = b.shape[1]
        ns = n // N_CHIPS if out_gathered else n
        bmr, bko, bn = _pick(m, bk_pref), _pick(kdim, bm_pref), _pick(ns, bn_pref)
        npb = ns // bn
        grid = (kdim // bko, n // bn, m // bmr)
        a_spec = pl.BlockSpec((bmr, bko), lambda i, j, k: (k, i))
        b_spec = pl.BlockSpec((bmr, bn), lambda i, j, k: (k, j))
        if out_gathered:
            o_spec = pl.BlockSpec((None, bko, bn), lambda i, j, k: (j // npb, i, j % npb))
            out_shape = (N_CHIPS, kdim, ns)
        else:
            o_spec = pl.BlockSpec((bko, bn), lambda i, j, k: (i, j))
            out_shape = (kdim, n)
        e_spec = None
        acc_shape = (bko, bn)
        dims = (((0,), (0,)), ((), ()))
    nk = grid[2]

    def body(*refs):
        if extra is not None:
            a_ref, b_ref, e_ref, o_ref, acc = refs
        else:
            a_ref, b_ref, o_ref, acc = refs
            e_ref = None
        k = pl.program_id(2)

        @pl.when(k == 0)
        def _():
            acc[...] = jnp.zeros_like(acc)

        av = a_ref[...]
        if a_fn is not None:
            av = a_fn(av)
        acc[...] += lax.dot_general(av, b_ref[...], dims, preferred_element_type=F32)

        @pl.when(k == nk - 1)
        def _():
            r = acc[...]
            if epi is not None:
                r = epi(r, e_ref[...])
            o_ref[...] = r.astype(o_ref.dtype)

    in_specs = [a_spec, b_spec]
    args = [a, b]
    if extra is not None:
        in_specs.append(e_spec)
        args.append(extra)
    return pl.pallas_call(
        body, grid=grid, in_specs=in_specs, out_specs=o_spec,
        out_shape=jax.ShapeDtypeStruct(out_shape, out_dtype),
        scratch_shapes=[pltpu.VMEM(acc_shape, F32)],
        compiler_params=_cparams(("parallel", "parallel", "arbitrary")), name=name)(*args)


def _row_tile(rows, cols, n_arrays):
    budget = VMEM_LIMIT_BYTES // 3
    cap = min(512, budget // (n_arrays * 2 * cols * 4))
    for t in range(cap - cap % SUBLANES, 0, -SUBLANES):
        if rows % t == 0:
            return t
    return rows


def _norm_mod(x, g, scale, shift, *, name):
    rows, d = x.shape
    tr = _row_tile(rows, d, 3)

    def body(x_ref, g_ref, sc_ref, sh_ref, o_ref):
        xv = x_ref[...]
        r = lax.rsqrt(jnp.mean(xv * xv, axis=-1, keepdims=True) + EPS)
        o_ref[...] = ((xv * r * g_ref[...]) * (1.0 + sc_ref[...]) + sh_ref[...]).astype(o_ref.dtype)

    row = pl.BlockSpec((tr, d), lambda i: (i, 0))
    vec = pl.BlockSpec((1, d), lambda i: (0, 0))
    return pl.pallas_call(
        body, grid=(rows // tr,), in_specs=[row, vec, vec, vec], out_specs=row,
        out_shape=jax.ShapeDtypeStruct((rows, d), BF16),
        compiler_params=_cparams(("parallel",)), name=name)(x, g, scale, shift)


def _conv_fwd(proj, w_dw, b_dw, ln_g, ln_b, *, cw):
    rows = proj.shape[0]
    tt = _pick(rows, 256, HALO)
    hb = tt // HALO

    def body(a_ref, g_ref, ha_ref, hg_ref, w_ref, b_ref, lg_ref, lb_ref, sl_ref, cv_ref, vs):
        i = pl.program_id(0)
        hv = ha_ref[...].astype(F32) * _sigmoid(hg_ref[...].astype(F32))
        vs[pl.ds(0, HALO), :] = jnp.where(i == 0, 0.0, hv)
        vs[pl.ds(HALO, tt), :] = a_ref[...].astype(F32) * _sigmoid(g_ref[...].astype(F32))
        acc = jnp.broadcast_to(b_ref[...], (tt, cw))
        for k in range(CONV_KERNEL):
            acc = acc + w_ref[pl.ds(k, 1), :] * vs[pl.ds(HALO - (CONV_KERNEL - 1) + k, tt), :]
        cv_ref[...] = acc
        mu = jnp.mean(acc, axis=-1, keepdims=True)
        xc = acc - mu
        rstd = lax.rsqrt(jnp.mean(xc * xc, axis=-1, keepdims=True) + EPS)
        ln = xc * rstd * lg_ref[...] + lb_ref[...]
        sl_ref[...] = (ln * _sigmoid(ln)).astype(sl_ref.dtype)

    tile = lambda c: pl.BlockSpec((tt, cw), lambda i, c=c: (i, c))
    halo = lambda c: pl.BlockSpec((HALO, cw), lambda i, c=c: (jnp.maximum(i * hb - 1, 0), c))
    vec = pl.BlockSpec((1, cw), lambda i: (0, 0))
    return pl.pallas_call(
        body, grid=(rows // tt,),
        in_specs=[tile(0), tile(1), halo(0), halo(1),
                  pl.BlockSpec((CONV_KERNEL, cw), lambda i: (0, 0)), vec, vec, vec],
        out_specs=[pl.BlockSpec((tt, cw), lambda i: (i, 0)), pl.BlockSpec((tt, cw), lambda i: (i, 0))],
        out_shape=[jax.ShapeDtypeStruct((rows, cw), BF16), jax.ShapeDtypeStruct((rows, cw), F32)],
        scratch_shapes=[pltpu.VMEM((HALO + tt, cw), F32)],
        compiler_params=_cparams(("parallel",)), name="conv_fwd")(proj, proj, proj, proj, w_dw, b_dw, ln_g, ln_b)


def _ln_bwd(dsl, cv, ln_g, ln_b):
    rows, cw = cv.shape
    tr = _row_tile(rows, cw, 3)

    def body(d_ref, cv_ref, lg_ref, lb_ref, o_ref, dg_ref, db_ref):
        i = pl.program_id(0)

        @pl.when(i == 0)
        def _():
            dg_ref[...] = jnp.zeros_like(dg_ref)
            db_ref[...] = jnp.zeros_like(db_ref)

        x = cv_ref[...]
        mu = jnp.mean(x, axis=-1, keepdims=True)
        xc = x - mu
        rstd = lax.rsqrt(jnp.mean(xc * xc, axis=-1, keepdims=True) + EPS)
        xh = xc * rstd
        ln = xh * lg_ref[...] + lb_ref[...]
        s = _sigmoid(ln)
        dln = d_ref[...].astype(F32) * (s * (1.0 + ln * (1.0 - s)))
        dg_ref[...] += jnp.sum(dln * xh, axis=0, keepdims=True)
        db_ref[...] += jnp.sum(dln, axis=0, keepdims=True)
        dxh = dln * lg_ref[...]
        m1 = jnp.mean(dxh, axis=-1, keepdims=True)
        m2 = jnp.mean(dxh * xh, axis=-1, keepdims=True)
        o_ref[...] = rstd * (dxh - m1 - xh * m2)

    row = pl.BlockSpec((tr, cw), lambda i: (i, 0))
    vec = pl.BlockSpec((1, cw), lambda i: (0, 0))
    return pl.pallas_call(
        body, grid=(rows // tr,), in_specs=[row, row, vec, vec], out_specs=[row, vec, vec],
        out_shape=[jax.ShapeDtypeStruct((rows, cw), F32), jax.ShapeDtypeStruct((1, cw), F32),
                   jax.ShapeDtypeStruct((1, cw), F32)],
        compiler_params=_cparams(("arbitrary",)), name="ln_bwd")(dsl, cv, ln_g, ln_b)


def _conv_bwd(dcv, proj, w_dw, *, cw):
    rows = proj.shape[0]
    tt = _pick(rows, 256, HALO)
    hb = tt // HALO
    nt = rows // tt
    taps = CONV_KERNEL

    def body(d_ref, dn_ref, a_ref, g_ref, ha_ref, hg_ref, w_ref, o_ref, dw_ref, db_ref, vs, ds):
        i = pl.program_id(0)

        @pl.when(i == 0)
        def _():
            dw_ref[...] = jnp.zeros_like(dw_ref)
            db_ref[...] = jnp.zeros_like(db_ref)

        av = a_ref[...].astype(F32)
        sg = _sigmoid(g_ref[...].astype(F32))
        hv = ha_ref[...].astype(F32) * _sigmoid(hg_ref[...].astype(F32))
        vs[pl.ds(0, HALO), :] = jnp.where(i == 0, 0.0, hv)
        vs[pl.ds(HALO, tt), :] = av * sg
        d = d_ref[...]
        ds[pl.ds(0, tt), :] = d
        ds[pl.ds(tt, HALO), :] = jnp.where(i == nt - 1, 0.0, dn_ref[...])
        db_ref[...] += jnp.sum(d, axis=0, keepdims=True)
        dv = jnp.zeros((tt, cw), F32)
        for k in range(taps):
            dv = dv + w_ref[pl.ds(k, 1), :] * ds[pl.ds(taps - 1 - k, tt), :]
            dw_ref[pl.ds(k, 1), :] += jnp.sum(d * vs[pl.ds(HALO - (taps - 1) + k, tt), :], axis=0, keepdims=True)
        o_ref[:, pl.ds(0, cw)] = (dv * sg).astype(o_ref.dtype)
        o_ref[:, pl.ds(cw, cw)] = (dv * av * sg * (1.0 - sg)).astype(o_ref.dtype)

    tile = lambda c: pl.BlockSpec((tt, cw), lambda i, c=c: (i, c))
    halo = lambda c: pl.BlockSpec((HALO, cw), lambda i, c=c: (jnp.maximum(i * hb - 1, 0), c))
    nxt = pl.BlockSpec((HALO, cw), lambda i: (jnp.minimum((i + 1) * hb, nt * hb - 1), 0))
    return pl.pallas_call(
        body, grid=(nt,),
        in_specs=[pl.BlockSpec((tt, cw), lambda i: (i, 0)), nxt, tile(0), tile(1), halo(0), halo(1),
                  pl.BlockSpec((taps, cw), lambda i: (0, 0))],
        out_specs=[pl.BlockSpec((tt, 2 * cw), lambda i: (i, 0)),
                   pl.BlockSpec((taps, cw), lambda i: (0, 0)), pl.BlockSpec((1, cw), lambda i: (0, 0))],
        out_shape=[jax.ShapeDtypeStruct((rows, 2 * cw), BF16), jax.ShapeDtypeStruct((taps, cw), F32),
                   jax.ShapeDtypeStruct((1, cw), F32)],
        scratch_shapes=[pltpu.VMEM((HALO + tt, cw), F32), pltpu.VMEM((tt + HALO, cw), F32)],
        compiler_params=_cparams(("arbitrary",)), name="conv_bwd")(dcv, dcv, proj, proj, proj, proj, w_dw)


def _merge_fwd(proj, y_conv, ya, yb, *, cw):
    rows = proj.shape[0]
    tr = _row_tile(rows, cw, 4)

    def body(gc_ref, gs_ref, yc_ref, ya_ref, yb_ref, o_ref):
        ys = ya_ref[...].astype(F32) * _sigmoid(yb_ref[...].astype(F32))
        o_ref[...] = (_sigmoid(gc_ref[...].astype(F32)) * yc_ref[...].astype(F32)
                      + _sigmoid(gs_ref[...].astype(F32)) * ys).astype(o_ref.dtype)

    blk = lambda off: pl.BlockSpec((tr, cw), lambda i, h, off=off: (i, off + h))
    return pl.pallas_call(
        body, grid=(rows // tr, 2), in_specs=[blk(3), blk(5), blk(0), blk(0), blk(0)], out_specs=blk(0),
        out_shape=jax.ShapeDtypeStruct((rows, 2 * cw), BF16),
        compiler_params=_cparams(("parallel", "parallel")), name="merge_fwd")(proj, proj, y_conv, ya, yb)


def _merge_bwd(dmerged, proj, y_conv, ya, yb, *, cw):
    rows = proj.shape[0]
    tr = _row_tile(rows, cw, 6)

    def body(d_ref, gc_ref, gs_ref, yc_ref, ya_ref, yb_ref, dgc_ref, dgs_ref, dyc_ref, dya_ref, dyb_ref):
        d = d_ref[...].astype(F32)
        sc = _sigmoid(gc_ref[...].astype(F32))
        ss = _sigmoid(gs_ref[...].astype(F32))
        sb = _sigmoid(yb_ref[...].astype(F32))
        yav = ya_ref[...].astype(F32)
        dgc_ref[...] = (d * yc_ref[...].astype(F32) * sc * (1.0 - sc)).astype(dgc_ref.dtype)
        dgs_ref[...] = (d * (yav * sb) * ss * (1.0 - ss)).astype(dgs_ref.dtype)
        dyc_ref[...] = (d * sc).astype(dyc_ref.dtype)
        dys = d * ss
        dya_ref[...] = (dys * sb).astype(dya_ref.dtype)
        dyb_ref[...] = (dys * yav * sb * (1.0 - sb)).astype(dyb_ref.dtype)

    blk = lambda off: pl.BlockSpec((tr, cw), lambda i, h, off=off: (i, off + h))
    o2 = jax.ShapeDtypeStruct((rows, 2 * cw), BF16)
    return pl.pallas_call(
        body, grid=(rows // tr, 2),
        in_specs=[blk(0), blk(3), blk(5), blk(0), blk(0), blk(0)],
        out_specs=[blk(0), blk(0), blk(0), blk(0), blk(0)],
        out_shape=[o2, o2, o2, o2, o2],
        compiler_params=_cparams(("parallel", "parallel")), name="merge_bwd")(dmerged, proj, proj, y_conv, ya, yb)


def _res_norm(x, mo, gate, g, scale, shift):
    rows, d = x.shape
    tr = _row_tile(rows, d, 4)

    def body(x_ref, mo_ref, gt_ref, g_ref, sc_ref, sh_ref, h_ref, z_ref):
        h = x_ref[...] + gt_ref[...] * mo_ref[...].astype(F32)
        h_ref[...] = h
        r = lax.rsqrt(jnp.mean(h * h, axis=-1, keepdims=True) + EPS)
        z_ref[...] = ((h * r * g_ref[...]) * (1.0 + sc_ref[...]) + sh_ref[...]).astype(z_ref.dtype)

    row = pl.BlockSpec((tr, d), lambda i: (i, 0))
    vec = pl.BlockSpec((1, d), lambda i: (0, 0))
    return pl.pallas_call(
        body, grid=(rows // tr,), in_specs=[row, row, vec, vec, vec, vec], out_specs=[row, row],
        out_shape=[jax.ShapeDtypeStruct((rows, d), F32), jax.ShapeDtypeStruct((rows, d), BF16)],
        compiler_params=_cparams(("parallel",)), name="res_norm")(x, mo, gate, g, scale, shift)


def _final_fwd_bwd(h1, ff, gate2, final_g, target):
    rows, d = h1.shape
    tr = _row_tile(rows, d, 5)

    def body(h_ref, ff_ref, gt_ref, fg_ref, t_ref, dh_ref, dff_ref, loss_ref, dfg_ref, dgt_ref):
        i = pl.program_id(0)

        @pl.when(i == 0)
        def _():
            loss_ref[...] = jnp.zeros_like(loss_ref)
            dfg_ref[...] = jnp.zeros_like(dfg_ref)
            dgt_ref[...] = jnp.zeros_like(dgt_ref)

        ffv = ff_ref[...].astype(F32)
        h2 = h_ref[...] + gt_ref[...] * ffv
        r = lax.rsqrt(jnp.mean(h2 * h2, axis=-1, keepdims=True) + EPS)
        y = h2 * r
        e = y * fg_ref[...] - t_ref[...]
        loss_ref[...] += 0.5 * jnp.sum(jnp.mean(e * e, axis=-1, keepdims=True))
        dout = e * (1.0 / d)
        dfg_ref[...] += jnp.sum(dout * y, axis=0, keepdims=True)
        dy = dout * fg_ref[...]
        dh2 = r * (dy - y * jnp.mean(dy * y, axis=-1, keepdims=True))
        dh_ref[...] = dh2
        dgt_ref[...] += jnp.sum(dh2 * ffv, axis=0, keepdims=True)
        dff_ref[...] = (dh2 * gt_ref[...]).astype(dff_ref.dtype)

    row = pl.BlockSpec((tr, d), lambda i: (i, 0))
    vec = pl.BlockSpec((1, d), lambda i: (0, 0))
    return pl.pallas_call(
        body, grid=(rows // tr,), in_specs=[row, row, vec, vec, row],
        out_specs=[row, row, pl.BlockSpec((1, LANES), lambda i: (0, 0)), vec, vec],
        out_shape=[jax.ShapeDtypeStruct((rows, d), F32), jax.ShapeDtypeStruct((rows, d), BF16),
                   jax.ShapeDtypeStruct((1, LANES), F32), jax.ShapeDtypeStruct((1, d), F32),
                   jax.ShapeDtypeStruct((1, d), F32)],
        compiler_params=_cparams(("arbitrary",)), name="final_fwd_bwd")(h1, ff, gate2, final_g, target)


def _norm_mod_bwd(dz, hin, dres, g, scale, gate, mo, *, name):
    rows, d = hin.shape
    with_gate = gate is not None
    tr = _row_tile(rows, d, 6)

    def body(*refs):
        if with_gate:
            (dz_ref, h_ref, dr_ref, g_ref, sc_ref, gt_ref, mo_ref,
             dh_ref, dsh_ref, dsc_ref, dg_ref, dmo_ref, dgt_ref) = refs
        else:
            dz_ref, h_ref, dr_ref, g_ref, sc_ref, dh_ref, dsh_ref, dsc_ref, dg_ref = refs
        i = pl.program_id(0)

        @pl.when(i == 0)
        def _():
            dsh_ref[...] = jnp.zeros_like(dsh_ref)
            dsc_ref[...] = jnp.zeros_like(dsc_ref)
            dg_ref[...] = jnp.zeros_like(dg_ref)
            if with_gate:
                dgt_ref[...] = jnp.zeros_like(dgt_ref)

        dzv = dz_ref[...].astype(F32)
        h = h_ref[...]
        r = lax.rsqrt(jnp.mean(h * h, axis=-1, keepdims=True) + EPS)
        y = h * r
        dsh_ref[...] += jnp.sum(dzv, axis=0, keepdims=True)
        dsc_ref[...] += jnp.sum(dzv * (y * g_ref[...]), axis=0, keepdims=True)
        dn = dzv * (1.0 + sc_ref[...])
        dg_ref[...] += jnp.sum(dn * y, axis=0, keepdims=True)
        dy = dn * g_ref[...]
        dh = dr_ref[...] + r * (dy - y * jnp.mean(dy * y, axis=-1, keepdims=True))
        dh_ref[...] = dh
        if with_gate:
            dmo_ref[...] = (dh * gt_ref[...]).astype(dmo_ref.dtype)
            dgt_ref[...] += jnp.sum(dh * mo_ref[...].astype(F32), axis=0, keepdims=True)

    row = pl.BlockSpec((tr, d), lambda i: (i, 0))
    vec = pl.BlockSpec((1, d), lambda i: (0, 0))
    vshape = jax.ShapeDtypeStruct((1, d), F32)
    in_specs = [row, row, row, vec, vec]
    args = [dz, hin, dres, g, scale]
    out_specs = [row, vec, vec, vec]
    out_shape = [jax.ShapeDtypeStruct((rows, d), F32), vshape, vshape, vshape]
    if with_gate:
        in_specs += [vec, row]
        args += [gate, mo]
        out_specs += [row, vec]
        out_shape += [jax.ShapeDtypeStruct((rows, d), BF16), vshape]
    return pl.pallas_call(
        body, grid=(rows // tr,), in_specs=in_specs, out_specs=out_specs, out_shape=out_shape,
        compiler_params=_cparams(("arbitrary",)), name=name)(*args)


def _s5_discretise(a_re, a_im, log_dt, b_re, b_im):
    dt = jnp.exp(log_dt)[:, None]
    er = jnp.exp(a_re * dt)
    lr = er * jnp.cos(a_im * dt)
    li = er * jnp.sin(a_im * dt)
    den = a_re * a_re + a_im * a_im
    cr = ((lr - 1.0) * a_re + li * a_im) / den
    ci = (li * a_re - (lr - 1.0) * a_im) / den
    bbr = cr[..., None] * b_re - ci[..., None] * b_im
    bbi = cr[..., None] * b_im + ci[..., None] * b_re
    return lr, li, bbr, bbi


def _block_diag(w):
    g, r, c = w.shape
    nb = g // GROUPS_PER_BLOCK
    eye = jnp.eye(GROUPS_PER_BLOCK, dtype=w.dtype)
    w5 = w.reshape(nb, GROUPS_PER_BLOCK, r, 1, c) * eye[None, :, None, :, None]
    return w5.reshape(nb, GROUPS_PER_BLOCK * r, GROUPS_PER_BLOCK * c)


def _block_diag_extract(m, r, c):
    nb = m.shape[0]
    m5 = m.reshape(nb, GROUPS_PER_BLOCK, r, GROUPS_PER_BLOCK, c)
    idx = jnp.arange(GROUPS_PER_BLOCK)
    d = m5[:, idx, :, idx, :]
    return jnp.moveaxis(d, 0, 1).reshape(nb * GROUPS_PER_BLOCK, r, c)


def _split_hi_lo(w):
    hi = w.astype(BF16)
    lo = (w - hi.astype(F32)).astype(BF16)
    return hi, lo


def _scan_multipliers(lr, li):
    pr, pi = [lr], [li]
    for _ in range(SUBLANES - 1):
        pr.append(pr[-1] * lr - pi[-1] * li)
        pi.append(pr[-2] * li + pi[-1] * lr)
    pr = jnp.concatenate(pr, axis=1)
    pi = jnp.concatenate(pi, axis=1)
    rows = jnp.arange(SUBLANES)[None, :, None]
    fr, fi, rr, ri = [], [], [], []
    for s in (1, 2, 4):
        mf = (rows >= s).astype(F32)
        mr = (rows <= SUBLANES - 1 - s).astype(F32)
        fr.append(mf * pr[:, s - 1:s, :])
        fi.append(mf * pi[:, s - 1:s, :])
        rr.append(mr * pr[:, s - 1:s, :])
        ri.append(mr * pi[:, s - 1:s, :])
    fr.append(pr)
    fi.append(pi)
    rr.append(pr[:, ::-1, :])
    ri.append(pi[:, ::-1, :])
    st = lambda xs: jnp.stack(xs, axis=1)
    return st(fr), st(fi), st(rr), st(ri)


def _scan_rows(sre, sim, mul_r, mul_i, n_groups, reverse):
    sgn = -1.0 if reverse else 1.0
    lanes = sre.shape[1]

    def step(k, carry):
        cr, ci = carry
        kk = (n_groups - 1 - k) if reverse else k
        r0 = pl.multiple_of(kk * SUBLANES, SUBLANES)
        xr = sre[pl.ds(r0, SUBLANES), :]
        xi = sim[pl.ds(r0, SUBLANES), :]
        for lvl, s in enumerate((1, 2, 4)):
            sh = (SUBLANES - s) if reverse else s
            nr = pltpu.roll(xr, sh, 0)
            ni = pltpu.roll(xi, sh, 0)
            mr = mul_r[lvl]
            mi = mul_i[lvl] * sgn
            xr, xi = xr + mr * nr - mi * ni, xi + mr * ni + mi * nr
        mr = mul_r[3]
        mi = mul_i[3] * sgn
        xr, xi = xr + mr * cr - mi * ci, xi + mr * ci + mi * cr
        sre[pl.ds(r0, SUBLANES), :] = xr
        sim[pl.ds(r0, SUBLANES), :] = xi
        edge = 0 if reverse else SUBLANES - 1
        ncr = jnp.broadcast_to(xr[edge:edge + 1, :], (SUBLANES, lanes))
        nci = jnp.broadcast_to(xi[edge:edge + 1, :], (SUBLANES, lanes))
        return ncr, nci

    zero = jnp.zeros((SUBLANES, lanes), F32)
    lax.fori_loop(0, n_groups, step, (zero, zero))


def _dot(a, b):
    return jnp.dot(a, b, preferred_element_type=F32)


def _dot3(a, b_hi, b_lo):
    a_hi = a.astype(BF16)
    a_lo = (a - a_hi.astype(F32)).astype(BF16)
    return _dot(a_hi, b_hi) + (_dot(a_hi, b_lo) + _dot(a_lo, b_hi))


def _s5_operands(lr, li, bbr, bbi, c_re, c_im):
    g = lr.shape[0]
    nb = g // GROUPS_PER_BLOCK
    tb = lambda w: jnp.swapaxes(w, 1, 2)
    b_in = [_block_diag(tb(bbr)), _block_diag(tb(bbi))]
    c_out = [_block_diag(tb(c_re)), _block_diag(tb(c_im))]
    b_out = [_block_diag(bbr), _block_diag(bbi)]
    c_in = [_block_diag(c_re), _block_diag(c_im)]
    lam_r = lr.reshape(nb, 1, STATE_LANES)
    lam_i = li.reshape(nb, 1, STATE_LANES)
    mults = _scan_multipliers(lam_r, lam_i)
    split = lambda ws: [p for w in ws for p in _split_hi_lo(w)]
    return split(b_in), split(c_out), split(b_out), split(c_in), mults


def _s5_fwd(proj, d_skip, b_in, c_out, mults, *, col0):
    rows = proj.shape[0]
    nb = b_in[0].shape[0]
    tm = _pick(rows, 512, SUBLANES)
    n_tiles = rows // tm
    s_l = STATE_LANES

    def body(u_ref, dk_ref, brh, brl, bih, bil, crh, crl, cih, cil, fr_ref, fi_ref, o_ref, sre, sim):
        for t in range(n_tiles):
            rs = pl.ds(t * tm, tm)
            ub = u_ref[rs, :]
            sre[rs, :] = _dot(ub, brh[...]) + _dot(ub, brl[...])
            sim[rs, :] = _dot(ub, bih[...]) + _dot(ub, bil[...])
        _scan_rows(sre, sim, fr_ref, fi_ref, rows // SUBLANES, False)
        for t in range(n_tiles):
            rs = pl.ds(t * tm, tm)
            y0 = _dot3(sre[rs, :], crh[...], crl[...]) - _dot3(sim[rs, :], cih[...], cil[...])
            y1 = y0 + dk_ref[...] * u_ref[rs, :].astype(F32)
            o_ref[rs, :] = _gelu(y1).astype(o_ref.dtype)

    mat_in = pl.BlockSpec((None, LANES, s_l), lambda g: (g, 0, 0))
    mat_out = pl.BlockSpec((None, s_l, LANES), lambda g: (g, 0, 0))
    mul = pl.BlockSpec((None, 4, SUBLANES, s_l), lambda g: (g, 0, 0, 0))
    return pl.pallas_call(
        body, grid=(nb,),
        in_specs=[pl.BlockSpec((rows, LANES), lambda g: (0, col0 + g)), pl.BlockSpec((1, LANES), lambda g: (0, g))]
        + [mat_in] * 4 + [mat_out] * 4 + [mul] * 2,
        out_specs=pl.BlockSpec((rows, LANES), lambda g: (0, g)),
        out_shape=jax.ShapeDtypeStruct((rows, nb * LANES), BF16),
        scratch_shapes=[pltpu.VMEM((rows, s_l), F32), pltpu.VMEM((rows, s_l), F32)],
        compiler_params=_cparams(("parallel",)), name="s5_fwd")(
            proj, d_skip, *b_in, *c_out, mults[0], mults[1])


def _s5_bwd(proj, dyg, d_skip, b_in, c_out, b_out, c_in, mults, *, col0):
    rows = proj.shape[0]
    nb = b_in[0].shape[0]
    tm = _pick(rows, 512, SUBLANES)
    n_tiles = rows // tm
    s_l = STATE_LANES
    n_groups = rows // SUBLANES
    tn = (((0,), (0,)), ((), ()))

    def body(u_ref, dy_ref, dk_ref, brh, brl, bih, bil, crh, crl, cih, cil, borh, borl, boih, boil,
             cirh, cirl, ciih, ciil, fr_ref, fi_ref, rr_ref, ri_ref,
             du_ref, ddk_ref, dbr_ref, dbi_ref, dcr_ref, dci_ref, dlr_ref, dli_ref,
             sre, sim, gre, gim, dy1):
        for t in range(n_tiles):
            rs = pl.ds(t * tm, tm)
            ub = u_ref[rs, :]
            sre[rs, :] = _dot(ub, brh[...]) + _dot(ub, brl[...])
            sim[rs, :] = _dot(ub, bih[...]) + _dot(ub, bil[...])
        _scan_rows(sre, sim, fr_ref, fi_ref, n_groups, False)

        ddk = jnp.zeros((1, LANES), F32)
        dcr = jnp.zeros((s_l, LANES), F32)
        dci = jnp.zeros((s_l, LANES), F32)
        for t in range(n_tiles):
            rs = pl.ds(t * tm, tm)
            sr = sre[rs, :]
            si = sim[rs, :]
            uf = u_ref[rs, :].astype(F32)
            y0 = _dot3(sr, crh[...], crl[...]) - _dot3(si, cih[...], cil[...])
            y1 = y0 + dk_ref[...] * uf
            d1 = dy_ref[rs, :].astype(F32) * _gelu_grad(y1)
            dy1[rs, :] = d1
            ddk = ddk + jnp.sum(d1 * uf, axis=0, keepdims=True)
            d1b = d1.astype(BF16)
            dcr = dcr + lax.dot_general(sr.astype(BF16), d1b, tn, preferred_element_type=F32)
            dci = dci - lax.dot_general(si.astype(BF16), d1b, tn, preferred_element_type=F32)
            gre[rs, :] = _dot3(d1, cirh[...], cirl[...])
            gim[rs, :] = -_dot3(d1, ciih[...], ciil[...])
        ddk_ref[...] = ddk
        dcr_ref[...] = dcr
        dci_ref[...] = dci
        _scan_rows(gre, gim, rr_ref, ri_ref, n_groups, True)

        dbr = jnp.zeros((LANES, s_l), F32)
        dbi = jnp.zeros((LANES, s_l), F32)
        for t in range(n_tiles):
            rs = pl.ds(t * tm, tm)
            gr = gre[rs, :]
            gi = gim[rs, :]
            du = _dot3(gr, borh[...], borl[...]) + _dot3(gi, boih[...], boil[...]) + dy1[rs, :] * dk_ref[...]
            du_ref[rs, :] = du.astype(du_ref.dtype)
            ub = u_ref[rs, :]
            dbr = dbr + lax.dot_general(ub, gr.astype(BF16), tn, preferred_element_type=F32)
            dbi = dbi + lax.dot_general(ub, gi.astype(BF16), tn, preferred_element_type=F32)
        dbr_ref[...] = dbr
        dbi_ref[...] = dbi

        row_id = lax.broadcasted_iota(jnp.int32, (SUBLANES, s_l), 0)

        def lam_step(k, carry):
            ar, ai = carry
            r0 = pl.multiple_of(k * SUBLANES, SUBLANES)
            p0 = pl.multiple_of(jnp.maximum(k - 1, 0) * SUBLANES, SUBLANES)
            keep = jnp.where(k == 0, 0.0, 1.0)
            gr = gre[pl.ds(r0, SUBLANES), :]
            gi = gim[pl.ds(r0, SUBLANES), :]
            pr = jnp.where(row_id == 0, pltpu.roll(sre[pl.ds(p0, SUBLANES), :], 1, 0) * keep,
                           pltpu.roll(sre[pl.ds(r0, SUBLANES), :], 1, 0))
            pi = jnp.where(row_id == 0, pltpu.roll(sim[pl.ds(p0, SUBLANES), :], 1, 0) * keep,
                           pltpu.roll(sim[pl.ds(r0, SUBLANES), :], 1, 0))
            return ar + gr * pr + gi * pi, ai + gi * pr - gr * pi

        zero = jnp.zeros((SUBLANES, s_l), F32)
        ar, ai = lax.fori_loop(0, n_groups, lam_step, (zero, zero))
        dlr_ref[...] = jnp.sum(ar, axis=0, keepdims=True)
        dli_ref[...] = jnp.sum(ai, axis=0, keepdims=True)

    mat_in = pl.BlockSpec((None, LANES, s_l), lambda g: (g, 0, 0))
    mat_out = pl.BlockSpec((None, s_l, LANES), lambda g: (g, 0, 0))
    mul = pl.BlockSpec((None, 4, SUBLANES, s_l), lambda g: (g, 0, 0, 0))
    lam = pl.BlockSpec((None, 1, s_l), lambda g: (g, 0, 0))
    col = pl.BlockSpec((rows, LANES), lambda g: (0, g))
    vec = pl.BlockSpec((1, LANES), lambda g: (0, g))
    return pl.pallas_call(
        body, grid=(nb,),
        in_specs=[pl.BlockSpec((rows, LANES), lambda g: (0, col0 + g)), col, vec]
        + [mat_in] * 4 + [mat_out] * 4 + [mat_out] * 4 + [mat_in] * 4 + [mul] * 4,
        out_specs=[col, vec, mat_in, mat_in, mat_out, mat_out, lam, lam],
        out_shape=[jax.ShapeDtypeStruct((rows, nb * LANES), BF16), jax.ShapeDtypeStruct((1, nb * LANES), F32),
                   jax.ShapeDtypeStruct((nb, LANES, s_l), F32), jax.ShapeDtypeStruct((nb, LANES, s_l), F32),
                   jax.ShapeDtypeStruct((nb, s_l, LANES), F32), jax.ShapeDtypeStruct((nb, s_l, LANES), F32),
                   jax.ShapeDtypeStruct((nb, 1, s_l), F32), jax.ShapeDtypeStruct((nb, 1, s_l), F32)],
        scratch_shapes=[pltpu.VMEM((rows, s_l), F32)] * 4 + [pltpu.VMEM((rows, LANES), F32)],
        compiler_params=_cparams(("parallel",)), name="s5_bwd")(
            proj, dyg, d_skip, *b_in, *c_out, *b_out, *c_in, *mults)


def _silu(v):
    return v * _sigmoid(v)


def _ada_fwd(c_all, w_shard, b_cols):
    d, n = w_shard.shape
    bn = _pick(n, 512)

    def body(c_ref, w_ref, b_ref, o_ref):
        ca = _silu(c_ref[...]).astype(BF16)
        o_ref[...] = _dot(ca, w_ref[...].astype(BF16)) + b_ref[...]

    return pl.pallas_call(
        body, grid=(n // bn,),
        in_specs=[pl.BlockSpec((N_DEV, d), lambda j: (0, 0)), pl.BlockSpec((d, bn), lambda j: (0, j)),
                  pl.BlockSpec((1, bn), lambda j: (0, j))],
        out_specs=pl.BlockSpec((N_DEV, bn), lambda j: (0, j)),
        out_shape=jax.ShapeDtypeStruct((N_DEV, n), F32),
        compiler_params=_cparams(("parallel",)), name="ada_fwd")(c_all, w_shard, b_cols)


def _ada_bwd(c_all, dmod_cols):
    d = c_all.shape[1]
    n = dmod_cols.shape[1]
    bn = _pick(n, 512)

    def body(c_ref, g_ref, o_ref):
        ca = _silu(c_ref[...]).astype(BF16)
        o_ref[...] = lax.dot_general(ca, g_ref[...].astype(BF16), (((0,), (0,)), ((), ())),
                                     preferred_element_type=F32)

    return pl.pallas_call(
        body, grid=(n // bn,),
        in_specs=[pl.BlockSpec((N_DEV, d), lambda j: (0, 0)), pl.BlockSpec((N_DEV, bn), lambda j: (0, j))],
        out_specs=pl.BlockSpec((d, bn), lambda j: (0, j)),
        out_shape=jax.ShapeDtypeStruct((d, n), F32),
        compiler_params=_cparams(("parallel",)), name="ada_bwd")(c_all, dmod_cols)


def _cast_bf16(w, *, name):
    rows, cols = w.shape
    tr = _row_tile(rows, cols, 2)

    def body(w_ref, o_ref):
        o_ref[...] = w_ref[...].astype(BF16)

    row = pl.BlockSpec((tr, cols), lambda i: (i, 0))
    return pl.pallas_call(
        body, grid=(rows // tr,), in_specs=[row], out_specs=row,
        out_shape=jax.ShapeDtypeStruct((rows, cols), BF16),
        compiler_params=_cparams(("parallel",)), name=name)(w)


def _adamw(w, g, m, v, *, name):
    rows, cols = w.shape
    tr = _row_tile(rows, cols, 7)
    c1 = 1.0 / (1.0 - ADAM_B1 ** ADAM_STEP)
    c2 = 1.0 / (1.0 - ADAM_B2 ** ADAM_STEP)

    def body(w_ref, g_ref, m_ref, v_ref, d_ref, nm_ref, nv_ref):
        gv = g_ref[...]
        nm = ADAM_B1 * m_ref[...] + (1.0 - ADAM_B1) * gv
        nv = ADAM_B2 * v_ref[...] + (1.0 - ADAM_B2) * (gv * gv)
        nm_ref[...] = nm
        nv_ref[...] = nv
        d_ref[...] = -ADAM_LR * ((nm * c1) / (jnp.sqrt(nv * c2) + ADAM_EPS) + ADAM_WD * w_ref[...])

    row = pl.BlockSpec((tr, cols), lambda i: (i, 0))
    shp = jax.ShapeDtypeStruct((rows, cols), F32)
    return pl.pallas_call(
        body, grid=(rows // tr,), in_specs=[row] * 4, out_specs=[row] * 3, out_shape=[shp] * 3,
        compiler_params=_cparams(("parallel",)), name=name)(w, g, m, v)


def _sum_leading(a, *, name, out_dtype=F32):
    n, rows, cols = a.shape
    tr = _row_tile(rows, cols, n + 1)

    def body(a_ref, o_ref):
        acc = a_ref[0].astype(F32)
        for i in range(1, n):
            acc = acc + a_ref[i].astype(F32)
        o_ref[...] = acc.astype(o_ref.dtype)

    return pl.pallas_call(
        body, grid=(rows // tr,), in_specs=[pl.BlockSpec((n, tr, cols), lambda i: (0, i, 0))],
        out_specs=pl.BlockSpec((tr, cols), lambda i: (i, 0)),
        out_shape=jax.ShapeDtypeStruct((rows, cols), out_dtype),
        compiler_params=_cparams(("parallel",)), name=name)(a)


def _add_half(dw, land, my_c, *, name):
    n, r, cols = dw.shape
    h = r // 2
    tr = _row_tile(h, cols, 3)
    hb = h // tr

    def body(c_ref, a_ref, b_ref, o_ref):
        o_ref[...] = (a_ref[...].astype(F32) + b_ref[...].astype(F32)).astype(o_ref.dtype)

    gs = pltpu.PrefetchScalarGridSpec(
        num_scalar_prefetch=1, grid=(n, hb),
        in_specs=[pl.BlockSpec((None, tr, cols), lambda s, i, c_ref: (s, c_ref[0] * hb + i, 0)),
                  pl.BlockSpec((None, tr, cols), lambda s, i, c_ref: (s, i, 0))],
        out_specs=pl.BlockSpec((None, tr, cols), lambda s, i, c_ref: (s, i, 0)))
    return pl.pallas_call(
        body, grid_spec=gs, out_shape=jax.ShapeDtypeStruct((n, h, cols), BF16),
        compiler_params=_cparams(("parallel", "parallel")), name=name)(my_c, dw, land)


def _mesh_pos():
    return lax.axis_index("x"), lax.axis_index("y"), lax.axis_index("c")


def _other_chips(x, y):
    return [(1 - x, y), (x, 1 - y), (1 - x, 1 - y)]


def _gather_small(blk, *, name):
    m_per, n = blk.shape

    def body(x_ref, out_ref, send_sems, recv_sems, local_sem):
        x, y, c = _mesh_pos()
        me, sibling = (x, y, c), (x, y, 1 - c)
        chips = _other_chips(x, y)

        def rows(px, py, pc):
            return out_ref.at[pl.ds((4 * px + 2 * py + pc) * m_per, m_per), :]

        def copy(k, block, to, src=None):
            return pltpu.make_async_remote_copy(
                src_ref=rows(*block) if src is None else src, dst_ref=rows(*block),
                send_sem=send_sems.at[k], recv_sem=recv_sems.at[k], device_id=to, device_id_type=MESH)

        mine = pltpu.make_async_copy(x_ref, rows(*me), local_sem)
        mine.start()
        first = [copy(0, me, sibling, src=x_ref)]
        first += [copy(1 + j, me, (*chip, c), src=x_ref) for j, chip in enumerate(chips)]
        for cp in first:
            cp.start()
        passed = [copy(4 + j, (*chip, c), sibling) for j, chip in enumerate(chips)]
        for j, chip in enumerate(chips):
            copy(1 + j, (*chip, c), me).wait_recv()
            passed[j].start()
        copy(0, sibling, me).wait_recv()
        for j, chip in enumerate(chips):
            copy(4 + j, (*chip, 1 - c), me).wait_recv()
        for cp in first + passed:
            cp.wait_send()
        mine.wait()

    return pl.pallas_call(
        body, out_shape=jax.ShapeDtypeStruct((N_DEV * m_per, n), blk.dtype),
        in_specs=[pl.BlockSpec(memory_space=pltpu.VMEM)], out_specs=pl.BlockSpec(memory_space=pltpu.VMEM),
        scratch_shapes=[pltpu.SemaphoreType.DMA((7,)), pltpu.SemaphoreType.DMA((7,)), pltpu.SemaphoreType.DMA],
        compiler_params=pltpu.CompilerParams(vmem_limit_bytes=VMEM_LIMIT_BYTES), name=name)(blk)


def _hbm_specs(n):
    return [pl.BlockSpec(memory_space=pl.ANY)] * n


def _gather_weights(shards):
    n = len(shards)

    def body(*refs):
        ins, outs = refs[:n], refs[n:2 * n]
        send_sems, recv_sems, local_sems = refs[2 * n:]
        x, y, c = _mesh_pos()
        me_chip = 2 * x + y
        sibling = (x, y, 1 - c)
        chips = _other_chips(x, y)

        def half(w, chip_idx, pc):
            h = shards[w].shape[0] // 2
            return outs[w].at[chip_idx, pl.ds(pc * h, h), :]

        def copy(w, k, chip_idx, pc, to, src=None):
            dst = half(w, chip_idx, pc)
            return pltpu.make_async_remote_copy(
                src_ref=dst if src is None else src, dst_ref=dst,
                send_sem=send_sems.at[6 * w + k], recv_sem=recv_sems.at[6 * w + k],
                device_id=to, device_id_type=MESH)

        local = [pltpu.make_async_copy(ins[w], outs[w].at[me_chip], local_sems.at[w]) for w in range(n)]
        for cp in local:
            cp.start()
        sends = []
        for w in range(n):
            h = shards[w].shape[0] // 2
            for j, chip in enumerate(chips):
                cp = copy(w, j, me_chip, c, (*chip, c), src=ins[w].at[pl.ds(c * h, h), :])
                cp.start()
                sends.append(cp)
        for w in range(n):
            for j, chip in enumerate(chips):
                chip_idx = 2 * chip[0] + chip[1]
                copy(w, j, chip_idx, c, (x, y, c)).wait_recv()
                cp = copy(w, 3 + j, chip_idx, c, sibling)
                cp.start()
                sends.append(cp)
        for w in range(n):
            for j, chip in enumerate(chips):
                copy(w, 3 + j, 2 * chip[0] + chip[1], 1 - c, (x, y, c)).wait_recv()
        for cp in sends:
            cp.wait_send()
        for cp in local:
            cp.wait()

    return pl.pallas_call(
        body, out_shape=[jax.ShapeDtypeStruct((N_CHIPS,) + s.shape, s.dtype) for s in shards],
        in_specs=_hbm_specs(n), out_specs=_hbm_specs(n),
        scratch_shapes=[pltpu.SemaphoreType.DMA((6 * n,)), pltpu.SemaphoreType.DMA((6 * n,)),
                        pltpu.SemaphoreType.DMA((n,))],
        name="gather_weights")(*shards)


def _swap_halves(dws):
    n = len(dws)

    def body(*refs):
        ins, outs = refs[:n], refs[n:2 * n]
        send_sems, recv_sems = refs[2 * n:]
        x, y, c = _mesh_pos()
        cps = []
        for w in range(n):
            h = dws[w].shape[1] // 2
            cp = pltpu.make_async_remote_copy(
                src_ref=ins[w].at[:, pl.ds((1 - c) * h, h), :], dst_ref=outs[w],
                send_sem=send_sems.at[w], recv_sem=recv_sems.at[w],
                device_id=(x, y, 1 - c), device_id_type=MESH)
            cp.start()
            cps.append(cp)
        for cp in cps:
            cp.wait()

    return pl.pallas_call(
        body, out_shape=[jax.ShapeDtypeStruct((s.shape[0], s.shape[1] // 2, s.shape[2]), s.dtype) for s in dws],
        in_specs=_hbm_specs(n), out_specs=_hbm_specs(n),
        scratch_shapes=[pltpu.SemaphoreType.DMA((n,)), pltpu.SemaphoreType.DMA((n,))],
        name="swap_halves")(*dws)


def _chip_exchange(parts):
    n = len(parts)

    def body(*refs):
        ins, outs = refs[:n], refs[n:2 * n]
        send_sems, recv_sems, local_sems = refs[2 * n:]
        x, y, c = _mesh_pos()
        me_chip = 2 * x + y
        chips = _other_chips(x, y)
        local = [pltpu.make_async_copy(ins[w].at[me_chip], outs[w].at[me_chip], local_sems.at[w]) for w in range(n)]
        for cp in local:
            cp.start()
        cps = []
        for w in range(n):
            for j, chip in enumerate(chips):
                cp = pltpu.make_async_remote_copy(
                    src_ref=ins[w].at[2 * chip[0] + chip[1]], dst_ref=outs[w].at[me_chip],
                    send_sem=send_sems.at[3 * w + j], recv_sem=recv_sems.at[3 * w + j],
                    device_id=(*chip, c), device_id_type=MESH)
                cp.start()
                cps.append((cp, w, j, chip))
        for cp, w, j, chip in cps:
            slot = outs[w].at[2 * chip[0] + chip[1]]
            pltpu.make_async_remote_copy(
                src_ref=slot, dst_ref=slot, send_sem=send_sems.at[3 * w + j], recv_sem=recv_sems.at[3 * w + j],
                device_id=(x, y, c), device_id_type=MESH).wait_recv()
        for cp, _, _, _ in cps:
            cp.wait_send()
        for cp in local:
            cp.wait()

    return pl.pallas_call(
        body, out_shape=[jax.ShapeDtypeStruct(s.shape, s.dtype) for s in parts],
        in_specs=_hbm_specs(n), out_specs=_hbm_specs(n),
        scratch_shapes=[pltpu.SemaphoreType.DMA((3 * n,)), pltpu.SemaphoreType.DMA((3 * n,)),
                        pltpu.SemaphoreType.DMA((n,))],
        name="chip_exchange")(*parts)


def _join_halves(halves):
    n = len(halves)

    def body(*refs):
        ins, outs = refs[:n], refs[n:2 * n]
        send_sems, recv_sems, local_sems = refs[2 * n:]
        x, y, c = _mesh_pos()
        cps, local = [], []
        for w in range(n):
            h = halves[w].shape[0]
            mine = outs[w].at[pl.ds(c * h, h), :]
            lc = pltpu.make_async_copy(ins[w], mine, local_sems.at[w])
            lc.start()
            local.append(lc)
            cp = pltpu.make_async_remote_copy(
                src_ref=ins[w], dst_ref=mine, send_sem=send_sems.at[w], recv_sem=recv_sems.at[w],
                device_id=(x, y, 1 - c), device_id_type=MESH)
            cp.start()
            cps.append(cp)
        for w in range(n):
            h = halves[w].shape[0]
            theirs = outs[w].at[pl.ds((1 - c) * h, h), :]
            pltpu.make_async_remote_copy(
                src_ref=theirs, dst_ref=theirs, send_sem=send_sems.at[w], recv_sem=recv_sems.at[w],
                device_id=(x, y, c), device_id_type=MESH).wait_recv()
        for cp in cps:
            cp.wait_send()
        for lc in local:
            lc.wait()

    return pl.pallas_call(
        body, out_shape=[jax.ShapeDtypeStruct((2 * s.shape[0], s.shape[1]), s.dtype) for s in halves],
        in_specs=_hbm_specs(n), out_specs=_hbm_specs(n),
        scratch_shapes=[pltpu.SemaphoreType.DMA((n,)), pltpu.SemaphoreType.DMA((n,)), pltpu.SemaphoreType.DMA((n,))],
        name="join_halves")(*halves)


def _local_step(x, target, mod, small, big):
    rows, d = x.shape
    cw = d // 2
    shift1, scale1, gate1, shift2, scale2, gate2 = mod
    lr, li, bbr, bbi = small["s5_disc"]
    b_in, c_out, b_out, c_in, mults = _s5_operands(lr, li, bbr, bbi, small["c_re"], small["c_im"])

    u = _norm_mod(x, small["norm1_g"], scale1, shift1, name="norm1_fwd")
    proj = _mm(u, big["w_in"], mode="nn", out_dtype=BF16, name="mm_in")
    sl, cv = _conv_fwd(proj, small["w_dw"], small["b_dw"], small["ln_g"], small["ln_b"], cw=cw)
    y_conv = _mm(sl, big["w_conv_out"], mode="nn", out_dtype=BF16, name="mm_conv_out")
    yg = _s5_fwd(proj, small["d_skip"], b_in, c_out, mults, col0=2 * cw // LANES)
    ya = _mm(yg, big["w_glu_a"], mode="nn", out_dtype=BF16, name="mm_glu_a")
    yb = _mm(yg, big["w_glu_b"], mode="nn", out_dtype=BF16, name="mm_glu_b")
    merged = _merge_fwd(proj, y_conv, ya, yb, cw=cw)
    mo = _mm(merged, big["w_out"], mode="nn", out_dtype=BF16, name="mm_out")
    h1, z = _res_norm(x, mo, gate1, small["norm2_g"], scale2, shift2)
    f1 = _mm(z, big["w_ff1"], mode="nn", out_dtype=BF16, name="mm_ff1")
    ff = _mm(f1, big["w_ff2"], mode="nn", out_dtype=BF16, name="mm_ff2", a_fn=_relu2_bf16)
    dh2, dff, loss, d_final_g, d_gate2 = _final_fwd_bwd(h1, ff, gate2, small["final_g"], target)

    gb = {}
    gb["w_ff2"] = _mm(f1, dff, mode="tn", out_dtype=BF16, name="mm_dw_ff2", a_fn=_relu2_bf16)
    df1 = _mm(dff, big["w_ff2"], mode="nt", out_dtype=BF16, name="mm_d_ff2", extra=f1,
              epi=lambda acc, f: acc * (2.0 * jnp.maximum(f.astype(F32), 0.0)))
    gb["w_ff1"] = _mm(z, df1, mode="tn", out_dtype=BF16, name="mm_dw_ff1", out_gathered=True)
    dz = _mm(df1, big["w_ff1"], mode="nt", out_dtype=F32, name="mm_d_ff1")
    dh1, d_shift2, d_scale2, d_norm2_g, dmo, d_gate1 = _norm_mod_bwd(
        dz, h1, dh2, small["norm2_g"], scale2, gate1, mo, name="norm2_bwd")
    gb["w_out"] = _mm(merged, dmo, mode="tn", out_dtype=BF16, name="mm_dw_out")
    dmerged = _mm(dmo, big["w_out"], mode="nt", out_dtype=BF16, name="mm_d_out")
    dgc, dgs, dy_conv, dya, dyb = _merge_bwd(dmerged, proj, y_conv, ya, yb, cw=cw)
    gb["w_glu_a"] = _mm(yg, dya, mode="tn", out_dtype=BF16, name="mm_dw_glu_a", out_gathered=True)
    gb["w_glu_b"] = _mm(yg, dyb, mode="tn", out_dtype=BF16, name="mm_dw_glu_b", out_gathered=True)
    dyg_a = _mm(dya, big["w_glu_a"], mode="nt", out_dtype=F32, name="mm_d_glu_a")
    dyg = _mm(dyb, big["w_glu_b"], mode="nt", out_dtype=F32, name="mm_d_glu_b", extra=dyg_a,
              epi=lambda acc, e: acc + e)
    gb["w_conv_out"] = _mm(sl, dy_conv, mode="tn", out_dtype=BF16, name="mm_dw_conv_out", out_gathered=True)
    dsl = _mm(dy_conv, big["w_conv_out"], mode="nt", out_dtype=F32, name="mm_d_conv_out")
    dcv, d_ln_g, d_ln_b = _ln_bwd(dsl, cv, small["ln_g"], small["ln_b"])
    dvconv, d_w_dw, d_b_dw = _conv_bwd(dcv, proj, small["w_dw"], cw=cw)
    dvssm, d_d_skip, dbr, dbi, dcr, dci, dlr, dli = _s5_bwd(
        proj, dyg, small["d_skip"], b_in, c_out, b_out, c_in, mults, col0=2 * cw // LANES)
    dproj = jnp.concatenate([dvconv, dvssm, dgc, dgs], axis=1)
    gb["w_in"] = _mm(u, dproj, mode="tn", out_dtype=BF16, name="mm_dw_in", out_gathered=True)
    du = _mm(dproj, big["w_in"], mode="nt", out_dtype=F32, name="mm_d_in")
    grad_x, d_shift1, d_scale1, d_norm1_g = _norm_mod_bwd(
        du, x, dh1, small["norm1_g"], scale1, None, None, name="norm1_bwd")

    sw = lambda m: jnp.swapaxes(m, 1, 2)
    gs = {
        "dmod": jnp.concatenate([d_shift1, d_scale1, d_gate1, d_shift2, d_scale2, d_gate2], axis=1),
        "norm1_g": d_norm1_g, "w_dw": d_w_dw, "b_dw": d_b_dw, "ln_g": d_ln_g, "ln_b": d_ln_b,
        "lam_re": dlr.reshape(-1, SSM_STATE), "lam_im": dli.reshape(-1, SSM_STATE),
        "bb_re": sw(_block_diag_extract(dbr, SSM_GROUP, SSM_STATE)),
        "bb_im": sw(_block_diag_extract(dbi, SSM_GROUP, SSM_STATE)),
        "c_re": sw(_block_diag_extract(dcr, SSM_STATE, SSM_GROUP)),
        "c_im": sw(_block_diag_extract(dci, SSM_STATE, SSM_GROUP)),
        "d_skip": d_d_skip, "norm2_g": d_norm2_g, "final_g": d_final_g,
    }
    return grad_x, gb, gs, loss


WEIGHT_NAMES = ["w_ada", "b_ada", "norm1_g", "w_in", "w_dw", "b_dw", "ln_g", "ln_b", "w_conv_out", "a_re", "a_im",
                "log_dt", "b_re", "b_im", "c_re", "c_im", "d_skip", "w_glu_a", "w_glu_b", "w_out", "norm2_g",
                "w_ff1", "w_ff2", "final_g"]
BIG_NAMES = ["w_in", "w_conv_out", "w_glu_a", "w_glu_b", "w_out", "w_ff1", "w_ff2"]
ROW_SHARDED = ("w_out", "w_ff2")
PACK_TILE = SUBLANES * 1024


def _pack(arrays):
    flats = [a.reshape(-1) for a in arrays]
    offs = []
    total = 0
    for f in flats:
        offs.append(total)
        total += f.shape[0]
    pad = (-total) % PACK_TILE
    if pad:
        flats.append(jnp.zeros((pad,), F32))
    return jnp.concatenate(flats), offs


def _unpack(flat, offs, like):
    return [flat[o:o + a.size].reshape(a.shape) for o, a in zip(offs, like)]


def _gather_w_dw(w_shard):
    k, n = w_shard.shape
    padded = jnp.pad(w_shard, ((0, HALO - k), (0, 0)))
    allw = _gather_small(padded, name="gather_w_dw").reshape(N_CHIPS, 2, HALO, n)[:, 0, :k]
    return jnp.moveaxis(allw, 0, 1).reshape(k, N_CHIPS * n)


def kernel(x, c, w_ada, b_ada, norm1_g, w_in, w_dw, b_dw, ln_g, ln_b, w_conv_out, a_re, a_im, log_dt, b_re, b_im, c_re, c_im, d_skip, w_glu_a, w_glu_b, w_out, norm2_g, w_ff1, w_ff2, final_g, loss_target, m_w_ada, m_b_ada, m_norm1_g, m_w_in, m_w_dw, m_b_dw, m_ln_g, m_ln_b, m_w_conv_out, m_a_re, m_a_im, m_log_dt, m_b_re, m_b_im, m_c_re, m_c_im, m_d_skip, m_w_glu_a, m_w_glu_b, m_w_out, m_norm2_g, m_w_ff1, m_w_ff2, m_final_g, v_w_ada, v_b_ada, v_norm1_g, v_w_in, v_w_dw, v_b_dw, v_ln_g, v_ln_b, v_w_conv_out, v_a_re, v_a_im, v_log_dt, v_b_re, v_b_im, v_c_re, v_c_im, v_d_skip, v_w_glu_a, v_w_glu_b, v_w_out, v_norm2_g, v_w_ff1, v_w_ff2, v_final_g):
    given = dict(locals())
    w = {n: given[n] for n in WEIGHT_NAMES}
    m = {n: given["m_" + n] for n in WEIGHT_NAMES}
    v = {n: given["v_" + n] for n in WEIGHT_NAMES}
    d = x.shape[2]
    xi, yi, ci = _mesh_pos()
    chip = 2 * xi + yi
    dev = 4 * xi + 2 * yi + ci
    my_c = jnp.reshape(ci, (1,)).astype(jnp.int32)

    c_all = _gather_small(c.reshape(SUBLANES, d // SUBLANES), name="gather_c").reshape(N_DEV, d)
    nmod = w_ada.shape[2]
    b_cols = lax.dynamic_slice(b_ada, (0, chip * nmod), (1, nmod))
    mod_part = _ada_fwd(c_all, w_ada[0], b_cols)
    mod_all = _gather_small(mod_part, name="gather_mod").reshape(N_CHIPS, 2, N_DEV, nmod)[:, 0]
    mod_full = jnp.moveaxis(mod_all, 0, 1).reshape(N_DEV, N_CHIPS * nmod)
    mod_row = lax.dynamic_slice(mod_full, (dev, 0), (1, N_CHIPS * nmod))
    mod = [mod_row[:, i * d:(i + 1) * d] for i in range(6)]

    shards = [_cast_bf16(w[n][0], name="cast_" + n) for n in BIG_NAMES]
    gathered = _gather_weights(shards)
    big = {}
    for n, g in zip(BIG_NAMES, gathered):
        big[n] = g.reshape(g.shape[0] * g.shape[1], g.shape[2]) if n in ROW_SHARDED else g

    disc_in = (a_re[0], a_im[0], log_dt[0], b_re[0], b_im[0])
    disc, disc_vjp = jax.vjp(_s5_discretise, *disc_in)
    small = {"norm1_g": norm1_g, "w_dw": _gather_w_dw(w_dw[0]), "b_dw": b_dw, "ln_g": ln_g, "ln_b": ln_b,
             "c_re": c_re[0], "c_im": c_im[0], "d_skip": d_skip, "norm2_g": norm2_g,
             "final_g": final_g[None, :], "s5_disc": disc}

    grad_x, gb, gs, loss = _local_step(x[0], loss_target[0], mod, small, big)

    dws = []
    for n in BIG_NAMES:
        g = gb[n]
        dws.append(g.reshape(N_CHIPS, g.shape[0] // N_CHIPS, g.shape[1]) if n in ROW_SHARDED else g)
    landed = _swap_halves(dws)
    parts = [_add_half(dw, l1, my_c, name="add_half_" + n) for n, dw, l1 in zip(BIG_NAMES, dws, landed)]
    exchanged = _chip_exchange(parts)
    halves = [_sum_leading(e, name="sum_chips_" + n) for n, e in zip(BIG_NAMES, exchanged)]
    grads = dict(zip(BIG_NAMES, _join_halves(halves)))

    small_keys = ["norm1_g", "w_dw", "b_dw", "ln_g", "ln_b", "lam_re", "lam_im", "bb_re", "bb_im", "c_re", "c_im",
                  "d_skip", "norm2_g", "final_g"]
    items = [gs["dmod"], loss[:, 0:1]] + [gs[k] for k in small_keys]
    flat, offs = _pack(items)
    npk = flat.shape[0]
    everyone = _gather_small(flat.reshape(SUBLANES, npk // SUBLANES), name="gather_small_grads")
    total = _sum_leading(everyone.reshape(N_DEV, npk // 1024, 1024), name="sum_small_grads").reshape(npk)
    summed = dict(zip(["dmod", "loss"] + small_keys, _unpack(total, offs, items)))
    dmod_all = everyone.reshape(N_DEV, npk)[:, :6 * d]

    grads["w_ada"] = _ada_bwd(c_all, lax.dynamic_slice(dmod_all, (0, chip * nmod), (N_DEV, nmod)))
    grads["b_ada"] = _sum_leading(dmod_all.reshape(N_DEV, SUBLANES, 6 * d // SUBLANES),
                                  name="sum_b_ada").reshape(1, 6 * d)
    da_re, da_im, dlog_dt, db_re, db_im = disc_vjp(
        (summed["lam_re"], summed["lam_im"], summed["bb_re"], summed["bb_im"]))
    ndw = w_dw.shape[2]
    grads.update({
        "norm1_g": summed["norm1_g"], "w_dw": lax.dynamic_slice(summed["w_dw"], (0, chip * ndw), (CONV_KERNEL, ndw)),
        "b_dw": summed["b_dw"], "ln_g": summed["ln_g"], "ln_b": summed["ln_b"],
        "a_re": da_re, "a_im": da_im, "log_dt": dlog_dt, "b_re": db_re, "b_im": db_im,
        "c_re": summed["c_re"], "c_im": summed["c_im"], "d_skip": summed["d_skip"],
        "norm2_g": summed["norm2_g"], "final_g": summed["final_g"],
    })
    grads = {n: grads[n].reshape(w[n].shape) for n in WEIGHT_NAMES}

    delta, new_m, new_v = {}, {}, {}
    for n in ["w_ada"] + BIG_NAMES:
        shp = w[n].shape
        two_d = lambda a: a.reshape(shp[1], shp[2])
        dl, nm, nv = _adamw(two_d(w[n]), two_d(grads[n]), two_d(m[n]), two_d(v[n]), name="adamw_" + n)
        delta[n], new_m[n], new_v[n] = dl.reshape(shp), nm.reshape(shp), nv.reshape(shp)
    rest = [n for n in WEIGHT_NAMES if n not in delta]
    packs = []
    for src in (w, grads, m, v):
        flat, offs = _pack([src[n] for n in rest])
        packs.append(flat.reshape(-1, 1024))
    outs = _adamw(*packs, name="adamw_small")
    for dst, o in zip((delta, new_m, new_v), outs):
        for n, a in zip(rest, _unpack(o.reshape(-1), offs, [w[k] for k in rest])):
            dst[n] = a

    return (summed["loss"].reshape(()), grad_x[None], *[grads[n] for n in WEIGHT_NAMES],
            *[delta[n] for n in WEIGHT_NAMES], *[new_m[n] for n in WEIGHT_NAMES],
            *[new_v[n] for n in WEIGHT_NAMES])
```

```python
import functools
import math

import jax
import jax.numpy as jnp
from jax import lax
from jax.experimental import pallas as pl
from jax.experimental.pallas import tpu as pltpu

F32 = jnp.float32
BF16 = jnp.bfloat16
EPS = 1e-6
CONV_KERNEL = 31
SSM_GROUP = 16
SSM_STATE = 64
ADAM_LR = 0.001
ADAM_B1 = 0.9
ADAM_B2 = 0.999
ADAM_EPS = 1e-08
ADAM_WD = 0.01
ADAM_STEP = 10

N_CHIPS = 4
N_DEV = 8
VMEM_LIMIT_BYTES = 56 * 1024 * 1024
LANES = 128
SUBLANES = 8
HALO = 32
GROUPS_PER_BLOCK = LANES // SSM_GROUP
STATE_LANES = GROUPS_PER_BLOCK * SSM_STATE
MESH = pl.DeviceIdType.MESH


def _cparams(sem):
    return pltpu.CompilerParams(dimension_semantics=sem, vmem_limit_bytes=VMEM_LIMIT_BYTES)


def _pick(n, pref, mult=LANES):
    if n <= pref:
        return n
    best = None
    for d in range(mult, pref + 1, mult):
        if n % d == 0:
            best = d
    assert best is not None, (n, pref)
    return best


def _sigmoid(v):
    return 1.0 / (1.0 + jnp.exp(-v))


def _gelu_parts(v):
    k0 = math.sqrt(2.0 / math.pi)
    inner = k0 * (v + 0.044715 * v * v * v)
    t = jnp.tanh(inner)
    return k0, t


def _gelu(v):
    _, t = _gelu_parts(v)
    return 0.5 * v * (1.0 + t)


def _gelu_grad(v):
    k0, t = _gelu_parts(v)
    return 0.5 * (1.0 + t) + 0.5 * v * (1.0 - t * t) * k0 * (1.0 + 3.0 * 0.044715 * v * v)


def _relu2_bf16(a):
    t = jnp.maximum(a.astype(F32), 0.0)
    return (t * t).astype(BF16)


class _Carry:
    def __init__(self, inputs, out_shapes, aliases, sem_shapes, start, finish):
        self.inputs = list(inputs)
        self.out_shapes = list(out_shapes)
        self.aliases = dict(aliases)
        self.sem_shapes = list(sem_shapes)
        self.start = start
        self.finish = finish


def _call(body, *, grid, in_specs, out_specs, out_shape, scratch_shapes, semantics, name, args, carry=None):
    n_in, n_out, n_scr = len(in_specs), len(out_specs), len(scratch_shapes)
    if carry is None:
        outs = pl.pallas_call(
            body, grid=grid, in_specs=in_specs, out_specs=out_specs, out_shape=out_shape,
            scratch_shapes=scratch_shapes, compiler_params=_cparams(semantics), name=name)(*args)
        return list(outs), []
    ci, co = len(carry.inputs), len(carry.out_shapes)

    def wrapped(*refs):
        ins, cins = refs[:n_in], refs[n_in:n_in + ci]
        p = n_in + ci
        outs, couts = refs[p:p + n_out], refs[p + n_out:p + n_out + co]
        p += n_out + co
        scr, csems = refs[p:p + n_scr], refs[p + n_scr:]
        first = pl.program_id(0) == 0
        last = pl.program_id(0) == grid[0] - 1
        for ax in range(1, len(grid)):
            first = jnp.logical_and(first, pl.program_id(ax) == 0)
            last = jnp.logical_and(last, pl.program_id(ax) == grid[ax] - 1)

        @pl.when(first)
        def _():
            carry.start(cins, couts, csems)

        body(*ins, *outs, *scr)

        @pl.when(last)
        def _():
            carry.finish(cins, couts, csems)

    any_spec = pl.BlockSpec(memory_space=pl.ANY)
    outs = pl.pallas_call(
        wrapped, grid=grid, in_specs=list(in_specs) + [any_spec] * ci,
        out_specs=list(out_specs) + [any_spec] * co, out_shape=list(out_shape) + carry.out_shapes,
        scratch_shapes=list(scratch_shapes) + carry.sem_shapes,
        input_output_aliases={n_in + i: n_out + o for i, o in carry.aliases.items()},
        compiler_params=_cparams(("arbitrary",) * len(grid)), name=name)(*args, *carry.inputs)
    return list(outs[:n_out]), list(outs[n_out:])


def _run_carry(carry, *, name):
    ci = len(carry.inputs)

    def body(*refs):
        cins, couts, csems = refs[:ci], refs[ci:ci + len(carry.out_shapes)], refs[ci + len(carry.out_shapes):]
        carry.start(cins, couts, csems)
        carry.finish(cins, couts, csems)

    any_spec = pl.BlockSpec(memory_space=pl.ANY)
    outs = pl.pallas_call(
        body, in_specs=[any_spec] * ci, out_specs=[any_spec] * len(carry.out_shapes), out_shape=carry.out_shapes,
        scratch_shapes=carry.sem_shapes, input_output_aliases=carry.aliases, name=name)(*carry.inputs)
    return list(outs)


def _mm(a, b, *, mode, out_dtype, name, out_gathered=False, a_fn=None, epi=None, extra=None,
        bm_pref=1024, bn_pref=1024, bk_pref=512, carry=None):
    gathered = (b.ndim == 3)
    if mode == "nn":
        m, kdim = a.shape
        ns = b.shape[-1]
        n = ns * (N_CHIPS if gathered else 1)
        bm, bn, bk = _pick(m, bm_pref), _pick(ns, bn_pref), _pick(kdim, bk_pref)
        npb = ns // bn
        grid = (m // bm, n // bn, kdim // bk)
        a_spec = pl.BlockSpec((bm, bk), lambda i, j, k: (i, k))
        if gathered:
            b_spec = pl.BlockSpec((None, bk, bn), lambda i, j, k: (j // npb, k, j % npb))
        else:
            b_spec = pl.BlockSpec((bk, bn), lambda i, j, k: (k, j))
        o_spec = pl.BlockSpec((bm, bn), lambda i, j, k: (i, j))
        e_spec = pl.BlockSpec((bm, bn), lambda i, j, k: (i, j))
        out_shape = (m, n)
        acc_shape = (bm, bn)
        dims = (((1,), (0,)), ((), ()))
    elif mode == "nt":
        m = a.shape[0]
        kdim, ns = b.shape[-2], b.shape[-1]
        n = ns * (N_CHIPS if gathered else 1)
        assert a.shape[1] == n
        bm, bko, bnr = _pick(m, bm_pref), _pick(kdim, bn_pref), _pick(ns, bk_pref)
        npb = ns // bnr
        grid = (m // bm, kdim // bko, n // bnr)
        a_spec = pl.BlockSpec((bm, bnr), lambda i, j, k: (i, k))
        if gathered:
            b_spec = pl.BlockSpec((None, bko, bnr), lambda i, j, k: (k // npb, j, k % npb))
        else:
            b_spec = pl.BlockSpec((bko, bnr), lambda i, j, k: (j, k))
        o_spec = pl.BlockSpec((bm, bko), lambda i, j, k: (i, j))
        e_spec = pl.BlockSpec((bm, bko), lambda i, j, k: (i, j))
        out_shape = (m, kdim)
        acc_shape = (bm, bko)
        dims = (((1,), (1,)), ((), ()))
    else:
        m, kdim = a.shape
        n = b.shape[1]
        ns = n // N_CHIPS if out_gathered else n
        bmr, bko, bn = _pick(m, bk_pref), _pick(kdim, bm_pref), _pick(ns, bn_pref)
        npb = ns // bn
        grid = (kdim // bko, n // bn, m // bmr)
        a_spec = pl.BlockSpec((bmr, bko), lambda i, j, k: (k, i))
        b_spec = pl.BlockSpec((bmr, bn), lambda i, j, k: (k, j))
        if out_gathered:
            o_spec = pl.BlockSpec((None, bko, bn), lambda i, j, k: (j // npb, i, j % npb))
            out_shape = (N_CHIPS, kdim, ns)
        else:
            o_spec = pl.BlockSpec((bko, bn), lambda i, j, k: (i, j))
            out_shape = (kdim, n)
        e_spec = None
        acc_shape = (bko, bn)
        dims = (((0,), (0,)), ((), ()))
    nk = grid[2]

    def body(*refs):
        if extra is not None:
            a_ref, b_ref, e_ref, o_ref, acc = refs
        else:
            a_ref, b_ref, o_ref, acc = refs
            e_ref = None
        k = pl.program_id(2)

        @pl.when(k == 0)
        def _():
            acc[...] = jnp.zeros_like(acc)

        av = a_ref[...]
        if a_fn is not None:
            av = a_fn(av)
        acc[...] += lax.dot_general(av, b_ref[...], dims, preferred_element_type=F32)

        @pl.when(k == nk - 1)
        def _():
            r = acc[...]
            if epi is not None:
                r = epi(r, e_ref[...])
            o_ref[...] = r.astype(o_ref.dtype)

    in_specs = [a_spec, b_spec]
    args = [a, b]
    if extra is not None:
        in_specs.append(e_spec)
        args.append(extra)
    outs, carried = _call(body, grid=grid, in_specs=in_specs, out_specs=[o_spec],
                          out_shape=[jax.ShapeDtypeStruct(out_shape, out_dtype)],
                          scratch_shapes=[pltpu.VMEM(acc_shape, F32)],
                          semantics=("parallel", "parallel", "arbitrary"), name=name, args=args, carry=carry)
    return outs[0] if carry is None else (outs[0], carried)


def _row_tile(rows, cols, n_arrays):
    budget = VMEM_LIMIT_BYTES // 3
    cap = min(512, budget // (n_arrays * 2 * cols * 4))
    for t in range(cap - cap % SUBLANES, 0, -SUBLANES):
        if rows % t == 0:
            return t
    return rows


def _norm_mod(x, g, scale, shift, *, name):
    rows, d = x.shape
    tr = _row_tile(rows, d, 3)

    def body(x_ref, g_ref, sc_ref, sh_ref, o_ref):
        xv = x_ref[...]
        r = lax.rsqrt(jnp.mean(xv * xv, axis=-1, keepdims=True) + EPS)
        o_ref[...] = ((xv * r * g_ref[...]) * (1.0 + sc_ref[...]) + sh_ref[...]).astype(o_ref.dtype)

    row = pl.BlockSpec((tr, d), lambda i: (i, 0))
    vec = pl.BlockSpec((1, d), lambda i: (0, 0))
    return pl.pallas_call(
        body, grid=(rows // tr,), in_specs=[row, vec, vec, vec], out_specs=row,
        out_shape=jax.ShapeDtypeStruct((rows, d), BF16),
        compiler_params=_cparams(("parallel",)), name=name)(x, g, scale, shift)


def _conv_fwd(proj, w_dw, b_dw, ln_g, ln_b, *, cw):
    rows = proj.shape[0]
    tt = _pick(rows, 256, HALO)
    hb = tt // HALO

    def body(a_ref, g_ref, ha_ref, hg_ref, w_ref, b_ref, lg_ref, lb_ref, sl_ref, cv_ref, vs):
        i = pl.program_id(0)
        hv = ha_ref[...].astype(F32) * _sigmoid(hg_ref[...].astype(F32))
        vs[pl.ds(0, HALO), :] = jnp.where(i == 0, 0.0, hv)
        vs[pl.ds(HALO, tt), :] = a_ref[...].astype(F32) * _sigmoid(g_ref[...].astype(F32))
        acc = jnp.broadcast_to(b_ref[...], (tt, cw))
        for k in range(CONV_KERNEL):
            acc = acc + w_ref[pl.ds(k, 1), :] * vs[pl.ds(HALO - (CONV_KERNEL - 1) + k, tt), :]
        cv_ref[...] = acc
        mu = jnp.mean(acc, axis=-1, keepdims=True)
        xc = acc - mu
        rstd = lax.rsqrt(jnp.mean(xc * xc, axis=-1, keepdims=True) + EPS)
        ln = xc * rstd * lg_ref[...] + lb_ref[...]
        sl_ref[...] = (ln * _sigmoid(ln)).astype(sl_ref.dtype)

    tile = lambda c: pl.BlockSpec((tt, cw), lambda i, c=c: (i, c))
    halo = lambda c: pl.BlockSpec((HALO, cw), lambda i, c=c: (jnp.maximum(i * hb - 1, 0), c))
    vec = pl.BlockSpec((1, cw), lambda i: (0, 0))
    return pl.pallas_call(
        body, grid=(rows // tt,),
        in_specs=[tile(0), tile(1), halo(0), halo(1),
                  pl.BlockSpec((CONV_KERNEL, cw), lambda i: (0, 0)), vec, vec, vec],
        out_specs=[pl.BlockSpec((tt, cw), lambda i: (i, 0)), pl.BlockSpec((tt, cw), lambda i: (i, 0))],
        out_shape=[jax.ShapeDtypeStruct((rows, cw), BF16), jax.ShapeDtypeStruct((rows, cw), F32)],
        scratch_shapes=[pltpu.VMEM((HALO + tt, cw), F32)],
        compiler_params=_cparams(("parallel",)), name="conv_fwd")(proj, proj, proj, proj, w_dw, b_dw, ln_g, ln_b)


def _ln_bwd(dsl, cv, ln_g, ln_b):
    rows, cw = cv.shape
    tr = _row_tile(rows, cw, 3)

    def body(d_ref, cv_ref, lg_ref, lb_ref, o_ref, dg_ref, db_ref):
        i = pl.program_id(0)

        @pl.when(i == 0)
        def _():
            dg_ref[...] = jnp.zeros_like(dg_ref)
            db_ref[...] = jnp.zeros_like(db_ref)

        x = cv_ref[...]
        mu = jnp.mean(x, axis=-1, keepdims=True)
        xc = x - mu
        rstd = lax.rsqrt(jnp.mean(xc * xc, axis=-1, keepdims=True) + EPS)
        xh = xc * rstd
        ln = xh * lg_ref[...] + lb_ref[...]
        s = _sigmoid(ln)
        dln = d_ref[...].astype(F32) * (s * (1.0 + ln * (1.0 - s)))
        dg_ref[...] += jnp.sum(dln * xh, axis=0, keepdims=True)
        db_ref[...] += jnp.sum(dln, axis=0, keepdims=True)
        dxh = dln * lg_ref[...]
        m1 = jnp.mean(dxh, axis=-1, keepdims=True)
        m2 = jnp.mean(dxh * xh, axis=-1, keepdims=True)
        o_ref[...] = rstd * (dxh - m1 - xh * m2)

    row = pl.BlockSpec((tr, cw), lambda i: (i, 0))
    vec = pl.BlockSpec((1, cw), lambda i: (0, 0))
    return pl.pallas_call(
        body, grid=(rows // tr,), in_specs=[row, row, vec, vec], out_specs=[row, vec, vec],
        out_shape=[jax.ShapeDtypeStruct((rows, cw), F32), jax.ShapeDtypeStruct((1, cw), F32),
                   jax.ShapeDtypeStruct((1, cw), F32)],
        compiler_params=_cparams(("arbitrary",)), name="ln_bwd")(dsl, cv, ln_g, ln_b)


def _conv_bwd(dcv, proj, w_dw, *, cw):
    rows = proj.shape[0]
    tt = _pick(rows, 256, HALO)
    hb = tt // HALO
    nt = rows // tt
    taps = CONV_KERNEL

    def body(d_ref, dn_ref, a_ref, g_ref, ha_ref, hg_ref, w_ref, o_ref, dw_ref, db_ref, vs, ds):
        i = pl.program_id(0)

        @pl.when(i == 0)
        def _():
            dw_ref[...] = jnp.zeros_like(dw_ref)
            db_ref[...] = jnp.zeros_like(db_ref)

        av = a_ref[...].astype(F32)
        sg = _sigmoid(g_ref[...].astype(F32))
        hv = ha_ref[...].astype(F32) * _sigmoid(hg_ref[...].astype(F32))
        vs[pl.ds(0, HALO), :] = jnp.where(i == 0, 0.0, hv)
        vs[pl.ds(HALO, tt), :] = av * sg
        d = d_ref[...]
        ds[pl.ds(0, tt), :] = d
        ds[pl.ds(tt, HALO), :] = jnp.where(i == nt - 1, 0.0, dn_ref[...])
        db_ref[...] += jnp.sum(d, axis=0, keepdims=True)
        dv = jnp.zeros((tt, cw), F32)
        for k in range(taps):
            dv = dv + w_ref[pl.ds(k, 1), :] * ds[pl.ds(taps - 1 - k, tt), :]
            dw_ref[pl.ds(k, 1), :] += jnp.sum(d * vs[pl.ds(HALO - (taps - 1) + k, tt), :], axis=0, keepdims=True)
        o_ref[:, pl.ds(0, cw)] = (dv * sg).astype(o_ref.dtype)
        o_ref[:, pl.ds(cw, cw)] = (dv * av * sg * (1.0 - sg)).astype(o_ref.dtype)

    tile = lambda c: pl.BlockSpec((tt, cw), lambda i, c=c: (i, c))
    halo = lambda c: pl.BlockSpec((HALO, cw), lambda i, c=c: (jnp.maximum(i * hb - 1, 0), c))
    nxt = pl.BlockSpec((HALO, cw), lambda i: (jnp.minimum((i + 1) * hb, nt * hb - 1), 0))
    return pl.pallas_call(
        body, grid=(nt,),
        in_specs=[pl.BlockSpec((tt, cw), lambda i: (i, 0)), nxt, tile(0), tile(1), halo(0), halo(1),
                  pl.BlockSpec((taps, cw), lambda i: (0, 0))],
        out_specs=[pl.BlockSpec((tt, 2 * cw), lambda i: (i, 0)),
                   pl.BlockSpec((taps, cw), lambda i: (0, 0)), pl.BlockSpec((1, cw), lambda i: (0, 0))],
        out_shape=[jax.ShapeDtypeStruct((rows, 2 * cw), BF16), jax.ShapeDtypeStruct((taps, cw), F32),
                   jax.ShapeDtypeStruct((1, cw), F32)],
        scratch_shapes=[pltpu.VMEM((HALO + tt, cw), F32), pltpu.VMEM((tt + HALO, cw), F32)],
        compiler_params=_cparams(("arbitrary",)), name="conv_bwd")(dcv, dcv, proj, proj, proj, proj, w_dw)


def _merge_fwd(proj, y_conv, ya, yb, *, cw):
    rows = proj.shape[0]
    tr = _row_tile(rows, cw, 4)

    def body(gc_ref, gs_ref, yc_ref, ya_ref, yb_ref, o_ref):
        ys = ya_ref[...].astype(F32) * _sigmoid(yb_ref[...].astype(F32))
        o_ref[...] = (_sigmoid(gc_ref[...].astype(F32)) * yc_ref[...].astype(F32)
                      + _sigmoid(gs_ref[...].astype(F32)) * ys).astype(o_ref.dtype)

    blk = lambda off: pl.BlockSpec((tr, cw), lambda i, h, off=off: (i, off + h))
    return pl.pallas_call(
        body, grid=(rows // tr, 2), in_specs=[blk(3), blk(5), blk(0), blk(0), blk(0)], out_specs=blk(0),
        out_shape=jax.ShapeDtypeStruct((rows, 2 * cw), BF16),
        compiler_params=_cparams(("parallel", "parallel")), name="merge_fwd")(proj, proj, y_conv, ya, yb)


def _merge_bwd(dmerged, proj, y_conv, ya, yb, *, cw):
    rows = proj.shape[0]
    tr = _row_tile(rows, cw, 6)

    def body(d_ref, gc_ref, gs_ref, yc_ref, ya_ref, yb_ref, dgc_ref, dgs_ref, dyc_ref, dya_ref, dyb_ref):
        d = d_ref[...].astype(F32)
        sc = _sigmoid(gc_ref[...].astype(F32))
        ss = _sigmoid(gs_ref[...].astype(F32))
        sb = _sigmoid(yb_ref[...].astype(F32))
        yav = ya_ref[...].astype(F32)
        dgc_ref[...] = (d * yc_ref[...].astype(F32) * sc * (1.0 - sc)).astype(dgc_ref.dtype)
        dgs_ref[...] = (d * (yav * sb) * ss * (1.0 - ss)).astype(dgs_ref.dtype)
        dyc_ref[...] = (d * sc).astype(dyc_ref.dtype)
        dys = d * ss
        dya_ref[...] = (dys * sb).astype(dya_ref.dtype)
        dyb_ref[...] = (dys * yav * sb * (1.0 - sb)).astype(dyb_ref.dtype)

    blk = lambda off: pl.BlockSpec((tr, cw), lambda i, h, off=off: (i, off + h))
    o2 = jax.ShapeDtypeStruct((rows, 2 * cw), BF16)
    return pl.pallas_call(
        body, grid=(rows // tr, 2),
        in_specs=[blk(0), blk(3), blk(5), blk(0), blk(0), blk(0)],
        out_specs=[blk(0), blk(0), blk(0), blk(0), blk(0)],
        out_shape=[o2, o2, o2, o2, o2],
        compiler_params=_cparams(("parallel", "parallel")), name="merge_bwd")(dmerged, proj, proj, y_conv, ya, yb)


def _res_norm(x, mo, gate, g, scale, shift):
    rows, d = x.shape
    tr = _row_tile(rows, d, 4)

    def body(x_ref, mo_ref, gt_ref, g_ref, sc_ref, sh_ref, h_ref, z_ref):
        h = x_ref[...] + gt_ref[...] * mo_ref[...].astype(F32)
        h_ref[...] = h
        r = lax.rsqrt(jnp.mean(h * h, axis=-1, keepdims=True) + EPS)
        z_ref[...] = ((h * r * g_ref[...]) * (1.0 + sc_ref[...]) + sh_ref[...]).astype(z_ref.dtype)

    row = pl.BlockSpec((tr, d), lambda i: (i, 0))
    vec = pl.BlockSpec((1, d), lambda i: (0, 0))
    return pl.pallas_call(
        body, grid=(rows // tr,), in_specs=[row, row, vec, vec, vec, vec], out_specs=[row, row],
        out_shape=[jax.ShapeDtypeStruct((rows, d), F32), jax.ShapeDtypeStruct((rows, d), BF16)],
        compiler_params=_cparams(("parallel",)), name="res_norm")(x, mo, gate, g, scale, shift)


def _final_fwd_bwd(h1, ff, gate2, final_g, target):
    rows, d = h1.shape
    tr = _row_tile(rows, d, 5)

    def body(h_ref, ff_ref, gt_ref, fg_ref, t_ref, dh_ref, dff_ref, loss_ref, dfg_ref, dgt_ref):
        i = pl.program_id(0)

        @pl.when(i == 0)
        def _():
            loss_ref[...] = jnp.zeros_like(loss_ref)
            dfg_ref[...] = jnp.zeros_like(dfg_ref)
            dgt_ref[...] = jnp.zeros_like(dgt_ref)

        ffv = ff_ref[...].astype(F32)
        h2 = h_ref[...] + gt_ref[...] * ffv
        r = lax.rsqrt(jnp.mean(h2 * h2, axis=-1, keepdims=True) + EPS)
        y = h2 * r
        e = y * fg_ref[...] - t_ref[...]
        loss_ref[...] += 0.5 * jnp.sum(jnp.mean(e * e, axis=-1, keepdims=True))
        dout = e * (1.0 / d)
        dfg_ref[...] += jnp.sum(dout * y, axis=0, keepdims=True)
        dy = dout * fg_ref[...]
        dh2 = r * (dy - y * jnp.mean(dy * y, axis=-1, keepdims=True))
        dh_ref[...] = dh2
        dgt_ref[...] += jnp.sum(dh2 * ffv, axis=0, keepdims=True)
        dff_ref[...] = (dh2 * gt_ref[...]).astype(dff_ref.dtype)

    row = pl.BlockSpec((tr, d), lambda i: (i, 0))
    vec = pl.BlockSpec((1, d), lambda i: (0, 0))
    return pl.pallas_call(
        body, grid=(rows // tr,), in_specs=[row, row, vec, vec, row],
        out_specs=[row, row, pl.BlockSpec((1, LANES), lambda i: (0, 0)), vec, vec],
        out_shape=[jax.ShapeDtypeStruct((rows, d), F32), jax.ShapeDtypeStruct((rows, d), BF16),
                   jax.ShapeDtypeStruct((1, LANES), F32), jax.ShapeDtypeStruct((1, d), F32),
                   jax.ShapeDtypeStruct((1, d), F32)],
        compiler_params=_cparams(("arbitrary",)), name="final_fwd_bwd")(h1, ff, gate2, final_g, target)


def _norm_mod_bwd(dz, hin, dres, g, scale, gate, mo, *, name):
    rows, d = hin.shape
    with_gate = gate is not None
    tr = _row_tile(rows, d, 6)

    def body(*refs):
        if with_gate:
            (dz_ref, h_ref, dr_ref, g_ref, sc_ref, gt_ref, mo_ref,
             dh_ref, dsh_ref, dsc_ref, dg_ref, dmo_ref, dgt_ref) = refs
        else:
            dz_ref, h_ref, dr_ref, g_ref, sc_ref, dh_ref, dsh_ref, dsc_ref, dg_ref = refs
        i = pl.program_id(0)

        @pl.when(i == 0)
        def _():
            dsh_ref[...] = jnp.zeros_like(dsh_ref)
            dsc_ref[...] = jnp.zeros_like(dsc_ref)
            dg_ref[...] = jnp.zeros_like(dg_ref)
            if with_gate:
                dgt_ref[...] = jnp.zeros_like(dgt_ref)

        dzv = dz_ref[...].astype(F32)
        h = h_ref[...]
        r = lax.rsqrt(jnp.mean(h * h, axis=-1, keepdims=True) + EPS)
        y = h * r
        dsh_ref[...] += jnp.sum(dzv, axis=0, keepdims=True)
        dsc_ref[...] += jnp.sum(dzv * (y * g_ref[...]), axis=0, keepdims=True)
        dn = dzv * (1.0 + sc_ref[...])
        dg_ref[...] += jnp.sum(dn * y, axis=0, keepdims=True)
        dy = dn * g_ref[...]
        dh = dr_ref[...] + r * (dy - y * jnp.mean(dy * y, axis=-1, keepdims=True))
        dh_ref[...] = dh
        if with_gate:
            dmo_ref[...] = (dh * gt_ref[...]).astype(dmo_ref.dtype)
            dgt_ref[...] += jnp.sum(dh * mo_ref[...].astype(F32), axis=0, keepdims=True)

    row = pl.BlockSpec((tr, d), lambda i: (i, 0))
    vec = pl.BlockSpec((1, d), lambda i: (0, 0))
    vshape = jax.ShapeDtypeStruct((1, d), F32)
    in_specs = [row, row, row, vec, vec]
    args = [dz, hin, dres, g, scale]
    out_specs = [row, vec, vec, vec]
    out_shape = [jax.ShapeDtypeStruct((rows, d), F32), vshape, vshape, vshape]
    if with_gate:
        in_specs += [vec, row]
        args += [gate, mo]
        out_specs += [row, vec]
        out_shape += [jax.ShapeDtypeStruct((rows, d), BF16), vshape]
    return pl.pallas_call(
        body, grid=(rows // tr,), in_specs=in_specs, out_specs=out_specs, out_shape=out_shape,
        compiler_params=_cparams(("arbitrary",)), name=name)(*args)


def _s5_discretise(a_re, a_im, log_dt, b_re, b_im):
    dt = jnp.exp(log_dt)[:, None]
    er = jnp.exp(a_re * dt)
    lr = er * jnp.cos(a_im * dt)
    li = er * jnp.sin(a_im * dt)
    den = a_re * a_re + a_im * a_im
    cr = ((lr - 1.0) * a_re + li * a_im) / den
    ci = (li * a_re - (lr - 1.0) * a_im) / den
    bbr = cr[..., None] * b_re - ci[..., None] * b_im
    bbi = cr[..., None] * b_im + ci[..., None] * b_re
    return lr, li, bbr, bbi


def _block_diag(w):
    g, r, c = w.shape
    nb = g // GROUPS_PER_BLOCK
    eye = jnp.eye(GROUPS_PER_BLOCK, dtype=w.dtype)
    w5 = w.reshape(nb, GROUPS_PER_BLOCK, r, 1, c) * eye[None, :, None, :, None]
    return w5.reshape(nb, GROUPS_PER_BLOCK * r, GROUPS_PER_BLOCK * c)


def _block_diag_extract(m, r, c):
    nb = m.shape[0]
    m5 = m.reshape(nb, GROUPS_PER_BLOCK, r, GROUPS_PER_BLOCK, c)
    idx = jnp.arange(GROUPS_PER_BLOCK)
    d = m5[:, idx, :, idx, :]
    return jnp.moveaxis(d, 0, 1).reshape(nb * GROUPS_PER_BLOCK, r, c)


def _split_hi_lo(w):
    hi = w.astype(BF16)
    lo = (w - hi.astype(F32)).astype(BF16)
    return hi, lo


def _scan_multipliers(lr, li):
    pr, pi = [lr], [li]
    for _ in range(SUBLANES - 1):
        pr.append(pr[-1] * lr - pi[-1] * li)
        pi.append(pr[-2] * li + pi[-1] * lr)
    pr = jnp.concatenate(pr, axis=1)
    pi = jnp.concatenate(pi, axis=1)
    rows = jnp.arange(SUBLANES)[None, :, None]
    fr, fi, rr, ri = [], [], [], []
    for s in (1, 2, 4):
        mf = (rows >= s).astype(F32)
        mr = (rows <= SUBLANES - 1 - s).astype(F32)
        fr.append(mf * pr[:, s - 1:s, :])
        fi.append(mf * pi[:, s - 1:s, :])
        rr.append(mr * pr[:, s - 1:s, :])
        ri.append(mr * pi[:, s - 1:s, :])
    fr.append(pr)
    fi.append(pi)
    rr.append(pr[:, ::-1, :])
    ri.append(pi[:, ::-1, :])
    st = lambda xs: jnp.stack(xs, axis=1)
    return st(fr), st(fi), st(rr), st(ri)


def _scan_rows(sre, sim, mul_r, mul_i, n_groups, reverse):
    sgn = -1.0 if reverse else 1.0
    lanes = sre.shape[1]

    def step(k, carry):
        cr, ci = carry
        kk = (n_groups - 1 - k) if reverse else k
        r0 = pl.multiple_of(kk * SUBLANES, SUBLANES)
        xr = sre[pl.ds(r0, SUBLANES), :]
        xi = sim[pl.ds(r0, SUBLANES), :]
        for lvl, s in enumerate((1, 2, 4)):
            sh = (SUBLANES - s) if reverse else s
            nr = pltpu.roll(xr, sh, 0)
            ni = pltpu.roll(xi, sh, 0)
            mr = mul_r[lvl]
            mi = mul_i[lvl] * sgn
            xr, xi = xr + mr * nr - mi * ni, xi + mr * ni + mi * nr
        mr = mul_r[3]
        mi = mul_i[3] * sgn
        xr, xi = xr + mr * cr - mi * ci, xi + mr * ci + mi * cr
        sre[pl.ds(r0, SUBLANES), :] = xr
        sim[pl.ds(r0, SUBLANES), :] = xi
        edge = 0 if reverse else SUBLANES - 1
        ncr = jnp.broadcast_to(xr[edge:edge + 1, :], (SUBLANES, lanes))
        nci = jnp.broadcast_to(xi[edge:edge + 1, :], (SUBLANES, lanes))
        return ncr, nci

    zero = jnp.zeros((SUBLANES, lanes), F32)
    lax.fori_loop(0, n_groups, step, (zero, zero))


def _dot(a, b):
    return jnp.dot(a, b, preferred_element_type=F32)


def _dot3(a, b_hi, b_lo):
    a_hi = a.astype(BF16)
    a_lo = (a - a_hi.astype(F32)).astype(BF16)
    return _dot(a_hi, b_hi) + (_dot(a_hi, b_lo) + _dot(a_lo, b_hi))


def _s5_operands(lr, li, bbr, bbi, c_re, c_im):
    g = lr.shape[0]
    nb = g // GROUPS_PER_BLOCK
    tb = lambda w: jnp.swapaxes(w, 1, 2)
    b_in = [_block_diag(tb(bbr)), _block_diag(tb(bbi))]
    c_out = [_block_diag(tb(c_re)), _block_diag(tb(c_im))]
    b_out = [_block_diag(bbr), _block_diag(bbi)]
    c_in = [_block_diag(c_re), _block_diag(c_im)]
    lam_r = lr.reshape(nb, 1, STATE_LANES)
    lam_i = li.reshape(nb, 1, STATE_LANES)
    mults = _scan_multipliers(lam_r, lam_i)
    split = lambda ws: [p for w in ws for p in _split_hi_lo(w)]
    return split(b_in), split(c_out), split(b_out), split(c_in), mults


def _s5_fwd(proj, d_skip, b_in, c_out, mults, *, col0, carry=None):
    rows = proj.shape[0]
    nb = b_in[0].shape[0]
    tm = _pick(rows, 512, SUBLANES)
    n_tiles = rows // tm
    s_l = STATE_LANES

    def body(u_ref, dk_ref, brh, brl, bih, bil, crh, crl, cih, cil, fr_ref, fi_ref, o_ref, sre, sim):
        for t in range(n_tiles):
            rs = pl.ds(t * tm, tm)
            ub = u_ref[rs, :]
            sre[rs, :] = _dot(ub, brh[...]) + _dot(ub, brl[...])
            sim[rs, :] = _dot(ub, bih[...]) + _dot(ub, bil[...])
        _scan_rows(sre, sim, fr_ref, fi_ref, rows // SUBLANES, False)
        for t in range(n_tiles):
            rs = pl.ds(t * tm, tm)
            y0 = _dot3(sre[rs, :], crh[...], crl[...]) - _dot3(sim[rs, :], cih[...], cil[...])
            y1 = y0 + dk_ref[...] * u_ref[rs, :].astype(F32)
            o_ref[rs, :] = _gelu(y1).astype(o_ref.dtype)

    mat_in = pl.BlockSpec((None, LANES, s_l), lambda g: (g, 0, 0))
    mat_out = pl.BlockSpec((None, s_l, LANES), lambda g: (g, 0, 0))
    mul = pl.BlockSpec((None, 4, SUBLANES, s_l), lambda g: (g, 0, 0, 0))
    outs, carried = _call(
        body, grid=(nb,),
        in_specs=[pl.BlockSpec((rows, LANES), lambda g: (0, col0 + g)), pl.BlockSpec((1, LANES), lambda g: (0, g))]
        + [mat_in] * 4 + [mat_out] * 4 + [mul] * 2,
        out_specs=[pl.BlockSpec((rows, LANES), lambda g: (0, g))],
        out_shape=[jax.ShapeDtypeStruct((rows, nb * LANES), BF16)],
        scratch_shapes=[pltpu.VMEM((rows, s_l), F32), pltpu.VMEM((rows, s_l), F32)],
        semantics=("parallel",), name="s5_fwd", args=[proj, d_skip, *b_in, *c_out, mults[0], mults[1]], carry=carry)
    return outs[0] if carry is None else (outs[0], carried)


def _s5_bwd(proj, dyg, d_skip, b_in, c_out, b_out, c_in, mults, *, col0, carry=None):
    rows = proj.shape[0]
    nb = b_in[0].shape[0]
    tm = _pick(rows, 512, SUBLANES)
    n_tiles = rows // tm
    s_l = STATE_LANES
    n_groups = rows // SUBLANES
    tn = (((0,), (0,)), ((), ()))

    def body(u_ref, dy_ref, dk_ref, brh, brl, bih, bil, crh, crl, cih, cil, borh, borl, boih, boil,
             cirh, cirl, ciih, ciil, fr_ref, fi_ref, rr_ref, ri_ref,
             du_ref, ddk_ref, dbr_ref, dbi_ref, dcr_ref, dci_ref, dlr_ref, dli_ref,
             sre, sim, gre, gim, dy1):
        for t in range(n_tiles):
            rs = pl.ds(t * tm, tm)
            ub = u_ref[rs, :]
            sre[rs, :] = _dot(ub, brh[...]) + _dot(ub, brl[...])
            sim[rs, :] = _dot(ub, bih[...]) + _dot(ub, bil[...])
        _scan_rows(sre, sim, fr_ref, fi_ref, n_groups, False)

        ddk = jnp.zeros((1, LANES), F32)
        dcr = jnp.zeros((s_l, LANES), F32)
        dci = jnp.zeros((s_l, LANES), F32)
        for t in range(n_tiles):
            rs = pl.ds(t * tm, tm)
            sr = sre[rs, :]
            si = sim[rs, :]
            uf = u_ref[rs, :].astype(F32)
            y0 = _dot3(sr, crh[...], crl[...]) - _dot3(si, cih[...], cil[...])
            y1 = y0 + dk_ref[...] * uf
            d1 = dy_ref[rs, :].astype(F32) * _gelu_grad(y1)
            dy1[rs, :] = d1
            ddk = ddk + jnp.sum(d1 * uf, axis=0, keepdims=True)
            d1b = d1.astype(BF16)
            dcr = dcr + lax.dot_general(sr.astype(BF16), d1b, tn, preferred_element_type=F32)
            dci = dci - lax.dot_general(si.astype(BF16), d1b, tn, preferred_element_type=F32)
            gre[rs, :] = _dot3(d1, cirh[...], cirl[...])
            gim[rs, :] = -_dot3(d1, ciih[...], ciil[...])
        ddk_ref[...] = ddk
        dcr_ref[...] = dcr
        dci_ref[...] = dci
        _scan_rows(gre, gim, rr_ref, ri_ref, n_groups, True)

        dbr = jnp.zeros((LANES, s_l), F32)
        dbi = jnp.zeros((LANES, s_l), F32)
        for t in range(n_tiles):
            rs = pl.ds(t * tm, tm)
            gr = gre[rs, :]
            gi = gim[rs, :]
            du = _dot3(gr, borh[...], borl[...]) + _dot3(gi, boih[...], boil[...]) + dy1[rs, :] * dk_ref[...]
            du_ref[rs, :] = du.astype(du_ref.dtype)
            ub = u_ref[rs, :]
            dbr = dbr + lax.dot_general(ub, gr.astype(BF16), tn, preferred_element_type=F32)
            dbi = dbi + lax.dot_general(ub, gi.astype(BF16), tn, preferred_element_type=F32)
        dbr_ref[...] = dbr
        dbi_ref[...] = dbi

        row_id = lax.broadcasted_iota(jnp.int32, (SUBLANES, s_l), 0)

        def lam_step(k, carry):
            ar, ai = carry
            r0 = pl.multiple_of(k * SUBLANES, SUBLANES)
            p0 = pl.multiple_of(jnp.maximum(k - 1, 0) * SUBLANES, SUBLANES)
            keep = jnp.where(k == 0, 0.0, 1.0)
            gr = gre[pl.ds(r0, SUBLANES), :]
            gi = gim[pl.ds(r0, SUBLANES), :]
            pr = jnp.where(row_id == 0, pltpu.roll(sre[pl.ds(p0, SUBLANES), :], 1, 0) * keep,
                           pltpu.roll(sre[pl.ds(r0, SUBLANES), :], 1, 0))
            pi = jnp.where(row_id == 0, pltpu.roll(sim[pl.ds(p0, SUBLANES), :], 1, 0) * keep,
                           pltpu.roll(sim[pl.ds(r0, SUBLANES), :], 1, 0))
            return ar + gr * pr + gi * pi, ai + gi * pr - gr * pi

        zero = jnp.zeros((SUBLANES, s_l), F32)
        ar, ai = lax.fori_loop(0, n_groups, lam_step, (zero, zero))
        dlr_ref[...] = jnp.sum(ar, axis=0, keepdims=True)
        dli_ref[...] = jnp.sum(ai, axis=0, keepdims=True)

    mat_in = pl.BlockSpec((None, LANES, s_l), lambda g: (g, 0, 0))
    mat_out = pl.BlockSpec((None, s_l, LANES), lambda g: (g, 0, 0))
    mul = pl.BlockSpec((None, 4, SUBLANES, s_l), lambda g: (g, 0, 0, 0))
    lam = pl.BlockSpec((None, 1, s_l), lambda g: (g, 0, 0))
    col = pl.BlockSpec((rows, LANES), lambda g: (0, g))
    vec = pl.BlockSpec((1, LANES), lambda g: (0, g))
    outs, carried = _call(
        body, grid=(nb,),
        in_specs=[pl.BlockSpec((rows, LANES), lambda g: (0, col0 + g)), col, vec]
        + [mat_in] * 4 + [mat_out] * 4 + [mat_out] * 4 + [mat_in] * 4 + [mul] * 4,
        out_specs=[col, vec, mat_in, mat_in, mat_out, mat_out, lam, lam],
        out_shape=[jax.ShapeDtypeStruct((rows, nb * LANES), BF16), jax.ShapeDtypeStruct((1, nb * LANES), F32),
                   jax.ShapeDtypeStruct((nb, LANES, s_l), F32), jax.ShapeDtypeStruct((nb, LANES, s_l), F32),
                   jax.ShapeDtypeStruct((nb, s_l, LANES), F32), jax.ShapeDtypeStruct((nb, s_l, LANES), F32),
                   jax.ShapeDtypeStruct((nb, 1, s_l), F32), jax.ShapeDtypeStruct((nb, 1, s_l), F32)],
        scratch_shapes=[pltpu.VMEM((rows, s_l), F32)] * 4 + [pltpu.VMEM((rows, LANES), F32)],
        semantics=("parallel",), name="s5_bwd",
        args=[proj, dyg, d_skip, *b_in, *c_out, *b_out, *c_in, *mults], carry=carry)
    return outs if carry is None else (outs, carried)


def _silu(v):
    return v * _sigmoid(v)


def _ada_fwd(c_all, w_shard, b_cols):
    d, n = w_shard.shape
    bn = _pick(n, 512)

    def body(c_ref, w_ref, b_ref, o_ref):
        ca = _silu(c_ref[...]).astype(BF16)
        o_ref[...] = _dot(ca, w_ref[...].astype(BF16)) + b_ref[...]

    return pl.pallas_call(
        body, grid=(n // bn,),
        in_specs=[pl.BlockSpec((N_DEV, d), lambda j: (0, 0)), pl.BlockSpec((d, bn), lambda j: (0, j)),
                  pl.BlockSpec((1, bn), lambda j: (0, j))],
        out_specs=pl.BlockSpec((N_DEV, bn), lambda j: (0, j)),
        out_shape=jax.ShapeDtypeStruct((N_DEV, n), F32),
        compiler_params=_cparams(("parallel",)), name="ada_fwd")(c_all, w_shard, b_cols)


def _ada_bwd(c_all, dmod_cols):
    d = c_all.shape[1]
    n = dmod_cols.shape[1]
    bn = _pick(n, 512)

    def body(c_ref, g_ref, o_ref):
        ca = _silu(c_ref[...]).astype(BF16)
        o_ref[...] = lax.dot_general(ca, g_ref[...].astype(BF16), (((0,), (0,)), ((), ())),
                                     preferred_element_type=F32)

    return pl.pallas_call(
        body, grid=(n // bn,),
        in_specs=[pl.BlockSpec((N_DEV, d), lambda j: (0, 0)), pl.BlockSpec((N_DEV, bn), lambda j: (0, j))],
        out_specs=pl.BlockSpec((d, bn), lambda j: (0, j)),
        out_shape=jax.ShapeDtypeStruct((d, n), F32),
        compiler_params=_cparams(("parallel",)), name="ada_bwd")(c_all, dmod_cols)


def _cast_bf16(w, *, name):
    rows, cols = w.shape
    tr = _row_tile(rows, cols, 2)

    def body(w_ref, o_ref):
        o_ref[...] = w_ref[...].astype(BF16)

    row = pl.BlockSpec((tr, cols), lambda i: (i, 0))
    return pl.pallas_call(
        body, grid=(rows // tr,), in_specs=[row], out_specs=row,
        out_shape=jax.ShapeDtypeStruct((rows, cols), BF16),
        compiler_params=_cparams(("parallel",)), name=name)(w)


def _adamw(w, g, m, v, *, name):
    rows, cols = w.shape
    tr = _row_tile(rows, cols, 7)
    c1 = 1.0 / (1.0 - ADAM_B1 ** ADAM_STEP)
    c2 = 1.0 / (1.0 - ADAM_B2 ** ADAM_STEP)

    def body(w_ref, g_ref, m_ref, v_ref, d_ref, nm_ref, nv_ref):
        gv = g_ref[...]
        nm = ADAM_B1 * m_ref[...] + (1.0 - ADAM_B1) * gv
        nv = ADAM_B2 * v_ref[...] + (1.0 - ADAM_B2) * (gv * gv)
        nm_ref[...] = nm
        nv_ref[...] = nv
        d_ref[...] = -ADAM_LR * ((nm * c1) / (jnp.sqrt(nv * c2) + ADAM_EPS) + ADAM_WD * w_ref[...])

    row = pl.BlockSpec((tr, cols), lambda i: (i, 0))
    shp = jax.ShapeDtypeStruct((rows, cols), F32)
    return pl.pallas_call(
        body, grid=(rows // tr,), in_specs=[row] * 4, out_specs=[row] * 3, out_shape=[shp] * 3,
        compiler_params=_cparams(("parallel",)), name=name)(w, g, m, v)


def _sum_leading(a, *, name, out_dtype=F32):
    n, rows, cols = a.shape
    tr = _row_tile(rows, cols, n + 1)

    def body(a_ref, o_ref):
        acc = a_ref[0].astype(F32)
        for i in range(1, n):
            acc = acc + a_ref[i].astype(F32)
        o_ref[...] = acc.astype(o_ref.dtype)

    return pl.pallas_call(
        body, grid=(rows // tr,), in_specs=[pl.BlockSpec((n, tr, cols), lambda i: (0, i, 0))],
        out_specs=pl.BlockSpec((tr, cols), lambda i: (i, 0)),
        out_shape=jax.ShapeDtypeStruct((rows, cols), out_dtype),
        compiler_params=_cparams(("parallel",)), name=name)(a)


def _add_half(dw, land, my_c, *, name):
    n, r, cols = dw.shape
    h = r // 2
    tr = _row_tile(h, cols, 3)
    hb = h // tr

    def body(c_ref, a_ref, b_ref, o_ref):
        o_ref[...] = (a_ref[...].astype(F32) + b_ref[...].astype(F32)).astype(o_ref.dtype)

    gs = pltpu.PrefetchScalarGridSpec(
        num_scalar_prefetch=1, grid=(n, hb),
        in_specs=[pl.BlockSpec((None, tr, cols), lambda s, i, c_ref: (s, c_ref[0] * hb + i, 0)),
                  pl.BlockSpec((None, tr, cols), lambda s, i, c_ref: (s, i, 0))],
        out_specs=pl.BlockSpec((None, tr, cols), lambda s, i, c_ref: (s, i, 0)))
    return pl.pallas_call(
        body, grid_spec=gs, out_shape=jax.ShapeDtypeStruct((n, h, cols), BF16),
        compiler_params=_cparams(("parallel", "parallel")), name=name)(my_c, dw, land)


def _mesh_pos():
    return lax.axis_index("x"), lax.axis_index("y"), lax.axis_index("c")


def _other_chips(x, y):
    return [(1 - x, y), (x, 1 - y), (1 - x, 1 - y)]


def _gather_small(blk, *, name):
    m_per, n = blk.shape

    def body(x_ref, out_ref, send_sems, recv_sems, local_sem):
        x, y, c = _mesh_pos()
        me, sibling = (x, y, c), (x, y, 1 - c)
        chips = _other_chips(x, y)

        def rows(px, py, pc):
            return out_ref.at[pl.ds((4 * px + 2 * py + pc) * m_per, m_per), :]

        def copy(k, block, to, src=None):
            return pltpu.make_async_remote_copy(
                src_ref=rows(*block) if src is None else src, dst_ref=rows(*block),
                send_sem=send_sems.at[k], recv_sem=recv_sems.at[k], device_id=to, device_id_type=MESH)

        mine = pltpu.make_async_copy(x_ref, rows(*me), local_sem)
        mine.start()
        first = [copy(0, me, sibling, src=x_ref)]
        first += [copy(1 + j, me, (*chip, c), src=x_ref) for j, chip in enumerate(chips)]
        for cp in first:
            cp.start()
        passed = [copy(4 + j, (*chip, c), sibling) for j, chip in enumerate(chips)]
        for j, chip in enumerate(chips):
            copy(1 + j, (*chip, c), me).wait_recv()
            passed[j].start()
        copy(0, sibling, me).wait_recv()
        for j, chip in enumerate(chips):
            copy(4 + j, (*chip, 1 - c), me).wait_recv()
        for cp in first + passed:
            cp.wait_send()
        mine.wait()

    return pl.pallas_call(
        body, out_shape=jax.ShapeDtypeStruct((N_DEV * m_per, n), blk.dtype),
        in_specs=[pl.BlockSpec(memory_space=pltpu.VMEM)], out_specs=pl.BlockSpec(memory_space=pltpu.VMEM),
        scratch_shapes=[pltpu.SemaphoreType.DMA((7,)), pltpu.SemaphoreType.DMA((7,)), pltpu.SemaphoreType.DMA],
        compiler_params=pltpu.CompilerParams(vmem_limit_bytes=VMEM_LIMIT_BYTES), name=name)(blk)


def _hbm_specs(n):
    return [pl.BlockSpec(memory_space=pl.ANY)] * n


def _gather_weights(shards):
    n = len(shards)

    def body(*refs):
        ins, outs = refs[:n], refs[n:2 * n]
        send_sems, recv_sems, local_sems = refs[2 * n:]
        x, y, c = _mesh_pos()
        me_chip = 2 * x + y
        sibling = (x, y, 1 - c)
        chips = _other_chips(x, y)

        def half(w, chip_idx, pc):
            h = shards[w].shape[0] // 2
            return outs[w].at[chip_idx, pl.ds(pc * h, h), :]

        def copy(w, k, chip_idx, pc, to, src=None):
            dst = half(w, chip_idx, pc)
            return pltpu.make_async_remote_copy(
                src_ref=dst if src is None else src, dst_ref=dst,
                send_sem=send_sems.at[6 * w + k], recv_sem=recv_sems.at[6 * w + k],
                device_id=to, device_id_type=MESH)

        local = [pltpu.make_async_copy(ins[w], outs[w].at[me_chip], local_sems.at[w]) for w in range(n)]
        for cp in local:
            cp.start()
        sends = []
        for w in range(n):
            h = shards[w].shape[0] // 2
            for j, chip in enumerate(chips):
                cp = copy(w, j, me_chip, c, (*chip, c), src=ins[w].at[pl.ds(c * h, h), :])
                cp.start()
                sends.append(cp)
        for w in range(n):
            for j, chip in enumerate(chips):
                chip_idx = 2 * chip[0] + chip[1]
                copy(w, j, chip_idx, c, (x, y, c)).wait_recv()
                cp = copy(w, 3 + j, chip_idx, c, sibling)
                cp.start()
                sends.append(cp)
        for w in range(n):
            for j, chip in enumerate(chips):
                copy(w, 3 + j, 2 * chip[0] + chip[1], 1 - c, (x, y, c)).wait_recv()
        for cp in sends:
            cp.wait_send()
        for cp in local:
            cp.wait()

    return pl.pallas_call(
        body, out_shape=[jax.ShapeDtypeStruct((N_CHIPS,) + s.shape, s.dtype) for s in shards],
        in_specs=_hbm_specs(n), out_specs=_hbm_specs(n),
        scratch_shapes=[pltpu.SemaphoreType.DMA((6 * n,)), pltpu.SemaphoreType.DMA((6 * n,)),
                        pltpu.SemaphoreType.DMA((n,))],
        name="gather_weights")(*shards)


def _swap_halves(dws, *, name):
    n = len(dws)

    def body(*refs):
        ins, outs = refs[:n], refs[n:2 * n]
        send_sems, recv_sems = refs[2 * n:]
        x, y, c = _mesh_pos()
        cps = []
        for w in range(n):
            h = dws[w].shape[1] // 2
            cp = pltpu.make_async_remote_copy(
                src_ref=ins[w].at[:, pl.ds((1 - c) * h, h), :], dst_ref=outs[w],
                send_sem=send_sems.at[w], recv_sem=recv_sems.at[w],
                device_id=(x, y, 1 - c), device_id_type=MESH)
            cp.start()
            cps.append(cp)
        for cp in cps:
            cp.wait()

    return pl.pallas_call(
        body, out_shape=[jax.ShapeDtypeStruct((s.shape[0], s.shape[1] // 2, s.shape[2]), s.dtype) for s in dws],
        in_specs=_hbm_specs(n), out_specs=_hbm_specs(n),
        scratch_shapes=[pltpu.SemaphoreType.DMA((n,)), pltpu.SemaphoreType.DMA((n,))],
        name=name)(*dws)


def _chip_exchange(parts):
    n = len(parts)

    def body(*refs):
        ins, outs = refs[:n], refs[n:2 * n]
        send_sems, recv_sems, local_sems = refs[2 * n:]
        x, y, c = _mesh_pos()
        me_chip = 2 * x + y
        chips = _other_chips(x, y)
        local = [pltpu.make_async_copy(ins[w].at[me_chip], outs[w].at[me_chip], local_sems.at[w]) for w in range(n)]
        for cp in local:
            cp.start()
        cps = []
        for w in range(n):
            for j, chip in enumerate(chips):
                cp = pltpu.make_async_remote_copy(
                    src_ref=ins[w].at[2 * chip[0] + chip[1]], dst_ref=outs[w].at[me_chip],
                    send_sem=send_sems.at[3 * w + j], recv_sem=recv_sems.at[3 * w + j],
                    device_id=(*chip, c), device_id_type=MESH)
                cp.start()
                cps.append((cp, w, j, chip))
        for cp, w, j, chip in cps:
            slot = outs[w].at[2 * chip[0] + chip[1]]
            pltpu.make_async_remote_copy(
                src_ref=slot, dst_ref=slot, send_sem=send_sems.at[3 * w + j], recv_sem=recv_sems.at[3 * w + j],
                device_id=(x, y, c), device_id_type=MESH).wait_recv()
        for cp, _, _, _ in cps:
            cp.wait_send()
        for cp in local:
            cp.wait()

    return pl.pallas_call(
        body, out_shape=[jax.ShapeDtypeStruct(s.shape, s.dtype) for s in parts],
        in_specs=_hbm_specs(n), out_specs=_hbm_specs(n),
        scratch_shapes=[pltpu.SemaphoreType.DMA((3 * n,)), pltpu.SemaphoreType.DMA((3 * n,)),
                        pltpu.SemaphoreType.DMA((n,))],
        name="chip_exchange")(*parts)


def _join_halves(halves):
    n = len(halves)

    def body(*refs):
        ins, outs = refs[:n], refs[n:2 * n]
        send_sems, recv_sems, local_sems = refs[2 * n:]
        x, y, c = _mesh_pos()
        cps, local = [], []
        for w in range(n):
            h = halves[w].shape[0]
            mine = outs[w].at[pl.ds(c * h, h), :]
            lc = pltpu.make_async_copy(ins[w], mine, local_sems.at[w])
            lc.start()
            local.append(lc)
            cp = pltpu.make_async_remote_copy(
                src_ref=ins[w], dst_ref=mine, send_sem=send_sems.at[w], recv_sem=recv_sems.at[w],
                device_id=(x, y, 1 - c), device_id_type=MESH)
            cp.start()
            cps.append(cp)
        for w in range(n):
            h = halves[w].shape[0]
            theirs = outs[w].at[pl.ds((1 - c) * h, h), :]
            pltpu.make_async_remote_copy(
                src_ref=theirs, dst_ref=theirs, send_sem=send_sems.at[w], recv_sem=recv_sems.at[w],
                device_id=(x, y, c), device_id_type=MESH).wait_recv()
        for cp in cps:
            cp.wait_send()
        for lc in local:
            lc.wait()

    return pl.pallas_call(
        body, out_shape=[jax.ShapeDtypeStruct((2 * s.shape[0], s.shape[1]), s.dtype) for s in halves],
        in_specs=_hbm_specs(n), out_specs=_hbm_specs(n),
        scratch_shapes=[pltpu.SemaphoreType.DMA((n,)), pltpu.SemaphoreType.DMA((n,)), pltpu.SemaphoreType.DMA((n,))],
        name="join_halves")(*halves)


def _cast_into_slot(w, chip, *, name):
    rows, cols = w.shape
    tr = _row_tile(rows, cols, 2)

    def body(chip_ref, w_ref, o_ref):
        o_ref[...] = w_ref[...].astype(BF16)

    gs = pltpu.PrefetchScalarGridSpec(
        num_scalar_prefetch=1, grid=(rows // tr,),
        in_specs=[pl.BlockSpec((tr, cols), lambda i, chip_ref: (i, 0))],
        out_specs=pl.BlockSpec((None, tr, cols), lambda i, chip_ref: (chip_ref[0], i, 0)))
    return pl.pallas_call(
        body, grid_spec=gs, out_shape=jax.ShapeDtypeStruct((N_CHIPS, rows, cols), BF16),
        compiler_params=_cparams(("parallel",)), name=name)(chip, w)


def _gather_carry(bufs):
    n = len(bufs)
    sem = pltpu.SemaphoreType.DMA((6 * n,))

    def copies(outs, sems):
        send_sems, recv_sems = sems
        x, y, c = _mesh_pos()
        me_chip = 2 * x + y
        chips = _other_chips(x, y)

        def copy(w, k, chip_idx, pc, to):
            h = bufs[w].shape[1] // 2
            ref = outs[w].at[chip_idx, pl.ds(pc * h, h), :]
            return pltpu.make_async_remote_copy(
                src_ref=ref, dst_ref=ref, send_sem=send_sems.at[6 * w + k], recv_sem=recv_sems.at[6 * w + k],
                device_id=to, device_id_type=MESH)

        out_ici = [[copy(w, j, me_chip, c, (*chip, c)) for j, chip in enumerate(chips)] for w in range(n)]
        in_ici = [[copy(w, j, 2 * chip[0] + chip[1], c, (x, y, c)) for j, chip in enumerate(chips)] for w in range(n)]
        out_d2d = [[copy(w, 3 + j, 2 * chip[0] + chip[1], c, (x, y, 1 - c)) for j, chip in enumerate(chips)]
                   for w in range(n)]
        in_d2d = [[copy(w, 3 + j, 2 * chip[0] + chip[1], 1 - c, (x, y, c)) for j, chip in enumerate(chips)]
                  for w in range(n)]
        return out_ici, in_ici, out_d2d, in_d2d

    def start(ins, outs, sems):
        out_ici, _, _, _ = copies(outs, sems)
        for per_w in out_ici:
            for cp in per_w:
                cp.start()

    def finish(ins, outs, sems):
        out_ici, in_ici, out_d2d, in_d2d = copies(outs, sems)
        for w in range(n):
            for j in range(3):
                in_ici[w][j].wait_recv()
                out_d2d[w][j].start()
        for w in range(n):
            for j in range(3):
                in_d2d[w][j].wait_recv()
        for w in range(n):
            for j in range(3):
                out_ici[w][j].wait_send()
                out_d2d[w][j].wait_send()

    shapes = [jax.ShapeDtypeStruct(b.shape, b.dtype) for b in bufs]
    return _Carry(bufs, shapes, {i: i for i in range(n)}, [sem, sem], start, finish)


def _exchange_carry(parts):
    n = len(parts)
    sem = pltpu.SemaphoreType.DMA((3 * n,))

    def copies(ins, outs, sems):
        send_sems, recv_sems = sems
        x, y, c = _mesh_pos()
        chips = _other_chips(x, y)
        sends, recvs = [], []
        for w in range(n):
            for j, chip in enumerate(chips):
                sends.append(pltpu.make_async_remote_copy(
                    src_ref=ins[w].at[2 * chip[0] + chip[1]], dst_ref=outs[w].at[j],
                    send_sem=send_sems.at[3 * w + j], recv_sem=recv_sems.at[3 * w + j],
                    device_id=(*chip, c), device_id_type=MESH))
                recvs.append(pltpu.make_async_remote_copy(
                    src_ref=outs[w].at[j], dst_ref=outs[w].at[j],
                    send_sem=send_sems.at[3 * w + j], recv_sem=recv_sems.at[3 * w + j],
                    device_id=(x, y, c), device_id_type=MESH))
        return sends, recvs

    def start(ins, outs, sems):
        for cp in copies(ins, outs, sems)[0]:
            cp.start()

    def finish(ins, outs, sems):
        sends, recvs = copies(ins, outs, sems)
        for cp in recvs:
            cp.wait_recv()
        for cp in sends:
            cp.wait_send()

    shapes = [jax.ShapeDtypeStruct((3,) + p.shape[1:], p.dtype) for p in parts]
    return _Carry(parts, shapes, {}, [sem, sem], start, finish)


def _sum_into_half(part, landed, chip, my_c, *, name):
    _, h, cols = part.shape
    tr = _row_tile(h, cols, 5)
    hb = h // tr

    def body(chip_ref, c_ref, p_ref, l_ref, o_ref):
        acc = p_ref[...].astype(F32)
        for j in range(3):
            acc = acc + l_ref[j].astype(F32)
        o_ref[...] = acc

    gs = pltpu.PrefetchScalarGridSpec(
        num_scalar_prefetch=2, grid=(hb,),
        in_specs=[pl.BlockSpec((None, tr, cols), lambda i, chip_ref, c_ref: (chip_ref[0], i, 0)),
                  pl.BlockSpec((3, tr, cols), lambda i, chip_ref, c_ref: (0, i, 0))],
        out_specs=pl.BlockSpec((tr, cols), lambda i, chip_ref, c_ref: (c_ref[0] * hb + i, 0)))
    return pl.pallas_call(
        body, grid_spec=gs, out_shape=jax.ShapeDtypeStruct((2 * h, cols), F32),
        compiler_params=_cparams(("parallel",)), name=name)(chip, my_c, part, landed)


def _join_carry(fulls):
    n = len(fulls)
    sem = pltpu.SemaphoreType.DMA((n,))

    def copies(outs, sems):
        send_sems, recv_sems = sems
        x, y, c = _mesh_pos()
        sends, recvs = [], []
        for w in range(n):
            h = fulls[w].shape[0] // 2
            mine = outs[w].at[pl.ds(c * h, h), :]
            theirs = outs[w].at[pl.ds((1 - c) * h, h), :]
            sends.append(pltpu.make_async_remote_copy(
                src_ref=mine, dst_ref=mine, send_sem=send_sems.at[w], recv_sem=recv_sems.at[w],
                device_id=(x, y, 1 - c), device_id_type=MESH))
            recvs.append(pltpu.make_async_remote_copy(
                src_ref=theirs, dst_ref=theirs, send_sem=send_sems.at[w], recv_sem=recv_sems.at[w],
                device_id=(x, y, c), device_id_type=MESH))
        return sends, recvs

    def start(ins, outs, sems):
        for cp in copies(outs, sems)[0]:
            cp.start()

    def finish(ins, outs, sems):
        sends, recvs = copies(outs, sems)
        for cp in recvs:
            cp.wait_recv()
        for cp in sends:
            cp.wait_send()

    shapes = [jax.ShapeDtypeStruct(f.shape, f.dtype) for f in fulls]
    return _Carry(fulls, shapes, {i: i for i in range(n)}, [sem, sem], start, finish)


class _NoComm:
    def __init__(self, big):
        self.big = big
        self.grads = {}

    def weight(self, name):
        return self.big[name]

    def carry(self, site):
        return None

    def done(self, site, carried):
        pass

    def grad(self, name, dw):
        self.grads[name] = dw


class _MeshComm:
    GATHER_AT = {"mm_in": ["w_conv_out", "w_glu_a", "w_glu_b", "w_out"], "s5_fwd": ["w_ff1"], "mm_ff1": ["w_ff2"]}
    EXCHANGE_AT = {"mm_d_ff2": ["w_ff2"], "mm_d_ff1": ["w_ff1"],
                   "s5_bwd": ["w_out", "w_glu_a", "w_glu_b", "w_conv_out"], "mm_d_in": ["w_in"]}

    def __init__(self, shards, chip, my_c):
        self.chip = chip
        self.my_c = my_c
        self.bufs = {n: _cast_into_slot(s, chip, name="cast_" + n) for n, s in shards.items()}
        self.bufs["w_in"], = _run_carry(_gather_carry([self.bufs["w_in"]]), name="gather_w_in")
        self.raw = {}
        self.parts = {}
        self.halves = {}
        self.pending = {}

    def weight(self, name):
        g = self.bufs[name]
        return g.reshape(g.shape[0] * g.shape[1], g.shape[2]) if name in ROW_SHARDED else g

    def carry(self, site):
        if site in self.GATHER_AT:
            names = self.GATHER_AT[site]
            self.pending[site] = names
            return _gather_carry([self.bufs[n] for n in names])
        names = self.EXCHANGE_AT[site]
        self.pending[site] = names
        raws = [self.raw.pop(n) for n in names]
        landed = _swap_halves(raws, name="swap_halves_" + site)
        parts = [_add_half(dw, l1, self.my_c, name="add_half_" + n) for n, dw, l1 in zip(names, raws, landed)]
        self.parts.update(zip(names, parts))
        return _exchange_carry(parts)

    def done(self, site, carried):
        names = self.pending.pop(site)
        if site in self.GATHER_AT:
            self.bufs.update(zip(names, carried))
            return
        for n, landed in zip(names, carried):
            self.halves[n] = _sum_into_half(self.parts.pop(n), landed, self.chip, self.my_c, name="sum_chips_" + n)

    def grad(self, name, dw):
        if name in ROW_SHARDED:
            dw = dw.reshape(N_CHIPS, dw.shape[0] // N_CHIPS, dw.shape[1])
        self.raw[name] = dw

    def finish(self):
        names = list(self.halves)
        return dict(zip(names, _run_carry(_join_carry([self.halves[n] for n in names]), name="join_halves")))


def _local_step(x, target, mod, small, comm):
    rows, d = x.shape
    cw = d // 2
    shift1, scale1, gate1, shift2, scale2, gate2 = mod
    lr, li, bbr, bbi = small["s5_disc"]
    b_in, c_out, b_out, c_in, mults = _s5_operands(lr, li, bbr, bbi, small["c_re"], small["c_im"])
    wt = comm.weight

    def riding(site, fn, *args, **kwargs):
        carry = comm.carry(site)
        if carry is None:
            return fn(*args, **kwargs)
        out, carried = fn(*args, carry=carry, **kwargs)
        comm.done(site, carried)
        return out

    u = _norm_mod(x, small["norm1_g"], scale1, shift1, name="norm1_fwd")
    proj = riding("mm_in", _mm, u, wt("w_in"), mode="nn", out_dtype=BF16, name="mm_in")
    sl, cv = _conv_fwd(proj, small["w_dw"], small["b_dw"], small["ln_g"], small["ln_b"], cw=cw)
    y_conv = _mm(sl, wt("w_conv_out"), mode="nn", out_dtype=BF16, name="mm_conv_out")
    yg = riding("s5_fwd", _s5_fwd, proj, small["d_skip"], b_in, c_out, mults, col0=2 * cw // LANES)
    ya = _mm(yg, wt("w_glu_a"), mode="nn", out_dtype=BF16, name="mm_glu_a")
    yb = _mm(yg, wt("w_glu_b"), mode="nn", out_dtype=BF16, name="mm_glu_b")
    merged = _merge_fwd(proj, y_conv, ya, yb, cw=cw)
    mo = _mm(merged, wt("w_out"), mode="nn", out_dtype=BF16, name="mm_out")
    h1, z = _res_norm(x, mo, gate1, small["norm2_g"], scale2, shift2)
    f1 = riding("mm_ff1", _mm, z, wt("w_ff1"), mode="nn", out_dtype=BF16, name="mm_ff1")
    ff = _mm(f1, wt("w_ff2"), mode="nn", out_dtype=BF16, name="mm_ff2", a_fn=_relu2_bf16)
    dh2, dff, loss, d_final_g, d_gate2 = _final_fwd_bwd(h1, ff, gate2, small["final_g"], target)

    comm.grad("w_ff2", _mm(f1, dff, mode="tn", out_dtype=BF16, name="mm_dw_ff2", a_fn=_relu2_bf16))
    df1 = riding("mm_d_ff2", _mm, dff, wt("w_ff2"), mode="nt", out_dtype=BF16, name="mm_d_ff2", extra=f1,
                 epi=lambda acc, f: acc * (2.0 * jnp.maximum(f.astype(F32), 0.0)))
    comm.grad("w_ff1", _mm(z, df1, mode="tn", out_dtype=BF16, name="mm_dw_ff1", out_gathered=True))
    dz = riding("mm_d_ff1", _mm, df1, wt("w_ff1"), mode="nt", out_dtype=F32, name="mm_d_ff1")
    dh1, d_shift2, d_scale2, d_norm2_g, dmo, d_gate1 = _norm_mod_bwd(
        dz, h1, dh2, small["norm2_g"], scale2, gate1, mo, name="norm2_bwd")
    comm.grad("w_out", _mm(merged, dmo, mode="tn", out_dtype=BF16, name="mm_dw_out"))
    dmerged = _mm(dmo, wt("w_out"), mode="nt", out_dtype=BF16, name="mm_d_out")
    dgc, dgs, dy_conv, dya, dyb = _merge_bwd(dmerged, proj, y_conv, ya, yb, cw=cw)
    comm.grad("w_glu_a", _mm(yg, dya, mode="tn", out_dtype=BF16, name="mm_dw_glu_a", out_gathered=True))
    comm.grad("w_glu_b", _mm(yg, dyb, mode="tn", out_dtype=BF16, name="mm_dw_glu_b", out_gathered=True))
    dyg_a = _mm(dya, wt("w_glu_a"), mode="nt", out_dtype=F32, name="mm_d_glu_a")
    dyg = _mm(dyb, wt("w_glu_b"), mode="nt", out_dtype=F32, name="mm_d_glu_b", extra=dyg_a,
              epi=lambda acc, e: acc + e)
    comm.grad("w_conv_out", _mm(sl, dy_conv, mode="tn", out_dtype=BF16, name="mm_dw_conv_out", out_gathered=True))
    dsl = _mm(dy_conv, wt("w_conv_out"), mode="nt", out_dtype=F32, name="mm_d_conv_out")
    dcv, d_ln_g, d_ln_b = _ln_bwd(dsl, cv, small["ln_g"], small["ln_b"])
    dvconv, d_w_dw, d_b_dw = _conv_bwd(dcv, proj, small["w_dw"], cw=cw)
    dvssm, d_d_skip, dbr, dbi, dcr, dci, dlr, dli = riding(
        "s5_bwd", _s5_bwd, proj, dyg, small["d_skip"], b_in, c_out, b_out, c_in, mults, col0=2 * cw // LANES)
    dproj = jnp.concatenate([dvconv, dvssm, dgc, dgs], axis=1)
    comm.grad("w_in", _mm(u, dproj, mode="tn", out_dtype=BF16, name="mm_dw_in", out_gathered=True))
    du = riding("mm_d_in", _mm, dproj, wt("w_in"), mode="nt", out_dtype=F32, name="mm_d_in")
    grad_x, d_shift1, d_scale1, d_norm1_g = _norm_mod_bwd(
        du, x, dh1, small["norm1_g"], scale1, None, None, name="norm1_bwd")

    sw = lambda m: jnp.swapaxes(m, 1, 2)
    gs = {
        "dmod": jnp.concatenate([d_shift1, d_scale1, d_gate1, d_shift2, d_scale2, d_gate2], axis=1),
        "norm1_g": d_norm1_g, "w_dw": d_w_dw, "b_dw": d_b_dw, "ln_g": d_ln_g, "ln_b": d_ln_b,
        "lam_re": dlr.reshape(-1, SSM_STATE), "lam_im": dli.reshape(-1, SSM_STATE),
        "bb_re": sw(_block_diag_extract(dbr, SSM_GROUP, SSM_STATE)),
        "bb_im": sw(_block_diag_extract(dbi, SSM_GROUP, SSM_STATE)),
        "c_re": sw(_block_diag_extract(dcr, SSM_STATE, SSM_GROUP)),
        "c_im": sw(_block_diag_extract(dci, SSM_STATE, SSM_GROUP)),
        "d_skip": d_d_skip, "norm2_g": d_norm2_g, "final_g": d_final_g,
    }
    return grad_x, gs, loss


WEIGHT_NAMES = ["w_ada", "b_ada", "norm1_g", "w_in", "w_dw", "b_dw", "ln_g", "ln_b", "w_conv_out", "a_re", "a_im",
                "log_dt", "b_re", "b_im", "c_re", "c_im", "d_skip", "w_glu_a", "w_glu_b", "w_out", "norm2_g",
                "w_ff1", "w_ff2", "final_g"]
BIG_NAMES = ["w_in", "w_conv_out", "w_glu_a", "w_glu_b", "w_out", "w_ff1", "w_ff2"]
ROW_SHARDED = ("w_out", "w_ff2")
PACK_TILE = SUBLANES * 1024


def _pack(arrays):
    flats = [a.reshape(-1) for a in arrays]
    offs = []
    total = 0
    for f in flats:
        offs.append(total)
        total += f.shape[0]
    pad = (-total) % PACK_TILE
    if pad:
        flats.append(jnp.zeros((pad,), F32))
    return jnp.concatenate(flats), offs


def _unpack(flat, offs, like):
    return [flat[o:o + a.size].reshape(a.shape) for o, a in zip(offs, like)]


def _gather_w_dw(w_shard):
    k, n = w_shard.shape
    padded = jnp.pad(w_shard, ((0, HALO - k), (0, 0)))
    allw = _gather_small(padded, name="gather_w_dw").reshape(N_CHIPS, 2, HALO, n)[:, 0, :k]
    return jnp.moveaxis(allw, 0, 1).reshape(k, N_CHIPS * n)


def kernel(x, c, w_ada, b_ada, norm1_g, w_in, w_dw, b_dw, ln_g, ln_b, w_conv_out, a_re, a_im, log_dt, b_re, b_im, c_re, c_im, d_skip, w_glu_a, w_glu_b, w_out, norm2_g, w_ff1, w_ff2, final_g, loss_target, m_w_ada, m_b_ada, m_norm1_g, m_w_in, m_w_dw, m_b_dw, m_ln_g, m_ln_b, m_w_conv_out, m_a_re, m_a_im, m_log_dt, m_b_re, m_b_im, m_c_re, m_c_im, m_d_skip, m_w_glu_a, m_w_glu_b, m_w_out, m_norm2_g, m_w_ff1, m_w_ff2, m_final_g, v_w_ada, v_b_ada, v_norm1_g, v_w_in, v_w_dw, v_b_dw, v_ln_g, v_ln_b, v_w_conv_out, v_a_re, v_a_im, v_log_dt, v_b_re, v_b_im, v_c_re, v_c_im, v_d_skip, v_w_glu_a, v_w_glu_b, v_w_out, v_norm2_g, v_w_ff1, v_w_ff2, v_final_g):
    given = dict(locals())
    w = {n: given[n] for n in WEIGHT_NAMES}
    m = {n: given["m_" + n] for n in WEIGHT_NAMES}
    v = {n: given["v_" + n] for n in WEIGHT_NAMES}
    d = x.shape[2]
    xi, yi, ci = _mesh_pos()
    chip = 2 * xi + yi
    dev = 4 * xi + 2 * yi + ci
    my_c = jnp.reshape(ci, (1,)).astype(jnp.int32)

    c_all = _gather_small(c.reshape(SUBLANES, d // SUBLANES), name="gather_c").reshape(N_DEV, d)
    nmod = w_ada.shape[2]
    b_cols = lax.dynamic_slice(b_ada, (0, chip * nmod), (1, nmod))
    mod_part = _ada_fwd(c_all, w_ada[0], b_cols)
    mod_all = _gather_small(mod_part, name="gather_mod").reshape(N_CHIPS, 2, N_DEV, nmod)[:, 0]
    mod_full = jnp.moveaxis(mod_all, 0, 1).reshape(N_DEV, N_CHIPS * nmod)
    mod_row = lax.dynamic_slice(mod_full, (dev, 0), (1, N_CHIPS * nmod))
    mod = [mod_row[:, i * d:(i + 1) * d] for i in range(6)]

    chip_arr = jnp.reshape(chip, (1,)).astype(jnp.int32)
    comm = _MeshComm({n: w[n][0] for n in BIG_NAMES}, chip_arr, my_c)

    disc_in = (a_re[0], a_im[0], log_dt[0], b_re[0], b_im[0])
    disc, disc_vjp = jax.vjp(_s5_discretise, *disc_in)
    small = {"norm1_g": norm1_g, "w_dw": _gather_w_dw(w_dw[0]), "b_dw": b_dw, "ln_g": ln_g, "ln_b": ln_b,
             "c_re": c_re[0], "c_im": c_im[0], "d_skip": d_skip, "norm2_g": norm2_g,
             "final_g": final_g[None, :], "s5_disc": disc}

    grad_x, gs, loss = _local_step(x[0], loss_target[0], mod, small, comm)
    grads = comm.finish()

    small_keys = ["norm1_g", "w_dw", "b_dw", "ln_g", "ln_b", "lam_re", "lam_im", "bb_re", "bb_im", "c_re", "c_im",
                  "d_skip", "norm2_g", "final_g"]
    items = [gs["dmod"], loss[:, 0:1]] + [gs[k] for k in small_keys]
    flat, offs = _pack(items)
    npk = flat.shape[0]
    everyone = _gather_small(flat.reshape(SUBLANES, npk // SUBLANES), name="gather_small_grads")
    total = _sum_leading(everyone.reshape(N_DEV, npk // 1024, 1024), name="sum_small_grads").reshape(npk)
    summed = dict(zip(["dmod", "loss"] + small_keys, _unpack(total, offs, items)))
    dmod_all = everyone.reshape(N_DEV, npk)[:, :6 * d]

    grads["w_ada"] = _ada_bwd(c_all, lax.dynamic_slice(dmod_all, (0, chip * nmod), (N_DEV, nmod)))
    grads["b_ada"] = _sum_leading(dmod_all.reshape(N_DEV, SUBLANES, 6 * d // SUBLANES),
                                  name="sum_b_ada").reshape(1, 6 * d)
    da_re, da_im, dlog_dt, db_re, db_im = disc_vjp(
        (summed["lam_re"], summed["lam_im"], summed["bb_re"], summed["bb_im"]))
    ndw = w_dw.shape[2]
    grads.update({
        "norm1_g": summed["norm1_g"], "w_dw": lax.dynamic_slice(summed["w_dw"], (0, chip * ndw), (CONV_KERNEL, ndw)),
        "b_dw": summed["b_dw"], "ln_g": summed["ln_g"], "ln_b": summed["ln_b"],
        "a_re": da_re, "a_im": da_im, "log_dt": dlog_dt, "b_re": db_re, "b_im": db_im,
        "c_re": summed["c_re"], "c_im": summed["c_im"], "d_skip": summed["d_skip"],
        "norm2_g": summed["norm2_g"], "final_g": summed["final_g"],
    })
    grads = {n: grads[n].reshape(w[n].shape) for n in WEIGHT_NAMES}

    delta, new_m, new_v = {}, {}, {}
    for n in ["w_ada"] + BIG_NAMES:
        shp = w[n].shape
        two_d = lambda a: a.reshape(shp[1], shp[2])
        dl, nm, nv = _adamw(two_d(w[n]), two_d(grads[n]), two_d(m[n]), two_d(v[n]), name="adamw_" + n)
        delta[n], new_m[n], new_v[n] = dl.reshape(shp), nm.reshape(shp), nv.reshape(shp)
    rest = [n for n in WEIGHT_NAMES if n not in delta]
    packs = []
    for src in (w, grads, m, v):
        flat, offs = _pack([src[n] for n in rest])
        packs.append(flat.reshape(-1, 1024))
    outs = _adamw(*packs, name="adamw_small")
    for dst, o in zip((delta, new_m, new_v), outs):
        for n, a in zip(rest, _unpack(o.reshape(-1), offs, [w[k] for k in rest])):
            dst[n] = a

    return (summed["loss"].reshape(()), grad_x[None], *[grads[n] for n in WEIGHT_NAMES],
            *[delta[n] for n in WEIGHT_NAMES], *[new_m[n] for n in WEIGHT_NAMES],
            *[new_v[n] for n in WEIGHT_NAMES])
```

```python
import functools
import math

import jax
import jax.numpy as jnp
from jax import lax
from jax.experimental import pallas as pl
from jax.experimental.pallas import tpu as pltpu

F32 = jnp.float32
BF16 = jnp.bfloat16
EPS = 1e-6
CONV_KERNEL = 31
SSM_GROUP = 16
SSM_STATE = 64
ADAM_LR = 0.001
ADAM_B1 = 0.9
ADAM_B2 = 0.999
ADAM_EPS = 1e-08
ADAM_WD = 0.01
ADAM_STEP = 10

N_CHIPS = 4
N_DEV = 8
VMEM_LIMIT_BYTES = 56 * 1024 * 1024
LANES = 128
SUBLANES = 8
HALO = 32
GROUPS_PER_BLOCK = LANES // SSM_GROUP
STATE_LANES = GROUPS_PER_BLOCK * SSM_STATE
MESH = pl.DeviceIdType.MESH


def _cparams(sem):
    return pltpu.CompilerParams(dimension_semantics=sem, vmem_limit_bytes=VMEM_LIMIT_BYTES)


def _pick(n, pref, mult=LANES):
    if n <= pref:
        return n
    best = None
    for d in range(mult, pref + 1, mult):
        if n % d == 0:
            best = d
    assert best is not None, (n, pref)
    return best


def _sigmoid(v):
    return 1.0 / (1.0 + jnp.exp(-v))


def _gelu_parts(v):
    k0 = math.sqrt(2.0 / math.pi)
    inner = k0 * (v + 0.044715 * v * v * v)
    t = jnp.tanh(inner)
    return k0, t


def _gelu(v):
    _, t = _gelu_parts(v)
    return 0.5 * v * (1.0 + t)


def _gelu_grad(v):
    k0, t = _gelu_parts(v)
    return 0.5 * (1.0 + t) + 0.5 * v * (1.0 - t * t) * k0 * (1.0 + 3.0 * 0.044715 * v * v)


def _relu2_bf16(a):
    t = jnp.maximum(a.astype(F32), 0.0)
    return (t * t).astype(BF16)


class _Carry:
    def __init__(self, inputs, out_shapes, aliases, sem_shapes, start, finish):
        self.inputs = list(inputs)
        self.out_shapes = list(out_shapes)
        self.aliases = dict(aliases)
        self.sem_shapes = list(sem_shapes)
        self.start = start
        self.finish = finish


def _call(body, *, grid, in_specs, out_specs, out_shape, scratch_shapes, semantics, name, args, carry=None):
    n_in, n_out, n_scr = len(in_specs), len(out_specs), len(scratch_shapes)
    if carry is None:
        outs = pl.pallas_call(
            body, grid=grid, in_specs=in_specs, out_specs=out_specs, out_shape=out_shape,
            scratch_shapes=scratch_shapes, compiler_params=_cparams(semantics), name=name)(*args)
        return list(outs), []
    ci, co = len(carry.inputs), len(carry.out_shapes)

    def wrapped(*refs):
        ins, cins = refs[:n_in], refs[n_in:n_in + ci]
        p = n_in + ci
        outs, couts = refs[p:p + n_out], refs[p + n_out:p + n_out + co]
        p += n_out + co
        scr, csems = refs[p:p + n_scr], refs[p + n_scr:]
        first = pl.program_id(0) == 0
        last = pl.program_id(0) == grid[0] - 1
        for ax in range(1, len(grid)):
            first = jnp.logical_and(first, pl.program_id(ax) == 0)
            last = jnp.logical_and(last, pl.program_id(ax) == grid[ax] - 1)

        @pl.when(first)
        def _():
            carry.start(cins, couts, csems)

        body(*ins, *outs, *scr)

        @pl.when(last)
        def _():
            carry.finish(cins, couts, csems)

    any_spec = pl.BlockSpec(memory_space=pl.ANY)
    outs = pl.pallas_call(
        wrapped, grid=grid, in_specs=list(in_specs) + [any_spec] * ci,
        out_specs=list(out_specs) + [any_spec] * co, out_shape=list(out_shape) + carry.out_shapes,
        scratch_shapes=list(scratch_shapes) + carry.sem_shapes,
        input_output_aliases={n_in + i: n_out + o for i, o in carry.aliases.items()},
        compiler_params=_cparams(("arbitrary",) * len(grid)), name=name)(*args, *carry.inputs)
    return list(outs[:n_out]), list(outs[n_out:])


def _run_carry(carry, *, name):
    ci = len(carry.inputs)

    def body(*refs):
        cins, couts, csems = refs[:ci], refs[ci:ci + len(carry.out_shapes)], refs[ci + len(carry.out_shapes):]
        carry.start(cins, couts, csems)
        carry.finish(cins, couts, csems)

    any_spec = pl.BlockSpec(memory_space=pl.ANY)
    outs = pl.pallas_call(
        body, in_specs=[any_spec] * ci, out_specs=[any_spec] * len(carry.out_shapes), out_shape=carry.out_shapes,
        scratch_shapes=carry.sem_shapes, input_output_aliases=carry.aliases, name=name)(*carry.inputs)
    return list(outs)


def _mm(a, b, *, mode, out_dtype, name, out_gathered=False, a_fn=None, epi=None, extra=None,
        bm_pref=1024, bn_pref=1024, bk_pref=2048, carry=None):
    gathered = (b.ndim == 3)
    if mode == "nn":
        m, kdim = a.shape
        ns = b.shape[-1]
        n = ns * (N_CHIPS if gathered else 1)
        bm, bn, bk = _pick(m, bm_pref), _pick(ns, bn_pref), _pick(kdim, bk_pref)
        npb = ns // bn
        grid = (m // bm, n // bn, kdim // bk)
        a_spec = pl.BlockSpec((bm, bk), lambda i, j, k: (i, k))
        if gathered:
            b_spec = pl.BlockSpec((None, bk, bn), lambda i, j, k: (j // npb, k, j % npb))
        else:
            b_spec = pl.BlockSpec((bk, bn), lambda i, j, k: (k, j))
        o_spec = pl.BlockSpec((bm, bn), lambda i, j, k: (i, j))
        e_spec = pl.BlockSpec((bm, bn), lambda i, j, k: (i, j))
        out_shape = (m, n)
        acc_shape = (bm, bn)
        dims = (((1,), (0,)), ((), ()))
    elif mode == "nt":
        m = a.shape[0]
        kdim, ns = b.shape[-2], b.shape[-1]
        n = ns * (N_CHIPS if gathered else 1)
        assert a.shape[1] == n
        bm, bko, bnr = _pick(m, bm_pref), _pick(kdim, bn_pref), _pick(ns, bk_pref)
        npb = ns // bnr
        grid = (m // bm, kdim // bko, n // bnr)
        a_spec = pl.BlockSpec((bm, bnr), lambda i, j, k: (i, k))
        if gathered:
            b_spec = pl.BlockSpec((None, bko, bnr), lambda i, j, k: (k // npb, j, k % npb))
        else:
            b_spec = pl.BlockSpec((bko, bnr), lambda i, j, k: (j, k))
        o_spec = pl.BlockSpec((bm, bko), lambda i, j, k: (i, j))
        e_spec = pl.BlockSpec((bm, bko), lambda i, j, k: (i, j))
        out_shape = (m, kdim)
        acc_shape = (bm, bko)
        dims = (((1,), (1,)), ((), ()))
    else:
        m, kdim = a.shape
        n = b.shape[1]
        ns = n // N_CHIPS if out_gathered else n
        bmr, bko, bn = _pick(m, bk_pref), _pick(kdim, bm_pref), _pick(ns, bn_pref)
        npb = ns // bn
        grid = (kdim // bko, n // bn, m // bmr)
        a_spec = pl.BlockSpec((bmr, bko), lambda i, j, k: (k, i))
        b_spec = pl.BlockSpec((bmr, bn), lambda i, j, k: (k, j))
        if out_gathered:
            o_spec = pl.BlockSpec((None, bko, bn), lambda i, j, k: (j // npb, i, j % npb))
            out_shape = (N_CHIPS, kdim, ns)
        else:
            o_spec = pl.BlockSpec((bko, bn), lambda i, j, k: (i, j))
            out_shape = (kdim, n)
        e_spec = None
        acc_shape = (bko, bn)
        dims = (((0,), (0,)), ((), ()))
    nk = grid[2]

    def body(*refs):
        if extra is not None:
            a_ref, b_ref, e_ref, o_ref, acc = refs
        else:
            a_ref, b_ref, o_ref, acc = refs
            e_ref = None
        k = pl.program_id(2)
        av = a_ref[...]
        if a_fn is not None:
            av = a_fn(av)
        part = lax.dot_general(av, b_ref[...], dims, preferred_element_type=F32)

        def finish(r):
            if epi is not None:
                r = epi(r, e_ref[...])
            o_ref[...] = r.astype(o_ref.dtype)

        if nk == 1:
            finish(part)
            return

        @pl.when(k == 0)
        def _():
            acc[...] = part

        @pl.when(jnp.logical_and(k > 0, k < nk - 1))
        def _():
            acc[...] += part

        @pl.when(k == nk - 1)
        def _():
            finish(acc[...] + part)

    in_specs = [a_spec, b_spec]
    args = [a, b]
    if extra is not None:
        in_specs.append(e_spec)
        args.append(extra)
    outs, carried = _call(body, grid=grid, in_specs=in_specs, out_specs=[o_spec],
                          out_shape=[jax.ShapeDtypeStruct(out_shape, out_dtype)],
                          scratch_shapes=[pltpu.VMEM(acc_shape, F32)],
                          semantics=("parallel", "parallel", "arbitrary"), name=name, args=args, carry=carry)
    return outs[0] if carry is None else (outs[0], carried)


def _row_tile(rows, cols, n_arrays):
    budget = VMEM_LIMIT_BYTES // 3
    cap = min(512, budget // (n_arrays * 2 * cols * 4))
    for t in range(cap - cap % SUBLANES, 0, -SUBLANES):
        if rows % t == 0:
            return t
    return rows


def _norm_mod(x, g, scale, shift, *, name):
    rows, d = x.shape
    tr = _row_tile(rows, d, 3)

    def body(x_ref, g_ref, sc_ref, sh_ref, o_ref):
        xv = x_ref[...]
        r = lax.rsqrt(jnp.mean(xv * xv, axis=-1, keepdims=True) + EPS)
        o_ref[...] = ((xv * r * g_ref[...]) * (1.0 + sc_ref[...]) + sh_ref[...]).astype(o_ref.dtype)

    row = pl.BlockSpec((tr, d), lambda i: (i, 0))
    vec = pl.BlockSpec((1, d), lambda i: (0, 0))
    return pl.pallas_call(
        body, grid=(rows // tr,), in_specs=[row, vec, vec, vec], out_specs=row,
        out_shape=jax.ShapeDtypeStruct((rows, d), BF16),
        compiler_params=_cparams(("parallel",)), name=name)(x, g, scale, shift)


CONV_CHUNK = 2 * SUBLANES


def _shifted_copies(buf, n):
    for r in range(1, SUBLANES):
        buf[r, pl.ds(0, n - SUBLANES), :] = buf[0, pl.ds(r, n - SUBLANES), :]


def _conv_fwd(proj, w_dw, b_dw, ln_g, ln_b, *, cw):
    rows = proj.shape[0]
    tt = _pick(rows, 256, HALO)
    hb = tt // HALO

    def body(a_ref, g_ref, ha_ref, hg_ref, w_ref, b_ref, lg_ref, lb_ref, sl_ref, cv_ref, vs):
        i = pl.program_id(0)
        hv = ha_ref[...].astype(F32) * _sigmoid(hg_ref[...].astype(F32))
        vs[0, pl.ds(0, HALO), :] = jnp.where(i == 0, 0.0, hv)
        vs[0, pl.ds(HALO, tt), :] = a_ref[...].astype(F32) * _sigmoid(g_ref[...].astype(F32))
        _shifted_copies(vs, HALO + tt)

        def chunk(ci, carry):
            r0 = pl.multiple_of(ci * CONV_CHUNK, CONV_CHUNK)
            acc = jnp.broadcast_to(b_ref[...], (CONV_CHUNK, cw))
            for k in range(CONV_KERNEL):
                q, r = divmod(HALO - (CONV_KERNEL - 1) + k, SUBLANES)
                acc = acc + w_ref[pl.ds(k, 1), :] * vs[r, pl.ds(r0 + q * SUBLANES, CONV_CHUNK), :]
            cv_ref[pl.ds(r0, CONV_CHUNK), :] = acc
            return carry

        lax.fori_loop(0, tt // CONV_CHUNK, chunk, 0)
        acc = cv_ref[...]
        mu = jnp.mean(acc, axis=-1, keepdims=True)
        xc = acc - mu
        rstd = lax.rsqrt(jnp.mean(xc * xc, axis=-1, keepdims=True) + EPS)
        ln = xc * rstd * lg_ref[...] + lb_ref[...]
        sl_ref[...] = (ln * _sigmoid(ln)).astype(sl_ref.dtype)

    tile = lambda c: pl.BlockSpec((tt, cw), lambda i, c=c: (i, c))
    halo = lambda c: pl.BlockSpec((HALO, cw), lambda i, c=c: (jnp.maximum(i * hb - 1, 0), c))
    vec = pl.BlockSpec((1, cw), lambda i: (0, 0))
    return pl.pallas_call(
        body, grid=(rows // tt,),
        in_specs=[tile(0), tile(1), halo(0), halo(1),
                  pl.BlockSpec((CONV_KERNEL, cw), lambda i: (0, 0)), vec, vec, vec],
        out_specs=[pl.BlockSpec((tt, cw), lambda i: (i, 0)), pl.BlockSpec((tt, cw), lambda i: (i, 0))],
        out_shape=[jax.ShapeDtypeStruct((rows, cw), BF16), jax.ShapeDtypeStruct((rows, cw), F32)],
        scratch_shapes=[pltpu.VMEM((SUBLANES, HALO + tt, cw), F32)],
        compiler_params=_cparams(("parallel",)), name="conv_fwd")(proj, proj, proj, proj, w_dw, b_dw, ln_g, ln_b)


def _ln_bwd(dsl, cv, ln_g, ln_b):
    rows, cw = cv.shape
    tr = _row_tile(rows, cw, 3)

    def body(d_ref, cv_ref, lg_ref, lb_ref, o_ref, dg_ref, db_ref):
        i = pl.program_id(0)

        @pl.when(i == 0)
        def _():
            dg_ref[...] = jnp.zeros_like(dg_ref)
            db_ref[...] = jnp.zeros_like(db_ref)

        x = cv_ref[...]
        mu = jnp.mean(x, axis=-1, keepdims=True)
        xc = x - mu
        rstd = lax.rsqrt(jnp.mean(xc * xc, axis=-1, keepdims=True) + EPS)
        xh = xc * rstd
        ln = xh * lg_ref[...] + lb_ref[...]
        s = _sigmoid(ln)
        dln = d_ref[...].astype(F32) * (s * (1.0 + ln * (1.0 - s)))
        dg_ref[...] += jnp.sum(dln * xh, axis=0, keepdims=True)
        db_ref[...] += jnp.sum(dln, axis=0, keepdims=True)
        dxh = dln * lg_ref[...]
        m1 = jnp.mean(dxh, axis=-1, keepdims=True)
        m2 = jnp.mean(dxh * xh, axis=-1, keepdims=True)
        o_ref[...] = rstd * (dxh - m1 - xh * m2)

    row = pl.BlockSpec((tr, cw), lambda i: (i, 0))
    vec = pl.BlockSpec((1, cw), lambda i: (0, 0))
    return pl.pallas_call(
        body, grid=(rows // tr,), in_specs=[row, row, vec, vec], out_specs=[row, vec, vec],
        out_shape=[jax.ShapeDtypeStruct((rows, cw), F32), jax.ShapeDtypeStruct((1, cw), F32),
                   jax.ShapeDtypeStruct((1, cw), F32)],
        compiler_params=_cparams(("arbitrary",)), name="ln_bwd")(dsl, cv, ln_g, ln_b)


def _conv_bwd(dcv, proj, w_dw, *, cw):
    rows = proj.shape[0]
    tt = _pick(rows, 256, HALO)
    hb = tt // HALO
    nt = rows // tt
    taps = CONV_KERNEL

    def body(d_ref, dn_ref, a_ref, g_ref, ha_ref, hg_ref, w_ref, o_ref, dw_ref, db_ref, vs, ds):
        i = pl.program_id(0)

        @pl.when(i == 0)
        def _():
            dw_ref[...] = jnp.zeros_like(dw_ref)
            db_ref[...] = jnp.zeros_like(db_ref)

        hv = ha_ref[...].astype(F32) * _sigmoid(hg_ref[...].astype(F32))
        vs[0, pl.ds(0, HALO), :] = jnp.where(i == 0, 0.0, hv)
        vs[0, pl.ds(HALO, tt), :] = a_ref[...].astype(F32) * _sigmoid(g_ref[...].astype(F32))
        _shifted_copies(vs, HALO + tt)
        ds[0, pl.ds(0, tt), :] = d_ref[...]
        ds[0, pl.ds(tt, HALO), :] = jnp.where(i == nt - 1, 0.0, dn_ref[...])
        _shifted_copies(ds, tt + HALO)
        db_ref[...] += jnp.sum(d_ref[...], axis=0, keepdims=True)
        for k in range(taps):
            q, r = divmod(HALO - (taps - 1) + k, SUBLANES)
            dw_ref[pl.ds(k, 1), :] += jnp.sum(d_ref[...] * vs[r, pl.ds(q * SUBLANES, tt), :], axis=0, keepdims=True)

        def chunk(ci, carry):
            r0 = pl.multiple_of(ci * CONV_CHUNK, CONV_CHUNK)
            dv = jnp.zeros((CONV_CHUNK, cw), F32)
            for k in range(taps):
                q, r = divmod(taps - 1 - k, SUBLANES)
                dv = dv + w_ref[pl.ds(k, 1), :] * ds[r, pl.ds(r0 + q * SUBLANES, CONV_CHUNK), :]
            av = a_ref[pl.ds(r0, CONV_CHUNK), :].astype(F32)
            sg = _sigmoid(g_ref[pl.ds(r0, CONV_CHUNK), :].astype(F32))
            o_ref[pl.ds(r0, CONV_CHUNK), pl.ds(0, cw)] = (dv * sg).astype(o_ref.dtype)
            o_ref[pl.ds(r0, CONV_CHUNK), pl.ds(cw, cw)] = (dv * av * sg * (1.0 - sg)).astype(o_ref.dtype)
            return carry

        lax.fori_loop(0, tt // CONV_CHUNK, chunk, 0)

    tile = lambda c: pl.BlockSpec((tt, cw), lambda i, c=c: (i, c))
    halo = lambda c: pl.BlockSpec((HALO, cw), lambda i, c=c: (jnp.maximum(i * hb - 1, 0), c))
    nxt = pl.BlockSpec((HALO, cw), lambda i: (jnp.minimum((i + 1) * hb, nt * hb - 1), 0))
    return pl.pallas_call(
        body, grid=(nt,),
        in_specs=[pl.BlockSpec((tt, cw), lambda i: (i, 0)), nxt, tile(0), tile(1), halo(0), halo(1),
                  pl.BlockSpec((taps, cw), lambda i: (0, 0))],
        out_specs=[pl.BlockSpec((tt, 2 * cw), lambda i: (i, 0)),
                   pl.BlockSpec((taps, cw), lambda i: (0, 0)), pl.BlockSpec((1, cw), lambda i: (0, 0))],
        out_shape=[jax.ShapeDtypeStruct((rows, 2 * cw), BF16), jax.ShapeDtypeStruct((taps, cw), F32),
                   jax.ShapeDtypeStruct((1, cw), F32)],
        scratch_shapes=[pltpu.VMEM((SUBLANES, HALO + tt, cw), F32), pltpu.VMEM((SUBLANES, tt + HALO, cw), F32)],
        compiler_params=_cparams(("arbitrary",)), name="conv_bwd")(dcv, dcv, proj, proj, proj, proj, w_dw)


def _merge_fwd(proj, y_conv, ya, yb, *, cw):
    rows = proj.shape[0]
    tr = _row_tile(rows, cw, 4)

    def body(gc_ref, gs_ref, yc_ref, ya_ref, yb_ref, o_ref):
        ys = ya_ref[...].astype(F32) * _sigmoid(yb_ref[...].astype(F32))
        o_ref[...] = (_sigmoid(gc_ref[...].astype(F32)) * yc_ref[...].astype(F32)
                      + _sigmoid(gs_ref[...].astype(F32)) * ys).astype(o_ref.dtype)

    blk = lambda off: pl.BlockSpec((tr, cw), lambda i, h, off=off: (i, off + h))
    return pl.pallas_call(
        body, grid=(rows // tr, 2), in_specs=[blk(3), blk(5), blk(0), blk(0), blk(0)], out_specs=blk(0),
        out_shape=jax.ShapeDtypeStruct((rows, 2 * cw), BF16),
        compiler_params=_cparams(("parallel", "parallel")), name="merge_fwd")(proj, proj, y_conv, ya, yb)


def _merge_bwd(dmerged, proj, y_conv, ya, yb, *, cw):
    rows = proj.shape[0]
    tr = _row_tile(rows, cw, 6)

    def body(d_ref, gc_ref, gs_ref, yc_ref, ya_ref, yb_ref, dgc_ref, dgs_ref, dyc_ref, dya_ref, dyb_ref):
        d = d_ref[...].astype(F32)
        sc = _sigmoid(gc_ref[...].astype(F32))
        ss = _sigmoid(gs_ref[...].astype(F32))
        sb = _sigmoid(yb_ref[...].astype(F32))
        yav = ya_ref[...].astype(F32)
        dgc_ref[...] = (d * yc_ref[...].astype(F32) * sc * (1.0 - sc)).astype(dgc_ref.dtype)
        dgs_ref[...] = (d * (yav * sb) * ss * (1.0 - ss)).astype(dgs_ref.dtype)
        dyc_ref[...] = (d * sc).astype(dyc_ref.dtype)
        dys = d * ss
        dya_ref[...] = (dys * sb).astype(dya_ref.dtype)
        dyb_ref[...] = (dys * yav * sb * (1.0 - sb)).astype(dyb_ref.dtype)

    blk = lambda off: pl.BlockSpec((tr, cw), lambda i, h, off=off: (i, off + h))
    o2 = jax.ShapeDtypeStruct((rows, 2 * cw), BF16)
    return pl.pallas_call(
        body, grid=(rows // tr, 2),
        in_specs=[blk(0), blk(3), blk(5), blk(0), blk(0), blk(0)],
        out_specs=[blk(0), blk(0), blk(0), blk(0), blk(0)],
        out_shape=[o2, o2, o2, o2, o2],
        compiler_params=_cparams(("parallel", "parallel")), name="merge_bwd")(dmerged, proj, proj, y_conv, ya, yb)


def _res_norm(x, mo, gate, g, scale, shift):
    rows, d = x.shape
    tr = _row_tile(rows, d, 4)

    def body(x_ref, mo_ref, gt_ref, g_ref, sc_ref, sh_ref, h_ref, z_ref):
        h = x_ref[...] + gt_ref[...] * mo_ref[...].astype(F32)
        h_ref[...] = h
        r = lax.rsqrt(jnp.mean(h * h, axis=-1, keepdims=True) + EPS)
        z_ref[...] = ((h * r * g_ref[...]) * (1.0 + sc_ref[...]) + sh_ref[...]).astype(z_ref.dtype)

    row = pl.BlockSpec((tr, d), lambda i: (i, 0))
    vec = pl.BlockSpec((1, d), lambda i: (0, 0))
    return pl.pallas_call(
        body, grid=(rows // tr,), in_specs=[row, row, vec, vec, vec, vec], out_specs=[row, row],
        out_shape=[jax.ShapeDtypeStruct((rows, d), F32), jax.ShapeDtypeStruct((rows, d), BF16)],
        compiler_params=_cparams(("parallel",)), name="res_norm")(x, mo, gate, g, scale, shift)


def _final_fwd_bwd(h1, ff, gate2, final_g, target):
    rows, d = h1.shape
    tr = _row_tile(rows, d, 5)

    def body(h_ref, ff_ref, gt_ref, fg_ref, t_ref, dh_ref, dff_ref, loss_ref, dfg_ref, dgt_ref):
        i = pl.program_id(0)

        @pl.when(i == 0)
        def _():
            loss_ref[...] = jnp.zeros_like(loss_ref)
            dfg_ref[...] = jnp.zeros_like(dfg_ref)
            dgt_ref[...] = jnp.zeros_like(dgt_ref)

        ffv = ff_ref[...].astype(F32)
        h2 = h_ref[...] + gt_ref[...] * ffv
        r = lax.rsqrt(jnp.mean(h2 * h2, axis=-1, keepdims=True) + EPS)
        y = h2 * r
        e = y * fg_ref[...] - t_ref[...]
        loss_ref[...] += 0.5 * jnp.sum(jnp.mean(e * e, axis=-1, keepdims=True))
        dout = e * (1.0 / d)
        dfg_ref[...] += jnp.sum(dout * y, axis=0, keepdims=True)
        dy = dout * fg_ref[...]
        dh2 = r * (dy - y * jnp.mean(dy * y, axis=-1, keepdims=True))
        dh_ref[...] = dh2
        dgt_ref[...] += jnp.sum(dh2 * ffv, axis=0, keepdims=True)
        dff_ref[...] = (dh2 * gt_ref[...]).astype(dff_ref.dtype)

    row = pl.BlockSpec((tr, d), lambda i: (i, 0))
    vec = pl.BlockSpec((1, d), lambda i: (0, 0))
    return pl.pallas_call(
        body, grid=(rows // tr,), in_specs=[row, row, vec, vec, row],
        out_specs=[row, row, pl.BlockSpec((1, LANES), lambda i: (0, 0)), vec, vec],
        out_shape=[jax.ShapeDtypeStruct((rows, d), F32), jax.ShapeDtypeStruct((rows, d), BF16),
                   jax.ShapeDtypeStruct((1, LANES), F32), jax.ShapeDtypeStruct((1, d), F32),
                   jax.ShapeDtypeStruct((1, d), F32)],
        compiler_params=_cparams(("arbitrary",)), name="final_fwd_bwd")(h1, ff, gate2, final_g, target)


def _norm_mod_bwd(dz, hin, dres, g, scale, gate, mo, *, name):
    rows, d = hin.shape
    with_gate = gate is not None
    tr = _row_tile(rows, d, 6)

    def body(*refs):
        if with_gate:
            (dz_ref, h_ref, dr_ref, g_ref, sc_ref, gt_ref, mo_ref,
             dh_ref, dsh_ref, dsc_ref, dg_ref, dmo_ref, dgt_ref) = refs
        else:
            dz_ref, h_ref, dr_ref, g_ref, sc_ref, dh_ref, dsh_ref, dsc_ref, dg_ref = refs
        i = pl.program_id(0)

        @pl.when(i == 0)
        def _():
            dsh_ref[...] = jnp.zeros_like(dsh_ref)
            dsc_ref[...] = jnp.zeros_like(dsc_ref)
            dg_ref[...] = jnp.zeros_like(dg_ref)
            if with_gate:
                dgt_ref[...] = jnp.zeros_like(dgt_ref)

        dzv = dz_ref[...].astype(F32)
        h = h_ref[...]
        r = lax.rsqrt(jnp.mean(h * h, axis=-1, keepdims=True) + EPS)
        y = h * r
        dsh_ref[...] += jnp.sum(dzv, axis=0, keepdims=True)
        dsc_ref[...] += jnp.sum(dzv * (y * g_ref[...]), axis=0, keepdims=True)
        dn = dzv * (1.0 + sc_ref[...])
        dg_ref[...] += jnp.sum(dn * y, axis=0, keepdims=True)
        dy = dn * g_ref[...]
        dh = dr_ref[...] + r * (dy - y * jnp.mean(dy * y, axis=-1, keepdims=True))
        dh_ref[...] = dh
        if with_gate:
            dmo_ref[...] = (dh * gt_ref[...]).astype(dmo_ref.dtype)
            dgt_ref[...] += jnp.sum(dh * mo_ref[...].astype(F32), axis=0, keepdims=True)

    row = pl.BlockSpec((tr, d), lambda i: (i, 0))
    vec = pl.BlockSpec((1, d), lambda i: (0, 0))
    vshape = jax.ShapeDtypeStruct((1, d), F32)
    in_specs = [row, row, row, vec, vec]
    args = [dz, hin, dres, g, scale]
    out_specs = [row, vec, vec, vec]
    out_shape = [jax.ShapeDtypeStruct((rows, d), F32), vshape, vshape, vshape]
    if with_gate:
        in_specs += [vec, row]
        args += [gate, mo]
        out_specs += [row, vec]
        out_shape += [jax.ShapeDtypeStruct((rows, d), BF16), vshape]
    return pl.pallas_call(
        body, grid=(rows // tr,), in_specs=in_specs, out_specs=out_specs, out_shape=out_shape,
        compiler_params=_cparams(("arbitrary",)), name=name)(*args)


def _s5_discretise(a_re, a_im, log_dt, b_re, b_im):
    dt = jnp.exp(log_dt)[:, None]
    er = jnp.exp(a_re * dt)
    lr = er * jnp.cos(a_im * dt)
    li = er * jnp.sin(a_im * dt)
    den = a_re * a_re + a_im * a_im
    cr = ((lr - 1.0) * a_re + li * a_im) / den
    ci = (li * a_re - (lr - 1.0) * a_im) / den
    bbr = cr[..., None] * b_re - ci[..., None] * b_im
    bbi = cr[..., None] * b_im + ci[..., None] * b_re
    return lr, li, bbr, bbi


def _block_diag(w):
    g, r, c = w.shape
    nb = g // GROUPS_PER_BLOCK
    eye = jnp.eye(GROUPS_PER_BLOCK, dtype=w.dtype)
    w5 = w.reshape(nb, GROUPS_PER_BLOCK, r, 1, c) * eye[None, :, None, :, None]
    return w5.reshape(nb, GROUPS_PER_BLOCK * r, GROUPS_PER_BLOCK * c)


def _block_diag_extract(m, r, c):
    nb = m.shape[0]
    m5 = m.reshape(nb, GROUPS_PER_BLOCK, r, GROUPS_PER_BLOCK, c)
    idx = jnp.arange(GROUPS_PER_BLOCK)
    d = m5[:, idx, :, idx, :]
    return jnp.moveaxis(d, 0, 1).reshape(nb * GROUPS_PER_BLOCK, r, c)


def _scan_multipliers(lr, li):
    pr, pi = [lr], [li]
    for _ in range(SUBLANES - 1):
        pr.append(pr[-1] * lr - pi[-1] * li)
        pi.append(pr[-2] * li + pi[-1] * lr)
    pr = jnp.concatenate(pr, axis=1)
    pi = jnp.concatenate(pi, axis=1)
    rows = jnp.arange(SUBLANES)[None, :, None]
    fr, fi, rr, ri = [], [], [], []
    for s in (1, 2, 4):
        mf = (rows >= s).astype(F32)
        mr = (rows <= SUBLANES - 1 - s).astype(F32)
        fr.append(mf * pr[:, s - 1:s, :])
        fi.append(mf * pi[:, s - 1:s, :])
        rr.append(mr * pr[:, s - 1:s, :])
        ri.append(mr * pi[:, s - 1:s, :])
    fr.append(pr)
    fi.append(pi)
    rr.append(pr[:, ::-1, :])
    ri.append(pi[:, ::-1, :])
    st = lambda xs: jnp.stack(xs, axis=1)
    return st(fr), st(fi), st(rr), st(ri)


def _scan_rows(sre, sim, mul_r, mul_i, n_groups, reverse):
    sgn = -1.0 if reverse else 1.0
    lanes = sre.shape[1]

    def step(k, carry):
        cr, ci = carry
        kk = (n_groups - 1 - k) if reverse else k
        r0 = pl.multiple_of(kk * SUBLANES, SUBLANES)
        xr = sre[pl.ds(r0, SUBLANES), :]
        xi = sim[pl.ds(r0, SUBLANES), :]
        for lvl, s in enumerate((1, 2, 4)):
            sh = (SUBLANES - s) if reverse else s
            nr = pltpu.roll(xr, sh, 0)
            ni = pltpu.roll(xi, sh, 0)
            mr = mul_r[lvl]
            mi = mul_i[lvl] * sgn
            xr, xi = xr + mr * nr - mi * ni, xi + mr * ni + mi * nr
        mr = mul_r[3]
        mi = mul_i[3] * sgn
        xr, xi = xr + mr * cr - mi * ci, xi + mr * ci + mi * cr
        sre[pl.ds(r0, SUBLANES), :] = xr
        sim[pl.ds(r0, SUBLANES), :] = xi
        edge = 0 if reverse else SUBLANES - 1
        ncr = jnp.broadcast_to(xr[edge:edge + 1, :], (SUBLANES, lanes))
        nci = jnp.broadcast_to(xi[edge:edge + 1, :], (SUBLANES, lanes))
        return ncr, nci

    zero = jnp.zeros((SUBLANES, lanes), F32)
    lax.fori_loop(0, n_groups, step, (zero, zero))


def _dot(a, b):
    return jnp.dot(a, b, preferred_element_type=F32)


def _dotf(a, b):
    return _dot(a.astype(BF16), b)


def _s5_operands(lr, li, bbr, bbi, c_re, c_im):
    g = lr.shape[0]
    nb = g // GROUPS_PER_BLOCK
    tb = lambda w: jnp.swapaxes(w, 1, 2)
    b_in = [_block_diag(tb(bbr)), _block_diag(tb(bbi))]
    c_out = [_block_diag(tb(c_re)), _block_diag(tb(c_im))]
    b_out = [_block_diag(bbr), _block_diag(bbi)]
    c_in = [_block_diag(c_re), _block_diag(c_im)]
    lam_r = lr.reshape(nb, 1, STATE_LANES)
    lam_i = li.reshape(nb, 1, STATE_LANES)
    mults = _scan_multipliers(lam_r, lam_i)
    cast = lambda ws: [w.astype(BF16) for w in ws]
    return cast(b_in), cast(c_out), cast(b_out), cast(c_in), mults


def _s5_fwd(proj, d_skip, b_in, c_out, mults, *, col0, carry=None):
    rows = proj.shape[0]
    nb = b_in[0].shape[0]
    tm = _pick(rows, 512, SUBLANES)
    n_tiles = rows // tm
    s_l = STATE_LANES

    def body(u_ref, dk_ref, br, bi, cr, ci, fr_ref, fi_ref, o_ref, sre, sim):
        for t in range(n_tiles):
            rs = pl.ds(t * tm, tm)
            ub = u_ref[rs, :]
            sre[rs, :] = _dot(ub, br[...])
            sim[rs, :] = _dot(ub, bi[...])
        _scan_rows(sre, sim, fr_ref, fi_ref, rows // SUBLANES, False)
        for t in range(n_tiles):
            rs = pl.ds(t * tm, tm)
            y0 = _dotf(sre[rs, :], cr[...]) - _dotf(sim[rs, :], ci[...])
            y1 = y0 + dk_ref[...] * u_ref[rs, :].astype(F32)
            o_ref[rs, :] = _gelu(y1).astype(o_ref.dtype)

    mat_in = pl.BlockSpec((None, LANES, s_l), lambda g: (g, 0, 0))
    mat_out = pl.BlockSpec((None, s_l, LANES), lambda g: (g, 0, 0))
    mul = pl.BlockSpec((None, 4, SUBLANES, s_l), lambda g: (g, 0, 0, 0))
    outs, carried = _call(
        body, grid=(nb,),
        in_specs=[pl.BlockSpec((rows, LANES), lambda g: (0, col0 + g)), pl.BlockSpec((1, LANES), lambda g: (0, g))]
        + [mat_in] * 2 + [mat_out] * 2 + [mul] * 2,
        out_specs=[pl.BlockSpec((rows, LANES), lambda g: (0, g))],
        out_shape=[jax.ShapeDtypeStruct((rows, nb * LANES), BF16)],
        scratch_shapes=[pltpu.VMEM((rows, s_l), F32), pltpu.VMEM((rows, s_l), F32)],
        semantics=("parallel",), name="s5_fwd", args=[proj, d_skip, *b_in, *c_out, mults[0], mults[1]], carry=carry)
    return outs[0] if carry is None else (outs[0], carried)


def _s5_bwd(proj, dyg, d_skip, b_in, c_out, b_out, c_in, mults, *, col0, carry=None):
    rows = proj.shape[0]
    nb = b_in[0].shape[0]
    tm = _pick(rows, 512, SUBLANES)
    n_tiles = rows // tm
    s_l = STATE_LANES
    n_groups = rows // SUBLANES
    tn = (((0,), (0,)), ((), ()))

    def body(u_ref, dy_ref, dk_ref, br, bi, cr, ci, bor, boi, cir, cii, fr_ref, fi_ref, rr_ref, ri_ref,
             du_ref, ddk_ref, dbr_ref, dbi_ref, dcr_ref, dci_ref, dlr_ref, dli_ref,
             sre, sim, gre, gim, dy1):
        for t in range(n_tiles):
            rs = pl.ds(t * tm, tm)
            ub = u_ref[rs, :]
            sre[rs, :] = _dot(ub, br[...])
            sim[rs, :] = _dot(ub, bi[...])
        _scan_rows(sre, sim, fr_ref, fi_ref, n_groups, False)

        ddk = jnp.zeros((1, LANES), F32)
        dcr = jnp.zeros((s_l, LANES), F32)
        dci = jnp.zeros((s_l, LANES), F32)
        for t in range(n_tiles):
            rs = pl.ds(t * tm, tm)
            sr = sre[rs, :]
            si = sim[rs, :]
            uf = u_ref[rs, :].astype(F32)
            y0 = _dotf(sr, cr[...]) - _dotf(si, ci[...])
            y1 = y0 + dk_ref[...] * uf
            d1 = dy_ref[rs, :].astype(F32) * _gelu_grad(y1)
            dy1[rs, :] = d1
            ddk = ddk + jnp.sum(d1 * uf, axis=0, keepdims=True)
            d1b = d1.astype(BF16)
            dcr = dcr + lax.dot_general(sr.astype(BF16), d1b, tn, preferred_element_type=F32)
            dci = dci - lax.dot_general(si.astype(BF16), d1b, tn, preferred_element_type=F32)
            gre[rs, :] = _dot(d1b, cir[...])
            gim[rs, :] = -_dot(d1b, cii[...])
        ddk_ref[...] = ddk
        dcr_ref[...] = dcr
        dci_ref[...] = dci
        _scan_rows(gre, gim, rr_ref, ri_ref, n_groups, True)

        dbr = jnp.zeros((LANES, s_l), F32)
        dbi = jnp.zeros((LANES, s_l), F32)
        for t in range(n_tiles):
            rs = pl.ds(t * tm, tm)
            gr = gre[rs, :]
            gi = gim[rs, :]
            grb = gr.astype(BF16)
            gib = gi.astype(BF16)
            du = _dot(grb, bor[...]) + _dot(gib, boi[...]) + dy1[rs, :] * dk_ref[...]
            du_ref[rs, :] = du.astype(du_ref.dtype)
            ub = u_ref[rs, :]
            dbr = dbr + lax.dot_general(ub, grb, tn, preferred_element_type=F32)
            dbi = dbi + lax.dot_general(ub, gib, tn, preferred_element_type=F32)
        dbr_ref[...] = dbr
        dbi_ref[...] = dbi

        row_id = lax.broadcasted_iota(jnp.int32, (SUBLANES, s_l), 0)

        def lam_step(k, carry):
            ar, ai = carry
            r0 = pl.multiple_of(k * SUBLANES, SUBLANES)
            p0 = pl.multiple_of(jnp.maximum(k - 1, 0) * SUBLANES, SUBLANES)
            keep = jnp.where(k == 0, 0.0, 1.0)
            gr = gre[pl.ds(r0, SUBLANES), :]
            gi = gim[pl.ds(r0, SUBLANES), :]
            pr = jnp.where(row_id == 0, pltpu.roll(sre[pl.ds(p0, SUBLANES), :], 1, 0) * keep,
                           pltpu.roll(sre[pl.ds(r0, SUBLANES), :], 1, 0))
            pi = jnp.where(row_id == 0, pltpu.roll(sim[pl.ds(p0, SUBLANES), :], 1, 0) * keep,
                           pltpu.roll(sim[pl.ds(r0, SUBLANES), :], 1, 0))
            return ar + gr * pr + gi * pi, ai + gi * pr - gr * pi

        zero = jnp.zeros((SUBLANES, s_l), F32)
        ar, ai = lax.fori_loop(0, n_groups, lam_step, (zero, zero))
        dlr_ref[...] = jnp.sum(ar, axis=0, keepdims=True)
        dli_ref[...] = jnp.sum(ai, axis=0, keepdims=True)

    mat_in = pl.BlockSpec((None, LANES, s_l), lambda g: (g, 0, 0))
    mat_out = pl.BlockSpec((None, s_l, LANES), lambda g: (g, 0, 0))
    mul = pl.BlockSpec((None, 4, SUBLANES, s_l), lambda g: (g, 0, 0, 0))
    lam = pl.BlockSpec((None, 1, s_l), lambda g: (g, 0, 0))
    col = pl.BlockSpec((rows, LANES), lambda g: (0, g))
    vec = pl.BlockSpec((1, LANES), lambda g: (0, g))
    outs, carried = _call(
        body, grid=(nb,),
        in_specs=[pl.BlockSpec((rows, LANES), lambda g: (0, col0 + g)), col, vec]
        + [mat_in] * 2 + [mat_out] * 2 + [mat_out] * 2 + [mat_in] * 2 + [mul] * 4,
        out_specs=[col, vec, mat_in, mat_in, mat_out, mat_out, lam, lam],
        out_shape=[jax.ShapeDtypeStruct((rows, nb * LANES), BF16), jax.ShapeDtypeStruct((1, nb * LANES), F32),
                   jax.ShapeDtypeStruct((nb, LANES, s_l), F32), jax.ShapeDtypeStruct((nb, LANES, s_l), F32),
                   jax.ShapeDtypeStruct((nb, s_l, LANES), F32), jax.ShapeDtypeStruct((nb, s_l, LANES), F32),
                   jax.ShapeDtypeStruct((nb, 1, s_l), F32), jax.ShapeDtypeStruct((nb, 1, s_l), F32)],
        scratch_shapes=[pltpu.VMEM((rows, s_l), F32)] * 4 + [pltpu.VMEM((rows, LANES), F32)],
        semantics=("parallel",), name="s5_bwd",
        args=[proj, dyg, d_skip, *b_in, *c_out, *b_out, *c_in, *mults], carry=carry)
    return outs if carry is None else (outs, carried)


def _silu(v):
    return v * _sigmoid(v)


def _ada_fwd(c_all, w_shard, b_cols):
    d, n = w_shard.shape
    bn = _pick(n, 512)

    def body(c_ref, w_ref, b_ref, o_ref):
        ca = _silu(c_ref[...]).astype(BF16)
        o_ref[...] = _dot(ca, w_ref[...].astype(BF16)) + b_ref[...]

    return pl.pallas_call(
        body, grid=(n // bn,),
        in_specs=[pl.BlockSpec((N_DEV, d), lambda j: (0, 0)), pl.BlockSpec((d, bn), lambda j: (0, j)),
                  pl.BlockSpec((1, bn), lambda j: (0, j))],
        out_specs=pl.BlockSpec((N_DEV, bn), lambda j: (0, j)),
        out_shape=jax.ShapeDtypeStruct((N_DEV, n), F32),
        compiler_params=_cparams(("parallel",)), name="ada_fwd")(c_all, w_shard, b_cols)


def _ada_bwd(c_all, dmod_cols):
    d = c_all.shape[1]
    n = dmod_cols.shape[1]
    bn = _pick(n, 512)

    def body(c_ref, g_ref, o_ref):
        ca = _silu(c_ref[...]).astype(BF16)
        o_ref[...] = lax.dot_general(ca, g_ref[...].astype(BF16), (((0,), (0,)), ((), ())),
                                     preferred_element_type=F32)

    return pl.pallas_call(
        body, grid=(n // bn,),
        in_specs=[pl.BlockSpec((N_DEV, d), lambda j: (0, 0)), pl.BlockSpec((N_DEV, bn), lambda j: (0, j))],
        out_specs=pl.BlockSpec((d, bn), lambda j: (0, j)),
        out_shape=jax.ShapeDtypeStruct((d, n), F32),
        compiler_params=_cparams(("parallel",)), name="ada_bwd")(c_all, dmod_cols)


def _cast_bf16(w, *, name):
    rows, cols = w.shape
    tr = _row_tile(rows, cols, 2)

    def body(w_ref, o_ref):
        o_ref[...] = w_ref[...].astype(BF16)

    row = pl.BlockSpec((tr, cols), lambda i: (i, 0))
    return pl.pallas_call(
        body, grid=(rows // tr,), in_specs=[row], out_specs=row,
        out_shape=jax.ShapeDtypeStruct((rows, cols), BF16),
        compiler_params=_cparams(("parallel",)), name=name)(w)


def _adamw(w, g, m, v, *, name):
    rows, cols = w.shape
    tr = _row_tile(rows, cols, 7)
    c1 = 1.0 / (1.0 - ADAM_B1 ** ADAM_STEP)
    c2 = 1.0 / (1.0 - ADAM_B2 ** ADAM_STEP)

    def body(w_ref, g_ref, m_ref, v_ref, d_ref, nm_ref, nv_ref):
        gv = g_ref[...]
        nm = ADAM_B1 * m_ref[...] + (1.0 - ADAM_B1) * gv
        nv = ADAM_B2 * v_ref[...] + (1.0 - ADAM_B2) * (gv * gv)
        nm_ref[...] = nm
        nv_ref[...] = nv
        d_ref[...] = -ADAM_LR * ((nm * c1) / (jnp.sqrt(nv * c2) + ADAM_EPS) + ADAM_WD * w_ref[...])

    row = pl.BlockSpec((tr, cols), lambda i: (i, 0))
    shp = jax.ShapeDtypeStruct((rows, cols), F32)
    return pl.pallas_call(
        body, grid=(rows // tr,), in_specs=[row] * 4, out_specs=[row] * 3, out_shape=[shp] * 3,
        compiler_params=_cparams(("parallel",)), name=name)(w, g, m, v)


def _sum_leading(a, *, name, out_dtype=F32):
    n, rows, cols = a.shape
    tr = _row_tile(rows, cols, n + 1)

    def body(a_ref, o_ref):
        acc = a_ref[0].astype(F32)
        for i in range(1, n):
            acc = acc + a_ref[i].astype(F32)
        o_ref[...] = acc.astype(o_ref.dtype)

    return pl.pallas_call(
        body, grid=(rows // tr,), in_specs=[pl.BlockSpec((n, tr, cols), lambda i: (0, i, 0))],
        out_specs=pl.BlockSpec((tr, cols), lambda i: (i, 0)),
        out_shape=jax.ShapeDtypeStruct((rows, cols), out_dtype),
        compiler_params=_cparams(("parallel",)), name=name)(a)


def _add_half(dw, land, my_c, *, name):
    n, r, cols = dw.shape
    h = r // 2
    tr = _row_tile(h, cols, 3)
    hb = h // tr

    def body(c_ref, a_ref, b_ref, o_ref):
        o_ref[...] = (a_ref[...].astype(F32) + b_ref[...].astype(F32)).astype(o_ref.dtype)

    gs = pltpu.PrefetchScalarGridSpec(
        num_scalar_prefetch=1, grid=(n, hb),
        in_specs=[pl.BlockSpec((None, tr, cols), lambda s, i, c_ref: (s, c_ref[0] * hb + i, 0)),
                  pl.BlockSpec((None, tr, cols), lambda s, i, c_ref: (s, i, 0))],
        out_specs=pl.BlockSpec((None, tr, cols), lambda s, i, c_ref: (s, i, 0)))
    return pl.pallas_call(
        body, grid_spec=gs, out_shape=jax.ShapeDtypeStruct((n, h, cols), BF16),
        compiler_params=_cparams(("parallel", "parallel")), name=name)(my_c, dw, land)


def _mesh_pos():
    return lax.axis_index("x"), lax.axis_index("y"), lax.axis_index("c")


def _other_chips(x, y):
    return [(1 - x, y), (x, 1 - y), (1 - x, 1 - y)]


def _gather_small(blk, *, name):
    m_per, n = blk.shape

    def body(x_ref, out_ref, send_sems, recv_sems, local_sem):
        x, y, c = _mesh_pos()
        me, sibling = (x, y, c), (x, y, 1 - c)
        chips = _other_chips(x, y)

        def rows(px, py, pc):
            return out_ref.at[pl.ds((4 * px + 2 * py + pc) * m_per, m_per), :]

        def copy(k, block, to, src=None):
            return pltpu.make_async_remote_copy(
                src_ref=rows(*block) if src is None else src, dst_ref=rows(*block),
                send_sem=send_sems.at[k], recv_sem=recv_sems.at[k], device_id=to, device_id_type=MESH)

        mine = pltpu.make_async_copy(x_ref, rows(*me), local_sem)
        mine.start()
        first = [copy(0, me, sibling, src=x_ref)]
        first += [copy(1 + j, me, (*chip, c), src=x_ref) for j, chip in enumerate(chips)]
        for cp in first:
            cp.start()
        passed = [copy(4 + j, (*chip, c), sibling) for j, chip in enumerate(chips)]
        for j, chip in enumerate(chips):
            copy(1 + j, (*chip, c), me).wait_recv()
            passed[j].start()
        copy(0, sibling, me).wait_recv()
        for j, chip in enumerate(chips):
            copy(4 + j, (*chip, 1 - c), me).wait_recv()
        for cp in first + passed:
            cp.wait_send()
        mine.wait()

    return pl.pallas_call(
        body, out_shape=jax.ShapeDtypeStruct((N_DEV * m_per, n), blk.dtype),
        in_specs=[pl.BlockSpec(memory_space=pltpu.VMEM)], out_specs=pl.BlockSpec(memory_space=pltpu.VMEM),
        scratch_shapes=[pltpu.SemaphoreType.DMA((7,)), pltpu.SemaphoreType.DMA((7,)), pltpu.SemaphoreType.DMA],
        compiler_params=pltpu.CompilerParams(vmem_limit_bytes=VMEM_LIMIT_BYTES), name=name)(blk)


def _hbm_specs(n):
    return [pl.BlockSpec(memory_space=pl.ANY)] * n


def _gather_weights(shards):
    n = len(shards)

    def body(*refs):
        ins, outs = refs[:n], refs[n:2 * n]
        send_sems, recv_sems, local_sems = refs[2 * n:]
        x, y, c = _mesh_pos()
        me_chip = 2 * x + y
        sibling = (x, y, 1 - c)
        chips = _other_chips(x, y)

        def half(w, chip_idx, pc):
            h = shards[w].shape[0] // 2
            return outs[w].at[chip_idx, pl.ds(pc * h, h), :]

        def copy(w, k, chip_idx, pc, to, src=None):
            dst = half(w, chip_idx, pc)
            return pltpu.make_async_remote_copy(
                src_ref=dst if src is None else src, dst_ref=dst,
                send_sem=send_sems.at[6 * w + k], recv_sem=recv_sems.at[6 * w + k],
                device_id=to, device_id_type=MESH)

        local = [pltpu.make_async_copy(ins[w], outs[w].at[me_chip], local_sems.at[w]) for w in range(n)]
        for cp in local:
            cp.start()
        sends = []
        for w in range(n):
            h = shards[w].shape[0] // 2
            for j, chip in enumerate(chips):
                cp = copy(w, j, me_chip, c, (*chip, c), src=ins[w].at[pl.ds(c * h, h), :])
                cp.start()
                sends.append(cp)
        for w in range(n):
            for j, chip in enumerate(chips):
                chip_idx = 2 * chip[0] + chip[1]
                copy(w, j, chip_idx, c, (x, y, c)).wait_recv()
                cp = copy(w, 3 + j, chip_idx, c, sibling)
                cp.start()
                sends.append(cp)
        for w in range(n):
            for j, chip in enumerate(chips):
                copy(w, 3 + j, 2 * chip[0] + chip[1], 1 - c, (x, y, c)).wait_recv()
        for cp in sends:
            cp.wait_send()
        for cp in local:
            cp.wait()

    return pl.pallas_call(
        body, out_shape=[jax.ShapeDtypeStruct((N_CHIPS,) + s.shape, s.dtype) for s in shards],
        in_specs=_hbm_specs(n), out_specs=_hbm_specs(n),
        scratch_shapes=[pltpu.SemaphoreType.DMA((6 * n,)), pltpu.SemaphoreType.DMA((6 * n,)),
                        pltpu.SemaphoreType.DMA((n,))],
        name="gather_weights")(*shards)


def _swap_halves(dws, *, name):
    n = len(dws)

    def body(*refs):
        ins, outs = refs[:n], refs[n:2 * n]
        send_sems, recv_sems = refs[2 * n:]
        x, y, c = _mesh_pos()
        cps = []
        for w in range(n):
            h = dws[w].shape[1] // 2
            cp = pltpu.make_async_remote_copy(
                src_ref=ins[w].at[:, pl.ds((1 - c) * h, h), :], dst_ref=outs[w],
                send_sem=send_sems.at[w], recv_sem=recv_sems.at[w],
                device_id=(x, y, 1 - c), device_id_type=MESH)
            cp.start()
            cps.append(cp)
        for cp in cps:
            cp.wait()

    return pl.pallas_call(
        body, out_shape=[jax.ShapeDtypeStruct((s.shape[0], s.shape[1] // 2, s.shape[2]), s.dtype) for s in dws],
        in_specs=_hbm_specs(n), out_specs=_hbm_specs(n),
        scratch_shapes=[pltpu.SemaphoreType.DMA((n,)), pltpu.SemaphoreType.DMA((n,))],
        name=name)(*dws)


def _chip_exchange(parts):
    n = len(parts)

    def body(*refs):
        ins, outs = refs[:n], refs[n:2 * n]
        send_sems, recv_sems, local_sems = refs[2 * n:]
        x, y, c = _mesh_pos()
        me_chip = 2 * x + y
        chips = _other_chips(x, y)
        local = [pltpu.make_async_copy(ins[w].at[me_chip], outs[w].at[me_chip], local_sems.at[w]) for w in range(n)]
        for cp in local:
            cp.start()
        cps = []
        for w in range(n):
            for j, chip in enumerate(chips):
                cp = pltpu.make_async_remote_copy(
                    src_ref=ins[w].at[2 * chip[0] + chip[1]], dst_ref=outs[w].at[me_chip],
                    send_sem=send_sems.at[3 * w + j], recv_sem=recv_sems.at[3 * w + j],
                    device_id=(*chip, c), device_id_type=MESH)
                cp.start()
                cps.append((cp, w, j, chip))
        for cp, w, j, chip in cps:
            slot = outs[w].at[2 * chip[0] + chip[1]]
            pltpu.make_async_remote_copy(
                src_ref=slot, dst_ref=slot, send_sem=send_sems.at[3 * w + j], recv_sem=recv_sems.at[3 * w + j],
                device_id=(x, y, c), device_id_type=MESH).wait_recv()
        for cp, _, _, _ in cps:
            cp.wait_send()
        for cp in local:
            cp.wait()

    return pl.pallas_call(
        body, out_shape=[jax.ShapeDtypeStruct(s.shape, s.dtype) for s in parts],
        in_specs=_hbm_specs(n), out_specs=_hbm_specs(n),
        scratch_shapes=[pltpu.SemaphoreType.DMA((3 * n,)), pltpu.SemaphoreType.DMA((3 * n,)),
                        pltpu.SemaphoreType.DMA((n,))],
        name="chip_exchange")(*parts)


def _join_halves(halves):
    n = len(halves)

    def body(*refs):
        ins, outs = refs[:n], refs[n:2 * n]
        send_sems, recv_sems, local_sems = refs[2 * n:]
        x, y, c = _mesh_pos()
        cps, local = [], []
        for w in range(n):
            h = halves[w].shape[0]
            mine = outs[w].at[pl.ds(c * h, h), :]
            lc = pltpu.make_async_copy(ins[w], mine, local_sems.at[w])
            lc.start()
            local.append(lc)
            cp = pltpu.make_async_remote_copy(
                src_ref=ins[w], dst_ref=mine, send_sem=send_sems.at[w], recv_sem=recv_sems.at[w],
                device_id=(x, y, 1 - c), device_id_type=MESH)
            cp.start()
            cps.append(cp)
        for w in range(n):
            h = halves[w].shape[0]
            theirs = outs[w].at[pl.ds((1 - c) * h, h), :]
            pltpu.make_async_remote_copy(
                src_ref=theirs, dst_ref=theirs, send_sem=send_sems.at[w], recv_sem=recv_sems.at[w],
                device_id=(x, y, c), device_id_type=MESH).wait_recv()
        for cp in cps:
            cp.wait_send()
        for lc in local:
            lc.wait()

    return pl.pallas_call(
        body, out_shape=[jax.ShapeDtypeStruct((2 * s.shape[0], s.shape[1]), s.dtype) for s in halves],
        in_specs=_hbm_specs(n), out_specs=_hbm_specs(n),
        scratch_shapes=[pltpu.SemaphoreType.DMA((n,)), pltpu.SemaphoreType.DMA((n,)), pltpu.SemaphoreType.DMA((n,))],
        name="join_halves")(*halves)


def _cast_into_slot(w, chip, *, name):
    rows, cols = w.shape
    tr = _row_tile(rows, cols, 2)

    def body(chip_ref, w_ref, o_ref):
        o_ref[...] = w_ref[...].astype(BF16)

    gs = pltpu.PrefetchScalarGridSpec(
        num_scalar_prefetch=1, grid=(rows // tr,),
        in_specs=[pl.BlockSpec((tr, cols), lambda i, chip_ref: (i, 0))],
        out_specs=pl.BlockSpec((None, tr, cols), lambda i, chip_ref: (chip_ref[0], i, 0)))
    return pl.pallas_call(
        body, grid_spec=gs, out_shape=jax.ShapeDtypeStruct((N_CHIPS, rows, cols), BF16),
        compiler_params=_cparams(("parallel",)), name=name)(chip, w)


def _gather_carry(bufs):
    n = len(bufs)
    sem = pltpu.SemaphoreType.DMA((6 * n,))

    def copies(outs, sems):
        send_sems, recv_sems = sems
        x, y, c = _mesh_pos()
        me_chip = 2 * x + y
        chips = _other_chips(x, y)

        def copy(w, k, chip_idx, pc, to):
            h = bufs[w].shape[1] // 2
            ref = outs[w].at[chip_idx, pl.ds(pc * h, h), :]
            return pltpu.make_async_remote_copy(
                src_ref=ref, dst_ref=ref, send_sem=send_sems.at[6 * w + k], recv_sem=recv_sems.at[6 * w + k],
                device_id=to, device_id_type=MESH)

        out_ici = [[copy(w, j, me_chip, c, (*chip, c)) for j, chip in enumerate(chips)] for w in range(n)]
        in_ici = [[copy(w, j, 2 * chip[0] + chip[1], c, (x, y, c)) for j, chip in enumerate(chips)] for w in range(n)]
        out_d2d = [[copy(w, 3 + j, 2 * chip[0] + chip[1], c, (x, y, 1 - c)) for j, chip in enumerate(chips)]
                   for w in range(n)]
        in_d2d = [[copy(w, 3 + j, 2 * chip[0] + chip[1], 1 - c, (x, y, c)) for j, chip in enumerate(chips)]
                  for w in range(n)]
        return out_ici, in_ici, out_d2d, in_d2d

    def start(ins, outs, sems):
        out_ici, _, _, _ = copies(outs, sems)
        for per_w in out_ici:
            for cp in per_w:
                cp.start()

    def finish(ins, outs, sems):
        out_ici, in_ici, out_d2d, in_d2d = copies(outs, sems)
        for w in range(n):
            for j in range(3):
                in_ici[w][j].wait_recv()
                out_d2d[w][j].start()
        for w in range(n):
            for j in range(3):
                in_d2d[w][j].wait_recv()
        for w in range(n):
            for j in range(3):
                out_ici[w][j].wait_send()
                out_d2d[w][j].wait_send()

    shapes = [jax.ShapeDtypeStruct(b.shape, b.dtype) for b in bufs]
    return _Carry(bufs, shapes, {i: i for i in range(n)}, [sem, sem], start, finish)


def _exchange_carry(parts):
    n = len(parts)
    sem = pltpu.SemaphoreType.DMA((3 * n,))

    def copies(ins, outs, sems):
        send_sems, recv_sems = sems
        x, y, c = _mesh_pos()
        chips = _other_chips(x, y)
        sends, recvs = [], []
        for w in range(n):
            for j, chip in enumerate(chips):
                sends.append(pltpu.make_async_remote_copy(
                    src_ref=ins[w].at[2 * chip[0] + chip[1]], dst_ref=outs[w].at[j],
                    send_sem=send_sems.at[3 * w + j], recv_sem=recv_sems.at[3 * w + j],
                    device_id=(*chip, c), device_id_type=MESH))
                recvs.append(pltpu.make_async_remote_copy(
                    src_ref=outs[w].at[j], dst_ref=outs[w].at[j],
                    send_sem=send_sems.at[3 * w + j], recv_sem=recv_sems.at[3 * w + j],
                    device_id=(x, y, c), device_id_type=MESH))
        return sends, recvs

    def start(ins, outs, sems):
        for cp in copies(ins, outs, sems)[0]:
            cp.start()

    def finish(ins, outs, sems):
        sends, recvs = copies(ins, outs, sems)
        for cp in recvs:
            cp.wait_recv()
        for cp in sends:
            cp.wait_send()

    shapes = [jax.ShapeDtypeStruct((3,) + p.shape[1:], p.dtype) for p in parts]
    return _Carry(parts, shapes, {}, [sem, sem], start, finish)


def _sum_into_half(part, landed, chip, my_c, *, name):
    _, h, cols = part.shape
    tr = _row_tile(h, cols, 5)
    hb = h // tr

    def body(chip_ref, c_ref, p_ref, l_ref, o_ref):
        acc = p_ref[...].astype(F32)
        for j in range(3):
            acc = acc + l_ref[j].astype(F32)
        o_ref[...] = acc

    gs = pltpu.PrefetchScalarGridSpec(
        num_scalar_prefetch=2, grid=(hb,),
        in_specs=[pl.BlockSpec((None, tr, cols), lambda i, chip_ref, c_ref: (chip_ref[0], i, 0)),
                  pl.BlockSpec((3, tr, cols), lambda i, chip_ref, c_ref: (0, i, 0))],
        out_specs=pl.BlockSpec((tr, cols), lambda i, chip_ref, c_ref: (c_ref[0] * hb + i, 0)))
    return pl.pallas_call(
        body, grid_spec=gs, out_shape=jax.ShapeDtypeStruct((2 * h, cols), F32),
        compiler_params=_cparams(("parallel",)), name=name)(chip, my_c, part, landed)


def _join_carry(fulls):
    n = len(fulls)
    sem = pltpu.SemaphoreType.DMA((n,))

    def copies(outs, sems):
        send_sems, recv_sems = sems
        x, y, c = _mesh_pos()
        sends, recvs = [], []
        for w in range(n):
            h = fulls[w].shape[0] // 2
            mine = outs[w].at[pl.ds(c * h, h), :]
            theirs = outs[w].at[pl.ds((1 - c) * h, h), :]
            sends.append(pltpu.make_async_remote_copy(
                src_ref=mine, dst_ref=mine, send_sem=send_sems.at[w], recv_sem=recv_sems.at[w],
                device_id=(x, y, 1 - c), device_id_type=MESH))
            recvs.append(pltpu.make_async_remote_copy(
                src_ref=theirs, dst_ref=theirs, send_sem=send_sems.at[w], recv_sem=recv_sems.at[w],
                device_id=(x, y, c), device_id_type=MESH))
        return sends, recvs

    def start(ins, outs, sems):
        for cp in copies(outs, sems)[0]:
            cp.start()

    def finish(ins, outs, sems):
        sends, recvs = copies(outs, sems)
        for cp in recvs:
            cp.wait_recv()
        for cp in sends:
            cp.wait_send()

    shapes = [jax.ShapeDtypeStruct(f.shape, f.dtype) for f in fulls]
    return _Carry(fulls, shapes, {i: i for i in range(n)}, [sem, sem], start, finish)


class _NoComm:
    def __init__(self, big):
        self.big = big
        self.grads = {}

    def weight(self, name):
        return self.big[name]

    def carry(self, site):
        return None

    def done(self, site, carried):
        pass

    def grad(self, name, dw):
        self.grads[name] = dw


class _MeshComm:
    GATHER_AT = {"mm_in": ["w_conv_out", "w_glu_a", "w_glu_b", "w_out"], "s5_fwd": ["w_ff1"], "mm_ff1": ["w_ff2"]}
    EXCHANGE_AT = {"mm_d_ff2": ["w_ff2"], "mm_d_ff1": ["w_ff1"],
                   "s5_bwd": ["w_out", "w_glu_a", "w_glu_b", "w_conv_out"], "mm_d_in": ["w_in"]}

    def __init__(self, shards, chip, my_c):
        self.chip = chip
        self.my_c = my_c
        self.bufs = {n: _cast_into_slot(s, chip, name="cast_" + n) for n, s in shards.items()}
        self.bufs["w_in"], = _run_carry(_gather_carry([self.bufs["w_in"]]), name="gather_w_in")
        self.raw = {}
        self.parts = {}
        self.halves = {}
        self.pending = {}

    def weight(self, name):
        g = self.bufs[name]
        return g.reshape(g.shape[0] * g.shape[1], g.shape[2]) if name in ROW_SHARDED else g

    def carry(self, site):
        if site in self.GATHER_AT:
            names = self.GATHER_AT[site]
            self.pending[site] = names
            return _gather_carry([self.bufs[n] for n in names])
        names = self.EXCHANGE_AT[site]
        self.pending[site] = names
        raws = [self.raw.pop(n) for n in names]
        landed = _swap_halves(raws, name="swap_halves_" + site)
        parts = [_add_half(dw, l1, self.my_c, name="add_half_" + n) for n, dw, l1 in zip(names, raws, landed)]
        self.parts.update(zip(names, parts))
        return _exchange_carry(parts)

    def done(self, site, carried):
        names = self.pending.pop(site)
        if site in self.GATHER_AT:
            self.bufs.update(zip(names, carried))
            return
        for n, landed in zip(names, carried):
            self.halves[n] = _sum_into_half(self.parts.pop(n), landed, self.chip, self.my_c, name="sum_chips_" + n)

    def grad(self, name, dw):
        if name in ROW_SHARDED:
            dw = dw.reshape(N_CHIPS, dw.shape[0] // N_CHIPS, dw.shape[1])
        self.raw[name] = dw

    def finish(self):
        names = list(self.halves)
        return dict(zip(names, _run_carry(_join_carry([self.halves[n] for n in names]), name="join_halves")))


def _local_step(x, target, mod, small, comm):
    rows, d = x.shape
    cw = d // 2
    shift1, scale1, gate1, shift2, scale2, gate2 = mod
    lr, li, bbr, bbi = small["s5_disc"]
    b_in, c_out, b_out, c_in, mults = _s5_operands(lr, li, bbr, bbi, small["c_re"], small["c_im"])
    wt = comm.weight

    def riding(site, fn, *args, **kwargs):
        carry = comm.carry(site)
        if carry is None:
            return fn(*args, **kwargs)
        out, carried = fn(*args, carry=carry, **kwargs)
        comm.done(site, carried)
        return out

    u = _norm_mod(x, small["norm1_g"], scale1, shift1, name="norm1_fwd")
    proj = riding("mm_in", _mm, u, wt("w_in"), mode="nn", out_dtype=BF16, name="mm_in")
    sl, cv = _conv_fwd(proj, small["w_dw"], small["b_dw"], small["ln_g"], small["ln_b"], cw=cw)
    y_conv = _mm(sl, wt("w_conv_out"), mode="nn", out_dtype=BF16, name="mm_conv_out")
    yg = riding("s5_fwd", _s5_fwd, proj, small["d_skip"], b_in, c_out, mults, col0=2 * cw // LANES)
    ya = _mm(yg, wt("w_glu_a"), mode="nn", out_dtype=BF16, name="mm_glu_a")
    yb = _mm(yg, wt("w_glu_b"), mode="nn", out_dtype=BF16, name="mm_glu_b")
    merged = _merge_fwd(proj, y_conv, ya, yb, cw=cw)
    mo = _mm(merged, wt("w_out"), mode="nn", out_dtype=BF16, name="mm_out")
    h1, z = _res_norm(x, mo, gate1, small["norm2_g"], scale2, shift2)
    f1 = riding("mm_ff1", _mm, z, wt("w_ff1"), mode="nn", out_dtype=BF16, name="mm_ff1")
    ff = _mm(f1, wt("w_ff2"), mode="nn", out_dtype=BF16, name="mm_ff2", a_fn=_relu2_bf16)
    dh2, dff, loss, d_final_g, d_gate2 = _final_fwd_bwd(h1, ff, gate2, small["final_g"], target)

    comm.grad("w_ff2", _mm(f1, dff, mode="tn", out_dtype=BF16, name="mm_dw_ff2", a_fn=_relu2_bf16))
    df1 = riding("mm_d_ff2", _mm, dff, wt("w_ff2"), mode="nt", out_dtype=BF16, name="mm_d_ff2", extra=f1,
                 epi=lambda acc, f: acc * (2.0 * jnp.maximum(f.astype(F32), 0.0)))
    comm.grad("w_ff1", _mm(z, df1, mode="tn", out_dtype=BF16, name="mm_dw_ff1", out_gathered=True))
    dz = riding("mm_d_ff1", _mm, df1, wt("w_ff1"), mode="nt", out_dtype=F32, name="mm_d_ff1")
    dh1, d_shift2, d_scale2, d_norm2_g, dmo, d_gate1 = _norm_mod_bwd(
        dz, h1, dh2, small["norm2_g"], scale2, gate1, mo, name="norm2_bwd")
    comm.grad("w_out", _mm(merged, dmo, mode="tn", out_dtype=BF16, name="mm_dw_out"))
    dmerged = _mm(dmo, wt("w_out"), mode="nt", out_dtype=BF16, name="mm_d_out")
    dgc, dgs, dy_conv, dya, dyb = _merge_bwd(dmerged, proj, y_conv, ya, yb, cw=cw)
    comm.grad("w_glu_a", _mm(yg, dya, mode="tn", out_dtype=BF16, name="mm_dw_glu_a", out_gathered=True))
    comm.grad("w_glu_b", _mm(yg, dyb, mode="tn", out_dtype=BF16, name="mm_dw_glu_b", out_gathered=True))
    dyg_a = _mm(dya, wt("w_glu_a"), mode="nt", out_dtype=F32, name="mm_d_glu_a")
    dyg = _mm(dyb, wt("w_glu_b"), mode="nt", out_dtype=F32, name="mm_d_glu_b", extra=dyg_a,
              epi=lambda acc, e: acc + e)
    comm.grad("w_conv_out", _mm(sl, dy_conv, mode="tn", out_dtype=BF16, name="mm_dw_conv_out", out_gathered=True))
    dsl = _mm(dy_conv, wt("w_conv_out"), mode="nt", out_dtype=F32, name="mm_d_conv_out")
    dcv, d_ln_g, d_ln_b = _ln_bwd(dsl, cv, small["ln_g"], small["ln_b"])
    dvconv, d_w_dw, d_b_dw = _conv_bwd(dcv, proj, small["w_dw"], cw=cw)
    dvssm, d_d_skip, dbr, dbi, dcr, dci, dlr, dli = riding(
        "s5_bwd", _s5_bwd, proj, dyg, small["d_skip"], b_in, c_out, b_out, c_in, mults, col0=2 * cw // LANES)
    dproj = jnp.concatenate([dvconv, dvssm, dgc, dgs], axis=1)
    comm.grad("w_in", _mm(u, dproj, mode="tn", out_dtype=BF16, name="mm_dw_in", out_gathered=True))
    du = riding("mm_d_in", _mm, dproj, wt("w_in"), mode="nt", out_dtype=F32, name="mm_d_in")
    grad_x, d_shift1, d_scale1, d_norm1_g = _norm_mod_bwd(
        du, x, dh1, small["norm1_g"], scale1, None, None, name="norm1_bwd")

    sw = lambda m: jnp.swapaxes(m, 1, 2)
    gs = {
        "dmod": jnp.concatenate([d_shift1, d_scale1, d_gate1, d_shift2, d_scale2, d_gate2], axis=1),
        "norm1_g": d_norm1_g, "w_dw": d_w_dw, "b_dw": d_b_dw, "ln_g": d_ln_g, "ln_b": d_ln_b,
        "lam_re": dlr.reshape(-1, SSM_STATE), "lam_im": dli.reshape(-1, SSM_STATE),
        "bb_re": sw(_block_diag_extract(dbr, SSM_GROUP, SSM_STATE)),
        "bb_im": sw(_block_diag_extract(dbi, SSM_GROUP, SSM_STATE)),
        "c_re": sw(_block_diag_extract(dcr, SSM_STATE, SSM_GROUP)),
        "c_im": sw(_block_diag_extract(dci, SSM_STATE, SSM_GROUP)),
        "d_skip": d_d_skip, "norm2_g": d_norm2_g, "final_g": d_final_g,
    }
    return grad_x, gs, loss


WEIGHT_NAMES = ["w_ada", "b_ada", "norm1_g", "w_in", "w_dw", "b_dw", "ln_g", "ln_b", "w_conv_out", "a_re", "a_im",
                "log_dt", "b_re", "b_im", "c_re", "c_im", "d_skip", "w_glu_a", "w_glu_b", "w_out", "norm2_g",
                "w_ff1", "w_ff2", "final_g"]
BIG_NAMES = ["w_in", "w_conv_out", "w_glu_a", "w_glu_b", "w_out", "w_ff1", "w_ff2"]
ROW_SHARDED = ("w_out", "w_ff2")
PACK_TILE = SUBLANES * 1024


def _pack(arrays):
    flats = [a.reshape(-1) for a in arrays]
    offs = []
    total = 0
    for f in flats:
        offs.append(total)
        total += f.shape[0]
    pad = (-total) % PACK_TILE
    if pad:
        flats.append(jnp.zeros((pad,), F32))
    return jnp.concatenate(flats), offs


def _unpack(flat, offs, like):
    return [flat[o:o + a.size].reshape(a.shape) for o, a in zip(offs, like)]


def _gather_w_dw(w_shard):
    k, n = w_shard.shape
    padded = jnp.pad(w_shard, ((0, HALO - k), (0, 0)))
    allw = _gather_small(padded, name="gather_w_dw").reshape(N_CHIPS, 2, HALO, n)[:, 0, :k]
    return jnp.moveaxis(allw, 0, 1).reshape(k, N_CHIPS * n)


def kernel(x, c, w_ada, b_ada, norm1_g, w_in, w_dw, b_dw, ln_g, ln_b, w_conv_out, a_re, a_im, log_dt, b_re, b_im, c_re, c_im, d_skip, w_glu_a, w_glu_b, w_out, norm2_g, w_ff1, w_ff2, final_g, loss_target, m_w_ada, m_b_ada, m_norm1_g, m_w_in, m_w_dw, m_b_dw, m_ln_g, m_ln_b, m_w_conv_out, m_a_re, m_a_im, m_log_dt, m_b_re, m_b_im, m_c_re, m_c_im, m_d_skip, m_w_glu_a, m_w_glu_b, m_w_out, m_norm2_g, m_w_ff1, m_w_ff2, m_final_g, v_w_ada, v_b_ada, v_norm1_g, v_w_in, v_w_dw, v_b_dw, v_ln_g, v_ln_b, v_w_conv_out, v_a_re, v_a_im, v_log_dt, v_b_re, v_b_im, v_c_re, v_c_im, v_d_skip, v_w_glu_a, v_w_glu_b, v_w_out, v_norm2_g, v_w_ff1, v_w_ff2, v_final_g):
    given = dict(locals())
    w = {n: given[n] for n in WEIGHT_NAMES}
    m = {n: given["m_" + n] for n in WEIGHT_NAMES}
    v = {n: given["v_" + n] for n in WEIGHT_NAMES}
    d = x.shape[2]
    xi, yi, ci = _mesh_pos()
    chip = 2 * xi + yi
    dev = 4 * xi + 2 * yi + ci
    my_c = jnp.reshape(ci, (1,)).astype(jnp.int32)

    c_all = _gather_small(c.reshape(SUBLANES, d // SUBLANES), name="gather_c").reshape(N_DEV, d)
    nmod = w_ada.shape[2]
    b_cols = lax.dynamic_slice(b_ada, (0, chip * nmod), (1, nmod))
    mod_part = _ada_fwd(c_all, w_ada[0], b_cols)
    mod_all = _gather_small(mod_part, name="gather_mod").reshape(N_CHIPS, 2, N_DEV, nmod)[:, 0]
    mod_full = jnp.moveaxis(mod_all, 0, 1).reshape(N_DEV, N_CHIPS * nmod)
    mod_row = lax.dynamic_slice(mod_full, (dev, 0), (1, N_CHIPS * nmod))
    mod = [mod_row[:, i * d:(i + 1) * d] for i in range(6)]

    chip_arr = jnp.reshape(chip, (1,)).astype(jnp.int32)
    comm = _MeshComm({n: w[n][0] for n in BIG_NAMES}, chip_arr, my_c)

    disc_in = (a_re[0], a_im[0], log_dt[0], b_re[0], b_im[0])
    disc, disc_vjp = jax.vjp(_s5_discretise, *disc_in)
    small = {"norm1_g": norm1_g, "w_dw": _gather_w_dw(w_dw[0]), "b_dw": b_dw, "ln_g": ln_g, "ln_b": ln_b,
             "c_re": c_re[0], "c_im": c_im[0], "d_skip": d_skip, "norm2_g": norm2_g,
             "final_g": final_g[None, :], "s5_disc": disc}

    grad_x, gs, loss = _local_step(x[0], loss_target[0], mod, small, comm)
    grads = comm.finish()

    small_keys = ["norm1_g", "w_dw", "b_dw", "ln_g", "ln_b", "lam_re", "lam_im", "bb_re", "bb_im", "c_re", "c_im",
                  "d_skip", "norm2_g", "final_g"]
    items = [gs["dmod"], loss[:, 0:1]] + [gs[k] for k in small_keys]
    flat, offs = _pack(items)
    npk = flat.shape[0]
    everyone = _gather_small(flat.reshape(SUBLANES, npk // SUBLANES), name="gather_small_grads")
    total = _sum_leading(everyone.reshape(N_DEV, npk // 1024, 1024), name="sum_small_grads").reshape(npk)
    summed = dict(zip(["dmod", "loss"] + small_keys, _unpack(total, offs, items)))
    dmod_all = everyone.reshape(N_DEV, npk)[:, :6 * d]

    grads["w_ada"] = _ada_bwd(c_all, lax.dynamic_slice(dmod_all, (0, chip * nmod), (N_DEV, nmod)))
    grads["b_ada"] = _sum_leading(dmod_all.reshape(N_DEV, SUBLANES, 6 * d // SUBLANES),
                                  name="sum_b_ada").reshape(1, 6 * d)
    da_re, da_im, dlog_dt, db_re, db_im = disc_vjp(
        (summed["lam_re"], summed["lam_im"], summed["bb_re"], summed["bb_im"]))
    ndw = w_dw.shape[2]
    grads.update({
        "norm1_g": summed["norm1_g"], "w_dw": lax.dynamic_slice(summed["w_dw"], (0, chip * ndw), (CONV_KERNEL, ndw)),
        "b_dw": summed["b_dw"], "ln_g": summed["ln_g"], "ln_b": summed["ln_b"],
        "a_re": da_re, "a_im": da_im, "log_dt": dlog_dt, "b_re": db_re, "b_im": db_im,
        "c_re": summed["c_re"], "c_im": summed["c_im"], "d_skip": summed["d_skip"],
        "norm2_g": summed["norm2_g"], "final_g": summed["final_g"],
    })
    grads = {n: grads[n].reshape(w[n].shape) for n in WEIGHT_NAMES}

    delta, new_m, new_v = {}, {}, {}
    for n in ["w_ada"] + BIG_NAMES:
        shp = w[n].shape
        two_d = lambda a: a.reshape(shp[1], shp[2])
        dl, nm, nv = _adamw(two_d(w[n]), two_d(grads[n]), two_d(m[n]), two_d(v[n]), name="adamw_" + n)
        delta[n], new_m[n], new_v[n] = dl.reshape(shp), nm.reshape(shp), nv.reshape(shp)
    rest = [n for n in WEIGHT_NAMES if n not in delta]
    packs = []
    for src in (w, grads, m, v):
        flat, offs = _pack([src[n] for n in rest])
        packs.append(flat.reshape(-1, 1024))
    outs = _adamw(*packs, name="adamw_small")
    for dst, o in zip((delta, new_m, new_v), outs):
        for n, a in zip(rest, _unpack(o.reshape(-1), offs, [w[k] for k in rest])):
            dst[n] = a

    return (summed["loss"].reshape(()), grad_x[None], *[grads[n] for n in WEIGHT_NAMES],
            *[delta[n] for n in WEIGHT_NAMES], *[new_m[n] for n in WEIGHT_NAMES],
            *[new_v[n] for n in WEIGHT_NAMES])
```

```python
import functools
import math

import jax
import jax.numpy as jnp
from jax import lax
from jax.experimental import pallas as pl
from jax.experimental.pallas import tpu as pltpu

F32 = jnp.float32
BF16 = jnp.bfloat16
EPS = 1e-6
CONV_KERNEL = 31
SSM_GROUP = 16
SSM_STATE = 64
ADAM_LR = 0.001
ADAM_B1 = 0.9
ADAM_B2 = 0.999
ADAM_EPS = 1e-08
ADAM_WD = 0.01
ADAM_STEP = 10

N_CHIPS = 4
N_DEV = 8
VMEM_LIMIT_BYTES = 56 * 1024 * 1024
LANES = 128
SUBLANES = 8
HALO = 32
GROUPS_PER_BLOCK = LANES // SSM_GROUP
STATE_LANES = GROUPS_PER_BLOCK * SSM_STATE
MESH = pl.DeviceIdType.MESH


def _cparams(sem):
    return pltpu.CompilerParams(dimension_semantics=sem, vmem_limit_bytes=VMEM_LIMIT_BYTES)


def _pick(n, pref, mult=LANES):
    if n <= pref:
        return n
    best = None
    for d in range(mult, pref + 1, mult):
        if n % d == 0:
            best = d
    assert best is not None, (n, pref)
    return best


def _sigmoid(v):
    return 1.0 / (1.0 + jnp.exp(-v))


def _gelu_parts(v):
    k0 = math.sqrt(2.0 / math.pi)
    inner = k0 * (v + 0.044715 * v * v * v)
    t = jnp.tanh(inner)
    return k0, t


def _gelu(v):
    _, t = _gelu_parts(v)
    return 0.5 * v * (1.0 + t)


def _gelu_grad(v):
    k0, t = _gelu_parts(v)
    return 0.5 * (1.0 + t) + 0.5 * v * (1.0 - t * t) * k0 * (1.0 + 3.0 * 0.044715 * v * v)


def _relu2_bf16(a):
    t = jnp.maximum(a.astype(F32), 0.0)
    return (t * t).astype(BF16)


class _Carry:
    def __init__(self, inputs, out_shapes, aliases, sem_shapes, start, finish):
        self.inputs = list(inputs)
        self.out_shapes = list(out_shapes)
        self.aliases = dict(aliases)
        self.sem_shapes = list(sem_shapes)
        self.start = start
        self.finish = finish


def _call(body, *, grid, in_specs, out_specs, out_shape, scratch_shapes, semantics, name, args, carry=None,
          prefetch=(), aliases=None):
    n_in, n_out, n_scr, n_pf = len(in_specs), len(out_specs), len(scratch_shapes), len(prefetch)
    own_aliases = {n_pf + i: o for i, o in (aliases or {}).items()}
    if carry is None:
        gs = pltpu.PrefetchScalarGridSpec(
            num_scalar_prefetch=n_pf, grid=grid, in_specs=in_specs, out_specs=out_specs,
            scratch_shapes=scratch_shapes)
        outs = pl.pallas_call(
            body, grid_spec=gs, out_shape=out_shape, input_output_aliases=own_aliases,
            compiler_params=_cparams(semantics), name=name)(*prefetch, *args)
        return list(outs), []
    ci, co = len(carry.inputs), len(carry.out_shapes)

    def wrapped(*refs):
        pf, refs = refs[:n_pf], refs[n_pf:]
        ins, cins = refs[:n_in], refs[n_in:n_in + ci]
        p = n_in + ci
        outs, couts = refs[p:p + n_out], refs[p + n_out:p + n_out + co]
        p += n_out + co
        scr, csems = refs[p:p + n_scr], refs[p + n_scr:]
        first = pl.program_id(0) == 0
        last = pl.program_id(0) == grid[0] - 1
        for ax in range(1, len(grid)):
            first = jnp.logical_and(first, pl.program_id(ax) == 0)
            last = jnp.logical_and(last, pl.program_id(ax) == grid[ax] - 1)

        @pl.when(first)
        def _():
            carry.start(cins, couts, csems)

        body(*pf, *ins, *outs, *scr)

        @pl.when(last)
        def _():
            carry.finish(cins, couts, csems)

    any_spec = pl.BlockSpec(memory_space=pl.ANY)
    gs = pltpu.PrefetchScalarGridSpec(
        num_scalar_prefetch=n_pf, grid=grid, in_specs=list(in_specs) + [any_spec] * ci,
        out_specs=list(out_specs) + [any_spec] * co, scratch_shapes=list(scratch_shapes) + carry.sem_shapes)
    all_aliases = dict(own_aliases)
    all_aliases.update({n_pf + n_in + i: n_out + o for i, o in carry.aliases.items()})
    outs = pl.pallas_call(
        wrapped, grid_spec=gs, out_shape=list(out_shape) + carry.out_shapes, input_output_aliases=all_aliases,
        compiler_params=_cparams(("arbitrary",) * len(grid)), name=name)(*prefetch, *args, *carry.inputs)
    return list(outs[:n_out]), list(outs[n_out:])


def _run_carry(carry, *, name):
    ci = len(carry.inputs)

    def body(*refs):
        cins, couts, csems = refs[:ci], refs[ci:ci + len(carry.out_shapes)], refs[ci + len(carry.out_shapes):]
        carry.start(cins, couts, csems)
        carry.finish(cins, couts, csems)

    any_spec = pl.BlockSpec(memory_space=pl.ANY)
    outs = pl.pallas_call(
        body, in_specs=[any_spec] * ci, out_specs=[any_spec] * len(carry.out_shapes), out_shape=carry.out_shapes,
        scratch_shapes=carry.sem_shapes, input_output_aliases=carry.aliases, name=name)(*carry.inputs)
    return list(outs)


def _mm(a, b, *, mode, out_dtype, name, out_gathered=False, a_fn=None, epi=None, extra=None,
        bm_pref=1024, bn_pref=1024, bk_pref=2048, carry=None, a_slots=None):
    gathered = (b.ndim == 3)
    if mode == "nn":
        m, kdim = a.shape
        ns = b.shape[-1]
        n = ns * (N_CHIPS if gathered else 1)
        bm, bn, bk = _pick(m, bm_pref), _pick(ns, bn_pref), _pick(kdim, bk_pref)
        npb = ns // bn
        grid = (m // bm, n // bn, kdim // bk)
        a_spec = pl.BlockSpec((bm, bk), lambda i, j, k: (i, k))
        if gathered:
            b_spec = pl.BlockSpec((None, bk, bn), lambda i, j, k: (j // npb, k, j % npb))
        else:
            b_spec = pl.BlockSpec((bk, bn), lambda i, j, k: (k, j))
        o_spec = pl.BlockSpec((bm, bn), lambda i, j, k: (i, j))
        e_spec = pl.BlockSpec((bm, bn), lambda i, j, k: (i, j))
        out_shape = (m, n)
        acc_shape = (bm, bn)
        dims = (((1,), (0,)), ((), ()))
    elif mode == "nt":
        m = a.shape[0]
        kdim, ns = b.shape[-2], b.shape[-1]
        n = ns * (N_CHIPS if gathered else 1)
        assert a.shape[1] == n
        bm, bko, bnr = _pick(m, bm_pref), _pick(kdim, bn_pref), _pick(ns, bk_pref)
        npb = ns // bnr
        grid = (m // bm, kdim // bko, n // bnr)
        a_spec = pl.BlockSpec((bm, bnr), lambda i, j, k: (i, k))
        if gathered:
            b_spec = pl.BlockSpec((None, bko, bnr), lambda i, j, k: (k // npb, j, k % npb))
        else:
            b_spec = pl.BlockSpec((bko, bnr), lambda i, j, k: (j, k))
        o_spec = pl.BlockSpec((bm, bko), lambda i, j, k: (i, j))
        e_spec = pl.BlockSpec((bm, bko), lambda i, j, k: (i, j))
        if a_slots is not None:
            assert gathered and extra is None
            a_spec = pl.BlockSpec((bm, bnr), lambda i, j, k, s_ref: (i, s_ref[k // npb] * npb + k % npb))
            b_spec = pl.BlockSpec((None, bko, bnr), lambda i, j, k, s_ref: (k // npb, j, k % npb))
            o_spec = pl.BlockSpec((bm, bko), lambda i, j, k, s_ref: (i, j))
        out_shape = (m, kdim)
        acc_shape = (bm, bko)
        dims = (((1,), (1,)), ((), ()))
    else:
        m, kdim = a.shape
        n = b.shape[1]
        ns = n // N_CHIPS if out_gathered else n
        bmr, bko, bn = _pick(m, bk_pref), _pick(kdim, bm_pref), _pick(ns, bn_pref)
        npb = ns // bn
        grid = (kdim // bko, n // bn, m // bmr)
        a_spec = pl.BlockSpec((bmr, bko), lambda i, j, k: (k, i))
        b_spec = pl.BlockSpec((bmr, bn), lambda i, j, k: (k, j))
        if out_gathered:
            o_spec = pl.BlockSpec((None, bko, bn), lambda i, j, k: (j // npb, i, j % npb))
            out_shape = (N_CHIPS, kdim, ns)
        else:
            o_spec = pl.BlockSpec((bko, bn), lambda i, j, k: (i, j))
            out_shape = (kdim, n)
        e_spec = None
        acc_shape = (bko, bn)
        dims = (((0,), (0,)), ((), ()))
    nk = grid[2]

    def body(*refs):
        if a_slots is not None:
            refs = refs[1:]
        if extra is not None:
            a_ref, b_ref, e_ref, o_ref, acc = refs
        else:
            a_ref, b_ref, o_ref, acc = refs
            e_ref = None
        k = pl.program_id(2)
        av = a_ref[...]
        if a_fn is not None:
            av = a_fn(av)
        part = lax.dot_general(av, b_ref[...], dims, preferred_element_type=F32)

        def finish(r):
            if epi is not None:
                r = epi(r, e_ref[...])
            o_ref[...] = r.astype(o_ref.dtype)

        if nk == 1:
            finish(part)
            return

        @pl.when(k == 0)
        def _():
            acc[...] = part

        @pl.when(jnp.logical_and(k > 0, k < nk - 1))
        def _():
            acc[...] += part

        @pl.when(k == nk - 1)
        def _():
            finish(acc[...] + part)

    in_specs = [a_spec, b_spec]
    args = [a, b]
    if extra is not None:
        in_specs.append(e_spec)
        args.append(extra)
    outs, carried = _call(body, grid=grid, in_specs=in_specs, out_specs=[o_spec],
                          out_shape=[jax.ShapeDtypeStruct(out_shape, out_dtype)],
                          scratch_shapes=[pltpu.VMEM(acc_shape, F32)],
                          semantics=("parallel", "parallel", "arbitrary"), name=name, args=args, carry=carry,
                          prefetch=() if a_slots is None else (a_slots,))
    return outs[0] if carry is None else (outs[0], carried)


def _mm_slots(a, wbuf, slots, prev, *, name, carry=None):
    m, kdim = a.shape
    ns = wbuf.shape[2]
    bm, bn = _pick(m, 1024), _pick(ns, 1024)
    npb = ns // bn
    grid = (m // bm, slots.shape[0], npb)

    def body(s_ref, a_ref, b_ref, *rest):
        o_ref = rest[-1]
        o_ref[...] = _dot(a_ref[...], b_ref[...]).astype(o_ref.dtype)

    in_specs = [pl.BlockSpec((bm, kdim), lambda i, s, j, s_ref: (i, 0)),
                pl.BlockSpec((None, kdim, bn), lambda i, s, j, s_ref: (s, 0, j))]
    args = [a, wbuf]
    aliases = None
    if prev is not None:
        in_specs.append(pl.BlockSpec(memory_space=pl.ANY))
        args.append(prev)
        aliases = {2: 0}
    outs, carried = _call(
        body, grid=grid, in_specs=in_specs,
        out_specs=[pl.BlockSpec((bm, bn), lambda i, s, j, s_ref: (i, s_ref[s] * npb + j))],
        out_shape=[jax.ShapeDtypeStruct((m, N_CHIPS * ns), BF16)], scratch_shapes=[],
        semantics=("parallel", "arbitrary", "arbitrary"), name=name, args=args, carry=carry,
        prefetch=(slots,), aliases=aliases)
    return outs[0] if carry is None else (outs[0], carried)


def _row_tile(rows, cols, n_arrays):
    budget = VMEM_LIMIT_BYTES // 3
    cap = min(512, budget // (n_arrays * 2 * cols * 4))
    for t in range(cap - cap % SUBLANES, 0, -SUBLANES):
        if rows % t == 0:
            return t
    return rows


def _norm_mod(x, g, scale, shift, *, name):
    rows, d = x.shape
    tr = _row_tile(rows, d, 3)

    def body(x_ref, g_ref, sc_ref, sh_ref, o_ref):
        xv = x_ref[...]
        r = lax.rsqrt(jnp.mean(xv * xv, axis=-1, keepdims=True) + EPS)
        o_ref[...] = ((xv * r * g_ref[...]) * (1.0 + sc_ref[...]) + sh_ref[...]).astype(o_ref.dtype)

    row = pl.BlockSpec((tr, d), lambda i: (i, 0))
    vec = pl.BlockSpec((1, d), lambda i: (0, 0))
    return pl.pallas_call(
        body, grid=(rows // tr,), in_specs=[row, vec, vec, vec], out_specs=row,
        out_shape=jax.ShapeDtypeStruct((rows, d), BF16),
        compiler_params=_cparams(("parallel",)), name=name)(x, g, scale, shift)


CONV_CHUNK = 2 * SUBLANES


def _shifted_copies(buf, n):
    for r in range(1, SUBLANES):
        buf[r, pl.ds(0, n - SUBLANES), :] = buf[0, pl.ds(r, n - SUBLANES), :]


def _conv_fwd(proj, w_dw, b_dw, ln_g, ln_b, *, cw, carry=None):
    rows = proj.shape[0]
    tt = _pick(rows, 256, HALO)
    hb = tt // HALO

    def body(a_ref, g_ref, ha_ref, hg_ref, w_ref, b_ref, lg_ref, lb_ref, sl_ref, cv_ref, vs):
        i = pl.program_id(0)
        hv = ha_ref[...].astype(F32) * _sigmoid(hg_ref[...].astype(F32))
        vs[0, pl.ds(0, HALO), :] = jnp.where(i == 0, 0.0, hv)
        vs[0, pl.ds(HALO, tt), :] = a_ref[...].astype(F32) * _sigmoid(g_ref[...].astype(F32))
        _shifted_copies(vs, HALO + tt)

        def chunk(ci, carry):
            r0 = pl.multiple_of(ci * CONV_CHUNK, CONV_CHUNK)
            acc = jnp.broadcast_to(b_ref[...], (CONV_CHUNK, cw))
            for k in range(CONV_KERNEL):
                q, r = divmod(HALO - (CONV_KERNEL - 1) + k, SUBLANES)
                acc = acc + w_ref[pl.ds(k, 1), :] * vs[r, pl.ds(r0 + q * SUBLANES, CONV_CHUNK), :]
            cv_ref[pl.ds(r0, CONV_CHUNK), :] = acc
            return carry

        lax.fori_loop(0, tt // CONV_CHUNK, chunk, 0)
        acc = cv_ref[...]
        mu = jnp.mean(acc, axis=-1, keepdims=True)
        xc = acc - mu
        rstd = lax.rsqrt(jnp.mean(xc * xc, axis=-1, keepdims=True) + EPS)
        ln = xc * rstd * lg_ref[...] + lb_ref[...]
        sl_ref[...] = (ln * _sigmoid(ln)).astype(sl_ref.dtype)

    tile = lambda c: pl.BlockSpec((tt, cw), lambda i, c=c: (i, c))
    halo = lambda c: pl.BlockSpec((HALO, cw), lambda i, c=c: (jnp.maximum(i * hb - 1, 0), c))
    vec = pl.BlockSpec((1, cw), lambda i: (0, 0))
    outs, carried = _call(
        body, grid=(rows // tt,),
        in_specs=[tile(0), tile(1), halo(0), halo(1),
                  pl.BlockSpec((CONV_KERNEL, cw), lambda i: (0, 0)), vec, vec, vec],
        out_specs=[pl.BlockSpec((tt, cw), lambda i: (i, 0)), pl.BlockSpec((tt, cw), lambda i: (i, 0))],
        out_shape=[jax.ShapeDtypeStruct((rows, cw), BF16), jax.ShapeDtypeStruct((rows, cw), F32)],
        scratch_shapes=[pltpu.VMEM((SUBLANES, HALO + tt, cw), F32)],
        semantics=("parallel",), name="conv_fwd", args=[proj, proj, proj, proj, w_dw, b_dw, ln_g, ln_b],
        carry=carry)
    return outs if carry is None else (outs, carried)


def _ln_bwd(dsl, cv, ln_g, ln_b):
    rows, cw = cv.shape
    tr = _row_tile(rows, cw, 3)

    def body(d_ref, cv_ref, lg_ref, lb_ref, o_ref, dg_ref, db_ref):
        i = pl.program_id(0)

        @pl.when(i == 0)
        def _():
            dg_ref[...] = jnp.zeros_like(dg_ref)
            db_ref[...] = jnp.zeros_like(db_ref)

        x = cv_ref[...]
        mu = jnp.mean(x, axis=-1, keepdims=True)
        xc = x - mu
        rstd = lax.rsqrt(jnp.mean(xc * xc, axis=-1, keepdims=True) + EPS)
        xh = xc * rstd
        ln = xh * lg_ref[...] + lb_ref[...]
        s = _sigmoid(ln)
        dln = d_ref[...].astype(F32) * (s * (1.0 + ln * (1.0 - s)))
        dg_ref[...] += jnp.sum(dln * xh, axis=0, keepdims=True)
        db_ref[...] += jnp.sum(dln, axis=0, keepdims=True)
        dxh = dln * lg_ref[...]
        m1 = jnp.mean(dxh, axis=-1, keepdims=True)
        m2 = jnp.mean(dxh * xh, axis=-1, keepdims=True)
        o_ref[...] = rstd * (dxh - m1 - xh * m2)

    row = pl.BlockSpec((tr, cw), lambda i: (i, 0))
    vec = pl.BlockSpec((1, cw), lambda i: (0, 0))
    return pl.pallas_call(
        body, grid=(rows // tr,), in_specs=[row, row, vec, vec], out_specs=[row, vec, vec],
        out_shape=[jax.ShapeDtypeStruct((rows, cw), F32), jax.ShapeDtypeStruct((1, cw), F32),
                   jax.ShapeDtypeStruct((1, cw), F32)],
        compiler_params=_cparams(("arbitrary",)), name="ln_bwd")(dsl, cv, ln_g, ln_b)


def _conv_bwd(dcv, proj, w_dw, *, cw):
    rows = proj.shape[0]
    tt = _pick(rows, 256, HALO)
    hb = tt // HALO
    nt = rows // tt
    taps = CONV_KERNEL

    def body(d_ref, dn_ref, a_ref, g_ref, ha_ref, hg_ref, w_ref, o_ref, dw_ref, db_ref, vs, ds):
        i = pl.program_id(0)

        @pl.when(i == 0)
        def _():
            dw_ref[...] = jnp.zeros_like(dw_ref)
            db_ref[...] = jnp.zeros_like(db_ref)

        hv = ha_ref[...].astype(F32) * _sigmoid(hg_ref[...].astype(F32))
        vs[0, pl.ds(0, HALO), :] = jnp.where(i == 0, 0.0, hv)
        vs[0, pl.ds(HALO, tt), :] = a_ref[...].astype(F32) * _sigmoid(g_ref[...].astype(F32))
        _shifted_copies(vs, HALO + tt)
        ds[0, pl.ds(0, tt), :] = d_ref[...]
        ds[0, pl.ds(tt, HALO), :] = jnp.where(i == nt - 1, 0.0, dn_ref[...])
        _shifted_copies(ds, tt + HALO)
        db_ref[...] += jnp.sum(d_ref[...], axis=0, keepdims=True)
        for k in range(taps):
            q, r = divmod(HALO - (taps - 1) + k, SUBLANES)
            dw_ref[pl.ds(k, 1), :] += jnp.sum(d_ref[...] * vs[r, pl.ds(q * SUBLANES, tt), :], axis=0, keepdims=True)

        def chunk(ci, carry):
            r0 = pl.multiple_of(ci * CONV_CHUNK, CONV_CHUNK)
            dv = jnp.zeros((CONV_CHUNK, cw), F32)
            for k in range(taps):
                q, r = divmod(taps - 1 - k, SUBLANES)
                dv = dv + w_ref[pl.ds(k, 1), :] * ds[r, pl.ds(r0 + q * SUBLANES, CONV_CHUNK), :]
            av = a_ref[pl.ds(r0, CONV_CHUNK), :].astype(F32)
            sg = _sigmoid(g_ref[pl.ds(r0, CONV_CHUNK), :].astype(F32))
            o_ref[pl.ds(r0, CONV_CHUNK), pl.ds(0, cw)] = (dv * sg).astype(o_ref.dtype)
            o_ref[pl.ds(r0, CONV_CHUNK), pl.ds(cw, cw)] = (dv * av * sg * (1.0 - sg)).astype(o_ref.dtype)
            return carry

        lax.fori_loop(0, tt // CONV_CHUNK, chunk, 0)

    tile = lambda c: pl.BlockSpec((tt, cw), lambda i, c=c: (i, c))
    halo = lambda c: pl.BlockSpec((HALO, cw), lambda i, c=c: (jnp.maximum(i * hb - 1, 0), c))
    nxt = pl.BlockSpec((HALO, cw), lambda i: (jnp.minimum((i + 1) * hb, nt * hb - 1), 0))
    return pl.pallas_call(
        body, grid=(nt,),
        in_specs=[pl.BlockSpec((tt, cw), lambda i: (i, 0)), nxt, tile(0), tile(1), halo(0), halo(1),
                  pl.BlockSpec((taps, cw), lambda i: (0, 0))],
        out_specs=[pl.BlockSpec((tt, 2 * cw), lambda i: (i, 0)),
                   pl.BlockSpec((taps, cw), lambda i: (0, 0)), pl.BlockSpec((1, cw), lambda i: (0, 0))],
        out_shape=[jax.ShapeDtypeStruct((rows, 2 * cw), BF16), jax.ShapeDtypeStruct((taps, cw), F32),
                   jax.ShapeDtypeStruct((1, cw), F32)],
        scratch_shapes=[pltpu.VMEM((SUBLANES, HALO + tt, cw), F32), pltpu.VMEM((SUBLANES, tt + HALO, cw), F32)],
        compiler_params=_cparams(("arbitrary",)), name="conv_bwd")(dcv, dcv, proj, proj, proj, proj, w_dw)


def _merge_fwd(proj, y_conv, ya, yb, *, cw):
    rows = proj.shape[0]
    tr = _row_tile(rows, cw, 4)

    def body(gc_ref, gs_ref, yc_ref, ya_ref, yb_ref, o_ref):
        ys = ya_ref[...].astype(F32) * _sigmoid(yb_ref[...].astype(F32))
        o_ref[...] = (_sigmoid(gc_ref[...].astype(F32)) * yc_ref[...].astype(F32)
                      + _sigmoid(gs_ref[...].astype(F32)) * ys).astype(o_ref.dtype)

    blk = lambda off: pl.BlockSpec((tr, cw), lambda i, h, off=off: (i, off + h))
    return pl.pallas_call(
        body, grid=(rows // tr, 2), in_specs=[blk(3), blk(5), blk(0), blk(0), blk(0)], out_specs=blk(0),
        out_shape=jax.ShapeDtypeStruct((rows, 2 * cw), BF16),
        compiler_params=_cparams(("parallel", "parallel")), name="merge_fwd")(proj, proj, y_conv, ya, yb)


def _merge_bwd(dmerged, proj, y_conv, ya, yb, *, cw):
    rows = proj.shape[0]
    tr = _row_tile(rows, cw, 6)

    def body(d_ref, gc_ref, gs_ref, yc_ref, ya_ref, yb_ref, dgc_ref, dgs_ref, dyc_ref, dya_ref, dyb_ref):
        d = d_ref[...].astype(F32)
        sc = _sigmoid(gc_ref[...].astype(F32))
        ss = _sigmoid(gs_ref[...].astype(F32))
        sb = _sigmoid(yb_ref[...].astype(F32))
        yav = ya_ref[...].astype(F32)
        dgc_ref[...] = (d * yc_ref[...].astype(F32) * sc * (1.0 - sc)).astype(dgc_ref.dtype)
        dgs_ref[...] = (d * (yav * sb) * ss * (1.0 - ss)).astype(dgs_ref.dtype)
        dyc_ref[...] = (d * sc).astype(dyc_ref.dtype)
        dys = d * ss
        dya_ref[...] = (dys * sb).astype(dya_ref.dtype)
        dyb_ref[...] = (dys * yav * sb * (1.0 - sb)).astype(dyb_ref.dtype)

    blk = lambda off: pl.BlockSpec((tr, cw), lambda i, h, off=off: (i, off + h))
    o2 = jax.ShapeDtypeStruct((rows, 2 * cw), BF16)
    return pl.pallas_call(
        body, grid=(rows // tr, 2),
        in_specs=[blk(0), blk(3), blk(5), blk(0), blk(0), blk(0)],
        out_specs=[blk(0), blk(0), blk(0), blk(0), blk(0)],
        out_shape=[o2, o2, o2, o2, o2],
        compiler_params=_cparams(("parallel", "parallel")), name="merge_bwd")(dmerged, proj, proj, y_conv, ya, yb)


def _res_norm(x, mo, gate, g, scale, shift):
    rows, d = x.shape
    tr = _row_tile(rows, d, 4)

    def body(x_ref, mo_ref, gt_ref, g_ref, sc_ref, sh_ref, h_ref, z_ref):
        h = x_ref[...] + gt_ref[...] * mo_ref[...].astype(F32)
        h_ref[...] = h
        r = lax.rsqrt(jnp.mean(h * h, axis=-1, keepdims=True) + EPS)
        z_ref[...] = ((h * r * g_ref[...]) * (1.0 + sc_ref[...]) + sh_ref[...]).astype(z_ref.dtype)

    row = pl.BlockSpec((tr, d), lambda i: (i, 0))
    vec = pl.BlockSpec((1, d), lambda i: (0, 0))
    return pl.pallas_call(
        body, grid=(rows // tr,), in_specs=[row, row, vec, vec, vec, vec], out_specs=[row, row],
        out_shape=[jax.ShapeDtypeStruct((rows, d), F32), jax.ShapeDtypeStruct((rows, d), BF16)],
        compiler_params=_cparams(("parallel",)), name="res_norm")(x, mo, gate, g, scale, shift)


def _final_fwd_bwd(h1, ff, gate2, final_g, target):
    rows, d = h1.shape
    tr = _row_tile(rows, d, 5)

    def body(h_ref, ff_ref, gt_ref, fg_ref, t_ref, dh_ref, dff_ref, loss_ref, dfg_ref, dgt_ref):
        i = pl.program_id(0)

        @pl.when(i == 0)
        def _():
            loss_ref[...] = jnp.zeros_like(loss_ref)
            dfg_ref[...] = jnp.zeros_like(dfg_ref)
            dgt_ref[...] = jnp.zeros_like(dgt_ref)

        ffv = ff_ref[...].astype(F32)
        h2 = h_ref[...] + gt_ref[...] * ffv
        r = lax.rsqrt(jnp.mean(h2 * h2, axis=-1, keepdims=True) + EPS)
        y = h2 * r
        e = y * fg_ref[...] - t_ref[...]
        loss_ref[...] += 0.5 * jnp.sum(jnp.mean(e * e, axis=-1, keepdims=True))
        dout = e * (1.0 / d)
        dfg_ref[...] += jnp.sum(dout * y, axis=0, keepdims=True)
        dy = dout * fg_ref[...]
        dh2 = r * (dy - y * jnp.mean(dy * y, axis=-1, keepdims=True))
        dh_ref[...] = dh2
        dgt_ref[...] += jnp.sum(dh2 * ffv, axis=0, keepdims=True)
        dff_ref[...] = (dh2 * gt_ref[...]).astype(dff_ref.dtype)

    row = pl.BlockSpec((tr, d), lambda i: (i, 0))
    vec = pl.BlockSpec((1, d), lambda i: (0, 0))
    return pl.pallas_call(
        body, grid=(rows // tr,), in_specs=[row, row, vec, vec, row],
        out_specs=[row, row, pl.BlockSpec((1, LANES), lambda i: (0, 0)), vec, vec],
        out_shape=[jax.ShapeDtypeStruct((rows, d), F32), jax.ShapeDtypeStruct((rows, d), BF16),
                   jax.ShapeDtypeStruct((1, LANES), F32), jax.ShapeDtypeStruct((1, d), F32),
                   jax.ShapeDtypeStruct((1, d), F32)],
        compiler_params=_cparams(("arbitrary",)), name="final_fwd_bwd")(h1, ff, gate2, final_g, target)


def _norm_mod_bwd(dz, hin, dres, g, scale, gate, mo, *, name):
    rows, d = hin.shape
    with_gate = gate is not None
    tr = _row_tile(rows, d, 6)

    def body(*refs):
        if with_gate:
            (dz_ref, h_ref, dr_ref, g_ref, sc_ref, gt_ref, mo_ref,
             dh_ref, dsh_ref, dsc_ref, dg_ref, dmo_ref, dgt_ref) = refs
        else:
            dz_ref, h_ref, dr_ref, g_ref, sc_ref, dh_ref, dsh_ref, dsc_ref, dg_ref = refs
        i = pl.program_id(0)

        @pl.when(i == 0)
        def _():
            dsh_ref[...] = jnp.zeros_like(dsh_ref)
            dsc_ref[...] = jnp.zeros_like(dsc_ref)
            dg_ref[...] = jnp.zeros_like(dg_ref)
            if with_gate:
                dgt_ref[...] = jnp.zeros_like(dgt_ref)

        dzv = dz_ref[...].astype(F32)
        h = h_ref[...]
        r = lax.rsqrt(jnp.mean(h * h, axis=-1, keepdims=True) + EPS)
        y = h * r
        dsh_ref[...] += jnp.sum(dzv, axis=0, keepdims=True)
        dsc_ref[...] += jnp.sum(dzv * (y * g_ref[...]), axis=0, keepdims=True)
        dn = dzv * (1.0 + sc_ref[...])
        dg_ref[...] += jnp.sum(dn * y, axis=0, keepdims=True)
        dy = dn * g_ref[...]
        dh = dr_ref[...] + r * (dy - y * jnp.mean(dy * y, axis=-1, keepdims=True))
        dh_ref[...] = dh
        if with_gate:
            dmo_ref[...] = (dh * gt_ref[...]).astype(dmo_ref.dtype)
            dgt_ref[...] += jnp.sum(dh * mo_ref[...].astype(F32), axis=0, keepdims=True)

    row = pl.BlockSpec((tr, d), lambda i: (i, 0))
    vec = pl.BlockSpec((1, d), lambda i: (0, 0))
    vshape = jax.ShapeDtypeStruct((1, d), F32)
    in_specs = [row, row, row, vec, vec]
    args = [dz, hin, dres, g, scale]
    out_specs = [row, vec, vec, vec]
    out_shape = [jax.ShapeDtypeStruct((rows, d), F32), vshape, vshape, vshape]
    if with_gate:
        in_specs += [vec, row]
        args += [gate, mo]
        out_specs += [row, vec]
        out_shape += [jax.ShapeDtypeStruct((rows, d), BF16), vshape]
    return pl.pallas_call(
        body, grid=(rows // tr,), in_specs=in_specs, out_specs=out_specs, out_shape=out_shape,
        compiler_params=_cparams(("arbitrary",)), name=name)(*args)


def _s5_discretise(a_re, a_im, log_dt, b_re, b_im):
    dt = jnp.exp(log_dt)[:, None]
    er = jnp.exp(a_re * dt)
    lr = er * jnp.cos(a_im * dt)
    li = er * jnp.sin(a_im * dt)
    den = a_re * a_re + a_im * a_im
    cr = ((lr - 1.0) * a_re + li * a_im) / den
    ci = (li * a_re - (lr - 1.0) * a_im) / den
    bbr = cr[..., None] * b_re - ci[..., None] * b_im
    bbi = cr[..., None] * b_im + ci[..., None] * b_re
    return lr, li, bbr, bbi


def _block_diag(w):
    g, r, c = w.shape
    nb = g // GROUPS_PER_BLOCK
    eye = jnp.eye(GROUPS_PER_BLOCK, dtype=w.dtype)
    w5 = w.reshape(nb, GROUPS_PER_BLOCK, r, 1, c) * eye[None, :, None, :, None]
    return w5.reshape(nb, GROUPS_PER_BLOCK * r, GROUPS_PER_BLOCK * c)


def _block_diag_extract(m, r, c):
    nb = m.shape[0]
    m5 = m.reshape(nb, GROUPS_PER_BLOCK, r, GROUPS_PER_BLOCK, c)
    idx = jnp.arange(GROUPS_PER_BLOCK)
    d = m5[:, idx, :, idx, :]
    return jnp.moveaxis(d, 0, 1).reshape(nb * GROUPS_PER_BLOCK, r, c)


def _scan_multipliers(lr, li):
    pr, pi = [lr], [li]
    for _ in range(SUBLANES - 1):
        pr.append(pr[-1] * lr - pi[-1] * li)
        pi.append(pr[-2] * li + pi[-1] * lr)
    pr = jnp.concatenate(pr, axis=1)
    pi = jnp.concatenate(pi, axis=1)
    rows = jnp.arange(SUBLANES)[None, :, None]
    fr, fi, rr, ri = [], [], [], []
    for s in (1, 2, 4):
        mf = (rows >= s).astype(F32)
        mr = (rows <= SUBLANES - 1 - s).astype(F32)
        fr.append(mf * pr[:, s - 1:s, :])
        fi.append(mf * pi[:, s - 1:s, :])
        rr.append(mr * pr[:, s - 1:s, :])
        ri.append(mr * pi[:, s - 1:s, :])
    fr.append(pr)
    fi.append(pi)
    rr.append(pr[:, ::-1, :])
    ri.append(pi[:, ::-1, :])
    st = lambda xs: jnp.stack(xs, axis=1)
    return st(fr), st(fi), st(rr), st(ri)


def _scan_rows(sre, sim, mul_r, mul_i, n_groups, reverse):
    sgn = -1.0 if reverse else 1.0
    lanes = sre.shape[1]

    def step(k, carry):
        cr, ci = carry
        kk = (n_groups - 1 - k) if reverse else k
        r0 = pl.multiple_of(kk * SUBLANES, SUBLANES)
        xr = sre[pl.ds(r0, SUBLANES), :]
        xi = sim[pl.ds(r0, SUBLANES), :]
        for lvl, s in enumerate((1, 2, 4)):
            sh = (SUBLANES - s) if reverse else s
            nr = pltpu.roll(xr, sh, 0)
            ni = pltpu.roll(xi, sh, 0)
            mr = mul_r[lvl]
            mi = mul_i[lvl] * sgn
            xr, xi = xr + mr * nr - mi * ni, xi + mr * ni + mi * nr
        mr = mul_r[3]
        mi = mul_i[3] * sgn
        xr, xi = xr + mr * cr - mi * ci, xi + mr * ci + mi * cr
        sre[pl.ds(r0, SUBLANES), :] = xr
        sim[pl.ds(r0, SUBLANES), :] = xi
        edge = 0 if reverse else SUBLANES - 1
        ncr = jnp.broadcast_to(xr[edge:edge + 1, :], (SUBLANES, lanes))
        nci = jnp.broadcast_to(xi[edge:edge + 1, :], (SUBLANES, lanes))
        return ncr, nci

    zero = jnp.zeros((SUBLANES, lanes), F32)
    lax.fori_loop(0, n_groups, step, (zero, zero))


def _dot(a, b):
    return jnp.dot(a, b, preferred_element_type=F32)


def _dotf(a, b):
    return _dot(a.astype(BF16), b)


def _s5_operands(lr, li, bbr, bbi, c_re, c_im):
    g = lr.shape[0]
    nb = g // GROUPS_PER_BLOCK
    tb = lambda w: jnp.swapaxes(w, 1, 2)
    b_in = [_block_diag(tb(bbr)), _block_diag(tb(bbi))]
    c_out = [_block_diag(tb(c_re)), _block_diag(tb(c_im))]
    b_out = [_block_diag(bbr), _block_diag(bbi)]
    c_in = [_block_diag(c_re), _block_diag(c_im)]
    lam_r = lr.reshape(nb, 1, STATE_LANES)
    lam_i = li.reshape(nb, 1, STATE_LANES)
    mults = _scan_multipliers(lam_r, lam_i)
    cast = lambda ws: [w.astype(BF16) for w in ws]
    return cast(b_in), cast(c_out), cast(b_out), cast(c_in), mults


def _s5_fwd(proj, d_skip, b_in, c_out, mults, *, col0, carry=None):
    rows = proj.shape[0]
    nb = b_in[0].shape[0]
    tm = _pick(rows, 512, SUBLANES)
    n_tiles = rows // tm
    s_l = STATE_LANES

    def body(u_ref, dk_ref, br, bi, cr, ci, fr_ref, fi_ref, o_ref, sre, sim):
        for t in range(n_tiles):
            rs = pl.ds(t * tm, tm)
            ub = u_ref[rs, :]
            sre[rs, :] = _dot(ub, br[...])
            sim[rs, :] = _dot(ub, bi[...])
        _scan_rows(sre, sim, fr_ref, fi_ref, rows // SUBLANES, False)
        for t in range(n_tiles):
            rs = pl.ds(t * tm, tm)
            y0 = _dotf(sre[rs, :], cr[...]) - _dotf(sim[rs, :], ci[...])
            y1 = y0 + dk_ref[...] * u_ref[rs, :].astype(F32)
            o_ref[rs, :] = _gelu(y1).astype(o_ref.dtype)

    mat_in = pl.BlockSpec((None, LANES, s_l), lambda g: (g, 0, 0))
    mat_out = pl.BlockSpec((None, s_l, LANES), lambda g: (g, 0, 0))
    mul = pl.BlockSpec((None, 4, SUBLANES, s_l), lambda g: (g, 0, 0, 0))
    outs, carried = _call(
        body, grid=(nb,),
        in_specs=[pl.BlockSpec((rows, LANES), lambda g: (0, col0 + g)), pl.BlockSpec((1, LANES), lambda g: (0, g))]
        + [mat_in] * 2 + [mat_out] * 2 + [mul] * 2,
        out_specs=[pl.BlockSpec((rows, LANES), lambda g: (0, g))],
        out_shape=[jax.ShapeDtypeStruct((rows, nb * LANES), BF16)],
        scratch_shapes=[pltpu.VMEM((rows, s_l), F32), pltpu.VMEM((rows, s_l), F32)],
        semantics=("parallel",), name="s5_fwd", args=[proj, d_skip, *b_in, *c_out, mults[0], mults[1]], carry=carry)
    return outs[0] if carry is None else (outs[0], carried)


def _s5_bwd(proj, dyg, d_skip, b_in, c_out, b_out, c_in, mults, *, col0, carry=None):
    rows = proj.shape[0]
    nb = b_in[0].shape[0]
    tm = _pick(rows, 512, SUBLANES)
    n_tiles = rows // tm
    s_l = STATE_LANES
    n_groups = rows // SUBLANES
    tn = (((0,), (0,)), ((), ()))

    def body(u_ref, dy_ref, dk_ref, br, bi, cr, ci, bor, boi, cir, cii, fr_ref, fi_ref, rr_ref, ri_ref,
             du_ref, ddk_ref, dbr_ref, dbi_ref, dcr_ref, dci_ref, dlr_ref, dli_ref,
             sre, sim, gre, gim, dy1):
        for t in range(n_tiles):
            rs = pl.ds(t * tm, tm)
            ub = u_ref[rs, :]
            sre[rs, :] = _dot(ub, br[...])
            sim[rs, :] = _dot(ub, bi[...])
        _scan_rows(sre, sim, fr_ref, fi_ref, n_groups, False)

        ddk = jnp.zeros((1, LANES), F32)
        dcr = jnp.zeros((s_l, LANES), F32)
        dci = jnp.zeros((s_l, LANES), F32)
        for t in range(n_tiles):
            rs = pl.ds(t * tm, tm)
            sr = sre[rs, :]
            si = sim[rs, :]
            uf = u_ref[rs, :].astype(F32)
            y0 = _dotf(sr, cr[...]) - _dotf(si, ci[...])
            y1 = y0 + dk_ref[...] * uf
            d1 = dy_ref[rs, :].astype(F32) * _gelu_grad(y1)
            dy1[rs, :] = d1
            ddk = ddk + jnp.sum(d1 * uf, axis=0, keepdims=True)
            d1b = d1.astype(BF16)
            dcr = dcr + lax.dot_general(sr.astype(BF16), d1b, tn, preferred_element_type=F32)
            dci = dci - lax.dot_general(si.astype(BF16), d1b, tn, preferred_element_type=F32)
            gre[rs, :] = _dot(d1b, cir[...])
            gim[rs, :] = -_dot(d1b, cii[...])
        ddk_ref[...] = ddk
        dcr_ref[...] = dcr
        dci_ref[...] = dci
        _scan_rows(gre, gim, rr_ref, ri_ref, n_groups, True)

        dbr = jnp.zeros((LANES, s_l), F32)
        dbi = jnp.zeros((LANES, s_l), F32)
        for t in range(n_tiles):
            rs = pl.ds(t * tm, tm)
            gr = gre[rs, :]
            gi = gim[rs, :]
            grb = gr.astype(BF16)
            gib = gi.astype(BF16)
            du = _dot(grb, bor[...]) + _dot(gib, boi[...]) + dy1[rs, :] * dk_ref[...]
            du_ref[rs, :] = du.astype(du_ref.dtype)
            ub = u_ref[rs, :]
            dbr = dbr + lax.dot_general(ub, grb, tn, preferred_element_type=F32)
            dbi = dbi + lax.dot_general(ub, gib, tn, preferred_element_type=F32)
        dbr_ref[...] = dbr
        dbi_ref[...] = dbi

        row_id = lax.broadcasted_iota(jnp.int32, (SUBLANES, s_l), 0)

        def lam_step(k, carry):
            ar, ai = carry
            r0 = pl.multiple_of(k * SUBLANES, SUBLANES)
            p0 = pl.multiple_of(jnp.maximum(k - 1, 0) * SUBLANES, SUBLANES)
            keep = jnp.where(k == 0, 0.0, 1.0)
            gr = gre[pl.ds(r0, SUBLANES), :]
            gi = gim[pl.ds(r0, SUBLANES), :]
            pr = jnp.where(row_id == 0, pltpu.roll(sre[pl.ds(p0, SUBLANES), :], 1, 0) * keep,
                           pltpu.roll(sre[pl.ds(r0, SUBLANES), :], 1, 0))
            pi = jnp.where(row_id == 0, pltpu.roll(sim[pl.ds(p0, SUBLANES), :], 1, 0) * keep,
                           pltpu.roll(sim[pl.ds(r0, SUBLANES), :], 1, 0))
            return ar + gr * pr + gi * pi, ai + gi * pr - gr * pi

        zero = jnp.zeros((SUBLANES, s_l), F32)
        ar, ai = lax.fori_loop(0, n_groups, lam_step, (zero, zero))
        dlr_ref[...] = jnp.sum(ar, axis=0, keepdims=True)
        dli_ref[...] = jnp.sum(ai, axis=0, keepdims=True)

    mat_in = pl.BlockSpec((None, LANES, s_l), lambda g: (g, 0, 0))
    mat_out = pl.BlockSpec((None, s_l, LANES), lambda g: (g, 0, 0))
    mul = pl.BlockSpec((None, 4, SUBLANES, s_l), lambda g: (g, 0, 0, 0))
    lam = pl.BlockSpec((None, 1, s_l), lambda g: (g, 0, 0))
    col = pl.BlockSpec((rows, LANES), lambda g: (0, g))
    vec = pl.BlockSpec((1, LANES), lambda g: (0, g))
    outs, carried = _call(
        body, grid=(nb,),
        in_specs=[pl.BlockSpec((rows, LANES), lambda g: (0, col0 + g)), col, vec]
        + [mat_in] * 2 + [mat_out] * 2 + [mat_out] * 2 + [mat_in] * 2 + [mul] * 4,
        out_specs=[col, vec, mat_in, mat_in, mat_out, mat_out, lam, lam],
        out_shape=[jax.ShapeDtypeStruct((rows, nb * LANES), BF16), jax.ShapeDtypeStruct((1, nb * LANES), F32),
                   jax.ShapeDtypeStruct((nb, LANES, s_l), F32), jax.ShapeDtypeStruct((nb, LANES, s_l), F32),
                   jax.ShapeDtypeStruct((nb, s_l, LANES), F32), jax.ShapeDtypeStruct((nb, s_l, LANES), F32),
                   jax.ShapeDtypeStruct((nb, 1, s_l), F32), jax.ShapeDtypeStruct((nb, 1, s_l), F32)],
        scratch_shapes=[pltpu.VMEM((rows, s_l), F32)] * 4 + [pltpu.VMEM((rows, LANES), F32)],
        semantics=("parallel",), name="s5_bwd",
        args=[proj, dyg, d_skip, *b_in, *c_out, *b_out, *c_in, *mults], carry=carry)
    return outs if carry is None else (outs, carried)


def _silu(v):
    return v * _sigmoid(v)


def _ada_fwd(c_all, w_shard, b_cols):
    d, n = w_shard.shape
    bn = _pick(n, 512)

    def body(c_ref, w_ref, b_ref, o_ref):
        ca = _silu(c_ref[...]).astype(BF16)
        o_ref[...] = _dot(ca, w_ref[...].astype(BF16)) + b_ref[...]

    return pl.pallas_call(
        body, grid=(n // bn,),
        in_specs=[pl.BlockSpec((N_DEV, d), lambda j: (0, 0)), pl.BlockSpec((d, bn), lambda j: (0, j)),
                  pl.BlockSpec((1, bn), lambda j: (0, j))],
        out_specs=pl.BlockSpec((N_DEV, bn), lambda j: (0, j)),
        out_shape=jax.ShapeDtypeStruct((N_DEV, n), F32),
        compiler_params=_cparams(("parallel",)), name="ada_fwd")(c_all, w_shard, b_cols)


def _ada_bwd(c_all, dmod_cols):
    d = c_all.shape[1]
    n = dmod_cols.shape[1]
    bn = _pick(n, 512)

    def body(c_ref, g_ref, o_ref):
        ca = _silu(c_ref[...]).astype(BF16)
        o_ref[...] = lax.dot_general(ca, g_ref[...].astype(BF16), (((0,), (0,)), ((), ())),
                                     preferred_element_type=F32)

    return pl.pallas_call(
        body, grid=(n // bn,),
        in_specs=[pl.BlockSpec((N_DEV, d), lambda j: (0, 0)), pl.BlockSpec((N_DEV, bn), lambda j: (0, j))],
        out_specs=pl.BlockSpec((d, bn), lambda j: (0, j)),
        out_shape=jax.ShapeDtypeStruct((d, n), F32),
        compiler_params=_cparams(("parallel",)), name="ada_bwd")(c_all, dmod_cols)


def _cast_bf16(w, *, name):
    rows, cols = w.shape
    tr = _row_tile(rows, cols, 2)

    def body(w_ref, o_ref):
        o_ref[...] = w_ref[...].astype(BF16)

    row = pl.BlockSpec((tr, cols), lambda i: (i, 0))
    return pl.pallas_call(
        body, grid=(rows // tr,), in_specs=[row], out_specs=row,
        out_shape=jax.ShapeDtypeStruct((rows, cols), BF16),
        compiler_params=_cparams(("parallel",)), name=name)(w)


def _adamw(w, g, m, v, *, name):
    rows, cols = w.shape
    tr = _row_tile(rows, cols, 7)
    c1 = 1.0 / (1.0 - ADAM_B1 ** ADAM_STEP)
    c2 = 1.0 / (1.0 - ADAM_B2 ** ADAM_STEP)

    def body(w_ref, g_ref, m_ref, v_ref, d_ref, nm_ref, nv_ref):
        gv = g_ref[...]
        nm = ADAM_B1 * m_ref[...] + (1.0 - ADAM_B1) * gv
        nv = ADAM_B2 * v_ref[...] + (1.0 - ADAM_B2) * (gv * gv)
        nm_ref[...] = nm
        nv_ref[...] = nv
        d_ref[...] = -ADAM_LR * ((nm * c1) / (jnp.sqrt(nv * c2) + ADAM_EPS) + ADAM_WD * w_ref[...])

    row = pl.BlockSpec((tr, cols), lambda i: (i, 0))
    shp = jax.ShapeDtypeStruct((rows, cols), F32)
    return pl.pallas_call(
        body, grid=(rows // tr,), in_specs=[row] * 4, out_specs=[row] * 3, out_shape=[shp] * 3,
        compiler_params=_cparams(("parallel",)), name=name)(w, g, m, v)


def _sum_leading(a, *, name, out_dtype=F32):
    n, rows, cols = a.shape
    tr = _row_tile(rows, cols, n + 1)

    def body(a_ref, o_ref):
        acc = a_ref[0].astype(F32)
        for i in range(1, n):
            acc = acc + a_ref[i].astype(F32)
        o_ref[...] = acc.astype(o_ref.dtype)

    return pl.pallas_call(
        body, grid=(rows // tr,), in_specs=[pl.BlockSpec((n, tr, cols), lambda i: (0, i, 0))],
        out_specs=pl.BlockSpec((tr, cols), lambda i: (i, 0)),
        out_shape=jax.ShapeDtypeStruct((rows, cols), out_dtype),
        compiler_params=_cparams(("parallel",)), name=name)(a)


def _add_half(dw, land, my_c, *, name):
    n, r, cols = dw.shape
    h = r // 2
    tr = _row_tile(h, cols, 3)
    hb = h // tr

    def body(c_ref, a_ref, b_ref, o_ref):
        o_ref[...] = (a_ref[...].astype(F32) + b_ref[...].astype(F32)).astype(o_ref.dtype)

    gs = pltpu.PrefetchScalarGridSpec(
        num_scalar_prefetch=1, grid=(n, hb),
        in_specs=[pl.BlockSpec((None, tr, cols), lambda s, i, c_ref: (s, c_ref[0] * hb + i, 0)),
                  pl.BlockSpec((None, tr, cols), lambda s, i, c_ref: (s, i, 0))],
        out_specs=pl.BlockSpec((None, tr, cols), lambda s, i, c_ref: (s, i, 0)))
    return pl.pallas_call(
        body, grid_spec=gs, out_shape=jax.ShapeDtypeStruct((n, h, cols), BF16),
        compiler_params=_cparams(("parallel", "parallel")), name=name)(my_c, dw, land)


def _mesh_pos():
    return lax.axis_index("x"), lax.axis_index("y"), lax.axis_index("c")


def _other_chips(x, y):
    return [(1 - x, y), (x, 1 - y), (1 - x, 1 - y)]


def _gather_small(blk, *, name):
    m_per, n = blk.shape

    def body(x_ref, out_ref, send_sems, recv_sems, local_sem):
        x, y, c = _mesh_pos()
        me, sibling = (x, y, c), (x, y, 1 - c)
        chips = _other_chips(x, y)

        def rows(px, py, pc):
            return out_ref.at[pl.ds((4 * px + 2 * py + pc) * m_per, m_per), :]

        def copy(k, block, to, src=None):
            return pltpu.make_async_remote_copy(
                src_ref=rows(*block) if src is None else src, dst_ref=rows(*block),
                send_sem=send_sems.at[k], recv_sem=recv_sems.at[k], device_id=to, device_id_type=MESH)

        mine = pltpu.make_async_copy(x_ref, rows(*me), local_sem)
        mine.start()
        first = [copy(0, me, sibling, src=x_ref)]
        first += [copy(1 + j, me, (*chip, c), src=x_ref) for j, chip in enumerate(chips)]
        for cp in first:
            cp.start()
        passed = [copy(4 + j, (*chip, c), sibling) for j, chip in enumerate(chips)]
        for j, chip in enumerate(chips):
            copy(1 + j, (*chip, c), me).wait_recv()
            passed[j].start()
        copy(0, sibling, me).wait_recv()
        for j, chip in enumerate(chips):
            copy(4 + j, (*chip, 1 - c), me).wait_recv()
        for cp in first + passed:
            cp.wait_send()
        mine.wait()

    return pl.pallas_call(
        body, out_shape=jax.ShapeDtypeStruct((N_DEV * m_per, n), blk.dtype),
        in_specs=[pl.BlockSpec(memory_space=pltpu.VMEM)], out_specs=pl.BlockSpec(memory_space=pltpu.VMEM),
        scratch_shapes=[pltpu.SemaphoreType.DMA((7,)), pltpu.SemaphoreType.DMA((7,)), pltpu.SemaphoreType.DMA],
        compiler_params=pltpu.CompilerParams(vmem_limit_bytes=VMEM_LIMIT_BYTES), name=name)(blk)


def _hbm_specs(n):
    return [pl.BlockSpec(memory_space=pl.ANY)] * n


def _gather_weights(shards):
    n = len(shards)

    def body(*refs):
        ins, outs = refs[:n], refs[n:2 * n]
        send_sems, recv_sems, local_sems = refs[2 * n:]
        x, y, c = _mesh_pos()
        me_chip = 2 * x + y
        sibling = (x, y, 1 - c)
        chips = _other_chips(x, y)

        def half(w, chip_idx, pc):
            h = shards[w].shape[0] // 2
            return outs[w].at[chip_idx, pl.ds(pc * h, h), :]

        def copy(w, k, chip_idx, pc, to, src=None):
            dst = half(w, chip_idx, pc)
            return pltpu.make_async_remote_copy(
                src_ref=dst if src is None else src, dst_ref=dst,
                send_sem=send_sems.at[6 * w + k], recv_sem=recv_sems.at[6 * w + k],
                device_id=to, device_id_type=MESH)

        local = [pltpu.make_async_copy(ins[w], outs[w].at[me_chip], local_sems.at[w]) for w in range(n)]
        for cp in local:
            cp.start()
        sends = []
        for w in range(n):
            h = shards[w].shape[0] // 2
            for j, chip in enumerate(chips):
                cp = copy(w, j, me_chip, c, (*chip, c), src=ins[w].at[pl.ds(c * h, h), :])
                cp.start()
                sends.append(cp)
        for w in range(n):
            for j, chip in enumerate(chips):
                chip_idx = 2 * chip[0] + chip[1]
                copy(w, j, chip_idx, c, (x, y, c)).wait_recv()
                cp = copy(w, 3 + j, chip_idx, c, sibling)
                cp.start()
                sends.append(cp)
        for w in range(n):
            for j, chip in enumerate(chips):
                copy(w, 3 + j, 2 * chip[0] + chip[1], 1 - c, (x, y, c)).wait_recv()
        for cp in sends:
            cp.wait_send()
        for cp in local:
            cp.wait()

    return pl.pallas_call(
        body, out_shape=[jax.ShapeDtypeStruct((N_CHIPS,) + s.shape, s.dtype) for s in shards],
        in_specs=_hbm_specs(n), out_specs=_hbm_specs(n),
        scratch_shapes=[pltpu.SemaphoreType.DMA((6 * n,)), pltpu.SemaphoreType.DMA((6 * n,)),
                        pltpu.SemaphoreType.DMA((n,))],
        name="gather_weights")(*shards)


def _swap_halves(dws, *, name):
    n = len(dws)

    def body(*refs):
        ins, outs = refs[:n], refs[n:2 * n]
        send_sems, recv_sems = refs[2 * n:]
        x, y, c = _mesh_pos()
        cps = []
        for w in range(n):
            h = dws[w].shape[1] // 2
            cp = pltpu.make_async_remote_copy(
                src_ref=ins[w].at[:, pl.ds((1 - c) * h, h), :], dst_ref=outs[w],
                send_sem=send_sems.at[w], recv_sem=recv_sems.at[w],
                device_id=(x, y, 1 - c), device_id_type=MESH)
            cp.start()
            cps.append(cp)
        for cp in cps:
            cp.wait()

    return pl.pallas_call(
        body, out_shape=[jax.ShapeDtypeStruct((s.shape[0], s.shape[1] // 2, s.shape[2]), s.dtype) for s in dws],
        in_specs=_hbm_specs(n), out_specs=_hbm_specs(n),
        scratch_shapes=[pltpu.SemaphoreType.DMA((n,)), pltpu.SemaphoreType.DMA((n,))],
        name=name)(*dws)


def _chip_exchange(parts):
    n = len(parts)

    def body(*refs):
        ins, outs = refs[:n], refs[n:2 * n]
        send_sems, recv_sems, local_sems = refs[2 * n:]
        x, y, c = _mesh_pos()
        me_chip = 2 * x + y
        chips = _other_chips(x, y)
        local = [pltpu.make_async_copy(ins[w].at[me_chip], outs[w].at[me_chip], local_sems.at[w]) for w in range(n)]
        for cp in local:
            cp.start()
        cps = []
        for w in range(n):
            for j, chip in enumerate(chips):
                cp = pltpu.make_async_remote_copy(
                    src_ref=ins[w].at[2 * chip[0] + chip[1]], dst_ref=outs[w].at[me_chip],
                    send_sem=send_sems.at[3 * w + j], recv_sem=recv_sems.at[3 * w + j],
                    device_id=(*chip, c), device_id_type=MESH)
                cp.start()
                cps.append((cp, w, j, chip))
        for cp, w, j, chip in cps:
            slot = outs[w].at[2 * chip[0] + chip[1]]
            pltpu.make_async_remote_copy(
                src_ref=slot, dst_ref=slot, send_sem=send_sems.at[3 * w + j], recv_sem=recv_sems.at[3 * w + j],
                device_id=(x, y, c), device_id_type=MESH).wait_recv()
        for cp, _, _, _ in cps:
            cp.wait_send()
        for cp in local:
            cp.wait()

    return pl.pallas_call(
        body, out_shape=[jax.ShapeDtypeStruct(s.shape, s.dtype) for s in parts],
        in_specs=_hbm_specs(n), out_specs=_hbm_specs(n),
        scratch_shapes=[pltpu.SemaphoreType.DMA((3 * n,)), pltpu.SemaphoreType.DMA((3 * n,)),
                        pltpu.SemaphoreType.DMA((n,))],
        name="chip_exchange")(*parts)


def _join_halves(halves):
    n = len(halves)

    def body(*refs):
        ins, outs = refs[:n], refs[n:2 * n]
        send_sems, recv_sems, local_sems = refs[2 * n:]
        x, y, c = _mesh_pos()
        cps, local = [], []
        for w in range(n):
            h = halves[w].shape[0]
            mine = outs[w].at[pl.ds(c * h, h), :]
            lc = pltpu.make_async_copy(ins[w], mine, local_sems.at[w])
            lc.start()
            local.append(lc)
            cp = pltpu.make_async_remote_copy(
                src_ref=ins[w], dst_ref=mine, send_sem=send_sems.at[w], recv_sem=recv_sems.at[w],
                device_id=(x, y, 1 - c), device_id_type=MESH)
            cp.start()
            cps.append(cp)
        for w in range(n):
            h = halves[w].shape[0]
            theirs = outs[w].at[pl.ds((1 - c) * h, h), :]
            pltpu.make_async_remote_copy(
                src_ref=theirs, dst_ref=theirs, send_sem=send_sems.at[w], recv_sem=recv_sems.at[w],
                device_id=(x, y, c), device_id_type=MESH).wait_recv()
        for cp in cps:
            cp.wait_send()
        for lc in local:
            lc.wait()

    return pl.pallas_call(
        body, out_shape=[jax.ShapeDtypeStruct((2 * s.shape[0], s.shape[1]), s.dtype) for s in halves],
        in_specs=_hbm_specs(n), out_specs=_hbm_specs(n),
        scratch_shapes=[pltpu.SemaphoreType.DMA((n,)), pltpu.SemaphoreType.DMA((n,)), pltpu.SemaphoreType.DMA((n,))],
        name="join_halves")(*halves)


def _cast_into_slot(w, chip, *, name):
    rows, cols = w.shape
    tr = _row_tile(rows, cols, 2)

    def body(chip_ref, w_ref, o_ref):
        o_ref[...] = w_ref[...].astype(BF16)

    gs = pltpu.PrefetchScalarGridSpec(
        num_scalar_prefetch=1, grid=(rows // tr,),
        in_specs=[pl.BlockSpec((tr, cols), lambda i, chip_ref: (i, 0))],
        out_specs=pl.BlockSpec((None, tr, cols), lambda i, chip_ref: (chip_ref[0], i, 0)))
    return pl.pallas_call(
        body, grid_spec=gs, out_shape=jax.ShapeDtypeStruct((N_CHIPS, rows, cols), BF16),
        compiler_params=_cparams(("parallel",)), name=name)(chip, w)


def _row_range(h, lo, hi, parts):
    step = h // parts
    assert step * parts == h and step % (2 * SUBLANES) == 0, (h, parts)
    return lo * step, (hi - lo) * step


def _gather_carry(items):
    n_copies = sum(len(js) for _, js, _, _, _ in items)
    sem = pltpu.SemaphoreType.DMA((2 * n_copies,))

    def copies(outs, sems):
        send_sems, recv_sems = sems
        x, y, c = _mesh_pos()
        me_chip = 2 * x + y
        chips = _other_chips(x, y)
        out_ici, in_ici, out_d2d, in_d2d = [], [], [], []
        k = 0
        for w, (buf, js, lo, hi, parts) in enumerate(items):
            h = buf.shape[1] // 2
            r0, nr = _row_range(h, lo, hi, parts)

            def copy(k, chip_idx, pc, to):
                ref = outs[w].at[chip_idx, pl.ds(pc * h + r0, nr), :]
                return pltpu.make_async_remote_copy(
                    src_ref=ref, dst_ref=ref, send_sem=send_sems.at[k], recv_sem=recv_sems.at[k],
                    device_id=to, device_id_type=MESH)

            for j in js:
                chip = chips[j]
                chip_idx = 2 * chip[0] + chip[1]
                out_ici.append(copy(k, me_chip, c, (*chip, c)))
                in_ici.append(copy(k, chip_idx, c, (x, y, c)))
                out_d2d.append(copy(k + 1, chip_idx, c, (x, y, 1 - c)))
                in_d2d.append(copy(k + 1, chip_idx, 1 - c, (x, y, c)))
                k += 2
        return out_ici, in_ici, out_d2d, in_d2d

    def start(ins, outs, sems):
        for cp in copies(outs, sems)[0]:
            cp.start()

    def finish(ins, outs, sems):
        out_ici, in_ici, out_d2d, in_d2d = copies(outs, sems)
        for arrived, onward in zip(in_ici, out_d2d):
            arrived.wait_recv()
            onward.start()
        for arrived in in_d2d:
            arrived.wait_recv()
        for cp in out_ici + out_d2d:
            cp.wait_send()

    bufs = [it[0] for it in items]
    shapes = [jax.ShapeDtypeStruct(b.shape, b.dtype) for b in bufs]
    return _Carry(bufs, shapes, {i: i for i in range(len(bufs))}, [sem, sem], start, finish)


def _exchange_carry(items):
    n = len(items)
    sem = pltpu.SemaphoreType.DMA((3 * n,))
    given = [w for w in range(n) if items[w][1] is not None]

    def copies(ins, outs, sems):
        send_sems, recv_sems = sems
        x, y, c = _mesh_pos()
        chips = _other_chips(x, y)
        sends, recvs = [], []
        for w, (part, _, lo, hi, parts) in enumerate(items):
            r0, nr = _row_range(part.shape[1], lo, hi, parts)
            for j, chip in enumerate(chips):
                land = outs[w].at[j, pl.ds(r0, nr), :]
                sends.append(pltpu.make_async_remote_copy(
                    src_ref=ins[w].at[2 * chip[0] + chip[1], pl.ds(r0, nr), :], dst_ref=land,
                    send_sem=send_sems.at[3 * w + j], recv_sem=recv_sems.at[3 * w + j],
                    device_id=(*chip, c), device_id_type=MESH))
                recvs.append(pltpu.make_async_remote_copy(
                    src_ref=land, dst_ref=land,
                    send_sem=send_sems.at[3 * w + j], recv_sem=recv_sems.at[3 * w + j],
                    device_id=(x, y, c), device_id_type=MESH))
        return sends, recvs

    def start(ins, outs, sems):
        for cp in copies(ins, outs, sems)[0]:
            cp.start()

    def finish(ins, outs, sems):
        sends, recvs = copies(ins, outs, sems)
        for cp in recvs:
            cp.wait_recv()
        for cp in sends:
            cp.wait_send()

    inputs = [it[0] for it in items] + [items[w][1] for w in given]
    shapes = [jax.ShapeDtypeStruct((3,) + it[0].shape[1:], it[0].dtype) for it in items]
    aliases = {n + i: w for i, w in enumerate(given)}
    return _Carry(inputs, shapes, aliases, [sem, sem], start, finish)


def _sum_into_half(part, landed, chip, my_c, *, name):
    _, h, cols = part.shape
    tr = _row_tile(h, cols, 5)
    hb = h // tr

    def body(chip_ref, c_ref, p_ref, l_ref, o_ref):
        acc = p_ref[...].astype(F32)
        for j in range(3):
            acc = acc + l_ref[j].astype(F32)
        o_ref[...] = acc

    gs = pltpu.PrefetchScalarGridSpec(
        num_scalar_prefetch=2, grid=(hb,),
        in_specs=[pl.BlockSpec((None, tr, cols), lambda i, chip_ref, c_ref: (chip_ref[0], i, 0)),
                  pl.BlockSpec((3, tr, cols), lambda i, chip_ref, c_ref: (0, i, 0))],
        out_specs=pl.BlockSpec((tr, cols), lambda i, chip_ref, c_ref: (c_ref[0] * hb + i, 0)))
    return pl.pallas_call(
        body, grid_spec=gs, out_shape=jax.ShapeDtypeStruct((2 * h, cols), F32),
        compiler_params=_cparams(("parallel",)), name=name)(chip, my_c, part, landed)


def _join_carry(fulls):
    n = len(fulls)
    sem = pltpu.SemaphoreType.DMA((n,))

    def copies(outs, sems):
        send_sems, recv_sems = sems
        x, y, c = _mesh_pos()
        sends, recvs = [], []
        for w in range(n):
            h = fulls[w].shape[0] // 2
            mine = outs[w].at[pl.ds(c * h, h), :]
            theirs = outs[w].at[pl.ds((1 - c) * h, h), :]
            sends.append(pltpu.make_async_remote_copy(
                src_ref=mine, dst_ref=mine, send_sem=send_sems.at[w], recv_sem=recv_sems.at[w],
                device_id=(x, y, 1 - c), device_id_type=MESH))
            recvs.append(pltpu.make_async_remote_copy(
                src_ref=theirs, dst_ref=theirs, send_sem=send_sems.at[w], recv_sem=recv_sems.at[w],
                device_id=(x, y, c), device_id_type=MESH))
        return sends, recvs

    def start(ins, outs, sems):
        for cp in copies(outs, sems)[0]:
            cp.start()

    def finish(ins, outs, sems):
        sends, recvs = copies(outs, sems)
        for cp in recvs:
            cp.wait_recv()
        for cp in sends:
            cp.wait_send()

    shapes = [jax.ShapeDtypeStruct(f.shape, f.dtype) for f in fulls]
    return _Carry(fulls, shapes, {i: i for i in range(n)}, [sem, sem], start, finish)


class _NoComm:
    def __init__(self, big):
        self.big = big
        self.grads = {}

    def weight(self, name):
        return self.big[name]

    def mm_in(self, u):
        return _mm(u, self.big["w_in"], mode="nn", out_dtype=BF16, name="mm_in")

    def mm_d_in(self, dproj):
        return _mm(dproj, self.big["w_in"], mode="nt", out_dtype=F32, name="mm_d_in")

    def carry(self, site):
        return None

    def done(self, site, carried):
        pass

    def grad(self, name, dw):
        self.grads[name] = dw


def _gather_fresh_carry(own, js):
    n = len(js)
    h = own.shape[0] // 2
    sem = pltpu.SemaphoreType.DMA((2 * n,))

    def copies(ins, outs, sems):
        send_sems, recv_sems = sems
        x, y, c = _mesh_pos()
        chips = _other_chips(x, y)
        out_ici, in_ici, out_d2d, in_d2d = [], [], [], []

        def copy(k, src, dst, to):
            return pltpu.make_async_remote_copy(
                src_ref=src, dst_ref=dst, send_sem=send_sems.at[k], recv_sem=recv_sems.at[k],
                device_id=to, device_id_type=MESH)

        for jj, j in enumerate(js):
            mine = ins[0].at[pl.ds(c * h, h), :]
            land = outs[0].at[jj, pl.ds(c * h, h), :]
            other = outs[0].at[jj, pl.ds((1 - c) * h, h), :]
            out_ici.append(copy(2 * jj, mine, land, (*chips[j], c)))
            in_ici.append(copy(2 * jj, land, land, (x, y, c)))
            out_d2d.append(copy(2 * jj + 1, land, land, (x, y, 1 - c)))
            in_d2d.append(copy(2 * jj + 1, other, other, (x, y, c)))
        return out_ici, in_ici, out_d2d, in_d2d

    def start(ins, outs, sems):
        for cp in copies(ins, outs, sems)[0]:
            cp.start()

    def finish(ins, outs, sems):
        out_ici, in_ici, out_d2d, in_d2d = copies(ins, outs, sems)
        for arrived, onward in zip(in_ici, out_d2d):
            arrived.wait_recv()
            onward.start()
        for arrived in in_d2d:
            arrived.wait_recv()
        for cp in out_ici + out_d2d:
            cp.wait_send()

    return _Carry([own], [jax.ShapeDtypeStruct((n,) + own.shape, own.dtype)], {}, [sem, sem], start, finish)


ALL_CHIPS = (0, 1, 2)


class _MeshComm:
    GATHER_AT = {
        "mm_in_diag": [("w_conv_out", ALL_CHIPS, 0, 1, 1)],
        "conv_fwd": [("w_glu_a", ALL_CHIPS, 0, 1, 1), ("w_glu_b", ALL_CHIPS, 0, 1, 1), ("w_out", ALL_CHIPS, 0, 1, 1)],
        "s5_fwd": [("w_ff1", ALL_CHIPS, 0, 6, 8)],
        "mm_glu_a": [("w_ff1", ALL_CHIPS, 6, 7, 8)],
        "mm_glu_b": [("w_ff1", ALL_CHIPS, 7, 8, 8)],
        "mm_out": [("w_ff2", ALL_CHIPS, 0, 2, 8)],
        "mm_ff1": [("w_ff2", ALL_CHIPS, 2, 8, 8)],
    }
    EXCHANGE_AT = {
        "mm_d_ff2": [("w_ff2", 0, 6, 8)],
        "mm_dw_ff1": [("w_ff2", 6, 8, 8)],
        "mm_d_ff1": [("w_ff1", 0, 6, 8)],
        "mm_dw_out": [("w_ff1", 6, 8, 8)],
        "s5_bwd": [("w_out", 0, 1, 1), ("w_glu_a", 0, 1, 1), ("w_glu_b", 0, 1, 1), ("w_conv_out", 0, 1, 1)],
        "mm_d_in": [("w_in", 0, 1, 1)],
    }

    def __init__(self, shards, pos, chip, my_c):
        self.pos = pos
        self.chip = chip
        self.my_c = my_c
        self.w_in_own = _cast_bf16(shards["w_in"], name="cast_w_in")
        self.bufs = {n: _cast_into_slot(s, chip, name="cast_" + n) for n, s in shards.items() if n != "w_in"}
        self.raw = {}
        self.parts = {}
        self.landing = {}
        self.halves = {}
        self.pending = {}
        self.last_site = {}
        for site, items in self.EXCHANGE_AT.items():
            for it in items:
                self.last_site[it[0]] = site

    def weight(self, name):
        g = self.bufs[name]
        return g.reshape(g.shape[0] * g.shape[1], g.shape[2]) if name in ROW_SHARDED else g

    def _slot_ids(self):
        x, y, _ = self.pos
        ids = [2 * x + y] + [2 * cx + cy for cx, cy in _other_chips(x, y)]
        return jnp.stack(ids).astype(jnp.int32)

    def mm_in(self, u):
        ids = self._slot_ids()
        own = self.w_in_own
        proj, (near,) = _mm_slots(u, own[None], ids[0:1], None, name="mm_in_own",
                                  carry=_gather_fresh_carry(own, (0, 1)))
        proj, (far,) = _mm_slots(u, near, ids[1:3], proj, name="mm_in_near",
                                 carry=_gather_fresh_carry(own, (2,)))
        proj, carried = _mm_slots(u, far, ids[3:4], proj, name="mm_in_diag", carry=self.carry("mm_in_diag"))
        self.done("mm_in_diag", carried)
        self.w_in_rel = jnp.concatenate([own[None], near, far], axis=0)
        return proj

    def mm_d_in(self, dproj):
        carry = self.carry("mm_d_in")
        du, carried = _mm(dproj, self.w_in_rel, mode="nt", out_dtype=F32, name="mm_d_in", carry=carry,
                          a_slots=self._slot_ids())
        self.done("mm_d_in", carried)
        return du

    def carry(self, site):
        if site in self.GATHER_AT:
            items = self.GATHER_AT[site]
            self.pending[site] = items
            return _gather_carry([(self.bufs[it[0]],) + tuple(it[1:]) for it in items])
        if site not in self.EXCHANGE_AT:
            return None
        items = self.EXCHANGE_AT[site]
        self.pending[site] = items
        fresh = [it[0] for it in items if it[0] in self.raw]
        if fresh:
            raws = [self.raw.pop(n) for n in fresh]
            landed = _swap_halves(raws, name="swap_halves_" + site)
            for n, dw, l1 in zip(fresh, raws, landed):
                self.parts[n] = _add_half(dw, l1, self.my_c, name="add_half_" + n)
        return _exchange_carry([(self.parts[it[0]], self.landing.get(it[0])) + tuple(it[1:]) for it in items])

    def done(self, site, carried):
        items = self.pending.pop(site)
        names = [it[0] for it in items]
        if site in self.GATHER_AT:
            self.bufs.update(zip(names, carried))
            return
        for n, landed in zip(names, carried):
            self.landing[n] = landed
            if self.last_site[n] == site:
                self.halves[n] = _sum_into_half(self.parts.pop(n), self.landing.pop(n), self.chip, self.my_c,
                                                name="sum_chips_" + n)

    def grad(self, name, dw):
        if name in ROW_SHARDED:
            dw = dw.reshape(N_CHIPS, dw.shape[0] // N_CHIPS, dw.shape[1])
        self.raw[name] = dw

    def finish(self):
        names = list(self.halves)
        return dict(zip(names, _run_carry(_join_carry([self.halves[n] for n in names]), name="join_halves")))


def _local_step(x, target, mod, small, comm):
    rows, d = x.shape
    cw = d // 2
    shift1, scale1, gate1, shift2, scale2, gate2 = mod
    lr, li, bbr, bbi = small["s5_disc"]
    b_in, c_out, b_out, c_in, mults = _s5_operands(lr, li, bbr, bbi, small["c_re"], small["c_im"])
    wt = comm.weight

    def riding(site, fn, *args, **kwargs):
        carry = comm.carry(site)
        if carry is None:
            return fn(*args, **kwargs)
        out, carried = fn(*args, carry=carry, **kwargs)
        comm.done(site, carried)
        return out

    u = _norm_mod(x, small["norm1_g"], scale1, shift1, name="norm1_fwd")
    proj = comm.mm_in(u)
    sl, cv = riding("conv_fwd", _conv_fwd, proj, small["w_dw"], small["b_dw"], small["ln_g"], small["ln_b"], cw=cw)
    y_conv = _mm(sl, wt("w_conv_out"), mode="nn", out_dtype=BF16, name="mm_conv_out")
    yg = riding("s5_fwd", _s5_fwd, proj, small["d_skip"], b_in, c_out, mults, col0=2 * cw // LANES)
    ya = riding("mm_glu_a", _mm, yg, wt("w_glu_a"), mode="nn", out_dtype=BF16, name="mm_glu_a")
    yb = riding("mm_glu_b", _mm, yg, wt("w_glu_b"), mode="nn", out_dtype=BF16, name="mm_glu_b")
    merged = _merge_fwd(proj, y_conv, ya, yb, cw=cw)
    mo = riding("mm_out", _mm, merged, wt("w_out"), mode="nn", out_dtype=BF16, name="mm_out")
    h1, z = _res_norm(x, mo, gate1, small["norm2_g"], scale2, shift2)
    f1 = riding("mm_ff1", _mm, z, wt("w_ff1"), mode="nn", out_dtype=BF16, name="mm_ff1")
    ff = _mm(f1, wt("w_ff2"), mode="nn", out_dtype=BF16, name="mm_ff2", a_fn=_relu2_bf16)
    dh2, dff, loss, d_final_g, d_gate2 = _final_fwd_bwd(h1, ff, gate2, small["final_g"], target)

    comm.grad("w_ff2", _mm(f1, dff, mode="tn", out_dtype=BF16, name="mm_dw_ff2", a_fn=_relu2_bf16))
    df1 = riding("mm_d_ff2", _mm, dff, wt("w_ff2"), mode="nt", out_dtype=BF16, name="mm_d_ff2", extra=f1,
                 epi=lambda acc, f: acc * (2.0 * jnp.maximum(f.astype(F32), 0.0)))
    comm.grad("w_ff1", riding("mm_dw_ff1", _mm, z, df1, mode="tn", out_dtype=BF16, name="mm_dw_ff1",
                              out_gathered=True))
    dz = riding("mm_d_ff1", _mm, df1, wt("w_ff1"), mode="nt", out_dtype=F32, name="mm_d_ff1")
    dh1, d_shift2, d_scale2, d_norm2_g, dmo, d_gate1 = _norm_mod_bwd(
        dz, h1, dh2, small["norm2_g"], scale2, gate1, mo, name="norm2_bwd")
    comm.grad("w_out", riding("mm_dw_out", _mm, merged, dmo, mode="tn", out_dtype=BF16, name="mm_dw_out"))
    dmerged = _mm(dmo, wt("w_out"), mode="nt", out_dtype=BF16, name="mm_d_out")
    dgc, dgs, dy_conv, dya, dyb = _merge_bwd(dmerged, proj, y_conv, ya, yb, cw=cw)
    comm.grad("w_glu_a", _mm(yg, dya, mode="tn", out_dtype=BF16, name="mm_dw_glu_a", out_gathered=True))
    comm.grad("w_glu_b", _mm(yg, dyb, mode="tn", out_dtype=BF16, name="mm_dw_glu_b", out_gathered=True))
    dyg_a = _mm(dya, wt("w_glu_a"), mode="nt", out_dtype=F32, name="mm_d_glu_a")
    dyg = _mm(dyb, wt("w_glu_b"), mode="nt", out_dtype=F32, name="mm_d_glu_b", extra=dyg_a,
              epi=lambda acc, e: acc + e)
    comm.grad("w_conv_out", _mm(sl, dy_conv, mode="tn", out_dtype=BF16, name="mm_dw_conv_out", out_gathered=True))
    dsl = _mm(dy_conv, wt("w_conv_out"), mode="nt", out_dtype=F32, name="mm_d_conv_out")
    dcv, d_ln_g, d_ln_b = _ln_bwd(dsl, cv, small["ln_g"], small["ln_b"])
    dvconv, d_w_dw, d_b_dw = _conv_bwd(dcv, proj, small["w_dw"], cw=cw)
    dvssm, d_d_skip, dbr, dbi, dcr, dci, dlr, dli = riding(
        "s5_bwd", _s5_bwd, proj, dyg, small["d_skip"], b_in, c_out, b_out, c_in, mults, col0=2 * cw // LANES)
    dproj = jnp.concatenate([dvconv, dvssm, dgc, dgs], axis=1)
    comm.grad("w_in", _mm(u, dproj, mode="tn", out_dtype=BF16, name="mm_dw_in", out_gathered=True))
    du = comm.mm_d_in(dproj)
    grad_x, d_shift1, d_scale1, d_norm1_g = _norm_mod_bwd(
        du, x, dh1, small["norm1_g"], scale1, None, None, name="norm1_bwd")

    sw = lambda m: jnp.swapaxes(m, 1, 2)
    gs = {
        "dmod": jnp.concatenate([d_shift1, d_scale1, d_gate1, d_shift2, d_scale2, d_gate2], axis=1),
        "norm1_g": d_norm1_g, "w_dw": d_w_dw, "b_dw": d_b_dw, "ln_g": d_ln_g, "ln_b": d_ln_b,
        "lam_re": dlr.reshape(-1, SSM_STATE), "lam_im": dli.reshape(-1, SSM_STATE),
        "bb_re": sw(_block_diag_extract(dbr, SSM_GROUP, SSM_STATE)),
        "bb_im": sw(_block_diag_extract(dbi, SSM_GROUP, SSM_STATE)),
        "c_re": sw(_block_diag_extract(dcr, SSM_STATE, SSM_GROUP)),
        "c_im": sw(_block_diag_extract(dci, SSM_STATE, SSM_GROUP)),
        "d_skip": d_d_skip, "norm2_g": d_norm2_g, "final_g": d_final_g,
    }
    return grad_x, gs, loss


WEIGHT_NAMES = ["w_ada", "b_ada", "norm1_g", "w_in", "w_dw", "b_dw", "ln_g", "ln_b", "w_conv_out", "a_re", "a_im",
                "log_dt", "b_re", "b_im", "c_re", "c_im", "d_skip", "w_glu_a", "w_glu_b", "w_out", "norm2_g",
                "w_ff1", "w_ff2", "final_g"]
BIG_NAMES = ["w_in", "w_conv_out", "w_glu_a", "w_glu_b", "w_out", "w_ff1", "w_ff2"]
ROW_SHARDED = ("w_out", "w_ff2")
PACK_TILE = SUBLANES * 1024


def _pack(arrays):
    flats = [a.reshape(-1) for a in arrays]
    offs = []
    total = 0
    for f in flats:
        offs.append(total)
        total += f.shape[0]
    pad = (-total) % PACK_TILE
    if pad:
        flats.append(jnp.zeros((pad,), F32))
    return jnp.concatenate(flats), offs


def _unpack(flat, offs, like):
    return [flat[o:o + a.size].reshape(a.shape) for o, a in zip(offs, like)]


def _gather_w_dw(w_shard):
    k, n = w_shard.shape
    padded = jnp.pad(w_shard, ((0, HALO - k), (0, 0)))
    allw = _gather_small(padded, name="gather_w_dw").reshape(N_CHIPS, 2, HALO, n)[:, 0, :k]
    return jnp.moveaxis(allw, 0, 1).reshape(k, N_CHIPS * n)


def kernel(x, c, w_ada, b_ada, norm1_g, w_in, w_dw, b_dw, ln_g, ln_b, w_conv_out, a_re, a_im, log_dt, b_re, b_im, c_re, c_im, d_skip, w_glu_a, w_glu_b, w_out, norm2_g, w_ff1, w_ff2, final_g, loss_target, m_w_ada, m_b_ada, m_norm1_g, m_w_in, m_w_dw, m_b_dw, m_ln_g, m_ln_b, m_w_conv_out, m_a_re, m_a_im, m_log_dt, m_b_re, m_b_im, m_c_re, m_c_im, m_d_skip, m_w_glu_a, m_w_glu_b, m_w_out, m_norm2_g, m_w_ff1, m_w_ff2, m_final_g, v_w_ada, v_b_ada, v_norm1_g, v_w_in, v_w_dw, v_b_dw, v_ln_g, v_ln_b, v_w_conv_out, v_a_re, v_a_im, v_log_dt, v_b_re, v_b_im, v_c_re, v_c_im, v_d_skip, v_w_glu_a, v_w_glu_b, v_w_out, v_norm2_g, v_w_ff1, v_w_ff2, v_final_g):
    given = dict(locals())
    w = {n: given[n] for n in WEIGHT_NAMES}
    m = {n: given["m_" + n] for n in WEIGHT_NAMES}
    v = {n: given["v_" + n] for n in WEIGHT_NAMES}
    d = x.shape[2]
    xi, yi, ci = _mesh_pos()
    chip = 2 * xi + yi
    dev = 4 * xi + 2 * yi + ci
    my_c = jnp.reshape(ci, (1,)).astype(jnp.int32)

    c_all = _gather_small(c.reshape(SUBLANES, d // SUBLANES), name="gather_c").reshape(N_DEV, d)
    nmod = w_ada.shape[2]
    b_cols = lax.dynamic_slice(b_ada, (0, chip * nmod), (1, nmod))
    mod_part = _ada_fwd(c_all, w_ada[0], b_cols)
    mod_all = _gather_small(mod_part, name="gather_mod").reshape(N_CHIPS, 2, N_DEV, nmod)[:, 0]
    mod_full = jnp.moveaxis(mod_all, 0, 1).reshape(N_DEV, N_CHIPS * nmod)
    mod_row = lax.dynamic_slice(mod_full, (dev, 0), (1, N_CHIPS * nmod))
    mod = [mod_row[:, i * d:(i + 1) * d] for i in range(6)]

    chip_arr = jnp.reshape(chip, (1,)).astype(jnp.int32)
    comm = _MeshComm({n: w[n][0] for n in BIG_NAMES}, (xi, yi, ci), chip_arr, my_c)

    disc_in = (a_re[0], a_im[0], log_dt[0], b_re[0], b_im[0])
    disc, disc_vjp = jax.vjp(_s5_discretise, *disc_in)
    small = {"norm1_g": norm1_g, "w_dw": _gather_w_dw(w_dw[0]), "b_dw": b_dw, "ln_g": ln_g, "ln_b": ln_b,
             "c_re": c_re[0], "c_im": c_im[0], "d_skip": d_skip, "norm2_g": norm2_g,
             "final_g": final_g[None, :], "s5_disc": disc}

    grad_x, gs, loss = _local_step(x[0], loss_target[0], mod, small, comm)
    grads = comm.finish()

    small_keys = ["norm1_g", "w_dw", "b_dw", "ln_g", "ln_b", "lam_re", "lam_im", "bb_re", "bb_im", "c_re", "c_im",
                  "d_skip", "norm2_g", "final_g"]
    items = [gs["dmod"], loss[:, 0:1]] + [gs[k] for k in small_keys]
    flat, offs = _pack(items)
    npk = flat.shape[0]
    everyone = _gather_small(flat.reshape(SUBLANES, npk // SUBLANES), name="gather_small_grads")
    total = _sum_leading(everyone.reshape(N_DEV, npk // 1024, 1024), name="sum_small_grads").reshape(npk)
    summed = dict(zip(["dmod", "loss"] + small_keys, _unpack(total, offs, items)))
    dmod_all = everyone.reshape(N_DEV, npk)[:, :6 * d]

    grads["w_ada"] = _ada_bwd(c_all, lax.dynamic_slice(dmod_all, (0, chip * nmod), (N_DEV, nmod)))
    grads["b_ada"] = _sum_leading(dmod_all.reshape(N_DEV, SUBLANES, 6 * d // SUBLANES),
                                  name="sum_b_ada").reshape(1, 6 * d)
    da_re, da_im, dlog_dt, db_re, db_im = disc_vjp(
        (summed["lam_re"], summed["lam_im"], summed["bb_re"], summed["bb_im"]))
    ndw = w_dw.shape[2]
    grads.update({
        "norm1_g": summed["norm1_g"], "w_dw": lax.dynamic_slice(summed["w_dw"], (0, chip * ndw), (CONV_KERNEL, ndw)),
        "b_dw": summed["b_dw"], "ln_g": summed["ln_g"], "ln_b": summed["ln_b"],
        "a_re": da_re, "a_im": da_im, "log_dt": dlog_dt, "b_re": db_re, "b_im": db_im,
        "c_re": summed["c_re"], "c_im": summed["c_im"], "d_skip": summed["d_skip"],
        "norm2_g": summed["norm2_g"], "final_g": summed["final_g"],
    })
    grads = {n: grads[n].reshape(w[n].shape) for n in WEIGHT_NAMES}

    delta, new_m, new_v = {}, {}, {}
    for n in ["w_ada"] + BIG_NAMES:
        shp = w[n].shape
        two_d = lambda a: a.reshape(shp[1], shp[2])
        dl, nm, nv = _adamw(two_d(w[n]), two_d(grads[n]), two_d(m[n]), two_d(v[n]), name="adamw_" + n)
        delta[n], new_m[n], new_v[n] = dl.reshape(shp), nm.reshape(shp), nv.reshape(shp)
    rest = [n for n in WEIGHT_NAMES if n not in delta]
    packs = []
    for src in (w, grads, m, v):
        flat, offs = _pack([src[n] for n in rest])
        packs.append(flat.reshape(-1, 1024))
    outs = _adamw(*packs, name="adamw_small")
    for dst, o in zip((delta, new_m, new_v), outs):
        for n, a in zip(rest, _unpack(o.reshape(-1), offs, [w[k] for k in rest])):
            dst[n] = a

    return (summed["loss"].reshape(()), grad_x[None], *[grads[n] for n in WEIGHT_NAMES],
            *[delta[n] for n in WEIGHT_NAMES], *[new_m[n] for n in WEIGHT_NAMES],
            *[new_v[n] for n in WEIGHT_NAMES])
```

```python
import functools
import math

import jax
import jax.numpy as jnp
from jax import lax
from jax.experimental import pallas as pl
from jax.experimental.pallas import tpu as pltpu

F32 = jnp.float32
BF16 = jnp.bfloat16
EPS = 1e-6
CONV_KERNEL = 31
SSM_GROUP = 16
SSM_STATE = 64
ADAM_LR = 0.001
ADAM_B1 = 0.9
ADAM_B2 = 0.999
ADAM_EPS = 1e-08
ADAM_WD = 0.01
ADAM_STEP = 10

N_CHIPS = 4
N_DEV = 8
VMEM_LIMIT_BYTES = 56 * 1024 * 1024
LANES = 128
SUBLANES = 8
HALO = 32
GROUPS_PER_BLOCK = LANES // SSM_GROUP
STATE_LANES = GROUPS_PER_BLOCK * SSM_STATE
MESH = pl.DeviceIdType.MESH


def _cparams(sem):
    return pltpu.CompilerParams(dimension_semantics=sem, vmem_limit_bytes=VMEM_LIMIT_BYTES)


def _pick(n, pref, mult=LANES):
    if n <= pref:
        return n
    best = None
    for d in range(mult, pref + 1, mult):
        if n % d == 0:
            best = d
    assert best is not None, (n, pref)
    return best


def _sigmoid(v):
    return 1.0 / (1.0 + jnp.exp(-v))


def _gelu_parts(v):
    k0 = math.sqrt(2.0 / math.pi)
    inner = k0 * (v + 0.044715 * v * v * v)
    t = jnp.tanh(inner)
    return k0, t


def _gelu(v):
    _, t = _gelu_parts(v)
    return 0.5 * v * (1.0 + t)


def _gelu_grad(v):
    k0, t = _gelu_parts(v)
    return 0.5 * (1.0 + t) + 0.5 * v * (1.0 - t * t) * k0 * (1.0 + 3.0 * 0.044715 * v * v)


def _relu2_bf16(a):
    t = jnp.maximum(a.astype(F32), 0.0)
    return (t * t).astype(BF16)


class _Carry:
    def __init__(self, inputs, out_shapes, aliases, sem_shapes, start, finish):
        self.inputs = list(inputs)
        self.out_shapes = list(out_shapes)
        self.aliases = dict(aliases)
        self.sem_shapes = list(sem_shapes)
        self.start = start
        self.finish = finish


def _call(body, *, grid, in_specs, out_specs, out_shape, scratch_shapes, semantics, name, args, carry=None,
          prefetch=(), aliases=None):
    n_in, n_out, n_scr, n_pf = len(in_specs), len(out_specs), len(scratch_shapes), len(prefetch)
    own_aliases = {n_pf + i: o for i, o in (aliases or {}).items()}
    if carry is None:
        gs = pltpu.PrefetchScalarGridSpec(
            num_scalar_prefetch=n_pf, grid=grid, in_specs=in_specs, out_specs=out_specs,
            scratch_shapes=scratch_shapes)
        outs = pl.pallas_call(
            body, grid_spec=gs, out_shape=out_shape, input_output_aliases=own_aliases,
            compiler_params=_cparams(semantics), name=name)(*prefetch, *args)
        return list(outs), []
    ci, co = len(carry.inputs), len(carry.out_shapes)

    def wrapped(*refs):
        pf, refs = refs[:n_pf], refs[n_pf:]
        ins, cins = refs[:n_in], refs[n_in:n_in + ci]
        p = n_in + ci
        outs, couts = refs[p:p + n_out], refs[p + n_out:p + n_out + co]
        p += n_out + co
        scr, csems = refs[p:p + n_scr], refs[p + n_scr:]
        first = pl.program_id(0) == 0
        last = pl.program_id(0) == grid[0] - 1
        for ax in range(1, len(grid)):
            first = jnp.logical_and(first, pl.program_id(ax) == 0)
            last = jnp.logical_and(last, pl.program_id(ax) == grid[ax] - 1)

        @pl.when(first)
        def _():
            carry.start(cins, couts, csems)

        body(*pf, *ins, *outs, *scr)

        @pl.when(last)
        def _():
            carry.finish(cins, couts, csems)

    any_spec = pl.BlockSpec(memory_space=pl.ANY)
    gs = pltpu.PrefetchScalarGridSpec(
        num_scalar_prefetch=n_pf, grid=grid, in_specs=list(in_specs) + [any_spec] * ci,
        out_specs=list(out_specs) + [any_spec] * co, scratch_shapes=list(scratch_shapes) + carry.sem_shapes)
    all_aliases = dict(own_aliases)
    all_aliases.update({n_pf + n_in + i: n_out + o for i, o in carry.aliases.items()})
    outs = pl.pallas_call(
        wrapped, grid_spec=gs, out_shape=list(out_shape) + carry.out_shapes, input_output_aliases=all_aliases,
        compiler_params=_cparams(("arbitrary",) * len(grid)), name=name)(*prefetch, *args, *carry.inputs)
    return list(outs[:n_out]), list(outs[n_out:])


def _run_carry(carry, *, name):
    ci = len(carry.inputs)

    def body(*refs):
        cins, couts, csems = refs[:ci], refs[ci:ci + len(carry.out_shapes)], refs[ci + len(carry.out_shapes):]
        carry.start(cins, couts, csems)
        carry.finish(cins, couts, csems)

    any_spec = pl.BlockSpec(memory_space=pl.ANY)
    outs = pl.pallas_call(
        body, in_specs=[any_spec] * ci, out_specs=[any_spec] * len(carry.out_shapes), out_shape=carry.out_shapes,
        scratch_shapes=carry.sem_shapes, input_output_aliases=carry.aliases, name=name)(*carry.inputs)
    return list(outs)


def _mm(a, b, *, mode, out_dtype, name, out_gathered=False, a_fn=None, epi=None, extra=None,
        bm_pref=1024, bn_pref=1024, bk_pref=2048, carry=None, a_slots=None):
    gathered = (b.ndim == 3)
    if mode == "nn":
        m, kdim = a.shape
        ns = b.shape[-1]
        n = ns * (N_CHIPS if gathered else 1)
        bm, bn, bk = _pick(m, bm_pref), _pick(ns, bn_pref), _pick(kdim, bk_pref)
        npb = ns // bn
        grid = (m // bm, n // bn, kdim // bk)
        a_spec = pl.BlockSpec((bm, bk), lambda i, j, k: (i, k))
        if gathered:
            b_spec = pl.BlockSpec((None, bk, bn), lambda i, j, k: (j // npb, k, j % npb))
        else:
            b_spec = pl.BlockSpec((bk, bn), lambda i, j, k: (k, j))
        o_spec = pl.BlockSpec((bm, bn), lambda i, j, k: (i, j))
        e_spec = pl.BlockSpec((bm, bn), lambda i, j, k: (i, j))
        out_shape = (m, n)
        acc_shape = (bm, bn)
        dims = (((1,), (0,)), ((), ()))
    elif mode == "nt":
        m = a.shape[0]
        kdim, ns = b.shape[-2], b.shape[-1]
        n = ns * (N_CHIPS if gathered else 1)
        assert a.shape[1] == n
        bm, bko, bnr = _pick(m, bm_pref), _pick(kdim, bn_pref), _pick(ns, bk_pref)
        npb = ns // bnr
        grid = (m // bm, kdim // bko, n // bnr)
        a_spec = pl.BlockSpec((bm, bnr), lambda i, j, k: (i, k))
        if gathered:
            b_spec = pl.BlockSpec((None, bko, bnr), lambda i, j, k: (k // npb, j, k % npb))
        else:
            b_spec = pl.BlockSpec((bko, bnr), lambda i, j, k: (j, k))
        o_spec = pl.BlockSpec((bm, bko), lambda i, j, k: (i, j))
        e_spec = pl.BlockSpec((bm, bko), lambda i, j, k: (i, j))
        if a_slots is not None:
            assert gathered and extra is None
            a_spec = pl.BlockSpec((bm, bnr), lambda i, j, k, s_ref: (i, s_ref[k // npb] * npb + k % npb))
            b_spec = pl.BlockSpec((None, bko, bnr), lambda i, j, k, s_ref: (k // npb, j, k % npb))
            o_spec = pl.BlockSpec((bm, bko), lambda i, j, k, s_ref: (i, j))
        out_shape = (m, kdim)
        acc_shape = (bm, bko)
        dims = (((1,), (1,)), ((), ()))
    else:
        m, kdim = a.shape
        n = b.shape[1]
        ns = n // N_CHIPS if out_gathered else n
        bmr, bko, bn = _pick(m, bk_pref), _pick(kdim, bm_pref), _pick(ns, bn_pref)
        npb = ns // bn
        grid = (kdim // bko, n // bn, m // bmr)
        a_spec = pl.BlockSpec((bmr, bko), lambda i, j, k: (k, i))
        b_spec = pl.BlockSpec((bmr, bn), lambda i, j, k: (k, j))
        if out_gathered:
            o_spec = pl.BlockSpec((None, bko, bn), lambda i, j, k: (j // npb, i, j % npb))
            out_shape = (N_CHIPS, kdim, ns)
        else:
            o_spec = pl.BlockSpec((bko, bn), lambda i, j, k: (i, j))
            out_shape = (kdim, n)
        e_spec = None
        acc_shape = (bko, bn)
        dims = (((0,), (0,)), ((), ()))
    nk = grid[2]

    def body(*refs):
        if a_slots is not None:
            refs = refs[1:]
        if extra is not None:
            a_ref, b_ref, e_ref, o_ref, acc = refs
        else:
            a_ref, b_ref, o_ref, acc = refs
            e_ref = None
        k = pl.program_id(2)
        av = a_ref[...]
        if a_fn is not None:
            av = a_fn(av)
        part = lax.dot_general(av, b_ref[...], dims, preferred_element_type=F32)

        def finish(r):
            if epi is not None:
                r = epi(r, e_ref[...])
            o_ref[...] = r.astype(o_ref.dtype)

        if nk == 1:
            finish(part)
            return

        @pl.when(k == 0)
        def _():
            acc[...] = part

        @pl.when(jnp.logical_and(k > 0, k < nk - 1))
        def _():
            acc[...] += part

        @pl.when(k == nk - 1)
        def _():
            finish(acc[...] + part)

    in_specs = [a_spec, b_spec]
    args = [a, b]
    if extra is not None:
        in_specs.append(e_spec)
        args.append(extra)
    outs, carried = _call(body, grid=grid, in_specs=in_specs, out_specs=[o_spec],
                          out_shape=[jax.ShapeDtypeStruct(out_shape, out_dtype)],
                          scratch_shapes=[pltpu.VMEM(acc_shape, F32)],
                          semantics=("parallel", "parallel", "arbitrary"), name=name, args=args, carry=carry,
                          prefetch=() if a_slots is None else (a_slots,))
    return outs[0] if carry is None else (outs[0], carried)


def _mm_slots(a, wbuf, slots, prev, *, name, carry=None):
    m, kdim = a.shape
    ns = wbuf.shape[2]
    bm, bn = _pick(m, 1024), _pick(ns, 1024)
    npb = ns // bn
    grid = (m // bm, slots.shape[0], npb)

    def body(s_ref, a_ref, b_ref, *rest):
        o_ref = rest[-1]
        o_ref[...] = _dot(a_ref[...], b_ref[...]).astype(o_ref.dtype)

    in_specs = [pl.BlockSpec((bm, kdim), lambda i, s, j, s_ref: (i, 0)),
                pl.BlockSpec((None, kdim, bn), lambda i, s, j, s_ref: (s, 0, j))]
    args = [a, wbuf]
    aliases = None
    if prev is not None:
        in_specs.append(pl.BlockSpec(memory_space=pl.ANY))
        args.append(prev)
        aliases = {2: 0}
    outs, carried = _call(
        body, grid=grid, in_specs=in_specs,
        out_specs=[pl.BlockSpec((bm, bn), lambda i, s, j, s_ref: (i, s_ref[s] * npb + j))],
        out_shape=[jax.ShapeDtypeStruct((m, N_CHIPS * ns), BF16)], scratch_shapes=[],
        semantics=("parallel", "arbitrary", "arbitrary"), name=name, args=args, carry=carry,
        prefetch=(slots,), aliases=aliases)
    return outs[0] if carry is None else (outs[0], carried)


def _row_tile(rows, cols, n_arrays):
    budget = VMEM_LIMIT_BYTES // 3
    cap = min(512, budget // (n_arrays * 2 * cols * 4))
    for t in range(cap - cap % SUBLANES, 0, -SUBLANES):
        if rows % t == 0:
            return t
    return rows


def _norm_mod(x, g, scale, shift, *, name):
    rows, d = x.shape
    tr = _row_tile(rows, d, 3)

    def body(x_ref, g_ref, sc_ref, sh_ref, o_ref):
        xv = x_ref[...]
        r = lax.rsqrt(jnp.mean(xv * xv, axis=-1, keepdims=True) + EPS)
        o_ref[...] = ((xv * r * g_ref[...]) * (1.0 + sc_ref[...]) + sh_ref[...]).astype(o_ref.dtype)

    row = pl.BlockSpec((tr, d), lambda i: (i, 0))
    vec = pl.BlockSpec((1, d), lambda i: (0, 0))
    return pl.pallas_call(
        body, grid=(rows // tr,), in_specs=[row, vec, vec, vec], out_specs=row,
        out_shape=jax.ShapeDtypeStruct((rows, d), BF16),
        compiler_params=_cparams(("parallel",)), name=name)(x, g, scale, shift)


CONV_CHUNK = 2 * SUBLANES


def _shifted_copies(buf, n):
    for r in range(1, SUBLANES):
        buf[r, pl.ds(0, n - SUBLANES), :] = buf[0, pl.ds(r, n - SUBLANES), :]


def _conv_fwd(proj, w_dw, b_dw, ln_g, ln_b, *, cw, carry=None):
    rows = proj.shape[0]
    tt = _pick(rows, 256, HALO)
    hb = tt // HALO

    def body(a_ref, g_ref, ha_ref, hg_ref, w_ref, b_ref, lg_ref, lb_ref, sl_ref, cv_ref, vs):
        i = pl.program_id(0)
        hv = ha_ref[...].astype(F32) * _sigmoid(hg_ref[...].astype(F32))
        vs[0, pl.ds(0, HALO), :] = jnp.where(i == 0, 0.0, hv)
        vs[0, pl.ds(HALO, tt), :] = a_ref[...].astype(F32) * _sigmoid(g_ref[...].astype(F32))
        _shifted_copies(vs, HALO + tt)

        def chunk(ci, carry):
            r0 = pl.multiple_of(ci * CONV_CHUNK, CONV_CHUNK)
            acc = jnp.broadcast_to(b_ref[...], (CONV_CHUNK, cw))
            for k in range(CONV_KERNEL):
                q, r = divmod(HALO - (CONV_KERNEL - 1) + k, SUBLANES)
                acc = acc + w_ref[pl.ds(k, 1), :] * vs[r, pl.ds(r0 + q * SUBLANES, CONV_CHUNK), :]
            cv_ref[pl.ds(r0, CONV_CHUNK), :] = acc
            return carry

        lax.fori_loop(0, tt // CONV_CHUNK, chunk, 0)
        acc = cv_ref[...]
        mu = jnp.mean(acc, axis=-1, keepdims=True)
        xc = acc - mu
        rstd = lax.rsqrt(jnp.mean(xc * xc, axis=-1, keepdims=True) + EPS)
        ln = xc * rstd * lg_ref[...] + lb_ref[...]
        sl_ref[...] = (ln * _sigmoid(ln)).astype(sl_ref.dtype)

    tile = lambda c: pl.BlockSpec((tt, cw), lambda i, c=c: (i, c))
    halo = lambda c: pl.BlockSpec((HALO, cw), lambda i, c=c: (jnp.maximum(i * hb - 1, 0), c))
    vec = pl.BlockSpec((1, cw), lambda i: (0, 0))
    outs, carried = _call(
        body, grid=(rows // tt,),
        in_specs=[tile(0), tile(1), halo(0), halo(1),
                  pl.BlockSpec((CONV_KERNEL, cw), lambda i: (0, 0)), vec, vec, vec],
        out_specs=[pl.BlockSpec((tt, cw), lambda i: (i, 0)), pl.BlockSpec((tt, cw), lambda i: (i, 0))],
        out_shape=[jax.ShapeDtypeStruct((rows, cw), BF16), jax.ShapeDtypeStruct((rows, cw), F32)],
        scratch_shapes=[pltpu.VMEM((SUBLANES, HALO + tt, cw), F32)],
        semantics=("parallel",), name="conv_fwd", args=[proj, proj, proj, proj, w_dw, b_dw, ln_g, ln_b],
        carry=carry)
    return outs if carry is None else (outs, carried)


def _ln_bwd(dsl, cv, ln_g, ln_b):
    rows, cw = cv.shape
    tr = _row_tile(rows, cw, 3)

    def body(d_ref, cv_ref, lg_ref, lb_ref, o_ref, dg_ref, db_ref):
        i = pl.program_id(0)

        @pl.when(i == 0)
        def _():
            dg_ref[...] = jnp.zeros_like(dg_ref)
            db_ref[...] = jnp.zeros_like(db_ref)

        x = cv_ref[...]
        mu = jnp.mean(x, axis=-1, keepdims=True)
        xc = x - mu
        rstd = lax.rsqrt(jnp.mean(xc * xc, axis=-1, keepdims=True) + EPS)
        xh = xc * rstd
        ln = xh * lg_ref[...] + lb_ref[...]
        s = _sigmoid(ln)
        dln = d_ref[...].astype(F32) * (s * (1.0 + ln * (1.0 - s)))
        dg_ref[...] += jnp.sum(dln * xh, axis=0, keepdims=True)
        db_ref[...] += jnp.sum(dln, axis=0, keepdims=True)
        dxh = dln * lg_ref[...]
        m1 = jnp.mean(dxh, axis=-1, keepdims=True)
        m2 = jnp.mean(dxh * xh, axis=-1, keepdims=True)
        o_ref[...] = rstd * (dxh - m1 - xh * m2)

    row = pl.BlockSpec((tr, cw), lambda i: (i, 0))
    vec = pl.BlockSpec((1, cw), lambda i: (0, 0))
    return pl.pallas_call(
        body, grid=(rows // tr,), in_specs=[row, row, vec, vec], out_specs=[row, vec, vec],
        out_shape=[jax.ShapeDtypeStruct((rows, cw), F32), jax.ShapeDtypeStruct((1, cw), F32),
                   jax.ShapeDtypeStruct((1, cw), F32)],
        compiler_params=_cparams(("arbitrary",)), name="ln_bwd")(dsl, cv, ln_g, ln_b)


def _conv_bwd(dcv, proj, w_dw, *, cw):
    rows = proj.shape[0]
    tt = _pick(rows, 256, HALO)
    hb = tt // HALO
    nt = rows // tt
    taps = CONV_KERNEL

    def body(d_ref, dn_ref, a_ref, g_ref, ha_ref, hg_ref, w_ref, o_ref, dw_ref, db_ref, vs, ds):
        i = pl.program_id(0)

        @pl.when(i == 0)
        def _():
            dw_ref[...] = jnp.zeros_like(dw_ref)
            db_ref[...] = jnp.zeros_like(db_ref)

        hv = ha_ref[...].astype(F32) * _sigmoid(hg_ref[...].astype(F32))
        vs[0, pl.ds(0, HALO), :] = jnp.where(i == 0, 0.0, hv)
        vs[0, pl.ds(HALO, tt), :] = a_ref[...].astype(F32) * _sigmoid(g_ref[...].astype(F32))
        _shifted_copies(vs, HALO + tt)
        ds[0, pl.ds(0, tt), :] = d_ref[...]
        ds[0, pl.ds(tt, HALO), :] = jnp.where(i == nt - 1, 0.0, dn_ref[...])
        _shifted_copies(ds, tt + HALO)
        db_ref[...] += jnp.sum(d_ref[...], axis=0, keepdims=True)
        for k in range(taps):
            q, r = divmod(HALO - (taps - 1) + k, SUBLANES)
            dw_ref[pl.ds(k, 1), :] += jnp.sum(d_ref[...] * vs[r, pl.ds(q * SUBLANES, tt), :], axis=0, keepdims=True)

        def chunk(ci, carry):
            r0 = pl.multiple_of(ci * CONV_CHUNK, CONV_CHUNK)
            dv = jnp.zeros((CONV_CHUNK, cw), F32)
            for k in range(taps):
                q, r = divmod(taps - 1 - k, SUBLANES)
                dv = dv + w_ref[pl.ds(k, 1), :] * ds[r, pl.ds(r0 + q * SUBLANES, CONV_CHUNK), :]
            av = a_ref[pl.ds(r0, CONV_CHUNK), :].astype(F32)
            sg = _sigmoid(g_ref[pl.ds(r0, CONV_CHUNK), :].astype(F32))
            o_ref[pl.ds(r0, CONV_CHUNK), pl.ds(0, cw)] = (dv * sg).astype(o_ref.dtype)
            o_ref[pl.ds(r0, CONV_CHUNK), pl.ds(cw, cw)] = (dv * av * sg * (1.0 - sg)).astype(o_ref.dtype)
            return carry

        lax.fori_loop(0, tt // CONV_CHUNK, chunk, 0)

    tile = lambda c: pl.BlockSpec((tt, cw), lambda i, c=c: (i, c))
    halo = lambda c: pl.BlockSpec((HALO, cw), lambda i, c=c: (jnp.maximum(i * hb - 1, 0), c))
    nxt = pl.BlockSpec((HALO, cw), lambda i: (jnp.minimum((i + 1) * hb, nt * hb - 1), 0))
    return pl.pallas_call(
        body, grid=(nt,),
        in_specs=[pl.BlockSpec((tt, cw), lambda i: (i, 0)), nxt, tile(0), tile(1), halo(0), halo(1),
                  pl.BlockSpec((taps, cw), lambda i: (0, 0))],
        out_specs=[pl.BlockSpec((tt, 2 * cw), lambda i: (i, 0)),
                   pl.BlockSpec((taps, cw), lambda i: (0, 0)), pl.BlockSpec((1, cw), lambda i: (0, 0))],
        out_shape=[jax.ShapeDtypeStruct((rows, 2 * cw), BF16), jax.ShapeDtypeStruct((taps, cw), F32),
                   jax.ShapeDtypeStruct((1, cw), F32)],
        scratch_shapes=[pltpu.VMEM((SUBLANES, HALO + tt, cw), F32), pltpu.VMEM((SUBLANES, tt + HALO, cw), F32)],
        compiler_params=_cparams(("arbitrary",)), name="conv_bwd")(dcv, dcv, proj, proj, proj, proj, w_dw)


def _merge_fwd(proj, y_conv, ya, yb, *, cw):
    rows = proj.shape[0]
    tr = _row_tile(rows, cw, 4)

    def body(gc_ref, gs_ref, yc_ref, ya_ref, yb_ref, o_ref):
        ys = ya_ref[...].astype(F32) * _sigmoid(yb_ref[...].astype(F32))
        o_ref[...] = (_sigmoid(gc_ref[...].astype(F32)) * yc_ref[...].astype(F32)
                      + _sigmoid(gs_ref[...].astype(F32)) * ys).astype(o_ref.dtype)

    blk = lambda off: pl.BlockSpec((tr, cw), lambda i, h, off=off: (i, off + h))
    return pl.pallas_call(
        body, grid=(rows // tr, 2), in_specs=[blk(3), blk(5), blk(0), blk(0), blk(0)], out_specs=blk(0),
        out_shape=jax.ShapeDtypeStruct((rows, 2 * cw), BF16),
        compiler_params=_cparams(("parallel", "parallel")), name="merge_fwd")(proj, proj, y_conv, ya, yb)


def _merge_bwd(dmerged, proj, y_conv, ya, yb, *, cw):
    rows = proj.shape[0]
    tr = _row_tile(rows, cw, 6)

    def body(d_ref, gc_ref, gs_ref, yc_ref, ya_ref, yb_ref, dgc_ref, dgs_ref, dyc_ref, dya_ref, dyb_ref):
        d = d_ref[...].astype(F32)
        sc = _sigmoid(gc_ref[...].astype(F32))
        ss = _sigmoid(gs_ref[...].astype(F32))
        sb = _sigmoid(yb_ref[...].astype(F32))
        yav = ya_ref[...].astype(F32)
        dgc_ref[...] = (d * yc_ref[...].astype(F32) * sc * (1.0 - sc)).astype(dgc_ref.dtype)
        dgs_ref[...] = (d * (yav * sb) * ss * (1.0 - ss)).astype(dgs_ref.dtype)
        dyc_ref[...] = (d * sc).astype(dyc_ref.dtype)
        dys = d * ss
        dya_ref[...] = (dys * sb).astype(dya_ref.dtype)
        dyb_ref[...] = (dys * yav * sb * (1.0 - sb)).astype(dyb_ref.dtype)

    blk = lambda off: pl.BlockSpec((tr, cw), lambda i, h, off=off: (i, off + h))
    o2 = jax.ShapeDtypeStruct((rows, 2 * cw), BF16)
    return pl.pallas_call(
        body, grid=(rows // tr, 2),
        in_specs=[blk(0), blk(3), blk(5), blk(0), blk(0), blk(0)],
        out_specs=[blk(0), blk(0), blk(0), blk(0), blk(0)],
        out_shape=[o2, o2, o2, o2, o2],
        compiler_params=_cparams(("parallel", "parallel")), name="merge_bwd")(dmerged, proj, proj, y_conv, ya, yb)


def _res_norm(x, mo, gate, g, scale, shift):
    rows, d = x.shape
    tr = _row_tile(rows, d, 4)

    def body(x_ref, mo_ref, gt_ref, g_ref, sc_ref, sh_ref, h_ref, z_ref):
        h = x_ref[...] + gt_ref[...] * mo_ref[...].astype(F32)
        h_ref[...] = h
        r = lax.rsqrt(jnp.mean(h * h, axis=-1, keepdims=True) + EPS)
        z_ref[...] = ((h * r * g_ref[...]) * (1.0 + sc_ref[...]) + sh_ref[...]).astype(z_ref.dtype)

    row = pl.BlockSpec((tr, d), lambda i: (i, 0))
    vec = pl.BlockSpec((1, d), lambda i: (0, 0))
    return pl.pallas_call(
        body, grid=(rows // tr,), in_specs=[row, row, vec, vec, vec, vec], out_specs=[row, row],
        out_shape=[jax.ShapeDtypeStruct((rows, d), F32), jax.ShapeDtypeStruct((rows, d), BF16)],
        compiler_params=_cparams(("parallel",)), name="res_norm")(x, mo, gate, g, scale, shift)


def _final_fwd_bwd(h1, ff, gate2, final_g, target):
    rows, d = h1.shape
    tr = _row_tile(rows, d, 5)

    def body(h_ref, ff_ref, gt_ref, fg_ref, t_ref, dh_ref, dff_ref, loss_ref, dfg_ref, dgt_ref):
        i = pl.program_id(0)

        @pl.when(i == 0)
        def _():
            loss_ref[...] = jnp.zeros_like(loss_ref)
            dfg_ref[...] = jnp.zeros_like(dfg_ref)
            dgt_ref[...] = jnp.zeros_like(dgt_ref)

        ffv = ff_ref[...].astype(F32)
        h2 = h_ref[...] + gt_ref[...] * ffv
        r = lax.rsqrt(jnp.mean(h2 * h2, axis=-1, keepdims=True) + EPS)
        y = h2 * r
        e = y * fg_ref[...] - t_ref[...]
        loss_ref[...] += 0.5 * jnp.sum(jnp.mean(e * e, axis=-1, keepdims=True))
        dout = e * (1.0 / d)
        dfg_ref[...] += jnp.sum(dout * y, axis=0, keepdims=True)
        dy = dout * fg_ref[...]
        dh2 = r * (dy - y * jnp.mean(dy * y, axis=-1, keepdims=True))
        dh_ref[...] = dh2
        dgt_ref[...] += jnp.sum(dh2 * ffv, axis=0, keepdims=True)
        dff_ref[...] = (dh2 * gt_ref[...]).astype(dff_ref.dtype)

    row = pl.BlockSpec((tr, d), lambda i: (i, 0))
    vec = pl.BlockSpec((1, d), lambda i: (0, 0))
    return pl.pallas_call(
        body, grid=(rows // tr,), in_specs=[row, row, vec, vec, row],
        out_specs=[row, row, pl.BlockSpec((1, LANES), lambda i: (0, 0)), vec, vec],
        out_shape=[jax.ShapeDtypeStruct((rows, d), F32), jax.ShapeDtypeStruct((rows, d), BF16),
                   jax.ShapeDtypeStruct((1, LANES), F32), jax.ShapeDtypeStruct((1, d), F32),
                   jax.ShapeDtypeStruct((1, d), F32)],
        compiler_params=_cparams(("arbitrary",)), name="final_fwd_bwd")(h1, ff, gate2, final_g, target)


def _norm_mod_bwd(dz, hin, dres, g, scale, gate, mo, *, name):
    rows, d = hin.shape
    with_gate = gate is not None
    tr = _row_tile(rows, d, 6)

    def body(*refs):
        if with_gate:
            (dz_ref, h_ref, dr_ref, g_ref, sc_ref, gt_ref, mo_ref,
             dh_ref, dsh_ref, dsc_ref, dg_ref, dmo_ref, dgt_ref) = refs
        else:
            dz_ref, h_ref, dr_ref, g_ref, sc_ref, dh_ref, dsh_ref, dsc_ref, dg_ref = refs
        i = pl.program_id(0)

        @pl.when(i == 0)
        def _():
            dsh_ref[...] = jnp.zeros_like(dsh_ref)
            dsc_ref[...] = jnp.zeros_like(dsc_ref)
            dg_ref[...] = jnp.zeros_like(dg_ref)
            if with_gate:
                dgt_ref[...] = jnp.zeros_like(dgt_ref)

        dzv = dz_ref[...].astype(F32)
        h = h_ref[...]
        r = lax.rsqrt(jnp.mean(h * h, axis=-1, keepdims=True) + EPS)
        y = h * r
        dsh_ref[...] += jnp.sum(dzv, axis=0, keepdims=True)
        dsc_ref[...] += jnp.sum(dzv * (y * g_ref[...]), axis=0, keepdims=True)
        dn = dzv * (1.0 + sc_ref[...])
        dg_ref[...] += jnp.sum(dn * y, axis=0, keepdims=True)
        dy = dn * g_ref[...]
        dh = dr_ref[...] + r * (dy - y * jnp.mean(dy * y, axis=-1, keepdims=True))
        dh_ref[...] = dh
        if with_gate:
            dmo_ref[...] = (dh * gt_ref[...]).astype(dmo_ref.dtype)
            dgt_ref[...] += jnp.sum(dh * mo_ref[...].astype(F32), axis=0, keepdims=True)

    row = pl.BlockSpec((tr, d), lambda i: (i, 0))
    vec = pl.BlockSpec((1, d), lambda i: (0, 0))
    vshape = jax.ShapeDtypeStruct((1, d), F32)
    in_specs = [row, row, row, vec, vec]
    args = [dz, hin, dres, g, scale]
    out_specs = [row, vec, vec, vec]
    out_shape = [jax.ShapeDtypeStruct((rows, d), F32), vshape, vshape, vshape]
    if with_gate:
        in_specs += [vec, row]
        args += [gate, mo]
        out_specs += [row, vec]
        out_shape += [jax.ShapeDtypeStruct((rows, d), BF16), vshape]
    return pl.pallas_call(
        body, grid=(rows // tr,), in_specs=in_specs, out_specs=out_specs, out_shape=out_shape,
        compiler_params=_cparams(("arbitrary",)), name=name)(*args)


def _s5_discretise(a_re, a_im, log_dt, b_re, b_im):
    dt = jnp.exp(log_dt)[:, None]
    er = jnp.exp(a_re * dt)
    lr = er * jnp.cos(a_im * dt)
    li = er * jnp.sin(a_im * dt)
    den = a_re * a_re + a_im * a_im
    cr = ((lr - 1.0) * a_re + li * a_im) / den
    ci = (li * a_re - (lr - 1.0) * a_im) / den
    bbr = cr[..., None] * b_re - ci[..., None] * b_im
    bbi = cr[..., None] * b_im + ci[..., None] * b_re
    return lr, li, bbr, bbi


def _block_diag(w):
    g, r, c = w.shape
    nb = g // GROUPS_PER_BLOCK
    eye = jnp.eye(GROUPS_PER_BLOCK, dtype=w.dtype)
    w5 = w.reshape(nb, GROUPS_PER_BLOCK, r, 1, c) * eye[None, :, None, :, None]
    return w5.reshape(nb, GROUPS_PER_BLOCK * r, GROUPS_PER_BLOCK * c)


def _block_diag_extract(m, r, c):
    nb = m.shape[0]
    m5 = m.reshape(nb, GROUPS_PER_BLOCK, r, GROUPS_PER_BLOCK, c)
    idx = jnp.arange(GROUPS_PER_BLOCK)
    d = m5[:, idx, :, idx, :]
    return jnp.moveaxis(d, 0, 1).reshape(nb * GROUPS_PER_BLOCK, r, c)


def _scan_multipliers(lr, li):
    power = jnp.arange(1, SUBLANES + 1, dtype=F32)[None, :, None]
    er = jnp.exp(power * lr)
    pr = er * jnp.cos(power * li)
    pi = er * jnp.sin(power * li)
    rows = jnp.arange(SUBLANES)[None, :, None]
    fr, fi, rr, ri = [], [], [], []
    for s in (1, 2, 4):
        mf = (rows >= s).astype(F32)
        mr = (rows <= SUBLANES - 1 - s).astype(F32)
        fr.append(mf * pr[:, s - 1:s, :])
        fi.append(mf * pi[:, s - 1:s, :])
        rr.append(mr * pr[:, s - 1:s, :])
        ri.append(mr * pi[:, s - 1:s, :])
    fr.append(pr)
    fi.append(pi)
    rr.append(pr[:, ::-1, :])
    ri.append(pi[:, ::-1, :])
    st = lambda xs: jnp.stack(xs, axis=1)
    return st(fr), st(fi), st(rr), st(ri)


def _scan_rows(sre, sim, mul_r, mul_i, n_groups, reverse):
    sgn = -1.0 if reverse else 1.0
    lanes = sre.shape[1]

    def step(k, carry):
        cr, ci = carry
        kk = (n_groups - 1 - k) if reverse else k
        r0 = pl.multiple_of(kk * SUBLANES, SUBLANES)
        xr = sre[pl.ds(r0, SUBLANES), :]
        xi = sim[pl.ds(r0, SUBLANES), :]
        for lvl, s in enumerate((1, 2, 4)):
            sh = (SUBLANES - s) if reverse else s
            nr = pltpu.roll(xr, sh, 0)
            ni = pltpu.roll(xi, sh, 0)
            mr = mul_r[lvl]
            mi = mul_i[lvl] * sgn
            xr, xi = xr + mr * nr - mi * ni, xi + mr * ni + mi * nr
        mr = mul_r[3]
        mi = mul_i[3] * sgn
        xr, xi = xr + mr * cr - mi * ci, xi + mr * ci + mi * cr
        sre[pl.ds(r0, SUBLANES), :] = xr
        sim[pl.ds(r0, SUBLANES), :] = xi
        edge = 0 if reverse else SUBLANES - 1
        ncr = jnp.broadcast_to(xr[edge:edge + 1, :], (SUBLANES, lanes))
        nci = jnp.broadcast_to(xi[edge:edge + 1, :], (SUBLANES, lanes))
        return ncr, nci

    zero = jnp.zeros((SUBLANES, lanes), F32)
    lax.fori_loop(0, n_groups, step, (zero, zero))


def _dot(a, b):
    return jnp.dot(a, b, preferred_element_type=F32)


def _dotf(a, b):
    return _dot(a.astype(BF16), b)


def _s5_operands(lr, li, bbr, bbi, c_re, c_im):
    g = lr.shape[0]
    nb = g // GROUPS_PER_BLOCK
    tb = lambda w: jnp.swapaxes(w, 1, 2)
    b_in = [_block_diag(tb(bbr)), _block_diag(tb(bbi))]
    c_out = [_block_diag(tb(c_re)), _block_diag(tb(c_im))]
    b_out = [_block_diag(bbr), _block_diag(bbi)]
    c_in = [_block_diag(c_re), _block_diag(c_im)]
    lam_r = lr.reshape(nb, 1, STATE_LANES)
    lam_i = li.reshape(nb, 1, STATE_LANES)
    mults = _scan_multipliers(lam_r, lam_i)
    cast = lambda ws: [w.astype(BF16) for w in ws]
    return cast(b_in), cast(c_out), cast(b_out), cast(c_in), mults


def _s5_fwd(proj, d_skip, b_in, c_out, mults, *, col0, carry=None):
    rows = proj.shape[0]
    nb = b_in[0].shape[0]
    tm = _pick(rows, 512, SUBLANES)
    n_tiles = rows // tm
    s_l = STATE_LANES

    def body(u_ref, dk_ref, br, bi, cr, ci, fr_ref, fi_ref, o_ref, sr_ref, si_ref, sre, sim):
        for t in range(n_tiles):
            rs = pl.ds(t * tm, tm)
            ub = u_ref[rs, :]
            sre[rs, :] = _dot(ub, br[...])
            sim[rs, :] = _dot(ub, bi[...])
        _scan_rows(sre, sim, fr_ref, fi_ref, rows // SUBLANES, False)
        for t in range(n_tiles):
            rs = pl.ds(t * tm, tm)
            srb = sre[rs, :].astype(BF16)
            sib = sim[rs, :].astype(BF16)
            sr_ref[rs, :] = srb
            si_ref[rs, :] = sib
            y0 = _dot(srb, cr[...]) - _dot(sib, ci[...])
            y1 = y0 + dk_ref[...] * u_ref[rs, :].astype(F32)
            o_ref[rs, :] = _gelu(y1).astype(o_ref.dtype)

    mat_in = pl.BlockSpec((None, LANES, s_l), lambda g: (g, 0, 0))
    mat_out = pl.BlockSpec((None, s_l, LANES), lambda g: (g, 0, 0))
    mul = pl.BlockSpec((None, 4, SUBLANES, s_l), lambda g: (g, 0, 0, 0))
    state = pl.BlockSpec((rows, s_l), lambda g: (0, g))
    outs, carried = _call(
        body, grid=(nb,),
        in_specs=[pl.BlockSpec((rows, LANES), lambda g: (0, col0 + g)), pl.BlockSpec((1, LANES), lambda g: (0, g))]
        + [mat_in] * 2 + [mat_out] * 2 + [mul] * 2,
        out_specs=[pl.BlockSpec((rows, LANES), lambda g: (0, g)), state, state],
        out_shape=[jax.ShapeDtypeStruct((rows, nb * LANES), BF16), jax.ShapeDtypeStruct((rows, nb * s_l), BF16),
                   jax.ShapeDtypeStruct((rows, nb * s_l), BF16)],
        scratch_shapes=[pltpu.VMEM((rows, s_l), F32), pltpu.VMEM((rows, s_l), F32)],
        semantics=("parallel",), name="s5_fwd", args=[proj, d_skip, *b_in, *c_out, mults[0], mults[1]], carry=carry)
    return outs if carry is None else (outs, carried)


def _s5_bwd(proj, dyg, d_skip, states, c_out, b_out, c_in, mults, *, col0, carry=None):
    rows = proj.shape[0]
    nb = c_out[0].shape[0]
    tm = _pick(rows, 512, SUBLANES)
    n_tiles = rows // tm
    s_l = STATE_LANES
    n_groups = rows // SUBLANES
    tn = (((0,), (0,)), ((), ()))

    def body(u_ref, dy_ref, dk_ref, sr_ref, si_ref, cr, ci, bor, boi, cir, cii, rr_ref, ri_ref,
             du_ref, ddk_ref, dbr_ref, dbi_ref, dcr_ref, dci_ref, dlr_ref, dli_ref,
             gre, gim, dy1):
        ddk = jnp.zeros((1, LANES), F32)
        dcr = jnp.zeros((s_l, LANES), F32)
        dci = jnp.zeros((s_l, LANES), F32)
        for t in range(n_tiles):
            rs = pl.ds(t * tm, tm)
            srb = sr_ref[rs, :]
            sib = si_ref[rs, :]
            uf = u_ref[rs, :].astype(F32)
            y0 = _dot(srb, cr[...]) - _dot(sib, ci[...])
            y1 = y0 + dk_ref[...] * uf
            d1 = dy_ref[rs, :].astype(F32) * _gelu_grad(y1)
            dy1[rs, :] = d1
            ddk = ddk + jnp.sum(d1 * uf, axis=0, keepdims=True)
            d1b = d1.astype(BF16)
            dcr = dcr + lax.dot_general(srb, d1b, tn, preferred_element_type=F32)
            dci = dci - lax.dot_general(sib, d1b, tn, preferred_element_type=F32)
            gre[rs, :] = _dot(d1b, cir[...])
            gim[rs, :] = -_dot(d1b, cii[...])
        ddk_ref[...] = ddk
        dcr_ref[...] = dcr
        dci_ref[...] = dci

        last_row = lax.broadcasted_iota(jnp.int32, (SUBLANES, s_l), 0) == SUBLANES - 1

        def group(r0, s_r, s_i, carry):
            cr_, ci_, ar, ai = carry
            xr = gre[pl.ds(r0, SUBLANES), :]
            xi = gim[pl.ds(r0, SUBLANES), :]
            for lvl, s in enumerate((1, 2, 4)):
                nr = pltpu.roll(xr, SUBLANES - s, 0)
                ni = pltpu.roll(xi, SUBLANES - s, 0)
                mr = rr_ref[lvl]
                mi = ri_ref[lvl]
                xr, xi = xr + mr * nr + mi * ni, xi + mr * ni - mi * nr
            mr = rr_ref[3]
            mi = ri_ref[3]
            xr, xi = xr + mr * cr_ + mi * ci_, xi + mr * ci_ - mi * cr_
            gre[pl.ds(r0, SUBLANES), :] = xr
            gim[pl.ds(r0, SUBLANES), :] = xi
            nxt_r = jnp.where(last_row, cr_, pltpu.roll(xr, SUBLANES - 1, 0))
            nxt_i = jnp.where(last_row, ci_, pltpu.roll(xi, SUBLANES - 1, 0))
            ncr = jnp.broadcast_to(xr[0:1, :], (SUBLANES, s_l))
            nci = jnp.broadcast_to(xi[0:1, :], (SUBLANES, s_l))
            return ncr, nci, ar + nxt_r * s_r + nxt_i * s_i, ai + nxt_i * s_r - nxt_r * s_i

        def rev_step(k, carry):
            r0 = pl.multiple_of((n_groups // 2 - 1 - k) * 2 * SUBLANES, 2 * SUBLANES)
            s_r = sr_ref[pl.ds(r0, 2 * SUBLANES), :].astype(F32)
            s_i = si_ref[pl.ds(r0, 2 * SUBLANES), :].astype(F32)
            carry = group(r0 + SUBLANES, s_r[SUBLANES:], s_i[SUBLANES:], carry)
            return group(r0, s_r[:SUBLANES], s_i[:SUBLANES], carry)

        zero = jnp.zeros((SUBLANES, s_l), F32)
        _, _, ar, ai = lax.fori_loop(0, n_groups // 2, rev_step, (zero, zero, zero, zero))
        dlr_ref[...] = jnp.sum(ar, axis=0, keepdims=True)
        dli_ref[...] = jnp.sum(ai, axis=0, keepdims=True)

        dbr = jnp.zeros((LANES, s_l), F32)
        dbi = jnp.zeros((LANES, s_l), F32)
        for t in range(n_tiles):
            rs = pl.ds(t * tm, tm)
            gr = gre[rs, :]
            gi = gim[rs, :]
            grb = gr.astype(BF16)
            gib = gi.astype(BF16)
            du = _dot(grb, bor[...]) + _dot(gib, boi[...]) + dy1[rs, :] * dk_ref[...]
            du_ref[rs, :] = du.astype(du_ref.dtype)
            ub = u_ref[rs, :]
            dbr = dbr + lax.dot_general(ub, grb, tn, preferred_element_type=F32)
            dbi = dbi + lax.dot_general(ub, gib, tn, preferred_element_type=F32)
        dbr_ref[...] = dbr
        dbi_ref[...] = dbi

    mat_in = pl.BlockSpec((None, LANES, s_l), lambda g: (g, 0, 0))
    mat_out = pl.BlockSpec((None, s_l, LANES), lambda g: (g, 0, 0))
    mul = pl.BlockSpec((None, 4, SUBLANES, s_l), lambda g: (g, 0, 0, 0))
    lam = pl.BlockSpec((None, 1, s_l), lambda g: (g, 0, 0))
    col = pl.BlockSpec((rows, LANES), lambda g: (0, g))
    vec = pl.BlockSpec((1, LANES), lambda g: (0, g))
    state = pl.BlockSpec((rows, s_l), lambda g: (0, g))
    outs, carried = _call(
        body, grid=(nb,),
        in_specs=[pl.BlockSpec((rows, LANES), lambda g: (0, col0 + g)), col, vec]
        + [state] * 2 + [mat_out] * 2 + [mat_out] * 2 + [mat_in] * 2 + [mul] * 2,
        out_specs=[col, vec, mat_in, mat_in, mat_out, mat_out, lam, lam],
        out_shape=[jax.ShapeDtypeStruct((rows, nb * LANES), BF16), jax.ShapeDtypeStruct((1, nb * LANES), F32),
                   jax.ShapeDtypeStruct((nb, LANES, s_l), F32), jax.ShapeDtypeStruct((nb, LANES, s_l), F32),
                   jax.ShapeDtypeStruct((nb, s_l, LANES), F32), jax.ShapeDtypeStruct((nb, s_l, LANES), F32),
                   jax.ShapeDtypeStruct((nb, 1, s_l), F32), jax.ShapeDtypeStruct((nb, 1, s_l), F32)],
        scratch_shapes=[pltpu.VMEM((rows, s_l), F32)] * 2 + [pltpu.VMEM((rows, LANES), F32)],
        semantics=("parallel",), name="s5_bwd",
        args=[proj, dyg, d_skip, *states, *c_out, *b_out, *c_in, mults[2], mults[3]], carry=carry)
    return outs if carry is None else (outs, carried)


def _silu(v):
    return v * _sigmoid(v)


def _ada_fwd(c_all, w_shard, b_cols):
    d, n = w_shard.shape
    bn = _pick(n, 512)

    def body(c_ref, w_ref, b_ref, o_ref):
        ca = _silu(c_ref[...]).astype(BF16)
        o_ref[...] = _dot(ca, w_ref[...].astype(BF16)) + b_ref[...]

    return pl.pallas_call(
        body, grid=(n // bn,),
        in_specs=[pl.BlockSpec((N_DEV, d), lambda j: (0, 0)), pl.BlockSpec((d, bn), lambda j: (0, j)),
                  pl.BlockSpec((1, bn), lambda j: (0, j))],
        out_specs=pl.BlockSpec((N_DEV, bn), lambda j: (0, j)),
        out_shape=jax.ShapeDtypeStruct((N_DEV, n), F32),
        compiler_params=_cparams(("parallel",)), name="ada_fwd")(c_all, w_shard, b_cols)


def _ada_bwd(c_all, dmod_cols):
    d = c_all.shape[1]
    n = dmod_cols.shape[1]
    bn = _pick(n, 512)

    def body(c_ref, g_ref, o_ref):
        ca = _silu(c_ref[...]).astype(BF16)
        o_ref[...] = lax.dot_general(ca, g_ref[...].astype(BF16), (((0,), (0,)), ((), ())),
                                     preferred_element_type=F32)

    return pl.pallas_call(
        body, grid=(n // bn,),
        in_specs=[pl.BlockSpec((N_DEV, d), lambda j: (0, 0)), pl.BlockSpec((N_DEV, bn), lambda j: (0, j))],
        out_specs=pl.BlockSpec((d, bn), lambda j: (0, j)),
        out_shape=jax.ShapeDtypeStruct((d, n), F32),
        compiler_params=_cparams(("parallel",)), name="ada_bwd")(c_all, dmod_cols)


def _cast_bf16(w, *, name):
    rows, cols = w.shape
    tr = _row_tile(rows, cols, 2)

    def body(w_ref, o_ref):
        o_ref[...] = w_ref[...].astype(BF16)

    row = pl.BlockSpec((tr, cols), lambda i: (i, 0))
    return pl.pallas_call(
        body, grid=(rows // tr,), in_specs=[row], out_specs=row,
        out_shape=jax.ShapeDtypeStruct((rows, cols), BF16),
        compiler_params=_cparams(("parallel",)), name=name)(w)


def _adamw(w, g, m, v, *, name):
    rows, cols = w.shape
    tr = _row_tile(rows, cols, 7)
    c1 = 1.0 / (1.0 - ADAM_B1 ** ADAM_STEP)
    c2 = 1.0 / (1.0 - ADAM_B2 ** ADAM_STEP)

    def body(w_ref, g_ref, m_ref, v_ref, d_ref, nm_ref, nv_ref):
        gv = g_ref[...]
        nm = ADAM_B1 * m_ref[...] + (1.0 - ADAM_B1) * gv
        nv = ADAM_B2 * v_ref[...] + (1.0 - ADAM_B2) * (gv * gv)
        nm_ref[...] = nm
        nv_ref[...] = nv
        d_ref[...] = -ADAM_LR * ((nm * c1) / (jnp.sqrt(nv * c2) + ADAM_EPS) + ADAM_WD * w_ref[...])

    row = pl.BlockSpec((tr, cols), lambda i: (i, 0))
    shp = jax.ShapeDtypeStruct((rows, cols), F32)
    return pl.pallas_call(
        body, grid=(rows // tr,), in_specs=[row] * 4, out_specs=[row] * 3, out_shape=[shp] * 3,
        compiler_params=_cparams(("parallel",)), name=name)(w, g, m, v)


def _sum_leading(a, *, name, out_dtype=F32):
    n, rows, cols = a.shape
    tr = _row_tile(rows, cols, n + 1)

    def body(a_ref, o_ref):
        acc = a_ref[0].astype(F32)
        for i in range(1, n):
            acc = acc + a_ref[i].astype(F32)
        o_ref[...] = acc.astype(o_ref.dtype)

    return pl.pallas_call(
        body, grid=(rows // tr,), in_specs=[pl.BlockSpec((n, tr, cols), lambda i: (0, i, 0))],
        out_specs=pl.BlockSpec((tr, cols), lambda i: (i, 0)),
        out_shape=jax.ShapeDtypeStruct((rows, cols), out_dtype),
        compiler_params=_cparams(("parallel",)), name=name)(a)


def _add_half(dw, land, my_c, *, name):
    n, r, cols = dw.shape
    h = r // 2
    tr = _row_tile(h, cols, 3)
    hb = h // tr

    def body(c_ref, a_ref, b_ref, o_ref):
        o_ref[...] = (a_ref[...].astype(F32) + b_ref[...].astype(F32)).astype(o_ref.dtype)

    gs = pltpu.PrefetchScalarGridSpec(
        num_scalar_prefetch=1, grid=(n, hb),
        in_specs=[pl.BlockSpec((None, tr, cols), lambda s, i, c_ref: (s, c_ref[0] * hb + i, 0)),
                  pl.BlockSpec((None, tr, cols), lambda s, i, c_ref: (s, i, 0))],
        out_specs=pl.BlockSpec((None, tr, cols), lambda s, i, c_ref: (s, i, 0)))
    return pl.pallas_call(
        body, grid_spec=gs, out_shape=jax.ShapeDtypeStruct((n, h, cols), BF16),
        compiler_params=_cparams(("parallel", "parallel")), name=name)(my_c, dw, land)


def _mesh_pos():
    return lax.axis_index("x"), lax.axis_index("y"), lax.axis_index("c")


def _other_chips(x, y):
    return [(1 - x, y), (x, 1 - y), (1 - x, 1 - y)]


def _gather_small(blk, *, name):
    m_per, n = blk.shape

    def body(x_ref, out_ref, send_sems, recv_sems, local_sem):
        x, y, c = _mesh_pos()
        me, sibling = (x, y, c), (x, y, 1 - c)
        chips = _other_chips(x, y)

        def rows(px, py, pc):
            return out_ref.at[pl.ds((4 * px + 2 * py + pc) * m_per, m_per), :]

        def copy(k, block, to, src=None):
            return pltpu.make_async_remote_copy(
                src_ref=rows(*block) if src is None else src, dst_ref=rows(*block),
                send_sem=send_sems.at[k], recv_sem=recv_sems.at[k], device_id=to, device_id_type=MESH)

        mine = pltpu.make_async_copy(x_ref, rows(*me), local_sem)
        mine.start()
        first = [copy(0, me, sibling, src=x_ref)]
        first += [copy(1 + j, me, (*chip, c), src=x_ref) for j, chip in enumerate(chips)]
        for cp in first:
            cp.start()
        passed = [copy(4 + j, (*chip, c), sibling) for j, chip in enumerate(chips)]
        for j, chip in enumerate(chips):
            copy(1 + j, (*chip, c), me).wait_recv()
            passed[j].start()
        copy(0, sibling, me).wait_recv()
        for j, chip in enumerate(chips):
            copy(4 + j, (*chip, 1 - c), me).wait_recv()
        for cp in first + passed:
            cp.wait_send()
        mine.wait()

    return pl.pallas_call(
        body, out_shape=jax.ShapeDtypeStruct((N_DEV * m_per, n), blk.dtype),
        in_specs=[pl.BlockSpec(memory_space=pltpu.VMEM)], out_specs=pl.BlockSpec(memory_space=pltpu.VMEM),
        scratch_shapes=[pltpu.SemaphoreType.DMA((7,)), pltpu.SemaphoreType.DMA((7,)), pltpu.SemaphoreType.DMA],
        compiler_params=pltpu.CompilerParams(vmem_limit_bytes=VMEM_LIMIT_BYTES), name=name)(blk)


def _hbm_specs(n):
    return [pl.BlockSpec(memory_space=pl.ANY)] * n


def _gather_weights(shards):
    n = len(shards)

    def body(*refs):
        ins, outs = refs[:n], refs[n:2 * n]
        send_sems, recv_sems, local_sems = refs[2 * n:]
        x, y, c = _mesh_pos()
        me_chip = 2 * x + y
        sibling = (x, y, 1 - c)
        chips = _other_chips(x, y)

        def half(w, chip_idx, pc):
            h = shards[w].shape[0] // 2
            return outs[w].at[chip_idx, pl.ds(pc * h, h), :]

        def copy(w, k, chip_idx, pc, to, src=None):
            dst = half(w, chip_idx, pc)
            return pltpu.make_async_remote_copy(
                src_ref=dst if src is None else src, dst_ref=dst,
                send_sem=send_sems.at[6 * w + k], recv_sem=recv_sems.at[6 * w + k],
                device_id=to, device_id_type=MESH)

        local = [pltpu.make_async_copy(ins[w], outs[w].at[me_chip], local_sems.at[w]) for w in range(n)]
        for cp in local:
            cp.start()
        sends = []
        for w in range(n):
            h = shards[w].shape[0] // 2
            for j, chip in enumerate(chips):
                cp = copy(w, j, me_chip, c, (*chip, c), src=ins[w].at[pl.ds(c * h, h), :])
                cp.start()
                sends.append(cp)
        for w in range(n):
            for j, chip in enumerate(chips):
                chip_idx = 2 * chip[0] + chip[1]
                copy(w, j, chip_idx, c, (x, y, c)).wait_recv()
                cp = copy(w, 3 + j, chip_idx, c, sibling)
                cp.start()
                sends.append(cp)
        for w in range(n):
            for j, chip in enumerate(chips):
                copy(w, 3 + j, 2 * chip[0] + chip[1], 1 - c, (x, y, c)).wait_recv()
        for cp in sends:
            cp.wait_send()
        for cp in local:
            cp.wait()

    return pl.pallas_call(
        body, out_shape=[jax.ShapeDtypeStruct((N_CHIPS,) + s.shape, s.dtype) for s in shards],
        in_specs=_hbm_specs(n), out_specs=_hbm_specs(n),
        scratch_shapes=[pltpu.SemaphoreType.DMA((6 * n,)), pltpu.SemaphoreType.DMA((6 * n,)),
                        pltpu.SemaphoreType.DMA((n,))],
        name="gather_weights")(*shards)


def _swap_halves(dws, *, name):
    n = len(dws)

    def body(*refs):
        ins, outs = refs[:n], refs[n:2 * n]
        send_sems, recv_sems = refs[2 * n:]
        x, y, c = _mesh_pos()
        cps = []
        for w in range(n):
            h = dws[w].shape[1] // 2
            cp = pltpu.make_async_remote_copy(
                src_ref=ins[w].at[:, pl.ds((1 - c) * h, h), :], dst_ref=outs[w],
                send_sem=send_sems.at[w], recv_sem=recv_sems.at[w],
                device_id=(x, y, 1 - c), device_id_type=MESH)
            cp.start()
            cps.append(cp)
        for cp in cps:
            cp.wait()

    return pl.pallas_call(
        body, out_shape=[jax.ShapeDtypeStruct((s.shape[0], s.shape[1] // 2, s.shape[2]), s.dtype) for s in dws],
        in_specs=_hbm_specs(n), out_specs=_hbm_specs(n),
        scratch_shapes=[pltpu.SemaphoreType.DMA((n,)), pltpu.SemaphoreType.DMA((n,))],
        name=name)(*dws)


def _chip_exchange(parts):
    n = len(parts)

    def body(*refs):
        ins, outs = refs[:n], refs[n:2 * n]
        send_sems, recv_sems, local_sems = refs[2 * n:]
        x, y, c = _mesh_pos()
        me_chip = 2 * x + y
        chips = _other_chips(x, y)
        local = [pltpu.make_async_copy(ins[w].at[me_chip], outs[w].at[me_chip], local_sems.at[w]) for w in range(n)]
        for cp in local:
            cp.start()
        cps = []
        for w in range(n):
            for j, chip in enumerate(chips):
                cp = pltpu.make_async_remote_copy(
                    src_ref=ins[w].at[2 * chip[0] + chip[1]], dst_ref=outs[w].at[me_chip],
                    send_sem=send_sems.at[3 * w + j], recv_sem=recv_sems.at[3 * w + j],
                    device_id=(*chip, c), device_id_type=MESH)
                cp.start()
                cps.append((cp, w, j, chip))
        for cp, w, j, chip in cps:
            slot = outs[w].at[2 * chip[0] + chip[1]]
            pltpu.make_async_remote_copy(
                src_ref=slot, dst_ref=slot, send_sem=send_sems.at[3 * w + j], recv_sem=recv_sems.at[3 * w + j],
                device_id=(x, y, c), device_id_type=MESH).wait_recv()
        for cp, _, _, _ in cps:
            cp.wait_send()
        for cp in local:
            cp.wait()

    return pl.pallas_call(
        body, out_shape=[jax.ShapeDtypeStruct(s.shape, s.dtype) for s in parts],
        in_specs=_hbm_specs(n), out_specs=_hbm_specs(n),
        scratch_shapes=[pltpu.SemaphoreType.DMA((3 * n,)), pltpu.SemaphoreType.DMA((3 * n,)),
                        pltpu.SemaphoreType.DMA((n,))],
        name="chip_exchange")(*parts)


def _join_halves(halves):
    n = len(halves)

    def body(*refs):
        ins, outs = refs[:n], refs[n:2 * n]
        send_sems, recv_sems, local_sems = refs[2 * n:]
        x, y, c = _mesh_pos()
        cps, local = [], []
        for w in range(n):
            h = halves[w].shape[0]
            mine = outs[w].at[pl.ds(c * h, h), :]
            lc = pltpu.make_async_copy(ins[w], mine, local_sems.at[w])
            lc.start()
            local.append(lc)
            cp = pltpu.make_async_remote_copy(
                src_ref=ins[w], dst_ref=mine, send_sem=send_sems.at[w], recv_sem=recv_sems.at[w],
                device_id=(x, y, 1 - c), device_id_type=MESH)
            cp.start()
            cps.append(cp)
        for w in range(n):
            h = halves[w].shape[0]
            theirs = outs[w].at[pl.ds((1 - c) * h, h), :]
            pltpu.make_async_remote_copy(
                src_ref=theirs, dst_ref=theirs, send_sem=send_sems.at[w], recv_sem=recv_sems.at[w],
                device_id=(x, y, c), device_id_type=MESH).wait_recv()
        for cp in cps:
            cp.wait_send()
        for lc in local:
            lc.wait()

    return pl.pallas_call(
        body, out_shape=[jax.ShapeDtypeStruct((2 * s.shape[0], s.shape[1]), s.dtype) for s in halves],
        in_specs=_hbm_specs(n), out_specs=_hbm_specs(n),
        scratch_shapes=[pltpu.SemaphoreType.DMA((n,)), pltpu.SemaphoreType.DMA((n,)), pltpu.SemaphoreType.DMA((n,))],
        name="join_halves")(*halves)


def _cast_into_slot(w, chip, *, name):
    rows, cols = w.shape
    tr = _row_tile(rows, cols, 2)

    def body(chip_ref, w_ref, o_ref):
        o_ref[...] = w_ref[...].astype(BF16)

    gs = pltpu.PrefetchScalarGridSpec(
        num_scalar_prefetch=1, grid=(rows // tr,),
        in_specs=[pl.BlockSpec((tr, cols), lambda i, chip_ref: (i, 0))],
        out_specs=pl.BlockSpec((None, tr, cols), lambda i, chip_ref: (chip_ref[0], i, 0)))
    return pl.pallas_call(
        body, grid_spec=gs, out_shape=jax.ShapeDtypeStruct((N_CHIPS, rows, cols), BF16),
        compiler_params=_cparams(("parallel",)), name=name)(chip, w)


def _row_range(h, lo, hi, parts):
    step = h // parts
    assert step * parts == h and step % (2 * SUBLANES) == 0, (h, parts)
    return lo * step, (hi - lo) * step


def _gather_carry(items):
    n_copies = sum(len(js) for _, js, _, _, _ in items)
    sem = pltpu.SemaphoreType.DMA((2 * n_copies,))

    def copies(outs, sems):
        send_sems, recv_sems = sems
        x, y, c = _mesh_pos()
        me_chip = 2 * x + y
        chips = _other_chips(x, y)
        out_ici, in_ici, out_d2d, in_d2d = [], [], [], []
        k = 0
        for w, (buf, js, lo, hi, parts) in enumerate(items):
            h = buf.shape[1] // 2
            r0, nr = _row_range(h, lo, hi, parts)

            def copy(k, chip_idx, pc, to):
                ref = outs[w].at[chip_idx, pl.ds(pc * h + r0, nr), :]
                return pltpu.make_async_remote_copy(
                    src_ref=ref, dst_ref=ref, send_sem=send_sems.at[k], recv_sem=recv_sems.at[k],
                    device_id=to, device_id_type=MESH)

            for j in js:
                chip = chips[j]
                chip_idx = 2 * chip[0] + chip[1]
                out_ici.append(copy(k, me_chip, c, (*chip, c)))
                in_ici.append(copy(k, chip_idx, c, (x, y, c)))
                out_d2d.append(copy(k + 1, chip_idx, c, (x, y, 1 - c)))
                in_d2d.append(copy(k + 1, chip_idx, 1 - c, (x, y, c)))
                k += 2
        return out_ici, in_ici, out_d2d, in_d2d

    def start(ins, outs, sems):
        for cp in copies(outs, sems)[0]:
            cp.start()

    def finish(ins, outs, sems):
        out_ici, in_ici, out_d2d, in_d2d = copies(outs, sems)
        for arrived, onward in zip(in_ici, out_d2d):
            arrived.wait_recv()
            onward.start()
        for arrived in in_d2d:
            arrived.wait_recv()
        for cp in out_ici + out_d2d:
            cp.wait_send()

    bufs = [it[0] for it in items]
    shapes = [jax.ShapeDtypeStruct(b.shape, b.dtype) for b in bufs]
    return _Carry(bufs, shapes, {i: i for i in range(len(bufs))}, [sem, sem], start, finish)


def _exchange_carry(items):
    n = len(items)
    sem = pltpu.SemaphoreType.DMA((3 * n,))
    given = [w for w in range(n) if items[w][1] is not None]

    def copies(ins, outs, sems):
        send_sems, recv_sems = sems
        x, y, c = _mesh_pos()
        chips = _other_chips(x, y)
        sends, recvs = [], []
        for w, (part, _, lo, hi, parts) in enumerate(items):
            r0, nr = _row_range(part.shape[1], lo, hi, parts)
            for j, chip in enumerate(chips):
                land = outs[w].at[j, pl.ds(r0, nr), :]
                sends.append(pltpu.make_async_remote_copy(
                    src_ref=ins[w].at[2 * chip[0] + chip[1], pl.ds(r0, nr), :], dst_ref=land,
                    send_sem=send_sems.at[3 * w + j], recv_sem=recv_sems.at[3 * w + j],
                    device_id=(*chip, c), device_id_type=MESH))
                recvs.append(pltpu.make_async_remote_copy(
                    src_ref=land, dst_ref=land,
                    send_sem=send_sems.at[3 * w + j], recv_sem=recv_sems.at[3 * w + j],
                    device_id=(x, y, c), device_id_type=MESH))
        return sends, recvs

    def start(ins, outs, sems):
        for cp in copies(ins, outs, sems)[0]:
            cp.start()

    def finish(ins, outs, sems):
        sends, recvs = copies(ins, outs, sems)
        for cp in recvs:
            cp.wait_recv()
        for cp in sends:
            cp.wait_send()

    inputs = [it[0] for it in items] + [items[w][1] for w in given]
    shapes = [jax.ShapeDtypeStruct((3,) + it[0].shape[1:], it[0].dtype) for it in items]
    aliases = {n + i: w for i, w in enumerate(given)}
    return _Carry(inputs, shapes, aliases, [sem, sem], start, finish)


def _sum_into_half(part, landed, chip, my_c, *, name):
    _, h, cols = part.shape
    tr = _row_tile(h, cols, 5)
    hb = h // tr

    def body(chip_ref, c_ref, p_ref, l_ref, o_ref):
        acc = p_ref[...].astype(F32)
        for j in range(3):
            acc = acc + l_ref[j].astype(F32)
        o_ref[...] = acc

    gs = pltpu.PrefetchScalarGridSpec(
        num_scalar_prefetch=2, grid=(hb,),
        in_specs=[pl.BlockSpec((None, tr, cols), lambda i, chip_ref, c_ref: (chip_ref[0], i, 0)),
                  pl.BlockSpec((3, tr, cols), lambda i, chip_ref, c_ref: (0, i, 0))],
        out_specs=pl.BlockSpec((tr, cols), lambda i, chip_ref, c_ref: (c_ref[0] * hb + i, 0)))
    return pl.pallas_call(
        body, grid_spec=gs, out_shape=jax.ShapeDtypeStruct((2 * h, cols), F32),
        compiler_params=_cparams(("parallel",)), name=name)(chip, my_c, part, landed)


def _join_carry(fulls):
    n = len(fulls)
    sem = pltpu.SemaphoreType.DMA((n,))

    def copies(outs, sems):
        send_sems, recv_sems = sems
        x, y, c = _mesh_pos()
        sends, recvs = [], []
        for w in range(n):
            h = fulls[w].shape[0] // 2
            mine = outs[w].at[pl.ds(c * h, h), :]
            theirs = outs[w].at[pl.ds((1 - c) * h, h), :]
            sends.append(pltpu.make_async_remote_copy(
                src_ref=mine, dst_ref=mine, send_sem=send_sems.at[w], recv_sem=recv_sems.at[w],
                device_id=(x, y, 1 - c), device_id_type=MESH))
            recvs.append(pltpu.make_async_remote_copy(
                src_ref=theirs, dst_ref=theirs, send_sem=send_sems.at[w], recv_sem=recv_sems.at[w],
                device_id=(x, y, c), device_id_type=MESH))
        return sends, recvs

    def start(ins, outs, sems):
        for cp in copies(outs, sems)[0]:
            cp.start()

    def finish(ins, outs, sems):
        sends, recvs = copies(outs, sems)
        for cp in recvs:
            cp.wait_recv()
        for cp in sends:
            cp.wait_send()

    shapes = [jax.ShapeDtypeStruct(f.shape, f.dtype) for f in fulls]
    return _Carry(fulls, shapes, {i: i for i in range(n)}, [sem, sem], start, finish)


class _NoComm:
    def __init__(self, big):
        self.big = big
        self.grads = {}

    def weight(self, name):
        return self.big[name]

    def mm_in(self, u):
        return _mm(u, self.big["w_in"], mode="nn", out_dtype=BF16, name="mm_in")

    def mm_d_in(self, dproj):
        return _mm(dproj, self.big["w_in"], mode="nt", out_dtype=F32, name="mm_d_in")

    def carry(self, site):
        return None

    def done(self, site, carried):
        pass

    def grad(self, name, dw):
        self.grads[name] = dw

    def early_grads(self, early):
        self.early = early


def _gather_rows_carry(blk):
    m_per = blk.shape[0]
    sem = pltpu.SemaphoreType.DMA((7,))

    def copies(ins, outs, sems):
        send_sems, recv_sems, local_sem = sems
        x, y, c = _mesh_pos()
        me, sibling = (x, y, c), (x, y, 1 - c)
        chips = _other_chips(x, y)

        def rows(px, py, pc):
            return outs[0].at[pl.ds((4 * px + 2 * py + pc) * m_per, m_per), :]

        def copy(k, block, to, src=None):
            return pltpu.make_async_remote_copy(
                src_ref=rows(*block) if src is None else src, dst_ref=rows(*block),
                send_sem=send_sems.at[k], recv_sem=recv_sems.at[k], device_id=to, device_id_type=MESH)

        mine = pltpu.make_async_copy(ins[0], rows(*me), local_sem.at[0])
        first = [copy(0, me, sibling, src=ins[0])]
        first += [copy(1 + j, me, (*chip, c), src=ins[0]) for j, chip in enumerate(chips)]
        passed = [copy(4 + j, (*chip, c), sibling) for j, chip in enumerate(chips)]
        landed = [copy(1 + j, (*chip, c), me) for j, chip in enumerate(chips)]
        from_sibling = [copy(0, sibling, me)] + [copy(4 + j, (*chip, 1 - c), me) for j, chip in enumerate(chips)]
        return mine, first, passed, landed, from_sibling

    def start(ins, outs, sems):
        mine, first, _, _, _ = copies(ins, outs, sems)
        mine.start()
        for cp in first:
            cp.start()

    def finish(ins, outs, sems):
        mine, first, passed, landed, from_sibling = copies(ins, outs, sems)
        for arrived, onward in zip(landed, passed):
            arrived.wait_recv()
            onward.start()
        for arrived in from_sibling:
            arrived.wait_recv()
        for cp in first + passed:
            cp.wait_send()
        mine.wait()

    shape = jax.ShapeDtypeStruct((N_DEV * m_per, blk.shape[1]), blk.dtype)
    return _Carry([blk], [shape], {}, [sem, sem, pltpu.SemaphoreType.DMA((1,))], start, finish)


def _gather_fresh_carry(own, js):
    n = len(js)
    h = own.shape[0] // 2
    sem = pltpu.SemaphoreType.DMA((2 * n,))

    def copies(ins, outs, sems):
        send_sems, recv_sems = sems
        x, y, c = _mesh_pos()
        chips = _other_chips(x, y)
        out_ici, in_ici, out_d2d, in_d2d = [], [], [], []

        def copy(k, src, dst, to):
            return pltpu.make_async_remote_copy(
                src_ref=src, dst_ref=dst, send_sem=send_sems.at[k], recv_sem=recv_sems.at[k],
                device_id=to, device_id_type=MESH)

        for jj, j in enumerate(js):
            mine = ins[0].at[pl.ds(c * h, h), :]
            land = outs[0].at[jj, pl.ds(c * h, h), :]
            other = outs[0].at[jj, pl.ds((1 - c) * h, h), :]
            out_ici.append(copy(2 * jj, mine, land, (*chips[j], c)))
            in_ici.append(copy(2 * jj, land, land, (x, y, c)))
            out_d2d.append(copy(2 * jj + 1, land, land, (x, y, 1 - c)))
            in_d2d.append(copy(2 * jj + 1, other, other, (x, y, c)))
        return out_ici, in_ici, out_d2d, in_d2d

    def start(ins, outs, sems):
        for cp in copies(ins, outs, sems)[0]:
            cp.start()

    def finish(ins, outs, sems):
        out_ici, in_ici, out_d2d, in_d2d = copies(ins, outs, sems)
        for arrived, onward in zip(in_ici, out_d2d):
            arrived.wait_recv()
            onward.start()
        for arrived in in_d2d:
            arrived.wait_recv()
        for cp in out_ici + out_d2d:
            cp.wait_send()

    return _Carry([own], [jax.ShapeDtypeStruct((n,) + own.shape, own.dtype)], {}, [sem, sem], start, finish)


ALL_CHIPS = (0, 1, 2)


class _MeshComm:
    GATHER_AT = {
        "mm_in_diag": [("w_conv_out", ALL_CHIPS, 0, 1, 1)],
        "conv_fwd": [("w_glu_a", ALL_CHIPS, 0, 1, 1), ("w_glu_b", ALL_CHIPS, 0, 1, 1), ("w_out", ALL_CHIPS, 0, 1, 1)],
        "s5_fwd": [("w_ff1", ALL_CHIPS, 0, 6, 8)],
        "mm_glu_a": [("w_ff1", ALL_CHIPS, 6, 7, 8)],
        "mm_glu_b": [("w_ff1", ALL_CHIPS, 7, 8, 8)],
        "mm_out": [("w_ff2", ALL_CHIPS, 0, 2, 8)],
        "mm_ff1": [("w_ff2", ALL_CHIPS, 2, 8, 8)],
    }
    EXCHANGE_AT = {
        "mm_d_ff2": [("w_ff2", 0, 6, 8)],
        "mm_dw_ff1": [("w_ff2", 6, 8, 8)],
        "mm_d_ff1": [("w_ff1", 0, 6, 8)],
        "mm_dw_out": [("w_ff1", 6, 8, 8)],
        "s5_bwd": [("w_out", 0, 1, 1), ("w_glu_a", 0, 1, 1), ("w_glu_b", 0, 1, 1), ("w_conv_out", 0, 1, 1)],
        "mm_d_in": [("w_in", 0, 1, 1)],
    }
    EARLY_AT = "mm_dw_in"

    def __init__(self, shards, pos, chip, my_c):
        self.pos = pos
        self.chip = chip
        self.my_c = my_c
        self.w_in_own = _cast_bf16(shards["w_in"], name="cast_w_in")
        self.bufs = {n: _cast_into_slot(s, chip, name="cast_" + n) for n, s in shards.items() if n != "w_in"}
        self.raw = {}
        self.parts = {}
        self.landing = {}
        self.halves = {}
        self.pending = {}
        self.last_site = {}
        for site, items in self.EXCHANGE_AT.items():
            for it in items:
                self.last_site[it[0]] = site

    def weight(self, name):
        g = self.bufs[name]
        return g.reshape(g.shape[0] * g.shape[1], g.shape[2]) if name in ROW_SHARDED else g

    def _slot_ids(self):
        x, y, _ = self.pos
        ids = [2 * x + y] + [2 * cx + cy for cx, cy in _other_chips(x, y)]
        return jnp.stack(ids).astype(jnp.int32)

    def mm_in(self, u):
        ids = self._slot_ids()
        own = self.w_in_own
        proj, (near,) = _mm_slots(u, own[None], ids[0:1], None, name="mm_in_own",
                                  carry=_gather_fresh_carry(own, (0, 1)))
        proj, (far,) = _mm_slots(u, near, ids[1:3], proj, name="mm_in_near",
                                 carry=_gather_fresh_carry(own, (2,)))
        proj, carried = _mm_slots(u, far, ids[3:4], proj, name="mm_in_diag", carry=self.carry("mm_in_diag"))
        self.done("mm_in_diag", carried)
        self.w_in_rel = jnp.concatenate([own[None], near, far], axis=0)
        return proj

    def mm_d_in(self, dproj):
        carry = self.carry("mm_d_in")
        du, carried = _mm(dproj, self.w_in_rel, mode="nt", out_dtype=F32, name="mm_d_in", carry=carry,
                          a_slots=self._slot_ids())
        self.done("mm_d_in", carried)
        return du

    def early_grads(self, early):
        self.early = early

    def carry(self, site):
        if site == self.EARLY_AT:
            flat, self.early_offs = _pack(list(self.early.values()))
            return _gather_rows_carry(flat.reshape(-1, PACK_COLS))
        if site in self.GATHER_AT:
            items = self.GATHER_AT[site]
            self.pending[site] = items
            return _gather_carry([(self.bufs[it[0]],) + tuple(it[1:]) for it in items])
        if site not in self.EXCHANGE_AT:
            return None
        items = self.EXCHANGE_AT[site]
        self.pending[site] = items
        fresh = [it[0] for it in items if it[0] in self.raw]
        if fresh:
            raws = [self.raw.pop(n) for n in fresh]
            landed = _swap_halves(raws, name="swap_halves_" + site)
            for n, dw, l1 in zip(fresh, raws, landed):
                self.parts[n] = _add_half(dw, l1, self.my_c, name="add_half_" + n)
        return _exchange_carry([(self.parts[it[0]], self.landing.get(it[0])) + tuple(it[1:]) for it in items])

    def done(self, site, carried):
        if site == self.EARLY_AT:
            self.early_all = carried[0]
            return
        items = self.pending.pop(site)
        names = [it[0] for it in items]
        if site in self.GATHER_AT:
            self.bufs.update(zip(names, carried))
            return
        for n, landed in zip(names, carried):
            self.landing[n] = landed
            if self.last_site[n] == site:
                self.halves[n] = _sum_into_half(self.parts.pop(n), self.landing.pop(n), self.chip, self.my_c,
                                                name="sum_chips_" + n)

    def grad(self, name, dw):
        if name in ROW_SHARDED:
            dw = dw.reshape(N_CHIPS, dw.shape[0] // N_CHIPS, dw.shape[1])
        self.raw[name] = dw

    def finish(self):
        names = list(self.halves)
        return dict(zip(names, _run_carry(_join_carry([self.halves[n] for n in names]), name="join_halves")))


def _local_step(x, target, mod, small, comm):
    rows, d = x.shape
    cw = d // 2
    shift1, scale1, gate1, shift2, scale2, gate2 = mod
    _, _, bbr, bbi = small["s5_disc"]
    b_in, c_out, b_out, c_in, mults = _s5_operands(*small["s5_loglam"], bbr, bbi, small["c_re"], small["c_im"])
    wt = comm.weight

    def riding(site, fn, *args, **kwargs):
        carry = comm.carry(site)
        if carry is None:
            return fn(*args, **kwargs)
        out, carried = fn(*args, carry=carry, **kwargs)
        comm.done(site, carried)
        return out

    u = _norm_mod(x, small["norm1_g"], scale1, shift1, name="norm1_fwd")
    proj = comm.mm_in(u)
    sl, cv = riding("conv_fwd", _conv_fwd, proj, small["w_dw"], small["b_dw"], small["ln_g"], small["ln_b"], cw=cw)
    y_conv = _mm(sl, wt("w_conv_out"), mode="nn", out_dtype=BF16, name="mm_conv_out")
    yg, st_re, st_im = riding("s5_fwd", _s5_fwd, proj, small["d_skip"], b_in, c_out, mults, col0=2 * cw // LANES)
    ya = riding("mm_glu_a", _mm, yg, wt("w_glu_a"), mode="nn", out_dtype=BF16, name="mm_glu_a")
    yb = riding("mm_glu_b", _mm, yg, wt("w_glu_b"), mode="nn", out_dtype=BF16, name="mm_glu_b")
    merged = _merge_fwd(proj, y_conv, ya, yb, cw=cw)
    mo = riding("mm_out", _mm, merged, wt("w_out"), mode="nn", out_dtype=BF16, name="mm_out")
    h1, z = _res_norm(x, mo, gate1, small["norm2_g"], scale2, shift2)
    f1 = riding("mm_ff1", _mm, z, wt("w_ff1"), mode="nn", out_dtype=BF16, name="mm_ff1")
    ff = _mm(f1, wt("w_ff2"), mode="nn", out_dtype=BF16, name="mm_ff2", a_fn=_relu2_bf16)
    dh2, dff, loss, d_final_g, d_gate2 = _final_fwd_bwd(h1, ff, gate2, small["final_g"], target)

    comm.grad("w_ff2", _mm(f1, dff, mode="tn", out_dtype=BF16, name="mm_dw_ff2", a_fn=_relu2_bf16))
    df1 = riding("mm_d_ff2", _mm, dff, wt("w_ff2"), mode="nt", out_dtype=BF16, name="mm_d_ff2", extra=f1,
                 epi=lambda acc, f: acc * (2.0 * jnp.maximum(f.astype(F32), 0.0)))
    comm.grad("w_ff1", riding("mm_dw_ff1", _mm, z, df1, mode="tn", out_dtype=BF16, name="mm_dw_ff1",
                              out_gathered=True))
    dz = riding("mm_d_ff1", _mm, df1, wt("w_ff1"), mode="nt", out_dtype=F32, name="mm_d_ff1")
    dh1, d_shift2, d_scale2, d_norm2_g, dmo, d_gate1 = _norm_mod_bwd(
        dz, h1, dh2, small["norm2_g"], scale2, gate1, mo, name="norm2_bwd")
    comm.grad("w_out", riding("mm_dw_out", _mm, merged, dmo, mode="tn", out_dtype=BF16, name="mm_dw_out"))
    dmerged = _mm(dmo, wt("w_out"), mode="nt", out_dtype=BF16, name="mm_d_out")
    dgc, dgs, dy_conv, dya, dyb = _merge_bwd(dmerged, proj, y_conv, ya, yb, cw=cw)
    comm.grad("w_glu_a", _mm(yg, dya, mode="tn", out_dtype=BF16, name="mm_dw_glu_a", out_gathered=True))
    comm.grad("w_glu_b", _mm(yg, dyb, mode="tn", out_dtype=BF16, name="mm_dw_glu_b", out_gathered=True))
    dyg_a = _mm(dya, wt("w_glu_a"), mode="nt", out_dtype=F32, name="mm_d_glu_a")
    dyg = _mm(dyb, wt("w_glu_b"), mode="nt", out_dtype=F32, name="mm_d_glu_b", extra=dyg_a,
              epi=lambda acc, e: acc + e)
    comm.grad("w_conv_out", _mm(sl, dy_conv, mode="tn", out_dtype=BF16, name="mm_dw_conv_out", out_gathered=True))
    dsl = _mm(dy_conv, wt("w_conv_out"), mode="nt", out_dtype=F32, name="mm_d_conv_out")
    dcv, d_ln_g, d_ln_b = _ln_bwd(dsl, cv, small["ln_g"], small["ln_b"])
    dvconv, d_w_dw, d_b_dw = _conv_bwd(dcv, proj, small["w_dw"], cw=cw)
    dvssm, d_d_skip, dbr, dbi, dcr, dci, dlr, dli = riding(
        "s5_bwd", _s5_bwd, proj, dyg, small["d_skip"], (st_re, st_im), c_out, b_out, c_in, mults,
        col0=2 * cw // LANES)
    sw = lambda m: jnp.swapaxes(m, 1, 2)
    early = {
        "dmod_tail": jnp.concatenate([d_gate1, d_shift2, d_scale2, d_gate2], axis=1), "loss": loss[:, 0:1],
        "w_dw": d_w_dw, "b_dw": d_b_dw, "ln_g": d_ln_g, "ln_b": d_ln_b,
        "lam_re": dlr.reshape(-1, SSM_STATE), "lam_im": dli.reshape(-1, SSM_STATE),
        "bb_re": sw(_block_diag_extract(dbr, SSM_GROUP, SSM_STATE)),
        "bb_im": sw(_block_diag_extract(dbi, SSM_GROUP, SSM_STATE)),
        "c_re": sw(_block_diag_extract(dcr, SSM_STATE, SSM_GROUP)),
        "c_im": sw(_block_diag_extract(dci, SSM_STATE, SSM_GROUP)),
        "d_skip": d_d_skip, "norm2_g": d_norm2_g, "final_g": d_final_g,
    }
    comm.early_grads(early)
    dproj = jnp.concatenate([dvconv, dvssm, dgc, dgs], axis=1)
    comm.grad("w_in", riding("mm_dw_in", _mm, u, dproj, mode="tn", out_dtype=BF16, name="mm_dw_in",
                             out_gathered=True))
    du = comm.mm_d_in(dproj)
    grad_x, d_shift1, d_scale1, d_norm1_g = _norm_mod_bwd(
        du, x, dh1, small["norm1_g"], scale1, None, None, name="norm1_bwd")
    late = {"dmod_head": jnp.concatenate([d_shift1, d_scale1], axis=1), "norm1_g": d_norm1_g}
    return grad_x, early, late


WEIGHT_NAMES = ["w_ada", "b_ada", "norm1_g", "w_in", "w_dw", "b_dw", "ln_g", "ln_b", "w_conv_out", "a_re", "a_im",
                "log_dt", "b_re", "b_im", "c_re", "c_im", "d_skip", "w_glu_a", "w_glu_b", "w_out", "norm2_g",
                "w_ff1", "w_ff2", "final_g"]
BIG_NAMES = ["w_in", "w_conv_out", "w_glu_a", "w_glu_b", "w_out", "w_ff1", "w_ff2"]
ROW_SHARDED = ("w_out", "w_ff2")
PACK_COLS = 1024
PACK_TILE = SUBLANES * PACK_COLS


def _pack(arrays):
    flats = [a.reshape(-1) for a in arrays]
    offs = []
    total = 0
    for f in flats:
        offs.append(total)
        total += f.shape[0]
    pad = (-total) % PACK_TILE
    if pad:
        flats.append(jnp.zeros((pad,), F32))
    return jnp.concatenate(flats), offs


def _unpack(flat, offs, like):
    return [flat[o:o + a.size].reshape(a.shape) for o, a in zip(offs, like)]


def _gather_w_dw(w_shard):
    k, n = w_shard.shape
    padded = jnp.pad(w_shard, ((0, HALO - k), (0, 0)))
    allw = _gather_small(padded, name="gather_w_dw").reshape(N_CHIPS, 2, HALO, n)[:, 0, :k]
    return jnp.moveaxis(allw, 0, 1).reshape(k, N_CHIPS * n)


def kernel(x, c, w_ada, b_ada, norm1_g, w_in, w_dw, b_dw, ln_g, ln_b, w_conv_out, a_re, a_im, log_dt, b_re, b_im, c_re, c_im, d_skip, w_glu_a, w_glu_b, w_out, norm2_g, w_ff1, w_ff2, final_g, loss_target, m_w_ada, m_b_ada, m_norm1_g, m_w_in, m_w_dw, m_b_dw, m_ln_g, m_ln_b, m_w_conv_out, m_a_re, m_a_im, m_log_dt, m_b_re, m_b_im, m_c_re, m_c_im, m_d_skip, m_w_glu_a, m_w_glu_b, m_w_out, m_norm2_g, m_w_ff1, m_w_ff2, m_final_g, v_w_ada, v_b_ada, v_norm1_g, v_w_in, v_w_dw, v_b_dw, v_ln_g, v_ln_b, v_w_conv_out, v_a_re, v_a_im, v_log_dt, v_b_re, v_b_im, v_c_re, v_c_im, v_d_skip, v_w_glu_a, v_w_glu_b, v_w_out, v_norm2_g, v_w_ff1, v_w_ff2, v_final_g):
    given = dict(locals())
    w = {n: given[n] for n in WEIGHT_NAMES}
    m = {n: given["m_" + n] for n in WEIGHT_NAMES}
    v = {n: given["v_" + n] for n in WEIGHT_NAMES}
    d = x.shape[2]
    xi, yi, ci = _mesh_pos()
    chip = 2 * xi + yi
    dev = 4 * xi + 2 * yi + ci
    my_c = jnp.reshape(ci, (1,)).astype(jnp.int32)

    ndw = w_dw.shape[2]
    assert d // SUBLANES == ndw
    first = jnp.concatenate([c.reshape(SUBLANES, ndw), jnp.pad(w_dw[0], ((0, HALO - CONV_KERNEL), (0, 0)))])
    first_all = _gather_small(first, name="gather_c_w_dw").reshape(N_DEV, SUBLANES + HALO, ndw)
    c_all = first_all[:, :SUBLANES].reshape(N_DEV, d)
    taps = first_all.reshape(N_CHIPS, 2, SUBLANES + HALO, ndw)[:, 0, SUBLANES:SUBLANES + CONV_KERNEL]
    w_dw_full = jnp.moveaxis(taps, 0, 1).reshape(CONV_KERNEL, N_CHIPS * ndw)

    nmod = w_ada.shape[2]
    b_cols = lax.dynamic_slice(b_ada, (0, chip * nmod), (1, nmod))
    mod_part = _ada_fwd(c_all, w_ada[0], b_cols)
    mod_all = _gather_small(mod_part, name="gather_mod").reshape(N_CHIPS, 2, N_DEV, nmod)[:, 0]
    mod_full = jnp.moveaxis(mod_all, 0, 1).reshape(N_DEV, N_CHIPS * nmod)
    mod_row = lax.dynamic_slice(mod_full, (dev, 0), (1, N_CHIPS * nmod))
    mod = [mod_row[:, i * d:(i + 1) * d] for i in range(6)]

    chip_arr = jnp.reshape(chip, (1,)).astype(jnp.int32)
    comm = _MeshComm({n: w[n][0] for n in BIG_NAMES}, (xi, yi, ci), chip_arr, my_c)

    disc_in = (a_re[0], a_im[0], log_dt[0], b_re[0], b_im[0])
    disc, disc_vjp = jax.vjp(_s5_discretise, *disc_in)
    dt = jnp.exp(log_dt[0])[:, None]
    small = {"norm1_g": norm1_g, "w_dw": w_dw_full, "b_dw": b_dw, "ln_g": ln_g, "ln_b": ln_b,
             "c_re": c_re[0], "c_im": c_im[0], "d_skip": d_skip, "norm2_g": norm2_g,
             "final_g": final_g[None, :], "s5_disc": disc, "s5_loglam": (a_re[0] * dt, a_im[0] * dt)}

    grad_x, early, late = _local_step(x[0], loss_target[0], mod, small, comm)
    grads = comm.finish()

    early_all = comm.early_all.reshape(N_DEV, -1, PACK_COLS)
    early_sum = _sum_leading(early_all, name="sum_small_grads").reshape(-1)
    summed = dict(zip(early, _unpack(early_sum, comm.early_offs, list(early.values()))))
    flat, late_offs = _pack(list(late.values()))
    late_all = _gather_small(flat.reshape(-1, PACK_COLS), name="gather_late_grads").reshape(N_DEV, -1, PACK_COLS)
    late_sum = _sum_leading(late_all, name="sum_late_grads").reshape(-1)
    summed.update(zip(late, _unpack(late_sum, late_offs, list(late.values()))))
    head = late_all[:, :2 * d // PACK_COLS].reshape(N_DEV, 2 * d)
    tail = early_all[:, :4 * d // PACK_COLS].reshape(N_DEV, 4 * d)
    dmod_all = jnp.concatenate([head, tail], axis=1)

    grads["w_ada"] = _ada_bwd(c_all, lax.dynamic_slice(dmod_all, (0, chip * nmod), (N_DEV, nmod)))
    grads["b_ada"] = _sum_leading(dmod_all.reshape(N_DEV, SUBLANES, 6 * d // SUBLANES),
                                  name="sum_b_ada").reshape(1, 6 * d)
    da_re, da_im, dlog_dt, db_re, db_im = disc_vjp(
        (summed["lam_re"], summed["lam_im"], summed["bb_re"], summed["bb_im"]))
    grads.update({
        "norm1_g": summed["norm1_g"], "w_dw": lax.dynamic_slice(summed["w_dw"], (0, chip * ndw), (CONV_KERNEL, ndw)),
        "b_dw": summed["b_dw"], "ln_g": summed["ln_g"], "ln_b": summed["ln_b"],
        "a_re": da_re, "a_im": da_im, "log_dt": dlog_dt, "b_re": db_re, "b_im": db_im,
        "c_re": summed["c_re"], "c_im": summed["c_im"], "d_skip": summed["d_skip"],
        "norm2_g": summed["norm2_g"], "final_g": summed["final_g"],
    })
    grads = {n: grads[n].reshape(w[n].shape) for n in WEIGHT_NAMES}

    delta, new_m, new_v = {}, {}, {}
    for n in ["w_ada"] + BIG_NAMES:
        shp = w[n].shape
        two_d = lambda a: a.reshape(shp[1], shp[2])
        dl, nm, nv = _adamw(two_d(w[n]), two_d(grads[n]), two_d(m[n]), two_d(v[n]), name="adamw_" + n)
        delta[n], new_m[n], new_v[n] = dl.reshape(shp), nm.reshape(shp), nv.reshape(shp)
    rest = [n for n in WEIGHT_NAMES if n not in delta]
    packs = []
    for src in (w, grads, m, v):
        flat, offs = _pack([src[n] for n in rest])
        packs.append(flat.reshape(-1, 1024))
    outs = _adamw(*packs, name="adamw_small")
    for dst, o in zip((delta, new_m, new_v), outs):
        for n, a in zip(rest, _unpack(o.reshape(-1), offs, [w[k] for k in rest])):
            dst[n] = a

    return (summed["loss"].reshape(()), grad_x[None], *[grads[n] for n in WEIGHT_NAMES],
            *[delta[n] for n in WEIGHT_NAMES], *[new_m[n] for n in WEIGHT_NAMES],
            *[new_v[n] for n in WEIGHT_NAMES])
```

```python
import functools
import math

import jax
import jax.numpy as jnp
from jax import lax
from jax.experimental import pallas as pl
from jax.experimental.pallas import tpu as pltpu

F32 = jnp.float32
BF16 = jnp.bfloat16
EPS = 1e-6
CONV_KERNEL = 31
SSM_GROUP = 16
SSM_STATE = 64
ADAM_LR = 0.001
ADAM_B1 = 0.9
ADAM_B2 = 0.999
ADAM_EPS = 1e-08
ADAM_WD = 0.01
ADAM_STEP = 10

N_CHIPS = 4
N_DEV = 8
VMEM_LIMIT_BYTES = 56 * 1024 * 1024
LANES = 128
SUBLANES = 8
HALO = 32
GROUPS_PER_BLOCK = LANES // SSM_GROUP
STATE_LANES = GROUPS_PER_BLOCK * SSM_STATE
MESH = pl.DeviceIdType.MESH


def _cparams(sem):
    return pltpu.CompilerParams(dimension_semantics=sem, vmem_limit_bytes=VMEM_LIMIT_BYTES)


def _pick(n, pref, mult=LANES):
    if n <= pref:
        return n
    best = None
    for d in range(mult, pref + 1, mult):
        if n % d == 0:
            best = d
    assert best is not None, (n, pref)
    return best


def _sigmoid(v):
    return 1.0 / (1.0 + jnp.exp(-v))


def _gelu_parts(v):
    k0 = math.sqrt(2.0 / math.pi)
    inner = k0 * (v + 0.044715 * v * v * v)
    t = jnp.tanh(inner)
    return k0, t


def _gelu(v):
    _, t = _gelu_parts(v)
    return 0.5 * v * (1.0 + t)


def _gelu_grad(v):
    k0, t = _gelu_parts(v)
    return 0.5 * (1.0 + t) + 0.5 * v * (1.0 - t * t) * k0 * (1.0 + 3.0 * 0.044715 * v * v)


def _relu2_bf16(a):
    t = jnp.maximum(a.astype(F32), 0.0)
    return (t * t).astype(BF16)


class _Carry:
    def __init__(self, inputs, out_shapes, aliases, sem_shapes, start, finish):
        self.inputs = list(inputs)
        self.out_shapes = list(out_shapes)
        self.aliases = dict(aliases)
        self.sem_shapes = list(sem_shapes)
        self.start = start
        self.finish = finish


def _call(body, *, grid, in_specs, out_specs, out_shape, scratch_shapes, semantics, name, args, carry=None,
          prefetch=(), aliases=None):
    n_in, n_out, n_scr, n_pf = len(in_specs), len(out_specs), len(scratch_shapes), len(prefetch)
    own_aliases = {n_pf + i: o for i, o in (aliases or {}).items()}
    if carry is None:
        gs = pltpu.PrefetchScalarGridSpec(
            num_scalar_prefetch=n_pf, grid=grid, in_specs=in_specs, out_specs=out_specs,
            scratch_shapes=scratch_shapes)
        outs = pl.pallas_call(
            body, grid_spec=gs, out_shape=out_shape, input_output_aliases=own_aliases,
            compiler_params=_cparams(semantics), name=name)(*prefetch, *args)
        return list(outs), []
    ci, co = len(carry.inputs), len(carry.out_shapes)

    def wrapped(*refs):
        pf, refs = refs[:n_pf], refs[n_pf:]
        ins, cins = refs[:n_in], refs[n_in:n_in + ci]
        p = n_in + ci
        outs, couts = refs[p:p + n_out], refs[p + n_out:p + n_out + co]
        p += n_out + co
        scr, csems = refs[p:p + n_scr], refs[p + n_scr:]
        first = pl.program_id(0) == 0
        last = pl.program_id(0) == grid[0] - 1
        for ax in range(1, len(grid)):
            first = jnp.logical_and(first, pl.program_id(ax) == 0)
            last = jnp.logical_and(last, pl.program_id(ax) == grid[ax] - 1)

        @pl.when(first)
        def _():
            carry.start(cins, couts, csems)

        body(*pf, *ins, *outs, *scr)

        @pl.when(last)
        def _():
            carry.finish(cins, couts, csems)

    any_spec = pl.BlockSpec(memory_space=pl.ANY)
    gs = pltpu.PrefetchScalarGridSpec(
        num_scalar_prefetch=n_pf, grid=grid, in_specs=list(in_specs) + [any_spec] * ci,
        out_specs=list(out_specs) + [any_spec] * co, scratch_shapes=list(scratch_shapes) + carry.sem_shapes)
    all_aliases = dict(own_aliases)
    all_aliases.update({n_pf + n_in + i: n_out + o for i, o in carry.aliases.items()})
    outs = pl.pallas_call(
        wrapped, grid_spec=gs, out_shape=list(out_shape) + carry.out_shapes, input_output_aliases=all_aliases,
        compiler_params=_cparams(("arbitrary",) * len(grid)), name=name)(*prefetch, *args, *carry.inputs)
    return list(outs[:n_out]), list(outs[n_out:])


def _run_carry(carry, *, name):
    ci = len(carry.inputs)

    def body(*refs):
        cins, couts, csems = refs[:ci], refs[ci:ci + len(carry.out_shapes)], refs[ci + len(carry.out_shapes):]
        carry.start(cins, couts, csems)
        carry.finish(cins, couts, csems)

    any_spec = pl.BlockSpec(memory_space=pl.ANY)
    outs = pl.pallas_call(
        body, in_specs=[any_spec] * ci, out_specs=[any_spec] * len(carry.out_shapes), out_shape=carry.out_shapes,
        scratch_shapes=carry.sem_shapes, input_output_aliases=carry.aliases, name=name)(*carry.inputs)
    return list(outs)


def _mm(a, b, *, mode, out_dtype, name, out_gathered=False, a_fn=None, epi=None, extra=None,
        bm_pref=1024, bn_pref=1024, bk_pref=2048, carry=None, a_slots=None):
    gathered = (b.ndim == 3)
    if mode == "nn":
        m, kdim = a.shape
        ns = b.shape[-1]
        n = ns * (N_CHIPS if gathered else 1)
        bm, bn, bk = _pick(m, bm_pref), _pick(ns, bn_pref), _pick(kdim, bk_pref)
        npb = ns // bn
        grid = (m // bm, n // bn, kdim // bk)
        a_spec = pl.BlockSpec((bm, bk), lambda i, j, k: (i, k))
        if gathered:
            b_spec = pl.BlockSpec((None, bk, bn), lambda i, j, k: (j // npb, k, j % npb))
        else:
            b_spec = pl.BlockSpec((bk, bn), lambda i, j, k: (k, j))
        o_spec = pl.BlockSpec((bm, bn), lambda i, j, k: (i, j))
        e_spec = pl.BlockSpec((bm, bn), lambda i, j, k: (i, j))
        out_shape = (m, n)
        acc_shape = (bm, bn)
        dims = (((1,), (0,)), ((), ()))
    elif mode == "nt":
        m = a.shape[0]
        kdim, ns = b.shape[-2], b.shape[-1]
        n = ns * (N_CHIPS if gathered else 1)
        assert a.shape[1] == n
        bm, bko, bnr = _pick(m, bm_pref), _pick(kdim, bn_pref), _pick(ns, bk_pref)
        npb = ns // bnr
        grid = (m // bm, kdim // bko, n // bnr)
        a_spec = pl.BlockSpec((bm, bnr), lambda i, j, k: (i, k))
        if gathered:
            b_spec = pl.BlockSpec((None, bko, bnr), lambda i, j, k: (k // npb, j, k % npb))
        else:
            b_spec = pl.BlockSpec((bko, bnr), lambda i, j, k: (j, k))
        o_spec = pl.BlockSpec((bm, bko), lambda i, j, k: (i, j))
        e_spec = pl.BlockSpec((bm, bko), lambda i, j, k: (i, j))
        if a_slots is not None:
            assert gathered and extra is None
            a_spec = pl.BlockSpec((bm, bnr), lambda i, j, k, s_ref: (i, s_ref[k // npb] * npb + k % npb))
            b_spec = pl.BlockSpec((None, bko, bnr), lambda i, j, k, s_ref: (k // npb, j, k % npb))
            o_spec = pl.BlockSpec((bm, bko), lambda i, j, k, s_ref: (i, j))
        out_shape = (m, kdim)
        acc_shape = (bm, bko)
        dims = (((1,), (1,)), ((), ()))
    else:
        m, kdim = a.shape
        n = b.shape[1]
        ns = n // N_CHIPS if out_gathered else n
        bmr, bko, bn = _pick(m, bk_pref), _pick(kdim, bm_pref), _pick(ns, bn_pref)
        npb = ns // bn
        grid = (kdim // bko, n // bn, m // bmr)
        a_spec = pl.BlockSpec((bmr, bko), lambda i, j, k: (k, i))
        b_spec = pl.BlockSpec((bmr, bn), lambda i, j, k: (k, j))
        if out_gathered:
            o_spec = pl.BlockSpec((None, bko, bn), lambda i, j, k: (j // npb, i, j % npb))
            out_shape = (N_CHIPS, kdim, ns)
        else:
            o_spec = pl.BlockSpec((bko, bn), lambda i, j, k: (i, j))
            out_shape = (kdim, n)
        e_spec = None
        acc_shape = (bko, bn)
        dims = (((0,), (0,)), ((), ()))
    nk = grid[2]

    def body(*refs):
        if a_slots is not None:
            refs = refs[1:]
        if extra is not None:
            a_ref, b_ref, e_ref, o_ref, acc = refs
        else:
            a_ref, b_ref, o_ref, acc = refs
            e_ref = None
        k = pl.program_id(2)
        av = a_ref[...]
        if a_fn is not None:
            av = a_fn(av)
        part = lax.dot_general(av, b_ref[...], dims, preferred_element_type=F32)

        def finish(r):
            if epi is not None:
                r = epi(r, e_ref[...])
            o_ref[...] = r.astype(o_ref.dtype)

        if nk == 1:
            finish(part)
            return

        @pl.when(k == 0)
        def _():
            acc[...] = part

        @pl.when(jnp.logical_and(k > 0, k < nk - 1))
        def _():
            acc[...] += part

        @pl.when(k == nk - 1)
        def _():
            finish(acc[...] + part)

    in_specs = [a_spec, b_spec]
    args = [a, b]
    if extra is not None:
        in_specs.append(e_spec)
        args.append(extra)
    outs, carried = _call(body, grid=grid, in_specs=in_specs, out_specs=[o_spec],
                          out_shape=[jax.ShapeDtypeStruct(out_shape, out_dtype)],
                          scratch_shapes=[pltpu.VMEM(acc_shape, F32)],
                          semantics=("parallel", "parallel", "arbitrary"), name=name, args=args, carry=carry,
                          prefetch=() if a_slots is None else (a_slots,))
    return outs[0] if carry is None else (outs[0], carried)


def _mm_slots(a, wbuf, slots, prev, *, name, carry=None):
    m, kdim = a.shape
    ns = wbuf.shape[2]
    bm, bn = _pick(m, 1024), _pick(ns, 1024)
    npb = ns // bn
    grid = (m // bm, slots.shape[0], npb)

    def body(s_ref, a_ref, b_ref, *rest):
        o_ref = rest[-1]
        o_ref[...] = _dot(a_ref[...], b_ref[...]).astype(o_ref.dtype)

    in_specs = [pl.BlockSpec((bm, kdim), lambda i, s, j, s_ref: (i, 0)),
                pl.BlockSpec((None, kdim, bn), lambda i, s, j, s_ref: (s, 0, j))]
    args = [a, wbuf]
    aliases = None
    if prev is not None:
        in_specs.append(pl.BlockSpec(memory_space=pl.ANY))
        args.append(prev)
        aliases = {2: 0}
    outs, carried = _call(
        body, grid=grid, in_specs=in_specs,
        out_specs=[pl.BlockSpec((bm, bn), lambda i, s, j, s_ref: (i, s_ref[s] * npb + j))],
        out_shape=[jax.ShapeDtypeStruct((m, N_CHIPS * ns), BF16)], scratch_shapes=[],
        semantics=("parallel", "arbitrary", "arbitrary"), name=name, args=args, carry=carry,
        prefetch=(slots,), aliases=aliases)
    return outs[0] if carry is None else (outs[0], carried)


def _row_tile(rows, cols, n_arrays):
    budget = VMEM_LIMIT_BYTES // 3
    cap = min(512, budget // (n_arrays * 2 * cols * 4))
    for t in range(cap - cap % SUBLANES, 0, -SUBLANES):
        if rows % t == 0:
            return t
    return rows


def _norm_mod(x, g, scale, shift, *, name):
    rows, d = x.shape
    tr = _row_tile(rows, d, 3)

    def body(x_ref, g_ref, sc_ref, sh_ref, o_ref):
        xv = x_ref[...]
        r = lax.rsqrt(jnp.mean(xv * xv, axis=-1, keepdims=True) + EPS)
        o_ref[...] = ((xv * r * g_ref[...]) * (1.0 + sc_ref[...]) + sh_ref[...]).astype(o_ref.dtype)

    row = pl.BlockSpec((tr, d), lambda i: (i, 0))
    vec = pl.BlockSpec((1, d), lambda i: (0, 0))
    return pl.pallas_call(
        body, grid=(rows // tr,), in_specs=[row, vec, vec, vec], out_specs=row,
        out_shape=jax.ShapeDtypeStruct((rows, d), BF16),
        compiler_params=_cparams(("parallel",)), name=name)(x, g, scale, shift)


CONV_CHUNK = 2 * SUBLANES


def _shifted_copies(buf, n):
    for r in range(1, SUBLANES):
        buf[r, pl.ds(0, n - SUBLANES), :] = buf[0, pl.ds(r, n - SUBLANES), :]


def _conv_fwd(proj, w_dw, b_dw, ln_g, ln_b, *, cw, carry=None):
    rows = proj.shape[0]
    tt = _pick(rows, 256, HALO)
    hb = tt // HALO

    def body(a_ref, g_ref, ha_ref, hg_ref, w_ref, b_ref, lg_ref, lb_ref, sl_ref, cv_ref, vs):
        i = pl.program_id(0)
        hv = ha_ref[...].astype(F32) * _sigmoid(hg_ref[...].astype(F32))
        vs[0, pl.ds(0, HALO), :] = jnp.where(i == 0, 0.0, hv)
        vs[0, pl.ds(HALO, tt), :] = a_ref[...].astype(F32) * _sigmoid(g_ref[...].astype(F32))
        _shifted_copies(vs, HALO + tt)

        def chunk(ci, carry):
            r0 = pl.multiple_of(ci * CONV_CHUNK, CONV_CHUNK)
            acc = jnp.broadcast_to(b_ref[...], (CONV_CHUNK, cw))
            for k in range(CONV_KERNEL):
                q, r = divmod(HALO - (CONV_KERNEL - 1) + k, SUBLANES)
                acc = acc + w_ref[pl.ds(k, 1), :] * vs[r, pl.ds(r0 + q * SUBLANES, CONV_CHUNK), :]
            cv_ref[pl.ds(r0, CONV_CHUNK), :] = acc
            return carry

        lax.fori_loop(0, tt // CONV_CHUNK, chunk, 0)
        acc = cv_ref[...]
        mu = jnp.mean(acc, axis=-1, keepdims=True)
        xc = acc - mu
        rstd = lax.rsqrt(jnp.mean(xc * xc, axis=-1, keepdims=True) + EPS)
        ln = xc * rstd * lg_ref[...] + lb_ref[...]
        sl_ref[...] = (ln * _sigmoid(ln)).astype(sl_ref.dtype)

    tile = lambda c: pl.BlockSpec((tt, cw), lambda i, c=c: (i, c))
    halo = lambda c: pl.BlockSpec((HALO, cw), lambda i, c=c: (jnp.maximum(i * hb - 1, 0), c))
    vec = pl.BlockSpec((1, cw), lambda i: (0, 0))
    outs, carried = _call(
        body, grid=(rows // tt,),
        in_specs=[tile(0), tile(1), halo(0), halo(1),
                  pl.BlockSpec((CONV_KERNEL, cw), lambda i: (0, 0)), vec, vec, vec],
        out_specs=[pl.BlockSpec((tt, cw), lambda i: (i, 0)), pl.BlockSpec((tt, cw), lambda i: (i, 0))],
        out_shape=[jax.ShapeDtypeStruct((rows, cw), BF16), jax.ShapeDtypeStruct((rows, cw), F32)],
        scratch_shapes=[pltpu.VMEM((SUBLANES, HALO + tt, cw), F32)],
        semantics=("parallel",), name="conv_fwd", args=[proj, proj, proj, proj, w_dw, b_dw, ln_g, ln_b],
        carry=carry)
    return outs if carry is None else (outs, carried)


def _ln_bwd(dsl, cv, ln_g, ln_b):
    rows, cw = cv.shape
    tr = _row_tile(rows, cw, 3)

    def body(d_ref, cv_ref, lg_ref, lb_ref, o_ref, dg_ref, db_ref):
        i = pl.program_id(0)

        @pl.when(i == 0)
        def _():
            dg_ref[...] = jnp.zeros_like(dg_ref)
            db_ref[...] = jnp.zeros_like(db_ref)

        x = cv_ref[...]
        mu = jnp.mean(x, axis=-1, keepdims=True)
        xc = x - mu
        rstd = lax.rsqrt(jnp.mean(xc * xc, axis=-1, keepdims=True) + EPS)
        xh = xc * rstd
        ln = xh * lg_ref[...] + lb_ref[...]
        s = _sigmoid(ln)
        dln = d_ref[...].astype(F32) * (s * (1.0 + ln * (1.0 - s)))
        dg_ref[...] += jnp.sum(dln * xh, axis=0, keepdims=True)
        db_ref[...] += jnp.sum(dln, axis=0, keepdims=True)
        dxh = dln * lg_ref[...]
        m1 = jnp.mean(dxh, axis=-1, keepdims=True)
        m2 = jnp.mean(dxh * xh, axis=-1, keepdims=True)
        o_ref[...] = rstd * (dxh - m1 - xh * m2)

    row = pl.BlockSpec((tr, cw), lambda i: (i, 0))
    vec = pl.BlockSpec((1, cw), lambda i: (0, 0))
    return pl.pallas_call(
        body, grid=(rows // tr,), in_specs=[row, row, vec, vec], out_specs=[row, vec, vec],
        out_shape=[jax.ShapeDtypeStruct((rows, cw), F32), jax.ShapeDtypeStruct((1, cw), F32),
                   jax.ShapeDtypeStruct((1, cw), F32)],
        compiler_params=_cparams(("arbitrary",)), name="ln_bwd")(dsl, cv, ln_g, ln_b)


def _conv_bwd(dcv, proj, w_dw, *, cw, carry=None):
    rows = proj.shape[0]
    tt = _pick(rows, 256, HALO)
    hb = tt // HALO
    nt = rows // tt
    taps = CONV_KERNEL

    def body(d_ref, dn_ref, a_ref, g_ref, ha_ref, hg_ref, w_ref, o_ref, dw_ref, db_ref, vs, ds):
        i = pl.program_id(0)

        @pl.when(i == 0)
        def _():
            dw_ref[...] = jnp.zeros_like(dw_ref)
            db_ref[...] = jnp.zeros_like(db_ref)

        hv = ha_ref[...].astype(F32) * _sigmoid(hg_ref[...].astype(F32))
        vs[0, pl.ds(0, HALO), :] = jnp.where(i == 0, 0.0, hv)
        vs[0, pl.ds(HALO, tt), :] = a_ref[...].astype(F32) * _sigmoid(g_ref[...].astype(F32))
        _shifted_copies(vs, HALO + tt)
        ds[0, pl.ds(0, tt), :] = d_ref[...]
        ds[0, pl.ds(tt, HALO), :] = jnp.where(i == nt - 1, 0.0, dn_ref[...])
        _shifted_copies(ds, tt + HALO)
        db_ref[...] += jnp.sum(d_ref[...], axis=0, keepdims=True)
        for k in range(taps):
            q, r = divmod(HALO - (taps - 1) + k, SUBLANES)
            dw_ref[pl.ds(k, 1), :] += jnp.sum(d_ref[...] * vs[r, pl.ds(q * SUBLANES, tt), :], axis=0, keepdims=True)

        def chunk(ci, carry):
            r0 = pl.multiple_of(ci * CONV_CHUNK, CONV_CHUNK)
            dv = jnp.zeros((CONV_CHUNK, cw), F32)
            for k in range(taps):
                q, r = divmod(taps - 1 - k, SUBLANES)
                dv = dv + w_ref[pl.ds(k, 1), :] * ds[r, pl.ds(r0 + q * SUBLANES, CONV_CHUNK), :]
            av = a_ref[pl.ds(r0, CONV_CHUNK), :].astype(F32)
            sg = _sigmoid(g_ref[pl.ds(r0, CONV_CHUNK), :].astype(F32))
            o_ref[pl.ds(r0, CONV_CHUNK), pl.ds(0, cw)] = (dv * sg).astype(o_ref.dtype)
            o_ref[pl.ds(r0, CONV_CHUNK), pl.ds(cw, cw)] = (dv * av * sg * (1.0 - sg)).astype(o_ref.dtype)
            return carry

        lax.fori_loop(0, tt // CONV_CHUNK, chunk, 0)

    tile = lambda c: pl.BlockSpec((tt, cw), lambda i, c=c: (i, c))
    halo = lambda c: pl.BlockSpec((HALO, cw), lambda i, c=c: (jnp.maximum(i * hb - 1, 0), c))
    nxt = pl.BlockSpec((HALO, cw), lambda i: (jnp.minimum((i + 1) * hb, nt * hb - 1), 0))
    outs, carried = _call(
        body, grid=(nt,),
        in_specs=[pl.BlockSpec((tt, cw), lambda i: (i, 0)), nxt, tile(0), tile(1), halo(0), halo(1),
                  pl.BlockSpec((taps, cw), lambda i: (0, 0))],
        out_specs=[pl.BlockSpec((tt, 2 * cw), lambda i: (i, 0)),
                   pl.BlockSpec((taps, cw), lambda i: (0, 0)), pl.BlockSpec((1, cw), lambda i: (0, 0))],
        out_shape=[jax.ShapeDtypeStruct((rows, 2 * cw), BF16), jax.ShapeDtypeStruct((taps, cw), F32),
                   jax.ShapeDtypeStruct((1, cw), F32)],
        scratch_shapes=[pltpu.VMEM((SUBLANES, HALO + tt, cw), F32), pltpu.VMEM((SUBLANES, tt + HALO, cw), F32)],
        semantics=("arbitrary",), name="conv_bwd", args=[dcv, dcv, proj, proj, proj, proj, w_dw], carry=carry)
    return outs if carry is None else (outs, carried)


def _merge_fwd(proj, y_conv, ya, yb, *, cw):
    rows = proj.shape[0]
    tr = _row_tile(rows, cw, 4)

    def body(gc_ref, gs_ref, yc_ref, ya_ref, yb_ref, o_ref):
        ys = ya_ref[...].astype(F32) * _sigmoid(yb_ref[...].astype(F32))
        o_ref[...] = (_sigmoid(gc_ref[...].astype(F32)) * yc_ref[...].astype(F32)
                      + _sigmoid(gs_ref[...].astype(F32)) * ys).astype(o_ref.dtype)

    blk = lambda off: pl.BlockSpec((tr, cw), lambda i, h, off=off: (i, off + h))
    return pl.pallas_call(
        body, grid=(rows // tr, 2), in_specs=[blk(3), blk(5), blk(0), blk(0), blk(0)], out_specs=blk(0),
        out_shape=jax.ShapeDtypeStruct((rows, 2 * cw), BF16),
        compiler_params=_cparams(("parallel", "parallel")), name="merge_fwd")(proj, proj, y_conv, ya, yb)


def _merge_bwd(dmerged, proj, y_conv, ya, yb, *, cw, carry=None):
    rows = proj.shape[0]
    tr = _row_tile(rows, cw, 6)

    def body(d_ref, gc_ref, gs_ref, yc_ref, ya_ref, yb_ref, dgc_ref, dgs_ref, dyc_ref, dya_ref, dyb_ref):
        d = d_ref[...].astype(F32)
        sc = _sigmoid(gc_ref[...].astype(F32))
        ss = _sigmoid(gs_ref[...].astype(F32))
        sb = _sigmoid(yb_ref[...].astype(F32))
        yav = ya_ref[...].astype(F32)
        dgc_ref[...] = (d * yc_ref[...].astype(F32) * sc * (1.0 - sc)).astype(dgc_ref.dtype)
        dgs_ref[...] = (d * (yav * sb) * ss * (1.0 - ss)).astype(dgs_ref.dtype)
        dyc_ref[...] = (d * sc).astype(dyc_ref.dtype)
        dys = d * ss
        dya_ref[...] = (dys * sb).astype(dya_ref.dtype)
        dyb_ref[...] = (dys * yav * sb * (1.0 - sb)).astype(dyb_ref.dtype)

    blk = lambda off: pl.BlockSpec((tr, cw), lambda i, h, off=off: (i, off + h))
    o2 = jax.ShapeDtypeStruct((rows, 2 * cw), BF16)
    outs, carried = _call(
        body, grid=(rows // tr, 2),
        in_specs=[blk(0), blk(3), blk(5), blk(0), blk(0), blk(0)],
        out_specs=[blk(0), blk(0), blk(0), blk(0), blk(0)],
        out_shape=[o2, o2, o2, o2, o2], scratch_shapes=[],
        semantics=("parallel", "parallel"), name="merge_bwd", args=[dmerged, proj, proj, y_conv, ya, yb],
        carry=carry)
    return outs if carry is None else (outs, carried)


def _res_norm(x, mo, gate, g, scale, shift):
    rows, d = x.shape
    tr = _row_tile(rows, d, 4)

    def body(x_ref, mo_ref, gt_ref, g_ref, sc_ref, sh_ref, h_ref, z_ref):
        h = x_ref[...] + gt_ref[...] * mo_ref[...].astype(F32)
        h_ref[...] = h
        r = lax.rsqrt(jnp.mean(h * h, axis=-1, keepdims=True) + EPS)
        z_ref[...] = ((h * r * g_ref[...]) * (1.0 + sc_ref[...]) + sh_ref[...]).astype(z_ref.dtype)

    row = pl.BlockSpec((tr, d), lambda i: (i, 0))
    vec = pl.BlockSpec((1, d), lambda i: (0, 0))
    return pl.pallas_call(
        body, grid=(rows // tr,), in_specs=[row, row, vec, vec, vec, vec], out_specs=[row, row],
        out_shape=[jax.ShapeDtypeStruct((rows, d), F32), jax.ShapeDtypeStruct((rows, d), BF16)],
        compiler_params=_cparams(("parallel",)), name="res_norm")(x, mo, gate, g, scale, shift)


def _final_fwd_bwd(h1, ff, gate2, final_g, target):
    rows, d = h1.shape
    tr = _row_tile(rows, d, 5)

    def body(h_ref, ff_ref, gt_ref, fg_ref, t_ref, dh_ref, dff_ref, loss_ref, dfg_ref, dgt_ref):
        i = pl.program_id(0)

        @pl.when(i == 0)
        def _():
            loss_ref[...] = jnp.zeros_like(loss_ref)
            dfg_ref[...] = jnp.zeros_like(dfg_ref)
            dgt_ref[...] = jnp.zeros_like(dgt_ref)

        ffv = ff_ref[...].astype(F32)
        h2 = h_ref[...] + gt_ref[...] * ffv
        r = lax.rsqrt(jnp.mean(h2 * h2, axis=-1, keepdims=True) + EPS)
        y = h2 * r
        e = y * fg_ref[...] - t_ref[...]
        loss_ref[...] += 0.5 * jnp.sum(jnp.mean(e * e, axis=-1, keepdims=True))
        dout = e * (1.0 / d)
        dfg_ref[...] += jnp.sum(dout * y, axis=0, keepdims=True)
        dy = dout * fg_ref[...]
        dh2 = r * (dy - y * jnp.mean(dy * y, axis=-1, keepdims=True))
        dh_ref[...] = dh2
        dgt_ref[...] += jnp.sum(dh2 * ffv, axis=0, keepdims=True)
        dff_ref[...] = (dh2 * gt_ref[...]).astype(dff_ref.dtype)

    row = pl.BlockSpec((tr, d), lambda i: (i, 0))
    vec = pl.BlockSpec((1, d), lambda i: (0, 0))
    return pl.pallas_call(
        body, grid=(rows // tr,), in_specs=[row, row, vec, vec, row],
        out_specs=[row, row, pl.BlockSpec((1, LANES), lambda i: (0, 0)), vec, vec],
        out_shape=[jax.ShapeDtypeStruct((rows, d), F32), jax.ShapeDtypeStruct((rows, d), BF16),
                   jax.ShapeDtypeStruct((1, LANES), F32), jax.ShapeDtypeStruct((1, d), F32),
                   jax.ShapeDtypeStruct((1, d), F32)],
        compiler_params=_cparams(("arbitrary",)), name="final_fwd_bwd")(h1, ff, gate2, final_g, target)


def _norm_mod_bwd(dz, hin, dres, g, scale, gate, mo, *, name, carry=None):
    rows, d = hin.shape
    with_gate = gate is not None
    tr = _row_tile(rows, d, 6)

    def body(*refs):
        if with_gate:
            (dz_ref, h_ref, dr_ref, g_ref, sc_ref, gt_ref, mo_ref,
             dh_ref, dsh_ref, dsc_ref, dg_ref, dmo_ref, dgt_ref) = refs
        else:
            dz_ref, h_ref, dr_ref, g_ref, sc_ref, dh_ref, dsh_ref, dsc_ref, dg_ref = refs
        i = pl.program_id(0)

        @pl.when(i == 0)
        def _():
            dsh_ref[...] = jnp.zeros_like(dsh_ref)
            dsc_ref[...] = jnp.zeros_like(dsc_ref)
            dg_ref[...] = jnp.zeros_like(dg_ref)
            if with_gate:
                dgt_ref[...] = jnp.zeros_like(dgt_ref)

        dzv = dz_ref[...].astype(F32)
        h = h_ref[...]
        r = lax.rsqrt(jnp.mean(h * h, axis=-1, keepdims=True) + EPS)
        y = h * r
        dsh_ref[...] += jnp.sum(dzv, axis=0, keepdims=True)
        dsc_ref[...] += jnp.sum(dzv * (y * g_ref[...]), axis=0, keepdims=True)
        dn = dzv * (1.0 + sc_ref[...])
        dg_ref[...] += jnp.sum(dn * y, axis=0, keepdims=True)
        dy = dn * g_ref[...]
        dh = dr_ref[...] + r * (dy - y * jnp.mean(dy * y, axis=-1, keepdims=True))
        dh_ref[...] = dh
        if with_gate:
            dmo_ref[...] = (dh * gt_ref[...]).astype(dmo_ref.dtype)
            dgt_ref[...] += jnp.sum(dh * mo_ref[...].astype(F32), axis=0, keepdims=True)

    row = pl.BlockSpec((tr, d), lambda i: (i, 0))
    vec = pl.BlockSpec((1, d), lambda i: (0, 0))
    vshape = jax.ShapeDtypeStruct((1, d), F32)
    in_specs = [row, row, row, vec, vec]
    args = [dz, hin, dres, g, scale]
    out_specs = [row, vec, vec, vec]
    out_shape = [jax.ShapeDtypeStruct((rows, d), F32), vshape, vshape, vshape]
    if with_gate:
        in_specs += [vec, row]
        args += [gate, mo]
        out_specs += [row, vec]
        out_shape += [jax.ShapeDtypeStruct((rows, d), BF16), vshape]
    outs, carried = _call(
        body, grid=(rows // tr,), in_specs=in_specs, out_specs=out_specs, out_shape=out_shape,
        scratch_shapes=[], semantics=("arbitrary",), name=name, args=args, carry=carry)
    return outs if carry is None else (outs, carried)


def _s5_discretise(a_re, a_im, log_dt, b_re, b_im):
    dt = jnp.exp(log_dt)[:, None]
    er = jnp.exp(a_re * dt)
    lr = er * jnp.cos(a_im * dt)
    li = er * jnp.sin(a_im * dt)
    den = a_re * a_re + a_im * a_im
    cr = ((lr - 1.0) * a_re + li * a_im) / den
    ci = (li * a_re - (lr - 1.0) * a_im) / den
    bbr = cr[..., None] * b_re - ci[..., None] * b_im
    bbi = cr[..., None] * b_im + ci[..., None] * b_re
    return lr, li, bbr, bbi


def _block_diag(w):
    g, r, c = w.shape
    nb = g // GROUPS_PER_BLOCK
    eye = jnp.eye(GROUPS_PER_BLOCK, dtype=w.dtype)
    w5 = w.reshape(nb, GROUPS_PER_BLOCK, r, 1, c) * eye[None, :, None, :, None]
    return w5.reshape(nb, GROUPS_PER_BLOCK * r, GROUPS_PER_BLOCK * c)


def _block_diag_extract(m, r, c):
    nb = m.shape[0]
    m5 = m.reshape(nb, GROUPS_PER_BLOCK, r, GROUPS_PER_BLOCK, c)
    idx = jnp.arange(GROUPS_PER_BLOCK)
    d = m5[:, idx, :, idx, :]
    return jnp.moveaxis(d, 0, 1).reshape(nb * GROUPS_PER_BLOCK, r, c)


def _scan_multipliers(lr, li):
    power = jnp.arange(1, SUBLANES + 1, dtype=F32)[None, :, None]
    er = jnp.exp(power * lr)
    pr = er * jnp.cos(power * li)
    pi = er * jnp.sin(power * li)
    rows = jnp.arange(SUBLANES)[None, :, None]
    fr, fi, rr, ri = [], [], [], []
    for s in (1, 2, 4):
        mf = (rows >= s).astype(F32)
        mr = (rows <= SUBLANES - 1 - s).astype(F32)
        fr.append(mf * pr[:, s - 1:s, :])
        fi.append(mf * pi[:, s - 1:s, :])
        rr.append(mr * pr[:, s - 1:s, :])
        ri.append(mr * pi[:, s - 1:s, :])
    fr.append(pr)
    fi.append(pi)
    rr.append(pr[:, ::-1, :])
    ri.append(pi[:, ::-1, :])
    st = lambda xs: jnp.stack(xs, axis=1)
    return st(fr), st(fi), st(rr), st(ri)


def _scan_rows(sre, sim, mul_r, mul_i, n_groups, reverse):
    sgn = -1.0 if reverse else 1.0
    lanes = sre.shape[1]

    def step(k, carry):
        cr, ci = carry
        kk = (n_groups - 1 - k) if reverse else k
        r0 = pl.multiple_of(kk * SUBLANES, SUBLANES)
        xr = sre[pl.ds(r0, SUBLANES), :]
        xi = sim[pl.ds(r0, SUBLANES), :]
        for lvl, s in enumerate((1, 2, 4)):
            sh = (SUBLANES - s) if reverse else s
            nr = pltpu.roll(xr, sh, 0)
            ni = pltpu.roll(xi, sh, 0)
            mr = mul_r[lvl]
            mi = mul_i[lvl] * sgn
            xr, xi = xr + mr * nr - mi * ni, xi + mr * ni + mi * nr
        mr = mul_r[3]
        mi = mul_i[3] * sgn
        xr, xi = xr + mr * cr - mi * ci, xi + mr * ci + mi * cr
        sre[pl.ds(r0, SUBLANES), :] = xr
        sim[pl.ds(r0, SUBLANES), :] = xi
        edge = 0 if reverse else SUBLANES - 1
        ncr = jnp.broadcast_to(xr[edge:edge + 1, :], (SUBLANES, lanes))
        nci = jnp.broadcast_to(xi[edge:edge + 1, :], (SUBLANES, lanes))
        return ncr, nci

    zero = jnp.zeros((SUBLANES, lanes), F32)
    lax.fori_loop(0, n_groups, step, (zero, zero))


def _dot(a, b):
    return jnp.dot(a, b, preferred_element_type=F32)


def _dotf(a, b):
    return _dot(a.astype(BF16), b)


def _s5_operands(lr, li, bbr, bbi, c_re, c_im):
    g = lr.shape[0]
    nb = g // GROUPS_PER_BLOCK
    tb = lambda w: jnp.swapaxes(w, 1, 2)
    b_in = [_block_diag(tb(bbr)), _block_diag(tb(bbi))]
    c_out = [_block_diag(tb(c_re)), _block_diag(tb(c_im))]
    b_out = [_block_diag(bbr), _block_diag(bbi)]
    c_in = [_block_diag(c_re), _block_diag(c_im)]
    lam_r = lr.reshape(nb, 1, STATE_LANES)
    lam_i = li.reshape(nb, 1, STATE_LANES)
    mults = _scan_multipliers(lam_r, lam_i)
    cast = lambda ws: [w.astype(BF16) for w in ws]
    return cast(b_in), cast(c_out), cast(b_out), cast(c_in), mults


def _s5_fwd(proj, d_skip, b_in, c_out, mults, *, col0, carry=None):
    rows = proj.shape[0]
    nb = b_in[0].shape[0]
    tm = _pick(rows, 512, SUBLANES)
    n_tiles = rows // tm
    s_l = STATE_LANES

    def body(u_ref, dk_ref, br, bi, cr, ci, fr_ref, fi_ref, o_ref, sr_ref, si_ref, sre, sim):
        for t in range(n_tiles):
            rs = pl.ds(t * tm, tm)
            ub = u_ref[rs, :]
            sre[rs, :] = _dot(ub, br[...])
            sim[rs, :] = _dot(ub, bi[...])
        _scan_rows(sre, sim, fr_ref, fi_ref, rows // SUBLANES, False)
        for t in range(n_tiles):
            rs = pl.ds(t * tm, tm)
            srb = sre[rs, :].astype(BF16)
            sib = sim[rs, :].astype(BF16)
            sr_ref[rs, :] = srb
            si_ref[rs, :] = sib
            y0 = _dot(srb, cr[...]) - _dot(sib, ci[...])
            y1 = y0 + dk_ref[...] * u_ref[rs, :].astype(F32)
            o_ref[rs, :] = _gelu(y1).astype(o_ref.dtype)

    mat_in = pl.BlockSpec((None, LANES, s_l), lambda g: (g, 0, 0))
    mat_out = pl.BlockSpec((None, s_l, LANES), lambda g: (g, 0, 0))
    mul = pl.BlockSpec((None, 4, SUBLANES, s_l), lambda g: (g, 0, 0, 0))
    state = pl.BlockSpec((rows, s_l), lambda g: (0, g))
    outs, carried = _call(
        body, grid=(nb,),
        in_specs=[pl.BlockSpec((rows, LANES), lambda g: (0, col0 + g)), pl.BlockSpec((1, LANES), lambda g: (0, g))]
        + [mat_in] * 2 + [mat_out] * 2 + [mul] * 2,
        out_specs=[pl.BlockSpec((rows, LANES), lambda g: (0, g)), state, state],
        out_shape=[jax.ShapeDtypeStruct((rows, nb * LANES), BF16), jax.ShapeDtypeStruct((rows, nb * s_l), BF16),
                   jax.ShapeDtypeStruct((rows, nb * s_l), BF16)],
        scratch_shapes=[pltpu.VMEM((rows, s_l), F32), pltpu.VMEM((rows, s_l), F32)],
        semantics=("parallel",), name="s5_fwd", args=[proj, d_skip, *b_in, *c_out, mults[0], mults[1]], carry=carry)
    return outs if carry is None else (outs, carried)


def _s5_bwd(proj, dyg, d_skip, states, c_out, b_out, c_in, mults, *, col0, carry=None):
    rows = proj.shape[0]
    nb = c_out[0].shape[0]
    tm = _pick(rows, 512, SUBLANES)
    n_tiles = rows // tm
    s_l = STATE_LANES
    n_groups = rows // SUBLANES
    tn = (((0,), (0,)), ((), ()))

    def body(u_ref, dy_ref, dk_ref, sr_ref, si_ref, cr, ci, bor, boi, cir, cii, rr_ref, ri_ref,
             du_ref, ddk_ref, dbr_ref, dbi_ref, dcr_ref, dci_ref, dlr_ref, dli_ref,
             gre, gim, dy1):
        ddk = jnp.zeros((1, LANES), F32)
        dcr = jnp.zeros((s_l, LANES), F32)
        dci = jnp.zeros((s_l, LANES), F32)
        for t in range(n_tiles):
            rs = pl.ds(t * tm, tm)
            srb = sr_ref[rs, :]
            sib = si_ref[rs, :]
            uf = u_ref[rs, :].astype(F32)
            y0 = _dot(srb, cr[...]) - _dot(sib, ci[...])
            y1 = y0 + dk_ref[...] * uf
            d1 = dy_ref[rs, :].astype(F32) * _gelu_grad(y1)
            dy1[rs, :] = d1
            ddk = ddk + jnp.sum(d1 * uf, axis=0, keepdims=True)
            d1b = d1.astype(BF16)
            dcr = dcr + lax.dot_general(srb, d1b, tn, preferred_element_type=F32)
            dci = dci - lax.dot_general(sib, d1b, tn, preferred_element_type=F32)
            gre[rs, :] = _dot(d1b, cir[...])
            gim[rs, :] = -_dot(d1b, cii[...])
        ddk_ref[...] = ddk
        dcr_ref[...] = dcr
        dci_ref[...] = dci

        last_row = lax.broadcasted_iota(jnp.int32, (SUBLANES, s_l), 0) == SUBLANES - 1

        def group(r0, s_r, s_i, carry):
            cr_, ci_, ar, ai = carry
            xr = gre[pl.ds(r0, SUBLANES), :]
            xi = gim[pl.ds(r0, SUBLANES), :]
            for lvl, s in enumerate((1, 2, 4)):
                nr = pltpu.roll(xr, SUBLANES - s, 0)
                ni = pltpu.roll(xi, SUBLANES - s, 0)
                mr = rr_ref[lvl]
                mi = ri_ref[lvl]
                xr, xi = xr + mr * nr + mi * ni, xi + mr * ni - mi * nr
            mr = rr_ref[3]
            mi = ri_ref[3]
            xr, xi = xr + mr * cr_ + mi * ci_, xi + mr * ci_ - mi * cr_
            gre[pl.ds(r0, SUBLANES), :] = xr
            gim[pl.ds(r0, SUBLANES), :] = xi
            nxt_r = jnp.where(last_row, cr_, pltpu.roll(xr, SUBLANES - 1, 0))
            nxt_i = jnp.where(last_row, ci_, pltpu.roll(xi, SUBLANES - 1, 0))
            ncr = jnp.broadcast_to(xr[0:1, :], (SUBLANES, s_l))
            nci = jnp.broadcast_to(xi[0:1, :], (SUBLANES, s_l))
            return ncr, nci, ar + nxt_r * s_r + nxt_i * s_i, ai + nxt_i * s_r - nxt_r * s_i

        def rev_step(k, carry):
            r0 = pl.multiple_of((n_groups // 2 - 1 - k) * 2 * SUBLANES, 2 * SUBLANES)
            s_r = sr_ref[pl.ds(r0, 2 * SUBLANES), :].astype(F32)
            s_i = si_ref[pl.ds(r0, 2 * SUBLANES), :].astype(F32)
            carry = group(r0 + SUBLANES, s_r[SUBLANES:], s_i[SUBLANES:], carry)
            return group(r0, s_r[:SUBLANES], s_i[:SUBLANES], carry)

        zero = jnp.zeros((SUBLANES, s_l), F32)
        _, _, ar, ai = lax.fori_loop(0, n_groups // 2, rev_step, (zero, zero, zero, zero))
        dlr_ref[...] = jnp.sum(ar, axis=0, keepdims=True)
        dli_ref[...] = jnp.sum(ai, axis=0, keepdims=True)

        dbr = jnp.zeros((LANES, s_l), F32)
        dbi = jnp.zeros((LANES, s_l), F32)
        for t in range(n_tiles):
            rs = pl.ds(t * tm, tm)
            gr = gre[rs, :]
            gi = gim[rs, :]
            grb = gr.astype(BF16)
            gib = gi.astype(BF16)
            du = _dot(grb, bor[...]) + _dot(gib, boi[...]) + dy1[rs, :] * dk_ref[...]
            du_ref[rs, :] = du.astype(du_ref.dtype)
            ub = u_ref[rs, :]
            dbr = dbr + lax.dot_general(ub, grb, tn, preferred_element_type=F32)
            dbi = dbi + lax.dot_general(ub, gib, tn, preferred_element_type=F32)
        dbr_ref[...] = dbr
        dbi_ref[...] = dbi

    mat_in = pl.BlockSpec((None, LANES, s_l), lambda g: (g, 0, 0))
    mat_out = pl.BlockSpec((None, s_l, LANES), lambda g: (g, 0, 0))
    mul = pl.BlockSpec((None, 4, SUBLANES, s_l), lambda g: (g, 0, 0, 0))
    lam = pl.BlockSpec((None, 1, s_l), lambda g: (g, 0, 0))
    col = pl.BlockSpec((rows, LANES), lambda g: (0, g))
    vec = pl.BlockSpec((1, LANES), lambda g: (0, g))
    state = pl.BlockSpec((rows, s_l), lambda g: (0, g))
    outs, carried = _call(
        body, grid=(nb,),
        in_specs=[pl.BlockSpec((rows, LANES), lambda g: (0, col0 + g)), col, vec]
        + [state] * 2 + [mat_out] * 2 + [mat_out] * 2 + [mat_in] * 2 + [mul] * 2,
        out_specs=[col, vec, mat_in, mat_in, mat_out, mat_out, lam, lam],
        out_shape=[jax.ShapeDtypeStruct((rows, nb * LANES), BF16), jax.ShapeDtypeStruct((1, nb * LANES), F32),
                   jax.ShapeDtypeStruct((nb, LANES, s_l), F32), jax.ShapeDtypeStruct((nb, LANES, s_l), F32),
                   jax.ShapeDtypeStruct((nb, s_l, LANES), F32), jax.ShapeDtypeStruct((nb, s_l, LANES), F32),
                   jax.ShapeDtypeStruct((nb, 1, s_l), F32), jax.ShapeDtypeStruct((nb, 1, s_l), F32)],
        scratch_shapes=[pltpu.VMEM((rows, s_l), F32)] * 2 + [pltpu.VMEM((rows, LANES), F32)],
        semantics=("parallel",), name="s5_bwd",
        args=[proj, dyg, d_skip, *states, *c_out, *b_out, *c_in, mults[2], mults[3]], carry=carry)
    return outs if carry is None else (outs, carried)


def _silu(v):
    return v * _sigmoid(v)


def _ada_fwd(c_all, w_shard, b_cols):
    d, n = w_shard.shape
    bn = _pick(n, 512)

    def body(c_ref, w_ref, b_ref, o_ref):
        ca = _silu(c_ref[...]).astype(BF16)
        o_ref[...] = _dot(ca, w_ref[...].astype(BF16)) + b_ref[...]

    return pl.pallas_call(
        body, grid=(n // bn,),
        in_specs=[pl.BlockSpec((N_DEV, d), lambda j: (0, 0)), pl.BlockSpec((d, bn), lambda j: (0, j)),
                  pl.BlockSpec((1, bn), lambda j: (0, j))],
        out_specs=pl.BlockSpec((N_DEV, bn), lambda j: (0, j)),
        out_shape=jax.ShapeDtypeStruct((N_DEV, n), F32),
        compiler_params=_cparams(("parallel",)), name="ada_fwd")(c_all, w_shard, b_cols)


def _ada_bwd(c_all, dmod_cols):
    d = c_all.shape[1]
    n = dmod_cols.shape[1]
    bn = _pick(n, 512)

    def body(c_ref, g_ref, o_ref):
        ca = _silu(c_ref[...]).astype(BF16)
        o_ref[...] = lax.dot_general(ca, g_ref[...].astype(BF16), (((0,), (0,)), ((), ())),
                                     preferred_element_type=F32)

    return pl.pallas_call(
        body, grid=(n // bn,),
        in_specs=[pl.BlockSpec((N_DEV, d), lambda j: (0, 0)), pl.BlockSpec((N_DEV, bn), lambda j: (0, j))],
        out_specs=pl.BlockSpec((d, bn), lambda j: (0, j)),
        out_shape=jax.ShapeDtypeStruct((d, n), F32),
        compiler_params=_cparams(("parallel",)), name="ada_bwd")(c_all, dmod_cols)


def _cast_bf16(w, *, name):
    rows, cols = w.shape
    tr = _row_tile(rows, cols, 2)

    def body(w_ref, o_ref):
        o_ref[...] = w_ref[...].astype(BF16)

    row = pl.BlockSpec((tr, cols), lambda i: (i, 0))
    return pl.pallas_call(
        body, grid=(rows // tr,), in_specs=[row], out_specs=row,
        out_shape=jax.ShapeDtypeStruct((rows, cols), BF16),
        compiler_params=_cparams(("parallel",)), name=name)(w)


def _adamw(w, g, m, v, *, name, carry=None):
    rows, cols = w.shape
    tr = _row_tile(rows, cols, 7)
    c1 = 1.0 / (1.0 - ADAM_B1 ** ADAM_STEP)
    c2 = 1.0 / (1.0 - ADAM_B2 ** ADAM_STEP)

    def body(w_ref, g_ref, m_ref, v_ref, d_ref, nm_ref, nv_ref):
        gv = g_ref[...]
        nm = ADAM_B1 * m_ref[...] + (1.0 - ADAM_B1) * gv
        nv = ADAM_B2 * v_ref[...] + (1.0 - ADAM_B2) * (gv * gv)
        nm_ref[...] = nm
        nv_ref[...] = nv
        d_ref[...] = -ADAM_LR * ((nm * c1) / (jnp.sqrt(nv * c2) + ADAM_EPS) + ADAM_WD * w_ref[...])

    row = pl.BlockSpec((tr, cols), lambda i: (i, 0))
    shp = jax.ShapeDtypeStruct((rows, cols), F32)
    outs, carried = _call(
        body, grid=(rows // tr,), in_specs=[row] * 4, out_specs=[row] * 3, out_shape=[shp] * 3,
        scratch_shapes=[], semantics=("parallel",), name=name, args=[w, g, m, v], carry=carry)
    return outs if carry is None else (outs, carried)


def _sum_leading(a, *, name, out_dtype=F32):
    n, rows, cols = a.shape
    tr = _row_tile(rows, cols, n + 1)

    def body(a_ref, o_ref):
        acc = a_ref[0].astype(F32)
        for i in range(1, n):
            acc = acc + a_ref[i].astype(F32)
        o_ref[...] = acc.astype(o_ref.dtype)

    return pl.pallas_call(
        body, grid=(rows // tr,), in_specs=[pl.BlockSpec((n, tr, cols), lambda i: (0, i, 0))],
        out_specs=pl.BlockSpec((tr, cols), lambda i: (i, 0)),
        out_shape=jax.ShapeDtypeStruct((rows, cols), out_dtype),
        compiler_params=_cparams(("parallel",)), name=name)(a)


def _add_half(dw, land, my_c, *, name):
    n, r, cols = dw.shape
    h = r // 2
    tr = _row_tile(h, cols, 3)
    hb = h // tr

    def body(c_ref, a_ref, b_ref, o_ref):
        o_ref[...] = (a_ref[...].astype(F32) + b_ref[...].astype(F32)).astype(o_ref.dtype)

    gs = pltpu.PrefetchScalarGridSpec(
        num_scalar_prefetch=1, grid=(n, hb),
        in_specs=[pl.BlockSpec((None, tr, cols), lambda s, i, c_ref: (s, c_ref[0] * hb + i, 0)),
                  pl.BlockSpec((None, tr, cols), lambda s, i, c_ref: (s, i, 0))],
        out_specs=pl.BlockSpec((None, tr, cols), lambda s, i, c_ref: (s, i, 0)))
    return pl.pallas_call(
        body, grid_spec=gs, out_shape=jax.ShapeDtypeStruct((n, h, cols), BF16),
        compiler_params=_cparams(("parallel", "parallel")), name=name)(my_c, dw, land)


def _mesh_pos():
    return lax.axis_index("x"), lax.axis_index("y"), lax.axis_index("c")


def _other_chips(x, y):
    return [(1 - x, y), (x, 1 - y), (1 - x, 1 - y)]


def _gather_small(blk, *, name):
    m_per, n = blk.shape

    def body(x_ref, out_ref, send_sems, recv_sems, local_sem):
        x, y, c = _mesh_pos()
        me, sibling = (x, y, c), (x, y, 1 - c)
        chips = _other_chips(x, y)

        def rows(px, py, pc):
            return out_ref.at[pl.ds((4 * px + 2 * py + pc) * m_per, m_per), :]

        def copy(k, block, to, src=None):
            return pltpu.make_async_remote_copy(
                src_ref=rows(*block) if src is None else src, dst_ref=rows(*block),
                send_sem=send_sems.at[k], recv_sem=recv_sems.at[k], device_id=to, device_id_type=MESH)

        mine = pltpu.make_async_copy(x_ref, rows(*me), local_sem)
        mine.start()
        first = [copy(0, me, sibling, src=x_ref)]
        first += [copy(1 + j, me, (*chip, c), src=x_ref) for j, chip in enumerate(chips)]
        for cp in first:
            cp.start()
        passed = [copy(4 + j, (*chip, c), sibling) for j, chip in enumerate(chips)]
        for j, chip in enumerate(chips):
            copy(1 + j, (*chip, c), me).wait_recv()
            passed[j].start()
        copy(0, sibling, me).wait_recv()
        for j, chip in enumerate(chips):
            copy(4 + j, (*chip, 1 - c), me).wait_recv()
        for cp in first + passed:
            cp.wait_send()
        mine.wait()

    return pl.pallas_call(
        body, out_shape=jax.ShapeDtypeStruct((N_DEV * m_per, n), blk.dtype),
        in_specs=[pl.BlockSpec(memory_space=pltpu.VMEM)], out_specs=pl.BlockSpec(memory_space=pltpu.VMEM),
        scratch_shapes=[pltpu.SemaphoreType.DMA((7,)), pltpu.SemaphoreType.DMA((7,)), pltpu.SemaphoreType.DMA],
        compiler_params=pltpu.CompilerParams(vmem_limit_bytes=VMEM_LIMIT_BYTES), name=name)(blk)


def _hbm_specs(n):
    return [pl.BlockSpec(memory_space=pl.ANY)] * n


def _gather_weights(shards):
    n = len(shards)

    def body(*refs):
        ins, outs = refs[:n], refs[n:2 * n]
        send_sems, recv_sems, local_sems = refs[2 * n:]
        x, y, c = _mesh_pos()
        me_chip = 2 * x + y
        sibling = (x, y, 1 - c)
        chips = _other_chips(x, y)

        def half(w, chip_idx, pc):
            h = shards[w].shape[0] // 2
            return outs[w].at[chip_idx, pl.ds(pc * h, h), :]

        def copy(w, k, chip_idx, pc, to, src=None):
            dst = half(w, chip_idx, pc)
            return pltpu.make_async_remote_copy(
                src_ref=dst if src is None else src, dst_ref=dst,
                send_sem=send_sems.at[6 * w + k], recv_sem=recv_sems.at[6 * w + k],
                device_id=to, device_id_type=MESH)

        local = [pltpu.make_async_copy(ins[w], outs[w].at[me_chip], local_sems.at[w]) for w in range(n)]
        for cp in local:
            cp.start()
        sends = []
        for w in range(n):
            h = shards[w].shape[0] // 2
            for j, chip in enumerate(chips):
                cp = copy(w, j, me_chip, c, (*chip, c), src=ins[w].at[pl.ds(c * h, h), :])
                cp.start()
                sends.append(cp)
        for w in range(n):
            for j, chip in enumerate(chips):
                chip_idx = 2 * chip[0] + chip[1]
                copy(w, j, chip_idx, c, (x, y, c)).wait_recv()
                cp = copy(w, 3 + j, chip_idx, c, sibling)
                cp.start()
                sends.append(cp)
        for w in range(n):
            for j, chip in enumerate(chips):
                copy(w, 3 + j, 2 * chip[0] + chip[1], 1 - c, (x, y, c)).wait_recv()
        for cp in sends:
            cp.wait_send()
        for cp in local:
            cp.wait()

    return pl.pallas_call(
        body, out_shape=[jax.ShapeDtypeStruct((N_CHIPS,) + s.shape, s.dtype) for s in shards],
        in_specs=_hbm_specs(n), out_specs=_hbm_specs(n),
        scratch_shapes=[pltpu.SemaphoreType.DMA((6 * n,)), pltpu.SemaphoreType.DMA((6 * n,)),
                        pltpu.SemaphoreType.DMA((n,))],
        name="gather_weights")(*shards)


def _swap_halves(dws, *, name):
    n = len(dws)

    def body(*refs):
        ins, outs = refs[:n], refs[n:2 * n]
        send_sems, recv_sems = refs[2 * n:]
        x, y, c = _mesh_pos()
        cps = []
        for w in range(n):
            h = dws[w].shape[1] // 2
            cp = pltpu.make_async_remote_copy(
                src_ref=ins[w].at[:, pl.ds((1 - c) * h, h), :], dst_ref=outs[w],
                send_sem=send_sems.at[w], recv_sem=recv_sems.at[w],
                device_id=(x, y, 1 - c), device_id_type=MESH)
            cp.start()
            cps.append(cp)
        for cp in cps:
            cp.wait()

    return pl.pallas_call(
        body, out_shape=[jax.ShapeDtypeStruct((s.shape[0], s.shape[1] // 2, s.shape[2]), s.dtype) for s in dws],
        in_specs=_hbm_specs(n), out_specs=_hbm_specs(n),
        scratch_shapes=[pltpu.SemaphoreType.DMA((n,)), pltpu.SemaphoreType.DMA((n,))],
        name=name)(*dws)


def _chip_exchange(parts):
    n = len(parts)

    def body(*refs):
        ins, outs = refs[:n], refs[n:2 * n]
        send_sems, recv_sems, local_sems = refs[2 * n:]
        x, y, c = _mesh_pos()
        me_chip = 2 * x + y
        chips = _other_chips(x, y)
        local = [pltpu.make_async_copy(ins[w].at[me_chip], outs[w].at[me_chip], local_sems.at[w]) for w in range(n)]
        for cp in local:
            cp.start()
        cps = []
        for w in range(n):
            for j, chip in enumerate(chips):
                cp = pltpu.make_async_remote_copy(
                    src_ref=ins[w].at[2 * chip[0] + chip[1]], dst_ref=outs[w].at[me_chip],
                    send_sem=send_sems.at[3 * w + j], recv_sem=recv_sems.at[3 * w + j],
                    device_id=(*chip, c), device_id_type=MESH)
                cp.start()
                cps.append((cp, w, j, chip))
        for cp, w, j, chip in cps:
            slot = outs[w].at[2 * chip[0] + chip[1]]
            pltpu.make_async_remote_copy(
                src_ref=slot, dst_ref=slot, send_sem=send_sems.at[3 * w + j], recv_sem=recv_sems.at[3 * w + j],
                device_id=(x, y, c), device_id_type=MESH).wait_recv()
        for cp, _, _, _ in cps:
            cp.wait_send()
        for cp in local:
            cp.wait()

    return pl.pallas_call(
        body, out_shape=[jax.ShapeDtypeStruct(s.shape, s.dtype) for s in parts],
        in_specs=_hbm_specs(n), out_specs=_hbm_specs(n),
        scratch_shapes=[pltpu.SemaphoreType.DMA((3 * n,)), pltpu.SemaphoreType.DMA((3 * n,)),
                        pltpu.SemaphoreType.DMA((n,))],
        name="chip_exchange")(*parts)


def _join_halves(halves):
    n = len(halves)

    def body(*refs):
        ins, outs = refs[:n], refs[n:2 * n]
        send_sems, recv_sems, local_sems = refs[2 * n:]
        x, y, c = _mesh_pos()
        cps, local = [], []
        for w in range(n):
            h = halves[w].shape[0]
            mine = outs[w].at[pl.ds(c * h, h), :]
            lc = pltpu.make_async_copy(ins[w], mine, local_sems.at[w])
            lc.start()
            local.append(lc)
            cp = pltpu.make_async_remote_copy(
                src_ref=ins[w], dst_ref=mine, send_sem=send_sems.at[w], recv_sem=recv_sems.at[w],
                device_id=(x, y, 1 - c), device_id_type=MESH)
            cp.start()
            cps.append(cp)
        for w in range(n):
            h = halves[w].shape[0]
            theirs = outs[w].at[pl.ds((1 - c) * h, h), :]
            pltpu.make_async_remote_copy(
                src_ref=theirs, dst_ref=theirs, send_sem=send_sems.at[w], recv_sem=recv_sems.at[w],
                device_id=(x, y, c), device_id_type=MESH).wait_recv()
        for cp in cps:
            cp.wait_send()
        for lc in local:
            lc.wait()

    return pl.pallas_call(
        body, out_shape=[jax.ShapeDtypeStruct((2 * s.shape[0], s.shape[1]), s.dtype) for s in halves],
        in_specs=_hbm_specs(n), out_specs=_hbm_specs(n),
        scratch_shapes=[pltpu.SemaphoreType.DMA((n,)), pltpu.SemaphoreType.DMA((n,)), pltpu.SemaphoreType.DMA((n,))],
        name="join_halves")(*halves)


def _cast_into_slot(w, chip, *, name):
    rows, cols = w.shape
    tr = _row_tile(rows, cols, 2)

    def body(chip_ref, w_ref, o_ref):
        o_ref[...] = w_ref[...].astype(BF16)

    gs = pltpu.PrefetchScalarGridSpec(
        num_scalar_prefetch=1, grid=(rows // tr,),
        in_specs=[pl.BlockSpec((tr, cols), lambda i, chip_ref: (i, 0))],
        out_specs=pl.BlockSpec((None, tr, cols), lambda i, chip_ref: (chip_ref[0], i, 0)))
    return pl.pallas_call(
        body, grid_spec=gs, out_shape=jax.ShapeDtypeStruct((N_CHIPS, rows, cols), BF16),
        compiler_params=_cparams(("parallel",)), name=name)(chip, w)


def _row_range(h, lo, hi, parts):
    step = h // parts
    assert step * parts == h and step % (2 * SUBLANES) == 0, (h, parts)
    return lo * step, (hi - lo) * step


def _gather_carry(items):
    n_copies = sum(len(js) for _, js, _, _, _ in items)
    sem = pltpu.SemaphoreType.DMA((2 * n_copies,))

    def copies(outs, sems):
        send_sems, recv_sems = sems
        x, y, c = _mesh_pos()
        me_chip = 2 * x + y
        chips = _other_chips(x, y)
        out_ici, in_ici, out_d2d, in_d2d = [], [], [], []
        k = 0
        for w, (buf, js, lo, hi, parts) in enumerate(items):
            h = buf.shape[1] // 2
            r0, nr = _row_range(h, lo, hi, parts)

            def copy(k, chip_idx, pc, to):
                ref = outs[w].at[chip_idx, pl.ds(pc * h + r0, nr), :]
                return pltpu.make_async_remote_copy(
                    src_ref=ref, dst_ref=ref, send_sem=send_sems.at[k], recv_sem=recv_sems.at[k],
                    device_id=to, device_id_type=MESH)

            for j in js:
                chip = chips[j]
                chip_idx = 2 * chip[0] + chip[1]
                out_ici.append(copy(k, me_chip, c, (*chip, c)))
                in_ici.append(copy(k, chip_idx, c, (x, y, c)))
                out_d2d.append(copy(k + 1, chip_idx, c, (x, y, 1 - c)))
                in_d2d.append(copy(k + 1, chip_idx, 1 - c, (x, y, c)))
                k += 2
        return out_ici, in_ici, out_d2d, in_d2d

    def start(ins, outs, sems):
        for cp in copies(outs, sems)[0]:
            cp.start()

    def finish(ins, outs, sems):
        out_ici, in_ici, out_d2d, in_d2d = copies(outs, sems)
        for arrived, onward in zip(in_ici, out_d2d):
            arrived.wait_recv()
            onward.start()
        for arrived in in_d2d:
            arrived.wait_recv()
        for cp in out_ici + out_d2d:
            cp.wait_send()

    bufs = [it[0] for it in items]
    shapes = [jax.ShapeDtypeStruct(b.shape, b.dtype) for b in bufs]
    return _Carry(bufs, shapes, {i: i for i in range(len(bufs))}, [sem, sem], start, finish)


def _exchange_carry(items):
    n = len(items)
    sem = pltpu.SemaphoreType.DMA((3 * n,))
    given = [w for w in range(n) if items[w][1] is not None]

    def copies(ins, outs, sems):
        send_sems, recv_sems = sems
        x, y, c = _mesh_pos()
        chips = _other_chips(x, y)
        sends, recvs = [], []
        for w, (part, _, lo, hi, parts) in enumerate(items):
            r0, nr = _row_range(part.shape[1], lo, hi, parts)
            for j, chip in enumerate(chips):
                land = outs[w].at[j, pl.ds(r0, nr), :]
                sends.append(pltpu.make_async_remote_copy(
                    src_ref=ins[w].at[2 * chip[0] + chip[1], pl.ds(r0, nr), :], dst_ref=land,
                    send_sem=send_sems.at[3 * w + j], recv_sem=recv_sems.at[3 * w + j],
                    device_id=(*chip, c), device_id_type=MESH))
                recvs.append(pltpu.make_async_remote_copy(
                    src_ref=land, dst_ref=land,
                    send_sem=send_sems.at[3 * w + j], recv_sem=recv_sems.at[3 * w + j],
                    device_id=(x, y, c), device_id_type=MESH))
        return sends, recvs

    def start(ins, outs, sems):
        for cp in copies(ins, outs, sems)[0]:
            cp.start()

    def finish(ins, outs, sems):
        sends, recvs = copies(ins, outs, sems)
        for cp in recvs:
            cp.wait_recv()
        for cp in sends:
            cp.wait_send()

    inputs = [it[0] for it in items] + [items[w][1] for w in given]
    shapes = [jax.ShapeDtypeStruct((3,) + it[0].shape[1:], it[0].dtype) for it in items]
    aliases = {n + i: w for i, w in enumerate(given)}
    return _Carry(inputs, shapes, aliases, [sem, sem], start, finish)


def _sum_into_half(part, landed, chip, my_c, *, name):
    _, h, cols = part.shape
    tr = _row_tile(h, cols, 5)
    hb = h // tr

    def body(chip_ref, c_ref, p_ref, l_ref, o_ref):
        acc = p_ref[...].astype(F32)
        for j in range(3):
            acc = acc + l_ref[j].astype(F32)
        o_ref[...] = acc

    gs = pltpu.PrefetchScalarGridSpec(
        num_scalar_prefetch=2, grid=(hb,),
        in_specs=[pl.BlockSpec((None, tr, cols), lambda i, chip_ref, c_ref: (chip_ref[0], i, 0)),
                  pl.BlockSpec((3, tr, cols), lambda i, chip_ref, c_ref: (0, i, 0))],
        out_specs=pl.BlockSpec((tr, cols), lambda i, chip_ref, c_ref: (c_ref[0] * hb + i, 0)))
    return pl.pallas_call(
        body, grid_spec=gs, out_shape=jax.ShapeDtypeStruct((2 * h, cols), F32),
        compiler_params=_cparams(("parallel",)), name=name)(chip, my_c, part, landed)


def _join_carry(fulls):
    n = len(fulls)
    sem = pltpu.SemaphoreType.DMA((n,))

    def copies(outs, sems):
        send_sems, recv_sems = sems
        x, y, c = _mesh_pos()
        sends, recvs = [], []
        for w in range(n):
            h = fulls[w].shape[0] // 2
            mine = outs[w].at[pl.ds(c * h, h), :]
            theirs = outs[w].at[pl.ds((1 - c) * h, h), :]
            sends.append(pltpu.make_async_remote_copy(
                src_ref=mine, dst_ref=mine, send_sem=send_sems.at[w], recv_sem=recv_sems.at[w],
                device_id=(x, y, 1 - c), device_id_type=MESH))
            recvs.append(pltpu.make_async_remote_copy(
                src_ref=theirs, dst_ref=theirs, send_sem=send_sems.at[w], recv_sem=recv_sems.at[w],
                device_id=(x, y, c), device_id_type=MESH))
        return sends, recvs

    def start(ins, outs, sems):
        for cp in copies(outs, sems)[0]:
            cp.start()

    def finish(ins, outs, sems):
        sends, recvs = copies(outs, sems)
        for cp in recvs:
            cp.wait_recv()
        for cp in sends:
            cp.wait_send()

    shapes = [jax.ShapeDtypeStruct(f.shape, f.dtype) for f in fulls]
    return _Carry(fulls, shapes, {i: i for i in range(n)}, [sem, sem], start, finish)


class _NoComm:
    def __init__(self, big):
        self.big = big
        self.grads = {}

    def weight(self, name):
        return self.big[name]

    def mm_in(self, u):
        return _mm(u, self.big["w_in"], mode="nn", out_dtype=BF16, name="mm_in")

    def mm_d_in(self, dproj):
        return _mm(dproj, self.big["w_in"], mode="nt", out_dtype=F32, name="mm_d_in")

    def carry(self, site):
        return None

    def done(self, site, carried):
        pass

    def grad(self, name, dw):
        self.grads[name] = dw

    def early_grads(self, early):
        self.early = early


def _gather_rows_carry(blk):
    m_per = blk.shape[0]
    sem = pltpu.SemaphoreType.DMA((7,))

    def copies(ins, outs, sems):
        send_sems, recv_sems, local_sem = sems
        x, y, c = _mesh_pos()
        me, sibling = (x, y, c), (x, y, 1 - c)
        chips = _other_chips(x, y)

        def rows(px, py, pc):
            return outs[0].at[pl.ds((4 * px + 2 * py + pc) * m_per, m_per), :]

        def copy(k, block, to, src=None):
            return pltpu.make_async_remote_copy(
                src_ref=rows(*block) if src is None else src, dst_ref=rows(*block),
                send_sem=send_sems.at[k], recv_sem=recv_sems.at[k], device_id=to, device_id_type=MESH)

        mine = pltpu.make_async_copy(ins[0], rows(*me), local_sem.at[0])
        first = [copy(0, me, sibling, src=ins[0])]
        first += [copy(1 + j, me, (*chip, c), src=ins[0]) for j, chip in enumerate(chips)]
        passed = [copy(4 + j, (*chip, c), sibling) for j, chip in enumerate(chips)]
        landed = [copy(1 + j, (*chip, c), me) for j, chip in enumerate(chips)]
        from_sibling = [copy(0, sibling, me)] + [copy(4 + j, (*chip, 1 - c), me) for j, chip in enumerate(chips)]
        return mine, first, passed, landed, from_sibling

    def start(ins, outs, sems):
        mine, first, _, _, _ = copies(ins, outs, sems)
        mine.start()
        for cp in first:
            cp.start()

    def finish(ins, outs, sems):
        mine, first, passed, landed, from_sibling = copies(ins, outs, sems)
        for arrived, onward in zip(landed, passed):
            arrived.wait_recv()
            onward.start()
        for arrived in from_sibling:
            arrived.wait_recv()
        for cp in first + passed:
            cp.wait_send()
        mine.wait()

    shape = jax.ShapeDtypeStruct((N_DEV * m_per, blk.shape[1]), blk.dtype)
    return _Carry([blk], [shape], {}, [sem, sem, pltpu.SemaphoreType.DMA((1,))], start, finish)


def _gather_fresh_carry(own, js):
    n = len(js)
    h = own.shape[0] // 2
    sem = pltpu.SemaphoreType.DMA((2 * n,))

    def copies(ins, outs, sems):
        send_sems, recv_sems = sems
        x, y, c = _mesh_pos()
        chips = _other_chips(x, y)
        out_ici, in_ici, out_d2d, in_d2d = [], [], [], []

        def copy(k, src, dst, to):
            return pltpu.make_async_remote_copy(
                src_ref=src, dst_ref=dst, send_sem=send_sems.at[k], recv_sem=recv_sems.at[k],
                device_id=to, device_id_type=MESH)

        for jj, j in enumerate(js):
            mine = ins[0].at[pl.ds(c * h, h), :]
            land = outs[0].at[jj, pl.ds(c * h, h), :]
            other = outs[0].at[jj, pl.ds((1 - c) * h, h), :]
            out_ici.append(copy(2 * jj, mine, land, (*chips[j], c)))
            in_ici.append(copy(2 * jj, land, land, (x, y, c)))
            out_d2d.append(copy(2 * jj + 1, land, land, (x, y, 1 - c)))
            in_d2d.append(copy(2 * jj + 1, other, other, (x, y, c)))
        return out_ici, in_ici, out_d2d, in_d2d

    def start(ins, outs, sems):
        for cp in copies(ins, outs, sems)[0]:
            cp.start()

    def finish(ins, outs, sems):
        out_ici, in_ici, out_d2d, in_d2d = copies(ins, outs, sems)
        for arrived, onward in zip(in_ici, out_d2d):
            arrived.wait_recv()
            onward.start()
        for arrived in in_d2d:
            arrived.wait_recv()
        for cp in out_ici + out_d2d:
            cp.wait_send()

    return _Carry([own], [jax.ShapeDtypeStruct((n,) + own.shape, own.dtype)], {}, [sem, sem], start, finish)


def _swap_carry(dws):
    n = len(dws)
    sem = pltpu.SemaphoreType.DMA((n,))

    def copies(ins, outs, sems):
        send_sems, recv_sems = sems
        x, y, c = _mesh_pos()
        cps = []
        for w in range(n):
            h = dws[w].shape[1] // 2
            cps.append(pltpu.make_async_remote_copy(
                src_ref=ins[w].at[:, pl.ds((1 - c) * h, h), :], dst_ref=outs[w],
                send_sem=send_sems.at[w], recv_sem=recv_sems.at[w],
                device_id=(x, y, 1 - c), device_id_type=MESH))
        return cps

    def start(ins, outs, sems):
        for cp in copies(ins, outs, sems):
            cp.start()

    def finish(ins, outs, sems):
        for cp in copies(ins, outs, sems):
            cp.wait()

    shapes = [jax.ShapeDtypeStruct((s.shape[0], s.shape[1] // 2, s.shape[2]), s.dtype) for s in dws]
    return _Carry(dws, shapes, {}, [sem, sem], start, finish)


def _merge_carries(carries):
    if len(carries) == 1:
        return carries[0]
    inputs, out_shapes, sem_shapes, aliases, spans = [], [], [], {}, []
    for cy in carries:
        i0, o0, s0 = len(inputs), len(out_shapes), len(sem_shapes)
        aliases.update({i0 + i: o0 + o for i, o in cy.aliases.items()})
        inputs += cy.inputs
        out_shapes += cy.out_shapes
        sem_shapes += cy.sem_shapes
        spans.append((slice(i0, len(inputs)), slice(o0, len(out_shapes)), slice(s0, len(sem_shapes))))

    def start(ins, outs, sems):
        for cy, (si, so, ss) in zip(carries, spans):
            cy.start(ins[si], outs[so], sems[ss])

    def finish(ins, outs, sems):
        for cy, (si, so, ss) in zip(carries, spans):
            cy.finish(ins[si], outs[so], sems[ss])

    return _Carry(inputs, out_shapes, aliases, sem_shapes, start, finish)


ALL_CHIPS = (0, 1, 2)


class _MeshComm:
    GATHER_AT = {
        "mm_in_diag": [("w_conv_out", ALL_CHIPS, 0, 1, 1)],
        "conv_fwd": [("w_glu_a", ALL_CHIPS, 0, 1, 1), ("w_glu_b", ALL_CHIPS, 0, 1, 1), ("w_out", ALL_CHIPS, 0, 1, 1)],
        "s5_fwd": [("w_ff1", ALL_CHIPS, 0, 6, 8)],
        "mm_glu_a": [("w_ff1", ALL_CHIPS, 6, 7, 8)],
        "mm_glu_b": [("w_ff1", ALL_CHIPS, 7, 8, 8)],
        "mm_out": [("w_ff2", ALL_CHIPS, 0, 2, 8)],
        "mm_ff1": [("w_ff2", ALL_CHIPS, 2, 8, 8)],
    }
    SWAP_AT = {
        "mm_d_ff2": ["w_ff2"],
        "mm_d_ff1": ["w_ff1"],
        "conv_bwd": ["w_out", "w_glu_a", "w_glu_b", "w_conv_out"],
        "mm_d_in": ["w_in"],
    }
    EXCHANGE_AT = {
        "mm_dw_ff1": [("w_ff2", 0, 6, 8)],
        "mm_d_ff1": [("w_ff2", 6, 8, 8)],
        "norm2_bwd": [("w_ff1", 0, 3, 8)],
        "mm_dw_out": [("w_ff1", 3, 5, 8)],
        "mm_d_out": [("w_ff1", 5, 7, 8)],
        "merge_bwd": [("w_ff1", 7, 8, 8)],
        "s5_bwd": [("w_out", 0, 1, 1), ("w_glu_a", 0, 1, 1), ("w_glu_b", 0, 1, 1), ("w_conv_out", 0, 1, 1)],
        "norm1_bwd": [("w_in", 0, 3, 8)],
        "adamw_w_ada": [("w_in", 3, 6, 8)],
        "adamw_w_ff2": [("w_in", 6, 8, 8)],
    }
    EARLY_AT = "mm_dw_in"

    def __init__(self, shards, pos, chip, my_c):
        self.pos = pos
        self.chip = chip
        self.my_c = my_c
        self.w_in_own = _cast_bf16(shards["w_in"], name="cast_w_in")
        self.bufs = {n: _cast_into_slot(s, chip, name="cast_" + n) for n, s in shards.items() if n != "w_in"}
        self.raw = {}
        self.parts = {}
        self.landing = {}
        self.halves = {}
        self.pending = {}
        self.last_site = {}
        for site, items in self.EXCHANGE_AT.items():
            for it in items:
                self.last_site[it[0]] = site

    def weight(self, name):
        g = self.bufs[name]
        return g.reshape(g.shape[0] * g.shape[1], g.shape[2]) if name in ROW_SHARDED else g

    def _slot_ids(self):
        x, y, _ = self.pos
        ids = [2 * x + y] + [2 * cx + cy for cx, cy in _other_chips(x, y)]
        return jnp.stack(ids).astype(jnp.int32)

    def mm_in(self, u):
        ids = self._slot_ids()
        own = self.w_in_own
        proj, (near,) = _mm_slots(u, own[None], ids[0:1], None, name="mm_in_own",
                                  carry=_gather_fresh_carry(own, (0, 1)))
        proj, (far,) = _mm_slots(u, near, ids[1:3], proj, name="mm_in_near",
                                 carry=_gather_fresh_carry(own, (2,)))
        proj, carried = _mm_slots(u, far, ids[3:4], proj, name="mm_in_diag", carry=self.carry("mm_in_diag"))
        self.done("mm_in_diag", carried)
        self.w_in_rel = jnp.concatenate([own[None], near, far], axis=0)
        return proj

    def mm_d_in(self, dproj):
        carry = self.carry("mm_d_in")
        du, carried = _mm(dproj, self.w_in_rel, mode="nt", out_dtype=F32, name="mm_d_in", carry=carry,
                          a_slots=self._slot_ids())
        self.done("mm_d_in", carried)
        return du

    def early_grads(self, early):
        self.early = early

    def carry(self, site):
        jobs = []
        if site == self.EARLY_AT:
            flat, self.early_offs = _pack(list(self.early.values()))
            jobs.append(("early", None, _gather_rows_carry(flat.reshape(-1, PACK_COLS))))
        if site in self.GATHER_AT:
            items = self.GATHER_AT[site]
            jobs.append(("gather", items, _gather_carry([(self.bufs[it[0]],) + tuple(it[1:]) for it in items])))
        if site in self.EXCHANGE_AT:
            items = self.EXCHANGE_AT[site]
            jobs.append(("exchange", items, _exchange_carry(
                [(self.parts[it[0]], self.landing.get(it[0])) + tuple(it[1:]) for it in items])))
        if site in self.SWAP_AT:
            names = self.SWAP_AT[site]
            jobs.append(("swap", names, _swap_carry([self.raw[n] for n in names])))
        if not jobs:
            return None
        self.pending[site] = jobs
        return _merge_carries([job[2] for job in jobs])

    def done(self, site, carried):
        pos = 0
        for kind, items, carry in self.pending.pop(site):
            outs = carried[pos:pos + len(carry.out_shapes)]
            pos += len(carry.out_shapes)
            if kind == "early":
                self.early_all = outs[0]
            elif kind == "gather":
                self.bufs.update(zip([it[0] for it in items], outs))
            elif kind == "swap":
                for n, landed in zip(items, outs):
                    self.parts[n] = _add_half(self.raw.pop(n), landed, self.my_c, name="add_half_" + n)
            else:
                for it, landed in zip(items, outs):
                    n = it[0]
                    self.landing[n] = landed
                    if self.last_site[n] == site:
                        self.halves[n] = _sum_into_half(self.parts.pop(n), self.landing.pop(n), self.chip,
                                                        self.my_c, name="sum_chips_" + n)

    def grad(self, name, dw):
        if name in ROW_SHARDED:
            dw = dw.reshape(N_CHIPS, dw.shape[0] // N_CHIPS, dw.shape[1])
        self.raw[name] = dw

    def join(self, names, *, name):
        return dict(zip(names, _run_carry(_join_carry([self.halves.pop(n) for n in names]), name=name)))


def _local_step(x, target, mod, small, comm):
    rows, d = x.shape
    cw = d // 2
    shift1, scale1, gate1, shift2, scale2, gate2 = mod
    _, _, bbr, bbi = small["s5_disc"]
    b_in, c_out, b_out, c_in, mults = _s5_operands(*small["s5_loglam"], bbr, bbi, small["c_re"], small["c_im"])
    wt = comm.weight

    def riding(site, fn, *args, **kwargs):
        carry = comm.carry(site)
        if carry is None:
            return fn(*args, **kwargs)
        out, carried = fn(*args, carry=carry, **kwargs)
        comm.done(site, carried)
        return out

    u = _norm_mod(x, small["norm1_g"], scale1, shift1, name="norm1_fwd")
    proj = comm.mm_in(u)
    sl, cv = riding("conv_fwd", _conv_fwd, proj, small["w_dw"], small["b_dw"], small["ln_g"], small["ln_b"], cw=cw)
    y_conv = _mm(sl, wt("w_conv_out"), mode="nn", out_dtype=BF16, name="mm_conv_out")
    yg, st_re, st_im = riding("s5_fwd", _s5_fwd, proj, small["d_skip"], b_in, c_out, mults, col0=2 * cw // LANES)
    ya = riding("mm_glu_a", _mm, yg, wt("w_glu_a"), mode="nn", out_dtype=BF16, name="mm_glu_a")
    yb = riding("mm_glu_b", _mm, yg, wt("w_glu_b"), mode="nn", out_dtype=BF16, name="mm_glu_b")
    merged = _merge_fwd(proj, y_conv, ya, yb, cw=cw)
    mo = riding("mm_out", _mm, merged, wt("w_out"), mode="nn", out_dtype=BF16, name="mm_out")
    h1, z = _res_norm(x, mo, gate1, small["norm2_g"], scale2, shift2)
    f1 = riding("mm_ff1", _mm, z, wt("w_ff1"), mode="nn", out_dtype=BF16, name="mm_ff1")
    ff = _mm(f1, wt("w_ff2"), mode="nn", out_dtype=BF16, name="mm_ff2", a_fn=_relu2_bf16)
    dh2, dff, loss, d_final_g, d_gate2 = _final_fwd_bwd(h1, ff, gate2, small["final_g"], target)

    comm.grad("w_ff2", _mm(f1, dff, mode="tn", out_dtype=BF16, name="mm_dw_ff2", a_fn=_relu2_bf16))
    df1 = riding("mm_d_ff2", _mm, dff, wt("w_ff2"), mode="nt", out_dtype=BF16, name="mm_d_ff2", extra=f1,
                 epi=lambda acc, f: acc * (2.0 * jnp.maximum(f.astype(F32), 0.0)))
    comm.grad("w_ff1", riding("mm_dw_ff1", _mm, z, df1, mode="tn", out_dtype=BF16, name="mm_dw_ff1",
                              out_gathered=True))
    dz = riding("mm_d_ff1", _mm, df1, wt("w_ff1"), mode="nt", out_dtype=F32, name="mm_d_ff1")
    dh1, d_shift2, d_scale2, d_norm2_g, dmo, d_gate1 = riding(
        "norm2_bwd", _norm_mod_bwd, dz, h1, dh2, small["norm2_g"], scale2, gate1, mo, name="norm2_bwd")
    comm.grad("w_out", riding("mm_dw_out", _mm, merged, dmo, mode="tn", out_dtype=BF16, name="mm_dw_out"))
    dmerged = riding("mm_d_out", _mm, dmo, wt("w_out"), mode="nt", out_dtype=BF16, name="mm_d_out")
    dgc, dgs, dy_conv, dya, dyb = riding("merge_bwd", _merge_bwd, dmerged, proj, y_conv, ya, yb, cw=cw)
    comm.grad("w_glu_a", _mm(yg, dya, mode="tn", out_dtype=BF16, name="mm_dw_glu_a", out_gathered=True))
    comm.grad("w_glu_b", _mm(yg, dyb, mode="tn", out_dtype=BF16, name="mm_dw_glu_b", out_gathered=True))
    dyg_a = _mm(dya, wt("w_glu_a"), mode="nt", out_dtype=F32, name="mm_d_glu_a")
    dyg = _mm(dyb, wt("w_glu_b"), mode="nt", out_dtype=F32, name="mm_d_glu_b", extra=dyg_a,
              epi=lambda acc, e: acc + e)
    comm.grad("w_conv_out", _mm(sl, dy_conv, mode="tn", out_dtype=BF16, name="mm_dw_conv_out", out_gathered=True))
    dsl = _mm(dy_conv, wt("w_conv_out"), mode="nt", out_dtype=F32, name="mm_d_conv_out")
    dcv, d_ln_g, d_ln_b = _ln_bwd(dsl, cv, small["ln_g"], small["ln_b"])
    dvconv, d_w_dw, d_b_dw = riding("conv_bwd", _conv_bwd, dcv, proj, small["w_dw"], cw=cw)
    dvssm, d_d_skip, dbr, dbi, dcr, dci, dlr, dli = riding(
        "s5_bwd", _s5_bwd, proj, dyg, small["d_skip"], (st_re, st_im), c_out, b_out, c_in, mults,
        col0=2 * cw // LANES)
    sw = lambda m: jnp.swapaxes(m, 1, 2)
    early = {
        "dmod_tail": jnp.concatenate([d_gate1, d_shift2, d_scale2, d_gate2], axis=1), "loss": loss[:, 0:1],
        "w_dw": d_w_dw, "b_dw": d_b_dw, "ln_g": d_ln_g, "ln_b": d_ln_b,
        "lam_re": dlr.reshape(-1, SSM_STATE), "lam_im": dli.reshape(-1, SSM_STATE),
        "bb_re": sw(_block_diag_extract(dbr, SSM_GROUP, SSM_STATE)),
        "bb_im": sw(_block_diag_extract(dbi, SSM_GROUP, SSM_STATE)),
        "c_re": sw(_block_diag_extract(dcr, SSM_STATE, SSM_GROUP)),
        "c_im": sw(_block_diag_extract(dci, SSM_STATE, SSM_GROUP)),
        "d_skip": d_d_skip, "norm2_g": d_norm2_g, "final_g": d_final_g,
    }
    comm.early_grads(early)
    dproj = jnp.concatenate([dvconv, dvssm, dgc, dgs], axis=1)
    comm.grad("w_in", riding("mm_dw_in", _mm, u, dproj, mode="tn", out_dtype=BF16, name="mm_dw_in",
                             out_gathered=True))
    du = comm.mm_d_in(dproj)
    grad_x, d_shift1, d_scale1, d_norm1_g = riding(
        "norm1_bwd", _norm_mod_bwd, du, x, dh1, small["norm1_g"], scale1, None, None, name="norm1_bwd")
    late ={"dmod_head": jnp.concatenate([d_shift1, d_scale1], axis=1), "norm1_g": d_norm1_g}
    return grad_x, early, late


WEIGHT_NAMES = ["w_ada", "b_ada", "norm1_g", "w_in", "w_dw", "b_dw", "ln_g", "ln_b", "w_conv_out", "a_re", "a_im",
                "log_dt", "b_re", "b_im", "c_re", "c_im", "d_skip", "w_glu_a", "w_glu_b", "w_out", "norm2_g",
                "w_ff1", "w_ff2", "final_g"]
BIG_NAMES = ["w_in", "w_conv_out", "w_glu_a", "w_glu_b", "w_out", "w_ff1", "w_ff2"]
ROW_SHARDED = ("w_out", "w_ff2")
PACK_COLS = 1024
PACK_TILE = SUBLANES * PACK_COLS


def _pack(arrays):
    flats = [a.reshape(-1) for a in arrays]
    offs = []
    total = 0
    for f in flats:
        offs.append(total)
        total += f.shape[0]
    pad = (-total) % PACK_TILE
    if pad:
        flats.append(jnp.zeros((pad,), F32))
    return jnp.concatenate(flats), offs


def _unpack(flat, offs, like):
    return [flat[o:o + a.size].reshape(a.shape) for o, a in zip(offs, like)]


def _gather_w_dw(w_shard):
    k, n = w_shard.shape
    padded = jnp.pad(w_shard, ((0, HALO - k), (0, 0)))
    allw = _gather_small(padded, name="gather_w_dw").reshape(N_CHIPS, 2, HALO, n)[:, 0, :k]
    return jnp.moveaxis(allw, 0, 1).reshape(k, N_CHIPS * n)


def kernel(x, c, w_ada, b_ada, norm1_g, w_in, w_dw, b_dw, ln_g, ln_b, w_conv_out, a_re, a_im, log_dt, b_re, b_im, c_re, c_im, d_skip, w_glu_a, w_glu_b, w_out, norm2_g, w_ff1, w_ff2, final_g, loss_target, m_w_ada, m_b_ada, m_norm1_g, m_w_in, m_w_dw, m_b_dw, m_ln_g, m_ln_b, m_w_conv_out, m_a_re, m_a_im, m_log_dt, m_b_re, m_b_im, m_c_re, m_c_im, m_d_skip, m_w_glu_a, m_w_glu_b, m_w_out, m_norm2_g, m_w_ff1, m_w_ff2, m_final_g, v_w_ada, v_b_ada, v_norm1_g, v_w_in, v_w_dw, v_b_dw, v_ln_g, v_ln_b, v_w_conv_out, v_a_re, v_a_im, v_log_dt, v_b_re, v_b_im, v_c_re, v_c_im, v_d_skip, v_w_glu_a, v_w_glu_b, v_w_out, v_norm2_g, v_w_ff1, v_w_ff2, v_final_g):
    given = dict(locals())
    w = {n: given[n] for n in WEIGHT_NAMES}
    m = {n: given["m_" + n] for n in WEIGHT_NAMES}
    v = {n: given["v_" + n] for n in WEIGHT_NAMES}
    d = x.shape[2]
    xi, yi, ci = _mesh_pos()
    chip = 2 * xi + yi
    dev = 4 * xi + 2 * yi + ci
    my_c = jnp.reshape(ci, (1,)).astype(jnp.int32)

    ndw = w_dw.shape[2]
    assert d // SUBLANES == ndw
    first = jnp.concatenate([c.reshape(SUBLANES, ndw), jnp.pad(w_dw[0], ((0, HALO - CONV_KERNEL), (0, 0)))])
    first_all = _gather_small(first, name="gather_c_w_dw").reshape(N_DEV, SUBLANES + HALO, ndw)
    c_all = first_all[:, :SUBLANES].reshape(N_DEV, d)
    taps = first_all.reshape(N_CHIPS, 2, SUBLANES + HALO, ndw)[:, 0, SUBLANES:SUBLANES + CONV_KERNEL]
    w_dw_full = jnp.moveaxis(taps, 0, 1).reshape(CONV_KERNEL, N_CHIPS * ndw)

    nmod = w_ada.shape[2]
    b_cols = lax.dynamic_slice(b_ada, (0, chip * nmod), (1, nmod))
    mod_part = _ada_fwd(c_all, w_ada[0], b_cols)
    mod_all = _gather_small(mod_part, name="gather_mod").reshape(N_CHIPS, 2, N_DEV, nmod)[:, 0]
    mod_full = jnp.moveaxis(mod_all, 0, 1).reshape(N_DEV, N_CHIPS * nmod)
    mod_row = lax.dynamic_slice(mod_full, (dev, 0), (1, N_CHIPS * nmod))
    mod = [mod_row[:, i * d:(i + 1) * d] for i in range(6)]

    chip_arr = jnp.reshape(chip, (1,)).astype(jnp.int32)
    comm = _MeshComm({n: w[n][0] for n in BIG_NAMES}, (xi, yi, ci), chip_arr, my_c)

    disc_in = (a_re[0], a_im[0], log_dt[0], b_re[0], b_im[0])
    disc, disc_vjp = jax.vjp(_s5_discretise, *disc_in)
    dt = jnp.exp(log_dt[0])[:, None]
    small = {"norm1_g": norm1_g, "w_dw": w_dw_full, "b_dw": b_dw, "ln_g": ln_g, "ln_b": ln_b,
             "c_re": c_re[0], "c_im": c_im[0], "d_skip": d_skip, "norm2_g": norm2_g,
             "final_g": final_g[None, :], "s5_disc": disc, "s5_loglam": (a_re[0] * dt, a_im[0] * dt)}

    grad_x, early, late = _local_step(x[0], loss_target[0], mod, small, comm)
    grads = {}

    early_all = comm.early_all.reshape(N_DEV, -1, PACK_COLS)
    early_sum = _sum_leading(early_all, name="sum_small_grads").reshape(-1)
    summed = dict(zip(early, _unpack(early_sum, comm.early_offs, list(early.values()))))
    flat, late_offs = _pack(list(late.values()))
    late_all = _gather_small(flat.reshape(-1, PACK_COLS), name="gather_late_grads").reshape(N_DEV, -1, PACK_COLS)
    late_sum = _sum_leading(late_all, name="sum_late_grads").reshape(-1)
    summed.update(zip(late, _unpack(late_sum, late_offs, list(late.values()))))
    head = late_all[:, :2 * d // PACK_COLS].reshape(N_DEV, 2 * d)
    tail = early_all[:, :4 * d // PACK_COLS].reshape(N_DEV, 4 * d)
    dmod_all = jnp.concatenate([head, tail], axis=1)

    grads["w_ada"] = _ada_bwd(c_all, lax.dynamic_slice(dmod_all, (0, chip * nmod), (N_DEV, nmod)))
    grads["b_ada"] = _sum_leading(dmod_all.reshape(N_DEV, SUBLANES, 6 * d // SUBLANES),
                                  name="sum_b_ada").reshape(1, 6 * d)
    da_re, da_im, dlog_dt, db_re, db_im = disc_vjp(
        (summed["lam_re"], summed["lam_im"], summed["bb_re"], summed["bb_im"]))
    grads.update({
        "norm1_g": summed["norm1_g"], "w_dw": lax.dynamic_slice(summed["w_dw"], (0, chip * ndw), (CONV_KERNEL, ndw)),
        "b_dw": summed["b_dw"], "ln_g": summed["ln_g"], "ln_b": summed["ln_b"],
        "a_re": da_re, "a_im": da_im, "log_dt": dlog_dt, "b_re": db_re, "b_im": db_im,
        "c_re": summed["c_re"], "c_im": summed["c_im"], "d_skip": summed["d_skip"],
        "norm2_g": summed["norm2_g"], "final_g": summed["final_g"],
    })

    delta, new_m, new_v = {}, {}, {}

    def adam_big(n, site=None):
        shp = w[n].shape
        two_d = lambda a: a.reshape(shp[1], shp[2])
        carry = comm.carry(site) if site is not None else None
        res = _adamw(two_d(w[n]), two_d(grads[n]), two_d(m[n]), two_d(v[n]), name="adamw_" + n, carry=carry)
        if carry is not None:
            res, carried = res
            comm.done(site, carried)
        delta[n], new_m[n], new_v[n] = [r.reshape(shp) for r in res]

    adam_big("w_ada", "adamw_w_ada")
    grads.update(comm.join([n for n in BIG_NAMES if n != "w_in"], name="join_halves"))
    adam_big("w_ff2", "adamw_w_ff2")
    for n in BIG_NAMES:
        if n not in ("w_in", "w_ff2"):
            adam_big(n)
    grads.update(comm.join(["w_in"], name="join_w_in"))
    adam_big("w_in")
    grads = {n: grads[n].reshape(w[n].shape) for n in WEIGHT_NAMES}
    rest = [n for n in WEIGHT_NAMES if n not in delta]
    packs = []
    for src in (w, grads, m, v):
        flat, offs = _pack([src[n] for n in rest])
        packs.append(flat.reshape(-1, 1024))
    outs = _adamw(*packs, name="adamw_small")
    for dst, o in zip((delta, new_m, new_v), outs):
        for n, a in zip(rest, _unpack(o.reshape(-1), offs, [w[k] for k in rest])):
            dst[n] = a

    return (summed["loss"].reshape(()), grad_x[None], *[grads[n] for n in WEIGHT_NAMES],
            *[delta[n] for n in WEIGHT_NAMES], *[new_m[n] for n in WEIGHT_NAMES],
            *[new_v[n] for n in WEIGHT_NAMES])
```

```python
import functools
import math

import jax
import jax.numpy as jnp
from jax import lax
from jax.experimental import pallas as pl
from jax.experimental.pallas import tpu as pltpu

F32 = jnp.float32
BF16 = jnp.bfloat16
EPS = 1e-6
CONV_KERNEL = 31
SSM_GROUP = 16
SSM_STATE = 64
ADAM_LR = 0.001
ADAM_B1 = 0.9
ADAM_B2 = 0.999
ADAM_EPS = 1e-08
ADAM_WD = 0.01
ADAM_STEP = 10

N_CHIPS = 4
N_DEV = 8
VMEM_LIMIT_BYTES = 56 * 1024 * 1024
LANES = 128
SUBLANES = 8
HALO = 32
GROUPS_PER_BLOCK = LANES // SSM_GROUP
STATE_LANES = GROUPS_PER_BLOCK * SSM_STATE
MESH = pl.DeviceIdType.MESH


def _cparams(sem):
    return pltpu.CompilerParams(dimension_semantics=sem, vmem_limit_bytes=VMEM_LIMIT_BYTES)


def _pick(n, pref, mult=LANES):
    if n <= pref:
        return n
    best = None
    for d in range(mult, pref + 1, mult):
        if n % d == 0:
            best = d
    assert best is not None, (n, pref)
    return best


def _sigmoid(v):
    return 1.0 / (1.0 + jnp.exp(-v))


def _gelu_parts(v):
    k0 = math.sqrt(2.0 / math.pi)
    inner = k0 * (v + 0.044715 * v * v * v)
    t = jnp.tanh(inner)
    return k0, t


def _gelu(v):
    _, t = _gelu_parts(v)
    return 0.5 * v * (1.0 + t)


def _gelu_grad(v):
    k0, t = _gelu_parts(v)
    return 0.5 * (1.0 + t) + 0.5 * v * (1.0 - t * t) * k0 * (1.0 + 3.0 * 0.044715 * v * v)


def _relu2_bf16(a):
    t = jnp.maximum(a.astype(F32), 0.0)
    return (t * t).astype(BF16)


class _Carry:
    def __init__(self, inputs, out_shapes, aliases, sem_shapes, start, finish):
        self.inputs = list(inputs)
        self.out_shapes = list(out_shapes)
        self.aliases = dict(aliases)
        self.sem_shapes = list(sem_shapes)
        self.start = start
        self.finish = finish


def _call(body, *, grid, in_specs, out_specs, out_shape, scratch_shapes, semantics, name, args, carry=None,
          prefetch=(), aliases=None):
    n_in, n_out, n_scr, n_pf = len(in_specs), len(out_specs), len(scratch_shapes), len(prefetch)
    own_aliases = {n_pf + i: o for i, o in (aliases or {}).items()}
    if carry is None:
        gs = pltpu.PrefetchScalarGridSpec(
            num_scalar_prefetch=n_pf, grid=grid, in_specs=in_specs, out_specs=out_specs,
            scratch_shapes=scratch_shapes)
        outs = pl.pallas_call(
            body, grid_spec=gs, out_shape=out_shape, input_output_aliases=own_aliases,
            compiler_params=_cparams(semantics), name=name)(*prefetch, *args)
        return list(outs), []
    ci, co = len(carry.inputs), len(carry.out_shapes)

    def wrapped(*refs):
        pf, refs = refs[:n_pf], refs[n_pf:]
        ins, cins = refs[:n_in], refs[n_in:n_in + ci]
        p = n_in + ci
        outs, couts = refs[p:p + n_out], refs[p + n_out:p + n_out + co]
        p += n_out + co
        scr, csems = refs[p:p + n_scr], refs[p + n_scr:]
        first = pl.program_id(0) == 0
        last = pl.program_id(0) == grid[0] - 1
        for ax in range(1, len(grid)):
            first = jnp.logical_and(first, pl.program_id(ax) == 0)
            last = jnp.logical_and(last, pl.program_id(ax) == grid[ax] - 1)

        @pl.when(first)
        def _():
            carry.start(cins, couts, csems)

        body(*pf, *ins, *outs, *scr)

        @pl.when(last)
        def _():
            carry.finish(cins, couts, csems)

    any_spec = pl.BlockSpec(memory_space=pl.ANY)
    gs = pltpu.PrefetchScalarGridSpec(
        num_scalar_prefetch=n_pf, grid=grid, in_specs=list(in_specs) + [any_spec] * ci,
        out_specs=list(out_specs) + [any_spec] * co, scratch_shapes=list(scratch_shapes) + carry.sem_shapes)
    all_aliases = dict(own_aliases)
    all_aliases.update({n_pf + n_in + i: n_out + o for i, o in carry.aliases.items()})
    outs = pl.pallas_call(
        wrapped, grid_spec=gs, out_shape=list(out_shape) + carry.out_shapes, input_output_aliases=all_aliases,
        compiler_params=_cparams(("arbitrary",) * len(grid)), name=name)(*prefetch, *args, *carry.inputs)
    return list(outs[:n_out]), list(outs[n_out:])


def _run_carry(carry, *, name):
    ci = len(carry.inputs)

    def body(*refs):
        cins, couts, csems = refs[:ci], refs[ci:ci + len(carry.out_shapes)], refs[ci + len(carry.out_shapes):]
        carry.start(cins, couts, csems)
        carry.finish(cins, couts, csems)

    any_spec = pl.BlockSpec(memory_space=pl.ANY)
    outs = pl.pallas_call(
        body, in_specs=[any_spec] * ci, out_specs=[any_spec] * len(carry.out_shapes), out_shape=carry.out_shapes,
        scratch_shapes=carry.sem_shapes, input_output_aliases=carry.aliases, name=name)(*carry.inputs)
    return list(outs)


def _mm(a, b, *, mode, out_dtype, name, out_gathered=False, a_fn=None, epi=None, extra=None,
        bm_pref=1024, bn_pref=1024, bk_pref=2048, carry=None, a_slots=None):
    gathered = (b.ndim == 3)
    if mode == "nn":
        m, kdim = a.shape
        ns = b.shape[-1]
        n = ns * (N_CHIPS if gathered else 1)
        bm, bn, bk = _pick(m, bm_pref), _pick(ns, bn_pref), _pick(kdim, bk_pref)
        npb = ns // bn
        grid = (m // bm, n // bn, kdim // bk)
        a_spec = pl.BlockSpec((bm, bk), lambda i, j, k: (i, k))
        if gathered:
            b_spec = pl.BlockSpec((None, bk, bn), lambda i, j, k: (j // npb, k, j % npb))
        else:
            b_spec = pl.BlockSpec((bk, bn), lambda i, j, k: (k, j))
        o_spec = pl.BlockSpec((bm, bn), lambda i, j, k: (i, j))
        e_spec = pl.BlockSpec((bm, bn), lambda i, j, k: (i, j))
        out_shape = (m, n)
        acc_shape = (bm, bn)
        dims = (((1,), (0,)), ((), ()))
    elif mode == "nt":
        m = a.shape[0]
        kdim, ns = b.shape[-2], b.shape[-1]
        n = ns * (N_CHIPS if gathered else 1)
        assert a.shape[1] == n
        bm, bko, bnr = _pick(m, bm_pref), _pick(kdim, bn_pref), _pick(ns, bk_pref)
        npb = ns // bnr
        grid = (m // bm, kdim // bko, n // bnr)
        a_spec = pl.BlockSpec((bm, bnr), lambda i, j, k: (i, k))
        if gathered:
            b_spec = pl.BlockSpec((None, bko, bnr), lambda i, j, k: (k // npb, j, k % npb))
        else:
            b_spec = pl.BlockSpec((bko, bnr), lambda i, j, k: (j, k))
        o_spec = pl.BlockSpec((bm, bko), lambda i, j, k: (i, j))
        e_spec = pl.BlockSpec((bm, bko), lambda i, j, k: (i, j))
        if a_slots is not None:
            assert gathered and extra is None
            a_spec = pl.BlockSpec((bm, bnr), lambda i, j, k, s_ref: (i, s_ref[k // npb] * npb + k % npb))
            b_spec = pl.BlockSpec((None, bko, bnr), lambda i, j, k, s_ref: (k // npb, j, k % npb))
            o_spec = pl.BlockSpec((bm, bko), lambda i, j, k, s_ref: (i, j))
        out_shape = (m, kdim)
        acc_shape = (bm, bko)
        dims = (((1,), (1,)), ((), ()))
    else:
        m, kdim = a.shape
        n = b.shape[1]
        ns = n // N_CHIPS if out_gathered else n
        bmr, bko, bn = _pick(m, bk_pref), _pick(kdim, bm_pref), _pick(ns, bn_pref)
        npb = ns // bn
        grid = (kdim // bko, n // bn, m // bmr)
        a_spec = pl.BlockSpec((bmr, bko), lambda i, j, k: (k, i))
        b_spec = pl.BlockSpec((bmr, bn), lambda i, j, k: (k, j))
        if out_gathered:
            o_spec = pl.BlockSpec((None, bko, bn), lambda i, j, k: (j // npb, i, j % npb))
            out_shape = (N_CHIPS, kdim, ns)
        else:
            o_spec = pl.BlockSpec((bko, bn), lambda i, j, k: (i, j))
            out_shape = (kdim, n)
        e_spec = None
        acc_shape = (bko, bn)
        dims = (((0,), (0,)), ((), ()))
    nk = grid[2]

    def body(*refs):
        if a_slots is not None:
            refs = refs[1:]
        if extra is not None:
            a_ref, b_ref, e_ref, o_ref, acc = refs
        else:
            a_ref, b_ref, o_ref, acc = refs
            e_ref = None
        k = pl.program_id(2)
        av = a_ref[...]
        if a_fn is not None:
            av = a_fn(av)
        part = lax.dot_general(av, b_ref[...], dims, preferred_element_type=F32)

        def finish(r):
            if epi is not None:
                r = epi(r, e_ref[...])
            o_ref[...] = r.astype(o_ref.dtype)

        if nk == 1:
            finish(part)
            return

        @pl.when(k == 0)
        def _():
            acc[...] = part

        @pl.when(jnp.logical_and(k > 0, k < nk - 1))
        def _():
            acc[...] += part

        @pl.when(k == nk - 1)
        def _():
            finish(acc[...] + part)

    in_specs = [a_spec, b_spec]
    args = [a, b]
    if extra is not None:
        in_specs.append(e_spec)
        args.append(extra)
    outs, carried = _call(body, grid=grid, in_specs=in_specs, out_specs=[o_spec],
                          out_shape=[jax.ShapeDtypeStruct(out_shape, out_dtype)],
                          scratch_shapes=[pltpu.VMEM(acc_shape, F32)],
                          semantics=("parallel", "parallel", "arbitrary"), name=name, args=args, carry=carry,
                          prefetch=() if a_slots is None else (a_slots,))
    return outs[0] if carry is None else (outs[0], carried)


def _mm_slots(a, wbuf, slots, prev, *, name, carry=None):
    m, kdim = a.shape
    ns = wbuf.shape[2]
    bm, bn = _pick(m, 1024), _pick(ns, 1024)
    npb = ns // bn
    grid = (m // bm, slots.shape[0], npb)

    def body(s_ref, a_ref, b_ref, *rest):
        o_ref = rest[-1]
        o_ref[...] = _dot(a_ref[...], b_ref[...]).astype(o_ref.dtype)

    in_specs = [pl.BlockSpec((bm, kdim), lambda i, s, j, s_ref: (i, 0)),
                pl.BlockSpec((None, kdim, bn), lambda i, s, j, s_ref: (s, 0, j))]
    args = [a, wbuf]
    aliases = None
    if prev is not None:
        in_specs.append(pl.BlockSpec(memory_space=pl.ANY))
        args.append(prev)
        aliases = {2: 0}
    outs, carried = _call(
        body, grid=grid, in_specs=in_specs,
        out_specs=[pl.BlockSpec((bm, bn), lambda i, s, j, s_ref: (i, s_ref[s] * npb + j))],
        out_shape=[jax.ShapeDtypeStruct((m, N_CHIPS * ns), BF16)], scratch_shapes=[],
        semantics=("parallel", "arbitrary", "arbitrary"), name=name, args=args, carry=carry,
        prefetch=(slots,), aliases=aliases)
    return outs[0] if carry is None else (outs[0], carried)


def _row_tile(rows, cols, n_arrays):
    budget = VMEM_LIMIT_BYTES // 3
    cap = min(512, budget // (n_arrays * 2 * cols * 4))
    for t in range(cap - cap % SUBLANES, 0, -SUBLANES):
        if rows % t == 0:
            return t
    return rows


def _norm_mod(x, g, scale, shift, *, name):
    rows, d = x.shape
    tr = _row_tile(rows, d, 3)

    def body(x_ref, g_ref, sc_ref, sh_ref, o_ref):
        xv = x_ref[...]
        r = lax.rsqrt(jnp.mean(xv * xv, axis=-1, keepdims=True) + EPS)
        o_ref[...] = ((xv * r * g_ref[...]) * (1.0 + sc_ref[...]) + sh_ref[...]).astype(o_ref.dtype)

    row = pl.BlockSpec((tr, d), lambda i: (i, 0))
    vec = pl.BlockSpec((1, d), lambda i: (0, 0))
    return pl.pallas_call(
        body, grid=(rows // tr,), in_specs=[row, vec, vec, vec], out_specs=row,
        out_shape=jax.ShapeDtypeStruct((rows, d), BF16),
        compiler_params=_cparams(("parallel",)), name=name)(x, g, scale, shift)


CONV_CHUNK = 2 * SUBLANES


def _shifted_copies(buf, n):
    for r in range(1, SUBLANES):
        buf[r, pl.ds(0, n - SUBLANES), :] = buf[0, pl.ds(r, n - SUBLANES), :]


def _conv_fwd(proj, w_dw, b_dw, ln_g, ln_b, *, cw, carry=None):
    rows = proj.shape[0]
    tt = _pick(rows, 256, HALO)
    hb = tt // HALO

    def body(a_ref, g_ref, ha_ref, hg_ref, w_ref, b_ref, lg_ref, lb_ref, sl_ref, cv_ref, vs):
        i = pl.program_id(0)
        hv = ha_ref[...].astype(F32) * _sigmoid(hg_ref[...].astype(F32))
        vs[0, pl.ds(0, HALO), :] = jnp.where(i == 0, 0.0, hv)
        vs[0, pl.ds(HALO, tt), :] = a_ref[...].astype(F32) * _sigmoid(g_ref[...].astype(F32))
        _shifted_copies(vs, HALO + tt)

        def chunk(ci, carry):
            r0 = pl.multiple_of(ci * CONV_CHUNK, CONV_CHUNK)
            acc = jnp.broadcast_to(b_ref[...], (CONV_CHUNK, cw))
            for k in range(CONV_KERNEL):
                q, r = divmod(HALO - (CONV_KERNEL - 1) + k, SUBLANES)
                acc = acc + w_ref[pl.ds(k, 1), :] * vs[r, pl.ds(r0 + q * SUBLANES, CONV_CHUNK), :]
            cv_ref[pl.ds(r0, CONV_CHUNK), :] = acc
            return carry

        lax.fori_loop(0, tt // CONV_CHUNK, chunk, 0)
        acc = cv_ref[...]
        mu = jnp.mean(acc, axis=-1, keepdims=True)
        xc = acc - mu
        rstd = lax.rsqrt(jnp.mean(xc * xc, axis=-1, keepdims=True) + EPS)
        ln = xc * rstd * lg_ref[...] + lb_ref[...]
        sl_ref[...] = (ln * _sigmoid(ln)).astype(sl_ref.dtype)

    tile = lambda c: pl.BlockSpec((tt, cw), lambda i, c=c: (i, c))
    halo = lambda c: pl.BlockSpec((HALO, cw), lambda i, c=c: (jnp.maximum(i * hb - 1, 0), c))
    vec = pl.BlockSpec((1, cw), lambda i: (0, 0))
    outs, carried = _call(
        body, grid=(rows // tt,),
        in_specs=[tile(0), tile(1), halo(0), halo(1),
                  pl.BlockSpec((CONV_KERNEL, cw), lambda i: (0, 0)), vec, vec, vec],
        out_specs=[pl.BlockSpec((tt, cw), lambda i: (i, 0)), pl.BlockSpec((tt, cw), lambda i: (i, 0))],
        out_shape=[jax.ShapeDtypeStruct((rows, cw), BF16), jax.ShapeDtypeStruct((rows, cw), F32)],
        scratch_shapes=[pltpu.VMEM((SUBLANES, HALO + tt, cw), F32)],
        semantics=("parallel",), name="conv_fwd", args=[proj, proj, proj, proj, w_dw, b_dw, ln_g, ln_b],
        carry=carry)
    return outs if carry is None else (outs, carried)


def _ln_bwd(dsl, cv, ln_g, ln_b):
    rows, cw = cv.shape
    tr = _row_tile(rows, cw, 3)

    def body(d_ref, cv_ref, lg_ref, lb_ref, o_ref, dg_ref, db_ref):
        i = pl.program_id(0)

        @pl.when(i == 0)
        def _():
            dg_ref[...] = jnp.zeros_like(dg_ref)
            db_ref[...] = jnp.zeros_like(db_ref)

        x = cv_ref[...]
        mu = jnp.mean(x, axis=-1, keepdims=True)
        xc = x - mu
        rstd = lax.rsqrt(jnp.mean(xc * xc, axis=-1, keepdims=True) + EPS)
        xh = xc * rstd
        ln = xh * lg_ref[...] + lb_ref[...]
        s = _sigmoid(ln)
        dln = d_ref[...].astype(F32) * (s * (1.0 + ln * (1.0 - s)))
        dg_ref[...] += jnp.sum(dln * xh, axis=0, keepdims=True)
        db_ref[...] += jnp.sum(dln, axis=0, keepdims=True)
        dxh = dln * lg_ref[...]
        m1 = jnp.mean(dxh, axis=-1, keepdims=True)
        m2 = jnp.mean(dxh * xh, axis=-1, keepdims=True)
        o_ref[...] = rstd * (dxh - m1 - xh * m2)

    row = pl.BlockSpec((tr, cw), lambda i: (i, 0))
    vec = pl.BlockSpec((1, cw), lambda i: (0, 0))
    return pl.pallas_call(
        body, grid=(rows // tr,), in_specs=[row, row, vec, vec], out_specs=[row, vec, vec],
        out_shape=[jax.ShapeDtypeStruct((rows, cw), F32), jax.ShapeDtypeStruct((1, cw), F32),
                   jax.ShapeDtypeStruct((1, cw), F32)],
        compiler_params=_cparams(("arbitrary",)), name="ln_bwd")(dsl, cv, ln_g, ln_b)


def _conv_bwd(dcv, proj, w_dw, *, cw, carry=None):
    rows = proj.shape[0]
    tt = _pick(rows, 256, HALO)
    hb = tt // HALO
    nt = rows // tt
    taps = CONV_KERNEL

    def body(d_ref, dn_ref, a_ref, g_ref, ha_ref, hg_ref, w_ref, o_ref, dw_ref, db_ref, vs, ds):
        i = pl.program_id(0)

        @pl.when(i == 0)
        def _():
            dw_ref[...] = jnp.zeros_like(dw_ref)
            db_ref[...] = jnp.zeros_like(db_ref)

        hv = ha_ref[...].astype(F32) * _sigmoid(hg_ref[...].astype(F32))
        vs[0, pl.ds(0, HALO), :] = jnp.where(i == 0, 0.0, hv)
        vs[0, pl.ds(HALO, tt), :] = a_ref[...].astype(F32) * _sigmoid(g_ref[...].astype(F32))
        _shifted_copies(vs, HALO + tt)
        ds[0, pl.ds(0, tt), :] = d_ref[...]
        ds[0, pl.ds(tt, HALO), :] = jnp.where(i == nt - 1, 0.0, dn_ref[...])
        _shifted_copies(ds, tt + HALO)
        db_ref[...] += jnp.sum(d_ref[...], axis=0, keepdims=True)
        for k in range(taps):
            q, r = divmod(HALO - (taps - 1) + k, SUBLANES)
            dw_ref[pl.ds(k, 1), :] += jnp.sum(d_ref[...] * vs[r, pl.ds(q * SUBLANES, tt), :], axis=0, keepdims=True)

        def chunk(ci, carry):
            r0 = pl.multiple_of(ci * CONV_CHUNK, CONV_CHUNK)
            dv = jnp.zeros((CONV_CHUNK, cw), F32)
            for k in range(taps):
                q, r = divmod(taps - 1 - k, SUBLANES)
                dv = dv + w_ref[pl.ds(k, 1), :] * ds[r, pl.ds(r0 + q * SUBLANES, CONV_CHUNK), :]
            av = a_ref[pl.ds(r0, CONV_CHUNK), :].astype(F32)
            sg = _sigmoid(g_ref[pl.ds(r0, CONV_CHUNK), :].astype(F32))
            o_ref[pl.ds(r0, CONV_CHUNK), pl.ds(0, cw)] = (dv * sg).astype(o_ref.dtype)
            o_ref[pl.ds(r0, CONV_CHUNK), pl.ds(cw, cw)] = (dv * av * sg * (1.0 - sg)).astype(o_ref.dtype)
            return carry

        lax.fori_loop(0, tt // CONV_CHUNK, chunk, 0)

    tile = lambda c: pl.BlockSpec((tt, cw), lambda i, c=c: (i, c))
    halo = lambda c: pl.BlockSpec((HALO, cw), lambda i, c=c: (jnp.maximum(i * hb - 1, 0), c))
    nxt = pl.BlockSpec((HALO, cw), lambda i: (jnp.minimum((i + 1) * hb, nt * hb - 1), 0))
    outs, carried = _call(
        body, grid=(nt,),
        in_specs=[pl.BlockSpec((tt, cw), lambda i: (i, 0)), nxt, tile(0), tile(1), halo(0), halo(1),
                  pl.BlockSpec((taps, cw), lambda i: (0, 0))],
        out_specs=[pl.BlockSpec((tt, 2 * cw), lambda i: (i, 0)),
                   pl.BlockSpec((taps, cw), lambda i: (0, 0)), pl.BlockSpec((1, cw), lambda i: (0, 0))],
        out_shape=[jax.ShapeDtypeStruct((rows, 2 * cw), BF16), jax.ShapeDtypeStruct((taps, cw), F32),
                   jax.ShapeDtypeStruct((1, cw), F32)],
        scratch_shapes=[pltpu.VMEM((SUBLANES, HALO + tt, cw), F32), pltpu.VMEM((SUBLANES, tt + HALO, cw), F32)],
        semantics=("arbitrary",), name="conv_bwd", args=[dcv, dcv, proj, proj, proj, proj, w_dw], carry=carry)
    return outs if carry is None else (outs, carried)


def _merge_fwd(proj, y_conv, ya, yb, *, cw):
    rows = proj.shape[0]
    tr = _row_tile(rows, cw, 4)

    def body(gc_ref, gs_ref, yc_ref, ya_ref, yb_ref, o_ref):
        ys = ya_ref[...].astype(F32) * _sigmoid(yb_ref[...].astype(F32))
        o_ref[...] = (_sigmoid(gc_ref[...].astype(F32)) * yc_ref[...].astype(F32)
                      + _sigmoid(gs_ref[...].astype(F32)) * ys).astype(o_ref.dtype)

    blk = lambda off: pl.BlockSpec((tr, cw), lambda i, h, off=off: (i, off + h))
    return pl.pallas_call(
        body, grid=(rows // tr, 2), in_specs=[blk(3), blk(5), blk(0), blk(0), blk(0)], out_specs=blk(0),
        out_shape=jax.ShapeDtypeStruct((rows, 2 * cw), BF16),
        compiler_params=_cparams(("parallel", "parallel")), name="merge_fwd")(proj, proj, y_conv, ya, yb)


def _merge_bwd(dmerged, proj, y_conv, ya, yb, *, cw, carry=None):
    rows = proj.shape[0]
    tr = _row_tile(rows, cw, 6)

    def body(d_ref, gc_ref, gs_ref, yc_ref, ya_ref, yb_ref, dgc_ref, dgs_ref, dyc_ref, dya_ref, dyb_ref):
        d = d_ref[...].astype(F32)
        sc = _sigmoid(gc_ref[...].astype(F32))
        ss = _sigmoid(gs_ref[...].astype(F32))
        sb = _sigmoid(yb_ref[...].astype(F32))
        yav = ya_ref[...].astype(F32)
        dgc_ref[...] = (d * yc_ref[...].astype(F32) * sc * (1.0 - sc)).astype(dgc_ref.dtype)
        dgs_ref[...] = (d * (yav * sb) * ss * (1.0 - ss)).astype(dgs_ref.dtype)
        dyc_ref[...] = (d * sc).astype(dyc_ref.dtype)
        dys = d * ss
        dya_ref[...] = (dys * sb).astype(dya_ref.dtype)
        dyb_ref[...] = (dys * yav * sb * (1.0 - sb)).astype(dyb_ref.dtype)

    blk = lambda off: pl.BlockSpec((tr, cw), lambda i, h, off=off: (i, off + h))
    o2 = jax.ShapeDtypeStruct((rows, 2 * cw), BF16)
    outs, carried = _call(
        body, grid=(rows // tr, 2),
        in_specs=[blk(0), blk(3), blk(5), blk(0), blk(0), blk(0)],
        out_specs=[blk(0), blk(0), blk(0), blk(0), blk(0)],
        out_shape=[o2, o2, o2, o2, o2], scratch_shapes=[],
        semantics=("parallel", "parallel"), name="merge_bwd", args=[dmerged, proj, proj, y_conv, ya, yb],
        carry=carry)
    return outs if carry is None else (outs, carried)


def _res_norm(x, mo, gate, g, scale, shift):
    rows, d = x.shape
    tr = _row_tile(rows, d, 4)

    def body(x_ref, mo_ref, gt_ref, g_ref, sc_ref, sh_ref, h_ref, z_ref):
        h = x_ref[...] + gt_ref[...] * mo_ref[...].astype(F32)
        h_ref[...] = h
        r = lax.rsqrt(jnp.mean(h * h, axis=-1, keepdims=True) + EPS)
        z_ref[...] = ((h * r * g_ref[...]) * (1.0 + sc_ref[...]) + sh_ref[...]).astype(z_ref.dtype)

    row = pl.BlockSpec((tr, d), lambda i: (i, 0))
    vec = pl.BlockSpec((1, d), lambda i: (0, 0))
    return pl.pallas_call(
        body, grid=(rows // tr,), in_specs=[row, row, vec, vec, vec, vec], out_specs=[row, row],
        out_shape=[jax.ShapeDtypeStruct((rows, d), F32), jax.ShapeDtypeStruct((rows, d), BF16)],
        compiler_params=_cparams(("parallel",)), name="res_norm")(x, mo, gate, g, scale, shift)


def _final_fwd_bwd(h1, ff, gate2, final_g, target):
    rows, d = h1.shape
    tr = _row_tile(rows, d, 5)

    def body(h_ref, ff_ref, gt_ref, fg_ref, t_ref, dh_ref, dff_ref, loss_ref, dfg_ref, dgt_ref):
        i = pl.program_id(0)

        @pl.when(i == 0)
        def _():
            loss_ref[...] = jnp.zeros_like(loss_ref)
            dfg_ref[...] = jnp.zeros_like(dfg_ref)
            dgt_ref[...] = jnp.zeros_like(dgt_ref)

        ffv = ff_ref[...].astype(F32)
        h2 = h_ref[...] + gt_ref[...] * ffv
        r = lax.rsqrt(jnp.mean(h2 * h2, axis=-1, keepdims=True) + EPS)
        y = h2 * r
        e = y * fg_ref[...] - t_ref[...]
        loss_ref[...] += 0.5 * jnp.sum(jnp.mean(e * e, axis=-1, keepdims=True))
        dout = e * (1.0 / d)
        dfg_ref[...] += jnp.sum(dout * y, axis=0, keepdims=True)
        dy = dout * fg_ref[...]
        dh2 = r * (dy - y * jnp.mean(dy * y, axis=-1, keepdims=True))
        dh_ref[...] = dh2
        dgt_ref[...] += jnp.sum(dh2 * ffv, axis=0, keepdims=True)
        dff_ref[...] = (dh2 * gt_ref[...]).astype(dff_ref.dtype)

    row = pl.BlockSpec((tr, d), lambda i: (i, 0))
    vec = pl.BlockSpec((1, d), lambda i: (0, 0))
    return pl.pallas_call(
        body, grid=(rows // tr,), in_specs=[row, row, vec, vec, row],
        out_specs=[row, row, pl.BlockSpec((1, LANES), lambda i: (0, 0)), vec, vec],
        out_shape=[jax.ShapeDtypeStruct((rows, d), F32), jax.ShapeDtypeStruct((rows, d), BF16),
                   jax.ShapeDtypeStruct((1, LANES), F32), jax.ShapeDtypeStruct((1, d), F32),
                   jax.ShapeDtypeStruct((1, d), F32)],
        compiler_params=_cparams(("arbitrary",)), name="final_fwd_bwd")(h1, ff, gate2, final_g, target)


def _norm_mod_bwd(dz, hin, dres, g, scale, gate, mo, *, name, carry=None):
    rows, d = hin.shape
    with_gate = gate is not None
    tr = _row_tile(rows, d, 6)

    def body(*refs):
        if with_gate:
            (dz_ref, h_ref, dr_ref, g_ref, sc_ref, gt_ref, mo_ref,
             dh_ref, dsh_ref, dsc_ref, dg_ref, dmo_ref, dgt_ref) = refs
        else:
            dz_ref, h_ref, dr_ref, g_ref, sc_ref, dh_ref, dsh_ref, dsc_ref, dg_ref = refs
        i = pl.program_id(0)

        @pl.when(i == 0)
        def _():
            dsh_ref[...] = jnp.zeros_like(dsh_ref)
            dsc_ref[...] = jnp.zeros_like(dsc_ref)
            dg_ref[...] = jnp.zeros_like(dg_ref)
            if with_gate:
                dgt_ref[...] = jnp.zeros_like(dgt_ref)

        dzv = dz_ref[...].astype(F32)
        h = h_ref[...]
        r = lax.rsqrt(jnp.mean(h * h, axis=-1, keepdims=True) + EPS)
        y = h * r
        dsh_ref[...] += jnp.sum(dzv, axis=0, keepdims=True)
        dsc_ref[...] += jnp.sum(dzv * (y * g_ref[...]), axis=0, keepdims=True)
        dn = dzv * (1.0 + sc_ref[...])
        dg_ref[...] += jnp.sum(dn * y, axis=0, keepdims=True)
        dy = dn * g_ref[...]
        dh = dr_ref[...] + r * (dy - y * jnp.mean(dy * y, axis=-1, keepdims=True))
        dh_ref[...] = dh
        if with_gate:
            dmo_ref[...] = (dh * gt_ref[...]).astype(dmo_ref.dtype)
            dgt_ref[...] += jnp.sum(dh * mo_ref[...].astype(F32), axis=0, keepdims=True)

    row = pl.BlockSpec((tr, d), lambda i: (i, 0))
    vec = pl.BlockSpec((1, d), lambda i: (0, 0))
    vshape = jax.ShapeDtypeStruct((1, d), F32)
    in_specs = [row, row, row, vec, vec]
    args = [dz, hin, dres, g, scale]
    out_specs = [row, vec, vec, vec]
    out_shape = [jax.ShapeDtypeStruct((rows, d), F32), vshape, vshape, vshape]
    if with_gate:
        in_specs += [vec, row]
        args += [gate, mo]
        out_specs += [row, vec]
        out_shape += [jax.ShapeDtypeStruct((rows, d), BF16), vshape]
    outs, carried = _call(
        body, grid=(rows // tr,), in_specs=in_specs, out_specs=out_specs, out_shape=out_shape,
        scratch_shapes=[], semantics=("arbitrary",), name=name, args=args, carry=carry)
    return outs if carry is None else (outs, carried)


def _s5_discretise(a_re, a_im, log_dt, b_re, b_im):
    dt = jnp.exp(log_dt)[:, None]
    er = jnp.exp(a_re * dt)
    lr = er * jnp.cos(a_im * dt)
    li = er * jnp.sin(a_im * dt)
    den = a_re * a_re + a_im * a_im
    cr = ((lr - 1.0) * a_re + li * a_im) / den
    ci = (li * a_re - (lr - 1.0) * a_im) / den
    bbr = cr[..., None] * b_re - ci[..., None] * b_im
    bbi = cr[..., None] * b_im + ci[..., None] * b_re
    return lr, li, bbr, bbi


def _block_diag(w):
    g, r, c = w.shape
    nb = g // GROUPS_PER_BLOCK
    eye = jnp.eye(GROUPS_PER_BLOCK, dtype=w.dtype)
    w5 = w.reshape(nb, GROUPS_PER_BLOCK, r, 1, c) * eye[None, :, None, :, None]
    return w5.reshape(nb, GROUPS_PER_BLOCK * r, GROUPS_PER_BLOCK * c)


def _block_diag_extract(m, r, c):
    nb = m.shape[0]
    m5 = m.reshape(nb, GROUPS_PER_BLOCK, r, GROUPS_PER_BLOCK, c)
    idx = jnp.arange(GROUPS_PER_BLOCK)
    d = m5[:, idx, :, idx, :]
    return jnp.moveaxis(d, 0, 1).reshape(nb * GROUPS_PER_BLOCK, r, c)


def _scan_multipliers(lr, li):
    power = jnp.arange(1, SUBLANES + 1, dtype=F32)[None, :, None]
    er = jnp.exp(power * lr)
    pr = er * jnp.cos(power * li)
    pi = er * jnp.sin(power * li)
    rows = jnp.arange(SUBLANES)[None, :, None]
    fr, fi, rr, ri = [], [], [], []
    for s in (1, 2, 4):
        mf = (rows >= s).astype(F32)
        mr = (rows <= SUBLANES - 1 - s).astype(F32)
        fr.append(mf * pr[:, s - 1:s, :])
        fi.append(mf * pi[:, s - 1:s, :])
        rr.append(mr * pr[:, s - 1:s, :])
        ri.append(mr * pi[:, s - 1:s, :])
    fr.append(pr)
    fi.append(pi)
    rr.append(pr[:, ::-1, :])
    ri.append(pi[:, ::-1, :])
    st = lambda xs: jnp.stack(xs, axis=1)
    return st(fr), st(fi), st(rr), st(ri)


def _scan_rows(sre, sim, mul_r, mul_i, n_groups, reverse):
    sgn = -1.0 if reverse else 1.0
    lanes = sre.shape[1]

    def step(k, carry):
        cr, ci = carry
        kk = (n_groups - 1 - k) if reverse else k
        r0 = pl.multiple_of(kk * SUBLANES, SUBLANES)
        xr = sre[pl.ds(r0, SUBLANES), :]
        xi = sim[pl.ds(r0, SUBLANES), :]
        for lvl, s in enumerate((1, 2, 4)):
            sh = (SUBLANES - s) if reverse else s
            nr = pltpu.roll(xr, sh, 0)
            ni = pltpu.roll(xi, sh, 0)
            mr = mul_r[lvl]
            mi = mul_i[lvl] * sgn
            xr, xi = xr + mr * nr - mi * ni, xi + mr * ni + mi * nr
        mr = mul_r[3]
        mi = mul_i[3] * sgn
        xr, xi = xr + mr * cr - mi * ci, xi + mr * ci + mi * cr
        sre[pl.ds(r0, SUBLANES), :] = xr
        sim[pl.ds(r0, SUBLANES), :] = xi
        edge = 0 if reverse else SUBLANES - 1
        ncr = jnp.broadcast_to(xr[edge:edge + 1, :], (SUBLANES, lanes))
        nci = jnp.broadcast_to(xi[edge:edge + 1, :], (SUBLANES, lanes))
        return ncr, nci

    zero = jnp.zeros((SUBLANES, lanes), F32)
    lax.fori_loop(0, n_groups, step, (zero, zero))


def _dot(a, b):
    return jnp.dot(a, b, preferred_element_type=F32)


def _dotf(a, b):
    return _dot(a.astype(BF16), b)


def _s5_operands(lr, li, bbr, bbi, c_re, c_im):
    g = lr.shape[0]
    nb = g // GROUPS_PER_BLOCK
    tb = lambda w: jnp.swapaxes(w, 1, 2)
    b_in = [_block_diag(tb(bbr)), _block_diag(tb(bbi))]
    c_out = [_block_diag(tb(c_re)), _block_diag(tb(c_im))]
    b_out = [_block_diag(bbr), _block_diag(bbi)]
    c_in = [_block_diag(c_re), _block_diag(c_im)]
    lam_r = lr.reshape(nb, 1, STATE_LANES)
    lam_i = li.reshape(nb, 1, STATE_LANES)
    mults = _scan_multipliers(lam_r, lam_i)
    cast = lambda ws: [w.astype(BF16) for w in ws]
    return cast(b_in), cast(c_out), cast(b_out), cast(c_in), mults


def _s5_fwd(proj, d_skip, b_in, c_out, mults, *, col0, carry=None):
    rows = proj.shape[0]
    nb = b_in[0].shape[0]
    tm = _pick(rows, 512, SUBLANES)
    n_tiles = rows // tm
    s_l = STATE_LANES

    def body(u_ref, dk_ref, br, bi, cr, ci, fr_ref, fi_ref, o_ref, sr_ref, si_ref, sre, sim):
        for t in range(n_tiles):
            rs = pl.ds(t * tm, tm)
            ub = u_ref[rs, :]
            sre[rs, :] = _dot(ub, br[...])
            sim[rs, :] = _dot(ub, bi[...])
        _scan_rows(sre, sim, fr_ref, fi_ref, rows // SUBLANES, False)
        for t in range(n_tiles):
            rs = pl.ds(t * tm, tm)
            srb = sre[rs, :].astype(BF16)
            sib = sim[rs, :].astype(BF16)
            sr_ref[rs, :] = srb
            si_ref[rs, :] = sib
            y0 = _dot(srb, cr[...]) - _dot(sib, ci[...])
            y1 = y0 + dk_ref[...] * u_ref[rs, :].astype(F32)
            o_ref[rs, :] = _gelu(y1).astype(o_ref.dtype)

    mat_in = pl.BlockSpec((None, LANES, s_l), lambda g: (g, 0, 0))
    mat_out = pl.BlockSpec((None, s_l, LANES), lambda g: (g, 0, 0))
    mul = pl.BlockSpec((None, 4, SUBLANES, s_l), lambda g: (g, 0, 0, 0))
    state = pl.BlockSpec((rows, s_l), lambda g: (0, g))
    outs, carried = _call(
        body, grid=(nb,),
        in_specs=[pl.BlockSpec((rows, LANES), lambda g: (0, col0 + g)), pl.BlockSpec((1, LANES), lambda g: (0, g))]
        + [mat_in] * 2 + [mat_out] * 2 + [mul] * 2,
        out_specs=[pl.BlockSpec((rows, LANES), lambda g: (0, g)), state, state],
        out_shape=[jax.ShapeDtypeStruct((rows, nb * LANES), BF16), jax.ShapeDtypeStruct((rows, nb * s_l), BF16),
                   jax.ShapeDtypeStruct((rows, nb * s_l), BF16)],
        scratch_shapes=[pltpu.VMEM((rows, s_l), F32), pltpu.VMEM((rows, s_l), F32)],
        semantics=("parallel",), name="s5_fwd", args=[proj, d_skip, *b_in, *c_out, mults[0], mults[1]], carry=carry)
    return outs if carry is None else (outs, carried)


def _s5_bwd(proj, dyg, d_skip, states, c_out, b_out, c_in, mults, *, col0, carry=None):
    rows = proj.shape[0]
    nb = c_out[0].shape[0]
    tm = _pick(rows, 512, SUBLANES)
    n_tiles = rows // tm
    s_l = STATE_LANES
    n_groups = rows // SUBLANES
    tn = (((0,), (0,)), ((), ()))

    def body(u_ref, dy_ref, dk_ref, sr_ref, si_ref, cr, ci, bor, boi, cir, cii, rr_ref, ri_ref,
             du_ref, ddk_ref, dbr_ref, dbi_ref, dcr_ref, dci_ref, dlr_ref, dli_ref,
             gre, gim, dy1):
        ddk = jnp.zeros((1, LANES), F32)
        dcr = jnp.zeros((s_l, LANES), F32)
        dci = jnp.zeros((s_l, LANES), F32)
        for t in range(n_tiles):
            rs = pl.ds(t * tm, tm)
            srb = sr_ref[rs, :]
            sib = si_ref[rs, :]
            uf = u_ref[rs, :].astype(F32)
            y0 = _dot(srb, cr[...]) - _dot(sib, ci[...])
            y1 = y0 + dk_ref[...] * uf
            d1 = dy_ref[rs, :].astype(F32) * _gelu_grad(y1)
            dy1[rs, :] = d1
            ddk = ddk + jnp.sum(d1 * uf, axis=0, keepdims=True)
            d1b = d1.astype(BF16)
            dcr = dcr + lax.dot_general(srb, d1b, tn, preferred_element_type=F32)
            dci = dci - lax.dot_general(sib, d1b, tn, preferred_element_type=F32)
            gre[rs, :] = _dot(d1b, cir[...])
            gim[rs, :] = -_dot(d1b, cii[...])
        ddk_ref[...] = ddk
        dcr_ref[...] = dcr
        dci_ref[...] = dci

        last_row = lax.broadcasted_iota(jnp.int32, (SUBLANES, s_l), 0) == SUBLANES - 1

        def group(r0, s_r, s_i, carry):
            cr_, ci_, ar, ai = carry
            xr = gre[pl.ds(r0, SUBLANES), :]
            xi = gim[pl.ds(r0, SUBLANES), :]
            for lvl, s in enumerate((1, 2, 4)):
                nr = pltpu.roll(xr, SUBLANES - s, 0)
                ni = pltpu.roll(xi, SUBLANES - s, 0)
                mr = rr_ref[lvl]
                mi = ri_ref[lvl]
                xr, xi = xr + mr * nr + mi * ni, xi + mr * ni - mi * nr
            mr = rr_ref[3]
            mi = ri_ref[3]
            xr, xi = xr + mr * cr_ + mi * ci_, xi + mr * ci_ - mi * cr_
            gre[pl.ds(r0, SUBLANES), :] = xr
            gim[pl.ds(r0, SUBLANES), :] = xi
            nxt_r = jnp.where(last_row, cr_, pltpu.roll(xr, SUBLANES - 1, 0))
            nxt_i = jnp.where(last_row, ci_, pltpu.roll(xi, SUBLANES - 1, 0))
            ncr = jnp.broadcast_to(xr[0:1, :], (SUBLANES, s_l))
            nci = jnp.broadcast_to(xi[0:1, :], (SUBLANES, s_l))
            return ncr, nci, ar + nxt_r * s_r + nxt_i * s_i, ai + nxt_i * s_r - nxt_r * s_i

        def rev_step(k, carry):
            r0 = pl.multiple_of((n_groups // 2 - 1 - k) * 2 * SUBLANES, 2 * SUBLANES)
            s_r = sr_ref[pl.ds(r0, 2 * SUBLANES), :].astype(F32)
            s_i = si_ref[pl.ds(r0, 2 * SUBLANES), :].astype(F32)
            carry = group(r0 + SUBLANES, s_r[SUBLANES:], s_i[SUBLANES:], carry)
            return group(r0, s_r[:SUBLANES], s_i[:SUBLANES], carry)

        zero = jnp.zeros((SUBLANES, s_l), F32)
        _, _, ar, ai = lax.fori_loop(0, n_groups // 2, rev_step, (zero, zero, zero, zero))
        dlr_ref[...] = jnp.sum(ar, axis=0, keepdims=True)
        dli_ref[...] = jnp.sum(ai, axis=0, keepdims=True)

        dbr = jnp.zeros((LANES, s_l), F32)
        dbi = jnp.zeros((LANES, s_l), F32)
        for t in range(n_tiles):
            rs = pl.ds(t * tm, tm)
            gr = gre[rs, :]
            gi = gim[rs, :]
            grb = gr.astype(BF16)
            gib = gi.astype(BF16)
            du = _dot(grb, bor[...]) + _dot(gib, boi[...]) + dy1[rs, :] * dk_ref[...]
            du_ref[rs, :] = du.astype(du_ref.dtype)
            ub = u_ref[rs, :]
            dbr = dbr + lax.dot_general(ub, grb, tn, preferred_element_type=F32)
            dbi = dbi + lax.dot_general(ub, gib, tn, preferred_element_type=F32)
        dbr_ref[...] = dbr
        dbi_ref[...] = dbi

    mat_in = pl.BlockSpec((None, LANES, s_l), lambda g: (g, 0, 0))
    mat_out = pl.BlockSpec((None, s_l, LANES), lambda g: (g, 0, 0))
    mul = pl.BlockSpec((None, 4, SUBLANES, s_l), lambda g: (g, 0, 0, 0))
    lam = pl.BlockSpec((None, 1, s_l), lambda g: (g, 0, 0))
    col = pl.BlockSpec((rows, LANES), lambda g: (0, g))
    vec = pl.BlockSpec((1, LANES), lambda g: (0, g))
    state = pl.BlockSpec((rows, s_l), lambda g: (0, g))
    outs, carried = _call(
        body, grid=(nb,),
        in_specs=[pl.BlockSpec((rows, LANES), lambda g: (0, col0 + g)), col, vec]
        + [state] * 2 + [mat_out] * 2 + [mat_out] * 2 + [mat_in] * 2 + [mul] * 2,
        out_specs=[col, vec, mat_in, mat_in, mat_out, mat_out, lam, lam],
        out_shape=[jax.ShapeDtypeStruct((rows, nb * LANES), BF16), jax.ShapeDtypeStruct((1, nb * LANES), F32),
                   jax.ShapeDtypeStruct((nb, LANES, s_l), F32), jax.ShapeDtypeStruct((nb, LANES, s_l), F32),
                   jax.ShapeDtypeStruct((nb, s_l, LANES), F32), jax.ShapeDtypeStruct((nb, s_l, LANES), F32),
                   jax.ShapeDtypeStruct((nb, 1, s_l), F32), jax.ShapeDtypeStruct((nb, 1, s_l), F32)],
        scratch_shapes=[pltpu.VMEM((rows, s_l), F32)] * 2 + [pltpu.VMEM((rows, LANES), F32)],
        semantics=("parallel",), name="s5_bwd",
        args=[proj, dyg, d_skip, *states, *c_out, *b_out, *c_in, mults[2], mults[3]], carry=carry)
    return outs if carry is None else (outs, carried)


def _silu(v):
    return v * _sigmoid(v)


def _ada_fwd(c_all, w_shard, b_cols):
    d, n = w_shard.shape
    bn = _pick(n, 512)

    def body(c_ref, w_ref, b_ref, o_ref):
        ca = _silu(c_ref[...]).astype(BF16)
        o_ref[...] = _dot(ca, w_ref[...].astype(BF16)) + b_ref[...]

    return pl.pallas_call(
        body, grid=(n // bn,),
        in_specs=[pl.BlockSpec((N_DEV, d), lambda j: (0, 0)), pl.BlockSpec((d, bn), lambda j: (0, j)),
                  pl.BlockSpec((1, bn), lambda j: (0, j))],
        out_specs=pl.BlockSpec((N_DEV, bn), lambda j: (0, j)),
        out_shape=jax.ShapeDtypeStruct((N_DEV, n), F32),
        compiler_params=_cparams(("parallel",)), name="ada_fwd")(c_all, w_shard, b_cols)


def _ada_bwd(c_all, dmod_cols):
    d = c_all.shape[1]
    n = dmod_cols.shape[1]
    bn = _pick(n, 512)

    def body(c_ref, g_ref, o_ref):
        ca = _silu(c_ref[...]).astype(BF16)
        o_ref[...] = lax.dot_general(ca, g_ref[...].astype(BF16), (((0,), (0,)), ((), ())),
                                     preferred_element_type=F32)

    return pl.pallas_call(
        body, grid=(n // bn,),
        in_specs=[pl.BlockSpec((N_DEV, d), lambda j: (0, 0)), pl.BlockSpec((N_DEV, bn), lambda j: (0, j))],
        out_specs=pl.BlockSpec((d, bn), lambda j: (0, j)),
        out_shape=jax.ShapeDtypeStruct((d, n), F32),
        compiler_params=_cparams(("parallel",)), name="ada_bwd")(c_all, dmod_cols)


def _cast_bf16(w, *, name):
    rows, cols = w.shape
    tr = _row_tile(rows, cols, 2)

    def body(w_ref, o_ref):
        o_ref[...] = w_ref[...].astype(BF16)

    row = pl.BlockSpec((tr, cols), lambda i: (i, 0))
    return pl.pallas_call(
        body, grid=(rows // tr,), in_specs=[row], out_specs=row,
        out_shape=jax.ShapeDtypeStruct((rows, cols), BF16),
        compiler_params=_cparams(("parallel",)), name=name)(w)


def _adamw(w, g, m, v, *, name, carry=None):
    rows, cols = w.shape
    tr = _row_tile(rows, cols, 7)
    c1 = 1.0 / (1.0 - ADAM_B1 ** ADAM_STEP)
    c2 = 1.0 / (1.0 - ADAM_B2 ** ADAM_STEP)

    def body(w_ref, g_ref, m_ref, v_ref, d_ref, nm_ref, nv_ref):
        gv = g_ref[...]
        nm = ADAM_B1 * m_ref[...] + (1.0 - ADAM_B1) * gv
        nv = ADAM_B2 * v_ref[...] + (1.0 - ADAM_B2) * (gv * gv)
        nm_ref[...] = nm
        nv_ref[...] = nv
        d_ref[...] = -ADAM_LR * ((nm * c1) / (jnp.sqrt(nv * c2) + ADAM_EPS) + ADAM_WD * w_ref[...])

    row = pl.BlockSpec((tr, cols), lambda i: (i, 0))
    shp = jax.ShapeDtypeStruct((rows, cols), F32)
    outs, carried = _call(
        body, grid=(rows // tr,), in_specs=[row] * 4, out_specs=[row] * 3, out_shape=[shp] * 3,
        scratch_shapes=[], semantics=("parallel",), name=name, args=[w, g, m, v], carry=carry)
    return outs if carry is None else (outs, carried)


def _sum_leading(a, *, name, out_dtype=F32):
    n, rows, cols = a.shape
    tr = _row_tile(rows, cols, n + 1)

    def body(a_ref, o_ref):
        acc = a_ref[0].astype(F32)
        for i in range(1, n):
            acc = acc + a_ref[i].astype(F32)
        o_ref[...] = acc.astype(o_ref.dtype)

    return pl.pallas_call(
        body, grid=(rows // tr,), in_specs=[pl.BlockSpec((n, tr, cols), lambda i: (0, i, 0))],
        out_specs=pl.BlockSpec((tr, cols), lambda i: (i, 0)),
        out_shape=jax.ShapeDtypeStruct((rows, cols), out_dtype),
        compiler_params=_cparams(("parallel",)), name=name)(a)


def _add_half(dw, land, my_c, *, name):
    n, r, cols = dw.shape
    h = r // 2
    tr = _row_tile(h, cols, 3)
    hb = h // tr

    def body(c_ref, a_ref, b_ref, o_ref):
        o_ref[...] = (a_ref[...].astype(F32) + b_ref[...].astype(F32)).astype(o_ref.dtype)

    gs = pltpu.PrefetchScalarGridSpec(
        num_scalar_prefetch=1, grid=(n, hb),
        in_specs=[pl.BlockSpec((None, tr, cols), lambda s, i, c_ref: (s, c_ref[0] * hb + i, 0)),
                  pl.BlockSpec((None, tr, cols), lambda s, i, c_ref: (s, i, 0))],
        out_specs=pl.BlockSpec((None, tr, cols), lambda s, i, c_ref: (s, i, 0)))
    return pl.pallas_call(
        body, grid_spec=gs, out_shape=jax.ShapeDtypeStruct((n, h, cols), BF16),
        compiler_params=_cparams(("parallel", "parallel")), name=name)(my_c, dw, land)


def _mesh_pos():
    return lax.axis_index("x"), lax.axis_index("y"), lax.axis_index("c")


def _other_chips(x, y):
    return [(1 - x, y), (x, 1 - y), (1 - x, 1 - y)]


def _gather_small(blk, *, name):
    m_per, n = blk.shape

    def body(x_ref, out_ref, send_sems, recv_sems, local_sem):
        x, y, c = _mesh_pos()
        me, sibling = (x, y, c), (x, y, 1 - c)
        chips = _other_chips(x, y)

        def rows(px, py, pc):
            return out_ref.at[pl.ds((4 * px + 2 * py + pc) * m_per, m_per), :]

        def copy(k, block, to, src=None):
            return pltpu.make_async_remote_copy(
                src_ref=rows(*block) if src is None else src, dst_ref=rows(*block),
                send_sem=send_sems.at[k], recv_sem=recv_sems.at[k], device_id=to, device_id_type=MESH)

        mine = pltpu.make_async_copy(x_ref, rows(*me), local_sem)
        mine.start()
        first = [copy(0, me, sibling, src=x_ref)]
        first += [copy(1 + j, me, (*chip, c), src=x_ref) for j, chip in enumerate(chips)]
        for cp in first:
            cp.start()
        passed = [copy(4 + j, (*chip, c), sibling) for j, chip in enumerate(chips)]
        for j, chip in enumerate(chips):
            copy(1 + j, (*chip, c), me).wait_recv()
            passed[j].start()
        copy(0, sibling, me).wait_recv()
        for j, chip in enumerate(chips):
            copy(4 + j, (*chip, 1 - c), me).wait_recv()
        for cp in first + passed:
            cp.wait_send()
        mine.wait()

    return pl.pallas_call(
        body, out_shape=jax.ShapeDtypeStruct((N_DEV * m_per, n), blk.dtype),
        in_specs=[pl.BlockSpec(memory_space=pltpu.VMEM)], out_specs=pl.BlockSpec(memory_space=pltpu.VMEM),
        scratch_shapes=[pltpu.SemaphoreType.DMA((7,)), pltpu.SemaphoreType.DMA((7,)), pltpu.SemaphoreType.DMA],
        compiler_params=pltpu.CompilerParams(vmem_limit_bytes=VMEM_LIMIT_BYTES), name=name)(blk)


def _hbm_specs(n):
    return [pl.BlockSpec(memory_space=pl.ANY)] * n


def _gather_weights(shards):
    n = len(shards)

    def body(*refs):
        ins, outs = refs[:n], refs[n:2 * n]
        send_sems, recv_sems, local_sems = refs[2 * n:]
        x, y, c = _mesh_pos()
        me_chip = 2 * x + y
        sibling = (x, y, 1 - c)
        chips = _other_chips(x, y)

        def half(w, chip_idx, pc):
            h = shards[w].shape[0] // 2
            return outs[w].at[chip_idx, pl.ds(pc * h, h), :]

        def copy(w, k, chip_idx, pc, to, src=None):
            dst = half(w, chip_idx, pc)
            return pltpu.make_async_remote_copy(
                src_ref=dst if src is None else src, dst_ref=dst,
                send_sem=send_sems.at[6 * w + k], recv_sem=recv_sems.at[6 * w + k],
                device_id=to, device_id_type=MESH)

        local = [pltpu.make_async_copy(ins[w], outs[w].at[me_chip], local_sems.at[w]) for w in range(n)]
        for cp in local:
            cp.start()
        sends = []
        for w in range(n):
            h = shards[w].shape[0] // 2
            for j, chip in enumerate(chips):
                cp = copy(w, j, me_chip, c, (*chip, c), src=ins[w].at[pl.ds(c * h, h), :])
                cp.start()
                sends.append(cp)
        for w in range(n):
            for j, chip in enumerate(chips):
                chip_idx = 2 * chip[0] + chip[1]
                copy(w, j, chip_idx, c, (x, y, c)).wait_recv()
                cp = copy(w, 3 + j, chip_idx, c, sibling)
                cp.start()
                sends.append(cp)
        for w in range(n):
            for j, chip in enumerate(chips):
                copy(w, 3 + j, 2 * chip[0] + chip[1], 1 - c, (x, y, c)).wait_recv()
        for cp in sends:
            cp.wait_send()
        for cp in local:
            cp.wait()

    return pl.pallas_call(
        body, out_shape=[jax.ShapeDtypeStruct((N_CHIPS,) + s.shape, s.dtype) for s in shards],
        in_specs=_hbm_specs(n), out_specs=_hbm_specs(n),
        scratch_shapes=[pltpu.SemaphoreType.DMA((6 * n,)), pltpu.SemaphoreType.DMA((6 * n,)),
                        pltpu.SemaphoreType.DMA((n,))],
        name="gather_weights")(*shards)


def _swap_halves(dws, *, name):
    n = len(dws)

    def body(*refs):
        ins, outs = refs[:n], refs[n:2 * n]
        send_sems, recv_sems = refs[2 * n:]
        x, y, c = _mesh_pos()
        cps = []
        for w in range(n):
            h = dws[w].shape[1] // 2
            cp = pltpu.make_async_remote_copy(
                src_ref=ins[w].at[:, pl.ds((1 - c) * h, h), :], dst_ref=outs[w],
                send_sem=send_sems.at[w], recv_sem=recv_sems.at[w],
                device_id=(x, y, 1 - c), device_id_type=MESH)
            cp.start()
            cps.append(cp)
        for cp in cps:
            cp.wait()

    return pl.pallas_call(
        body, out_shape=[jax.ShapeDtypeStruct((s.shape[0], s.shape[1] // 2, s.shape[2]), s.dtype) for s in dws],
        in_specs=_hbm_specs(n), out_specs=_hbm_specs(n),
        scratch_shapes=[pltpu.SemaphoreType.DMA((n,)), pltpu.SemaphoreType.DMA((n,))],
        name=name)(*dws)


def _chip_exchange(parts):
    n = len(parts)

    def body(*refs):
        ins, outs = refs[:n], refs[n:2 * n]
        send_sems, recv_sems, local_sems = refs[2 * n:]
        x, y, c = _mesh_pos()
        me_chip = 2 * x + y
        chips = _other_chips(x, y)
        local = [pltpu.make_async_copy(ins[w].at[me_chip], outs[w].at[me_chip], local_sems.at[w]) for w in range(n)]
        for cp in local:
            cp.start()
        cps = []
        for w in range(n):
            for j, chip in enumerate(chips):
                cp = pltpu.make_async_remote_copy(
                    src_ref=ins[w].at[2 * chip[0] + chip[1]], dst_ref=outs[w].at[me_chip],
                    send_sem=send_sems.at[3 * w + j], recv_sem=recv_sems.at[3 * w + j],
                    device_id=(*chip, c), device_id_type=MESH)
                cp.start()
                cps.append((cp, w, j, chip))
        for cp, w, j, chip in cps:
            slot = outs[w].at[2 * chip[0] + chip[1]]
            pltpu.make_async_remote_copy(
                src_ref=slot, dst_ref=slot, send_sem=send_sems.at[3 * w + j], recv_sem=recv_sems.at[3 * w + j],
                device_id=(x, y, c), device_id_type=MESH).wait_recv()
        for cp, _, _, _ in cps:
            cp.wait_send()
        for cp in local:
            cp.wait()

    return pl.pallas_call(
        body, out_shape=[jax.ShapeDtypeStruct(s.shape, s.dtype) for s in parts],
        in_specs=_hbm_specs(n), out_specs=_hbm_specs(n),
        scratch_shapes=[pltpu.SemaphoreType.DMA((3 * n,)), pltpu.SemaphoreType.DMA((3 * n,)),
                        pltpu.SemaphoreType.DMA((n,))],
        name="chip_exchange")(*parts)


def _join_halves(halves):
    n = len(halves)

    def body(*refs):
        ins, outs = refs[:n], refs[n:2 * n]
        send_sems, recv_sems, local_sems = refs[2 * n:]
        x, y, c = _mesh_pos()
        cps, local = [], []
        for w in range(n):
            h = halves[w].shape[0]
            mine = outs[w].at[pl.ds(c * h, h), :]
            lc = pltpu.make_async_copy(ins[w], mine, local_sems.at[w])
            lc.start()
            local.append(lc)
            cp = pltpu.make_async_remote_copy(
                src_ref=ins[w], dst_ref=mine, send_sem=send_sems.at[w], recv_sem=recv_sems.at[w],
                device_id=(x, y, 1 - c), device_id_type=MESH)
            cp.start()
            cps.append(cp)
        for w in range(n):
            h = halves[w].shape[0]
            theirs = outs[w].at[pl.ds((1 - c) * h, h), :]
            pltpu.make_async_remote_copy(
                src_ref=theirs, dst_ref=theirs, send_sem=send_sems.at[w], recv_sem=recv_sems.at[w],
                device_id=(x, y, c), device_id_type=MESH).wait_recv()
        for cp in cps:
            cp.wait_send()
        for lc in local:
            lc.wait()

    return pl.pallas_call(
        body, out_shape=[jax.ShapeDtypeStruct((2 * s.shape[0], s.shape[1]), s.dtype) for s in halves],
        in_specs=_hbm_specs(n), out_specs=_hbm_specs(n),
        scratch_shapes=[pltpu.SemaphoreType.DMA((n,)), pltpu.SemaphoreType.DMA((n,)), pltpu.SemaphoreType.DMA((n,))],
        name="join_halves")(*halves)


def _cast_into_slot(w, chip, *, name):
    rows, cols = w.shape
    tr = _row_tile(rows, cols, 2)

    def body(chip_ref, w_ref, o_ref):
        o_ref[...] = w_ref[...].astype(BF16)

    gs = pltpu.PrefetchScalarGridSpec(
        num_scalar_prefetch=1, grid=(rows // tr,),
        in_specs=[pl.BlockSpec((tr, cols), lambda i, chip_ref: (i, 0))],
        out_specs=pl.BlockSpec((None, tr, cols), lambda i, chip_ref: (chip_ref[0], i, 0)))
    return pl.pallas_call(
        body, grid_spec=gs, out_shape=jax.ShapeDtypeStruct((N_CHIPS, rows, cols), BF16),
        compiler_params=_cparams(("parallel",)), name=name)(chip, w)


def _row_range(h, lo, hi, parts):
    step = h // parts
    assert step * parts == h and step % (2 * SUBLANES) == 0, (h, parts)
    return lo * step, (hi - lo) * step


def _gather_carry(items):
    n_copies = sum(len(js) for _, js, _, _, _ in items)
    sem = pltpu.SemaphoreType.DMA((2 * n_copies,))

    def copies(outs, sems):
        send_sems, recv_sems = sems
        x, y, c = _mesh_pos()
        me_chip = 2 * x + y
        chips = _other_chips(x, y)
        out_ici, in_ici, out_d2d, in_d2d = [], [], [], []
        k = 0
        for w, (buf, js, lo, hi, parts) in enumerate(items):
            h = buf.shape[1] // 2
            r0, nr = _row_range(h, lo, hi, parts)

            def copy(k, chip_idx, pc, to):
                ref = outs[w].at[chip_idx, pl.ds(pc * h + r0, nr), :]
                return pltpu.make_async_remote_copy(
                    src_ref=ref, dst_ref=ref, send_sem=send_sems.at[k], recv_sem=recv_sems.at[k],
                    device_id=to, device_id_type=MESH)

            for j in js:
                chip = chips[j]
                chip_idx = 2 * chip[0] + chip[1]
                out_ici.append(copy(k, me_chip, c, (*chip, c)))
                in_ici.append(copy(k, chip_idx, c, (x, y, c)))
                out_d2d.append(copy(k + 1, chip_idx, c, (x, y, 1 - c)))
                in_d2d.append(copy(k + 1, chip_idx, 1 - c, (x, y, c)))
                k += 2
        return out_ici, in_ici, out_d2d, in_d2d

    def start(ins, outs, sems):
        for cp in copies(outs, sems)[0]:
            cp.start()

    def finish(ins, outs, sems):
        out_ici, in_ici, out_d2d, in_d2d = copies(outs, sems)
        for arrived, onward in zip(in_ici, out_d2d):
            arrived.wait_recv()
            onward.start()
        for arrived in in_d2d:
            arrived.wait_recv()
        for cp in out_ici + out_d2d:
            cp.wait_send()

    bufs = [it[0] for it in items]
    shapes = [jax.ShapeDtypeStruct(b.shape, b.dtype) for b in bufs]
    return _Carry(bufs, shapes, {i: i for i in range(len(bufs))}, [sem, sem], start, finish)


def _exchange_carry(items):
    n = len(items)
    sem = pltpu.SemaphoreType.DMA((3 * n,))
    given = [w for w in range(n) if items[w][1] is not None]

    def copies(ins, outs, sems):
        send_sems, recv_sems = sems
        x, y, c = _mesh_pos()
        chips = _other_chips(x, y)
        sends, recvs = [], []
        for w, (part, _, lo, hi, parts) in enumerate(items):
            r0, nr = _row_range(part.shape[1], lo, hi, parts)
            for j, chip in enumerate(chips):
                land = outs[w].at[j, pl.ds(r0, nr), :]
                sends.append(pltpu.make_async_remote_copy(
                    src_ref=ins[w].at[2 * chip[0] + chip[1], pl.ds(r0, nr), :], dst_ref=land,
                    send_sem=send_sems.at[3 * w + j], recv_sem=recv_sems.at[3 * w + j],
                    device_id=(*chip, c), device_id_type=MESH))
                recvs.append(pltpu.make_async_remote_copy(
                    src_ref=land, dst_ref=land,
                    send_sem=send_sems.at[3 * w + j], recv_sem=recv_sems.at[3 * w + j],
                    device_id=(x, y, c), device_id_type=MESH))
        return sends, recvs

    def start(ins, outs, sems):
        for cp in copies(ins, outs, sems)[0]:
            cp.start()

    def finish(ins, outs, sems):
        sends, recvs = copies(ins, outs, sems)
        for cp in recvs:
            cp.wait_recv()
        for cp in sends:
            cp.wait_send()

    inputs = [it[0] for it in items] + [items[w][1] for w in given]
    shapes = [jax.ShapeDtypeStruct((3,) + it[0].shape[1:], it[0].dtype) for it in items]
    aliases = {n + i: w for i, w in enumerate(given)}
    return _Carry(inputs, shapes, aliases, [sem, sem], start, finish)


def _sum_into_half(part, landed, chip, my_c, *, name):
    _, h, cols = part.shape
    tr = _row_tile(h, cols, 5)
    hb = h // tr

    def body(chip_ref, c_ref, p_ref, l_ref, o_ref):
        acc = p_ref[...].astype(F32)
        for j in range(3):
            acc = acc + l_ref[j].astype(F32)
        o_ref[...] = acc

    gs = pltpu.PrefetchScalarGridSpec(
        num_scalar_prefetch=2, grid=(hb,),
        in_specs=[pl.BlockSpec((None, tr, cols), lambda i, chip_ref, c_ref: (chip_ref[0], i, 0)),
                  pl.BlockSpec((3, tr, cols), lambda i, chip_ref, c_ref: (0, i, 0))],
        out_specs=pl.BlockSpec((tr, cols), lambda i, chip_ref, c_ref: (c_ref[0] * hb + i, 0)))
    return pl.pallas_call(
        body, grid_spec=gs, out_shape=jax.ShapeDtypeStruct((2 * h, cols), F32),
        compiler_params=_cparams(("parallel",)), name=name)(chip, my_c, part, landed)


def _join_carry(fulls):
    n = len(fulls)
    sem = pltpu.SemaphoreType.DMA((n,))

    def copies(outs, sems):
        send_sems, recv_sems = sems
        x, y, c = _mesh_pos()
        sends, recvs = [], []
        for w in range(n):
            h = fulls[w].shape[0] // 2
            mine = outs[w].at[pl.ds(c * h, h), :]
            theirs = outs[w].at[pl.ds((1 - c) * h, h), :]
            sends.append(pltpu.make_async_remote_copy(
                src_ref=mine, dst_ref=mine, send_sem=send_sems.at[w], recv_sem=recv_sems.at[w],
                device_id=(x, y, 1 - c), device_id_type=MESH))
            recvs.append(pltpu.make_async_remote_copy(
                src_ref=theirs, dst_ref=theirs, send_sem=send_sems.at[w], recv_sem=recv_sems.at[w],
                device_id=(x, y, c), device_id_type=MESH))
        return sends, recvs

    def start(ins, outs, sems):
        for cp in copies(outs, sems)[0]:
            cp.start()

    def finish(ins, outs, sems):
        sends, recvs = copies(outs, sems)
        for cp in recvs:
            cp.wait_recv()
        for cp in sends:
            cp.wait_send()

    shapes = [jax.ShapeDtypeStruct(f.shape, f.dtype) for f in fulls]
    return _Carry(fulls, shapes, {i: i for i in range(n)}, [sem, sem], start, finish)


class _NoComm:
    def __init__(self, big):
        self.big = big
        self.grads = {}

    def weight(self, name):
        return self.big[name]

    def mm_in(self, u):
        return _mm(u, self.big["w_in"], mode="nn", out_dtype=BF16, name="mm_in")

    def mm_d_in(self, dproj):
        return _mm(dproj, self.big["w_in"], mode="nt", out_dtype=F32, name="mm_d_in")

    def carry(self, site):
        return None

    def done(self, site, carried):
        pass

    def grad(self, name, dw):
        self.grads[name] = dw

    def early_grads(self, early):
        self.early = early


def _gather_rows_carry(blk):
    m_per = blk.shape[0]
    sem = pltpu.SemaphoreType.DMA((7,))

    def copies(ins, outs, sems):
        send_sems, recv_sems, local_sem = sems
        x, y, c = _mesh_pos()
        me, sibling = (x, y, c), (x, y, 1 - c)
        chips = _other_chips(x, y)

        def rows(px, py, pc):
            return outs[0].at[pl.ds((4 * px + 2 * py + pc) * m_per, m_per), :]

        def copy(k, block, to, src=None):
            return pltpu.make_async_remote_copy(
                src_ref=rows(*block) if src is None else src, dst_ref=rows(*block),
                send_sem=send_sems.at[k], recv_sem=recv_sems.at[k], device_id=to, device_id_type=MESH)

        mine = pltpu.make_async_copy(ins[0], rows(*me), local_sem.at[0])
        first = [copy(0, me, sibling, src=ins[0])]
        first += [copy(1 + j, me, (*chip, c), src=ins[0]) for j, chip in enumerate(chips)]
        passed = [copy(4 + j, (*chip, c), sibling) for j, chip in enumerate(chips)]
        landed = [copy(1 + j, (*chip, c), me) for j, chip in enumerate(chips)]
        from_sibling = [copy(0, sibling, me)] + [copy(4 + j, (*chip, 1 - c), me) for j, chip in enumerate(chips)]
        return mine, first, passed, landed, from_sibling

    def start(ins, outs, sems):
        mine, first, _, _, _ = copies(ins, outs, sems)
        mine.start()
        for cp in first:
            cp.start()

    def finish(ins, outs, sems):
        mine, first, passed, landed, from_sibling = copies(ins, outs, sems)
        for arrived, onward in zip(landed, passed):
            arrived.wait_recv()
            onward.start()
        for arrived in from_sibling:
            arrived.wait_recv()
        for cp in first + passed:
            cp.wait_send()
        mine.wait()

    shape = jax.ShapeDtypeStruct((N_DEV * m_per, blk.shape[1]), blk.dtype)
    return _Carry([blk], [shape], {}, [sem, sem, pltpu.SemaphoreType.DMA((1,))], start, finish)


def _gather_fresh_carry(own, js):
    n = len(js)
    h = own.shape[0] // 2
    sem = pltpu.SemaphoreType.DMA((2 * n,))

    def copies(ins, outs, sems):
        send_sems, recv_sems = sems
        x, y, c = _mesh_pos()
        chips = _other_chips(x, y)
        out_ici, in_ici, out_d2d, in_d2d = [], [], [], []

        def copy(k, src, dst, to):
            return pltpu.make_async_remote_copy(
                src_ref=src, dst_ref=dst, send_sem=send_sems.at[k], recv_sem=recv_sems.at[k],
                device_id=to, device_id_type=MESH)

        for jj, j in enumerate(js):
            mine = ins[0].at[pl.ds(c * h, h), :]
            land = outs[0].at[jj, pl.ds(c * h, h), :]
            other = outs[0].at[jj, pl.ds((1 - c) * h, h), :]
            out_ici.append(copy(2 * jj, mine, land, (*chips[j], c)))
            in_ici.append(copy(2 * jj, land, land, (x, y, c)))
            out_d2d.append(copy(2 * jj + 1, land, land, (x, y, 1 - c)))
            in_d2d.append(copy(2 * jj + 1, other, other, (x, y, c)))
        return out_ici, in_ici, out_d2d, in_d2d

    def start(ins, outs, sems):
        for cp in copies(ins, outs, sems)[0]:
            cp.start()

    def finish(ins, outs, sems):
        out_ici, in_ici, out_d2d, in_d2d = copies(ins, outs, sems)
        for arrived, onward in zip(in_ici, out_d2d):
            arrived.wait_recv()
            onward.start()
        for arrived in in_d2d:
            arrived.wait_recv()
        for cp in out_ici + out_d2d:
            cp.wait_send()

    return _Carry([own], [jax.ShapeDtypeStruct((n,) + own.shape, own.dtype)], {}, [sem, sem], start, finish)


def _swap_carry(dws):
    n = len(dws)
    sem = pltpu.SemaphoreType.DMA((n,))

    def copies(ins, outs, sems):
        send_sems, recv_sems = sems
        x, y, c = _mesh_pos()
        cps = []
        for w in range(n):
            h = dws[w].shape[1] // 2
            cps.append(pltpu.make_async_remote_copy(
                src_ref=ins[w].at[:, pl.ds((1 - c) * h, h), :], dst_ref=outs[w],
                send_sem=send_sems.at[w], recv_sem=recv_sems.at[w],
                device_id=(x, y, 1 - c), device_id_type=MESH))
        return cps

    def start(ins, outs, sems):
        for cp in copies(ins, outs, sems):
            cp.start()

    def finish(ins, outs, sems):
        for cp in copies(ins, outs, sems):
            cp.wait()

    shapes = [jax.ShapeDtypeStruct((s.shape[0], s.shape[1] // 2, s.shape[2]), s.dtype) for s in dws]
    return _Carry(dws, shapes, {}, [sem, sem], start, finish)


def _merge_carries(carries):
    if len(carries) == 1:
        return carries[0]
    inputs, out_shapes, sem_shapes, aliases, spans = [], [], [], {}, []
    for cy in carries:
        i0, o0, s0 = len(inputs), len(out_shapes), len(sem_shapes)
        aliases.update({i0 + i: o0 + o for i, o in cy.aliases.items()})
        inputs += cy.inputs
        out_shapes += cy.out_shapes
        sem_shapes += cy.sem_shapes
        spans.append((slice(i0, len(inputs)), slice(o0, len(out_shapes)), slice(s0, len(sem_shapes))))

    def start(ins, outs, sems):
        for cy, (si, so, ss) in zip(carries, spans):
            cy.start(ins[si], outs[so], sems[ss])

    def finish(ins, outs, sems):
        for cy, (si, so, ss) in zip(carries, spans):
            cy.finish(ins[si], outs[so], sems[ss])

    return _Carry(inputs, out_shapes, aliases, sem_shapes, start, finish)


ALL_CHIPS = (0, 1, 2)


class _MeshComm:
    GATHER_AT = {
        "mm_in_diag": [("w_conv_out", ALL_CHIPS, 0, 1, 1)],
        "conv_fwd": [("w_glu_a", ALL_CHIPS, 0, 1, 1), ("w_glu_b", ALL_CHIPS, 0, 1, 1), ("w_out", ALL_CHIPS, 0, 1, 1)],
        "s5_fwd": [("w_ff1", ALL_CHIPS, 0, 6, 8)],
        "mm_glu_a": [("w_ff1", ALL_CHIPS, 6, 7, 8)],
        "mm_glu_b": [("w_ff1", ALL_CHIPS, 7, 8, 8)],
        "mm_out": [("w_ff2", ALL_CHIPS, 0, 2, 8)],
        "mm_ff1": [("w_ff2", ALL_CHIPS, 2, 8, 8)],
    }
    SWAP_AT = {
        "mm_d_ff2": ["w_ff2"],
        "mm_d_ff1": ["w_ff1"],
        "conv_bwd": ["w_out", "w_glu_a", "w_glu_b", "w_conv_out"],
    }
    EXCHANGE_AT = {
        "mm_dw_ff1": [("w_ff2", 0, 6, 8)],
        "mm_d_ff1": [("w_ff2", 6, 8, 8)],
        "norm2_bwd": [("w_ff1", 0, 3, 8)],
        "mm_dw_out": [("w_ff1", 3, 5, 8)],
        "mm_d_out": [("w_ff1", 5, 7, 8)],
        "merge_bwd": [("w_ff1", 7, 8, 8)],
        "s5_bwd": [("w_out", 0, 1, 1), ("w_glu_a", 0, 1, 1), ("w_glu_b", 0, 1, 1), ("w_conv_out", 0, 1, 1)],
        "mm_d_in": [("w_in", 0, 1, 1)],
    }
    EARLY_AT = "mm_dw_in"

    def __init__(self, shards, pos, chip, my_c):
        self.pos = pos
        self.chip = chip
        self.my_c = my_c
        self.w_in_own = _cast_bf16(shards["w_in"], name="cast_w_in")
        self.bufs = {n: _cast_into_slot(s, chip, name="cast_" + n) for n, s in shards.items() if n != "w_in"}
        self.raw = {}
        self.parts = {}
        self.landing = {}
        self.halves = {}
        self.pending = {}
        self.last_site = {}
        for site, items in self.EXCHANGE_AT.items():
            for it in items:
                self.last_site[it[0]] = site

    def weight(self, name):
        g = self.bufs[name]
        return g.reshape(g.shape[0] * g.shape[1], g.shape[2]) if name in ROW_SHARDED else g

    def _slot_ids(self):
        x, y, _ = self.pos
        ids = [2 * x + y] + [2 * cx + cy for cx, cy in _other_chips(x, y)]
        return jnp.stack(ids).astype(jnp.int32)

    def mm_in(self, u):
        ids = self._slot_ids()
        own = self.w_in_own
        proj, (near,) = _mm_slots(u, own[None], ids[0:1], None, name="mm_in_own",
                                  carry=_gather_fresh_carry(own, (0, 1)))
        proj, (far,) = _mm_slots(u, near, ids[1:3], proj, name="mm_in_near",
                                 carry=_gather_fresh_carry(own, (2,)))
        proj, carried = _mm_slots(u, far, ids[3:4], proj, name="mm_in_diag", carry=self.carry("mm_in_diag"))
        self.done("mm_in_diag", carried)
        self.w_in_rel = jnp.concatenate([own[None], near, far], axis=0)
        return proj

    def mm_d_in(self, dproj):
        raw = self.raw.pop("w_in")
        landed, = _run_carry(_swap_carry([raw]), name="swap_halves_w_in")
        self.parts["w_in"] = _add_half(raw, landed, self.my_c, name="add_half_w_in")
        carry = self.carry("mm_d_in")
        du, carried = _mm(dproj, self.w_in_rel, mode="nt", out_dtype=F32, name="mm_d_in", carry=carry,
                          a_slots=self._slot_ids())
        self.done("mm_d_in", carried)
        return du

    def early_grads(self, early):
        self.early = early

    def carry(self, site):
        jobs = []
        if site == self.EARLY_AT:
            flat, self.early_offs = _pack(list(self.early.values()))
            jobs.append(("early", None, _gather_rows_carry(flat.reshape(-1, PACK_COLS))))
        if site in self.GATHER_AT:
            items = self.GATHER_AT[site]
            jobs.append(("gather", items, _gather_carry([(self.bufs[it[0]],) + tuple(it[1:]) for it in items])))
        if site in self.EXCHANGE_AT:
            items = self.EXCHANGE_AT[site]
            jobs.append(("exchange", items, _exchange_carry(
                [(self.parts[it[0]], self.landing.get(it[0])) + tuple(it[1:]) for it in items])))
        if site in self.SWAP_AT:
            names = self.SWAP_AT[site]
            jobs.append(("swap", names, _swap_carry([self.raw[n] for n in names])))
        if not jobs:
            return None
        self.pending[site] = jobs
        return _merge_carries([job[2] for job in jobs])

    def done(self, site, carried):
        pos = 0
        for kind, items, carry in self.pending.pop(site):
            outs = carried[pos:pos + len(carry.out_shapes)]
            pos += len(carry.out_shapes)
            if kind == "early":
                self.early_all = outs[0]
            elif kind == "gather":
                self.bufs.update(zip([it[0] for it in items], outs))
            elif kind == "swap":
                for n, landed in zip(items, outs):
                    self.parts[n] = _add_half(self.raw.pop(n), landed, self.my_c, name="add_half_" + n)
            else:
                for it, landed in zip(items, outs):
                    n = it[0]
                    self.landing[n] = landed
                    if self.last_site[n] == site:
                        self.halves[n] = _sum_into_half(self.parts.pop(n), self.landing.pop(n), self.chip,
                                                        self.my_c, name="sum_chips_" + n)

    def grad(self, name, dw):
        if name in ROW_SHARDED:
            dw = dw.reshape(N_CHIPS, dw.shape[0] // N_CHIPS, dw.shape[1])
        self.raw[name] = dw

    def join(self, names, *, name):
        return dict(zip(names, _run_carry(_join_carry([self.halves.pop(n) for n in names]), name=name)))


def _local_step(x, target, mod, small, comm):
    rows, d = x.shape
    cw = d // 2
    shift1, scale1, gate1, shift2, scale2, gate2 = mod
    _, _, bbr, bbi = small["s5_disc"]
    b_in, c_out, b_out, c_in, mults = _s5_operands(*small["s5_loglam"], bbr, bbi, small["c_re"], small["c_im"])
    wt = comm.weight

    def riding(site, fn, *args, **kwargs):
        carry = comm.carry(site)
        if carry is None:
            return fn(*args, **kwargs)
        out, carried = fn(*args, carry=carry, **kwargs)
        comm.done(site, carried)
        return out

    u = _norm_mod(x, small["norm1_g"], scale1, shift1, name="norm1_fwd")
    proj = comm.mm_in(u)
    sl, cv = riding("conv_fwd", _conv_fwd, proj, small["w_dw"], small["b_dw"], small["ln_g"], small["ln_b"], cw=cw)
    y_conv = _mm(sl, wt("w_conv_out"), mode="nn", out_dtype=BF16, name="mm_conv_out")
    yg, st_re, st_im = riding("s5_fwd", _s5_fwd, proj, small["d_skip"], b_in, c_out, mults, col0=2 * cw // LANES)
    ya = riding("mm_glu_a", _mm, yg, wt("w_glu_a"), mode="nn", out_dtype=BF16, name="mm_glu_a")
    yb = riding("mm_glu_b", _mm, yg, wt("w_glu_b"), mode="nn", out_dtype=BF16, name="mm_glu_b")
    merged = _merge_fwd(proj, y_conv, ya, yb, cw=cw)
    mo = riding("mm_out", _mm, merged, wt("w_out"), mode="nn", out_dtype=BF16, name="mm_out")
    h1, z = _res_norm(x, mo, gate1, small["norm2_g"], scale2, shift2)
    f1 = riding("mm_ff1", _mm, z, wt("w_ff1"), mode="nn", out_dtype=BF16, name="mm_ff1")
    ff = _mm(f1, wt("w_ff2"), mode="nn", out_dtype=BF16, name="mm_ff2", a_fn=_relu2_bf16)
    dh2, dff, loss, d_final_g, d_gate2 = _final_fwd_bwd(h1, ff, gate2, small["final_g"], target)

    comm.grad("w_ff2", _mm(f1, dff, mode="tn", out_dtype=BF16, name="mm_dw_ff2", a_fn=_relu2_bf16))
    df1 = riding("mm_d_ff2", _mm, dff, wt("w_ff2"), mode="nt", out_dtype=BF16, name="mm_d_ff2", extra=f1,
                 epi=lambda acc, f: acc * (2.0 * jnp.maximum(f.astype(F32), 0.0)))
    comm.grad("w_ff1", riding("mm_dw_ff1", _mm, z, df1, mode="tn", out_dtype=BF16, name="mm_dw_ff1",
                              out_gathered=True))
    dz = riding("mm_d_ff1", _mm, df1, wt("w_ff1"), mode="nt", out_dtype=F32, name="mm_d_ff1")
    dh1, d_shift2, d_scale2, d_norm2_g, dmo, d_gate1 = riding(
        "norm2_bwd", _norm_mod_bwd, dz, h1, dh2, small["norm2_g"], scale2, gate1, mo, name="norm2_bwd")
    comm.grad("w_out", riding("mm_dw_out", _mm, merged, dmo, mode="tn", out_dtype=BF16, name="mm_dw_out"))
    dmerged = riding("mm_d_out", _mm, dmo, wt("w_out"), mode="nt", out_dtype=BF16, name="mm_d_out")
    dgc, dgs, dy_conv, dya, dyb = riding("merge_bwd", _merge_bwd, dmerged, proj, y_conv, ya, yb, cw=cw)
    comm.grad("w_glu_a", _mm(yg, dya, mode="tn", out_dtype=BF16, name="mm_dw_glu_a", out_gathered=True))
    comm.grad("w_glu_b", _mm(yg, dyb, mode="tn", out_dtype=BF16, name="mm_dw_glu_b", out_gathered=True))
    dyg_a = _mm(dya, wt("w_glu_a"), mode="nt", out_dtype=F32, name="mm_d_glu_a")
    dyg = _mm(dyb, wt("w_glu_b"), mode="nt", out_dtype=F32, name="mm_d_glu_b", extra=dyg_a,
              epi=lambda acc, e: acc + e)
    comm.grad("w_conv_out", _mm(sl, dy_conv, mode="tn", out_dtype=BF16, name="mm_dw_conv_out", out_gathered=True))
    dsl = _mm(dy_conv, wt("w_conv_out"), mode="nt", out_dtype=F32, name="mm_d_conv_out")
    dcv, d_ln_g, d_ln_b = _ln_bwd(dsl, cv, small["ln_g"], small["ln_b"])
    dvconv, d_w_dw, d_b_dw = riding("conv_bwd", _conv_bwd, dcv, proj, small["w_dw"], cw=cw)
    dvssm, d_d_skip, dbr, dbi, dcr, dci, dlr, dli = riding(
        "s5_bwd", _s5_bwd, proj, dyg, small["d_skip"], (st_re, st_im), c_out, b_out, c_in, mults,
        col0=2 * cw // LANES)
    sw = lambda m: jnp.swapaxes(m, 1, 2)
    early = {
        "dmod_tail": jnp.concatenate([d_gate1, d_shift2, d_scale2, d_gate2], axis=1), "loss": loss[:, 0:1],
        "w_dw": d_w_dw, "b_dw": d_b_dw, "ln_g": d_ln_g, "ln_b": d_ln_b,
        "lam_re": dlr.reshape(-1, SSM_STATE), "lam_im": dli.reshape(-1, SSM_STATE),
        "bb_re": sw(_block_diag_extract(dbr, SSM_GROUP, SSM_STATE)),
        "bb_im": sw(_block_diag_extract(dbi, SSM_GROUP, SSM_STATE)),
        "c_re": sw(_block_diag_extract(dcr, SSM_STATE, SSM_GROUP)),
        "c_im": sw(_block_diag_extract(dci, SSM_STATE, SSM_GROUP)),
        "d_skip": d_d_skip, "norm2_g": d_norm2_g, "final_g": d_final_g,
    }
    comm.early_grads(early)
    dproj = jnp.concatenate([dvconv, dvssm, dgc, dgs], axis=1)
    comm.grad("w_in", riding("mm_dw_in", _mm, u, dproj, mode="tn", out_dtype=BF16, name="mm_dw_in",
                             out_gathered=True))
    du = comm.mm_d_in(dproj)
    grad_x, d_shift1, d_scale1, d_norm1_g = riding(
        "norm1_bwd", _norm_mod_bwd, du, x, dh1, small["norm1_g"], scale1, None, None, name="norm1_bwd")
    late ={"dmod_head": jnp.concatenate([d_shift1, d_scale1], axis=1), "norm1_g": d_norm1_g}
    return grad_x, early, late


WEIGHT_NAMES = ["w_ada", "b_ada", "norm1_g", "w_in", "w_dw", "b_dw", "ln_g", "ln_b", "w_conv_out", "a_re", "a_im",
                "log_dt", "b_re", "b_im", "c_re", "c_im", "d_skip", "w_glu_a", "w_glu_b", "w_out", "norm2_g",
                "w_ff1", "w_ff2", "final_g"]
BIG_NAMES = ["w_in", "w_conv_out", "w_glu_a", "w_glu_b", "w_out", "w_ff1", "w_ff2"]
ROW_SHARDED = ("w_out", "w_ff2")
PACK_COLS = 1024
PACK_TILE = SUBLANES * PACK_COLS


def _pack(arrays):
    flats = [a.reshape(-1) for a in arrays]
    offs = []
    total = 0
    for f in flats:
        offs.append(total)
        total += f.shape[0]
    pad = (-total) % PACK_TILE
    if pad:
        flats.append(jnp.zeros((pad,), F32))
    return jnp.concatenate(flats), offs


def _unpack(flat, offs, like):
    return [flat[o:o + a.size].reshape(a.shape) for o, a in zip(offs, like)]


def _gather_w_dw(w_shard):
    k, n = w_shard.shape
    padded = jnp.pad(w_shard, ((0, HALO - k), (0, 0)))
    allw = _gather_small(padded, name="gather_w_dw").reshape(N_CHIPS, 2, HALO, n)[:, 0, :k]
    return jnp.moveaxis(allw, 0, 1).reshape(k, N_CHIPS * n)


def kernel(x, c, w_ada, b_ada, norm1_g, w_in, w_dw, b_dw, ln_g, ln_b, w_conv_out, a_re, a_im, log_dt, b_re, b_im, c_re, c_im, d_skip, w_glu_a, w_glu_b, w_out, norm2_g, w_ff1, w_ff2, final_g, loss_target, m_w_ada, m_b_ada, m_norm1_g, m_w_in, m_w_dw, m_b_dw, m_ln_g, m_ln_b, m_w_conv_out, m_a_re, m_a_im, m_log_dt, m_b_re, m_b_im, m_c_re, m_c_im, m_d_skip, m_w_glu_a, m_w_glu_b, m_w_out, m_norm2_g, m_w_ff1, m_w_ff2, m_final_g, v_w_ada, v_b_ada, v_norm1_g, v_w_in, v_w_dw, v_b_dw, v_ln_g, v_ln_b, v_w_conv_out, v_a_re, v_a_im, v_log_dt, v_b_re, v_b_im, v_c_re, v_c_im, v_d_skip, v_w_glu_a, v_w_glu_b, v_w_out, v_norm2_g, v_w_ff1, v_w_ff2, v_final_g):
    given = dict(locals())
    w = {n: given[n] for n in WEIGHT_NAMES}
    m = {n: given["m_" + n] for n in WEIGHT_NAMES}
    v = {n: given["v_" + n] for n in WEIGHT_NAMES}
    d = x.shape[2]
    xi, yi, ci = _mesh_pos()
    chip = 2 * xi + yi
    dev = 4 * xi + 2 * yi + ci
    my_c = jnp.reshape(ci, (1,)).astype(jnp.int32)

    ndw = w_dw.shape[2]
    assert d // SUBLANES == ndw
    first = jnp.concatenate([c.reshape(SUBLANES, ndw), jnp.pad(w_dw[0], ((0, HALO - CONV_KERNEL), (0, 0)))])
    first_all = _gather_small(first, name="gather_c_w_dw").reshape(N_DEV, SUBLANES + HALO, ndw)
    c_all = first_all[:, :SUBLANES].reshape(N_DEV, d)
    taps = first_all.reshape(N_CHIPS, 2, SUBLANES + HALO, ndw)[:, 0, SUBLANES:SUBLANES + CONV_KERNEL]
    w_dw_full = jnp.moveaxis(taps, 0, 1).reshape(CONV_KERNEL, N_CHIPS * ndw)

    nmod = w_ada.shape[2]
    b_cols = lax.dynamic_slice(b_ada, (0, chip * nmod), (1, nmod))
    mod_part = _ada_fwd(c_all, w_ada[0], b_cols)
    mod_all = _gather_small(mod_part, name="gather_mod").reshape(N_CHIPS, 2, N_DEV, nmod)[:, 0]
    mod_full = jnp.moveaxis(mod_all, 0, 1).reshape(N_DEV, N_CHIPS * nmod)
    mod_row = lax.dynamic_slice(mod_full, (dev, 0), (1, N_CHIPS * nmod))
    mod = [mod_row[:, i * d:(i + 1) * d] for i in range(6)]

    chip_arr = jnp.reshape(chip, (1,)).astype(jnp.int32)
    comm = _MeshComm({n: w[n][0] for n in BIG_NAMES}, (xi, yi, ci), chip_arr, my_c)

    disc_in = (a_re[0], a_im[0], log_dt[0], b_re[0], b_im[0])
    disc, disc_vjp = jax.vjp(_s5_discretise, *disc_in)
    dt = jnp.exp(log_dt[0])[:, None]
    small = {"norm1_g": norm1_g, "w_dw": w_dw_full, "b_dw": b_dw, "ln_g": ln_g, "ln_b": ln_b,
             "c_re": c_re[0], "c_im": c_im[0], "d_skip": d_skip, "norm2_g": norm2_g,
             "final_g": final_g[None, :], "s5_disc": disc, "s5_loglam": (a_re[0] * dt, a_im[0] * dt)}

    grad_x, early, late = _local_step(x[0], loss_target[0], mod, small, comm)
    grads = {}

    early_all = comm.early_all.reshape(N_DEV, -1, PACK_COLS)
    early_sum = _sum_leading(early_all, name="sum_small_grads").reshape(-1)
    summed = dict(zip(early, _unpack(early_sum, comm.early_offs, list(early.values()))))
    flat, late_offs = _pack(list(late.values()))
    late_all = _gather_small(flat.reshape(-1, PACK_COLS), name="gather_late_grads").reshape(N_DEV, -1, PACK_COLS)
    late_sum = _sum_leading(late_all, name="sum_late_grads").reshape(-1)
    summed.update(zip(late, _unpack(late_sum, late_offs, list(late.values()))))
    head = late_all[:, :2 * d // PACK_COLS].reshape(N_DEV, 2 * d)
    tail = early_all[:, :4 * d // PACK_COLS].reshape(N_DEV, 4 * d)
    dmod_all = jnp.concatenate([head, tail], axis=1)

    grads["w_ada"] = _ada_bwd(c_all, lax.dynamic_slice(dmod_all, (0, chip * nmod), (N_DEV, nmod)))
    grads["b_ada"] = _sum_leading(dmod_all.reshape(N_DEV, SUBLANES, 6 * d // SUBLANES),
                                  name="sum_b_ada").reshape(1, 6 * d)
    da_re, da_im, dlog_dt, db_re, db_im = disc_vjp(
        (summed["lam_re"], summed["lam_im"], summed["bb_re"], summed["bb_im"]))
    grads.update({
        "norm1_g": summed["norm1_g"], "w_dw": lax.dynamic_slice(summed["w_dw"], (0, chip * ndw), (CONV_KERNEL, ndw)),
        "b_dw": summed["b_dw"], "ln_g": summed["ln_g"], "ln_b": summed["ln_b"],
        "a_re": da_re, "a_im": da_im, "log_dt": dlog_dt, "b_re": db_re, "b_im": db_im,
        "c_re": summed["c_re"], "c_im": summed["c_im"], "d_skip": summed["d_skip"],
        "norm2_g": summed["norm2_g"], "final_g": summed["final_g"],
    })

    delta, new_m, new_v = {}, {}, {}
    grads.update(comm.join(BIG_NAMES, name="join_halves"))
    for n in ["w_ada"] + BIG_NAMES:
        shp = w[n].shape
        two_d = lambda a: a.reshape(shp[1], shp[2])
        res = _adamw(two_d(w[n]), two_d(grads[n]), two_d(m[n]), two_d(v[n]), name="adamw_" + n)
        delta[n], new_m[n], new_v[n] = [r.reshape(shp) for r in res]
    grads = {n: grads[n].reshape(w[n].shape) for n in WEIGHT_NAMES}
    rest = [n for n in WEIGHT_NAMES if n not in delta]
    packs = []
    for src in (w, grads, m, v):
        flat, offs = _pack([src[n] for n in rest])
        packs.append(flat.reshape(-1, 1024))
    outs = _adamw(*packs, name="adamw_small")
    for dst, o in zip((delta, new_m, new_v), outs):
        for n, a in zip(rest, _unpack(o.reshape(-1), offs, [w[k] for k in rest])):
            dst[n] = a

    return (summed["loss"].reshape(()), grad_x[None], *[grads[n] for n in WEIGHT_NAMES],
            *[delta[n] for n in WEIGHT_NAMES], *[new_m[n] for n in WEIGHT_NAMES],
            *[new_v[n] for n in WEIGHT_NAMES])
```

```python
import functools
import math

import jax
import jax.numpy as jnp
from jax import lax
from jax.experimental import pallas as pl
from jax.experimental.pallas import tpu as pltpu

F32 = jnp.float32
BF16 = jnp.bfloat16
EPS = 1e-6
CONV_KERNEL = 31
SSM_GROUP = 16
SSM_STATE = 64
ADAM_LR = 0.001
ADAM_B1 = 0.9
ADAM_B2 = 0.999
ADAM_EPS = 1e-08
ADAM_WD = 0.01
ADAM_STEP = 10

N_CHIPS = 4
N_DEV = 8
VMEM_LIMIT_BYTES = 56 * 1024 * 1024
LANES = 128
SUBLANES = 8
HALO = 32
GROUPS_PER_BLOCK = LANES // SSM_GROUP
STATE_LANES = GROUPS_PER_BLOCK * SSM_STATE
MESH = pl.DeviceIdType.MESH


def _cparams(sem):
    return pltpu.CompilerParams(dimension_semantics=sem, vmem_limit_bytes=VMEM_LIMIT_BYTES)


def _pick(n, pref, mult=LANES):
    if n <= pref:
        return n
    best = None
    for d in range(mult, pref + 1, mult):
        if n % d == 0:
            best = d
    assert best is not None, (n, pref)
    return best


def _sigmoid(v):
    return 1.0 / (1.0 + jnp.exp(-v))


def _gelu_parts(v):
    k0 = math.sqrt(2.0 / math.pi)
    inner = k0 * (v + 0.044715 * v * v * v)
    t = jnp.tanh(inner)
    return k0, t


def _gelu(v):
    _, t = _gelu_parts(v)
    return 0.5 * v * (1.0 + t)


def _gelu_grad(v):
    k0, t = _gelu_parts(v)
    return 0.5 * (1.0 + t) + 0.5 * v * (1.0 - t * t) * k0 * (1.0 + 3.0 * 0.044715 * v * v)


def _relu2_bf16(a):
    t = jnp.maximum(a.astype(F32), 0.0)
    return (t * t).astype(BF16)


class _Carry:
    def __init__(self, inputs, out_shapes, aliases, sem_shapes, start, finish):
        self.inputs = list(inputs)
        self.out_shapes = list(out_shapes)
        self.aliases = dict(aliases)
        self.sem_shapes = list(sem_shapes)
        self.start = start
        self.finish = finish


def _call(body, *, grid, in_specs, out_specs, out_shape, scratch_shapes, semantics, name, args, carry=None,
          prefetch=(), aliases=None):
    n_in, n_out, n_scr, n_pf = len(in_specs), len(out_specs), len(scratch_shapes), len(prefetch)
    own_aliases = {n_pf + i: o for i, o in (aliases or {}).items()}
    if carry is None:
        gs = pltpu.PrefetchScalarGridSpec(
            num_scalar_prefetch=n_pf, grid=grid, in_specs=in_specs, out_specs=out_specs,
            scratch_shapes=scratch_shapes)
        outs = pl.pallas_call(
            body, grid_spec=gs, out_shape=out_shape, input_output_aliases=own_aliases,
            compiler_params=_cparams(semantics), name=name)(*prefetch, *args)
        return list(outs), []
    ci, co = len(carry.inputs), len(carry.out_shapes)

    def wrapped(*refs):
        pf, refs = refs[:n_pf], refs[n_pf:]
        ins, cins = refs[:n_in], refs[n_in:n_in + ci]
        p = n_in + ci
        outs, couts = refs[p:p + n_out], refs[p + n_out:p + n_out + co]
        p += n_out + co
        scr, csems = refs[p:p + n_scr], refs[p + n_scr:]
        first = pl.program_id(0) == 0
        last = pl.program_id(0) == grid[0] - 1
        for ax in range(1, len(grid)):
            first = jnp.logical_and(first, pl.program_id(ax) == 0)
            last = jnp.logical_and(last, pl.program_id(ax) == grid[ax] - 1)

        @pl.when(first)
        def _():
            carry.start(cins, couts, csems)

        body(*pf, *ins, *outs, *scr)

        @pl.when(last)
        def _():
            carry.finish(cins, couts, csems)

    any_spec = pl.BlockSpec(memory_space=pl.ANY)
    gs = pltpu.PrefetchScalarGridSpec(
        num_scalar_prefetch=n_pf, grid=grid, in_specs=list(in_specs) + [any_spec] * ci,
        out_specs=list(out_specs) + [any_spec] * co, scratch_shapes=list(scratch_shapes) + carry.sem_shapes)
    all_aliases = dict(own_aliases)
    all_aliases.update({n_pf + n_in + i: n_out + o for i, o in carry.aliases.items()})
    outs = pl.pallas_call(
        wrapped, grid_spec=gs, out_shape=list(out_shape) + carry.out_shapes, input_output_aliases=all_aliases,
        compiler_params=_cparams(("arbitrary",) * len(grid)), name=name)(*prefetch, *args, *carry.inputs)
    return list(outs[:n_out]), list(outs[n_out:])


def _run_carry(carry, *, name):
    ci = len(carry.inputs)

    def body(*refs):
        cins, couts, csems = refs[:ci], refs[ci:ci + len(carry.out_shapes)], refs[ci + len(carry.out_shapes):]
        carry.start(cins, couts, csems)
        carry.finish(cins, couts, csems)

    any_spec = pl.BlockSpec(memory_space=pl.ANY)
    outs = pl.pallas_call(
        body, in_specs=[any_spec] * ci, out_specs=[any_spec] * len(carry.out_shapes), out_shape=carry.out_shapes,
        scratch_shapes=carry.sem_shapes, input_output_aliases=carry.aliases, name=name)(*carry.inputs)
    return list(outs)


def _mm(a, b, *, mode, out_dtype, name, out_gathered=False, a_fn=None, epi=None, extra=None,
        bm_pref=1024, bn_pref=1024, bk_pref=2048, carry=None, a_slots=None):
    gathered = (b.ndim == 3)
    if mode == "nn":
        m, kdim = a.shape
        ns = b.shape[-1]
        n = ns * (N_CHIPS if gathered else 1)
        bm, bn, bk = _pick(m, bm_pref), _pick(ns, bn_pref), _pick(kdim, bk_pref)
        npb = ns // bn
        grid = (m // bm, n // bn, kdim // bk)
        a_spec = pl.BlockSpec((bm, bk), lambda i, j, k: (i, k))
        if gathered:
            b_spec = pl.BlockSpec((None, bk, bn), lambda i, j, k: (j // npb, k, j % npb))
        else:
            b_spec = pl.BlockSpec((bk, bn), lambda i, j, k: (k, j))
        o_spec = pl.BlockSpec((bm, bn), lambda i, j, k: (i, j))
        e_spec = pl.BlockSpec((bm, bn), lambda i, j, k: (i, j))
        out_shape = (m, n)
        acc_shape = (bm, bn)
        dims = (((1,), (0,)), ((), ()))
    elif mode == "nt":
        m = a.shape[0]
        kdim, ns = b.shape[-2], b.shape[-1]
        n = ns * (N_CHIPS if gathered else 1)
        assert a.shape[1] == n
        bm, bko, bnr = _pick(m, bm_pref), _pick(kdim, bn_pref), _pick(ns, bk_pref)
        npb = ns // bnr
        grid = (m // bm, kdim // bko, n // bnr)
        a_spec = pl.BlockSpec((bm, bnr), lambda i, j, k: (i, k))
        if gathered:
            b_spec = pl.BlockSpec((None, bko, bnr), lambda i, j, k: (k // npb, j, k % npb))
        else:
            b_spec = pl.BlockSpec((bko, bnr), lambda i, j, k: (j, k))
        o_spec = pl.BlockSpec((bm, bko), lambda i, j, k: (i, j))
        e_spec = pl.BlockSpec((bm, bko), lambda i, j, k: (i, j))
        if a_slots is not None:
            assert gathered and extra is None
            a_spec = pl.BlockSpec((bm, bnr), lambda i, j, k, s_ref: (i, s_ref[k // npb] * npb + k % npb))
            b_spec = pl.BlockSpec((None, bko, bnr), lambda i, j, k, s_ref: (k // npb, j, k % npb))
            o_spec = pl.BlockSpec((bm, bko), lambda i, j, k, s_ref: (i, j))
        out_shape = (m, kdim)
        acc_shape = (bm, bko)
        dims = (((1,), (1,)), ((), ()))
    else:
        m, kdim = a.shape
        n = b.shape[1]
        ns = n // N_CHIPS if out_gathered else n
        bmr, bko, bn = _pick(m, bk_pref), _pick(kdim, bm_pref), _pick(ns, bn_pref)
        npb = ns // bn
        grid = (kdim // bko, n // bn, m // bmr)
        a_spec = pl.BlockSpec((bmr, bko), lambda i, j, k: (k, i))
        b_spec = pl.BlockSpec((bmr, bn), lambda i, j, k: (k, j))
        if out_gathered:
            o_spec = pl.BlockSpec((None, bko, bn), lambda i, j, k: (j // npb, i, j % npb))
            out_shape = (N_CHIPS, kdim, ns)
        else:
            o_spec = pl.BlockSpec((bko, bn), lambda i, j, k: (i, j))
            out_shape = (kdim, n)
        e_spec = None
        acc_shape = (bko, bn)
        dims = (((0,), (0,)), ((), ()))
    nk = grid[2]

    def body(*refs):
        if a_slots is not None:
            refs = refs[1:]
        if extra is not None:
            a_ref, b_ref, e_ref, o_ref, acc = refs
        else:
            a_ref, b_ref, o_ref, acc = refs
            e_ref = None
        k = pl.program_id(2)
        av = a_ref[...]
        if a_fn is not None:
            av = a_fn(av)
        part = lax.dot_general(av, b_ref[...], dims, preferred_element_type=F32)

        def finish(r):
            if epi is not None:
                r = epi(r, e_ref[...])
            o_ref[...] = r.astype(o_ref.dtype)

        if nk == 1:
            finish(part)
            return

        @pl.when(k == 0)
        def _():
            acc[...] = part

        @pl.when(jnp.logical_and(k > 0, k < nk - 1))
        def _():
            acc[...] += part

        @pl.when(k == nk - 1)
        def _():
            finish(acc[...] + part)

    in_specs = [a_spec, b_spec]
    args = [a, b]
    if extra is not None:
        in_specs.append(e_spec)
        args.append(extra)
    outs, carried = _call(body, grid=grid, in_specs=in_specs, out_specs=[o_spec],
                          out_shape=[jax.ShapeDtypeStruct(out_shape, out_dtype)],
                          scratch_shapes=[pltpu.VMEM(acc_shape, F32)],
                          semantics=("parallel", "parallel", "arbitrary"), name=name, args=args, carry=carry,
                          prefetch=() if a_slots is None else (a_slots,))
    return outs[0] if carry is None else (outs[0], carried)


def _mm_slots(a, wbuf, slots, prev, *, name, carry=None):
    m, kdim = a.shape
    ns = wbuf.shape[2]
    bm, bn = _pick(m, 1024), _pick(ns, 1024)
    npb = ns // bn
    grid = (m // bm, slots.shape[0], npb)

    def body(s_ref, a_ref, b_ref, *rest):
        o_ref = rest[-1]
        o_ref[...] = _dot(a_ref[...], b_ref[...]).astype(o_ref.dtype)

    in_specs = [pl.BlockSpec((bm, kdim), lambda i, s, j, s_ref: (i, 0)),
                pl.BlockSpec((None, kdim, bn), lambda i, s, j, s_ref: (s, 0, j))]
    args = [a, wbuf]
    aliases = None
    if prev is not None:
        in_specs.append(pl.BlockSpec(memory_space=pl.ANY))
        args.append(prev)
        aliases = {2: 0}
    outs, carried = _call(
        body, grid=grid, in_specs=in_specs,
        out_specs=[pl.BlockSpec((bm, bn), lambda i, s, j, s_ref: (i, s_ref[s] * npb + j))],
        out_shape=[jax.ShapeDtypeStruct((m, N_CHIPS * ns), BF16)], scratch_shapes=[],
        semantics=("parallel", "arbitrary", "arbitrary"), name=name, args=args, carry=carry,
        prefetch=(slots,), aliases=aliases)
    return outs[0] if carry is None else (outs[0], carried)


def _row_tile(rows, cols, n_arrays):
    budget = VMEM_LIMIT_BYTES // 3
    cap = min(512, budget // (n_arrays * 2 * cols * 4))
    for t in range(cap - cap % SUBLANES, 0, -SUBLANES):
        if rows % t == 0:
            return t
    return rows


def _norm_mod(x, g, scale, shift, *, name):
    rows, d = x.shape
    tr = _row_tile(rows, d, 3)

    def body(x_ref, g_ref, sc_ref, sh_ref, o_ref):
        xv = x_ref[...]
        r = lax.rsqrt(jnp.mean(xv * xv, axis=-1, keepdims=True) + EPS)
        o_ref[...] = ((xv * r * g_ref[...]) * (1.0 + sc_ref[...]) + sh_ref[...]).astype(o_ref.dtype)

    row = pl.BlockSpec((tr, d), lambda i: (i, 0))
    vec = pl.BlockSpec((1, d), lambda i: (0, 0))
    return pl.pallas_call(
        body, grid=(rows // tr,), in_specs=[row, vec, vec, vec], out_specs=row,
        out_shape=jax.ShapeDtypeStruct((rows, d), BF16),
        compiler_params=_cparams(("parallel",)), name=name)(x, g, scale, shift)


CONV_CHUNK = 2 * SUBLANES


def _shifted_copies(buf, n):
    for r in range(1, SUBLANES):
        buf[r, pl.ds(0, n - SUBLANES), :] = buf[0, pl.ds(r, n - SUBLANES), :]


def _conv_fwd(proj, w_dw, b_dw, ln_g, ln_b, *, cw, carry=None):
    rows = proj.shape[0]
    tt = _pick(rows, 256, HALO)
    hb = tt // HALO

    def body(a_ref, g_ref, ha_ref, hg_ref, w_ref, b_ref, lg_ref, lb_ref, sl_ref, cv_ref, vs):
        i = pl.program_id(0)
        hv = ha_ref[...].astype(F32) * _sigmoid(hg_ref[...].astype(F32))
        vs[0, pl.ds(0, HALO), :] = jnp.where(i == 0, 0.0, hv)
        vs[0, pl.ds(HALO, tt), :] = a_ref[...].astype(F32) * _sigmoid(g_ref[...].astype(F32))
        _shifted_copies(vs, HALO + tt)

        def chunk(ci, carry):
            r0 = pl.multiple_of(ci * CONV_CHUNK, CONV_CHUNK)
            acc = jnp.broadcast_to(b_ref[...], (CONV_CHUNK, cw))
            for k in range(CONV_KERNEL):
                q, r = divmod(HALO - (CONV_KERNEL - 1) + k, SUBLANES)
                acc = acc + w_ref[pl.ds(k, 1), :] * vs[r, pl.ds(r0 + q * SUBLANES, CONV_CHUNK), :]
            cv_ref[pl.ds(r0, CONV_CHUNK), :] = acc
            return carry

        lax.fori_loop(0, tt // CONV_CHUNK, chunk, 0)
        acc = cv_ref[...]
        mu = jnp.mean(acc, axis=-1, keepdims=True)
        xc = acc - mu
        rstd = lax.rsqrt(jnp.mean(xc * xc, axis=-1, keepdims=True) + EPS)
        ln = xc * rstd * lg_ref[...] + lb_ref[...]
        sl_ref[...] = (ln * _sigmoid(ln)).astype(sl_ref.dtype)

    tile = lambda c: pl.BlockSpec((tt, cw), lambda i, c=c: (i, c))
    halo = lambda c: pl.BlockSpec((HALO, cw), lambda i, c=c: (jnp.maximum(i * hb - 1, 0), c))
    vec = pl.BlockSpec((1, cw), lambda i: (0, 0))
    outs, carried = _call(
        body, grid=(rows // tt,),
        in_specs=[tile(0), tile(1), halo(0), halo(1),
                  pl.BlockSpec((CONV_KERNEL, cw), lambda i: (0, 0)), vec, vec, vec],
        out_specs=[pl.BlockSpec((tt, cw), lambda i: (i, 0)), pl.BlockSpec((tt, cw), lambda i: (i, 0))],
        out_shape=[jax.ShapeDtypeStruct((rows, cw), BF16), jax.ShapeDtypeStruct((rows, cw), F32)],
        scratch_shapes=[pltpu.VMEM((SUBLANES, HALO + tt, cw), F32)],
        semantics=("parallel",), name="conv_fwd", args=[proj, proj, proj, proj, w_dw, b_dw, ln_g, ln_b],
        carry=carry)
    return outs if carry is None else (outs, carried)


def _ln_bwd(dsl, cv, ln_g, ln_b):
    rows, cw = cv.shape
    tr = _row_tile(rows, cw, 3)

    def body(d_ref, cv_ref, lg_ref, lb_ref, o_ref, dg_ref, db_ref):
        i = pl.program_id(0)

        @pl.when(i == 0)
        def _():
            dg_ref[...] = jnp.zeros_like(dg_ref)
            db_ref[...] = jnp.zeros_like(db_ref)

        x = cv_ref[...]
        mu = jnp.mean(x, axis=-1, keepdims=True)
        xc = x - mu
        rstd = lax.rsqrt(jnp.mean(xc * xc, axis=-1, keepdims=True) + EPS)
        xh = xc * rstd
        ln = xh * lg_ref[...] + lb_ref[...]
        s = _sigmoid(ln)
        dln = d_ref[...].astype(F32) * (s * (1.0 + ln * (1.0 - s)))
        dg_ref[...] += jnp.sum(dln * xh, axis=0, keepdims=True)
        db_ref[...] += jnp.sum(dln, axis=0, keepdims=True)
        dxh = dln * lg_ref[...]
        m1 = jnp.mean(dxh, axis=-1, keepdims=True)
        m2 = jnp.mean(dxh * xh, axis=-1, keepdims=True)
        o_ref[...] = rstd * (dxh - m1 - xh * m2)

    row = pl.BlockSpec((tr, cw), lambda i: (i, 0))
    vec = pl.BlockSpec((1, cw), lambda i: (0, 0))
    return pl.pallas_call(
        body, grid=(rows // tr,), in_specs=[row, row, vec, vec], out_specs=[row, vec, vec],
        out_shape=[jax.ShapeDtypeStruct((rows, cw), F32), jax.ShapeDtypeStruct((1, cw), F32),
                   jax.ShapeDtypeStruct((1, cw), F32)],
        compiler_params=_cparams(("arbitrary",)), name="ln_bwd")(dsl, cv, ln_g, ln_b)


def _conv_bwd(dcv, proj, w_dw, *, cw, carry=None):
    rows = proj.shape[0]
    tt = _pick(rows, 256, HALO)
    hb = tt // HALO
    nt = rows // tt
    taps = CONV_KERNEL

    def body(d_ref, dn_ref, a_ref, g_ref, ha_ref, hg_ref, w_ref, o_ref, dw_ref, db_ref, vs, ds):
        i = pl.program_id(0)

        @pl.when(i == 0)
        def _():
            dw_ref[...] = jnp.zeros_like(dw_ref)
            db_ref[...] = jnp.zeros_like(db_ref)

        hv = ha_ref[...].astype(F32) * _sigmoid(hg_ref[...].astype(F32))
        vs[0, pl.ds(0, HALO), :] = jnp.where(i == 0, 0.0, hv)
        vs[0, pl.ds(HALO, tt), :] = a_ref[...].astype(F32) * _sigmoid(g_ref[...].astype(F32))
        _shifted_copies(vs, HALO + tt)
        ds[0, pl.ds(0, tt), :] = d_ref[...]
        ds[0, pl.ds(tt, HALO), :] = jnp.where(i == nt - 1, 0.0, dn_ref[...])
        _shifted_copies(ds, tt + HALO)
        db_ref[...] += jnp.sum(d_ref[...], axis=0, keepdims=True)
        for k in range(taps):
            q, r = divmod(HALO - (taps - 1) + k, SUBLANES)
            dw_ref[pl.ds(k, 1), :] += jnp.sum(d_ref[...] * vs[r, pl.ds(q * SUBLANES, tt), :], axis=0, keepdims=True)

        def chunk(ci, carry):
            r0 = pl.multiple_of(ci * CONV_CHUNK, CONV_CHUNK)
            dv = jnp.zeros((CONV_CHUNK, cw), F32)
            for k in range(taps):
                q, r = divmod(taps - 1 - k, SUBLANES)
                dv = dv + w_ref[pl.ds(k, 1), :] * ds[r, pl.ds(r0 + q * SUBLANES, CONV_CHUNK), :]
            av = a_ref[pl.ds(r0, CONV_CHUNK), :].astype(F32)
            sg = _sigmoid(g_ref[pl.ds(r0, CONV_CHUNK), :].astype(F32))
            o_ref[pl.ds(r0, CONV_CHUNK), pl.ds(0, cw)] = (dv * sg).astype(o_ref.dtype)
            o_ref[pl.ds(r0, CONV_CHUNK), pl.ds(cw, cw)] = (dv * av * sg * (1.0 - sg)).astype(o_ref.dtype)
            return carry

        lax.fori_loop(0, tt // CONV_CHUNK, chunk, 0)

    tile = lambda c: pl.BlockSpec((tt, cw), lambda i, c=c: (i, c))
    halo = lambda c: pl.BlockSpec((HALO, cw), lambda i, c=c: (jnp.maximum(i * hb - 1, 0), c))
    nxt = pl.BlockSpec((HALO, cw), lambda i: (jnp.minimum((i + 1) * hb, nt * hb - 1), 0))
    outs, carried = _call(
        body, grid=(nt,),
        in_specs=[pl.BlockSpec((tt, cw), lambda i: (i, 0)), nxt, tile(0), tile(1), halo(0), halo(1),
                  pl.BlockSpec((taps, cw), lambda i: (0, 0))],
        out_specs=[pl.BlockSpec((tt, 2 * cw), lambda i: (i, 0)),
                   pl.BlockSpec((taps, cw), lambda i: (0, 0)), pl.BlockSpec((1, cw), lambda i: (0, 0))],
        out_shape=[jax.ShapeDtypeStruct((rows, 2 * cw), BF16), jax.ShapeDtypeStruct((taps, cw), F32),
                   jax.ShapeDtypeStruct((1, cw), F32)],
        scratch_shapes=[pltpu.VMEM((SUBLANES, HALO + tt, cw), F32), pltpu.VMEM((SUBLANES, tt + HALO, cw), F32)],
        semantics=("arbitrary",), name="conv_bwd", args=[dcv, dcv, proj, proj, proj, proj, w_dw], carry=carry)
    return outs if carry is None else (outs, carried)


def _merge_fwd(proj, y_conv, ya, yb, *, cw):
    rows = proj.shape[0]
    tr = _row_tile(rows, cw, 4)

    def body(gc_ref, gs_ref, yc_ref, ya_ref, yb_ref, o_ref):
        ys = ya_ref[...].astype(F32) * _sigmoid(yb_ref[...].astype(F32))
        o_ref[...] = (_sigmoid(gc_ref[...].astype(F32)) * yc_ref[...].astype(F32)
                      + _sigmoid(gs_ref[...].astype(F32)) * ys).astype(o_ref.dtype)

    blk = lambda off: pl.BlockSpec((tr, cw), lambda i, h, off=off: (i, off + h))
    return pl.pallas_call(
        body, grid=(rows // tr, 2), in_specs=[blk(3), blk(5), blk(0), blk(0), blk(0)], out_specs=blk(0),
        out_shape=jax.ShapeDtypeStruct((rows, 2 * cw), BF16),
        compiler_params=_cparams(("parallel", "parallel")), name="merge_fwd")(proj, proj, y_conv, ya, yb)


def _merge_bwd(dmerged, proj, y_conv, ya, yb, *, cw, carry=None):
    rows = proj.shape[0]
    tr = _row_tile(rows, cw, 6)

    def body(d_ref, gc_ref, gs_ref, yc_ref, ya_ref, yb_ref, dgc_ref, dgs_ref, dyc_ref, dya_ref, dyb_ref):
        d = d_ref[...].astype(F32)
        sc = _sigmoid(gc_ref[...].astype(F32))
        ss = _sigmoid(gs_ref[...].astype(F32))
        sb = _sigmoid(yb_ref[...].astype(F32))
        yav = ya_ref[...].astype(F32)
        dgc_ref[...] = (d * yc_ref[...].astype(F32) * sc * (1.0 - sc)).astype(dgc_ref.dtype)
        dgs_ref[...] = (d * (yav * sb) * ss * (1.0 - ss)).astype(dgs_ref.dtype)
        dyc_ref[...] = (d * sc).astype(dyc_ref.dtype)
        dys = d * ss
        dya_ref[...] = (dys * sb).astype(dya_ref.dtype)
        dyb_ref[...] = (dys * yav * sb * (1.0 - sb)).astype(dyb_ref.dtype)

    blk = lambda off: pl.BlockSpec((tr, cw), lambda i, h, off=off: (i, off + h))
    o2 = jax.ShapeDtypeStruct((rows, 2 * cw), BF16)
    outs, carried = _call(
        body, grid=(rows // tr, 2),
        in_specs=[blk(0), blk(3), blk(5), blk(0), blk(0), blk(0)],
        out_specs=[blk(0), blk(0), blk(0), blk(0), blk(0)],
        out_shape=[o2, o2, o2, o2, o2], scratch_shapes=[],
        semantics=("parallel", "parallel"), name="merge_bwd", args=[dmerged, proj, proj, y_conv, ya, yb],
        carry=carry)
    return outs if carry is None else (outs, carried)


def _res_norm(x, mo, gate, g, scale, shift):
    rows, d = x.shape
    tr = _row_tile(rows, d, 4)

    def body(x_ref, mo_ref, gt_ref, g_ref, sc_ref, sh_ref, h_ref, z_ref):
        h = x_ref[...] + gt_ref[...] * mo_ref[...].astype(F32)
        h_ref[...] = h
        r = lax.rsqrt(jnp.mean(h * h, axis=-1, keepdims=True) + EPS)
        z_ref[...] = ((h * r * g_ref[...]) * (1.0 + sc_ref[...]) + sh_ref[...]).astype(z_ref.dtype)

    row = pl.BlockSpec((tr, d), lambda i: (i, 0))
    vec = pl.BlockSpec((1, d), lambda i: (0, 0))
    return pl.pallas_call(
        body, grid=(rows // tr,), in_specs=[row, row, vec, vec, vec, vec], out_specs=[row, row],
        out_shape=[jax.ShapeDtypeStruct((rows, d), F32), jax.ShapeDtypeStruct((rows, d), BF16)],
        compiler_params=_cparams(("parallel",)), name="res_norm")(x, mo, gate, g, scale, shift)


def _final_fwd_bwd(h1, ff, gate2, final_g, target):
    rows, d = h1.shape
    tr = _row_tile(rows, d, 5)

    def body(h_ref, ff_ref, gt_ref, fg_ref, t_ref, dh_ref, dff_ref, loss_ref, dfg_ref, dgt_ref):
        i = pl.program_id(0)

        @pl.when(i == 0)
        def _():
            loss_ref[...] = jnp.zeros_like(loss_ref)
            dfg_ref[...] = jnp.zeros_like(dfg_ref)
            dgt_ref[...] = jnp.zeros_like(dgt_ref)

        ffv = ff_ref[...].astype(F32)
        h2 = h_ref[...] + gt_ref[...] * ffv
        r = lax.rsqrt(jnp.mean(h2 * h2, axis=-1, keepdims=True) + EPS)
        y = h2 * r
        e = y * fg_ref[...] - t_ref[...]
        loss_ref[...] += 0.5 * jnp.sum(jnp.mean(e * e, axis=-1, keepdims=True))
        dout = e * (1.0 / d)
        dfg_ref[...] += jnp.sum(dout * y, axis=0, keepdims=True)
        dy = dout * fg_ref[...]
        dh2 = r * (dy - y * jnp.mean(dy * y, axis=-1, keepdims=True))
        dh_ref[...] = dh2
        dgt_ref[...] += jnp.sum(dh2 * ffv, axis=0, keepdims=True)
        dff_ref[...] = (dh2 * gt_ref[...]).astype(dff_ref.dtype)

    row = pl.BlockSpec((tr, d), lambda i: (i, 0))
    vec = pl.BlockSpec((1, d), lambda i: (0, 0))
    return pl.pallas_call(
        body, grid=(rows // tr,), in_specs=[row, row, vec, vec, row],
        out_specs=[row, row, pl.BlockSpec((1, LANES), lambda i: (0, 0)), vec, vec],
        out_shape=[jax.ShapeDtypeStruct((rows, d), F32), jax.ShapeDtypeStruct((rows, d), BF16),
                   jax.ShapeDtypeStruct((1, LANES), F32), jax.ShapeDtypeStruct((1, d), F32),
                   jax.ShapeDtypeStruct((1, d), F32)],
        compiler_params=_cparams(("arbitrary",)), name="final_fwd_bwd")(h1, ff, gate2, final_g, target)


def _norm_mod_bwd(dz, hin, dres, g, scale, gate, mo, *, name, carry=None):
    rows, d = hin.shape
    with_gate = gate is not None
    tr = _row_tile(rows, d, 6)

    def body(*refs):
        if with_gate:
            (dz_ref, h_ref, dr_ref, g_ref, sc_ref, gt_ref, mo_ref,
             dh_ref, dsh_ref, dsc_ref, dg_ref, dmo_ref, dgt_ref) = refs
        else:
            dz_ref, h_ref, dr_ref, g_ref, sc_ref, dh_ref, dsh_ref, dsc_ref, dg_ref = refs
        i = pl.program_id(0)

        @pl.when(i == 0)
        def _():
            dsh_ref[...] = jnp.zeros_like(dsh_ref)
            dsc_ref[...] = jnp.zeros_like(dsc_ref)
            dg_ref[...] = jnp.zeros_like(dg_ref)
            if with_gate:
                dgt_ref[...] = jnp.zeros_like(dgt_ref)

        dzv = dz_ref[...].astype(F32)
        h = h_ref[...]
        r = lax.rsqrt(jnp.mean(h * h, axis=-1, keepdims=True) + EPS)
        y = h * r
        dsh_ref[...] += jnp.sum(dzv, axis=0, keepdims=True)
        dsc_ref[...] += jnp.sum(dzv * (y * g_ref[...]), axis=0, keepdims=True)
        dn = dzv * (1.0 + sc_ref[...])
        dg_ref[...] += jnp.sum(dn * y, axis=0, keepdims=True)
        dy = dn * g_ref[...]
        dh = dr_ref[...] + r * (dy - y * jnp.mean(dy * y, axis=-1, keepdims=True))
        dh_ref[...] = dh
        if with_gate:
            dmo_ref[...] = (dh * gt_ref[...]).astype(dmo_ref.dtype)
            dgt_ref[...] += jnp.sum(dh * mo_ref[...].astype(F32), axis=0, keepdims=True)

    row = pl.BlockSpec((tr, d), lambda i: (i, 0))
    vec = pl.BlockSpec((1, d), lambda i: (0, 0))
    vshape = jax.ShapeDtypeStruct((1, d), F32)
    in_specs = [row, row, row, vec, vec]
    args = [dz, hin, dres, g, scale]
    out_specs = [row, vec, vec, vec]
    out_shape = [jax.ShapeDtypeStruct((rows, d), F32), vshape, vshape, vshape]
    if with_gate:
        in_specs += [vec, row]
        args += [gate, mo]
        out_specs += [row, vec]
        out_shape += [jax.ShapeDtypeStruct((rows, d), BF16), vshape]
    outs, carried = _call(
        body, grid=(rows // tr,), in_specs=in_specs, out_specs=out_specs, out_shape=out_shape,
        scratch_shapes=[], semantics=("arbitrary",), name=name, args=args, carry=carry)
    return outs if carry is None else (outs, carried)


def _s5_discretise(a_re, a_im, log_dt, b_re, b_im):
    dt = jnp.exp(log_dt)[:, None]
    er = jnp.exp(a_re * dt)
    lr = er * jnp.cos(a_im * dt)
    li = er * jnp.sin(a_im * dt)
    den = a_re * a_re + a_im * a_im
    cr = ((lr - 1.0) * a_re + li * a_im) / den
    ci = (li * a_re - (lr - 1.0) * a_im) / den
    bbr = cr[..., None] * b_re - ci[..., None] * b_im
    bbi = cr[..., None] * b_im + ci[..., None] * b_re
    return lr, li, bbr, bbi


def _block_diag(w):
    g, r, c = w.shape
    nb = g // GROUPS_PER_BLOCK
    eye = jnp.eye(GROUPS_PER_BLOCK, dtype=w.dtype)
    w5 = w.reshape(nb, GROUPS_PER_BLOCK, r, 1, c) * eye[None, :, None, :, None]
    return w5.reshape(nb, GROUPS_PER_BLOCK * r, GROUPS_PER_BLOCK * c)


def _block_diag_extract(m, r, c):
    nb = m.shape[0]
    m5 = m.reshape(nb, GROUPS_PER_BLOCK, r, GROUPS_PER_BLOCK, c)
    idx = jnp.arange(GROUPS_PER_BLOCK)
    d = m5[:, idx, :, idx, :]
    return jnp.moveaxis(d, 0, 1).reshape(nb * GROUPS_PER_BLOCK, r, c)


def _scan_multipliers(lr, li):
    power = jnp.arange(1, SUBLANES + 1, dtype=F32)[None, :, None]
    er = jnp.exp(power * lr)
    pr = er * jnp.cos(power * li)
    pi = er * jnp.sin(power * li)
    rows = jnp.arange(SUBLANES)[None, :, None]
    fr, fi, rr, ri = [], [], [], []
    for s in (1, 2, 4):
        mf = (rows >= s).astype(F32)
        mr = (rows <= SUBLANES - 1 - s).astype(F32)
        fr.append(mf * pr[:, s - 1:s, :])
        fi.append(mf * pi[:, s - 1:s, :])
        rr.append(mr * pr[:, s - 1:s, :])
        ri.append(mr * pi[:, s - 1:s, :])
    fr.append(pr)
    fi.append(pi)
    rr.append(pr[:, ::-1, :])
    ri.append(pi[:, ::-1, :])
    st = lambda xs: jnp.stack(xs, axis=1)
    return st(fr), st(fi), st(rr), st(ri)


def _scan_rows(sre, sim, mul_r, mul_i, n_groups, reverse):
    sgn = -1.0 if reverse else 1.0
    lanes = sre.shape[1]

    def step(k, carry):
        cr, ci = carry
        kk = (n_groups - 1 - k) if reverse else k
        r0 = pl.multiple_of(kk * SUBLANES, SUBLANES)
        xr = sre[pl.ds(r0, SUBLANES), :]
        xi = sim[pl.ds(r0, SUBLANES), :]
        for lvl, s in enumerate((1, 2, 4)):
            sh = (SUBLANES - s) if reverse else s
            nr = pltpu.roll(xr, sh, 0)
            ni = pltpu.roll(xi, sh, 0)
            mr = mul_r[lvl]
            mi = mul_i[lvl] * sgn
            xr, xi = xr + mr * nr - mi * ni, xi + mr * ni + mi * nr
        mr = mul_r[3]
        mi = mul_i[3] * sgn
        xr, xi = xr + mr * cr - mi * ci, xi + mr * ci + mi * cr
        sre[pl.ds(r0, SUBLANES), :] = xr
        sim[pl.ds(r0, SUBLANES), :] = xi
        edge = 0 if reverse else SUBLANES - 1
        ncr = jnp.broadcast_to(xr[edge:edge + 1, :], (SUBLANES, lanes))
        nci = jnp.broadcast_to(xi[edge:edge + 1, :], (SUBLANES, lanes))
        return ncr, nci

    zero = jnp.zeros((SUBLANES, lanes), F32)
    lax.fori_loop(0, n_groups, step, (zero, zero))


def _dot(a, b):
    return jnp.dot(a, b, preferred_element_type=F32)


def _dotf(a, b):
    return _dot(a.astype(BF16), b)


def _s5_operands(lr, li, bbr, bbi, c_re, c_im):
    g = lr.shape[0]
    nb = g // GROUPS_PER_BLOCK
    tb = lambda w: jnp.swapaxes(w, 1, 2)
    b_in = [_block_diag(tb(bbr)), _block_diag(tb(bbi))]
    c_out = [_block_diag(tb(c_re)), _block_diag(tb(c_im))]
    b_out = [_block_diag(bbr), _block_diag(bbi)]
    c_in = [_block_diag(c_re), _block_diag(c_im)]
    lam_r = lr.reshape(nb, 1, STATE_LANES)
    lam_i = li.reshape(nb, 1, STATE_LANES)
    mults = _scan_multipliers(lam_r, lam_i)
    cast = lambda ws: [w.astype(BF16) for w in ws]
    return cast(b_in), cast(c_out), cast(b_out), cast(c_in), mults


def _s5_fwd(proj, d_skip, b_in, c_out, mults, *, col0, carry=None):
    rows = proj.shape[0]
    nb = b_in[0].shape[0]
    tm = _pick(rows, 512, SUBLANES)
    n_tiles = rows // tm
    s_l = STATE_LANES

    def body(u_ref, dk_ref, br, bi, cr, ci, fr_ref, fi_ref, o_ref, sr_ref, si_ref, sre, sim):
        for t in range(n_tiles):
            rs = pl.ds(t * tm, tm)
            ub = u_ref[rs, :]
            sre[rs, :] = _dot(ub, br[...])
            sim[rs, :] = _dot(ub, bi[...])
        _scan_rows(sre, sim, fr_ref, fi_ref, rows // SUBLANES, False)
        for t in range(n_tiles):
            rs = pl.ds(t * tm, tm)
            srb = sre[rs, :].astype(BF16)
            sib = sim[rs, :].astype(BF16)
            sr_ref[rs, :] = srb
            si_ref[rs, :] = sib
            y0 = _dot(srb, cr[...]) - _dot(sib, ci[...])
            y1 = y0 + dk_ref[...] * u_ref[rs, :].astype(F32)
            o_ref[rs, :] = _gelu(y1).astype(o_ref.dtype)

    mat_in = pl.BlockSpec((None, LANES, s_l), lambda g: (g, 0, 0))
    mat_out = pl.BlockSpec((None, s_l, LANES), lambda g: (g, 0, 0))
    mul = pl.BlockSpec((None, 4, SUBLANES, s_l), lambda g: (g, 0, 0, 0))
    state = pl.BlockSpec((rows, s_l), lambda g: (0, g))
    outs, carried = _call(
        body, grid=(nb,),
        in_specs=[pl.BlockSpec((rows, LANES), lambda g: (0, col0 + g)), pl.BlockSpec((1, LANES), lambda g: (0, g))]
        + [mat_in] * 2 + [mat_out] * 2 + [mul] * 2,
        out_specs=[pl.BlockSpec((rows, LANES), lambda g: (0, g)), state, state],
        out_shape=[jax.ShapeDtypeStruct((rows, nb * LANES), BF16), jax.ShapeDtypeStruct((rows, nb * s_l), BF16),
                   jax.ShapeDtypeStruct((rows, nb * s_l), BF16)],
        scratch_shapes=[pltpu.VMEM((rows, s_l), F32), pltpu.VMEM((rows, s_l), F32)],
        semantics=("parallel",), name="s5_fwd", args=[proj, d_skip, *b_in, *c_out, mults[0], mults[1]], carry=carry)
    return outs if carry is None else (outs, carried)


def _s5_bwd(proj, dyg, d_skip, states, c_out, b_out, c_in, mults, *, col0, carry=None):
    rows = proj.shape[0]
    nb = c_out[0].shape[0]
    tm = _pick(rows, 512, SUBLANES)
    n_tiles = rows // tm
    s_l = STATE_LANES
    n_groups = rows // SUBLANES
    tn = (((0,), (0,)), ((), ()))

    def body(u_ref, dy_ref, dk_ref, sr_ref, si_ref, cr, ci, bor, boi, cir, cii, rr_ref, ri_ref,
             du_ref, ddk_ref, dbr_ref, dbi_ref, dcr_ref, dci_ref, dlr_ref, dli_ref,
             gre, gim, dy1):
        ddk = jnp.zeros((1, LANES), F32)
        dcr = jnp.zeros((s_l, LANES), F32)
        dci = jnp.zeros((s_l, LANES), F32)
        for t in range(n_tiles):
            rs = pl.ds(t * tm, tm)
            srb = sr_ref[rs, :]
            sib = si_ref[rs, :]
            uf = u_ref[rs, :].astype(F32)
            y0 = _dot(srb, cr[...]) - _dot(sib, ci[...])
            y1 = y0 + dk_ref[...] * uf
            d1 = dy_ref[rs, :].astype(F32) * _gelu_grad(y1)
            dy1[rs, :] = d1
            ddk = ddk + jnp.sum(d1 * uf, axis=0, keepdims=True)
            d1b = d1.astype(BF16)
            dcr = dcr + lax.dot_general(srb, d1b, tn, preferred_element_type=F32)
            dci = dci - lax.dot_general(sib, d1b, tn, preferred_element_type=F32)
            gre[rs, :] = _dot(d1b, cir[...])
            gim[rs, :] = -_dot(d1b, cii[...])
        ddk_ref[...] = ddk
        dcr_ref[...] = dcr
        dci_ref[...] = dci

        last_row = lax.broadcasted_iota(jnp.int32, (SUBLANES, s_l), 0) == SUBLANES - 1

        def group(r0, s_r, s_i, carry):
            cr_, ci_, ar, ai = carry
            xr = gre[pl.ds(r0, SUBLANES), :]
            xi = gim[pl.ds(r0, SUBLANES), :]
            for lvl, s in enumerate((1, 2, 4)):
                nr = pltpu.roll(xr, SUBLANES - s, 0)
                ni = pltpu.roll(xi, SUBLANES - s, 0)
                mr = rr_ref[lvl]
                mi = ri_ref[lvl]
                xr, xi = xr + mr * nr + mi * ni, xi + mr * ni - mi * nr
            mr = rr_ref[3]
            mi = ri_ref[3]
            xr, xi = xr + mr * cr_ + mi * ci_, xi + mr * ci_ - mi * cr_
            gre[pl.ds(r0, SUBLANES), :] = xr
            gim[pl.ds(r0, SUBLANES), :] = xi
            nxt_r = jnp.where(last_row, cr_, pltpu.roll(xr, SUBLANES - 1, 0))
            nxt_i = jnp.where(last_row, ci_, pltpu.roll(xi, SUBLANES - 1, 0))
            ncr = jnp.broadcast_to(xr[0:1, :], (SUBLANES, s_l))
            nci = jnp.broadcast_to(xi[0:1, :], (SUBLANES, s_l))
            return ncr, nci, ar + nxt_r * s_r + nxt_i * s_i, ai + nxt_i * s_r - nxt_r * s_i

        def rev_step(k, carry):
            r0 = pl.multiple_of((n_groups // 2 - 1 - k) * 2 * SUBLANES, 2 * SUBLANES)
            s_r = sr_ref[pl.ds(r0, 2 * SUBLANES), :].astype(F32)
            s_i = si_ref[pl.ds(r0, 2 * SUBLANES), :].astype(F32)
            carry = group(r0 + SUBLANES, s_r[SUBLANES:], s_i[SUBLANES:], carry)
            return group(r0, s_r[:SUBLANES], s_i[:SUBLANES], carry)

        zero = jnp.zeros((SUBLANES, s_l), F32)
        _, _, ar, ai = lax.fori_loop(0, n_groups // 2, rev_step, (zero, zero, zero, zero))
        dlr_ref[...] = jnp.sum(ar, axis=0, keepdims=True)
        dli_ref[...] = jnp.sum(ai, axis=0, keepdims=True)

        dbr = jnp.zeros((LANES, s_l), F32)
        dbi = jnp.zeros((LANES, s_l), F32)
        for t in range(n_tiles):
            rs = pl.ds(t * tm, tm)
            gr = gre[rs, :]
            gi = gim[rs, :]
            grb = gr.astype(BF16)
            gib = gi.astype(BF16)
            du = _dot(grb, bor[...]) + _dot(gib, boi[...]) + dy1[rs, :] * dk_ref[...]
            du_ref[rs, :] = du.astype(du_ref.dtype)
            ub = u_ref[rs, :]
            dbr = dbr + lax.dot_general(ub, grb, tn, preferred_element_type=F32)
            dbi = dbi + lax.dot_general(ub, gib, tn, preferred_element_type=F32)
        dbr_ref[...] = dbr
        dbi_ref[...] = dbi

    mat_in = pl.BlockSpec((None, LANES, s_l), lambda g: (g, 0, 0))
    mat_out = pl.BlockSpec((None, s_l, LANES), lambda g: (g, 0, 0))
    mul = pl.BlockSpec((None, 4, SUBLANES, s_l), lambda g: (g, 0, 0, 0))
    lam = pl.BlockSpec((None, 1, s_l), lambda g: (g, 0, 0))
    col = pl.BlockSpec((rows, LANES), lambda g: (0, g))
    vec = pl.BlockSpec((1, LANES), lambda g: (0, g))
    state = pl.BlockSpec((rows, s_l), lambda g: (0, g))
    outs, carried = _call(
        body, grid=(nb,),
        in_specs=[pl.BlockSpec((rows, LANES), lambda g: (0, col0 + g)), col, vec]
        + [state] * 2 + [mat_out] * 2 + [mat_out] * 2 + [mat_in] * 2 + [mul] * 2,
        out_specs=[col, vec, mat_in, mat_in, mat_out, mat_out, lam, lam],
        out_shape=[jax.ShapeDtypeStruct((rows, nb * LANES), BF16), jax.ShapeDtypeStruct((1, nb * LANES), F32),
                   jax.ShapeDtypeStruct((nb, LANES, s_l), F32), jax.ShapeDtypeStruct((nb, LANES, s_l), F32),
                   jax.ShapeDtypeStruct((nb, s_l, LANES), F32), jax.ShapeDtypeStruct((nb, s_l, LANES), F32),
                   jax.ShapeDtypeStruct((nb, 1, s_l), F32), jax.ShapeDtypeStruct((nb, 1, s_l), F32)],
        scratch_shapes=[pltpu.VMEM((rows, s_l), F32)] * 2 + [pltpu.VMEM((rows, LANES), F32)],
        semantics=("parallel",), name="s5_bwd",
        args=[proj, dyg, d_skip, *states, *c_out, *b_out, *c_in, mults[2], mults[3]], carry=carry)
    return outs if carry is None else (outs, carried)


def _silu(v):
    return v * _sigmoid(v)


def _ada_fwd(c_all, w_shard, b_cols):
    d, n = w_shard.shape
    bn = _pick(n, 512)

    def body(c_ref, w_ref, b_ref, o_ref):
        ca = _silu(c_ref[...]).astype(BF16)
        o_ref[...] = _dot(ca, w_ref[...].astype(BF16)) + b_ref[...]

    return pl.pallas_call(
        body, grid=(n // bn,),
        in_specs=[pl.BlockSpec((N_DEV, d), lambda j: (0, 0)), pl.BlockSpec((d, bn), lambda j: (0, j)),
                  pl.BlockSpec((1, bn), lambda j: (0, j))],
        out_specs=pl.BlockSpec((N_DEV, bn), lambda j: (0, j)),
        out_shape=jax.ShapeDtypeStruct((N_DEV, n), F32),
        compiler_params=_cparams(("parallel",)), name="ada_fwd")(c_all, w_shard, b_cols)


def _ada_bwd(c_all, dmod_cols):
    d = c_all.shape[1]
    n = dmod_cols.shape[1]
    bn = _pick(n, 512)

    def body(c_ref, g_ref, o_ref):
        ca = _silu(c_ref[...]).astype(BF16)
        o_ref[...] = lax.dot_general(ca, g_ref[...].astype(BF16), (((0,), (0,)), ((), ())),
                                     preferred_element_type=F32)

    return pl.pallas_call(
        body, grid=(n // bn,),
        in_specs=[pl.BlockSpec((N_DEV, d), lambda j: (0, 0)), pl.BlockSpec((N_DEV, bn), lambda j: (0, j))],
        out_specs=pl.BlockSpec((d, bn), lambda j: (0, j)),
        out_shape=jax.ShapeDtypeStruct((d, n), F32),
        compiler_params=_cparams(("parallel",)), name="ada_bwd")(c_all, dmod_cols)


def _cast_bf16(w, *, name):
    rows, cols = w.shape
    tr = _row_tile(rows, cols, 2)

    def body(w_ref, o_ref):
        o_ref[...] = w_ref[...].astype(BF16)

    row = pl.BlockSpec((tr, cols), lambda i: (i, 0))
    return pl.pallas_call(
        body, grid=(rows // tr,), in_specs=[row], out_specs=row,
        out_shape=jax.ShapeDtypeStruct((rows, cols), BF16),
        compiler_params=_cparams(("parallel",)), name=name)(w)


def _adamw(w, g, m, v, *, name, carry=None):
    rows, cols = w.shape
    tr = _row_tile(rows, cols, 7)
    c1 = 1.0 / (1.0 - ADAM_B1 ** ADAM_STEP)
    c2 = 1.0 / (1.0 - ADAM_B2 ** ADAM_STEP)

    def body(w_ref, g_ref, m_ref, v_ref, d_ref, nm_ref, nv_ref):
        gv = g_ref[...]
        nm = ADAM_B1 * m_ref[...] + (1.0 - ADAM_B1) * gv
        nv = ADAM_B2 * v_ref[...] + (1.0 - ADAM_B2) * (gv * gv)
        nm_ref[...] = nm
        nv_ref[...] = nv
        d_ref[...] = -ADAM_LR * ((nm * c1) / (jnp.sqrt(nv * c2) + ADAM_EPS) + ADAM_WD * w_ref[...])

    row = pl.BlockSpec((tr, cols), lambda i: (i, 0))
    shp = jax.ShapeDtypeStruct((rows, cols), F32)
    outs, carried = _call(
        body, grid=(rows // tr,), in_specs=[row] * 4, out_specs=[row] * 3, out_shape=[shp] * 3,
        scratch_shapes=[], semantics=("parallel",), name=name, args=[w, g, m, v], carry=carry)
    return outs if carry is None else (outs, carried)


def _sum_leading(a, *, name, out_dtype=F32):
    n, rows, cols = a.shape
    tr = _row_tile(rows, cols, n + 1)

    def body(a_ref, o_ref):
        acc = a_ref[0].astype(F32)
        for i in range(1, n):
            acc = acc + a_ref[i].astype(F32)
        o_ref[...] = acc.astype(o_ref.dtype)

    return pl.pallas_call(
        body, grid=(rows // tr,), in_specs=[pl.BlockSpec((n, tr, cols), lambda i: (0, i, 0))],
        out_specs=pl.BlockSpec((tr, cols), lambda i: (i, 0)),
        out_shape=jax.ShapeDtypeStruct((rows, cols), out_dtype),
        compiler_params=_cparams(("parallel",)), name=name)(a)


def _add_half(dw, land, my_c, *, name):
    n, r, cols = dw.shape
    h = r // 2
    tr = _row_tile(h, cols, 3)
    hb = h // tr

    def body(c_ref, a_ref, b_ref, o_ref):
        o_ref[...] = (a_ref[...].astype(F32) + b_ref[...].astype(F32)).astype(o_ref.dtype)

    gs = pltpu.PrefetchScalarGridSpec(
        num_scalar_prefetch=1, grid=(n, hb),
        in_specs=[pl.BlockSpec((None, tr, cols), lambda s, i, c_ref: (s, c_ref[0] * hb + i, 0)),
                  pl.BlockSpec((None, tr, cols), lambda s, i, c_ref: (s, i, 0))],
        out_specs=pl.BlockSpec((None, tr, cols), lambda s, i, c_ref: (s, i, 0)))
    return pl.pallas_call(
        body, grid_spec=gs, out_shape=jax.ShapeDtypeStruct((n, h, cols), BF16),
        compiler_params=_cparams(("parallel", "parallel")), name=name)(my_c, dw, land)


def _mesh_pos():
    return lax.axis_index("x"), lax.axis_index("y"), lax.axis_index("c")


def _other_chips(x, y):
    return [(1 - x, y), (x, 1 - y), (1 - x, 1 - y)]


def _gather_small(blk, *, name):
    m_per, n = blk.shape

    def body(x_ref, out_ref, send_sems, recv_sems, local_sem):
        x, y, c = _mesh_pos()
        me, sibling = (x, y, c), (x, y, 1 - c)
        chips = _other_chips(x, y)

        def rows(px, py, pc):
            return out_ref.at[pl.ds((4 * px + 2 * py + pc) * m_per, m_per), :]

        def copy(k, block, to, src=None):
            return pltpu.make_async_remote_copy(
                src_ref=rows(*block) if src is None else src, dst_ref=rows(*block),
                send_sem=send_sems.at[k], recv_sem=recv_sems.at[k], device_id=to, device_id_type=MESH)

        mine = pltpu.make_async_copy(x_ref, rows(*me), local_sem)
        mine.start()
        first = [copy(0, me, sibling, src=x_ref)]
        first += [copy(1 + j, me, (*chip, c), src=x_ref) for j, chip in enumerate(chips)]
        for cp in first:
            cp.start()
        passed = [copy(4 + j, (*chip, c), sibling) for j, chip in enumerate(chips)]
        for j, chip in enumerate(chips):
            copy(1 + j, (*chip, c), me).wait_recv()
            passed[j].start()
        copy(0, sibling, me).wait_recv()
        for j, chip in enumerate(chips):
            copy(4 + j, (*chip, 1 - c), me).wait_recv()
        for cp in first + passed:
            cp.wait_send()
        mine.wait()

    return pl.pallas_call(
        body, out_shape=jax.ShapeDtypeStruct((N_DEV * m_per, n), blk.dtype),
        in_specs=[pl.BlockSpec(memory_space=pltpu.VMEM)], out_specs=pl.BlockSpec(memory_space=pltpu.VMEM),
        scratch_shapes=[pltpu.SemaphoreType.DMA((7,)), pltpu.SemaphoreType.DMA((7,)), pltpu.SemaphoreType.DMA],
        compiler_params=pltpu.CompilerParams(vmem_limit_bytes=VMEM_LIMIT_BYTES), name=name)(blk)


def _hbm_specs(n):
    return [pl.BlockSpec(memory_space=pl.ANY)] * n


def _gather_weights(shards):
    n = len(shards)

    def body(*refs):
        ins, outs = refs[:n], refs[n:2 * n]
        send_sems, recv_sems, local_sems = refs[2 * n:]
        x, y, c = _mesh_pos()
        me_chip = 2 * x + y
        sibling = (x, y, 1 - c)
        chips = _other_chips(x, y)

        def half(w, chip_idx, pc):
            h = shards[w].shape[0] // 2
            return outs[w].at[chip_idx, pl.ds(pc * h, h), :]

        def copy(w, k, chip_idx, pc, to, src=None):
            dst = half(w, chip_idx, pc)
            return pltpu.make_async_remote_copy(
                src_ref=dst if src is None else src, dst_ref=dst,
                send_sem=send_sems.at[6 * w + k], recv_sem=recv_sems.at[6 * w + k],
                device_id=to, device_id_type=MESH)

        local = [pltpu.make_async_copy(ins[w], outs[w].at[me_chip], local_sems.at[w]) for w in range(n)]
        for cp in local:
            cp.start()
        sends = []
        for w in range(n):
            h = shards[w].shape[0] // 2
            for j, chip in enumerate(chips):
                cp = copy(w, j, me_chip, c, (*chip, c), src=ins[w].at[pl.ds(c * h, h), :])
                cp.start()
                sends.append(cp)
        for w in range(n):
            for j, chip in enumerate(chips):
                chip_idx = 2 * chip[0] + chip[1]
                copy(w, j, chip_idx, c, (x, y, c)).wait_recv()
                cp = copy(w, 3 + j, chip_idx, c, sibling)
                cp.start()
                sends.append(cp)
        for w in range(n):
            for j, chip in enumerate(chips):
                copy(w, 3 + j, 2 * chip[0] + chip[1], 1 - c, (x, y, c)).wait_recv()
        for cp in sends:
            cp.wait_send()
        for cp in local:
            cp.wait()

    return pl.pallas_call(
        body, out_shape=[jax.ShapeDtypeStruct((N_CHIPS,) + s.shape, s.dtype) for s in shards],
        in_specs=_hbm_specs(n), out_specs=_hbm_specs(n),
        scratch_shapes=[pltpu.SemaphoreType.DMA((6 * n,)), pltpu.SemaphoreType.DMA((6 * n,)),
                        pltpu.SemaphoreType.DMA((n,))],
        name="gather_weights")(*shards)


def _swap_halves(dws, *, name):
    n = len(dws)

    def body(*refs):
        ins, outs = refs[:n], refs[n:2 * n]
        send_sems, recv_sems = refs[2 * n:]
        x, y, c = _mesh_pos()
        cps = []
        for w in range(n):
            h = dws[w].shape[1] // 2
            cp = pltpu.make_async_remote_copy(
                src_ref=ins[w].at[:, pl.ds((1 - c) * h, h), :], dst_ref=outs[w],
                send_sem=send_sems.at[w], recv_sem=recv_sems.at[w],
                device_id=(x, y, 1 - c), device_id_type=MESH)
            cp.start()
            cps.append(cp)
        for cp in cps:
            cp.wait()

    return pl.pallas_call(
        body, out_shape=[jax.ShapeDtypeStruct((s.shape[0], s.shape[1] // 2, s.shape[2]), s.dtype) for s in dws],
        in_specs=_hbm_specs(n), out_specs=_hbm_specs(n),
        scratch_shapes=[pltpu.SemaphoreType.DMA((n,)), pltpu.SemaphoreType.DMA((n,))],
        name=name)(*dws)


def _chip_exchange(parts):
    n = len(parts)

    def body(*refs):
        ins, outs = refs[:n], refs[n:2 * n]
        send_sems, recv_sems, local_sems = refs[2 * n:]
        x, y, c = _mesh_pos()
        me_chip = 2 * x + y
        chips = _other_chips(x, y)
        local = [pltpu.make_async_copy(ins[w].at[me_chip], outs[w].at[me_chip], local_sems.at[w]) for w in range(n)]
        for cp in local:
            cp.start()
        cps = []
        for w in range(n):
            for j, chip in enumerate(chips):
                cp = pltpu.make_async_remote_copy(
                    src_ref=ins[w].at[2 * chip[0] + chip[1]], dst_ref=outs[w].at[me_chip],
                    send_sem=send_sems.at[3 * w + j], recv_sem=recv_sems.at[3 * w + j],
                    device_id=(*chip, c), device_id_type=MESH)
                cp.start()
                cps.append((cp, w, j, chip))
        for cp, w, j, chip in cps:
            slot = outs[w].at[2 * chip[0] + chip[1]]
            pltpu.make_async_remote_copy(
                src_ref=slot, dst_ref=slot, send_sem=send_sems.at[3 * w + j], recv_sem=recv_sems.at[3 * w + j],
                device_id=(x, y, c), device_id_type=MESH).wait_recv()
        for cp, _, _, _ in cps:
            cp.wait_send()
        for cp in local:
            cp.wait()

    return pl.pallas_call(
        body, out_shape=[jax.ShapeDtypeStruct(s.shape, s.dtype) for s in parts],
        in_specs=_hbm_specs(n), out_specs=_hbm_specs(n),
        scratch_shapes=[pltpu.SemaphoreType.DMA((3 * n,)), pltpu.SemaphoreType.DMA((3 * n,)),
                        pltpu.SemaphoreType.DMA((n,))],
        name="chip_exchange")(*parts)


def _join_halves(halves):
    n = len(halves)

    def body(*refs):
        ins, outs = refs[:n], refs[n:2 * n]
        send_sems, recv_sems, local_sems = refs[2 * n:]
        x, y, c = _mesh_pos()
        cps, local = [], []
        for w in range(n):
            h = halves[w].shape[0]
            mine = outs[w].at[pl.ds(c * h, h), :]
            lc = pltpu.make_async_copy(ins[w], mine, local_sems.at[w])
            lc.start()
            local.append(lc)
            cp = pltpu.make_async_remote_copy(
                src_ref=ins[w], dst_ref=mine, send_sem=send_sems.at[w], recv_sem=recv_sems.at[w],
                device_id=(x, y, 1 - c), device_id_type=MESH)
            cp.start()
            cps.append(cp)
        for w in range(n):
            h = halves[w].shape[0]
            theirs = outs[w].at[pl.ds((1 - c) * h, h), :]
            pltpu.make_async_remote_copy(
                src_ref=theirs, dst_ref=theirs, send_sem=send_sems.at[w], recv_sem=recv_sems.at[w],
                device_id=(x, y, c), device_id_type=MESH).wait_recv()
        for cp in cps:
            cp.wait_send()
        for lc in local:
            lc.wait()

    return pl.pallas_call(
        body, out_shape=[jax.ShapeDtypeStruct((2 * s.shape[0], s.shape[1]), s.dtype) for s in halves],
        in_specs=_hbm_specs(n), out_specs=_hbm_specs(n),
        scratch_shapes=[pltpu.SemaphoreType.DMA((n,)), pltpu.SemaphoreType.DMA((n,)), pltpu.SemaphoreType.DMA((n,))],
        name="join_halves")(*halves)


def _cast_into_slot(w, chip, *, name):
    rows, cols = w.shape
    tr = _row_tile(rows, cols, 2)

    def body(chip_ref, w_ref, o_ref):
        o_ref[...] = w_ref[...].astype(BF16)

    gs = pltpu.PrefetchScalarGridSpec(
        num_scalar_prefetch=1, grid=(rows // tr,),
        in_specs=[pl.BlockSpec((tr, cols), lambda i, chip_ref: (i, 0))],
        out_specs=pl.BlockSpec((None, tr, cols), lambda i, chip_ref: (chip_ref[0], i, 0)))
    return pl.pallas_call(
        body, grid_spec=gs, out_shape=jax.ShapeDtypeStruct((N_CHIPS, rows, cols), BF16),
        compiler_params=_cparams(("parallel",)), name=name)(chip, w)


def _row_range(h, lo, hi, parts):
    step = h // parts
    assert step * parts == h and step % (2 * SUBLANES) == 0, (h, parts)
    return lo * step, (hi - lo) * step


def _gather_carry(items):
    n_copies = sum(len(js) for _, js, _, _, _ in items)
    sem = pltpu.SemaphoreType.DMA((2 * n_copies,))

    def copies(outs, sems):
        send_sems, recv_sems = sems
        x, y, c = _mesh_pos()
        me_chip = 2 * x + y
        chips = _other_chips(x, y)
        out_ici, in_ici, out_d2d, in_d2d = [], [], [], []
        k = 0
        for w, (buf, js, lo, hi, parts) in enumerate(items):
            h = buf.shape[1] // 2
            r0, nr = _row_range(h, lo, hi, parts)

            def copy(k, chip_idx, pc, to):
                ref = outs[w].at[chip_idx, pl.ds(pc * h + r0, nr), :]
                return pltpu.make_async_remote_copy(
                    src_ref=ref, dst_ref=ref, send_sem=send_sems.at[k], recv_sem=recv_sems.at[k],
                    device_id=to, device_id_type=MESH)

            for j in js:
                chip = chips[j]
                chip_idx = 2 * chip[0] + chip[1]
                out_ici.append(copy(k, me_chip, c, (*chip, c)))
                in_ici.append(copy(k, chip_idx, c, (x, y, c)))
                out_d2d.append(copy(k + 1, chip_idx, c, (x, y, 1 - c)))
                in_d2d.append(copy(k + 1, chip_idx, 1 - c, (x, y, c)))
                k += 2
        return out_ici, in_ici, out_d2d, in_d2d

    def start(ins, outs, sems):
        for cp in copies(outs, sems)[0]:
            cp.start()

    def finish(ins, outs, sems):
        out_ici, in_ici, out_d2d, in_d2d = copies(outs, sems)
        for arrived, onward in zip(in_ici, out_d2d):
            arrived.wait_recv()
            onward.start()
        for arrived in in_d2d:
            arrived.wait_recv()
        for cp in out_ici + out_d2d:
            cp.wait_send()

    bufs = [it[0] for it in items]
    shapes = [jax.ShapeDtypeStruct(b.shape, b.dtype) for b in bufs]
    return _Carry(bufs, shapes, {i: i for i in range(len(bufs))}, [sem, sem], start, finish)


def _exchange_carry(items):
    n = len(items)
    sem = pltpu.SemaphoreType.DMA((3 * n,))
    given = [w for w in range(n) if items[w][1] is not None]

    def copies(ins, outs, sems):
        send_sems, recv_sems = sems
        x, y, c = _mesh_pos()
        chips = _other_chips(x, y)
        sends, recvs = [], []
        for w, (part, _, lo, hi, parts) in enumerate(items):
            r0, nr = _row_range(part.shape[1], lo, hi, parts)
            for j, chip in enumerate(chips):
                land = outs[w].at[j, pl.ds(r0, nr), :]
                sends.append(pltpu.make_async_remote_copy(
                    src_ref=ins[w].at[2 * chip[0] + chip[1], pl.ds(r0, nr), :], dst_ref=land,
                    send_sem=send_sems.at[3 * w + j], recv_sem=recv_sems.at[3 * w + j],
                    device_id=(*chip, c), device_id_type=MESH))
                recvs.append(pltpu.make_async_remote_copy(
                    src_ref=land, dst_ref=land,
                    send_sem=send_sems.at[3 * w + j], recv_sem=recv_sems.at[3 * w + j],
                    device_id=(x, y, c), device_id_type=MESH))
        return sends, recvs

    def start(ins, outs, sems):
        for cp in copies(ins, outs, sems)[0]:
            cp.start()

    def finish(ins, outs, sems):
        sends, recvs = copies(ins, outs, sems)
        for cp in recvs:
            cp.wait_recv()
        for cp in sends:
            cp.wait_send()

    inputs = [it[0] for it in items] + [items[w][1] for w in given]
    shapes = [jax.ShapeDtypeStruct((3,) + it[0].shape[1:], it[0].dtype) for it in items]
    aliases = {n + i: w for i, w in enumerate(given)}
    return _Carry(inputs, shapes, aliases, [sem, sem], start, finish)


def _sum_into_half(part, landed, chip, my_c, *, name):
    _, h, cols = part.shape
    tr = _row_tile(h, cols, 5)
    hb = h // tr

    def body(chip_ref, c_ref, p_ref, l_ref, o_ref):
        acc = p_ref[...].astype(F32)
        for j in range(3):
            acc = acc + l_ref[j].astype(F32)
        o_ref[...] = acc

    gs = pltpu.PrefetchScalarGridSpec(
        num_scalar_prefetch=2, grid=(hb,),
        in_specs=[pl.BlockSpec((None, tr, cols), lambda i, chip_ref, c_ref: (chip_ref[0], i, 0)),
                  pl.BlockSpec((3, tr, cols), lambda i, chip_ref, c_ref: (0, i, 0))],
        out_specs=pl.BlockSpec((tr, cols), lambda i, chip_ref, c_ref: (c_ref[0] * hb + i, 0)))
    return pl.pallas_call(
        body, grid_spec=gs, out_shape=jax.ShapeDtypeStruct((2 * h, cols), F32),
        compiler_params=_cparams(("parallel",)), name=name)(chip, my_c, part, landed)


def _join_carry(fulls):
    n = len(fulls)
    sem = pltpu.SemaphoreType.DMA((n,))

    def copies(outs, sems):
        send_sems, recv_sems = sems
        x, y, c = _mesh_pos()
        sends, recvs = [], []
        for w in range(n):
            h = fulls[w].shape[0] // 2
            mine = outs[w].at[pl.ds(c * h, h), :]
            theirs = outs[w].at[pl.ds((1 - c) * h, h), :]
            sends.append(pltpu.make_async_remote_copy(
                src_ref=mine, dst_ref=mine, send_sem=send_sems.at[w], recv_sem=recv_sems.at[w],
                device_id=(x, y, 1 - c), device_id_type=MESH))
            recvs.append(pltpu.make_async_remote_copy(
                src_ref=theirs, dst_ref=theirs, send_sem=send_sems.at[w], recv_sem=recv_sems.at[w],
                device_id=(x, y, c), device_id_type=MESH))
        return sends, recvs

    def start(ins, outs, sems):
        for cp in copies(outs, sems)[0]:
            cp.start()

    def finish(ins, outs, sems):
        sends, recvs = copies(outs, sems)
        for cp in recvs:
            cp.wait_recv()
        for cp in sends:
            cp.wait_send()

    shapes = [jax.ShapeDtypeStruct(f.shape, f.dtype) for f in fulls]
    return _Carry(fulls, shapes, {i: i for i in range(n)}, [sem, sem], start, finish)


class _NoComm:
    def __init__(self, big):
        self.big = big
        self.grads = {}

    def weight(self, name):
        return self.big[name]

    def mm_in(self, u):
        return _mm(u, self.big["w_in"], mode="nn", out_dtype=BF16, name="mm_in")

    def mm_d_in(self, dproj):
        return _mm(dproj, self.big["w_in"], mode="nt", out_dtype=F32, name="mm_d_in")

    def carry(self, site):
        return None

    def done(self, site, carried):
        pass

    def grad(self, name, dw):
        self.grads[name] = dw

    def early_grads(self, early):
        self.early = early


def _gather_rows_carry(blk):
    m_per = blk.shape[0]
    sem = pltpu.SemaphoreType.DMA((7,))

    def copies(ins, outs, sems):
        send_sems, recv_sems, local_sem = sems
        x, y, c = _mesh_pos()
        me, sibling = (x, y, c), (x, y, 1 - c)
        chips = _other_chips(x, y)

        def rows(px, py, pc):
            return outs[0].at[pl.ds((4 * px + 2 * py + pc) * m_per, m_per), :]

        def copy(k, block, to, src=None):
            return pltpu.make_async_remote_copy(
                src_ref=rows(*block) if src is None else src, dst_ref=rows(*block),
                send_sem=send_sems.at[k], recv_sem=recv_sems.at[k], device_id=to, device_id_type=MESH)

        mine = pltpu.make_async_copy(ins[0], rows(*me), local_sem.at[0])
        first = [copy(0, me, sibling, src=ins[0])]
        first += [copy(1 + j, me, (*chip, c), src=ins[0]) for j, chip in enumerate(chips)]
        passed = [copy(4 + j, (*chip, c), sibling) for j, chip in enumerate(chips)]
        landed = [copy(1 + j, (*chip, c), me) for j, chip in enumerate(chips)]
        from_sibling = [copy(0, sibling, me)] + [copy(4 + j, (*chip, 1 - c), me) for j, chip in enumerate(chips)]
        return mine, first, passed, landed, from_sibling

    def start(ins, outs, sems):
        mine, first, _, _, _ = copies(ins, outs, sems)
        mine.start()
        for cp in first:
            cp.start()

    def finish(ins, outs, sems):
        mine, first, passed, landed, from_sibling = copies(ins, outs, sems)
        for arrived, onward in zip(landed, passed):
            arrived.wait_recv()
            onward.start()
        for arrived in from_sibling:
            arrived.wait_recv()
        for cp in first + passed:
            cp.wait_send()
        mine.wait()

    shape = jax.ShapeDtypeStruct((N_DEV * m_per, blk.shape[1]), blk.dtype)
    return _Carry([blk], [shape], {}, [sem, sem, pltpu.SemaphoreType.DMA((1,))], start, finish)


def _gather_fresh_carry(own, js):
    n = len(js)
    h = own.shape[0] // 2
    sem = pltpu.SemaphoreType.DMA((2 * n,))

    def copies(ins, outs, sems):
        send_sems, recv_sems = sems
        x, y, c = _mesh_pos()
        chips = _other_chips(x, y)
        out_ici, in_ici, out_d2d, in_d2d = [], [], [], []

        def copy(k, src, dst, to):
            return pltpu.make_async_remote_copy(
                src_ref=src, dst_ref=dst, send_sem=send_sems.at[k], recv_sem=recv_sems.at[k],
                device_id=to, device_id_type=MESH)

        for jj, j in enumerate(js):
            mine = ins[0].at[pl.ds(c * h, h), :]
            land = outs[0].at[jj, pl.ds(c * h, h), :]
            other = outs[0].at[jj, pl.ds((1 - c) * h, h), :]
            out_ici.append(copy(2 * jj, mine, land, (*chips[j], c)))
            in_ici.append(copy(2 * jj, land, land, (x, y, c)))
            out_d2d.append(copy(2 * jj + 1, land, land, (x, y, 1 - c)))
            in_d2d.append(copy(2 * jj + 1, other, other, (x, y, c)))
        return out_ici, in_ici, out_d2d, in_d2d

    def start(ins, outs, sems):
        for cp in copies(ins, outs, sems)[0]:
            cp.start()

    def finish(ins, outs, sems):
        out_ici, in_ici, out_d2d, in_d2d = copies(ins, outs, sems)
        for arrived, onward in zip(in_ici, out_d2d):
            arrived.wait_recv()
            onward.start()
        for arrived in in_d2d:
            arrived.wait_recv()
        for cp in out_ici + out_d2d:
            cp.wait_send()

    return _Carry([own], [jax.ShapeDtypeStruct((n,) + own.shape, own.dtype)], {}, [sem, sem], start, finish)


def _w_in_copies(own_ref, land_ref, send_sems, recv_sems):
    x, y, c = _mesh_pos()
    h = own_ref.shape[0] // 2
    return [pltpu.make_async_remote_copy(
        src_ref=own_ref.at[pl.ds(c * h, h), :], dst_ref=land_ref.at[j, pl.ds(c * h, h), :],
        send_sem=send_sems[j], recv_sem=recv_sems[j], device_id=(*chip, c), device_id_type=MESH)
        for j, chip in enumerate(_other_chips(x, y))]


def _w_in_send(own):
    hbm = pl.BlockSpec(memory_space=pltpu.HBM)
    sem = pl.BlockSpec(memory_space=pltpu.SEMAPHORE)
    land_shape = (3,) + own.shape

    def body(own_ref, land_ref, s0, s1, s2, r0, r1, r2, own_thru, land_thru, token):
        for cp in _w_in_copies(own_ref, land_ref, (s0, s1, s2), (r0, r1, r2)):
            cp.start()
        token[...] = jnp.zeros_like(token)

    outs = pl.pallas_call(
        body, name="w_in_send",
        out_shape=(pltpu.SemaphoreType.DMA(()),) * 6 + (
            pltpu.HBM(own.shape, own.dtype), pltpu.HBM(land_shape, own.dtype), jax.ShapeDtypeStruct((8, LANES), F32)),
        in_specs=(hbm, hbm), out_specs=(sem,) * 6 + (hbm, hbm, pl.BlockSpec(memory_space=pltpu.VMEM)),
        input_output_aliases={0: 6, 1: 7},
        compiler_params=pltpu.CompilerParams(has_side_effects=pltpu.SideEffectType.DATAFLOW_SIDE_EFFECTING),
    )(pltpu.with_memory_space_constraint(own, pltpu.HBM),
      pltpu.with_memory_space_constraint(lax.empty(land_shape, own.dtype), pltpu.HBM))
    return outs[:6], outs[6], outs[7], outs[8]


def _w_in_wait(sems, own, land, after):
    hbm = pl.BlockSpec(memory_space=pltpu.HBM)
    sem = pl.BlockSpec(memory_space=pltpu.SEMAPHORE)

    def body(own_ref, land_ref, s0, s1, s2, r0, r1, r2, after_ref, own_out, land_out):
        for cp in _w_in_copies(own_ref, land_ref, (s0, s1, s2), (r0, r1, r2)):
            cp.wait_send()
            cp.wait_recv()

    return pl.pallas_call(
        body, name="w_in_wait", out_shape=(pltpu.HBM(own.shape, own.dtype), pltpu.HBM(land.shape, land.dtype)),
        in_specs=(hbm, hbm) + (sem,) * 6 + (pl.BlockSpec(memory_space=pl.ANY),), out_specs=(hbm, hbm),
        input_output_aliases={0: 0, 1: 1},
        compiler_params=pltpu.CompilerParams(has_side_effects=pltpu.SideEffectType.DATAFLOW_SIDE_EFFECTING),
    )(own, land, *sems, after)


def _forward_carry(land):
    n = land.shape[0]
    h = land.shape[1] // 2
    sem = pltpu.SemaphoreType.DMA((n,))

    def copies(outs, sems):
        send_sems, recv_sems = sems
        x, y, c = _mesh_pos()
        sends, recvs = [], []
        for j in range(n):
            mine = outs[0].at[j, pl.ds(c * h, h), :]
            other = outs[0].at[j, pl.ds((1 - c) * h, h), :]
            sends.append(pltpu.make_async_remote_copy(
                src_ref=mine, dst_ref=mine, send_sem=send_sems.at[j], recv_sem=recv_sems.at[j],
                device_id=(x, y, 1 - c), device_id_type=MESH))
            recvs.append(pltpu.make_async_remote_copy(
                src_ref=other, dst_ref=other, send_sem=send_sems.at[j], recv_sem=recv_sems.at[j],
                device_id=(x, y, c), device_id_type=MESH))
        return sends, recvs

    def start(ins, outs, sems):
        for cp in copies(outs, sems)[0]:
            cp.start()

    def finish(ins, outs, sems):
        sends, recvs = copies(outs, sems)
        for cp in recvs:
            cp.wait_recv()
        for cp in sends:
            cp.wait_send()

    return _Carry([land], [jax.ShapeDtypeStruct(land.shape, land.dtype)], {0: 0}, [sem, sem], start, finish)


def _swap_carry(dws):
    n = len(dws)
    sem = pltpu.SemaphoreType.DMA((n,))

    def copies(ins, outs, sems):
        send_sems, recv_sems = sems
        x, y, c = _mesh_pos()
        cps = []
        for w in range(n):
            h = dws[w].shape[1] // 2
            cps.append(pltpu.make_async_remote_copy(
                src_ref=ins[w].at[:, pl.ds((1 - c) * h, h), :], dst_ref=outs[w],
                send_sem=send_sems.at[w], recv_sem=recv_sems.at[w],
                device_id=(x, y, 1 - c), device_id_type=MESH))
        return cps

    def start(ins, outs, sems):
        for cp in copies(ins, outs, sems):
            cp.start()

    def finish(ins, outs, sems):
        for cp in copies(ins, outs, sems):
            cp.wait()

    shapes = [jax.ShapeDtypeStruct((s.shape[0], s.shape[1] // 2, s.shape[2]), s.dtype) for s in dws]
    return _Carry(dws, shapes, {}, [sem, sem], start, finish)


def _merge_carries(carries):
    if len(carries) == 1:
        return carries[0]
    inputs, out_shapes, sem_shapes, aliases, spans = [], [], [], {}, []
    for cy in carries:
        i0, o0, s0 = len(inputs), len(out_shapes), len(sem_shapes)
        aliases.update({i0 + i: o0 + o for i, o in cy.aliases.items()})
        inputs += cy.inputs
        out_shapes += cy.out_shapes
        sem_shapes += cy.sem_shapes
        spans.append((slice(i0, len(inputs)), slice(o0, len(out_shapes)), slice(s0, len(sem_shapes))))

    def start(ins, outs, sems):
        for cy, (si, so, ss) in zip(carries, spans):
            cy.start(ins[si], outs[so], sems[ss])

    def finish(ins, outs, sems):
        for cy, (si, so, ss) in zip(carries, spans):
            cy.finish(ins[si], outs[so], sems[ss])

    return _Carry(inputs, out_shapes, aliases, sem_shapes, start, finish)


ALL_CHIPS = (0, 1, 2)


class _MeshComm:
    GATHER_AT = {
        "mm_in_rest": [("w_conv_out", ALL_CHIPS, 0, 1, 1), ("w_glu_a", ALL_CHIPS, 0, 1, 1),
                       ("w_glu_b", ALL_CHIPS, 0, 1, 1)],
        "conv_fwd": [("w_out", ALL_CHIPS, 0, 1, 1), ("w_ff1", ALL_CHIPS, 0, 1, 8)],
        "s5_fwd": [("w_ff1", ALL_CHIPS, 1, 6, 8)],
        "mm_glu_a": [("w_ff1", ALL_CHIPS, 6, 7, 8)],
        "mm_glu_b": [("w_ff1", ALL_CHIPS, 7, 8, 8)],
        "mm_out": [("w_ff2", ALL_CHIPS, 0, 2, 8)],
        "mm_ff1": [("w_ff2", ALL_CHIPS, 2, 8, 8)],
    }
    SWAP_AT = {
        "mm_d_ff2": ["w_ff2"],
        "mm_d_ff1": ["w_ff1"],
        "conv_bwd": ["w_out", "w_glu_a", "w_glu_b", "w_conv_out"],
    }
    EXCHANGE_AT = {
        "mm_dw_ff1": [("w_ff2", 0, 6, 8)],
        "mm_d_ff1": [("w_ff2", 6, 8, 8)],
        "norm2_bwd": [("w_ff1", 0, 3, 8)],
        "mm_dw_out": [("w_ff1", 3, 5, 8)],
        "mm_d_out": [("w_ff1", 5, 7, 8)],
        "merge_bwd": [("w_ff1", 7, 8, 8)],
        "s5_bwd": [("w_out", 0, 1, 1), ("w_glu_a", 0, 1, 1), ("w_glu_b", 0, 1, 1), ("w_conv_out", 0, 1, 1)],
        "mm_d_in": [("w_in", 0, 1, 1)],
    }
    EARLY_AT = "mm_dw_in"

    def __init__(self, shards, pos, chip, my_c):
        self.pos = pos
        self.chip = chip
        self.my_c = my_c
        *self.w_in_flight, self.token = _w_in_send(_cast_bf16(shards["w_in"], name="cast_w_in"))
        self.bufs = {n: _cast_into_slot(s, chip, name="cast_" + n) for n, s in shards.items() if n != "w_in"}
        self.raw = {}
        self.parts = {}
        self.landing = {}
        self.halves = {}
        self.pending = {}
        self.last_site = {}
        for site, items in self.EXCHANGE_AT.items():
            for it in items:
                self.last_site[it[0]] = site

    def weight(self, name):
        g = self.bufs[name]
        return g.reshape(g.shape[0] * g.shape[1], g.shape[2]) if name in ROW_SHARDED else g

    def _slot_ids(self):
        x, y, _ = self.pos
        ids = [2 * x + y] + [2 * cx + cy for cx, cy in _other_chips(x, y)]
        return jnp.stack(ids).astype(jnp.int32)

    def mm_in(self, u):
        ids = self._slot_ids()
        sems, own, land = self.w_in_flight
        own, land = _w_in_wait(sems, own, land, u)
        proj, (land,) = _mm_slots(u, own[None], ids[0:1], None, name="mm_in_own", carry=_forward_carry(land))
        proj, carried = _mm_slots(u, land, ids[1:4], proj, name="mm_in_rest", carry=self.carry("mm_in_rest"))
        self.done("mm_in_rest", carried)
        self.w_in_rel = jnp.concatenate([own[None], land], axis=0)
        return proj

    def mm_d_in(self, dproj):
        raw = self.raw.pop("w_in")
        landed, = _run_carry(_swap_carry([raw]), name="swap_halves_w_in")
        self.parts["w_in"] = _add_half(raw, landed, self.my_c, name="add_half_w_in")
        carry = self.carry("mm_d_in")
        du, carried = _mm(dproj, self.w_in_rel, mode="nt", out_dtype=F32, name="mm_d_in", carry=carry,
                          a_slots=self._slot_ids())
        self.done("mm_d_in", carried)
        return du

    def early_grads(self, early):
        self.early = early

    def carry(self, site):
        jobs = []
        if site == self.EARLY_AT:
            flat, self.early_offs = _pack(list(self.early.values()))
            jobs.append(("early", None, _gather_rows_carry(flat.reshape(-1, PACK_COLS))))
        if site in self.GATHER_AT:
            items = self.GATHER_AT[site]
            jobs.append(("gather", items, _gather_carry([(self.bufs[it[0]],) + tuple(it[1:]) for it in items])))
        if site in self.EXCHANGE_AT:
            items = self.EXCHANGE_AT[site]
            jobs.append(("exchange", items, _exchange_carry(
                [(self.parts[it[0]], self.landing.get(it[0])) + tuple(it[1:]) for it in items])))
        if site in self.SWAP_AT:
            names = self.SWAP_AT[site]
            jobs.append(("swap", names, _swap_carry([self.raw[n] for n in names])))
        if not jobs:
            return None
        self.pending[site] = jobs
        return _merge_carries([job[2] for job in jobs])

    def done(self, site, carried):
        pos = 0
        for kind, items, carry in self.pending.pop(site):
            outs = carried[pos:pos + len(carry.out_shapes)]
            pos += len(carry.out_shapes)
            if kind == "early":
                self.early_all = outs[0]
            elif kind == "gather":
                self.bufs.update(zip([it[0] for it in items], outs))
            elif kind == "swap":
                for n, landed in zip(items, outs):
                    self.parts[n] = _add_half(self.raw.pop(n), landed, self.my_c, name="add_half_" + n)
            else:
                for it, landed in zip(items, outs):
                    n = it[0]
                    self.landing[n] = landed
                    if self.last_site[n] == site:
                        self.halves[n] = _sum_into_half(self.parts.pop(n), self.landing.pop(n), self.chip,
                                                        self.my_c, name="sum_chips_" + n)

    def grad(self, name, dw):
        if name in ROW_SHARDED:
            dw = dw.reshape(N_CHIPS, dw.shape[0] // N_CHIPS, dw.shape[1])
        self.raw[name] = dw

    def join(self, names, *, name):
        return dict(zip(names, _run_carry(_join_carry([self.halves.pop(n) for n in names]), name=name)))


def _local_step(x, target, mod, small, comm):
    rows, d = x.shape
    cw = d // 2
    shift1, scale1, gate1, shift2, scale2, gate2 = mod
    _, _, bbr, bbi = small["s5_disc"]
    b_in, c_out, b_out, c_in, mults = _s5_operands(*small["s5_loglam"], bbr, bbi, small["c_re"], small["c_im"])
    wt = comm.weight

    def riding(site, fn, *args, **kwargs):
        carry = comm.carry(site)
        if carry is None:
            return fn(*args, **kwargs)
        out, carried = fn(*args, carry=carry, **kwargs)
        comm.done(site, carried)
        return out

    u = _norm_mod(x, small["norm1_g"], scale1, shift1, name="norm1_fwd")
    proj = comm.mm_in(u)
    sl, cv = riding("conv_fwd", _conv_fwd, proj, small["w_dw"], small["b_dw"], small["ln_g"], small["ln_b"], cw=cw)
    y_conv = _mm(sl, wt("w_conv_out"), mode="nn", out_dtype=BF16, name="mm_conv_out")
    yg, st_re, st_im = riding("s5_fwd", _s5_fwd, proj, small["d_skip"], b_in, c_out, mults, col0=2 * cw // LANES)
    ya = riding("mm_glu_a", _mm, yg, wt("w_glu_a"), mode="nn", out_dtype=BF16, name="mm_glu_a")
    yb = riding("mm_glu_b", _mm, yg, wt("w_glu_b"), mode="nn", out_dtype=BF16, name="mm_glu_b")
    merged = _merge_fwd(proj, y_conv, ya, yb, cw=cw)
    mo = riding("mm_out", _mm, merged, wt("w_out"), mode="nn", out_dtype=BF16, name="mm_out")
    h1, z = _res_norm(x, mo, gate1, small["norm2_g"], scale2, shift2)
    f1 = riding("mm_ff1", _mm, z, wt("w_ff1"), mode="nn", out_dtype=BF16, name="mm_ff1")
    ff = _mm(f1, wt("w_ff2"), mode="nn", out_dtype=BF16, name="mm_ff2", a_fn=_relu2_bf16)
    dh2, dff, loss, d_final_g, d_gate2 = _final_fwd_bwd(h1, ff, gate2, small["final_g"], target)

    comm.grad("w_ff2", _mm(f1, dff, mode="tn", out_dtype=BF16, name="mm_dw_ff2", a_fn=_relu2_bf16))
    df1 = riding("mm_d_ff2", _mm, dff, wt("w_ff2"), mode="nt", out_dtype=BF16, name="mm_d_ff2", extra=f1,
                 epi=lambda acc, f: acc * (2.0 * jnp.maximum(f.astype(F32), 0.0)))
    comm.grad("w_ff1", riding("mm_dw_ff1", _mm, z, df1, mode="tn", out_dtype=BF16, name="mm_dw_ff1",
                              out_gathered=True))
    dz = riding("mm_d_ff1", _mm, df1, wt("w_ff1"), mode="nt", out_dtype=F32, name="mm_d_ff1")
    dh1, d_shift2, d_scale2, d_norm2_g, dmo, d_gate1 = riding(
        "norm2_bwd", _norm_mod_bwd, dz, h1, dh2, small["norm2_g"], scale2, gate1, mo, name="norm2_bwd")
    comm.grad("w_out", riding("mm_dw_out", _mm, merged, dmo, mode="tn", out_dtype=BF16, name="mm_dw_out"))
    dmerged = riding("mm_d_out", _mm, dmo, wt("w_out"), mode="nt", out_dtype=BF16, name="mm_d_out")
    dgc, dgs, dy_conv, dya, dyb = riding("merge_bwd", _merge_bwd, dmerged, proj, y_conv, ya, yb, cw=cw)
    comm.grad("w_glu_a", _mm(yg, dya, mode="tn", out_dtype=BF16, name="mm_dw_glu_a", out_gathered=True))
    comm.grad("w_glu_b", _mm(yg, dyb, mode="tn", out_dtype=BF16, name="mm_dw_glu_b", out_gathered=True))
    dyg_a = _mm(dya, wt("w_glu_a"), mode="nt", out_dtype=F32, name="mm_d_glu_a")
    dyg = _mm(dyb, wt("w_glu_b"), mode="nt", out_dtype=F32, name="mm_d_glu_b", extra=dyg_a,
              epi=lambda acc, e: acc + e)
    comm.grad("w_conv_out", _mm(sl, dy_conv, mode="tn", out_dtype=BF16, name="mm_dw_conv_out", out_gathered=True))
    dsl = _mm(dy_conv, wt("w_conv_out"), mode="nt", out_dtype=F32, name="mm_d_conv_out")
    dcv, d_ln_g, d_ln_b = _ln_bwd(dsl, cv, small["ln_g"], small["ln_b"])
    dvconv, d_w_dw, d_b_dw = riding("conv_bwd", _conv_bwd, dcv, proj, small["w_dw"], cw=cw)
    dvssm, d_d_skip, dbr, dbi, dcr, dci, dlr, dli = riding(
        "s5_bwd", _s5_bwd, proj, dyg, small["d_skip"], (st_re, st_im), c_out, b_out, c_in, mults,
        col0=2 * cw // LANES)
    sw = lambda m: jnp.swapaxes(m, 1, 2)
    early = {
        "dmod_tail": jnp.concatenate([d_gate1, d_shift2, d_scale2, d_gate2], axis=1), "loss": loss[:, 0:1],
        "w_dw": d_w_dw, "b_dw": d_b_dw, "ln_g": d_ln_g, "ln_b": d_ln_b,
        "lam_re": dlr.reshape(-1, SSM_STATE), "lam_im": dli.reshape(-1, SSM_STATE),
        "bb_re": sw(_block_diag_extract(dbr, SSM_GROUP, SSM_STATE)),
        "bb_im": sw(_block_diag_extract(dbi, SSM_GROUP, SSM_STATE)),
        "c_re": sw(_block_diag_extract(dcr, SSM_STATE, SSM_GROUP)),
        "c_im": sw(_block_diag_extract(dci, SSM_STATE, SSM_GROUP)),
        "d_skip": d_d_skip, "norm2_g": d_norm2_g, "final_g": d_final_g,
    }
    comm.early_grads(early)
    dproj = jnp.concatenate([dvconv, dvssm, dgc, dgs], axis=1)
    comm.grad("w_in", riding("mm_dw_in", _mm, u, dproj, mode="tn", out_dtype=BF16, name="mm_dw_in",
                             out_gathered=True))
    du = comm.mm_d_in(dproj)
    grad_x, d_shift1, d_scale1, d_norm1_g = riding(
        "norm1_bwd", _norm_mod_bwd, du, x, dh1, small["norm1_g"], scale1, None, None, name="norm1_bwd")
    late ={"dmod_head": jnp.concatenate([d_shift1, d_scale1], axis=1), "norm1_g": d_norm1_g}
    return grad_x, early, late


WEIGHT_NAMES = ["w_ada", "b_ada", "norm1_g", "w_in", "w_dw", "b_dw", "ln_g", "ln_b", "w_conv_out", "a_re", "a_im",
                "log_dt", "b_re", "b_im", "c_re", "c_im", "d_skip", "w_glu_a", "w_glu_b", "w_out", "norm2_g",
                "w_ff1", "w_ff2", "final_g"]
BIG_NAMES = ["w_in", "w_conv_out", "w_glu_a", "w_glu_b", "w_out", "w_ff1", "w_ff2"]
ROW_SHARDED = ("w_out", "w_ff2")
PACK_COLS = 1024
PACK_TILE = SUBLANES * PACK_COLS


def _pack(arrays):
    flats = [a.reshape(-1) for a in arrays]
    offs = []
    total = 0
    for f in flats:
        offs.append(total)
        total += f.shape[0]
    pad = (-total) % PACK_TILE
    if pad:
        flats.append(jnp.zeros((pad,), F32))
    return jnp.concatenate(flats), offs


def _unpack(flat, offs, like):
    return [flat[o:o + a.size].reshape(a.shape) for o, a in zip(offs, like)]


def _gather_w_dw(w_shard):
    k, n = w_shard.shape
    padded = jnp.pad(w_shard, ((0, HALO - k), (0, 0)))
    allw = _gather_small(padded, name="gather_w_dw").reshape(N_CHIPS, 2, HALO, n)[:, 0, :k]
    return jnp.moveaxis(allw, 0, 1).reshape(k, N_CHIPS * n)


def kernel(x, c, w_ada, b_ada, norm1_g, w_in, w_dw, b_dw, ln_g, ln_b, w_conv_out, a_re, a_im, log_dt, b_re, b_im, c_re, c_im, d_skip, w_glu_a, w_glu_b, w_out, norm2_g, w_ff1, w_ff2, final_g, loss_target, m_w_ada, m_b_ada, m_norm1_g, m_w_in, m_w_dw, m_b_dw, m_ln_g, m_ln_b, m_w_conv_out, m_a_re, m_a_im, m_log_dt, m_b_re, m_b_im, m_c_re, m_c_im, m_d_skip, m_w_glu_a, m_w_glu_b, m_w_out, m_norm2_g, m_w_ff1, m_w_ff2, m_final_g, v_w_ada, v_b_ada, v_norm1_g, v_w_in, v_w_dw, v_b_dw, v_ln_g, v_ln_b, v_w_conv_out, v_a_re, v_a_im, v_log_dt, v_b_re, v_b_im, v_c_re, v_c_im, v_d_skip, v_w_glu_a, v_w_glu_b, v_w_out, v_norm2_g, v_w_ff1, v_w_ff2, v_final_g):
    given = dict(locals())
    w = {n: given[n] for n in WEIGHT_NAMES}
    m = {n: given["m_" + n] for n in WEIGHT_NAMES}
    v = {n: given["v_" + n] for n in WEIGHT_NAMES}
    d = x.shape[2]
    xi, yi, ci = _mesh_pos()
    chip = 2 * xi + yi
    dev = 4 * xi + 2 * yi + ci
    my_c = jnp.reshape(ci, (1,)).astype(jnp.int32)
    chip_arr = jnp.reshape(chip, (1,)).astype(jnp.int32)

    comm = _MeshComm({n: w[n][0] for n in BIG_NAMES}, (xi, yi, ci), chip_arr, my_c)
    c = c + comm.token[0:1, 0:1]

    ndw = w_dw.shape[2]
    assert d // SUBLANES == ndw
    first = jnp.concatenate([c.reshape(SUBLANES, ndw), jnp.pad(w_dw[0], ((0, HALO - CONV_KERNEL), (0, 0)))])
    first_all = _gather_small(first, name="gather_c_w_dw").reshape(N_DEV, SUBLANES + HALO, ndw)
    c_all = first_all[:, :SUBLANES].reshape(N_DEV, d)
    taps = first_all.reshape(N_CHIPS, 2, SUBLANES + HALO, ndw)[:, 0, SUBLANES:SUBLANES + CONV_KERNEL]
    w_dw_full = jnp.moveaxis(taps, 0, 1).reshape(CONV_KERNEL, N_CHIPS * ndw)

    nmod = w_ada.shape[2]
    b_cols = lax.dynamic_slice(b_ada, (0, chip * nmod), (1, nmod))
    mod_part = _ada_fwd(c_all, w_ada[0], b_cols)
    mod_all = _gather_small(mod_part, name="gather_mod").reshape(N_CHIPS, 2, N_DEV, nmod)[:, 0]
    mod_full = jnp.moveaxis(mod_all, 0, 1).reshape(N_DEV, N_CHIPS * nmod)
    mod_row = lax.dynamic_slice(mod_full, (dev, 0), (1, N_CHIPS * nmod))
    mod = [mod_row[:, i * d:(i + 1) * d] for i in range(6)]


    disc_in = (a_re[0], a_im[0], log_dt[0], b_re[0], b_im[0])
    disc, disc_vjp = jax.vjp(_s5_discretise, *disc_in)
    dt = jnp.exp(log_dt[0])[:, None]
    small = {"norm1_g": norm1_g, "w_dw": w_dw_full, "b_dw": b_dw, "ln_g": ln_g, "ln_b": ln_b,
             "c_re": c_re[0], "c_im": c_im[0], "d_skip": d_skip, "norm2_g": norm2_g,
             "final_g": final_g[None, :], "s5_disc": disc, "s5_loglam": (a_re[0] * dt, a_im[0] * dt)}

    grad_x, early, late = _local_step(x[0], loss_target[0], mod, small, comm)
    grads = {}

    early_all = comm.early_all.reshape(N_DEV, -1, PACK_COLS)
    early_sum = _sum_leading(early_all, name="sum_small_grads").reshape(-1)
    summed = dict(zip(early, _unpack(early_sum, comm.early_offs, list(early.values()))))
    flat, late_offs = _pack(list(late.values()))
    late_all = _gather_small(flat.reshape(-1, PACK_COLS), name="gather_late_grads").reshape(N_DEV, -1, PACK_COLS)
    late_sum = _sum_leading(late_all, name="sum_late_grads").reshape(-1)
    summed.update(zip(late, _unpack(late_sum, late_offs, list(late.values()))))
    head = late_all[:, :2 * d // PACK_COLS].reshape(N_DEV, 2 * d)
    tail = early_all[:, :4 * d // PACK_COLS].reshape(N_DEV, 4 * d)
    dmod_all = jnp.concatenate([head, tail], axis=1)

    grads["w_ada"] = _ada_bwd(c_all, lax.dynamic_slice(dmod_all, (0, chip * nmod), (N_DEV, nmod)))
    grads["b_ada"] = _sum_leading(dmod_all.reshape(N_DEV, SUBLANES, 6 * d // SUBLANES),
                                  name="sum_b_ada").reshape(1, 6 * d)
    da_re, da_im, dlog_dt, db_re, db_im = disc_vjp(
        (summed["lam_re"], summed["lam_im"], summed["bb_re"], summed["bb_im"]))
    grads.update({
        "norm1_g": summed["norm1_g"], "w_dw": lax.dynamic_slice(summed["w_dw"], (0, chip * ndw), (CONV_KERNEL, ndw)),
        "b_dw": summed["b_dw"], "ln_g": summed["ln_g"], "ln_b": summed["ln_b"],
        "a_re": da_re, "a_im": da_im, "log_dt": dlog_dt, "b_re": db_re, "b_im": db_im,
        "c_re": summed["c_re"], "c_im": summed["c_im"], "d_skip": summed["d_skip"],
        "norm2_g": summed["norm2_g"], "final_g": summed["final_g"],
    })

    delta, new_m, new_v = {}, {}, {}
    grads.update(comm.join(BIG_NAMES, name="join_halves"))
    for n in ["w_ada"] + BIG_NAMES:
        shp = w[n].shape
        two_d = lambda a: a.reshape(shp[1], shp[2])
        res = _adamw(two_d(w[n]), two_d(grads[n]), two_d(m[n]), two_d(v[n]), name="adamw_" + n)
        delta[n], new_m[n], new_v[n] = [r.reshape(shp) for r in res]
    grads = {n: grads[n].reshape(w[n].shape) for n in WEIGHT_NAMES}
    rest = [n for n in WEIGHT_NAMES if n not in delta]
    packs = []
    for src in (w, grads, m, v):
        flat, offs = _pack([src[n] for n in rest])
        packs.append(flat.reshape(-1, 1024))
    outs = _adamw(*packs, name="adamw_small")
    for dst, o in zip((delta, new_m, new_v), outs):
        for n, a in zip(rest, _unpack(o.reshape(-1), offs, [w[k] for k in rest])):
            dst[n] = a

    return (summed["loss"].reshape(()), grad_x[None], *[grads[n] for n in WEIGHT_NAMES],
            *[delta[n] for n in WEIGHT_NAMES], *[new_m[n] for n in WEIGHT_NAMES],
            *[new_v[n] for n in WEIGHT_NAMES])
```

```python
import functools
import math

import jax
import jax.numpy as jnp
from jax import lax
from jax.experimental import pallas as pl
from jax.experimental.pallas import tpu as pltpu

F32 = jnp.float32
BF16 = jnp.bfloat16
EPS = 1e-6
CONV_KERNEL = 31
SSM_GROUP = 16
SSM_STATE = 64
ADAM_LR = 0.001
ADAM_B1 = 0.9
ADAM_B2 = 0.999
ADAM_EPS = 1e-08
ADAM_WD = 0.01
ADAM_STEP = 10

N_CHIPS = 4
N_DEV = 8
VMEM_LIMIT_BYTES = 56 * 1024 * 1024
LANES = 128
SUBLANES = 8
HALO = 32
GROUPS_PER_BLOCK = LANES // SSM_GROUP
STATE_LANES = GROUPS_PER_BLOCK * SSM_STATE
MESH = pl.DeviceIdType.MESH


def _cparams(sem):
    return pltpu.CompilerParams(dimension_semantics=sem, vmem_limit_bytes=VMEM_LIMIT_BYTES)


def _pick(n, pref, mult=LANES):
    if n <= pref:
        return n
    best = None
    for d in range(mult, pref + 1, mult):
        if n % d == 0:
            best = d
    assert best is not None, (n, pref)
    return best


def _sigmoid(v):
    return 1.0 / (1.0 + jnp.exp(-v))


def _gelu_parts(v):
    k0 = math.sqrt(2.0 / math.pi)
    inner = k0 * (v + 0.044715 * v * v * v)
    t = jnp.tanh(inner)
    return k0, t


def _gelu(v):
    _, t = _gelu_parts(v)
    return 0.5 * v * (1.0 + t)


def _gelu_grad(v):
    k0, t = _gelu_parts(v)
    return 0.5 * (1.0 + t) + 0.5 * v * (1.0 - t * t) * k0 * (1.0 + 3.0 * 0.044715 * v * v)


def _relu2_bf16(a):
    t = jnp.maximum(a.astype(F32), 0.0)
    return (t * t).astype(BF16)


class _Carry:
    def __init__(self, inputs, out_shapes, aliases, sem_shapes, start, finish):
        self.inputs = list(inputs)
        self.out_shapes = list(out_shapes)
        self.aliases = dict(aliases)
        self.sem_shapes = list(sem_shapes)
        self.start = start
        self.finish = finish


def _call(body, *, grid, in_specs, out_specs, out_shape, scratch_shapes, semantics, name, args, carry=None,
          prefetch=(), aliases=None):
    n_in, n_out, n_scr, n_pf = len(in_specs), len(out_specs), len(scratch_shapes), len(prefetch)
    own_aliases = {n_pf + i: o for i, o in (aliases or {}).items()}
    if carry is None:
        gs = pltpu.PrefetchScalarGridSpec(
            num_scalar_prefetch=n_pf, grid=grid, in_specs=in_specs, out_specs=out_specs,
            scratch_shapes=scratch_shapes)
        outs = pl.pallas_call(
            body, grid_spec=gs, out_shape=out_shape, input_output_aliases=own_aliases,
            compiler_params=_cparams(semantics), name=name)(*prefetch, *args)
        return list(outs), []
    ci, co = len(carry.inputs), len(carry.out_shapes)

    def wrapped(*refs):
        pf, refs = refs[:n_pf], refs[n_pf:]
        ins, cins = refs[:n_in], refs[n_in:n_in + ci]
        p = n_in + ci
        outs, couts = refs[p:p + n_out], refs[p + n_out:p + n_out + co]
        p += n_out + co
        scr, csems = refs[p:p + n_scr], refs[p + n_scr:]
        first = pl.program_id(0) == 0
        last = pl.program_id(0) == grid[0] - 1
        for ax in range(1, len(grid)):
            first = jnp.logical_and(first, pl.program_id(ax) == 0)
            last = jnp.logical_and(last, pl.program_id(ax) == grid[ax] - 1)

        @pl.when(first)
        def _():
            carry.start(cins, couts, csems)

        body(*pf, *ins, *outs, *scr)

        @pl.when(last)
        def _():
            carry.finish(cins, couts, csems)

    any_spec = pl.BlockSpec(memory_space=pl.ANY)
    gs = pltpu.PrefetchScalarGridSpec(
        num_scalar_prefetch=n_pf, grid=grid, in_specs=list(in_specs) + [any_spec] * ci,
        out_specs=list(out_specs) + [any_spec] * co, scratch_shapes=list(scratch_shapes) + carry.sem_shapes)
    all_aliases = dict(own_aliases)
    all_aliases.update({n_pf + n_in + i: n_out + o for i, o in carry.aliases.items()})
    outs = pl.pallas_call(
        wrapped, grid_spec=gs, out_shape=list(out_shape) + carry.out_shapes, input_output_aliases=all_aliases,
        compiler_params=_cparams(("arbitrary",) * len(grid)), name=name)(*prefetch, *args, *carry.inputs)
    return list(outs[:n_out]), list(outs[n_out:])


def _run_carry(carry, *, name):
    ci = len(carry.inputs)

    def body(*refs):
        cins, couts, csems = refs[:ci], refs[ci:ci + len(carry.out_shapes)], refs[ci + len(carry.out_shapes):]
        carry.start(cins, couts, csems)
        carry.finish(cins, couts, csems)

    any_spec = pl.BlockSpec(memory_space=pl.ANY)
    outs = pl.pallas_call(
        body, in_specs=[any_spec] * ci, out_specs=[any_spec] * len(carry.out_shapes), out_shape=carry.out_shapes,
        scratch_shapes=carry.sem_shapes, input_output_aliases=carry.aliases, name=name)(*carry.inputs)
    return list(outs)


def _mm(a, b, *, mode, out_dtype, name, out_gathered=False, a_fn=None, epi=None, extra=None,
        bm_pref=1024, bn_pref=1024, bk_pref=2048, carry=None, a_slots=None):
    gathered = (b.ndim == 3)
    if mode == "nn":
        m, kdim = a.shape
        ns = b.shape[-1]
        n = ns * (N_CHIPS if gathered else 1)
        bm, bn, bk = _pick(m, bm_pref), _pick(ns, bn_pref), _pick(kdim, bk_pref)
        npb = ns // bn
        grid = (m // bm, n // bn, kdim // bk)
        a_spec = pl.BlockSpec((bm, bk), lambda i, j, k: (i, k))
        if gathered:
            b_spec = pl.BlockSpec((None, bk, bn), lambda i, j, k: (j // npb, k, j % npb))
        else:
            b_spec = pl.BlockSpec((bk, bn), lambda i, j, k: (k, j))
        o_spec = pl.BlockSpec((bm, bn), lambda i, j, k: (i, j))
        e_spec = pl.BlockSpec((bm, bn), lambda i, j, k: (i, j))
        out_shape = (m, n)
        acc_shape = (bm, bn)
        dims = (((1,), (0,)), ((), ()))
    elif mode == "nt":
        m = a.shape[0]
        kdim, ns = b.shape[-2], b.shape[-1]
        n = ns * (N_CHIPS if gathered else 1)
        assert a.shape[1] == n
        bm, bko, bnr = _pick(m, bm_pref), _pick(kdim, bn_pref), _pick(ns, bk_pref)
        npb = ns // bnr
        grid = (m // bm, kdim // bko, n // bnr)
        a_spec = pl.BlockSpec((bm, bnr), lambda i, j, k: (i, k))
        if gathered:
            b_spec = pl.BlockSpec((None, bko, bnr), lambda i, j, k: (k // npb, j, k % npb))
        else:
            b_spec = pl.BlockSpec((bko, bnr), lambda i, j, k: (j, k))
        o_spec = pl.BlockSpec((bm, bko), lambda i, j, k: (i, j))
        e_spec = pl.BlockSpec((bm, bko), lambda i, j, k: (i, j))
        if a_slots is not None:
            assert gathered and extra is None
            a_spec = pl.BlockSpec((bm, bnr), lambda i, j, k, s_ref: (i, s_ref[k // npb] * npb + k % npb))
            b_spec = pl.BlockSpec((None, bko, bnr), lambda i, j, k, s_ref: (k // npb, j, k % npb))
            o_spec = pl.BlockSpec((bm, bko), lambda i, j, k, s_ref: (i, j))
        out_shape = (m, kdim)
        acc_shape = (bm, bko)
        dims = (((1,), (1,)), ((), ()))
    else:
        m, kdim = a.shape
        n = b.shape[1]
        ns = n // N_CHIPS if out_gathered else n
        bmr, bko, bn = _pick(m, bk_pref), _pick(kdim, bm_pref), _pick(ns, bn_pref)
        npb = ns // bn
        grid = (kdim // bko, n // bn, m // bmr)
        a_spec = pl.BlockSpec((bmr, bko), lambda i, j, k: (k, i))
        b_spec = pl.BlockSpec((bmr, bn), lambda i, j, k: (k, j))
        if out_gathered:
            o_spec = pl.BlockSpec((None, bko, bn), lambda i, j, k: (j // npb, i, j % npb))
            out_shape = (N_CHIPS, kdim, ns)
        else:
            o_spec = pl.BlockSpec((bko, bn), lambda i, j, k: (i, j))
            out_shape = (kdim, n)
        e_spec = None
        acc_shape = (bko, bn)
        dims = (((0,), (0,)), ((), ()))
    nk = grid[2]

    def body(*refs):
        if a_slots is not None:
            refs = refs[1:]
        if extra is not None:
            a_ref, b_ref, e_ref, o_ref, acc = refs
        else:
            a_ref, b_ref, o_ref, acc = refs
            e_ref = None
        k = pl.program_id(2)
        av = a_ref[...]
        if a_fn is not None:
            av = a_fn(av)
        part = lax.dot_general(av, b_ref[...], dims, preferred_element_type=F32)

        def finish(r):
            if epi is not None:
                r = epi(r, e_ref[...])
            o_ref[...] = r.astype(o_ref.dtype)

        if nk == 1:
            finish(part)
            return

        @pl.when(k == 0)
        def _():
            acc[...] = part

        @pl.when(jnp.logical_and(k > 0, k < nk - 1))
        def _():
            acc[...] += part

        @pl.when(k == nk - 1)
        def _():
            finish(acc[...] + part)

    in_specs = [a_spec, b_spec]
    args = [a, b]
    if extra is not None:
        in_specs.append(e_spec)
        args.append(extra)
    outs, carried = _call(body, grid=grid, in_specs=in_specs, out_specs=[o_spec],
                          out_shape=[jax.ShapeDtypeStruct(out_shape, out_dtype)],
                          scratch_shapes=[pltpu.VMEM(acc_shape, F32)],
                          semantics=("parallel", "parallel", "arbitrary"), name=name, args=args, carry=carry,
                          prefetch=() if a_slots is None else (a_slots,))
    return outs[0] if carry is None else (outs[0], carried)


def _mm_slots(a, wbuf, slots, prev, *, name, carry=None):
    m, kdim = a.shape
    ns = wbuf.shape[2]
    bm, bn = _pick(m, 1024), _pick(ns, 1024)
    npb = ns // bn
    grid = (m // bm, slots.shape[0], npb)

    def body(s_ref, a_ref, b_ref, *rest):
        o_ref = rest[-1]
        o_ref[...] = _dot(a_ref[...], b_ref[...]).astype(o_ref.dtype)

    in_specs = [pl.BlockSpec((bm, kdim), lambda i, s, j, s_ref: (i, 0)),
                pl.BlockSpec((None, kdim, bn), lambda i, s, j, s_ref: (s, 0, j))]
    args = [a, wbuf]
    aliases = None
    if prev is not None:
        in_specs.append(pl.BlockSpec(memory_space=pl.ANY))
        args.append(prev)
        aliases = {2: 0}
    outs, carried = _call(
        body, grid=grid, in_specs=in_specs,
        out_specs=[pl.BlockSpec((bm, bn), lambda i, s, j, s_ref: (i, s_ref[s] * npb + j))],
        out_shape=[jax.ShapeDtypeStruct((m, N_CHIPS * ns), BF16)], scratch_shapes=[],
        semantics=("parallel", "arbitrary", "arbitrary"), name=name, args=args, carry=carry,
        prefetch=(slots,), aliases=aliases)
    return outs[0] if carry is None else (outs[0], carried)


def _row_tile(rows, cols, n_arrays):
    budget = VMEM_LIMIT_BYTES // 3
    cap = min(512, budget // (n_arrays * 2 * cols * 4))
    for t in range(cap - cap % SUBLANES, 0, -SUBLANES):
        if rows % t == 0:
            return t
    return rows


def _norm_mod(x, g, scale, shift, *, name):
    rows, d = x.shape
    tr = _row_tile(rows, d, 3)

    def body(x_ref, g_ref, sc_ref, sh_ref, o_ref):
        xv = x_ref[...]
        r = lax.rsqrt(jnp.mean(xv * xv, axis=-1, keepdims=True) + EPS)
        o_ref[...] = ((xv * r * g_ref[...]) * (1.0 + sc_ref[...]) + sh_ref[...]).astype(o_ref.dtype)

    row = pl.BlockSpec((tr, d), lambda i: (i, 0))
    vec = pl.BlockSpec((1, d), lambda i: (0, 0))
    return pl.pallas_call(
        body, grid=(rows // tr,), in_specs=[row, vec, vec, vec], out_specs=row,
        out_shape=jax.ShapeDtypeStruct((rows, d), BF16),
        compiler_params=_cparams(("parallel",)), name=name)(x, g, scale, shift)


CONV_CHUNK = 2 * SUBLANES


def _shifted_copies(buf, n):
    for r in range(1, SUBLANES):
        buf[r, pl.ds(0, n - SUBLANES), :] = buf[0, pl.ds(r, n - SUBLANES), :]


def _conv_fwd(proj, w_dw, b_dw, ln_g, ln_b, *, cw, carry=None):
    rows = proj.shape[0]
    tt = _pick(rows, 256, HALO)
    hb = tt // HALO

    def body(a_ref, g_ref, ha_ref, hg_ref, w_ref, b_ref, lg_ref, lb_ref, sl_ref, cv_ref, vs):
        i = pl.program_id(0)
        hv = ha_ref[...].astype(F32) * _sigmoid(hg_ref[...].astype(F32))
        vs[0, pl.ds(0, HALO), :] = jnp.where(i == 0, 0.0, hv)
        vs[0, pl.ds(HALO, tt), :] = a_ref[...].astype(F32) * _sigmoid(g_ref[...].astype(F32))
        _shifted_copies(vs, HALO + tt)

        def chunk(ci, carry):
            r0 = pl.multiple_of(ci * CONV_CHUNK, CONV_CHUNK)
            acc = jnp.broadcast_to(b_ref[...], (CONV_CHUNK, cw))
            for k in range(CONV_KERNEL):
                q, r = divmod(HALO - (CONV_KERNEL - 1) + k, SUBLANES)
                acc = acc + w_ref[pl.ds(k, 1), :] * vs[r, pl.ds(r0 + q * SUBLANES, CONV_CHUNK), :]
            cv_ref[pl.ds(r0, CONV_CHUNK), :] = acc
            return carry

        lax.fori_loop(0, tt // CONV_CHUNK, chunk, 0)
        acc = cv_ref[...]
        mu = jnp.mean(acc, axis=-1, keepdims=True)
        xc = acc - mu
        rstd = lax.rsqrt(jnp.mean(xc * xc, axis=-1, keepdims=True) + EPS)
        ln = xc * rstd * lg_ref[...] + lb_ref[...]
        sl_ref[...] = (ln * _sigmoid(ln)).astype(sl_ref.dtype)

    tile = lambda c: pl.BlockSpec((tt, cw), lambda i, c=c: (i, c))
    halo = lambda c: pl.BlockSpec((HALO, cw), lambda i, c=c: (jnp.maximum(i * hb - 1, 0), c))
    vec = pl.BlockSpec((1, cw), lambda i: (0, 0))
    outs, carried = _call(
        body, grid=(rows // tt,),
        in_specs=[tile(0), tile(1), halo(0), halo(1),
                  pl.BlockSpec((CONV_KERNEL, cw), lambda i: (0, 0)), vec, vec, vec],
        out_specs=[pl.BlockSpec((tt, cw), lambda i: (i, 0)), pl.BlockSpec((tt, cw), lambda i: (i, 0))],
        out_shape=[jax.ShapeDtypeStruct((rows, cw), BF16), jax.ShapeDtypeStruct((rows, cw), F32)],
        scratch_shapes=[pltpu.VMEM((SUBLANES, HALO + tt, cw), F32)],
        semantics=("parallel",), name="conv_fwd", args=[proj, proj, proj, proj, w_dw, b_dw, ln_g, ln_b],
        carry=carry)
    return outs if carry is None else (outs, carried)


def _ln_bwd(dsl, cv, ln_g, ln_b):
    rows, cw = cv.shape
    tr = _row_tile(rows, cw, 3)

    def body(d_ref, cv_ref, lg_ref, lb_ref, o_ref, dg_ref, db_ref):
        i = pl.program_id(0)

        @pl.when(i == 0)
        def _():
            dg_ref[...] = jnp.zeros_like(dg_ref)
            db_ref[...] = jnp.zeros_like(db_ref)

        x = cv_ref[...]
        mu = jnp.mean(x, axis=-1, keepdims=True)
        xc = x - mu
        rstd = lax.rsqrt(jnp.mean(xc * xc, axis=-1, keepdims=True) + EPS)
        xh = xc * rstd
        ln = xh * lg_ref[...] + lb_ref[...]
        s = _sigmoid(ln)
        dln = d_ref[...].astype(F32) * (s * (1.0 + ln * (1.0 - s)))
        dg_ref[...] += jnp.sum(dln * xh, axis=0, keepdims=True)
        db_ref[...] += jnp.sum(dln, axis=0, keepdims=True)
        dxh = dln * lg_ref[...]
        m1 = jnp.mean(dxh, axis=-1, keepdims=True)
        m2 = jnp.mean(dxh * xh, axis=-1, keepdims=True)
        o_ref[...] = rstd * (dxh - m1 - xh * m2)

    row = pl.BlockSpec((tr, cw), lambda i: (i, 0))
    vec = pl.BlockSpec((1, cw), lambda i: (0, 0))
    return pl.pallas_call(
        body, grid=(rows // tr,), in_specs=[row, row, vec, vec], out_specs=[row, vec, vec],
        out_shape=[jax.ShapeDtypeStruct((rows, cw), F32), jax.ShapeDtypeStruct((1, cw), F32),
                   jax.ShapeDtypeStruct((1, cw), F32)],
        compiler_params=_cparams(("arbitrary",)), name="ln_bwd")(dsl, cv, ln_g, ln_b)


def _conv_bwd(dcv, proj, w_dw, *, cw, carry=None):
    rows = proj.shape[0]
    tt = _pick(rows, 256, HALO)
    hb = tt // HALO
    nt = rows // tt
    taps = CONV_KERNEL

    def body(d_ref, dn_ref, a_ref, g_ref, ha_ref, hg_ref, w_ref, o_ref, dw_ref, db_ref, vs, ds):
        i = pl.program_id(0)

        @pl.when(i == 0)
        def _():
            dw_ref[...] = jnp.zeros_like(dw_ref)
            db_ref[...] = jnp.zeros_like(db_ref)

        hv = ha_ref[...].astype(F32) * _sigmoid(hg_ref[...].astype(F32))
        vs[0, pl.ds(0, HALO), :] = jnp.where(i == 0, 0.0, hv)
        vs[0, pl.ds(HALO, tt), :] = a_ref[...].astype(F32) * _sigmoid(g_ref[...].astype(F32))
        _shifted_copies(vs, HALO + tt)
        ds[0, pl.ds(0, tt), :] = d_ref[...]
        ds[0, pl.ds(tt, HALO), :] = jnp.where(i == nt - 1, 0.0, dn_ref[...])
        _shifted_copies(ds, tt + HALO)
        db_ref[...] += jnp.sum(d_ref[...], axis=0, keepdims=True)
        for k in range(taps):
            q, r = divmod(HALO - (taps - 1) + k, SUBLANES)
            dw_ref[pl.ds(k, 1), :] += jnp.sum(d_ref[...] * vs[r, pl.ds(q * SUBLANES, tt), :], axis=0, keepdims=True)

        def chunk(ci, carry):
            r0 = pl.multiple_of(ci * CONV_CHUNK, CONV_CHUNK)
            dv = jnp.zeros((CONV_CHUNK, cw), F32)
            for k in range(taps):
                q, r = divmod(taps - 1 - k, SUBLANES)
                dv = dv + w_ref[pl.ds(k, 1), :] * ds[r, pl.ds(r0 + q * SUBLANES, CONV_CHUNK), :]
            av = a_ref[pl.ds(r0, CONV_CHUNK), :].astype(F32)
            sg = _sigmoid(g_ref[pl.ds(r0, CONV_CHUNK), :].astype(F32))
            o_ref[pl.ds(r0, CONV_CHUNK), pl.ds(0, cw)] = (dv * sg).astype(o_ref.dtype)
            o_ref[pl.ds(r0, CONV_CHUNK), pl.ds(cw, cw)] = (dv * av * sg * (1.0 - sg)).astype(o_ref.dtype)
            return carry

        lax.fori_loop(0, tt // CONV_CHUNK, chunk, 0)

    tile = lambda c: pl.BlockSpec((tt, cw), lambda i, c=c: (i, c))
    halo = lambda c: pl.BlockSpec((HALO, cw), lambda i, c=c: (jnp.maximum(i * hb - 1, 0), c))
    nxt = pl.BlockSpec((HALO, cw), lambda i: (jnp.minimum((i + 1) * hb, nt * hb - 1), 0))
    outs, carried = _call(
        body, grid=(nt,),
        in_specs=[pl.BlockSpec((tt, cw), lambda i: (i, 0)), nxt, tile(0), tile(1), halo(0), halo(1),
                  pl.BlockSpec((taps, cw), lambda i: (0, 0))],
        out_specs=[pl.BlockSpec((tt, 2 * cw), lambda i: (i, 0)),
                   pl.BlockSpec((taps, cw), lambda i: (0, 0)), pl.BlockSpec((1, cw), lambda i: (0, 0))],
        out_shape=[jax.ShapeDtypeStruct((rows, 2 * cw), BF16), jax.ShapeDtypeStruct((taps, cw), F32),
                   jax.ShapeDtypeStruct((1, cw), F32)],
        scratch_shapes=[pltpu.VMEM((SUBLANES, HALO + tt, cw), F32), pltpu.VMEM((SUBLANES, tt + HALO, cw), F32)],
        semantics=("arbitrary",), name="conv_bwd", args=[dcv, dcv, proj, proj, proj, proj, w_dw], carry=carry)
    return outs if carry is None else (outs, carried)


def _merge_fwd(proj, y_conv, ya, yb, *, cw):
    rows = proj.shape[0]
    tr = _row_tile(rows, cw, 4)

    def body(gc_ref, gs_ref, yc_ref, ya_ref, yb_ref, o_ref):
        ys = ya_ref[...].astype(F32) * _sigmoid(yb_ref[...].astype(F32))
        o_ref[...] = (_sigmoid(gc_ref[...].astype(F32)) * yc_ref[...].astype(F32)
                      + _sigmoid(gs_ref[...].astype(F32)) * ys).astype(o_ref.dtype)

    blk = lambda off: pl.BlockSpec((tr, cw), lambda i, h, off=off: (i, off + h))
    return pl.pallas_call(
        body, grid=(rows // tr, 2), in_specs=[blk(3), blk(5), blk(0), blk(0), blk(0)], out_specs=blk(0),
        out_shape=jax.ShapeDtypeStruct((rows, 2 * cw), BF16),
        compiler_params=_cparams(("parallel", "parallel")), name="merge_fwd")(proj, proj, y_conv, ya, yb)


def _merge_bwd(dmerged, proj, y_conv, ya, yb, *, cw, carry=None):
    rows = proj.shape[0]
    tr = _row_tile(rows, cw, 6)

    def body(d_ref, gc_ref, gs_ref, yc_ref, ya_ref, yb_ref, dgc_ref, dgs_ref, dyc_ref, dya_ref, dyb_ref):
        d = d_ref[...].astype(F32)
        sc = _sigmoid(gc_ref[...].astype(F32))
        ss = _sigmoid(gs_ref[...].astype(F32))
        sb = _sigmoid(yb_ref[...].astype(F32))
        yav = ya_ref[...].astype(F32)
        dgc_ref[...] = (d * yc_ref[...].astype(F32) * sc * (1.0 - sc)).astype(dgc_ref.dtype)
        dgs_ref[...] = (d * (yav * sb) * ss * (1.0 - ss)).astype(dgs_ref.dtype)
        dyc_ref[...] = (d * sc).astype(dyc_ref.dtype)
        dys = d * ss
        dya_ref[...] = (dys * sb).astype(dya_ref.dtype)
        dyb_ref[...] = (dys * yav * sb * (1.0 - sb)).astype(dyb_ref.dtype)

    blk = lambda off: pl.BlockSpec((tr, cw), lambda i, h, off=off: (i, off + h))
    o2 = jax.ShapeDtypeStruct((rows, 2 * cw), BF16)
    outs, carried = _call(
        body, grid=(rows // tr, 2),
        in_specs=[blk(0), blk(3), blk(5), blk(0), blk(0), blk(0)],
        out_specs=[blk(0), blk(0), blk(0), blk(0), blk(0)],
        out_shape=[o2, o2, o2, o2, o2], scratch_shapes=[],
        semantics=("parallel", "parallel"), name="merge_bwd", args=[dmerged, proj, proj, y_conv, ya, yb],
        carry=carry)
    return outs if carry is None else (outs, carried)


def _res_norm(x, mo, gate, g, scale, shift):
    rows, d = x.shape
    tr = _row_tile(rows, d, 4)

    def body(x_ref, mo_ref, gt_ref, g_ref, sc_ref, sh_ref, h_ref, z_ref):
        h = x_ref[...] + gt_ref[...] * mo_ref[...].astype(F32)
        h_ref[...] = h
        r = lax.rsqrt(jnp.mean(h * h, axis=-1, keepdims=True) + EPS)
        z_ref[...] = ((h * r * g_ref[...]) * (1.0 + sc_ref[...]) + sh_ref[...]).astype(z_ref.dtype)

    row = pl.BlockSpec((tr, d), lambda i: (i, 0))
    vec = pl.BlockSpec((1, d), lambda i: (0, 0))
    return pl.pallas_call(
        body, grid=(rows // tr,), in_specs=[row, row, vec, vec, vec, vec], out_specs=[row, row],
        out_shape=[jax.ShapeDtypeStruct((rows, d), F32), jax.ShapeDtypeStruct((rows, d), BF16)],
        compiler_params=_cparams(("parallel",)), name="res_norm")(x, mo, gate, g, scale, shift)


def _final_fwd_bwd(h1, ff, gate2, final_g, target):
    rows, d = h1.shape
    tr = _row_tile(rows, d, 5)

    def body(h_ref, ff_ref, gt_ref, fg_ref, t_ref, dh_ref, dff_ref, loss_ref, dfg_ref, dgt_ref):
        i = pl.program_id(0)

        @pl.when(i == 0)
        def _():
            loss_ref[...] = jnp.zeros_like(loss_ref)
            dfg_ref[...] = jnp.zeros_like(dfg_ref)
            dgt_ref[...] = jnp.zeros_like(dgt_ref)

        ffv = ff_ref[...].astype(F32)
        h2 = h_ref[...] + gt_ref[...] * ffv
        r = lax.rsqrt(jnp.mean(h2 * h2, axis=-1, keepdims=True) + EPS)
        y = h2 * r
        e = y * fg_ref[...] - t_ref[...]
        loss_ref[...] += 0.5 * jnp.sum(jnp.mean(e * e, axis=-1, keepdims=True))
        dout = e * (1.0 / d)
        dfg_ref[...] += jnp.sum(dout * y, axis=0, keepdims=True)
        dy = dout * fg_ref[...]
        dh2 = r * (dy - y * jnp.mean(dy * y, axis=-1, keepdims=True))
        dh_ref[...] = dh2
        dgt_ref[...] += jnp.sum(dh2 * ffv, axis=0, keepdims=True)
        dff_ref[...] = (dh2 * gt_ref[...]).astype(dff_ref.dtype)

    row = pl.BlockSpec((tr, d), lambda i: (i, 0))
    vec = pl.BlockSpec((1, d), lambda i: (0, 0))
    return pl.pallas_call(
        body, grid=(rows // tr,), in_specs=[row, row, vec, vec, row],
        out_specs=[row, row, pl.BlockSpec((1, LANES), lambda i: (0, 0)), vec, vec],
        out_shape=[jax.ShapeDtypeStruct((rows, d), F32), jax.ShapeDtypeStruct((rows, d), BF16),
                   jax.ShapeDtypeStruct((1, LANES), F32), jax.ShapeDtypeStruct((1, d), F32),
                   jax.ShapeDtypeStruct((1, d), F32)],
        compiler_params=_cparams(("arbitrary",)), name="final_fwd_bwd")(h1, ff, gate2, final_g, target)


def _norm_mod_bwd(dz, hin, dres, g, scale, gate, mo, *, name, carry=None):
    rows, d = hin.shape
    with_gate = gate is not None
    tr = _row_tile(rows, d, 6)

    def body(*refs):
        if with_gate:
            (dz_ref, h_ref, dr_ref, g_ref, sc_ref, gt_ref, mo_ref,
             dh_ref, dsh_ref, dsc_ref, dg_ref, dmo_ref, dgt_ref) = refs
        else:
            dz_ref, h_ref, dr_ref, g_ref, sc_ref, dh_ref, dsh_ref, dsc_ref, dg_ref = refs
        i = pl.program_id(0)

        @pl.when(i == 0)
        def _():
            dsh_ref[...] = jnp.zeros_like(dsh_ref)
            dsc_ref[...] = jnp.zeros_like(dsc_ref)
            dg_ref[...] = jnp.zeros_like(dg_ref)
            if with_gate:
                dgt_ref[...] = jnp.zeros_like(dgt_ref)

        dzv = dz_ref[...].astype(F32)
        h = h_ref[...]
        r = lax.rsqrt(jnp.mean(h * h, axis=-1, keepdims=True) + EPS)
        y = h * r
        dsh_ref[...] += jnp.sum(dzv, axis=0, keepdims=True)
        dsc_ref[...] += jnp.sum(dzv * (y * g_ref[...]), axis=0, keepdims=True)
        dn = dzv * (1.0 + sc_ref[...])
        dg_ref[...] += jnp.sum(dn * y, axis=0, keepdims=True)
        dy = dn * g_ref[...]
        dh = dr_ref[...] + r * (dy - y * jnp.mean(dy * y, axis=-1, keepdims=True))
        dh_ref[...] = dh
        if with_gate:
            dmo_ref[...] = (dh * gt_ref[...]).astype(dmo_ref.dtype)
            dgt_ref[...] += jnp.sum(dh * mo_ref[...].astype(F32), axis=0, keepdims=True)

    row = pl.BlockSpec((tr, d), lambda i: (i, 0))
    vec = pl.BlockSpec((1, d), lambda i: (0, 0))
    vshape = jax.ShapeDtypeStruct((1, d), F32)
    in_specs = [row, row, row, vec, vec]
    args = [dz, hin, dres, g, scale]
    out_specs = [row, vec, vec, vec]
    out_shape = [jax.ShapeDtypeStruct((rows, d), F32), vshape, vshape, vshape]
    if with_gate:
        in_specs += [vec, row]
        args += [gate, mo]
        out_specs += [row, vec]
        out_shape += [jax.ShapeDtypeStruct((rows, d), BF16), vshape]
    outs, carried = _call(
        body, grid=(rows // tr,), in_specs=in_specs, out_specs=out_specs, out_shape=out_shape,
        scratch_shapes=[], semantics=("arbitrary",), name=name, args=args, carry=carry)
    return outs if carry is None else (outs, carried)


def _s5_discretise(a_re, a_im, log_dt, b_re, b_im):
    dt = jnp.exp(log_dt)[:, None]
    er = jnp.exp(a_re * dt)
    lr = er * jnp.cos(a_im * dt)
    li = er * jnp.sin(a_im * dt)
    den = a_re * a_re + a_im * a_im
    cr = ((lr - 1.0) * a_re + li * a_im) / den
    ci = (li * a_re - (lr - 1.0) * a_im) / den
    bbr = cr[..., None] * b_re - ci[..., None] * b_im
    bbi = cr[..., None] * b_im + ci[..., None] * b_re
    return lr, li, bbr, bbi


def _block_diag(w):
    g, r, c = w.shape
    nb = g // GROUPS_PER_BLOCK
    eye = jnp.eye(GROUPS_PER_BLOCK, dtype=w.dtype)
    w5 = w.reshape(nb, GROUPS_PER_BLOCK, r, 1, c) * eye[None, :, None, :, None]
    return w5.reshape(nb, GROUPS_PER_BLOCK * r, GROUPS_PER_BLOCK * c)


def _block_diag_extract(m, r, c):
    nb = m.shape[0]
    m5 = m.reshape(nb, GROUPS_PER_BLOCK, r, GROUPS_PER_BLOCK, c)
    idx = jnp.arange(GROUPS_PER_BLOCK)
    d = m5[:, idx, :, idx, :]
    return jnp.moveaxis(d, 0, 1).reshape(nb * GROUPS_PER_BLOCK, r, c)


def _scan_multipliers(lr, li):
    power = jnp.arange(1, SUBLANES + 1, dtype=F32)[None, :, None]
    er = jnp.exp(power * lr)
    pr = er * jnp.cos(power * li)
    pi = er * jnp.sin(power * li)
    rows = jnp.arange(SUBLANES)[None, :, None]
    fr, fi, rr, ri = [], [], [], []
    for s in (1, 2, 4):
        mf = (rows >= s).astype(F32)
        mr = (rows <= SUBLANES - 1 - s).astype(F32)
        fr.append(mf * pr[:, s - 1:s, :])
        fi.append(mf * pi[:, s - 1:s, :])
        rr.append(mr * pr[:, s - 1:s, :])
        ri.append(mr * pi[:, s - 1:s, :])
    fr.append(pr)
    fi.append(pi)
    rr.append(pr[:, ::-1, :])
    ri.append(pi[:, ::-1, :])
    st = lambda xs: jnp.stack(xs, axis=1)
    return st(fr), st(fi), st(rr), st(ri)


def _scan_rows(sre, sim, mul_r, mul_i, n_groups, reverse):
    sgn = -1.0 if reverse else 1.0
    lanes = sre.shape[1]

    def step(k, carry):
        cr, ci = carry
        kk = (n_groups - 1 - k) if reverse else k
        r0 = pl.multiple_of(kk * SUBLANES, SUBLANES)
        xr = sre[pl.ds(r0, SUBLANES), :]
        xi = sim[pl.ds(r0, SUBLANES), :]
        for lvl, s in enumerate((1, 2, 4)):
            sh = (SUBLANES - s) if reverse else s
            nr = pltpu.roll(xr, sh, 0)
            ni = pltpu.roll(xi, sh, 0)
            mr = mul_r[lvl]
            mi = mul_i[lvl] * sgn
            xr, xi = xr + mr * nr - mi * ni, xi + mr * ni + mi * nr
        mr = mul_r[3]
        mi = mul_i[3] * sgn
        xr, xi = xr + mr * cr - mi * ci, xi + mr * ci + mi * cr
        sre[pl.ds(r0, SUBLANES), :] = xr
        sim[pl.ds(r0, SUBLANES), :] = xi
        edge = 0 if reverse else SUBLANES - 1
        ncr = jnp.broadcast_to(xr[edge:edge + 1, :], (SUBLANES, lanes))
        nci = jnp.broadcast_to(xi[edge:edge + 1, :], (SUBLANES, lanes))
        return ncr, nci

    zero = jnp.zeros((SUBLANES, lanes), F32)
    lax.fori_loop(0, n_groups, step, (zero, zero))


def _dot(a, b):
    return jnp.dot(a, b, preferred_element_type=F32)


def _dotf(a, b):
    return _dot(a.astype(BF16), b)


def _s5_operands(lr, li, bbr, bbi, c_re, c_im):
    g = lr.shape[0]
    nb = g // GROUPS_PER_BLOCK
    tb = lambda w: jnp.swapaxes(w, 1, 2)
    b_in = [_block_diag(tb(bbr)), _block_diag(tb(bbi))]
    c_out = [_block_diag(tb(c_re)), _block_diag(tb(c_im))]
    b_out = [_block_diag(bbr), _block_diag(bbi)]
    c_in = [_block_diag(c_re), _block_diag(c_im)]
    lam_r = lr.reshape(nb, 1, STATE_LANES)
    lam_i = li.reshape(nb, 1, STATE_LANES)
    mults = _scan_multipliers(lam_r, lam_i)
    cast = lambda ws: [w.astype(BF16) for w in ws]
    return cast(b_in), cast(c_out), cast(b_out), cast(c_in), mults


def _s5_fwd(proj, d_skip, b_in, c_out, mults, *, col0, carry=None):
    rows = proj.shape[0]
    nb = b_in[0].shape[0]
    tm = _pick(rows, 512, SUBLANES)
    n_tiles = rows // tm
    s_l = STATE_LANES

    def body(u_ref, dk_ref, br, bi, cr, ci, fr_ref, fi_ref, o_ref, sr_ref, si_ref, sre, sim):
        for t in range(n_tiles):
            rs = pl.ds(t * tm, tm)
            ub = u_ref[rs, :]
            sre[rs, :] = _dot(ub, br[...])
            sim[rs, :] = _dot(ub, bi[...])
        _scan_rows(sre, sim, fr_ref, fi_ref, rows // SUBLANES, False)
        for t in range(n_tiles):
            rs = pl.ds(t * tm, tm)
            srb = sre[rs, :].astype(BF16)
            sib = sim[rs, :].astype(BF16)
            sr_ref[rs, :] = srb
            si_ref[rs, :] = sib
            y0 = _dot(srb, cr[...]) - _dot(sib, ci[...])
            y1 = y0 + dk_ref[...] * u_ref[rs, :].astype(F32)
            o_ref[rs, :] = _gelu(y1).astype(o_ref.dtype)

    mat_in = pl.BlockSpec((None, LANES, s_l), lambda g: (g, 0, 0))
    mat_out = pl.BlockSpec((None, s_l, LANES), lambda g: (g, 0, 0))
    mul = pl.BlockSpec((None, 4, SUBLANES, s_l), lambda g: (g, 0, 0, 0))
    state = pl.BlockSpec((rows, s_l), lambda g: (0, g))
    outs, carried = _call(
        body, grid=(nb,),
        in_specs=[pl.BlockSpec((rows, LANES), lambda g: (0, col0 + g)), pl.BlockSpec((1, LANES), lambda g: (0, g))]
        + [mat_in] * 2 + [mat_out] * 2 + [mul] * 2,
        out_specs=[pl.BlockSpec((rows, LANES), lambda g: (0, g)), state, state],
        out_shape=[jax.ShapeDtypeStruct((rows, nb * LANES), BF16), jax.ShapeDtypeStruct((rows, nb * s_l), BF16),
                   jax.ShapeDtypeStruct((rows, nb * s_l), BF16)],
        scratch_shapes=[pltpu.VMEM((rows, s_l), F32), pltpu.VMEM((rows, s_l), F32)],
        semantics=("parallel",), name="s5_fwd", args=[proj, d_skip, *b_in, *c_out, mults[0], mults[1]], carry=carry)
    return outs if carry is None else (outs, carried)


def _s5_bwd(proj, dyg, d_skip, states, c_out, b_out, c_in, mults, *, col0, carry=None):
    rows = proj.shape[0]
    nb = c_out[0].shape[0]
    tm = _pick(rows, 512, SUBLANES)
    n_tiles = rows // tm
    s_l = STATE_LANES
    n_groups = rows // SUBLANES
    tn = (((0,), (0,)), ((), ()))

    def body(u_ref, dy_ref, dk_ref, sr_ref, si_ref, cr, ci, bor, boi, cir, cii, rr_ref, ri_ref,
             du_ref, ddk_ref, dbr_ref, dbi_ref, dcr_ref, dci_ref, dlr_ref, dli_ref,
             gre, gim, dy1):
        ddk = jnp.zeros((1, LANES), F32)
        dcr = jnp.zeros((s_l, LANES), F32)
        dci = jnp.zeros((s_l, LANES), F32)
        for t in range(n_tiles):
            rs = pl.ds(t * tm, tm)
            srb = sr_ref[rs, :]
            sib = si_ref[rs, :]
            uf = u_ref[rs, :].astype(F32)
            y0 = _dot(srb, cr[...]) - _dot(sib, ci[...])
            y1 = y0 + dk_ref[...] * uf
            d1 = dy_ref[rs, :].astype(F32) * _gelu_grad(y1)
            dy1[rs, :] = d1
            ddk = ddk + jnp.sum(d1 * uf, axis=0, keepdims=True)
            d1b = d1.astype(BF16)
            dcr = dcr + lax.dot_general(srb, d1b, tn, preferred_element_type=F32)
            dci = dci - lax.dot_general(sib, d1b, tn, preferred_element_type=F32)
            gre[rs, :] = _dot(d1b, cir[...])
            gim[rs, :] = -_dot(d1b, cii[...])
        ddk_ref[...] = ddk
        dcr_ref[...] = dcr
        dci_ref[...] = dci

        last_row = lax.broadcasted_iota(jnp.int32, (SUBLANES, s_l), 0) == SUBLANES - 1

        def group(r0, s_r, s_i, carry):
            cr_, ci_, ar, ai = carry
            xr = gre[pl.ds(r0, SUBLANES), :]
            xi = gim[pl.ds(r0, SUBLANES), :]
            for lvl, s in enumerate((1, 2, 4)):
                nr = pltpu.roll(xr, SUBLANES - s, 0)
                ni = pltpu.roll(xi, SUBLANES - s, 0)
                mr = rr_ref[lvl]
                mi = ri_ref[lvl]
                xr, xi = xr + mr * nr + mi * ni, xi + mr * ni - mi * nr
            mr = rr_ref[3]
            mi = ri_ref[3]
            xr, xi = xr + mr * cr_ + mi * ci_, xi + mr * ci_ - mi * cr_
            gre[pl.ds(r0, SUBLANES), :] = xr
            gim[pl.ds(r0, SUBLANES), :] = xi
            nxt_r = jnp.where(last_row, cr_, pltpu.roll(xr, SUBLANES - 1, 0))
            nxt_i = jnp.where(last_row, ci_, pltpu.roll(xi, SUBLANES - 1, 0))
            ncr = jnp.broadcast_to(xr[0:1, :], (SUBLANES, s_l))
            nci = jnp.broadcast_to(xi[0:1, :], (SUBLANES, s_l))
            return ncr, nci, ar + nxt_r * s_r + nxt_i * s_i, ai + nxt_i * s_r - nxt_r * s_i

        def rev_step(k, carry):
            r0 = pl.multiple_of((n_groups // 2 - 1 - k) * 2 * SUBLANES, 2 * SUBLANES)
            s_r = sr_ref[pl.ds(r0, 2 * SUBLANES), :].astype(F32)
            s_i = si_ref[pl.ds(r0, 2 * SUBLANES), :].astype(F32)
            carry = group(r0 + SUBLANES, s_r[SUBLANES:], s_i[SUBLANES:], carry)
            return group(r0, s_r[:SUBLANES], s_i[:SUBLANES], carry)

        zero = jnp.zeros((SUBLANES, s_l), F32)
        _, _, ar, ai = lax.fori_loop(0, n_groups // 2, rev_step, (zero, zero, zero, zero))
        dlr_ref[...] = jnp.sum(ar, axis=0, keepdims=True)
        dli_ref[...] = jnp.sum(ai, axis=0, keepdims=True)

        dbr = jnp.zeros((LANES, s_l), F32)
        dbi = jnp.zeros((LANES, s_l), F32)
        for t in range(n_tiles):
            rs = pl.ds(t * tm, tm)
            gr = gre[rs, :]
            gi = gim[rs, :]
            grb = gr.astype(BF16)
            gib = gi.astype(BF16)
            du = _dot(grb, bor[...]) + _dot(gib, boi[...]) + dy1[rs, :] * dk_ref[...]
            du_ref[rs, :] = du.astype(du_ref.dtype)
            ub = u_ref[rs, :]
            dbr = dbr + lax.dot_general(ub, grb, tn, preferred_element_type=F32)
            dbi = dbi + lax.dot_general(ub, gib, tn, preferred_element_type=F32)
        dbr_ref[...] = dbr
        dbi_ref[...] = dbi

    mat_in = pl.BlockSpec((None, LANES, s_l), lambda g: (g, 0, 0))
    mat_out = pl.BlockSpec((None, s_l, LANES), lambda g: (g, 0, 0))
    mul = pl.BlockSpec((None, 4, SUBLANES, s_l), lambda g: (g, 0, 0, 0))
    lam = pl.BlockSpec((None, 1, s_l), lambda g: (g, 0, 0))
    col = pl.BlockSpec((rows, LANES), lambda g: (0, g))
    vec = pl.BlockSpec((1, LANES), lambda g: (0, g))
    state = pl.BlockSpec((rows, s_l), lambda g: (0, g))
    outs, carried = _call(
        body, grid=(nb,),
        in_specs=[pl.BlockSpec((rows, LANES), lambda g: (0, col0 + g)), col, vec]
        + [state] * 2 + [mat_out] * 2 + [mat_out] * 2 + [mat_in] * 2 + [mul] * 2,
        out_specs=[col, vec, mat_in, mat_in, mat_out, mat_out, lam, lam],
        out_shape=[jax.ShapeDtypeStruct((rows, nb * LANES), BF16), jax.ShapeDtypeStruct((1, nb * LANES), F32),
                   jax.ShapeDtypeStruct((nb, LANES, s_l), F32), jax.ShapeDtypeStruct((nb, LANES, s_l), F32),
                   jax.ShapeDtypeStruct((nb, s_l, LANES), F32), jax.ShapeDtypeStruct((nb, s_l, LANES), F32),
                   jax.ShapeDtypeStruct((nb, 1, s_l), F32), jax.ShapeDtypeStruct((nb, 1, s_l), F32)],
        scratch_shapes=[pltpu.VMEM((rows, s_l), F32)] * 2 + [pltpu.VMEM((rows, LANES), F32)],
        semantics=("parallel",), name="s5_bwd",
        args=[proj, dyg, d_skip, *states, *c_out, *b_out, *c_in, mults[2], mults[3]], carry=carry)
    return outs if carry is None else (outs, carried)


def _silu(v):
    return v * _sigmoid(v)


def _ada_fwd(c_all, w_shard, b_cols):
    d, n = w_shard.shape
    bn = _pick(n, 512)

    def body(c_ref, w_ref, b_ref, o_ref):
        ca = _silu(c_ref[...]).astype(BF16)
        o_ref[...] = _dot(ca, w_ref[...].astype(BF16)) + b_ref[...]

    return pl.pallas_call(
        body, grid=(n // bn,),
        in_specs=[pl.BlockSpec((N_DEV, d), lambda j: (0, 0)), pl.BlockSpec((d, bn), lambda j: (0, j)),
                  pl.BlockSpec((1, bn), lambda j: (0, j))],
        out_specs=pl.BlockSpec((N_DEV, bn), lambda j: (0, j)),
        out_shape=jax.ShapeDtypeStruct((N_DEV, n), F32),
        compiler_params=_cparams(("parallel",)), name="ada_fwd")(c_all, w_shard, b_cols)


def _ada_bwd(c_all, dmod_cols):
    d = c_all.shape[1]
    n = dmod_cols.shape[1]
    bn = _pick(n, 512)

    def body(c_ref, g_ref, o_ref):
        ca = _silu(c_ref[...]).astype(BF16)
        o_ref[...] = lax.dot_general(ca, g_ref[...].astype(BF16), (((0,), (0,)), ((), ())),
                                     preferred_element_type=F32)

    return pl.pallas_call(
        body, grid=(n // bn,),
        in_specs=[pl.BlockSpec((N_DEV, d), lambda j: (0, 0)), pl.BlockSpec((N_DEV, bn), lambda j: (0, j))],
        out_specs=pl.BlockSpec((d, bn), lambda j: (0, j)),
        out_shape=jax.ShapeDtypeStruct((d, n), F32),
        compiler_params=_cparams(("parallel",)), name="ada_bwd")(c_all, dmod_cols)


def _cast_bf16(w, *, name):
    rows, cols = w.shape
    tr = _row_tile(rows, cols, 2)

    def body(w_ref, o_ref):
        o_ref[...] = w_ref[...].astype(BF16)

    row = pl.BlockSpec((tr, cols), lambda i: (i, 0))
    return pl.pallas_call(
        body, grid=(rows // tr,), in_specs=[row], out_specs=row,
        out_shape=jax.ShapeDtypeStruct((rows, cols), BF16),
        compiler_params=_cparams(("parallel",)), name=name)(w)


def _adamw(w, g, m, v, *, name, carry=None):
    rows, cols = w.shape
    tr = _row_tile(rows, cols, 7)
    c1 = 1.0 / (1.0 - ADAM_B1 ** ADAM_STEP)
    c2 = 1.0 / (1.0 - ADAM_B2 ** ADAM_STEP)

    def body(w_ref, g_ref, m_ref, v_ref, d_ref, nm_ref, nv_ref):
        gv = g_ref[...]
        nm = ADAM_B1 * m_ref[...] + (1.0 - ADAM_B1) * gv
        nv = ADAM_B2 * v_ref[...] + (1.0 - ADAM_B2) * (gv * gv)
        nm_ref[...] = nm
        nv_ref[...] = nv
        d_ref[...] = -ADAM_LR * ((nm * c1) / (jnp.sqrt(nv * c2) + ADAM_EPS) + ADAM_WD * w_ref[...])

    row = pl.BlockSpec((tr, cols), lambda i: (i, 0))
    shp = jax.ShapeDtypeStruct((rows, cols), F32)
    outs, carried = _call(
        body, grid=(rows // tr,), in_specs=[row] * 4, out_specs=[row] * 3, out_shape=[shp] * 3,
        scratch_shapes=[], semantics=("parallel",), name=name, args=[w, g, m, v], carry=carry)
    return outs if carry is None else (outs, carried)


def _sum_leading(a, *, name, out_dtype=F32):
    n, rows, cols = a.shape
    tr = _row_tile(rows, cols, n + 1)

    def body(a_ref, o_ref):
        acc = a_ref[0].astype(F32)
        for i in range(1, n):
            acc = acc + a_ref[i].astype(F32)
        o_ref[...] = acc.astype(o_ref.dtype)

    return pl.pallas_call(
        body, grid=(rows // tr,), in_specs=[pl.BlockSpec((n, tr, cols), lambda i: (0, i, 0))],
        out_specs=pl.BlockSpec((tr, cols), lambda i: (i, 0)),
        out_shape=jax.ShapeDtypeStruct((rows, cols), out_dtype),
        compiler_params=_cparams(("parallel",)), name=name)(a)


def _add_half(dw, land, my_c, *, name):
    n, r, cols = dw.shape
    h = r // 2
    tr = _row_tile(h, cols, 3)
    hb = h // tr

    def body(c_ref, a_ref, b_ref, o_ref):
        o_ref[...] = (a_ref[...].astype(F32) + b_ref[...].astype(F32)).astype(o_ref.dtype)

    gs = pltpu.PrefetchScalarGridSpec(
        num_scalar_prefetch=1, grid=(n, hb),
        in_specs=[pl.BlockSpec((None, tr, cols), lambda s, i, c_ref: (s, c_ref[0] * hb + i, 0)),
                  pl.BlockSpec((None, tr, cols), lambda s, i, c_ref: (s, i, 0))],
        out_specs=pl.BlockSpec((None, tr, cols), lambda s, i, c_ref: (s, i, 0)))
    return pl.pallas_call(
        body, grid_spec=gs, out_shape=jax.ShapeDtypeStruct((n, h, cols), BF16),
        compiler_params=_cparams(("parallel", "parallel")), name=name)(my_c, dw, land)


def _mesh_pos():
    return lax.axis_index("x"), lax.axis_index("y"), lax.axis_index("c")


def _other_chips(x, y):
    return [(1 - x, y), (x, 1 - y), (1 - x, 1 - y)]


def _gather_small(blk, *, name):
    m_per, n = blk.shape

    def body(x_ref, out_ref, send_sems, recv_sems, local_sem):
        x, y, c = _mesh_pos()
        me, sibling = (x, y, c), (x, y, 1 - c)
        chips = _other_chips(x, y)

        def rows(px, py, pc):
            return out_ref.at[pl.ds((4 * px + 2 * py + pc) * m_per, m_per), :]

        def copy(k, block, to, src=None):
            return pltpu.make_async_remote_copy(
                src_ref=rows(*block) if src is None else src, dst_ref=rows(*block),
                send_sem=send_sems.at[k], recv_sem=recv_sems.at[k], device_id=to, device_id_type=MESH)

        mine = pltpu.make_async_copy(x_ref, rows(*me), local_sem)
        mine.start()
        first = [copy(0, me, sibling, src=x_ref)]
        first += [copy(1 + j, me, (*chip, c), src=x_ref) for j, chip in enumerate(chips)]
        for cp in first:
            cp.start()
        passed = [copy(4 + j, (*chip, c), sibling) for j, chip in enumerate(chips)]
        for j, chip in enumerate(chips):
            copy(1 + j, (*chip, c), me).wait_recv()
            passed[j].start()
        copy(0, sibling, me).wait_recv()
        for j, chip in enumerate(chips):
            copy(4 + j, (*chip, 1 - c), me).wait_recv()
        for cp in first + passed:
            cp.wait_send()
        mine.wait()

    return pl.pallas_call(
        body, out_shape=jax.ShapeDtypeStruct((N_DEV * m_per, n), blk.dtype),
        in_specs=[pl.BlockSpec(memory_space=pltpu.VMEM)], out_specs=pl.BlockSpec(memory_space=pltpu.VMEM),
        scratch_shapes=[pltpu.SemaphoreType.DMA((7,)), pltpu.SemaphoreType.DMA((7,)), pltpu.SemaphoreType.DMA],
        compiler_params=pltpu.CompilerParams(vmem_limit_bytes=VMEM_LIMIT_BYTES), name=name)(blk)


def _hbm_specs(n):
    return [pl.BlockSpec(memory_space=pl.ANY)] * n


def _gather_weights(shards):
    n = len(shards)

    def body(*refs):
        ins, outs = refs[:n], refs[n:2 * n]
        send_sems, recv_sems, local_sems = refs[2 * n:]
        x, y, c = _mesh_pos()
        me_chip = 2 * x + y
        sibling = (x, y, 1 - c)
        chips = _other_chips(x, y)

        def half(w, chip_idx, pc):
            h = shards[w].shape[0] // 2
            return outs[w].at[chip_idx, pl.ds(pc * h, h), :]

        def copy(w, k, chip_idx, pc, to, src=None):
            dst = half(w, chip_idx, pc)
            return pltpu.make_async_remote_copy(
                src_ref=dst if src is None else src, dst_ref=dst,
                send_sem=send_sems.at[6 * w + k], recv_sem=recv_sems.at[6 * w + k],
                device_id=to, device_id_type=MESH)

        local = [pltpu.make_async_copy(ins[w], outs[w].at[me_chip], local_sems.at[w]) for w in range(n)]
        for cp in local:
            cp.start()
        sends = []
        for w in range(n):
            h = shards[w].shape[0] // 2
            for j, chip in enumerate(chips):
                cp = copy(w, j, me_chip, c, (*chip, c), src=ins[w].at[pl.ds(c * h, h), :])
                cp.start()
                sends.append(cp)
        for w in range(n):
            for j, chip in enumerate(chips):
                chip_idx = 2 * chip[0] + chip[1]
                copy(w, j, chip_idx, c, (x, y, c)).wait_recv()
                cp = copy(w, 3 + j, chip_idx, c, sibling)
                cp.start()
                sends.append(cp)
        for w in range(n):
            for j, chip in enumerate(chips):
                copy(w, 3 + j, 2 * chip[0] + chip[1], 1 - c, (x, y, c)).wait_recv()
        for cp in sends:
            cp.wait_send()
        for cp in local:
            cp.wait()

    return pl.pallas_call(
        body, out_shape=[jax.ShapeDtypeStruct((N_CHIPS,) + s.shape, s.dtype) for s in shards],
        in_specs=_hbm_specs(n), out_specs=_hbm_specs(n),
        scratch_shapes=[pltpu.SemaphoreType.DMA((6 * n,)), pltpu.SemaphoreType.DMA((6 * n,)),
                        pltpu.SemaphoreType.DMA((n,))],
        name="gather_weights")(*shards)


def _swap_halves(dws, *, name):
    n = len(dws)

    def body(*refs):
        ins, outs = refs[:n], refs[n:2 * n]
        send_sems, recv_sems = refs[2 * n:]
        x, y, c = _mesh_pos()
        cps = []
        for w in range(n):
            h = dws[w].shape[1] // 2
            cp = pltpu.make_async_remote_copy(
                src_ref=ins[w].at[:, pl.ds((1 - c) * h, h), :], dst_ref=outs[w],
                send_sem=send_sems.at[w], recv_sem=recv_sems.at[w],
                device_id=(x, y, 1 - c), device_id_type=MESH)
            cp.start()
            cps.append(cp)
        for cp in cps:
            cp.wait()

    return pl.pallas_call(
        body, out_shape=[jax.ShapeDtypeStruct((s.shape[0], s.shape[1] // 2, s.shape[2]), s.dtype) for s in dws],
        in_specs=_hbm_specs(n), out_specs=_hbm_specs(n),
        scratch_shapes=[pltpu.SemaphoreType.DMA((n,)), pltpu.SemaphoreType.DMA((n,))],
        name=name)(*dws)


def _chip_exchange(parts):
    n = len(parts)

    def body(*refs):
        ins, outs = refs[:n], refs[n:2 * n]
        send_sems, recv_sems, local_sems = refs[2 * n:]
        x, y, c = _mesh_pos()
        me_chip = 2 * x + y
        chips = _other_chips(x, y)
        local = [pltpu.make_async_copy(ins[w].at[me_chip], outs[w].at[me_chip], local_sems.at[w]) for w in range(n)]
        for cp in local:
            cp.start()
        cps = []
        for w in range(n):
            for j, chip in enumerate(chips):
                cp = pltpu.make_async_remote_copy(
                    src_ref=ins[w].at[2 * chip[0] + chip[1]], dst_ref=outs[w].at[me_chip],
                    send_sem=send_sems.at[3 * w + j], recv_sem=recv_sems.at[3 * w + j],
                    device_id=(*chip, c), device_id_type=MESH)
                cp.start()
                cps.append((cp, w, j, chip))
        for cp, w, j, chip in cps:
            slot = outs[w].at[2 * chip[0] + chip[1]]
            pltpu.make_async_remote_copy(
                src_ref=slot, dst_ref=slot, send_sem=send_sems.at[3 * w + j], recv_sem=recv_sems.at[3 * w + j],
                device_id=(x, y, c), device_id_type=MESH).wait_recv()
        for cp, _, _, _ in cps:
            cp.wait_send()
        for cp in local:
            cp.wait()

    return pl.pallas_call(
        body, out_shape=[jax.ShapeDtypeStruct(s.shape, s.dtype) for s in parts],
        in_specs=_hbm_specs(n), out_specs=_hbm_specs(n),
        scratch_shapes=[pltpu.SemaphoreType.DMA((3 * n,)), pltpu.SemaphoreType.DMA((3 * n,)),
                        pltpu.SemaphoreType.DMA((n,))],
        name="chip_exchange")(*parts)


def _join_halves(halves):
    n = len(halves)

    def body(*refs):
        ins, outs = refs[:n], refs[n:2 * n]
        send_sems, recv_sems, local_sems = refs[2 * n:]
        x, y, c = _mesh_pos()
        cps, local = [], []
        for w in range(n):
            h = halves[w].shape[0]
            mine = outs[w].at[pl.ds(c * h, h), :]
            lc = pltpu.make_async_copy(ins[w], mine, local_sems.at[w])
            lc.start()
            local.append(lc)
            cp = pltpu.make_async_remote_copy(
                src_ref=ins[w], dst_ref=mine, send_sem=send_sems.at[w], recv_sem=recv_sems.at[w],
                device_id=(x, y, 1 - c), device_id_type=MESH)
            cp.start()
            cps.append(cp)
        for w in range(n):
            h = halves[w].shape[0]
            theirs = outs[w].at[pl.ds((1 - c) * h, h), :]
            pltpu.make_async_remote_copy(
                src_ref=theirs, dst_ref=theirs, send_sem=send_sems.at[w], recv_sem=recv_sems.at[w],
                device_id=(x, y, c), device_id_type=MESH).wait_recv()
        for cp in cps:
            cp.wait_send()
        for lc in local:
            lc.wait()

    return pl.pallas_call(
        body, out_shape=[jax.ShapeDtypeStruct((2 * s.shape[0], s.shape[1]), s.dtype) for s in halves],
        in_specs=_hbm_specs(n), out_specs=_hbm_specs(n),
        scratch_shapes=[pltpu.SemaphoreType.DMA((n,)), pltpu.SemaphoreType.DMA((n,)), pltpu.SemaphoreType.DMA((n,))],
        name="join_halves")(*halves)


def _cast_into_slot(w, chip, after, *, name):
    rows, cols = w.shape
    tr = _row_tile(rows, cols, 2)

    def body(chip_ref, w_ref, after_ref, o_ref):
        o_ref[...] = w_ref[...].astype(BF16)

    gs = pltpu.PrefetchScalarGridSpec(
        num_scalar_prefetch=1, grid=(rows // tr,),
        in_specs=[pl.BlockSpec((tr, cols), lambda i, chip_ref: (i, 0)), pl.BlockSpec(memory_space=pl.ANY)],
        out_specs=pl.BlockSpec((None, tr, cols), lambda i, chip_ref: (chip_ref[0], i, 0)))
    return pl.pallas_call(
        body, grid_spec=gs, out_shape=jax.ShapeDtypeStruct((N_CHIPS, rows, cols), BF16),
        compiler_params=_cparams(("parallel",)), name=name)(chip, w, after)


def _row_range(h, lo, hi, parts):
    step = h // parts
    assert step * parts == h and step % (2 * SUBLANES) == 0, (h, parts)
    return lo * step, (hi - lo) * step


def _gather_carry(items):
    n_copies = sum(len(js) for _, js, _, _, _ in items)
    sem = pltpu.SemaphoreType.DMA((2 * n_copies,))

    def copies(outs, sems):
        send_sems, recv_sems = sems
        x, y, c = _mesh_pos()
        me_chip = 2 * x + y
        chips = _other_chips(x, y)
        out_ici, in_ici, out_d2d, in_d2d = [], [], [], []
        k = 0
        for w, (buf, js, lo, hi, parts) in enumerate(items):
            h = buf.shape[1] // 2
            r0, nr = _row_range(h, lo, hi, parts)

            def copy(k, chip_idx, pc, to):
                ref = outs[w].at[chip_idx, pl.ds(pc * h + r0, nr), :]
                return pltpu.make_async_remote_copy(
                    src_ref=ref, dst_ref=ref, send_sem=send_sems.at[k], recv_sem=recv_sems.at[k],
                    device_id=to, device_id_type=MESH)

            for j in js:
                chip = chips[j]
                chip_idx = 2 * chip[0] + chip[1]
                out_ici.append(copy(k, me_chip, c, (*chip, c)))
                in_ici.append(copy(k, chip_idx, c, (x, y, c)))
                out_d2d.append(copy(k + 1, chip_idx, c, (x, y, 1 - c)))
                in_d2d.append(copy(k + 1, chip_idx, 1 - c, (x, y, c)))
                k += 2
        return out_ici, in_ici, out_d2d, in_d2d

    def start(ins, outs, sems):
        for cp in copies(outs, sems)[0]:
            cp.start()

    def finish(ins, outs, sems):
        out_ici, in_ici, out_d2d, in_d2d = copies(outs, sems)
        for arrived, onward in zip(in_ici, out_d2d):
            arrived.wait_recv()
            onward.start()
        for arrived in in_d2d:
            arrived.wait_recv()
        for cp in out_ici + out_d2d:
            cp.wait_send()

    bufs = [it[0] for it in items]
    shapes = [jax.ShapeDtypeStruct(b.shape, b.dtype) for b in bufs]
    return _Carry(bufs, shapes, {i: i for i in range(len(bufs))}, [sem, sem], start, finish)


def _exchange_carry(items):
    n = len(items)
    sem = pltpu.SemaphoreType.DMA((3 * n,))
    given = [w for w in range(n) if items[w][1] is not None]

    def copies(ins, outs, sems):
        send_sems, recv_sems = sems
        x, y, c = _mesh_pos()
        chips = _other_chips(x, y)
        sends, recvs = [], []
        for w, (part, _, lo, hi, parts) in enumerate(items):
            r0, nr = _row_range(part.shape[1], lo, hi, parts)
            for j, chip in enumerate(chips):
                land = outs[w].at[j, pl.ds(r0, nr), :]
                sends.append(pltpu.make_async_remote_copy(
                    src_ref=ins[w].at[2 * chip[0] + chip[1], pl.ds(r0, nr), :], dst_ref=land,
                    send_sem=send_sems.at[3 * w + j], recv_sem=recv_sems.at[3 * w + j],
                    device_id=(*chip, c), device_id_type=MESH))
                recvs.append(pltpu.make_async_remote_copy(
                    src_ref=land, dst_ref=land,
                    send_sem=send_sems.at[3 * w + j], recv_sem=recv_sems.at[3 * w + j],
                    device_id=(x, y, c), device_id_type=MESH))
        return sends, recvs

    def start(ins, outs, sems):
        for cp in copies(ins, outs, sems)[0]:
            cp.start()

    def finish(ins, outs, sems):
        sends, recvs = copies(ins, outs, sems)
        for cp in recvs:
            cp.wait_recv()
        for cp in sends:
            cp.wait_send()

    inputs = [it[0] for it in items] + [items[w][1] for w in given]
    shapes = [jax.ShapeDtypeStruct((3,) + it[0].shape[1:], it[0].dtype) for it in items]
    aliases = {n + i: w for i, w in enumerate(given)}
    return _Carry(inputs, shapes, aliases, [sem, sem], start, finish)


def _sum_into_half(part, landed, chip, my_c, *, name):
    _, h, cols = part.shape
    tr = _row_tile(h, cols, 5)
    hb = h // tr

    def body(chip_ref, c_ref, p_ref, l_ref, o_ref):
        acc = p_ref[...].astype(F32)
        for j in range(3):
            acc = acc + l_ref[j].astype(F32)
        o_ref[...] = acc

    gs = pltpu.PrefetchScalarGridSpec(
        num_scalar_prefetch=2, grid=(hb,),
        in_specs=[pl.BlockSpec((None, tr, cols), lambda i, chip_ref, c_ref: (chip_ref[0], i, 0)),
                  pl.BlockSpec((3, tr, cols), lambda i, chip_ref, c_ref: (0, i, 0))],
        out_specs=pl.BlockSpec((tr, cols), lambda i, chip_ref, c_ref: (c_ref[0] * hb + i, 0)))
    return pl.pallas_call(
        body, grid_spec=gs, out_shape=jax.ShapeDtypeStruct((2 * h, cols), F32),
        compiler_params=_cparams(("parallel",)), name=name)(chip, my_c, part, landed)


def _join_carry(fulls):
    n = len(fulls)
    sem = pltpu.SemaphoreType.DMA((n,))

    def copies(outs, sems):
        send_sems, recv_sems = sems
        x, y, c = _mesh_pos()
        sends, recvs = [], []
        for w in range(n):
            h = fulls[w].shape[0] // 2
            mine = outs[w].at[pl.ds(c * h, h), :]
            theirs = outs[w].at[pl.ds((1 - c) * h, h), :]
            sends.append(pltpu.make_async_remote_copy(
                src_ref=mine, dst_ref=mine, send_sem=send_sems.at[w], recv_sem=recv_sems.at[w],
                device_id=(x, y, 1 - c), device_id_type=MESH))
            recvs.append(pltpu.make_async_remote_copy(
                src_ref=theirs, dst_ref=theirs, send_sem=send_sems.at[w], recv_sem=recv_sems.at[w],
                device_id=(x, y, c), device_id_type=MESH))
        return sends, recvs

    def start(ins, outs, sems):
        for cp in copies(outs, sems)[0]:
            cp.start()

    def finish(ins, outs, sems):
        sends, recvs = copies(outs, sems)
        for cp in recvs:
            cp.wait_recv()
        for cp in sends:
            cp.wait_send()

    shapes = [jax.ShapeDtypeStruct(f.shape, f.dtype) for f in fulls]
    return _Carry(fulls, shapes, {i: i for i in range(n)}, [sem, sem], start, finish)


class _NoComm:
    def __init__(self, big):
        self.big = big
        self.grads = {}

    def weight(self, name):
        return self.big[name]

    def mm_in(self, u, afters):
        return _mm(u, self.big["w_in"], mode="nn", out_dtype=BF16, name="mm_in")

    def mm_d_in(self, dproj):
        return _mm(dproj, self.big["w_in"], mode="nt", out_dtype=F32, name="mm_d_in")

    def carry(self, site):
        return None

    def done(self, site, carried):
        pass

    def grad(self, name, dw):
        self.grads[name] = dw

    def early_grads(self, early):
        self.early = early


def _gather_rows_carry(blk):
    m_per = blk.shape[0]
    sem = pltpu.SemaphoreType.DMA((7,))

    def copies(ins, outs, sems):
        send_sems, recv_sems, local_sem = sems
        x, y, c = _mesh_pos()
        me, sibling = (x, y, c), (x, y, 1 - c)
        chips = _other_chips(x, y)

        def rows(px, py, pc):
            return outs[0].at[pl.ds((4 * px + 2 * py + pc) * m_per, m_per), :]

        def copy(k, block, to, src=None):
            return pltpu.make_async_remote_copy(
                src_ref=rows(*block) if src is None else src, dst_ref=rows(*block),
                send_sem=send_sems.at[k], recv_sem=recv_sems.at[k], device_id=to, device_id_type=MESH)

        mine = pltpu.make_async_copy(ins[0], rows(*me), local_sem.at[0])
        first = [copy(0, me, sibling, src=ins[0])]
        first += [copy(1 + j, me, (*chip, c), src=ins[0]) for j, chip in enumerate(chips)]
        passed = [copy(4 + j, (*chip, c), sibling) for j, chip in enumerate(chips)]
        landed = [copy(1 + j, (*chip, c), me) for j, chip in enumerate(chips)]
        from_sibling = [copy(0, sibling, me)] + [copy(4 + j, (*chip, 1 - c), me) for j, chip in enumerate(chips)]
        return mine, first, passed, landed, from_sibling

    def start(ins, outs, sems):
        mine, first, _, _, _ = copies(ins, outs, sems)
        mine.start()
        for cp in first:
            cp.start()

    def finish(ins, outs, sems):
        mine, first, passed, landed, from_sibling = copies(ins, outs, sems)
        for arrived, onward in zip(landed, passed):
            arrived.wait_recv()
            onward.start()
        for arrived in from_sibling:
            arrived.wait_recv()
        for cp in first + passed:
            cp.wait_send()
        mine.wait()

    shape = jax.ShapeDtypeStruct((N_DEV * m_per, blk.shape[1]), blk.dtype)
    return _Carry([blk], [shape], {}, [sem, sem, pltpu.SemaphoreType.DMA((1,))], start, finish)


def _gather_fresh_carry(own, js):
    n = len(js)
    h = own.shape[0] // 2
    sem = pltpu.SemaphoreType.DMA((2 * n,))

    def copies(ins, outs, sems):
        send_sems, recv_sems = sems
        x, y, c = _mesh_pos()
        chips = _other_chips(x, y)
        out_ici, in_ici, out_d2d, in_d2d = [], [], [], []

        def copy(k, src, dst, to):
            return pltpu.make_async_remote_copy(
                src_ref=src, dst_ref=dst, send_sem=send_sems.at[k], recv_sem=recv_sems.at[k],
                device_id=to, device_id_type=MESH)

        for jj, j in enumerate(js):
            mine = ins[0].at[pl.ds(c * h, h), :]
            land = outs[0].at[jj, pl.ds(c * h, h), :]
            other = outs[0].at[jj, pl.ds((1 - c) * h, h), :]
            out_ici.append(copy(2 * jj, mine, land, (*chips[j], c)))
            in_ici.append(copy(2 * jj, land, land, (x, y, c)))
            out_d2d.append(copy(2 * jj + 1, land, land, (x, y, 1 - c)))
            in_d2d.append(copy(2 * jj + 1, other, other, (x, y, c)))
        return out_ici, in_ici, out_d2d, in_d2d

    def start(ins, outs, sems):
        for cp in copies(ins, outs, sems)[0]:
            cp.start()

    def finish(ins, outs, sems):
        out_ici, in_ici, out_d2d, in_d2d = copies(ins, outs, sems)
        for arrived, onward in zip(in_ici, out_d2d):
            arrived.wait_recv()
            onward.start()
        for arrived in in_d2d:
            arrived.wait_recv()
        for cp in out_ici + out_d2d:
            cp.wait_send()

    return _Carry([own], [jax.ShapeDtypeStruct((n,) + own.shape, own.dtype)], {}, [sem, sem], start, finish)


def _w_in_copies(own_ref, land_ref, send_sems, recv_sems):
    x, y, c = _mesh_pos()
    h = own_ref.shape[0] // 2
    return [pltpu.make_async_remote_copy(
        src_ref=own_ref.at[pl.ds(c * h, h), :], dst_ref=land_ref.at[j, pl.ds(c * h, h), :],
        send_sem=send_sems[j], recv_sem=recv_sems[j], device_id=(*chip, c), device_id_type=MESH)
        for j, chip in enumerate(_other_chips(x, y))]


def _w_in_send(own, after):
    hbm = pl.BlockSpec(memory_space=pltpu.HBM)
    sem = pl.BlockSpec(memory_space=pltpu.SEMAPHORE)
    land_shape = (3,) + own.shape

    def body(own_ref, land_ref, after_ref, s0, s1, s2, r0, r1, r2, own_thru, land_thru, token):
        for cp in _w_in_copies(own_ref, land_ref, (s0, s1, s2), (r0, r1, r2)):
            cp.start()
        token[...] = jnp.zeros_like(token)

    outs = pl.pallas_call(
        body, name="w_in_send",
        out_shape=(pltpu.SemaphoreType.DMA(()),) * 6 + (
            pltpu.HBM(own.shape, own.dtype), pltpu.HBM(land_shape, own.dtype), jax.ShapeDtypeStruct((8, LANES), F32)),
        in_specs=(hbm, hbm, pl.BlockSpec(memory_space=pl.ANY)),
        out_specs=(sem,) * 6 + (hbm, hbm, pl.BlockSpec(memory_space=pltpu.VMEM)),
        input_output_aliases={0: 6, 1: 7},
        compiler_params=pltpu.CompilerParams(has_side_effects=pltpu.SideEffectType.DATAFLOW_SIDE_EFFECTING),
    )(pltpu.with_memory_space_constraint(own, pltpu.HBM),
      pltpu.with_memory_space_constraint(lax.empty(land_shape, own.dtype), pltpu.HBM), after)
    return outs[:6], outs[6], outs[7], outs[8]


def _w_in_wait(sems, own, land, afters):
    hbm = pl.BlockSpec(memory_space=pltpu.HBM)
    sem = pl.BlockSpec(memory_space=pltpu.SEMAPHORE)
    n_after = len(afters)

    def body(own_ref, land_ref, s0, s1, s2, r0, r1, r2, *rest):
        for cp in _w_in_copies(own_ref, land_ref, (s0, s1, s2), (r0, r1, r2)):
            cp.wait_send()
            cp.wait_recv()

    return pl.pallas_call(
        body, name="w_in_wait", out_shape=(pltpu.HBM(own.shape, own.dtype), pltpu.HBM(land.shape, land.dtype)),
        in_specs=(hbm, hbm) + (sem,) * 6 + (pl.BlockSpec(memory_space=pl.ANY),) * n_after, out_specs=(hbm, hbm),
        input_output_aliases={0: 0, 1: 1},
        compiler_params=pltpu.CompilerParams(has_side_effects=pltpu.SideEffectType.DATAFLOW_SIDE_EFFECTING),
    )(own, land, *sems, *afters)


def _forward_carry(land):
    n = land.shape[0]
    h = land.shape[1] // 2
    sem = pltpu.SemaphoreType.DMA((n,))

    def copies(outs, sems):
        send_sems, recv_sems = sems
        x, y, c = _mesh_pos()
        sends, recvs = [], []
        for j in range(n):
            mine = outs[0].at[j, pl.ds(c * h, h), :]
            other = outs[0].at[j, pl.ds((1 - c) * h, h), :]
            sends.append(pltpu.make_async_remote_copy(
                src_ref=mine, dst_ref=mine, send_sem=send_sems.at[j], recv_sem=recv_sems.at[j],
                device_id=(x, y, 1 - c), device_id_type=MESH))
            recvs.append(pltpu.make_async_remote_copy(
                src_ref=other, dst_ref=other, send_sem=send_sems.at[j], recv_sem=recv_sems.at[j],
                device_id=(x, y, c), device_id_type=MESH))
        return sends, recvs

    def start(ins, outs, sems):
        for cp in copies(outs, sems)[0]:
            cp.start()

    def finish(ins, outs, sems):
        sends, recvs = copies(outs, sems)
        for cp in recvs:
            cp.wait_recv()
        for cp in sends:
            cp.wait_send()

    return _Carry([land], [jax.ShapeDtypeStruct(land.shape, land.dtype)], {0: 0}, [sem, sem], start, finish)


def _swap_carry(dws):
    n = len(dws)
    sem = pltpu.SemaphoreType.DMA((n,))

    def copies(ins, outs, sems):
        send_sems, recv_sems = sems
        x, y, c = _mesh_pos()
        cps = []
        for w in range(n):
            h = dws[w].shape[1] // 2
            cps.append(pltpu.make_async_remote_copy(
                src_ref=ins[w].at[:, pl.ds((1 - c) * h, h), :], dst_ref=outs[w],
                send_sem=send_sems.at[w], recv_sem=recv_sems.at[w],
                device_id=(x, y, 1 - c), device_id_type=MESH))
        return cps

    def start(ins, outs, sems):
        for cp in copies(ins, outs, sems):
            cp.start()

    def finish(ins, outs, sems):
        for cp in copies(ins, outs, sems):
            cp.wait()

    shapes = [jax.ShapeDtypeStruct((s.shape[0], s.shape[1] // 2, s.shape[2]), s.dtype) for s in dws]
    return _Carry(dws, shapes, {}, [sem, sem], start, finish)


def _merge_carries(carries):
    if len(carries) == 1:
        return carries[0]
    inputs, out_shapes, sem_shapes, aliases, spans = [], [], [], {}, []
    for cy in carries:
        i0, o0, s0 = len(inputs), len(out_shapes), len(sem_shapes)
        aliases.update({i0 + i: o0 + o for i, o in cy.aliases.items()})
        inputs += cy.inputs
        out_shapes += cy.out_shapes
        sem_shapes += cy.sem_shapes
        spans.append((slice(i0, len(inputs)), slice(o0, len(out_shapes)), slice(s0, len(sem_shapes))))

    def start(ins, outs, sems):
        for cy, (si, so, ss) in zip(carries, spans):
            cy.start(ins[si], outs[so], sems[ss])

    def finish(ins, outs, sems):
        for cy, (si, so, ss) in zip(carries, spans):
            cy.finish(ins[si], outs[so], sems[ss])

    return _Carry(inputs, out_shapes, aliases, sem_shapes, start, finish)


ALL_CHIPS = (0, 1, 2)


class _MeshComm:
    GATHER_AT = {
        "mm_in_rest": [("w_conv_out", ALL_CHIPS, 0, 1, 1), ("w_glu_a", ALL_CHIPS, 0, 1, 1),
                       ("w_glu_b", ALL_CHIPS, 0, 1, 1)],
        "conv_fwd": [("w_out", ALL_CHIPS, 0, 1, 1), ("w_ff1", ALL_CHIPS, 0, 1, 8)],
        "s5_fwd": [("w_ff1", ALL_CHIPS, 1, 6, 8)],
        "mm_glu_a": [("w_ff1", ALL_CHIPS, 6, 7, 8)],
        "mm_glu_b": [("w_ff1", ALL_CHIPS, 7, 8, 8)],
        "mm_out": [("w_ff2", ALL_CHIPS, 0, 2, 8)],
        "mm_ff1": [("w_ff2", ALL_CHIPS, 2, 8, 8)],
    }
    SWAP_AT = {
        "mm_d_ff2": ["w_ff2"],
        "mm_d_ff1": ["w_ff1"],
        "conv_bwd": ["w_out", "w_glu_a", "w_glu_b", "w_conv_out"],
    }
    EXCHANGE_AT = {
        "mm_dw_ff1": [("w_ff2", 0, 6, 8)],
        "mm_d_ff1": [("w_ff2", 6, 8, 8)],
        "norm2_bwd": [("w_ff1", 0, 3, 8)],
        "mm_dw_out": [("w_ff1", 3, 5, 8)],
        "mm_d_out": [("w_ff1", 5, 7, 8)],
        "merge_bwd": [("w_ff1", 7, 8, 8)],
        "s5_bwd": [("w_out", 0, 1, 1), ("w_glu_a", 0, 1, 1), ("w_glu_b", 0, 1, 1), ("w_conv_out", 0, 1, 1)],
        "mm_d_in": [("w_in", 0, 1, 1)],
    }
    EARLY_AT = "mm_dw_in"

    def __init__(self, shards, pos, chip, my_c):
        self.pos = pos
        self.chip = chip
        self.my_c = my_c
        self.shards = shards
        self.w_in_own = _cast_bf16(shards["w_in"], name="cast_w_in")
        self.raw = {}
        self.parts = {}
        self.landing = {}
        self.halves = {}
        self.pending = {}
        self.last_site = {}
        for site, items in self.EXCHANGE_AT.items():
            for it in items:
                self.last_site[it[0]] = site

    def weight(self, name):
        g = self.bufs[name]
        return g.reshape(g.shape[0] * g.shape[1], g.shape[2]) if name in ROW_SHARDED else g

    def _slot_ids(self):
        x, y, _ = self.pos
        ids = [2 * x + y] + [2 * cx + cy for cx, cy in _other_chips(x, y)]
        return jnp.stack(ids).astype(jnp.int32)

    def start_w_in(self, after):
        *self.w_in_flight, token = _w_in_send(self.w_in_own, after)
        self.bufs = {n: _cast_into_slot(s, self.chip, token, name="cast_" + n)
                     for n, s in self.shards.items() if n != "w_in"}
        return token

    def mm_in(self, u, afters):
        ids = self._slot_ids()
        sems, own, land = self.w_in_flight
        proj = _mm_slots(u, own[None], ids[0:1], None, name="mm_in_own")
        own, land = _w_in_wait(sems, own, land, [proj, self.bufs["w_ff2"]] + list(afters))
        land, = _run_carry(_forward_carry(land), name="forward_w_in")
        proj, carried = _mm_slots(u, land, ids[1:4], proj, name="mm_in_rest", carry=self.carry("mm_in_rest"))
        self.done("mm_in_rest", carried)
        self.w_in_rel = jnp.concatenate([own[None], land], axis=0)
        return proj

    def mm_d_in(self, dproj):
        raw = self.raw.pop("w_in")
        landed, = _run_carry(_swap_carry([raw]), name="swap_halves_w_in")
        self.parts["w_in"] = _add_half(raw, landed, self.my_c, name="add_half_w_in")
        carry = self.carry("mm_d_in")
        du, carried = _mm(dproj, self.w_in_rel, mode="nt", out_dtype=F32, name="mm_d_in", carry=carry,
                          a_slots=self._slot_ids())
        self.done("mm_d_in", carried)
        return du

    def early_grads(self, early):
        self.early = early

    def carry(self, site):
        jobs = []
        if site == self.EARLY_AT:
            flat, self.early_offs = _pack(list(self.early.values()))
            jobs.append(("early", None, _gather_rows_carry(flat.reshape(-1, PACK_COLS))))
        if site in self.GATHER_AT:
            items = self.GATHER_AT[site]
            jobs.append(("gather", items, _gather_carry([(self.bufs[it[0]],) + tuple(it[1:]) for it in items])))
        if site in self.EXCHANGE_AT:
            items = self.EXCHANGE_AT[site]
            jobs.append(("exchange", items, _exchange_carry(
                [(self.parts[it[0]], self.landing.get(it[0])) + tuple(it[1:]) for it in items])))
        if site in self.SWAP_AT:
            names = self.SWAP_AT[site]
            jobs.append(("swap", names, _swap_carry([self.raw[n] for n in names])))
        if not jobs:
            return None
        self.pending[site] = jobs
        return _merge_carries([job[2] for job in jobs])

    def done(self, site, carried):
        pos = 0
        for kind, items, carry in self.pending.pop(site):
            outs = carried[pos:pos + len(carry.out_shapes)]
            pos += len(carry.out_shapes)
            if kind == "early":
                self.early_all = outs[0]
            elif kind == "gather":
                self.bufs.update(zip([it[0] for it in items], outs))
            elif kind == "swap":
                for n, landed in zip(items, outs):
                    self.parts[n] = _add_half(self.raw.pop(n), landed, self.my_c, name="add_half_" + n)
            else:
                for it, landed in zip(items, outs):
                    n = it[0]
                    self.landing[n] = landed
                    if self.last_site[n] == site:
                        self.halves[n] = _sum_into_half(self.parts.pop(n), self.landing.pop(n), self.chip,
                                                        self.my_c, name="sum_chips_" + n)

    def grad(self, name, dw):
        if name in ROW_SHARDED:
            dw = dw.reshape(N_CHIPS, dw.shape[0] // N_CHIPS, dw.shape[1])
        self.raw[name] = dw

    def join(self, names, *, name):
        return dict(zip(names, _run_carry(_join_carry([self.halves.pop(n) for n in names]), name=name)))


def _local_step(x, target, mod, small, comm):
    rows, d = x.shape
    cw = d // 2
    shift1, scale1, gate1, shift2, scale2, gate2 = mod
    _, _, bbr, bbi = small["s5_disc"]
    b_in, c_out, b_out, c_in, mults = _s5_operands(*small["s5_loglam"], bbr, bbi, small["c_re"], small["c_im"])
    wt = comm.weight

    def riding(site, fn, *args, **kwargs):
        carry = comm.carry(site)
        if carry is None:
            return fn(*args, **kwargs)
        out, carried = fn(*args, carry=carry, **kwargs)
        comm.done(site, carried)
        return out

    u = _norm_mod(x, small["norm1_g"], scale1, shift1, name="norm1_fwd")
    proj = comm.mm_in(u, [b_in[0]])
    sl, cv = riding("conv_fwd", _conv_fwd, proj, small["w_dw"], small["b_dw"], small["ln_g"], small["ln_b"], cw=cw)
    y_conv = _mm(sl, wt("w_conv_out"), mode="nn", out_dtype=BF16, name="mm_conv_out")
    yg, st_re, st_im = riding("s5_fwd", _s5_fwd, proj, small["d_skip"], b_in, c_out, mults, col0=2 * cw // LANES)
    ya = riding("mm_glu_a", _mm, yg, wt("w_glu_a"), mode="nn", out_dtype=BF16, name="mm_glu_a")
    yb = riding("mm_glu_b", _mm, yg, wt("w_glu_b"), mode="nn", out_dtype=BF16, name="mm_glu_b")
    merged = _merge_fwd(proj, y_conv, ya, yb, cw=cw)
    mo = riding("mm_out", _mm, merged, wt("w_out"), mode="nn", out_dtype=BF16, name="mm_out")
    h1, z = _res_norm(x, mo, gate1, small["norm2_g"], scale2, shift2)
    f1 = riding("mm_ff1", _mm, z, wt("w_ff1"), mode="nn", out_dtype=BF16, name="mm_ff1")
    ff = _mm(f1, wt("w_ff2"), mode="nn", out_dtype=BF16, name="mm_ff2", a_fn=_relu2_bf16)
    dh2, dff, loss, d_final_g, d_gate2 = _final_fwd_bwd(h1, ff, gate2, small["final_g"], target)

    comm.grad("w_ff2", _mm(f1, dff, mode="tn", out_dtype=BF16, name="mm_dw_ff2", a_fn=_relu2_bf16))
    df1 = riding("mm_d_ff2", _mm, dff, wt("w_ff2"), mode="nt", out_dtype=BF16, name="mm_d_ff2", extra=f1,
                 epi=lambda acc, f: acc * (2.0 * jnp.maximum(f.astype(F32), 0.0)))
    comm.grad("w_ff1", riding("mm_dw_ff1", _mm, z, df1, mode="tn", out_dtype=BF16, name="mm_dw_ff1",
                              out_gathered=True))
    dz = riding("mm_d_ff1", _mm, df1, wt("w_ff1"), mode="nt", out_dtype=F32, name="mm_d_ff1")
    dh1, d_shift2, d_scale2, d_norm2_g, dmo, d_gate1 = riding(
        "norm2_bwd", _norm_mod_bwd, dz, h1, dh2, small["norm2_g"], scale2, gate1, mo, name="norm2_bwd")
    comm.grad("w_out", riding("mm_dw_out", _mm, merged, dmo, mode="tn", out_dtype=BF16, name="mm_dw_out"))
    dmerged = riding("mm_d_out", _mm, dmo, wt("w_out"), mode="nt", out_dtype=BF16, name="mm_d_out")
    dgc, dgs, dy_conv, dya, dyb = riding("merge_bwd", _merge_bwd, dmerged, proj, y_conv, ya, yb, cw=cw)
    comm.grad("w_glu_a", _mm(yg, dya, mode="tn", out_dtype=BF16, name="mm_dw_glu_a", out_gathered=True))
    comm.grad("w_glu_b", _mm(yg, dyb, mode="tn", out_dtype=BF16, name="mm_dw_glu_b", out_gathered=True))
    dyg_a = _mm(dya, wt("w_glu_a"), mode="nt", out_dtype=F32, name="mm_d_glu_a")
    dyg = _mm(dyb, wt("w_glu_b"), mode="nt", out_dtype=F32, name="mm_d_glu_b", extra=dyg_a,
              epi=lambda acc, e: acc + e)
    comm.grad("w_conv_out", _mm(sl, dy_conv, mode="tn", out_dtype=BF16, name="mm_dw_conv_out", out_gathered=True))
    dsl = _mm(dy_conv, wt("w_conv_out"), mode="nt", out_dtype=F32, name="mm_d_conv_out")
    dcv, d_ln_g, d_ln_b = _ln_bwd(dsl, cv, small["ln_g"], small["ln_b"])
    dvconv, d_w_dw, d_b_dw = riding("conv_bwd", _conv_bwd, dcv, proj, small["w_dw"], cw=cw)
    dvssm, d_d_skip, dbr, dbi, dcr, dci, dlr, dli = riding(
        "s5_bwd", _s5_bwd, proj, dyg, small["d_skip"], (st_re, st_im), c_out, b_out, c_in, mults,
        col0=2 * cw // LANES)
    sw = lambda m: jnp.swapaxes(m, 1, 2)
    early = {
        "dmod_tail": jnp.concatenate([d_gate1, d_shift2, d_scale2, d_gate2], axis=1), "loss": loss[:, 0:1],
        "w_dw": d_w_dw, "b_dw": d_b_dw, "ln_g": d_ln_g, "ln_b": d_ln_b,
        "lam_re": dlr.reshape(-1, SSM_STATE), "lam_im": dli.reshape(-1, SSM_STATE),
        "bb_re": sw(_block_diag_extract(dbr, SSM_GROUP, SSM_STATE)),
        "bb_im": sw(_block_diag_extract(dbi, SSM_GROUP, SSM_STATE)),
        "c_re": sw(_block_diag_extract(dcr, SSM_STATE, SSM_GROUP)),
        "c_im": sw(_block_diag_extract(dci, SSM_STATE, SSM_GROUP)),
        "d_skip": d_d_skip, "norm2_g": d_norm2_g, "final_g": d_final_g,
    }
    comm.early_grads(early)
    dproj = jnp.concatenate([dvconv, dvssm, dgc, dgs], axis=1)
    comm.grad("w_in", riding("mm_dw_in", _mm, u, dproj, mode="tn", out_dtype=BF16, name="mm_dw_in",
                             out_gathered=True))
    du = comm.mm_d_in(dproj)
    grad_x, d_shift1, d_scale1, d_norm1_g = riding(
        "norm1_bwd", _norm_mod_bwd, du, x, dh1, small["norm1_g"], scale1, None, None, name="norm1_bwd")
    late ={"dmod_head": jnp.concatenate([d_shift1, d_scale1], axis=1), "norm1_g": d_norm1_g}
    return grad_x, early, late


WEIGHT_NAMES = ["w_ada", "b_ada", "norm1_g", "w_in", "w_dw", "b_dw", "ln_g", "ln_b", "w_conv_out", "a_re", "a_im",
                "log_dt", "b_re", "b_im", "c_re", "c_im", "d_skip", "w_glu_a", "w_glu_b", "w_out", "norm2_g",
                "w_ff1", "w_ff2", "final_g"]
BIG_NAMES = ["w_in", "w_conv_out", "w_glu_a", "w_glu_b", "w_out", "w_ff1", "w_ff2"]
ROW_SHARDED = ("w_out", "w_ff2")
PACK_COLS = 1024
PACK_TILE = SUBLANES * PACK_COLS


def _pack(arrays):
    flats = [a.reshape(-1) for a in arrays]
    offs = []
    total = 0
    for f in flats:
        offs.append(total)
        total += f.shape[0]
    pad = (-total) % PACK_TILE
    if pad:
        flats.append(jnp.zeros((pad,), F32))
    return jnp.concatenate(flats), offs


def _unpack(flat, offs, like):
    return [flat[o:o + a.size].reshape(a.shape) for o, a in zip(offs, like)]


def _gather_w_dw(w_shard):
    k, n = w_shard.shape
    padded = jnp.pad(w_shard, ((0, HALO - k), (0, 0)))
    allw = _gather_small(padded, name="gather_w_dw").reshape(N_CHIPS, 2, HALO, n)[:, 0, :k]
    return jnp.moveaxis(allw, 0, 1).reshape(k, N_CHIPS * n)


def kernel(x, c, w_ada, b_ada, norm1_g, w_in, w_dw, b_dw, ln_g, ln_b, w_conv_out, a_re, a_im, log_dt, b_re, b_im, c_re, c_im, d_skip, w_glu_a, w_glu_b, w_out, norm2_g, w_ff1, w_ff2, final_g, loss_target, m_w_ada, m_b_ada, m_norm1_g, m_w_in, m_w_dw, m_b_dw, m_ln_g, m_ln_b, m_w_conv_out, m_a_re, m_a_im, m_log_dt, m_b_re, m_b_im, m_c_re, m_c_im, m_d_skip, m_w_glu_a, m_w_glu_b, m_w_out, m_norm2_g, m_w_ff1, m_w_ff2, m_final_g, v_w_ada, v_b_ada, v_norm1_g, v_w_in, v_w_dw, v_b_dw, v_ln_g, v_ln_b, v_w_conv_out, v_a_re, v_a_im, v_log_dt, v_b_re, v_b_im, v_c_re, v_c_im, v_d_skip, v_w_glu_a, v_w_glu_b, v_w_out, v_norm2_g, v_w_ff1, v_w_ff2, v_final_g):
    given = dict(locals())
    w = {n: given[n] for n in WEIGHT_NAMES}
    m = {n: given["m_" + n] for n in WEIGHT_NAMES}
    v = {n: given["v_" + n] for n in WEIGHT_NAMES}
    d = x.shape[2]
    xi, yi, ci = _mesh_pos()
    chip = 2 * xi + yi
    dev = 4 * xi + 2 * yi + ci
    my_c = jnp.reshape(ci, (1,)).astype(jnp.int32)
    chip_arr = jnp.reshape(chip, (1,)).astype(jnp.int32)

    comm = _MeshComm({n: w[n][0] for n in BIG_NAMES}, (xi, yi, ci), chip_arr, my_c)

    ndw = w_dw.shape[2]
    assert d // SUBLANES == ndw
    first = jnp.concatenate([c.reshape(SUBLANES, ndw), jnp.pad(w_dw[0], ((0, HALO - CONV_KERNEL), (0, 0)))])
    first_all = _gather_small(first, name="gather_c_w_dw").reshape(N_DEV, SUBLANES + HALO, ndw)
    c_all = first_all[:, :SUBLANES].reshape(N_DEV, d)
    taps = first_all.reshape(N_CHIPS, 2, SUBLANES + HALO, ndw)[:, 0, SUBLANES:SUBLANES + CONV_KERNEL]
    w_dw_full = jnp.moveaxis(taps, 0, 1).reshape(CONV_KERNEL, N_CHIPS * ndw)

    nmod = w_ada.shape[2]
    b_cols = lax.dynamic_slice(b_ada, (0, chip * nmod), (1, nmod))
    mod_part = _ada_fwd(c_all, w_ada[0], b_cols)
    mod_all = _gather_small(mod_part, name="gather_mod").reshape(N_CHIPS, 2, N_DEV, nmod)[:, 0]
    mod_full = jnp.moveaxis(mod_all, 0, 1).reshape(N_DEV, N_CHIPS * nmod)
    mod_row = lax.dynamic_slice(mod_full, (dev, 0), (1, N_CHIPS * nmod))
    mod = [mod_row[:, i * d:(i + 1) * d] for i in range(6)]

    token = comm.start_w_in(mod_row)
    log_dt_0 = log_dt[0] + token[0, 0]

    disc_in = (a_re[0], a_im[0], log_dt_0, b_re[0], b_im[0])
    disc, disc_vjp = jax.vjp(_s5_discretise, *disc_in)
    dt = jnp.exp(log_dt_0)[:, None]
    small = {"norm1_g": norm1_g, "w_dw": w_dw_full, "b_dw": b_dw, "ln_g": ln_g, "ln_b": ln_b,
             "c_re": c_re[0], "c_im": c_im[0], "d_skip": d_skip, "norm2_g": norm2_g,
             "final_g": final_g[None, :], "s5_disc": disc, "s5_loglam": (a_re[0] * dt, a_im[0] * dt)}

    grad_x, early, late = _local_step(x[0], loss_target[0], mod, small, comm)
    grads = {}

    early_all = comm.early_all.reshape(N_DEV, -1, PACK_COLS)
    early_sum = _sum_leading(early_all, name="sum_small_grads").reshape(-1)
    summed = dict(zip(early, _unpack(early_sum, comm.early_offs, list(early.values()))))
    flat, late_offs = _pack(list(late.values()))
    late_all = _gather_small(flat.reshape(-1, PACK_COLS), name="gather_late_grads").reshape(N_DEV, -1, PACK_COLS)
    late_sum = _sum_leading(late_all, name="sum_late_grads").reshape(-1)
    summed.update(zip(late, _unpack(late_sum, late_offs, list(late.values()))))
    head = late_all[:, :2 * d // PACK_COLS].reshape(N_DEV, 2 * d)
    tail = early_all[:, :4 * d // PACK_COLS].reshape(N_DEV, 4 * d)
    dmod_all = jnp.concatenate([head, tail], axis=1)

    grads["w_ada"] = _ada_bwd(c_all, lax.dynamic_slice(dmod_all, (0, chip * nmod), (N_DEV, nmod)))
    grads["b_ada"] = _sum_leading(dmod_all.reshape(N_DEV, SUBLANES, 6 * d // SUBLANES),
                                  name="sum_b_ada").reshape(1, 6 * d)
    da_re, da_im, dlog_dt, db_re, db_im = disc_vjp(
        (summed["lam_re"], summed["lam_im"], summed["bb_re"], summed["bb_im"]))
    grads.update({
        "norm1_g": summed["norm1_g"], "w_dw": lax.dynamic_slice(summed["w_dw"], (0, chip * ndw), (CONV_KERNEL, ndw)),
        "b_dw": summed["b_dw"], "ln_g": summed["ln_g"], "ln_b": summed["ln_b"],
        "a_re": da_re, "a_im": da_im, "log_dt": dlog_dt, "b_re": db_re, "b_im": db_im,
        "c_re": summed["c_re"], "c_im": summed["c_im"], "d_skip": summed["d_skip"],
        "norm2_g": summed["norm2_g"], "final_g": summed["final_g"],
    })

    delta, new_m, new_v = {}, {}, {}
    grads.update(comm.join(BIG_NAMES, name="join_halves"))
    for n in ["w_ada"] + BIG_NAMES:
        shp = w[n].shape
        two_d = lambda a: a.reshape(shp[1], shp[2])
        res = _adamw(two_d(w[n]), two_d(grads[n]), two_d(m[n]), two_d(v[n]), name="adamw_" + n)
        delta[n], new_m[n], new_v[n] = [r.reshape(shp) for r in res]
    grads = {n: grads[n].reshape(w[n].shape) for n in WEIGHT_NAMES}
    rest = [n for n in WEIGHT_NAMES if n not in delta]
    packs = []
    for src in (w, grads, m, v):
        flat, offs = _pack([src[n] for n in rest])
        packs.append(flat.reshape(-1, 1024))
    outs = _adamw(*packs, name="adamw_small")
    for dst, o in zip((delta, new_m, new_v), outs):
        for n, a in zip(rest, _unpack(o.reshape(-1), offs, [w[k] for k in rest])):
            dst[n] = a

    return (summed["loss"].reshape(()), grad_x[None], *[grads[n] for n in WEIGHT_NAMES],
            *[delta[n] for n in WEIGHT_NAMES], *[new_m[n] for n in WEIGHT_NAMES],
            *[new_v[n] for n in WEIGHT_NAMES])
```

```python
import functools
import math

import jax
import jax.numpy as jnp
from jax import lax
from jax.experimental import pallas as pl
from jax.experimental.pallas import tpu as pltpu

F32 = jnp.float32
BF16 = jnp.bfloat16
EPS = 1e-6
CONV_KERNEL = 31
SSM_GROUP = 16
SSM_STATE = 64
ADAM_LR = 0.001
ADAM_B1 = 0.9
ADAM_B2 = 0.999
ADAM_EPS = 1e-08
ADAM_WD = 0.01
ADAM_STEP = 10

N_CHIPS = 4
N_DEV = 8
VMEM_LIMIT_BYTES = 56 * 1024 * 1024
LANES = 128
SUBLANES = 8
HALO = 32
GROUPS_PER_BLOCK = LANES // SSM_GROUP
STATE_LANES = GROUPS_PER_BLOCK * SSM_STATE
MESH = pl.DeviceIdType.MESH


def _cparams(sem):
    return pltpu.CompilerParams(dimension_semantics=sem, vmem_limit_bytes=VMEM_LIMIT_BYTES)


def _pick(n, pref, mult=LANES):
    if n <= pref:
        return n
    best = None
    for d in range(mult, pref + 1, mult):
        if n % d == 0:
            best = d
    assert best is not None, (n, pref)
    return best


def _sigmoid(v):
    return 1.0 / (1.0 + jnp.exp(-v))


def _gelu_parts(v):
    k0 = math.sqrt(2.0 / math.pi)
    inner = k0 * (v + 0.044715 * v * v * v)
    t = jnp.tanh(inner)
    return k0, t


def _gelu(v):
    _, t = _gelu_parts(v)
    return 0.5 * v * (1.0 + t)


def _gelu_grad(v):
    k0, t = _gelu_parts(v)
    return 0.5 * (1.0 + t) + 0.5 * v * (1.0 - t * t) * k0 * (1.0 + 3.0 * 0.044715 * v * v)


def _relu2_bf16(a):
    t = jnp.maximum(a.astype(F32), 0.0)
    return (t * t).astype(BF16)


class _Carry:
    def __init__(self, inputs, out_shapes, aliases, sem_shapes, start, finish):
        self.inputs = list(inputs)
        self.out_shapes = list(out_shapes)
        self.aliases = dict(aliases)
        self.sem_shapes = list(sem_shapes)
        self.start = start
        self.finish = finish


def _call(body, *, grid, in_specs, out_specs, out_shape, scratch_shapes, semantics, name, args, carry=None,
          prefetch=(), aliases=None):
    n_in, n_out, n_scr, n_pf = len(in_specs), len(out_specs), len(scratch_shapes), len(prefetch)
    own_aliases = {n_pf + i: o for i, o in (aliases or {}).items()}
    if carry is None:
        gs = pltpu.PrefetchScalarGridSpec(
            num_scalar_prefetch=n_pf, grid=grid, in_specs=in_specs, out_specs=out_specs,
            scratch_shapes=scratch_shapes)
        outs = pl.pallas_call(
            body, grid_spec=gs, out_shape=out_shape, input_output_aliases=own_aliases,
            compiler_params=_cparams(semantics), name=name)(*prefetch, *args)
        return list(outs), []
    ci, co = len(carry.inputs), len(carry.out_shapes)

    def wrapped(*refs):
        pf, refs = refs[:n_pf], refs[n_pf:]
        ins, cins = refs[:n_in], refs[n_in:n_in + ci]
        p = n_in + ci
        outs, couts = refs[p:p + n_out], refs[p + n_out:p + n_out + co]
        p += n_out + co
        scr, csems = refs[p:p + n_scr], refs[p + n_scr:]
        first = pl.program_id(0) == 0
        last = pl.program_id(0) == grid[0] - 1
        for ax in range(1, len(grid)):
            first = jnp.logical_and(first, pl.program_id(ax) == 0)
            last = jnp.logical_and(last, pl.program_id(ax) == grid[ax] - 1)

        @pl.when(first)
        def _():
            carry.start(cins, couts, csems)

        body(*pf, *ins, *outs, *scr)

        @pl.when(last)
        def _():
            carry.finish(cins, couts, csems)

    any_spec = pl.BlockSpec(memory_space=pl.ANY)
    gs = pltpu.PrefetchScalarGridSpec(
        num_scalar_prefetch=n_pf, grid=grid, in_specs=list(in_specs) + [any_spec] * ci,
        out_specs=list(out_specs) + [any_spec] * co, scratch_shapes=list(scratch_shapes) + carry.sem_shapes)
    all_aliases = dict(own_aliases)
    all_aliases.update({n_pf + n_in + i: n_out + o for i, o in carry.aliases.items()})
    outs = pl.pallas_call(
        wrapped, grid_spec=gs, out_shape=list(out_shape) + carry.out_shapes, input_output_aliases=all_aliases,
        compiler_params=_cparams(("arbitrary",) * len(grid)), name=name)(*prefetch, *args, *carry.inputs)
    return list(outs[:n_out]), list(outs[n_out:])


def _run_carry(carry, *, name):
    ci = len(carry.inputs)

    def body(*refs):
        cins, couts, csems = refs[:ci], refs[ci:ci + len(carry.out_shapes)], refs[ci + len(carry.out_shapes):]
        carry.start(cins, couts, csems)
        carry.finish(cins, couts, csems)

    any_spec = pl.BlockSpec(memory_space=pl.ANY)
    outs = pl.pallas_call(
        body, in_specs=[any_spec] * ci, out_specs=[any_spec] * len(carry.out_shapes), out_shape=carry.out_shapes,
        scratch_shapes=carry.sem_shapes, input_output_aliases=carry.aliases, name=name)(*carry.inputs)
    return list(outs)


def _mm(a, b, *, mode, out_dtype, name, out_gathered=False, a_fn=None, epi=None, extra=None,
        bm_pref=1024, bn_pref=1024, bk_pref=2048, carry=None, a_slots=None):
    gathered = (b.ndim == 3)
    if mode == "nn":
        m, kdim = a.shape
        ns = b.shape[-1]
        n = ns * (N_CHIPS if gathered else 1)
        bm, bn, bk = _pick(m, bm_pref), _pick(ns, bn_pref), _pick(kdim, bk_pref)
        npb = ns // bn
        grid = (m // bm, n // bn, kdim // bk)
        a_spec = pl.BlockSpec((bm, bk), lambda i, j, k: (i, k))
        if gathered:
            b_spec = pl.BlockSpec((None, bk, bn), lambda i, j, k: (j // npb, k, j % npb))
        else:
            b_spec = pl.BlockSpec((bk, bn), lambda i, j, k: (k, j))
        o_spec = pl.BlockSpec((bm, bn), lambda i, j, k: (i, j))
        e_spec = pl.BlockSpec((bm, bn), lambda i, j, k: (i, j))
        out_shape = (m, n)
        acc_shape = (bm, bn)
        dims = (((1,), (0,)), ((), ()))
    elif mode == "nt":
        m = a.shape[0]
        kdim, ns = b.shape[-2], b.shape[-1]
        n = ns * (N_CHIPS if gathered else 1)
        assert a.shape[1] == n
        bm, bko, bnr = _pick(m, bm_pref), _pick(kdim, bn_pref), _pick(ns, bk_pref)
        npb = ns // bnr
        grid = (m // bm, kdim // bko, n // bnr)
        a_spec = pl.BlockSpec((bm, bnr), lambda i, j, k: (i, k))
        if gathered:
            b_spec = pl.BlockSpec((None, bko, bnr), lambda i, j, k: (k // npb, j, k % npb))
        else:
            b_spec = pl.BlockSpec((bko, bnr), lambda i, j, k: (j, k))
        o_spec = pl.BlockSpec((bm, bko), lambda i, j, k: (i, j))
        e_spec = pl.BlockSpec((bm, bko), lambda i, j, k: (i, j))
        if a_slots is not None:
            assert gathered and extra is None
            a_spec = pl.BlockSpec((bm, bnr), lambda i, j, k, s_ref: (i, s_ref[k // npb] * npb + k % npb))
            b_spec = pl.BlockSpec((None, bko, bnr), lambda i, j, k, s_ref: (k // npb, j, k % npb))
            o_spec = pl.BlockSpec((bm, bko), lambda i, j, k, s_ref: (i, j))
        out_shape = (m, kdim)
        acc_shape = (bm, bko)
        dims = (((1,), (1,)), ((), ()))
    else:
        m, kdim = a.shape
        n = b.shape[1]
        ns = n // N_CHIPS if out_gathered else n
        bmr, bko, bn = _pick(m, bk_pref), _pick(kdim, bm_pref), _pick(ns, bn_pref)
        npb = ns // bn
        grid = (kdim // bko, n // bn, m // bmr)
        a_spec = pl.BlockSpec((bmr, bko), lambda i, j, k: (k, i))
        b_spec = pl.BlockSpec((bmr, bn), lambda i, j, k: (k, j))
        if out_gathered:
            o_spec = pl.BlockSpec((None, bko, bn), lambda i, j, k: (j // npb, i, j % npb))
            out_shape = (N_CHIPS, kdim, ns)
        else:
            o_spec = pl.BlockSpec((bko, bn), lambda i, j, k: (i, j))
            out_shape = (kdim, n)
        e_spec = None
        acc_shape = (bko, bn)
        dims = (((0,), (0,)), ((), ()))
    nk = grid[2]

    def body(*refs):
        if a_slots is not None:
            refs = refs[1:]
        if extra is not None:
            a_ref, b_ref, e_ref, o_ref, acc = refs
        else:
            a_ref, b_ref, o_ref, acc = refs
            e_ref = None
        k = pl.program_id(2)
        av = a_ref[...]
        if a_fn is not None:
            av = a_fn(av)
        part = lax.dot_general(av, b_ref[...], dims, preferred_element_type=F32)

        def finish(r):
            if epi is not None:
                r = epi(r, e_ref[...])
            o_ref[...] = r.astype(o_ref.dtype)

        if nk == 1:
            finish(part)
            return

        @pl.when(k == 0)
        def _():
            acc[...] = part

        @pl.when(jnp.logical_and(k > 0, k < nk - 1))
        def _():
            acc[...] += part

        @pl.when(k == nk - 1)
        def _():
            finish(acc[...] + part)

    in_specs = [a_spec, b_spec]
    args = [a, b]
    if extra is not None:
        in_specs.append(e_spec)
        args.append(extra)
    outs, carried = _call(body, grid=grid, in_specs=in_specs, out_specs=[o_spec],
                          out_shape=[jax.ShapeDtypeStruct(out_shape, out_dtype)],
                          scratch_shapes=[pltpu.VMEM(acc_shape, F32)],
                          semantics=("parallel", "parallel", "arbitrary"), name=name, args=args, carry=carry,
                          prefetch=() if a_slots is None else (a_slots,))
    return outs[0] if carry is None else (outs[0], carried)


def _mm_slots(a, wbuf, slots, prev, *, name, carry=None):
    m, kdim = a.shape
    ns = wbuf.shape[2]
    bm, bn = _pick(m, 1024), _pick(ns, 1024)
    npb = ns // bn
    grid = (m // bm, slots.shape[0], npb)

    def body(s_ref, a_ref, b_ref, *rest):
        o_ref = rest[-1]
        o_ref[...] = _dot(a_ref[...], b_ref[...]).astype(o_ref.dtype)

    in_specs = [pl.BlockSpec((bm, kdim), lambda i, s, j, s_ref: (i, 0)),
                pl.BlockSpec((None, kdim, bn), lambda i, s, j, s_ref: (s, 0, j))]
    args = [a, wbuf]
    aliases = None
    if prev is not None:
        in_specs.append(pl.BlockSpec(memory_space=pl.ANY))
        args.append(prev)
        aliases = {2: 0}
    outs, carried = _call(
        body, grid=grid, in_specs=in_specs,
        out_specs=[pl.BlockSpec((bm, bn), lambda i, s, j, s_ref: (i, s_ref[s] * npb + j))],
        out_shape=[jax.ShapeDtypeStruct((m, N_CHIPS * ns), BF16)], scratch_shapes=[],
        semantics=("parallel", "arbitrary", "arbitrary"), name=name, args=args, carry=carry,
        prefetch=(slots,), aliases=aliases)
    return outs[0] if carry is None else (outs[0], carried)


def _row_tile(rows, cols, n_arrays):
    budget = VMEM_LIMIT_BYTES // 3
    cap = min(512, budget // (n_arrays * 2 * cols * 4))
    for t in range(cap - cap % SUBLANES, 0, -SUBLANES):
        if rows % t == 0:
            return t
    return rows


def _norm_mod(x, g, scale, shift, *, name):
    rows, d = x.shape
    tr = _row_tile(rows, d, 3)

    def body(x_ref, g_ref, sc_ref, sh_ref, o_ref):
        xv = x_ref[...]
        r = lax.rsqrt(jnp.mean(xv * xv, axis=-1, keepdims=True) + EPS)
        o_ref[...] = ((xv * r * g_ref[...]) * (1.0 + sc_ref[...]) + sh_ref[...]).astype(o_ref.dtype)

    row = pl.BlockSpec((tr, d), lambda i: (i, 0))
    vec = pl.BlockSpec((1, d), lambda i: (0, 0))
    return pl.pallas_call(
        body, grid=(rows // tr,), in_specs=[row, vec, vec, vec], out_specs=row,
        out_shape=jax.ShapeDtypeStruct((rows, d), BF16),
        compiler_params=_cparams(("parallel",)), name=name)(x, g, scale, shift)


CONV_CHUNK = 2 * SUBLANES


def _shifted_copies(buf, n):
    for r in range(1, SUBLANES):
        buf[r, pl.ds(0, n - SUBLANES), :] = buf[0, pl.ds(r, n - SUBLANES), :]


def _conv_fwd(proj, w_dw, b_dw, ln_g, ln_b, *, cw, carry=None):
    rows = proj.shape[0]
    tt = _pick(rows, 256, HALO)
    hb = tt // HALO

    def body(a_ref, g_ref, ha_ref, hg_ref, w_ref, b_ref, lg_ref, lb_ref, sl_ref, cv_ref, vs):
        i = pl.program_id(0)
        hv = ha_ref[...].astype(F32) * _sigmoid(hg_ref[...].astype(F32))
        vs[0, pl.ds(0, HALO), :] = jnp.where(i == 0, 0.0, hv)
        vs[0, pl.ds(HALO, tt), :] = a_ref[...].astype(F32) * _sigmoid(g_ref[...].astype(F32))
        _shifted_copies(vs, HALO + tt)

        def chunk(ci, carry):
            r0 = pl.multiple_of(ci * CONV_CHUNK, CONV_CHUNK)
            acc = jnp.broadcast_to(b_ref[...], (CONV_CHUNK, cw))
            for k in range(CONV_KERNEL):
                q, r = divmod(HALO - (CONV_KERNEL - 1) + k, SUBLANES)
                acc = acc + w_ref[pl.ds(k, 1), :] * vs[r, pl.ds(r0 + q * SUBLANES, CONV_CHUNK), :]
            cv_ref[pl.ds(r0, CONV_CHUNK), :] = acc
            return carry

        lax.fori_loop(0, tt // CONV_CHUNK, chunk, 0)
        acc = cv_ref[...]
        mu = jnp.mean(acc, axis=-1, keepdims=True)
        xc = acc - mu
        rstd = lax.rsqrt(jnp.mean(xc * xc, axis=-1, keepdims=True) + EPS)
        ln = xc * rstd * lg_ref[...] + lb_ref[...]
        sl_ref[...] = (ln * _sigmoid(ln)).astype(sl_ref.dtype)

    tile = lambda c: pl.BlockSpec((tt, cw), lambda i, c=c: (i, c))
    halo = lambda c: pl.BlockSpec((HALO, cw), lambda i, c=c: (jnp.maximum(i * hb - 1, 0), c))
    vec = pl.BlockSpec((1, cw), lambda i: (0, 0))
    outs, carried = _call(
        body, grid=(rows // tt,),
        in_specs=[tile(0), tile(1), halo(0), halo(1),
                  pl.BlockSpec((CONV_KERNEL, cw), lambda i: (0, 0)), vec, vec, vec],
        out_specs=[pl.BlockSpec((tt, cw), lambda i: (i, 0)), pl.BlockSpec((tt, cw), lambda i: (i, 0))],
        out_shape=[jax.ShapeDtypeStruct((rows, cw), BF16), jax.ShapeDtypeStruct((rows, cw), F32)],
        scratch_shapes=[pltpu.VMEM((SUBLANES, HALO + tt, cw), F32)],
        semantics=("parallel",), name="conv_fwd", args=[proj, proj, proj, proj, w_dw, b_dw, ln_g, ln_b],
        carry=carry)
    return outs if carry is None else (outs, carried)


def _ln_bwd(dsl, cv, ln_g, ln_b):
    rows, cw = cv.shape
    tr = _row_tile(rows, cw, 3)

    def body(d_ref, cv_ref, lg_ref, lb_ref, o_ref, dg_ref, db_ref):
        i = pl.program_id(0)

        @pl.when(i == 0)
        def _():
            dg_ref[...] = jnp.zeros_like(dg_ref)
            db_ref[...] = jnp.zeros_like(db_ref)

        x = cv_ref[...]
        mu = jnp.mean(x, axis=-1, keepdims=True)
        xc = x - mu
        rstd = lax.rsqrt(jnp.mean(xc * xc, axis=-1, keepdims=True) + EPS)
        xh = xc * rstd
        ln = xh * lg_ref[...] + lb_ref[...]
        s = _sigmoid(ln)
        dln = d_ref[...].astype(F32) * (s * (1.0 + ln * (1.0 - s)))
        dg_ref[...] += jnp.sum(dln * xh, axis=0, keepdims=True)
        db_ref[...] += jnp.sum(dln, axis=0, keepdims=True)
        dxh = dln * lg_ref[...]
        m1 = jnp.mean(dxh, axis=-1, keepdims=True)
        m2 = jnp.mean(dxh * xh, axis=-1, keepdims=True)
        o_ref[...] = rstd * (dxh - m1 - xh * m2)

    row = pl.BlockSpec((tr, cw), lambda i: (i, 0))
    vec = pl.BlockSpec((1, cw), lambda i: (0, 0))
    return pl.pallas_call(
        body, grid=(rows // tr,), in_specs=[row, row, vec, vec], out_specs=[row, vec, vec],
        out_shape=[jax.ShapeDtypeStruct((rows, cw), F32), jax.ShapeDtypeStruct((1, cw), F32),
                   jax.ShapeDtypeStruct((1, cw), F32)],
        compiler_params=_cparams(("arbitrary",)), name="ln_bwd")(dsl, cv, ln_g, ln_b)


def _conv_bwd(dcv, proj, w_dw, dproj, *, cw, carry=None):
    rows = proj.shape[0]
    tt = _pick(rows, 256, HALO)
    hb = tt // HALO
    nt = rows // tt
    taps = CONV_KERNEL

    def body(d_ref, dn_ref, a_ref, g_ref, ha_ref, hg_ref, w_ref, dproj_ref, o_ref, dw_ref, db_ref, vs, ds):
        i = pl.program_id(0)

        @pl.when(i == 0)
        def _():
            dw_ref[...] = jnp.zeros_like(dw_ref)
            db_ref[...] = jnp.zeros_like(db_ref)

        hv = ha_ref[...].astype(F32) * _sigmoid(hg_ref[...].astype(F32))
        vs[0, pl.ds(0, HALO), :] = jnp.where(i == 0, 0.0, hv)
        vs[0, pl.ds(HALO, tt), :] = a_ref[...].astype(F32) * _sigmoid(g_ref[...].astype(F32))
        _shifted_copies(vs, HALO + tt)
        ds[0, pl.ds(0, tt), :] = d_ref[...]
        ds[0, pl.ds(tt, HALO), :] = jnp.where(i == nt - 1, 0.0, dn_ref[...])
        _shifted_copies(ds, tt + HALO)
        db_ref[...] += jnp.sum(d_ref[...], axis=0, keepdims=True)
        for k in range(taps):
            q, r = divmod(HALO - (taps - 1) + k, SUBLANES)
            dw_ref[pl.ds(k, 1), :] += jnp.sum(d_ref[...] * vs[r, pl.ds(q * SUBLANES, tt), :], axis=0, keepdims=True)

        def chunk(ci, carry):
            r0 = pl.multiple_of(ci * CONV_CHUNK, CONV_CHUNK)
            dv = jnp.zeros((CONV_CHUNK, cw), F32)
            for k in range(taps):
                q, r = divmod(taps - 1 - k, SUBLANES)
                dv = dv + w_ref[pl.ds(k, 1), :] * ds[r, pl.ds(r0 + q * SUBLANES, CONV_CHUNK), :]
            av = a_ref[pl.ds(r0, CONV_CHUNK), :].astype(F32)
            sg = _sigmoid(g_ref[pl.ds(r0, CONV_CHUNK), :].astype(F32))
            o_ref[pl.ds(r0, CONV_CHUNK), pl.ds(0, cw)] = (dv * sg).astype(o_ref.dtype)
            o_ref[pl.ds(r0, CONV_CHUNK), pl.ds(cw, cw)] = (dv * av * sg * (1.0 - sg)).astype(o_ref.dtype)
            return carry

        lax.fori_loop(0, tt // CONV_CHUNK, chunk, 0)

    tile = lambda c: pl.BlockSpec((tt, cw), lambda i, c=c: (i, c))
    halo = lambda c: pl.BlockSpec((HALO, cw), lambda i, c=c: (jnp.maximum(i * hb - 1, 0), c))
    nxt = pl.BlockSpec((HALO, cw), lambda i: (jnp.minimum((i + 1) * hb, nt * hb - 1), 0))
    outs, carried = _call(
        body, grid=(nt,),
        in_specs=[pl.BlockSpec((tt, cw), lambda i: (i, 0)), nxt, tile(0), tile(1), halo(0), halo(1),
                  pl.BlockSpec((taps, cw), lambda i: (0, 0)), pl.BlockSpec(memory_space=pl.ANY)],
        out_specs=[pl.BlockSpec((tt, 2 * cw), lambda i: (i, 0)),
                   pl.BlockSpec((taps, cw), lambda i: (0, 0)), pl.BlockSpec((1, cw), lambda i: (0, 0))],
        out_shape=[jax.ShapeDtypeStruct(dproj.shape, dproj.dtype), jax.ShapeDtypeStruct((taps, cw), F32),
                   jax.ShapeDtypeStruct((1, cw), F32)],
        scratch_shapes=[pltpu.VMEM((SUBLANES, HALO + tt, cw), F32), pltpu.VMEM((SUBLANES, tt + HALO, cw), F32)],
        semantics=("arbitrary",), name="conv_bwd", args=[dcv, dcv, proj, proj, proj, proj, w_dw, dproj],
        carry=carry, aliases={7: 0})
    return outs if carry is None else (outs, carried)


def _merge_fwd(proj, y_conv, ya, yb, *, cw):
    rows = proj.shape[0]
    tr = _row_tile(rows, cw, 4)

    def body(gc_ref, gs_ref, yc_ref, ya_ref, yb_ref, o_ref):
        ys = ya_ref[...].astype(F32) * _sigmoid(yb_ref[...].astype(F32))
        o_ref[...] = (_sigmoid(gc_ref[...].astype(F32)) * yc_ref[...].astype(F32)
                      + _sigmoid(gs_ref[...].astype(F32)) * ys).astype(o_ref.dtype)

    blk = lambda off: pl.BlockSpec((tr, cw), lambda i, h, off=off: (i, off + h))
    return pl.pallas_call(
        body, grid=(rows // tr, 2), in_specs=[blk(3), blk(5), blk(0), blk(0), blk(0)], out_specs=blk(0),
        out_shape=jax.ShapeDtypeStruct((rows, 2 * cw), BF16),
        compiler_params=_cparams(("parallel", "parallel")), name="merge_fwd")(proj, proj, y_conv, ya, yb)


def _merge_bwd(dmerged, proj, y_conv, ya, yb, *, cw, carry=None):
    rows = proj.shape[0]
    tr = _row_tile(rows, cw, 6)

    def body(d_ref, g_ref, yc_ref, ya_ref, yb_ref, dg_ref, dyc_ref, dya_ref, dyb_ref):
        q = pl.program_id(1)
        d = d_ref[...].astype(F32)
        sg = _sigmoid(g_ref[...].astype(F32))

        @pl.when(q < 2)
        def _():
            dg_ref[...] = (d * yc_ref[...].astype(F32) * sg * (1.0 - sg)).astype(dg_ref.dtype)
            dyc_ref[...] = (d * sg).astype(dyc_ref.dtype)

        @pl.when(q >= 2)
        def _():
            sb = _sigmoid(yb_ref[...].astype(F32))
            yav = ya_ref[...].astype(F32)
            dg_ref[...] = (d * (yav * sb) * sg * (1.0 - sg)).astype(dg_ref.dtype)
            dys = d * sg
            dya_ref[...] = (dys * sb).astype(dya_ref.dtype)
            dyb_ref[...] = (dys * yav * sb * (1.0 - sb)).astype(dyb_ref.dtype)

    spec = lambda f: pl.BlockSpec((tr, cw), lambda i, q, f=f: (i, f(q)))
    conv_half = spec(lambda q: jnp.minimum(q, 1))
    ssm_half = spec(lambda q: jnp.maximum(q - 2, 0))
    o2 = jax.ShapeDtypeStruct((rows, 2 * cw), BF16)
    outs, carried = _call(
        body, grid=(rows // tr, 4),
        in_specs=[spec(lambda q: q % 2), spec(lambda q: 3 + q), conv_half, ssm_half, ssm_half],
        out_specs=[spec(lambda q: 3 + q), conv_half, ssm_half, ssm_half],
        out_shape=[jax.ShapeDtypeStruct((rows, 7 * cw), BF16), o2, o2, o2], scratch_shapes=[],
        semantics=("parallel", "arbitrary"), name="merge_bwd", args=[dmerged, proj, y_conv, ya, yb],
        carry=carry)
    return outs if carry is None else (outs, carried)


def _res_norm(x, mo, gate, g, scale, shift):
    rows, d = x.shape
    tr = _row_tile(rows, d, 4)

    def body(x_ref, mo_ref, gt_ref, g_ref, sc_ref, sh_ref, h_ref, z_ref):
        h = x_ref[...] + gt_ref[...] * mo_ref[...].astype(F32)
        h_ref[...] = h
        r = lax.rsqrt(jnp.mean(h * h, axis=-1, keepdims=True) + EPS)
        z_ref[...] = ((h * r * g_ref[...]) * (1.0 + sc_ref[...]) + sh_ref[...]).astype(z_ref.dtype)

    row = pl.BlockSpec((tr, d), lambda i: (i, 0))
    vec = pl.BlockSpec((1, d), lambda i: (0, 0))
    return pl.pallas_call(
        body, grid=(rows // tr,), in_specs=[row, row, vec, vec, vec, vec], out_specs=[row, row],
        out_shape=[jax.ShapeDtypeStruct((rows, d), F32), jax.ShapeDtypeStruct((rows, d), BF16)],
        compiler_params=_cparams(("parallel",)), name="res_norm")(x, mo, gate, g, scale, shift)


def _final_fwd_bwd(h1, ff, gate2, final_g, target):
    rows, d = h1.shape
    tr = _row_tile(rows, d, 5)

    def body(h_ref, ff_ref, gt_ref, fg_ref, t_ref, dh_ref, dff_ref, loss_ref, dfg_ref, dgt_ref):
        i = pl.program_id(0)

        @pl.when(i == 0)
        def _():
            loss_ref[...] = jnp.zeros_like(loss_ref)
            dfg_ref[...] = jnp.zeros_like(dfg_ref)
            dgt_ref[...] = jnp.zeros_like(dgt_ref)

        ffv = ff_ref[...].astype(F32)
        h2 = h_ref[...] + gt_ref[...] * ffv
        r = lax.rsqrt(jnp.mean(h2 * h2, axis=-1, keepdims=True) + EPS)
        y = h2 * r
        e = y * fg_ref[...] - t_ref[...]
        loss_ref[...] += 0.5 * jnp.sum(jnp.mean(e * e, axis=-1, keepdims=True))
        dout = e * (1.0 / d)
        dfg_ref[...] += jnp.sum(dout * y, axis=0, keepdims=True)
        dy = dout * fg_ref[...]
        dh2 = r * (dy - y * jnp.mean(dy * y, axis=-1, keepdims=True))
        dh_ref[...] = dh2
        dgt_ref[...] += jnp.sum(dh2 * ffv, axis=0, keepdims=True)
        dff_ref[...] = (dh2 * gt_ref[...]).astype(dff_ref.dtype)

    row = pl.BlockSpec((tr, d), lambda i: (i, 0))
    vec = pl.BlockSpec((1, d), lambda i: (0, 0))
    return pl.pallas_call(
        body, grid=(rows // tr,), in_specs=[row, row, vec, vec, row],
        out_specs=[row, row, pl.BlockSpec((1, LANES), lambda i: (0, 0)), vec, vec],
        out_shape=[jax.ShapeDtypeStruct((rows, d), F32), jax.ShapeDtypeStruct((rows, d), BF16),
                   jax.ShapeDtypeStruct((1, LANES), F32), jax.ShapeDtypeStruct((1, d), F32),
                   jax.ShapeDtypeStruct((1, d), F32)],
        compiler_params=_cparams(("arbitrary",)), name="final_fwd_bwd")(h1, ff, gate2, final_g, target)


def _norm_mod_bwd(dz, hin, dres, g, scale, gate, mo, *, name, carry=None):
    rows, d = hin.shape
    with_gate = gate is not None
    tr = _row_tile(rows, d, 6)

    def body(*refs):
        if with_gate:
            (dz_ref, h_ref, dr_ref, g_ref, sc_ref, gt_ref, mo_ref,
             dh_ref, dsh_ref, dsc_ref, dg_ref, dmo_ref, dgt_ref) = refs
        else:
            dz_ref, h_ref, dr_ref, g_ref, sc_ref, dh_ref, dsh_ref, dsc_ref, dg_ref = refs
        i = pl.program_id(0)

        @pl.when(i == 0)
        def _():
            dsh_ref[...] = jnp.zeros_like(dsh_ref)
            dsc_ref[...] = jnp.zeros_like(dsc_ref)
            dg_ref[...] = jnp.zeros_like(dg_ref)
            if with_gate:
                dgt_ref[...] = jnp.zeros_like(dgt_ref)

        dzv = dz_ref[...].astype(F32)
        h = h_ref[...]
        r = lax.rsqrt(jnp.mean(h * h, axis=-1, keepdims=True) + EPS)
        y = h * r
        dsh_ref[...] += jnp.sum(dzv, axis=0, keepdims=True)
        dsc_ref[...] += jnp.sum(dzv * (y * g_ref[...]), axis=0, keepdims=True)
        dn = dzv * (1.0 + sc_ref[...])
        dg_ref[...] += jnp.sum(dn * y, axis=0, keepdims=True)
        dy = dn * g_ref[...]
        dh = dr_ref[...] + r * (dy - y * jnp.mean(dy * y, axis=-1, keepdims=True))
        dh_ref[...] = dh
        if with_gate:
            dmo_ref[...] = (dh * gt_ref[...]).astype(dmo_ref.dtype)
            dgt_ref[...] += jnp.sum(dh * mo_ref[...].astype(F32), axis=0, keepdims=True)

    row = pl.BlockSpec((tr, d), lambda i: (i, 0))
    vec = pl.BlockSpec((1, d), lambda i: (0, 0))
    vshape = jax.ShapeDtypeStruct((1, d), F32)
    in_specs = [row, row, row, vec, vec]
    args = [dz, hin, dres, g, scale]
    out_specs = [row, vec, vec, vec]
    out_shape = [jax.ShapeDtypeStruct((rows, d), F32), vshape, vshape, vshape]
    if with_gate:
        in_specs += [vec, row]
        args += [gate, mo]
        out_specs += [row, vec]
        out_shape += [jax.ShapeDtypeStruct((rows, d), BF16), vshape]
    outs, carried = _call(
        body, grid=(rows // tr,), in_specs=in_specs, out_specs=out_specs, out_shape=out_shape,
        scratch_shapes=[], semantics=("arbitrary",), name=name, args=args, carry=carry)
    return outs if carry is None else (outs, carried)


def _s5_discretise(a_re, a_im, log_dt, b_re, b_im):
    dt = jnp.exp(log_dt)[:, None]
    er = jnp.exp(a_re * dt)
    lr = er * jnp.cos(a_im * dt)
    li = er * jnp.sin(a_im * dt)
    den = a_re * a_re + a_im * a_im
    cr = ((lr - 1.0) * a_re + li * a_im) / den
    ci = (li * a_re - (lr - 1.0) * a_im) / den
    bbr = cr[..., None] * b_re - ci[..., None] * b_im
    bbi = cr[..., None] * b_im + ci[..., None] * b_re
    return lr, li, bbr, bbi


def _block_diag(w):
    g, r, c = w.shape
    nb = g // GROUPS_PER_BLOCK
    eye = jnp.eye(GROUPS_PER_BLOCK, dtype=w.dtype)
    w5 = w.reshape(nb, GROUPS_PER_BLOCK, r, 1, c) * eye[None, :, None, :, None]
    return w5.reshape(nb, GROUPS_PER_BLOCK * r, GROUPS_PER_BLOCK * c)


def _block_diag_extract(m, r, c):
    nb = m.shape[0]
    m5 = m.reshape(nb, GROUPS_PER_BLOCK, r, GROUPS_PER_BLOCK, c)
    idx = jnp.arange(GROUPS_PER_BLOCK)
    d = m5[:, idx, :, idx, :]
    return jnp.moveaxis(d, 0, 1).reshape(nb * GROUPS_PER_BLOCK, r, c)


def _scan_multipliers(lr, li):
    power = jnp.arange(1, SUBLANES + 1, dtype=F32)[None, :, None]
    er = jnp.exp(power * lr)
    pr = er * jnp.cos(power * li)
    pi = er * jnp.sin(power * li)
    rows = jnp.arange(SUBLANES)[None, :, None]
    fr, fi, rr, ri = [], [], [], []
    for s in (1, 2, 4):
        mf = (rows >= s).astype(F32)
        mr = (rows <= SUBLANES - 1 - s).astype(F32)
        fr.append(mf * pr[:, s - 1:s, :])
        fi.append(mf * pi[:, s - 1:s, :])
        rr.append(mr * pr[:, s - 1:s, :])
        ri.append(mr * pi[:, s - 1:s, :])
    fr.append(pr)
    fi.append(pi)
    rr.append(pr[:, ::-1, :])
    ri.append(pi[:, ::-1, :])
    st = lambda xs: jnp.stack(xs, axis=1)
    return st(fr), st(fi), st(rr), st(ri)


def _scan_rows(sre, sim, mul_r, mul_i, n_groups, reverse):
    sgn = -1.0 if reverse else 1.0
    lanes = sre.shape[1]

    def step(k, carry):
        cr, ci = carry
        kk = (n_groups - 1 - k) if reverse else k
        r0 = pl.multiple_of(kk * SUBLANES, SUBLANES)
        xr = sre[pl.ds(r0, SUBLANES), :]
        xi = sim[pl.ds(r0, SUBLANES), :]
        for lvl, s in enumerate((1, 2, 4)):
            sh = (SUBLANES - s) if reverse else s
            nr = pltpu.roll(xr, sh, 0)
            ni = pltpu.roll(xi, sh, 0)
            mr = mul_r[lvl]
            mi = mul_i[lvl] * sgn
            xr, xi = xr + mr * nr - mi * ni, xi + mr * ni + mi * nr
        mr = mul_r[3]
        mi = mul_i[3] * sgn
        xr, xi = xr + mr * cr - mi * ci, xi + mr * ci + mi * cr
        sre[pl.ds(r0, SUBLANES), :] = xr
        sim[pl.ds(r0, SUBLANES), :] = xi
        edge = 0 if reverse else SUBLANES - 1
        ncr = jnp.broadcast_to(xr[edge:edge + 1, :], (SUBLANES, lanes))
        nci = jnp.broadcast_to(xi[edge:edge + 1, :], (SUBLANES, lanes))
        return ncr, nci

    zero = jnp.zeros((SUBLANES, lanes), F32)
    lax.fori_loop(0, n_groups, step, (zero, zero))


def _dot(a, b):
    return jnp.dot(a, b, preferred_element_type=F32)


def _dotf(a, b):
    return _dot(a.astype(BF16), b)


def _s5_operands(lr, li, bbr, bbi, c_re, c_im):
    g = lr.shape[0]
    nb = g // GROUPS_PER_BLOCK
    tb = lambda w: jnp.swapaxes(w, 1, 2)
    b_in = [_block_diag(tb(bbr)), _block_diag(tb(bbi))]
    c_out = [_block_diag(tb(c_re)), _block_diag(tb(c_im))]
    b_out = [_block_diag(bbr), _block_diag(bbi)]
    c_in = [_block_diag(c_re), _block_diag(c_im)]
    lam_r = lr.reshape(nb, 1, STATE_LANES)
    lam_i = li.reshape(nb, 1, STATE_LANES)
    mults = _scan_multipliers(lam_r, lam_i)
    cast = lambda ws: [w.astype(BF16) for w in ws]
    return cast(b_in), cast(c_out), cast(b_out), cast(c_in), mults


def _s5_fwd(proj, d_skip, b_in, c_out, mults, *, col0, carry=None):
    rows = proj.shape[0]
    nb = b_in[0].shape[0]
    tm = _pick(rows, 512, SUBLANES)
    n_tiles = rows // tm
    s_l = STATE_LANES

    def body(u_ref, dk_ref, br, bi, cr, ci, fr_ref, fi_ref, o_ref, sr_ref, si_ref, sre, sim):
        for t in range(n_tiles):
            rs = pl.ds(t * tm, tm)
            ub = u_ref[rs, :]
            sre[rs, :] = _dot(ub, br[...])
            sim[rs, :] = _dot(ub, bi[...])
        _scan_rows(sre, sim, fr_ref, fi_ref, rows // SUBLANES, False)
        for t in range(n_tiles):
            rs = pl.ds(t * tm, tm)
            srb = sre[rs, :].astype(BF16)
            sib = sim[rs, :].astype(BF16)
            sr_ref[rs, :] = srb
            si_ref[rs, :] = sib
            y0 = _dot(srb, cr[...]) - _dot(sib, ci[...])
            y1 = y0 + dk_ref[...] * u_ref[rs, :].astype(F32)
            o_ref[rs, :] = _gelu(y1).astype(o_ref.dtype)

    mat_in = pl.BlockSpec((None, LANES, s_l), lambda g: (g, 0, 0))
    mat_out = pl.BlockSpec((None, s_l, LANES), lambda g: (g, 0, 0))
    mul = pl.BlockSpec((None, 4, SUBLANES, s_l), lambda g: (g, 0, 0, 0))
    state = pl.BlockSpec((rows, s_l), lambda g: (0, g))
    outs, carried = _call(
        body, grid=(nb,),
        in_specs=[pl.BlockSpec((rows, LANES), lambda g: (0, col0 + g)), pl.BlockSpec((1, LANES), lambda g: (0, g))]
        + [mat_in] * 2 + [mat_out] * 2 + [mul] * 2,
        out_specs=[pl.BlockSpec((rows, LANES), lambda g: (0, g)), state, state],
        out_shape=[jax.ShapeDtypeStruct((rows, nb * LANES), BF16), jax.ShapeDtypeStruct((rows, nb * s_l), BF16),
                   jax.ShapeDtypeStruct((rows, nb * s_l), BF16)],
        scratch_shapes=[pltpu.VMEM((rows, s_l), F32), pltpu.VMEM((rows, s_l), F32)],
        semantics=("parallel",), name="s5_fwd", args=[proj, d_skip, *b_in, *c_out, mults[0], mults[1]], carry=carry)
    return outs if carry is None else (outs, carried)


def _s5_bwd(proj, dyg, d_skip, states, c_out, b_out, c_in, mults, dproj, *, col0, carry=None):
    rows = proj.shape[0]
    nb = c_out[0].shape[0]
    tm = _pick(rows, 512, SUBLANES)
    n_tiles = rows // tm
    s_l = STATE_LANES
    n_groups = rows // SUBLANES
    tn = (((0,), (0,)), ((), ()))

    def body(u_ref, dy_ref, dk_ref, sr_ref, si_ref, cr, ci, bor, boi, cir, cii, rr_ref, ri_ref, dproj_ref,
             du_ref, ddk_ref, dbr_ref, dbi_ref, dcr_ref, dci_ref, dlr_ref, dli_ref,
             gre, gim, dy1):
        ddk = jnp.zeros((1, LANES), F32)
        dcr = jnp.zeros((s_l, LANES), F32)
        dci = jnp.zeros((s_l, LANES), F32)
        for t in range(n_tiles):
            rs = pl.ds(t * tm, tm)
            srb = sr_ref[rs, :]
            sib = si_ref[rs, :]
            uf = u_ref[rs, :].astype(F32)
            y0 = _dot(srb, cr[...]) - _dot(sib, ci[...])
            y1 = y0 + dk_ref[...] * uf
            d1 = dy_ref[rs, :].astype(F32) * _gelu_grad(y1)
            dy1[rs, :] = d1
            ddk = ddk + jnp.sum(d1 * uf, axis=0, keepdims=True)
            d1b = d1.astype(BF16)
            dcr = dcr + lax.dot_general(srb, d1b, tn, preferred_element_type=F32)
            dci = dci - lax.dot_general(sib, d1b, tn, preferred_element_type=F32)
            gre[rs, :] = _dot(d1b, cir[...])
            gim[rs, :] = -_dot(d1b, cii[...])
        ddk_ref[...] = ddk
        dcr_ref[...] = dcr
        dci_ref[...] = dci

        last_row = lax.broadcasted_iota(jnp.int32, (SUBLANES, s_l), 0) == SUBLANES - 1

        def group(r0, s_r, s_i, carry):
            cr_, ci_, ar, ai = carry
            xr = gre[pl.ds(r0, SUBLANES), :]
            xi = gim[pl.ds(r0, SUBLANES), :]
            for lvl, s in enumerate((1, 2, 4)):
                nr = pltpu.roll(xr, SUBLANES - s, 0)
                ni = pltpu.roll(xi, SUBLANES - s, 0)
                mr = rr_ref[lvl]
                mi = ri_ref[lvl]
                xr, xi = xr + mr * nr + mi * ni, xi + mr * ni - mi * nr
            mr = rr_ref[3]
            mi = ri_ref[3]
            xr, xi = xr + mr * cr_ + mi * ci_, xi + mr * ci_ - mi * cr_
            gre[pl.ds(r0, SUBLANES), :] = xr
            gim[pl.ds(r0, SUBLANES), :] = xi
            nxt_r = jnp.where(last_row, cr_, pltpu.roll(xr, SUBLANES - 1, 0))
            nxt_i = jnp.where(last_row, ci_, pltpu.roll(xi, SUBLANES - 1, 0))
            ncr = jnp.broadcast_to(xr[0:1, :], (SUBLANES, s_l))
            nci = jnp.broadcast_to(xi[0:1, :], (SUBLANES, s_l))
            return ncr, nci, ar + nxt_r * s_r + nxt_i * s_i, ai + nxt_i * s_r - nxt_r * s_i

        def rev_step(k, carry):
            r0 = pl.multiple_of((n_groups // 2 - 1 - k) * 2 * SUBLANES, 2 * SUBLANES)
            s_r = sr_ref[pl.ds(r0, 2 * SUBLANES), :].astype(F32)
            s_i = si_ref[pl.ds(r0, 2 * SUBLANES), :].astype(F32)
            carry = group(r0 + SUBLANES, s_r[SUBLANES:], s_i[SUBLANES:], carry)
            return group(r0, s_r[:SUBLANES], s_i[:SUBLANES], carry)

        zero = jnp.zeros((SUBLANES, s_l), F32)
        _, _, ar, ai = lax.fori_loop(0, n_groups // 2, rev_step, (zero, zero, zero, zero))
        dlr_ref[...] = jnp.sum(ar, axis=0, keepdims=True)
        dli_ref[...] = jnp.sum(ai, axis=0, keepdims=True)

        dbr = jnp.zeros((LANES, s_l), F32)
        dbi = jnp.zeros((LANES, s_l), F32)
        for t in range(n_tiles):
            rs = pl.ds(t * tm, tm)
            gr = gre[rs, :]
            gi = gim[rs, :]
            grb = gr.astype(BF16)
            gib = gi.astype(BF16)
            du = _dot(grb, bor[...]) + _dot(gib, boi[...]) + dy1[rs, :] * dk_ref[...]
            du_ref[rs, :] = du.astype(du_ref.dtype)
            ub = u_ref[rs, :]
            dbr = dbr + lax.dot_general(ub, grb, tn, preferred_element_type=F32)
            dbi = dbi + lax.dot_general(ub, gib, tn, preferred_element_type=F32)
        dbr_ref[...] = dbr
        dbi_ref[...] = dbi

    mat_in = pl.BlockSpec((None, LANES, s_l), lambda g: (g, 0, 0))
    mat_out = pl.BlockSpec((None, s_l, LANES), lambda g: (g, 0, 0))
    mul = pl.BlockSpec((None, 4, SUBLANES, s_l), lambda g: (g, 0, 0, 0))
    lam = pl.BlockSpec((None, 1, s_l), lambda g: (g, 0, 0))
    col = pl.BlockSpec((rows, LANES), lambda g: (0, g))
    vec = pl.BlockSpec((1, LANES), lambda g: (0, g))
    state = pl.BlockSpec((rows, s_l), lambda g: (0, g))
    outs, carried = _call(
        body, grid=(nb,),
        in_specs=[pl.BlockSpec((rows, LANES), lambda g: (0, col0 + g)), col, vec]
        + [state] * 2 + [mat_out] * 2 + [mat_out] * 2 + [mat_in] * 2 + [mul] * 2
        + [pl.BlockSpec(memory_space=pl.ANY)],
        out_specs=[pl.BlockSpec((rows, LANES), lambda g: (0, col0 + g)), vec, mat_in, mat_in, mat_out, mat_out,
                   lam, lam],
        out_shape=[jax.ShapeDtypeStruct(dproj.shape, dproj.dtype), jax.ShapeDtypeStruct((1, nb * LANES), F32),
                   jax.ShapeDtypeStruct((nb, LANES, s_l), F32), jax.ShapeDtypeStruct((nb, LANES, s_l), F32),
                   jax.ShapeDtypeStruct((nb, s_l, LANES), F32), jax.ShapeDtypeStruct((nb, s_l, LANES), F32),
                   jax.ShapeDtypeStruct((nb, 1, s_l), F32), jax.ShapeDtypeStruct((nb, 1, s_l), F32)],
        scratch_shapes=[pltpu.VMEM((rows, s_l), F32)] * 2 + [pltpu.VMEM((rows, LANES), F32)],
        semantics=("parallel",), name="s5_bwd",
        args=[proj, dyg, d_skip, *states, *c_out, *b_out, *c_in, mults[2], mults[3], dproj], carry=carry,
        aliases={13: 0})
    return outs if carry is None else (outs, carried)


def _silu(v):
    return v * _sigmoid(v)


def _ada_fwd(c_all, w_shard, b_cols):
    d, n = w_shard.shape
    bn = _pick(n, 512)

    def body(c_ref, w_ref, b_ref, o_ref):
        ca = _silu(c_ref[...]).astype(BF16)
        o_ref[...] = _dot(ca, w_ref[...].astype(BF16)) + b_ref[...]

    return pl.pallas_call(
        body, grid=(n // bn,),
        in_specs=[pl.BlockSpec((N_DEV, d), lambda j: (0, 0)), pl.BlockSpec((d, bn), lambda j: (0, j)),
                  pl.BlockSpec((1, bn), lambda j: (0, j))],
        out_specs=pl.BlockSpec((N_DEV, bn), lambda j: (0, j)),
        out_shape=jax.ShapeDtypeStruct((N_DEV, n), F32),
        compiler_params=_cparams(("parallel",)), name="ada_fwd")(c_all, w_shard, b_cols)


def _ada_bwd(c_all, dmod_cols):
    d = c_all.shape[1]
    n = dmod_cols.shape[1]
    bn = _pick(n, 512)

    def body(c_ref, g_ref, o_ref):
        ca = _silu(c_ref[...]).astype(BF16)
        o_ref[...] = lax.dot_general(ca, g_ref[...].astype(BF16), (((0,), (0,)), ((), ())),
                                     preferred_element_type=F32)

    return pl.pallas_call(
        body, grid=(n // bn,),
        in_specs=[pl.BlockSpec((N_DEV, d), lambda j: (0, 0)), pl.BlockSpec((N_DEV, bn), lambda j: (0, j))],
        out_specs=pl.BlockSpec((d, bn), lambda j: (0, j)),
        out_shape=jax.ShapeDtypeStruct((d, n), F32),
        compiler_params=_cparams(("parallel",)), name="ada_bwd")(c_all, dmod_cols)


def _cast_bf16(w, *, name):
    rows, cols = w.shape
    tr = _row_tile(rows, cols, 2)

    def body(w_ref, o_ref):
        o_ref[...] = w_ref[...].astype(BF16)

    row = pl.BlockSpec((tr, cols), lambda i: (i, 0))
    return pl.pallas_call(
        body, grid=(rows // tr,), in_specs=[row], out_specs=row,
        out_shape=jax.ShapeDtypeStruct((rows, cols), BF16),
        compiler_params=_cparams(("parallel",)), name=name)(w)


def _adamw(w, g, m, v, *, name, carry=None):
    rows, cols = w.shape
    tr = _row_tile(rows, cols, 7)
    c1 = 1.0 / (1.0 - ADAM_B1 ** ADAM_STEP)
    c2 = 1.0 / (1.0 - ADAM_B2 ** ADAM_STEP)

    def body(w_ref, g_ref, m_ref, v_ref, d_ref, nm_ref, nv_ref):
        gv = g_ref[...]
        nm = ADAM_B1 * m_ref[...] + (1.0 - ADAM_B1) * gv
        nv = ADAM_B2 * v_ref[...] + (1.0 - ADAM_B2) * (gv * gv)
        nm_ref[...] = nm
        nv_ref[...] = nv
        d_ref[...] = -ADAM_LR * ((nm * c1) / (jnp.sqrt(nv * c2) + ADAM_EPS) + ADAM_WD * w_ref[...])

    row = pl.BlockSpec((tr, cols), lambda i: (i, 0))
    shp = jax.ShapeDtypeStruct((rows, cols), F32)
    outs, carried = _call(
        body, grid=(rows // tr,), in_specs=[row] * 4, out_specs=[row] * 3, out_shape=[shp] * 3,
        scratch_shapes=[], semantics=("parallel",), name=name, args=[w, g, m, v], carry=carry)
    return outs if carry is None else (outs, carried)


def _sum_leading(a, *, name, out_dtype=F32):
    n, rows, cols = a.shape
    tr = _row_tile(rows, cols, n + 1)

    def body(a_ref, o_ref):
        acc = a_ref[0].astype(F32)
        for i in range(1, n):
            acc = acc + a_ref[i].astype(F32)
        o_ref[...] = acc.astype(o_ref.dtype)

    return pl.pallas_call(
        body, grid=(rows // tr,), in_specs=[pl.BlockSpec((n, tr, cols), lambda i: (0, i, 0))],
        out_specs=pl.BlockSpec((tr, cols), lambda i: (i, 0)),
        out_shape=jax.ShapeDtypeStruct((rows, cols), out_dtype),
        compiler_params=_cparams(("parallel",)), name=name)(a)


def _add_half(dw, land, my_c, *, name):
    n, r, cols = dw.shape
    h = r // 2
    tr = _row_tile(h, cols, 3)
    hb = h // tr

    def body(c_ref, a_ref, b_ref, o_ref):
        o_ref[...] = (a_ref[...].astype(F32) + b_ref[...].astype(F32)).astype(o_ref.dtype)

    gs = pltpu.PrefetchScalarGridSpec(
        num_scalar_prefetch=1, grid=(n, hb),
        in_specs=[pl.BlockSpec((None, tr, cols), lambda s, i, c_ref: (s, c_ref[0] * hb + i, 0)),
                  pl.BlockSpec((None, tr, cols), lambda s, i, c_ref: (s, i, 0))],
        out_specs=pl.BlockSpec((None, tr, cols), lambda s, i, c_ref: (s, i, 0)))
    return pl.pallas_call(
        body, grid_spec=gs, out_shape=jax.ShapeDtypeStruct((n, h, cols), BF16),
        compiler_params=_cparams(("parallel", "parallel")), name=name)(my_c, dw, land)


def _mesh_pos():
    return lax.axis_index("x"), lax.axis_index("y"), lax.axis_index("c")


def _other_chips(x, y):
    return [(1 - x, y), (x, 1 - y), (1 - x, 1 - y)]


def _gather_small(blk, *, name):
    m_per, n = blk.shape

    def body(x_ref, out_ref, send_sems, recv_sems, local_sem):
        x, y, c = _mesh_pos()
        me, sibling = (x, y, c), (x, y, 1 - c)
        chips = _other_chips(x, y)

        def rows(px, py, pc):
            return out_ref.at[pl.ds((4 * px + 2 * py + pc) * m_per, m_per), :]

        def copy(k, block, to, src=None):
            return pltpu.make_async_remote_copy(
                src_ref=rows(*block) if src is None else src, dst_ref=rows(*block),
                send_sem=send_sems.at[k], recv_sem=recv_sems.at[k], device_id=to, device_id_type=MESH)

        mine = pltpu.make_async_copy(x_ref, rows(*me), local_sem)
        mine.start()
        first = [copy(0, me, sibling, src=x_ref)]
        first += [copy(1 + j, me, (*chip, c), src=x_ref) for j, chip in enumerate(chips)]
        for cp in first:
            cp.start()
        passed = [copy(4 + j, (*chip, c), sibling) for j, chip in enumerate(chips)]
        for j, chip in enumerate(chips):
            copy(1 + j, (*chip, c), me).wait_recv()
            passed[j].start()
        copy(0, sibling, me).wait_recv()
        for j, chip in enumerate(chips):
            copy(4 + j, (*chip, 1 - c), me).wait_recv()
        for cp in first + passed:
            cp.wait_send()
        mine.wait()

    return pl.pallas_call(
        body, out_shape=jax.ShapeDtypeStruct((N_DEV * m_per, n), blk.dtype),
        in_specs=[pl.BlockSpec(memory_space=pltpu.VMEM)], out_specs=pl.BlockSpec(memory_space=pltpu.VMEM),
        scratch_shapes=[pltpu.SemaphoreType.DMA((7,)), pltpu.SemaphoreType.DMA((7,)), pltpu.SemaphoreType.DMA],
        compiler_params=pltpu.CompilerParams(vmem_limit_bytes=VMEM_LIMIT_BYTES), name=name)(blk)


def _hbm_specs(n):
    return [pl.BlockSpec(memory_space=pl.ANY)] * n


def _gather_weights(shards):
    n = len(shards)

    def body(*refs):
        ins, outs = refs[:n], refs[n:2 * n]
        send_sems, recv_sems, local_sems = refs[2 * n:]
        x, y, c = _mesh_pos()
        me_chip = 2 * x + y
        sibling = (x, y, 1 - c)
        chips = _other_chips(x, y)

        def half(w, chip_idx, pc):
            h = shards[w].shape[0] // 2
            return outs[w].at[chip_idx, pl.ds(pc * h, h), :]

        def copy(w, k, chip_idx, pc, to, src=None):
            dst = half(w, chip_idx, pc)
            return pltpu.make_async_remote_copy(
                src_ref=dst if src is None else src, dst_ref=dst,
                send_sem=send_sems.at[6 * w + k], recv_sem=recv_sems.at[6 * w + k],
                device_id=to, device_id_type=MESH)

        local = [pltpu.make_async_copy(ins[w], outs[w].at[me_chip], local_sems.at[w]) for w in range(n)]
        for cp in local:
            cp.start()
        sends = []
        for w in range(n):
            h = shards[w].shape[0] // 2
            for j, chip in enumerate(chips):
                cp = copy(w, j, me_chip, c, (*chip, c), src=ins[w].at[pl.ds(c * h, h), :])
                cp.start()
                sends.append(cp)
        for w in range(n):
            for j, chip in enumerate(chips):
                chip_idx = 2 * chip[0] + chip[1]
                copy(w, j, chip_idx, c, (x, y, c)).wait_recv()
                cp = copy(w, 3 + j, chip_idx, c, sibling)
                cp.start()
                sends.append(cp)
        for w in range(n):
            for j, chip in enumerate(chips):
                copy(w, 3 + j, 2 * chip[0] + chip[1], 1 - c, (x, y, c)).wait_recv()
        for cp in sends:
            cp.wait_send()
        for cp in local:
            cp.wait()

    return pl.pallas_call(
        body, out_shape=[jax.ShapeDtypeStruct((N_CHIPS,) + s.shape, s.dtype) for s in shards],
        in_specs=_hbm_specs(n), out_specs=_hbm_specs(n),
        scratch_shapes=[pltpu.SemaphoreType.DMA((6 * n,)), pltpu.SemaphoreType.DMA((6 * n,)),
                        pltpu.SemaphoreType.DMA((n,))],
        name="gather_weights")(*shards)


def _swap_halves(dws, *, name):
    n = len(dws)

    def body(*refs):
        ins, outs = refs[:n], refs[n:2 * n]
        send_sems, recv_sems = refs[2 * n:]
        x, y, c = _mesh_pos()
        cps = []
        for w in range(n):
            h = dws[w].shape[1] // 2
            cp = pltpu.make_async_remote_copy(
                src_ref=ins[w].at[:, pl.ds((1 - c) * h, h), :], dst_ref=outs[w],
                send_sem=send_sems.at[w], recv_sem=recv_sems.at[w],
                device_id=(x, y, 1 - c), device_id_type=MESH)
            cp.start()
            cps.append(cp)
        for cp in cps:
            cp.wait()

    return pl.pallas_call(
        body, out_shape=[jax.ShapeDtypeStruct((s.shape[0], s.shape[1] // 2, s.shape[2]), s.dtype) for s in dws],
        in_specs=_hbm_specs(n), out_specs=_hbm_specs(n),
        scratch_shapes=[pltpu.SemaphoreType.DMA((n,)), pltpu.SemaphoreType.DMA((n,))],
        name=name)(*dws)


def _chip_exchange(parts):
    n = len(parts)

    def body(*refs):
        ins, outs = refs[:n], refs[n:2 * n]
        send_sems, recv_sems, local_sems = refs[2 * n:]
        x, y, c = _mesh_pos()
        me_chip = 2 * x + y
        chips = _other_chips(x, y)
        local = [pltpu.make_async_copy(ins[w].at[me_chip], outs[w].at[me_chip], local_sems.at[w]) for w in range(n)]
        for cp in local:
            cp.start()
        cps = []
        for w in range(n):
            for j, chip in enumerate(chips):
                cp = pltpu.make_async_remote_copy(
                    src_ref=ins[w].at[2 * chip[0] + chip[1]], dst_ref=outs[w].at[me_chip],
                    send_sem=send_sems.at[3 * w + j], recv_sem=recv_sems.at[3 * w + j],
                    device_id=(*chip, c), device_id_type=MESH)
                cp.start()
                cps.append((cp, w, j, chip))
        for cp, w, j, chip in cps:
            slot = outs[w].at[2 * chip[0] + chip[1]]
            pltpu.make_async_remote_copy(
                src_ref=slot, dst_ref=slot, send_sem=send_sems.at[3 * w + j], recv_sem=recv_sems.at[3 * w + j],
                device_id=(x, y, c), device_id_type=MESH).wait_recv()
        for cp, _, _, _ in cps:
            cp.wait_send()
        for cp in local:
            cp.wait()

    return pl.pallas_call(
        body, out_shape=[jax.ShapeDtypeStruct(s.shape, s.dtype) for s in parts],
        in_specs=_hbm_specs(n), out_specs=_hbm_specs(n),
        scratch_shapes=[pltpu.SemaphoreType.DMA((3 * n,)), pltpu.SemaphoreType.DMA((3 * n,)),
                        pltpu.SemaphoreType.DMA((n,))],
        name="chip_exchange")(*parts)


def _join_halves(halves):
    n = len(halves)

    def body(*refs):
        ins, outs = refs[:n], refs[n:2 * n]
        send_sems, recv_sems, local_sems = refs[2 * n:]
        x, y, c = _mesh_pos()
        cps, local = [], []
        for w in range(n):
            h = halves[w].shape[0]
            mine = outs[w].at[pl.ds(c * h, h), :]
            lc = pltpu.make_async_copy(ins[w], mine, local_sems.at[w])
            lc.start()
            local.append(lc)
            cp = pltpu.make_async_remote_copy(
                src_ref=ins[w], dst_ref=mine, send_sem=send_sems.at[w], recv_sem=recv_sems.at[w],
                device_id=(x, y, 1 - c), device_id_type=MESH)
            cp.start()
            cps.append(cp)
        for w in range(n):
            h = halves[w].shape[0]
            theirs = outs[w].at[pl.ds((1 - c) * h, h), :]
            pltpu.make_async_remote_copy(
                src_ref=theirs, dst_ref=theirs, send_sem=send_sems.at[w], recv_sem=recv_sems.at[w],
                device_id=(x, y, c), device_id_type=MESH).wait_recv()
        for cp in cps:
            cp.wait_send()
        for lc in local:
            lc.wait()

    return pl.pallas_call(
        body, out_shape=[jax.ShapeDtypeStruct((2 * s.shape[0], s.shape[1]), s.dtype) for s in halves],
        in_specs=_hbm_specs(n), out_specs=_hbm_specs(n),
        scratch_shapes=[pltpu.SemaphoreType.DMA((n,)), pltpu.SemaphoreType.DMA((n,)), pltpu.SemaphoreType.DMA((n,))],
        name="join_halves")(*halves)


def _cast_into_slot(w, chip, after, *, name):
    rows, cols = w.shape
    tr = _row_tile(rows, cols, 2)

    def body(chip_ref, w_ref, after_ref, o_ref):
        o_ref[...] = w_ref[...].astype(BF16)

    gs = pltpu.PrefetchScalarGridSpec(
        num_scalar_prefetch=1, grid=(rows // tr,),
        in_specs=[pl.BlockSpec((tr, cols), lambda i, chip_ref: (i, 0)), pl.BlockSpec(memory_space=pl.ANY)],
        out_specs=pl.BlockSpec((None, tr, cols), lambda i, chip_ref: (chip_ref[0], i, 0)))
    return pl.pallas_call(
        body, grid_spec=gs, out_shape=jax.ShapeDtypeStruct((N_CHIPS, rows, cols), BF16),
        compiler_params=_cparams(("parallel",)), name=name)(chip, w, after)


def _row_range(h, lo, hi, parts):
    step = h // parts
    assert step * parts == h and step % (2 * SUBLANES) == 0, (h, parts)
    return lo * step, (hi - lo) * step


def _gather_carry(items):
    n_copies = sum(len(js) for _, js, _, _, _ in items)
    sem = pltpu.SemaphoreType.DMA((2 * n_copies,))

    def copies(outs, sems):
        send_sems, recv_sems = sems
        x, y, c = _mesh_pos()
        me_chip = 2 * x + y
        chips = _other_chips(x, y)
        out_ici, in_ici, out_d2d, in_d2d = [], [], [], []
        k = 0
        for w, (buf, js, lo, hi, parts) in enumerate(items):
            h = buf.shape[1] // 2
            r0, nr = _row_range(h, lo, hi, parts)

            def copy(k, chip_idx, pc, to):
                ref = outs[w].at[chip_idx, pl.ds(pc * h + r0, nr), :]
                return pltpu.make_async_remote_copy(
                    src_ref=ref, dst_ref=ref, send_sem=send_sems.at[k], recv_sem=recv_sems.at[k],
                    device_id=to, device_id_type=MESH)

            for j in js:
                chip = chips[j]
                chip_idx = 2 * chip[0] + chip[1]
                out_ici.append(copy(k, me_chip, c, (*chip, c)))
                in_ici.append(copy(k, chip_idx, c, (x, y, c)))
                out_d2d.append(copy(k + 1, chip_idx, c, (x, y, 1 - c)))
                in_d2d.append(copy(k + 1, chip_idx, 1 - c, (x, y, c)))
                k += 2
        return out_ici, in_ici, out_d2d, in_d2d

    def start(ins, outs, sems):
        for cp in copies(outs, sems)[0]:
            cp.start()

    def finish(ins, outs, sems):
        out_ici, in_ici, out_d2d, in_d2d = copies(outs, sems)
        for arrived, onward in zip(in_ici, out_d2d):
            arrived.wait_recv()
            onward.start()
        for arrived in in_d2d:
            arrived.wait_recv()
        for cp in out_ici + out_d2d:
            cp.wait_send()

    bufs = [it[0] for it in items]
    shapes = [jax.ShapeDtypeStruct(b.shape, b.dtype) for b in bufs]
    return _Carry(bufs, shapes, {i: i for i in range(len(bufs))}, [sem, sem], start, finish)


def _exchange_carry(items):
    n = len(items)
    sem = pltpu.SemaphoreType.DMA((3 * n,))
    given = [w for w in range(n) if items[w][1] is not None]

    def copies(ins, outs, sems):
        send_sems, recv_sems = sems
        x, y, c = _mesh_pos()
        chips = _other_chips(x, y)
        sends, recvs = [], []
        for w, (part, _, lo, hi, parts) in enumerate(items):
            r0, nr = _row_range(part.shape[1], lo, hi, parts)
            for j, chip in enumerate(chips):
                land = outs[w].at[j, pl.ds(r0, nr), :]
                sends.append(pltpu.make_async_remote_copy(
                    src_ref=ins[w].at[2 * chip[0] + chip[1], pl.ds(r0, nr), :], dst_ref=land,
                    send_sem=send_sems.at[3 * w + j], recv_sem=recv_sems.at[3 * w + j],
                    device_id=(*chip, c), device_id_type=MESH))
                recvs.append(pltpu.make_async_remote_copy(
                    src_ref=land, dst_ref=land,
                    send_sem=send_sems.at[3 * w + j], recv_sem=recv_sems.at[3 * w + j],
                    device_id=(x, y, c), device_id_type=MESH))
        return sends, recvs

    def start(ins, outs, sems):
        for cp in copies(ins, outs, sems)[0]:
            cp.start()

    def finish(ins, outs, sems):
        sends, recvs = copies(ins, outs, sems)
        for cp in recvs:
            cp.wait_recv()
        for cp in sends:
            cp.wait_send()

    inputs = [it[0] for it in items] + [items[w][1] for w in given]
    shapes = [jax.ShapeDtypeStruct((3,) + it[0].shape[1:], it[0].dtype) for it in items]
    aliases = {n + i: w for i, w in enumerate(given)}
    return _Carry(inputs, shapes, aliases, [sem, sem], start, finish)


def _sum_into_half(part, landed, chip, my_c, *, name):
    _, h, cols = part.shape
    tr = _row_tile(h, cols, 5)
    hb = h // tr

    def body(chip_ref, c_ref, p_ref, l_ref, o_ref):
        acc = p_ref[...].astype(F32)
        for j in range(3):
            acc = acc + l_ref[j].astype(F32)
        o_ref[...] = acc

    gs = pltpu.PrefetchScalarGridSpec(
        num_scalar_prefetch=2, grid=(hb,),
        in_specs=[pl.BlockSpec((None, tr, cols), lambda i, chip_ref, c_ref: (chip_ref[0], i, 0)),
                  pl.BlockSpec((3, tr, cols), lambda i, chip_ref, c_ref: (0, i, 0))],
        out_specs=pl.BlockSpec((tr, cols), lambda i, chip_ref, c_ref: (c_ref[0] * hb + i, 0)))
    return pl.pallas_call(
        body, grid_spec=gs, out_shape=jax.ShapeDtypeStruct((2 * h, cols), F32),
        compiler_params=_cparams(("parallel",)), name=name)(chip, my_c, part, landed)


def _join_carry(fulls):
    n = len(fulls)
    sem = pltpu.SemaphoreType.DMA((n,))

    def copies(outs, sems):
        send_sems, recv_sems = sems
        x, y, c = _mesh_pos()
        sends, recvs = [], []
        for w in range(n):
            h = fulls[w].shape[0] // 2
            mine = outs[w].at[pl.ds(c * h, h), :]
            theirs = outs[w].at[pl.ds((1 - c) * h, h), :]
            sends.append(pltpu.make_async_remote_copy(
                src_ref=mine, dst_ref=mine, send_sem=send_sems.at[w], recv_sem=recv_sems.at[w],
                device_id=(x, y, 1 - c), device_id_type=MESH))
            recvs.append(pltpu.make_async_remote_copy(
                src_ref=theirs, dst_ref=theirs, send_sem=send_sems.at[w], recv_sem=recv_sems.at[w],
                device_id=(x, y, c), device_id_type=MESH))
        return sends, recvs

    def start(ins, outs, sems):
        for cp in copies(outs, sems)[0]:
            cp.start()

    def finish(ins, outs, sems):
        sends, recvs = copies(outs, sems)
        for cp in recvs:
            cp.wait_recv()
        for cp in sends:
            cp.wait_send()

    shapes = [jax.ShapeDtypeStruct(f.shape, f.dtype) for f in fulls]
    return _Carry(fulls, shapes, {i: i for i in range(n)}, [sem, sem], start, finish)


class _NoComm:
    def __init__(self, big):
        self.big = big
        self.grads = {}

    def weight(self, name):
        return self.big[name]

    def mm_in(self, u, afters):
        return _mm(u, self.big["w_in"], mode="nn", out_dtype=BF16, name="mm_in")

    def mm_d_in(self, dproj):
        return _mm(dproj, self.big["w_in"], mode="nt", out_dtype=F32, name="mm_d_in")

    def carry(self, site):
        return None

    def done(self, site, carried):
        pass

    def grad(self, name, dw):
        self.grads[name] = dw

    def early_grads(self, early):
        self.early = early


def _gather_rows_carry(blk):
    m_per = blk.shape[0]
    sem = pltpu.SemaphoreType.DMA((7,))

    def copies(ins, outs, sems):
        send_sems, recv_sems, local_sem = sems
        x, y, c = _mesh_pos()
        me, sibling = (x, y, c), (x, y, 1 - c)
        chips = _other_chips(x, y)

        def rows(px, py, pc):
            return outs[0].at[pl.ds((4 * px + 2 * py + pc) * m_per, m_per), :]

        def copy(k, block, to, src=None):
            return pltpu.make_async_remote_copy(
                src_ref=rows(*block) if src is None else src, dst_ref=rows(*block),
                send_sem=send_sems.at[k], recv_sem=recv_sems.at[k], device_id=to, device_id_type=MESH)

        mine = pltpu.make_async_copy(ins[0], rows(*me), local_sem.at[0])
        first = [copy(0, me, sibling, src=ins[0])]
        first += [copy(1 + j, me, (*chip, c), src=ins[0]) for j, chip in enumerate(chips)]
        passed = [copy(4 + j, (*chip, c), sibling) for j, chip in enumerate(chips)]
        landed = [copy(1 + j, (*chip, c), me) for j, chip in enumerate(chips)]
        from_sibling = [copy(0, sibling, me)] + [copy(4 + j, (*chip, 1 - c), me) for j, chip in enumerate(chips)]
        return mine, first, passed, landed, from_sibling

    def start(ins, outs, sems):
        mine, first, _, _, _ = copies(ins, outs, sems)
        mine.start()
        for cp in first:
            cp.start()

    def finish(ins, outs, sems):
        mine, first, passed, landed, from_sibling = copies(ins, outs, sems)
        for arrived, onward in zip(landed, passed):
            arrived.wait_recv()
            onward.start()
        for arrived in from_sibling:
            arrived.wait_recv()
        for cp in first + passed:
            cp.wait_send()
        mine.wait()

    shape = jax.ShapeDtypeStruct((N_DEV * m_per, blk.shape[1]), blk.dtype)
    return _Carry([blk], [shape], {}, [sem, sem, pltpu.SemaphoreType.DMA((1,))], start, finish)


def _gather_fresh_carry(own, js):
    n = len(js)
    h = own.shape[0] // 2
    sem = pltpu.SemaphoreType.DMA((2 * n,))

    def copies(ins, outs, sems):
        send_sems, recv_sems = sems
        x, y, c = _mesh_pos()
        chips = _other_chips(x, y)
        out_ici, in_ici, out_d2d, in_d2d = [], [], [], []

        def copy(k, src, dst, to):
            return pltpu.make_async_remote_copy(
                src_ref=src, dst_ref=dst, send_sem=send_sems.at[k], recv_sem=recv_sems.at[k],
                device_id=to, device_id_type=MESH)

        for jj, j in enumerate(js):
            mine = ins[0].at[pl.ds(c * h, h), :]
            land = outs[0].at[jj, pl.ds(c * h, h), :]
            other = outs[0].at[jj, pl.ds((1 - c) * h, h), :]
            out_ici.append(copy(2 * jj, mine, land, (*chips[j], c)))
            in_ici.append(copy(2 * jj, land, land, (x, y, c)))
            out_d2d.append(copy(2 * jj + 1, land, land, (x, y, 1 - c)))
            in_d2d.append(copy(2 * jj + 1, other, other, (x, y, c)))
        return out_ici, in_ici, out_d2d, in_d2d

    def start(ins, outs, sems):
        for cp in copies(ins, outs, sems)[0]:
            cp.start()

    def finish(ins, outs, sems):
        out_ici, in_ici, out_d2d, in_d2d = copies(ins, outs, sems)
        for arrived, onward in zip(in_ici, out_d2d):
            arrived.wait_recv()
            onward.start()
        for arrived in in_d2d:
            arrived.wait_recv()
        for cp in out_ici + out_d2d:
            cp.wait_send()

    return _Carry([own], [jax.ShapeDtypeStruct((n,) + own.shape, own.dtype)], {}, [sem, sem], start, finish)


def _w_in_copies(own_ref, land_ref, send_sems, recv_sems):
    x, y, c = _mesh_pos()
    h = own_ref.shape[0] // 2
    return [pltpu.make_async_remote_copy(
        src_ref=own_ref.at[pl.ds(c * h, h), :], dst_ref=land_ref.at[j, pl.ds(c * h, h), :],
        send_sem=send_sems[j], recv_sem=recv_sems[j], device_id=(*chip, c), device_id_type=MESH)
        for j, chip in enumerate(_other_chips(x, y))]


def _w_in_send(own, after):
    hbm = pl.BlockSpec(memory_space=pltpu.HBM)
    sem = pl.BlockSpec(memory_space=pltpu.SEMAPHORE)
    land_shape = (3,) + own.shape

    def body(own_ref, land_ref, after_ref, s0, s1, s2, r0, r1, r2, own_thru, land_thru, token):
        for cp in _w_in_copies(own_ref, land_ref, (s0, s1, s2), (r0, r1, r2)):
            cp.start()
        token[...] = jnp.zeros_like(token)

    outs = pl.pallas_call(
        body, name="w_in_send",
        out_shape=(pltpu.SemaphoreType.DMA(()),) * 6 + (
            pltpu.HBM(own.shape, own.dtype), pltpu.HBM(land_shape, own.dtype), jax.ShapeDtypeStruct((8, LANES), F32)),
        in_specs=(hbm, hbm, pl.BlockSpec(memory_space=pl.ANY)),
        out_specs=(sem,) * 6 + (hbm, hbm, pl.BlockSpec(memory_space=pltpu.VMEM)),
        input_output_aliases={0: 6, 1: 7},
        compiler_params=pltpu.CompilerParams(has_side_effects=pltpu.SideEffectType.DATAFLOW_SIDE_EFFECTING),
    )(pltpu.with_memory_space_constraint(own, pltpu.HBM),
      pltpu.with_memory_space_constraint(lax.empty(land_shape, own.dtype), pltpu.HBM), after)
    return outs[:6], outs[6], outs[7], outs[8]


def _w_in_wait(sems, own, land, afters):
    hbm = pl.BlockSpec(memory_space=pltpu.HBM)
    sem = pl.BlockSpec(memory_space=pltpu.SEMAPHORE)
    n_after = len(afters)

    def body(own_ref, land_ref, s0, s1, s2, r0, r1, r2, *rest):
        for cp in _w_in_copies(own_ref, land_ref, (s0, s1, s2), (r0, r1, r2)):
            cp.wait_send()
            cp.wait_recv()

    return pl.pallas_call(
        body, name="w_in_wait", out_shape=(pltpu.HBM(own.shape, own.dtype), pltpu.HBM(land.shape, land.dtype)),
        in_specs=(hbm, hbm) + (sem,) * 6 + (pl.BlockSpec(memory_space=pl.ANY),) * n_after, out_specs=(hbm, hbm),
        input_output_aliases={0: 0, 1: 1},
        compiler_params=pltpu.CompilerParams(has_side_effects=pltpu.SideEffectType.DATAFLOW_SIDE_EFFECTING),
    )(own, land, *sems, *afters)


def _forward_carry(land):
    n = land.shape[0]
    h = land.shape[1] // 2
    sem = pltpu.SemaphoreType.DMA((n,))

    def copies(outs, sems):
        send_sems, recv_sems = sems
        x, y, c = _mesh_pos()
        sends, recvs = [], []
        for j in range(n):
            mine = outs[0].at[j, pl.ds(c * h, h), :]
            other = outs[0].at[j, pl.ds((1 - c) * h, h), :]
            sends.append(pltpu.make_async_remote_copy(
                src_ref=mine, dst_ref=mine, send_sem=send_sems.at[j], recv_sem=recv_sems.at[j],
                device_id=(x, y, 1 - c), device_id_type=MESH))
            recvs.append(pltpu.make_async_remote_copy(
                src_ref=other, dst_ref=other, send_sem=send_sems.at[j], recv_sem=recv_sems.at[j],
                device_id=(x, y, c), device_id_type=MESH))
        return sends, recvs

    def start(ins, outs, sems):
        for cp in copies(outs, sems)[0]:
            cp.start()

    def finish(ins, outs, sems):
        sends, recvs = copies(outs, sems)
        for cp in recvs:
            cp.wait_recv()
        for cp in sends:
            cp.wait_send()

    return _Carry([land], [jax.ShapeDtypeStruct(land.shape, land.dtype)], {0: 0}, [sem, sem], start, finish)


def _swap_carry(dws):
    n = len(dws)
    sem = pltpu.SemaphoreType.DMA((n,))

    def copies(ins, outs, sems):
        send_sems, recv_sems = sems
        x, y, c = _mesh_pos()
        cps = []
        for w in range(n):
            h = dws[w].shape[1] // 2
            cps.append(pltpu.make_async_remote_copy(
                src_ref=ins[w].at[:, pl.ds((1 - c) * h, h), :], dst_ref=outs[w],
                send_sem=send_sems.at[w], recv_sem=recv_sems.at[w],
                device_id=(x, y, 1 - c), device_id_type=MESH))
        return cps

    def start(ins, outs, sems):
        for cp in copies(ins, outs, sems):
            cp.start()

    def finish(ins, outs, sems):
        for cp in copies(ins, outs, sems):
            cp.wait()

    shapes = [jax.ShapeDtypeStruct((s.shape[0], s.shape[1] // 2, s.shape[2]), s.dtype) for s in dws]
    return _Carry(dws, shapes, {}, [sem, sem], start, finish)


def _merge_carries(carries):
    if len(carries) == 1:
        return carries[0]
    inputs, out_shapes, sem_shapes, aliases, spans = [], [], [], {}, []
    for cy in carries:
        i0, o0, s0 = len(inputs), len(out_shapes), len(sem_shapes)
        aliases.update({i0 + i: o0 + o for i, o in cy.aliases.items()})
        inputs += cy.inputs
        out_shapes += cy.out_shapes
        sem_shapes += cy.sem_shapes
        spans.append((slice(i0, len(inputs)), slice(o0, len(out_shapes)), slice(s0, len(sem_shapes))))

    def start(ins, outs, sems):
        for cy, (si, so, ss) in zip(carries, spans):
            cy.start(ins[si], outs[so], sems[ss])

    def finish(ins, outs, sems):
        for cy, (si, so, ss) in zip(carries, spans):
            cy.finish(ins[si], outs[so], sems[ss])

    return _Carry(inputs, out_shapes, aliases, sem_shapes, start, finish)


ALL_CHIPS = (0, 1, 2)


class _MeshComm:
    GATHER_AT = {
        "mm_in_rest": [("w_conv_out", ALL_CHIPS, 0, 1, 1), ("w_glu_a", ALL_CHIPS, 0, 1, 1),
                       ("w_glu_b", ALL_CHIPS, 0, 1, 1)],
        "conv_fwd": [("w_out", ALL_CHIPS, 0, 1, 1), ("w_ff1", ALL_CHIPS, 0, 1, 8)],
        "s5_fwd": [("w_ff1", ALL_CHIPS, 1, 6, 8)],
        "mm_glu_a": [("w_ff1", ALL_CHIPS, 6, 7, 8)],
        "mm_glu_b": [("w_ff1", ALL_CHIPS, 7, 8, 8)],
        "mm_out": [("w_ff2", ALL_CHIPS, 0, 2, 8)],
        "mm_ff1": [("w_ff2", ALL_CHIPS, 2, 8, 8)],
    }
    SWAP_AT = {
        "mm_d_ff2": ["w_ff2"],
        "mm_d_ff1": ["w_ff1"],
        "conv_bwd": ["w_out", "w_glu_a", "w_glu_b", "w_conv_out"],
    }
    EXCHANGE_AT = {
        "mm_dw_ff1": [("w_ff2", 0, 6, 8)],
        "mm_d_ff1": [("w_ff2", 6, 8, 8)],
        "norm2_bwd": [("w_ff1", 0, 3, 8)],
        "mm_dw_out": [("w_ff1", 3, 5, 8)],
        "mm_d_out": [("w_ff1", 5, 7, 8)],
        "merge_bwd": [("w_ff1", 7, 8, 8)],
        "s5_bwd": [("w_out", 0, 1, 1), ("w_glu_a", 0, 1, 1), ("w_glu_b", 0, 1, 1), ("w_conv_out", 0, 1, 1)],
        "mm_d_in": [("w_in", 0, 1, 1)],
    }
    EARLY_AT = "mm_dw_in"

    def __init__(self, shards, pos, chip, my_c):
        self.pos = pos
        self.chip = chip
        self.my_c = my_c
        self.shards = shards
        self.w_in_own = _cast_bf16(shards["w_in"], name="cast_w_in")
        self.raw = {}
        self.parts = {}
        self.landing = {}
        self.halves = {}
        self.pending = {}
        self.last_site = {}
        for site, items in self.EXCHANGE_AT.items():
            for it in items:
                self.last_site[it[0]] = site

    def weight(self, name):
        g = self.bufs[name]
        return g.reshape(g.shape[0] * g.shape[1], g.shape[2]) if name in ROW_SHARDED else g

    def _slot_ids(self):
        x, y, _ = self.pos
        ids = [2 * x + y] + [2 * cx + cy for cx, cy in _other_chips(x, y)]
        return jnp.stack(ids).astype(jnp.int32)

    def start_w_in(self, after):
        *self.w_in_flight, token = _w_in_send(self.w_in_own, after)
        self.bufs = {n: _cast_into_slot(s, self.chip, token, name="cast_" + n)
                     for n, s in self.shards.items() if n != "w_in"}
        return token

    def mm_in(self, u, afters):
        ids = self._slot_ids()
        sems, own, land = self.w_in_flight
        proj = _mm_slots(u, own[None], ids[0:1], None, name="mm_in_own")
        own, land = _w_in_wait(sems, own, land, [proj] + list(self.bufs.values()) + list(afters))
        land, = _run_carry(_forward_carry(land), name="forward_w_in")
        proj, carried = _mm_slots(u, land, ids[1:4], proj, name="mm_in_rest", carry=self.carry("mm_in_rest"))
        self.done("mm_in_rest", carried)
        self.w_in_rel = jnp.concatenate([own[None], land], axis=0)
        return proj

    def mm_d_in(self, dproj):
        raw = self.raw.pop("w_in")
        landed, = _run_carry(_swap_carry([raw]), name="swap_halves_w_in")
        self.parts["w_in"] = _add_half(raw, landed, self.my_c, name="add_half_w_in")
        carry = self.carry("mm_d_in")
        du, carried = _mm(dproj, self.w_in_rel, mode="nt", out_dtype=F32, name="mm_d_in", carry=carry,
                          a_slots=self._slot_ids())
        self.done("mm_d_in", carried)
        return du

    def early_grads(self, early):
        self.early = early

    def carry(self, site):
        jobs = []
        if site == self.EARLY_AT:
            flat, self.early_offs = _pack(list(self.early.values()))
            jobs.append(("early", None, _gather_rows_carry(flat.reshape(-1, PACK_COLS))))
        if site in self.GATHER_AT:
            items = self.GATHER_AT[site]
            jobs.append(("gather", items, _gather_carry([(self.bufs[it[0]],) + tuple(it[1:]) for it in items])))
        if site in self.EXCHANGE_AT:
            items = self.EXCHANGE_AT[site]
            jobs.append(("exchange", items, _exchange_carry(
                [(self.parts[it[0]], self.landing.get(it[0])) + tuple(it[1:]) for it in items])))
        if site in self.SWAP_AT:
            names = self.SWAP_AT[site]
            jobs.append(("swap", names, _swap_carry([self.raw[n] for n in names])))
        if not jobs:
            return None
        self.pending[site] = jobs
        return _merge_carries([job[2] for job in jobs])

    def done(self, site, carried):
        pos = 0
        for kind, items, carry in self.pending.pop(site):
            outs = carried[pos:pos + len(carry.out_shapes)]
            pos += len(carry.out_shapes)
            if kind == "early":
                self.early_all = outs[0]
            elif kind == "gather":
                self.bufs.update(zip([it[0] for it in items], outs))
            elif kind == "swap":
                for n, landed in zip(items, outs):
                    self.parts[n] = _add_half(self.raw.pop(n), landed, self.my_c, name="add_half_" + n)
            else:
                for it, landed in zip(items, outs):
                    n = it[0]
                    self.landing[n] = landed
                    if self.last_site[n] == site:
                        self.halves[n] = _sum_into_half(self.parts.pop(n), self.landing.pop(n), self.chip,
                                                        self.my_c, name="sum_chips_" + n)

    def grad(self, name, dw):
        if name in ROW_SHARDED:
            dw = dw.reshape(N_CHIPS, dw.shape[0] // N_CHIPS, dw.shape[1])
        self.raw[name] = dw

    def join(self, names, *, name):
        return dict(zip(names, _run_carry(_join_carry([self.halves.pop(n) for n in names]), name=name)))


def _local_step(x, target, mod, small, comm):
    rows, d = x.shape
    cw = d // 2
    shift1, scale1, gate1, shift2, scale2, gate2 = mod
    _, _, bbr, bbi = small["s5_disc"]
    b_in, c_out, b_out, c_in, mults = _s5_operands(*small["s5_loglam"], bbr, bbi, small["c_re"], small["c_im"])
    wt = comm.weight

    def riding(site, fn, *args, **kwargs):
        carry = comm.carry(site)
        if carry is None:
            return fn(*args, **kwargs)
        out, carried = fn(*args, carry=carry, **kwargs)
        comm.done(site, carried)
        return out

    u = _norm_mod(x, small["norm1_g"], scale1, shift1, name="norm1_fwd")
    proj = comm.mm_in(u, [*b_in, *c_out, *b_out, *c_in, *mults])
    sl, cv = riding("conv_fwd", _conv_fwd, proj, small["w_dw"], small["b_dw"], small["ln_g"], small["ln_b"], cw=cw)
    y_conv = _mm(sl, wt("w_conv_out"), mode="nn", out_dtype=BF16, name="mm_conv_out")
    yg, st_re, st_im = riding("s5_fwd", _s5_fwd, proj, small["d_skip"], b_in, c_out, mults, col0=2 * cw // LANES)
    ya = riding("mm_glu_a", _mm, yg, wt("w_glu_a"), mode="nn", out_dtype=BF16, name="mm_glu_a")
    yb = riding("mm_glu_b", _mm, yg, wt("w_glu_b"), mode="nn", out_dtype=BF16, name="mm_glu_b")
    merged = _merge_fwd(proj, y_conv, ya, yb, cw=cw)
    mo = riding("mm_out", _mm, merged, wt("w_out"), mode="nn", out_dtype=BF16, name="mm_out")
    h1, z = _res_norm(x, mo, gate1, small["norm2_g"], scale2, shift2)
    f1 = riding("mm_ff1", _mm, z, wt("w_ff1"), mode="nn", out_dtype=BF16, name="mm_ff1")
    ff = _mm(f1, wt("w_ff2"), mode="nn", out_dtype=BF16, name="mm_ff2", a_fn=_relu2_bf16)
    dh2, dff, loss, d_final_g, d_gate2 = _final_fwd_bwd(h1, ff, gate2, small["final_g"], target)

    comm.grad("w_ff2", _mm(f1, dff, mode="tn", out_dtype=BF16, name="mm_dw_ff2", a_fn=_relu2_bf16))
    df1 = riding("mm_d_ff2", _mm, dff, wt("w_ff2"), mode="nt", out_dtype=BF16, name="mm_d_ff2", extra=f1,
                 epi=lambda acc, f: acc * (2.0 * jnp.maximum(f.astype(F32), 0.0)))
    comm.grad("w_ff1", riding("mm_dw_ff1", _mm, z, df1, mode="tn", out_dtype=BF16, name="mm_dw_ff1",
                              out_gathered=True))
    dz = riding("mm_d_ff1", _mm, df1, wt("w_ff1"), mode="nt", out_dtype=F32, name="mm_d_ff1")
    dh1, d_shift2, d_scale2, d_norm2_g, dmo, d_gate1 = riding(
        "norm2_bwd", _norm_mod_bwd, dz, h1, dh2, small["norm2_g"], scale2, gate1, mo, name="norm2_bwd")
    comm.grad("w_out", riding("mm_dw_out", _mm, merged, dmo, mode="tn", out_dtype=BF16, name="mm_dw_out"))
    dmerged = riding("mm_d_out", _mm, dmo, wt("w_out"), mode="nt", out_dtype=BF16, name="mm_d_out")
    dproj, dy_conv, dya, dyb = riding("merge_bwd", _merge_bwd, dmerged, proj, y_conv, ya, yb, cw=cw)
    comm.grad("w_glu_a", _mm(yg, dya, mode="tn", out_dtype=BF16, name="mm_dw_glu_a", out_gathered=True))
    comm.grad("w_glu_b", _mm(yg, dyb, mode="tn", out_dtype=BF16, name="mm_dw_glu_b", out_gathered=True))
    dyg_a = _mm(dya, wt("w_glu_a"), mode="nt", out_dtype=F32, name="mm_d_glu_a")
    dyg = _mm(dyb, wt("w_glu_b"), mode="nt", out_dtype=F32, name="mm_d_glu_b", extra=dyg_a,
              epi=lambda acc, e: acc + e)
    comm.grad("w_conv_out", _mm(sl, dy_conv, mode="tn", out_dtype=BF16, name="mm_dw_conv_out", out_gathered=True))
    dsl = _mm(dy_conv, wt("w_conv_out"), mode="nt", out_dtype=F32, name="mm_d_conv_out")
    dcv, d_ln_g, d_ln_b = _ln_bwd(dsl, cv, small["ln_g"], small["ln_b"])
    dproj, d_w_dw, d_b_dw = riding("conv_bwd", _conv_bwd, dcv, proj, small["w_dw"], dproj, cw=cw)
    dproj, d_d_skip, dbr, dbi, dcr, dci, dlr, dli = riding(
        "s5_bwd", _s5_bwd, proj, dyg, small["d_skip"], (st_re, st_im), c_out, b_out, c_in, mults, dproj,
        col0=2 * cw // LANES)
    sw = lambda m: jnp.swapaxes(m, 1, 2)
    early = {
        "dmod_tail": jnp.concatenate([d_gate1, d_shift2, d_scale2, d_gate2], axis=1), "loss": loss[:, 0:1],
        "w_dw": d_w_dw, "b_dw": d_b_dw, "ln_g": d_ln_g, "ln_b": d_ln_b,
        "lam_re": dlr.reshape(-1, SSM_STATE), "lam_im": dli.reshape(-1, SSM_STATE),
        "bb_re": sw(_block_diag_extract(dbr, SSM_GROUP, SSM_STATE)),
        "bb_im": sw(_block_diag_extract(dbi, SSM_GROUP, SSM_STATE)),
        "c_re": sw(_block_diag_extract(dcr, SSM_STATE, SSM_GROUP)),
        "c_im": sw(_block_diag_extract(dci, SSM_STATE, SSM_GROUP)),
        "d_skip": d_d_skip, "norm2_g": d_norm2_g, "final_g": d_final_g,
    }
    comm.early_grads(early)
    comm.grad("w_in", riding("mm_dw_in", _mm, u, dproj, mode="tn", out_dtype=BF16, name="mm_dw_in",
                             out_gathered=True))
    du = comm.mm_d_in(dproj)
    grad_x, d_shift1, d_scale1, d_norm1_g = riding(
        "norm1_bwd", _norm_mod_bwd, du, x, dh1, small["norm1_g"], scale1, None, None, name="norm1_bwd")
    late ={"dmod_head": jnp.concatenate([d_shift1, d_scale1], axis=1), "norm1_g": d_norm1_g}
    return grad_x, early, late


WEIGHT_NAMES = ["w_ada", "b_ada", "norm1_g", "w_in", "w_dw", "b_dw", "ln_g", "ln_b", "w_conv_out", "a_re", "a_im",
                "log_dt", "b_re", "b_im", "c_re", "c_im", "d_skip", "w_glu_a", "w_glu_b", "w_out", "norm2_g",
                "w_ff1", "w_ff2", "final_g"]
BIG_NAMES = ["w_in", "w_conv_out", "w_glu_a", "w_glu_b", "w_out", "w_ff1", "w_ff2"]
ROW_SHARDED = ("w_out", "w_ff2")
PACK_COLS = 1024
PACK_TILE = SUBLANES * PACK_COLS


def _pack(arrays):
    flats = [a.reshape(-1) for a in arrays]
    offs = []
    total = 0
    for f in flats:
        offs.append(total)
        total += f.shape[0]
    pad = (-total) % PACK_TILE
    if pad:
        flats.append(jnp.zeros((pad,), F32))
    return jnp.concatenate(flats), offs


def _unpack(flat, offs, like):
    return [flat[o:o + a.size].reshape(a.shape) for o, a in zip(offs, like)]


def _gather_w_dw(w_shard):
    k, n = w_shard.shape
    padded = jnp.pad(w_shard, ((0, HALO - k), (0, 0)))
    allw = _gather_small(padded, name="gather_w_dw").reshape(N_CHIPS, 2, HALO, n)[:, 0, :k]
    return jnp.moveaxis(allw, 0, 1).reshape(k, N_CHIPS * n)


def kernel(x, c, w_ada, b_ada, norm1_g, w_in, w_dw, b_dw, ln_g, ln_b, w_conv_out, a_re, a_im, log_dt, b_re, b_im, c_re, c_im, d_skip, w_glu_a, w_glu_b, w_out, norm2_g, w_ff1, w_ff2, final_g, loss_target, m_w_ada, m_b_ada, m_norm1_g, m_w_in, m_w_dw, m_b_dw, m_ln_g, m_ln_b, m_w_conv_out, m_a_re, m_a_im, m_log_dt, m_b_re, m_b_im, m_c_re, m_c_im, m_d_skip, m_w_glu_a, m_w_glu_b, m_w_out, m_norm2_g, m_w_ff1, m_w_ff2, m_final_g, v_w_ada, v_b_ada, v_norm1_g, v_w_in, v_w_dw, v_b_dw, v_ln_g, v_ln_b, v_w_conv_out, v_a_re, v_a_im, v_log_dt, v_b_re, v_b_im, v_c_re, v_c_im, v_d_skip, v_w_glu_a, v_w_glu_b, v_w_out, v_norm2_g, v_w_ff1, v_w_ff2, v_final_g):
    given = dict(locals())
    w = {n: given[n] for n in WEIGHT_NAMES}
    m = {n: given["m_" + n] for n in WEIGHT_NAMES}
    v = {n: given["v_" + n] for n in WEIGHT_NAMES}
    d = x.shape[2]
    xi, yi, ci = _mesh_pos()
    chip = 2 * xi + yi
    dev = 4 * xi + 2 * yi + ci
    my_c = jnp.reshape(ci, (1,)).astype(jnp.int32)
    chip_arr = jnp.reshape(chip, (1,)).astype(jnp.int32)

    comm = _MeshComm({n: w[n][0] for n in BIG_NAMES}, (xi, yi, ci), chip_arr, my_c)

    ndw = w_dw.shape[2]
    assert d // SUBLANES == ndw
    first = jnp.concatenate([c.reshape(SUBLANES, ndw), jnp.pad(w_dw[0], ((0, HALO - CONV_KERNEL), (0, 0)))])
    first_all = _gather_small(first, name="gather_c_w_dw").reshape(N_DEV, SUBLANES + HALO, ndw)
    c_all = first_all[:, :SUBLANES].reshape(N_DEV, d)
    taps = first_all.reshape(N_CHIPS, 2, SUBLANES + HALO, ndw)[:, 0, SUBLANES:SUBLANES + CONV_KERNEL]
    w_dw_full = jnp.moveaxis(taps, 0, 1).reshape(CONV_KERNEL, N_CHIPS * ndw)

    nmod = w_ada.shape[2]
    b_cols = lax.dynamic_slice(b_ada, (0, chip * nmod), (1, nmod))
    mod_part = _ada_fwd(c_all, w_ada[0], b_cols)
    mod_all = _gather_small(mod_part, name="gather_mod").reshape(N_CHIPS, 2, N_DEV, nmod)[:, 0]
    mod_full = jnp.moveaxis(mod_all, 0, 1).reshape(N_DEV, N_CHIPS * nmod)
    mod_row = lax.dynamic_slice(mod_full, (dev, 0), (1, N_CHIPS * nmod))
    mod = [mod_row[:, i * d:(i + 1) * d] for i in range(6)]

    token = comm.start_w_in(mod_row)
    log_dt_0 = log_dt[0] + token[0, 0]

    disc_in = (a_re[0], a_im[0], log_dt_0, b_re[0], b_im[0])
    disc, disc_vjp = jax.vjp(_s5_discretise, *disc_in)
    dt = jnp.exp(log_dt_0)[:, None]
    small = {"norm1_g": norm1_g, "w_dw": w_dw_full, "b_dw": b_dw, "ln_g": ln_g, "ln_b": ln_b,
             "c_re": c_re[0], "c_im": c_im[0], "d_skip": d_skip, "norm2_g": norm2_g,
             "final_g": final_g[None, :], "s5_disc": disc, "s5_loglam": (a_re[0] * dt, a_im[0] * dt)}

    grad_x, early, late = _local_step(x[0], loss_target[0], mod, small, comm)
    grads = {}

    early_all = comm.early_all.reshape(N_DEV, -1, PACK_COLS)
    early_sum = _sum_leading(early_all, name="sum_small_grads").reshape(-1)
    summed = dict(zip(early, _unpack(early_sum, comm.early_offs, list(early.values()))))
    flat, late_offs = _pack(list(late.values()))
    late_all = _gather_small(flat.reshape(-1, PACK_COLS), name="gather_late_grads").reshape(N_DEV, -1, PACK_COLS)
    late_sum = _sum_leading(late_all, name="sum_late_grads").reshape(-1)
    summed.update(zip(late, _unpack(late_sum, late_offs, list(late.values()))))
    head = late_all[:, :2 * d // PACK_COLS].reshape(N_DEV, 2 * d)
    tail = early_all[:, :4 * d // PACK_COLS].reshape(N_DEV, 4 * d)
    dmod_all = jnp.concatenate([head, tail], axis=1)

    grads["w_ada"] = _ada_bwd(c_all, lax.dynamic_slice(dmod_all, (0, chip * nmod), (N_DEV, nmod)))
    grads["b_ada"] = _sum_leading(dmod_all.reshape(N_DEV, SUBLANES, 6 * d // SUBLANES),
                                  name="sum_b_ada").reshape(1, 6 * d)
    da_re, da_im, dlog_dt, db_re, db_im = disc_vjp(
        (summed["lam_re"], summed["lam_im"], summed["bb_re"], summed["bb_im"]))
    grads.update({
        "norm1_g": summed["norm1_g"], "w_dw": lax.dynamic_slice(summed["w_dw"], (0, chip * ndw), (CONV_KERNEL, ndw)),
        "b_dw": summed["b_dw"], "ln_g": summed["ln_g"], "ln_b": summed["ln_b"],
        "a_re": da_re, "a_im": da_im, "log_dt": dlog_dt, "b_re": db_re, "b_im": db_im,
        "c_re": summed["c_re"], "c_im": summed["c_im"], "d_skip": summed["d_skip"],
        "norm2_g": summed["norm2_g"], "final_g": summed["final_g"],
    })

    delta, new_m, new_v = {}, {}, {}
    grads.update(comm.join(BIG_NAMES, name="join_halves"))
    for n in ["w_ada"] + BIG_NAMES:
        shp = w[n].shape
        two_d = lambda a: a.reshape(shp[1], shp[2])
        res = _adamw(two_d(w[n]), two_d(grads[n]), two_d(m[n]), two_d(v[n]), name="adamw_" + n)
        delta[n], new_m[n], new_v[n] = [r.reshape(shp) for r in res]
    grads = {n: grads[n].reshape(w[n].shape) for n in WEIGHT_NAMES}
    rest = [n for n in WEIGHT_NAMES if n not in delta]
    packs = []
    for src in (w, grads, m, v):
        flat, offs = _pack([src[n] for n in rest])
        packs.append(flat.reshape(-1, 1024))
    outs = _adamw(*packs, name="adamw_small")
    for dst, o in zip((delta, new_m, new_v), outs):
        for n, a in zip(rest, _unpack(o.reshape(-1), offs, [w[k] for k in rest])):
            dst[n] = a

    return (summed["loss"].reshape(()), grad_x[None], *[grads[n] for n in WEIGHT_NAMES],
            *[delta[n] for n in WEIGHT_NAMES], *[new_m[n] for n in WEIGHT_NAMES],
            *[new_v[n] for n in WEIGHT_NAMES])
```

```python
import functools
import math

import jax
import jax.numpy as jnp
from jax import lax
from jax.experimental import pallas as pl
from jax.experimental.pallas import tpu as pltpu

F32 = jnp.float32
BF16 = jnp.bfloat16
EPS = 1e-6
CONV_KERNEL = 31
SSM_GROUP = 16
SSM_STATE = 64
ADAM_LR = 0.001
ADAM_B1 = 0.9
ADAM_B2 = 0.999
ADAM_EPS = 1e-08
ADAM_WD = 0.01
ADAM_STEP = 10

N_CHIPS = 4
N_DEV = 8
VMEM_LIMIT_BYTES = 56 * 1024 * 1024
LANES = 128
SUBLANES = 8
HALO = 32
GROUPS_PER_BLOCK = LANES // SSM_GROUP
STATE_LANES = GROUPS_PER_BLOCK * SSM_STATE
MESH = pl.DeviceIdType.MESH


def _cparams(sem):
    return pltpu.CompilerParams(dimension_semantics=sem, vmem_limit_bytes=VMEM_LIMIT_BYTES)


def _pick(n, pref, mult=LANES):
    if n <= pref:
        return n
    best = None
    for d in range(mult, pref + 1, mult):
        if n % d == 0:
            best = d
    assert best is not None, (n, pref)
    return best


def _sigmoid(v):
    return 1.0 / (1.0 + jnp.exp(-v))


def _gelu_parts(v):
    k0 = math.sqrt(2.0 / math.pi)
    inner = k0 * (v + 0.044715 * v * v * v)
    t = jnp.tanh(inner)
    return k0, t


def _gelu(v):
    _, t = _gelu_parts(v)
    return 0.5 * v * (1.0 + t)


def _gelu_grad(v):
    k0, t = _gelu_parts(v)
    return 0.5 * (1.0 + t) + 0.5 * v * (1.0 - t * t) * k0 * (1.0 + 3.0 * 0.044715 * v * v)


def _relu2_bf16(a):
    t = jnp.maximum(a.astype(F32), 0.0)
    return (t * t).astype(BF16)


class _Carry:
    def __init__(self, inputs, out_shapes, aliases, sem_shapes, start, finish):
        self.inputs = list(inputs)
        self.out_shapes = list(out_shapes)
        self.aliases = dict(aliases)
        self.sem_shapes = list(sem_shapes)
        self.start = start
        self.finish = finish


def _call(body, *, grid, in_specs, out_specs, out_shape, scratch_shapes, semantics, name, args, carry=None,
          prefetch=(), aliases=None):
    n_in, n_out, n_scr, n_pf = len(in_specs), len(out_specs), len(scratch_shapes), len(prefetch)
    own_aliases = {n_pf + i: o for i, o in (aliases or {}).items()}
    if carry is None:
        gs = pltpu.PrefetchScalarGridSpec(
            num_scalar_prefetch=n_pf, grid=grid, in_specs=in_specs, out_specs=out_specs,
            scratch_shapes=scratch_shapes)
        outs = pl.pallas_call(
            body, grid_spec=gs, out_shape=out_shape, input_output_aliases=own_aliases,
            compiler_params=_cparams(semantics), name=name)(*prefetch, *args)
        return list(outs), []
    ci, co = len(carry.inputs), len(carry.out_shapes)

    def wrapped(*refs):
        pf, refs = refs[:n_pf], refs[n_pf:]
        ins, cins = refs[:n_in], refs[n_in:n_in + ci]
        p = n_in + ci
        outs, couts = refs[p:p + n_out], refs[p + n_out:p + n_out + co]
        p += n_out + co
        scr, csems = refs[p:p + n_scr], refs[p + n_scr:]
        first = pl.program_id(0) == 0
        last = pl.program_id(0) == grid[0] - 1
        for ax in range(1, len(grid)):
            first = jnp.logical_and(first, pl.program_id(ax) == 0)
            last = jnp.logical_and(last, pl.program_id(ax) == grid[ax] - 1)

        @pl.when(first)
        def _():
            carry.start(cins, couts, csems)

        body(*pf, *ins, *outs, *scr)

        @pl.when(last)
        def _():
            carry.finish(cins, couts, csems)

    any_spec = pl.BlockSpec(memory_space=pl.ANY)
    gs = pltpu.PrefetchScalarGridSpec(
        num_scalar_prefetch=n_pf, grid=grid, in_specs=list(in_specs) + [any_spec] * ci,
        out_specs=list(out_specs) + [any_spec] * co, scratch_shapes=list(scratch_shapes) + carry.sem_shapes)
    all_aliases = dict(own_aliases)
    all_aliases.update({n_pf + n_in + i: n_out + o for i, o in carry.aliases.items()})
    outs = pl.pallas_call(
        wrapped, grid_spec=gs, out_shape=list(out_shape) + carry.out_shapes, input_output_aliases=all_aliases,
        compiler_params=_cparams(("arbitrary",) * len(grid)), name=name)(*prefetch, *args, *carry.inputs)
    return list(outs[:n_out]), list(outs[n_out:])


def _run_carry(carry, *, name):
    ci = len(carry.inputs)

    def body(*refs):
        cins, couts, csems = refs[:ci], refs[ci:ci + len(carry.out_shapes)], refs[ci + len(carry.out_shapes):]
        carry.start(cins, couts, csems)
        carry.finish(cins, couts, csems)

    any_spec = pl.BlockSpec(memory_space=pl.ANY)
    outs = pl.pallas_call(
        body, in_specs=[any_spec] * ci, out_specs=[any_spec] * len(carry.out_shapes), out_shape=carry.out_shapes,
        scratch_shapes=carry.sem_shapes, input_output_aliases=carry.aliases, name=name)(*carry.inputs)
    return list(outs)


def _mm(a, b, *, mode, out_dtype, name, out_gathered=False, a_fn=None, epi=None, extra=None,
        bm_pref=1024, bn_pref=1024, bk_pref=2048, carry=None, a_slots=None):
    gathered = (b.ndim == 3)
    if mode == "nn":
        m, kdim = a.shape
        ns = b.shape[-1]
        n = ns * (N_CHIPS if gathered else 1)
        bm, bn, bk = _pick(m, bm_pref), _pick(ns, bn_pref), _pick(kdim, bk_pref)
        npb = ns // bn
        grid = (m // bm, n // bn, kdim // bk)
        a_spec = pl.BlockSpec((bm, bk), lambda i, j, k: (i, k))
        if gathered:
            b_spec = pl.BlockSpec((None, bk, bn), lambda i, j, k: (j // npb, k, j % npb))
        else:
            b_spec = pl.BlockSpec((bk, bn), lambda i, j, k: (k, j))
        o_spec = pl.BlockSpec((bm, bn), lambda i, j, k: (i, j))
        e_spec = pl.BlockSpec((bm, bn), lambda i, j, k: (i, j))
        out_shape = (m, n)
        acc_shape = (bm, bn)
        dims = (((1,), (0,)), ((), ()))
    elif mode == "nt":
        m = a.shape[0]
        kdim, ns = b.shape[-2], b.shape[-1]
        n = ns * (N_CHIPS if gathered else 1)
        assert a.shape[1] == n
        bm, bko, bnr = _pick(m, bm_pref), _pick(kdim, bn_pref), _pick(ns, bk_pref)
        npb = ns // bnr
        grid = (m // bm, kdim // bko, n // bnr)
        a_spec = pl.BlockSpec((bm, bnr), lambda i, j, k: (i, k))
        if gathered:
            b_spec = pl.BlockSpec((None, bko, bnr), lambda i, j, k: (k // npb, j, k % npb))
        else:
            b_spec = pl.BlockSpec((bko, bnr), lambda i, j, k: (j, k))
        o_spec = pl.BlockSpec((bm, bko), lambda i, j, k: (i, j))
        e_spec = pl.BlockSpec((bm, bko), lambda i, j, k: (i, j))
        if a_slots is not None:
            assert gathered and extra is None
            a_spec = pl.BlockSpec((bm, bnr), lambda i, j, k, s_ref: (i, s_ref[k // npb] * npb + k % npb))
            b_spec = pl.BlockSpec((None, bko, bnr), lambda i, j, k, s_ref: (k // npb, j, k % npb))
            o_spec = pl.BlockSpec((bm, bko), lambda i, j, k, s_ref: (i, j))
        out_shape = (m, kdim)
        acc_shape = (bm, bko)
        dims = (((1,), (1,)), ((), ()))
    else:
        m, kdim = a.shape
        n = b.shape[1]
        ns = n // N_CHIPS if out_gathered else n
        bmr, bko, bn = _pick(m, bk_pref), _pick(kdim, bm_pref), _pick(ns, bn_pref)
        npb = ns // bn
        grid = (kdim // bko, n // bn, m // bmr)
        a_spec = pl.BlockSpec((bmr, bko), lambda i, j, k: (k, i))
        b_spec = pl.BlockSpec((bmr, bn), lambda i, j, k: (k, j))
        if out_gathered:
            o_spec = pl.BlockSpec((None, bko, bn), lambda i, j, k: (j // npb, i, j % npb))
            out_shape = (N_CHIPS, kdim, ns)
        else:
            o_spec = pl.BlockSpec((bko, bn), lambda i, j, k: (i, j))
            out_shape = (kdim, n)
        e_spec = None
        acc_shape = (bko, bn)
        dims = (((0,), (0,)), ((), ()))
    nk = grid[2]

    def body(*refs):
        if a_slots is not None:
            refs = refs[1:]
        if extra is not None:
            a_ref, b_ref, e_ref, o_ref, acc = refs
        else:
            a_ref, b_ref, o_ref, acc = refs
            e_ref = None
        k = pl.program_id(2)
        av = a_ref[...]
        if a_fn is not None:
            av = a_fn(av)
        part = lax.dot_general(av, b_ref[...], dims, preferred_element_type=F32)

        def finish(r):
            if epi is not None:
                r = epi(r, e_ref[...])
            o_ref[...] = r.astype(o_ref.dtype)

        if nk == 1:
            finish(part)
            return

        @pl.when(k == 0)
        def _():
            acc[...] = part

        @pl.when(jnp.logical_and(k > 0, k < nk - 1))
        def _():
            acc[...] += part

        @pl.when(k == nk - 1)
        def _():
            finish(acc[...] + part)

    in_specs = [a_spec, b_spec]
    args = [a, b]
    if extra is not None:
        in_specs.append(e_spec)
        args.append(extra)
    outs, carried = _call(body, grid=grid, in_specs=in_specs, out_specs=[o_spec],
                          out_shape=[jax.ShapeDtypeStruct(out_shape, out_dtype)],
                          scratch_shapes=[pltpu.VMEM(acc_shape, F32)],
                          semantics=("parallel", "parallel", "arbitrary"), name=name, args=args, carry=carry,
                          prefetch=() if a_slots is None else (a_slots,))
    return outs[0] if carry is None else (outs[0], carried)


def _mm_slots(a, wbuf, slots, prev, *, name, carry=None):
    m, kdim = a.shape
    ns = wbuf.shape[2]
    bm, bn = _pick(m, 1024), _pick(ns, 1024)
    npb = ns // bn
    grid = (m // bm, slots.shape[0], npb)

    def body(s_ref, a_ref, b_ref, *rest):
        o_ref = rest[-1]
        o_ref[...] = _dot(a_ref[...], b_ref[...]).astype(o_ref.dtype)

    in_specs = [pl.BlockSpec((bm, kdim), lambda i, s, j, s_ref: (i, 0)),
                pl.BlockSpec((None, kdim, bn), lambda i, s, j, s_ref: (s, 0, j))]
    args = [a, wbuf]
    aliases = None
    if prev is not None:
        in_specs.append(pl.BlockSpec(memory_space=pl.ANY))
        args.append(prev)
        aliases = {2: 0}
    outs, carried = _call(
        body, grid=grid, in_specs=in_specs,
        out_specs=[pl.BlockSpec((bm, bn), lambda i, s, j, s_ref: (i, s_ref[s] * npb + j))],
        out_shape=[jax.ShapeDtypeStruct((m, N_CHIPS * ns), BF16)], scratch_shapes=[],
        semantics=("parallel", "arbitrary", "arbitrary"), name=name, args=args, carry=carry,
        prefetch=(slots,), aliases=aliases)
    return outs[0] if carry is None else (outs[0], carried)


def _row_tile(rows, cols, n_arrays):
    budget = VMEM_LIMIT_BYTES // 3
    cap = min(512, budget // (n_arrays * 2 * cols * 4))
    for t in range(cap - cap % SUBLANES, 0, -SUBLANES):
        if rows % t == 0:
            return t
    return rows


def _norm_mod(x, g, scale, shift, *, name):
    rows, d = x.shape
    tr = _row_tile(rows, d, 3)

    def body(x_ref, g_ref, sc_ref, sh_ref, o_ref):
        xv = x_ref[...]
        r = lax.rsqrt(jnp.mean(xv * xv, axis=-1, keepdims=True) + EPS)
        o_ref[...] = ((xv * r * g_ref[...]) * (1.0 + sc_ref[...]) + sh_ref[...]).astype(o_ref.dtype)

    row = pl.BlockSpec((tr, d), lambda i: (i, 0))
    vec = pl.BlockSpec((1, d), lambda i: (0, 0))
    return pl.pallas_call(
        body, grid=(rows // tr,), in_specs=[row, vec, vec, vec], out_specs=row,
        out_shape=jax.ShapeDtypeStruct((rows, d), BF16),
        compiler_params=_cparams(("parallel",)), name=name)(x, g, scale, shift)


CONV_CHUNK = 2 * SUBLANES


def _shifted_copies(buf, n):
    for r in range(1, SUBLANES):
        buf[r, pl.ds(0, n - SUBLANES), :] = buf[0, pl.ds(r, n - SUBLANES), :]


def _conv_fwd(proj, w_dw, b_dw, ln_g, ln_b, *, cw, carry=None):
    rows = proj.shape[0]
    tt = _pick(rows, 256, HALO)
    hb = tt // HALO

    def body(a_ref, g_ref, ha_ref, hg_ref, w_ref, b_ref, lg_ref, lb_ref, sl_ref, cv_ref, vs):
        i = pl.program_id(0)
        hv = ha_ref[...].astype(F32) * _sigmoid(hg_ref[...].astype(F32))
        vs[0, pl.ds(0, HALO), :] = jnp.where(i == 0, 0.0, hv)
        vs[0, pl.ds(HALO, tt), :] = a_ref[...].astype(F32) * _sigmoid(g_ref[...].astype(F32))
        _shifted_copies(vs, HALO + tt)

        def chunk(ci, carry):
            r0 = pl.multiple_of(ci * CONV_CHUNK, CONV_CHUNK)
            acc = jnp.broadcast_to(b_ref[...], (CONV_CHUNK, cw))
            for k in range(CONV_KERNEL):
                q, r = divmod(HALO - (CONV_KERNEL - 1) + k, SUBLANES)
                acc = acc + w_ref[pl.ds(k, 1), :] * vs[r, pl.ds(r0 + q * SUBLANES, CONV_CHUNK), :]
            cv_ref[pl.ds(r0, CONV_CHUNK), :] = acc
            return carry

        lax.fori_loop(0, tt // CONV_CHUNK, chunk, 0)
        acc = cv_ref[...]
        mu = jnp.mean(acc, axis=-1, keepdims=True)
        xc = acc - mu
        rstd = lax.rsqrt(jnp.mean(xc * xc, axis=-1, keepdims=True) + EPS)
        ln = xc * rstd * lg_ref[...] + lb_ref[...]
        sl_ref[...] = (ln * _sigmoid(ln)).astype(sl_ref.dtype)

    tile = lambda c: pl.BlockSpec((tt, cw), lambda i, c=c: (i, c))
    halo = lambda c: pl.BlockSpec((HALO, cw), lambda i, c=c: (jnp.maximum(i * hb - 1, 0), c))
    vec = pl.BlockSpec((1, cw), lambda i: (0, 0))
    outs, carried = _call(
        body, grid=(rows // tt,),
        in_specs=[tile(0), tile(1), halo(0), halo(1),
                  pl.BlockSpec((CONV_KERNEL, cw), lambda i: (0, 0)), vec, vec, vec],
        out_specs=[pl.BlockSpec((tt, cw), lambda i: (i, 0)), pl.BlockSpec((tt, cw), lambda i: (i, 0))],
        out_shape=[jax.ShapeDtypeStruct((rows, cw), BF16), jax.ShapeDtypeStruct((rows, cw), F32)],
        scratch_shapes=[pltpu.VMEM((SUBLANES, HALO + tt, cw), F32)],
        semantics=("parallel",), name="conv_fwd", args=[proj, proj, proj, proj, w_dw, b_dw, ln_g, ln_b],
        carry=carry)
    return outs if carry is None else (outs, carried)


def _ln_bwd(dsl, cv, ln_g, ln_b):
    rows, cw = cv.shape
    tr = _row_tile(rows, cw, 3)

    def body(d_ref, cv_ref, lg_ref, lb_ref, o_ref, dg_ref, db_ref):
        i = pl.program_id(0)

        @pl.when(i == 0)
        def _():
            dg_ref[...] = jnp.zeros_like(dg_ref)
            db_ref[...] = jnp.zeros_like(db_ref)

        x = cv_ref[...]
        mu = jnp.mean(x, axis=-1, keepdims=True)
        xc = x - mu
        rstd = lax.rsqrt(jnp.mean(xc * xc, axis=-1, keepdims=True) + EPS)
        xh = xc * rstd
        ln = xh * lg_ref[...] + lb_ref[...]
        s = _sigmoid(ln)
        dln = d_ref[...].astype(F32) * (s * (1.0 + ln * (1.0 - s)))
        dg_ref[...] += jnp.sum(dln * xh, axis=0, keepdims=True)
        db_ref[...] += jnp.sum(dln, axis=0, keepdims=True)
        dxh = dln * lg_ref[...]
        m1 = jnp.mean(dxh, axis=-1, keepdims=True)
        m2 = jnp.mean(dxh * xh, axis=-1, keepdims=True)
        o_ref[...] = rstd * (dxh - m1 - xh * m2)

    row = pl.BlockSpec((tr, cw), lambda i: (i, 0))
    vec = pl.BlockSpec((1, cw), lambda i: (0, 0))
    return pl.pallas_call(
        body, grid=(rows // tr,), in_specs=[row, row, vec, vec], out_specs=[row, vec, vec],
        out_shape=[jax.ShapeDtypeStruct((rows, cw), F32), jax.ShapeDtypeStruct((1, cw), F32),
                   jax.ShapeDtypeStruct((1, cw), F32)],
        compiler_params=_cparams(("arbitrary",)), name="ln_bwd")(dsl, cv, ln_g, ln_b)


def _conv_bwd(dcv, proj, w_dw, dproj, *, cw, carry=None):
    rows = proj.shape[0]
    tt = _pick(rows, 256, HALO)
    hb = tt // HALO
    nt = rows // tt
    taps = CONV_KERNEL

    def body(d_ref, dn_ref, a_ref, g_ref, ha_ref, hg_ref, w_ref, dproj_ref, o_ref, dw_ref, db_ref, vs, ds):
        i = pl.program_id(0)

        @pl.when(i == 0)
        def _():
            dw_ref[...] = jnp.zeros_like(dw_ref)
            db_ref[...] = jnp.zeros_like(db_ref)

        hv = ha_ref[...].astype(F32) * _sigmoid(hg_ref[...].astype(F32))
        vs[0, pl.ds(0, HALO), :] = jnp.where(i == 0, 0.0, hv)
        vs[0, pl.ds(HALO, tt), :] = a_ref[...].astype(F32) * _sigmoid(g_ref[...].astype(F32))
        _shifted_copies(vs, HALO + tt)
        ds[0, pl.ds(0, tt), :] = d_ref[...]
        ds[0, pl.ds(tt, HALO), :] = jnp.where(i == nt - 1, 0.0, dn_ref[...])
        _shifted_copies(ds, tt + HALO)
        db_ref[...] += jnp.sum(d_ref[...], axis=0, keepdims=True)
        for k in range(taps):
            q, r = divmod(HALO - (taps - 1) + k, SUBLANES)
            dw_ref[pl.ds(k, 1), :] += jnp.sum(d_ref[...] * vs[r, pl.ds(q * SUBLANES, tt), :], axis=0, keepdims=True)

        def chunk(ci, carry):
            r0 = pl.multiple_of(ci * CONV_CHUNK, CONV_CHUNK)
            dv = jnp.zeros((CONV_CHUNK, cw), F32)
            for k in range(taps):
                q, r = divmod(taps - 1 - k, SUBLANES)
                dv = dv + w_ref[pl.ds(k, 1), :] * ds[r, pl.ds(r0 + q * SUBLANES, CONV_CHUNK), :]
            av = a_ref[pl.ds(r0, CONV_CHUNK), :].astype(F32)
            sg = _sigmoid(g_ref[pl.ds(r0, CONV_CHUNK), :].astype(F32))
            o_ref[pl.ds(r0, CONV_CHUNK), pl.ds(0, cw)] = (dv * sg).astype(o_ref.dtype)
            o_ref[pl.ds(r0, CONV_CHUNK), pl.ds(cw, cw)] = (dv * av * sg * (1.0 - sg)).astype(o_ref.dtype)
            return carry

        lax.fori_loop(0, tt // CONV_CHUNK, chunk, 0)

    tile = lambda c: pl.BlockSpec((tt, cw), lambda i, c=c: (i, c))
    halo = lambda c: pl.BlockSpec((HALO, cw), lambda i, c=c: (jnp.maximum(i * hb - 1, 0), c))
    nxt = pl.BlockSpec((HALO, cw), lambda i: (jnp.minimum((i + 1) * hb, nt * hb - 1), 0))
    outs, carried = _call(
        body, grid=(nt,),
        in_specs=[pl.BlockSpec((tt, cw), lambda i: (i, 0)), nxt, tile(0), tile(1), halo(0), halo(1),
                  pl.BlockSpec((taps, cw), lambda i: (0, 0)), pl.BlockSpec(memory_space=pl.ANY)],
        out_specs=[pl.BlockSpec((tt, 2 * cw), lambda i: (i, 0)),
                   pl.BlockSpec((taps, cw), lambda i: (0, 0)), pl.BlockSpec((1, cw), lambda i: (0, 0))],
        out_shape=[jax.ShapeDtypeStruct(dproj.shape, dproj.dtype), jax.ShapeDtypeStruct((taps, cw), F32),
                   jax.ShapeDtypeStruct((1, cw), F32)],
        scratch_shapes=[pltpu.VMEM((SUBLANES, HALO + tt, cw), F32), pltpu.VMEM((SUBLANES, tt + HALO, cw), F32)],
        semantics=("arbitrary",), name="conv_bwd", args=[dcv, dcv, proj, proj, proj, proj, w_dw, dproj],
        carry=carry, aliases={7: 0})
    return outs if carry is None else (outs, carried)


def _merge_fwd(proj, y_conv, ya, yb, *, cw):
    rows = proj.shape[0]
    tr = _row_tile(rows, cw, 4)

    def body(gc_ref, gs_ref, yc_ref, ya_ref, yb_ref, o_ref):
        ys = ya_ref[...].astype(F32) * _sigmoid(yb_ref[...].astype(F32))
        o_ref[...] = (_sigmoid(gc_ref[...].astype(F32)) * yc_ref[...].astype(F32)
                      + _sigmoid(gs_ref[...].astype(F32)) * ys).astype(o_ref.dtype)

    blk = lambda off: pl.BlockSpec((tr, cw), lambda i, h, off=off: (i, off + h))
    return pl.pallas_call(
        body, grid=(rows // tr, 2), in_specs=[blk(3), blk(5), blk(0), blk(0), blk(0)], out_specs=blk(0),
        out_shape=jax.ShapeDtypeStruct((rows, 2 * cw), BF16),
        compiler_params=_cparams(("parallel", "parallel")), name="merge_fwd")(proj, proj, y_conv, ya, yb)


def _merge_bwd(dmerged, proj, y_conv, ya, yb, *, cw, carry=None):
    rows = proj.shape[0]
    tr = _row_tile(rows, cw, 6)

    def body(d_ref, g_ref, yc_ref, ya_ref, yb_ref, dg_ref, dyc_ref, dya_ref, dyb_ref):
        q = pl.program_id(1)
        d = d_ref[...].astype(F32)
        sg = _sigmoid(g_ref[...].astype(F32))

        @pl.when(q < 2)
        def _():
            dg_ref[...] = (d * yc_ref[...].astype(F32) * sg * (1.0 - sg)).astype(dg_ref.dtype)
            dyc_ref[...] = (d * sg).astype(dyc_ref.dtype)

        @pl.when(q >= 2)
        def _():
            sb = _sigmoid(yb_ref[...].astype(F32))
            yav = ya_ref[...].astype(F32)
            dg_ref[...] = (d * (yav * sb) * sg * (1.0 - sg)).astype(dg_ref.dtype)
            dys = d * sg
            dya_ref[...] = (dys * sb).astype(dya_ref.dtype)
            dyb_ref[...] = (dys * yav * sb * (1.0 - sb)).astype(dyb_ref.dtype)

    spec = lambda f: pl.BlockSpec((tr, cw), lambda i, q, f=f: (i, f(q)))
    conv_half = spec(lambda q: jnp.minimum(q, 1))
    ssm_half = spec(lambda q: jnp.maximum(q - 2, 0))
    o2 = jax.ShapeDtypeStruct((rows, 2 * cw), BF16)
    outs, carried = _call(
        body, grid=(rows // tr, 4),
        in_specs=[spec(lambda q: q % 2), spec(lambda q: 3 + q), conv_half, ssm_half, ssm_half],
        out_specs=[spec(lambda q: 3 + q), conv_half, ssm_half, ssm_half],
        out_shape=[jax.ShapeDtypeStruct((rows, 7 * cw), BF16), o2, o2, o2], scratch_shapes=[],
        semantics=("parallel", "arbitrary"), name="merge_bwd", args=[dmerged, proj, y_conv, ya, yb],
        carry=carry)
    return outs if carry is None else (outs, carried)


def _res_norm(x, mo, gate, g, scale, shift):
    rows, d = x.shape
    tr = _row_tile(rows, d, 4)

    def body(x_ref, mo_ref, gt_ref, g_ref, sc_ref, sh_ref, h_ref, z_ref):
        h = x_ref[...] + gt_ref[...] * mo_ref[...].astype(F32)
        h_ref[...] = h
        r = lax.rsqrt(jnp.mean(h * h, axis=-1, keepdims=True) + EPS)
        z_ref[...] = ((h * r * g_ref[...]) * (1.0 + sc_ref[...]) + sh_ref[...]).astype(z_ref.dtype)

    row = pl.BlockSpec((tr, d), lambda i: (i, 0))
    vec = pl.BlockSpec((1, d), lambda i: (0, 0))
    return pl.pallas_call(
        body, grid=(rows // tr,), in_specs=[row, row, vec, vec, vec, vec], out_specs=[row, row],
        out_shape=[jax.ShapeDtypeStruct((rows, d), F32), jax.ShapeDtypeStruct((rows, d), BF16)],
        compiler_params=_cparams(("parallel",)), name="res_norm")(x, mo, gate, g, scale, shift)


def _final_fwd_bwd(h1, ff, gate2, final_g, target):
    rows, d = h1.shape
    tr = _row_tile(rows, d, 5)

    def body(h_ref, ff_ref, gt_ref, fg_ref, t_ref, dh_ref, dff_ref, loss_ref, dfg_ref, dgt_ref):
        i = pl.program_id(0)

        @pl.when(i == 0)
        def _():
            loss_ref[...] = jnp.zeros_like(loss_ref)
            dfg_ref[...] = jnp.zeros_like(dfg_ref)
            dgt_ref[...] = jnp.zeros_like(dgt_ref)

        ffv = ff_ref[...].astype(F32)
        h2 = h_ref[...] + gt_ref[...] * ffv
        r = lax.rsqrt(jnp.mean(h2 * h2, axis=-1, keepdims=True) + EPS)
        y = h2 * r
        e = y * fg_ref[...] - t_ref[...]
        loss_ref[...] += 0.5 * jnp.sum(jnp.mean(e * e, axis=-1, keepdims=True))
        dout = e * (1.0 / d)
        dfg_ref[...] += jnp.sum(dout * y, axis=0, keepdims=True)
        dy = dout * fg_ref[...]
        dh2 = r * (dy - y * jnp.mean(dy * y, axis=-1, keepdims=True))
        dh_ref[...] = dh2
        dgt_ref[...] += jnp.sum(dh2 * ffv, axis=0, keepdims=True)
        dff_ref[...] = (dh2 * gt_ref[...]).astype(dff_ref.dtype)

    row = pl.BlockSpec((tr, d), lambda i: (i, 0))
    vec = pl.BlockSpec((1, d), lambda i: (0, 0))
    return pl.pallas_call(
        body, grid=(rows // tr,), in_specs=[row, row, vec, vec, row],
        out_specs=[row, row, pl.BlockSpec((1, LANES), lambda i: (0, 0)), vec, vec],
        out_shape=[jax.ShapeDtypeStruct((rows, d), F32), jax.ShapeDtypeStruct((rows, d), BF16),
                   jax.ShapeDtypeStruct((1, LANES), F32), jax.ShapeDtypeStruct((1, d), F32),
                   jax.ShapeDtypeStruct((1, d), F32)],
        compiler_params=_cparams(("arbitrary",)), name="final_fwd_bwd")(h1, ff, gate2, final_g, target)


def _norm_mod_bwd(dz, hin, dres, g, scale, gate, mo, *, name, carry=None):
    rows, d = hin.shape
    with_gate = gate is not None
    tr = _row_tile(rows, d, 6)

    def body(*refs):
        if with_gate:
            (dz_ref, h_ref, dr_ref, g_ref, sc_ref, gt_ref, mo_ref,
             dh_ref, dsh_ref, dsc_ref, dg_ref, dmo_ref, dgt_ref) = refs
        else:
            dz_ref, h_ref, dr_ref, g_ref, sc_ref, dh_ref, dsh_ref, dsc_ref, dg_ref = refs
        i = pl.program_id(0)

        @pl.when(i == 0)
        def _():
            dsh_ref[...] = jnp.zeros_like(dsh_ref)
            dsc_ref[...] = jnp.zeros_like(dsc_ref)
            dg_ref[...] = jnp.zeros_like(dg_ref)
            if with_gate:
                dgt_ref[...] = jnp.zeros_like(dgt_ref)

        dzv = dz_ref[...].astype(F32)
        h = h_ref[...]
        r = lax.rsqrt(jnp.mean(h * h, axis=-1, keepdims=True) + EPS)
        y = h * r
        dsh_ref[...] += jnp.sum(dzv, axis=0, keepdims=True)
        dsc_ref[...] += jnp.sum(dzv * (y * g_ref[...]), axis=0, keepdims=True)
        dn = dzv * (1.0 + sc_ref[...])
        dg_ref[...] += jnp.sum(dn * y, axis=0, keepdims=True)
        dy = dn * g_ref[...]
        dh = dr_ref[...] + r * (dy - y * jnp.mean(dy * y, axis=-1, keepdims=True))
        dh_ref[...] = dh
        if with_gate:
            dmo_ref[...] = (dh * gt_ref[...]).astype(dmo_ref.dtype)
            dgt_ref[...] += jnp.sum(dh * mo_ref[...].astype(F32), axis=0, keepdims=True)

    row = pl.BlockSpec((tr, d), lambda i: (i, 0))
    vec = pl.BlockSpec((1, d), lambda i: (0, 0))
    vshape = jax.ShapeDtypeStruct((1, d), F32)
    in_specs = [row, row, row, vec, vec]
    args = [dz, hin, dres, g, scale]
    out_specs = [row, vec, vec, vec]
    out_shape = [jax.ShapeDtypeStruct((rows, d), F32), vshape, vshape, vshape]
    if with_gate:
        in_specs += [vec, row]
        args += [gate, mo]
        out_specs += [row, vec]
        out_shape += [jax.ShapeDtypeStruct((rows, d), BF16), vshape]
    outs, carried = _call(
        body, grid=(rows // tr,), in_specs=in_specs, out_specs=out_specs, out_shape=out_shape,
        scratch_shapes=[], semantics=("arbitrary",), name=name, args=args, carry=carry)
    return outs if carry is None else (outs, carried)


def _s5_discretise(a_re, a_im, log_dt, b_re, b_im):
    dt = jnp.exp(log_dt)[:, None]
    er = jnp.exp(a_re * dt)
    lr = er * jnp.cos(a_im * dt)
    li = er * jnp.sin(a_im * dt)
    den = a_re * a_re + a_im * a_im
    cr = ((lr - 1.0) * a_re + li * a_im) / den
    ci = (li * a_re - (lr - 1.0) * a_im) / den
    bbr = cr[..., None] * b_re - ci[..., None] * b_im
    bbi = cr[..., None] * b_im + ci[..., None] * b_re
    return lr, li, bbr, bbi


def _block_diag(w):
    g, r, c = w.shape
    nb = g // GROUPS_PER_BLOCK
    eye = jnp.eye(GROUPS_PER_BLOCK, dtype=w.dtype)
    w5 = w.reshape(nb, GROUPS_PER_BLOCK, r, 1, c) * eye[None, :, None, :, None]
    return w5.reshape(nb, GROUPS_PER_BLOCK * r, GROUPS_PER_BLOCK * c)


def _block_diag_extract(m, r, c):
    nb = m.shape[0]
    m5 = m.reshape(nb, GROUPS_PER_BLOCK, r, GROUPS_PER_BLOCK, c)
    idx = jnp.arange(GROUPS_PER_BLOCK)
    d = m5[:, idx, :, idx, :]
    return jnp.moveaxis(d, 0, 1).reshape(nb * GROUPS_PER_BLOCK, r, c)


def _scan_multipliers(lr, li):
    power = jnp.arange(1, SUBLANES + 1, dtype=F32)[None, :, None]
    er = jnp.exp(power * lr)
    pr = er * jnp.cos(power * li)
    pi = er * jnp.sin(power * li)
    rows = jnp.arange(SUBLANES)[None, :, None]
    fr, fi, rr, ri = [], [], [], []
    for s in (1, 2, 4):
        mf = (rows >= s).astype(F32)
        mr = (rows <= SUBLANES - 1 - s).astype(F32)
        fr.append(mf * pr[:, s - 1:s, :])
        fi.append(mf * pi[:, s - 1:s, :])
        rr.append(mr * pr[:, s - 1:s, :])
        ri.append(mr * pi[:, s - 1:s, :])
    fr.append(pr)
    fi.append(pi)
    rr.append(pr[:, ::-1, :])
    ri.append(pi[:, ::-1, :])
    st = lambda xs: jnp.stack(xs, axis=1)
    return st(fr), st(fi), st(rr), st(ri)


def _scan_rows(sre, sim, mul_r, mul_i, n_groups, reverse):
    sgn = -1.0 if reverse else 1.0
    lanes = sre.shape[1]

    def step(k, carry):
        cr, ci = carry
        kk = (n_groups - 1 - k) if reverse else k
        r0 = pl.multiple_of(kk * SUBLANES, SUBLANES)
        xr = sre[pl.ds(r0, SUBLANES), :]
        xi = sim[pl.ds(r0, SUBLANES), :]
        for lvl, s in enumerate((1, 2, 4)):
            sh = (SUBLANES - s) if reverse else s
            nr = pltpu.roll(xr, sh, 0)
            ni = pltpu.roll(xi, sh, 0)
            mr = mul_r[lvl]
            mi = mul_i[lvl] * sgn
            xr, xi = xr + mr * nr - mi * ni, xi + mr * ni + mi * nr
        mr = mul_r[3]
        mi = mul_i[3] * sgn
        xr, xi = xr + mr * cr - mi * ci, xi + mr * ci + mi * cr
        sre[pl.ds(r0, SUBLANES), :] = xr
        sim[pl.ds(r0, SUBLANES), :] = xi
        edge = 0 if reverse else SUBLANES - 1
        ncr = jnp.broadcast_to(xr[edge:edge + 1, :], (SUBLANES, lanes))
        nci = jnp.broadcast_to(xi[edge:edge + 1, :], (SUBLANES, lanes))
        return ncr, nci

    zero = jnp.zeros((SUBLANES, lanes), F32)
    lax.fori_loop(0, n_groups, step, (zero, zero))


def _dot(a, b):
    return jnp.dot(a, b, preferred_element_type=F32)


def _dotf(a, b):
    return _dot(a.astype(BF16), b)


def _s5_operands(lr, li, bbr, bbi, c_re, c_im):
    g = lr.shape[0]
    nb = g // GROUPS_PER_BLOCK
    tb = lambda w: jnp.swapaxes(w, 1, 2)
    b_in = [_block_diag(tb(bbr)), _block_diag(tb(bbi))]
    c_out = [_block_diag(tb(c_re)), _block_diag(tb(c_im))]
    b_out = [_block_diag(bbr), _block_diag(bbi)]
    c_in = [_block_diag(c_re), _block_diag(c_im)]
    lam_r = lr.reshape(nb, 1, STATE_LANES)
    lam_i = li.reshape(nb, 1, STATE_LANES)
    mults = _scan_multipliers(lam_r, lam_i)
    cast = lambda ws: [w.astype(BF16) for w in ws]
    return cast(b_in), cast(c_out), cast(b_out), cast(c_in), mults


def _s5_fwd(proj, d_skip, b_in, c_out, mults, *, col0, carry=None):
    rows = proj.shape[0]
    nb = b_in[0].shape[0]
    tm = _pick(rows, 512, SUBLANES)
    n_tiles = rows // tm
    s_l = STATE_LANES

    def body(u_ref, dk_ref, br, bi, cr, ci, fr_ref, fi_ref, o_ref, sr_ref, si_ref, sre, sim):
        for t in range(n_tiles):
            rs = pl.ds(t * tm, tm)
            ub = u_ref[rs, :]
            sre[rs, :] = _dot(ub, br[...])
            sim[rs, :] = _dot(ub, bi[...])
        _scan_rows(sre, sim, fr_ref, fi_ref, rows // SUBLANES, False)
        for t in range(n_tiles):
            rs = pl.ds(t * tm, tm)
            srb = sre[rs, :].astype(BF16)
            sib = sim[rs, :].astype(BF16)
            sr_ref[rs, :] = srb
            si_ref[rs, :] = sib
            y0 = _dot(srb, cr[...]) - _dot(sib, ci[...])
            y1 = y0 + dk_ref[...] * u_ref[rs, :].astype(F32)
            o_ref[rs, :] = _gelu(y1).astype(o_ref.dtype)

    mat_in = pl.BlockSpec((None, LANES, s_l), lambda g: (g, 0, 0))
    mat_out = pl.BlockSpec((None, s_l, LANES), lambda g: (g, 0, 0))
    mul = pl.BlockSpec((None, 4, SUBLANES, s_l), lambda g: (g, 0, 0, 0))
    state = pl.BlockSpec((rows, s_l), lambda g: (0, g))
    outs, carried = _call(
        body, grid=(nb,),
        in_specs=[pl.BlockSpec((rows, LANES), lambda g: (0, col0 + g)), pl.BlockSpec((1, LANES), lambda g: (0, g))]
        + [mat_in] * 2 + [mat_out] * 2 + [mul] * 2,
        out_specs=[pl.BlockSpec((rows, LANES), lambda g: (0, g)), state, state],
        out_shape=[jax.ShapeDtypeStruct((rows, nb * LANES), BF16), jax.ShapeDtypeStruct((rows, nb * s_l), BF16),
                   jax.ShapeDtypeStruct((rows, nb * s_l), BF16)],
        scratch_shapes=[pltpu.VMEM((rows, s_l), F32), pltpu.VMEM((rows, s_l), F32)],
        semantics=("parallel",), name="s5_fwd", args=[proj, d_skip, *b_in, *c_out, mults[0], mults[1]], carry=carry)
    return outs if carry is None else (outs, carried)


def _s5_bwd(proj, dyg, d_skip, states, c_out, b_out, c_in, mults, dproj, *, col0, carry=None):
    rows = proj.shape[0]
    nb = c_out[0].shape[0]
    tm = _pick(rows, 512, SUBLANES)
    n_tiles = rows // tm
    s_l = STATE_LANES
    n_groups = rows // SUBLANES
    tn = (((0,), (0,)), ((), ()))

    def body(u_ref, dy_ref, dk_ref, sr_ref, si_ref, cr, ci, bor, boi, cir, cii, rr_ref, ri_ref, dproj_ref,
             du_ref, ddk_ref, dbr_ref, dbi_ref, dcr_ref, dci_ref, dlr_ref, dli_ref,
             gre, gim, dy1):
        ddk = jnp.zeros((1, LANES), F32)
        dcr = jnp.zeros((s_l, LANES), F32)
        dci = jnp.zeros((s_l, LANES), F32)
        for t in range(n_tiles):
            rs = pl.ds(t * tm, tm)
            srb = sr_ref[rs, :]
            sib = si_ref[rs, :]
            uf = u_ref[rs, :].astype(F32)
            y0 = _dot(srb, cr[...]) - _dot(sib, ci[...])
            y1 = y0 + dk_ref[...] * uf
            d1 = dy_ref[rs, :].astype(F32) * _gelu_grad(y1)
            dy1[rs, :] = d1
            ddk = ddk + jnp.sum(d1 * uf, axis=0, keepdims=True)
            d1b = d1.astype(BF16)
            dcr = dcr + lax.dot_general(srb, d1b, tn, preferred_element_type=F32)
            dci = dci - lax.dot_general(sib, d1b, tn, preferred_element_type=F32)
            gre[rs, :] = _dot(d1b, cir[...])
            gim[rs, :] = -_dot(d1b, cii[...])
        ddk_ref[...] = ddk
        dcr_ref[...] = dcr
        dci_ref[...] = dci

        last_row = lax.broadcasted_iota(jnp.int32, (SUBLANES, s_l), 0) == SUBLANES - 1

        def group(r0, s_r, s_i, carry):
            cr_, ci_, ar, ai = carry
            xr = gre[pl.ds(r0, SUBLANES), :]
            xi = gim[pl.ds(r0, SUBLANES), :]
            for lvl, s in enumerate((1, 2, 4)):
                nr = pltpu.roll(xr, SUBLANES - s, 0)
                ni = pltpu.roll(xi, SUBLANES - s, 0)
                mr = rr_ref[lvl]
                mi = ri_ref[lvl]
                xr, xi = xr + mr * nr + mi * ni, xi + mr * ni - mi * nr
            mr = rr_ref[3]
            mi = ri_ref[3]
            xr, xi = xr + mr * cr_ + mi * ci_, xi + mr * ci_ - mi * cr_
            gre[pl.ds(r0, SUBLANES), :] = xr
            gim[pl.ds(r0, SUBLANES), :] = xi
            nxt_r = jnp.where(last_row, cr_, pltpu.roll(xr, SUBLANES - 1, 0))
            nxt_i = jnp.where(last_row, ci_, pltpu.roll(xi, SUBLANES - 1, 0))
            ncr = jnp.broadcast_to(xr[0:1, :], (SUBLANES, s_l))
            nci = jnp.broadcast_to(xi[0:1, :], (SUBLANES, s_l))
            return ncr, nci, ar + nxt_r * s_r + nxt_i * s_i, ai + nxt_i * s_r - nxt_r * s_i

        def rev_step(k, carry):
            r0 = pl.multiple_of((n_groups // 2 - 1 - k) * 2 * SUBLANES, 2 * SUBLANES)
            s_r = sr_ref[pl.ds(r0, 2 * SUBLANES), :].astype(F32)
            s_i = si_ref[pl.ds(r0, 2 * SUBLANES), :].astype(F32)
            carry = group(r0 + SUBLANES, s_r[SUBLANES:], s_i[SUBLANES:], carry)
            return group(r0, s_r[:SUBLANES], s_i[:SUBLANES], carry)

        zero = jnp.zeros((SUBLANES, s_l), F32)
        _, _, ar, ai = lax.fori_loop(0, n_groups // 2, rev_step, (zero, zero, zero, zero))
        dlr_ref[...] = jnp.sum(ar, axis=0, keepdims=True)
        dli_ref[...] = jnp.sum(ai, axis=0, keepdims=True)

        dbr = jnp.zeros((LANES, s_l), F32)
        dbi = jnp.zeros((LANES, s_l), F32)
        for t in range(n_tiles):
            rs = pl.ds(t * tm, tm)
            gr = gre[rs, :]
            gi = gim[rs, :]
            grb = gr.astype(BF16)
            gib = gi.astype(BF16)
            du = _dot(grb, bor[...]) + _dot(gib, boi[...]) + dy1[rs, :] * dk_ref[...]
            du_ref[rs, :] = du.astype(du_ref.dtype)
            ub = u_ref[rs, :]
            dbr = dbr + lax.dot_general(ub, grb, tn, preferred_element_type=F32)
            dbi = dbi + lax.dot_general(ub, gib, tn, preferred_element_type=F32)
        dbr_ref[...] = dbr
        dbi_ref[...] = dbi

    mat_in = pl.BlockSpec((None, LANES, s_l), lambda g: (g, 0, 0))
    mat_out = pl.BlockSpec((None, s_l, LANES), lambda g: (g, 0, 0))
    mul = pl.BlockSpec((None, 4, SUBLANES, s_l), lambda g: (g, 0, 0, 0))
    lam = pl.BlockSpec((None, 1, s_l), lambda g: (g, 0, 0))
    col = pl.BlockSpec((rows, LANES), lambda g: (0, g))
    vec = pl.BlockSpec((1, LANES), lambda g: (0, g))
    state = pl.BlockSpec((rows, s_l), lambda g: (0, g))
    outs, carried = _call(
        body, grid=(nb,),
        in_specs=[pl.BlockSpec((rows, LANES), lambda g: (0, col0 + g)), col, vec]
        + [state] * 2 + [mat_out] * 2 + [mat_out] * 2 + [mat_in] * 2 + [mul] * 2
        + [pl.BlockSpec(memory_space=pl.ANY)],
        out_specs=[pl.BlockSpec((rows, LANES), lambda g: (0, col0 + g)), vec, mat_in, mat_in, mat_out, mat_out,
                   lam, lam],
        out_shape=[jax.ShapeDtypeStruct(dproj.shape, dproj.dtype), jax.ShapeDtypeStruct((1, nb * LANES), F32),
                   jax.ShapeDtypeStruct((nb, LANES, s_l), F32), jax.ShapeDtypeStruct((nb, LANES, s_l), F32),
                   jax.ShapeDtypeStruct((nb, s_l, LANES), F32), jax.ShapeDtypeStruct((nb, s_l, LANES), F32),
                   jax.ShapeDtypeStruct((nb, 1, s_l), F32), jax.ShapeDtypeStruct((nb, 1, s_l), F32)],
        scratch_shapes=[pltpu.VMEM((rows, s_l), F32)] * 2 + [pltpu.VMEM((rows, LANES), F32)],
        semantics=("parallel",), name="s5_bwd",
        args=[proj, dyg, d_skip, *states, *c_out, *b_out, *c_in, mults[2], mults[3], dproj], carry=carry,
        aliases={13: 0})
    return outs if carry is None else (outs, carried)


def _silu(v):
    return v * _sigmoid(v)


def _ada_fwd(c_all, w_shard, b_cols):
    d, n = w_shard.shape
    bn = _pick(n, 512)

    def body(c_ref, w_ref, b_ref, o_ref):
        ca = _silu(c_ref[...]).astype(BF16)
        o_ref[...] = _dot(ca, w_ref[...].astype(BF16)) + b_ref[...]

    return pl.pallas_call(
        body, grid=(n // bn,),
        in_specs=[pl.BlockSpec((N_DEV, d), lambda j: (0, 0)), pl.BlockSpec((d, bn), lambda j: (0, j)),
                  pl.BlockSpec((1, bn), lambda j: (0, j))],
        out_specs=pl.BlockSpec((N_DEV, bn), lambda j: (0, j)),
        out_shape=jax.ShapeDtypeStruct((N_DEV, n), F32),
        compiler_params=_cparams(("parallel",)), name="ada_fwd")(c_all, w_shard, b_cols)


def _ada_bwd(c_all, dmod_cols):
    d = c_all.shape[1]
    n = dmod_cols.shape[1]
    bn = _pick(n, 512)

    def body(c_ref, g_ref, o_ref):
        ca = _silu(c_ref[...]).astype(BF16)
        o_ref[...] = lax.dot_general(ca, g_ref[...].astype(BF16), (((0,), (0,)), ((), ())),
                                     preferred_element_type=F32)

    return pl.pallas_call(
        body, grid=(n // bn,),
        in_specs=[pl.BlockSpec((N_DEV, d), lambda j: (0, 0)), pl.BlockSpec((N_DEV, bn), lambda j: (0, j))],
        out_specs=pl.BlockSpec((d, bn), lambda j: (0, j)),
        out_shape=jax.ShapeDtypeStruct((d, n), F32),
        compiler_params=_cparams(("parallel",)), name="ada_bwd")(c_all, dmod_cols)


def _cast_bf16(w, *, name):
    rows, cols = w.shape
    tr = _row_tile(rows, cols, 2)

    def body(w_ref, o_ref):
        o_ref[...] = w_ref[...].astype(BF16)

    row = pl.BlockSpec((tr, cols), lambda i: (i, 0))
    return pl.pallas_call(
        body, grid=(rows // tr,), in_specs=[row], out_specs=row,
        out_shape=jax.ShapeDtypeStruct((rows, cols), BF16),
        compiler_params=_cparams(("parallel",)), name=name)(w)


def _adamw(w, g, m, v, *, name, carry=None):
    rows, cols = w.shape
    tr = _row_tile(rows, cols, 7)
    c1 = 1.0 / (1.0 - ADAM_B1 ** ADAM_STEP)
    c2 = 1.0 / (1.0 - ADAM_B2 ** ADAM_STEP)

    def body(w_ref, g_ref, m_ref, v_ref, d_ref, nm_ref, nv_ref):
        gv = g_ref[...]
        nm = ADAM_B1 * m_ref[...] + (1.0 - ADAM_B1) * gv
        nv = ADAM_B2 * v_ref[...] + (1.0 - ADAM_B2) * (gv * gv)
        nm_ref[...] = nm
        nv_ref[...] = nv
        d_ref[...] = -ADAM_LR * ((nm * c1) / (jnp.sqrt(nv * c2) + ADAM_EPS) + ADAM_WD * w_ref[...])

    row = pl.BlockSpec((tr, cols), lambda i: (i, 0))
    shp = jax.ShapeDtypeStruct((rows, cols), F32)
    outs, carried = _call(
        body, grid=(rows // tr,), in_specs=[row] * 4, out_specs=[row] * 3, out_shape=[shp] * 3,
        scratch_shapes=[], semantics=("parallel",), name=name, args=[w, g, m, v], carry=carry)
    return outs if carry is None else (outs, carried)


def _sum_leading(a, *, name, out_dtype=F32):
    n, rows, cols = a.shape
    tr = _row_tile(rows, cols, n + 1)

    def body(a_ref, o_ref):
        acc = a_ref[0].astype(F32)
        for i in range(1, n):
            acc = acc + a_ref[i].astype(F32)
        o_ref[...] = acc.astype(o_ref.dtype)

    return pl.pallas_call(
        body, grid=(rows // tr,), in_specs=[pl.BlockSpec((n, tr, cols), lambda i: (0, i, 0))],
        out_specs=pl.BlockSpec((tr, cols), lambda i: (i, 0)),
        out_shape=jax.ShapeDtypeStruct((rows, cols), out_dtype),
        compiler_params=_cparams(("parallel",)), name=name)(a)


def _add_half(dw, land, my_c, *, name):
    n, r, cols = dw.shape
    h = r // 2
    tr = _row_tile(h, cols, 3)
    hb = h // tr

    def body(c_ref, a_ref, b_ref, o_ref):
        o_ref[...] = (a_ref[...].astype(F32) + b_ref[...].astype(F32)).astype(o_ref.dtype)

    gs = pltpu.PrefetchScalarGridSpec(
        num_scalar_prefetch=1, grid=(n, hb),
        in_specs=[pl.BlockSpec((None, tr, cols), lambda s, i, c_ref: (s, c_ref[0] * hb + i, 0)),
                  pl.BlockSpec((None, tr, cols), lambda s, i, c_ref: (s, i, 0))],
        out_specs=pl.BlockSpec((None, tr, cols), lambda s, i, c_ref: (s, i, 0)))
    return pl.pallas_call(
        body, grid_spec=gs, out_shape=jax.ShapeDtypeStruct((n, h, cols), BF16),
        compiler_params=_cparams(("parallel", "parallel")), name=name)(my_c, dw, land)


def _mesh_pos():
    return lax.axis_index("x"), lax.axis_index("y"), lax.axis_index("c")


def _other_chips(x, y):
    return [(1 - x, y), (x, 1 - y), (1 - x, 1 - y)]


def _gather_small(blk, *, name):
    m_per, n = blk.shape

    def body(x_ref, out_ref, send_sems, recv_sems, local_sem):
        x, y, c = _mesh_pos()
        me, sibling = (x, y, c), (x, y, 1 - c)
        chips = _other_chips(x, y)

        def rows(px, py, pc):
            return out_ref.at[pl.ds((4 * px + 2 * py + pc) * m_per, m_per), :]

        def copy(k, block, to, src=None):
            return pltpu.make_async_remote_copy(
                src_ref=rows(*block) if src is None else src, dst_ref=rows(*block),
                send_sem=send_sems.at[k], recv_sem=recv_sems.at[k], device_id=to, device_id_type=MESH)

        mine = pltpu.make_async_copy(x_ref, rows(*me), local_sem)
        mine.start()
        first = [copy(0, me, sibling, src=x_ref)]
        first += [copy(1 + j, me, (*chip, c), src=x_ref) for j, chip in enumerate(chips)]
        for cp in first:
            cp.start()
        passed = [copy(4 + j, (*chip, c), sibling) for j, chip in enumerate(chips)]
        for j, chip in enumerate(chips):
            copy(1 + j, (*chip, c), me).wait_recv()
            passed[j].start()
        copy(0, sibling, me).wait_recv()
        for j, chip in enumerate(chips):
            copy(4 + j, (*chip, 1 - c), me).wait_recv()
        for cp in first + passed:
            cp.wait_send()
        mine.wait()

    return pl.pallas_call(
        body, out_shape=jax.ShapeDtypeStruct((N_DEV * m_per, n), blk.dtype),
        in_specs=[pl.BlockSpec(memory_space=pltpu.VMEM)], out_specs=pl.BlockSpec(memory_space=pltpu.VMEM),
        scratch_shapes=[pltpu.SemaphoreType.DMA((7,)), pltpu.SemaphoreType.DMA((7,)), pltpu.SemaphoreType.DMA],
        compiler_params=pltpu.CompilerParams(vmem_limit_bytes=VMEM_LIMIT_BYTES), name=name)(blk)


def _hbm_specs(n):
    return [pl.BlockSpec(memory_space=pl.ANY)] * n


def _gather_weights(shards):
    n = len(shards)

    def body(*refs):
        ins, outs = refs[:n], refs[n:2 * n]
        send_sems, recv_sems, local_sems = refs[2 * n:]
        x, y, c = _mesh_pos()
        me_chip = 2 * x + y
        sibling = (x, y, 1 - c)
        chips = _other_chips(x, y)

        def half(w, chip_idx, pc):
            h = shards[w].shape[0] // 2
            return outs[w].at[chip_idx, pl.ds(pc * h, h), :]

        def copy(w, k, chip_idx, pc, to, src=None):
            dst = half(w, chip_idx, pc)
            return pltpu.make_async_remote_copy(
                src_ref=dst if src is None else src, dst_ref=dst,
                send_sem=send_sems.at[6 * w + k], recv_sem=recv_sems.at[6 * w + k],
                device_id=to, device_id_type=MESH)

        local = [pltpu.make_async_copy(ins[w], outs[w].at[me_chip], local_sems.at[w]) for w in range(n)]
        for cp in local:
            cp.start()
        sends = []
        for w in range(n):
            h = shards[w].shape[0] // 2
            for j, chip in enumerate(chips):
                cp = copy(w, j, me_chip, c, (*chip, c), src=ins[w].at[pl.ds(c * h, h), :])
                cp.start()
                sends.append(cp)
        for w in range(n):
            for j, chip in enumerate(chips):
                chip_idx = 2 * chip[0] + chip[1]
                copy(w, j, chip_idx, c, (x, y, c)).wait_recv()
                cp = copy(w, 3 + j, chip_idx, c, sibling)
                cp.start()
                sends.append(cp)
        for w in range(n):
            for j, chip in enumerate(chips):
                copy(w, 3 + j, 2 * chip[0] + chip[1], 1 - c, (x, y, c)).wait_recv()
        for cp in sends:
            cp.wait_send()
        for cp in local:
            cp.wait()

    return pl.pallas_call(
        body, out_shape=[jax.ShapeDtypeStruct((N_CHIPS,) + s.shape, s.dtype) for s in shards],
        in_specs=_hbm_specs(n), out_specs=_hbm_specs(n),
        scratch_shapes=[pltpu.SemaphoreType.DMA((6 * n,)), pltpu.SemaphoreType.DMA((6 * n,)),
                        pltpu.SemaphoreType.DMA((n,))],
        name="gather_weights")(*shards)


def _swap_halves(dws, *, name):
    n = len(dws)

    def body(*refs):
        ins, outs = refs[:n], refs[n:2 * n]
        send_sems, recv_sems = refs[2 * n:]
        x, y, c = _mesh_pos()
        cps = []
        for w in range(n):
            h = dws[w].shape[1] // 2
            cp = pltpu.make_async_remote_copy(
                src_ref=ins[w].at[:, pl.ds((1 - c) * h, h), :], dst_ref=outs[w],
                send_sem=send_sems.at[w], recv_sem=recv_sems.at[w],
                device_id=(x, y, 1 - c), device_id_type=MESH)
            cp.start()
            cps.append(cp)
        for cp in cps:
            cp.wait()

    return pl.pallas_call(
        body, out_shape=[jax.ShapeDtypeStruct((s.shape[0], s.shape[1] // 2, s.shape[2]), s.dtype) for s in dws],
        in_specs=_hbm_specs(n), out_specs=_hbm_specs(n),
        scratch_shapes=[pltpu.SemaphoreType.DMA((n,)), pltpu.SemaphoreType.DMA((n,))],
        name=name)(*dws)


def _chip_exchange(parts):
    n = len(parts)

    def body(*refs):
        ins, outs = refs[:n], refs[n:2 * n]
        send_sems, recv_sems, local_sems = refs[2 * n:]
        x, y, c = _mesh_pos()
        me_chip = 2 * x + y
        chips = _other_chips(x, y)
        local = [pltpu.make_async_copy(ins[w].at[me_chip], outs[w].at[me_chip], local_sems.at[w]) for w in range(n)]
        for cp in local:
            cp.start()
        cps = []
        for w in range(n):
            for j, chip in enumerate(chips):
                cp = pltpu.make_async_remote_copy(
                    src_ref=ins[w].at[2 * chip[0] + chip[1]], dst_ref=outs[w].at[me_chip],
                    send_sem=send_sems.at[3 * w + j], recv_sem=recv_sems.at[3 * w + j],
                    device_id=(*chip, c), device_id_type=MESH)
                cp.start()
                cps.append((cp, w, j, chip))
        for cp, w, j, chip in cps:
            slot = outs[w].at[2 * chip[0] + chip[1]]
            pltpu.make_async_remote_copy(
                src_ref=slot, dst_ref=slot, send_sem=send_sems.at[3 * w + j], recv_sem=recv_sems.at[3 * w + j],
                device_id=(x, y, c), device_id_type=MESH).wait_recv()
        for cp, _, _, _ in cps:
            cp.wait_send()
        for cp in local:
            cp.wait()

    return pl.pallas_call(
        body, out_shape=[jax.ShapeDtypeStruct(s.shape, s.dtype) for s in parts],
        in_specs=_hbm_specs(n), out_specs=_hbm_specs(n),
        scratch_shapes=[pltpu.SemaphoreType.DMA((3 * n,)), pltpu.SemaphoreType.DMA((3 * n,)),
                        pltpu.SemaphoreType.DMA((n,))],
        name="chip_exchange")(*parts)


def _join_halves(halves):
    n = len(halves)

    def body(*refs):
        ins, outs = refs[:n], refs[n:2 * n]
        send_sems, recv_sems, local_sems = refs[2 * n:]
        x, y, c = _mesh_pos()
        cps, local = [], []
        for w in range(n):
            h = halves[w].shape[0]
            mine = outs[w].at[pl.ds(c * h, h), :]
            lc = pltpu.make_async_copy(ins[w], mine, local_sems.at[w])
            lc.start()
            local.append(lc)
            cp = pltpu.make_async_remote_copy(
                src_ref=ins[w], dst_ref=mine, send_sem=send_sems.at[w], recv_sem=recv_sems.at[w],
                device_id=(x, y, 1 - c), device_id_type=MESH)
            cp.start()
            cps.append(cp)
        for w in range(n):
            h = halves[w].shape[0]
            theirs = outs[w].at[pl.ds((1 - c) * h, h), :]
            pltpu.make_async_remote_copy(
                src_ref=theirs, dst_ref=theirs, send_sem=send_sems.at[w], recv_sem=recv_sems.at[w],
                device_id=(x, y, c), device_id_type=MESH).wait_recv()
        for cp in cps:
            cp.wait_send()
        for lc in local:
            lc.wait()

    return pl.pallas_call(
        body, out_shape=[jax.ShapeDtypeStruct((2 * s.shape[0], s.shape[1]), s.dtype) for s in halves],
        in_specs=_hbm_specs(n), out_specs=_hbm_specs(n),
        scratch_shapes=[pltpu.SemaphoreType.DMA((n,)), pltpu.SemaphoreType.DMA((n,)), pltpu.SemaphoreType.DMA((n,))],
        name="join_halves")(*halves)


def _cast_into_slot(w, chip, after, *, name):
    rows, cols = w.shape
    tr = _row_tile(rows, cols, 2)

    def body(chip_ref, w_ref, after_ref, o_ref):
        o_ref[...] = w_ref[...].astype(BF16)

    gs = pltpu.PrefetchScalarGridSpec(
        num_scalar_prefetch=1, grid=(rows // tr,),
        in_specs=[pl.BlockSpec((tr, cols), lambda i, chip_ref: (i, 0)), pl.BlockSpec(memory_space=pl.ANY)],
        out_specs=pl.BlockSpec((None, tr, cols), lambda i, chip_ref: (chip_ref[0], i, 0)))
    return pl.pallas_call(
        body, grid_spec=gs, out_shape=jax.ShapeDtypeStruct((N_CHIPS, rows, cols), BF16),
        compiler_params=_cparams(("parallel",)), name=name)(chip, w, after)


def _row_range(h, lo, hi, parts):
    step = h // parts
    assert step * parts == h and step % (2 * SUBLANES) == 0, (h, parts)
    return lo * step, (hi - lo) * step


def _gather_carry(items):
    n_copies = sum(len(js) for _, js, _, _, _ in items)
    sem = pltpu.SemaphoreType.DMA((2 * n_copies,))

    def copies(outs, sems):
        send_sems, recv_sems = sems
        x, y, c = _mesh_pos()
        me_chip = 2 * x + y
        chips = _other_chips(x, y)
        out_ici, in_ici, out_d2d, in_d2d = [], [], [], []
        k = 0
        for w, (buf, js, lo, hi, parts) in enumerate(items):
            h = buf.shape[1] // 2
            r0, nr = _row_range(h, lo, hi, parts)

            def copy(k, chip_idx, pc, to):
                ref = outs[w].at[chip_idx, pl.ds(pc * h + r0, nr), :]
                return pltpu.make_async_remote_copy(
                    src_ref=ref, dst_ref=ref, send_sem=send_sems.at[k], recv_sem=recv_sems.at[k],
                    device_id=to, device_id_type=MESH)

            for j in js:
                chip = chips[j]
                chip_idx = 2 * chip[0] + chip[1]
                out_ici.append(copy(k, me_chip, c, (*chip, c)))
                in_ici.append(copy(k, chip_idx, c, (x, y, c)))
                out_d2d.append(copy(k + 1, chip_idx, c, (x, y, 1 - c)))
                in_d2d.append(copy(k + 1, chip_idx, 1 - c, (x, y, c)))
                k += 2
        return out_ici, in_ici, out_d2d, in_d2d

    def start(ins, outs, sems):
        for cp in copies(outs, sems)[0]:
            cp.start()

    def finish(ins, outs, sems):
        out_ici, in_ici, out_d2d, in_d2d = copies(outs, sems)
        for arrived, onward in zip(in_ici, out_d2d):
            arrived.wait_recv()
            onward.start()
        for arrived in in_d2d:
            arrived.wait_recv()
        for cp in out_ici + out_d2d:
            cp.wait_send()

    bufs = [it[0] for it in items]
    shapes = [jax.ShapeDtypeStruct(b.shape, b.dtype) for b in bufs]
    return _Carry(bufs, shapes, {i: i for i in range(len(bufs))}, [sem, sem], start, finish)


def _exchange_carry(items):
    n = len(items)
    sem = pltpu.SemaphoreType.DMA((3 * n,))
    given = [w for w in range(n) if items[w][1] is not None]

    def copies(ins, outs, sems):
        send_sems, recv_sems = sems
        x, y, c = _mesh_pos()
        chips = _other_chips(x, y)
        sends, recvs = [], []
        for w, (part, _, lo, hi, parts) in enumerate(items):
            r0, nr = _row_range(part.shape[1], lo, hi, parts)
            for j, chip in enumerate(chips):
                land = outs[w].at[j, pl.ds(r0, nr), :]
                sends.append(pltpu.make_async_remote_copy(
                    src_ref=ins[w].at[2 * chip[0] + chip[1], pl.ds(r0, nr), :], dst_ref=land,
                    send_sem=send_sems.at[3 * w + j], recv_sem=recv_sems.at[3 * w + j],
                    device_id=(*chip, c), device_id_type=MESH))
                recvs.append(pltpu.make_async_remote_copy(
                    src_ref=land, dst_ref=land,
                    send_sem=send_sems.at[3 * w + j], recv_sem=recv_sems.at[3 * w + j],
                    device_id=(x, y, c), device_id_type=MESH))
        return sends, recvs

    def start(ins, outs, sems):
        for cp in copies(ins, outs, sems)[0]:
            cp.start()

    def finish(ins, outs, sems):
        sends, recvs = copies(ins, outs, sems)
        for cp in recvs:
            cp.wait_recv()
        for cp in sends:
            cp.wait_send()

    inputs = [it[0] for it in items] + [items[w][1] for w in given]
    shapes = [jax.ShapeDtypeStruct((3,) + it[0].shape[1:], it[0].dtype) for it in items]
    aliases = {n + i: w for i, w in enumerate(given)}
    return _Carry(inputs, shapes, aliases, [sem, sem], start, finish)


def _sum_into_half(part, landed, chip, my_c, *, name):
    _, h, cols = part.shape
    tr = _row_tile(h, cols, 5)
    hb = h // tr

    def body(chip_ref, c_ref, p_ref, l_ref, o_ref):
        acc = p_ref[...].astype(F32)
        for j in range(3):
            acc = acc + l_ref[j].astype(F32)
        o_ref[...] = acc

    gs = pltpu.PrefetchScalarGridSpec(
        num_scalar_prefetch=2, grid=(hb,),
        in_specs=[pl.BlockSpec((None, tr, cols), lambda i, chip_ref, c_ref: (chip_ref[0], i, 0)),
                  pl.BlockSpec((3, tr, cols), lambda i, chip_ref, c_ref: (0, i, 0))],
        out_specs=pl.BlockSpec((tr, cols), lambda i, chip_ref, c_ref: (c_ref[0] * hb + i, 0)))
    return pl.pallas_call(
        body, grid_spec=gs, out_shape=jax.ShapeDtypeStruct((2 * h, cols), F32),
        compiler_params=_cparams(("parallel",)), name=name)(chip, my_c, part, landed)


def _join_carry(fulls):
    n = len(fulls)
    sem = pltpu.SemaphoreType.DMA((n,))

    def copies(outs, sems):
        send_sems, recv_sems = sems
        x, y, c = _mesh_pos()
        sends, recvs = [], []
        for w in range(n):
            h = fulls[w].shape[0] // 2
            mine = outs[w].at[pl.ds(c * h, h), :]
            theirs = outs[w].at[pl.ds((1 - c) * h, h), :]
            sends.append(pltpu.make_async_remote_copy(
                src_ref=mine, dst_ref=mine, send_sem=send_sems.at[w], recv_sem=recv_sems.at[w],
                device_id=(x, y, 1 - c), device_id_type=MESH))
            recvs.append(pltpu.make_async_remote_copy(
                src_ref=theirs, dst_ref=theirs, send_sem=send_sems.at[w], recv_sem=recv_sems.at[w],
                device_id=(x, y, c), device_id_type=MESH))
        return sends, recvs

    def start(ins, outs, sems):
        for cp in copies(outs, sems)[0]:
            cp.start()

    def finish(ins, outs, sems):
        sends, recvs = copies(outs, sems)
        for cp in recvs:
            cp.wait_recv()
        for cp in sends:
            cp.wait_send()

    shapes = [jax.ShapeDtypeStruct(f.shape, f.dtype) for f in fulls]
    return _Carry(fulls, shapes, {i: i for i in range(n)}, [sem, sem], start, finish)


class _NoComm:
    def __init__(self, big):
        self.big = big
        self.grads = {}

    def weight(self, name):
        return self.big[name]

    def mm_in(self, u, afters):
        return _mm(u, self.big["w_in"], mode="nn", out_dtype=BF16, name="mm_in")

    def mm_d_in(self, dproj):
        return _mm(dproj, self.big["w_in"], mode="nt", out_dtype=F32, name="mm_d_in")

    def carry(self, site):
        return None

    def done(self, site, carried):
        pass

    def grad(self, name, dw):
        self.grads[name] = dw

    def early_grads(self, early):
        self.early = early


def _gather_rows_carry(blk):
    m_per = blk.shape[0]
    sem = pltpu.SemaphoreType.DMA((7,))

    def copies(ins, outs, sems):
        send_sems, recv_sems, local_sem = sems
        x, y, c = _mesh_pos()
        me, sibling = (x, y, c), (x, y, 1 - c)
        chips = _other_chips(x, y)

        def rows(px, py, pc):
            return outs[0].at[pl.ds((4 * px + 2 * py + pc) * m_per, m_per), :]

        def copy(k, block, to, src=None):
            return pltpu.make_async_remote_copy(
                src_ref=rows(*block) if src is None else src, dst_ref=rows(*block),
                send_sem=send_sems.at[k], recv_sem=recv_sems.at[k], device_id=to, device_id_type=MESH)

        mine = pltpu.make_async_copy(ins[0], rows(*me), local_sem.at[0])
        first = [copy(0, me, sibling, src=ins[0])]
        first += [copy(1 + j, me, (*chip, c), src=ins[0]) for j, chip in enumerate(chips)]
        passed = [copy(4 + j, (*chip, c), sibling) for j, chip in enumerate(chips)]
        landed = [copy(1 + j, (*chip, c), me) for j, chip in enumerate(chips)]
        from_sibling = [copy(0, sibling, me)] + [copy(4 + j, (*chip, 1 - c), me) for j, chip in enumerate(chips)]
        return mine, first, passed, landed, from_sibling

    def start(ins, outs, sems):
        mine, first, _, _, _ = copies(ins, outs, sems)
        mine.start()
        for cp in first:
            cp.start()

    def finish(ins, outs, sems):
        mine, first, passed, landed, from_sibling = copies(ins, outs, sems)
        for arrived, onward in zip(landed, passed):
            arrived.wait_recv()
            onward.start()
        for arrived in from_sibling:
            arrived.wait_recv()
        for cp in first + passed:
            cp.wait_send()
        mine.wait()

    shape = jax.ShapeDtypeStruct((N_DEV * m_per, blk.shape[1]), blk.dtype)
    return _Carry([blk], [shape], {}, [sem, sem, pltpu.SemaphoreType.DMA((1,))], start, finish)


def _gather_fresh_carry(own, js):
    n = len(js)
    h = own.shape[0] // 2
    sem = pltpu.SemaphoreType.DMA((2 * n,))

    def copies(ins, outs, sems):
        send_sems, recv_sems = sems
        x, y, c = _mesh_pos()
        chips = _other_chips(x, y)
        out_ici, in_ici, out_d2d, in_d2d = [], [], [], []

        def copy(k, src, dst, to):
            return pltpu.make_async_remote_copy(
                src_ref=src, dst_ref=dst, send_sem=send_sems.at[k], recv_sem=recv_sems.at[k],
                device_id=to, device_id_type=MESH)

        for jj, j in enumerate(js):
            mine = ins[0].at[pl.ds(c * h, h), :]
            land = outs[0].at[jj, pl.ds(c * h, h), :]
            other = outs[0].at[jj, pl.ds((1 - c) * h, h), :]
            out_ici.append(copy(2 * jj, mine, land, (*chips[j], c)))
            in_ici.append(copy(2 * jj, land, land, (x, y, c)))
            out_d2d.append(copy(2 * jj + 1, land, land, (x, y, 1 - c)))
            in_d2d.append(copy(2 * jj + 1, other, other, (x, y, c)))
        return out_ici, in_ici, out_d2d, in_d2d

    def start(ins, outs, sems):
        for cp in copies(ins, outs, sems)[0]:
            cp.start()

    def finish(ins, outs, sems):
        out_ici, in_ici, out_d2d, in_d2d = copies(ins, outs, sems)
        for arrived, onward in zip(in_ici, out_d2d):
            arrived.wait_recv()
            onward.start()
        for arrived in in_d2d:
            arrived.wait_recv()
        for cp in out_ici + out_d2d:
            cp.wait_send()

    return _Carry([own], [jax.ShapeDtypeStruct((n,) + own.shape, own.dtype)], {}, [sem, sem], start, finish)


def _w_in_copies(own_ref, land_ref, send_sems, recv_sems):
    x, y, c = _mesh_pos()
    h = own_ref.shape[0] // 2
    return [pltpu.make_async_remote_copy(
        src_ref=own_ref.at[pl.ds(c * h, h), :], dst_ref=land_ref.at[j, pl.ds(c * h, h), :],
        send_sem=send_sems[j], recv_sem=recv_sems[j], device_id=(*chip, c), device_id_type=MESH)
        for j, chip in enumerate(_other_chips(x, y))]


def _w_in_send(own, after):
    hbm = pl.BlockSpec(memory_space=pltpu.HBM)
    sem = pl.BlockSpec(memory_space=pltpu.SEMAPHORE)
    land_shape = (3,) + own.shape

    def body(own_ref, land_ref, after_ref, s0, s1, s2, r0, r1, r2, own_thru, land_thru, token):
        for cp in _w_in_copies(own_ref, land_ref, (s0, s1, s2), (r0, r1, r2)):
            cp.start()
        token[...] = jnp.zeros_like(token)

    outs = pl.pallas_call(
        body, name="w_in_send",
        out_shape=(pltpu.SemaphoreType.DMA(()),) * 6 + (
            pltpu.HBM(own.shape, own.dtype), pltpu.HBM(land_shape, own.dtype), jax.ShapeDtypeStruct((8, LANES), F32)),
        in_specs=(hbm, hbm, pl.BlockSpec(memory_space=pl.ANY)),
        out_specs=(sem,) * 6 + (hbm, hbm, pl.BlockSpec(memory_space=pltpu.VMEM)),
        input_output_aliases={0: 6, 1: 7},
        compiler_params=pltpu.CompilerParams(has_side_effects=pltpu.SideEffectType.DATAFLOW_SIDE_EFFECTING),
    )(pltpu.with_memory_space_constraint(own, pltpu.HBM),
      pltpu.with_memory_space_constraint(lax.empty(land_shape, own.dtype), pltpu.HBM), after)
    return outs[:6], outs[6], outs[7], outs[8]


def _w_in_wait(sems, own, land, afters):
    hbm = pl.BlockSpec(memory_space=pltpu.HBM)
    sem = pl.BlockSpec(memory_space=pltpu.SEMAPHORE)
    n_after = len(afters)

    def body(own_ref, land_ref, s0, s1, s2, r0, r1, r2, *rest):
        for cp in _w_in_copies(own_ref, land_ref, (s0, s1, s2), (r0, r1, r2)):
            cp.wait_send()
            cp.wait_recv()

    return pl.pallas_call(
        body, name="w_in_wait", out_shape=(pltpu.HBM(own.shape, own.dtype), pltpu.HBM(land.shape, land.dtype)),
        in_specs=(hbm, hbm) + (sem,) * 6 + (pl.BlockSpec(memory_space=pl.ANY),) * n_after, out_specs=(hbm, hbm),
        input_output_aliases={0: 0, 1: 1},
        compiler_params=pltpu.CompilerParams(has_side_effects=pltpu.SideEffectType.DATAFLOW_SIDE_EFFECTING),
    )(own, land, *sems, *afters)


def _exchange_copies(part_refs, land_refs, send_sems, recv_sems):
    x, y, c = _mesh_pos()
    cps = []
    for w, (part, land) in enumerate(zip(part_refs, land_refs)):
        for j, chip in enumerate(_other_chips(x, y)):
            cps.append(pltpu.make_async_remote_copy(
                src_ref=part.at[2 * chip[0] + chip[1]], dst_ref=land.at[j],
                send_sem=send_sems[3 * w + j], recv_sem=recv_sems[3 * w + j],
                device_id=(*chip, c), device_id_type=MESH))
    return cps


def _exchange_send(parts, *, name):
    n = len(parts)
    hbm = pl.BlockSpec(memory_space=pltpu.HBM)
    sem = pl.BlockSpec(memory_space=pltpu.SEMAPHORE)
    land_shapes = [(3,) + p.shape[1:] for p in parts]

    def body(*refs):
        part_refs, land_refs = refs[:n], refs[n:2 * n]
        sems = refs[2 * n:8 * n]
        for cp in _exchange_copies(part_refs, land_refs, sems[:3 * n], sems[3 * n:]):
            cp.start()

    outs = pl.pallas_call(
        body, name=name,
        out_shape=(pltpu.SemaphoreType.DMA(()),) * (6 * n)
        + tuple(pltpu.HBM(p.shape, p.dtype) for p in parts)
        + tuple(pltpu.HBM(s, p.dtype) for s, p in zip(land_shapes, parts)),
        in_specs=(hbm,) * (2 * n), out_specs=(sem,) * (6 * n) + (hbm,) * (2 * n),
        input_output_aliases={i: 6 * n + i for i in range(2 * n)},
        compiler_params=pltpu.CompilerParams(has_side_effects=pltpu.SideEffectType.DATAFLOW_SIDE_EFFECTING),
    )(*[pltpu.with_memory_space_constraint(p, pltpu.HBM) for p in parts],
      *[pltpu.with_memory_space_constraint(lax.empty(s, p.dtype), pltpu.HBM) for s, p in zip(land_shapes, parts)])
    return outs[:6 * n], outs[6 * n:7 * n], outs[7 * n:]


def _exchange_wait(sems, parts, lands, afters, *, name):
    n = len(parts)
    hbm = pl.BlockSpec(memory_space=pltpu.HBM)
    sem = pl.BlockSpec(memory_space=pltpu.SEMAPHORE)

    def body(*refs):
        part_refs, land_refs = refs[:n], refs[n:2 * n]
        sem_refs = refs[2 * n:8 * n]
        for cp in _exchange_copies(part_refs, land_refs, sem_refs[:3 * n], sem_refs[3 * n:]):
            cp.wait_send()
            cp.wait_recv()

    outs = pl.pallas_call(
        body, name=name,
        out_shape=tuple(pltpu.HBM(p.shape, p.dtype) for p in parts) + tuple(pltpu.HBM(l.shape, l.dtype) for l in lands),
        in_specs=(hbm,) * (2 * n) + (sem,) * (6 * n) + (pl.BlockSpec(memory_space=pl.ANY),) * len(afters),
        out_specs=(hbm,) * (2 * n), input_output_aliases={i: i for i in range(2 * n)},
        compiler_params=pltpu.CompilerParams(has_side_effects=pltpu.SideEffectType.DATAFLOW_SIDE_EFFECTING),
    )(*parts, *lands, *sems, *afters)
    return outs[:n], outs[n:]


def _forward_carry(land):
    n = land.shape[0]
    h = land.shape[1] // 2
    sem = pltpu.SemaphoreType.DMA((n,))

    def copies(outs, sems):
        send_sems, recv_sems = sems
        x, y, c = _mesh_pos()
        sends, recvs = [], []
        for j in range(n):
            mine = outs[0].at[j, pl.ds(c * h, h), :]
            other = outs[0].at[j, pl.ds((1 - c) * h, h), :]
            sends.append(pltpu.make_async_remote_copy(
                src_ref=mine, dst_ref=mine, send_sem=send_sems.at[j], recv_sem=recv_sems.at[j],
                device_id=(x, y, 1 - c), device_id_type=MESH))
            recvs.append(pltpu.make_async_remote_copy(
                src_ref=other, dst_ref=other, send_sem=send_sems.at[j], recv_sem=recv_sems.at[j],
                device_id=(x, y, c), device_id_type=MESH))
        return sends, recvs

    def start(ins, outs, sems):
        for cp in copies(outs, sems)[0]:
            cp.start()

    def finish(ins, outs, sems):
        sends, recvs = copies(outs, sems)
        for cp in recvs:
            cp.wait_recv()
        for cp in sends:
            cp.wait_send()

    return _Carry([land], [jax.ShapeDtypeStruct(land.shape, land.dtype)], {0: 0}, [sem, sem], start, finish)


def _swap_carry(dws):
    n = len(dws)
    sem = pltpu.SemaphoreType.DMA((n,))

    def copies(ins, outs, sems):
        send_sems, recv_sems = sems
        x, y, c = _mesh_pos()
        cps = []
        for w in range(n):
            h = dws[w].shape[1] // 2
            cps.append(pltpu.make_async_remote_copy(
                src_ref=ins[w].at[:, pl.ds((1 - c) * h, h), :], dst_ref=outs[w],
                send_sem=send_sems.at[w], recv_sem=recv_sems.at[w],
                device_id=(x, y, 1 - c), device_id_type=MESH))
        return cps

    def start(ins, outs, sems):
        for cp in copies(ins, outs, sems):
            cp.start()

    def finish(ins, outs, sems):
        for cp in copies(ins, outs, sems):
            cp.wait()

    shapes = [jax.ShapeDtypeStruct((s.shape[0], s.shape[1] // 2, s.shape[2]), s.dtype) for s in dws]
    return _Carry(dws, shapes, {}, [sem, sem], start, finish)


def _merge_carries(carries):
    if len(carries) == 1:
        return carries[0]
    inputs, out_shapes, sem_shapes, aliases, spans = [], [], [], {}, []
    for cy in carries:
        i0, o0, s0 = len(inputs), len(out_shapes), len(sem_shapes)
        aliases.update({i0 + i: o0 + o for i, o in cy.aliases.items()})
        inputs += cy.inputs
        out_shapes += cy.out_shapes
        sem_shapes += cy.sem_shapes
        spans.append((slice(i0, len(inputs)), slice(o0, len(out_shapes)), slice(s0, len(sem_shapes))))

    def start(ins, outs, sems):
        for cy, (si, so, ss) in zip(carries, spans):
            cy.start(ins[si], outs[so], sems[ss])

    def finish(ins, outs, sems):
        for cy, (si, so, ss) in zip(carries, spans):
            cy.finish(ins[si], outs[so], sems[ss])

    return _Carry(inputs, out_shapes, aliases, sem_shapes, start, finish)


ALL_CHIPS = (0, 1, 2)


class _MeshComm:
    GATHER_AT = {
        "mm_in_rest": [("w_conv_out", ALL_CHIPS, 0, 1, 1), ("w_glu_a", ALL_CHIPS, 0, 1, 1),
                       ("w_glu_b", ALL_CHIPS, 0, 1, 1)],
        "conv_fwd": [("w_out", ALL_CHIPS, 0, 1, 1), ("w_ff1", ALL_CHIPS, 0, 1, 8)],
        "s5_fwd": [("w_ff1", ALL_CHIPS, 1, 6, 8)],
        "mm_glu_a": [("w_ff1", ALL_CHIPS, 6, 7, 8)],
        "mm_glu_b": [("w_ff1", ALL_CHIPS, 7, 8, 8)],
        "mm_out": [("w_ff2", ALL_CHIPS, 0, 2, 8)],
        "mm_ff1": [("w_ff2", ALL_CHIPS, 2, 8, 8)],
    }
    SWAP_AT = {
        "mm_d_ff2": ["w_ff2"],
        "mm_d_ff1": ["w_ff1"],
        "conv_bwd": ["w_out", "w_glu_a", "w_glu_b", "w_conv_out"],
    }
    EXCHANGE_AT = {}
    EARLY_AT = "mm_dw_in"

    def __init__(self, shards, pos, chip, my_c):
        self.pos = pos
        self.chip = chip
        self.my_c = my_c
        self.shards = shards
        self.w_in_own = _cast_bf16(shards["w_in"], name="cast_w_in")
        self.raw = {}
        self.flights = []
        self.parts = {}
        self.landing = {}
        self.halves = {}
        self.pending = {}
        self.last_site = {}
        for site, items in self.EXCHANGE_AT.items():
            for it in items:
                self.last_site[it[0]] = site

    def weight(self, name):
        g = self.bufs[name]
        return g.reshape(g.shape[0] * g.shape[1], g.shape[2]) if name in ROW_SHARDED else g

    def _slot_ids(self):
        x, y, _ = self.pos
        ids = [2 * x + y] + [2 * cx + cy for cx, cy in _other_chips(x, y)]
        return jnp.stack(ids).astype(jnp.int32)

    def start_w_in(self, after):
        *self.w_in_flight, token = _w_in_send(self.w_in_own, after)
        self.bufs = {n: _cast_into_slot(s, self.chip, token, name="cast_" + n)
                     for n, s in self.shards.items() if n != "w_in"}
        return token

    def mm_in(self, u, afters):
        ids = self._slot_ids()
        sems, own, land = self.w_in_flight
        proj = _mm_slots(u, own[None], ids[0:1], None, name="mm_in_own")
        own, land = _w_in_wait(sems, own, land, [proj] + list(self.bufs.values()) + list(afters))
        land, = _run_carry(_forward_carry(land), name="forward_w_in")
        proj, carried = _mm_slots(u, land, ids[1:4], proj, name="mm_in_rest", carry=self.carry("mm_in_rest"))
        self.done("mm_in_rest", carried)
        self.w_in_rel = jnp.concatenate([own[None], land], axis=0)
        return proj

    def _add_and_send(self, names, landed, site):
        parts = [_add_half(self.raw.pop(n), l1, self.my_c, name="add_half_" + n) for n, l1 in zip(names, landed)]
        self.flights.append((names,) + _exchange_send(parts, name="exchange_send_" + site))

    def mm_d_in(self, dproj):
        landed = _run_carry(_swap_carry([self.raw["w_in"]]), name="swap_halves_w_in")
        self._add_and_send(["w_in"], landed, "w_in")
        return _mm(dproj, self.w_in_rel, mode="nt", out_dtype=F32, name="mm_d_in", a_slots=self._slot_ids())

    def early_grads(self, early):
        self.early = early

    def carry(self, site):
        jobs = []
        if site == self.EARLY_AT:
            flat, self.early_offs = _pack(list(self.early.values()))
            jobs.append(("early", None, _gather_rows_carry(flat.reshape(-1, PACK_COLS))))
        if site in self.GATHER_AT:
            items = self.GATHER_AT[site]
            jobs.append(("gather", items, _gather_carry([(self.bufs[it[0]],) + tuple(it[1:]) for it in items])))
        if site in self.EXCHANGE_AT:
            items = self.EXCHANGE_AT[site]
            jobs.append(("exchange", items, _exchange_carry(
                [(self.parts[it[0]], self.landing.get(it[0])) + tuple(it[1:]) for it in items])))
        if site in self.SWAP_AT:
            names = self.SWAP_AT[site]
            jobs.append(("swap", names, _swap_carry([self.raw[n] for n in names])))
        if not jobs:
            return None
        self.pending[site] = jobs
        return _merge_carries([job[2] for job in jobs])

    def done(self, site, carried):
        pos = 0
        for kind, items, carry in self.pending.pop(site):
            outs = carried[pos:pos + len(carry.out_shapes)]
            pos += len(carry.out_shapes)
            if kind == "early":
                self.early_all = outs[0]
            elif kind == "gather":
                self.bufs.update(zip([it[0] for it in items], outs))
            elif kind == "swap":
                self._add_and_send(items, outs, site)
            else:
                for it, landed in zip(items, outs):
                    n = it[0]
                    self.landing[n] = landed
                    if self.last_site[n] == site:
                        self.halves[n] = _sum_into_half(self.parts.pop(n), self.landing.pop(n), self.chip,
                                                        self.my_c, name="sum_chips_" + n)

    def grad(self, name, dw):
        if name in ROW_SHARDED:
            dw = dw.reshape(N_CHIPS, dw.shape[0] // N_CHIPS, dw.shape[1])
        self.raw[name] = dw

    def join(self, names, afters, *, name):
        for i, (group, sems, parts, lands) in enumerate(self.flights):
            parts, lands = _exchange_wait(sems, parts, lands, afters, name="exchange_wait_%d" % i)
            for n, part, land in zip(group, parts, lands):
                self.halves[n] = _sum_into_half(part, land, self.chip, self.my_c, name="sum_chips_" + n)
        self.flights = []
        return dict(zip(names, _run_carry(_join_carry([self.halves.pop(n) for n in names]), name=name)))


def _local_step(x, target, mod, small, comm):
    rows, d = x.shape
    cw = d // 2
    shift1, scale1, gate1, shift2, scale2, gate2 = mod
    _, _, bbr, bbi = small["s5_disc"]
    b_in, c_out, b_out, c_in, mults = _s5_operands(*small["s5_loglam"], bbr, bbi, small["c_re"], small["c_im"])
    wt = comm.weight

    def riding(site, fn, *args, **kwargs):
        carry = comm.carry(site)
        if carry is None:
            return fn(*args, **kwargs)
        out, carried = fn(*args, carry=carry, **kwargs)
        comm.done(site, carried)
        return out

    u = _norm_mod(x, small["norm1_g"], scale1, shift1, name="norm1_fwd")
    proj = comm.mm_in(u, [*b_in, *c_out, *b_out, *c_in, *mults])
    sl, cv = riding("conv_fwd", _conv_fwd, proj, small["w_dw"], small["b_dw"], small["ln_g"], small["ln_b"], cw=cw)
    y_conv = _mm(sl, wt("w_conv_out"), mode="nn", out_dtype=BF16, name="mm_conv_out")
    yg, st_re, st_im = riding("s5_fwd", _s5_fwd, proj, small["d_skip"], b_in, c_out, mults, col0=2 * cw // LANES)
    ya = riding("mm_glu_a", _mm, yg, wt("w_glu_a"), mode="nn", out_dtype=BF16, name="mm_glu_a")
    yb = riding("mm_glu_b", _mm, yg, wt("w_glu_b"), mode="nn", out_dtype=BF16, name="mm_glu_b")
    merged = _merge_fwd(proj, y_conv, ya, yb, cw=cw)
    mo = riding("mm_out", _mm, merged, wt("w_out"), mode="nn", out_dtype=BF16, name="mm_out")
    h1, z = _res_norm(x, mo, gate1, small["norm2_g"], scale2, shift2)
    f1 = riding("mm_ff1", _mm, z, wt("w_ff1"), mode="nn", out_dtype=BF16, name="mm_ff1")
    ff = _mm(f1, wt("w_ff2"), mode="nn", out_dtype=BF16, name="mm_ff2", a_fn=_relu2_bf16)
    dh2, dff, loss, d_final_g, d_gate2 = _final_fwd_bwd(h1, ff, gate2, small["final_g"], target)

    comm.grad("w_ff2", _mm(f1, dff, mode="tn", out_dtype=BF16, name="mm_dw_ff2", a_fn=_relu2_bf16))
    df1 = riding("mm_d_ff2", _mm, dff, wt("w_ff2"), mode="nt", out_dtype=BF16, name="mm_d_ff2", extra=f1,
                 epi=lambda acc, f: acc * (2.0 * jnp.maximum(f.astype(F32), 0.0)))
    comm.grad("w_ff1", riding("mm_dw_ff1", _mm, z, df1, mode="tn", out_dtype=BF16, name="mm_dw_ff1",
                              out_gathered=True))
    dz = riding("mm_d_ff1", _mm, df1, wt("w_ff1"), mode="nt", out_dtype=F32, name="mm_d_ff1")
    dh1, d_shift2, d_scale2, d_norm2_g, dmo, d_gate1 = riding(
        "norm2_bwd", _norm_mod_bwd, dz, h1, dh2, small["norm2_g"], scale2, gate1, mo, name="norm2_bwd")
    comm.grad("w_out", riding("mm_dw_out", _mm, merged, dmo, mode="tn", out_dtype=BF16, name="mm_dw_out"))
    dmerged = riding("mm_d_out", _mm, dmo, wt("w_out"), mode="nt", out_dtype=BF16, name="mm_d_out")
    dproj, dy_conv, dya, dyb = riding("merge_bwd", _merge_bwd, dmerged, proj, y_conv, ya, yb, cw=cw)
    comm.grad("w_glu_a", _mm(yg, dya, mode="tn", out_dtype=BF16, name="mm_dw_glu_a", out_gathered=True))
    comm.grad("w_glu_b", _mm(yg, dyb, mode="tn", out_dtype=BF16, name="mm_dw_glu_b", out_gathered=True))
    dyg_a = _mm(dya, wt("w_glu_a"), mode="nt", out_dtype=F32, name="mm_d_glu_a")
    dyg = _mm(dyb, wt("w_glu_b"), mode="nt", out_dtype=F32, name="mm_d_glu_b", extra=dyg_a,
              epi=lambda acc, e: acc + e)
    comm.grad("w_conv_out", _mm(sl, dy_conv, mode="tn", out_dtype=BF16, name="mm_dw_conv_out", out_gathered=True))
    dsl = _mm(dy_conv, wt("w_conv_out"), mode="nt", out_dtype=F32, name="mm_d_conv_out")
    dcv, d_ln_g, d_ln_b = _ln_bwd(dsl, cv, small["ln_g"], small["ln_b"])
    dproj, d_w_dw, d_b_dw = riding("conv_bwd", _conv_bwd, dcv, proj, small["w_dw"], dproj, cw=cw)
    dproj, d_d_skip, dbr, dbi, dcr, dci, dlr, dli = riding(
        "s5_bwd", _s5_bwd, proj, dyg, small["d_skip"], (st_re, st_im), c_out, b_out, c_in, mults, dproj,
        col0=2 * cw // LANES)
    sw = lambda m: jnp.swapaxes(m, 1, 2)
    early = {
        "dmod_tail": jnp.concatenate([d_gate1, d_shift2, d_scale2, d_gate2], axis=1), "loss": loss[:, 0:1],
        "w_dw": d_w_dw, "b_dw": d_b_dw, "ln_g": d_ln_g, "ln_b": d_ln_b,
        "lam_re": dlr.reshape(-1, SSM_STATE), "lam_im": dli.reshape(-1, SSM_STATE),
        "bb_re": sw(_block_diag_extract(dbr, SSM_GROUP, SSM_STATE)),
        "bb_im": sw(_block_diag_extract(dbi, SSM_GROUP, SSM_STATE)),
        "c_re": sw(_block_diag_extract(dcr, SSM_STATE, SSM_GROUP)),
        "c_im": sw(_block_diag_extract(dci, SSM_STATE, SSM_GROUP)),
        "d_skip": d_d_skip, "norm2_g": d_norm2_g, "final_g": d_final_g,
    }
    comm.early_grads(early)
    comm.grad("w_in", riding("mm_dw_in", _mm, u, dproj, mode="tn", out_dtype=BF16, name="mm_dw_in",
                             out_gathered=True))
    du = comm.mm_d_in(dproj)
    grad_x, d_shift1, d_scale1, d_norm1_g = riding(
        "norm1_bwd", _norm_mod_bwd, du, x, dh1, small["norm1_g"], scale1, None, None, name="norm1_bwd")
    late ={"dmod_head": jnp.concatenate([d_shift1, d_scale1], axis=1), "norm1_g": d_norm1_g}
    return grad_x, early, late


WEIGHT_NAMES = ["w_ada", "b_ada", "norm1_g", "w_in", "w_dw", "b_dw", "ln_g", "ln_b", "w_conv_out", "a_re", "a_im",
                "log_dt", "b_re", "b_im", "c_re", "c_im", "d_skip", "w_glu_a", "w_glu_b", "w_out", "norm2_g",
                "w_ff1", "w_ff2", "final_g"]
BIG_NAMES = ["w_in", "w_conv_out", "w_glu_a", "w_glu_b", "w_out", "w_ff1", "w_ff2"]
ROW_SHARDED = ("w_out", "w_ff2")
PACK_COLS = 1024
PACK_TILE = SUBLANES * PACK_COLS


def _pack(arrays):
    flats = [a.reshape(-1) for a in arrays]
    offs = []
    total = 0
    for f in flats:
        offs.append(total)
        total += f.shape[0]
    pad = (-total) % PACK_TILE
    if pad:
        flats.append(jnp.zeros((pad,), F32))
    return jnp.concatenate(flats), offs


def _unpack(flat, offs, like):
    return [flat[o:o + a.size].reshape(a.shape) for o, a in zip(offs, like)]


def _gather_w_dw(w_shard):
    k, n = w_shard.shape
    padded = jnp.pad(w_shard, ((0, HALO - k), (0, 0)))
    allw = _gather_small(padded, name="gather_w_dw").reshape(N_CHIPS, 2, HALO, n)[:, 0, :k]
    return jnp.moveaxis(allw, 0, 1).reshape(k, N_CHIPS * n)


def kernel(x, c, w_ada, b_ada, norm1_g, w_in, w_dw, b_dw, ln_g, ln_b, w_conv_out, a_re, a_im, log_dt, b_re, b_im, c_re, c_im, d_skip, w_glu_a, w_glu_b, w_out, norm2_g, w_ff1, w_ff2, final_g, loss_target, m_w_ada, m_b_ada, m_norm1_g, m_w_in, m_w_dw, m_b_dw, m_ln_g, m_ln_b, m_w_conv_out, m_a_re, m_a_im, m_log_dt, m_b_re, m_b_im, m_c_re, m_c_im, m_d_skip, m_w_glu_a, m_w_glu_b, m_w_out, m_norm2_g, m_w_ff1, m_w_ff2, m_final_g, v_w_ada, v_b_ada, v_norm1_g, v_w_in, v_w_dw, v_b_dw, v_ln_g, v_ln_b, v_w_conv_out, v_a_re, v_a_im, v_log_dt, v_b_re, v_b_im, v_c_re, v_c_im, v_d_skip, v_w_glu_a, v_w_glu_b, v_w_out, v_norm2_g, v_w_ff1, v_w_ff2, v_final_g):
    given = dict(locals())
    w = {n: given[n] for n in WEIGHT_NAMES}
    m = {n: given["m_" + n] for n in WEIGHT_NAMES}
    v = {n: given["v_" + n] for n in WEIGHT_NAMES}
    d = x.shape[2]
    xi, yi, ci = _mesh_pos()
    chip = 2 * xi + yi
    dev = 4 * xi + 2 * yi + ci
    my_c = jnp.reshape(ci, (1,)).astype(jnp.int32)
    chip_arr = jnp.reshape(chip, (1,)).astype(jnp.int32)

    comm = _MeshComm({n: w[n][0] for n in BIG_NAMES}, (xi, yi, ci), chip_arr, my_c)

    ndw = w_dw.shape[2]
    assert d // SUBLANES == ndw
    first = jnp.concatenate([c.reshape(SUBLANES, ndw), jnp.pad(w_dw[0], ((0, HALO - CONV_KERNEL), (0, 0)))])
    first_all = _gather_small(first, name="gather_c_w_dw").reshape(N_DEV, SUBLANES + HALO, ndw)
    c_all = first_all[:, :SUBLANES].reshape(N_DEV, d)
    taps = first_all.reshape(N_CHIPS, 2, SUBLANES + HALO, ndw)[:, 0, SUBLANES:SUBLANES + CONV_KERNEL]
    w_dw_full = jnp.moveaxis(taps, 0, 1).reshape(CONV_KERNEL, N_CHIPS * ndw)

    nmod = w_ada.shape[2]
    b_cols = lax.dynamic_slice(b_ada, (0, chip * nmod), (1, nmod))
    mod_part = _ada_fwd(c_all, w_ada[0], b_cols)
    mod_all = _gather_small(mod_part, name="gather_mod").reshape(N_CHIPS, 2, N_DEV, nmod)[:, 0]
    mod_full = jnp.moveaxis(mod_all, 0, 1).reshape(N_DEV, N_CHIPS * nmod)
    mod_row = lax.dynamic_slice(mod_full, (dev, 0), (1, N_CHIPS * nmod))
    mod = [mod_row[:, i * d:(i + 1) * d] for i in range(6)]

    token = comm.start_w_in(mod_row)
    log_dt_0 = log_dt[0] + token[0, 0]

    disc_in = (a_re[0], a_im[0], log_dt_0, b_re[0], b_im[0])
    disc, disc_vjp = jax.vjp(_s5_discretise, *disc_in)
    dt = jnp.exp(log_dt_0)[:, None]
    small = {"norm1_g": norm1_g, "w_dw": w_dw_full, "b_dw": b_dw, "ln_g": ln_g, "ln_b": ln_b,
             "c_re": c_re[0], "c_im": c_im[0], "d_skip": d_skip, "norm2_g": norm2_g,
             "final_g": final_g[None, :], "s5_disc": disc, "s5_loglam": (a_re[0] * dt, a_im[0] * dt)}

    grad_x, early, late = _local_step(x[0], loss_target[0], mod, small, comm)
    grads = {}

    early_all = comm.early_all.reshape(N_DEV, -1, PACK_COLS)
    early_sum = _sum_leading(early_all, name="sum_small_grads").reshape(-1)
    summed = dict(zip(early, _unpack(early_sum, comm.early_offs, list(early.values()))))
    flat, late_offs = _pack(list(late.values()))
    late_all = _gather_small(flat.reshape(-1, PACK_COLS), name="gather_late_grads").reshape(N_DEV, -1, PACK_COLS)
    late_sum = _sum_leading(late_all, name="sum_late_grads").reshape(-1)
    summed.update(zip(late, _unpack(late_sum, late_offs, list(late.values()))))
    head = late_all[:, :2 * d // PACK_COLS].reshape(N_DEV, 2 * d)
    tail = early_all[:, :4 * d // PACK_COLS].reshape(N_DEV, 4 * d)
    dmod_all = jnp.concatenate([head, tail], axis=1)

    grads["w_ada"] = _ada_bwd(c_all, lax.dynamic_slice(dmod_all, (0, chip * nmod), (N_DEV, nmod)))
    grads["b_ada"] = _sum_leading(dmod_all.reshape(N_DEV, SUBLANES, 6 * d // SUBLANES),
                                  name="sum_b_ada").reshape(1, 6 * d)
    da_re, da_im, dlog_dt, db_re, db_im = disc_vjp(
        (summed["lam_re"], summed["lam_im"], summed["bb_re"], summed["bb_im"]))
    grads.update({
        "norm1_g": summed["norm1_g"], "w_dw": lax.dynamic_slice(summed["w_dw"], (0, chip * ndw), (CONV_KERNEL, ndw)),
        "b_dw": summed["b_dw"], "ln_g": summed["ln_g"], "ln_b": summed["ln_b"],
        "a_re": da_re, "a_im": da_im, "log_dt": dlog_dt, "b_re": db_re, "b_im": db_im,
        "c_re": summed["c_re"], "c_im": summed["c_im"], "d_skip": summed["d_skip"],
        "norm2_g": summed["norm2_g"], "final_g": summed["final_g"],
    })

    delta, new_m, new_v = {}, {}, {}
    grads.update(comm.join(BIG_NAMES, [late_all], name="join_halves"))
    for n in ["w_ada"] + BIG_NAMES:
        shp = w[n].shape
        two_d = lambda a: a.reshape(shp[1], shp[2])
        res = _adamw(two_d(w[n]), two_d(grads[n]), two_d(m[n]), two_d(v[n]), name="adamw_" + n)
        delta[n], new_m[n], new_v[n] = [r.reshape(shp) for r in res]
    grads = {n: grads[n].reshape(w[n].shape) for n in WEIGHT_NAMES}
    rest = [n for n in WEIGHT_NAMES if n not in delta]
    packs = []
    for src in (w, grads, m, v):
        flat, offs = _pack([src[n] for n in rest])
        packs.append(flat.reshape(-1, 1024))
    outs = _adamw(*packs, name="adamw_small")
    for dst, o in zip((delta, new_m, new_v), outs):
        for n, a in zip(rest, _unpack(o.reshape(-1), offs, [w[k] for k in rest])):
            dst[n] = a

    return (summed["loss"].reshape(()), grad_x[None], *[grads[n] for n in WEIGHT_NAMES],
            *[delta[n] for n in WEIGHT_NAMES], *[new_m[n] for n in WEIGHT_NAMES],
            *[new_v[n] for n in WEIGHT_NAMES])
```

```python
import functools
import math

import jax
import jax.numpy as jnp
from jax import lax
from jax.experimental import pallas as pl
from jax.experimental.pallas import tpu as pltpu

F32 = jnp.float32
BF16 = jnp.bfloat16
EPS = 1e-6
CONV_KERNEL = 31
SSM_GROUP = 16
SSM_STATE = 64
ADAM_LR = 0.001
ADAM_B1 = 0.9
ADAM_B2 = 0.999
ADAM_EPS = 1e-08
ADAM_WD = 0.01
ADAM_STEP = 10

N_CHIPS = 4
N_DEV = 8
VMEM_LIMIT_BYTES = 56 * 1024 * 1024
LANES = 128
SUBLANES = 8
HALO = 32
GROUPS_PER_BLOCK = LANES // SSM_GROUP
STATE_LANES = GROUPS_PER_BLOCK * SSM_STATE
MESH = pl.DeviceIdType.MESH


def _cparams(sem):
    return pltpu.CompilerParams(dimension_semantics=sem, vmem_limit_bytes=VMEM_LIMIT_BYTES)


def _pick(n, pref, mult=LANES):
    if n <= pref:
        return n
    best = None
    for d in range(mult, pref + 1, mult):
        if n % d == 0:
            best = d
    assert best is not None, (n, pref)
    return best


def _sigmoid(v):
    return 1.0 / (1.0 + jnp.exp(-v))


def _gelu_parts(v):
    k0 = math.sqrt(2.0 / math.pi)
    inner = k0 * (v + 0.044715 * v * v * v)
    t = jnp.tanh(inner)
    return k0, t


def _gelu(v):
    _, t = _gelu_parts(v)
    return 0.5 * v * (1.0 + t)


def _gelu_grad(v):
    k0, t = _gelu_parts(v)
    return 0.5 * (1.0 + t) + 0.5 * v * (1.0 - t * t) * k0 * (1.0 + 3.0 * 0.044715 * v * v)


def _relu2_bf16(a):
    t = jnp.maximum(a.astype(F32), 0.0)
    return (t * t).astype(BF16)


class _Carry:
    def __init__(self, inputs, out_shapes, aliases, sem_shapes, start, finish):
        self.inputs = list(inputs)
        self.out_shapes = list(out_shapes)
        self.aliases = dict(aliases)
        self.sem_shapes = list(sem_shapes)
        self.start = start
        self.finish = finish


def _call(body, *, grid, in_specs, out_specs, out_shape, scratch_shapes, semantics, name, args, carry=None,
          prefetch=(), aliases=None):
    n_in, n_out, n_scr, n_pf = len(in_specs), len(out_specs), len(scratch_shapes), len(prefetch)
    own_aliases = {n_pf + i: o for i, o in (aliases or {}).items()}
    if carry is None:
        gs = pltpu.PrefetchScalarGridSpec(
            num_scalar_prefetch=n_pf, grid=grid, in_specs=in_specs, out_specs=out_specs,
            scratch_shapes=scratch_shapes)
        outs = pl.pallas_call(
            body, grid_spec=gs, out_shape=out_shape, input_output_aliases=own_aliases,
            compiler_params=_cparams(semantics), name=name)(*prefetch, *args)
        return list(outs), []
    ci, co = len(carry.inputs), len(carry.out_shapes)

    def wrapped(*refs):
        pf, refs = refs[:n_pf], refs[n_pf:]
        ins, cins = refs[:n_in], refs[n_in:n_in + ci]
        p = n_in + ci
        outs, couts = refs[p:p + n_out], refs[p + n_out:p + n_out + co]
        p += n_out + co
        scr, csems = refs[p:p + n_scr], refs[p + n_scr:]
        first = pl.program_id(0) == 0
        last = pl.program_id(0) == grid[0] - 1
        for ax in range(1, len(grid)):
            first = jnp.logical_and(first, pl.program_id(ax) == 0)
            last = jnp.logical_and(last, pl.program_id(ax) == grid[ax] - 1)

        @pl.when(first)
        def _():
            carry.start(cins, couts, csems)

        body(*pf, *ins, *outs, *scr)

        @pl.when(last)
        def _():
            carry.finish(cins, couts, csems)

    any_spec = pl.BlockSpec(memory_space=pl.ANY)
    gs = pltpu.PrefetchScalarGridSpec(
        num_scalar_prefetch=n_pf, grid=grid, in_specs=list(in_specs) + [any_spec] * ci,
        out_specs=list(out_specs) + [any_spec] * co, scratch_shapes=list(scratch_shapes) + carry.sem_shapes)
    all_aliases = dict(own_aliases)
    all_aliases.update({n_pf + n_in + i: n_out + o for i, o in carry.aliases.items()})
    outs = pl.pallas_call(
        wrapped, grid_spec=gs, out_shape=list(out_shape) + carry.out_shapes, input_output_aliases=all_aliases,
        compiler_params=_cparams(("arbitrary",) * len(grid)), name=name)(*prefetch, *args, *carry.inputs)
    return list(outs[:n_out]), list(outs[n_out:])


def _run_carry(carry, *, name):
    ci = len(carry.inputs)

    def body(*refs):
        cins, couts, csems = refs[:ci], refs[ci:ci + len(carry.out_shapes)], refs[ci + len(carry.out_shapes):]
        carry.start(cins, couts, csems)
        carry.finish(cins, couts, csems)

    any_spec = pl.BlockSpec(memory_space=pl.ANY)
    outs = pl.pallas_call(
        body, in_specs=[any_spec] * ci, out_specs=[any_spec] * len(carry.out_shapes), out_shape=carry.out_shapes,
        scratch_shapes=carry.sem_shapes, input_output_aliases=carry.aliases, name=name)(*carry.inputs)
    return list(outs)


def _mm(a, b, *, mode, out_dtype, name, out_gathered=False, a_fn=None, epi=None, extra=None,
        bm_pref=1024, bn_pref=1024, bk_pref=2048, carry=None, a_slots=None):
    gathered = (b.ndim == 3)
    if mode == "nn":
        m, kdim = a.shape
        ns = b.shape[-1]
        n = ns * (N_CHIPS if gathered else 1)
        bm, bn, bk = _pick(m, bm_pref), _pick(ns, bn_pref), _pick(kdim, bk_pref)
        npb = ns // bn
        grid = (m // bm, n // bn, kdim // bk)
        a_spec = pl.BlockSpec((bm, bk), lambda i, j, k: (i, k))
        if gathered:
            b_spec = pl.BlockSpec((None, bk, bn), lambda i, j, k: (j // npb, k, j % npb))
        else:
            b_spec = pl.BlockSpec((bk, bn), lambda i, j, k: (k, j))
        o_spec = pl.BlockSpec((bm, bn), lambda i, j, k: (i, j))
        e_spec = pl.BlockSpec((bm, bn), lambda i, j, k: (i, j))
        out_shape = (m, n)
        acc_shape = (bm, bn)
        dims = (((1,), (0,)), ((), ()))
    elif mode == "nt":
        m = a.shape[0]
        kdim, ns = b.shape[-2], b.shape[-1]
        n = ns * (N_CHIPS if gathered else 1)
        assert a.shape[1] == n
        bm, bko, bnr = _pick(m, bm_pref), _pick(kdim, bn_pref), _pick(ns, bk_pref)
        npb = ns // bnr
        grid = (m // bm, kdim // bko, n // bnr)
        a_spec = pl.BlockSpec((bm, bnr), lambda i, j, k: (i, k))
        if gathered:
            b_spec = pl.BlockSpec((None, bko, bnr), lambda i, j, k: (k // npb, j, k % npb))
        else:
            b_spec = pl.BlockSpec((bko, bnr), lambda i, j, k: (j, k))
        o_spec = pl.BlockSpec((bm, bko), lambda i, j, k: (i, j))
        e_spec = pl.BlockSpec((bm, bko), lambda i, j, k: (i, j))
        if a_slots is not None:
            assert gathered and extra is None
            a_spec = pl.BlockSpec((bm, bnr), lambda i, j, k, s_ref: (i, s_ref[k // npb] * npb + k % npb))
            b_spec = pl.BlockSpec((None, bko, bnr), lambda i, j, k, s_ref: (k // npb, j, k % npb))
            o_spec = pl.BlockSpec((bm, bko), lambda i, j, k, s_ref: (i, j))
        out_shape = (m, kdim)
        acc_shape = (bm, bko)
        dims = (((1,), (1,)), ((), ()))
    else:
        m, kdim = a.shape
        n = b.shape[1]
        ns = n // N_CHIPS if out_gathered else n
        bmr, bko, bn = _pick(m, bk_pref), _pick(kdim, bm_pref), _pick(ns, bn_pref)
        npb = ns // bn
        grid = (kdim // bko, n // bn, m // bmr)
        a_spec = pl.BlockSpec((bmr, bko), lambda i, j, k: (k, i))
        b_spec = pl.BlockSpec((bmr, bn), lambda i, j, k: (k, j))
        if out_gathered:
            o_spec = pl.BlockSpec((None, bko, bn), lambda i, j, k: (j // npb, i, j % npb))
            out_shape = (N_CHIPS, kdim, ns)
        else:
            o_spec = pl.BlockSpec((bko, bn), lambda i, j, k: (i, j))
            out_shape = (kdim, n)
        e_spec = None
        acc_shape = (bko, bn)
        dims = (((0,), (0,)), ((), ()))
    nk = grid[2]

    def body(*refs):
        if a_slots is not None:
            refs = refs[1:]
        if extra is not None:
            a_ref, b_ref, e_ref, o_ref, acc = refs
        else:
            a_ref, b_ref, o_ref, acc = refs
            e_ref = None
        k = pl.program_id(2)
        av = a_ref[...]
        if a_fn is not None:
            av = a_fn(av)
        part = lax.dot_general(av, b_ref[...], dims, preferred_element_type=F32)

        def finish(r):
            if epi is not None:
                r = epi(r, e_ref[...])
            o_ref[...] = r.astype(o_ref.dtype)

        if nk == 1:
            finish(part)
            return

        @pl.when(k == 0)
        def _():
            acc[...] = part

        @pl.when(jnp.logical_and(k > 0, k < nk - 1))
        def _():
            acc[...] += part

        @pl.when(k == nk - 1)
        def _():
            finish(acc[...] + part)

    in_specs = [a_spec, b_spec]
    args = [a, b]
    if extra is not None:
        in_specs.append(e_spec)
        args.append(extra)
    outs, carried = _call(body, grid=grid, in_specs=in_specs, out_specs=[o_spec],
                          out_shape=[jax.ShapeDtypeStruct(out_shape, out_dtype)],
                          scratch_shapes=[pltpu.VMEM(acc_shape, F32)],
                          semantics=("parallel", "parallel", "arbitrary"), name=name, args=args, carry=carry,
                          prefetch=() if a_slots is None else (a_slots,))
    return outs[0] if carry is None else (outs[0], carried)


def _mm_slots(a, wbuf, slots, prev, *, name, carry=None):
    m, kdim = a.shape
    ns = wbuf.shape[2]
    bm, bn = _pick(m, 1024), _pick(ns, 1024)
    npb = ns // bn
    grid = (m // bm, slots.shape[0], npb)

    def body(s_ref, a_ref, b_ref, *rest):
        o_ref = rest[-1]
        o_ref[...] = _dot(a_ref[...], b_ref[...]).astype(o_ref.dtype)

    in_specs = [pl.BlockSpec((bm, kdim), lambda i, s, j, s_ref: (i, 0)),
                pl.BlockSpec((None, kdim, bn), lambda i, s, j, s_ref: (s, 0, j))]
    args = [a, wbuf]
    aliases = None
    if prev is not None:
        in_specs.append(pl.BlockSpec(memory_space=pl.ANY))
        args.append(prev)
        aliases = {2: 0}
    outs, carried = _call(
        body, grid=grid, in_specs=in_specs,
        out_specs=[pl.BlockSpec((bm, bn), lambda i, s, j, s_ref: (i, s_ref[s] * npb + j))],
        out_shape=[jax.ShapeDtypeStruct((m, N_CHIPS * ns), BF16)], scratch_shapes=[],
        semantics=("parallel", "arbitrary", "arbitrary"), name=name, args=args, carry=carry,
        prefetch=(slots,), aliases=aliases)
    return outs[0] if carry is None else (outs[0], carried)


def _row_tile(rows, cols, n_arrays):
    budget = VMEM_LIMIT_BYTES // 3
    cap = min(512, budget // (n_arrays * 2 * cols * 4))
    for t in range(cap - cap % SUBLANES, 0, -SUBLANES):
        if rows % t == 0:
            return t
    return rows


def _norm_mod(x, g, scale, shift, *, name):
    rows, d = x.shape
    tr = _row_tile(rows, d, 3)

    def body(x_ref, g_ref, sc_ref, sh_ref, o_ref):
        xv = x_ref[...]
        r = lax.rsqrt(jnp.mean(xv * xv, axis=-1, keepdims=True) + EPS)
        o_ref[...] = ((xv * r * g_ref[...]) * (1.0 + sc_ref[...]) + sh_ref[...]).astype(o_ref.dtype)

    row = pl.BlockSpec((tr, d), lambda i: (i, 0))
    vec = pl.BlockSpec((1, d), lambda i: (0, 0))
    return pl.pallas_call(
        body, grid=(rows // tr,), in_specs=[row, vec, vec, vec], out_specs=row,
        out_shape=jax.ShapeDtypeStruct((rows, d), BF16),
        compiler_params=_cparams(("parallel",)), name=name)(x, g, scale, shift)


CONV_CHUNK = 2 * SUBLANES


def _shifted_copies(buf, n):
    for r in range(1, SUBLANES):
        buf[r, pl.ds(0, n - SUBLANES), :] = buf[0, pl.ds(r, n - SUBLANES), :]


def _conv_fwd(proj, w_dw, b_dw, ln_g, ln_b, *, cw, carry=None):
    rows = proj.shape[0]
    tt = _pick(rows, 256, HALO)
    hb = tt // HALO

    def body(a_ref, g_ref, ha_ref, hg_ref, w_ref, b_ref, lg_ref, lb_ref, sl_ref, cv_ref, vs):
        i = pl.program_id(0)
        hv = ha_ref[...].astype(F32) * _sigmoid(hg_ref[...].astype(F32))
        vs[0, pl.ds(0, HALO), :] = jnp.where(i == 0, 0.0, hv)
        vs[0, pl.ds(HALO, tt), :] = a_ref[...].astype(F32) * _sigmoid(g_ref[...].astype(F32))
        _shifted_copies(vs, HALO + tt)

        def chunk(ci, carry):
            r0 = pl.multiple_of(ci * CONV_CHUNK, CONV_CHUNK)
            acc = jnp.broadcast_to(b_ref[...], (CONV_CHUNK, cw))
            for k in range(CONV_KERNEL):
                q, r = divmod(HALO - (CONV_KERNEL - 1) + k, SUBLANES)
                acc = acc + w_ref[pl.ds(k, 1), :] * vs[r, pl.ds(r0 + q * SUBLANES, CONV_CHUNK), :]
            cv_ref[pl.ds(r0, CONV_CHUNK), :] = acc
            return carry

        lax.fori_loop(0, tt // CONV_CHUNK, chunk, 0)
        acc = cv_ref[...]
        mu = jnp.mean(acc, axis=-1, keepdims=True)
        xc = acc - mu
        rstd = lax.rsqrt(jnp.mean(xc * xc, axis=-1, keepdims=True) + EPS)
        ln = xc * rstd * lg_ref[...] + lb_ref[...]
        sl_ref[...] = (ln * _sigmoid(ln)).astype(sl_ref.dtype)

    tile = lambda c: pl.BlockSpec((tt, cw), lambda i, c=c: (i, c))
    halo = lambda c: pl.BlockSpec((HALO, cw), lambda i, c=c: (jnp.maximum(i * hb - 1, 0), c))
    vec = pl.BlockSpec((1, cw), lambda i: (0, 0))
    outs, carried = _call(
        body, grid=(rows // tt,),
        in_specs=[tile(0), tile(1), halo(0), halo(1),
                  pl.BlockSpec((CONV_KERNEL, cw), lambda i: (0, 0)), vec, vec, vec],
        out_specs=[pl.BlockSpec((tt, cw), lambda i: (i, 0)), pl.BlockSpec((tt, cw), lambda i: (i, 0))],
        out_shape=[jax.ShapeDtypeStruct((rows, cw), BF16), jax.ShapeDtypeStruct((rows, cw), F32)],
        scratch_shapes=[pltpu.VMEM((SUBLANES, HALO + tt, cw), F32)],
        semantics=("parallel",), name="conv_fwd", args=[proj, proj, proj, proj, w_dw, b_dw, ln_g, ln_b],
        carry=carry)
    return outs if carry is None else (outs, carried)


def _ln_bwd(dsl, cv, ln_g, ln_b):
    rows, cw = cv.shape
    tr = _row_tile(rows, cw, 3)

    def body(d_ref, cv_ref, lg_ref, lb_ref, o_ref, dg_ref, db_ref):
        i = pl.program_id(0)

        @pl.when(i == 0)
        def _():
            dg_ref[...] = jnp.zeros_like(dg_ref)
            db_ref[...] = jnp.zeros_like(db_ref)

        x = cv_ref[...]
        mu = jnp.mean(x, axis=-1, keepdims=True)
        xc = x - mu
        rstd = lax.rsqrt(jnp.mean(xc * xc, axis=-1, keepdims=True) + EPS)
        xh = xc * rstd
        ln = xh * lg_ref[...] + lb_ref[...]
        s = _sigmoid(ln)
        dln = d_ref[...].astype(F32) * (s * (1.0 + ln * (1.0 - s)))
        dg_ref[...] += jnp.sum(dln * xh, axis=0, keepdims=True)
        db_ref[...] += jnp.sum(dln, axis=0, keepdims=True)
        dxh = dln * lg_ref[...]
        m1 = jnp.mean(dxh, axis=-1, keepdims=True)
        m2 = jnp.mean(dxh * xh, axis=-1, keepdims=True)
        o_ref[...] = rstd * (dxh - m1 - xh * m2)

    row = pl.BlockSpec((tr, cw), lambda i: (i, 0))
    vec = pl.BlockSpec((1, cw), lambda i: (0, 0))
    return pl.pallas_call(
        body, grid=(rows // tr,), in_specs=[row, row, vec, vec], out_specs=[row, vec, vec],
        out_shape=[jax.ShapeDtypeStruct((rows, cw), F32), jax.ShapeDtypeStruct((1, cw), F32),
                   jax.ShapeDtypeStruct((1, cw), F32)],
        compiler_params=_cparams(("arbitrary",)), name="ln_bwd")(dsl, cv, ln_g, ln_b)


def _conv_bwd(dcv, proj, w_dw, dproj, *, cw, carry=None):
    rows = proj.shape[0]
    tt = _pick(rows, 256, HALO)
    hb = tt // HALO
    nt = rows // tt
    taps = CONV_KERNEL

    def body(d_ref, dn_ref, a_ref, g_ref, ha_ref, hg_ref, w_ref, dproj_ref, o_ref, dw_ref, db_ref, vs, ds):
        i = pl.program_id(0)

        @pl.when(i == 0)
        def _():
            dw_ref[...] = jnp.zeros_like(dw_ref)
            db_ref[...] = jnp.zeros_like(db_ref)

        hv = ha_ref[...].astype(F32) * _sigmoid(hg_ref[...].astype(F32))
        vs[0, pl.ds(0, HALO), :] = jnp.where(i == 0, 0.0, hv)
        vs[0, pl.ds(HALO, tt), :] = a_ref[...].astype(F32) * _sigmoid(g_ref[...].astype(F32))
        _shifted_copies(vs, HALO + tt)
        ds[0, pl.ds(0, tt), :] = d_ref[...]
        ds[0, pl.ds(tt, HALO), :] = jnp.where(i == nt - 1, 0.0, dn_ref[...])
        _shifted_copies(ds, tt + HALO)
        db_ref[...] += jnp.sum(d_ref[...], axis=0, keepdims=True)
        for k in range(taps):
            q, r = divmod(HALO - (taps - 1) + k, SUBLANES)
            dw_ref[pl.ds(k, 1), :] += jnp.sum(d_ref[...] * vs[r, pl.ds(q * SUBLANES, tt), :], axis=0, keepdims=True)

        def chunk(ci, carry):
            r0 = pl.multiple_of(ci * CONV_CHUNK, CONV_CHUNK)
            dv = jnp.zeros((CONV_CHUNK, cw), F32)
            for k in range(taps):
                q, r = divmod(taps - 1 - k, SUBLANES)
                dv = dv + w_ref[pl.ds(k, 1), :] * ds[r, pl.ds(r0 + q * SUBLANES, CONV_CHUNK), :]
            av = a_ref[pl.ds(r0, CONV_CHUNK), :].astype(F32)
            sg = _sigmoid(g_ref[pl.ds(r0, CONV_CHUNK), :].astype(F32))
            o_ref[pl.ds(r0, CONV_CHUNK), pl.ds(0, cw)] = (dv * sg).astype(o_ref.dtype)
            o_ref[pl.ds(r0, CONV_CHUNK), pl.ds(cw, cw)] = (dv * av * sg * (1.0 - sg)).astype(o_ref.dtype)
            return carry

        lax.fori_loop(0, tt // CONV_CHUNK, chunk, 0)

    tile = lambda c: pl.BlockSpec((tt, cw), lambda i, c=c: (i, c))
    halo = lambda c: pl.BlockSpec((HALO, cw), lambda i, c=c: (jnp.maximum(i * hb - 1, 0), c))
    nxt = pl.BlockSpec((HALO, cw), lambda i: (jnp.minimum((i + 1) * hb, nt * hb - 1), 0))
    outs, carried = _call(
        body, grid=(nt,),
        in_specs=[pl.BlockSpec((tt, cw), lambda i: (i, 0)), nxt, tile(0), tile(1), halo(0), halo(1),
                  pl.BlockSpec((taps, cw), lambda i: (0, 0)), pl.BlockSpec(memory_space=pl.ANY)],
        out_specs=[pl.BlockSpec((tt, 2 * cw), lambda i: (i, 0)),
                   pl.BlockSpec((taps, cw), lambda i: (0, 0)), pl.BlockSpec((1, cw), lambda i: (0, 0))],
        out_shape=[jax.ShapeDtypeStruct(dproj.shape, dproj.dtype), jax.ShapeDtypeStruct((taps, cw), F32),
                   jax.ShapeDtypeStruct((1, cw), F32)],
        scratch_shapes=[pltpu.VMEM((SUBLANES, HALO + tt, cw), F32), pltpu.VMEM((SUBLANES, tt + HALO, cw), F32)],
        semantics=("arbitrary",), name="conv_bwd", args=[dcv, dcv, proj, proj, proj, proj, w_dw, dproj],
        carry=carry, aliases={7: 0})
    return outs if carry is None else (outs, carried)


def _merge_fwd(proj, y_conv, ya, yb, *, cw):
    rows = proj.shape[0]
    tr = _row_tile(rows, cw, 4)

    def body(gc_ref, gs_ref, yc_ref, ya_ref, yb_ref, o_ref):
        ys = ya_ref[...].astype(F32) * _sigmoid(yb_ref[...].astype(F32))
        o_ref[...] = (_sigmoid(gc_ref[...].astype(F32)) * yc_ref[...].astype(F32)
                      + _sigmoid(gs_ref[...].astype(F32)) * ys).astype(o_ref.dtype)

    blk = lambda off: pl.BlockSpec((tr, cw), lambda i, h, off=off: (i, off + h))
    return pl.pallas_call(
        body, grid=(rows // tr, 2), in_specs=[blk(3), blk(5), blk(0), blk(0), blk(0)], out_specs=blk(0),
        out_shape=jax.ShapeDtypeStruct((rows, 2 * cw), BF16),
        compiler_params=_cparams(("parallel", "parallel")), name="merge_fwd")(proj, proj, y_conv, ya, yb)


def _merge_bwd(dmerged, proj, y_conv, ya, yb, *, cw, carry=None):
    rows = proj.shape[0]
    tr = _row_tile(rows, cw, 6)

    def body(d_ref, g_ref, yc_ref, ya_ref, yb_ref, dg_ref, dyc_ref, dya_ref, dyb_ref):
        q = pl.program_id(1)
        d = d_ref[...].astype(F32)
        sg = _sigmoid(g_ref[...].astype(F32))

        @pl.when(q < 2)
        def _():
            dg_ref[...] = (d * yc_ref[...].astype(F32) * sg * (1.0 - sg)).astype(dg_ref.dtype)
            dyc_ref[...] = (d * sg).astype(dyc_ref.dtype)

        @pl.when(q >= 2)
        def _():
            sb = _sigmoid(yb_ref[...].astype(F32))
            yav = ya_ref[...].astype(F32)
            dg_ref[...] = (d * (yav * sb) * sg * (1.0 - sg)).astype(dg_ref.dtype)
            dys = d * sg
            dya_ref[...] = (dys * sb).astype(dya_ref.dtype)
            dyb_ref[...] = (dys * yav * sb * (1.0 - sb)).astype(dyb_ref.dtype)

    spec = lambda f: pl.BlockSpec((tr, cw), lambda i, q, f=f: (i, f(q)))
    conv_half = spec(lambda q: jnp.minimum(q, 1))
    ssm_half = spec(lambda q: jnp.maximum(q - 2, 0))
    o2 = jax.ShapeDtypeStruct((rows, 2 * cw), BF16)
    outs, carried = _call(
        body, grid=(rows // tr, 4),
        in_specs=[spec(lambda q: q % 2), spec(lambda q: 3 + q), conv_half, ssm_half, ssm_half],
        out_specs=[spec(lambda q: 3 + q), conv_half, ssm_half, ssm_half],
        out_shape=[jax.ShapeDtypeStruct((rows, 7 * cw), BF16), o2, o2, o2], scratch_shapes=[],
        semantics=("parallel", "arbitrary"), name="merge_bwd", args=[dmerged, proj, y_conv, ya, yb],
        carry=carry)
    return outs if carry is None else (outs, carried)


def _res_norm(x, mo, gate, g, scale, shift):
    rows, d = x.shape
    tr = _row_tile(rows, d, 4)

    def body(x_ref, mo_ref, gt_ref, g_ref, sc_ref, sh_ref, h_ref, z_ref):
        h = x_ref[...] + gt_ref[...] * mo_ref[...].astype(F32)
        h_ref[...] = h
        r = lax.rsqrt(jnp.mean(h * h, axis=-1, keepdims=True) + EPS)
        z_ref[...] = ((h * r * g_ref[...]) * (1.0 + sc_ref[...]) + sh_ref[...]).astype(z_ref.dtype)

    row = pl.BlockSpec((tr, d), lambda i: (i, 0))
    vec = pl.BlockSpec((1, d), lambda i: (0, 0))
    return pl.pallas_call(
        body, grid=(rows // tr,), in_specs=[row, row, vec, vec, vec, vec], out_specs=[row, row],
        out_shape=[jax.ShapeDtypeStruct((rows, d), F32), jax.ShapeDtypeStruct((rows, d), BF16)],
        compiler_params=_cparams(("parallel",)), name="res_norm")(x, mo, gate, g, scale, shift)


def _final_fwd_bwd(h1, ff, gate2, final_g, target):
    rows, d = h1.shape
    tr = _row_tile(rows, d, 5)

    def body(h_ref, ff_ref, gt_ref, fg_ref, t_ref, dh_ref, dff_ref, loss_ref, dfg_ref, dgt_ref):
        i = pl.program_id(0)

        @pl.when(i == 0)
        def _():
            loss_ref[...] = jnp.zeros_like(loss_ref)
            dfg_ref[...] = jnp.zeros_like(dfg_ref)
            dgt_ref[...] = jnp.zeros_like(dgt_ref)

        ffv = ff_ref[...].astype(F32)
        h2 = h_ref[...] + gt_ref[...] * ffv
        r = lax.rsqrt(jnp.mean(h2 * h2, axis=-1, keepdims=True) + EPS)
        y = h2 * r
        e = y * fg_ref[...] - t_ref[...]
        loss_ref[...] += 0.5 * jnp.sum(jnp.mean(e * e, axis=-1, keepdims=True))
        dout = e * (1.0 / d)
        dfg_ref[...] += jnp.sum(dout * y, axis=0, keepdims=True)
        dy = dout * fg_ref[...]
        dh2 = r * (dy - y * jnp.mean(dy * y, axis=-1, keepdims=True))
        dh_ref[...] = dh2
        dgt_ref[...] += jnp.sum(dh2 * ffv, axis=0, keepdims=True)
        dff_ref[...] = (dh2 * gt_ref[...]).astype(dff_ref.dtype)

    row = pl.BlockSpec((tr, d), lambda i: (i, 0))
    vec = pl.BlockSpec((1, d), lambda i: (0, 0))
    return pl.pallas_call(
        body, grid=(rows // tr,), in_specs=[row, row, vec, vec, row],
        out_specs=[row, row, pl.BlockSpec((1, LANES), lambda i: (0, 0)), vec, vec],
        out_shape=[jax.ShapeDtypeStruct((rows, d), F32), jax.ShapeDtypeStruct((rows, d), BF16),
                   jax.ShapeDtypeStruct((1, LANES), F32), jax.ShapeDtypeStruct((1, d), F32),
                   jax.ShapeDtypeStruct((1, d), F32)],
        compiler_params=_cparams(("arbitrary",)), name="final_fwd_bwd")(h1, ff, gate2, final_g, target)


def _norm_mod_bwd(dz, hin, dres, g, scale, gate, mo, *, name, carry=None):
    rows, d = hin.shape
    with_gate = gate is not None
    tr = _row_tile(rows, d, 6)

    def body(*refs):
        if with_gate:
            (dz_ref, h_ref, dr_ref, g_ref, sc_ref, gt_ref, mo_ref,
             dh_ref, dsh_ref, dsc_ref, dg_ref, dmo_ref, dgt_ref) = refs
        else:
            dz_ref, h_ref, dr_ref, g_ref, sc_ref, dh_ref, dsh_ref, dsc_ref, dg_ref = refs
        i = pl.program_id(0)

        @pl.when(i == 0)
        def _():
            dsh_ref[...] = jnp.zeros_like(dsh_ref)
            dsc_ref[...] = jnp.zeros_like(dsc_ref)
            dg_ref[...] = jnp.zeros_like(dg_ref)
            if with_gate:
                dgt_ref[...] = jnp.zeros_like(dgt_ref)

        dzv = dz_ref[...].astype(F32)
        h = h_ref[...]
        r = lax.rsqrt(jnp.mean(h * h, axis=-1, keepdims=True) + EPS)
        y = h * r
        dsh_ref[...] += jnp.sum(dzv, axis=0, keepdims=True)
        dsc_ref[...] += jnp.sum(dzv * (y * g_ref[...]), axis=0, keepdims=True)
        dn = dzv * (1.0 + sc_ref[...])
        dg_ref[...] += jnp.sum(dn * y, axis=0, keepdims=True)
        dy = dn * g_ref[...]
        dh = dr_ref[...] + r * (dy - y * jnp.mean(dy * y, axis=-1, keepdims=True))
        dh_ref[...] = dh
        if with_gate:
            dmo_ref[...] = (dh * gt_ref[...]).astype(dmo_ref.dtype)
            dgt_ref[...] += jnp.sum(dh * mo_ref[...].astype(F32), axis=0, keepdims=True)

    row = pl.BlockSpec((tr, d), lambda i: (i, 0))
    vec = pl.BlockSpec((1, d), lambda i: (0, 0))
    vshape = jax.ShapeDtypeStruct((1, d), F32)
    in_specs = [row, row, row, vec, vec]
    args = [dz, hin, dres, g, scale]
    out_specs = [row, vec, vec, vec]
    out_shape = [jax.ShapeDtypeStruct((rows, d), F32), vshape, vshape, vshape]
    if with_gate:
        in_specs += [vec, row]
        args += [gate, mo]
        out_specs += [row, vec]
        out_shape += [jax.ShapeDtypeStruct((rows, d), BF16), vshape]
    outs, carried = _call(
        body, grid=(rows // tr,), in_specs=in_specs, out_specs=out_specs, out_shape=out_shape,
        scratch_shapes=[], semantics=("arbitrary",), name=name, args=args, carry=carry)
    return outs if carry is None else (outs, carried)


def _s5_discretise(a_re, a_im, log_dt, b_re, b_im):
    dt = jnp.exp(log_dt)[:, None]
    er = jnp.exp(a_re * dt)
    lr = er * jnp.cos(a_im * dt)
    li = er * jnp.sin(a_im * dt)
    den = a_re * a_re + a_im * a_im
    cr = ((lr - 1.0) * a_re + li * a_im) / den
    ci = (li * a_re - (lr - 1.0) * a_im) / den
    bbr = cr[..., None] * b_re - ci[..., None] * b_im
    bbi = cr[..., None] * b_im + ci[..., None] * b_re
    return lr, li, bbr, bbi


def _block_diag(w):
    g, r, c = w.shape
    nb = g // GROUPS_PER_BLOCK
    eye = jnp.eye(GROUPS_PER_BLOCK, dtype=w.dtype)
    w5 = w.reshape(nb, GROUPS_PER_BLOCK, r, 1, c) * eye[None, :, None, :, None]
    return w5.reshape(nb, GROUPS_PER_BLOCK * r, GROUPS_PER_BLOCK * c)


def _block_diag_extract(m, r, c):
    nb = m.shape[0]
    m5 = m.reshape(nb, GROUPS_PER_BLOCK, r, GROUPS_PER_BLOCK, c)
    idx = jnp.arange(GROUPS_PER_BLOCK)
    d = m5[:, idx, :, idx, :]
    return jnp.moveaxis(d, 0, 1).reshape(nb * GROUPS_PER_BLOCK, r, c)


def _scan_multipliers(lr, li):
    power = jnp.arange(1, SUBLANES + 1, dtype=F32)[None, :, None]
    er = jnp.exp(power * lr)
    pr = er * jnp.cos(power * li)
    pi = er * jnp.sin(power * li)
    rows = jnp.arange(SUBLANES)[None, :, None]
    fr, fi, rr, ri = [], [], [], []
    for s in (1, 2, 4):
        mf = (rows >= s).astype(F32)
        mr = (rows <= SUBLANES - 1 - s).astype(F32)
        fr.append(mf * pr[:, s - 1:s, :])
        fi.append(mf * pi[:, s - 1:s, :])
        rr.append(mr * pr[:, s - 1:s, :])
        ri.append(mr * pi[:, s - 1:s, :])
    fr.append(pr)
    fi.append(pi)
    rr.append(pr[:, ::-1, :])
    ri.append(pi[:, ::-1, :])
    st = lambda xs: jnp.stack(xs, axis=1)
    return st(fr), st(fi), st(rr), st(ri)


def _scan_rows(sre, sim, mul_r, mul_i, n_groups, reverse):
    sgn = -1.0 if reverse else 1.0
    lanes = sre.shape[1]

    def step(k, carry):
        cr, ci = carry
        kk = (n_groups - 1 - k) if reverse else k
        r0 = pl.multiple_of(kk * SUBLANES, SUBLANES)
        xr = sre[pl.ds(r0, SUBLANES), :]
        xi = sim[pl.ds(r0, SUBLANES), :]
        for lvl, s in enumerate((1, 2, 4)):
            sh = (SUBLANES - s) if reverse else s
            nr = pltpu.roll(xr, sh, 0)
            ni = pltpu.roll(xi, sh, 0)
            mr = mul_r[lvl]
            mi = mul_i[lvl] * sgn
            xr, xi = xr + mr * nr - mi * ni, xi + mr * ni + mi * nr
        mr = mul_r[3]
        mi = mul_i[3] * sgn
        xr, xi = xr + mr * cr - mi * ci, xi + mr * ci + mi * cr
        sre[pl.ds(r0, SUBLANES), :] = xr
        sim[pl.ds(r0, SUBLANES), :] = xi
        edge = 0 if reverse else SUBLANES - 1
        ncr = jnp.broadcast_to(xr[edge:edge + 1, :], (SUBLANES, lanes))
        nci = jnp.broadcast_to(xi[edge:edge + 1, :], (SUBLANES, lanes))
        return ncr, nci

    zero = jnp.zeros((SUBLANES, lanes), F32)
    lax.fori_loop(0, n_groups, step, (zero, zero))


def _dot(a, b):
    return jnp.dot(a, b, preferred_element_type=F32)


def _dotf(a, b):
    return _dot(a.astype(BF16), b)


def _s5_operands(lr, li, bbr, bbi, c_re, c_im):
    g = lr.shape[0]
    nb = g // GROUPS_PER_BLOCK
    tb = lambda w: jnp.swapaxes(w, 1, 2)
    b_in = [_block_diag(tb(bbr)), _block_diag(tb(bbi))]
    c_out = [_block_diag(tb(c_re)), _block_diag(tb(c_im))]
    b_out = [_block_diag(bbr), _block_diag(bbi)]
    c_in = [_block_diag(c_re), _block_diag(c_im)]
    lam_r = lr.reshape(nb, 1, STATE_LANES)
    lam_i = li.reshape(nb, 1, STATE_LANES)
    mults = _scan_multipliers(lam_r, lam_i)
    cast = lambda ws: [w.astype(BF16) for w in ws]
    return cast(b_in), cast(c_out), cast(b_out), cast(c_in), mults


def _s5_fwd(proj, d_skip, b_in, c_out, mults, *, col0, carry=None):
    rows = proj.shape[0]
    nb = b_in[0].shape[0]
    tm = _pick(rows, 512, SUBLANES)
    n_tiles = rows // tm
    s_l = STATE_LANES

    def body(u_ref, dk_ref, br, bi, cr, ci, fr_ref, fi_ref, o_ref, sr_ref, si_ref, sre, sim):
        for t in range(n_tiles):
            rs = pl.ds(t * tm, tm)
            ub = u_ref[rs, :]
            sre[rs, :] = _dot(ub, br[...])
            sim[rs, :] = _dot(ub, bi[...])
        _scan_rows(sre, sim, fr_ref, fi_ref, rows // SUBLANES, False)
        for t in range(n_tiles):
            rs = pl.ds(t * tm, tm)
            srb = sre[rs, :].astype(BF16)
            sib = sim[rs, :].astype(BF16)
            sr_ref[rs, :] = srb
            si_ref[rs, :] = sib
            y0 = _dot(srb, cr[...]) - _dot(sib, ci[...])
            y1 = y0 + dk_ref[...] * u_ref[rs, :].astype(F32)
            o_ref[rs, :] = _gelu(y1).astype(o_ref.dtype)

    mat_in = pl.BlockSpec((None, LANES, s_l), lambda g: (g, 0, 0))
    mat_out = pl.BlockSpec((None, s_l, LANES), lambda g: (g, 0, 0))
    mul = pl.BlockSpec((None, 4, SUBLANES, s_l), lambda g: (g, 0, 0, 0))
    state = pl.BlockSpec((rows, s_l), lambda g: (0, g))
    outs, carried = _call(
        body, grid=(nb,),
        in_specs=[pl.BlockSpec((rows, LANES), lambda g: (0, col0 + g)), pl.BlockSpec((1, LANES), lambda g: (0, g))]
        + [mat_in] * 2 + [mat_out] * 2 + [mul] * 2,
        out_specs=[pl.BlockSpec((rows, LANES), lambda g: (0, g)), state, state],
        out_shape=[jax.ShapeDtypeStruct((rows, nb * LANES), BF16), jax.ShapeDtypeStruct((rows, nb * s_l), BF16),
                   jax.ShapeDtypeStruct((rows, nb * s_l), BF16)],
        scratch_shapes=[pltpu.VMEM((rows, s_l), F32), pltpu.VMEM((rows, s_l), F32)],
        semantics=("parallel",), name="s5_fwd", args=[proj, d_skip, *b_in, *c_out, mults[0], mults[1]], carry=carry)
    return outs if carry is None else (outs, carried)


def _s5_bwd(proj, dyg, d_skip, states, c_out, b_out, c_in, mults, dproj, *, col0, carry=None):
    rows = proj.shape[0]
    nb = c_out[0].shape[0]
    tm = _pick(rows, 512, SUBLANES)
    n_tiles = rows // tm
    s_l = STATE_LANES
    n_groups = rows // SUBLANES
    tn = (((0,), (0,)), ((), ()))

    def body(u_ref, dy_ref, dk_ref, sr_ref, si_ref, cr, ci, bor, boi, cir, cii, rr_ref, ri_ref, dproj_ref,
             du_ref, ddk_ref, dbr_ref, dbi_ref, dcr_ref, dci_ref, dlr_ref, dli_ref,
             gre, gim, dy1):
        ddk = jnp.zeros((1, LANES), F32)
        dcr = jnp.zeros((s_l, LANES), F32)
        dci = jnp.zeros((s_l, LANES), F32)
        for t in range(n_tiles):
            rs = pl.ds(t * tm, tm)
            srb = sr_ref[rs, :]
            sib = si_ref[rs, :]
            uf = u_ref[rs, :].astype(F32)
            y0 = _dot(srb, cr[...]) - _dot(sib, ci[...])
            y1 = y0 + dk_ref[...] * uf
            d1 = dy_ref[rs, :].astype(F32) * _gelu_grad(y1)
            dy1[rs, :] = d1
            ddk = ddk + jnp.sum(d1 * uf, axis=0, keepdims=True)
            d1b = d1.astype(BF16)
            dcr = dcr + lax.dot_general(srb, d1b, tn, preferred_element_type=F32)
            dci = dci - lax.dot_general(sib, d1b, tn, preferred_element_type=F32)
            gre[rs, :] = _dot(d1b, cir[...])
            gim[rs, :] = -_dot(d1b, cii[...])
        ddk_ref[...] = ddk
        dcr_ref[...] = dcr
        dci_ref[...] = dci

        last_row = lax.broadcasted_iota(jnp.int32, (SUBLANES, s_l), 0) == SUBLANES - 1

        def group(r0, s_r, s_i, carry):
            cr_, ci_, ar, ai = carry
            xr = gre[pl.ds(r0, SUBLANES), :]
            xi = gim[pl.ds(r0, SUBLANES), :]
            for lvl, s in enumerate((1, 2, 4)):
                nr = pltpu.roll(xr, SUBLANES - s, 0)
                ni = pltpu.roll(xi, SUBLANES - s, 0)
                mr = rr_ref[lvl]
                mi = ri_ref[lvl]
                xr, xi = xr + mr * nr + mi * ni, xi + mr * ni - mi * nr
            mr = rr_ref[3]
            mi = ri_ref[3]
            xr, xi = xr + mr * cr_ + mi * ci_, xi + mr * ci_ - mi * cr_
            gre[pl.ds(r0, SUBLANES), :] = xr
            gim[pl.ds(r0, SUBLANES), :] = xi
            nxt_r = jnp.where(last_row, cr_, pltpu.roll(xr, SUBLANES - 1, 0))
            nxt_i = jnp.where(last_row, ci_, pltpu.roll(xi, SUBLANES - 1, 0))
            ncr = jnp.broadcast_to(xr[0:1, :], (SUBLANES, s_l))
            nci = jnp.broadcast_to(xi[0:1, :], (SUBLANES, s_l))
            return ncr, nci, ar + nxt_r * s_r + nxt_i * s_i, ai + nxt_i * s_r - nxt_r * s_i

        def rev_step(k, carry):
            r0 = pl.multiple_of((n_groups // 2 - 1 - k) * 2 * SUBLANES, 2 * SUBLANES)
            s_r = sr_ref[pl.ds(r0, 2 * SUBLANES), :].astype(F32)
            s_i = si_ref[pl.ds(r0, 2 * SUBLANES), :].astype(F32)
            carry = group(r0 + SUBLANES, s_r[SUBLANES:], s_i[SUBLANES:], carry)
            return group(r0, s_r[:SUBLANES], s_i[:SUBLANES], carry)

        zero = jnp.zeros((SUBLANES, s_l), F32)
        _, _, ar, ai = lax.fori_loop(0, n_groups // 2, rev_step, (zero, zero, zero, zero))
        dlr_ref[...] = jnp.sum(ar, axis=0, keepdims=True)
        dli_ref[...] = jnp.sum(ai, axis=0, keepdims=True)

        dbr = jnp.zeros((LANES, s_l), F32)
        dbi = jnp.zeros((LANES, s_l), F32)
        for t in range(n_tiles):
            rs = pl.ds(t * tm, tm)
            gr = gre[rs, :]
            gi = gim[rs, :]
            grb = gr.astype(BF16)
            gib = gi.astype(BF16)
            du = _dot(grb, bor[...]) + _dot(gib, boi[...]) + dy1[rs, :] * dk_ref[...]
            du_ref[rs, :] = du.astype(du_ref.dtype)
            ub = u_ref[rs, :]
            dbr = dbr + lax.dot_general(ub, grb, tn, preferred_element_type=F32)
            dbi = dbi + lax.dot_general(ub, gib, tn, preferred_element_type=F32)
        dbr_ref[...] = dbr
        dbi_ref[...] = dbi

    mat_in = pl.BlockSpec((None, LANES, s_l), lambda g: (g, 0, 0))
    mat_out = pl.BlockSpec((None, s_l, LANES), lambda g: (g, 0, 0))
    mul = pl.BlockSpec((None, 4, SUBLANES, s_l), lambda g: (g, 0, 0, 0))
    lam = pl.BlockSpec((None, 1, s_l), lambda g: (g, 0, 0))
    col = pl.BlockSpec((rows, LANES), lambda g: (0, g))
    vec = pl.BlockSpec((1, LANES), lambda g: (0, g))
    state = pl.BlockSpec((rows, s_l), lambda g: (0, g))
    outs, carried = _call(
        body, grid=(nb,),
        in_specs=[pl.BlockSpec((rows, LANES), lambda g: (0, col0 + g)), col, vec]
        + [state] * 2 + [mat_out] * 2 + [mat_out] * 2 + [mat_in] * 2 + [mul] * 2
        + [pl.BlockSpec(memory_space=pl.ANY)],
        out_specs=[pl.BlockSpec((rows, LANES), lambda g: (0, col0 + g)), vec, mat_in, mat_in, mat_out, mat_out,
                   lam, lam],
        out_shape=[jax.ShapeDtypeStruct(dproj.shape, dproj.dtype), jax.ShapeDtypeStruct((1, nb * LANES), F32),
                   jax.ShapeDtypeStruct((nb, LANES, s_l), F32), jax.ShapeDtypeStruct((nb, LANES, s_l), F32),
                   jax.ShapeDtypeStruct((nb, s_l, LANES), F32), jax.ShapeDtypeStruct((nb, s_l, LANES), F32),
                   jax.ShapeDtypeStruct((nb, 1, s_l), F32), jax.ShapeDtypeStruct((nb, 1, s_l), F32)],
        scratch_shapes=[pltpu.VMEM((rows, s_l), F32)] * 2 + [pltpu.VMEM((rows, LANES), F32)],
        semantics=("parallel",), name="s5_bwd",
        args=[proj, dyg, d_skip, *states, *c_out, *b_out, *c_in, mults[2], mults[3], dproj], carry=carry,
        aliases={13: 0})
    return outs if carry is None else (outs, carried)


def _silu(v):
    return v * _sigmoid(v)


def _ada_fwd(c_all, w_shard, b_cols):
    d, n = w_shard.shape
    bn = _pick(n, 512)

    def body(c_ref, w_ref, b_ref, o_ref):
        ca = _silu(c_ref[...]).astype(BF16)
        o_ref[...] = _dot(ca, w_ref[...].astype(BF16)) + b_ref[...]

    return pl.pallas_call(
        body, grid=(n // bn,),
        in_specs=[pl.BlockSpec((N_DEV, d), lambda j: (0, 0)), pl.BlockSpec((d, bn), lambda j: (0, j)),
                  pl.BlockSpec((1, bn), lambda j: (0, j))],
        out_specs=pl.BlockSpec((N_DEV, bn), lambda j: (0, j)),
        out_shape=jax.ShapeDtypeStruct((N_DEV, n), F32),
        compiler_params=_cparams(("parallel",)), name="ada_fwd")(c_all, w_shard, b_cols)


def _ada_bwd(c_all, dmod_cols):
    d = c_all.shape[1]
    n = dmod_cols.shape[1]
    bn = _pick(n, 512)

    def body(c_ref, g_ref, o_ref):
        ca = _silu(c_ref[...]).astype(BF16)
        o_ref[...] = lax.dot_general(ca, g_ref[...].astype(BF16), (((0,), (0,)), ((), ())),
                                     preferred_element_type=F32)

    return pl.pallas_call(
        body, grid=(n // bn,),
        in_specs=[pl.BlockSpec((N_DEV, d), lambda j: (0, 0)), pl.BlockSpec((N_DEV, bn), lambda j: (0, j))],
        out_specs=pl.BlockSpec((d, bn), lambda j: (0, j)),
        out_shape=jax.ShapeDtypeStruct((d, n), F32),
        compiler_params=_cparams(("parallel",)), name="ada_bwd")(c_all, dmod_cols)


def _cast_bf16(w, *, name):
    rows, cols = w.shape
    tr = _row_tile(rows, cols, 2)

    def body(w_ref, o_ref):
        o_ref[...] = w_ref[...].astype(BF16)

    row = pl.BlockSpec((tr, cols), lambda i: (i, 0))
    return pl.pallas_call(
        body, grid=(rows // tr,), in_specs=[row], out_specs=row,
        out_shape=jax.ShapeDtypeStruct((rows, cols), BF16),
        compiler_params=_cparams(("parallel",)), name=name)(w)


def _adamw(w, g, m, v, *, name, carry=None):
    rows, cols = w.shape
    tr = _row_tile(rows, cols, 7)
    c1 = 1.0 / (1.0 - ADAM_B1 ** ADAM_STEP)
    c2 = 1.0 / (1.0 - ADAM_B2 ** ADAM_STEP)

    def body(w_ref, g_ref, m_ref, v_ref, d_ref, nm_ref, nv_ref):
        gv = g_ref[...]
        nm = ADAM_B1 * m_ref[...] + (1.0 - ADAM_B1) * gv
        nv = ADAM_B2 * v_ref[...] + (1.0 - ADAM_B2) * (gv * gv)
        nm_ref[...] = nm
        nv_ref[...] = nv
        d_ref[...] = -ADAM_LR * ((nm * c1) / (jnp.sqrt(nv * c2) + ADAM_EPS) + ADAM_WD * w_ref[...])

    row = pl.BlockSpec((tr, cols), lambda i: (i, 0))
    shp = jax.ShapeDtypeStruct((rows, cols), F32)
    outs, carried = _call(
        body, grid=(rows // tr,), in_specs=[row] * 4, out_specs=[row] * 3, out_shape=[shp] * 3,
        scratch_shapes=[], semantics=("parallel",), name=name, args=[w, g, m, v], carry=carry)
    return outs if carry is None else (outs, carried)


def _sum_leading(a, *, name, out_dtype=F32):
    n, rows, cols = a.shape
    tr = _row_tile(rows, cols, n + 1)

    def body(a_ref, o_ref):
        acc = a_ref[0].astype(F32)
        for i in range(1, n):
            acc = acc + a_ref[i].astype(F32)
        o_ref[...] = acc.astype(o_ref.dtype)

    return pl.pallas_call(
        body, grid=(rows // tr,), in_specs=[pl.BlockSpec((n, tr, cols), lambda i: (0, i, 0))],
        out_specs=pl.BlockSpec((tr, cols), lambda i: (i, 0)),
        out_shape=jax.ShapeDtypeStruct((rows, cols), out_dtype),
        compiler_params=_cparams(("parallel",)), name=name)(a)


def _add_half(dw, land, my_c, *, name):
    n, r, cols = dw.shape
    h = r // 2
    tr = _row_tile(h, cols, 3)
    hb = h // tr

    def body(c_ref, a_ref, b_ref, o_ref):
        o_ref[...] = (a_ref[...].astype(F32) + b_ref[...].astype(F32)).astype(o_ref.dtype)

    gs = pltpu.PrefetchScalarGridSpec(
        num_scalar_prefetch=1, grid=(n, hb),
        in_specs=[pl.BlockSpec((None, tr, cols), lambda s, i, c_ref: (s, c_ref[0] * hb + i, 0)),
                  pl.BlockSpec((None, tr, cols), lambda s, i, c_ref: (s, i, 0))],
        out_specs=pl.BlockSpec((None, tr, cols), lambda s, i, c_ref: (s, i, 0)))
    return pl.pallas_call(
        body, grid_spec=gs, out_shape=jax.ShapeDtypeStruct((n, h, cols), BF16),
        compiler_params=_cparams(("parallel", "parallel")), name=name)(my_c, dw, land)


def _mesh_pos():
    return lax.axis_index("x"), lax.axis_index("y"), lax.axis_index("c")


def _other_chips(x, y):
    return [(1 - x, y), (x, 1 - y), (1 - x, 1 - y)]


def _gather_small(blk, *, name):
    m_per, n = blk.shape

    def body(x_ref, out_ref, send_sems, recv_sems, local_sem):
        x, y, c = _mesh_pos()
        me, sibling = (x, y, c), (x, y, 1 - c)
        chips = _other_chips(x, y)

        def rows(px, py, pc):
            return out_ref.at[pl.ds((4 * px + 2 * py + pc) * m_per, m_per), :]

        def copy(k, block, to, src=None):
            return pltpu.make_async_remote_copy(
                src_ref=rows(*block) if src is None else src, dst_ref=rows(*block),
                send_sem=send_sems.at[k], recv_sem=recv_sems.at[k], device_id=to, device_id_type=MESH)

        mine = pltpu.make_async_copy(x_ref, rows(*me), local_sem)
        mine.start()
        first = [copy(0, me, sibling, src=x_ref)]
        first += [copy(1 + j, me, (*chip, c), src=x_ref) for j, chip in enumerate(chips)]
        for cp in first:
            cp.start()
        passed = [copy(4 + j, (*chip, c), sibling) for j, chip in enumerate(chips)]
        for j, chip in enumerate(chips):
            copy(1 + j, (*chip, c), me).wait_recv()
            passed[j].start()
        copy(0, sibling, me).wait_recv()
        for j, chip in enumerate(chips):
            copy(4 + j, (*chip, 1 - c), me).wait_recv()
        for cp in first + passed:
            cp.wait_send()
        mine.wait()

    return pl.pallas_call(
        body, out_shape=jax.ShapeDtypeStruct((N_DEV * m_per, n), blk.dtype),
        in_specs=[pl.BlockSpec(memory_space=pltpu.VMEM)], out_specs=pl.BlockSpec(memory_space=pltpu.VMEM),
        scratch_shapes=[pltpu.SemaphoreType.DMA((7,)), pltpu.SemaphoreType.DMA((7,)), pltpu.SemaphoreType.DMA],
        compiler_params=pltpu.CompilerParams(vmem_limit_bytes=VMEM_LIMIT_BYTES), name=name)(blk)


def _hbm_specs(n):
    return [pl.BlockSpec(memory_space=pl.ANY)] * n


def _gather_weights(shards):
    n = len(shards)

    def body(*refs):
        ins, outs = refs[:n], refs[n:2 * n]
        send_sems, recv_sems, local_sems = refs[2 * n:]
        x, y, c = _mesh_pos()
        me_chip = 2 * x + y
        sibling = (x, y, 1 - c)
        chips = _other_chips(x, y)

        def half(w, chip_idx, pc):
            h = shards[w].shape[0] // 2
            return outs[w].at[chip_idx, pl.ds(pc * h, h), :]

        def copy(w, k, chip_idx, pc, to, src=None):
            dst = half(w, chip_idx, pc)
            return pltpu.make_async_remote_copy(
                src_ref=dst if src is None else src, dst_ref=dst,
                send_sem=send_sems.at[6 * w + k], recv_sem=recv_sems.at[6 * w + k],
                device_id=to, device_id_type=MESH)

        local = [pltpu.make_async_copy(ins[w], outs[w].at[me_chip], local_sems.at[w]) for w in range(n)]
        for cp in local:
            cp.start()
        sends = []
        for w in range(n):
            h = shards[w].shape[0] // 2
            for j, chip in enumerate(chips):
                cp = copy(w, j, me_chip, c, (*chip, c), src=ins[w].at[pl.ds(c * h, h), :])
                cp.start()
                sends.append(cp)
        for w in range(n):
            for j, chip in enumerate(chips):
                chip_idx = 2 * chip[0] + chip[1]
                copy(w, j, chip_idx, c, (x, y, c)).wait_recv()
                cp = copy(w, 3 + j, chip_idx, c, sibling)
                cp.start()
                sends.append(cp)
        for w in range(n):
            for j, chip in enumerate(chips):
                copy(w, 3 + j, 2 * chip[0] + chip[1], 1 - c, (x, y, c)).wait_recv()
        for cp in sends:
            cp.wait_send()
        for cp in local:
            cp.wait()

    return pl.pallas_call(
        body, out_shape=[jax.ShapeDtypeStruct((N_CHIPS,) + s.shape, s.dtype) for s in shards],
        in_specs=_hbm_specs(n), out_specs=_hbm_specs(n),
        scratch_shapes=[pltpu.SemaphoreType.DMA((6 * n,)), pltpu.SemaphoreType.DMA((6 * n,)),
                        pltpu.SemaphoreType.DMA((n,))],
        name="gather_weights")(*shards)


def _swap_halves(dws, *, name):
    n = len(dws)

    def body(*refs):
        ins, outs = refs[:n], refs[n:2 * n]
        send_sems, recv_sems = refs[2 * n:]
        x, y, c = _mesh_pos()
        cps = []
        for w in range(n):
            h = dws[w].shape[1] // 2
            cp = pltpu.make_async_remote_copy(
                src_ref=ins[w].at[:, pl.ds((1 - c) * h, h), :], dst_ref=outs[w],
                send_sem=send_sems.at[w], recv_sem=recv_sems.at[w],
                device_id=(x, y, 1 - c), device_id_type=MESH)
            cp.start()
            cps.append(cp)
        for cp in cps:
            cp.wait()

    return pl.pallas_call(
        body, out_shape=[jax.ShapeDtypeStruct((s.shape[0], s.shape[1] // 2, s.shape[2]), s.dtype) for s in dws],
        in_specs=_hbm_specs(n), out_specs=_hbm_specs(n),
        scratch_shapes=[pltpu.SemaphoreType.DMA((n,)), pltpu.SemaphoreType.DMA((n,))],
        name=name)(*dws)


def _chip_exchange(parts):
    n = len(parts)

    def body(*refs):
        ins, outs = refs[:n], refs[n:2 * n]
        send_sems, recv_sems, local_sems = refs[2 * n:]
        x, y, c = _mesh_pos()
        me_chip = 2 * x + y
        chips = _other_chips(x, y)
        local = [pltpu.make_async_copy(ins[w].at[me_chip], outs[w].at[me_chip], local_sems.at[w]) for w in range(n)]
        for cp in local:
            cp.start()
        cps = []
        for w in range(n):
            for j, chip in enumerate(chips):
                cp = pltpu.make_async_remote_copy(
                    src_ref=ins[w].at[2 * chip[0] + chip[1]], dst_ref=outs[w].at[me_chip],
                    send_sem=send_sems.at[3 * w + j], recv_sem=recv_sems.at[3 * w + j],
                    device_id=(*chip, c), device_id_type=MESH)
                cp.start()
                cps.append((cp, w, j, chip))
        for cp, w, j, chip in cps:
            slot = outs[w].at[2 * chip[0] + chip[1]]
            pltpu.make_async_remote_copy(
                src_ref=slot, dst_ref=slot, send_sem=send_sems.at[3 * w + j], recv_sem=recv_sems.at[3 * w + j],
                device_id=(x, y, c), device_id_type=MESH).wait_recv()
        for cp, _, _, _ in cps:
            cp.wait_send()
        for cp in local:
            cp.wait()

    return pl.pallas_call(
        body, out_shape=[jax.ShapeDtypeStruct(s.shape, s.dtype) for s in parts],
        in_specs=_hbm_specs(n), out_specs=_hbm_specs(n),
        scratch_shapes=[pltpu.SemaphoreType.DMA((3 * n,)), pltpu.SemaphoreType.DMA((3 * n,)),
                        pltpu.SemaphoreType.DMA((n,))],
        name="chip_exchange")(*parts)


def _join_halves(halves):
    n = len(halves)

    def body(*refs):
        ins, outs = refs[:n], refs[n:2 * n]
        send_sems, recv_sems, local_sems = refs[2 * n:]
        x, y, c = _mesh_pos()
        cps, local = [], []
        for w in range(n):
            h = halves[w].shape[0]
            mine = outs[w].at[pl.ds(c * h, h), :]
            lc = pltpu.make_async_copy(ins[w], mine, local_sems.at[w])
            lc.start()
            local.append(lc)
            cp = pltpu.make_async_remote_copy(
                src_ref=ins[w], dst_ref=mine, send_sem=send_sems.at[w], recv_sem=recv_sems.at[w],
                device_id=(x, y, 1 - c), device_id_type=MESH)
            cp.start()
            cps.append(cp)
        for w in range(n):
            h = halves[w].shape[0]
            theirs = outs[w].at[pl.ds((1 - c) * h, h), :]
            pltpu.make_async_remote_copy(
                src_ref=theirs, dst_ref=theirs, send_sem=send_sems.at[w], recv_sem=recv_sems.at[w],
                device_id=(x, y, c), device_id_type=MESH).wait_recv()
        for cp in cps:
            cp.wait_send()
        for lc in local:
            lc.wait()

    return pl.pallas_call(
        body, out_shape=[jax.ShapeDtypeStruct((2 * s.shape[0], s.shape[1]), s.dtype) for s in halves],
        in_specs=_hbm_specs(n), out_specs=_hbm_specs(n),
        scratch_shapes=[pltpu.SemaphoreType.DMA((n,)), pltpu.SemaphoreType.DMA((n,)), pltpu.SemaphoreType.DMA((n,))],
        name="join_halves")(*halves)


def _cast_into_slot(w, chip, after, *, name):
    rows, cols = w.shape
    tr = _row_tile(rows, cols, 2)

    def body(chip_ref, w_ref, after_ref, o_ref):
        o_ref[...] = w_ref[...].astype(BF16)

    gs = pltpu.PrefetchScalarGridSpec(
        num_scalar_prefetch=1, grid=(rows // tr,),
        in_specs=[pl.BlockSpec((tr, cols), lambda i, chip_ref: (i, 0)), pl.BlockSpec(memory_space=pl.ANY)],
        out_specs=pl.BlockSpec((None, tr, cols), lambda i, chip_ref: (chip_ref[0], i, 0)))
    return pl.pallas_call(
        body, grid_spec=gs, out_shape=jax.ShapeDtypeStruct((N_CHIPS, rows, cols), BF16),
        compiler_params=_cparams(("parallel",)), name=name)(chip, w, after)


def _row_range(h, lo, hi, parts):
    step = h // parts
    assert step * parts == h and step % (2 * SUBLANES) == 0, (h, parts)
    return lo * step, (hi - lo) * step


def _gather_carry(items):
    n_copies = sum(len(js) for _, js, _, _, _ in items)
    sem = pltpu.SemaphoreType.DMA((2 * n_copies,))

    def copies(outs, sems):
        send_sems, recv_sems = sems
        x, y, c = _mesh_pos()
        me_chip = 2 * x + y
        chips = _other_chips(x, y)
        out_ici, in_ici, out_d2d, in_d2d = [], [], [], []
        k = 0
        for w, (buf, js, lo, hi, parts) in enumerate(items):
            h = buf.shape[1] // 2
            r0, nr = _row_range(h, lo, hi, parts)

            def copy(k, chip_idx, pc, to):
                ref = outs[w].at[chip_idx, pl.ds(pc * h + r0, nr), :]
                return pltpu.make_async_remote_copy(
                    src_ref=ref, dst_ref=ref, send_sem=send_sems.at[k], recv_sem=recv_sems.at[k],
                    device_id=to, device_id_type=MESH)

            for j in js:
                chip = chips[j]
                chip_idx = 2 * chip[0] + chip[1]
                out_ici.append(copy(k, me_chip, c, (*chip, c)))
                in_ici.append(copy(k, chip_idx, c, (x, y, c)))
                out_d2d.append(copy(k + 1, chip_idx, c, (x, y, 1 - c)))
                in_d2d.append(copy(k + 1, chip_idx, 1 - c, (x, y, c)))
                k += 2
        return out_ici, in_ici, out_d2d, in_d2d

    def start(ins, outs, sems):
        for cp in copies(outs, sems)[0]:
            cp.start()

    def finish(ins, outs, sems):
        out_ici, in_ici, out_d2d, in_d2d = copies(outs, sems)
        for arrived, onward in zip(in_ici, out_d2d):
            arrived.wait_recv()
            onward.start()
        for arrived in in_d2d:
            arrived.wait_recv()
        for cp in out_ici + out_d2d:
            cp.wait_send()

    bufs = [it[0] for it in items]
    shapes = [jax.ShapeDtypeStruct(b.shape, b.dtype) for b in bufs]
    return _Carry(bufs, shapes, {i: i for i in range(len(bufs))}, [sem, sem], start, finish)


def _exchange_carry(items):
    n = len(items)
    sem = pltpu.SemaphoreType.DMA((3 * n,))
    given = [w for w in range(n) if items[w][1] is not None]

    def copies(ins, outs, sems):
        send_sems, recv_sems = sems
        x, y, c = _mesh_pos()
        chips = _other_chips(x, y)
        sends, recvs = [], []
        for w, (part, _, lo, hi, parts) in enumerate(items):
            r0, nr = _row_range(part.shape[1], lo, hi, parts)
            for j, chip in enumerate(chips):
                land = outs[w].at[j, pl.ds(r0, nr), :]
                sends.append(pltpu.make_async_remote_copy(
                    src_ref=ins[w].at[2 * chip[0] + chip[1], pl.ds(r0, nr), :], dst_ref=land,
                    send_sem=send_sems.at[3 * w + j], recv_sem=recv_sems.at[3 * w + j],
                    device_id=(*chip, c), device_id_type=MESH))
                recvs.append(pltpu.make_async_remote_copy(
                    src_ref=land, dst_ref=land,
                    send_sem=send_sems.at[3 * w + j], recv_sem=recv_sems.at[3 * w + j],
                    device_id=(x, y, c), device_id_type=MESH))
        return sends, recvs

    def start(ins, outs, sems):
        for cp in copies(ins, outs, sems)[0]:
            cp.start()

    def finish(ins, outs, sems):
        sends, recvs = copies(ins, outs, sems)
        for cp in recvs:
            cp.wait_recv()
        for cp in sends:
            cp.wait_send()

    inputs = [it[0] for it in items] + [items[w][1] for w in given]
    shapes = [jax.ShapeDtypeStruct((3,) + it[0].shape[1:], it[0].dtype) for it in items]
    aliases = {n + i: w for i, w in enumerate(given)}
    return _Carry(inputs, shapes, aliases, [sem, sem], start, finish)


def _sum_into_half(part, landed, chip, my_c, *, name):
    _, h, cols = part.shape
    tr = _row_tile(h, cols, 5)
    hb = h // tr

    def body(chip_ref, c_ref, p_ref, l_ref, o_ref):
        acc = p_ref[...].astype(F32)
        for j in range(3):
            acc = acc + l_ref[j].astype(F32)
        o_ref[...] = acc

    gs = pltpu.PrefetchScalarGridSpec(
        num_scalar_prefetch=2, grid=(hb,),
        in_specs=[pl.BlockSpec((None, tr, cols), lambda i, chip_ref, c_ref: (chip_ref[0], i, 0)),
                  pl.BlockSpec((3, tr, cols), lambda i, chip_ref, c_ref: (0, i, 0))],
        out_specs=pl.BlockSpec((tr, cols), lambda i, chip_ref, c_ref: (c_ref[0] * hb + i, 0)))
    return pl.pallas_call(
        body, grid_spec=gs, out_shape=jax.ShapeDtypeStruct((2 * h, cols), F32),
        compiler_params=_cparams(("parallel",)), name=name)(chip, my_c, part, landed)


def _join_carry(fulls):
    n = len(fulls)
    sem = pltpu.SemaphoreType.DMA((n,))

    def copies(outs, sems):
        send_sems, recv_sems = sems
        x, y, c = _mesh_pos()
        sends, recvs = [], []
        for w in range(n):
            h = fulls[w].shape[0] // 2
            mine = outs[w].at[pl.ds(c * h, h), :]
            theirs = outs[w].at[pl.ds((1 - c) * h, h), :]
            sends.append(pltpu.make_async_remote_copy(
                src_ref=mine, dst_ref=mine, send_sem=send_sems.at[w], recv_sem=recv_sems.at[w],
                device_id=(x, y, 1 - c), device_id_type=MESH))
            recvs.append(pltpu.make_async_remote_copy(
                src_ref=theirs, dst_ref=theirs, send_sem=send_sems.at[w], recv_sem=recv_sems.at[w],
                device_id=(x, y, c), device_id_type=MESH))
        return sends, recvs

    def start(ins, outs, sems):
        for cp in copies(outs, sems)[0]:
            cp.start()

    def finish(ins, outs, sems):
        sends, recvs = copies(outs, sems)
        for cp in recvs:
            cp.wait_recv()
        for cp in sends:
            cp.wait_send()

    shapes = [jax.ShapeDtypeStruct(f.shape, f.dtype) for f in fulls]
    return _Carry(fulls, shapes, {i: i for i in range(n)}, [sem, sem], start, finish)


class _NoComm:
    def __init__(self, big):
        self.big = big
        self.grads = {}

    def weight(self, name):
        return self.big[name]

    def mm_in(self, u, afters):
        return _mm(u, self.big["w_in"], mode="nn", out_dtype=BF16, name="mm_in")

    def mm_d_in(self, dproj):
        return _mm(dproj, self.big["w_in"], mode="nt", out_dtype=F32, name="mm_d_in")

    def carry(self, site):
        return None

    def done(self, site, carried, out=None):
        return out

    def grad(self, name, dw):
        self.grads[name] = dw

    def early_grads(self, early):
        self.early = early


def _gather_rows_carry(blk):
    m_per = blk.shape[0]
    sem = pltpu.SemaphoreType.DMA((7,))

    def copies(ins, outs, sems):
        send_sems, recv_sems, local_sem = sems
        x, y, c = _mesh_pos()
        me, sibling = (x, y, c), (x, y, 1 - c)
        chips = _other_chips(x, y)

        def rows(px, py, pc):
            return outs[0].at[pl.ds((4 * px + 2 * py + pc) * m_per, m_per), :]

        def copy(k, block, to, src=None):
            return pltpu.make_async_remote_copy(
                src_ref=rows(*block) if src is None else src, dst_ref=rows(*block),
                send_sem=send_sems.at[k], recv_sem=recv_sems.at[k], device_id=to, device_id_type=MESH)

        mine = pltpu.make_async_copy(ins[0], rows(*me), local_sem.at[0])
        first = [copy(0, me, sibling, src=ins[0])]
        first += [copy(1 + j, me, (*chip, c), src=ins[0]) for j, chip in enumerate(chips)]
        passed = [copy(4 + j, (*chip, c), sibling) for j, chip in enumerate(chips)]
        landed = [copy(1 + j, (*chip, c), me) for j, chip in enumerate(chips)]
        from_sibling = [copy(0, sibling, me)] + [copy(4 + j, (*chip, 1 - c), me) for j, chip in enumerate(chips)]
        return mine, first, passed, landed, from_sibling

    def start(ins, outs, sems):
        mine, first, _, _, _ = copies(ins, outs, sems)
        mine.start()
        for cp in first:
            cp.start()

    def finish(ins, outs, sems):
        mine, first, passed, landed, from_sibling = copies(ins, outs, sems)
        for arrived, onward in zip(landed, passed):
            arrived.wait_recv()
            onward.start()
        for arrived in from_sibling:
            arrived.wait_recv()
        for cp in first + passed:
            cp.wait_send()
        mine.wait()

    shape = jax.ShapeDtypeStruct((N_DEV * m_per, blk.shape[1]), blk.dtype)
    return _Carry([blk], [shape], {}, [sem, sem, pltpu.SemaphoreType.DMA((1,))], start, finish)


def _gather_fresh_carry(own, js):
    n = len(js)
    h = own.shape[0] // 2
    sem = pltpu.SemaphoreType.DMA((2 * n,))

    def copies(ins, outs, sems):
        send_sems, recv_sems = sems
        x, y, c = _mesh_pos()
        chips = _other_chips(x, y)
        out_ici, in_ici, out_d2d, in_d2d = [], [], [], []

        def copy(k, src, dst, to):
            return pltpu.make_async_remote_copy(
                src_ref=src, dst_ref=dst, send_sem=send_sems.at[k], recv_sem=recv_sems.at[k],
                device_id=to, device_id_type=MESH)

        for jj, j in enumerate(js):
            mine = ins[0].at[pl.ds(c * h, h), :]
            land = outs[0].at[jj, pl.ds(c * h, h), :]
            other = outs[0].at[jj, pl.ds((1 - c) * h, h), :]
            out_ici.append(copy(2 * jj, mine, land, (*chips[j], c)))
            in_ici.append(copy(2 * jj, land, land, (x, y, c)))
            out_d2d.append(copy(2 * jj + 1, land, land, (x, y, 1 - c)))
            in_d2d.append(copy(2 * jj + 1, other, other, (x, y, c)))
        return out_ici, in_ici, out_d2d, in_d2d

    def start(ins, outs, sems):
        for cp in copies(ins, outs, sems)[0]:
            cp.start()

    def finish(ins, outs, sems):
        out_ici, in_ici, out_d2d, in_d2d = copies(ins, outs, sems)
        for arrived, onward in zip(in_ici, out_d2d):
            arrived.wait_recv()
            onward.start()
        for arrived in in_d2d:
            arrived.wait_recv()
        for cp in out_ici + out_d2d:
            cp.wait_send()

    return _Carry([own], [jax.ShapeDtypeStruct((n,) + own.shape, own.dtype)], {}, [sem, sem], start, finish)


def _w_in_copies(own_ref, land_ref, send_sems, recv_sems):
    x, y, c = _mesh_pos()
    h = own_ref.shape[0] // 2
    return [pltpu.make_async_remote_copy(
        src_ref=own_ref.at[pl.ds(c * h, h), :], dst_ref=land_ref.at[j, pl.ds(c * h, h), :],
        send_sem=send_sems[j], recv_sem=recv_sems[j], device_id=(*chip, c), device_id_type=MESH)
        for j, chip in enumerate(_other_chips(x, y))]


def _w_in_send(own, after):
    hbm = pl.BlockSpec(memory_space=pltpu.HBM)
    sem = pl.BlockSpec(memory_space=pltpu.SEMAPHORE)
    land_shape = (3,) + own.shape

    def body(own_ref, land_ref, after_ref, s0, s1, s2, r0, r1, r2, own_thru, land_thru, token):
        for cp in _w_in_copies(own_ref, land_ref, (s0, s1, s2), (r0, r1, r2)):
            cp.start()
        token[...] = jnp.zeros_like(token)

    outs = pl.pallas_call(
        body, name="w_in_send",
        out_shape=(pltpu.SemaphoreType.DMA(()),) * 6 + (
            pltpu.HBM(own.shape, own.dtype), pltpu.HBM(land_shape, own.dtype), jax.ShapeDtypeStruct((8, LANES), F32)),
        in_specs=(hbm, hbm, pl.BlockSpec(memory_space=pl.ANY)),
        out_specs=(sem,) * 6 + (hbm, hbm, pl.BlockSpec(memory_space=pltpu.VMEM)),
        input_output_aliases={0: 6, 1: 7},
        compiler_params=pltpu.CompilerParams(has_side_effects=pltpu.SideEffectType.DATAFLOW_SIDE_EFFECTING),
    )(pltpu.with_memory_space_constraint(own, pltpu.HBM),
      pltpu.with_memory_space_constraint(lax.empty(land_shape, own.dtype), pltpu.HBM), after)
    return outs[:6], outs[6], outs[7], outs[8]


def _w_in_wait(sems, own, land, afters):
    hbm = pl.BlockSpec(memory_space=pltpu.HBM)
    sem = pl.BlockSpec(memory_space=pltpu.SEMAPHORE)
    n_after = len(afters)

    def body(own_ref, land_ref, s0, s1, s2, r0, r1, r2, *rest):
        for cp in _w_in_copies(own_ref, land_ref, (s0, s1, s2), (r0, r1, r2)):
            cp.wait_send()
            cp.wait_recv()

    return pl.pallas_call(
        body, name="w_in_wait", out_shape=(pltpu.HBM(own.shape, own.dtype), pltpu.HBM(land.shape, land.dtype)),
        in_specs=(hbm, hbm) + (sem,) * 6 + (pl.BlockSpec(memory_space=pl.ANY),) * n_after, out_specs=(hbm, hbm),
        input_output_aliases={0: 0, 1: 1},
        compiler_params=pltpu.CompilerParams(has_side_effects=pltpu.SideEffectType.DATAFLOW_SIDE_EFFECTING),
    )(own, land, *sems, *afters)


def _exchange_copies(part_refs, land_refs, send_sems, recv_sems):
    x, y, c = _mesh_pos()
    cps = []
    for w, (part, land) in enumerate(zip(part_refs, land_refs)):
        for j, chip in enumerate(_other_chips(x, y)):
            cps.append(pltpu.make_async_remote_copy(
                src_ref=part.at[2 * chip[0] + chip[1]], dst_ref=land.at[j],
                send_sem=send_sems[3 * w + j], recv_sem=recv_sems[3 * w + j],
                device_id=(*chip, c), device_id_type=MESH))
    return cps


def _exchange_send(parts, through, *, name):
    n = len(parts)
    hbm = pl.BlockSpec(memory_space=pltpu.HBM)
    sem = pl.BlockSpec(memory_space=pltpu.SEMAPHORE)
    any_spec = pl.BlockSpec(memory_space=pl.ANY)
    land_shapes = [(3,) + p.shape[1:] for p in parts]

    def body(*refs):
        part_refs, land_refs = refs[:n], refs[n:2 * n]
        sems = refs[2 * n + 1:8 * n + 1]
        for cp in _exchange_copies(part_refs, land_refs, sems[:3 * n], sems[3 * n:]):
            cp.start()

    outs = pl.pallas_call(
        body, name=name,
        out_shape=(pltpu.SemaphoreType.DMA(()),) * (6 * n)
        + tuple(pltpu.HBM(p.shape, p.dtype) for p in parts)
        + tuple(pltpu.HBM(s, p.dtype) for s, p in zip(land_shapes, parts))
        + (jax.ShapeDtypeStruct(through.shape, through.dtype),),
        in_specs=(hbm,) * (2 * n) + (any_spec,), out_specs=(sem,) * (6 * n) + (hbm,) * (2 * n) + (any_spec,),
        input_output_aliases={i: 6 * n + i for i in range(2 * n + 1)},
        compiler_params=pltpu.CompilerParams(has_side_effects=pltpu.SideEffectType.DATAFLOW_SIDE_EFFECTING),
    )(*[pltpu.with_memory_space_constraint(p, pltpu.HBM) for p in parts],
      *[pltpu.with_memory_space_constraint(lax.empty(s, p.dtype), pltpu.HBM) for s, p in zip(land_shapes, parts)],
      through)
    return outs[:6 * n], outs[6 * n:7 * n], outs[7 * n:8 * n], outs[8 * n]


def _exchange_wait(sems, parts, lands, afters, *, name):
    n = len(parts)
    hbm = pl.BlockSpec(memory_space=pltpu.HBM)
    sem = pl.BlockSpec(memory_space=pltpu.SEMAPHORE)

    def body(*refs):
        part_refs, land_refs = refs[:n], refs[n:2 * n]
        sem_refs = refs[2 * n:8 * n]
        for cp in _exchange_copies(part_refs, land_refs, sem_refs[:3 * n], sem_refs[3 * n:]):
            cp.wait_send()
            cp.wait_recv()

    outs = pl.pallas_call(
        body, name=name,
        out_shape=tuple(pltpu.HBM(p.shape, p.dtype) for p in parts) + tuple(pltpu.HBM(l.shape, l.dtype) for l in lands),
        in_specs=(hbm,) * (2 * n) + (sem,) * (6 * n) + (pl.BlockSpec(memory_space=pl.ANY),) * len(afters),
        out_specs=(hbm,) * (2 * n), input_output_aliases={i: i for i in range(2 * n)},
        compiler_params=pltpu.CompilerParams(has_side_effects=pltpu.SideEffectType.DATAFLOW_SIDE_EFFECTING),
    )(*parts, *lands, *sems, *afters)
    return outs[:n], outs[n:]


def _forward_carry(land):
    n = land.shape[0]
    h = land.shape[1] // 2
    sem = pltpu.SemaphoreType.DMA((n,))

    def copies(outs, sems):
        send_sems, recv_sems = sems
        x, y, c = _mesh_pos()
        sends, recvs = [], []
        for j in range(n):
            mine = outs[0].at[j, pl.ds(c * h, h), :]
            other = outs[0].at[j, pl.ds((1 - c) * h, h), :]
            sends.append(pltpu.make_async_remote_copy(
                src_ref=mine, dst_ref=mine, send_sem=send_sems.at[j], recv_sem=recv_sems.at[j],
                device_id=(x, y, 1 - c), device_id_type=MESH))
            recvs.append(pltpu.make_async_remote_copy(
                src_ref=other, dst_ref=other, send_sem=send_sems.at[j], recv_sem=recv_sems.at[j],
                device_id=(x, y, c), device_id_type=MESH))
        return sends, recvs

    def start(ins, outs, sems):
        for cp in copies(outs, sems)[0]:
            cp.start()

    def finish(ins, outs, sems):
        sends, recvs = copies(outs, sems)
        for cp in recvs:
            cp.wait_recv()
        for cp in sends:
            cp.wait_send()

    return _Carry([land], [jax.ShapeDtypeStruct(land.shape, land.dtype)], {0: 0}, [sem, sem], start, finish)


def _swap_carry(dws):
    n = len(dws)
    sem = pltpu.SemaphoreType.DMA((n,))

    def copies(ins, outs, sems):
        send_sems, recv_sems = sems
        x, y, c = _mesh_pos()
        cps = []
        for w in range(n):
            h = dws[w].shape[1] // 2
            cps.append(pltpu.make_async_remote_copy(
                src_ref=ins[w].at[:, pl.ds((1 - c) * h, h), :], dst_ref=outs[w],
                send_sem=send_sems.at[w], recv_sem=recv_sems.at[w],
                device_id=(x, y, 1 - c), device_id_type=MESH))
        return cps

    def start(ins, outs, sems):
        for cp in copies(ins, outs, sems):
            cp.start()

    def finish(ins, outs, sems):
        for cp in copies(ins, outs, sems):
            cp.wait()

    shapes = [jax.ShapeDtypeStruct((s.shape[0], s.shape[1] // 2, s.shape[2]), s.dtype) for s in dws]
    return _Carry(dws, shapes, {}, [sem, sem], start, finish)


def _merge_carries(carries):
    if len(carries) == 1:
        return carries[0]
    inputs, out_shapes, sem_shapes, aliases, spans = [], [], [], {}, []
    for cy in carries:
        i0, o0, s0 = len(inputs), len(out_shapes), len(sem_shapes)
        aliases.update({i0 + i: o0 + o for i, o in cy.aliases.items()})
        inputs += cy.inputs
        out_shapes += cy.out_shapes
        sem_shapes += cy.sem_shapes
        spans.append((slice(i0, len(inputs)), slice(o0, len(out_shapes)), slice(s0, len(sem_shapes))))

    def start(ins, outs, sems):
        for cy, (si, so, ss) in zip(carries, spans):
            cy.start(ins[si], outs[so], sems[ss])

    def finish(ins, outs, sems):
        for cy, (si, so, ss) in zip(carries, spans):
            cy.finish(ins[si], outs[so], sems[ss])

    return _Carry(inputs, out_shapes, aliases, sem_shapes, start, finish)


ALL_CHIPS = (0, 1, 2)


class _MeshComm:
    GATHER_AT = {
        "mm_in_rest": [("w_conv_out", ALL_CHIPS, 0, 1, 1), ("w_glu_a", ALL_CHIPS, 0, 1, 1),
                       ("w_glu_b", ALL_CHIPS, 0, 1, 1)],
        "conv_fwd": [("w_out", ALL_CHIPS, 0, 1, 1), ("w_ff1", ALL_CHIPS, 0, 1, 8)],
        "s5_fwd": [("w_ff1", ALL_CHIPS, 1, 6, 8)],
        "mm_glu_a": [("w_ff1", ALL_CHIPS, 6, 7, 8)],
        "mm_glu_b": [("w_ff1", ALL_CHIPS, 7, 8, 8)],
        "mm_out": [("w_ff2", ALL_CHIPS, 0, 2, 8)],
        "mm_ff1": [("w_ff2", ALL_CHIPS, 2, 8, 8)],
    }
    SWAP_AT = {
        "mm_d_ff2": ["w_ff2"],
        "mm_d_ff1": ["w_ff1"],
        "conv_bwd": ["w_out", "w_glu_a", "w_glu_b", "w_conv_out"],
    }
    EXCHANGE_AT = {}
    EARLY_AT = "mm_dw_in"

    def __init__(self, shards, pos, chip, my_c):
        self.pos = pos
        self.chip = chip
        self.my_c = my_c
        self.shards = shards
        self.w_in_own = _cast_bf16(shards["w_in"], name="cast_w_in")
        self.raw = {}
        self.flights = []
        self.parts = {}
        self.landing = {}
        self.halves = {}
        self.pending = {}
        self.last_site = {}
        for site, items in self.EXCHANGE_AT.items():
            for it in items:
                self.last_site[it[0]] = site

    def weight(self, name):
        g = self.bufs[name]
        return g.reshape(g.shape[0] * g.shape[1], g.shape[2]) if name in ROW_SHARDED else g

    def _slot_ids(self):
        x, y, _ = self.pos
        ids = [2 * x + y] + [2 * cx + cy for cx, cy in _other_chips(x, y)]
        return jnp.stack(ids).astype(jnp.int32)

    def start_w_in(self, after):
        *self.w_in_flight, token = _w_in_send(self.w_in_own, after)
        self.bufs = {n: _cast_into_slot(s, self.chip, token, name="cast_" + n)
                     for n, s in self.shards.items() if n != "w_in"}
        return token

    def mm_in(self, u, afters):
        ids = self._slot_ids()
        sems, own, land = self.w_in_flight
        proj = _mm_slots(u, own[None], ids[0:1], None, name="mm_in_own")
        own, land = _w_in_wait(sems, own, land, [proj] + list(self.bufs.values()) + list(afters))
        land, = _run_carry(_forward_carry(land), name="forward_w_in")
        proj, carried = _mm_slots(u, land, ids[1:4], proj, name="mm_in_rest", carry=self.carry("mm_in_rest"))
        self.done("mm_in_rest", carried)
        self.w_in_rel = jnp.concatenate([own[None], land], axis=0)
        return proj

    def _add_and_send(self, names, landed, site, through):
        parts = [_add_half(self.raw.pop(n), l1, self.my_c, name="add_half_" + n) for n, l1 in zip(names, landed)]
        sems, parts, lands, through = _exchange_send(parts, through, name="exchange_send_" + site)
        self.flights.append((names, sems, parts, lands))
        return through

    def mm_d_in(self, dproj):
        landed = _run_carry(_swap_carry([self.raw["w_in"]]), name="swap_halves_w_in")
        dproj = self._add_and_send(["w_in"], landed, "w_in", dproj)
        return _mm(dproj, self.w_in_rel, mode="nt", out_dtype=F32, name="mm_d_in", a_slots=self._slot_ids())

    def early_grads(self, early):
        self.early = early

    def carry(self, site):
        jobs = []
        if site == self.EARLY_AT:
            flat, self.early_offs = _pack(list(self.early.values()))
            jobs.append(("early", None, _gather_rows_carry(flat.reshape(-1, PACK_COLS))))
        if site in self.GATHER_AT:
            items = self.GATHER_AT[site]
            jobs.append(("gather", items, _gather_carry([(self.bufs[it[0]],) + tuple(it[1:]) for it in items])))
        if site in self.EXCHANGE_AT:
            items = self.EXCHANGE_AT[site]
            jobs.append(("exchange", items, _exchange_carry(
                [(self.parts[it[0]], self.landing.get(it[0])) + tuple(it[1:]) for it in items])))
        if site in self.SWAP_AT:
            names = self.SWAP_AT[site]
            jobs.append(("swap", names, _swap_carry([self.raw[n] for n in names])))
        if not jobs:
            return None
        self.pending[site] = jobs
        return _merge_carries([job[2] for job in jobs])

    def done(self, site, carried, out=None):
        pos = 0
        for kind, items, carry in self.pending.pop(site):
            outs = carried[pos:pos + len(carry.out_shapes)]
            pos += len(carry.out_shapes)
            if kind == "early":
                self.early_all = outs[0]
            elif kind == "gather":
                self.bufs.update(zip([it[0] for it in items], outs))
            elif kind == "swap":
                if isinstance(out, (list, tuple)):
                    out = [self._add_and_send(items, outs, site, out[0])] + list(out[1:])
                else:
                    out = self._add_and_send(items, outs, site, out)
            else:
                for it, landed in zip(items, outs):
                    n = it[0]
                    self.landing[n] = landed
                    if self.last_site[n] == site:
                        self.halves[n] = _sum_into_half(self.parts.pop(n), self.landing.pop(n), self.chip,
                                                        self.my_c, name="sum_chips_" + n)
        return out

    def grad(self, name, dw):
        if name in ROW_SHARDED:
            dw = dw.reshape(N_CHIPS, dw.shape[0] // N_CHIPS, dw.shape[1])
        self.raw[name] = dw

    def join(self, names, afters, *, name):
        for i, (group, sems, parts, lands) in enumerate(self.flights):
            parts, lands = _exchange_wait(sems, parts, lands, afters, name="exchange_wait_%d" % i)
            for n, part, land in zip(group, parts, lands):
                self.halves[n] = _sum_into_half(part, land, self.chip, self.my_c, name="sum_chips_" + n)
        self.flights = []
        return dict(zip(names, _run_carry(_join_carry([self.halves.pop(n) for n in names]), name=name)))


def _local_step(x, target, mod, small, comm):
    rows, d = x.shape
    cw = d // 2
    shift1, scale1, gate1, shift2, scale2, gate2 = mod
    _, _, bbr, bbi = small["s5_disc"]
    b_in, c_out, b_out, c_in, mults = _s5_operands(*small["s5_loglam"], bbr, bbi, small["c_re"], small["c_im"])
    wt = comm.weight

    def riding(site, fn, *args, **kwargs):
        carry = comm.carry(site)
        if carry is None:
            return fn(*args, **kwargs)
        out, carried = fn(*args, carry=carry, **kwargs)
        return comm.done(site, carried, out)

    u = _norm_mod(x, small["norm1_g"], scale1, shift1, name="norm1_fwd")
    proj = comm.mm_in(u, [*b_in, *c_out, *b_out, *c_in, *mults])
    sl, cv = riding("conv_fwd", _conv_fwd, proj, small["w_dw"], small["b_dw"], small["ln_g"], small["ln_b"], cw=cw)
    y_conv = _mm(sl, wt("w_conv_out"), mode="nn", out_dtype=BF16, name="mm_conv_out")
    yg, st_re, st_im = riding("s5_fwd", _s5_fwd, proj, small["d_skip"], b_in, c_out, mults, col0=2 * cw // LANES)
    ya = riding("mm_glu_a", _mm, yg, wt("w_glu_a"), mode="nn", out_dtype=BF16, name="mm_glu_a")
    yb = riding("mm_glu_b", _mm, yg, wt("w_glu_b"), mode="nn", out_dtype=BF16, name="mm_glu_b")
    merged = _merge_fwd(proj, y_conv, ya, yb, cw=cw)
    mo = riding("mm_out", _mm, merged, wt("w_out"), mode="nn", out_dtype=BF16, name="mm_out")
    h1, z = _res_norm(x, mo, gate1, small["norm2_g"], scale2, shift2)
    f1 = riding("mm_ff1", _mm, z, wt("w_ff1"), mode="nn", out_dtype=BF16, name="mm_ff1")
    ff = _mm(f1, wt("w_ff2"), mode="nn", out_dtype=BF16, name="mm_ff2", a_fn=_relu2_bf16)
    dh2, dff, loss, d_final_g, d_gate2 = _final_fwd_bwd(h1, ff, gate2, small["final_g"], target)

    comm.grad("w_ff2", _mm(f1, dff, mode="tn", out_dtype=BF16, name="mm_dw_ff2", a_fn=_relu2_bf16))
    df1 = riding("mm_d_ff2", _mm, dff, wt("w_ff2"), mode="nt", out_dtype=BF16, name="mm_d_ff2", extra=f1,
                 epi=lambda acc, f: acc * (2.0 * jnp.maximum(f.astype(F32), 0.0)))
    comm.grad("w_ff1", riding("mm_dw_ff1", _mm, z, df1, mode="tn", out_dtype=BF16, name="mm_dw_ff1",
                              out_gathered=True))
    dz = riding("mm_d_ff1", _mm, df1, wt("w_ff1"), mode="nt", out_dtype=F32, name="mm_d_ff1")
    dh1, d_shift2, d_scale2, d_norm2_g, dmo, d_gate1 = riding(
        "norm2_bwd", _norm_mod_bwd, dz, h1, dh2, small["norm2_g"], scale2, gate1, mo, name="norm2_bwd")
    comm.grad("w_out", riding("mm_dw_out", _mm, merged, dmo, mode="tn", out_dtype=BF16, name="mm_dw_out"))
    dmerged = riding("mm_d_out", _mm, dmo, wt("w_out"), mode="nt", out_dtype=BF16, name="mm_d_out")
    dproj, dy_conv, dya, dyb = riding("merge_bwd", _merge_bwd, dmerged, proj, y_conv, ya, yb, cw=cw)
    comm.grad("w_glu_a", _mm(yg, dya, mode="tn", out_dtype=BF16, name="mm_dw_glu_a", out_gathered=True))
    comm.grad("w_glu_b", _mm(yg, dyb, mode="tn", out_dtype=BF16, name="mm_dw_glu_b", out_gathered=True))
    dyg_a = _mm(dya, wt("w_glu_a"), mode="nt", out_dtype=F32, name="mm_d_glu_a")
    dyg = _mm(dyb, wt("w_glu_b"), mode="nt", out_dtype=F32, name="mm_d_glu_b", extra=dyg_a,
              epi=lambda acc, e: acc + e)
    comm.grad("w_conv_out", _mm(sl, dy_conv, mode="tn", out_dtype=BF16, name="mm_dw_conv_out", out_gathered=True))
    dsl = _mm(dy_conv, wt("w_conv_out"), mode="nt", out_dtype=F32, name="mm_d_conv_out")
    dcv, d_ln_g, d_ln_b = _ln_bwd(dsl, cv, small["ln_g"], small["ln_b"])
    dproj, d_w_dw, d_b_dw = riding("conv_bwd", _conv_bwd, dcv, proj, small["w_dw"], dproj, cw=cw)
    dproj, d_d_skip, dbr, dbi, dcr, dci, dlr, dli = riding(
        "s5_bwd", _s5_bwd, proj, dyg, small["d_skip"], (st_re, st_im), c_out, b_out, c_in, mults, dproj,
        col0=2 * cw // LANES)
    sw = lambda m: jnp.swapaxes(m, 1, 2)
    early = {
        "dmod_tail": jnp.concatenate([d_gate1, d_shift2, d_scale2, d_gate2], axis=1), "loss": loss[:, 0:1],
        "w_dw": d_w_dw, "b_dw": d_b_dw, "ln_g": d_ln_g, "ln_b": d_ln_b,
        "lam_re": dlr.reshape(-1, SSM_STATE), "lam_im": dli.reshape(-1, SSM_STATE),
        "bb_re": sw(_block_diag_extract(dbr, SSM_GROUP, SSM_STATE)),
        "bb_im": sw(_block_diag_extract(dbi, SSM_GROUP, SSM_STATE)),
        "c_re": sw(_block_diag_extract(dcr, SSM_STATE, SSM_GROUP)),
        "c_im": sw(_block_diag_extract(dci, SSM_STATE, SSM_GROUP)),
        "d_skip": d_d_skip, "norm2_g": d_norm2_g, "final_g": d_final_g,
    }
    comm.early_grads(early)
    comm.grad("w_in", riding("mm_dw_in", _mm, u, dproj, mode="tn", out_dtype=BF16, name="mm_dw_in",
                             out_gathered=True))
    du = comm.mm_d_in(dproj)
    grad_x, d_shift1, d_scale1, d_norm1_g = riding(
        "norm1_bwd", _norm_mod_bwd, du, x, dh1, small["norm1_g"], scale1, None, None, name="norm1_bwd")
    late ={"dmod_head": jnp.concatenate([d_shift1, d_scale1], axis=1), "norm1_g": d_norm1_g}
    return grad_x, early, late


WEIGHT_NAMES = ["w_ada", "b_ada", "norm1_g", "w_in", "w_dw", "b_dw", "ln_g", "ln_b", "w_conv_out", "a_re", "a_im",
                "log_dt", "b_re", "b_im", "c_re", "c_im", "d_skip", "w_glu_a", "w_glu_b", "w_out", "norm2_g",
                "w_ff1", "w_ff2", "final_g"]
BIG_NAMES = ["w_in", "w_conv_out", "w_glu_a", "w_glu_b", "w_out", "w_ff1", "w_ff2"]
ROW_SHARDED = ("w_out", "w_ff2")
PACK_COLS = 1024
PACK_TILE = SUBLANES * PACK_COLS


def _pack(arrays):
    flats = [a.reshape(-1) for a in arrays]
    offs = []
    total = 0
    for f in flats:
        offs.append(total)
        total += f.shape[0]
    pad = (-total) % PACK_TILE
    if pad:
        flats.append(jnp.zeros((pad,), F32))
    return jnp.concatenate(flats), offs


def _unpack(flat, offs, like):
    return [flat[o:o + a.size].reshape(a.shape) for o, a in zip(offs, like)]


def _gather_w_dw(w_shard):
    k, n = w_shard.shape
    padded = jnp.pad(w_shard, ((0, HALO - k), (0, 0)))
    allw = _gather_small(padded, name="gather_w_dw").reshape(N_CHIPS, 2, HALO, n)[:, 0, :k]
    return jnp.moveaxis(allw, 0, 1).reshape(k, N_CHIPS * n)


def kernel(x, c, w_ada, b_ada, norm1_g, w_in, w_dw, b_dw, ln_g, ln_b, w_conv_out, a_re, a_im, log_dt, b_re, b_im, c_re, c_im, d_skip, w_glu_a, w_glu_b, w_out, norm2_g, w_ff1, w_ff2, final_g, loss_target, m_w_ada, m_b_ada, m_norm1_g, m_w_in, m_w_dw, m_b_dw, m_ln_g, m_ln_b, m_w_conv_out, m_a_re, m_a_im, m_log_dt, m_b_re, m_b_im, m_c_re, m_c_im, m_d_skip, m_w_glu_a, m_w_glu_b, m_w_out, m_norm2_g, m_w_ff1, m_w_ff2, m_final_g, v_w_ada, v_b_ada, v_norm1_g, v_w_in, v_w_dw, v_b_dw, v_ln_g, v_ln_b, v_w_conv_out, v_a_re, v_a_im, v_log_dt, v_b_re, v_b_im, v_c_re, v_c_im, v_d_skip, v_w_glu_a, v_w_glu_b, v_w_out, v_norm2_g, v_w_ff1, v_w_ff2, v_final_g):
    given = dict(locals())
    w = {n: given[n] for n in WEIGHT_NAMES}
    m = {n: given["m_" + n] for n in WEIGHT_NAMES}
    v = {n: given["v_" + n] for n in WEIGHT_NAMES}
    d = x.shape[2]
    xi, yi, ci = _mesh_pos()
    chip = 2 * xi + yi
    dev = 4 * xi + 2 * yi + ci
    my_c = jnp.reshape(ci, (1,)).astype(jnp.int32)
    chip_arr = jnp.reshape(chip, (1,)).astype(jnp.int32)

    comm = _MeshComm({n: w[n][0] for n in BIG_NAMES}, (xi, yi, ci), chip_arr, my_c)

    ndw = w_dw.shape[2]
    assert d // SUBLANES == ndw
    first = jnp.concatenate([c.reshape(SUBLANES, ndw), jnp.pad(w_dw[0], ((0, HALO - CONV_KERNEL), (0, 0)))])
    first_all = _gather_small(first, name="gather_c_w_dw").reshape(N_DEV, SUBLANES + HALO, ndw)
    c_all = first_all[:, :SUBLANES].reshape(N_DEV, d)
    taps = first_all.reshape(N_CHIPS, 2, SUBLANES + HALO, ndw)[:, 0, SUBLANES:SUBLANES + CONV_KERNEL]
    w_dw_full = jnp.moveaxis(taps, 0, 1).reshape(CONV_KERNEL, N_CHIPS * ndw)

    nmod = w_ada.shape[2]
    b_cols = lax.dynamic_slice(b_ada, (0, chip * nmod), (1, nmod))
    mod_part = _ada_fwd(c_all, w_ada[0], b_cols)
    mod_all = _gather_small(mod_part, name="gather_mod").reshape(N_CHIPS, 2, N_DEV, nmod)[:, 0]
    mod_full = jnp.moveaxis(mod_all, 0, 1).reshape(N_DEV, N_CHIPS * nmod)
    mod_row = lax.dynamic_slice(mod_full, (dev, 0), (1, N_CHIPS * nmod))
    mod = [mod_row[:, i * d:(i + 1) * d] for i in range(6)]

    token = comm.start_w_in(mod_row)
    log_dt_0 = log_dt[0] + token[0, 0]

    disc_in = (a_re[0], a_im[0], log_dt_0, b_re[0], b_im[0])
    disc, disc_vjp = jax.vjp(_s5_discretise, *disc_in)
    dt = jnp.exp(log_dt_0)[:, None]
    small = {"norm1_g": norm1_g, "w_dw": w_dw_full, "b_dw": b_dw, "ln_g": ln_g, "ln_b": ln_b,
             "c_re": c_re[0], "c_im": c_im[0], "d_skip": d_skip, "norm2_g": norm2_g,
             "final_g": final_g[None, :], "s5_disc": disc, "s5_loglam": (a_re[0] * dt, a_im[0] * dt)}

    grad_x, early, late = _local_step(x[0], loss_target[0], mod, small, comm)
    grads = {}

    early_all = comm.early_all.reshape(N_DEV, -1, PACK_COLS)
    early_sum = _sum_leading(early_all, name="sum_small_grads").reshape(-1)
    summed = dict(zip(early, _unpack(early_sum, comm.early_offs, list(early.values()))))
    flat, late_offs = _pack(list(late.values()))
    late_all = _gather_small(flat.reshape(-1, PACK_COLS), name="gather_late_grads").reshape(N_DEV, -1, PACK_COLS)
    late_sum = _sum_leading(late_all, name="sum_late_grads").reshape(-1)
    summed.update(zip(late, _unpack(late_sum, late_offs, list(late.values()))))
    head = late_all[:, :2 * d // PACK_COLS].reshape(N_DEV, 2 * d)
    tail = early_all[:, :4 * d // PACK_COLS].reshape(N_DEV, 4 * d)
    dmod_all = jnp.concatenate([head, tail], axis=1)

    grads["w_ada"] = _ada_bwd(c_all, lax.dynamic_slice(dmod_all, (0, chip * nmod), (N_DEV, nmod)))
    grads["b_ada"] = _sum_leading(dmod_all.reshape(N_DEV, SUBLANES, 6 * d // SUBLANES),
                                  name="sum_b_ada").reshape(1, 6 * d)
    da_re, da_im, dlog_dt, db_re, db_im = disc_vjp(
        (summed["lam_re"], summed["lam_im"], summed["bb_re"], summed["bb_im"]))
    grads.update({
        "norm1_g": summed["norm1_g"], "w_dw": lax.dynamic_slice(summed["w_dw"], (0, chip * ndw), (CONV_KERNEL, ndw)),
        "b_dw": summed["b_dw"], "ln_g": summed["ln_g"], "ln_b": summed["ln_b"],
        "a_re": da_re, "a_im": da_im, "log_dt": dlog_dt, "b_re": db_re, "b_im": db_im,
        "c_re": summed["c_re"], "c_im": summed["c_im"], "d_skip": summed["d_skip"],
        "norm2_g": summed["norm2_g"], "final_g": summed["final_g"],
    })

    delta, new_m, new_v = {}, {}, {}
    grads.update(comm.join(BIG_NAMES, [late_all], name="join_halves"))
    for n in ["w_ada"] + BIG_NAMES:
        shp = w[n].shape
        two_d = lambda a: a.reshape(shp[1], shp[2])
        res = _adamw(two_d(w[n]), two_d(grads[n]), two_d(m[n]), two_d(v[n]), name="adamw_" + n)
        delta[n], new_m[n], new_v[n] = [r.reshape(shp) for r in res]
    grads = {n: grads[n].reshape(w[n].shape) for n in WEIGHT_NAMES}
    rest = [n for n in WEIGHT_NAMES if n not in delta]
    packs = []
    for src in (w, grads, m, v):
        flat, offs = _pack([src[n] for n in rest])
        packs.append(flat.reshape(-1, 1024))
    outs = _adamw(*packs, name="adamw_small")
    for dst, o in zip((delta, new_m, new_v), outs):
        for n, a in zip(rest, _unpack(o.reshape(-1), offs, [w[k] for k in rest])):
            dst[n] = a

    return (summed["loss"].reshape(()), grad_x[None], *[grads[n] for n in WEIGHT_NAMES],
            *[delta[n] for n in WEIGHT_NAMES], *[new_m[n] for n in WEIGHT_NAMES],
            *[new_v[n] for n in WEIGHT_NAMES])
```

```python
import functools
import math

import jax
import jax.numpy as jnp
from jax import lax
from jax.experimental import pallas as pl
from jax.experimental.pallas import tpu as pltpu

F32 = jnp.float32
BF16 = jnp.bfloat16
EPS = 1e-6
CONV_KERNEL = 31
SSM_GROUP = 16
SSM_STATE = 64
ADAM_LR = 0.001
ADAM_B1 = 0.9
ADAM_B2 = 0.999
ADAM_EPS = 1e-08
ADAM_WD = 0.01
ADAM_STEP = 10

N_CHIPS = 4
N_DEV = 8
VMEM_LIMIT_BYTES = 56 * 1024 * 1024
LANES = 128
SUBLANES = 8
HALO = 32
GROUPS_PER_BLOCK = LANES // SSM_GROUP
STATE_LANES = GROUPS_PER_BLOCK * SSM_STATE
MESH = pl.DeviceIdType.MESH


def _cparams(sem):
    return pltpu.CompilerParams(dimension_semantics=sem, vmem_limit_bytes=VMEM_LIMIT_BYTES)


def _pick(n, pref, mult=LANES):
    if n <= pref:
        return n
    best = None
    for d in range(mult, pref + 1, mult):
        if n % d == 0:
            best = d
    assert best is not None, (n, pref)
    return best


def _sigmoid(v):
    return 1.0 / (1.0 + jnp.exp(-v))


def _gelu_parts(v):
    k0 = math.sqrt(2.0 / math.pi)
    inner = k0 * (v + 0.044715 * v * v * v)
    t = jnp.tanh(inner)
    return k0, t


def _gelu(v):
    _, t = _gelu_parts(v)
    return 0.5 * v * (1.0 + t)


def _gelu_grad(v):
    k0, t = _gelu_parts(v)
    return 0.5 * (1.0 + t) + 0.5 * v * (1.0 - t * t) * k0 * (1.0 + 3.0 * 0.044715 * v * v)


def _relu2_bf16(a):
    t = jnp.maximum(a.astype(F32), 0.0)
    return (t * t).astype(BF16)


class _Carry:
    def __init__(self, inputs, out_shapes, aliases, sem_shapes, start, finish):
        self.inputs = list(inputs)
        self.out_shapes = list(out_shapes)
        self.aliases = dict(aliases)
        self.sem_shapes = list(sem_shapes)
        self.start = start
        self.finish = finish


def _call(body, *, grid, in_specs, out_specs, out_shape, scratch_shapes, semantics, name, args, carry=None,
          prefetch=(), aliases=None):
    n_in, n_out, n_scr, n_pf = len(in_specs), len(out_specs), len(scratch_shapes), len(prefetch)
    own_aliases = {n_pf + i: o for i, o in (aliases or {}).items()}
    if carry is None:
        gs = pltpu.PrefetchScalarGridSpec(
            num_scalar_prefetch=n_pf, grid=grid, in_specs=in_specs, out_specs=out_specs,
            scratch_shapes=scratch_shapes)
        outs = pl.pallas_call(
            body, grid_spec=gs, out_shape=out_shape, input_output_aliases=own_aliases,
            compiler_params=_cparams(semantics), name=name)(*prefetch, *args)
        return list(outs), []
    ci, co = len(carry.inputs), len(carry.out_shapes)

    def wrapped(*refs):
        pf, refs = refs[:n_pf], refs[n_pf:]
        ins, cins = refs[:n_in], refs[n_in:n_in + ci]
        p = n_in + ci
        outs, couts = refs[p:p + n_out], refs[p + n_out:p + n_out + co]
        p += n_out + co
        scr, csems = refs[p:p + n_scr], refs[p + n_scr:]
        first = pl.program_id(0) == 0
        last = pl.program_id(0) == grid[0] - 1
        for ax in range(1, len(grid)):
            first = jnp.logical_and(first, pl.program_id(ax) == 0)
            last = jnp.logical_and(last, pl.program_id(ax) == grid[ax] - 1)

        @pl.when(first)
        def _():
            carry.start(cins, couts, csems)

        body(*pf, *ins, *outs, *scr)

        @pl.when(last)
        def _():
            carry.finish(cins, couts, csems)

    any_spec = pl.BlockSpec(memory_space=pl.ANY)
    gs = pltpu.PrefetchScalarGridSpec(
        num_scalar_prefetch=n_pf, grid=grid, in_specs=list(in_specs) + [any_spec] * ci,
        out_specs=list(out_specs) + [any_spec] * co, scratch_shapes=list(scratch_shapes) + carry.sem_shapes)
    all_aliases = dict(own_aliases)
    all_aliases.update({n_pf + n_in + i: n_out + o for i, o in carry.aliases.items()})
    outs = pl.pallas_call(
        wrapped, grid_spec=gs, out_shape=list(out_shape) + carry.out_shapes, input_output_aliases=all_aliases,
        compiler_params=_cparams(("arbitrary",) * len(grid)), name=name)(*prefetch, *args, *carry.inputs)
    return list(outs[:n_out]), list(outs[n_out:])


def _run_carry(carry, *, name):
    ci = len(carry.inputs)

    def body(*refs):
        cins, couts, csems = refs[:ci], refs[ci:ci + len(carry.out_shapes)], refs[ci + len(carry.out_shapes):]
        carry.start(cins, couts, csems)
        carry.finish(cins, couts, csems)

    any_spec = pl.BlockSpec(memory_space=pl.ANY)
    outs = pl.pallas_call(
        body, in_specs=[any_spec] * ci, out_specs=[any_spec] * len(carry.out_shapes), out_shape=carry.out_shapes,
        scratch_shapes=carry.sem_shapes, input_output_aliases=carry.aliases, name=name)(*carry.inputs)
    return list(outs)


def _mm(a, b, *, mode, out_dtype, name, out_gathered=False, a_fn=None, epi=None, extra=None,
        bm_pref=1024, bn_pref=1024, bk_pref=2048, carry=None, a_slots=None):
    gathered = (b.ndim == 3)
    if mode == "nn":
        m, kdim = a.shape
        ns = b.shape[-1]
        n = ns * (N_CHIPS if gathered else 1)
        bm, bn, bk = _pick(m, bm_pref), _pick(ns, bn_pref), _pick(kdim, bk_pref)
        npb = ns // bn
        grid = (m // bm, n // bn, kdim // bk)
        a_spec = pl.BlockSpec((bm, bk), lambda i, j, k: (i, k))
        if gathered:
            b_spec = pl.BlockSpec((None, bk, bn), lambda i, j, k: (j // npb, k, j % npb))
        else:
            b_spec = pl.BlockSpec((bk, bn), lambda i, j, k: (k, j))
        o_spec = pl.BlockSpec((bm, bn), lambda i, j, k: (i, j))
        e_spec = pl.BlockSpec((bm, bn), lambda i, j, k: (i, j))
        out_shape = (m, n)
        acc_shape = (bm, bn)
        dims = (((1,), (0,)), ((), ()))
    elif mode == "nt":
        m = a.shape[0]
        kdim, ns = b.shape[-2], b.shape[-1]
        n = ns * (N_CHIPS if gathered else 1)
        assert a.shape[1] == n
        bm, bko, bnr = _pick(m, bm_pref), _pick(kdim, bn_pref), _pick(ns, bk_pref)
        npb = ns // bnr
        grid = (m // bm, kdim // bko, n // bnr)
        a_spec = pl.BlockSpec((bm, bnr), lambda i, j, k: (i, k))
        if gathered:
            b_spec = pl.BlockSpec((None, bko, bnr), lambda i, j, k: (k // npb, j, k % npb))
        else:
            b_spec = pl.BlockSpec((bko, bnr), lambda i, j, k: (j, k))
        o_spec = pl.BlockSpec((bm, bko), lambda i, j, k: (i, j))
        e_spec = pl.BlockSpec((bm, bko), lambda i, j, k: (i, j))
        if a_slots is not None:
            assert gathered and extra is None
            a_spec = pl.BlockSpec((bm, bnr), lambda i, j, k, s_ref: (i, s_ref[k // npb] * npb + k % npb))
            b_spec = pl.BlockSpec((None, bko, bnr), lambda i, j, k, s_ref: (k // npb, j, k % npb))
            o_spec = pl.BlockSpec((bm, bko), lambda i, j, k, s_ref: (i, j))
        out_shape = (m, kdim)
        acc_shape = (bm, bko)
        dims = (((1,), (1,)), ((), ()))
    else:
        m, kdim = a.shape
        n = b.shape[1]
        ns = n // N_CHIPS if out_gathered else n
        bmr, bko, bn = _pick(m, bk_pref), _pick(kdim, bm_pref), _pick(ns, bn_pref)
        npb = ns // bn
        grid = (kdim // bko, n // bn, m // bmr)
        a_spec = pl.BlockSpec((bmr, bko), lambda i, j, k: (k, i))
        b_spec = pl.BlockSpec((bmr, bn), lambda i, j, k: (k, j))
        if out_gathered:
            o_spec = pl.BlockSpec((None, bko, bn), lambda i, j, k: (j // npb, i, j % npb))
            out_shape = (N_CHIPS, kdim, ns)
        else:
            o_spec = pl.BlockSpec((bko, bn), lambda i, j, k: (i, j))
            out_shape = (kdim, n)
        e_spec = None
        acc_shape = (bko, bn)
        dims = (((0,), (0,)), ((), ()))
    nk = grid[2]

    def body(*refs):
        if a_slots is not None:
            refs = refs[1:]
        if extra is not None:
            a_ref, b_ref, e_ref, o_ref, acc = refs
        else:
            a_ref, b_ref, o_ref, acc = refs
            e_ref = None
        k = pl.program_id(2)
        av = a_ref[...]
        if a_fn is not None:
            av = a_fn(av)
        part = lax.dot_general(av, b_ref[...], dims, preferred_element_type=F32)

        def finish(r):
            if epi is not None:
                r = epi(r, e_ref[...])
            o_ref[...] = r.astype(o_ref.dtype)

        if nk == 1:
            finish(part)
            return

        @pl.when(k == 0)
        def _():
            acc[...] = part

        @pl.when(jnp.logical_and(k > 0, k < nk - 1))
        def _():
            acc[...] += part

        @pl.when(k == nk - 1)
        def _():
            finish(acc[...] + part)

    in_specs = [a_spec, b_spec]
    args = [a, b]
    if extra is not None:
        in_specs.append(e_spec)
        args.append(extra)
    outs, carried = _call(body, grid=grid, in_specs=in_specs, out_specs=[o_spec],
                          out_shape=[jax.ShapeDtypeStruct(out_shape, out_dtype)],
                          scratch_shapes=[pltpu.VMEM(acc_shape, F32)],
                          semantics=("parallel", "parallel", "arbitrary"), name=name, args=args, carry=carry,
                          prefetch=() if a_slots is None else (a_slots,))
    return outs[0] if carry is None else (outs[0], carried)


def _mm_slots(a, wbuf, slots, prev, *, name, carry=None):
    m, kdim = a.shape
    ns = wbuf.shape[2]
    bm, bn = _pick(m, 1024), _pick(ns, 1024)
    npb = ns // bn
    grid = (m // bm, slots.shape[0], npb)

    def body(s_ref, a_ref, b_ref, *rest):
        o_ref = rest[-1]
        o_ref[...] = _dot(a_ref[...], b_ref[...]).astype(o_ref.dtype)

    in_specs = [pl.BlockSpec((bm, kdim), lambda i, s, j, s_ref: (i, 0)),
                pl.BlockSpec((None, kdim, bn), lambda i, s, j, s_ref: (s, 0, j))]
    args = [a, wbuf]
    aliases = None
    if prev is not None:
        in_specs.append(pl.BlockSpec(memory_space=pl.ANY))
        args.append(prev)
        aliases = {2: 0}
    outs, carried = _call(
        body, grid=grid, in_specs=in_specs,
        out_specs=[pl.BlockSpec((bm, bn), lambda i, s, j, s_ref: (i, s_ref[s] * npb + j))],
        out_shape=[jax.ShapeDtypeStruct((m, N_CHIPS * ns), BF16)], scratch_shapes=[],
        semantics=("parallel", "arbitrary", "arbitrary"), name=name, args=args, carry=carry,
        prefetch=(slots,), aliases=aliases)
    return outs[0] if carry is None else (outs[0], carried)


def _row_tile(rows, cols, n_arrays):
    budget = VMEM_LIMIT_BYTES // 3
    cap = min(512, budget // (n_arrays * 2 * cols * 4))
    for t in range(cap - cap % SUBLANES, 0, -SUBLANES):
        if rows % t == 0:
            return t
    return rows


def _norm_mod(x, g, scale, shift, *, name):
    rows, d = x.shape
    tr = _row_tile(rows, d, 3)

    def body(x_ref, g_ref, sc_ref, sh_ref, o_ref):
        xv = x_ref[...]
        r = lax.rsqrt(jnp.mean(xv * xv, axis=-1, keepdims=True) + EPS)
        o_ref[...] = ((xv * r * g_ref[...]) * (1.0 + sc_ref[...]) + sh_ref[...]).astype(o_ref.dtype)

    row = pl.BlockSpec((tr, d), lambda i: (i, 0))
    vec = pl.BlockSpec((1, d), lambda i: (0, 0))
    return pl.pallas_call(
        body, grid=(rows // tr,), in_specs=[row, vec, vec, vec], out_specs=row,
        out_shape=jax.ShapeDtypeStruct((rows, d), BF16),
        compiler_params=_cparams(("parallel",)), name=name)(x, g, scale, shift)


CONV_CHUNK = 2 * SUBLANES


def _shifted_copies(buf, n):
    for r in range(1, SUBLANES):
        buf[r, pl.ds(0, n - SUBLANES), :] = buf[0, pl.ds(r, n - SUBLANES), :]


def _conv_fwd(proj, w_dw, b_dw, ln_g, ln_b, *, cw, carry=None):
    rows = proj.shape[0]
    tt = _pick(rows, 256, HALO)
    hb = tt // HALO

    def body(a_ref, g_ref, ha_ref, hg_ref, w_ref, b_ref, lg_ref, lb_ref, sl_ref, cv_ref, vs):
        i = pl.program_id(0)
        hv = ha_ref[...].astype(F32) * _sigmoid(hg_ref[...].astype(F32))
        vs[0, pl.ds(0, HALO), :] = jnp.where(i == 0, 0.0, hv)
        vs[0, pl.ds(HALO, tt), :] = a_ref[...].astype(F32) * _sigmoid(g_ref[...].astype(F32))
        _shifted_copies(vs, HALO + tt)

        def chunk(ci, carry):
            r0 = pl.multiple_of(ci * CONV_CHUNK, CONV_CHUNK)
            acc = jnp.broadcast_to(b_ref[...], (CONV_CHUNK, cw))
            for k in range(CONV_KERNEL):
                q, r = divmod(HALO - (CONV_KERNEL - 1) + k, SUBLANES)
                acc = acc + w_ref[pl.ds(k, 1), :] * vs[r, pl.ds(r0 + q * SUBLANES, CONV_CHUNK), :]
            cv_ref[pl.ds(r0, CONV_CHUNK), :] = acc
            return carry

        lax.fori_loop(0, tt // CONV_CHUNK, chunk, 0)
        acc = cv_ref[...]
        mu = jnp.mean(acc, axis=-1, keepdims=True)
        xc = acc - mu
        rstd = lax.rsqrt(jnp.mean(xc * xc, axis=-1, keepdims=True) + EPS)
        ln = xc * rstd * lg_ref[...] + lb_ref[...]
        sl_ref[...] = (ln * _sigmoid(ln)).astype(sl_ref.dtype)

    tile = lambda c: pl.BlockSpec((tt, cw), lambda i, c=c: (i, c))
    halo = lambda c: pl.BlockSpec((HALO, cw), lambda i, c=c: (jnp.maximum(i * hb - 1, 0), c))
    vec = pl.BlockSpec((1, cw), lambda i: (0, 0))
    outs, carried = _call(
        body, grid=(rows // tt,),
        in_specs=[tile(0), tile(1), halo(0), halo(1),
                  pl.BlockSpec((CONV_KERNEL, cw), lambda i: (0, 0)), vec, vec, vec],
        out_specs=[pl.BlockSpec((tt, cw), lambda i: (i, 0)), pl.BlockSpec((tt, cw), lambda i: (i, 0))],
        out_shape=[jax.ShapeDtypeStruct((rows, cw), BF16), jax.ShapeDtypeStruct((rows, cw), F32)],
        scratch_shapes=[pltpu.VMEM((SUBLANES, HALO + tt, cw), F32)],
        semantics=("parallel",), name="conv_fwd", args=[proj, proj, proj, proj, w_dw, b_dw, ln_g, ln_b],
        carry=carry)
    return outs if carry is None else (outs, carried)


def _ln_bwd(dsl, cv, ln_g, ln_b):
    rows, cw = cv.shape
    tr = _row_tile(rows, cw, 3)

    def body(d_ref, cv_ref, lg_ref, lb_ref, o_ref, dg_ref, db_ref):
        i = pl.program_id(0)

        @pl.when(i == 0)
        def _():
            dg_ref[...] = jnp.zeros_like(dg_ref)
            db_ref[...] = jnp.zeros_like(db_ref)

        x = cv_ref[...]
        mu = jnp.mean(x, axis=-1, keepdims=True)
        xc = x - mu
        rstd = lax.rsqrt(jnp.mean(xc * xc, axis=-1, keepdims=True) + EPS)
        xh = xc * rstd
        ln = xh * lg_ref[...] + lb_ref[...]
        s = _sigmoid(ln)
        dln = d_ref[...].astype(F32) * (s * (1.0 + ln * (1.0 - s)))
        dg_ref[...] += jnp.sum(dln * xh, axis=0, keepdims=True)
        db_ref[...] += jnp.sum(dln, axis=0, keepdims=True)
        dxh = dln * lg_ref[...]
        m1 = jnp.mean(dxh, axis=-1, keepdims=True)
        m2 = jnp.mean(dxh * xh, axis=-1, keepdims=True)
        o_ref[...] = rstd * (dxh - m1 - xh * m2)

    row = pl.BlockSpec((tr, cw), lambda i: (i, 0))
    vec = pl.BlockSpec((1, cw), lambda i: (0, 0))
    return pl.pallas_call(
        body, grid=(rows // tr,), in_specs=[row, row, vec, vec], out_specs=[row, vec, vec],
        out_shape=[jax.ShapeDtypeStruct((rows, cw), F32), jax.ShapeDtypeStruct((1, cw), F32),
                   jax.ShapeDtypeStruct((1, cw), F32)],
        compiler_params=_cparams(("arbitrary",)), name="ln_bwd")(dsl, cv, ln_g, ln_b)


def _conv_bwd(dcv, proj, w_dw, dproj, *, cw, carry=None):
    rows = proj.shape[0]
    tt = _pick(rows, 256, HALO)
    hb = tt // HALO
    nt = rows // tt
    taps = CONV_KERNEL

    def body(d_ref, dn_ref, a_ref, g_ref, ha_ref, hg_ref, w_ref, dproj_ref, o_ref, dw_ref, db_ref, vs, ds):
        i = pl.program_id(0)

        @pl.when(i == 0)
        def _():
            dw_ref[...] = jnp.zeros_like(dw_ref)
            db_ref[...] = jnp.zeros_like(db_ref)

        hv = ha_ref[...].astype(F32) * _sigmoid(hg_ref[...].astype(F32))
        vs[0, pl.ds(0, HALO), :] = jnp.where(i == 0, 0.0, hv)
        vs[0, pl.ds(HALO, tt), :] = a_ref[...].astype(F32) * _sigmoid(g_ref[...].astype(F32))
        _shifted_copies(vs, HALO + tt)
        ds[0, pl.ds(0, tt), :] = d_ref[...]
        ds[0, pl.ds(tt, HALO), :] = jnp.where(i == nt - 1, 0.0, dn_ref[...])
        _shifted_copies(ds, tt + HALO)
        db_ref[...] += jnp.sum(d_ref[...], axis=0, keepdims=True)
        for k in range(taps):
            q, r = divmod(HALO - (taps - 1) + k, SUBLANES)
            dw_ref[pl.ds(k, 1), :] += jnp.sum(d_ref[...] * vs[r, pl.ds(q * SUBLANES, tt), :], axis=0, keepdims=True)

        def chunk(ci, carry):
            r0 = pl.multiple_of(ci * CONV_CHUNK, CONV_CHUNK)
            dv = jnp.zeros((CONV_CHUNK, cw), F32)
            for k in range(taps):
                q, r = divmod(taps - 1 - k, SUBLANES)
                dv = dv + w_ref[pl.ds(k, 1), :] * ds[r, pl.ds(r0 + q * SUBLANES, CONV_CHUNK), :]
            av = a_ref[pl.ds(r0, CONV_CHUNK), :].astype(F32)
            sg = _sigmoid(g_ref[pl.ds(r0, CONV_CHUNK), :].astype(F32))
            o_ref[pl.ds(r0, CONV_CHUNK), pl.ds(0, cw)] = (dv * sg).astype(o_ref.dtype)
            o_ref[pl.ds(r0, CONV_CHUNK), pl.ds(cw, cw)] = (dv * av * sg * (1.0 - sg)).astype(o_ref.dtype)
            return carry

        lax.fori_loop(0, tt // CONV_CHUNK, chunk, 0)

    tile = lambda c: pl.BlockSpec((tt, cw), lambda i, c=c: (i, c))
    halo = lambda c: pl.BlockSpec((HALO, cw), lambda i, c=c: (jnp.maximum(i * hb - 1, 0), c))
    nxt = pl.BlockSpec((HALO, cw), lambda i: (jnp.minimum((i + 1) * hb, nt * hb - 1), 0))
    outs, carried = _call(
        body, grid=(nt,),
        in_specs=[pl.BlockSpec((tt, cw), lambda i: (i, 0)), nxt, tile(0), tile(1), halo(0), halo(1),
                  pl.BlockSpec((taps, cw), lambda i: (0, 0)), pl.BlockSpec(memory_space=pl.ANY)],
        out_specs=[pl.BlockSpec((tt, 2 * cw), lambda i: (i, 0)),
                   pl.BlockSpec((taps, cw), lambda i: (0, 0)), pl.BlockSpec((1, cw), lambda i: (0, 0))],
        out_shape=[jax.ShapeDtypeStruct(dproj.shape, dproj.dtype), jax.ShapeDtypeStruct((taps, cw), F32),
                   jax.ShapeDtypeStruct((1, cw), F32)],
        scratch_shapes=[pltpu.VMEM((SUBLANES, HALO + tt, cw), F32), pltpu.VMEM((SUBLANES, tt + HALO, cw), F32)],
        semantics=("arbitrary",), name="conv_bwd", args=[dcv, dcv, proj, proj, proj, proj, w_dw, dproj],
        carry=carry, aliases={7: 0})
    return outs if carry is None else (outs, carried)


def _merge_fwd(proj, y_conv, ya, yb, *, cw):
    rows = proj.shape[0]
    tr = _row_tile(rows, cw, 4)

    def body(gc_ref, gs_ref, yc_ref, ya_ref, yb_ref, o_ref):
        ys = ya_ref[...].astype(F32) * _sigmoid(yb_ref[...].astype(F32))
        o_ref[...] = (_sigmoid(gc_ref[...].astype(F32)) * yc_ref[...].astype(F32)
                      + _sigmoid(gs_ref[...].astype(F32)) * ys).astype(o_ref.dtype)

    blk = lambda off: pl.BlockSpec((tr, cw), lambda i, h, off=off: (i, off + h))
    return pl.pallas_call(
        body, grid=(rows // tr, 2), in_specs=[blk(3), blk(5), blk(0), blk(0), blk(0)], out_specs=blk(0),
        out_shape=jax.ShapeDtypeStruct((rows, 2 * cw), BF16),
        compiler_params=_cparams(("parallel", "parallel")), name="merge_fwd")(proj, proj, y_conv, ya, yb)


def _merge_bwd(dmerged, proj, y_conv, ya, yb, *, cw, carry=None):
    rows = proj.shape[0]
    tr = _row_tile(rows, cw, 6)

    def body(d_ref, g_ref, yc_ref, ya_ref, yb_ref, dg_ref, dyc_ref, dya_ref, dyb_ref):
        q = pl.program_id(1)
        d = d_ref[...].astype(F32)
        sg = _sigmoid(g_ref[...].astype(F32))

        @pl.when(q < 2)
        def _():
            dg_ref[...] = (d * yc_ref[...].astype(F32) * sg * (1.0 - sg)).astype(dg_ref.dtype)
            dyc_ref[...] = (d * sg).astype(dyc_ref.dtype)

        @pl.when(q >= 2)
        def _():
            sb = _sigmoid(yb_ref[...].astype(F32))
            yav = ya_ref[...].astype(F32)
            dg_ref[...] = (d * (yav * sb) * sg * (1.0 - sg)).astype(dg_ref.dtype)
            dys = d * sg
            dya_ref[...] = (dys * sb).astype(dya_ref.dtype)
            dyb_ref[...] = (dys * yav * sb * (1.0 - sb)).astype(dyb_ref.dtype)

    spec = lambda f: pl.BlockSpec((tr, cw), lambda i, q, f=f: (i, f(q)))
    conv_half = spec(lambda q: jnp.minimum(q, 1))
    ssm_half = spec(lambda q: jnp.maximum(q - 2, 0))
    o2 = jax.ShapeDtypeStruct((rows, 2 * cw), BF16)
    outs, carried = _call(
        body, grid=(rows // tr, 4),
        in_specs=[spec(lambda q: q % 2), spec(lambda q: 3 + q), conv_half, ssm_half, ssm_half],
        out_specs=[spec(lambda q: 3 + q), conv_half, ssm_half, ssm_half],
        out_shape=[jax.ShapeDtypeStruct((rows, 7 * cw), BF16), o2, o2, o2], scratch_shapes=[],
        semantics=("parallel", "arbitrary"), name="merge_bwd", args=[dmerged, proj, y_conv, ya, yb],
        carry=carry)
    return outs if carry is None else (outs, carried)


def _res_norm(x, mo, gate, g, scale, shift):
    rows, d = x.shape
    tr = _row_tile(rows, d, 4)

    def body(x_ref, mo_ref, gt_ref, g_ref, sc_ref, sh_ref, h_ref, z_ref):
        h = x_ref[...] + gt_ref[...] * mo_ref[...].astype(F32)
        h_ref[...] = h
        r = lax.rsqrt(jnp.mean(h * h, axis=-1, keepdims=True) + EPS)
        z_ref[...] = ((h * r * g_ref[...]) * (1.0 + sc_ref[...]) + sh_ref[...]).astype(z_ref.dtype)

    row = pl.BlockSpec((tr, d), lambda i: (i, 0))
    vec = pl.BlockSpec((1, d), lambda i: (0, 0))
    return pl.pallas_call(
        body, grid=(rows // tr,), in_specs=[row, row, vec, vec, vec, vec], out_specs=[row, row],
        out_shape=[jax.ShapeDtypeStruct((rows, d), F32), jax.ShapeDtypeStruct((rows, d), BF16)],
        compiler_params=_cparams(("parallel",)), name="res_norm")(x, mo, gate, g, scale, shift)


def _final_fwd_bwd(h1, ff, gate2, final_g, target):
    rows, d = h1.shape
    tr = _row_tile(rows, d, 5)

    def body(h_ref, ff_ref, gt_ref, fg_ref, t_ref, dh_ref, dff_ref, loss_ref, dfg_ref, dgt_ref):
        i = pl.program_id(0)

        @pl.when(i == 0)
        def _():
            loss_ref[...] = jnp.zeros_like(loss_ref)
            dfg_ref[...] = jnp.zeros_like(dfg_ref)
            dgt_ref[...] = jnp.zeros_like(dgt_ref)

        ffv = ff_ref[...].astype(F32)
        h2 = h_ref[...] + gt_ref[...] * ffv
        r = lax.rsqrt(jnp.mean(h2 * h2, axis=-1, keepdims=True) + EPS)
        y = h2 * r
        e = y * fg_ref[...] - t_ref[...]
        loss_ref[...] += 0.5 * jnp.sum(jnp.mean(e * e, axis=-1, keepdims=True))
        dout = e * (1.0 / d)
        dfg_ref[...] += jnp.sum(dout * y, axis=0, keepdims=True)
        dy = dout * fg_ref[...]
        dh2 = r * (dy - y * jnp.mean(dy * y, axis=-1, keepdims=True))
        dh_ref[...] = dh2
        dgt_ref[...] += jnp.sum(dh2 * ffv, axis=0, keepdims=True)
        dff_ref[...] = (dh2 * gt_ref[...]).astype(dff_ref.dtype)

    row = pl.BlockSpec((tr, d), lambda i: (i, 0))
    vec = pl.BlockSpec((1, d), lambda i: (0, 0))
    return pl.pallas_call(
        body, grid=(rows // tr,), in_specs=[row, row, vec, vec, row],
        out_specs=[row, row, pl.BlockSpec((1, LANES), lambda i: (0, 0)), vec, vec],
        out_shape=[jax.ShapeDtypeStruct((rows, d), F32), jax.ShapeDtypeStruct((rows, d), BF16),
                   jax.ShapeDtypeStruct((1, LANES), F32), jax.ShapeDtypeStruct((1, d), F32),
                   jax.ShapeDtypeStruct((1, d), F32)],
        compiler_params=_cparams(("arbitrary",)), name="final_fwd_bwd")(h1, ff, gate2, final_g, target)


def _norm_mod_bwd(dz, hin, dres, g, scale, gate, mo, *, name, carry=None):
    rows, d = hin.shape
    with_gate = gate is not None
    tr = _row_tile(rows, d, 6)

    def body(*refs):
        if with_gate:
            (dz_ref, h_ref, dr_ref, g_ref, sc_ref, gt_ref, mo_ref,
             dh_ref, dsh_ref, dsc_ref, dg_ref, dmo_ref, dgt_ref) = refs
        else:
            dz_ref, h_ref, dr_ref, g_ref, sc_ref, dh_ref, dsh_ref, dsc_ref, dg_ref = refs
        i = pl.program_id(0)

        @pl.when(i == 0)
        def _():
            dsh_ref[...] = jnp.zeros_like(dsh_ref)
            dsc_ref[...] = jnp.zeros_like(dsc_ref)
            dg_ref[...] = jnp.zeros_like(dg_ref)
            if with_gate:
                dgt_ref[...] = jnp.zeros_like(dgt_ref)

        dzv = dz_ref[...].astype(F32)
        h = h_ref[...]
        r = lax.rsqrt(jnp.mean(h * h, axis=-1, keepdims=True) + EPS)
        y = h * r
        dsh_ref[...] += jnp.sum(dzv, axis=0, keepdims=True)
        dsc_ref[...] += jnp.sum(dzv * (y * g_ref[...]), axis=0, keepdims=True)
        dn = dzv * (1.0 + sc_ref[...])
        dg_ref[...] += jnp.sum(dn * y, axis=0, keepdims=True)
        dy = dn * g_ref[...]
        dh = dr_ref[...] + r * (dy - y * jnp.mean(dy * y, axis=-1, keepdims=True))
        dh_ref[...] = dh
        if with_gate:
            dmo_ref[...] = (dh * gt_ref[...]).astype(dmo_ref.dtype)
            dgt_ref[...] += jnp.sum(dh * mo_ref[...].astype(F32), axis=0, keepdims=True)

    row = pl.BlockSpec((tr, d), lambda i: (i, 0))
    vec = pl.BlockSpec((1, d), lambda i: (0, 0))
    vshape = jax.ShapeDtypeStruct((1, d), F32)
    in_specs = [row, row, row, vec, vec]
    args = [dz, hin, dres, g, scale]
    out_specs = [row, vec, vec, vec]
    out_shape = [jax.ShapeDtypeStruct((rows, d), F32), vshape, vshape, vshape]
    if with_gate:
        in_specs += [vec, row]
        args += [gate, mo]
        out_specs += [row, vec]
        out_shape += [jax.ShapeDtypeStruct((rows, d), BF16), vshape]
    outs, carried = _call(
        body, grid=(rows // tr,), in_specs=in_specs, out_specs=out_specs, out_shape=out_shape,
        scratch_shapes=[], semantics=("arbitrary",), name=name, args=args, carry=carry)
    return outs if carry is None else (outs, carried)


def _s5_discretise(a_re, a_im, log_dt, b_re, b_im):
    dt = jnp.exp(log_dt)[:, None]
    er = jnp.exp(a_re * dt)
    lr = er * jnp.cos(a_im * dt)
    li = er * jnp.sin(a_im * dt)
    den = a_re * a_re + a_im * a_im
    cr = ((lr - 1.0) * a_re + li * a_im) / den
    ci = (li * a_re - (lr - 1.0) * a_im) / den
    bbr = cr[..., None] * b_re - ci[..., None] * b_im
    bbi = cr[..., None] * b_im + ci[..., None] * b_re
    return lr, li, bbr, bbi


def _block_diag(w):
    g, r, c = w.shape
    nb = g // GROUPS_PER_BLOCK
    eye = jnp.eye(GROUPS_PER_BLOCK, dtype=w.dtype)
    w5 = w.reshape(nb, GROUPS_PER_BLOCK, r, 1, c) * eye[None, :, None, :, None]
    return w5.reshape(nb, GROUPS_PER_BLOCK * r, GROUPS_PER_BLOCK * c)


def _block_diag_extract(m, r, c):
    nb = m.shape[0]
    m5 = m.reshape(nb, GROUPS_PER_BLOCK, r, GROUPS_PER_BLOCK, c)
    idx = jnp.arange(GROUPS_PER_BLOCK)
    d = m5[:, idx, :, idx, :]
    return jnp.moveaxis(d, 0, 1).reshape(nb * GROUPS_PER_BLOCK, r, c)


def _scan_multipliers(lr, li):
    power = jnp.arange(1, SUBLANES + 1, dtype=F32)[None, :, None]
    er = jnp.exp(power * lr)
    pr = er * jnp.cos(power * li)
    pi = er * jnp.sin(power * li)
    rows = jnp.arange(SUBLANES)[None, :, None]
    fr, fi, rr, ri = [], [], [], []
    for s in (1, 2, 4):
        mf = (rows >= s).astype(F32)
        mr = (rows <= SUBLANES - 1 - s).astype(F32)
        fr.append(mf * pr[:, s - 1:s, :])
        fi.append(mf * pi[:, s - 1:s, :])
        rr.append(mr * pr[:, s - 1:s, :])
        ri.append(mr * pi[:, s - 1:s, :])
    fr.append(pr)
    fi.append(pi)
    rr.append(pr[:, ::-1, :])
    ri.append(pi[:, ::-1, :])
    st = lambda xs: jnp.stack(xs, axis=1)
    return st(fr), st(fi), st(rr), st(ri)


def _scan_rows(sre, sim, mul_r, mul_i, n_groups, reverse):
    sgn = -1.0 if reverse else 1.0
    lanes = sre.shape[1]

    def step(k, carry):
        cr, ci = carry
        kk = (n_groups - 1 - k) if reverse else k
        r0 = pl.multiple_of(kk * SUBLANES, SUBLANES)
        xr = sre[pl.ds(r0, SUBLANES), :]
        xi = sim[pl.ds(r0, SUBLANES), :]
        for lvl, s in enumerate((1, 2, 4)):
            sh = (SUBLANES - s) if reverse else s
            nr = pltpu.roll(xr, sh, 0)
            ni = pltpu.roll(xi, sh, 0)
            mr = mul_r[lvl]
            mi = mul_i[lvl] * sgn
            xr, xi = xr + mr * nr - mi * ni, xi + mr * ni + mi * nr
        mr = mul_r[3]
        mi = mul_i[3] * sgn
        xr, xi = xr + mr * cr - mi * ci, xi + mr * ci + mi * cr
        sre[pl.ds(r0, SUBLANES), :] = xr
        sim[pl.ds(r0, SUBLANES), :] = xi
        edge = 0 if reverse else SUBLANES - 1
        ncr = jnp.broadcast_to(xr[edge:edge + 1, :], (SUBLANES, lanes))
        nci = jnp.broadcast_to(xi[edge:edge + 1, :], (SUBLANES, lanes))
        return ncr, nci

    zero = jnp.zeros((SUBLANES, lanes), F32)
    lax.fori_loop(0, n_groups, step, (zero, zero))


def _dot(a, b):
    return jnp.dot(a, b, preferred_element_type=F32)


def _dotf(a, b):
    return _dot(a.astype(BF16), b)


def _s5_operands(lr, li, bbr, bbi, c_re, c_im):
    g = lr.shape[0]
    nb = g // GROUPS_PER_BLOCK
    tb = lambda w: jnp.swapaxes(w, 1, 2)
    b_in = [_block_diag(tb(bbr)), _block_diag(tb(bbi))]
    c_out = [_block_diag(tb(c_re)), _block_diag(tb(c_im))]
    b_out = [_block_diag(bbr), _block_diag(bbi)]
    c_in = [_block_diag(c_re), _block_diag(c_im)]
    lam_r = lr.reshape(nb, 1, STATE_LANES)
    lam_i = li.reshape(nb, 1, STATE_LANES)
    mults = _scan_multipliers(lam_r, lam_i)
    cast = lambda ws: [w.astype(BF16) for w in ws]
    return cast(b_in), cast(c_out), cast(b_out), cast(c_in), mults


def _s5_fwd(proj, d_skip, b_in, c_out, mults, *, col0, carry=None):
    rows = proj.shape[0]
    nb = b_in[0].shape[0]
    tm = _pick(rows, 512, SUBLANES)
    n_tiles = rows // tm
    s_l = STATE_LANES

    def body(u_ref, dk_ref, br, bi, cr, ci, fr_ref, fi_ref, o_ref, sr_ref, si_ref, sre, sim):
        for t in range(n_tiles):
            rs = pl.ds(t * tm, tm)
            ub = u_ref[rs, :]
            sre[rs, :] = _dot(ub, br[...])
            sim[rs, :] = _dot(ub, bi[...])
        _scan_rows(sre, sim, fr_ref, fi_ref, rows // SUBLANES, False)
        for t in range(n_tiles):
            rs = pl.ds(t * tm, tm)
            srb = sre[rs, :].astype(BF16)
            sib = sim[rs, :].astype(BF16)
            sr_ref[rs, :] = srb
            si_ref[rs, :] = sib
            y0 = _dot(srb, cr[...]) - _dot(sib, ci[...])
            y1 = y0 + dk_ref[...] * u_ref[rs, :].astype(F32)
            o_ref[rs, :] = _gelu(y1).astype(o_ref.dtype)

    mat_in = pl.BlockSpec((None, LANES, s_l), lambda g: (g, 0, 0))
    mat_out = pl.BlockSpec((None, s_l, LANES), lambda g: (g, 0, 0))
    mul = pl.BlockSpec((None, 4, SUBLANES, s_l), lambda g: (g, 0, 0, 0))
    state = pl.BlockSpec((rows, s_l), lambda g: (0, g))
    outs, carried = _call(
        body, grid=(nb,),
        in_specs=[pl.BlockSpec((rows, LANES), lambda g: (0, col0 + g)), pl.BlockSpec((1, LANES), lambda g: (0, g))]
        + [mat_in] * 2 + [mat_out] * 2 + [mul] * 2,
        out_specs=[pl.BlockSpec((rows, LANES), lambda g: (0, g)), state, state],
        out_shape=[jax.ShapeDtypeStruct((rows, nb * LANES), BF16), jax.ShapeDtypeStruct((rows, nb * s_l), BF16),
                   jax.ShapeDtypeStruct((rows, nb * s_l), BF16)],
        scratch_shapes=[pltpu.VMEM((rows, s_l), F32), pltpu.VMEM((rows, s_l), F32)],
        semantics=("parallel",), name="s5_fwd", args=[proj, d_skip, *b_in, *c_out, mults[0], mults[1]], carry=carry)
    return outs if carry is None else (outs, carried)


def _s5_bwd(proj, dyg, d_skip, states, c_out, b_out, c_in, mults, dproj, *, col0, carry=None):
    rows = proj.shape[0]
    nb = c_out[0].shape[0]
    tm = _pick(rows, 512, SUBLANES)
    n_tiles = rows // tm
    s_l = STATE_LANES
    n_groups = rows // SUBLANES
    tn = (((0,), (0,)), ((), ()))

    def body(u_ref, dy_ref, dk_ref, sr_ref, si_ref, cr, ci, bor, boi, cir, cii, rr_ref, ri_ref, dproj_ref,
             du_ref, ddk_ref, dbr_ref, dbi_ref, dcr_ref, dci_ref, dlr_ref, dli_ref,
             gre, gim, dy1):
        ddk = jnp.zeros((1, LANES), F32)
        dcr = jnp.zeros((s_l, LANES), F32)
        dci = jnp.zeros((s_l, LANES), F32)
        for t in range(n_tiles):
            rs = pl.ds(t * tm, tm)
            srb = sr_ref[rs, :]
            sib = si_ref[rs, :]
            uf = u_ref[rs, :].astype(F32)
            y0 = _dot(srb, cr[...]) - _dot(sib, ci[...])
            y1 = y0 + dk_ref[...] * uf
            d1 = dy_ref[rs, :].astype(F32) * _gelu_grad(y1)
            dy1[rs, :] = d1
            ddk = ddk + jnp.sum(d1 * uf, axis=0, keepdims=True)
            d1b = d1.astype(BF16)
            dcr = dcr + lax.dot_general(srb, d1b, tn, preferred_element_type=F32)
            dci = dci - lax.dot_general(sib, d1b, tn, preferred_element_type=F32)
            gre[rs, :] = _dot(d1b, cir[...])
            gim[rs, :] = -_dot(d1b, cii[...])
        ddk_ref[...] = ddk
        dcr_ref[...] = dcr
        dci_ref[...] = dci

        last_row = lax.broadcasted_iota(jnp.int32, (SUBLANES, s_l), 0) == SUBLANES - 1

        def group(r0, s_r, s_i, carry):
            cr_, ci_, ar, ai = carry
            xr = gre[pl.ds(r0, SUBLANES), :]
            xi = gim[pl.ds(r0, SUBLANES), :]
            for lvl, s in enumerate((1, 2, 4)):
                nr = pltpu.roll(xr, SUBLANES - s, 0)
                ni = pltpu.roll(xi, SUBLANES - s, 0)
                mr = rr_ref[lvl]
                mi = ri_ref[lvl]
                xr, xi = xr + mr * nr + mi * ni, xi + mr * ni - mi * nr
            mr = rr_ref[3]
            mi = ri_ref[3]
            xr, xi = xr + mr * cr_ + mi * ci_, xi + mr * ci_ - mi * cr_
            gre[pl.ds(r0, SUBLANES), :] = xr
            gim[pl.ds(r0, SUBLANES), :] = xi
            nxt_r = jnp.where(last_row, cr_, pltpu.roll(xr, SUBLANES - 1, 0))
            nxt_i = jnp.where(last_row, ci_, pltpu.roll(xi, SUBLANES - 1, 0))
            ncr = jnp.broadcast_to(xr[0:1, :], (SUBLANES, s_l))
            nci = jnp.broadcast_to(xi[0:1, :], (SUBLANES, s_l))
            return ncr, nci, ar + nxt_r * s_r + nxt_i * s_i, ai + nxt_i * s_r - nxt_r * s_i

        def rev_step(k, carry):
            r0 = pl.multiple_of((n_groups // 2 - 1 - k) * 2 * SUBLANES, 2 * SUBLANES)
            s_r = sr_ref[pl.ds(r0, 2 * SUBLANES), :].astype(F32)
            s_i = si_ref[pl.ds(r0, 2 * SUBLANES), :].astype(F32)
            carry = group(r0 + SUBLANES, s_r[SUBLANES:], s_i[SUBLANES:], carry)
            return group(r0, s_r[:SUBLANES], s_i[:SUBLANES], carry)

        zero = jnp.zeros((SUBLANES, s_l), F32)
        _, _, ar, ai = lax.fori_loop(0, n_groups // 2, rev_step, (zero, zero, zero, zero))
        dlr_ref[...] = jnp.sum(ar, axis=0, keepdims=True)
        dli_ref[...] = jnp.sum(ai, axis=0, keepdims=True)

        dbr = jnp.zeros((LANES, s_l), F32)
        dbi = jnp.zeros((LANES, s_l), F32)
        for t in range(n_tiles):
            rs = pl.ds(t * tm, tm)
            gr = gre[rs, :]
            gi = gim[rs, :]
            grb = gr.astype(BF16)
            gib = gi.astype(BF16)
            du = _dot(grb, bor[...]) + _dot(gib, boi[...]) + dy1[rs, :] * dk_ref[...]
            du_ref[rs, :] = du.astype(du_ref.dtype)
            ub = u_ref[rs, :]
            dbr = dbr + lax.dot_general(ub, grb, tn, preferred_element_type=F32)
            dbi = dbi + lax.dot_general(ub, gib, tn, preferred_element_type=F32)
        dbr_ref[...] = dbr
        dbi_ref[...] = dbi

    mat_in = pl.BlockSpec((None, LANES, s_l), lambda g: (g, 0, 0))
    mat_out = pl.BlockSpec((None, s_l, LANES), lambda g: (g, 0, 0))
    mul = pl.BlockSpec((None, 4, SUBLANES, s_l), lambda g: (g, 0, 0, 0))
    lam = pl.BlockSpec((None, 1, s_l), lambda g: (g, 0, 0))
    col = pl.BlockSpec((rows, LANES), lambda g: (0, g))
    vec = pl.BlockSpec((1, LANES), lambda g: (0, g))
    state = pl.BlockSpec((rows, s_l), lambda g: (0, g))
    outs, carried = _call(
        body, grid=(nb,),
        in_specs=[pl.BlockSpec((rows, LANES), lambda g: (0, col0 + g)), col, vec]
        + [state] * 2 + [mat_out] * 2 + [mat_out] * 2 + [mat_in] * 2 + [mul] * 2
        + [pl.BlockSpec(memory_space=pl.ANY)],
        out_specs=[pl.BlockSpec((rows, LANES), lambda g: (0, col0 + g)), vec, mat_in, mat_in, mat_out, mat_out,
                   lam, lam],
        out_shape=[jax.ShapeDtypeStruct(dproj.shape, dproj.dtype), jax.ShapeDtypeStruct((1, nb * LANES), F32),
                   jax.ShapeDtypeStruct((nb, LANES, s_l), F32), jax.ShapeDtypeStruct((nb, LANES, s_l), F32),
                   jax.ShapeDtypeStruct((nb, s_l, LANES), F32), jax.ShapeDtypeStruct((nb, s_l, LANES), F32),
                   jax.ShapeDtypeStruct((nb, 1, s_l), F32), jax.ShapeDtypeStruct((nb, 1, s_l), F32)],
        scratch_shapes=[pltpu.VMEM((rows, s_l), F32)] * 2 + [pltpu.VMEM((rows, LANES), F32)],
        semantics=("parallel",), name="s5_bwd",
        args=[proj, dyg, d_skip, *states, *c_out, *b_out, *c_in, mults[2], mults[3], dproj], carry=carry,
        aliases={13: 0})
    return outs if carry is None else (outs, carried)


def _silu(v):
    return v * _sigmoid(v)


def _ada_fwd(c_all, w_shard, b_cols):
    d, n = w_shard.shape
    bn = _pick(n, 512)

    def body(c_ref, w_ref, b_ref, o_ref):
        ca = _silu(c_ref[...]).astype(BF16)
        o_ref[...] = _dot(ca, w_ref[...].astype(BF16)) + b_ref[...]

    return pl.pallas_call(
        body, grid=(n // bn,),
        in_specs=[pl.BlockSpec((N_DEV, d), lambda j: (0, 0)), pl.BlockSpec((d, bn), lambda j: (0, j)),
                  pl.BlockSpec((1, bn), lambda j: (0, j))],
        out_specs=pl.BlockSpec((N_DEV, bn), lambda j: (0, j)),
        out_shape=jax.ShapeDtypeStruct((N_DEV, n), F32),
        compiler_params=_cparams(("parallel",)), name="ada_fwd")(c_all, w_shard, b_cols)


def _ada_bwd(c_all, dmod_cols):
    d = c_all.shape[1]
    n = dmod_cols.shape[1]
    bn = _pick(n, 512)

    def body(c_ref, g_ref, o_ref):
        ca = _silu(c_ref[...]).astype(BF16)
        o_ref[...] = lax.dot_general(ca, g_ref[...].astype(BF16), (((0,), (0,)), ((), ())),
                                     preferred_element_type=F32)

    return pl.pallas_call(
        body, grid=(n // bn,),
        in_specs=[pl.BlockSpec((N_DEV, d), lambda j: (0, 0)), pl.BlockSpec((N_DEV, bn), lambda j: (0, j))],
        out_specs=pl.BlockSpec((d, bn), lambda j: (0, j)),
        out_shape=jax.ShapeDtypeStruct((d, n), F32),
        compiler_params=_cparams(("parallel",)), name="ada_bwd")(c_all, dmod_cols)


def _cast_bf16(w, *, name):
    rows, cols = w.shape
    tr = _row_tile(rows, cols, 2)

    def body(w_ref, o_ref):
        o_ref[...] = w_ref[...].astype(BF16)

    row = pl.BlockSpec((tr, cols), lambda i: (i, 0))
    return pl.pallas_call(
        body, grid=(rows // tr,), in_specs=[row], out_specs=row,
        out_shape=jax.ShapeDtypeStruct((rows, cols), BF16),
        compiler_params=_cparams(("parallel",)), name=name)(w)


def _adamw(w, g, m, v, *, name, carry=None):
    rows, cols = w.shape
    tr = _row_tile(rows, cols, 7)
    c1 = 1.0 / (1.0 - ADAM_B1 ** ADAM_STEP)
    c2 = 1.0 / (1.0 - ADAM_B2 ** ADAM_STEP)

    def body(w_ref, g_ref, m_ref, v_ref, d_ref, nm_ref, nv_ref):
        gv = g_ref[...]
        nm = ADAM_B1 * m_ref[...] + (1.0 - ADAM_B1) * gv
        nv = ADAM_B2 * v_ref[...] + (1.0 - ADAM_B2) * (gv * gv)
        nm_ref[...] = nm
        nv_ref[...] = nv
        d_ref[...] = -ADAM_LR * ((nm * c1) / (jnp.sqrt(nv * c2) + ADAM_EPS) + ADAM_WD * w_ref[...])

    row = pl.BlockSpec((tr, cols), lambda i: (i, 0))
    shp = jax.ShapeDtypeStruct((rows, cols), F32)
    outs, carried = _call(
        body, grid=(rows // tr,), in_specs=[row] * 4, out_specs=[row] * 3, out_shape=[shp] * 3,
        scratch_shapes=[], semantics=("parallel",), name=name, args=[w, g, m, v], carry=carry)
    return outs if carry is None else (outs, carried)


def _sum_leading(a, *, name, out_dtype=F32):
    n, rows, cols = a.shape
    tr = _row_tile(rows, cols, n + 1)

    def body(a_ref, o_ref):
        acc = a_ref[0].astype(F32)
        for i in range(1, n):
            acc = acc + a_ref[i].astype(F32)
        o_ref[...] = acc.astype(o_ref.dtype)

    return pl.pallas_call(
        body, grid=(rows // tr,), in_specs=[pl.BlockSpec((n, tr, cols), lambda i: (0, i, 0))],
        out_specs=pl.BlockSpec((tr, cols), lambda i: (i, 0)),
        out_shape=jax.ShapeDtypeStruct((rows, cols), out_dtype),
        compiler_params=_cparams(("parallel",)), name=name)(a)


def _add_half(dw, land, my_c, *, name):
    n, r, cols = dw.shape
    h = r // 2
    tr = _row_tile(h, cols, 3)
    hb = h // tr

    def body(c_ref, a_ref, b_ref, o_ref):
        o_ref[...] = (a_ref[...].astype(F32) + b_ref[...].astype(F32)).astype(o_ref.dtype)

    gs = pltpu.PrefetchScalarGridSpec(
        num_scalar_prefetch=1, grid=(n, hb),
        in_specs=[pl.BlockSpec((None, tr, cols), lambda s, i, c_ref: (s, c_ref[0] * hb + i, 0)),
                  pl.BlockSpec((None, tr, cols), lambda s, i, c_ref: (s, i, 0))],
        out_specs=pl.BlockSpec((None, tr, cols), lambda s, i, c_ref: (s, i, 0)))
    return pl.pallas_call(
        body, grid_spec=gs, out_shape=jax.ShapeDtypeStruct((n, h, cols), BF16),
        compiler_params=_cparams(("parallel", "parallel")), name=name)(my_c, dw, land)


def _mesh_pos():
    return lax.axis_index("x"), lax.axis_index("y"), lax.axis_index("c")


def _other_chips(x, y):
    return [(1 - x, y), (x, 1 - y), (1 - x, 1 - y)]


def _gather_small(blk, *, name):
    m_per, n = blk.shape

    def body(x_ref, out_ref, send_sems, recv_sems, local_sem):
        x, y, c = _mesh_pos()
        me, sibling = (x, y, c), (x, y, 1 - c)
        chips = _other_chips(x, y)

        def rows(px, py, pc):
            return out_ref.at[pl.ds((4 * px + 2 * py + pc) * m_per, m_per), :]

        def copy(k, block, to, src=None):
            return pltpu.make_async_remote_copy(
                src_ref=rows(*block) if src is None else src, dst_ref=rows(*block),
                send_sem=send_sems.at[k], recv_sem=recv_sems.at[k], device_id=to, device_id_type=MESH)

        mine = pltpu.make_async_copy(x_ref, rows(*me), local_sem)
        mine.start()
        first = [copy(0, me, sibling, src=x_ref)]
        first += [copy(1 + j, me, (*chip, c), src=x_ref) for j, chip in enumerate(chips)]
        for cp in first:
            cp.start()
        passed = [copy(4 + j, (*chip, c), sibling) for j, chip in enumerate(chips)]
        for j, chip in enumerate(chips):
            copy(1 + j, (*chip, c), me).wait_recv()
            passed[j].start()
        copy(0, sibling, me).wait_recv()
        for j, chip in enumerate(chips):
            copy(4 + j, (*chip, 1 - c), me).wait_recv()
        for cp in first + passed:
            cp.wait_send()
        mine.wait()

    return pl.pallas_call(
        body, out_shape=jax.ShapeDtypeStruct((N_DEV * m_per, n), blk.dtype),
        in_specs=[pl.BlockSpec(memory_space=pltpu.VMEM)], out_specs=pl.BlockSpec(memory_space=pltpu.VMEM),
        scratch_shapes=[pltpu.SemaphoreType.DMA((7,)), pltpu.SemaphoreType.DMA((7,)), pltpu.SemaphoreType.DMA],
        compiler_params=pltpu.CompilerParams(vmem_limit_bytes=VMEM_LIMIT_BYTES), name=name)(blk)


def _hbm_specs(n):
    return [pl.BlockSpec(memory_space=pl.ANY)] * n


def _gather_weights(shards):
    n = len(shards)

    def body(*refs):
        ins, outs = refs[:n], refs[n:2 * n]
        send_sems, recv_sems, local_sems = refs[2 * n:]
        x, y, c = _mesh_pos()
        me_chip = 2 * x + y
        sibling = (x, y, 1 - c)
        chips = _other_chips(x, y)

        def half(w, chip_idx, pc):
            h = shards[w].shape[0] // 2
            return outs[w].at[chip_idx, pl.ds(pc * h, h), :]

        def copy(w, k, chip_idx, pc, to, src=None):
            dst = half(w, chip_idx, pc)
            return pltpu.make_async_remote_copy(
                src_ref=dst if src is None else src, dst_ref=dst,
                send_sem=send_sems.at[6 * w + k], recv_sem=recv_sems.at[6 * w + k],
                device_id=to, device_id_type=MESH)

        local = [pltpu.make_async_copy(ins[w], outs[w].at[me_chip], local_sems.at[w]) for w in range(n)]
        for cp in local:
            cp.start()
        sends = []
        for w in range(n):
            h = shards[w].shape[0] // 2
            for j, chip in enumerate(chips):
                cp = copy(w, j, me_chip, c, (*chip, c), src=ins[w].at[pl.ds(c * h, h), :])
                cp.start()
                sends.append(cp)
        for w in range(n):
            for j, chip in enumerate(chips):
                chip_idx = 2 * chip[0] + chip[1]
                copy(w, j, chip_idx, c, (x, y, c)).wait_recv()
                cp = copy(w, 3 + j, chip_idx, c, sibling)
                cp.start()
                sends.append(cp)
        for w in range(n):
            for j, chip in enumerate(chips):
                copy(w, 3 + j, 2 * chip[0] + chip[1], 1 - c, (x, y, c)).wait_recv()
        for cp in sends:
            cp.wait_send()
        for cp in local:
            cp.wait()

    return pl.pallas_call(
        body, out_shape=[jax.ShapeDtypeStruct((N_CHIPS,) + s.shape, s.dtype) for s in shards],
        in_specs=_hbm_specs(n), out_specs=_hbm_specs(n),
        scratch_shapes=[pltpu.SemaphoreType.DMA((6 * n,)), pltpu.SemaphoreType.DMA((6 * n,)),
                        pltpu.SemaphoreType.DMA((n,))],
        name="gather_weights")(*shards)


def _swap_halves(dws, *, name):
    n = len(dws)

    def body(*refs):
        ins, outs = refs[:n], refs[n:2 * n]
        send_sems, recv_sems = refs[2 * n:]
        x, y, c = _mesh_pos()
        cps = []
        for w in range(n):
            h = dws[w].shape[1] // 2
            cp = pltpu.make_async_remote_copy(
                src_ref=ins[w].at[:, pl.ds((1 - c) * h, h), :], dst_ref=outs[w],
                send_sem=send_sems.at[w], recv_sem=recv_sems.at[w],
                device_id=(x, y, 1 - c), device_id_type=MESH)
            cp.start()
            cps.append(cp)
        for cp in cps:
            cp.wait()

    return pl.pallas_call(
        body, out_shape=[jax.ShapeDtypeStruct((s.shape[0], s.shape[1] // 2, s.shape[2]), s.dtype) for s in dws],
        in_specs=_hbm_specs(n), out_specs=_hbm_specs(n),
        scratch_shapes=[pltpu.SemaphoreType.DMA((n,)), pltpu.SemaphoreType.DMA((n,))],
        name=name)(*dws)


def _chip_exchange(parts):
    n = len(parts)

    def body(*refs):
        ins, outs = refs[:n], refs[n:2 * n]
        send_sems, recv_sems, local_sems = refs[2 * n:]
        x, y, c = _mesh_pos()
        me_chip = 2 * x + y
        chips = _other_chips(x, y)
        local = [pltpu.make_async_copy(ins[w].at[me_chip], outs[w].at[me_chip], local_sems.at[w]) for w in range(n)]
        for cp in local:
            cp.start()
        cps = []
        for w in range(n):
            for j, chip in enumerate(chips):
                cp = pltpu.make_async_remote_copy(
                    src_ref=ins[w].at[2 * chip[0] + chip[1]], dst_ref=outs[w].at[me_chip],
                    send_sem=send_sems.at[3 * w + j], recv_sem=recv_sems.at[3 * w + j],
                    device_id=(*chip, c), device_id_type=MESH)
                cp.start()
                cps.append((cp, w, j, chip))
        for cp, w, j, chip in cps:
            slot = outs[w].at[2 * chip[0] + chip[1]]
            pltpu.make_async_remote_copy(
                src_ref=slot, dst_ref=slot, send_sem=send_sems.at[3 * w + j], recv_sem=recv_sems.at[3 * w + j],
                device_id=(x, y, c), device_id_type=MESH).wait_recv()
        for cp, _, _, _ in cps:
            cp.wait_send()
        for cp in local:
            cp.wait()

    return pl.pallas_call(
        body, out_shape=[jax.ShapeDtypeStruct(s.shape, s.dtype) for s in parts],
        in_specs=_hbm_specs(n), out_specs=_hbm_specs(n),
        scratch_shapes=[pltpu.SemaphoreType.DMA((3 * n,)), pltpu.SemaphoreType.DMA((3 * n,)),
                        pltpu.SemaphoreType.DMA((n,))],
        name="chip_exchange")(*parts)


def _join_halves(halves):
    n = len(halves)

    def body(*refs):
        ins, outs = refs[:n], refs[n:2 * n]
        send_sems, recv_sems, local_sems = refs[2 * n:]
        x, y, c = _mesh_pos()
        cps, local = [], []
        for w in range(n):
            h = halves[w].shape[0]
            mine = outs[w].at[pl.ds(c * h, h), :]
            lc = pltpu.make_async_copy(ins[w], mine, local_sems.at[w])
            lc.start()
            local.append(lc)
            cp = pltpu.make_async_remote_copy(
                src_ref=ins[w], dst_ref=mine, send_sem=send_sems.at[w], recv_sem=recv_sems.at[w],
                device_id=(x, y, 1 - c), device_id_type=MESH)
            cp.start()
            cps.append(cp)
        for w in range(n):
            h = halves[w].shape[0]
            theirs = outs[w].at[pl.ds((1 - c) * h, h), :]
            pltpu.make_async_remote_copy(
                src_ref=theirs, dst_ref=theirs, send_sem=send_sems.at[w], recv_sem=recv_sems.at[w],
                device_id=(x, y, c), device_id_type=MESH).wait_recv()
        for cp in cps:
            cp.wait_send()
        for lc in local:
            lc.wait()

    return pl.pallas_call(
        body, out_shape=[jax.ShapeDtypeStruct((2 * s.shape[0], s.shape[1]), s.dtype) for s in halves],
        in_specs=_hbm_specs(n), out_specs=_hbm_specs(n),
        scratch_shapes=[pltpu.SemaphoreType.DMA((n,)), pltpu.SemaphoreType.DMA((n,)), pltpu.SemaphoreType.DMA((n,))],
        name="join_halves")(*halves)


def _cast_into_slot(w, chip, after, *, name):
    rows, cols = w.shape
    tr = _row_tile(rows, cols, 2)

    def body(chip_ref, w_ref, after_ref, o_ref):
        o_ref[...] = w_ref[...].astype(BF16)

    gs = pltpu.PrefetchScalarGridSpec(
        num_scalar_prefetch=1, grid=(rows // tr,),
        in_specs=[pl.BlockSpec((tr, cols), lambda i, chip_ref: (i, 0)), pl.BlockSpec(memory_space=pl.ANY)],
        out_specs=pl.BlockSpec((None, tr, cols), lambda i, chip_ref: (chip_ref[0], i, 0)))
    return pl.pallas_call(
        body, grid_spec=gs, out_shape=jax.ShapeDtypeStruct((N_CHIPS, rows, cols), BF16),
        compiler_params=_cparams(("parallel",)), name=name)(chip, w, after)


def _row_range(h, lo, hi, parts):
    step = h // parts
    assert step * parts == h and step % (2 * SUBLANES) == 0, (h, parts)
    return lo * step, (hi - lo) * step


def _gather_carry(items):
    n_copies = sum(len(js) for _, js, _, _, _ in items)
    sem = pltpu.SemaphoreType.DMA((2 * n_copies,))

    def copies(outs, sems):
        send_sems, recv_sems = sems
        x, y, c = _mesh_pos()
        me_chip = 2 * x + y
        chips = _other_chips(x, y)
        out_ici, in_ici, out_d2d, in_d2d = [], [], [], []
        k = 0
        for w, (buf, js, lo, hi, parts) in enumerate(items):
            h = buf.shape[1] // 2
            r0, nr = _row_range(h, lo, hi, parts)

            def copy(k, chip_idx, pc, to):
                ref = outs[w].at[chip_idx, pl.ds(pc * h + r0, nr), :]
                return pltpu.make_async_remote_copy(
                    src_ref=ref, dst_ref=ref, send_sem=send_sems.at[k], recv_sem=recv_sems.at[k],
                    device_id=to, device_id_type=MESH)

            for j in js:
                chip = chips[j]
                chip_idx = 2 * chip[0] + chip[1]
                out_ici.append(copy(k, me_chip, c, (*chip, c)))
                in_ici.append(copy(k, chip_idx, c, (x, y, c)))
                out_d2d.append(copy(k + 1, chip_idx, c, (x, y, 1 - c)))
                in_d2d.append(copy(k + 1, chip_idx, 1 - c, (x, y, c)))
                k += 2
        return out_ici, in_ici, out_d2d, in_d2d

    def start(ins, outs, sems):
        for cp in copies(outs, sems)[0]:
            cp.start()

    def finish(ins, outs, sems):
        out_ici, in_ici, out_d2d, in_d2d = copies(outs, sems)
        for arrived, onward in zip(in_ici, out_d2d):
            arrived.wait_recv()
            onward.start()
        for arrived in in_d2d:
            arrived.wait_recv()
        for cp in out_ici + out_d2d:
            cp.wait_send()

    bufs = [it[0] for it in items]
    shapes = [jax.ShapeDtypeStruct(b.shape, b.dtype) for b in bufs]
    return _Carry(bufs, shapes, {i: i for i in range(len(bufs))}, [sem, sem], start, finish)


def _exchange_carry(items):
    n = len(items)
    sem = pltpu.SemaphoreType.DMA((3 * n,))
    given = [w for w in range(n) if items[w][1] is not None]

    def copies(ins, outs, sems):
        send_sems, recv_sems = sems
        x, y, c = _mesh_pos()
        chips = _other_chips(x, y)
        sends, recvs = [], []
        for w, (part, _, lo, hi, parts) in enumerate(items):
            r0, nr = _row_range(part.shape[1], lo, hi, parts)
            for j, chip in enumerate(chips):
                land = outs[w].at[j, pl.ds(r0, nr), :]
                sends.append(pltpu.make_async_remote_copy(
                    src_ref=ins[w].at[2 * chip[0] + chip[1], pl.ds(r0, nr), :], dst_ref=land,
                    send_sem=send_sems.at[3 * w + j], recv_sem=recv_sems.at[3 * w + j],
                    device_id=(*chip, c), device_id_type=MESH))
                recvs.append(pltpu.make_async_remote_copy(
                    src_ref=land, dst_ref=land,
                    send_sem=send_sems.at[3 * w + j], recv_sem=recv_sems.at[3 * w + j],
                    device_id=(x, y, c), device_id_type=MESH))
        return sends, recvs

    def start(ins, outs, sems):
        for cp in copies(ins, outs, sems)[0]:
            cp.start()

    def finish(ins, outs, sems):
        sends, recvs = copies(ins, outs, sems)
        for cp in recvs:
            cp.wait_recv()
        for cp in sends:
            cp.wait_send()

    inputs = [it[0] for it in items] + [items[w][1] for w in given]
    shapes = [jax.ShapeDtypeStruct((3,) + it[0].shape[1:], it[0].dtype) for it in items]
    aliases = {n + i: w for i, w in enumerate(given)}
    return _Carry(inputs, shapes, aliases, [sem, sem], start, finish)


def _sum_into_half(part, landed, chip, my_c, *, name):
    _, h, cols = part.shape
    tr = _row_tile(h, cols, 5)
    hb = h // tr

    def body(chip_ref, c_ref, p_ref, l_ref, o_ref):
        acc = p_ref[...].astype(F32)
        for j in range(3):
            acc = acc + l_ref[j].astype(F32)
        o_ref[...] = acc

    gs = pltpu.PrefetchScalarGridSpec(
        num_scalar_prefetch=2, grid=(hb,),
        in_specs=[pl.BlockSpec((None, tr, cols), lambda i, chip_ref, c_ref: (chip_ref[0], i, 0)),
                  pl.BlockSpec((3, tr, cols), lambda i, chip_ref, c_ref: (0, i, 0))],
        out_specs=pl.BlockSpec((tr, cols), lambda i, chip_ref, c_ref: (c_ref[0] * hb + i, 0)))
    return pl.pallas_call(
        body, grid_spec=gs, out_shape=jax.ShapeDtypeStruct((2 * h, cols), F32),
        compiler_params=_cparams(("parallel",)), name=name)(chip, my_c, part, landed)


def _join_carry(fulls):
    n = len(fulls)
    sem = pltpu.SemaphoreType.DMA((n,))

    def copies(outs, sems):
        send_sems, recv_sems = sems
        x, y, c = _mesh_pos()
        sends, recvs = [], []
        for w in range(n):
            h = fulls[w].shape[0] // 2
            mine = outs[w].at[pl.ds(c * h, h), :]
            theirs = outs[w].at[pl.ds((1 - c) * h, h), :]
            sends.append(pltpu.make_async_remote_copy(
                src_ref=mine, dst_ref=mine, send_sem=send_sems.at[w], recv_sem=recv_sems.at[w],
                device_id=(x, y, 1 - c), device_id_type=MESH))
            recvs.append(pltpu.make_async_remote_copy(
                src_ref=theirs, dst_ref=theirs, send_sem=send_sems.at[w], recv_sem=recv_sems.at[w],
                device_id=(x, y, c), device_id_type=MESH))
        return sends, recvs

    def start(ins, outs, sems):
        for cp in copies(outs, sems)[0]:
            cp.start()

    def finish(ins, outs, sems):
        sends, recvs = copies(outs, sems)
        for cp in recvs:
            cp.wait_recv()
        for cp in sends:
            cp.wait_send()

    shapes = [jax.ShapeDtypeStruct(f.shape, f.dtype) for f in fulls]
    return _Carry(fulls, shapes, {i: i for i in range(n)}, [sem, sem], start, finish)


class _NoComm:
    def __init__(self, big):
        self.big = big
        self.grads = {}

    def weight(self, name):
        return self.big[name]

    def mm_in(self, u, afters):
        return _mm(u, self.big["w_in"], mode="nn", out_dtype=BF16, name="mm_in")

    def mm_d_in(self, dproj):
        return _mm(dproj, self.big["w_in"], mode="nt", out_dtype=F32, name="mm_d_in")

    def carry(self, site, args=()):
        return None

    def done(self, site, carried, out=None):
        return out

    def grad(self, name, dw):
        self.grads[name] = dw

    def early_grads(self, early):
        self.early = early


def _gather_rows_carry(blk):
    m_per = blk.shape[0]
    sem = pltpu.SemaphoreType.DMA((7,))

    def copies(ins, outs, sems):
        send_sems, recv_sems, local_sem = sems
        x, y, c = _mesh_pos()
        me, sibling = (x, y, c), (x, y, 1 - c)
        chips = _other_chips(x, y)

        def rows(px, py, pc):
            return outs[0].at[pl.ds((4 * px + 2 * py + pc) * m_per, m_per), :]

        def copy(k, block, to, src=None):
            return pltpu.make_async_remote_copy(
                src_ref=rows(*block) if src is None else src, dst_ref=rows(*block),
                send_sem=send_sems.at[k], recv_sem=recv_sems.at[k], device_id=to, device_id_type=MESH)

        mine = pltpu.make_async_copy(ins[0], rows(*me), local_sem.at[0])
        first = [copy(0, me, sibling, src=ins[0])]
        first += [copy(1 + j, me, (*chip, c), src=ins[0]) for j, chip in enumerate(chips)]
        passed = [copy(4 + j, (*chip, c), sibling) for j, chip in enumerate(chips)]
        landed = [copy(1 + j, (*chip, c), me) for j, chip in enumerate(chips)]
        from_sibling = [copy(0, sibling, me)] + [copy(4 + j, (*chip, 1 - c), me) for j, chip in enumerate(chips)]
        return mine, first, passed, landed, from_sibling

    def start(ins, outs, sems):
        mine, first, _, _, _ = copies(ins, outs, sems)
        mine.start()
        for cp in first:
            cp.start()

    def finish(ins, outs, sems):
        mine, first, passed, landed, from_sibling = copies(ins, outs, sems)
        for arrived, onward in zip(landed, passed):
            arrived.wait_recv()
            onward.start()
        for arrived in from_sibling:
            arrived.wait_recv()
        for cp in first + passed:
            cp.wait_send()
        mine.wait()

    shape = jax.ShapeDtypeStruct((N_DEV * m_per, blk.shape[1]), blk.dtype)
    return _Carry([blk], [shape], {}, [sem, sem, pltpu.SemaphoreType.DMA((1,))], start, finish)


def _gather_fresh_carry(own, js):
    n = len(js)
    h = own.shape[0] // 2
    sem = pltpu.SemaphoreType.DMA((2 * n,))

    def copies(ins, outs, sems):
        send_sems, recv_sems = sems
        x, y, c = _mesh_pos()
        chips = _other_chips(x, y)
        out_ici, in_ici, out_d2d, in_d2d = [], [], [], []

        def copy(k, src, dst, to):
            return pltpu.make_async_remote_copy(
                src_ref=src, dst_ref=dst, send_sem=send_sems.at[k], recv_sem=recv_sems.at[k],
                device_id=to, device_id_type=MESH)

        for jj, j in enumerate(js):
            mine = ins[0].at[pl.ds(c * h, h), :]
            land = outs[0].at[jj, pl.ds(c * h, h), :]
            other = outs[0].at[jj, pl.ds((1 - c) * h, h), :]
            out_ici.append(copy(2 * jj, mine, land, (*chips[j], c)))
            in_ici.append(copy(2 * jj, land, land, (x, y, c)))
            out_d2d.append(copy(2 * jj + 1, land, land, (x, y, 1 - c)))
            in_d2d.append(copy(2 * jj + 1, other, other, (x, y, c)))
        return out_ici, in_ici, out_d2d, in_d2d

    def start(ins, outs, sems):
        for cp in copies(ins, outs, sems)[0]:
            cp.start()

    def finish(ins, outs, sems):
        out_ici, in_ici, out_d2d, in_d2d = copies(ins, outs, sems)
        for arrived, onward in zip(in_ici, out_d2d):
            arrived.wait_recv()
            onward.start()
        for arrived in in_d2d:
            arrived.wait_recv()
        for cp in out_ici + out_d2d:
            cp.wait_send()

    return _Carry([own], [jax.ShapeDtypeStruct((n,) + own.shape, own.dtype)], {}, [sem, sem], start, finish)


def _w_in_copies(own_ref, land_ref, send_sems, recv_sems):
    x, y, c = _mesh_pos()
    h = own_ref.shape[0] // 2
    return [pltpu.make_async_remote_copy(
        src_ref=own_ref.at[pl.ds(c * h, h), :], dst_ref=land_ref.at[j, pl.ds(c * h, h), :],
        send_sem=send_sems[j], recv_sem=recv_sems[j], device_id=(*chip, c), device_id_type=MESH)
        for j, chip in enumerate(_other_chips(x, y))]


def _w_in_send(own, after):
    hbm = pl.BlockSpec(memory_space=pltpu.HBM)
    sem = pl.BlockSpec(memory_space=pltpu.SEMAPHORE)
    land_shape = (3,) + own.shape

    def body(own_ref, land_ref, after_ref, s0, s1, s2, r0, r1, r2, own_thru, land_thru, token):
        for cp in _w_in_copies(own_ref, land_ref, (s0, s1, s2), (r0, r1, r2)):
            cp.start()
        token[...] = jnp.zeros_like(token)

    outs = pl.pallas_call(
        body, name="w_in_send",
        out_shape=(pltpu.SemaphoreType.DMA(()),) * 6 + (
            pltpu.HBM(own.shape, own.dtype), pltpu.HBM(land_shape, own.dtype), jax.ShapeDtypeStruct((8, LANES), F32)),
        in_specs=(hbm, hbm, pl.BlockSpec(memory_space=pl.ANY)),
        out_specs=(sem,) * 6 + (hbm, hbm, pl.BlockSpec(memory_space=pltpu.VMEM)),
        input_output_aliases={0: 6, 1: 7},
        compiler_params=pltpu.CompilerParams(has_side_effects=pltpu.SideEffectType.DATAFLOW_SIDE_EFFECTING),
    )(pltpu.with_memory_space_constraint(own, pltpu.HBM),
      pltpu.with_memory_space_constraint(lax.empty(land_shape, own.dtype), pltpu.HBM), after)
    return outs[:6], outs[6], outs[7], outs[8]


def _w_in_wait(sems, own, land, afters):
    hbm = pl.BlockSpec(memory_space=pltpu.HBM)
    sem = pl.BlockSpec(memory_space=pltpu.SEMAPHORE)
    n_after = len(afters)

    def body(own_ref, land_ref, s0, s1, s2, r0, r1, r2, *rest):
        for cp in _w_in_copies(own_ref, land_ref, (s0, s1, s2), (r0, r1, r2)):
            cp.wait_send()
            cp.wait_recv()

    return pl.pallas_call(
        body, name="w_in_wait", out_shape=(pltpu.HBM(own.shape, own.dtype), pltpu.HBM(land.shape, land.dtype)),
        in_specs=(hbm, hbm) + (sem,) * 6 + (pl.BlockSpec(memory_space=pl.ANY),) * n_after, out_specs=(hbm, hbm),
        input_output_aliases={0: 0, 1: 1},
        compiler_params=pltpu.CompilerParams(has_side_effects=pltpu.SideEffectType.DATAFLOW_SIDE_EFFECTING),
    )(own, land, *sems, *afters)


def _exchange_copies(part_refs, land_refs, send_sems, recv_sems):
    x, y, c = _mesh_pos()
    cps = []
    for w, (part, land) in enumerate(zip(part_refs, land_refs)):
        for j, chip in enumerate(_other_chips(x, y)):
            cps.append(pltpu.make_async_remote_copy(
                src_ref=part.at[2 * chip[0] + chip[1]], dst_ref=land.at[j],
                send_sem=send_sems[3 * w + j], recv_sem=recv_sems[3 * w + j],
                device_id=(*chip, c), device_id_type=MESH))
    return cps


def _exchange_send(parts, through, *, name):
    n = len(parts)
    hbm = pl.BlockSpec(memory_space=pltpu.HBM)
    sem = pl.BlockSpec(memory_space=pltpu.SEMAPHORE)
    any_spec = pl.BlockSpec(memory_space=pl.ANY)
    land_shapes = [(3,) + p.shape[1:] for p in parts]

    def body(*refs):
        part_refs, land_refs = refs[:n], refs[n:2 * n]
        sems = refs[2 * n + 1:8 * n + 1]
        for cp in _exchange_copies(part_refs, land_refs, sems[:3 * n], sems[3 * n:]):
            cp.start()

    outs = pl.pallas_call(
        body, name=name,
        out_shape=(pltpu.SemaphoreType.DMA(()),) * (6 * n)
        + tuple(pltpu.HBM(p.shape, p.dtype) for p in parts)
        + tuple(pltpu.HBM(s, p.dtype) for s, p in zip(land_shapes, parts))
        + (jax.ShapeDtypeStruct(through.shape, through.dtype),),
        in_specs=(hbm,) * (2 * n) + (any_spec,), out_specs=(sem,) * (6 * n) + (hbm,) * (2 * n) + (any_spec,),
        input_output_aliases={i: 6 * n + i for i in range(2 * n + 1)},
        compiler_params=pltpu.CompilerParams(has_side_effects=pltpu.SideEffectType.DATAFLOW_SIDE_EFFECTING),
    )(*[pltpu.with_memory_space_constraint(p, pltpu.HBM) for p in parts],
      *[pltpu.with_memory_space_constraint(lax.empty(s, p.dtype), pltpu.HBM) for s, p in zip(land_shapes, parts)],
      through)
    return outs[:6 * n], outs[6 * n:7 * n], outs[7 * n:8 * n], outs[8 * n]


def _exchange_wait(sems, parts, lands, afters, *, name):
    n = len(parts)
    hbm = pl.BlockSpec(memory_space=pltpu.HBM)
    sem = pl.BlockSpec(memory_space=pltpu.SEMAPHORE)

    def body(*refs):
        part_refs, land_refs = refs[:n], refs[n:2 * n]
        sem_refs = refs[2 * n:8 * n]
        for cp in _exchange_copies(part_refs, land_refs, sem_refs[:3 * n], sem_refs[3 * n:]):
            cp.wait_send()
            cp.wait_recv()

    outs = pl.pallas_call(
        body, name=name,
        out_shape=tuple(pltpu.HBM(p.shape, p.dtype) for p in parts) + tuple(pltpu.HBM(l.shape, l.dtype) for l in lands),
        in_specs=(hbm,) * (2 * n) + (sem,) * (6 * n) + (pl.BlockSpec(memory_space=pl.ANY),) * len(afters),
        out_specs=(hbm,) * (2 * n), input_output_aliases={i: i for i in range(2 * n)},
        compiler_params=pltpu.CompilerParams(has_side_effects=pltpu.SideEffectType.DATAFLOW_SIDE_EFFECTING),
    )(*parts, *lands, *sems, *afters)
    return outs[:n], outs[n:]


def _gather_ici_copies(buf_refs, send_sems, recv_sems):
    x, y, c = _mesh_pos()
    me_chip = 2 * x + y
    cps = []
    for w, buf in enumerate(buf_refs):
        h = buf.shape[1] // 2
        ref = buf.at[me_chip, pl.ds(c * h, h), :]
        for j, chip in enumerate(_other_chips(x, y)):
            cps.append(pltpu.make_async_remote_copy(
                src_ref=ref, dst_ref=ref, send_sem=send_sems[3 * w + j], recv_sem=recv_sems[3 * w + j],
                device_id=(*chip, c), device_id_type=MESH))
    return cps


def _gather_send(bufs, *, name):
    n = len(bufs)
    hbm = pl.BlockSpec(memory_space=pltpu.HBM)
    sem = pl.BlockSpec(memory_space=pltpu.SEMAPHORE)

    def body(*refs):
        sems = refs[n:7 * n]
        for cp in _gather_ici_copies(refs[:n], sems[:3 * n], sems[3 * n:]):
            cp.start()

    outs = pl.pallas_call(
        body, name=name,
        out_shape=(pltpu.SemaphoreType.DMA(()),) * (6 * n) + tuple(pltpu.HBM(b.shape, b.dtype) for b in bufs),
        in_specs=(hbm,) * n, out_specs=(sem,) * (6 * n) + (hbm,) * n,
        input_output_aliases={i: 6 * n + i for i in range(n)},
        compiler_params=pltpu.CompilerParams(has_side_effects=pltpu.SideEffectType.DATAFLOW_SIDE_EFFECTING),
    )(*[pltpu.with_memory_space_constraint(b, pltpu.HBM) for b in bufs])
    send_sems, recv_sems = outs[:3 * n], outs[3 * n:6 * n]
    per_buf = [tuple(send_sems[3 * w:3 * w + 3]) + tuple(recv_sems[3 * w:3 * w + 3]) for w in range(n)]
    return per_buf, list(outs[6 * n:])


def _gather_wait(sems, bufs, afters, *, name):
    n = len(bufs)
    hbm = pl.BlockSpec(memory_space=pltpu.HBM)
    sem = pl.BlockSpec(memory_space=pltpu.SEMAPHORE)
    flat = [s for six in sems for s in six[:3]] + [s for six in sems for s in six[3:]]

    def body(*refs):
        sem_refs = refs[n:7 * n]
        for cp in _gather_ici_copies(refs[:n], sem_refs[:3 * n], sem_refs[3 * n:]):
            cp.wait_send()
            cp.wait_recv()

    outs = pl.pallas_call(
        body, name=name, out_shape=tuple(pltpu.HBM(b.shape, b.dtype) for b in bufs),
        in_specs=(hbm,) * n + (sem,) * (6 * n) + (pl.BlockSpec(memory_space=pl.ANY),) * len(afters),
        out_specs=(hbm,) * n, input_output_aliases={i: i for i in range(n)},
        compiler_params=pltpu.CompilerParams(has_side_effects=pltpu.SideEffectType.DATAFLOW_SIDE_EFFECTING),
    )(*bufs, *flat, *afters)
    return list(outs)


def _forward_abs_carry(bufs):
    n = len(bufs)
    sem = pltpu.SemaphoreType.DMA((3 * n,))

    def copies(outs, sems):
        send_sems, recv_sems = sems
        x, y, c = _mesh_pos()
        sends, recvs = [], []
        for w in range(n):
            h = bufs[w].shape[1] // 2
            for j, chip in enumerate(_other_chips(x, y)):
                slot = 2 * chip[0] + chip[1]
                mine = outs[w].at[slot, pl.ds(c * h, h), :]
                other = outs[w].at[slot, pl.ds((1 - c) * h, h), :]
                sends.append(pltpu.make_async_remote_copy(
                    src_ref=mine, dst_ref=mine, send_sem=send_sems.at[3 * w + j], recv_sem=recv_sems.at[3 * w + j],
                    device_id=(x, y, 1 - c), device_id_type=MESH))
                recvs.append(pltpu.make_async_remote_copy(
                    src_ref=other, dst_ref=other, send_sem=send_sems.at[3 * w + j], recv_sem=recv_sems.at[3 * w + j],
                    device_id=(x, y, c), device_id_type=MESH))
        return sends, recvs

    def start(ins, outs, sems):
        for cp in copies(outs, sems)[0]:
            cp.start()

    def finish(ins, outs, sems):
        sends, recvs = copies(outs, sems)
        for cp in recvs:
            cp.wait_recv()
        for cp in sends:
            cp.wait_send()

    shapes = [jax.ShapeDtypeStruct(b.shape, b.dtype) for b in bufs]
    return _Carry(bufs, shapes, {i: i for i in range(n)}, [sem, sem], start, finish)


def _forward_carry(land):
    n = land.shape[0]
    h = land.shape[1] // 2
    sem = pltpu.SemaphoreType.DMA((n,))

    def copies(outs, sems):
        send_sems, recv_sems = sems
        x, y, c = _mesh_pos()
        sends, recvs = [], []
        for j in range(n):
            mine = outs[0].at[j, pl.ds(c * h, h), :]
            other = outs[0].at[j, pl.ds((1 - c) * h, h), :]
            sends.append(pltpu.make_async_remote_copy(
                src_ref=mine, dst_ref=mine, send_sem=send_sems.at[j], recv_sem=recv_sems.at[j],
                device_id=(x, y, 1 - c), device_id_type=MESH))
            recvs.append(pltpu.make_async_remote_copy(
                src_ref=other, dst_ref=other, send_sem=send_sems.at[j], recv_sem=recv_sems.at[j],
                device_id=(x, y, c), device_id_type=MESH))
        return sends, recvs

    def start(ins, outs, sems):
        for cp in copies(outs, sems)[0]:
            cp.start()

    def finish(ins, outs, sems):
        sends, recvs = copies(outs, sems)
        for cp in recvs:
            cp.wait_recv()
        for cp in sends:
            cp.wait_send()

    return _Carry([land], [jax.ShapeDtypeStruct(land.shape, land.dtype)], {0: 0}, [sem, sem], start, finish)


def _swap_carry(dws):
    n = len(dws)
    sem = pltpu.SemaphoreType.DMA((n,))

    def copies(ins, outs, sems):
        send_sems, recv_sems = sems
        x, y, c = _mesh_pos()
        cps = []
        for w in range(n):
            h = dws[w].shape[1] // 2
            cps.append(pltpu.make_async_remote_copy(
                src_ref=ins[w].at[:, pl.ds((1 - c) * h, h), :], dst_ref=outs[w],
                send_sem=send_sems.at[w], recv_sem=recv_sems.at[w],
                device_id=(x, y, 1 - c), device_id_type=MESH))
        return cps

    def start(ins, outs, sems):
        for cp in copies(ins, outs, sems):
            cp.start()

    def finish(ins, outs, sems):
        for cp in copies(ins, outs, sems):
            cp.wait()

    shapes = [jax.ShapeDtypeStruct((s.shape[0], s.shape[1] // 2, s.shape[2]), s.dtype) for s in dws]
    return _Carry(dws, shapes, {}, [sem, sem], start, finish)


def _merge_carries(carries):
    if len(carries) == 1:
        return carries[0]
    inputs, out_shapes, sem_shapes, aliases, spans = [], [], [], {}, []
    for cy in carries:
        i0, o0, s0 = len(inputs), len(out_shapes), len(sem_shapes)
        aliases.update({i0 + i: o0 + o for i, o in cy.aliases.items()})
        inputs += cy.inputs
        out_shapes += cy.out_shapes
        sem_shapes += cy.sem_shapes
        spans.append((slice(i0, len(inputs)), slice(o0, len(out_shapes)), slice(s0, len(sem_shapes))))

    def start(ins, outs, sems):
        for cy, (si, so, ss) in zip(carries, spans):
            cy.start(ins[si], outs[so], sems[ss])

    def finish(ins, outs, sems):
        for cy, (si, so, ss) in zip(carries, spans):
            cy.finish(ins[si], outs[so], sems[ss])

    return _Carry(inputs, out_shapes, aliases, sem_shapes, start, finish)


ALL_CHIPS = (0, 1, 2)


class _MeshComm:
    GATHER_AT = {}
    FORWARD_AT = {
        "conv_fwd": ["w_conv_out", "w_glu_a", "w_glu_b"],
        "s5_fwd": ["w_out"],
        "mm_out": ["w_ff1"],
        "mm_ff1": ["w_ff2"],
    }
    SWAP_AT = {
        "mm_d_ff2": ["w_ff2"],
        "mm_d_ff1": ["w_ff1"],
        "conv_bwd": ["w_out", "w_glu_a", "w_glu_b", "w_conv_out"],
    }
    EXCHANGE_AT = {}
    EARLY_AT = "mm_dw_in"

    def __init__(self, shards, pos, chip, my_c):
        self.pos = pos
        self.chip = chip
        self.my_c = my_c
        self.shards = shards
        self.w_in_own = _cast_bf16(shards["w_in"], name="cast_w_in")
        self.raw = {}
        self.flights = []
        self.parts = {}
        self.landing = {}
        self.halves = {}
        self.pending = {}
        self.last_site = {}
        for site, items in self.EXCHANGE_AT.items():
            for it in items:
                self.last_site[it[0]] = site

    def weight(self, name):
        g = self.bufs[name]
        return g.reshape(g.shape[0] * g.shape[1], g.shape[2]) if name in ROW_SHARDED else g

    def _slot_ids(self):
        x, y, _ = self.pos
        ids = [2 * x + y] + [2 * cx + cy for cx, cy in _other_chips(x, y)]
        return jnp.stack(ids).astype(jnp.int32)

    def start_w_in(self, after):
        *self.w_in_flight, token = _w_in_send(self.w_in_own, after)
        order = [n for names in self.FORWARD_AT.values() for n in names]
        casts = [_cast_into_slot(self.shards[n], self.chip, token, name="cast_" + n) for n in order]
        sems, bufs = _gather_send(casts, name="gather_send")
        self.bufs = dict(zip(order, bufs))
        self.gather_sems = dict(zip(order, sems))
        return token

    def mm_in(self, u, afters):
        ids = self._slot_ids()
        sems, own, land = self.w_in_flight
        proj = _mm_slots(u, own[None], ids[0:1], None, name="mm_in_own")
        own, land = _w_in_wait(sems, own, land, [proj] + list(self.bufs.values()) + list(afters))
        land, = _run_carry(_forward_carry(land), name="forward_w_in")
        proj = _mm_slots(u, land, ids[1:4], proj, name="mm_in_rest")
        self.w_in_rel = jnp.concatenate([own[None], land], axis=0)
        return proj

    def _add_and_send(self, names, landed, site, through):
        parts = [_add_half(self.raw.pop(n), l1, self.my_c, name="add_half_" + n) for n, l1 in zip(names, landed)]
        sems, parts, lands, through = _exchange_send(parts, through, name="exchange_send_" + site)
        self.flights.append((names, sems, parts, lands))
        return through

    def mm_d_in(self, dproj):
        landed = _run_carry(_swap_carry([self.raw["w_in"]]), name="swap_halves_w_in")
        dproj = self._add_and_send(["w_in"], landed, "w_in", dproj)
        return _mm(dproj, self.w_in_rel, mode="nt", out_dtype=F32, name="mm_d_in", a_slots=self._slot_ids())

    def early_grads(self, early):
        self.early = early

    def carry(self, site, args=()):
        jobs = []
        if site in self.FORWARD_AT:
            names = self.FORWARD_AT[site]
            landed = _gather_wait([self.gather_sems.pop(n) for n in names], [self.bufs[n] for n in names],
                                  [args[0]], name="gather_wait_" + site)
            jobs.append(("gather", [(n,) for n in names], _forward_abs_carry(landed)))
        if site == self.EARLY_AT:
            flat, self.early_offs = _pack(list(self.early.values()))
            jobs.append(("early", None, _gather_rows_carry(flat.reshape(-1, PACK_COLS))))
        if site in self.GATHER_AT:
            items = self.GATHER_AT[site]
            jobs.append(("gather", items, _gather_carry([(self.bufs[it[0]],) + tuple(it[1:]) for it in items])))
        if site in self.EXCHANGE_AT:
            items = self.EXCHANGE_AT[site]
            jobs.append(("exchange", items, _exchange_carry(
                [(self.parts[it[0]], self.landing.get(it[0])) + tuple(it[1:]) for it in items])))
        if site in self.SWAP_AT:
            names = self.SWAP_AT[site]
            jobs.append(("swap", names, _swap_carry([self.raw[n] for n in names])))
        if not jobs:
            return None
        self.pending[site] = jobs
        return _merge_carries([job[2] for job in jobs])

    def done(self, site, carried, out=None):
        pos = 0
        for kind, items, carry in self.pending.pop(site):
            outs = carried[pos:pos + len(carry.out_shapes)]
            pos += len(carry.out_shapes)
            if kind == "early":
                self.early_all = outs[0]
            elif kind == "gather":
                self.bufs.update(zip([it[0] for it in items], outs))
            elif kind == "swap":
                if isinstance(out, (list, tuple)):
                    out = [self._add_and_send(items, outs, site, out[0])] + list(out[1:])
                else:
                    out = self._add_and_send(items, outs, site, out)
            else:
                for it, landed in zip(items, outs):
                    n = it[0]
                    self.landing[n] = landed
                    if self.last_site[n] == site:
                        self.halves[n] = _sum_into_half(self.parts.pop(n), self.landing.pop(n), self.chip,
                                                        self.my_c, name="sum_chips_" + n)
        return out

    def grad(self, name, dw):
        if name in ROW_SHARDED:
            dw = dw.reshape(N_CHIPS, dw.shape[0] // N_CHIPS, dw.shape[1])
        self.raw[name] = dw

    def join(self, names, afters, *, name):
        for i, (group, sems, parts, lands) in enumerate(self.flights):
            parts, lands = _exchange_wait(sems, parts, lands, afters, name="exchange_wait_%d" % i)
            for n, part, land in zip(group, parts, lands):
                self.halves[n] = _sum_into_half(part, land, self.chip, self.my_c, name="sum_chips_" + n)
        self.flights = []
        return dict(zip(names, _run_carry(_join_carry([self.halves.pop(n) for n in names]), name=name)))


def _local_step(x, target, mod, small, comm):
    rows, d = x.shape
    cw = d // 2
    shift1, scale1, gate1, shift2, scale2, gate2 = mod
    _, _, bbr, bbi = small["s5_disc"]
    b_in, c_out, b_out, c_in, mults = _s5_operands(*small["s5_loglam"], bbr, bbi, small["c_re"], small["c_im"])
    wt = comm.weight

    def riding(site, fn, *args, **kwargs):
        carry = comm.carry(site, args)
        if carry is None:
            return fn(*args, **kwargs)
        out, carried = fn(*args, carry=carry, **kwargs)
        return comm.done(site, carried, out)

    u = _norm_mod(x, small["norm1_g"], scale1, shift1, name="norm1_fwd")
    proj = comm.mm_in(u, [*b_in, *c_out, *b_out, *c_in, *mults])
    sl, cv = riding("conv_fwd", _conv_fwd, proj, small["w_dw"], small["b_dw"], small["ln_g"], small["ln_b"], cw=cw)
    y_conv = _mm(sl, wt("w_conv_out"), mode="nn", out_dtype=BF16, name="mm_conv_out")
    yg, st_re, st_im = riding("s5_fwd", _s5_fwd, proj, small["d_skip"], b_in, c_out, mults, col0=2 * cw // LANES)
    ya = riding("mm_glu_a", _mm, yg, wt("w_glu_a"), mode="nn", out_dtype=BF16, name="mm_glu_a")
    yb = riding("mm_glu_b", _mm, yg, wt("w_glu_b"), mode="nn", out_dtype=BF16, name="mm_glu_b")
    merged = _merge_fwd(proj, y_conv, ya, yb, cw=cw)
    mo = riding("mm_out", _mm, merged, wt("w_out"), mode="nn", out_dtype=BF16, name="mm_out")
    h1, z = _res_norm(x, mo, gate1, small["norm2_g"], scale2, shift2)
    f1 = riding("mm_ff1", _mm, z, wt("w_ff1"), mode="nn", out_dtype=BF16, name="mm_ff1")
    ff = _mm(f1, wt("w_ff2"), mode="nn", out_dtype=BF16, name="mm_ff2", a_fn=_relu2_bf16)
    dh2, dff, loss, d_final_g, d_gate2 = _final_fwd_bwd(h1, ff, gate2, small["final_g"], target)

    comm.grad("w_ff2", _mm(f1, dff, mode="tn", out_dtype=BF16, name="mm_dw_ff2", a_fn=_relu2_bf16))
    df1 = riding("mm_d_ff2", _mm, dff, wt("w_ff2"), mode="nt", out_dtype=BF16, name="mm_d_ff2", extra=f1,
                 epi=lambda acc, f: acc * (2.0 * jnp.maximum(f.astype(F32), 0.0)))
    comm.grad("w_ff1", riding("mm_dw_ff1", _mm, z, df1, mode="tn", out_dtype=BF16, name="mm_dw_ff1",
                              out_gathered=True))
    dz = riding("mm_d_ff1", _mm, df1, wt("w_ff1"), mode="nt", out_dtype=F32, name="mm_d_ff1")
    dh1, d_shift2, d_scale2, d_norm2_g, dmo, d_gate1 = riding(
        "norm2_bwd", _norm_mod_bwd, dz, h1, dh2, small["norm2_g"], scale2, gate1, mo, name="norm2_bwd")
    comm.grad("w_out", riding("mm_dw_out", _mm, merged, dmo, mode="tn", out_dtype=BF16, name="mm_dw_out"))
    dmerged = riding("mm_d_out", _mm, dmo, wt("w_out"), mode="nt", out_dtype=BF16, name="mm_d_out")
    dproj, dy_conv, dya, dyb = riding("merge_bwd", _merge_bwd, dmerged, proj, y_conv, ya, yb, cw=cw)
    comm.grad("w_glu_a", _mm(yg, dya, mode="tn", out_dtype=BF16, name="mm_dw_glu_a", out_gathered=True))
    comm.grad("w_glu_b", _mm(yg, dyb, mode="tn", out_dtype=BF16, name="mm_dw_glu_b", out_gathered=True))
    dyg_a = _mm(dya, wt("w_glu_a"), mode="nt", out_dtype=F32, name="mm_d_glu_a")
    dyg = _mm(dyb, wt("w_glu_b"), mode="nt", out_dtype=F32, name="mm_d_glu_b", extra=dyg_a,
              epi=lambda acc, e: acc + e)
    comm.grad("w_conv_out", _mm(sl, dy_conv, mode="tn", out_dtype=BF16, name="mm_dw_conv_out", out_gathered=True))
    dsl = _mm(dy_conv, wt("w_conv_out"), mode="nt", out_dtype=F32, name="mm_d_conv_out")
    dcv, d_ln_g, d_ln_b = _ln_bwd(dsl, cv, small["ln_g"], small["ln_b"])
    dproj, d_w_dw, d_b_dw = riding("conv_bwd", _conv_bwd, dcv, proj, small["w_dw"], dproj, cw=cw)
    dproj, d_d_skip, dbr, dbi, dcr, dci, dlr, dli = riding(
        "s5_bwd", _s5_bwd, proj, dyg, small["d_skip"], (st_re, st_im), c_out, b_out, c_in, mults, dproj,
        col0=2 * cw // LANES)
    sw = lambda m: jnp.swapaxes(m, 1, 2)
    early = {
        "dmod_tail": jnp.concatenate([d_gate1, d_shift2, d_scale2, d_gate2], axis=1), "loss": loss[:, 0:1],
        "w_dw": d_w_dw, "b_dw": d_b_dw, "ln_g": d_ln_g, "ln_b": d_ln_b,
        "lam_re": dlr.reshape(-1, SSM_STATE), "lam_im": dli.reshape(-1, SSM_STATE),
        "bb_re": sw(_block_diag_extract(dbr, SSM_GROUP, SSM_STATE)),
        "bb_im": sw(_block_diag_extract(dbi, SSM_GROUP, SSM_STATE)),
        "c_re": sw(_block_diag_extract(dcr, SSM_STATE, SSM_GROUP)),
        "c_im": sw(_block_diag_extract(dci, SSM_STATE, SSM_GROUP)),
        "d_skip": d_d_skip, "norm2_g": d_norm2_g, "final_g": d_final_g,
    }
    comm.early_grads(early)
    comm.grad("w_in", riding("mm_dw_in", _mm, u, dproj, mode="tn", out_dtype=BF16, name="mm_dw_in",
                             out_gathered=True))
    du = comm.mm_d_in(dproj)
    grad_x, d_shift1, d_scale1, d_norm1_g = riding(
        "norm1_bwd", _norm_mod_bwd, du, x, dh1, small["norm1_g"], scale1, None, None, name="norm1_bwd")
    late ={"dmod_head": jnp.concatenate([d_shift1, d_scale1], axis=1), "norm1_g": d_norm1_g}
    return grad_x, early, late


WEIGHT_NAMES = ["w_ada", "b_ada", "norm1_g", "w_in", "w_dw", "b_dw", "ln_g", "ln_b", "w_conv_out", "a_re", "a_im",
                "log_dt", "b_re", "b_im", "c_re", "c_im", "d_skip", "w_glu_a", "w_glu_b", "w_out", "norm2_g",
                "w_ff1", "w_ff2", "final_g"]
BIG_NAMES = ["w_in", "w_conv_out", "w_glu_a", "w_glu_b", "w_out", "w_ff1", "w_ff2"]
ROW_SHARDED = ("w_out", "w_ff2")
PACK_COLS = 1024
PACK_TILE = SUBLANES * PACK_COLS


def _pack(arrays):
    flats = [a.reshape(-1) for a in arrays]
    offs = []
    total = 0
    for f in flats:
        offs.append(total)
        total += f.shape[0]
    pad = (-total) % PACK_TILE
    if pad:
        flats.append(jnp.zeros((pad,), F32))
    return jnp.concatenate(flats), offs


def _unpack(flat, offs, like):
    return [flat[o:o + a.size].reshape(a.shape) for o, a in zip(offs, like)]


def _gather_w_dw(w_shard):
    k, n = w_shard.shape
    padded = jnp.pad(w_shard, ((0, HALO - k), (0, 0)))
    allw = _gather_small(padded, name="gather_w_dw").reshape(N_CHIPS, 2, HALO, n)[:, 0, :k]
    return jnp.moveaxis(allw, 0, 1).reshape(k, N_CHIPS * n)


def kernel(x, c, w_ada, b_ada, norm1_g, w_in, w_dw, b_dw, ln_g, ln_b, w_conv_out, a_re, a_im, log_dt, b_re, b_im, c_re, c_im, d_skip, w_glu_a, w_glu_b, w_out, norm2_g, w_ff1, w_ff2, final_g, loss_target, m_w_ada, m_b_ada, m_norm1_g, m_w_in, m_w_dw, m_b_dw, m_ln_g, m_ln_b, m_w_conv_out, m_a_re, m_a_im, m_log_dt, m_b_re, m_b_im, m_c_re, m_c_im, m_d_skip, m_w_glu_a, m_w_glu_b, m_w_out, m_norm2_g, m_w_ff1, m_w_ff2, m_final_g, v_w_ada, v_b_ada, v_norm1_g, v_w_in, v_w_dw, v_b_dw, v_ln_g, v_ln_b, v_w_conv_out, v_a_re, v_a_im, v_log_dt, v_b_re, v_b_im, v_c_re, v_c_im, v_d_skip, v_w_glu_a, v_w_glu_b, v_w_out, v_norm2_g, v_w_ff1, v_w_ff2, v_final_g):
    given = dict(locals())
    w = {n: given[n] for n in WEIGHT_NAMES}
    m = {n: given["m_" + n] for n in WEIGHT_NAMES}
    v = {n: given["v_" + n] for n in WEIGHT_NAMES}
    d = x.shape[2]
    xi, yi, ci = _mesh_pos()
    chip = 2 * xi + yi
    dev = 4 * xi + 2 * yi + ci
    my_c = jnp.reshape(ci, (1,)).astype(jnp.int32)
    chip_arr = jnp.reshape(chip, (1,)).astype(jnp.int32)

    comm = _MeshComm({n: w[n][0] for n in BIG_NAMES}, (xi, yi, ci), chip_arr, my_c)

    ndw = w_dw.shape[2]
    assert d // SUBLANES == ndw
    first = jnp.concatenate([c.reshape(SUBLANES, ndw), jnp.pad(w_dw[0], ((0, HALO - CONV_KERNEL), (0, 0)))])
    first_all = _gather_small(first, name="gather_c_w_dw").reshape(N_DEV, SUBLANES + HALO, ndw)
    c_all = first_all[:, :SUBLANES].reshape(N_DEV, d)
    taps = first_all.reshape(N_CHIPS, 2, SUBLANES + HALO, ndw)[:, 0, SUBLANES:SUBLANES + CONV_KERNEL]
    w_dw_full = jnp.moveaxis(taps, 0, 1).reshape(CONV_KERNEL, N_CHIPS * ndw)

    nmod = w_ada.shape[2]
    b_cols = lax.dynamic_slice(b_ada, (0, chip * nmod), (1, nmod))
    mod_part = _ada_fwd(c_all, w_ada[0], b_cols)
    mod_all = _gather_small(mod_part, name="gather_mod").reshape(N_CHIPS, 2, N_DEV, nmod)[:, 0]
    mod_full = jnp.moveaxis(mod_all, 0, 1).reshape(N_DEV, N_CHIPS * nmod)
    mod_row = lax.dynamic_slice(mod_full, (dev, 0), (1, N_CHIPS * nmod))
    mod = [mod_row[:, i * d:(i + 1) * d] for i in range(6)]

    token = comm.start_w_in(mod_row)
    log_dt_0 = log_dt[0] + token[0, 0]

    disc_in = (a_re[0], a_im[0], log_dt_0, b_re[0], b_im[0])
    disc, disc_vjp = jax.vjp(_s5_discretise, *disc_in)
    dt = jnp.exp(log_dt_0)[:, None]
    small = {"norm1_g": norm1_g, "w_dw": w_dw_full, "b_dw": b_dw, "ln_g": ln_g, "ln_b": ln_b,
             "c_re": c_re[0], "c_im": c_im[0], "d_skip": d_skip, "norm2_g": norm2_g,
             "final_g": final_g[None, :], "s5_disc": disc, "s5_loglam": (a_re[0] * dt, a_im[0] * dt)}

    grad_x, early, late = _local_step(x[0], loss_target[0], mod, small, comm)
    grads = {}

    early_all = comm.early_all.reshape(N_DEV, -1, PACK_COLS)
    early_sum = _sum_leading(early_all, name="sum_small_grads").reshape(-1)
    summed = dict(zip(early, _unpack(early_sum, comm.early_offs, list(early.values()))))
    flat, late_offs = _pack(list(late.values()))
    late_all = _gather_small(flat.reshape(-1, PACK_COLS), name="gather_late_grads").reshape(N_DEV, -1, PACK_COLS)
    late_sum = _sum_leading(late_all, name="sum_late_grads").reshape(-1)
    summed.update(zip(late, _unpack(late_sum, late_offs, list(late.values()))))
    head = late_all[:, :2 * d // PACK_COLS].reshape(N_DEV, 2 * d)
    tail = early_all[:, :4 * d // PACK_COLS].reshape(N_DEV, 4 * d)
    dmod_all = jnp.concatenate([head, tail], axis=1)

    grads["w_ada"] = _ada_bwd(c_all, lax.dynamic_slice(dmod_all, (0, chip * nmod), (N_DEV, nmod)))
    grads["b_ada"] = _sum_leading(dmod_all.reshape(N_DEV, SUBLANES, 6 * d // SUBLANES),
                                  name="sum_b_ada").reshape(1, 6 * d)
    da_re, da_im, dlog_dt, db_re, db_im = disc_vjp(
        (summed["lam_re"], summed["lam_im"], summed["bb_re"], summed["bb_im"]))
    grads.update({
        "norm1_g": summed["norm1_g"], "w_dw": lax.dynamic_slice(summed["w_dw"], (0, chip * ndw), (CONV_KERNEL, ndw)),
        "b_dw": summed["b_dw"], "ln_g": summed["ln_g"], "ln_b": summed["ln_b"],
        "a_re": da_re, "a_im": da_im, "log_dt": dlog_dt, "b_re": db_re, "b_im": db_im,
        "c_re": summed["c_re"], "c_im": summed["c_im"], "d_skip": summed["d_skip"],
        "norm2_g": summed["norm2_g"], "final_g": summed["final_g"],
    })

    delta, new_m, new_v = {}, {}, {}
    grads.update(comm.join(BIG_NAMES, [late_all], name="join_halves"))
    for n in ["w_ada"] + BIG_NAMES:
        shp = w[n].shape
        two_d = lambda a: a.reshape(shp[1], shp[2])
        res = _adamw(two_d(w[n]), two_d(grads[n]), two_d(m[n]), two_d(v[n]), name="adamw_" + n)
        delta[n], new_m[n], new_v[n] = [r.reshape(shp) for r in res]
    grads = {n: grads[n].reshape(w[n].shape) for n in WEIGHT_NAMES}
    rest = [n for n in WEIGHT_NAMES if n not in delta]
    packs = []
    for src in (w, grads, m, v):
        flat, offs = _pack([src[n] for n in rest])
        packs.append(flat.reshape(-1, 1024))
    outs = _adamw(*packs, name="adamw_small")
    for dst, o in zip((delta, new_m, new_v), outs):
        for n, a in zip(rest, _unpack(o.reshape(-1), offs, [w[k] for k in rest])):
            dst[n] = a

    return (summed["loss"].reshape(()), grad_x[None], *[grads[n] for n in WEIGHT_NAMES],
            *[delta[n] for n in WEIGHT_NAMES], *[new_m[n] for n in WEIGHT_NAMES],
            *[new_v[n] for n in WEIGHT_NAMES])
```

```python
import functools
import math

import jax
import jax.numpy as jnp
from jax import lax
from jax.experimental import pallas as pl
from jax.experimental.pallas import tpu as pltpu

F32 = jnp.float32
BF16 = jnp.bfloat16
EPS = 1e-6
CONV_KERNEL = 31
SSM_GROUP = 16
SSM_STATE = 64
ADAM_LR = 0.001
ADAM_B1 = 0.9
ADAM_B2 = 0.999
ADAM_EPS = 1e-08
ADAM_WD = 0.01
ADAM_STEP = 10

N_CHIPS = 4
N_DEV = 8
VMEM_LIMIT_BYTES = 56 * 1024 * 1024
LANES = 128
SUBLANES = 8
HALO = 32
GROUPS_PER_BLOCK = LANES // SSM_GROUP
STATE_LANES = GROUPS_PER_BLOCK * SSM_STATE
MESH = pl.DeviceIdType.MESH


def _cparams(sem):
    return pltpu.CompilerParams(dimension_semantics=sem, vmem_limit_bytes=VMEM_LIMIT_BYTES)


def _pick(n, pref, mult=LANES):
    if n <= pref:
        return n
    best = None
    for d in range(mult, pref + 1, mult):
        if n % d == 0:
            best = d
    assert best is not None, (n, pref)
    return best


def _sigmoid(v):
    return 1.0 / (1.0 + jnp.exp(-v))


def _gelu_parts(v):
    k0 = math.sqrt(2.0 / math.pi)
    inner = k0 * (v + 0.044715 * v * v * v)
    t = jnp.tanh(inner)
    return k0, t


def _gelu(v):
    _, t = _gelu_parts(v)
    return 0.5 * v * (1.0 + t)


def _gelu_grad(v):
    k0, t = _gelu_parts(v)
    return 0.5 * (1.0 + t) + 0.5 * v * (1.0 - t * t) * k0 * (1.0 + 3.0 * 0.044715 * v * v)


def _relu2_bf16(a):
    t = jnp.maximum(a.astype(F32), 0.0)
    return (t * t).astype(BF16)


class _Carry:
    def __init__(self, inputs, out_shapes, aliases, sem_shapes, start, finish):
        self.inputs = list(inputs)
        self.out_shapes = list(out_shapes)
        self.aliases = dict(aliases)
        self.sem_shapes = list(sem_shapes)
        self.start = start
        self.finish = finish


def _call(body, *, grid, in_specs, out_specs, out_shape, scratch_shapes, semantics, name, args, carry=None,
          prefetch=(), aliases=None):
    n_in, n_out, n_scr, n_pf = len(in_specs), len(out_specs), len(scratch_shapes), len(prefetch)
    own_aliases = {n_pf + i: o for i, o in (aliases or {}).items()}
    if carry is None:
        gs = pltpu.PrefetchScalarGridSpec(
            num_scalar_prefetch=n_pf, grid=grid, in_specs=in_specs, out_specs=out_specs,
            scratch_shapes=scratch_shapes)
        outs = pl.pallas_call(
            body, grid_spec=gs, out_shape=out_shape, input_output_aliases=own_aliases,
            compiler_params=_cparams(semantics), name=name)(*prefetch, *args)
        return list(outs), []
    ci, co = len(carry.inputs), len(carry.out_shapes)

    def wrapped(*refs):
        pf, refs = refs[:n_pf], refs[n_pf:]
        ins, cins = refs[:n_in], refs[n_in:n_in + ci]
        p = n_in + ci
        outs, couts = refs[p:p + n_out], refs[p + n_out:p + n_out + co]
        p += n_out + co
        scr, csems = refs[p:p + n_scr], refs[p + n_scr:]
        first = pl.program_id(0) == 0
        last = pl.program_id(0) == grid[0] - 1
        for ax in range(1, len(grid)):
            first = jnp.logical_and(first, pl.program_id(ax) == 0)
            last = jnp.logical_and(last, pl.program_id(ax) == grid[ax] - 1)

        @pl.when(first)
        def _():
            carry.start(cins, couts, csems)

        body(*pf, *ins, *outs, *scr)

        @pl.when(last)
        def _():
            carry.finish(cins, couts, csems)

    any_spec = pl.BlockSpec(memory_space=pl.ANY)
    gs = pltpu.PrefetchScalarGridSpec(
        num_scalar_prefetch=n_pf, grid=grid, in_specs=list(in_specs) + [any_spec] * ci,
        out_specs=list(out_specs) + [any_spec] * co, scratch_shapes=list(scratch_shapes) + carry.sem_shapes)
    all_aliases = dict(own_aliases)
    all_aliases.update({n_pf + n_in + i: n_out + o for i, o in carry.aliases.items()})
    outs = pl.pallas_call(
        wrapped, grid_spec=gs, out_shape=list(out_shape) + carry.out_shapes, input_output_aliases=all_aliases,
        compiler_params=_cparams(("arbitrary",) * len(grid)), name=name)(*prefetch, *args, *carry.inputs)
    return list(outs[:n_out]), list(outs[n_out:])


def _run_carry(carry, *, name):
    ci = len(carry.inputs)

    def body(*refs):
        cins, couts, csems = refs[:ci], refs[ci:ci + len(carry.out_shapes)], refs[ci + len(carry.out_shapes):]
        carry.start(cins, couts, csems)
        carry.finish(cins, couts, csems)

    any_spec = pl.BlockSpec(memory_space=pl.ANY)
    outs = pl.pallas_call(
        body, in_specs=[any_spec] * ci, out_specs=[any_spec] * len(carry.out_shapes), out_shape=carry.out_shapes,
        scratch_shapes=carry.sem_shapes, input_output_aliases=carry.aliases, name=name)(*carry.inputs)
    return list(outs)


def _mm(a, b, *, mode, out_dtype, name, out_gathered=False, a_fn=None, epi=None, extra=None,
        bm_pref=1024, bn_pref=1024, bk_pref=2048, carry=None, a_slots=None):
    gathered = (b.ndim == 3)
    if mode == "nn":
        m, kdim = a.shape
        ns = b.shape[-1]
        n = ns * (N_CHIPS if gathered else 1)
        bm, bn, bk = _pick(m, bm_pref), _pick(ns, bn_pref), _pick(kdim, bk_pref)
        npb = ns // bn
        grid = (m // bm, n // bn, kdim // bk)
        a_spec = pl.BlockSpec((bm, bk), lambda i, j, k: (i, k))
        if gathered:
            b_spec = pl.BlockSpec((None, bk, bn), lambda i, j, k: (j // npb, k, j % npb))
        else:
            b_spec = pl.BlockSpec((bk, bn), lambda i, j, k: (k, j))
        o_spec = pl.BlockSpec((bm, bn), lambda i, j, k: (i, j))
        e_spec = pl.BlockSpec((bm, bn), lambda i, j, k: (i, j))
        out_shape = (m, n)
        acc_shape = (bm, bn)
        dims = (((1,), (0,)), ((), ()))
    elif mode == "nt":
        m = a.shape[0]
        kdim, ns = b.shape[-2], b.shape[-1]
        n = ns * (N_CHIPS if gathered else 1)
        assert a.shape[1] == n
        bm, bko, bnr = _pick(m, bm_pref), _pick(kdim, bn_pref), _pick(ns, bk_pref)
        npb = ns // bnr
        grid = (m // bm, kdim // bko, n // bnr)
        a_spec = pl.BlockSpec((bm, bnr), lambda i, j, k: (i, k))
        if gathered:
            b_spec = pl.BlockSpec((None, bko, bnr), lambda i, j, k: (k // npb, j, k % npb))
        else:
            b_spec = pl.BlockSpec((bko, bnr), lambda i, j, k: (j, k))
        o_spec = pl.BlockSpec((bm, bko), lambda i, j, k: (i, j))
        e_spec = pl.BlockSpec((bm, bko), lambda i, j, k: (i, j))
        if a_slots is not None:
            assert gathered and extra is None
            a_spec = pl.BlockSpec((bm, bnr), lambda i, j, k, s_ref: (i, s_ref[k // npb] * npb + k % npb))
            b_spec = pl.BlockSpec((None, bko, bnr), lambda i, j, k, s_ref: (k // npb, j, k % npb))
            o_spec = pl.BlockSpec((bm, bko), lambda i, j, k, s_ref: (i, j))
        out_shape = (m, kdim)
        acc_shape = (bm, bko)
        dims = (((1,), (1,)), ((), ()))
    else:
        m, kdim = a.shape
        n = b.shape[1]
        ns = n // N_CHIPS if out_gathered else n
        bmr, bko, bn = _pick(m, bk_pref), _pick(kdim, bm_pref), _pick(ns, bn_pref)
        npb = ns // bn
        grid = (kdim // bko, n // bn, m // bmr)
        a_spec = pl.BlockSpec((bmr, bko), lambda i, j, k: (k, i))
        b_spec = pl.BlockSpec((bmr, bn), lambda i, j, k: (k, j))
        if out_gathered:
            o_spec = pl.BlockSpec((None, bko, bn), lambda i, j, k: (j // npb, i, j % npb))
            out_shape = (N_CHIPS, kdim, ns)
        else:
            o_spec = pl.BlockSpec((bko, bn), lambda i, j, k: (i, j))
            out_shape = (kdim, n)
        e_spec = None
        acc_shape = (bko, bn)
        dims = (((0,), (0,)), ((), ()))
    nk = grid[2]

    def body(*refs):
        if a_slots is not None:
            refs = refs[1:]
        if extra is not None:
            a_ref, b_ref, e_ref, o_ref, acc = refs
        else:
            a_ref, b_ref, o_ref, acc = refs
            e_ref = None
        k = pl.program_id(2)
        av = a_ref[...]
        if a_fn is not None:
            av = a_fn(av)
        part = lax.dot_general(av, b_ref[...], dims, preferred_element_type=F32)

        def finish(r):
            if epi is not None:
                r = epi(r, e_ref[...])
            o_ref[...] = r.astype(o_ref.dtype)

        if nk == 1:
            finish(part)
            return

        @pl.when(k == 0)
        def _():
            acc[...] = part

        @pl.when(jnp.logical_and(k > 0, k < nk - 1))
        def _():
            acc[...] += part

        @pl.when(k == nk - 1)
        def _():
            finish(acc[...] + part)

    in_specs = [a_spec, b_spec]
    args = [a, b]
    if extra is not None:
        in_specs.append(e_spec)
        args.append(extra)
    outs, carried = _call(body, grid=grid, in_specs=in_specs, out_specs=[o_spec],
                          out_shape=[jax.ShapeDtypeStruct(out_shape, out_dtype)],
                          scratch_shapes=[pltpu.VMEM(acc_shape, F32)],
                          semantics=("parallel", "parallel", "arbitrary"), name=name, args=args, carry=carry,
                          prefetch=() if a_slots is None else (a_slots,))
    return outs[0] if carry is None else (outs[0], carried)


def _mm_slots(a, wbuf, slots, prev, *, name, carry=None):
    m, kdim = a.shape
    ns = wbuf.shape[2]
    bm, bn = _pick(m, 1024), _pick(ns, 1024)
    npb = ns // bn
    grid = (m // bm, slots.shape[0], npb)

    def body(s_ref, a_ref, b_ref, *rest):
        o_ref = rest[-1]
        o_ref[...] = _dot(a_ref[...], b_ref[...]).astype(o_ref.dtype)

    in_specs = [pl.BlockSpec((bm, kdim), lambda i, s, j, s_ref: (i, 0)),
                pl.BlockSpec((None, kdim, bn), lambda i, s, j, s_ref: (s, 0, j))]
    args = [a, wbuf]
    aliases = None
    if prev is not None:
        in_specs.append(pl.BlockSpec(memory_space=pl.ANY))
        args.append(prev)
        aliases = {2: 0}
    outs, carried = _call(
        body, grid=grid, in_specs=in_specs,
        out_specs=[pl.BlockSpec((bm, bn), lambda i, s, j, s_ref: (i, s_ref[s] * npb + j))],
        out_shape=[jax.ShapeDtypeStruct((m, N_CHIPS * ns), BF16)], scratch_shapes=[],
        semantics=("parallel", "arbitrary", "arbitrary"), name=name, args=args, carry=carry,
        prefetch=(slots,), aliases=aliases)
    return outs[0] if carry is None else (outs[0], carried)


def _row_tile(rows, cols, n_arrays):
    budget = VMEM_LIMIT_BYTES // 3
    cap = min(512, budget // (n_arrays * 2 * cols * 4))
    for t in range(cap - cap % SUBLANES, 0, -SUBLANES):
        if rows % t == 0:
            return t
    return rows


def _norm_mod(x, g, scale, shift, *, name):
    rows, d = x.shape
    tr = _row_tile(rows, d, 3)

    def body(x_ref, g_ref, sc_ref, sh_ref, o_ref):
        xv = x_ref[...]
        r = lax.rsqrt(jnp.mean(xv * xv, axis=-1, keepdims=True) + EPS)
        o_ref[...] = ((xv * r * g_ref[...]) * (1.0 + sc_ref[...]) + sh_ref[...]).astype(o_ref.dtype)

    row = pl.BlockSpec((tr, d), lambda i: (i, 0))
    vec = pl.BlockSpec((1, d), lambda i: (0, 0))
    return pl.pallas_call(
        body, grid=(rows // tr,), in_specs=[row, vec, vec, vec], out_specs=row,
        out_shape=jax.ShapeDtypeStruct((rows, d), BF16),
        compiler_params=_cparams(("parallel",)), name=name)(x, g, scale, shift)


CONV_CHUNK = 2 * SUBLANES


def _shifted_copies(buf, n):
    for r in range(1, SUBLANES):
        buf[r, pl.ds(0, n - SUBLANES), :] = buf[0, pl.ds(r, n - SUBLANES), :]


def _conv_fwd(proj, w_dw, b_dw, ln_g, ln_b, *, cw, carry=None):
    rows = proj.shape[0]
    tt = _pick(rows, 256, HALO)
    hb = tt // HALO

    def body(a_ref, g_ref, ha_ref, hg_ref, w_ref, b_ref, lg_ref, lb_ref, sl_ref, cv_ref, vs):
        i = pl.program_id(0)
        hv = ha_ref[...].astype(F32) * _sigmoid(hg_ref[...].astype(F32))
        vs[0, pl.ds(0, HALO), :] = jnp.where(i == 0, 0.0, hv)
        vs[0, pl.ds(HALO, tt), :] = a_ref[...].astype(F32) * _sigmoid(g_ref[...].astype(F32))
        _shifted_copies(vs, HALO + tt)

        def chunk(ci, carry):
            r0 = pl.multiple_of(ci * CONV_CHUNK, CONV_CHUNK)
            acc = jnp.broadcast_to(b_ref[...], (CONV_CHUNK, cw))
            for k in range(CONV_KERNEL):
                q, r = divmod(HALO - (CONV_KERNEL - 1) + k, SUBLANES)
                acc = acc + w_ref[pl.ds(k, 1), :] * vs[r, pl.ds(r0 + q * SUBLANES, CONV_CHUNK), :]
            cv_ref[pl.ds(r0, CONV_CHUNK), :] = acc
            return carry

        lax.fori_loop(0, tt // CONV_CHUNK, chunk, 0)
        acc = cv_ref[...]
        mu = jnp.mean(acc, axis=-1, keepdims=True)
        xc = acc - mu
        rstd = lax.rsqrt(jnp.mean(xc * xc, axis=-1, keepdims=True) + EPS)
        ln = xc * rstd * lg_ref[...] + lb_ref[...]
        sl_ref[...] = (ln * _sigmoid(ln)).astype(sl_ref.dtype)

    tile = lambda c: pl.BlockSpec((tt, cw), lambda i, c=c: (i, c))
    halo = lambda c: pl.BlockSpec((HALO, cw), lambda i, c=c: (jnp.maximum(i * hb - 1, 0), c))
    vec = pl.BlockSpec((1, cw), lambda i: (0, 0))
    outs, carried = _call(
        body, grid=(rows // tt,),
        in_specs=[tile(0), tile(1), halo(0), halo(1),
                  pl.BlockSpec((CONV_KERNEL, cw), lambda i: (0, 0)), vec, vec, vec],
        out_specs=[pl.BlockSpec((tt, cw), lambda i: (i, 0)), pl.BlockSpec((tt, cw), lambda i: (i, 0))],
        out_shape=[jax.ShapeDtypeStruct((rows, cw), BF16), jax.ShapeDtypeStruct((rows, cw), F32)],
        scratch_shapes=[pltpu.VMEM((SUBLANES, HALO + tt, cw), F32)],
        semantics=("parallel",), name="conv_fwd", args=[proj, proj, proj, proj, w_dw, b_dw, ln_g, ln_b],
        carry=carry)
    return outs if carry is None else (outs, carried)


def _ln_bwd(dsl, cv, ln_g, ln_b):
    rows, cw = cv.shape
    tr = _row_tile(rows, cw, 3)

    def body(d_ref, cv_ref, lg_ref, lb_ref, o_ref, dg_ref, db_ref):
        i = pl.program_id(0)

        @pl.when(i == 0)
        def _():
            dg_ref[...] = jnp.zeros_like(dg_ref)
            db_ref[...] = jnp.zeros_like(db_ref)

        x = cv_ref[...]
        mu = jnp.mean(x, axis=-1, keepdims=True)
        xc = x - mu
        rstd = lax.rsqrt(jnp.mean(xc * xc, axis=-1, keepdims=True) + EPS)
        xh = xc * rstd
        ln = xh * lg_ref[...] + lb_ref[...]
        s = _sigmoid(ln)
        dln = d_ref[...].astype(F32) * (s * (1.0 + ln * (1.0 - s)))
        dg_ref[...] += jnp.sum(dln * xh, axis=0, keepdims=True)
        db_ref[...] += jnp.sum(dln, axis=0, keepdims=True)
        dxh = dln * lg_ref[...]
        m1 = jnp.mean(dxh, axis=-1, keepdims=True)
        m2 = jnp.mean(dxh * xh, axis=-1, keepdims=True)
        o_ref[...] = rstd * (dxh - m1 - xh * m2)

    row = pl.BlockSpec((tr, cw), lambda i: (i, 0))
    vec = pl.BlockSpec((1, cw), lambda i: (0, 0))
    return pl.pallas_call(
        body, grid=(rows // tr,), in_specs=[row, row, vec, vec], out_specs=[row, vec, vec],
        out_shape=[jax.ShapeDtypeStruct((rows, cw), F32), jax.ShapeDtypeStruct((1, cw), F32),
                   jax.ShapeDtypeStruct((1, cw), F32)],
        compiler_params=_cparams(("arbitrary",)), name="ln_bwd")(dsl, cv, ln_g, ln_b)


def _conv_bwd(dcv, proj, w_dw, dproj, *, cw, carry=None):
    rows = proj.shape[0]
    tt = _pick(rows, 256, HALO)
    hb = tt // HALO
    nt = rows // tt
    taps = CONV_KERNEL

    def body(d_ref, dn_ref, a_ref, g_ref, ha_ref, hg_ref, w_ref, dproj_ref, o_ref, dw_ref, db_ref, vs, ds):
        i = pl.program_id(0)

        @pl.when(i == 0)
        def _():
            dw_ref[...] = jnp.zeros_like(dw_ref)
            db_ref[...] = jnp.zeros_like(db_ref)

        hv = ha_ref[...].astype(F32) * _sigmoid(hg_ref[...].astype(F32))
        vs[0, pl.ds(0, HALO), :] = jnp.where(i == 0, 0.0, hv)
        vs[0, pl.ds(HALO, tt), :] = a_ref[...].astype(F32) * _sigmoid(g_ref[...].astype(F32))
        _shifted_copies(vs, HALO + tt)
        ds[0, pl.ds(0, tt), :] = d_ref[...]
        ds[0, pl.ds(tt, HALO), :] = jnp.where(i == nt - 1, 0.0, dn_ref[...])
        _shifted_copies(ds, tt + HALO)
        db_ref[...] += jnp.sum(d_ref[...], axis=0, keepdims=True)
        for k in range(taps):
            q, r = divmod(HALO - (taps - 1) + k, SUBLANES)
            dw_ref[pl.ds(k, 1), :] += jnp.sum(d_ref[...] * vs[r, pl.ds(q * SUBLANES, tt), :], axis=0, keepdims=True)

        def chunk(ci, carry):
            r0 = pl.multiple_of(ci * CONV_CHUNK, CONV_CHUNK)
            dv = jnp.zeros((CONV_CHUNK, cw), F32)
            for k in range(taps):
                q, r = divmod(taps - 1 - k, SUBLANES)
                dv = dv + w_ref[pl.ds(k, 1), :] * ds[r, pl.ds(r0 + q * SUBLANES, CONV_CHUNK), :]
            av = a_ref[pl.ds(r0, CONV_CHUNK), :].astype(F32)
            sg = _sigmoid(g_ref[pl.ds(r0, CONV_CHUNK), :].astype(F32))
            o_ref[pl.ds(r0, CONV_CHUNK), pl.ds(0, cw)] = (dv * sg).astype(o_ref.dtype)
            o_ref[pl.ds(r0, CONV_CHUNK), pl.ds(cw, cw)] = (dv * av * sg * (1.0 - sg)).astype(o_ref.dtype)
            return carry

        lax.fori_loop(0, tt // CONV_CHUNK, chunk, 0)

    tile = lambda c: pl.BlockSpec((tt, cw), lambda i, c=c: (i, c))
    halo = lambda c: pl.BlockSpec((HALO, cw), lambda i, c=c: (jnp.maximum(i * hb - 1, 0), c))
    nxt = pl.BlockSpec((HALO, cw), lambda i: (jnp.minimum((i + 1) * hb, nt * hb - 1), 0))
    outs, carried = _call(
        body, grid=(nt,),
        in_specs=[pl.BlockSpec((tt, cw), lambda i: (i, 0)), nxt, tile(0), tile(1), halo(0), halo(1),
                  pl.BlockSpec((taps, cw), lambda i: (0, 0)), pl.BlockSpec(memory_space=pl.ANY)],
        out_specs=[pl.BlockSpec((tt, 2 * cw), lambda i: (i, 0)),
                   pl.BlockSpec((taps, cw), lambda i: (0, 0)), pl.BlockSpec((1, cw), lambda i: (0, 0))],
        out_shape=[jax.ShapeDtypeStruct(dproj.shape, dproj.dtype), jax.ShapeDtypeStruct((taps, cw), F32),
                   jax.ShapeDtypeStruct((1, cw), F32)],
        scratch_shapes=[pltpu.VMEM((SUBLANES, HALO + tt, cw), F32), pltpu.VMEM((SUBLANES, tt + HALO, cw), F32)],
        semantics=("arbitrary",), name="conv_bwd", args=[dcv, dcv, proj, proj, proj, proj, w_dw, dproj],
        carry=carry, aliases={7: 0})
    return outs if carry is None else (outs, carried)


def _merge_fwd(proj, y_conv, ya, yb, *, cw):
    rows = proj.shape[0]
    tr = _row_tile(rows, cw, 4)

    def body(gc_ref, gs_ref, yc_ref, ya_ref, yb_ref, o_ref):
        ys = ya_ref[...].astype(F32) * _sigmoid(yb_ref[...].astype(F32))
        o_ref[...] = (_sigmoid(gc_ref[...].astype(F32)) * yc_ref[...].astype(F32)
                      + _sigmoid(gs_ref[...].astype(F32)) * ys).astype(o_ref.dtype)

    blk = lambda off: pl.BlockSpec((tr, cw), lambda i, h, off=off: (i, off + h))
    return pl.pallas_call(
        body, grid=(rows // tr, 2), in_specs=[blk(3), blk(5), blk(0), blk(0), blk(0)], out_specs=blk(0),
        out_shape=jax.ShapeDtypeStruct((rows, 2 * cw), BF16),
        compiler_params=_cparams(("parallel", "parallel")), name="merge_fwd")(proj, proj, y_conv, ya, yb)


def _merge_bwd(dmerged, proj, y_conv, ya, yb, *, cw, carry=None):
    rows = proj.shape[0]
    tr = _row_tile(rows, cw, 6)

    def body(d_ref, g_ref, yc_ref, ya_ref, yb_ref, dg_ref, dyc_ref, dya_ref, dyb_ref):
        q = pl.program_id(1)
        d = d_ref[...].astype(F32)
        sg = _sigmoid(g_ref[...].astype(F32))

        @pl.when(q < 2)
        def _():
            dg_ref[...] = (d * yc_ref[...].astype(F32) * sg * (1.0 - sg)).astype(dg_ref.dtype)
            dyc_ref[...] = (d * sg).astype(dyc_ref.dtype)

        @pl.when(q >= 2)
        def _():
            sb = _sigmoid(yb_ref[...].astype(F32))
            yav = ya_ref[...].astype(F32)
            dg_ref[...] = (d * (yav * sb) * sg * (1.0 - sg)).astype(dg_ref.dtype)
            dys = d * sg
            dya_ref[...] = (dys * sb).astype(dya_ref.dtype)
            dyb_ref[...] = (dys * yav * sb * (1.0 - sb)).astype(dyb_ref.dtype)

    spec = lambda f: pl.BlockSpec((tr, cw), lambda i, q, f=f: (i, f(q)))
    conv_half = spec(lambda q: jnp.minimum(q, 1))
    ssm_half = spec(lambda q: jnp.maximum(q - 2, 0))
    o2 = jax.ShapeDtypeStruct((rows, 2 * cw), BF16)
    outs, carried = _call(
        body, grid=(rows // tr, 4),
        in_specs=[spec(lambda q: q % 2), spec(lambda q: 3 + q), conv_half, ssm_half, ssm_half],
        out_specs=[spec(lambda q: 3 + q), conv_half, ssm_half, ssm_half],
        out_shape=[jax.ShapeDtypeStruct((rows, 7 * cw), BF16), o2, o2, o2], scratch_shapes=[],
        semantics=("parallel", "arbitrary"), name="merge_bwd", args=[dmerged, proj, y_conv, ya, yb],
        carry=carry)
    return outs if carry is None else (outs, carried)


def _res_norm(x, mo, gate, g, scale, shift):
    rows, d = x.shape
    tr = _row_tile(rows, d, 4)

    def body(x_ref, mo_ref, gt_ref, g_ref, sc_ref, sh_ref, h_ref, z_ref):
        h = x_ref[...] + gt_ref[...] * mo_ref[...].astype(F32)
        h_ref[...] = h
        r = lax.rsqrt(jnp.mean(h * h, axis=-1, keepdims=True) + EPS)
        z_ref[...] = ((h * r * g_ref[...]) * (1.0 + sc_ref[...]) + sh_ref[...]).astype(z_ref.dtype)

    row = pl.BlockSpec((tr, d), lambda i: (i, 0))
    vec = pl.BlockSpec((1, d), lambda i: (0, 0))
    return pl.pallas_call(
        body, grid=(rows // tr,), in_specs=[row, row, vec, vec, vec, vec], out_specs=[row, row],
        out_shape=[jax.ShapeDtypeStruct((rows, d), F32), jax.ShapeDtypeStruct((rows, d), BF16)],
        compiler_params=_cparams(("parallel",)), name="res_norm")(x, mo, gate, g, scale, shift)


def _final_fwd_bwd(h1, ff, gate2, final_g, target):
    rows, d = h1.shape
    tr = _row_tile(rows, d, 5)

    def body(h_ref, ff_ref, gt_ref, fg_ref, t_ref, dh_ref, dff_ref, loss_ref, dfg_ref, dgt_ref):
        i = pl.program_id(0)

        @pl.when(i == 0)
        def _():
            loss_ref[...] = jnp.zeros_like(loss_ref)
            dfg_ref[...] = jnp.zeros_like(dfg_ref)
            dgt_ref[...] = jnp.zeros_like(dgt_ref)

        ffv = ff_ref[...].astype(F32)
        h2 = h_ref[...] + gt_ref[...] * ffv
        r = lax.rsqrt(jnp.mean(h2 * h2, axis=-1, keepdims=True) + EPS)
        y = h2 * r
        e = y * fg_ref[...] - t_ref[...]
        loss_ref[...] += 0.5 * jnp.sum(jnp.mean(e * e, axis=-1, keepdims=True))
        dout = e * (1.0 / d)
        dfg_ref[...] += jnp.sum(dout * y, axis=0, keepdims=True)
        dy = dout * fg_ref[...]
        dh2 = r * (dy - y * jnp.mean(dy * y, axis=-1, keepdims=True))
        dh_ref[...] = dh2
        dgt_ref[...] += jnp.sum(dh2 * ffv, axis=0, keepdims=True)
        dff_ref[...] = (dh2 * gt_ref[...]).astype(dff_ref.dtype)

    row = pl.BlockSpec((tr, d), lambda i: (i, 0))
    vec = pl.BlockSpec((1, d), lambda i: (0, 0))
    return pl.pallas_call(
        body, grid=(rows // tr,), in_specs=[row, row, vec, vec, row],
        out_specs=[row, row, pl.BlockSpec((1, LANES), lambda i: (0, 0)), vec, vec],
        out_shape=[jax.ShapeDtypeStruct((rows, d), F32), jax.ShapeDtypeStruct((rows, d), BF16),
                   jax.ShapeDtypeStruct((1, LANES), F32), jax.ShapeDtypeStruct((1, d), F32),
                   jax.ShapeDtypeStruct((1, d), F32)],
        compiler_params=_cparams(("arbitrary",)), name="final_fwd_bwd")(h1, ff, gate2, final_g, target)


def _norm_mod_bwd(dz, hin, dres, g, scale, gate, mo, *, name, carry=None):
    rows, d = hin.shape
    with_gate = gate is not None
    tr = _row_tile(rows, d, 6)

    def body(*refs):
        if with_gate:
            (dz_ref, h_ref, dr_ref, g_ref, sc_ref, gt_ref, mo_ref,
             dh_ref, dsh_ref, dsc_ref, dg_ref, dmo_ref, dgt_ref) = refs
        else:
            dz_ref, h_ref, dr_ref, g_ref, sc_ref, dh_ref, dsh_ref, dsc_ref, dg_ref = refs
        i = pl.program_id(0)

        @pl.when(i == 0)
        def _():
            dsh_ref[...] = jnp.zeros_like(dsh_ref)
            dsc_ref[...] = jnp.zeros_like(dsc_ref)
            dg_ref[...] = jnp.zeros_like(dg_ref)
            if with_gate:
                dgt_ref[...] = jnp.zeros_like(dgt_ref)

        dzv = dz_ref[...].astype(F32)
        h = h_ref[...]
        r = lax.rsqrt(jnp.mean(h * h, axis=-1, keepdims=True) + EPS)
        y = h * r
        dsh_ref[...] += jnp.sum(dzv, axis=0, keepdims=True)
        dsc_ref[...] += jnp.sum(dzv * (y * g_ref[...]), axis=0, keepdims=True)
        dn = dzv * (1.0 + sc_ref[...])
        dg_ref[...] += jnp.sum(dn * y, axis=0, keepdims=True)
        dy = dn * g_ref[...]
        dh = dr_ref[...] + r * (dy - y * jnp.mean(dy * y, axis=-1, keepdims=True))
        dh_ref[...] = dh
        if with_gate:
            dmo_ref[...] = (dh * gt_ref[...]).astype(dmo_ref.dtype)
            dgt_ref[...] += jnp.sum(dh * mo_ref[...].astype(F32), axis=0, keepdims=True)

    row = pl.BlockSpec((tr, d), lambda i: (i, 0))
    vec = pl.BlockSpec((1, d), lambda i: (0, 0))
    vshape = jax.ShapeDtypeStruct((1, d), F32)
    in_specs = [row, row, row, vec, vec]
    args = [dz, hin, dres, g, scale]
    out_specs = [row, vec, vec, vec]
    out_shape = [jax.ShapeDtypeStruct((rows, d), F32), vshape, vshape, vshape]
    if with_gate:
        in_specs += [vec, row]
        args += [gate, mo]
        out_specs += [row, vec]
        out_shape += [jax.ShapeDtypeStruct((rows, d), BF16), vshape]
    outs, carried = _call(
        body, grid=(rows // tr,), in_specs=in_specs, out_specs=out_specs, out_shape=out_shape,
        scratch_shapes=[], semantics=("arbitrary",), name=name, args=args, carry=carry)
    return outs if carry is None else (outs, carried)


def _s5_discretise(a_re, a_im, log_dt, b_re, b_im):
    dt = jnp.exp(log_dt)[:, None]
    er = jnp.exp(a_re * dt)
    lr = er * jnp.cos(a_im * dt)
    li = er * jnp.sin(a_im * dt)
    den = a_re * a_re + a_im * a_im
    cr = ((lr - 1.0) * a_re + li * a_im) / den
    ci = (li * a_re - (lr - 1.0) * a_im) / den
    bbr = cr[..., None] * b_re - ci[..., None] * b_im
    bbi = cr[..., None] * b_im + ci[..., None] * b_re
    return lr, li, bbr, bbi


def _block_diag(w):
    g, r, c = w.shape
    nb = g // GROUPS_PER_BLOCK
    eye = jnp.eye(GROUPS_PER_BLOCK, dtype=w.dtype)
    w5 = w.reshape(nb, GROUPS_PER_BLOCK, r, 1, c) * eye[None, :, None, :, None]
    return w5.reshape(nb, GROUPS_PER_BLOCK * r, GROUPS_PER_BLOCK * c)


def _block_diag_extract(m, r, c):
    nb = m.shape[0]
    m5 = m.reshape(nb, GROUPS_PER_BLOCK, r, GROUPS_PER_BLOCK, c)
    idx = jnp.arange(GROUPS_PER_BLOCK)
    d = m5[:, idx, :, idx, :]
    return jnp.moveaxis(d, 0, 1).reshape(nb * GROUPS_PER_BLOCK, r, c)


def _scan_multipliers(lr, li):
    power = jnp.arange(1, SUBLANES + 1, dtype=F32)[None, :, None]
    er = jnp.exp(power * lr)
    pr = er * jnp.cos(power * li)
    pi = er * jnp.sin(power * li)
    rows = jnp.arange(SUBLANES)[None, :, None]
    fr, fi, rr, ri = [], [], [], []
    for s in (1, 2, 4):
        mf = (rows >= s).astype(F32)
        mr = (rows <= SUBLANES - 1 - s).astype(F32)
        fr.append(mf * pr[:, s - 1:s, :])
        fi.append(mf * pi[:, s - 1:s, :])
        rr.append(mr * pr[:, s - 1:s, :])
        ri.append(mr * pi[:, s - 1:s, :])
    fr.append(pr)
    fi.append(pi)
    rr.append(pr[:, ::-1, :])
    ri.append(pi[:, ::-1, :])
    st = lambda xs: jnp.stack(xs, axis=1)
    return st(fr), st(fi), st(rr), st(ri)


def _scan_rows(sre, sim, mul_r, mul_i, n_groups, reverse):
    sgn = -1.0 if reverse else 1.0
    lanes = sre.shape[1]

    def step(k, carry):
        cr, ci = carry
        kk = (n_groups - 1 - k) if reverse else k
        r0 = pl.multiple_of(kk * SUBLANES, SUBLANES)
        xr = sre[pl.ds(r0, SUBLANES), :]
        xi = sim[pl.ds(r0, SUBLANES), :]
        for lvl, s in enumerate((1, 2, 4)):
            sh = (SUBLANES - s) if reverse else s
            nr = pltpu.roll(xr, sh, 0)
            ni = pltpu.roll(xi, sh, 0)
            mr = mul_r[lvl]
            mi = mul_i[lvl] * sgn
            xr, xi = xr + mr * nr - mi * ni, xi + mr * ni + mi * nr
        mr = mul_r[3]
        mi = mul_i[3] * sgn
        xr, xi = xr + mr * cr - mi * ci, xi + mr * ci + mi * cr
        sre[pl.ds(r0, SUBLANES), :] = xr
        sim[pl.ds(r0, SUBLANES), :] = xi
        edge = 0 if reverse else SUBLANES - 1
        ncr = jnp.broadcast_to(xr[edge:edge + 1, :], (SUBLANES, lanes))
        nci = jnp.broadcast_to(xi[edge:edge + 1, :], (SUBLANES, lanes))
        return ncr, nci

    zero = jnp.zeros((SUBLANES, lanes), F32)
    lax.fori_loop(0, n_groups, step, (zero, zero))


def _dot(a, b):
    return jnp.dot(a, b, preferred_element_type=F32)


def _dotf(a, b):
    return _dot(a.astype(BF16), b)


def _s5_operands(lr, li, bbr, bbi, c_re, c_im):
    g = lr.shape[0]
    nb = g // GROUPS_PER_BLOCK
    tb = lambda w: jnp.swapaxes(w, 1, 2)
    b_in = [_block_diag(tb(bbr)), _block_diag(tb(bbi))]
    c_out = [_block_diag(tb(c_re)), _block_diag(tb(c_im))]
    b_out = [_block_diag(bbr), _block_diag(bbi)]
    c_in = [_block_diag(c_re), _block_diag(c_im)]
    lam_r = lr.reshape(nb, 1, STATE_LANES)
    lam_i = li.reshape(nb, 1, STATE_LANES)
    mults = _scan_multipliers(lam_r, lam_i)
    cast = lambda ws: [w.astype(BF16) for w in ws]
    return cast(b_in), cast(c_out), cast(b_out), cast(c_in), mults


def _s5_fwd(proj, d_skip, b_in, c_out, mults, *, col0, carry=None):
    rows = proj.shape[0]
    nb = b_in[0].shape[0]
    tm = _pick(rows, 512, SUBLANES)
    n_tiles = rows // tm
    s_l = STATE_LANES

    def body(u_ref, dk_ref, br, bi, cr, ci, fr_ref, fi_ref, o_ref, sr_ref, si_ref, sre, sim):
        for t in range(n_tiles):
            rs = pl.ds(t * tm, tm)
            ub = u_ref[rs, :]
            sre[rs, :] = _dot(ub, br[...])
            sim[rs, :] = _dot(ub, bi[...])
        _scan_rows(sre, sim, fr_ref, fi_ref, rows // SUBLANES, False)
        for t in range(n_tiles):
            rs = pl.ds(t * tm, tm)
            srb = sre[rs, :].astype(BF16)
            sib = sim[rs, :].astype(BF16)
            sr_ref[rs, :] = srb
            si_ref[rs, :] = sib
            y0 = _dot(srb, cr[...]) - _dot(sib, ci[...])
            y1 = y0 + dk_ref[...] * u_ref[rs, :].astype(F32)
            o_ref[rs, :] = _gelu(y1).astype(o_ref.dtype)

    mat_in = pl.BlockSpec((None, LANES, s_l), lambda g: (g, 0, 0))
    mat_out = pl.BlockSpec((None, s_l, LANES), lambda g: (g, 0, 0))
    mul = pl.BlockSpec((None, 4, SUBLANES, s_l), lambda g: (g, 0, 0, 0))
    state = pl.BlockSpec((rows, s_l), lambda g: (0, g))
    outs, carried = _call(
        body, grid=(nb,),
        in_specs=[pl.BlockSpec((rows, LANES), lambda g: (0, col0 + g)), pl.BlockSpec((1, LANES), lambda g: (0, g))]
        + [mat_in] * 2 + [mat_out] * 2 + [mul] * 2,
        out_specs=[pl.BlockSpec((rows, LANES), lambda g: (0, g)), state, state],
        out_shape=[jax.ShapeDtypeStruct((rows, nb * LANES), BF16), jax.ShapeDtypeStruct((rows, nb * s_l), BF16),
                   jax.ShapeDtypeStruct((rows, nb * s_l), BF16)],
        scratch_shapes=[pltpu.VMEM((rows, s_l), F32), pltpu.VMEM((rows, s_l), F32)],
        semantics=("parallel",), name="s5_fwd", args=[proj, d_skip, *b_in, *c_out, mults[0], mults[1]], carry=carry)
    return outs if carry is None else (outs, carried)


def _s5_bwd(proj, dyg, d_skip, states, c_out, b_out, c_in, mults, dproj, *, col0, carry=None):
    rows = proj.shape[0]
    nb = c_out[0].shape[0]
    tm = _pick(rows, 512, SUBLANES)
    n_tiles = rows // tm
    s_l = STATE_LANES
    n_groups = rows // SUBLANES
    tn = (((0,), (0,)), ((), ()))

    def body(u_ref, dy_ref, dk_ref, sr_ref, si_ref, cr, ci, bor, boi, cir, cii, rr_ref, ri_ref, dproj_ref,
             du_ref, ddk_ref, dbr_ref, dbi_ref, dcr_ref, dci_ref, dlr_ref, dli_ref,
             gre, gim, dy1):
        ddk = jnp.zeros((1, LANES), F32)
        dcr = jnp.zeros((s_l, LANES), F32)
        dci = jnp.zeros((s_l, LANES), F32)
        for t in range(n_tiles):
            rs = pl.ds(t * tm, tm)
            srb = sr_ref[rs, :]
            sib = si_ref[rs, :]
            uf = u_ref[rs, :].astype(F32)
            y0 = _dot(srb, cr[...]) - _dot(sib, ci[...])
            y1 = y0 + dk_ref[...] * uf
            d1 = dy_ref[rs, :].astype(F32) * _gelu_grad(y1)
            dy1[rs, :] = d1
            ddk = ddk + jnp.sum(d1 * uf, axis=0, keepdims=True)
            d1b = d1.astype(BF16)
            dcr = dcr + lax.dot_general(srb, d1b, tn, preferred_element_type=F32)
            dci = dci - lax.dot_general(sib, d1b, tn, preferred_element_type=F32)
            gre[rs, :] = _dot(d1b, cir[...])
            gim[rs, :] = -_dot(d1b, cii[...])
        ddk_ref[...] = ddk
        dcr_ref[...] = dcr
        dci_ref[...] = dci

        last_row = lax.broadcasted_iota(jnp.int32, (SUBLANES, s_l), 0) == SUBLANES - 1

        def group(r0, s_r, s_i, carry):
            cr_, ci_, ar, ai = carry
            xr = gre[pl.ds(r0, SUBLANES), :]
            xi = gim[pl.ds(r0, SUBLANES), :]
            for lvl, s in enumerate((1, 2, 4)):
                nr = pltpu.roll(xr, SUBLANES - s, 0)
                ni = pltpu.roll(xi, SUBLANES - s, 0)
                mr = rr_ref[lvl]
                mi = ri_ref[lvl]
                xr, xi = xr + mr * nr + mi * ni, xi + mr * ni - mi * nr
            mr = rr_ref[3]
            mi = ri_ref[3]
            xr, xi = xr + mr * cr_ + mi * ci_, xi + mr * ci_ - mi * cr_
            gre[pl.ds(r0, SUBLANES), :] = xr
            gim[pl.ds(r0, SUBLANES), :] = xi
            nxt_r = jnp.where(last_row, cr_, pltpu.roll(xr, SUBLANES - 1, 0))
            nxt_i = jnp.where(last_row, ci_, pltpu.roll(xi, SUBLANES - 1, 0))
            ncr = jnp.broadcast_to(xr[0:1, :], (SUBLANES, s_l))
            nci = jnp.broadcast_to(xi[0:1, :], (SUBLANES, s_l))
            return ncr, nci, ar + nxt_r * s_r + nxt_i * s_i, ai + nxt_i * s_r - nxt_r * s_i

        def rev_step(k, carry):
            r0 = pl.multiple_of((n_groups // 2 - 1 - k) * 2 * SUBLANES, 2 * SUBLANES)
            s_r = sr_ref[pl.ds(r0, 2 * SUBLANES), :].astype(F32)
            s_i = si_ref[pl.ds(r0, 2 * SUBLANES), :].astype(F32)
            carry = group(r0 + SUBLANES, s_r[SUBLANES:], s_i[SUBLANES:], carry)
            return group(r0, s_r[:SUBLANES], s_i[:SUBLANES], carry)

        zero = jnp.zeros((SUBLANES, s_l), F32)
        _, _, ar, ai = lax.fori_loop(0, n_groups // 2, rev_step, (zero, zero, zero, zero))
        dlr_ref[...] = jnp.sum(ar, axis=0, keepdims=True)
        dli_ref[...] = jnp.sum(ai, axis=0, keepdims=True)

        dbr = jnp.zeros((LANES, s_l), F32)
        dbi = jnp.zeros((LANES, s_l), F32)
        for t in range(n_tiles):
            rs = pl.ds(t * tm, tm)
            gr = gre[rs, :]
            gi = gim[rs, :]
            grb = gr.astype(BF16)
            gib = gi.astype(BF16)
            du = _dot(grb, bor[...]) + _dot(gib, boi[...]) + dy1[rs, :] * dk_ref[...]
            du_ref[rs, :] = du.astype(du_ref.dtype)
            ub = u_ref[rs, :]
            dbr = dbr + lax.dot_general(ub, grb, tn, preferred_element_type=F32)
            dbi = dbi + lax.dot_general(ub, gib, tn, preferred_element_type=F32)
        dbr_ref[...] = dbr
        dbi_ref[...] = dbi

    mat_in = pl.BlockSpec((None, LANES, s_l), lambda g: (g, 0, 0))
    mat_out = pl.BlockSpec((None, s_l, LANES), lambda g: (g, 0, 0))
    mul = pl.BlockSpec((None, 4, SUBLANES, s_l), lambda g: (g, 0, 0, 0))
    lam = pl.BlockSpec((None, 1, s_l), lambda g: (g, 0, 0))
    col = pl.BlockSpec((rows, LANES), lambda g: (0, g))
    vec = pl.BlockSpec((1, LANES), lambda g: (0, g))
    state = pl.BlockSpec((rows, s_l), lambda g: (0, g))
    outs, carried = _call(
        body, grid=(nb,),
        in_specs=[pl.BlockSpec((rows, LANES), lambda g: (0, col0 + g)), col, vec]
        + [state] * 2 + [mat_out] * 2 + [mat_out] * 2 + [mat_in] * 2 + [mul] * 2
        + [pl.BlockSpec(memory_space=pl.ANY)],
        out_specs=[pl.BlockSpec((rows, LANES), lambda g: (0, col0 + g)), vec, mat_in, mat_in, mat_out, mat_out,
                   lam, lam],
        out_shape=[jax.ShapeDtypeStruct(dproj.shape, dproj.dtype), jax.ShapeDtypeStruct((1, nb * LANES), F32),
                   jax.ShapeDtypeStruct((nb, LANES, s_l), F32), jax.ShapeDtypeStruct((nb, LANES, s_l), F32),
                   jax.ShapeDtypeStruct((nb, s_l, LANES), F32), jax.ShapeDtypeStruct((nb, s_l, LANES), F32),
                   jax.ShapeDtypeStruct((nb, 1, s_l), F32), jax.ShapeDtypeStruct((nb, 1, s_l), F32)],
        scratch_shapes=[pltpu.VMEM((rows, s_l), F32)] * 2 + [pltpu.VMEM((rows, LANES), F32)],
        semantics=("parallel",), name="s5_bwd",
        args=[proj, dyg, d_skip, *states, *c_out, *b_out, *c_in, mults[2], mults[3], dproj], carry=carry,
        aliases={13: 0})
    return outs if carry is None else (outs, carried)


def _silu(v):
    return v * _sigmoid(v)


def _ada_fwd(c_all, w_shard, b_cols):
    d, n = w_shard.shape
    bn = _pick(n, 512)

    def body(c_ref, w_ref, b_ref, o_ref):
        ca = _silu(c_ref[...]).astype(BF16)
        o_ref[...] = _dot(ca, w_ref[...].astype(BF16)) + b_ref[...]

    return pl.pallas_call(
        body, grid=(n // bn,),
        in_specs=[pl.BlockSpec((N_DEV, d), lambda j: (0, 0)), pl.BlockSpec((d, bn), lambda j: (0, j)),
                  pl.BlockSpec((1, bn), lambda j: (0, j))],
        out_specs=pl.BlockSpec((N_DEV, bn), lambda j: (0, j)),
        out_shape=jax.ShapeDtypeStruct((N_DEV, n), F32),
        compiler_params=_cparams(("parallel",)), name="ada_fwd")(c_all, w_shard, b_cols)


def _ada_bwd(c_all, dmod_cols):
    d = c_all.shape[1]
    n = dmod_cols.shape[1]
    bn = _pick(n, 512)

    def body(c_ref, g_ref, o_ref):
        ca = _silu(c_ref[...]).astype(BF16)
        o_ref[...] = lax.dot_general(ca, g_ref[...].astype(BF16), (((0,), (0,)), ((), ())),
                                     preferred_element_type=F32)

    return pl.pallas_call(
        body, grid=(n // bn,),
        in_specs=[pl.BlockSpec((N_DEV, d), lambda j: (0, 0)), pl.BlockSpec((N_DEV, bn), lambda j: (0, j))],
        out_specs=pl.BlockSpec((d, bn), lambda j: (0, j)),
        out_shape=jax.ShapeDtypeStruct((d, n), F32),
        compiler_params=_cparams(("parallel",)), name="ada_bwd")(c_all, dmod_cols)


def _cast_bf16(w, *, name):
    rows, cols = w.shape
    tr = _row_tile(rows, cols, 2)

    def body(w_ref, o_ref):
        o_ref[...] = w_ref[...].astype(BF16)

    row = pl.BlockSpec((tr, cols), lambda i: (i, 0))
    return pl.pallas_call(
        body, grid=(rows // tr,), in_specs=[row], out_specs=row,
        out_shape=jax.ShapeDtypeStruct((rows, cols), BF16),
        compiler_params=_cparams(("parallel",)), name=name)(w)


def _adamw_update(w_ref, g_ref, m_ref, v_ref, d_ref, nm_ref, nv_ref):
    c1 = 1.0 / (1.0 - ADAM_B1 ** ADAM_STEP)
    c2 = 1.0 / (1.0 - ADAM_B2 ** ADAM_STEP)
    gv = g_ref[...]
    nm = ADAM_B1 * m_ref[...] + (1.0 - ADAM_B1) * gv
    nv = ADAM_B2 * v_ref[...] + (1.0 - ADAM_B2) * (gv * gv)
    nm_ref[...] = nm
    nv_ref[...] = nv
    d_ref[...] = -ADAM_LR * ((nm * c1) / (jnp.sqrt(nv * c2) + ADAM_EPS) + ADAM_WD * w_ref[...])


def _adamw_many(ws, gs, ms, vs, *, name):
    n = len(ws)

    def body(*refs):
        for i in range(n):
            _adamw_update(*[refs[k * n + i] for k in range(7)])

    vm = pl.BlockSpec(memory_space=pltpu.VMEM)
    shapes = [jax.ShapeDtypeStruct(a.shape, F32) for a in ws]
    outs = pl.pallas_call(
        body, in_specs=[vm] * (4 * n), out_specs=[vm] * (3 * n), out_shape=shapes * 3,
        compiler_params=pltpu.CompilerParams(vmem_limit_bytes=VMEM_LIMIT_BYTES), name=name)(*ws, *gs, *ms, *vs)
    return list(outs[:n]), list(outs[n:2 * n]), list(outs[2 * n:])


def _adamw(w, g, m, v, *, name, after=None):
    rows, cols = w.shape
    tr = _row_tile(rows, cols, 7)

    def body(w_ref, g_ref, m_ref, v_ref, *rest):
        _adamw_update(w_ref, g_ref, m_ref, v_ref, *rest[-3:])

    row = pl.BlockSpec((tr, cols), lambda i: (i, 0))
    shp = jax.ShapeDtypeStruct((rows, cols), F32)
    extra = [] if after is None else [after]
    outs, _ = _call(
        body, grid=(rows // tr,), in_specs=[row] * 4 + [pl.BlockSpec(memory_space=pl.ANY)] * len(extra),
        out_specs=[row] * 3, out_shape=[shp] * 3, scratch_shapes=[], semantics=("parallel",), name=name,
        args=[w, g, m, v] + extra)
    return outs


def _sum_leading(a, *, name, out_dtype=F32):
    n, rows, cols = a.shape
    tr = _row_tile(rows, cols, n + 1)

    def body(a_ref, o_ref):
        acc = a_ref[0].astype(F32)
        for i in range(1, n):
            acc = acc + a_ref[i].astype(F32)
        o_ref[...] = acc.astype(o_ref.dtype)

    return pl.pallas_call(
        body, grid=(rows // tr,), in_specs=[pl.BlockSpec((n, tr, cols), lambda i: (0, i, 0))],
        out_specs=pl.BlockSpec((tr, cols), lambda i: (i, 0)),
        out_shape=jax.ShapeDtypeStruct((rows, cols), out_dtype),
        compiler_params=_cparams(("parallel",)), name=name)(a)


def _add_half(dw, land, my_c, *, name):
    n, r, cols = dw.shape
    h = r // 2
    tr = _row_tile(h, cols, 3)
    hb = h // tr

    def body(c_ref, a_ref, b_ref, o_ref):
        o_ref[...] = (a_ref[...].astype(F32) + b_ref[...].astype(F32)).astype(o_ref.dtype)

    gs = pltpu.PrefetchScalarGridSpec(
        num_scalar_prefetch=1, grid=(n, hb),
        in_specs=[pl.BlockSpec((None, tr, cols), lambda s, i, c_ref: (s, c_ref[0] * hb + i, 0)),
                  pl.BlockSpec((None, tr, cols), lambda s, i, c_ref: (s, i, 0))],
        out_specs=pl.BlockSpec((None, tr, cols), lambda s, i, c_ref: (s, i, 0)))
    return pl.pallas_call(
        body, grid_spec=gs, out_shape=jax.ShapeDtypeStruct((n, h, cols), BF16),
        compiler_params=_cparams(("parallel", "parallel")), name=name)(my_c, dw, land)


def _mesh_pos():
    return lax.axis_index("x"), lax.axis_index("y"), lax.axis_index("c")


def _other_chips(x, y):
    return [(1 - x, y), (x, 1 - y), (1 - x, 1 - y)]


def _gather_small(blk, *, name):
    m_per, n = blk.shape

    def body(x_ref, out_ref, send_sems, recv_sems, local_sem):
        x, y, c = _mesh_pos()
        me, sibling = (x, y, c), (x, y, 1 - c)
        chips = _other_chips(x, y)

        def rows(px, py, pc):
            return out_ref.at[pl.ds((4 * px + 2 * py + pc) * m_per, m_per), :]

        def copy(k, block, to, src=None):
            return pltpu.make_async_remote_copy(
                src_ref=rows(*block) if src is None else src, dst_ref=rows(*block),
                send_sem=send_sems.at[k], recv_sem=recv_sems.at[k], device_id=to, device_id_type=MESH)

        mine = pltpu.make_async_copy(x_ref, rows(*me), local_sem)
        mine.start()
        first = [copy(0, me, sibling, src=x_ref)]
        first += [copy(1 + j, me, (*chip, c), src=x_ref) for j, chip in enumerate(chips)]
        for cp in first:
            cp.start()
        passed = [copy(4 + j, (*chip, c), sibling) for j, chip in enumerate(chips)]
        for j, chip in enumerate(chips):
            copy(1 + j, (*chip, c), me).wait_recv()
            passed[j].start()
        copy(0, sibling, me).wait_recv()
        for j, chip in enumerate(chips):
            copy(4 + j, (*chip, 1 - c), me).wait_recv()
        for cp in first + passed:
            cp.wait_send()
        mine.wait()

    return pl.pallas_call(
        body, out_shape=jax.ShapeDtypeStruct((N_DEV * m_per, n), blk.dtype),
        in_specs=[pl.BlockSpec(memory_space=pltpu.VMEM)], out_specs=pl.BlockSpec(memory_space=pltpu.VMEM),
        scratch_shapes=[pltpu.SemaphoreType.DMA((7,)), pltpu.SemaphoreType.DMA((7,)), pltpu.SemaphoreType.DMA],
        compiler_params=pltpu.CompilerParams(vmem_limit_bytes=VMEM_LIMIT_BYTES), name=name)(blk)


def _hbm_specs(n):
    return [pl.BlockSpec(memory_space=pl.ANY)] * n


def _gather_weights(shards):
    n = len(shards)

    def body(*refs):
        ins, outs = refs[:n], refs[n:2 * n]
        send_sems, recv_sems, local_sems = refs[2 * n:]
        x, y, c = _mesh_pos()
        me_chip = 2 * x + y
        sibling = (x, y, 1 - c)
        chips = _other_chips(x, y)

        def half(w, chip_idx, pc):
            h = shards[w].shape[0] // 2
            return outs[w].at[chip_idx, pl.ds(pc * h, h), :]

        def copy(w, k, chip_idx, pc, to, src=None):
            dst = half(w, chip_idx, pc)
            return pltpu.make_async_remote_copy(
                src_ref=dst if src is None else src, dst_ref=dst,
                send_sem=send_sems.at[6 * w + k], recv_sem=recv_sems.at[6 * w + k],
                device_id=to, device_id_type=MESH)

        local = [pltpu.make_async_copy(ins[w], outs[w].at[me_chip], local_sems.at[w]) for w in range(n)]
        for cp in local:
            cp.start()
        sends = []
        for w in range(n):
            h = shards[w].shape[0] // 2
            for j, chip in enumerate(chips):
                cp = copy(w, j, me_chip, c, (*chip, c), src=ins[w].at[pl.ds(c * h, h), :])
                cp.start()
                sends.append(cp)
        for w in range(n):
            for j, chip in enumerate(chips):
                chip_idx = 2 * chip[0] + chip[1]
                copy(w, j, chip_idx, c, (x, y, c)).wait_recv()
                cp = copy(w, 3 + j, chip_idx, c, sibling)
                cp.start()
                sends.append(cp)
        for w in range(n):
            for j, chip in enumerate(chips):
                copy(w, 3 + j, 2 * chip[0] + chip[1], 1 - c, (x, y, c)).wait_recv()
        for cp in sends:
            cp.wait_send()
        for cp in local:
            cp.wait()

    return pl.pallas_call(
        body, out_shape=[jax.ShapeDtypeStruct((N_CHIPS,) + s.shape, s.dtype) for s in shards],
        in_specs=_hbm_specs(n), out_specs=_hbm_specs(n),
        scratch_shapes=[pltpu.SemaphoreType.DMA((6 * n,)), pltpu.SemaphoreType.DMA((6 * n,)),
                        pltpu.SemaphoreType.DMA((n,))],
        name="gather_weights")(*shards)


def _swap_halves(dws, *, name):
    n = len(dws)

    def body(*refs):
        ins, outs = refs[:n], refs[n:2 * n]
        send_sems, recv_sems = refs[2 * n:]
        x, y, c = _mesh_pos()
        cps = []
        for w in range(n):
            h = dws[w].shape[1] // 2
            cp = pltpu.make_async_remote_copy(
                src_ref=ins[w].at[:, pl.ds((1 - c) * h, h), :], dst_ref=outs[w],
                send_sem=send_sems.at[w], recv_sem=recv_sems.at[w],
                device_id=(x, y, 1 - c), device_id_type=MESH)
            cp.start()
            cps.append(cp)
        for cp in cps:
            cp.wait()

    return pl.pallas_call(
        body, out_shape=[jax.ShapeDtypeStruct((s.shape[0], s.shape[1] // 2, s.shape[2]), s.dtype) for s in dws],
        in_specs=_hbm_specs(n), out_specs=_hbm_specs(n),
        scratch_shapes=[pltpu.SemaphoreType.DMA((n,)), pltpu.SemaphoreType.DMA((n,))],
        name=name)(*dws)


def _chip_exchange(parts):
    n = len(parts)

    def body(*refs):
        ins, outs = refs[:n], refs[n:2 * n]
        send_sems, recv_sems, local_sems = refs[2 * n:]
        x, y, c = _mesh_pos()
        me_chip = 2 * x + y
        chips = _other_chips(x, y)
        local = [pltpu.make_async_copy(ins[w].at[me_chip], outs[w].at[me_chip], local_sems.at[w]) for w in range(n)]
        for cp in local:
            cp.start()
        cps = []
        for w in range(n):
            for j, chip in enumerate(chips):
                cp = pltpu.make_async_remote_copy(
                    src_ref=ins[w].at[2 * chip[0] + chip[1]], dst_ref=outs[w].at[me_chip],
                    send_sem=send_sems.at[3 * w + j], recv_sem=recv_sems.at[3 * w + j],
                    device_id=(*chip, c), device_id_type=MESH)
                cp.start()
                cps.append((cp, w, j, chip))
        for cp, w, j, chip in cps:
            slot = outs[w].at[2 * chip[0] + chip[1]]
            pltpu.make_async_remote_copy(
                src_ref=slot, dst_ref=slot, send_sem=send_sems.at[3 * w + j], recv_sem=recv_sems.at[3 * w + j],
                device_id=(x, y, c), device_id_type=MESH).wait_recv()
        for cp, _, _, _ in cps:
            cp.wait_send()
        for cp in local:
            cp.wait()

    return pl.pallas_call(
        body, out_shape=[jax.ShapeDtypeStruct(s.shape, s.dtype) for s in parts],
        in_specs=_hbm_specs(n), out_specs=_hbm_specs(n),
        scratch_shapes=[pltpu.SemaphoreType.DMA((3 * n,)), pltpu.SemaphoreType.DMA((3 * n,)),
                        pltpu.SemaphoreType.DMA((n,))],
        name="chip_exchange")(*parts)


def _join_halves(halves):
    n = len(halves)

    def body(*refs):
        ins, outs = refs[:n], refs[n:2 * n]
        send_sems, recv_sems, local_sems = refs[2 * n:]
        x, y, c = _mesh_pos()
        cps, local = [], []
        for w in range(n):
            h = halves[w].shape[0]
            mine = outs[w].at[pl.ds(c * h, h), :]
            lc = pltpu.make_async_copy(ins[w], mine, local_sems.at[w])
            lc.start()
            local.append(lc)
            cp = pltpu.make_async_remote_copy(
                src_ref=ins[w], dst_ref=mine, send_sem=send_sems.at[w], recv_sem=recv_sems.at[w],
                device_id=(x, y, 1 - c), device_id_type=MESH)
            cp.start()
            cps.append(cp)
        for w in range(n):
            h = halves[w].shape[0]
            theirs = outs[w].at[pl.ds((1 - c) * h, h), :]
            pltpu.make_async_remote_copy(
                src_ref=theirs, dst_ref=theirs, send_sem=send_sems.at[w], recv_sem=recv_sems.at[w],
                device_id=(x, y, c), device_id_type=MESH).wait_recv()
        for cp in cps:
            cp.wait_send()
        for lc in local:
            lc.wait()

    return pl.pallas_call(
        body, out_shape=[jax.ShapeDtypeStruct((2 * s.shape[0], s.shape[1]), s.dtype) for s in halves],
        in_specs=_hbm_specs(n), out_specs=_hbm_specs(n),
        scratch_shapes=[pltpu.SemaphoreType.DMA((n,)), pltpu.SemaphoreType.DMA((n,)), pltpu.SemaphoreType.DMA((n,))],
        name="join_halves")(*halves)


def _cast_into_slot(w, chip, after, *, name):
    rows, cols = w.shape
    tr = _row_tile(rows, cols, 2)

    def body(chip_ref, w_ref, after_ref, o_ref):
        o_ref[...] = w_ref[...].astype(BF16)

    gs = pltpu.PrefetchScalarGridSpec(
        num_scalar_prefetch=1, grid=(rows // tr,),
        in_specs=[pl.BlockSpec((tr, cols), lambda i, chip_ref: (i, 0)), pl.BlockSpec(memory_space=pl.ANY)],
        out_specs=pl.BlockSpec((None, tr, cols), lambda i, chip_ref: (chip_ref[0], i, 0)))
    return pl.pallas_call(
        body, grid_spec=gs, out_shape=jax.ShapeDtypeStruct((N_CHIPS, rows, cols), BF16),
        compiler_params=_cparams(("parallel",)), name=name)(chip, w, after)


def _row_range(h, lo, hi, parts):
    step = h // parts
    assert step * parts == h and step % (2 * SUBLANES) == 0, (h, parts)
    return lo * step, (hi - lo) * step


def _gather_carry(items):
    n_copies = sum(len(js) for _, js, _, _, _ in items)
    sem = pltpu.SemaphoreType.DMA((2 * n_copies,))

    def copies(outs, sems):
        send_sems, recv_sems = sems
        x, y, c = _mesh_pos()
        me_chip = 2 * x + y
        chips = _other_chips(x, y)
        out_ici, in_ici, out_d2d, in_d2d = [], [], [], []
        k = 0
        for w, (buf, js, lo, hi, parts) in enumerate(items):
            h = buf.shape[1] // 2
            r0, nr = _row_range(h, lo, hi, parts)

            def copy(k, chip_idx, pc, to):
                ref = outs[w].at[chip_idx, pl.ds(pc * h + r0, nr), :]
                return pltpu.make_async_remote_copy(
                    src_ref=ref, dst_ref=ref, send_sem=send_sems.at[k], recv_sem=recv_sems.at[k],
                    device_id=to, device_id_type=MESH)

            for j in js:
                chip = chips[j]
                chip_idx = 2 * chip[0] + chip[1]
                out_ici.append(copy(k, me_chip, c, (*chip, c)))
                in_ici.append(copy(k, chip_idx, c, (x, y, c)))
                out_d2d.append(copy(k + 1, chip_idx, c, (x, y, 1 - c)))
                in_d2d.append(copy(k + 1, chip_idx, 1 - c, (x, y, c)))
                k += 2
        return out_ici, in_ici, out_d2d, in_d2d

    def start(ins, outs, sems):
        for cp in copies(outs, sems)[0]:
            cp.start()

    def finish(ins, outs, sems):
        out_ici, in_ici, out_d2d, in_d2d = copies(outs, sems)
        for arrived, onward in zip(in_ici, out_d2d):
            arrived.wait_recv()
            onward.start()
        for arrived in in_d2d:
            arrived.wait_recv()
        for cp in out_ici + out_d2d:
            cp.wait_send()

    bufs = [it[0] for it in items]
    shapes = [jax.ShapeDtypeStruct(b.shape, b.dtype) for b in bufs]
    return _Carry(bufs, shapes, {i: i for i in range(len(bufs))}, [sem, sem], start, finish)


def _exchange_carry(items):
    n = len(items)
    sem = pltpu.SemaphoreType.DMA((3 * n,))
    given = [w for w in range(n) if items[w][1] is not None]

    def copies(ins, outs, sems):
        send_sems, recv_sems = sems
        x, y, c = _mesh_pos()
        chips = _other_chips(x, y)
        sends, recvs = [], []
        for w, (part, _, lo, hi, parts) in enumerate(items):
            r0, nr = _row_range(part.shape[1], lo, hi, parts)
            for j, chip in enumerate(chips):
                land = outs[w].at[j, pl.ds(r0, nr), :]
                sends.append(pltpu.make_async_remote_copy(
                    src_ref=ins[w].at[2 * chip[0] + chip[1], pl.ds(r0, nr), :], dst_ref=land,
                    send_sem=send_sems.at[3 * w + j], recv_sem=recv_sems.at[3 * w + j],
                    device_id=(*chip, c), device_id_type=MESH))
                recvs.append(pltpu.make_async_remote_copy(
                    src_ref=land, dst_ref=land,
                    send_sem=send_sems.at[3 * w + j], recv_sem=recv_sems.at[3 * w + j],
                    device_id=(x, y, c), device_id_type=MESH))
        return sends, recvs

    def start(ins, outs, sems):
        for cp in copies(ins, outs, sems)[0]:
            cp.start()

    def finish(ins, outs, sems):
        sends, recvs = copies(ins, outs, sems)
        for cp in recvs:
            cp.wait_recv()
        for cp in sends:
            cp.wait_send()

    inputs = [it[0] for it in items] + [items[w][1] for w in given]
    shapes = [jax.ShapeDtypeStruct((3,) + it[0].shape[1:], it[0].dtype) for it in items]
    aliases = {n + i: w for i, w in enumerate(given)}
    return _Carry(inputs, shapes, aliases, [sem, sem], start, finish)


def _sum_into_half(part, landed, chip, my_c, *, name):
    _, h, cols = part.shape
    tr = _row_tile(h, cols, 5)
    hb = h // tr

    def body(chip_ref, c_ref, p_ref, l_ref, o_ref):
        acc = p_ref[...].astype(F32)
        for j in range(3):
            acc = acc + l_ref[j].astype(F32)
        o_ref[...] = acc

    gs = pltpu.PrefetchScalarGridSpec(
        num_scalar_prefetch=2, grid=(hb,),
        in_specs=[pl.BlockSpec((None, tr, cols), lambda i, chip_ref, c_ref: (chip_ref[0], i, 0)),
                  pl.BlockSpec((3, tr, cols), lambda i, chip_ref, c_ref: (0, i, 0))],
        out_specs=pl.BlockSpec((tr, cols), lambda i, chip_ref, c_ref: (c_ref[0] * hb + i, 0)))
    return pl.pallas_call(
        body, grid_spec=gs, out_shape=jax.ShapeDtypeStruct((2 * h, cols), F32),
        compiler_params=_cparams(("parallel",)), name=name)(chip, my_c, part, landed)


def _join_carry(fulls):
    n = len(fulls)
    sem = pltpu.SemaphoreType.DMA((n,))

    def copies(outs, sems):
        send_sems, recv_sems = sems
        x, y, c = _mesh_pos()
        sends, recvs = [], []
        for w in range(n):
            h = fulls[w].shape[0] // 2
            mine = outs[w].at[pl.ds(c * h, h), :]
            theirs = outs[w].at[pl.ds((1 - c) * h, h), :]
            sends.append(pltpu.make_async_remote_copy(
                src_ref=mine, dst_ref=mine, send_sem=send_sems.at[w], recv_sem=recv_sems.at[w],
                device_id=(x, y, 1 - c), device_id_type=MESH))
            recvs.append(pltpu.make_async_remote_copy(
                src_ref=theirs, dst_ref=theirs, send_sem=send_sems.at[w], recv_sem=recv_sems.at[w],
                device_id=(x, y, c), device_id_type=MESH))
        return sends, recvs

    def start(ins, outs, sems):
        for cp in copies(outs, sems)[0]:
            cp.start()

    def finish(ins, outs, sems):
        sends, recvs = copies(outs, sems)
        for cp in recvs:
            cp.wait_recv()
        for cp in sends:
            cp.wait_send()

    shapes = [jax.ShapeDtypeStruct(f.shape, f.dtype) for f in fulls]
    return _Carry(fulls, shapes, {i: i for i in range(n)}, [sem, sem], start, finish)


class _NoComm:
    def __init__(self, big):
        self.big = big
        self.grads = {}

    def weight(self, name):
        return self.big[name]

    def mm_in(self, u, afters):
        return _mm(u, self.big["w_in"], mode="nn", out_dtype=BF16, name="mm_in")

    def mm_d_in(self, dproj):
        return _mm(dproj, self.big["w_in"], mode="nt", out_dtype=F32, name="mm_d_in")

    def carry(self, site, args=()):
        return None

    def done(self, site, carried, out=None):
        return out

    def grad(self, name, dw):
        self.grads[name] = dw

    def early_grads(self, early):
        self.early = early


def _gather_rows_carry(blk):
    m_per = blk.shape[0]
    sem = pltpu.SemaphoreType.DMA((7,))

    def copies(ins, outs, sems):
        send_sems, recv_sems, local_sem = sems
        x, y, c = _mesh_pos()
        me, sibling = (x, y, c), (x, y, 1 - c)
        chips = _other_chips(x, y)

        def rows(px, py, pc):
            return outs[0].at[pl.ds((4 * px + 2 * py + pc) * m_per, m_per), :]

        def copy(k, block, to, src=None):
            return pltpu.make_async_remote_copy(
                src_ref=rows(*block) if src is None else src, dst_ref=rows(*block),
                send_sem=send_sems.at[k], recv_sem=recv_sems.at[k], device_id=to, device_id_type=MESH)

        mine = pltpu.make_async_copy(ins[0], rows(*me), local_sem.at[0])
        first = [copy(0, me, sibling, src=ins[0])]
        first += [copy(1 + j, me, (*chip, c), src=ins[0]) for j, chip in enumerate(chips)]
        passed = [copy(4 + j, (*chip, c), sibling) for j, chip in enumerate(chips)]
        landed = [copy(1 + j, (*chip, c), me) for j, chip in enumerate(chips)]
        from_sibling = [copy(0, sibling, me)] + [copy(4 + j, (*chip, 1 - c), me) for j, chip in enumerate(chips)]
        return mine, first, passed, landed, from_sibling

    def start(ins, outs, sems):
        mine, first, _, _, _ = copies(ins, outs, sems)
        mine.start()
        for cp in first:
            cp.start()

    def finish(ins, outs, sems):
        mine, first, passed, landed, from_sibling = copies(ins, outs, sems)
        for arrived, onward in zip(landed, passed):
            arrived.wait_recv()
            onward.start()
        for arrived in from_sibling:
            arrived.wait_recv()
        for cp in first + passed:
            cp.wait_send()
        mine.wait()

    shape = jax.ShapeDtypeStruct((N_DEV * m_per, blk.shape[1]), blk.dtype)
    return _Carry([blk], [shape], {}, [sem, sem, pltpu.SemaphoreType.DMA((1,))], start, finish)


def _gather_fresh_carry(own, js):
    n = len(js)
    h = own.shape[0] // 2
    sem = pltpu.SemaphoreType.DMA((2 * n,))

    def copies(ins, outs, sems):
        send_sems, recv_sems = sems
        x, y, c = _mesh_pos()
        chips = _other_chips(x, y)
        out_ici, in_ici, out_d2d, in_d2d = [], [], [], []

        def copy(k, src, dst, to):
            return pltpu.make_async_remote_copy(
                src_ref=src, dst_ref=dst, send_sem=send_sems.at[k], recv_sem=recv_sems.at[k],
                device_id=to, device_id_type=MESH)

        for jj, j in enumerate(js):
            mine = ins[0].at[pl.ds(c * h, h), :]
            land = outs[0].at[jj, pl.ds(c * h, h), :]
            other = outs[0].at[jj, pl.ds((1 - c) * h, h), :]
            out_ici.append(copy(2 * jj, mine, land, (*chips[j], c)))
            in_ici.append(copy(2 * jj, land, land, (x, y, c)))
            out_d2d.append(copy(2 * jj + 1, land, land, (x, y, 1 - c)))
            in_d2d.append(copy(2 * jj + 1, other, other, (x, y, c)))
        return out_ici, in_ici, out_d2d, in_d2d

    def start(ins, outs, sems):
        for cp in copies(ins, outs, sems)[0]:
            cp.start()

    def finish(ins, outs, sems):
        out_ici, in_ici, out_d2d, in_d2d = copies(ins, outs, sems)
        for arrived, onward in zip(in_ici, out_d2d):
            arrived.wait_recv()
            onward.start()
        for arrived in in_d2d:
            arrived.wait_recv()
        for cp in out_ici + out_d2d:
            cp.wait_send()

    return _Carry([own], [jax.ShapeDtypeStruct((n,) + own.shape, own.dtype)], {}, [sem, sem], start, finish)


def _w_in_copies(own_ref, land_ref, send_sems, recv_sems):
    x, y, c = _mesh_pos()
    h = own_ref.shape[0] // 2
    return [pltpu.make_async_remote_copy(
        src_ref=own_ref.at[pl.ds(c * h, h), :], dst_ref=land_ref.at[j, pl.ds(c * h, h), :],
        send_sem=send_sems[j], recv_sem=recv_sems[j], device_id=(*chip, c), device_id_type=MESH)
        for j, chip in enumerate(_other_chips(x, y))]


def _w_in_send(own, after):
    hbm = pl.BlockSpec(memory_space=pltpu.HBM)
    sem = pl.BlockSpec(memory_space=pltpu.SEMAPHORE)
    land_shape = (3,) + own.shape

    def body(own_ref, land_ref, after_ref, s0, s1, s2, r0, r1, r2, own_thru, land_thru, token):
        for cp in _w_in_copies(own_ref, land_ref, (s0, s1, s2), (r0, r1, r2)):
            cp.start()
        token[...] = jnp.zeros_like(token)

    outs = pl.pallas_call(
        body, name="w_in_send",
        out_shape=(pltpu.SemaphoreType.DMA(()),) * 6 + (
            pltpu.HBM(own.shape, own.dtype), pltpu.HBM(land_shape, own.dtype), jax.ShapeDtypeStruct((8, LANES), F32)),
        in_specs=(hbm, hbm, pl.BlockSpec(memory_space=pl.ANY)),
        out_specs=(sem,) * 6 + (hbm, hbm, pl.BlockSpec(memory_space=pltpu.VMEM)),
        input_output_aliases={0: 6, 1: 7},
        compiler_params=pltpu.CompilerParams(has_side_effects=pltpu.SideEffectType.DATAFLOW_SIDE_EFFECTING),
    )(pltpu.with_memory_space_constraint(own, pltpu.HBM),
      pltpu.with_memory_space_constraint(lax.empty(land_shape, own.dtype), pltpu.HBM), after)
    return outs[:6], outs[6], outs[7], outs[8]


def _w_in_wait(sems, own, land, afters):
    hbm = pl.BlockSpec(memory_space=pltpu.HBM)
    sem = pl.BlockSpec(memory_space=pltpu.SEMAPHORE)
    n_after = len(afters)

    def body(own_ref, land_ref, s0, s1, s2, r0, r1, r2, *rest):
        for cp in _w_in_copies(own_ref, land_ref, (s0, s1, s2), (r0, r1, r2)):
            cp.wait_send()
            cp.wait_recv()

    return pl.pallas_call(
        body, name="w_in_wait", out_shape=(pltpu.HBM(own.shape, own.dtype), pltpu.HBM(land.shape, land.dtype)),
        in_specs=(hbm, hbm) + (sem,) * 6 + (pl.BlockSpec(memory_space=pl.ANY),) * n_after, out_specs=(hbm, hbm),
        input_output_aliases={0: 0, 1: 1},
        compiler_params=pltpu.CompilerParams(has_side_effects=pltpu.SideEffectType.DATAFLOW_SIDE_EFFECTING),
    )(own, land, *sems, *afters)


def _exchange_copies(part_refs, land_refs, send_sems, recv_sems):
    x, y, c = _mesh_pos()
    cps = []
    for w, (part, land) in enumerate(zip(part_refs, land_refs)):
        for j, chip in enumerate(_other_chips(x, y)):
            cps.append(pltpu.make_async_remote_copy(
                src_ref=part.at[2 * chip[0] + chip[1]], dst_ref=land.at[j],
                send_sem=send_sems[3 * w + j], recv_sem=recv_sems[3 * w + j],
                device_id=(*chip, c), device_id_type=MESH))
    return cps


def _exchange_send(parts, through, *, name):
    n = len(parts)
    hbm = pl.BlockSpec(memory_space=pltpu.HBM)
    sem = pl.BlockSpec(memory_space=pltpu.SEMAPHORE)
    any_spec = pl.BlockSpec(memory_space=pl.ANY)
    land_shapes = [(3,) + p.shape[1:] for p in parts]

    def body(*refs):
        part_refs, land_refs = refs[:n], refs[n:2 * n]
        sems = refs[2 * n + 1:8 * n + 1]
        for cp in _exchange_copies(part_refs, land_refs, sems[:3 * n], sems[3 * n:]):
            cp.start()

    outs = pl.pallas_call(
        body, name=name,
        out_shape=(pltpu.SemaphoreType.DMA(()),) * (6 * n)
        + tuple(pltpu.HBM(p.shape, p.dtype) for p in parts)
        + tuple(pltpu.HBM(s, p.dtype) for s, p in zip(land_shapes, parts))
        + (jax.ShapeDtypeStruct(through.shape, through.dtype),),
        in_specs=(hbm,) * (2 * n) + (any_spec,), out_specs=(sem,) * (6 * n) + (hbm,) * (2 * n) + (any_spec,),
        input_output_aliases={i: 6 * n + i for i in range(2 * n + 1)},
        compiler_params=pltpu.CompilerParams(has_side_effects=pltpu.SideEffectType.DATAFLOW_SIDE_EFFECTING),
    )(*[pltpu.with_memory_space_constraint(p, pltpu.HBM) for p in parts],
      *[pltpu.with_memory_space_constraint(lax.empty(s, p.dtype), pltpu.HBM) for s, p in zip(land_shapes, parts)],
      through)
    return outs[:6 * n], outs[6 * n:7 * n], outs[7 * n:8 * n], outs[8 * n]


def _exchange_wait(sems, parts, lands, afters, *, name):
    n = len(parts)
    hbm = pl.BlockSpec(memory_space=pltpu.HBM)
    sem = pl.BlockSpec(memory_space=pltpu.SEMAPHORE)

    def body(*refs):
        part_refs, land_refs = refs[:n], refs[n:2 * n]
        sem_refs = refs[2 * n:8 * n]
        for cp in _exchange_copies(part_refs, land_refs, sem_refs[:3 * n], sem_refs[3 * n:]):
            cp.wait_send()
            cp.wait_recv()

    outs = pl.pallas_call(
        body, name=name,
        out_shape=tuple(pltpu.HBM(p.shape, p.dtype) for p in parts) + tuple(pltpu.HBM(l.shape, l.dtype) for l in lands),
        in_specs=(hbm,) * (2 * n) + (sem,) * (6 * n) + (pl.BlockSpec(memory_space=pl.ANY),) * len(afters),
        out_specs=(hbm,) * (2 * n), input_output_aliases={i: i for i in range(2 * n)},
        compiler_params=pltpu.CompilerParams(has_side_effects=pltpu.SideEffectType.DATAFLOW_SIDE_EFFECTING),
    )(*parts, *lands, *sems, *afters)
    return outs[:n], outs[n:]


def _join_copies(full_refs, send_sems, recv_sems):
    x, y, c = _mesh_pos()
    cps = []
    for w, full in enumerate(full_refs):
        h = full.shape[0] // 2
        mine = full.at[pl.ds(c * h, h), :]
        cps.append(pltpu.make_async_remote_copy(
            src_ref=mine, dst_ref=mine, send_sem=send_sems[w], recv_sem=recv_sems[w],
            device_id=(x, y, 1 - c), device_id_type=MESH))
    return cps


def _join_send(fulls, *, name):
    n = len(fulls)
    hbm = pl.BlockSpec(memory_space=pltpu.HBM)
    sem = pl.BlockSpec(memory_space=pltpu.SEMAPHORE)

    def body(*refs):
        sems = refs[n:3 * n]
        for cp in _join_copies(refs[:n], sems[:n], sems[n:]):
            cp.start()
        token = refs[-1]
        token[...] = jnp.zeros_like(token)

    outs = pl.pallas_call(
        body, name=name,
        out_shape=(pltpu.SemaphoreType.DMA(()),) * (2 * n) + tuple(pltpu.HBM(f.shape, f.dtype) for f in fulls)
        + (jax.ShapeDtypeStruct((SUBLANES, LANES), F32),),
        in_specs=(hbm,) * n,
        out_specs=(sem,) * (2 * n) + (hbm,) * n + (pl.BlockSpec(memory_space=pltpu.VMEM),),
        input_output_aliases={i: 2 * n + i for i in range(n)},
        compiler_params=pltpu.CompilerParams(has_side_effects=pltpu.SideEffectType.DATAFLOW_SIDE_EFFECTING),
    )(*[pltpu.with_memory_space_constraint(f, pltpu.HBM) for f in fulls])
    return outs[:2 * n], list(outs[2 * n:3 * n]), outs[3 * n]


def _join_wait(sems, fulls, afters, *, name):
    n = len(fulls)
    hbm = pl.BlockSpec(memory_space=pltpu.HBM)
    sem = pl.BlockSpec(memory_space=pltpu.SEMAPHORE)

    def body(*refs):
        sem_refs = refs[n:3 * n]
        for cp in _join_copies(refs[:n], sem_refs[:n], sem_refs[n:]):
            cp.wait_send()
            cp.wait_recv()

    outs = pl.pallas_call(
        body, name=name, out_shape=tuple(pltpu.HBM(f.shape, f.dtype) for f in fulls),
        in_specs=(hbm,) * n + (sem,) * (2 * n) + (pl.BlockSpec(memory_space=pl.ANY),) * len(afters),
        out_specs=(hbm,) * n, input_output_aliases={i: i for i in range(n)},
        compiler_params=pltpu.CompilerParams(has_side_effects=pltpu.SideEffectType.DATAFLOW_SIDE_EFFECTING),
    )(*fulls, *sems, *afters)
    return list(outs)


def _gather_ici_copies(buf_refs, send_sems, recv_sems):
    x, y, c = _mesh_pos()
    me_chip = 2 * x + y
    cps = []
    for w, buf in enumerate(buf_refs):
        h = buf.shape[1] // 2
        ref = buf.at[me_chip, pl.ds(c * h, h), :]
        for j, chip in enumerate(_other_chips(x, y)):
            cps.append(pltpu.make_async_remote_copy(
                src_ref=ref, dst_ref=ref, send_sem=send_sems[3 * w + j], recv_sem=recv_sems[3 * w + j],
                device_id=(*chip, c), device_id_type=MESH))
    return cps


def _gather_send(bufs, *, name):
    n = len(bufs)
    hbm = pl.BlockSpec(memory_space=pltpu.HBM)
    sem = pl.BlockSpec(memory_space=pltpu.SEMAPHORE)

    def body(*refs):
        sems = refs[n:7 * n]
        for cp in _gather_ici_copies(refs[:n], sems[:3 * n], sems[3 * n:]):
            cp.start()

    outs = pl.pallas_call(
        body, name=name,
        out_shape=(pltpu.SemaphoreType.DMA(()),) * (6 * n) + tuple(pltpu.HBM(b.shape, b.dtype) for b in bufs),
        in_specs=(hbm,) * n, out_specs=(sem,) * (6 * n) + (hbm,) * n,
        input_output_aliases={i: 6 * n + i for i in range(n)},
        compiler_params=pltpu.CompilerParams(has_side_effects=pltpu.SideEffectType.DATAFLOW_SIDE_EFFECTING),
    )(*[pltpu.with_memory_space_constraint(b, pltpu.HBM) for b in bufs])
    send_sems, recv_sems = outs[:3 * n], outs[3 * n:6 * n]
    per_buf = [tuple(send_sems[3 * w:3 * w + 3]) + tuple(recv_sems[3 * w:3 * w + 3]) for w in range(n)]
    return per_buf, list(outs[6 * n:])


def _gather_wait(sems, bufs, afters, *, name):
    n = len(bufs)
    hbm = pl.BlockSpec(memory_space=pltpu.HBM)
    sem = pl.BlockSpec(memory_space=pltpu.SEMAPHORE)
    flat = [s for six in sems for s in six[:3]] + [s for six in sems for s in six[3:]]

    def body(*refs):
        sem_refs = refs[n:7 * n]
        for cp in _gather_ici_copies(refs[:n], sem_refs[:3 * n], sem_refs[3 * n:]):
            cp.wait_send()
            cp.wait_recv()

    outs = pl.pallas_call(
        body, name=name, out_shape=tuple(pltpu.HBM(b.shape, b.dtype) for b in bufs),
        in_specs=(hbm,) * n + (sem,) * (6 * n) + (pl.BlockSpec(memory_space=pl.ANY),) * len(afters),
        out_specs=(hbm,) * n, input_output_aliases={i: i for i in range(n)},
        compiler_params=pltpu.CompilerParams(has_side_effects=pltpu.SideEffectType.DATAFLOW_SIDE_EFFECTING),
    )(*bufs, *flat, *afters)
    return list(outs)


def _forward_abs_carry(bufs):
    n = len(bufs)
    sem = pltpu.SemaphoreType.DMA((3 * n,))

    def copies(outs, sems):
        send_sems, recv_sems = sems
        x, y, c = _mesh_pos()
        sends, recvs = [], []
        for w in range(n):
            h = bufs[w].shape[1] // 2
            for j, chip in enumerate(_other_chips(x, y)):
                slot = 2 * chip[0] + chip[1]
                mine = outs[w].at[slot, pl.ds(c * h, h), :]
                other = outs[w].at[slot, pl.ds((1 - c) * h, h), :]
                sends.append(pltpu.make_async_remote_copy(
                    src_ref=mine, dst_ref=mine, send_sem=send_sems.at[3 * w + j], recv_sem=recv_sems.at[3 * w + j],
                    device_id=(x, y, 1 - c), device_id_type=MESH))
                recvs.append(pltpu.make_async_remote_copy(
                    src_ref=other, dst_ref=other, send_sem=send_sems.at[3 * w + j], recv_sem=recv_sems.at[3 * w + j],
                    device_id=(x, y, c), device_id_type=MESH))
        return sends, recvs

    def start(ins, outs, sems):
        for cp in copies(outs, sems)[0]:
            cp.start()

    def finish(ins, outs, sems):
        sends, recvs = copies(outs, sems)
        for cp in recvs:
            cp.wait_recv()
        for cp in sends:
            cp.wait_send()

    shapes = [jax.ShapeDtypeStruct(b.shape, b.dtype) for b in bufs]
    return _Carry(bufs, shapes, {i: i for i in range(n)}, [sem, sem], start, finish)


def _forward_carry(land):
    n = land.shape[0]
    h = land.shape[1] // 2
    sem = pltpu.SemaphoreType.DMA((n,))

    def copies(outs, sems):
        send_sems, recv_sems = sems
        x, y, c = _mesh_pos()
        sends, recvs = [], []
        for j in range(n):
            mine = outs[0].at[j, pl.ds(c * h, h), :]
            other = outs[0].at[j, pl.ds((1 - c) * h, h), :]
            sends.append(pltpu.make_async_remote_copy(
                src_ref=mine, dst_ref=mine, send_sem=send_sems.at[j], recv_sem=recv_sems.at[j],
                device_id=(x, y, 1 - c), device_id_type=MESH))
            recvs.append(pltpu.make_async_remote_copy(
                src_ref=other, dst_ref=other, send_sem=send_sems.at[j], recv_sem=recv_sems.at[j],
                device_id=(x, y, c), device_id_type=MESH))
        return sends, recvs

    def start(ins, outs, sems):
        for cp in copies(outs, sems)[0]:
            cp.start()

    def finish(ins, outs, sems):
        sends, recvs = copies(outs, sems)
        for cp in recvs:
            cp.wait_recv()
        for cp in sends:
            cp.wait_send()

    return _Carry([land], [jax.ShapeDtypeStruct(land.shape, land.dtype)], {0: 0}, [sem, sem], start, finish)


def _swap_carry(dws):
    n = len(dws)
    sem = pltpu.SemaphoreType.DMA((n,))

    def copies(ins, outs, sems):
        send_sems, recv_sems = sems
        x, y, c = _mesh_pos()
        cps = []
        for w in range(n):
            h = dws[w].shape[1] // 2
            cps.append(pltpu.make_async_remote_copy(
                src_ref=ins[w].at[:, pl.ds((1 - c) * h, h), :], dst_ref=outs[w],
                send_sem=send_sems.at[w], recv_sem=recv_sems.at[w],
                device_id=(x, y, 1 - c), device_id_type=MESH))
        return cps

    def start(ins, outs, sems):
        for cp in copies(ins, outs, sems):
            cp.start()

    def finish(ins, outs, sems):
        for cp in copies(ins, outs, sems):
            cp.wait()

    shapes = [jax.ShapeDtypeStruct((s.shape[0], s.shape[1] // 2, s.shape[2]), s.dtype) for s in dws]
    return _Carry(dws, shapes, {}, [sem, sem], start, finish)


def _merge_carries(carries):
    if len(carries) == 1:
        return carries[0]
    inputs, out_shapes, sem_shapes, aliases, spans = [], [], [], {}, []
    for cy in carries:
        i0, o0, s0 = len(inputs), len(out_shapes), len(sem_shapes)
        aliases.update({i0 + i: o0 + o for i, o in cy.aliases.items()})
        inputs += cy.inputs
        out_shapes += cy.out_shapes
        sem_shapes += cy.sem_shapes
        spans.append((slice(i0, len(inputs)), slice(o0, len(out_shapes)), slice(s0, len(sem_shapes))))

    def start(ins, outs, sems):
        for cy, (si, so, ss) in zip(carries, spans):
            cy.start(ins[si], outs[so], sems[ss])

    def finish(ins, outs, sems):
        for cy, (si, so, ss) in zip(carries, spans):
            cy.finish(ins[si], outs[so], sems[ss])

    return _Carry(inputs, out_shapes, aliases, sem_shapes, start, finish)


ALL_CHIPS = (0, 1, 2)


class _MeshComm:
    GATHER_AT = {}
    FORWARD_AT = {
        "conv_fwd": ["w_conv_out", "w_glu_a", "w_glu_b"],
        "s5_fwd": ["w_out"],
        "mm_out": ["w_ff1"],
        "mm_ff1": ["w_ff2"],
    }
    SWAP_AT = {
        "mm_d_ff2": ["w_ff2"],
        "mm_d_ff1": ["w_ff1"],
        "conv_bwd": ["w_out", "w_glu_a", "w_glu_b", "w_conv_out"],
    }
    EXCHANGE_AT = {}
    EARLY_AT = "mm_dw_in"

    def __init__(self, shards, pos, chip, my_c):
        self.pos = pos
        self.chip = chip
        self.my_c = my_c
        self.shards = shards
        self.w_in_own = _cast_bf16(shards["w_in"], name="cast_w_in")
        self.raw = {}
        self.flights = []
        self.parts = {}
        self.landing = {}
        self.halves = {}
        self.pending = {}
        self.last_site = {}
        for site, items in self.EXCHANGE_AT.items():
            for it in items:
                self.last_site[it[0]] = site

    def weight(self, name):
        g = self.bufs[name]
        return g.reshape(g.shape[0] * g.shape[1], g.shape[2]) if name in ROW_SHARDED else g

    def _slot_ids(self):
        x, y, _ = self.pos
        ids = [2 * x + y] + [2 * cx + cy for cx, cy in _other_chips(x, y)]
        return jnp.stack(ids).astype(jnp.int32)

    def start_w_in(self, after):
        *self.w_in_flight, token = _w_in_send(self.w_in_own, after)
        order = [n for names in self.FORWARD_AT.values() for n in names]
        casts = [_cast_into_slot(self.shards[n], self.chip, token, name="cast_" + n) for n in order]
        sems, bufs = _gather_send(casts, name="gather_send")
        self.bufs = dict(zip(order, bufs))
        self.gather_sems = dict(zip(order, sems))
        return token

    def mm_in(self, u, afters):
        ids = self._slot_ids()
        sems, own, land = self.w_in_flight
        proj = _mm_slots(u, own[None], ids[0:1], None, name="mm_in_own")
        own, land = _w_in_wait(sems, own, land, [proj] + list(self.bufs.values()) + list(afters))
        land, = _run_carry(_forward_carry(land), name="forward_w_in")
        proj = _mm_slots(u, land, ids[1:4], proj, name="mm_in_rest")
        self.w_in_rel = jnp.concatenate([own[None], land], axis=0)
        return proj

    def _add_and_send(self, names, landed, site, through):
        parts = [_add_half(self.raw.pop(n), l1, self.my_c, name="add_half_" + n) for n, l1 in zip(names, landed)]
        sems, parts, lands, through = _exchange_send(parts, through, name="exchange_send_" + site)
        self.flights.append((names, sems, parts, lands))
        return through

    def mm_d_in(self, dproj):
        landed = _run_carry(_swap_carry([self.raw["w_in"]]), name="swap_halves_w_in")
        dproj = self._add_and_send(["w_in"], landed, "w_in", dproj)
        return _mm(dproj, self.w_in_rel, mode="nt", out_dtype=F32, name="mm_d_in", a_slots=self._slot_ids())

    def early_grads(self, early):
        self.early = early

    def carry(self, site, args=()):
        jobs = []
        if site in self.FORWARD_AT:
            names = self.FORWARD_AT[site]
            landed = _gather_wait([self.gather_sems.pop(n) for n in names], [self.bufs[n] for n in names],
                                  [args[0]], name="gather_wait_" + site)
            jobs.append(("gather", [(n,) for n in names], _forward_abs_carry(landed)))
        if site == self.EARLY_AT:
            flat, self.early_offs = _pack(list(self.early.values()))
            jobs.append(("early", None, _gather_rows_carry(flat.reshape(-1, PACK_COLS))))
        if site in self.GATHER_AT:
            items = self.GATHER_AT[site]
            jobs.append(("gather", items, _gather_carry([(self.bufs[it[0]],) + tuple(it[1:]) for it in items])))
        if site in self.EXCHANGE_AT:
            items = self.EXCHANGE_AT[site]
            jobs.append(("exchange", items, _exchange_carry(
                [(self.parts[it[0]], self.landing.get(it[0])) + tuple(it[1:]) for it in items])))
        if site in self.SWAP_AT:
            names = self.SWAP_AT[site]
            jobs.append(("swap", names, _swap_carry([self.raw[n] for n in names])))
        if not jobs:
            return None
        self.pending[site] = jobs
        return _merge_carries([job[2] for job in jobs])

    def done(self, site, carried, out=None):
        pos = 0
        for kind, items, carry in self.pending.pop(site):
            outs = carried[pos:pos + len(carry.out_shapes)]
            pos += len(carry.out_shapes)
            if kind == "early":
                self.early_all = outs[0]
            elif kind == "gather":
                self.bufs.update(zip([it[0] for it in items], outs))
            elif kind == "swap":
                if isinstance(out, (list, tuple)):
                    out = [self._add_and_send(items, outs, site, out[0])] + list(out[1:])
                else:
                    out = self._add_and_send(items, outs, site, out)
            else:
                for it, landed in zip(items, outs):
                    n = it[0]
                    self.landing[n] = landed
                    if self.last_site[n] == site:
                        self.halves[n] = _sum_into_half(self.parts.pop(n), self.landing.pop(n), self.chip,
                                                        self.my_c, name="sum_chips_" + n)
        return out

    def grad(self, name, dw):
        if name in ROW_SHARDED:
            dw = dw.reshape(N_CHIPS, dw.shape[0] // N_CHIPS, dw.shape[1])
        self.raw[name] = dw

    def join_start(self, names, afters):
        for i, (group, sems, parts, lands) in enumerate(self.flights):
            parts, lands = _exchange_wait(sems, parts, lands, afters, name="exchange_wait_%d" % i)
            for n, part, land in zip(group, parts, lands):
                self.halves[n] = _sum_into_half(part, land, self.chip, self.my_c, name="sum_chips_" + n)
        self.flights = []
        sems, fulls, token = _join_send([self.halves.pop(n) for n in names], name="join_send")
        self.join_flight = (names, sems, fulls)
        return token

    def join_finish(self, afters):
        names, sems, fulls = self.join_flight
        return dict(zip(names, _join_wait(sems, fulls, afters, name="join_wait")))


def _local_step(x, target, mod, small, comm):
    rows, d = x.shape
    cw = d // 2
    shift1, scale1, gate1, shift2, scale2, gate2 = mod
    _, _, bbr, bbi = small["s5_disc"]
    b_in, c_out, b_out, c_in, mults = _s5_operands(*small["s5_loglam"], bbr, bbi, small["c_re"], small["c_im"])
    wt = comm.weight

    def riding(site, fn, *args, **kwargs):
        carry = comm.carry(site, args)
        if carry is None:
            return fn(*args, **kwargs)
        out, carried = fn(*args, carry=carry, **kwargs)
        return comm.done(site, carried, out)

    u = _norm_mod(x, small["norm1_g"], scale1, shift1, name="norm1_fwd")
    proj = comm.mm_in(u, [*b_in, *c_out, *b_out, *c_in, *mults])
    sl, cv = riding("conv_fwd", _conv_fwd, proj, small["w_dw"], small["b_dw"], small["ln_g"], small["ln_b"], cw=cw)
    y_conv = _mm(sl, wt("w_conv_out"), mode="nn", out_dtype=BF16, name="mm_conv_out")
    yg, st_re, st_im = riding("s5_fwd", _s5_fwd, proj, small["d_skip"], b_in, c_out, mults, col0=2 * cw // LANES)
    ya = riding("mm_glu_a", _mm, yg, wt("w_glu_a"), mode="nn", out_dtype=BF16, name="mm_glu_a")
    yb = riding("mm_glu_b", _mm, yg, wt("w_glu_b"), mode="nn", out_dtype=BF16, name="mm_glu_b")
    merged = _merge_fwd(proj, y_conv, ya, yb, cw=cw)
    mo = riding("mm_out", _mm, merged, wt("w_out"), mode="nn", out_dtype=BF16, name="mm_out")
    h1, z = _res_norm(x, mo, gate1, small["norm2_g"], scale2, shift2)
    f1 = riding("mm_ff1", _mm, z, wt("w_ff1"), mode="nn", out_dtype=BF16, name="mm_ff1")
    ff = _mm(f1, wt("w_ff2"), mode="nn", out_dtype=BF16, name="mm_ff2", a_fn=_relu2_bf16)
    dh2, dff, loss, d_final_g, d_gate2 = _final_fwd_bwd(h1, ff, gate2, small["final_g"], target)

    comm.grad("w_ff2", _mm(f1, dff, mode="tn", out_dtype=BF16, name="mm_dw_ff2", a_fn=_relu2_bf16))
    df1 = riding("mm_d_ff2", _mm, dff, wt("w_ff2"), mode="nt", out_dtype=BF16, name="mm_d_ff2", extra=f1,
                 epi=lambda acc, f: acc * (2.0 * jnp.maximum(f.astype(F32), 0.0)))
    comm.grad("w_ff1", riding("mm_dw_ff1", _mm, z, df1, mode="tn", out_dtype=BF16, name="mm_dw_ff1",
                              out_gathered=True))
    dz = riding("mm_d_ff1", _mm, df1, wt("w_ff1"), mode="nt", out_dtype=F32, name="mm_d_ff1")
    dh1, d_shift2, d_scale2, d_norm2_g, dmo, d_gate1 = riding(
        "norm2_bwd", _norm_mod_bwd, dz, h1, dh2, small["norm2_g"], scale2, gate1, mo, name="norm2_bwd")
    comm.grad("w_out", riding("mm_dw_out", _mm, merged, dmo, mode="tn", out_dtype=BF16, name="mm_dw_out"))
    dmerged = riding("mm_d_out", _mm, dmo, wt("w_out"), mode="nt", out_dtype=BF16, name="mm_d_out")
    dproj, dy_conv, dya, dyb = riding("merge_bwd", _merge_bwd, dmerged, proj, y_conv, ya, yb, cw=cw)
    comm.grad("w_glu_a", _mm(yg, dya, mode="tn", out_dtype=BF16, name="mm_dw_glu_a", out_gathered=True))
    comm.grad("w_glu_b", _mm(yg, dyb, mode="tn", out_dtype=BF16, name="mm_dw_glu_b", out_gathered=True))
    dyg_a = _mm(dya, wt("w_glu_a"), mode="nt", out_dtype=F32, name="mm_d_glu_a")
    dyg = _mm(dyb, wt("w_glu_b"), mode="nt", out_dtype=F32, name="mm_d_glu_b", extra=dyg_a,
              epi=lambda acc, e: acc + e)
    comm.grad("w_conv_out", _mm(sl, dy_conv, mode="tn", out_dtype=BF16, name="mm_dw_conv_out", out_gathered=True))
    dsl = _mm(dy_conv, wt("w_conv_out"), mode="nt", out_dtype=F32, name="mm_d_conv_out")
    dcv, d_ln_g, d_ln_b = _ln_bwd(dsl, cv, small["ln_g"], small["ln_b"])
    dproj, d_w_dw, d_b_dw = riding("conv_bwd", _conv_bwd, dcv, proj, small["w_dw"], dproj, cw=cw)
    dproj, d_d_skip, dbr, dbi, dcr, dci, dlr, dli = riding(
        "s5_bwd", _s5_bwd, proj, dyg, small["d_skip"], (st_re, st_im), c_out, b_out, c_in, mults, dproj,
        col0=2 * cw // LANES)
    sw = lambda m: jnp.swapaxes(m, 1, 2)
    early = {
        "dmod_tail": jnp.concatenate([d_gate1, d_shift2, d_scale2, d_gate2], axis=1), "loss": loss[:, 0:1],
        "w_dw": d_w_dw, "b_dw": d_b_dw, "ln_g": d_ln_g, "ln_b": d_ln_b,
        "lam_re": dlr.reshape(-1, SSM_STATE), "lam_im": dli.reshape(-1, SSM_STATE),
        "bb_re": sw(_block_diag_extract(dbr, SSM_GROUP, SSM_STATE)),
        "bb_im": sw(_block_diag_extract(dbi, SSM_GROUP, SSM_STATE)),
        "c_re": sw(_block_diag_extract(dcr, SSM_STATE, SSM_GROUP)),
        "c_im": sw(_block_diag_extract(dci, SSM_STATE, SSM_GROUP)),
        "d_skip": d_d_skip, "norm2_g": d_norm2_g, "final_g": d_final_g,
    }
    comm.early_grads(early)
    comm.grad("w_in", riding("mm_dw_in", _mm, u, dproj, mode="tn", out_dtype=BF16, name="mm_dw_in",
                             out_gathered=True))
    du = comm.mm_d_in(dproj)
    grad_x, d_shift1, d_scale1, d_norm1_g = riding(
        "norm1_bwd", _norm_mod_bwd, du, x, dh1, small["norm1_g"], scale1, None, None, name="norm1_bwd")
    late ={"dmod_head": jnp.concatenate([d_shift1, d_scale1], axis=1), "norm1_g": d_norm1_g}
    return grad_x, early, late


WEIGHT_NAMES = ["w_ada", "b_ada", "norm1_g", "w_in", "w_dw", "b_dw", "ln_g", "ln_b", "w_conv_out", "a_re", "a_im",
                "log_dt", "b_re", "b_im", "c_re", "c_im", "d_skip", "w_glu_a", "w_glu_b", "w_out", "norm2_g",
                "w_ff1", "w_ff2", "final_g"]
BIG_NAMES = ["w_in", "w_conv_out", "w_glu_a", "w_glu_b", "w_out", "w_ff1", "w_ff2"]
ROW_SHARDED = ("w_out", "w_ff2")
PACK_COLS = 1024
PACK_TILE = SUBLANES * PACK_COLS


def _pack(arrays):
    flats = [a.reshape(-1) for a in arrays]
    offs = []
    total = 0
    for f in flats:
        offs.append(total)
        total += f.shape[0]
    pad = (-total) % PACK_TILE
    if pad:
        flats.append(jnp.zeros((pad,), F32))
    return jnp.concatenate(flats), offs


def _unpack(flat, offs, like):
    return [flat[o:o + a.size].reshape(a.shape) for o, a in zip(offs, like)]


def _gather_w_dw(w_shard):
    k, n = w_shard.shape
    padded = jnp.pad(w_shard, ((0, HALO - k), (0, 0)))
    allw = _gather_small(padded, name="gather_w_dw").reshape(N_CHIPS, 2, HALO, n)[:, 0, :k]
    return jnp.moveaxis(allw, 0, 1).reshape(k, N_CHIPS * n)


def kernel(x, c, w_ada, b_ada, norm1_g, w_in, w_dw, b_dw, ln_g, ln_b, w_conv_out, a_re, a_im, log_dt, b_re, b_im, c_re, c_im, d_skip, w_glu_a, w_glu_b, w_out, norm2_g, w_ff1, w_ff2, final_g, loss_target, m_w_ada, m_b_ada, m_norm1_g, m_w_in, m_w_dw, m_b_dw, m_ln_g, m_ln_b, m_w_conv_out, m_a_re, m_a_im, m_log_dt, m_b_re, m_b_im, m_c_re, m_c_im, m_d_skip, m_w_glu_a, m_w_glu_b, m_w_out, m_norm2_g, m_w_ff1, m_w_ff2, m_final_g, v_w_ada, v_b_ada, v_norm1_g, v_w_in, v_w_dw, v_b_dw, v_ln_g, v_ln_b, v_w_conv_out, v_a_re, v_a_im, v_log_dt, v_b_re, v_b_im, v_c_re, v_c_im, v_d_skip, v_w_glu_a, v_w_glu_b, v_w_out, v_norm2_g, v_w_ff1, v_w_ff2, v_final_g):
    given = dict(locals())
    w = {n: given[n] for n in WEIGHT_NAMES}
    m = {n: given["m_" + n] for n in WEIGHT_NAMES}
    v = {n: given["v_" + n] for n in WEIGHT_NAMES}
    d = x.shape[2]
    xi, yi, ci = _mesh_pos()
    chip = 2 * xi + yi
    dev = 4 * xi + 2 * yi + ci
    my_c = jnp.reshape(ci, (1,)).astype(jnp.int32)
    chip_arr = jnp.reshape(chip, (1,)).astype(jnp.int32)

    comm = _MeshComm({n: w[n][0] for n in BIG_NAMES}, (xi, yi, ci), chip_arr, my_c)

    ndw = w_dw.shape[2]
    assert d // SUBLANES == ndw
    first = jnp.concatenate([c.reshape(SUBLANES, ndw), jnp.pad(w_dw[0], ((0, HALO - CONV_KERNEL), (0, 0)))])
    first_all = _gather_small(first, name="gather_c_w_dw").reshape(N_DEV, SUBLANES + HALO, ndw)
    c_all = first_all[:, :SUBLANES].reshape(N_DEV, d)
    taps = first_all.reshape(N_CHIPS, 2, SUBLANES + HALO, ndw)[:, 0, SUBLANES:SUBLANES + CONV_KERNEL]
    w_dw_full = jnp.moveaxis(taps, 0, 1).reshape(CONV_KERNEL, N_CHIPS * ndw)

    nmod = w_ada.shape[2]
    b_cols = lax.dynamic_slice(b_ada, (0, chip * nmod), (1, nmod))
    mod_part = _ada_fwd(c_all, w_ada[0], b_cols)
    mod_all = _gather_small(mod_part, name="gather_mod").reshape(N_CHIPS, 2, N_DEV, nmod)[:, 0]
    mod_full = jnp.moveaxis(mod_all, 0, 1).reshape(N_DEV, N_CHIPS * nmod)
    mod_row = lax.dynamic_slice(mod_full, (dev, 0), (1, N_CHIPS * nmod))
    mod = [mod_row[:, i * d:(i + 1) * d] for i in range(6)]

    token = comm.start_w_in(mod_row)
    log_dt_0 = log_dt[0] + token[0, 0]

    disc_in = (a_re[0], a_im[0], log_dt_0, b_re[0], b_im[0])
    disc, disc_vjp = jax.vjp(_s5_discretise, *disc_in)
    dt = jnp.exp(log_dt_0)[:, None]
    small = {"norm1_g": norm1_g, "w_dw": w_dw_full, "b_dw": b_dw, "ln_g": ln_g, "ln_b": ln_b,
             "c_re": c_re[0], "c_im": c_im[0], "d_skip": d_skip, "norm2_g": norm2_g,
             "final_g": final_g[None, :], "s5_disc": disc, "s5_loglam": (a_re[0] * dt, a_im[0] * dt)}

    grad_x, early, late = _local_step(x[0], loss_target[0], mod, small, comm)
    grads = {}

    early_all = comm.early_all.reshape(N_DEV, -1, PACK_COLS)
    early_sum = _sum_leading(early_all, name="sum_small_grads").reshape(-1)
    summed = dict(zip(early, _unpack(early_sum, comm.early_offs, list(early.values()))))
    flat, late_offs = _pack(list(late.values()))
    late_all = _gather_small(flat.reshape(-1, PACK_COLS), name="gather_late_grads").reshape(N_DEV, -1, PACK_COLS)
    late_sum = _sum_leading(late_all, name="sum_late_grads").reshape(-1)
    summed.update(zip(late, _unpack(late_sum, late_offs, list(late.values()))))
    head = late_all[:, :2 * d // PACK_COLS].reshape(N_DEV, 2 * d)
    tail = early_all[:, :4 * d // PACK_COLS].reshape(N_DEV, 4 * d)
    dmod_all = jnp.concatenate([head, tail], axis=1)

    grads["w_ada"] = _ada_bwd(c_all, lax.dynamic_slice(dmod_all, (0, chip * nmod), (N_DEV, nmod)))
    grads["b_ada"] = _sum_leading(dmod_all.reshape(N_DEV, SUBLANES, 6 * d // SUBLANES),
                                  name="sum_b_ada").reshape(1, 6 * d)
    da_re, da_im, dlog_dt, db_re, db_im = disc_vjp(
        (summed["lam_re"], summed["lam_im"], summed["bb_re"], summed["bb_im"]))
    grads.update({
        "norm1_g": summed["norm1_g"], "w_dw": lax.dynamic_slice(summed["w_dw"], (0, chip * ndw), (CONV_KERNEL, ndw)),
        "b_dw": summed["b_dw"], "ln_g": summed["ln_g"], "ln_b": summed["ln_b"],
        "a_re": da_re, "a_im": da_im, "log_dt": dlog_dt, "b_re": db_re, "b_im": db_im,
        "c_re": summed["c_re"], "c_im": summed["c_im"], "d_skip": summed["d_skip"],
        "norm2_g": summed["norm2_g"], "final_g": summed["final_g"],
    })

    delta, new_m, new_v = {}, {}, {}

    def adam_big(n, after=None):
        shp = w[n].shape
        two_d = lambda a: a.reshape(shp[1], shp[2])
        res = _adamw(two_d(w[n]), two_d(grads[n]), two_d(m[n]), two_d(v[n]), name="adamw_" + n, after=after)
        delta[n], new_m[n], new_v[n] = [r.reshape(shp) for r in res]

    token = comm.join_start(BIG_NAMES, [late_all])
    adam_big("w_ada", token)
    grads.update(comm.join_finish([delta["w_ada"]]))
    for n in BIG_NAMES:
        adam_big(n)
    grads = {n: grads[n].reshape(w[n].shape) for n in WEIGHT_NAMES}
    rest = [n for n in WEIGHT_NAMES if n not in delta]
    as_2d = lambda a: a.reshape(1, -1) if a.ndim == 1 else a
    outs = _adamw_many(*[[as_2d(src[n]) for n in rest] for src in (w, grads, m, v)], name="adamw_small")
    for dst, arrays in zip((delta, new_m, new_v), outs):
        for n, a in zip(rest, arrays):
            dst[n] = a.reshape(w[n].shape)

    return (summed["loss"].reshape(()), grad_x[None], *[grads[n] for n in WEIGHT_NAMES],
            *[delta[n] for n in WEIGHT_NAMES], *[new_m[n] for n in WEIGHT_NAMES],
            *[new_v[n] for n in WEIGHT_NAMES])
```

```python
import functools
import math

import jax
import jax.numpy as jnp
from jax import lax
from jax.experimental import pallas as pl
from jax.experimental.pallas import tpu as pltpu

F32 = jnp.float32
BF16 = jnp.bfloat16
EPS = 1e-6
CONV_KERNEL = 31
SSM_GROUP = 16
SSM_STATE = 64
ADAM_LR = 0.001
ADAM_B1 = 0.9
ADAM_B2 = 0.999
ADAM_EPS = 1e-08
ADAM_WD = 0.01
ADAM_STEP = 10

N_CHIPS = 4
N_DEV = 8
VMEM_LIMIT_BYTES = 56 * 1024 * 1024
LANES = 128
SUBLANES = 8
HALO = 32
GROUPS_PER_BLOCK = LANES // SSM_GROUP
STATE_LANES = GROUPS_PER_BLOCK * SSM_STATE
MESH = pl.DeviceIdType.MESH


def _cparams(sem):
    return pltpu.CompilerParams(dimension_semantics=sem, vmem_limit_bytes=VMEM_LIMIT_BYTES)


def _pick(n, pref, mult=LANES):
    if n <= pref:
        return n
    best = None
    for d in range(mult, pref + 1, mult):
        if n % d == 0:
            best = d
    assert best is not None, (n, pref)
    return best


def _sigmoid(v):
    return 1.0 / (1.0 + jnp.exp(-v))


def _gelu_parts(v):
    k0 = math.sqrt(2.0 / math.pi)
    inner = k0 * (v + 0.044715 * v * v * v)
    t = jnp.tanh(inner)
    return k0, t


def _gelu(v):
    _, t = _gelu_parts(v)
    return 0.5 * v * (1.0 + t)


def _gelu_grad(v):
    k0, t = _gelu_parts(v)
    return 0.5 * (1.0 + t) + 0.5 * v * (1.0 - t * t) * k0 * (1.0 + 3.0 * 0.044715 * v * v)


def _relu2_bf16(a):
    t = jnp.maximum(a.astype(F32), 0.0)
    return (t * t).astype(BF16)


class _Carry:
    def __init__(self, inputs, out_shapes, aliases, sem_shapes, start, finish):
        self.inputs = list(inputs)
        self.out_shapes = list(out_shapes)
        self.aliases = dict(aliases)
        self.sem_shapes = list(sem_shapes)
        self.start = start
        self.finish = finish


def _call(body, *, grid, in_specs, out_specs, out_shape, scratch_shapes, semantics, name, args, carry=None,
          prefetch=(), aliases=None):
    n_in, n_out, n_scr, n_pf = len(in_specs), len(out_specs), len(scratch_shapes), len(prefetch)
    own_aliases = {n_pf + i: o for i, o in (aliases or {}).items()}
    if carry is None:
        gs = pltpu.PrefetchScalarGridSpec(
            num_scalar_prefetch=n_pf, grid=grid, in_specs=in_specs, out_specs=out_specs,
            scratch_shapes=scratch_shapes)
        outs = pl.pallas_call(
            body, grid_spec=gs, out_shape=out_shape, input_output_aliases=own_aliases,
            compiler_params=_cparams(semantics), name=name)(*prefetch, *args)
        return list(outs), []
    ci, co = len(carry.inputs), len(carry.out_shapes)

    def wrapped(*refs):
        pf, refs = refs[:n_pf], refs[n_pf:]
        ins, cins = refs[:n_in], refs[n_in:n_in + ci]
        p = n_in + ci
        outs, couts = refs[p:p + n_out], refs[p + n_out:p + n_out + co]
        p += n_out + co
        scr, csems = refs[p:p + n_scr], refs[p + n_scr:]
        first = pl.program_id(0) == 0
        last = pl.program_id(0) == grid[0] - 1
        for ax in range(1, len(grid)):
            first = jnp.logical_and(first, pl.program_id(ax) == 0)
            last = jnp.logical_and(last, pl.program_id(ax) == grid[ax] - 1)

        @pl.when(first)
        def _():
            carry.start(cins, couts, csems)

        body(*pf, *ins, *outs, *scr)

        @pl.when(last)
        def _():
            carry.finish(cins, couts, csems)

    any_spec = pl.BlockSpec(memory_space=pl.ANY)
    gs = pltpu.PrefetchScalarGridSpec(
        num_scalar_prefetch=n_pf, grid=grid, in_specs=list(in_specs) + [any_spec] * ci,
        out_specs=list(out_specs) + [any_spec] * co, scratch_shapes=list(scratch_shapes) + carry.sem_shapes)
    all_aliases = dict(own_aliases)
    all_aliases.update({n_pf + n_in + i: n_out + o for i, o in carry.aliases.items()})
    outs = pl.pallas_call(
        wrapped, grid_spec=gs, out_shape=list(out_shape) + carry.out_shapes, input_output_aliases=all_aliases,
        compiler_params=_cparams(("arbitrary",) * len(grid)), name=name)(*prefetch, *args, *carry.inputs)
    return list(outs[:n_out]), list(outs[n_out:])


def _run_carry(carry, *, name):
    ci = len(carry.inputs)

    def body(*refs):
        cins, couts, csems = refs[:ci], refs[ci:ci + len(carry.out_shapes)], refs[ci + len(carry.out_shapes):]
        carry.start(cins, couts, csems)
        carry.finish(cins, couts, csems)

    any_spec = pl.BlockSpec(memory_space=pl.ANY)
    outs = pl.pallas_call(
        body, in_specs=[any_spec] * ci, out_specs=[any_spec] * len(carry.out_shapes), out_shape=carry.out_shapes,
        scratch_shapes=carry.sem_shapes, input_output_aliases=carry.aliases, name=name)(*carry.inputs)
    return list(outs)


def _mm(a, b, *, mode, out_dtype, name, out_gathered=False, a_fn=None, epi=None, extra=None,
        bm_pref=1024, bn_pref=1024, bk_pref=2048, carry=None, a_slots=None):
    gathered = (b.ndim == 3)
    if mode == "nn":
        m, kdim = a.shape
        ns = b.shape[-1]
        n = ns * (N_CHIPS if gathered else 1)
        bm, bn, bk = _pick(m, bm_pref), _pick(ns, bn_pref), _pick(kdim, bk_pref)
        npb = ns // bn
        grid = (m // bm, n // bn, kdim // bk)
        a_spec = pl.BlockSpec((bm, bk), lambda i, j, k: (i, k))
        if gathered:
            b_spec = pl.BlockSpec((None, bk, bn), lambda i, j, k: (j // npb, k, j % npb))
        else:
            b_spec = pl.BlockSpec((bk, bn), lambda i, j, k: (k, j))
        o_spec = pl.BlockSpec((bm, bn), lambda i, j, k: (i, j))
        e_spec = pl.BlockSpec((bm, bn), lambda i, j, k: (i, j))
        out_shape = (m, n)
        acc_shape = (bm, bn)
        dims = (((1,), (0,)), ((), ()))
    elif mode == "nt":
        m = a.shape[0]
        kdim, ns = b.shape[-2], b.shape[-1]
        n = ns * (N_CHIPS if gathered else 1)
        assert a.shape[1] == n
        bm, bko, bnr = _pick(m, bm_pref), _pick(kdim, bn_pref), _pick(ns, bk_pref)
        npb = ns // bnr
        grid = (m // bm, kdim // bko, n // bnr)
        a_spec = pl.BlockSpec((bm, bnr), lambda i, j, k: (i, k))
        if gathered:
            b_spec = pl.BlockSpec((None, bko, bnr), lambda i, j, k: (k // npb, j, k % npb))
        else:
            b_spec = pl.BlockSpec((bko, bnr), lambda i, j, k: (j, k))
        o_spec = pl.BlockSpec((bm, bko), lambda i, j, k: (i, j))
        e_spec = pl.BlockSpec((bm, bko), lambda i, j, k: (i, j))
        if a_slots is not None:
            assert gathered and extra is None
            a_spec = pl.BlockSpec((bm, bnr), lambda i, j, k, s_ref: (i, s_ref[k // npb] * npb + k % npb))
            b_spec = pl.BlockSpec((None, bko, bnr), lambda i, j, k, s_ref: (k // npb, j, k % npb))
            o_spec = pl.BlockSpec((bm, bko), lambda i, j, k, s_ref: (i, j))
        out_shape = (m, kdim)
        acc_shape = (bm, bko)
        dims = (((1,), (1,)), ((), ()))
    else:
        m, kdim = a.shape
        n = b.shape[1]
        ns = n // N_CHIPS if out_gathered else n
        bmr, bko, bn = _pick(m, bk_pref), _pick(kdim, bm_pref), _pick(ns, bn_pref)
        npb = ns // bn
        grid = (kdim // bko, n // bn, m // bmr)
        a_spec = pl.BlockSpec((bmr, bko), lambda i, j, k: (k, i))
        b_spec = pl.BlockSpec((bmr, bn), lambda i, j, k: (k, j))
        if out_gathered:
            o_spec = pl.BlockSpec((None, bko, bn), lambda i, j, k: (j // npb, i, j % npb))
            out_shape = (N_CHIPS, kdim, ns)
        else:
            o_spec = pl.BlockSpec((bko, bn), lambda i, j, k: (i, j))
            out_shape = (kdim, n)
        e_spec = None
        acc_shape = (bko, bn)
        dims = (((0,), (0,)), ((), ()))
    nk = grid[2]

    def body(*refs):
        if a_slots is not None:
            refs = refs[1:]
        if extra is not None:
            a_ref, b_ref, e_ref, o_ref, acc = refs
        else:
            a_ref, b_ref, o_ref, acc = refs
            e_ref = None
        k = pl.program_id(2)
        av = a_ref[...]
        if a_fn is not None:
            av = a_fn(av)
        part = lax.dot_general(av, b_ref[...], dims, preferred_element_type=F32)

        def finish(r):
            if epi is not None:
                r = epi(r, e_ref[...])
            o_ref[...] = r.astype(o_ref.dtype)

        if nk == 1:
            finish(part)
            return

        @pl.when(k == 0)
        def _():
            acc[...] = part

        @pl.when(jnp.logical_and(k > 0, k < nk - 1))
        def _():
            acc[...] += part

        @pl.when(k == nk - 1)
        def _():
            finish(acc[...] + part)

    in_specs = [a_spec, b_spec]
    args = [a, b]
    if extra is not None:
        in_specs.append(e_spec)
        args.append(extra)
    outs, carried = _call(body, grid=grid, in_specs=in_specs, out_specs=[o_spec],
                          out_shape=[jax.ShapeDtypeStruct(out_shape, out_dtype)],
                          scratch_shapes=[pltpu.VMEM(acc_shape, F32)],
                          semantics=("parallel", "parallel", "arbitrary"), name=name, args=args, carry=carry,
                          prefetch=() if a_slots is None else (a_slots,))
    return outs[0] if carry is None else (outs[0], carried)


def _mm_slots(a, wbuf, slots, prev, *, name, carry=None, first=0):
    m, kdim = a.shape
    ns = wbuf.shape[2]
    bm, bn = _pick(m, 1024), _pick(ns, 1024)
    npb = ns // bn
    grid = (m // bm, slots.shape[0], npb)

    def body(s_ref, a_ref, b_ref, *rest):
        o_ref = rest[-1]
        o_ref[...] = _dot(a_ref[...], b_ref[...]).astype(o_ref.dtype)

    in_specs = [pl.BlockSpec((bm, kdim), lambda i, s, j, s_ref: (i, 0)),
                pl.BlockSpec((None, kdim, bn), lambda i, s, j, s_ref: (first + s, 0, j))]
    args = [a, wbuf]
    aliases = None
    if prev is not None:
        in_specs.append(pl.BlockSpec(memory_space=pl.ANY))
        args.append(prev)
        aliases = {2: 0}
    outs, carried = _call(
        body, grid=grid, in_specs=in_specs,
        out_specs=[pl.BlockSpec((bm, bn), lambda i, s, j, s_ref: (i, s_ref[s] * npb + j))],
        out_shape=[jax.ShapeDtypeStruct((m, N_CHIPS * ns), BF16)], scratch_shapes=[],
        semantics=("parallel", "arbitrary", "arbitrary"), name=name, args=args, carry=carry,
        prefetch=(slots,), aliases=aliases)
    return outs[0] if carry is None else (outs[0], carried)


def _row_tile(rows, cols, n_arrays):
    budget = VMEM_LIMIT_BYTES // 3
    cap = min(512, budget // (n_arrays * 2 * cols * 4))
    for t in range(cap - cap % SUBLANES, 0, -SUBLANES):
        if rows % t == 0:
            return t
    return rows


def _norm_mod(x, g, scale, shift, *, name):
    rows, d = x.shape
    tr = _row_tile(rows, d, 3)

    def body(x_ref, g_ref, sc_ref, sh_ref, o_ref):
        xv = x_ref[...]
        r = lax.rsqrt(jnp.mean(xv * xv, axis=-1, keepdims=True) + EPS)
        o_ref[...] = ((xv * r * g_ref[...]) * (1.0 + sc_ref[...]) + sh_ref[...]).astype(o_ref.dtype)

    row = pl.BlockSpec((tr, d), lambda i: (i, 0))
    vec = pl.BlockSpec((1, d), lambda i: (0, 0))
    return pl.pallas_call(
        body, grid=(rows // tr,), in_specs=[row, vec, vec, vec], out_specs=row,
        out_shape=jax.ShapeDtypeStruct((rows, d), BF16),
        compiler_params=_cparams(("parallel",)), name=name)(x, g, scale, shift)


CONV_CHUNK = 2 * SUBLANES


def _shifted_copies(buf, n):
    for r in range(1, SUBLANES):
        buf[r, pl.ds(0, n - SUBLANES), :] = buf[0, pl.ds(r, n - SUBLANES), :]


def _conv_fwd(proj, w_dw, b_dw, ln_g, ln_b, *, cw, carry=None):
    rows = proj.shape[0]
    tt = _pick(rows, 256, HALO)
    hb = tt // HALO

    def body(a_ref, g_ref, ha_ref, hg_ref, w_ref, b_ref, lg_ref, lb_ref, sl_ref, cv_ref, vs):
        i = pl.program_id(0)
        hv = ha_ref[...].astype(F32) * _sigmoid(hg_ref[...].astype(F32))
        vs[0, pl.ds(0, HALO), :] = jnp.where(i == 0, 0.0, hv)
        vs[0, pl.ds(HALO, tt), :] = a_ref[...].astype(F32) * _sigmoid(g_ref[...].astype(F32))
        _shifted_copies(vs, HALO + tt)

        def chunk(ci, carry):
            r0 = pl.multiple_of(ci * CONV_CHUNK, CONV_CHUNK)
            acc = jnp.broadcast_to(b_ref[...], (CONV_CHUNK, cw))
            for k in range(CONV_KERNEL):
                q, r = divmod(HALO - (CONV_KERNEL - 1) + k, SUBLANES)
                acc = acc + w_ref[pl.ds(k, 1), :] * vs[r, pl.ds(r0 + q * SUBLANES, CONV_CHUNK), :]
            cv_ref[pl.ds(r0, CONV_CHUNK), :] = acc
            return carry

        lax.fori_loop(0, tt // CONV_CHUNK, chunk, 0)
        acc = cv_ref[...]
        mu = jnp.mean(acc, axis=-1, keepdims=True)
        xc = acc - mu
        rstd = lax.rsqrt(jnp.mean(xc * xc, axis=-1, keepdims=True) + EPS)
        ln = xc * rstd * lg_ref[...] + lb_ref[...]
        sl_ref[...] = (ln * _sigmoid(ln)).astype(sl_ref.dtype)

    tile = lambda c: pl.BlockSpec((tt, cw), lambda i, c=c: (i, c))
    halo = lambda c: pl.BlockSpec((HALO, cw), lambda i, c=c: (jnp.maximum(i * hb - 1, 0), c))
    vec = pl.BlockSpec((1, cw), lambda i: (0, 0))
    outs, carried = _call(
        body, grid=(rows // tt,),
        in_specs=[tile(0), tile(1), halo(0), halo(1),
                  pl.BlockSpec((CONV_KERNEL, cw), lambda i: (0, 0)), vec, vec, vec],
        out_specs=[pl.BlockSpec((tt, cw), lambda i: (i, 0)), pl.BlockSpec((tt, cw), lambda i: (i, 0))],
        out_shape=[jax.ShapeDtypeStruct((rows, cw), BF16), jax.ShapeDtypeStruct((rows, cw), F32)],
        scratch_shapes=[pltpu.VMEM((SUBLANES, HALO + tt, cw), F32)],
        semantics=("parallel",), name="conv_fwd", args=[proj, proj, proj, proj, w_dw, b_dw, ln_g, ln_b],
        carry=carry)
    return outs if carry is None else (outs, carried)


def _ln_bwd(dsl, cv, ln_g, ln_b):
    rows, cw = cv.shape
    tr = _row_tile(rows, cw, 3)

    def body(d_ref, cv_ref, lg_ref, lb_ref, o_ref, dg_ref, db_ref):
        i = pl.program_id(0)

        @pl.when(i == 0)
        def _():
            dg_ref[...] = jnp.zeros_like(dg_ref)
            db_ref[...] = jnp.zeros_like(db_ref)

        x = cv_ref[...]
        mu = jnp.mean(x, axis=-1, keepdims=True)
        xc = x - mu
        rstd = lax.rsqrt(jnp.mean(xc * xc, axis=-1, keepdims=True) + EPS)
        xh = xc * rstd
        ln = xh * lg_ref[...] + lb_ref[...]
        s = _sigmoid(ln)
        dln = d_ref[...].astype(F32) * (s * (1.0 + ln * (1.0 - s)))
        dg_ref[...] += jnp.sum(dln * xh, axis=0, keepdims=True)
        db_ref[...] += jnp.sum(dln, axis=0, keepdims=True)
        dxh = dln * lg_ref[...]
        m1 = jnp.mean(dxh, axis=-1, keepdims=True)
        m2 = jnp.mean(dxh * xh, axis=-1, keepdims=True)
        o_ref[...] = rstd * (dxh - m1 - xh * m2)

    row = pl.BlockSpec((tr, cw), lambda i: (i, 0))
    vec = pl.BlockSpec((1, cw), lambda i: (0, 0))
    return pl.pallas_call(
        body, grid=(rows // tr,), in_specs=[row, row, vec, vec], out_specs=[row, vec, vec],
        out_shape=[jax.ShapeDtypeStruct((rows, cw), F32), jax.ShapeDtypeStruct((1, cw), F32),
                   jax.ShapeDtypeStruct((1, cw), F32)],
        compiler_params=_cparams(("arbitrary",)), name="ln_bwd")(dsl, cv, ln_g, ln_b)


def _conv_bwd(dcv, proj, w_dw, dproj, *, cw, carry=None):
    rows = proj.shape[0]
    tt = _pick(rows, 256, HALO)
    hb = tt // HALO
    nt = rows // tt
    taps = CONV_KERNEL

    def body(d_ref, dn_ref, a_ref, g_ref, ha_ref, hg_ref, w_ref, dproj_ref, o_ref, dw_ref, db_ref, vs, ds):
        i = pl.program_id(0)

        @pl.when(i == 0)
        def _():
            dw_ref[...] = jnp.zeros_like(dw_ref)
            db_ref[...] = jnp.zeros_like(db_ref)

        hv = ha_ref[...].astype(F32) * _sigmoid(hg_ref[...].astype(F32))
        vs[0, pl.ds(0, HALO), :] = jnp.where(i == 0, 0.0, hv)
        vs[0, pl.ds(HALO, tt), :] = a_ref[...].astype(F32) * _sigmoid(g_ref[...].astype(F32))
        _shifted_copies(vs, HALO + tt)
        ds[0, pl.ds(0, tt), :] = d_ref[...]
        ds[0, pl.ds(tt, HALO), :] = jnp.where(i == nt - 1, 0.0, dn_ref[...])
        _shifted_copies(ds, tt + HALO)
        db_ref[...] += jnp.sum(d_ref[...], axis=0, keepdims=True)
        for k in range(taps):
            q, r = divmod(HALO - (taps - 1) + k, SUBLANES)
            dw_ref[pl.ds(k, 1), :] += jnp.sum(d_ref[...] * vs[r, pl.ds(q * SUBLANES, tt), :], axis=0, keepdims=True)

        def chunk(ci, carry):
            r0 = pl.multiple_of(ci * CONV_CHUNK, CONV_CHUNK)
            dv = jnp.zeros((CONV_CHUNK, cw), F32)
            for k in range(taps):
                q, r = divmod(taps - 1 - k, SUBLANES)
                dv = dv + w_ref[pl.ds(k, 1), :] * ds[r, pl.ds(r0 + q * SUBLANES, CONV_CHUNK), :]
            av = a_ref[pl.ds(r0, CONV_CHUNK), :].astype(F32)
            sg = _sigmoid(g_ref[pl.ds(r0, CONV_CHUNK), :].astype(F32))
            o_ref[pl.ds(r0, CONV_CHUNK), pl.ds(0, cw)] = (dv * sg).astype(o_ref.dtype)
            o_ref[pl.ds(r0, CONV_CHUNK), pl.ds(cw, cw)] = (dv * av * sg * (1.0 - sg)).astype(o_ref.dtype)
            return carry

        lax.fori_loop(0, tt // CONV_CHUNK, chunk, 0)

    tile = lambda c: pl.BlockSpec((tt, cw), lambda i, c=c: (i, c))
    halo = lambda c: pl.BlockSpec((HALO, cw), lambda i, c=c: (jnp.maximum(i * hb - 1, 0), c))
    nxt = pl.BlockSpec((HALO, cw), lambda i: (jnp.minimum((i + 1) * hb, nt * hb - 1), 0))
    outs, carried = _call(
        body, grid=(nt,),
        in_specs=[pl.BlockSpec((tt, cw), lambda i: (i, 0)), nxt, tile(0), tile(1), halo(0), halo(1),
                  pl.BlockSpec((taps, cw), lambda i: (0, 0)), pl.BlockSpec(memory_space=pl.ANY)],
        out_specs=[pl.BlockSpec((tt, 2 * cw), lambda i: (i, 0)),
                   pl.BlockSpec((taps, cw), lambda i: (0, 0)), pl.BlockSpec((1, cw), lambda i: (0, 0))],
        out_shape=[jax.ShapeDtypeStruct(dproj.shape, dproj.dtype), jax.ShapeDtypeStruct((taps, cw), F32),
                   jax.ShapeDtypeStruct((1, cw), F32)],
        scratch_shapes=[pltpu.VMEM((SUBLANES, HALO + tt, cw), F32), pltpu.VMEM((SUBLANES, tt + HALO, cw), F32)],
        semantics=("arbitrary",), name="conv_bwd", args=[dcv, dcv, proj, proj, proj, proj, w_dw, dproj],
        carry=carry, aliases={7: 0})
    return outs if carry is None else (outs, carried)


def _merge_fwd(proj, y_conv, ya, yb, *, cw):
    rows = proj.shape[0]
    tr = _row_tile(rows, cw, 4)

    def body(gc_ref, gs_ref, yc_ref, ya_ref, yb_ref, o_ref):
        ys = ya_ref[...].astype(F32) * _sigmoid(yb_ref[...].astype(F32))
        o_ref[...] = (_sigmoid(gc_ref[...].astype(F32)) * yc_ref[...].astype(F32)
                      + _sigmoid(gs_ref[...].astype(F32)) * ys).astype(o_ref.dtype)

    blk = lambda off: pl.BlockSpec((tr, cw), lambda i, h, off=off: (i, off + h))
    return pl.pallas_call(
        body, grid=(rows // tr, 2), in_specs=[blk(3), blk(5), blk(0), blk(0), blk(0)], out_specs=blk(0),
        out_shape=jax.ShapeDtypeStruct((rows, 2 * cw), BF16),
        compiler_params=_cparams(("parallel", "parallel")), name="merge_fwd")(proj, proj, y_conv, ya, yb)


def _merge_bwd(dmerged, proj, y_conv, ya, yb, *, cw, carry=None):
    rows = proj.shape[0]
    tr = _row_tile(rows, cw, 6)

    def body(d_ref, g_ref, yc_ref, ya_ref, yb_ref, dg_ref, dyc_ref, dya_ref, dyb_ref):
        q = pl.program_id(1)
        d = d_ref[...].astype(F32)
        sg = _sigmoid(g_ref[...].astype(F32))

        @pl.when(q < 2)
        def _():
            dg_ref[...] = (d * yc_ref[...].astype(F32) * sg * (1.0 - sg)).astype(dg_ref.dtype)
            dyc_ref[...] = (d * sg).astype(dyc_ref.dtype)

        @pl.when(q >= 2)
        def _():
            sb = _sigmoid(yb_ref[...].astype(F32))
            yav = ya_ref[...].astype(F32)
            dg_ref[...] = (d * (yav * sb) * sg * (1.0 - sg)).astype(dg_ref.dtype)
            dys = d * sg
            dya_ref[...] = (dys * sb).astype(dya_ref.dtype)
            dyb_ref[...] = (dys * yav * sb * (1.0 - sb)).astype(dyb_ref.dtype)

    spec = lambda f: pl.BlockSpec((tr, cw), lambda i, q, f=f: (i, f(q)))
    conv_half = spec(lambda q: jnp.minimum(q, 1))
    ssm_half = spec(lambda q: jnp.maximum(q - 2, 0))
    o2 = jax.ShapeDtypeStruct((rows, 2 * cw), BF16)
    outs, carried = _call(
        body, grid=(rows // tr, 4),
        in_specs=[spec(lambda q: q % 2), spec(lambda q: 3 + q), conv_half, ssm_half, ssm_half],
        out_specs=[spec(lambda q: 3 + q), conv_half, ssm_half, ssm_half],
        out_shape=[jax.ShapeDtypeStruct((rows, 7 * cw), BF16), o2, o2, o2], scratch_shapes=[],
        semantics=("parallel", "arbitrary"), name="merge_bwd", args=[dmerged, proj, y_conv, ya, yb],
        carry=carry)
    return outs if carry is None else (outs, carried)


def _res_norm(x, mo, gate, g, scale, shift):
    rows, d = x.shape
    tr = _row_tile(rows, d, 4)

    def body(x_ref, mo_ref, gt_ref, g_ref, sc_ref, sh_ref, h_ref, z_ref):
        h = x_ref[...] + gt_ref[...] * mo_ref[...].astype(F32)
        h_ref[...] = h
        r = lax.rsqrt(jnp.mean(h * h, axis=-1, keepdims=True) + EPS)
        z_ref[...] = ((h * r * g_ref[...]) * (1.0 + sc_ref[...]) + sh_ref[...]).astype(z_ref.dtype)

    row = pl.BlockSpec((tr, d), lambda i: (i, 0))
    vec = pl.BlockSpec((1, d), lambda i: (0, 0))
    return pl.pallas_call(
        body, grid=(rows // tr,), in_specs=[row, row, vec, vec, vec, vec], out_specs=[row, row],
        out_shape=[jax.ShapeDtypeStruct((rows, d), F32), jax.ShapeDtypeStruct((rows, d), BF16)],
        compiler_params=_cparams(("parallel",)), name="res_norm")(x, mo, gate, g, scale, shift)


def _final_fwd_bwd(h1, ff, gate2, final_g, target):
    rows, d = h1.shape
    tr = _row_tile(rows, d, 5)

    def body(h_ref, ff_ref, gt_ref, fg_ref, t_ref, dh_ref, dff_ref, loss_ref, dfg_ref, dgt_ref):
        i = pl.program_id(0)

        @pl.when(i == 0)
        def _():
            loss_ref[...] = jnp.zeros_like(loss_ref)
            dfg_ref[...] = jnp.zeros_like(dfg_ref)
            dgt_ref[...] = jnp.zeros_like(dgt_ref)

        ffv = ff_ref[...].astype(F32)
        h2 = h_ref[...] + gt_ref[...] * ffv
        r = lax.rsqrt(jnp.mean(h2 * h2, axis=-1, keepdims=True) + EPS)
        y = h2 * r
        e = y * fg_ref[...] - t_ref[...]
        loss_ref[...] += 0.5 * jnp.sum(jnp.mean(e * e, axis=-1, keepdims=True))
        dout = e * (1.0 / d)
        dfg_ref[...] += jnp.sum(dout * y, axis=0, keepdims=True)
        dy = dout * fg_ref[...]
        dh2 = r * (dy - y * jnp.mean(dy * y, axis=-1, keepdims=True))
        dh_ref[...] = dh2
        dgt_ref[...] += jnp.sum(dh2 * ffv, axis=0, keepdims=True)
        dff_ref[...] = (dh2 * gt_ref[...]).astype(dff_ref.dtype)

    row = pl.BlockSpec((tr, d), lambda i: (i, 0))
    vec = pl.BlockSpec((1, d), lambda i: (0, 0))
    return pl.pallas_call(
        body, grid=(rows // tr,), in_specs=[row, row, vec, vec, row],
        out_specs=[row, row, pl.BlockSpec((1, LANES), lambda i: (0, 0)), vec, vec],
        out_shape=[jax.ShapeDtypeStruct((rows, d), F32), jax.ShapeDtypeStruct((rows, d), BF16),
                   jax.ShapeDtypeStruct((1, LANES), F32), jax.ShapeDtypeStruct((1, d), F32),
                   jax.ShapeDtypeStruct((1, d), F32)],
        compiler_params=_cparams(("arbitrary",)), name="final_fwd_bwd")(h1, ff, gate2, final_g, target)


def _norm_mod_bwd(dz, hin, dres, g, scale, gate, mo, *, name, carry=None):
    rows, d = hin.shape
    with_gate = gate is not None
    tr = _row_tile(rows, d, 6)

    def body(*refs):
        if with_gate:
            (dz_ref, h_ref, dr_ref, g_ref, sc_ref, gt_ref, mo_ref,
             dh_ref, dsh_ref, dsc_ref, dg_ref, dmo_ref, dgt_ref) = refs
        else:
            dz_ref, h_ref, dr_ref, g_ref, sc_ref, dh_ref, dsh_ref, dsc_ref, dg_ref = refs
        i = pl.program_id(0)

        @pl.when(i == 0)
        def _():
            dsh_ref[...] = jnp.zeros_like(dsh_ref)
            dsc_ref[...] = jnp.zeros_like(dsc_ref)
            dg_ref[...] = jnp.zeros_like(dg_ref)
            if with_gate:
                dgt_ref[...] = jnp.zeros_like(dgt_ref)

        dzv = dz_ref[...].astype(F32)
        h = h_ref[...]
        r = lax.rsqrt(jnp.mean(h * h, axis=-1, keepdims=True) + EPS)
        y = h * r
        dsh_ref[...] += jnp.sum(dzv, axis=0, keepdims=True)
        dsc_ref[...] += jnp.sum(dzv * (y * g_ref[...]), axis=0, keepdims=True)
        dn = dzv * (1.0 + sc_ref[...])
        dg_ref[...] += jnp.sum(dn * y, axis=0, keepdims=True)
        dy = dn * g_ref[...]
        dh = dr_ref[...] + r * (dy - y * jnp.mean(dy * y, axis=-1, keepdims=True))
        dh_ref[...] = dh
        if with_gate:
            dmo_ref[...] = (dh * gt_ref[...]).astype(dmo_ref.dtype)
            dgt_ref[...] += jnp.sum(dh * mo_ref[...].astype(F32), axis=0, keepdims=True)

    row = pl.BlockSpec((tr, d), lambda i: (i, 0))
    vec = pl.BlockSpec((1, d), lambda i: (0, 0))
    vshape = jax.ShapeDtypeStruct((1, d), F32)
    in_specs = [row, row, row, vec, vec]
    args = [dz, hin, dres, g, scale]
    out_specs = [row, vec, vec, vec]
    out_shape = [jax.ShapeDtypeStruct((rows, d), F32), vshape, vshape, vshape]
    if with_gate:
        in_specs += [vec, row]
        args += [gate, mo]
        out_specs += [row, vec]
        out_shape += [jax.ShapeDtypeStruct((rows, d), BF16), vshape]
    outs, carried = _call(
        body, grid=(rows // tr,), in_specs=in_specs, out_specs=out_specs, out_shape=out_shape,
        scratch_shapes=[], semantics=("arbitrary",), name=name, args=args, carry=carry)
    return outs if carry is None else (outs, carried)


def _s5_discretise(a_re, a_im, log_dt, b_re, b_im):
    dt = jnp.exp(log_dt)[:, None]
    er = jnp.exp(a_re * dt)
    lr = er * jnp.cos(a_im * dt)
    li = er * jnp.sin(a_im * dt)
    den = a_re * a_re + a_im * a_im
    cr = ((lr - 1.0) * a_re + li * a_im) / den
    ci = (li * a_re - (lr - 1.0) * a_im) / den
    bbr = cr[..., None] * b_re - ci[..., None] * b_im
    bbi = cr[..., None] * b_im + ci[..., None] * b_re
    return lr, li, bbr, bbi


def _block_diag(w):
    g, r, c = w.shape
    nb = g // GROUPS_PER_BLOCK
    eye = jnp.eye(GROUPS_PER_BLOCK, dtype=w.dtype)
    w5 = w.reshape(nb, GROUPS_PER_BLOCK, r, 1, c) * eye[None, :, None, :, None]
    return w5.reshape(nb, GROUPS_PER_BLOCK * r, GROUPS_PER_BLOCK * c)


def _block_diag_extract(m, r, c):
    nb = m.shape[0]
    m5 = m.reshape(nb, GROUPS_PER_BLOCK, r, GROUPS_PER_BLOCK, c)
    idx = jnp.arange(GROUPS_PER_BLOCK)
    d = m5[:, idx, :, idx, :]
    return jnp.moveaxis(d, 0, 1).reshape(nb * GROUPS_PER_BLOCK, r, c)


def _scan_multipliers(lr, li):
    power = jnp.arange(1, SUBLANES + 1, dtype=F32)[None, :, None]
    er = jnp.exp(power * lr)
    pr = er * jnp.cos(power * li)
    pi = er * jnp.sin(power * li)
    rows = jnp.arange(SUBLANES)[None, :, None]
    fr, fi, rr, ri = [], [], [], []
    for s in (1, 2, 4):
        mf = (rows >= s).astype(F32)
        mr = (rows <= SUBLANES - 1 - s).astype(F32)
        fr.append(mf * pr[:, s - 1:s, :])
        fi.append(mf * pi[:, s - 1:s, :])
        rr.append(mr * pr[:, s - 1:s, :])
        ri.append(mr * pi[:, s - 1:s, :])
    fr.append(pr)
    fi.append(pi)
    rr.append(pr[:, ::-1, :])
    ri.append(pi[:, ::-1, :])
    st = lambda xs: jnp.stack(xs, axis=1)
    return st(fr), st(fi), st(rr), st(ri)


def _scan_rows(sre, sim, mul_r, mul_i, n_groups, reverse):
    sgn = -1.0 if reverse else 1.0
    lanes = sre.shape[1]

    def step(k, carry):
        cr, ci = carry
        kk = (n_groups - 1 - k) if reverse else k
        r0 = pl.multiple_of(kk * SUBLANES, SUBLANES)
        xr = sre[pl.ds(r0, SUBLANES), :]
        xi = sim[pl.ds(r0, SUBLANES), :]
        for lvl, s in enumerate((1, 2, 4)):
            sh = (SUBLANES - s) if reverse else s
            nr = pltpu.roll(xr, sh, 0)
            ni = pltpu.roll(xi, sh, 0)
            mr = mul_r[lvl]
            mi = mul_i[lvl] * sgn
            xr, xi = xr + mr * nr - mi * ni, xi + mr * ni + mi * nr
        mr = mul_r[3]
        mi = mul_i[3] * sgn
        xr, xi = xr + mr * cr - mi * ci, xi + mr * ci + mi * cr
        sre[pl.ds(r0, SUBLANES), :] = xr
        sim[pl.ds(r0, SUBLANES), :] = xi
        edge = 0 if reverse else SUBLANES - 1
        ncr = jnp.broadcast_to(xr[edge:edge + 1, :], (SUBLANES, lanes))
        nci = jnp.broadcast_to(xi[edge:edge + 1, :], (SUBLANES, lanes))
        return ncr, nci

    zero = jnp.zeros((SUBLANES, lanes), F32)
    lax.fori_loop(0, n_groups, step, (zero, zero))


def _dot(a, b):
    return jnp.dot(a, b, preferred_element_type=F32)


def _dotf(a, b):
    return _dot(a.astype(BF16), b)


def _s5_operands(lr, li, bbr, bbi, c_re, c_im):
    g = lr.shape[0]
    nb = g // GROUPS_PER_BLOCK
    tb = lambda w: jnp.swapaxes(w, 1, 2)
    b_in = [_block_diag(tb(bbr)), _block_diag(tb(bbi))]
    c_out = [_block_diag(tb(c_re)), _block_diag(tb(c_im))]
    b_out = [_block_diag(bbr), _block_diag(bbi)]
    c_in = [_block_diag(c_re), _block_diag(c_im)]
    lam_r = lr.reshape(nb, 1, STATE_LANES)
    lam_i = li.reshape(nb, 1, STATE_LANES)
    mults = _scan_multipliers(lam_r, lam_i)
    cast = lambda ws: [w.astype(BF16) for w in ws]
    return cast(b_in), cast(c_out), cast(b_out), cast(c_in), mults


def _s5_fwd(proj, d_skip, b_in, c_out, mults, *, col0, carry=None):
    rows = proj.shape[0]
    nb = b_in[0].shape[0]
    tm = _pick(rows, 512, SUBLANES)
    n_tiles = rows // tm
    s_l = STATE_LANES

    def body(u_ref, dk_ref, br, bi, cr, ci, fr_ref, fi_ref, o_ref, sr_ref, si_ref, sre, sim):
        for t in range(n_tiles):
            rs = pl.ds(t * tm, tm)
            ub = u_ref[rs, :]
            sre[rs, :] = _dot(ub, br[...])
            sim[rs, :] = _dot(ub, bi[...])
        _scan_rows(sre, sim, fr_ref, fi_ref, rows // SUBLANES, False)
        for t in range(n_tiles):
            rs = pl.ds(t * tm, tm)
            srb = sre[rs, :].astype(BF16)
            sib = sim[rs, :].astype(BF16)
            sr_ref[rs, :] = srb
            si_ref[rs, :] = sib
            y0 = _dot(srb, cr[...]) - _dot(sib, ci[...])
            y1 = y0 + dk_ref[...] * u_ref[rs, :].astype(F32)
            o_ref[rs, :] = _gelu(y1).astype(o_ref.dtype)

    mat_in = pl.BlockSpec((None, LANES, s_l), lambda g: (g, 0, 0))
    mat_out = pl.BlockSpec((None, s_l, LANES), lambda g: (g, 0, 0))
    mul = pl.BlockSpec((None, 4, SUBLANES, s_l), lambda g: (g, 0, 0, 0))
    state = pl.BlockSpec((rows, s_l), lambda g: (0, g))
    outs, carried = _call(
        body, grid=(nb,),
        in_specs=[pl.BlockSpec((rows, LANES), lambda g: (0, col0 + g)), pl.BlockSpec((1, LANES), lambda g: (0, g))]
        + [mat_in] * 2 + [mat_out] * 2 + [mul] * 2,
        out_specs=[pl.BlockSpec((rows, LANES), lambda g: (0, g)), state, state],
        out_shape=[jax.ShapeDtypeStruct((rows, nb * LANES), BF16), jax.ShapeDtypeStruct((rows, nb * s_l), BF16),
                   jax.ShapeDtypeStruct((rows, nb * s_l), BF16)],
        scratch_shapes=[pltpu.VMEM((rows, s_l), F32), pltpu.VMEM((rows, s_l), F32)],
        semantics=("parallel",), name="s5_fwd", args=[proj, d_skip, *b_in, *c_out, mults[0], mults[1]], carry=carry)
    return outs if carry is None else (outs, carried)


def _s5_bwd(proj, dyg, d_skip, states, c_out, b_out, c_in, mults, dproj, *, col0, carry=None):
    rows = proj.shape[0]
    nb = c_out[0].shape[0]
    tm = _pick(rows, 512, SUBLANES)
    n_tiles = rows // tm
    s_l = STATE_LANES
    n_groups = rows // SUBLANES
    tn = (((0,), (0,)), ((), ()))

    def body(u_ref, dy_ref, dk_ref, sr_ref, si_ref, cr, ci, bor, boi, cir, cii, rr_ref, ri_ref, dproj_ref,
             du_ref, ddk_ref, dbr_ref, dbi_ref, dcr_ref, dci_ref, dlr_ref, dli_ref,
             gre, gim, dy1):
        ddk = jnp.zeros((1, LANES), F32)
        dcr = jnp.zeros((s_l, LANES), F32)
        dci = jnp.zeros((s_l, LANES), F32)
        for t in range(n_tiles):
            rs = pl.ds(t * tm, tm)
            srb = sr_ref[rs, :]
            sib = si_ref[rs, :]
            uf = u_ref[rs, :].astype(F32)
            y0 = _dot(srb, cr[...]) - _dot(sib, ci[...])
            y1 = y0 + dk_ref[...] * uf
            d1 = dy_ref[rs, :].astype(F32) * _gelu_grad(y1)
            dy1[rs, :] = d1
            ddk = ddk + jnp.sum(d1 * uf, axis=0, keepdims=True)
            d1b = d1.astype(BF16)
            dcr = dcr + lax.dot_general(srb, d1b, tn, preferred_element_type=F32)
            dci = dci - lax.dot_general(sib, d1b, tn, preferred_element_type=F32)
            gre[rs, :] = _dot(d1b, cir[...])
            gim[rs, :] = -_dot(d1b, cii[...])
        ddk_ref[...] = ddk
        dcr_ref[...] = dcr
        dci_ref[...] = dci

        last_row = lax.broadcasted_iota(jnp.int32, (SUBLANES, s_l), 0) == SUBLANES - 1

        def group(r0, s_r, s_i, carry):
            cr_, ci_, ar, ai = carry
            xr = gre[pl.ds(r0, SUBLANES), :]
            xi = gim[pl.ds(r0, SUBLANES), :]
            for lvl, s in enumerate((1, 2, 4)):
                nr = pltpu.roll(xr, SUBLANES - s, 0)
                ni = pltpu.roll(xi, SUBLANES - s, 0)
                mr = rr_ref[lvl]
                mi = ri_ref[lvl]
                xr, xi = xr + mr * nr + mi * ni, xi + mr * ni - mi * nr
            mr = rr_ref[3]
            mi = ri_ref[3]
            xr, xi = xr + mr * cr_ + mi * ci_, xi + mr * ci_ - mi * cr_
            gre[pl.ds(r0, SUBLANES), :] = xr
            gim[pl.ds(r0, SUBLANES), :] = xi
            nxt_r = jnp.where(last_row, cr_, pltpu.roll(xr, SUBLANES - 1, 0))
            nxt_i = jnp.where(last_row, ci_, pltpu.roll(xi, SUBLANES - 1, 0))
            ncr = jnp.broadcast_to(xr[0:1, :], (SUBLANES, s_l))
            nci = jnp.broadcast_to(xi[0:1, :], (SUBLANES, s_l))
            return ncr, nci, ar + nxt_r * s_r + nxt_i * s_i, ai + nxt_i * s_r - nxt_r * s_i

        def rev_step(k, carry):
            r0 = pl.multiple_of((n_groups // 2 - 1 - k) * 2 * SUBLANES, 2 * SUBLANES)
            s_r = sr_ref[pl.ds(r0, 2 * SUBLANES), :].astype(F32)
            s_i = si_ref[pl.ds(r0, 2 * SUBLANES), :].astype(F32)
            carry = group(r0 + SUBLANES, s_r[SUBLANES:], s_i[SUBLANES:], carry)
            return group(r0, s_r[:SUBLANES], s_i[:SUBLANES], carry)

        zero = jnp.zeros((SUBLANES, s_l), F32)
        _, _, ar, ai = lax.fori_loop(0, n_groups // 2, rev_step, (zero, zero, zero, zero))
        dlr_ref[...] = jnp.sum(ar, axis=0, keepdims=True)
        dli_ref[...] = jnp.sum(ai, axis=0, keepdims=True)

        dbr = jnp.zeros((LANES, s_l), F32)
        dbi = jnp.zeros((LANES, s_l), F32)
        for t in range(n_tiles):
            rs = pl.ds(t * tm, tm)
            gr = gre[rs, :]
            gi = gim[rs, :]
            grb = gr.astype(BF16)
            gib = gi.astype(BF16)
            du = _dot(grb, bor[...]) + _dot(gib, boi[...]) + dy1[rs, :] * dk_ref[...]
            du_ref[rs, :] = du.astype(du_ref.dtype)
            ub = u_ref[rs, :]
            dbr = dbr + lax.dot_general(ub, grb, tn, preferred_element_type=F32)
            dbi = dbi + lax.dot_general(ub, gib, tn, preferred_element_type=F32)
        dbr_ref[...] = dbr
        dbi_ref[...] = dbi

    mat_in = pl.BlockSpec((None, LANES, s_l), lambda g: (g, 0, 0))
    mat_out = pl.BlockSpec((None, s_l, LANES), lambda g: (g, 0, 0))
    mul = pl.BlockSpec((None, 4, SUBLANES, s_l), lambda g: (g, 0, 0, 0))
    lam = pl.BlockSpec((None, 1, s_l), lambda g: (g, 0, 0))
    col = pl.BlockSpec((rows, LANES), lambda g: (0, g))
    vec = pl.BlockSpec((1, LANES), lambda g: (0, g))
    state = pl.BlockSpec((rows, s_l), lambda g: (0, g))
    outs, carried = _call(
        body, grid=(nb,),
        in_specs=[pl.BlockSpec((rows, LANES), lambda g: (0, col0 + g)), col, vec]
        + [state] * 2 + [mat_out] * 2 + [mat_out] * 2 + [mat_in] * 2 + [mul] * 2
        + [pl.BlockSpec(memory_space=pl.ANY)],
        out_specs=[pl.BlockSpec((rows, LANES), lambda g: (0, col0 + g)), vec, mat_in, mat_in, mat_out, mat_out,
                   lam, lam],
        out_shape=[jax.ShapeDtypeStruct(dproj.shape, dproj.dtype), jax.ShapeDtypeStruct((1, nb * LANES), F32),
                   jax.ShapeDtypeStruct((nb, LANES, s_l), F32), jax.ShapeDtypeStruct((nb, LANES, s_l), F32),
                   jax.ShapeDtypeStruct((nb, s_l, LANES), F32), jax.ShapeDtypeStruct((nb, s_l, LANES), F32),
                   jax.ShapeDtypeStruct((nb, 1, s_l), F32), jax.ShapeDtypeStruct((nb, 1, s_l), F32)],
        scratch_shapes=[pltpu.VMEM((rows, s_l), F32)] * 2 + [pltpu.VMEM((rows, LANES), F32)],
        semantics=("parallel",), name="s5_bwd",
        args=[proj, dyg, d_skip, *states, *c_out, *b_out, *c_in, mults[2], mults[3], dproj], carry=carry,
        aliases={13: 0})
    return outs if carry is None else (outs, carried)


def _silu(v):
    return v * _sigmoid(v)


def _ada_fwd(c_all, w_shard, b_cols):
    d, n = w_shard.shape
    bn = _pick(n, 512)

    def body(c_ref, w_ref, b_ref, o_ref):
        ca = _silu(c_ref[...]).astype(BF16)
        o_ref[...] = _dot(ca, w_ref[...].astype(BF16)) + b_ref[...]

    return pl.pallas_call(
        body, grid=(n // bn,),
        in_specs=[pl.BlockSpec((N_DEV, d), lambda j: (0, 0)), pl.BlockSpec((d, bn), lambda j: (0, j)),
                  pl.BlockSpec((1, bn), lambda j: (0, j))],
        out_specs=pl.BlockSpec((N_DEV, bn), lambda j: (0, j)),
        out_shape=jax.ShapeDtypeStruct((N_DEV, n), F32),
        compiler_params=_cparams(("parallel",)), name="ada_fwd")(c_all, w_shard, b_cols)


def _ada_bwd(c_all, dmod_cols):
    d = c_all.shape[1]
    n = dmod_cols.shape[1]
    bn = _pick(n, 512)

    def body(c_ref, g_ref, o_ref):
        ca = _silu(c_ref[...]).astype(BF16)
        o_ref[...] = lax.dot_general(ca, g_ref[...].astype(BF16), (((0,), (0,)), ((), ())),
                                     preferred_element_type=F32)

    return pl.pallas_call(
        body, grid=(n // bn,),
        in_specs=[pl.BlockSpec((N_DEV, d), lambda j: (0, 0)), pl.BlockSpec((N_DEV, bn), lambda j: (0, j))],
        out_specs=pl.BlockSpec((d, bn), lambda j: (0, j)),
        out_shape=jax.ShapeDtypeStruct((d, n), F32),
        compiler_params=_cparams(("parallel",)), name="ada_bwd")(c_all, dmod_cols)


def _cast_bf16(w, *, name):
    rows, cols = w.shape
    tr = _row_tile(rows, cols, 3)

    def body(w_ref, o_ref, slot_ref):
        o_ref[...] = w_ref[...].astype(BF16)
        slot_ref[...] = w_ref[...].astype(BF16)

    row = pl.BlockSpec((tr, cols), lambda i: (i, 0))
    return pl.pallas_call(
        body, grid=(rows // tr,), in_specs=[row],
        out_specs=[row, pl.BlockSpec((None, tr, cols), lambda i: (0, i, 0))],
        out_shape=[jax.ShapeDtypeStruct((rows, cols), BF16), jax.ShapeDtypeStruct((N_CHIPS, rows, cols), BF16)],
        compiler_params=_cparams(("parallel",)), name=name)(w)


def _adamw_update(w_ref, g_ref, m_ref, v_ref, d_ref, nm_ref, nv_ref):
    c1 = 1.0 / (1.0 - ADAM_B1 ** ADAM_STEP)
    c2 = 1.0 / (1.0 - ADAM_B2 ** ADAM_STEP)
    gv = g_ref[...]
    nm = ADAM_B1 * m_ref[...] + (1.0 - ADAM_B1) * gv
    nv = ADAM_B2 * v_ref[...] + (1.0 - ADAM_B2) * (gv * gv)
    nm_ref[...] = nm
    nv_ref[...] = nv
    d_ref[...] = -ADAM_LR * ((nm * c1) / (jnp.sqrt(nv * c2) + ADAM_EPS) + ADAM_WD * w_ref[...])


def _adamw_many(ws, gs, ms, vs, *, name):
    n = len(ws)

    def body(*refs):
        for i in range(n):
            _adamw_update(*[refs[k * n + i] for k in range(7)])

    vm = pl.BlockSpec(memory_space=pltpu.VMEM)
    shapes = [jax.ShapeDtypeStruct(a.shape, F32) for a in ws]
    outs = pl.pallas_call(
        body, in_specs=[vm] * (4 * n), out_specs=[vm] * (3 * n), out_shape=shapes * 3,
        compiler_params=pltpu.CompilerParams(vmem_limit_bytes=VMEM_LIMIT_BYTES), name=name)(*ws, *gs, *ms, *vs)
    return list(outs[:n]), list(outs[n:2 * n]), list(outs[2 * n:])


def _adamw(w, g, m, v, *, name, after=None):
    rows, cols = w.shape
    tr = _row_tile(rows, cols, 7)

    def body(w_ref, g_ref, m_ref, v_ref, *rest):
        _adamw_update(w_ref, g_ref, m_ref, v_ref, *rest[-3:])

    row = pl.BlockSpec((tr, cols), lambda i: (i, 0))
    shp = jax.ShapeDtypeStruct((rows, cols), F32)
    extra = [] if after is None else [after]
    outs, _ = _call(
        body, grid=(rows // tr,), in_specs=[row] * 4 + [pl.BlockSpec(memory_space=pl.ANY)] * len(extra),
        out_specs=[row] * 3, out_shape=[shp] * 3, scratch_shapes=[], semantics=("parallel",), name=name,
        args=[w, g, m, v] + extra)
    return outs


def _sum_leading(a, *, name, out_dtype=F32):
    n, rows, cols = a.shape
    tr = _row_tile(rows, cols, n + 1)

    def body(a_ref, o_ref):
        acc = a_ref[0].astype(F32)
        for i in range(1, n):
            acc = acc + a_ref[i].astype(F32)
        o_ref[...] = acc.astype(o_ref.dtype)

    return pl.pallas_call(
        body, grid=(rows // tr,), in_specs=[pl.BlockSpec((n, tr, cols), lambda i: (0, i, 0))],
        out_specs=pl.BlockSpec((tr, cols), lambda i: (i, 0)),
        out_shape=jax.ShapeDtypeStruct((rows, cols), out_dtype),
        compiler_params=_cparams(("parallel",)), name=name)(a)


def _add_half(dw, land, my_c, *, name):
    n, r, cols = dw.shape
    h = r // 2
    tr = _row_tile(h, cols, 3)
    hb = h // tr

    def body(c_ref, a_ref, b_ref, o_ref):
        o_ref[...] = (a_ref[...].astype(F32) + b_ref[...].astype(F32)).astype(o_ref.dtype)

    gs = pltpu.PrefetchScalarGridSpec(
        num_scalar_prefetch=1, grid=(n, hb),
        in_specs=[pl.BlockSpec((None, tr, cols), lambda s, i, c_ref: (s, c_ref[0] * hb + i, 0)),
                  pl.BlockSpec((None, tr, cols), lambda s, i, c_ref: (s, i, 0))],
        out_specs=pl.BlockSpec((None, tr, cols), lambda s, i, c_ref: (s, i, 0)))
    return pl.pallas_call(
        body, grid_spec=gs, out_shape=jax.ShapeDtypeStruct((n, h, cols), BF16),
        compiler_params=_cparams(("parallel", "parallel")), name=name)(my_c, dw, land)


def _mesh_pos():
    return lax.axis_index("x"), lax.axis_index("y"), lax.axis_index("c")


def _other_chips(x, y):
    return [(1 - x, y), (x, 1 - y), (1 - x, 1 - y)]


def _gather_small(blk, *, name):
    m_per, n = blk.shape

    def body(x_ref, out_ref, send_sems, recv_sems, local_sem):
        x, y, c = _mesh_pos()
        me, sibling = (x, y, c), (x, y, 1 - c)
        chips = _other_chips(x, y)

        def rows(px, py, pc):
            return out_ref.at[pl.ds((4 * px + 2 * py + pc) * m_per, m_per), :]

        def copy(k, block, to, src=None):
            return pltpu.make_async_remote_copy(
                src_ref=rows(*block) if src is None else src, dst_ref=rows(*block),
                send_sem=send_sems.at[k], recv_sem=recv_sems.at[k], device_id=to, device_id_type=MESH)

        mine = pltpu.make_async_copy(x_ref, rows(*me), local_sem)
        mine.start()
        first = [copy(0, me, sibling, src=x_ref)]
        first += [copy(1 + j, me, (*chip, c), src=x_ref) for j, chip in enumerate(chips)]
        for cp in first:
            cp.start()
        passed = [copy(4 + j, (*chip, c), sibling) for j, chip in enumerate(chips)]
        for j, chip in enumerate(chips):
            copy(1 + j, (*chip, c), me).wait_recv()
            passed[j].start()
        copy(0, sibling, me).wait_recv()
        for j, chip in enumerate(chips):
            copy(4 + j, (*chip, 1 - c), me).wait_recv()
        for cp in first + passed:
            cp.wait_send()
        mine.wait()

    return pl.pallas_call(
        body, out_shape=jax.ShapeDtypeStruct((N_DEV * m_per, n), blk.dtype),
        in_specs=[pl.BlockSpec(memory_space=pltpu.VMEM)], out_specs=pl.BlockSpec(memory_space=pltpu.VMEM),
        scratch_shapes=[pltpu.SemaphoreType.DMA((7,)), pltpu.SemaphoreType.DMA((7,)), pltpu.SemaphoreType.DMA],
        compiler_params=pltpu.CompilerParams(vmem_limit_bytes=VMEM_LIMIT_BYTES), name=name)(blk)


def _hbm_specs(n):
    return [pl.BlockSpec(memory_space=pl.ANY)] * n


def _gather_weights(shards):
    n = len(shards)

    def body(*refs):
        ins, outs = refs[:n], refs[n:2 * n]
        send_sems, recv_sems, local_sems = refs[2 * n:]
        x, y, c = _mesh_pos()
        me_chip = 2 * x + y
        sibling = (x, y, 1 - c)
        chips = _other_chips(x, y)

        def half(w, chip_idx, pc):
            h = shards[w].shape[0] // 2
            return outs[w].at[chip_idx, pl.ds(pc * h, h), :]

        def copy(w, k, chip_idx, pc, to, src=None):
            dst = half(w, chip_idx, pc)
            return pltpu.make_async_remote_copy(
                src_ref=dst if src is None else src, dst_ref=dst,
                send_sem=send_sems.at[6 * w + k], recv_sem=recv_sems.at[6 * w + k],
                device_id=to, device_id_type=MESH)

        local = [pltpu.make_async_copy(ins[w], outs[w].at[me_chip], local_sems.at[w]) for w in range(n)]
        for cp in local:
            cp.start()
        sends = []
        for w in range(n):
            h = shards[w].shape[0] // 2
            for j, chip in enumerate(chips):
                cp = copy(w, j, me_chip, c, (*chip, c), src=ins[w].at[pl.ds(c * h, h), :])
                cp.start()
                sends.append(cp)
        for w in range(n):
            for j, chip in enumerate(chips):
                chip_idx = 2 * chip[0] + chip[1]
                copy(w, j, chip_idx, c, (x, y, c)).wait_recv()
                cp = copy(w, 3 + j, chip_idx, c, sibling)
                cp.start()
                sends.append(cp)
        for w in range(n):
            for j, chip in enumerate(chips):
                copy(w, 3 + j, 2 * chip[0] + chip[1], 1 - c, (x, y, c)).wait_recv()
        for cp in sends:
            cp.wait_send()
        for cp in local:
            cp.wait()

    return pl.pallas_call(
        body, out_shape=[jax.ShapeDtypeStruct((N_CHIPS,) + s.shape, s.dtype) for s in shards],
        in_specs=_hbm_specs(n), out_specs=_hbm_specs(n),
        scratch_shapes=[pltpu.SemaphoreType.DMA((6 * n,)), pltpu.SemaphoreType.DMA((6 * n,)),
                        pltpu.SemaphoreType.DMA((n,))],
        name="gather_weights")(*shards)


def _swap_halves(dws, *, name):
    n = len(dws)

    def body(*refs):
        ins, outs = refs[:n], refs[n:2 * n]
        send_sems, recv_sems = refs[2 * n:]
        x, y, c = _mesh_pos()
        cps = []
        for w in range(n):
            h = dws[w].shape[1] // 2
            cp = pltpu.make_async_remote_copy(
                src_ref=ins[w].at[:, pl.ds((1 - c) * h, h), :], dst_ref=outs[w],
                send_sem=send_sems.at[w], recv_sem=recv_sems.at[w],
                device_id=(x, y, 1 - c), device_id_type=MESH)
            cp.start()
            cps.append(cp)
        for cp in cps:
            cp.wait()

    return pl.pallas_call(
        body, out_shape=[jax.ShapeDtypeStruct((s.shape[0], s.shape[1] // 2, s.shape[2]), s.dtype) for s in dws],
        in_specs=_hbm_specs(n), out_specs=_hbm_specs(n),
        scratch_shapes=[pltpu.SemaphoreType.DMA((n,)), pltpu.SemaphoreType.DMA((n,))],
        name=name)(*dws)


def _chip_exchange(parts):
    n = len(parts)

    def body(*refs):
        ins, outs = refs[:n], refs[n:2 * n]
        send_sems, recv_sems, local_sems = refs[2 * n:]
        x, y, c = _mesh_pos()
        me_chip = 2 * x + y
        chips = _other_chips(x, y)
        local = [pltpu.make_async_copy(ins[w].at[me_chip], outs[w].at[me_chip], local_sems.at[w]) for w in range(n)]
        for cp in local:
            cp.start()
        cps = []
        for w in range(n):
            for j, chip in enumerate(chips):
                cp = pltpu.make_async_remote_copy(
                    src_ref=ins[w].at[2 * chip[0] + chip[1]], dst_ref=outs[w].at[me_chip],
                    send_sem=send_sems.at[3 * w + j], recv_sem=recv_sems.at[3 * w + j],
                    device_id=(*chip, c), device_id_type=MESH)
                cp.start()
                cps.append((cp, w, j, chip))
        for cp, w, j, chip in cps:
            slot = outs[w].at[2 * chip[0] + chip[1]]
            pltpu.make_async_remote_copy(
                src_ref=slot, dst_ref=slot, send_sem=send_sems.at[3 * w + j], recv_sem=recv_sems.at[3 * w + j],
                device_id=(x, y, c), device_id_type=MESH).wait_recv()
        for cp, _, _, _ in cps:
            cp.wait_send()
        for cp in local:
            cp.wait()

    return pl.pallas_call(
        body, out_shape=[jax.ShapeDtypeStruct(s.shape, s.dtype) for s in parts],
        in_specs=_hbm_specs(n), out_specs=_hbm_specs(n),
        scratch_shapes=[pltpu.SemaphoreType.DMA((3 * n,)), pltpu.SemaphoreType.DMA((3 * n,)),
                        pltpu.SemaphoreType.DMA((n,))],
        name="chip_exchange")(*parts)


def _join_halves(halves):
    n = len(halves)

    def body(*refs):
        ins, outs = refs[:n], refs[n:2 * n]
        send_sems, recv_sems, local_sems = refs[2 * n:]
        x, y, c = _mesh_pos()
        cps, local = [], []
        for w in range(n):
            h = halves[w].shape[0]
            mine = outs[w].at[pl.ds(c * h, h), :]
            lc = pltpu.make_async_copy(ins[w], mine, local_sems.at[w])
            lc.start()
            local.append(lc)
            cp = pltpu.make_async_remote_copy(
                src_ref=ins[w], dst_ref=mine, send_sem=send_sems.at[w], recv_sem=recv_sems.at[w],
                device_id=(x, y, 1 - c), device_id_type=MESH)
            cp.start()
            cps.append(cp)
        for w in range(n):
            h = halves[w].shape[0]
            theirs = outs[w].at[pl.ds((1 - c) * h, h), :]
            pltpu.make_async_remote_copy(
                src_ref=theirs, dst_ref=theirs, send_sem=send_sems.at[w], recv_sem=recv_sems.at[w],
                device_id=(x, y, c), device_id_type=MESH).wait_recv()
        for cp in cps:
            cp.wait_send()
        for lc in local:
            lc.wait()

    return pl.pallas_call(
        body, out_shape=[jax.ShapeDtypeStruct((2 * s.shape[0], s.shape[1]), s.dtype) for s in halves],
        in_specs=_hbm_specs(n), out_specs=_hbm_specs(n),
        scratch_shapes=[pltpu.SemaphoreType.DMA((n,)), pltpu.SemaphoreType.DMA((n,)), pltpu.SemaphoreType.DMA((n,))],
        name="join_halves")(*halves)


def _cast_into_slot(w, chip, after, *, name):
    rows, cols = w.shape
    tr = _row_tile(rows, cols, 2)

    def body(chip_ref, w_ref, after_ref, o_ref):
        o_ref[...] = w_ref[...].astype(BF16)

    gs = pltpu.PrefetchScalarGridSpec(
        num_scalar_prefetch=1, grid=(rows // tr,),
        in_specs=[pl.BlockSpec((tr, cols), lambda i, chip_ref: (i, 0)), pl.BlockSpec(memory_space=pl.ANY)],
        out_specs=pl.BlockSpec((None, tr, cols), lambda i, chip_ref: (chip_ref[0], i, 0)))
    return pl.pallas_call(
        body, grid_spec=gs, out_shape=jax.ShapeDtypeStruct((N_CHIPS, rows, cols), BF16),
        compiler_params=_cparams(("parallel",)), name=name)(chip, w, after)


def _row_range(h, lo, hi, parts):
    step = h // parts
    assert step * parts == h and step % (2 * SUBLANES) == 0, (h, parts)
    return lo * step, (hi - lo) * step


def _gather_carry(items):
    n_copies = sum(len(js) for _, js, _, _, _ in items)
    sem = pltpu.SemaphoreType.DMA((2 * n_copies,))

    def copies(outs, sems):
        send_sems, recv_sems = sems
        x, y, c = _mesh_pos()
        me_chip = 2 * x + y
        chips = _other_chips(x, y)
        out_ici, in_ici, out_d2d, in_d2d = [], [], [], []
        k = 0
        for w, (buf, js, lo, hi, parts) in enumerate(items):
            h = buf.shape[1] // 2
            r0, nr = _row_range(h, lo, hi, parts)

            def copy(k, chip_idx, pc, to):
                ref = outs[w].at[chip_idx, pl.ds(pc * h + r0, nr), :]
                return pltpu.make_async_remote_copy(
                    src_ref=ref, dst_ref=ref, send_sem=send_sems.at[k], recv_sem=recv_sems.at[k],
                    device_id=to, device_id_type=MESH)

            for j in js:
                chip = chips[j]
                chip_idx = 2 * chip[0] + chip[1]
                out_ici.append(copy(k, me_chip, c, (*chip, c)))
                in_ici.append(copy(k, chip_idx, c, (x, y, c)))
                out_d2d.append(copy(k + 1, chip_idx, c, (x, y, 1 - c)))
                in_d2d.append(copy(k + 1, chip_idx, 1 - c, (x, y, c)))
                k += 2
        return out_ici, in_ici, out_d2d, in_d2d

    def start(ins, outs, sems):
        for cp in copies(outs, sems)[0]:
            cp.start()

    def finish(ins, outs, sems):
        out_ici, in_ici, out_d2d, in_d2d = copies(outs, sems)
        for arrived, onward in zip(in_ici, out_d2d):
            arrived.wait_recv()
            onward.start()
        for arrived in in_d2d:
            arrived.wait_recv()
        for cp in out_ici + out_d2d:
            cp.wait_send()

    bufs = [it[0] for it in items]
    shapes = [jax.ShapeDtypeStruct(b.shape, b.dtype) for b in bufs]
    return _Carry(bufs, shapes, {i: i for i in range(len(bufs))}, [sem, sem], start, finish)


def _exchange_carry(items):
    n = len(items)
    sem = pltpu.SemaphoreType.DMA((3 * n,))
    given = [w for w in range(n) if items[w][1] is not None]

    def copies(ins, outs, sems):
        send_sems, recv_sems = sems
        x, y, c = _mesh_pos()
        chips = _other_chips(x, y)
        sends, recvs = [], []
        for w, (part, _, lo, hi, parts) in enumerate(items):
            r0, nr = _row_range(part.shape[1], lo, hi, parts)
            for j, chip in enumerate(chips):
                land = outs[w].at[j, pl.ds(r0, nr), :]
                sends.append(pltpu.make_async_remote_copy(
                    src_ref=ins[w].at[2 * chip[0] + chip[1], pl.ds(r0, nr), :], dst_ref=land,
                    send_sem=send_sems.at[3 * w + j], recv_sem=recv_sems.at[3 * w + j],
                    device_id=(*chip, c), device_id_type=MESH))
                recvs.append(pltpu.make_async_remote_copy(
                    src_ref=land, dst_ref=land,
                    send_sem=send_sems.at[3 * w + j], recv_sem=recv_sems.at[3 * w + j],
                    device_id=(x, y, c), device_id_type=MESH))
        return sends, recvs

    def start(ins, outs, sems):
        for cp in copies(ins, outs, sems)[0]:
            cp.start()

    def finish(ins, outs, sems):
        sends, recvs = copies(ins, outs, sems)
        for cp in recvs:
            cp.wait_recv()
        for cp in sends:
            cp.wait_send()

    inputs = [it[0] for it in items] + [items[w][1] for w in given]
    shapes = [jax.ShapeDtypeStruct((3,) + it[0].shape[1:], it[0].dtype) for it in items]
    aliases = {n + i: w for i, w in enumerate(given)}
    return _Carry(inputs, shapes, aliases, [sem, sem], start, finish)


def _sum_into_half(part, landed, chip, my_c, *, name):
    _, h, cols = part.shape
    tr = _row_tile(h, cols, 5)
    hb = h // tr

    def body(chip_ref, c_ref, p_ref, l_ref, o_ref):
        acc = p_ref[...].astype(F32)
        for j in range(3):
            acc = acc + l_ref[j].astype(F32)
        o_ref[...] = acc

    gs = pltpu.PrefetchScalarGridSpec(
        num_scalar_prefetch=2, grid=(hb,),
        in_specs=[pl.BlockSpec((None, tr, cols), lambda i, chip_ref, c_ref: (chip_ref[0], i, 0)),
                  pl.BlockSpec((3, tr, cols), lambda i, chip_ref, c_ref: (0, i, 0))],
        out_specs=pl.BlockSpec((tr, cols), lambda i, chip_ref, c_ref: (c_ref[0] * hb + i, 0)))
    return pl.pallas_call(
        body, grid_spec=gs, out_shape=jax.ShapeDtypeStruct((2 * h, cols), F32),
        compiler_params=_cparams(("parallel",)), name=name)(chip, my_c, part, landed)


def _join_carry(fulls):
    n = len(fulls)
    sem = pltpu.SemaphoreType.DMA((n,))

    def copies(outs, sems):
        send_sems, recv_sems = sems
        x, y, c = _mesh_pos()
        sends, recvs = [], []
        for w in range(n):
            h = fulls[w].shape[0] // 2
            mine = outs[w].at[pl.ds(c * h, h), :]
            theirs = outs[w].at[pl.ds((1 - c) * h, h), :]
            sends.append(pltpu.make_async_remote_copy(
                src_ref=mine, dst_ref=mine, send_sem=send_sems.at[w], recv_sem=recv_sems.at[w],
                device_id=(x, y, 1 - c), device_id_type=MESH))
            recvs.append(pltpu.make_async_remote_copy(
                src_ref=theirs, dst_ref=theirs, send_sem=send_sems.at[w], recv_sem=recv_sems.at[w],
                device_id=(x, y, c), device_id_type=MESH))
        return sends, recvs

    def start(ins, outs, sems):
        for cp in copies(outs, sems)[0]:
            cp.start()

    def finish(ins, outs, sems):
        sends, recvs = copies(outs, sems)
        for cp in recvs:
            cp.wait_recv()
        for cp in sends:
            cp.wait_send()

    shapes = [jax.ShapeDtypeStruct(f.shape, f.dtype) for f in fulls]
    return _Carry(fulls, shapes, {i: i for i in range(n)}, [sem, sem], start, finish)


class _NoComm:
    def __init__(self, big):
        self.big = big
        self.grads = {}

    def weight(self, name):
        return self.big[name]

    def mm_in(self, u, afters):
        return _mm(u, self.big["w_in"], mode="nn", out_dtype=BF16, name="mm_in")

    def mm_d_in(self, dproj):
        return _mm(dproj, self.big["w_in"], mode="nt", out_dtype=F32, name="mm_d_in")

    def carry(self, site, args=()):
        return None

    def done(self, site, carried, out=None):
        return out

    def grad(self, name, dw):
        self.grads[name] = dw

    def early_grads(self, early):
        self.early = early


def _gather_rows_carry(blk):
    m_per = blk.shape[0]
    sem = pltpu.SemaphoreType.DMA((7,))

    def copies(ins, outs, sems):
        send_sems, recv_sems, local_sem = sems
        x, y, c = _mesh_pos()
        me, sibling = (x, y, c), (x, y, 1 - c)
        chips = _other_chips(x, y)

        def rows(px, py, pc):
            return outs[0].at[pl.ds((4 * px + 2 * py + pc) * m_per, m_per), :]

        def copy(k, block, to, src=None):
            return pltpu.make_async_remote_copy(
                src_ref=rows(*block) if src is None else src, dst_ref=rows(*block),
                send_sem=send_sems.at[k], recv_sem=recv_sems.at[k], device_id=to, device_id_type=MESH)

        mine = pltpu.make_async_copy(ins[0], rows(*me), local_sem.at[0])
        first = [copy(0, me, sibling, src=ins[0])]
        first += [copy(1 + j, me, (*chip, c), src=ins[0]) for j, chip in enumerate(chips)]
        passed = [copy(4 + j, (*chip, c), sibling) for j, chip in enumerate(chips)]
        landed = [copy(1 + j, (*chip, c), me) for j, chip in enumerate(chips)]
        from_sibling = [copy(0, sibling, me)] + [copy(4 + j, (*chip, 1 - c), me) for j, chip in enumerate(chips)]
        return mine, first, passed, landed, from_sibling

    def start(ins, outs, sems):
        mine, first, _, _, _ = copies(ins, outs, sems)
        mine.start()
        for cp in first:
            cp.start()

    def finish(ins, outs, sems):
        mine, first, passed, landed, from_sibling = copies(ins, outs, sems)
        for arrived, onward in zip(landed, passed):
            arrived.wait_recv()
            onward.start()
        for arrived in from_sibling:
            arrived.wait_recv()
        for cp in first + passed:
            cp.wait_send()
        mine.wait()

    shape = jax.ShapeDtypeStruct((N_DEV * m_per, blk.shape[1]), blk.dtype)
    return _Carry([blk], [shape], {}, [sem, sem, pltpu.SemaphoreType.DMA((1,))], start, finish)


def _gather_fresh_carry(own, js):
    n = len(js)
    h = own.shape[0] // 2
    sem = pltpu.SemaphoreType.DMA((2 * n,))

    def copies(ins, outs, sems):
        send_sems, recv_sems = sems
        x, y, c = _mesh_pos()
        chips = _other_chips(x, y)
        out_ici, in_ici, out_d2d, in_d2d = [], [], [], []

        def copy(k, src, dst, to):
            return pltpu.make_async_remote_copy(
                src_ref=src, dst_ref=dst, send_sem=send_sems.at[k], recv_sem=recv_sems.at[k],
                device_id=to, device_id_type=MESH)

        for jj, j in enumerate(js):
            mine = ins[0].at[pl.ds(c * h, h), :]
            land = outs[0].at[jj, pl.ds(c * h, h), :]
            other = outs[0].at[jj, pl.ds((1 - c) * h, h), :]
            out_ici.append(copy(2 * jj, mine, land, (*chips[j], c)))
            in_ici.append(copy(2 * jj, land, land, (x, y, c)))
            out_d2d.append(copy(2 * jj + 1, land, land, (x, y, 1 - c)))
            in_d2d.append(copy(2 * jj + 1, other, other, (x, y, c)))
        return out_ici, in_ici, out_d2d, in_d2d

    def start(ins, outs, sems):
        for cp in copies(ins, outs, sems)[0]:
            cp.start()

    def finish(ins, outs, sems):
        out_ici, in_ici, out_d2d, in_d2d = copies(ins, outs, sems)
        for arrived, onward in zip(in_ici, out_d2d):
            arrived.wait_recv()
            onward.start()
        for arrived in in_d2d:
            arrived.wait_recv()
        for cp in out_ici + out_d2d:
            cp.wait_send()

    return _Carry([own], [jax.ShapeDtypeStruct((n,) + own.shape, own.dtype)], {}, [sem, sem], start, finish)


def _w_in_copies(own_ref, land_ref, send_sems, recv_sems):
    x, y, c = _mesh_pos()
    h = own_ref.shape[0] // 2
    return [pltpu.make_async_remote_copy(
        src_ref=own_ref.at[pl.ds(c * h, h), :], dst_ref=land_ref.at[1 + j, pl.ds(c * h, h), :],
        send_sem=send_sems[j], recv_sem=recv_sems[j], device_id=(*chip, c), device_id_type=MESH)
        for j, chip in enumerate(_other_chips(x, y))]


def _w_in_send(own, land, after):
    hbm = pl.BlockSpec(memory_space=pltpu.HBM)
    sem = pl.BlockSpec(memory_space=pltpu.SEMAPHORE)
    land_shape = land.shape

    def body(own_ref, land_ref, after_ref, s0, s1, s2, r0, r1, r2, own_thru, land_thru, token):
        for cp in _w_in_copies(own_ref, land_ref, (s0, s1, s2), (r0, r1, r2)):
            cp.start()
        token[...] = jnp.zeros_like(token)

    outs = pl.pallas_call(
        body, name="w_in_send",
        out_shape=(pltpu.SemaphoreType.DMA(()),) * 6 + (
            pltpu.HBM(own.shape, own.dtype), pltpu.HBM(land_shape, own.dtype), jax.ShapeDtypeStruct((8, LANES), F32)),
        in_specs=(hbm, hbm, pl.BlockSpec(memory_space=pl.ANY)),
        out_specs=(sem,) * 6 + (hbm, hbm, pl.BlockSpec(memory_space=pltpu.VMEM)),
        input_output_aliases={0: 6, 1: 7},
        compiler_params=pltpu.CompilerParams(has_side_effects=pltpu.SideEffectType.DATAFLOW_SIDE_EFFECTING),
    )(pltpu.with_memory_space_constraint(own, pltpu.HBM), pltpu.with_memory_space_constraint(land, pltpu.HBM), after)
    return outs[:6], outs[6], outs[7], outs[8]


def _w_in_wait(sems, own, land, afters):
    hbm = pl.BlockSpec(memory_space=pltpu.HBM)
    sem = pl.BlockSpec(memory_space=pltpu.SEMAPHORE)
    n_after = len(afters)

    def body(own_ref, land_ref, s0, s1, s2, r0, r1, r2, *rest):
        for cp in _w_in_copies(own_ref, land_ref, (s0, s1, s2), (r0, r1, r2)):
            cp.wait_send()
            cp.wait_recv()

    return pl.pallas_call(
        body, name="w_in_wait", out_shape=(pltpu.HBM(own.shape, own.dtype), pltpu.HBM(land.shape, land.dtype)),
        in_specs=(hbm, hbm) + (sem,) * 6 + (pl.BlockSpec(memory_space=pl.ANY),) * n_after, out_specs=(hbm, hbm),
        input_output_aliases={0: 0, 1: 1},
        compiler_params=pltpu.CompilerParams(has_side_effects=pltpu.SideEffectType.DATAFLOW_SIDE_EFFECTING),
    )(own, land, *sems, *afters)


def _exchange_copies(part_refs, land_refs, send_sems, recv_sems):
    x, y, c = _mesh_pos()
    cps = []
    for w, (part, land) in enumerate(zip(part_refs, land_refs)):
        for j, chip in enumerate(_other_chips(x, y)):
            cps.append(pltpu.make_async_remote_copy(
                src_ref=part.at[2 * chip[0] + chip[1]], dst_ref=land.at[j],
                send_sem=send_sems[3 * w + j], recv_sem=recv_sems[3 * w + j],
                device_id=(*chip, c), device_id_type=MESH))
    return cps


def _exchange_send(parts, through, *, name):
    n = len(parts)
    hbm = pl.BlockSpec(memory_space=pltpu.HBM)
    sem = pl.BlockSpec(memory_space=pltpu.SEMAPHORE)
    any_spec = pl.BlockSpec(memory_space=pl.ANY)
    land_shapes = [(3,) + p.shape[1:] for p in parts]

    def body(*refs):
        part_refs, land_refs = refs[:n], refs[n:2 * n]
        sems = refs[2 * n + 1:8 * n + 1]
        for cp in _exchange_copies(part_refs, land_refs, sems[:3 * n], sems[3 * n:]):
            cp.start()

    outs = pl.pallas_call(
        body, name=name,
        out_shape=(pltpu.SemaphoreType.DMA(()),) * (6 * n)
        + tuple(pltpu.HBM(p.shape, p.dtype) for p in parts)
        + tuple(pltpu.HBM(s, p.dtype) for s, p in zip(land_shapes, parts))
        + (jax.ShapeDtypeStruct(through.shape, through.dtype),),
        in_specs=(hbm,) * (2 * n) + (any_spec,), out_specs=(sem,) * (6 * n) + (hbm,) * (2 * n) + (any_spec,),
        input_output_aliases={i: 6 * n + i for i in range(2 * n + 1)},
        compiler_params=pltpu.CompilerParams(has_side_effects=pltpu.SideEffectType.DATAFLOW_SIDE_EFFECTING),
    )(*[pltpu.with_memory_space_constraint(p, pltpu.HBM) for p in parts],
      *[pltpu.with_memory_space_constraint(lax.empty(s, p.dtype), pltpu.HBM) for s, p in zip(land_shapes, parts)],
      through)
    return outs[:6 * n], outs[6 * n:7 * n], outs[7 * n:8 * n], outs[8 * n]


def _exchange_wait(sems, parts, lands, afters, *, name):
    n = len(parts)
    hbm = pl.BlockSpec(memory_space=pltpu.HBM)
    sem = pl.BlockSpec(memory_space=pltpu.SEMAPHORE)

    def body(*refs):
        part_refs, land_refs = refs[:n], refs[n:2 * n]
        sem_refs = refs[2 * n:8 * n]
        for cp in _exchange_copies(part_refs, land_refs, sem_refs[:3 * n], sem_refs[3 * n:]):
            cp.wait_send()
            cp.wait_recv()

    outs = pl.pallas_call(
        body, name=name,
        out_shape=tuple(pltpu.HBM(p.shape, p.dtype) for p in parts) + tuple(pltpu.HBM(l.shape, l.dtype) for l in lands),
        in_specs=(hbm,) * (2 * n) + (sem,) * (6 * n) + (pl.BlockSpec(memory_space=pl.ANY),) * len(afters),
        out_specs=(hbm,) * (2 * n), input_output_aliases={i: i for i in range(2 * n)},
        compiler_params=pltpu.CompilerParams(has_side_effects=pltpu.SideEffectType.DATAFLOW_SIDE_EFFECTING),
    )(*parts, *lands, *sems, *afters)
    return outs[:n], outs[n:]


def _join_copies(full_refs, send_sems, recv_sems):
    x, y, c = _mesh_pos()
    cps = []
    for w, full in enumerate(full_refs):
        h = full.shape[0] // 2
        mine = full.at[pl.ds(c * h, h), :]
        cps.append(pltpu.make_async_remote_copy(
            src_ref=mine, dst_ref=mine, send_sem=send_sems[w], recv_sem=recv_sems[w],
            device_id=(x, y, 1 - c), device_id_type=MESH))
    return cps


def _join_send(fulls, *, name):
    n = len(fulls)
    hbm = pl.BlockSpec(memory_space=pltpu.HBM)
    sem = pl.BlockSpec(memory_space=pltpu.SEMAPHORE)

    def body(*refs):
        sems = refs[n:3 * n]
        for cp in _join_copies(refs[:n], sems[:n], sems[n:]):
            cp.start()
        token = refs[-1]
        token[...] = jnp.zeros_like(token)

    outs = pl.pallas_call(
        body, name=name,
        out_shape=(pltpu.SemaphoreType.DMA(()),) * (2 * n) + tuple(pltpu.HBM(f.shape, f.dtype) for f in fulls)
        + (jax.ShapeDtypeStruct((SUBLANES, LANES), F32),),
        in_specs=(hbm,) * n,
        out_specs=(sem,) * (2 * n) + (hbm,) * n + (pl.BlockSpec(memory_space=pltpu.VMEM),),
        input_output_aliases={i: 2 * n + i for i in range(n)},
        compiler_params=pltpu.CompilerParams(has_side_effects=pltpu.SideEffectType.DATAFLOW_SIDE_EFFECTING),
    )(*[pltpu.with_memory_space_constraint(f, pltpu.HBM) for f in fulls])
    return outs[:2 * n], list(outs[2 * n:3 * n]), outs[3 * n]


def _join_wait(sems, fulls, afters, *, name):
    n = len(fulls)
    hbm = pl.BlockSpec(memory_space=pltpu.HBM)
    sem = pl.BlockSpec(memory_space=pltpu.SEMAPHORE)

    def body(*refs):
        sem_refs = refs[n:3 * n]
        for cp in _join_copies(refs[:n], sem_refs[:n], sem_refs[n:]):
            cp.wait_send()
            cp.wait_recv()

    outs = pl.pallas_call(
        body, name=name, out_shape=tuple(pltpu.HBM(f.shape, f.dtype) for f in fulls),
        in_specs=(hbm,) * n + (sem,) * (2 * n) + (pl.BlockSpec(memory_space=pl.ANY),) * len(afters),
        out_specs=(hbm,) * n, input_output_aliases={i: i for i in range(n)},
        compiler_params=pltpu.CompilerParams(has_side_effects=pltpu.SideEffectType.DATAFLOW_SIDE_EFFECTING),
    )(*fulls, *sems, *afters)
    return list(outs)


def _gather_ici_copies(buf_refs, send_sems, recv_sems):
    x, y, c = _mesh_pos()
    me_chip = 2 * x + y
    cps = []
    for w, buf in enumerate(buf_refs):
        h = buf.shape[1] // 2
        ref = buf.at[me_chip, pl.ds(c * h, h), :]
        for j, chip in enumerate(_other_chips(x, y)):
            cps.append(pltpu.make_async_remote_copy(
                src_ref=ref, dst_ref=ref, send_sem=send_sems[3 * w + j], recv_sem=recv_sems[3 * w + j],
                device_id=(*chip, c), device_id_type=MESH))
    return cps


def _gather_send(bufs, *, name):
    n = len(bufs)
    hbm = pl.BlockSpec(memory_space=pltpu.HBM)
    sem = pl.BlockSpec(memory_space=pltpu.SEMAPHORE)

    def body(*refs):
        sems = refs[n:7 * n]
        for cp in _gather_ici_copies(refs[:n], sems[:3 * n], sems[3 * n:]):
            cp.start()

    outs = pl.pallas_call(
        body, name=name,
        out_shape=(pltpu.SemaphoreType.DMA(()),) * (6 * n) + tuple(pltpu.HBM(b.shape, b.dtype) for b in bufs),
        in_specs=(hbm,) * n, out_specs=(sem,) * (6 * n) + (hbm,) * n,
        input_output_aliases={i: 6 * n + i for i in range(n)},
        compiler_params=pltpu.CompilerParams(has_side_effects=pltpu.SideEffectType.DATAFLOW_SIDE_EFFECTING),
    )(*[pltpu.with_memory_space_constraint(b, pltpu.HBM) for b in bufs])
    send_sems, recv_sems = outs[:3 * n], outs[3 * n:6 * n]
    per_buf = [tuple(send_sems[3 * w:3 * w + 3]) + tuple(recv_sems[3 * w:3 * w + 3]) for w in range(n)]
    return per_buf, list(outs[6 * n:])


def _gather_wait(sems, bufs, afters, *, name):
    n = len(bufs)
    hbm = pl.BlockSpec(memory_space=pltpu.HBM)
    sem = pl.BlockSpec(memory_space=pltpu.SEMAPHORE)
    flat = [s for six in sems for s in six[:3]] + [s for six in sems for s in six[3:]]

    def body(*refs):
        sem_refs = refs[n:7 * n]
        for cp in _gather_ici_copies(refs[:n], sem_refs[:3 * n], sem_refs[3 * n:]):
            cp.wait_send()
            cp.wait_recv()

    outs = pl.pallas_call(
        body, name=name, out_shape=tuple(pltpu.HBM(b.shape, b.dtype) for b in bufs),
        in_specs=(hbm,) * n + (sem,) * (6 * n) + (pl.BlockSpec(memory_space=pl.ANY),) * len(afters),
        out_specs=(hbm,) * n, input_output_aliases={i: i for i in range(n)},
        compiler_params=pltpu.CompilerParams(has_side_effects=pltpu.SideEffectType.DATAFLOW_SIDE_EFFECTING),
    )(*bufs, *flat, *afters)
    return list(outs)


def _forward_abs_carry(bufs):
    n = len(bufs)
    sem = pltpu.SemaphoreType.DMA((3 * n,))

    def copies(outs, sems):
        send_sems, recv_sems = sems
        x, y, c = _mesh_pos()
        sends, recvs = [], []
        for w in range(n):
            h = bufs[w].shape[1] // 2
            for j, chip in enumerate(_other_chips(x, y)):
                slot = 2 * chip[0] + chip[1]
                mine = outs[w].at[slot, pl.ds(c * h, h), :]
                other = outs[w].at[slot, pl.ds((1 - c) * h, h), :]
                sends.append(pltpu.make_async_remote_copy(
                    src_ref=mine, dst_ref=mine, send_sem=send_sems.at[3 * w + j], recv_sem=recv_sems.at[3 * w + j],
                    device_id=(x, y, 1 - c), device_id_type=MESH))
                recvs.append(pltpu.make_async_remote_copy(
                    src_ref=other, dst_ref=other, send_sem=send_sems.at[3 * w + j], recv_sem=recv_sems.at[3 * w + j],
                    device_id=(x, y, c), device_id_type=MESH))
        return sends, recvs

    def start(ins, outs, sems):
        for cp in copies(outs, sems)[0]:
            cp.start()

    def finish(ins, outs, sems):
        sends, recvs = copies(outs, sems)
        for cp in recvs:
            cp.wait_recv()
        for cp in sends:
            cp.wait_send()

    shapes = [jax.ShapeDtypeStruct(b.shape, b.dtype) for b in bufs]
    return _Carry(bufs, shapes, {i: i for i in range(n)}, [sem, sem], start, finish)


def _forward_carry(land):
    n = land.shape[0] - 1
    h = land.shape[1] // 2
    sem = pltpu.SemaphoreType.DMA((n,))

    def copies(outs, sems):
        send_sems, recv_sems = sems
        x, y, c = _mesh_pos()
        sends, recvs = [], []
        for j in range(n):
            mine = outs[0].at[1 + j, pl.ds(c * h, h), :]
            other = outs[0].at[1 + j, pl.ds((1 - c) * h, h), :]
            sends.append(pltpu.make_async_remote_copy(
                src_ref=mine, dst_ref=mine, send_sem=send_sems.at[j], recv_sem=recv_sems.at[j],
                device_id=(x, y, 1 - c), device_id_type=MESH))
            recvs.append(pltpu.make_async_remote_copy(
                src_ref=other, dst_ref=other, send_sem=send_sems.at[j], recv_sem=recv_sems.at[j],
                device_id=(x, y, c), device_id_type=MESH))
        return sends, recvs

    def start(ins, outs, sems):
        for cp in copies(outs, sems)[0]:
            cp.start()

    def finish(ins, outs, sems):
        sends, recvs = copies(outs, sems)
        for cp in recvs:
            cp.wait_recv()
        for cp in sends:
            cp.wait_send()

    return _Carry([land], [jax.ShapeDtypeStruct(land.shape, land.dtype)], {0: 0}, [sem, sem], start, finish)


def _swap_carry(dws):
    n = len(dws)
    sem = pltpu.SemaphoreType.DMA((n,))

    def copies(ins, outs, sems):
        send_sems, recv_sems = sems
        x, y, c = _mesh_pos()
        cps = []
        for w in range(n):
            h = dws[w].shape[1] // 2
            cps.append(pltpu.make_async_remote_copy(
                src_ref=ins[w].at[:, pl.ds((1 - c) * h, h), :], dst_ref=outs[w],
                send_sem=send_sems.at[w], recv_sem=recv_sems.at[w],
                device_id=(x, y, 1 - c), device_id_type=MESH))
        return cps

    def start(ins, outs, sems):
        for cp in copies(ins, outs, sems):
            cp.start()

    def finish(ins, outs, sems):
        for cp in copies(ins, outs, sems):
            cp.wait()

    shapes = [jax.ShapeDtypeStruct((s.shape[0], s.shape[1] // 2, s.shape[2]), s.dtype) for s in dws]
    return _Carry(dws, shapes, {}, [sem, sem], start, finish)


def _merge_carries(carries):
    if len(carries) == 1:
        return carries[0]
    inputs, out_shapes, sem_shapes, aliases, spans = [], [], [], {}, []
    for cy in carries:
        i0, o0, s0 = len(inputs), len(out_shapes), len(sem_shapes)
        aliases.update({i0 + i: o0 + o for i, o in cy.aliases.items()})
        inputs += cy.inputs
        out_shapes += cy.out_shapes
        sem_shapes += cy.sem_shapes
        spans.append((slice(i0, len(inputs)), slice(o0, len(out_shapes)), slice(s0, len(sem_shapes))))

    def start(ins, outs, sems):
        for cy, (si, so, ss) in zip(carries, spans):
            cy.start(ins[si], outs[so], sems[ss])

    def finish(ins, outs, sems):
        for cy, (si, so, ss) in zip(carries, spans):
            cy.finish(ins[si], outs[so], sems[ss])

    return _Carry(inputs, out_shapes, aliases, sem_shapes, start, finish)


ALL_CHIPS = (0, 1, 2)


class _MeshComm:
    GATHER_AT = {}
    FORWARD_AT = {
        "conv_fwd": ["w_conv_out", "w_glu_a", "w_glu_b", "w_out"],
        "mm_out": ["w_ff1"],
        "mm_ff1": ["w_ff2"],
    }
    SWAP_AT = {
        "mm_d_ff2": ["w_ff2"],
        "mm_d_ff1": ["w_ff1"],
        "conv_bwd": ["w_out", "w_glu_a", "w_glu_b", "w_conv_out"],
    }
    EXCHANGE_AT = {}
    EARLY_AT = "mm_dw_in"

    def __init__(self, shards, pos, chip, my_c):
        self.pos = pos
        self.chip = chip
        self.my_c = my_c
        self.shards = shards
        self.w_in_own, self.w_in_rel = _cast_bf16(shards["w_in"], name="cast_w_in")
        self.raw = {}
        self.flights = []
        self.parts = {}
        self.landing = {}
        self.halves = {}
        self.pending = {}
        self.last_site = {}
        for site, items in self.EXCHANGE_AT.items():
            for it in items:
                self.last_site[it[0]] = site

    def weight(self, name):
        g = self.bufs[name]
        return g.reshape(g.shape[0] * g.shape[1], g.shape[2]) if name in ROW_SHARDED else g

    def _slot_ids(self):
        x, y, _ = self.pos
        ids = [2 * x + y] + [2 * cx + cy for cx, cy in _other_chips(x, y)]
        return jnp.stack(ids).astype(jnp.int32)

    def start_w_in(self, after):
        *self.w_in_flight, token = _w_in_send(self.w_in_own, self.w_in_rel, after)
        order = [n for names in self.FORWARD_AT.values() for n in names]
        casts = [_cast_into_slot(self.shards[n], self.chip, token, name="cast_" + n) for n in order]
        sems, bufs = _gather_send(casts, name="gather_send")
        self.bufs = dict(zip(order, bufs))
        self.gather_sems = dict(zip(order, sems))
        return token

    def mm_in(self, u, afters):
        ids = self._slot_ids()
        sems, own, land = self.w_in_flight
        proj = _mm_slots(u, own[None], ids[0:1], None, name="mm_in_own")
        own, land = _w_in_wait(sems, own, land, [proj] + list(self.bufs.values()) + list(afters))
        land, = _run_carry(_forward_carry(land), name="forward_w_in")
        proj = _mm_slots(u, land, ids[1:4], proj, name="mm_in_rest", first=1)
        self.w_in_rel = land
        return proj

    def _add_and_send(self, names, landed, site, through):
        parts = [_add_half(self.raw.pop(n), l1, self.my_c, name="add_half_" + n) for n, l1 in zip(names, landed)]
        sems, parts, lands, through = _exchange_send(parts, through, name="exchange_send_" + site)
        self.flights.append((names, sems, parts, lands))
        return through

    def mm_d_in(self, dproj):
        landed = _run_carry(_swap_carry([self.raw["w_in"]]), name="swap_halves_w_in")
        dproj = self._add_and_send(["w_in"], landed, "w_in", dproj)
        return _mm(dproj, self.w_in_rel, mode="nt", out_dtype=F32, name="mm_d_in", a_slots=self._slot_ids())

    def early_grads(self, early):
        self.early = early

    def carry(self, site, args=()):
        jobs = []
        if site in self.FORWARD_AT:
            names = self.FORWARD_AT[site]
            landed = _gather_wait([self.gather_sems.pop(n) for n in names], [self.bufs[n] for n in names],
                                  [args[0]], name="gather_wait_" + site)
            jobs.append(("gather", [(n,) for n in names], _forward_abs_carry(landed)))
        if site == self.EARLY_AT:
            flat, self.early_offs = _pack(list(self.early.values()))
            jobs.append(("early", None, _gather_rows_carry(flat.reshape(-1, PACK_COLS))))
        if site in self.GATHER_AT:
            items = self.GATHER_AT[site]
            jobs.append(("gather", items, _gather_carry([(self.bufs[it[0]],) + tuple(it[1:]) for it in items])))
        if site in self.EXCHANGE_AT:
            items = self.EXCHANGE_AT[site]
            jobs.append(("exchange", items, _exchange_carry(
                [(self.parts[it[0]], self.landing.get(it[0])) + tuple(it[1:]) for it in items])))
        if site in self.SWAP_AT:
            names = self.SWAP_AT[site]
            jobs.append(("swap", names, _swap_carry([self.raw[n] for n in names])))
        if not jobs:
            return None
        self.pending[site] = jobs
        return _merge_carries([job[2] for job in jobs])

    def done(self, site, carried, out=None):
        pos = 0
        for kind, items, carry in self.pending.pop(site):
            outs = carried[pos:pos + len(carry.out_shapes)]
            pos += len(carry.out_shapes)
            if kind == "early":
                self.early_all = outs[0]
            elif kind == "gather":
                self.bufs.update(zip([it[0] for it in items], outs))
            elif kind == "swap":
                if isinstance(out, (list, tuple)):
                    out = [self._add_and_send(items, outs, site, out[0])] + list(out[1:])
                else:
                    out = self._add_and_send(items, outs, site, out)
            else:
                for it, landed in zip(items, outs):
                    n = it[0]
                    self.landing[n] = landed
                    if self.last_site[n] == site:
                        self.halves[n] = _sum_into_half(self.parts.pop(n), self.landing.pop(n), self.chip,
                                                        self.my_c, name="sum_chips_" + n)
        return out

    def grad(self, name, dw):
        if name in ROW_SHARDED:
            dw = dw.reshape(N_CHIPS, dw.shape[0] // N_CHIPS, dw.shape[1])
        self.raw[name] = dw

    def join_start(self, names, afters):
        for i, (group, sems, parts, lands) in enumerate(self.flights):
            parts, lands = _exchange_wait(sems, parts, lands, afters, name="exchange_wait_%d" % i)
            for n, part, land in zip(group, parts, lands):
                self.halves[n] = _sum_into_half(part, land, self.chip, self.my_c, name="sum_chips_" + n)
        self.flights = []
        sems, fulls, token = _join_send([self.halves.pop(n) for n in names], name="join_send")
        self.join_flight = (names, sems, fulls)
        return token

    def join_finish(self, afters):
        names, sems, fulls = self.join_flight
        return dict(zip(names, _join_wait(sems, fulls, afters, name="join_wait")))


def _local_step(x, target, mod, small, comm):
    rows, d = x.shape
    cw = d // 2
    shift1, scale1, gate1, shift2, scale2, gate2 = mod
    _, _, bbr, bbi = small["s5_disc"]
    b_in, c_out, b_out, c_in, mults = _s5_operands(*small["s5_loglam"], bbr, bbi, small["c_re"], small["c_im"])
    wt = comm.weight

    def riding(site, fn, *args, **kwargs):
        carry = comm.carry(site, args)
        if carry is None:
            return fn(*args, **kwargs)
        out, carried = fn(*args, carry=carry, **kwargs)
        return comm.done(site, carried, out)

    u = _norm_mod(x, small["norm1_g"], scale1, shift1, name="norm1_fwd")
    proj = comm.mm_in(u, [*b_in, *c_out, *b_out, *c_in, *mults])
    sl, cv = riding("conv_fwd", _conv_fwd, proj, small["w_dw"], small["b_dw"], small["ln_g"], small["ln_b"], cw=cw)
    y_conv = _mm(sl, wt("w_conv_out"), mode="nn", out_dtype=BF16, name="mm_conv_out")
    yg, st_re, st_im = riding("s5_fwd", _s5_fwd, proj, small["d_skip"], b_in, c_out, mults, col0=2 * cw // LANES)
    ya = riding("mm_glu_a", _mm, yg, wt("w_glu_a"), mode="nn", out_dtype=BF16, name="mm_glu_a")
    yb = riding("mm_glu_b", _mm, yg, wt("w_glu_b"), mode="nn", out_dtype=BF16, name="mm_glu_b")
    merged = _merge_fwd(proj, y_conv, ya, yb, cw=cw)
    mo = riding("mm_out", _mm, merged, wt("w_out"), mode="nn", out_dtype=BF16, name="mm_out")
    h1, z = _res_norm(x, mo, gate1, small["norm2_g"], scale2, shift2)
    f1 = riding("mm_ff1", _mm, z, wt("w_ff1"), mode="nn", out_dtype=BF16, name="mm_ff1")
    ff = _mm(f1, wt("w_ff2"), mode="nn", out_dtype=BF16, name="mm_ff2", a_fn=_relu2_bf16)
    dh2, dff, loss, d_final_g, d_gate2 = _final_fwd_bwd(h1, ff, gate2, small["final_g"], target)

    comm.grad("w_ff2", _mm(f1, dff, mode="tn", out_dtype=BF16, name="mm_dw_ff2", a_fn=_relu2_bf16))
    df1 = riding("mm_d_ff2", _mm, dff, wt("w_ff2"), mode="nt", out_dtype=BF16, name="mm_d_ff2", extra=f1,
                 epi=lambda acc, f: acc * (2.0 * jnp.maximum(f.astype(F32), 0.0)))
    comm.grad("w_ff1", riding("mm_dw_ff1", _mm, z, df1, mode="tn", out_dtype=BF16, name="mm_dw_ff1",
                              out_gathered=True))
    dz = riding("mm_d_ff1", _mm, df1, wt("w_ff1"), mode="nt", out_dtype=F32, name="mm_d_ff1")
    dh1, d_shift2, d_scale2, d_norm2_g, dmo, d_gate1 = riding(
        "norm2_bwd", _norm_mod_bwd, dz, h1, dh2, small["norm2_g"], scale2, gate1, mo, name="norm2_bwd")
    comm.grad("w_out", riding("mm_dw_out", _mm, merged, dmo, mode="tn", out_dtype=BF16, name="mm_dw_out"))
    dmerged = riding("mm_d_out", _mm, dmo, wt("w_out"), mode="nt", out_dtype=BF16, name="mm_d_out")
    dproj, dy_conv, dya, dyb = riding("merge_bwd", _merge_bwd, dmerged, proj, y_conv, ya, yb, cw=cw)
    comm.grad("w_glu_a", _mm(yg, dya, mode="tn", out_dtype=BF16, name="mm_dw_glu_a", out_gathered=True))
    comm.grad("w_glu_b", _mm(yg, dyb, mode="tn", out_dtype=BF16, name="mm_dw_glu_b", out_gathered=True))
    dyg_a = _mm(dya, wt("w_glu_a"), mode="nt", out_dtype=F32, name="mm_d_glu_a")
    dyg = _mm(dyb, wt("w_glu_b"), mode="nt", out_dtype=F32, name="mm_d_glu_b", extra=dyg_a,
              epi=lambda acc, e: acc + e)
    comm.grad("w_conv_out", _mm(sl, dy_conv, mode="tn", out_dtype=BF16, name="mm_dw_conv_out", out_gathered=True))
    dsl = _mm(dy_conv, wt("w_conv_out"), mode="nt", out_dtype=F32, name="mm_d_conv_out")
    dcv, d_ln_g, d_ln_b = _ln_bwd(dsl, cv, small["ln_g"], small["ln_b"])
    dproj, d_w_dw, d_b_dw = riding("conv_bwd", _conv_bwd, dcv, proj, small["w_dw"], dproj, cw=cw)
    dproj, d_d_skip, dbr, dbi, dcr, dci, dlr, dli = riding(
        "s5_bwd", _s5_bwd, proj, dyg, small["d_skip"], (st_re, st_im), c_out, b_out, c_in, mults, dproj,
        col0=2 * cw // LANES)
    sw = lambda m: jnp.swapaxes(m, 1, 2)
    early = {
        "dmod_tail": jnp.concatenate([d_gate1, d_shift2, d_scale2, d_gate2], axis=1), "loss": loss[:, 0:1],
        "w_dw": d_w_dw, "b_dw": d_b_dw, "ln_g": d_ln_g, "ln_b": d_ln_b,
        "lam_re": dlr.reshape(-1, SSM_STATE), "lam_im": dli.reshape(-1, SSM_STATE),
        "bb_re": sw(_block_diag_extract(dbr, SSM_GROUP, SSM_STATE)),
        "bb_im": sw(_block_diag_extract(dbi, SSM_GROUP, SSM_STATE)),
        "c_re": sw(_block_diag_extract(dcr, SSM_STATE, SSM_GROUP)),
        "c_im": sw(_block_diag_extract(dci, SSM_STATE, SSM_GROUP)),
        "d_skip": d_d_skip, "norm2_g": d_norm2_g, "final_g": d_final_g,
    }
    comm.early_grads(early)
    comm.grad("w_in", riding("mm_dw_in", _mm, u, dproj, mode="tn", out_dtype=BF16, name="mm_dw_in",
                             out_gathered=True))
    du = comm.mm_d_in(dproj)
    grad_x, d_shift1, d_scale1, d_norm1_g = riding(
        "norm1_bwd", _norm_mod_bwd, du, x, dh1, small["norm1_g"], scale1, None, None, name="norm1_bwd")
    late ={"dmod_head": jnp.concatenate([d_shift1, d_scale1], axis=1), "norm1_g": d_norm1_g}
    return grad_x, early, late


WEIGHT_NAMES = ["w_ada", "b_ada", "norm1_g", "w_in", "w_dw", "b_dw", "ln_g", "ln_b", "w_conv_out", "a_re", "a_im",
                "log_dt", "b_re", "b_im", "c_re", "c_im", "d_skip", "w_glu_a", "w_glu_b", "w_out", "norm2_g",
                "w_ff1", "w_ff2", "final_g"]
BIG_NAMES = ["w_in", "w_conv_out", "w_glu_a", "w_glu_b", "w_out", "w_ff1", "w_ff2"]
ROW_SHARDED = ("w_out", "w_ff2")
PACK_COLS = 1024
PACK_TILE = SUBLANES * PACK_COLS


def _pack(arrays):
    flats = [a.reshape(-1) for a in arrays]
    offs = []
    total = 0
    for f in flats:
        offs.append(total)
        total += f.shape[0]
    pad = (-total) % PACK_TILE
    if pad:
        flats.append(jnp.zeros((pad,), F32))
    return jnp.concatenate(flats), offs


def _unpack(flat, offs, like):
    return [flat[o:o + a.size].reshape(a.shape) for o, a in zip(offs, like)]


def _gather_w_dw(w_shard):
    k, n = w_shard.shape
    padded = jnp.pad(w_shard, ((0, HALO - k), (0, 0)))
    allw = _gather_small(padded, name="gather_w_dw").reshape(N_CHIPS, 2, HALO, n)[:, 0, :k]
    return jnp.moveaxis(allw, 0, 1).reshape(k, N_CHIPS * n)


def kernel(x, c, w_ada, b_ada, norm1_g, w_in, w_dw, b_dw, ln_g, ln_b, w_conv_out, a_re, a_im, log_dt, b_re, b_im, c_re, c_im, d_skip, w_glu_a, w_glu_b, w_out, norm2_g, w_ff1, w_ff2, final_g, loss_target, m_w_ada, m_b_ada, m_norm1_g, m_w_in, m_w_dw, m_b_dw, m_ln_g, m_ln_b, m_w_conv_out, m_a_re, m_a_im, m_log_dt, m_b_re, m_b_im, m_c_re, m_c_im, m_d_skip, m_w_glu_a, m_w_glu_b, m_w_out, m_norm2_g, m_w_ff1, m_w_ff2, m_final_g, v_w_ada, v_b_ada, v_norm1_g, v_w_in, v_w_dw, v_b_dw, v_ln_g, v_ln_b, v_w_conv_out, v_a_re, v_a_im, v_log_dt, v_b_re, v_b_im, v_c_re, v_c_im, v_d_skip, v_w_glu_a, v_w_glu_b, v_w_out, v_norm2_g, v_w_ff1, v_w_ff2, v_final_g):
    given = dict(locals())
    w = {n: given[n] for n in WEIGHT_NAMES}
    m = {n: given["m_" + n] for n in WEIGHT_NAMES}
    v = {n: given["v_" + n] for n in WEIGHT_NAMES}
    d = x.shape[2]
    xi, yi, ci = _mesh_pos()
    chip = 2 * xi + yi
    dev = 4 * xi + 2 * yi + ci
    my_c = jnp.reshape(ci, (1,)).astype(jnp.int32)
    chip_arr = jnp.reshape(chip, (1,)).astype(jnp.int32)

    comm = _MeshComm({n: w[n][0] for n in BIG_NAMES}, (xi, yi, ci), chip_arr, my_c)

    ndw = w_dw.shape[2]
    assert d // SUBLANES == ndw
    first = jnp.concatenate([c.reshape(SUBLANES, ndw), jnp.pad(w_dw[0], ((0, HALO - CONV_KERNEL), (0, 0)))])
    first_all = _gather_small(first, name="gather_c_w_dw").reshape(N_DEV, SUBLANES + HALO, ndw)
    c_all = first_all[:, :SUBLANES].reshape(N_DEV, d)
    taps = first_all.reshape(N_CHIPS, 2, SUBLANES + HALO, ndw)[:, 0, SUBLANES:SUBLANES + CONV_KERNEL]
    w_dw_full = jnp.moveaxis(taps, 0, 1).reshape(CONV_KERNEL, N_CHIPS * ndw)

    nmod = w_ada.shape[2]
    b_cols = lax.dynamic_slice(b_ada, (0, chip * nmod), (1, nmod))
    mod_part = _ada_fwd(c_all, w_ada[0], b_cols)
    mod_all = _gather_small(mod_part, name="gather_mod").reshape(N_CHIPS, 2, N_DEV, nmod)[:, 0]
    mod_full = jnp.moveaxis(mod_all, 0, 1).reshape(N_DEV, N_CHIPS * nmod)
    mod_row = lax.dynamic_slice(mod_full, (dev, 0), (1, N_CHIPS * nmod))
    mod = [mod_row[:, i * d:(i + 1) * d] for i in range(6)]

    token = comm.start_w_in(mod_row)
    log_dt_0 = log_dt[0] + token[0, 0]

    disc_in = (a_re[0], a_im[0], log_dt_0, b_re[0], b_im[0])
    disc, disc_vjp = jax.vjp(_s5_discretise, *disc_in)
    dt = jnp.exp(log_dt_0)[:, None]
    small = {"norm1_g": norm1_g, "w_dw": w_dw_full, "b_dw": b_dw, "ln_g": ln_g, "ln_b": ln_b,
             "c_re": c_re[0], "c_im": c_im[0], "d_skip": d_skip, "norm2_g": norm2_g,
             "final_g": final_g[None, :], "s5_disc": disc, "s5_loglam": (a_re[0] * dt, a_im[0] * dt)}

    grad_x, early, late = _local_step(x[0], loss_target[0], mod, small, comm)
    grads = {}

    early_all = comm.early_all.reshape(N_DEV, -1, PACK_COLS)
    early_sum = _sum_leading(early_all, name="sum_small_grads").reshape(-1)
    summed = dict(zip(early, _unpack(early_sum, comm.early_offs, list(early.values()))))
    flat, late_offs = _pack(list(late.values()))
    late_all = _gather_small(flat.reshape(-1, PACK_COLS), name="gather_late_grads").reshape(N_DEV, -1, PACK_COLS)
    late_sum = _sum_leading(late_all, name="sum_late_grads").reshape(-1)
    summed.update(zip(late, _unpack(late_sum, late_offs, list(late.values()))))
    head = late_all[:, :2 * d // PACK_COLS].reshape(N_DEV, 2 * d)
    tail = early_all[:, :4 * d // PACK_COLS].reshape(N_DEV, 4 * d)
    dmod_all = jnp.concatenate([head, tail], axis=1)

    grads["w_ada"] = _ada_bwd(c_all, lax.dynamic_slice(dmod_all, (0, chip * nmod), (N_DEV, nmod)))
    grads["b_ada"] = _sum_leading(dmod_all.reshape(N_DEV, SUBLANES, 6 * d // SUBLANES),
                                  name="sum_b_ada").reshape(1, 6 * d)
    da_re, da_im, dlog_dt, db_re, db_im = disc_vjp(
        (summed["lam_re"], summed["lam_im"], summed["bb_re"], summed["bb_im"]))
    grads.update({
        "norm1_g": summed["norm1_g"], "w_dw": lax.dynamic_slice(summed["w_dw"], (0, chip * ndw), (CONV_KERNEL, ndw)),
        "b_dw": summed["b_dw"], "ln_g": summed["ln_g"], "ln_b": summed["ln_b"],
        "a_re": da_re, "a_im": da_im, "log_dt": dlog_dt, "b_re": db_re, "b_im": db_im,
        "c_re": summed["c_re"], "c_im": summed["c_im"], "d_skip": summed["d_skip"],
        "norm2_g": summed["norm2_g"], "final_g": summed["final_g"],
    })

    delta, new_m, new_v = {}, {}, {}

    def adam_big(n, after=None):
        shp = w[n].shape
        two_d = lambda a: a.reshape(shp[1], shp[2])
        res = _adamw(two_d(w[n]), two_d(grads[n]), two_d(m[n]), two_d(v[n]), name="adamw_" + n, after=after)
        delta[n], new_m[n], new_v[n] = [r.reshape(shp) for r in res]

    token = comm.join_start(BIG_NAMES, [late_all])
    adam_big("w_ada", token)
    grads.update(comm.join_finish([delta["w_ada"]]))
    for n in BIG_NAMES:
        adam_big(n)
    grads = {n: grads[n].reshape(w[n].shape) for n in WEIGHT_NAMES}
    rest = [n for n in WEIGHT_NAMES if n not in delta]
    as_2d = lambda a: a.reshape(1, -1) if a.ndim == 1 else a
    outs = _adamw_many(*[[as_2d(src[n]) for n in rest] for src in (w, grads, m, v)], name="adamw_small")
    for dst, arrays in zip((delta, new_m, new_v), outs):
        for n, a in zip(rest, arrays):
            dst[n] = a.reshape(w[n].shape)

    return (summed["loss"].reshape(()), grad_x[None], *[grads[n] for n in WEIGHT_NAMES],
            *[delta[n] for n in WEIGHT_NAMES], *[new_m[n] for n in WEIGHT_NAMES],
            *[new_v[n] for n in WEIGHT_NAMES])
```

```python
import functools
import math

import jax
import jax.numpy as jnp
from jax import lax
from jax.experimental import pallas as pl
from jax.experimental.pallas import tpu as pltpu

F32 = jnp.float32
BF16 = jnp.bfloat16
EPS = 1e-6
CONV_KERNEL = 31
SSM_GROUP = 16
SSM_STATE = 64
ADAM_LR = 0.001
ADAM_B1 = 0.9
ADAM_B2 = 0.999
ADAM_EPS = 1e-08
ADAM_WD = 0.01
ADAM_STEP = 10

N_CHIPS = 4
N_DEV = 8
VMEM_LIMIT_BYTES = 56 * 1024 * 1024
LANES = 128
SUBLANES = 8
HALO = 32
GROUPS_PER_BLOCK = LANES // SSM_GROUP
STATE_LANES = GROUPS_PER_BLOCK * SSM_STATE
MESH = pl.DeviceIdType.MESH


def _cparams(sem):
    return pltpu.CompilerParams(dimension_semantics=sem, vmem_limit_bytes=VMEM_LIMIT_BYTES)


def _pick(n, pref, mult=LANES):
    if n <= pref:
        return n
    best = None
    for d in range(mult, pref + 1, mult):
        if n % d == 0:
            best = d
    assert best is not None, (n, pref)
    return best


def _sigmoid(v):
    return 1.0 / (1.0 + jnp.exp(-v))


def _gelu_parts(v):
    k0 = math.sqrt(2.0 / math.pi)
    inner = k0 * (v + 0.044715 * v * v * v)
    t = jnp.tanh(inner)
    return k0, t


def _gelu(v):
    _, t = _gelu_parts(v)
    return 0.5 * v * (1.0 + t)


def _gelu_grad(v):
    k0, t = _gelu_parts(v)
    return 0.5 * (1.0 + t) + 0.5 * v * (1.0 - t * t) * k0 * (1.0 + 3.0 * 0.044715 * v * v)


def _relu2_bf16(a):
    t = jnp.maximum(a.astype(F32), 0.0)
    return (t * t).astype(BF16)


class _Carry:
    def __init__(self, inputs, out_shapes, aliases, sem_shapes, start, finish):
        self.inputs = list(inputs)
        self.out_shapes = list(out_shapes)
        self.aliases = dict(aliases)
        self.sem_shapes = list(sem_shapes)
        self.start = start
        self.finish = finish


def _call(body, *, grid, in_specs, out_specs, out_shape, scratch_shapes, semantics, name, args, carry=None,
          prefetch=(), aliases=None):
    n_in, n_out, n_scr, n_pf = len(in_specs), len(out_specs), len(scratch_shapes), len(prefetch)
    own_aliases = {n_pf + i: o for i, o in (aliases or {}).items()}
    if carry is None:
        gs = pltpu.PrefetchScalarGridSpec(
            num_scalar_prefetch=n_pf, grid=grid, in_specs=in_specs, out_specs=out_specs,
            scratch_shapes=scratch_shapes)
        outs = pl.pallas_call(
            body, grid_spec=gs, out_shape=out_shape, input_output_aliases=own_aliases,
            compiler_params=_cparams(semantics), name=name)(*prefetch, *args)
        return list(outs), []
    ci, co = len(carry.inputs), len(carry.out_shapes)

    def wrapped(*refs):
        pf, refs = refs[:n_pf], refs[n_pf:]
        ins, cins = refs[:n_in], refs[n_in:n_in + ci]
        p = n_in + ci
        outs, couts = refs[p:p + n_out], refs[p + n_out:p + n_out + co]
        p += n_out + co
        scr, csems = refs[p:p + n_scr], refs[p + n_scr:]
        first = pl.program_id(0) == 0
        last = pl.program_id(0) == grid[0] - 1
        for ax in range(1, len(grid)):
            first = jnp.logical_and(first, pl.program_id(ax) == 0)
            last = jnp.logical_and(last, pl.program_id(ax) == grid[ax] - 1)

        @pl.when(first)
        def _():
            carry.start(cins, couts, csems)

        body(*pf, *ins, *outs, *scr)

        @pl.when(last)
        def _():
            carry.finish(cins, couts, csems)

    any_spec = pl.BlockSpec(memory_space=pl.ANY)
    gs = pltpu.PrefetchScalarGridSpec(
        num_scalar_prefetch=n_pf, grid=grid, in_specs=list(in_specs) + [any_spec] * ci,
        out_specs=list(out_specs) + [any_spec] * co, scratch_shapes=list(scratch_shapes) + carry.sem_shapes)
    all_aliases = dict(own_aliases)
    all_aliases.update({n_pf + n_in + i: n_out + o for i, o in carry.aliases.items()})
    outs = pl.pallas_call(
        wrapped, grid_spec=gs, out_shape=list(out_shape) + carry.out_shapes, input_output_aliases=all_aliases,
        compiler_params=_cparams(("arbitrary",) * len(grid)), name=name)(*prefetch, *args, *carry.inputs)
    return list(outs[:n_out]), list(outs[n_out:])


def _run_carry(carry, *, name):
    ci = len(carry.inputs)

    def body(*refs):
        cins, couts, csems = refs[:ci], refs[ci:ci + len(carry.out_shapes)], refs[ci + len(carry.out_shapes):]
        carry.start(cins, couts, csems)
        carry.finish(cins, couts, csems)

    any_spec = pl.BlockSpec(memory_space=pl.ANY)
    outs = pl.pallas_call(
        body, in_specs=[any_spec] * ci, out_specs=[any_spec] * len(carry.out_shapes), out_shape=carry.out_shapes,
        scratch_shapes=carry.sem_shapes, input_output_aliases=carry.aliases, name=name)(*carry.inputs)
    return list(outs)


def _mm(a, b, *, mode, out_dtype, name, out_gathered=False, a_fn=None, epi=None, extra=None,
        bm_pref=1024, bn_pref=1024, bk_pref=2048, carry=None, a_slots=None):
    gathered = (b.ndim == 3)
    if mode == "nn":
        m, kdim = a.shape
        ns = b.shape[-1]
        n = ns * (N_CHIPS if gathered else 1)
        bm, bn, bk = _pick(m, bm_pref), _pick(ns, bn_pref), _pick(kdim, bk_pref)
        npb = ns // bn
        grid = (m // bm, n // bn, kdim // bk)
        a_spec = pl.BlockSpec((bm, bk), lambda i, j, k: (i, k))
        if gathered:
            b_spec = pl.BlockSpec((None, bk, bn), lambda i, j, k: (j // npb, k, j % npb))
        else:
            b_spec = pl.BlockSpec((bk, bn), lambda i, j, k: (k, j))
        o_spec = pl.BlockSpec((bm, bn), lambda i, j, k: (i, j))
        e_spec = pl.BlockSpec((bm, bn), lambda i, j, k: (i, j))
        out_shape = (m, n)
        acc_shape = (bm, bn)
        dims = (((1,), (0,)), ((), ()))
    elif mode == "nt":
        m = a.shape[0]
        kdim, ns = b.shape[-2], b.shape[-1]
        n = ns * (N_CHIPS if gathered else 1)
        assert a.shape[1] == n
        bm, bko, bnr = _pick(m, bm_pref), _pick(kdim, bn_pref), _pick(ns, bk_pref)
        npb = ns // bnr
        grid = (m // bm, kdim // bko, n // bnr)
        a_spec = pl.BlockSpec((bm, bnr), lambda i, j, k: (i, k))
        if gathered:
            b_spec = pl.BlockSpec((None, bko, bnr), lambda i, j, k: (k // npb, j, k % npb))
        else:
            b_spec = pl.BlockSpec((bko, bnr), lambda i, j, k: (j, k))
        o_spec = pl.BlockSpec((bm, bko), lambda i, j, k: (i, j))
        e_spec = pl.BlockSpec((bm, bko), lambda i, j, k: (i, j))
        if a_slots is not None:
            assert gathered and extra is None
            a_spec = pl.BlockSpec((bm, bnr), lambda i, j, k, s_ref: (i, s_ref[k // npb] * npb + k % npb))
            b_spec = pl.BlockSpec((None, bko, bnr), lambda i, j, k, s_ref: (k // npb, j, k % npb))
            o_spec = pl.BlockSpec((bm, bko), lambda i, j, k, s_ref: (i, j))
        out_shape = (m, kdim)
        acc_shape = (bm, bko)
        dims = (((1,), (1,)), ((), ()))
    else:
        m, kdim = a.shape
        n = b.shape[1]
        ns = n // N_CHIPS if out_gathered else n
        bmr, bko, bn = _pick(m, bk_pref), _pick(kdim, bm_pref), _pick(ns, bn_pref)
        npb = ns // bn
        grid = (kdim // bko, n // bn, m // bmr)
        a_spec = pl.BlockSpec((bmr, bko), lambda i, j, k: (k, i))
        b_spec = pl.BlockSpec((bmr, bn), lambda i, j, k: (k, j))
        if out_gathered:
            o_spec = pl.BlockSpec((None, bko, bn), lambda i, j, k: (j // npb, i, j % npb))
            out_shape = (N_CHIPS, kdim, ns)
        else:
            o_spec = pl.BlockSpec((bko, bn), lambda i, j, k: (i, j))
            out_shape = (kdim, n)
        e_spec = None
        acc_shape = (bko, bn)
        dims = (((0,), (0,)), ((), ()))
    nk = grid[2]

    def body(*refs):
        if a_slots is not None:
            refs = refs[1:]
        if extra is not None:
            a_ref, b_ref, e_ref, o_ref, acc = refs
        else:
            a_ref, b_ref, o_ref, acc = refs
            e_ref = None
        k = pl.program_id(2)
        av = a_ref[...]
        if a_fn is not None:
            av = a_fn(av)
        part = lax.dot_general(av, b_ref[...], dims, preferred_element_type=F32)

        def finish(r):
            if epi is not None:
                r = epi(r, e_ref[...])
            o_ref[...] = r.astype(o_ref.dtype)

        if nk == 1:
            finish(part)
            return

        @pl.when(k == 0)
        def _():
            acc[...] = part

        @pl.when(jnp.logical_and(k > 0, k < nk - 1))
        def _():
            acc[...] += part

        @pl.when(k == nk - 1)
        def _():
            finish(acc[...] + part)

    in_specs = [a_spec, b_spec]
    args = [a, b]
    if extra is not None:
        in_specs.append(e_spec)
        args.append(extra)
    outs, carried = _call(body, grid=grid, in_specs=in_specs, out_specs=[o_spec],
                          out_shape=[jax.ShapeDtypeStruct(out_shape, out_dtype)],
                          scratch_shapes=[pltpu.VMEM(acc_shape, F32)],
                          semantics=("parallel", "parallel", "arbitrary"), name=name, args=args, carry=carry,
                          prefetch=() if a_slots is None else (a_slots,))
    return outs[0] if carry is None else (outs[0], carried)


def _mm_slots(a, wbuf, slots, prev, *, name, carry=None, first=0):
    m, kdim = a.shape
    ns = wbuf.shape[2]
    bm, bn = _pick(m, 1024), _pick(ns, 1024)
    npb = ns // bn
    grid = (m // bm, slots.shape[0], npb)

    def body(s_ref, a_ref, b_ref, *rest):
        o_ref = rest[-1]
        o_ref[...] = _dot(a_ref[...], b_ref[...]).astype(o_ref.dtype)

    in_specs = [pl.BlockSpec((bm, kdim), lambda i, s, j, s_ref: (i, 0)),
                pl.BlockSpec((None, kdim, bn), lambda i, s, j, s_ref: (first + s, 0, j))]
    args = [a, wbuf]
    aliases = None
    if prev is not None:
        in_specs.append(pl.BlockSpec(memory_space=pl.ANY))
        args.append(prev)
        aliases = {2: 0}
    outs, carried = _call(
        body, grid=grid, in_specs=in_specs,
        out_specs=[pl.BlockSpec((bm, bn), lambda i, s, j, s_ref: (i, s_ref[s] * npb + j))],
        out_shape=[jax.ShapeDtypeStruct((m, N_CHIPS * ns), BF16)], scratch_shapes=[],
        semantics=("parallel", "arbitrary", "arbitrary"), name=name, args=args, carry=carry,
        prefetch=(slots,), aliases=aliases)
    return outs[0] if carry is None else (outs[0], carried)


def _row_tile(rows, cols, n_arrays):
    budget = VMEM_LIMIT_BYTES // 3
    cap = min(512, budget // (n_arrays * 2 * cols * 4))
    for t in range(cap - cap % SUBLANES, 0, -SUBLANES):
        if rows % t == 0:
            return t
    return rows


def _norm_mod(x, g, scale, shift, *, name):
    rows, d = x.shape
    tr = _row_tile(rows, d, 3)

    def body(x_ref, g_ref, sc_ref, sh_ref, o_ref):
        xv = x_ref[...]
        r = lax.rsqrt(jnp.mean(xv * xv, axis=-1, keepdims=True) + EPS)
        o_ref[...] = ((xv * r * g_ref[...]) * (1.0 + sc_ref[...]) + sh_ref[...]).astype(o_ref.dtype)

    row = pl.BlockSpec((tr, d), lambda i: (i, 0))
    vec = pl.BlockSpec((1, d), lambda i: (0, 0))
    return pl.pallas_call(
        body, grid=(rows // tr,), in_specs=[row, vec, vec, vec], out_specs=row,
        out_shape=jax.ShapeDtypeStruct((rows, d), BF16),
        compiler_params=_cparams(("parallel",)), name=name)(x, g, scale, shift)


CONV_CHUNK = 2 * SUBLANES


def _shifted_copies(buf, n):
    for r in range(1, SUBLANES):
        buf[r, pl.ds(0, n - SUBLANES), :] = buf[0, pl.ds(r, n - SUBLANES), :]


def _conv_fwd(proj, w_dw, b_dw, ln_g, ln_b, *, cw, carry=None):
    rows = proj.shape[0]
    tt = _pick(rows, 256, HALO)
    hb = tt // HALO

    def body(a_ref, g_ref, ha_ref, hg_ref, w_ref, b_ref, lg_ref, lb_ref, sl_ref, cv_ref, vs):
        i = pl.program_id(0)
        hv = ha_ref[...].astype(F32) * _sigmoid(hg_ref[...].astype(F32))
        vs[0, pl.ds(0, HALO), :] = jnp.where(i == 0, 0.0, hv)
        vs[0, pl.ds(HALO, tt), :] = a_ref[...].astype(F32) * _sigmoid(g_ref[...].astype(F32))
        _shifted_copies(vs, HALO + tt)

        def chunk(ci, carry):
            r0 = pl.multiple_of(ci * CONV_CHUNK, CONV_CHUNK)
            acc = jnp.broadcast_to(b_ref[...], (CONV_CHUNK, cw))
            for k in range(CONV_KERNEL):
                q, r = divmod(HALO - (CONV_KERNEL - 1) + k, SUBLANES)
                acc = acc + w_ref[pl.ds(k, 1), :] * vs[r, pl.ds(r0 + q * SUBLANES, CONV_CHUNK), :]
            cv_ref[pl.ds(r0, CONV_CHUNK), :] = acc
            return carry

        lax.fori_loop(0, tt // CONV_CHUNK, chunk, 0)
        acc = cv_ref[...]
        mu = jnp.mean(acc, axis=-1, keepdims=True)
        xc = acc - mu
        rstd = lax.rsqrt(jnp.mean(xc * xc, axis=-1, keepdims=True) + EPS)
        ln = xc * rstd * lg_ref[...] + lb_ref[...]
        sl_ref[...] = (ln * _sigmoid(ln)).astype(sl_ref.dtype)

    tile = lambda c: pl.BlockSpec((tt, cw), lambda i, c=c: (i, c))
    halo = lambda c: pl.BlockSpec((HALO, cw), lambda i, c=c: (jnp.maximum(i * hb - 1, 0), c))
    vec = pl.BlockSpec((1, cw), lambda i: (0, 0))
    outs, carried = _call(
        body, grid=(rows // tt,),
        in_specs=[tile(0), tile(1), halo(0), halo(1),
                  pl.BlockSpec((CONV_KERNEL, cw), lambda i: (0, 0)), vec, vec, vec],
        out_specs=[pl.BlockSpec((tt, cw), lambda i: (i, 0)), pl.BlockSpec((tt, cw), lambda i: (i, 0))],
        out_shape=[jax.ShapeDtypeStruct((rows, cw), BF16), jax.ShapeDtypeStruct((rows, cw), F32)],
        scratch_shapes=[pltpu.VMEM((SUBLANES, HALO + tt, cw), F32)],
        semantics=("parallel",), name="conv_fwd", args=[proj, proj, proj, proj, w_dw, b_dw, ln_g, ln_b],
        carry=carry)
    return outs if carry is None else (outs, carried)


def _ln_bwd(dsl, cv, ln_g, ln_b):
    rows, cw = cv.shape
    tr = _row_tile(rows, cw, 3)

    def body(d_ref, cv_ref, lg_ref, lb_ref, o_ref, dg_ref, db_ref):
        i = pl.program_id(0)

        @pl.when(i == 0)
        def _():
            dg_ref[...] = jnp.zeros_like(dg_ref)
            db_ref[...] = jnp.zeros_like(db_ref)

        x = cv_ref[...]
        mu = jnp.mean(x, axis=-1, keepdims=True)
        xc = x - mu
        rstd = lax.rsqrt(jnp.mean(xc * xc, axis=-1, keepdims=True) + EPS)
        xh = xc * rstd
        ln = xh * lg_ref[...] + lb_ref[...]
        s = _sigmoid(ln)
        dln = d_ref[...].astype(F32) * (s * (1.0 + ln * (1.0 - s)))
        dg_ref[...] += jnp.sum(dln * xh, axis=0, keepdims=True)
        db_ref[...] += jnp.sum(dln, axis=0, keepdims=True)
        dxh = dln * lg_ref[...]
        m1 = jnp.mean(dxh, axis=-1, keepdims=True)
        m2 = jnp.mean(dxh * xh, axis=-1, keepdims=True)
        o_ref[...] = rstd * (dxh - m1 - xh * m2)

    row = pl.BlockSpec((tr, cw), lambda i: (i, 0))
    vec = pl.BlockSpec((1, cw), lambda i: (0, 0))
    return pl.pallas_call(
        body, grid=(rows // tr,), in_specs=[row, row, vec, vec], out_specs=[row, vec, vec],
        out_shape=[jax.ShapeDtypeStruct((rows, cw), F32), jax.ShapeDtypeStruct((1, cw), F32),
                   jax.ShapeDtypeStruct((1, cw), F32)],
        compiler_params=_cparams(("arbitrary",)), name="ln_bwd")(dsl, cv, ln_g, ln_b)


def _conv_bwd(dcv, proj, w_dw, dproj, *, cw, carry=None):
    rows = proj.shape[0]
    tt = _pick(rows, 256, HALO)
    hb = tt // HALO
    nt = rows // tt
    taps = CONV_KERNEL

    def body(d_ref, dn_ref, a_ref, g_ref, ha_ref, hg_ref, w_ref, dproj_ref, o_ref, dw_ref, db_ref, vs, ds):
        i = pl.program_id(0)

        @pl.when(i == 0)
        def _():
            dw_ref[...] = jnp.zeros_like(dw_ref)
            db_ref[...] = jnp.zeros_like(db_ref)

        hv = ha_ref[...].astype(F32) * _sigmoid(hg_ref[...].astype(F32))
        vs[0, pl.ds(0, HALO), :] = jnp.where(i == 0, 0.0, hv)
        vs[0, pl.ds(HALO, tt), :] = a_ref[...].astype(F32) * _sigmoid(g_ref[...].astype(F32))
        _shifted_copies(vs, HALO + tt)
        ds[0, pl.ds(0, tt), :] = d_ref[...]
        ds[0, pl.ds(tt, HALO), :] = jnp.where(i == nt - 1, 0.0, dn_ref[...])
        _shifted_copies(ds, tt + HALO)
        db_ref[...] += jnp.sum(d_ref[...], axis=0, keepdims=True)
        for k in range(taps):
            q, r = divmod(HALO - (taps - 1) + k, SUBLANES)
            dw_ref[pl.ds(k, 1), :] += jnp.sum(d_ref[...] * vs[r, pl.ds(q * SUBLANES, tt), :], axis=0, keepdims=True)

        def chunk(ci, carry):
            r0 = pl.multiple_of(ci * CONV_CHUNK, CONV_CHUNK)
            dv = jnp.zeros((CONV_CHUNK, cw), F32)
            for k in range(taps):
                q, r = divmod(taps - 1 - k, SUBLANES)
                dv = dv + w_ref[pl.ds(k, 1), :] * ds[r, pl.ds(r0 + q * SUBLANES, CONV_CHUNK), :]
            av = a_ref[pl.ds(r0, CONV_CHUNK), :].astype(F32)
            sg = _sigmoid(g_ref[pl.ds(r0, CONV_CHUNK), :].astype(F32))
            o_ref[pl.ds(r0, CONV_CHUNK), pl.ds(0, cw)] = (dv * sg).astype(o_ref.dtype)
            o_ref[pl.ds(r0, CONV_CHUNK), pl.ds(cw, cw)] = (dv * av * sg * (1.0 - sg)).astype(o_ref.dtype)
            return carry

        lax.fori_loop(0, tt // CONV_CHUNK, chunk, 0)

    tile = lambda c: pl.BlockSpec((tt, cw), lambda i, c=c: (i, c))
    halo = lambda c: pl.BlockSpec((HALO, cw), lambda i, c=c: (jnp.maximum(i * hb - 1, 0), c))
    nxt = pl.BlockSpec((HALO, cw), lambda i: (jnp.minimum((i + 1) * hb, nt * hb - 1), 0))
    outs, carried = _call(
        body, grid=(nt,),
        in_specs=[pl.BlockSpec((tt, cw), lambda i: (i, 0)), nxt, tile(0), tile(1), halo(0), halo(1),
                  pl.BlockSpec((taps, cw), lambda i: (0, 0)), pl.BlockSpec(memory_space=pl.ANY)],
        out_specs=[pl.BlockSpec((tt, 2 * cw), lambda i: (i, 0)),
                   pl.BlockSpec((taps, cw), lambda i: (0, 0)), pl.BlockSpec((1, cw), lambda i: (0, 0))],
        out_shape=[jax.ShapeDtypeStruct(dproj.shape, dproj.dtype), jax.ShapeDtypeStruct((taps, cw), F32),
                   jax.ShapeDtypeStruct((1, cw), F32)],
        scratch_shapes=[pltpu.VMEM((SUBLANES, HALO + tt, cw), F32), pltpu.VMEM((SUBLANES, tt + HALO, cw), F32)],
        semantics=("arbitrary",), name="conv_bwd", args=[dcv, dcv, proj, proj, proj, proj, w_dw, dproj],
        carry=carry, aliases={7: 0})
    return outs if carry is None else (outs, carried)


def _merge_fwd(proj, y_conv, ya, yb, *, cw):
    rows = proj.shape[0]
    tr = _row_tile(rows, cw, 4)

    def body(gc_ref, gs_ref, yc_ref, ya_ref, yb_ref, o_ref):
        ys = ya_ref[...].astype(F32) * _sigmoid(yb_ref[...].astype(F32))
        o_ref[...] = (_sigmoid(gc_ref[...].astype(F32)) * yc_ref[...].astype(F32)
                      + _sigmoid(gs_ref[...].astype(F32)) * ys).astype(o_ref.dtype)

    blk = lambda off: pl.BlockSpec((tr, cw), lambda i, h, off=off: (i, off + h))
    return pl.pallas_call(
        body, grid=(rows // tr, 2), in_specs=[blk(3), blk(5), blk(0), blk(0), blk(0)], out_specs=blk(0),
        out_shape=jax.ShapeDtypeStruct((rows, 2 * cw), BF16),
        compiler_params=_cparams(("parallel", "parallel")), name="merge_fwd")(proj, proj, y_conv, ya, yb)


def _merge_bwd(dmerged, proj, y_conv, ya, yb, *, cw, carry=None):
    rows = proj.shape[0]
    tr = _row_tile(rows, cw, 3)

    def body(d_ref, g_ref, yc_ref, ya_ref, yb_ref, dg_ref, dyc_ref, dya_ref, dyb_ref):
        q = pl.program_id(1)
        d = d_ref[...].astype(F32)
        sg = _sigmoid(g_ref[...].astype(F32))

        @pl.when(q < 2)
        def _():
            dg_ref[...] = (d * yc_ref[...].astype(F32) * sg * (1.0 - sg)).astype(dg_ref.dtype)
            dyc_ref[...] = (d * sg).astype(dyc_ref.dtype)

        @pl.when(q >= 2)
        def _():
            sb = _sigmoid(yb_ref[...].astype(F32))
            yav = ya_ref[...].astype(F32)
            dg_ref[...] = (d * (yav * sb) * sg * (1.0 - sg)).astype(dg_ref.dtype)
            dys = d * sg
            dya_ref[...] = (dys * sb).astype(dya_ref.dtype)
            dyb_ref[...] = (dys * yav * sb * (1.0 - sb)).astype(dyb_ref.dtype)

    spec = lambda f: pl.BlockSpec((tr, cw), lambda i, q, f=f: (i, f(q)))
    conv_half = spec(lambda q: jnp.minimum(q, 1))
    ssm_half = spec(lambda q: jnp.maximum(q - 2, 0))
    o2 = jax.ShapeDtypeStruct((rows, 2 * cw), BF16)
    outs, carried = _call(
        body, grid=(rows // tr, 4),
        in_specs=[spec(lambda q: q % 2), spec(lambda q: 3 + q), conv_half, ssm_half, ssm_half],
        out_specs=[spec(lambda q: 3 + q), conv_half, ssm_half, ssm_half],
        out_shape=[jax.ShapeDtypeStruct((rows, 7 * cw), BF16), o2, o2, o2], scratch_shapes=[],
        semantics=("parallel", "arbitrary"), name="merge_bwd", args=[dmerged, proj, y_conv, ya, yb],
        carry=carry)
    return outs if carry is None else (outs, carried)


def _res_norm(x, mo, gate, g, scale, shift):
    rows, d = x.shape
    tr = _row_tile(rows, d, 4)

    def body(x_ref, mo_ref, gt_ref, g_ref, sc_ref, sh_ref, h_ref, z_ref):
        h = x_ref[...] + gt_ref[...] * mo_ref[...].astype(F32)
        h_ref[...] = h
        r = lax.rsqrt(jnp.mean(h * h, axis=-1, keepdims=True) + EPS)
        z_ref[...] = ((h * r * g_ref[...]) * (1.0 + sc_ref[...]) + sh_ref[...]).astype(z_ref.dtype)

    row = pl.BlockSpec((tr, d), lambda i: (i, 0))
    vec = pl.BlockSpec((1, d), lambda i: (0, 0))
    return pl.pallas_call(
        body, grid=(rows // tr,), in_specs=[row, row, vec, vec, vec, vec], out_specs=[row, row],
        out_shape=[jax.ShapeDtypeStruct((rows, d), F32), jax.ShapeDtypeStruct((rows, d), BF16)],
        compiler_params=_cparams(("parallel",)), name="res_norm")(x, mo, gate, g, scale, shift)


def _final_fwd_bwd(h1, ff, gate2, final_g, target):
    rows, d = h1.shape
    tr = _row_tile(rows, d, 5)

    def body(h_ref, ff_ref, gt_ref, fg_ref, t_ref, dh_ref, dff_ref, loss_ref, dfg_ref, dgt_ref):
        i = pl.program_id(0)

        @pl.when(i == 0)
        def _():
            loss_ref[...] = jnp.zeros_like(loss_ref)
            dfg_ref[...] = jnp.zeros_like(dfg_ref)
            dgt_ref[...] = jnp.zeros_like(dgt_ref)

        ffv = ff_ref[...].astype(F32)
        h2 = h_ref[...] + gt_ref[...] * ffv
        r = lax.rsqrt(jnp.mean(h2 * h2, axis=-1, keepdims=True) + EPS)
        y = h2 * r
        e = y * fg_ref[...] - t_ref[...]
        loss_ref[...] += 0.5 * jnp.sum(jnp.mean(e * e, axis=-1, keepdims=True))
        dout = e * (1.0 / d)
        dfg_ref[...] += jnp.sum(dout * y, axis=0, keepdims=True)
        dy = dout * fg_ref[...]
        dh2 = r * (dy - y * jnp.mean(dy * y, axis=-1, keepdims=True))
        dh_ref[...] = dh2
        dgt_ref[...] += jnp.sum(dh2 * ffv, axis=0, keepdims=True)
        dff_ref[...] = (dh2 * gt_ref[...]).astype(dff_ref.dtype)

    row = pl.BlockSpec((tr, d), lambda i: (i, 0))
    vec = pl.BlockSpec((1, d), lambda i: (0, 0))
    return pl.pallas_call(
        body, grid=(rows // tr,), in_specs=[row, row, vec, vec, row],
        out_specs=[row, row, pl.BlockSpec((1, LANES), lambda i: (0, 0)), vec, vec],
        out_shape=[jax.ShapeDtypeStruct((rows, d), F32), jax.ShapeDtypeStruct((rows, d), BF16),
                   jax.ShapeDtypeStruct((1, LANES), F32), jax.ShapeDtypeStruct((1, d), F32),
                   jax.ShapeDtypeStruct((1, d), F32)],
        compiler_params=_cparams(("arbitrary",)), name="final_fwd_bwd")(h1, ff, gate2, final_g, target)


def _norm_mod_bwd(dz, hin, dres, g, scale, gate, mo, *, name, carry=None):
    rows, d = hin.shape
    with_gate = gate is not None
    tr = _row_tile(rows, d, 6)

    def body(*refs):
        if with_gate:
            (dz_ref, h_ref, dr_ref, g_ref, sc_ref, gt_ref, mo_ref,
             dh_ref, dsh_ref, dsc_ref, dg_ref, dmo_ref, dgt_ref) = refs
        else:
            dz_ref, h_ref, dr_ref, g_ref, sc_ref, dh_ref, dsh_ref, dsc_ref, dg_ref = refs
        i = pl.program_id(0)

        @pl.when(i == 0)
        def _():
            dsh_ref[...] = jnp.zeros_like(dsh_ref)
            dsc_ref[...] = jnp.zeros_like(dsc_ref)
            dg_ref[...] = jnp.zeros_like(dg_ref)
            if with_gate:
                dgt_ref[...] = jnp.zeros_like(dgt_ref)

        dzv = dz_ref[...].astype(F32)
        h = h_ref[...]
        r = lax.rsqrt(jnp.mean(h * h, axis=-1, keepdims=True) + EPS)
        y = h * r
        dsh_ref[...] += jnp.sum(dzv, axis=0, keepdims=True)
        dsc_ref[...] += jnp.sum(dzv * (y * g_ref[...]), axis=0, keepdims=True)
        dn = dzv * (1.0 + sc_ref[...])
        dg_ref[...] += jnp.sum(dn * y, axis=0, keepdims=True)
        dy = dn * g_ref[...]
        dh = dr_ref[...] + r * (dy - y * jnp.mean(dy * y, axis=-1, keepdims=True))
        dh_ref[...] = dh
        if with_gate:
            dmo_ref[...] = (dh * gt_ref[...]).astype(dmo_ref.dtype)
            dgt_ref[...] += jnp.sum(dh * mo_ref[...].astype(F32), axis=0, keepdims=True)

    row = pl.BlockSpec((tr, d), lambda i: (i, 0))
    vec = pl.BlockSpec((1, d), lambda i: (0, 0))
    vshape = jax.ShapeDtypeStruct((1, d), F32)
    in_specs = [row, row, row, vec, vec]
    args = [dz, hin, dres, g, scale]
    out_specs = [row, vec, vec, vec]
    out_shape = [jax.ShapeDtypeStruct((rows, d), F32), vshape, vshape, vshape]
    if with_gate:
        in_specs += [vec, row]
        args += [gate, mo]
        out_specs += [row, vec]
        out_shape += [jax.ShapeDtypeStruct((rows, d), BF16), vshape]
    outs, carried = _call(
        body, grid=(rows // tr,), in_specs=in_specs, out_specs=out_specs, out_shape=out_shape,
        scratch_shapes=[], semantics=("arbitrary",), name=name, args=args, carry=carry)
    return outs if carry is None else (outs, carried)


def _s5_discretise(a_re, a_im, log_dt, b_re, b_im):
    dt = jnp.exp(log_dt)[:, None]
    er = jnp.exp(a_re * dt)
    lr = er * jnp.cos(a_im * dt)
    li = er * jnp.sin(a_im * dt)
    den = a_re * a_re + a_im * a_im
    cr = ((lr - 1.0) * a_re + li * a_im) / den
    ci = (li * a_re - (lr - 1.0) * a_im) / den
    bbr = cr[..., None] * b_re - ci[..., None] * b_im
    bbi = cr[..., None] * b_im + ci[..., None] * b_re
    return lr, li, bbr, bbi


def _block_diag(w):
    g, r, c = w.shape
    nb = g // GROUPS_PER_BLOCK
    eye = jnp.eye(GROUPS_PER_BLOCK, dtype=w.dtype)
    w5 = w.reshape(nb, GROUPS_PER_BLOCK, r, 1, c) * eye[None, :, None, :, None]
    return w5.reshape(nb, GROUPS_PER_BLOCK * r, GROUPS_PER_BLOCK * c)


def _block_diag_extract(m, r, c):
    nb = m.shape[0]
    m5 = m.reshape(nb, GROUPS_PER_BLOCK, r, GROUPS_PER_BLOCK, c)
    idx = jnp.arange(GROUPS_PER_BLOCK)
    d = m5[:, idx, :, idx, :]
    return jnp.moveaxis(d, 0, 1).reshape(nb * GROUPS_PER_BLOCK, r, c)


def _scan_multipliers(lr, li):
    power = jnp.arange(1, SUBLANES + 1, dtype=F32)[None, :, None]
    er = jnp.exp(power * lr)
    pr = er * jnp.cos(power * li)
    pi = er * jnp.sin(power * li)
    rows = jnp.arange(SUBLANES)[None, :, None]
    fr, fi, rr, ri = [], [], [], []
    for s in (1, 2, 4):
        mf = (rows >= s).astype(F32)
        mr = (rows <= SUBLANES - 1 - s).astype(F32)
        fr.append(mf * pr[:, s - 1:s, :])
        fi.append(mf * pi[:, s - 1:s, :])
        rr.append(mr * pr[:, s - 1:s, :])
        ri.append(mr * pi[:, s - 1:s, :])
    fr.append(pr)
    fi.append(pi)
    rr.append(pr[:, ::-1, :])
    ri.append(pi[:, ::-1, :])
    st = lambda xs: jnp.stack(xs, axis=1)
    return st(fr), st(fi), st(rr), st(ri)


def _scan_rows(sre, sim, mul_r, mul_i, n_groups, reverse):
    sgn = -1.0 if reverse else 1.0
    lanes = sre.shape[1]

    def step(k, carry):
        cr, ci = carry
        kk = (n_groups - 1 - k) if reverse else k
        r0 = pl.multiple_of(kk * SUBLANES, SUBLANES)
        xr = sre[pl.ds(r0, SUBLANES), :]
        xi = sim[pl.ds(r0, SUBLANES), :]
        for lvl, s in enumerate((1, 2, 4)):
            sh = (SUBLANES - s) if reverse else s
            nr = pltpu.roll(xr, sh, 0)
            ni = pltpu.roll(xi, sh, 0)
            mr = mul_r[lvl]
            mi = mul_i[lvl] * sgn
            xr, xi = xr + mr * nr - mi * ni, xi + mr * ni + mi * nr
        mr = mul_r[3]
        mi = mul_i[3] * sgn
        xr, xi = xr + mr * cr - mi * ci, xi + mr * ci + mi * cr
        sre[pl.ds(r0, SUBLANES), :] = xr
        sim[pl.ds(r0, SUBLANES), :] = xi
        edge = 0 if reverse else SUBLANES - 1
        ncr = jnp.broadcast_to(xr[edge:edge + 1, :], (SUBLANES, lanes))
        nci = jnp.broadcast_to(xi[edge:edge + 1, :], (SUBLANES, lanes))
        return ncr, nci

    zero = jnp.zeros((SUBLANES, lanes), F32)
    lax.fori_loop(0, n_groups, step, (zero, zero))


def _dot(a, b):
    return jnp.dot(a, b, preferred_element_type=F32)


def _dotf(a, b):
    return _dot(a.astype(BF16), b)


def _s5_operands(lr, li, bbr, bbi, c_re, c_im):
    g = lr.shape[0]
    nb = g // GROUPS_PER_BLOCK
    tb = lambda w: jnp.swapaxes(w, 1, 2)
    b_in = [_block_diag(tb(bbr)), _block_diag(tb(bbi))]
    c_out = [_block_diag(tb(c_re)), _block_diag(tb(c_im))]
    b_out = [_block_diag(bbr), _block_diag(bbi)]
    c_in = [_block_diag(c_re), _block_diag(c_im)]
    lam_r = lr.reshape(nb, 1, STATE_LANES)
    lam_i = li.reshape(nb, 1, STATE_LANES)
    mults = _scan_multipliers(lam_r, lam_i)
    cast = lambda ws: [w.astype(BF16) for w in ws]
    return cast(b_in), cast(c_out), cast(b_out), cast(c_in), mults


def _s5_fwd(proj, d_skip, b_in, c_out, mults, *, col0, carry=None):
    rows = proj.shape[0]
    nb = b_in[0].shape[0]
    tm = _pick(rows, 512, SUBLANES)
    n_tiles = rows // tm
    s_l = STATE_LANES

    def body(u_ref, dk_ref, br, bi, cr, ci, fr_ref, fi_ref, o_ref, sr_ref, si_ref, sre, sim):
        for t in range(n_tiles):
            rs = pl.ds(t * tm, tm)
            ub = u_ref[rs, :]
            sre[rs, :] = _dot(ub, br[...])
            sim[rs, :] = _dot(ub, bi[...])
        _scan_rows(sre, sim, fr_ref, fi_ref, rows // SUBLANES, False)
        for t in range(n_tiles):
            rs = pl.ds(t * tm, tm)
            srb = sre[rs, :].astype(BF16)
            sib = sim[rs, :].astype(BF16)
            sr_ref[rs, :] = srb
            si_ref[rs, :] = sib
            y0 = _dot(srb, cr[...]) - _dot(sib, ci[...])
            y1 = y0 + dk_ref[...] * u_ref[rs, :].astype(F32)
            o_ref[rs, :] = _gelu(y1).astype(o_ref.dtype)

    mat_in = pl.BlockSpec((None, LANES, s_l), lambda g: (g, 0, 0))
    mat_out = pl.BlockSpec((None, s_l, LANES), lambda g: (g, 0, 0))
    mul = pl.BlockSpec((None, 4, SUBLANES, s_l), lambda g: (g, 0, 0, 0))
    state = pl.BlockSpec((rows, s_l), lambda g: (0, g))
    outs, carried = _call(
        body, grid=(nb,),
        in_specs=[pl.BlockSpec((rows, LANES), lambda g: (0, col0 + g)), pl.BlockSpec((1, LANES), lambda g: (0, g))]
        + [mat_in] * 2 + [mat_out] * 2 + [mul] * 2,
        out_specs=[pl.BlockSpec((rows, LANES), lambda g: (0, g)), state, state],
        out_shape=[jax.ShapeDtypeStruct((rows, nb * LANES), BF16), jax.ShapeDtypeStruct((rows, nb * s_l), BF16),
                   jax.ShapeDtypeStruct((rows, nb * s_l), BF16)],
        scratch_shapes=[pltpu.VMEM((rows, s_l), F32), pltpu.VMEM((rows, s_l), F32)],
        semantics=("parallel",), name="s5_fwd", args=[proj, d_skip, *b_in, *c_out, mults[0], mults[1]], carry=carry)
    return outs if carry is None else (outs, carried)


def _s5_bwd(proj, dyg, d_skip, states, c_out, b_out, c_in, mults, dproj, *, col0, carry=None):
    rows = proj.shape[0]
    nb = c_out[0].shape[0]
    tm = _pick(rows, 512, SUBLANES)
    n_tiles = rows // tm
    s_l = STATE_LANES
    n_groups = rows // SUBLANES
    tn = (((0,), (0,)), ((), ()))

    def body(u_ref, dy_ref, dk_ref, sr_ref, si_ref, cr, ci, bor, boi, cir, cii, rr_ref, ri_ref, dproj_ref,
             du_ref, ddk_ref, dbr_ref, dbi_ref, dcr_ref, dci_ref, dlr_ref, dli_ref,
             gre, gim, dy1):
        ddk = jnp.zeros((1, LANES), F32)
        dcr = jnp.zeros((s_l, LANES), F32)
        dci = jnp.zeros((s_l, LANES), F32)
        for t in range(n_tiles):
            rs = pl.ds(t * tm, tm)
            srb = sr_ref[rs, :]
            sib = si_ref[rs, :]
            uf = u_ref[rs, :].astype(F32)
            y0 = _dot(srb, cr[...]) - _dot(sib, ci[...])
            y1 = y0 + dk_ref[...] * uf
            d1 = dy_ref[rs, :].astype(F32) * _gelu_grad(y1)
            dy1[rs, :] = d1
            ddk = ddk + jnp.sum(d1 * uf, axis=0, keepdims=True)
            d1b = d1.astype(BF16)
            dcr = dcr + lax.dot_general(srb, d1b, tn, preferred_element_type=F32)
            dci = dci - lax.dot_general(sib, d1b, tn, preferred_element_type=F32)
            gre[rs, :] = _dot(d1b, cir[...])
            gim[rs, :] = -_dot(d1b, cii[...])
        ddk_ref[...] = ddk
        dcr_ref[...] = dcr
        dci_ref[...] = dci

        last_row = lax.broadcasted_iota(jnp.int32, (SUBLANES, s_l), 0) == SUBLANES - 1

        def group(r0, s_r, s_i, carry):
            cr_, ci_, ar, ai = carry
            xr = gre[pl.ds(r0, SUBLANES), :]
            xi = gim[pl.ds(r0, SUBLANES), :]
            for lvl, s in enumerate((1, 2, 4)):
                nr = pltpu.roll(xr, SUBLANES - s, 0)
                ni = pltpu.roll(xi, SUBLANES - s, 0)
                mr = rr_ref[lvl]
                mi = ri_ref[lvl]
                xr, xi = xr + mr * nr + mi * ni, xi + mr * ni - mi * nr
            mr = rr_ref[3]
            mi = ri_ref[3]
            xr, xi = xr + mr * cr_ + mi * ci_, xi + mr * ci_ - mi * cr_
            gre[pl.ds(r0, SUBLANES), :] = xr
            gim[pl.ds(r0, SUBLANES), :] = xi
            nxt_r = jnp.where(last_row, cr_, pltpu.roll(xr, SUBLANES - 1, 0))
            nxt_i = jnp.where(last_row, ci_, pltpu.roll(xi, SUBLANES - 1, 0))
            ncr = jnp.broadcast_to(xr[0:1, :], (SUBLANES, s_l))
            nci = jnp.broadcast_to(xi[0:1, :], (SUBLANES, s_l))
            return ncr, nci, ar + nxt_r * s_r + nxt_i * s_i, ai + nxt_i * s_r - nxt_r * s_i

        def rev_step(k, carry):
            r0 = pl.multiple_of((n_groups // 2 - 1 - k) * 2 * SUBLANES, 2 * SUBLANES)
            s_r = sr_ref[pl.ds(r0, 2 * SUBLANES), :].astype(F32)
            s_i = si_ref[pl.ds(r0, 2 * SUBLANES), :].astype(F32)
            carry = group(r0 + SUBLANES, s_r[SUBLANES:], s_i[SUBLANES:], carry)
            return group(r0, s_r[:SUBLANES], s_i[:SUBLANES], carry)

        zero = jnp.zeros((SUBLANES, s_l), F32)
        _, _, ar, ai = lax.fori_loop(0, n_groups // 2, rev_step, (zero, zero, zero, zero))
        dlr_ref[...] = jnp.sum(ar, axis=0, keepdims=True)
        dli_ref[...] = jnp.sum(ai, axis=0, keepdims=True)

        dbr = jnp.zeros((LANES, s_l), F32)
        dbi = jnp.zeros((LANES, s_l), F32)
        for t in range(n_tiles):
            rs = pl.ds(t * tm, tm)
            gr = gre[rs, :]
            gi = gim[rs, :]
            grb = gr.astype(BF16)
            gib = gi.astype(BF16)
            du = _dot(grb, bor[...]) + _dot(gib, boi[...]) + dy1[rs, :] * dk_ref[...]
            du_ref[rs, :] = du.astype(du_ref.dtype)
            ub = u_ref[rs, :]
            dbr = dbr + lax.dot_general(ub, grb, tn, preferred_element_type=F32)
            dbi = dbi + lax.dot_general(ub, gib, tn, preferred_element_type=F32)
        dbr_ref[...] = dbr
        dbi_ref[...] = dbi

    mat_in = pl.BlockSpec((None, LANES, s_l), lambda g: (g, 0, 0))
    mat_out = pl.BlockSpec((None, s_l, LANES), lambda g: (g, 0, 0))
    mul = pl.BlockSpec((None, 4, SUBLANES, s_l), lambda g: (g, 0, 0, 0))
    lam = pl.BlockSpec((None, 1, s_l), lambda g: (g, 0, 0))
    col = pl.BlockSpec((rows, LANES), lambda g: (0, g))
    vec = pl.BlockSpec((1, LANES), lambda g: (0, g))
    state = pl.BlockSpec((rows, s_l), lambda g: (0, g))
    outs, carried = _call(
        body, grid=(nb,),
        in_specs=[pl.BlockSpec((rows, LANES), lambda g: (0, col0 + g)), col, vec]
        + [state] * 2 + [mat_out] * 2 + [mat_out] * 2 + [mat_in] * 2 + [mul] * 2
        + [pl.BlockSpec(memory_space=pl.ANY)],
        out_specs=[pl.BlockSpec((rows, LANES), lambda g: (0, col0 + g)), vec, mat_in, mat_in, mat_out, mat_out,
                   lam, lam],
        out_shape=[jax.ShapeDtypeStruct(dproj.shape, dproj.dtype), jax.ShapeDtypeStruct((1, nb * LANES), F32),
                   jax.ShapeDtypeStruct((nb, LANES, s_l), F32), jax.ShapeDtypeStruct((nb, LANES, s_l), F32),
                   jax.ShapeDtypeStruct((nb, s_l, LANES), F32), jax.ShapeDtypeStruct((nb, s_l, LANES), F32),
                   jax.ShapeDtypeStruct((nb, 1, s_l), F32), jax.ShapeDtypeStruct((nb, 1, s_l), F32)],
        scratch_shapes=[pltpu.VMEM((rows, s_l), F32)] * 2 + [pltpu.VMEM((rows, LANES), F32)],
        semantics=("parallel",), name="s5_bwd",
        args=[proj, dyg, d_skip, *states, *c_out, *b_out, *c_in, mults[2], mults[3], dproj], carry=carry,
        aliases={13: 0})
    return outs if carry is None else (outs, carried)


def _silu(v):
    return v * _sigmoid(v)


def _ada_fwd(c_all, w_shard, b_cols):
    d, n = w_shard.shape
    bn = _pick(n, 512)

    def body(c_ref, w_ref, b_ref, o_ref):
        ca = _silu(c_ref[...]).astype(BF16)
        o_ref[...] = _dot(ca, w_ref[...].astype(BF16)) + b_ref[...]

    return pl.pallas_call(
        body, grid=(n // bn,),
        in_specs=[pl.BlockSpec((N_DEV, d), lambda j: (0, 0)), pl.BlockSpec((d, bn), lambda j: (0, j)),
                  pl.BlockSpec((1, bn), lambda j: (0, j))],
        out_specs=pl.BlockSpec((N_DEV, bn), lambda j: (0, j)),
        out_shape=jax.ShapeDtypeStruct((N_DEV, n), F32),
        compiler_params=_cparams(("parallel",)), name="ada_fwd")(c_all, w_shard, b_cols)


def _ada_bwd(c_all, dmod_cols):
    d = c_all.shape[1]
    n = dmod_cols.shape[1]
    bn = _pick(n, 512)

    def body(c_ref, g_ref, o_ref):
        ca = _silu(c_ref[...]).astype(BF16)
        o_ref[...] = lax.dot_general(ca, g_ref[...].astype(BF16), (((0,), (0,)), ((), ())),
                                     preferred_element_type=F32)

    return pl.pallas_call(
        body, grid=(n // bn,),
        in_specs=[pl.BlockSpec((N_DEV, d), lambda j: (0, 0)), pl.BlockSpec((N_DEV, bn), lambda j: (0, j))],
        out_specs=pl.BlockSpec((d, bn), lambda j: (0, j)),
        out_shape=jax.ShapeDtypeStruct((d, n), F32),
        compiler_params=_cparams(("parallel",)), name="ada_bwd")(c_all, dmod_cols)


def _cast_bf16(w, *, name):
    rows, cols = w.shape
    tr = _row_tile(rows, cols, 3)

    def body(w_ref, o_ref, slot_ref):
        o_ref[...] = w_ref[...].astype(BF16)
        slot_ref[...] = w_ref[...].astype(BF16)

    row = pl.BlockSpec((tr, cols), lambda i: (i, 0))
    return pl.pallas_call(
        body, grid=(rows // tr,), in_specs=[row],
        out_specs=[row, pl.BlockSpec((None, tr, cols), lambda i: (0, i, 0))],
        out_shape=[jax.ShapeDtypeStruct((rows, cols), BF16), jax.ShapeDtypeStruct((N_CHIPS, rows, cols), BF16)],
        compiler_params=_cparams(("parallel",)), name=name)(w)


def _adamw_update(w_ref, g_ref, m_ref, v_ref, d_ref, nm_ref, nv_ref):
    c1 = 1.0 / (1.0 - ADAM_B1 ** ADAM_STEP)
    c2 = 1.0 / (1.0 - ADAM_B2 ** ADAM_STEP)
    gv = g_ref[...]
    nm = ADAM_B1 * m_ref[...] + (1.0 - ADAM_B1) * gv
    nv = ADAM_B2 * v_ref[...] + (1.0 - ADAM_B2) * (gv * gv)
    nm_ref[...] = nm
    nv_ref[...] = nv
    d_ref[...] = -ADAM_LR * ((nm * c1) / (jnp.sqrt(nv * c2) + ADAM_EPS) + ADAM_WD * w_ref[...])


def _adamw_many(ws, gs, ms, vs, *, name):
    n = len(ws)

    def body(*refs):
        for i in range(n):
            _adamw_update(*[refs[k * n + i] for k in range(7)])

    vm = pl.BlockSpec(memory_space=pltpu.VMEM)
    shapes = [jax.ShapeDtypeStruct(a.shape, F32) for a in ws]
    outs = pl.pallas_call(
        body, in_specs=[vm] * (4 * n), out_specs=[vm] * (3 * n), out_shape=shapes * 3,
        compiler_params=pltpu.CompilerParams(vmem_limit_bytes=VMEM_LIMIT_BYTES), name=name)(*ws, *gs, *ms, *vs)
    return list(outs[:n]), list(outs[n:2 * n]), list(outs[2 * n:])


def _adamw(w, g, m, v, *, name, after=None, with_grad=False):
    rows, cols = w.shape
    tr = _row_tile(rows, cols, 8)
    n_out = 4 if with_grad else 3

    def body(w_ref, g_ref, m_ref, v_ref, *rest):
        outs = rest[-n_out:]
        _adamw_update(w_ref, g_ref, m_ref, v_ref, *outs[:3])
        if with_grad:
            outs[3][...] = g_ref[...]

    row = pl.BlockSpec((tr, cols), lambda i: (i, 0))
    shp = jax.ShapeDtypeStruct((rows, cols), F32)
    extra = [] if after is None else [after]
    outs, _ = _call(
        body, grid=(rows // tr,), in_specs=[row] * 4 + [pl.BlockSpec(memory_space=pl.ANY)] * len(extra),
        out_specs=[row] * n_out, out_shape=[shp] * n_out, scratch_shapes=[], semantics=("parallel",), name=name,
        args=[w, g, m, v] + extra)
    return outs


def _sum_leading(a, *, name, out_dtype=F32):
    n, rows, cols = a.shape
    tr = _row_tile(rows, cols, n + 1)

    def body(a_ref, o_ref):
        acc = a_ref[0].astype(F32)
        for i in range(1, n):
            acc = acc + a_ref[i].astype(F32)
        o_ref[...] = acc.astype(o_ref.dtype)

    return pl.pallas_call(
        body, grid=(rows // tr,), in_specs=[pl.BlockSpec((n, tr, cols), lambda i: (0, i, 0))],
        out_specs=pl.BlockSpec((tr, cols), lambda i: (i, 0)),
        out_shape=jax.ShapeDtypeStruct((rows, cols), out_dtype),
        compiler_params=_cparams(("parallel",)), name=name)(a)


def _add_half(dw, land, my_c, *, name):
    n, r, cols = dw.shape
    h = r // 2
    tr = _row_tile(h, cols, 3)
    hb = h // tr

    def body(c_ref, a_ref, b_ref, o_ref):
        o_ref[...] = (a_ref[...].astype(F32) + b_ref[...].astype(F32)).astype(o_ref.dtype)

    gs = pltpu.PrefetchScalarGridSpec(
        num_scalar_prefetch=1, grid=(n, hb),
        in_specs=[pl.BlockSpec((None, tr, cols), lambda s, i, c_ref: (s, c_ref[0] * hb + i, 0)),
                  pl.BlockSpec((None, tr, cols), lambda s, i, c_ref: (s, i, 0))],
        out_specs=pl.BlockSpec((None, tr, cols), lambda s, i, c_ref: (s, i, 0)))
    return pl.pallas_call(
        body, grid_spec=gs, out_shape=jax.ShapeDtypeStruct((n, h, cols), BF16),
        compiler_params=_cparams(("parallel", "parallel")), name=name)(my_c, dw, land)


def _mesh_pos():
    return lax.axis_index("x"), lax.axis_index("y"), lax.axis_index("c")


def _other_chips(x, y):
    return [(1 - x, y), (x, 1 - y), (1 - x, 1 - y)]


def _gather_small(blk, *, name):
    m_per, n = blk.shape

    def body(x_ref, out_ref, send_sems, recv_sems, local_sem):
        x, y, c = _mesh_pos()
        me, sibling = (x, y, c), (x, y, 1 - c)
        chips = _other_chips(x, y)

        def rows(px, py, pc):
            return out_ref.at[pl.ds((4 * px + 2 * py + pc) * m_per, m_per), :]

        def copy(k, block, to, src=None):
            return pltpu.make_async_remote_copy(
                src_ref=rows(*block) if src is None else src, dst_ref=rows(*block),
                send_sem=send_sems.at[k], recv_sem=recv_sems.at[k], device_id=to, device_id_type=MESH)

        mine = pltpu.make_async_copy(x_ref, rows(*me), local_sem)
        mine.start()
        first = [copy(0, me, sibling, src=x_ref)]
        first += [copy(1 + j, me, (*chip, c), src=x_ref) for j, chip in enumerate(chips)]
        for cp in first:
            cp.start()
        passed = [copy(4 + j, (*chip, c), sibling) for j, chip in enumerate(chips)]
        for j, chip in enumerate(chips):
            copy(1 + j, (*chip, c), me).wait_recv()
            passed[j].start()
        copy(0, sibling, me).wait_recv()
        for j, chip in enumerate(chips):
            copy(4 + j, (*chip, 1 - c), me).wait_recv()
        for cp in first + passed:
            cp.wait_send()
        mine.wait()

    return pl.pallas_call(
        body, out_shape=jax.ShapeDtypeStruct((N_DEV * m_per, n), blk.dtype),
        in_specs=[pl.BlockSpec(memory_space=pltpu.VMEM)], out_specs=pl.BlockSpec(memory_space=pltpu.VMEM),
        scratch_shapes=[pltpu.SemaphoreType.DMA((7,)), pltpu.SemaphoreType.DMA((7,)), pltpu.SemaphoreType.DMA],
        compiler_params=pltpu.CompilerParams(vmem_limit_bytes=VMEM_LIMIT_BYTES), name=name)(blk)


def _hbm_specs(n):
    return [pl.BlockSpec(memory_space=pl.ANY)] * n


def _gather_weights(shards):
    n = len(shards)

    def body(*refs):
        ins, outs = refs[:n], refs[n:2 * n]
        send_sems, recv_sems, local_sems = refs[2 * n:]
        x, y, c = _mesh_pos()
        me_chip = 2 * x + y
        sibling = (x, y, 1 - c)
        chips = _other_chips(x, y)

        def half(w, chip_idx, pc):
            h = shards[w].shape[0] // 2
            return outs[w].at[chip_idx, pl.ds(pc * h, h), :]

        def copy(w, k, chip_idx, pc, to, src=None):
            dst = half(w, chip_idx, pc)
            return pltpu.make_async_remote_copy(
                src_ref=dst if src is None else src, dst_ref=dst,
                send_sem=send_sems.at[6 * w + k], recv_sem=recv_sems.at[6 * w + k],
                device_id=to, device_id_type=MESH)

        local = [pltpu.make_async_copy(ins[w], outs[w].at[me_chip], local_sems.at[w]) for w in range(n)]
        for cp in local:
            cp.start()
        sends = []
        for w in range(n):
            h = shards[w].shape[0] // 2
            for j, chip in enumerate(chips):
                cp = copy(w, j, me_chip, c, (*chip, c), src=ins[w].at[pl.ds(c * h, h), :])
                cp.start()
                sends.append(cp)
        for w in range(n):
            for j, chip in enumerate(chips):
                chip_idx = 2 * chip[0] + chip[1]
                copy(w, j, chip_idx, c, (x, y, c)).wait_recv()
                cp = copy(w, 3 + j, chip_idx, c, sibling)
                cp.start()
                sends.append(cp)
        for w in range(n):
            for j, chip in enumerate(chips):
                copy(w, 3 + j, 2 * chip[0] + chip[1], 1 - c, (x, y, c)).wait_recv()
        for cp in sends:
            cp.wait_send()
        for cp in local:
            cp.wait()

    return pl.pallas_call(
        body, out_shape=[jax.ShapeDtypeStruct((N_CHIPS,) + s.shape, s.dtype) for s in shards],
        in_specs=_hbm_specs(n), out_specs=_hbm_specs(n),
        scratch_shapes=[pltpu.SemaphoreType.DMA((6 * n,)), pltpu.SemaphoreType.DMA((6 * n,)),
                        pltpu.SemaphoreType.DMA((n,))],
        name="gather_weights")(*shards)


def _swap_halves(dws, *, name):
    n = len(dws)

    def body(*refs):
        ins, outs = refs[:n], refs[n:2 * n]
        send_sems, recv_sems = refs[2 * n:]
        x, y, c = _mesh_pos()
        cps = []
        for w in range(n):
            h = dws[w].shape[1] // 2
            cp = pltpu.make_async_remote_copy(
                src_ref=ins[w].at[:, pl.ds((1 - c) * h, h), :], dst_ref=outs[w],
                send_sem=send_sems.at[w], recv_sem=recv_sems.at[w],
                device_id=(x, y, 1 - c), device_id_type=MESH)
            cp.start()
            cps.append(cp)
        for cp in cps:
            cp.wait()

    return pl.pallas_call(
        body, out_shape=[jax.ShapeDtypeStruct((s.shape[0], s.shape[1] // 2, s.shape[2]), s.dtype) for s in dws],
        in_specs=_hbm_specs(n), out_specs=_hbm_specs(n),
        scratch_shapes=[pltpu.SemaphoreType.DMA((n,)), pltpu.SemaphoreType.DMA((n,))],
        name=name)(*dws)


def _chip_exchange(parts):
    n = len(parts)

    def body(*refs):
        ins, outs = refs[:n], refs[n:2 * n]
        send_sems, recv_sems, local_sems = refs[2 * n:]
        x, y, c = _mesh_pos()
        me_chip = 2 * x + y
        chips = _other_chips(x, y)
        local = [pltpu.make_async_copy(ins[w].at[me_chip], outs[w].at[me_chip], local_sems.at[w]) for w in range(n)]
        for cp in local:
            cp.start()
        cps = []
        for w in range(n):
            for j, chip in enumerate(chips):
                cp = pltpu.make_async_remote_copy(
                    src_ref=ins[w].at[2 * chip[0] + chip[1]], dst_ref=outs[w].at[me_chip],
                    send_sem=send_sems.at[3 * w + j], recv_sem=recv_sems.at[3 * w + j],
                    device_id=(*chip, c), device_id_type=MESH)
                cp.start()
                cps.append((cp, w, j, chip))
        for cp, w, j, chip in cps:
            slot = outs[w].at[2 * chip[0] + chip[1]]
            pltpu.make_async_remote_copy(
                src_ref=slot, dst_ref=slot, send_sem=send_sems.at[3 * w + j], recv_sem=recv_sems.at[3 * w + j],
                device_id=(x, y, c), device_id_type=MESH).wait_recv()
        for cp, _, _, _ in cps:
            cp.wait_send()
        for cp in local:
            cp.wait()

    return pl.pallas_call(
        body, out_shape=[jax.ShapeDtypeStruct(s.shape, s.dtype) for s in parts],
        in_specs=_hbm_specs(n), out_specs=_hbm_specs(n),
        scratch_shapes=[pltpu.SemaphoreType.DMA((3 * n,)), pltpu.SemaphoreType.DMA((3 * n,)),
                        pltpu.SemaphoreType.DMA((n,))],
        name="chip_exchange")(*parts)


def _join_halves(halves):
    n = len(halves)

    def body(*refs):
        ins, outs = refs[:n], refs[n:2 * n]
        send_sems, recv_sems, local_sems = refs[2 * n:]
        x, y, c = _mesh_pos()
        cps, local = [], []
        for w in range(n):
            h = halves[w].shape[0]
            mine = outs[w].at[pl.ds(c * h, h), :]
            lc = pltpu.make_async_copy(ins[w], mine, local_sems.at[w])
            lc.start()
            local.append(lc)
            cp = pltpu.make_async_remote_copy(
                src_ref=ins[w], dst_ref=mine, send_sem=send_sems.at[w], recv_sem=recv_sems.at[w],
                device_id=(x, y, 1 - c), device_id_type=MESH)
            cp.start()
            cps.append(cp)
        for w in range(n):
            h = halves[w].shape[0]
            theirs = outs[w].at[pl.ds((1 - c) * h, h), :]
            pltpu.make_async_remote_copy(
                src_ref=theirs, dst_ref=theirs, send_sem=send_sems.at[w], recv_sem=recv_sems.at[w],
                device_id=(x, y, c), device_id_type=MESH).wait_recv()
        for cp in cps:
            cp.wait_send()
        for lc in local:
            lc.wait()

    return pl.pallas_call(
        body, out_shape=[jax.ShapeDtypeStruct((2 * s.shape[0], s.shape[1]), s.dtype) for s in halves],
        in_specs=_hbm_specs(n), out_specs=_hbm_specs(n),
        scratch_shapes=[pltpu.SemaphoreType.DMA((n,)), pltpu.SemaphoreType.DMA((n,)), pltpu.SemaphoreType.DMA((n,))],
        name="join_halves")(*halves)


def _cast_into_slot(w, chip, after, *, name):
    rows, cols = w.shape
    tr = _row_tile(rows, cols, 2)

    def body(chip_ref, w_ref, after_ref, o_ref):
        o_ref[...] = w_ref[...].astype(BF16)

    gs = pltpu.PrefetchScalarGridSpec(
        num_scalar_prefetch=1, grid=(rows // tr,),
        in_specs=[pl.BlockSpec((tr, cols), lambda i, chip_ref: (i, 0)), pl.BlockSpec(memory_space=pl.ANY)],
        out_specs=pl.BlockSpec((None, tr, cols), lambda i, chip_ref: (chip_ref[0], i, 0)))
    return pl.pallas_call(
        body, grid_spec=gs, out_shape=jax.ShapeDtypeStruct((N_CHIPS, rows, cols), BF16),
        compiler_params=_cparams(("parallel",)), name=name)(chip, w, after)


def _row_range(h, lo, hi, parts):
    step = h // parts
    assert step * parts == h and step % (2 * SUBLANES) == 0, (h, parts)
    return lo * step, (hi - lo) * step


def _gather_carry(items):
    n_copies = sum(len(js) for _, js, _, _, _ in items)
    sem = pltpu.SemaphoreType.DMA((2 * n_copies,))

    def copies(outs, sems):
        send_sems, recv_sems = sems
        x, y, c = _mesh_pos()
        me_chip = 2 * x + y
        chips = _other_chips(x, y)
        out_ici, in_ici, out_d2d, in_d2d = [], [], [], []
        k = 0
        for w, (buf, js, lo, hi, parts) in enumerate(items):
            h = buf.shape[1] // 2
            r0, nr = _row_range(h, lo, hi, parts)

            def copy(k, chip_idx, pc, to):
                ref = outs[w].at[chip_idx, pl.ds(pc * h + r0, nr), :]
                return pltpu.make_async_remote_copy(
                    src_ref=ref, dst_ref=ref, send_sem=send_sems.at[k], recv_sem=recv_sems.at[k],
                    device_id=to, device_id_type=MESH)

            for j in js:
                chip = chips[j]
                chip_idx = 2 * chip[0] + chip[1]
                out_ici.append(copy(k, me_chip, c, (*chip, c)))
                in_ici.append(copy(k, chip_idx, c, (x, y, c)))
                out_d2d.append(copy(k + 1, chip_idx, c, (x, y, 1 - c)))
                in_d2d.append(copy(k + 1, chip_idx, 1 - c, (x, y, c)))
                k += 2
        return out_ici, in_ici, out_d2d, in_d2d

    def start(ins, outs, sems):
        for cp in copies(outs, sems)[0]:
            cp.start()

    def finish(ins, outs, sems):
        out_ici, in_ici, out_d2d, in_d2d = copies(outs, sems)
        for arrived, onward in zip(in_ici, out_d2d):
            arrived.wait_recv()
            onward.start()
        for arrived in in_d2d:
            arrived.wait_recv()
        for cp in out_ici + out_d2d:
            cp.wait_send()

    bufs = [it[0] for it in items]
    shapes = [jax.ShapeDtypeStruct(b.shape, b.dtype) for b in bufs]
    return _Carry(bufs, shapes, {i: i for i in range(len(bufs))}, [sem, sem], start, finish)


def _exchange_carry(items):
    n = len(items)
    sem = pltpu.SemaphoreType.DMA((3 * n,))
    given = [w for w in range(n) if items[w][1] is not None]

    def copies(ins, outs, sems):
        send_sems, recv_sems = sems
        x, y, c = _mesh_pos()
        chips = _other_chips(x, y)
        sends, recvs = [], []
        for w, (part, _, lo, hi, parts) in enumerate(items):
            r0, nr = _row_range(part.shape[1], lo, hi, parts)
            for j, chip in enumerate(chips):
                land = outs[w].at[j, pl.ds(r0, nr), :]
                sends.append(pltpu.make_async_remote_copy(
                    src_ref=ins[w].at[2 * chip[0] + chip[1], pl.ds(r0, nr), :], dst_ref=land,
                    send_sem=send_sems.at[3 * w + j], recv_sem=recv_sems.at[3 * w + j],
                    device_id=(*chip, c), device_id_type=MESH))
                recvs.append(pltpu.make_async_remote_copy(
                    src_ref=land, dst_ref=land,
                    send_sem=send_sems.at[3 * w + j], recv_sem=recv_sems.at[3 * w + j],
                    device_id=(x, y, c), device_id_type=MESH))
        return sends, recvs

    def start(ins, outs, sems):
        for cp in copies(ins, outs, sems)[0]:
            cp.start()

    def finish(ins, outs, sems):
        sends, recvs = copies(ins, outs, sems)
        for cp in recvs:
            cp.wait_recv()
        for cp in sends:
            cp.wait_send()

    inputs = [it[0] for it in items] + [items[w][1] for w in given]
    shapes = [jax.ShapeDtypeStruct((3,) + it[0].shape[1:], it[0].dtype) for it in items]
    aliases = {n + i: w for i, w in enumerate(given)}
    return _Carry(inputs, shapes, aliases, [sem, sem], start, finish)


def _sum_into_half(part, landed, chip, my_c, *, name):
    _, h, cols = part.shape
    tr = _row_tile(h, cols, 5)
    hb = h // tr

    def body(chip_ref, c_ref, p_ref, l_ref, o_ref):
        acc = p_ref[...].astype(F32)
        for j in range(3):
            acc = acc + l_ref[j].astype(F32)
        o_ref[...] = acc

    gs = pltpu.PrefetchScalarGridSpec(
        num_scalar_prefetch=2, grid=(hb,),
        in_specs=[pl.BlockSpec((None, tr, cols), lambda i, chip_ref, c_ref: (chip_ref[0], i, 0)),
                  pl.BlockSpec((3, tr, cols), lambda i, chip_ref, c_ref: (0, i, 0))],
        out_specs=pl.BlockSpec((tr, cols), lambda i, chip_ref, c_ref: (c_ref[0] * hb + i, 0)))
    return pl.pallas_call(
        body, grid_spec=gs, out_shape=jax.ShapeDtypeStruct((2 * h, cols), F32),
        compiler_params=_cparams(("parallel",)), name=name)(chip, my_c, part, landed)


def _join_carry(fulls):
    n = len(fulls)
    sem = pltpu.SemaphoreType.DMA((n,))

    def copies(outs, sems):
        send_sems, recv_sems = sems
        x, y, c = _mesh_pos()
        sends, recvs = [], []
        for w in range(n):
            h = fulls[w].shape[0] // 2
            mine = outs[w].at[pl.ds(c * h, h), :]
            theirs = outs[w].at[pl.ds((1 - c) * h, h), :]
            sends.append(pltpu.make_async_remote_copy(
                src_ref=mine, dst_ref=mine, send_sem=send_sems.at[w], recv_sem=recv_sems.at[w],
                device_id=(x, y, 1 - c), device_id_type=MESH))
            recvs.append(pltpu.make_async_remote_copy(
                src_ref=theirs, dst_ref=theirs, send_sem=send_sems.at[w], recv_sem=recv_sems.at[w],
                device_id=(x, y, c), device_id_type=MESH))
        return sends, recvs

    def start(ins, outs, sems):
        for cp in copies(outs, sems)[0]:
            cp.start()

    def finish(ins, outs, sems):
        sends, recvs = copies(outs, sems)
        for cp in recvs:
            cp.wait_recv()
        for cp in sends:
            cp.wait_send()

    shapes = [jax.ShapeDtypeStruct(f.shape, f.dtype) for f in fulls]
    return _Carry(fulls, shapes, {i: i for i in range(n)}, [sem, sem], start, finish)


class _NoComm:
    def __init__(self, big):
        self.big = big
        self.grads = {}

    def weight(self, name):
        return self.big[name]

    def mm_in(self, u, afters):
        return _mm(u, self.big["w_in"], mode="nn", out_dtype=BF16, name="mm_in")

    def mm_d_in(self, dproj):
        return _mm(dproj, self.big["w_in"], mode="nt", out_dtype=F32, name="mm_d_in")

    def carry(self, site, args=()):
        return None

    def done(self, site, carried, out=None):
        return out

    def grad(self, name, dw):
        self.grads[name] = dw

    def early_grads(self, early):
        self.early = early


def _gather_rows_carry(blk):
    m_per = blk.shape[0]
    sem = pltpu.SemaphoreType.DMA((7,))

    def copies(ins, outs, sems):
        send_sems, recv_sems, local_sem = sems
        x, y, c = _mesh_pos()
        me, sibling = (x, y, c), (x, y, 1 - c)
        chips = _other_chips(x, y)

        def rows(px, py, pc):
            return outs[0].at[pl.ds((4 * px + 2 * py + pc) * m_per, m_per), :]

        def copy(k, block, to, src=None):
            return pltpu.make_async_remote_copy(
                src_ref=rows(*block) if src is None else src, dst_ref=rows(*block),
                send_sem=send_sems.at[k], recv_sem=recv_sems.at[k], device_id=to, device_id_type=MESH)

        mine = pltpu.make_async_copy(ins[0], rows(*me), local_sem.at[0])
        first = [copy(0, me, sibling, src=ins[0])]
        first += [copy(1 + j, me, (*chip, c), src=ins[0]) for j, chip in enumerate(chips)]
        passed = [copy(4 + j, (*chip, c), sibling) for j, chip in enumerate(chips)]
        landed = [copy(1 + j, (*chip, c), me) for j, chip in enumerate(chips)]
        from_sibling = [copy(0, sibling, me)] + [copy(4 + j, (*chip, 1 - c), me) for j, chip in enumerate(chips)]
        return mine, first, passed, landed, from_sibling

    def start(ins, outs, sems):
        mine, first, _, _, _ = copies(ins, outs, sems)
        mine.start()
        for cp in first:
            cp.start()

    def finish(ins, outs, sems):
        mine, first, passed, landed, from_sibling = copies(ins, outs, sems)
        for arrived, onward in zip(landed, passed):
            arrived.wait_recv()
            onward.start()
        for arrived in from_sibling:
            arrived.wait_recv()
        for cp in first + passed:
            cp.wait_send()
        mine.wait()

    shape = jax.ShapeDtypeStruct((N_DEV * m_per, blk.shape[1]), blk.dtype)
    return _Carry([blk], [shape], {}, [sem, sem, pltpu.SemaphoreType.DMA((1,))], start, finish)


def _gather_fresh_carry(own, js):
    n = len(js)
    h = own.shape[0] // 2
    sem = pltpu.SemaphoreType.DMA((2 * n,))

    def copies(ins, outs, sems):
        send_sems, recv_sems = sems
        x, y, c = _mesh_pos()
        chips = _other_chips(x, y)
        out_ici, in_ici, out_d2d, in_d2d = [], [], [], []

        def copy(k, src, dst, to):
            return pltpu.make_async_remote_copy(
                src_ref=src, dst_ref=dst, send_sem=send_sems.at[k], recv_sem=recv_sems.at[k],
                device_id=to, device_id_type=MESH)

        for jj, j in enumerate(js):
            mine = ins[0].at[pl.ds(c * h, h), :]
            land = outs[0].at[jj, pl.ds(c * h, h), :]
            other = outs[0].at[jj, pl.ds((1 - c) * h, h), :]
            out_ici.append(copy(2 * jj, mine, land, (*chips[j], c)))
            in_ici.append(copy(2 * jj, land, land, (x, y, c)))
            out_d2d.append(copy(2 * jj + 1, land, land, (x, y, 1 - c)))
            in_d2d.append(copy(2 * jj + 1, other, other, (x, y, c)))
        return out_ici, in_ici, out_d2d, in_d2d

    def start(ins, outs, sems):
        for cp in copies(ins, outs, sems)[0]:
            cp.start()

    def finish(ins, outs, sems):
        out_ici, in_ici, out_d2d, in_d2d = copies(ins, outs, sems)
        for arrived, onward in zip(in_ici, out_d2d):
            arrived.wait_recv()
            onward.start()
        for arrived in in_d2d:
            arrived.wait_recv()
        for cp in out_ici + out_d2d:
            cp.wait_send()

    return _Carry([own], [jax.ShapeDtypeStruct((n,) + own.shape, own.dtype)], {}, [sem, sem], start, finish)


def _w_in_copies(own_ref, land_ref, send_sems, recv_sems):
    x, y, c = _mesh_pos()
    h = own_ref.shape[0] // 2
    return [pltpu.make_async_remote_copy(
        src_ref=own_ref.at[pl.ds(c * h, h), :], dst_ref=land_ref.at[1 + j, pl.ds(c * h, h), :],
        send_sem=send_sems[j], recv_sem=recv_sems[j], device_id=(*chip, c), device_id_type=MESH)
        for j, chip in enumerate(_other_chips(x, y))]


def _w_in_send(own, land, after):
    hbm = pl.BlockSpec(memory_space=pltpu.HBM)
    sem = pl.BlockSpec(memory_space=pltpu.SEMAPHORE)
    land_shape = land.shape

    def body(own_ref, land_ref, after_ref, s0, s1, s2, r0, r1, r2, own_thru, land_thru, token):
        for cp in _w_in_copies(own_ref, land_ref, (s0, s1, s2), (r0, r1, r2)):
            cp.start()
        token[...] = jnp.zeros_like(token)

    outs = pl.pallas_call(
        body, name="w_in_send",
        out_shape=(pltpu.SemaphoreType.DMA(()),) * 6 + (
            pltpu.HBM(own.shape, own.dtype), pltpu.HBM(land_shape, own.dtype), jax.ShapeDtypeStruct((8, LANES), F32)),
        in_specs=(hbm, hbm, pl.BlockSpec(memory_space=pl.ANY)),
        out_specs=(sem,) * 6 + (hbm, hbm, pl.BlockSpec(memory_space=pltpu.VMEM)),
        input_output_aliases={0: 6, 1: 7},
        compiler_params=pltpu.CompilerParams(has_side_effects=pltpu.SideEffectType.DATAFLOW_SIDE_EFFECTING),
    )(pltpu.with_memory_space_constraint(own, pltpu.HBM), pltpu.with_memory_space_constraint(land, pltpu.HBM), after)
    return outs[:6], outs[6], outs[7], outs[8]


def _w_in_wait(sems, own, land, afters):
    hbm = pl.BlockSpec(memory_space=pltpu.HBM)
    sem = pl.BlockSpec(memory_space=pltpu.SEMAPHORE)
    n_after = len(afters)

    def body(own_ref, land_ref, s0, s1, s2, r0, r1, r2, *rest):
        for cp in _w_in_copies(own_ref, land_ref, (s0, s1, s2), (r0, r1, r2)):
            cp.wait_send()
            cp.wait_recv()

    return pl.pallas_call(
        body, name="w_in_wait", out_shape=(pltpu.HBM(own.shape, own.dtype), pltpu.HBM(land.shape, land.dtype)),
        in_specs=(hbm, hbm) + (sem,) * 6 + (pl.BlockSpec(memory_space=pl.ANY),) * n_after, out_specs=(hbm, hbm),
        input_output_aliases={0: 0, 1: 1},
        compiler_params=pltpu.CompilerParams(has_side_effects=pltpu.SideEffectType.DATAFLOW_SIDE_EFFECTING),
    )(own, land, *sems, *afters)


def _exchange_copies(part_refs, land_refs, send_sems, recv_sems):
    x, y, c = _mesh_pos()
    cps = []
    for w, (part, land) in enumerate(zip(part_refs, land_refs)):
        for j, chip in enumerate(_other_chips(x, y)):
            cps.append(pltpu.make_async_remote_copy(
                src_ref=part.at[2 * chip[0] + chip[1]], dst_ref=land.at[j],
                send_sem=send_sems[3 * w + j], recv_sem=recv_sems[3 * w + j],
                device_id=(*chip, c), device_id_type=MESH))
    return cps


def _exchange_send(parts, through, *, name):
    n = len(parts)
    hbm = pl.BlockSpec(memory_space=pltpu.HBM)
    sem = pl.BlockSpec(memory_space=pltpu.SEMAPHORE)
    any_spec = pl.BlockSpec(memory_space=pl.ANY)
    land_shapes = [(3,) + p.shape[1:] for p in parts]

    def body(*refs):
        part_refs, land_refs = refs[:n], refs[n:2 * n]
        sems = refs[2 * n + 1:8 * n + 1]
        for cp in _exchange_copies(part_refs, land_refs, sems[:3 * n], sems[3 * n:]):
            cp.start()

    outs = pl.pallas_call(
        body, name=name,
        out_shape=(pltpu.SemaphoreType.DMA(()),) * (6 * n)
        + tuple(pltpu.HBM(p.shape, p.dtype) for p in parts)
        + tuple(pltpu.HBM(s, p.dtype) for s, p in zip(land_shapes, parts))
        + (jax.ShapeDtypeStruct(through.shape, through.dtype),),
        in_specs=(hbm,) * (2 * n) + (any_spec,), out_specs=(sem,) * (6 * n) + (hbm,) * (2 * n) + (any_spec,),
        input_output_aliases={i: 6 * n + i for i in range(2 * n + 1)},
        compiler_params=pltpu.CompilerParams(has_side_effects=pltpu.SideEffectType.DATAFLOW_SIDE_EFFECTING),
    )(*[pltpu.with_memory_space_constraint(p, pltpu.HBM) for p in parts],
      *[pltpu.with_memory_space_constraint(lax.empty(s, p.dtype), pltpu.HBM) for s, p in zip(land_shapes, parts)],
      through)
    return outs[:6 * n], outs[6 * n:7 * n], outs[7 * n:8 * n], outs[8 * n]


def _exchange_wait(sems, parts, lands, afters, *, name):
    n = len(parts)
    hbm = pl.BlockSpec(memory_space=pltpu.HBM)
    sem = pl.BlockSpec(memory_space=pltpu.SEMAPHORE)

    def body(*refs):
        part_refs, land_refs = refs[:n], refs[n:2 * n]
        sem_refs = refs[2 * n:8 * n]
        for cp in _exchange_copies(part_refs, land_refs, sem_refs[:3 * n], sem_refs[3 * n:]):
            cp.wait_send()
            cp.wait_recv()

    outs = pl.pallas_call(
        body, name=name,
        out_shape=tuple(pltpu.HBM(p.shape, p.dtype) for p in parts) + tuple(pltpu.HBM(l.shape, l.dtype) for l in lands),
        in_specs=(hbm,) * (2 * n) + (sem,) * (6 * n) + (pl.BlockSpec(memory_space=pl.ANY),) * len(afters),
        out_specs=(hbm,) * (2 * n), input_output_aliases={i: i for i in range(2 * n)},
        compiler_params=pltpu.CompilerParams(has_side_effects=pltpu.SideEffectType.DATAFLOW_SIDE_EFFECTING),
    )(*parts, *lands, *sems, *afters)
    return outs[:n], outs[n:]


def _join_copies(full_refs, send_sems, recv_sems):
    x, y, c = _mesh_pos()
    cps = []
    for w, full in enumerate(full_refs):
        h = full.shape[0] // 2
        mine = full.at[pl.ds(c * h, h), :]
        cps.append(pltpu.make_async_remote_copy(
            src_ref=mine, dst_ref=mine, send_sem=send_sems[w], recv_sem=recv_sems[w],
            device_id=(x, y, 1 - c), device_id_type=MESH))
    return cps


def _join_send(fulls, *, name):
    n = len(fulls)
    hbm = pl.BlockSpec(memory_space=pltpu.HBM)
    sem = pl.BlockSpec(memory_space=pltpu.SEMAPHORE)

    def body(*refs):
        sems = refs[n:3 * n]
        for cp in _join_copies(refs[:n], sems[:n], sems[n:]):
            cp.start()
        token = refs[-1]
        token[...] = jnp.zeros_like(token)

    outs = pl.pallas_call(
        body, name=name,
        out_shape=(pltpu.SemaphoreType.DMA(()),) * (2 * n) + tuple(pltpu.HBM(f.shape, f.dtype) for f in fulls)
        + (jax.ShapeDtypeStruct((SUBLANES, LANES), F32),),
        in_specs=(hbm,) * n,
        out_specs=(sem,) * (2 * n) + (hbm,) * n + (pl.BlockSpec(memory_space=pltpu.VMEM),),
        input_output_aliases={i: 2 * n + i for i in range(n)},
        compiler_params=pltpu.CompilerParams(has_side_effects=pltpu.SideEffectType.DATAFLOW_SIDE_EFFECTING),
    )(*[pltpu.with_memory_space_constraint(f, pltpu.HBM) for f in fulls])
    return outs[:2 * n], list(outs[2 * n:3 * n]), outs[3 * n]


def _join_wait(sems, fulls, afters, *, name):
    n = len(fulls)
    hbm = pl.BlockSpec(memory_space=pltpu.HBM)
    sem = pl.BlockSpec(memory_space=pltpu.SEMAPHORE)

    def body(*refs):
        sem_refs = refs[n:3 * n]
        for cp in _join_copies(refs[:n], sem_refs[:n], sem_refs[n:]):
            cp.wait_send()
            cp.wait_recv()

    outs = pl.pallas_call(
        body, name=name, out_shape=tuple(pltpu.HBM(f.shape, f.dtype) for f in fulls),
        in_specs=(hbm,) * n + (sem,) * (2 * n) + (pl.BlockSpec(memory_space=pl.ANY),) * len(afters),
        out_specs=(hbm,) * n, input_output_aliases={i: i for i in range(n)},
        compiler_params=pltpu.CompilerParams(has_side_effects=pltpu.SideEffectType.DATAFLOW_SIDE_EFFECTING),
    )(*fulls, *sems, *afters)
    return list(outs)


def _gather_ici_copies(buf_refs, send_sems, recv_sems):
    x, y, c = _mesh_pos()
    me_chip = 2 * x + y
    cps = []
    for w, buf in enumerate(buf_refs):
        h = buf.shape[1] // 2
        ref = buf.at[me_chip, pl.ds(c * h, h), :]
        for j, chip in enumerate(_other_chips(x, y)):
            cps.append(pltpu.make_async_remote_copy(
                src_ref=ref, dst_ref=ref, send_sem=send_sems[3 * w + j], recv_sem=recv_sems[3 * w + j],
                device_id=(*chip, c), device_id_type=MESH))
    return cps


def _gather_send(bufs, *, name):
    n = len(bufs)
    hbm = pl.BlockSpec(memory_space=pltpu.HBM)
    sem = pl.BlockSpec(memory_space=pltpu.SEMAPHORE)

    def body(*refs):
        sems = refs[n:7 * n]
        for cp in _gather_ici_copies(refs[:n], sems[:3 * n], sems[3 * n:]):
            cp.start()

    outs = pl.pallas_call(
        body, name=name,
        out_shape=(pltpu.SemaphoreType.DMA(()),) * (6 * n) + tuple(pltpu.HBM(b.shape, b.dtype) for b in bufs),
        in_specs=(hbm,) * n, out_specs=(sem,) * (6 * n) + (hbm,) * n,
        input_output_aliases={i: 6 * n + i for i in range(n)},
        compiler_params=pltpu.CompilerParams(has_side_effects=pltpu.SideEffectType.DATAFLOW_SIDE_EFFECTING),
    )(*[pltpu.with_memory_space_constraint(b, pltpu.HBM) for b in bufs])
    send_sems, recv_sems = outs[:3 * n], outs[3 * n:6 * n]
    per_buf = [tuple(send_sems[3 * w:3 * w + 3]) + tuple(recv_sems[3 * w:3 * w + 3]) for w in range(n)]
    return per_buf, list(outs[6 * n:])


def _gather_wait(sems, bufs, afters, *, name):
    n = len(bufs)
    hbm = pl.BlockSpec(memory_space=pltpu.HBM)
    sem = pl.BlockSpec(memory_space=pltpu.SEMAPHORE)
    flat = [s for six in sems for s in six[:3]] + [s for six in sems for s in six[3:]]

    def body(*refs):
        sem_refs = refs[n:7 * n]
        for cp in _gather_ici_copies(refs[:n], sem_refs[:3 * n], sem_refs[3 * n:]):
            cp.wait_send()
            cp.wait_recv()

    outs = pl.pallas_call(
        body, name=name, out_shape=tuple(pltpu.HBM(b.shape, b.dtype) for b in bufs),
        in_specs=(hbm,) * n + (sem,) * (6 * n) + (pl.BlockSpec(memory_space=pl.ANY),) * len(afters),
        out_specs=(hbm,) * n, input_output_aliases={i: i for i in range(n)},
        compiler_params=pltpu.CompilerParams(has_side_effects=pltpu.SideEffectType.DATAFLOW_SIDE_EFFECTING),
    )(*bufs, *flat, *afters)
    return list(outs)


def _forward_abs_carry(bufs):
    n = len(bufs)
    sem = pltpu.SemaphoreType.DMA((3 * n,))

    def copies(outs, sems):
        send_sems, recv_sems = sems
        x, y, c = _mesh_pos()
        sends, recvs = [], []
        for w in range(n):
            h = bufs[w].shape[1] // 2
            for j, chip in enumerate(_other_chips(x, y)):
                slot = 2 * chip[0] + chip[1]
                mine = outs[w].at[slot, pl.ds(c * h, h), :]
                other = outs[w].at[slot, pl.ds((1 - c) * h, h), :]
                sends.append(pltpu.make_async_remote_copy(
                    src_ref=mine, dst_ref=mine, send_sem=send_sems.at[3 * w + j], recv_sem=recv_sems.at[3 * w + j],
                    device_id=(x, y, 1 - c), device_id_type=MESH))
                recvs.append(pltpu.make_async_remote_copy(
                    src_ref=other, dst_ref=other, send_sem=send_sems.at[3 * w + j], recv_sem=recv_sems.at[3 * w + j],
                    device_id=(x, y, c), device_id_type=MESH))
        return sends, recvs

    def start(ins, outs, sems):
        for cp in copies(outs, sems)[0]:
            cp.start()

    def finish(ins, outs, sems):
        sends, recvs = copies(outs, sems)
        for cp in recvs:
            cp.wait_recv()
        for cp in sends:
            cp.wait_send()

    shapes = [jax.ShapeDtypeStruct(b.shape, b.dtype) for b in bufs]
    return _Carry(bufs, shapes, {i: i for i in range(n)}, [sem, sem], start, finish)


def _forward_carry(land):
    n = land.shape[0] - 1
    h = land.shape[1] // 2
    sem = pltpu.SemaphoreType.DMA((n,))

    def copies(outs, sems):
        send_sems, recv_sems = sems
        x, y, c = _mesh_pos()
        sends, recvs = [], []
        for j in range(n):
            mine = outs[0].at[1 + j, pl.ds(c * h, h), :]
            other = outs[0].at[1 + j, pl.ds((1 - c) * h, h), :]
            sends.append(pltpu.make_async_remote_copy(
                src_ref=mine, dst_ref=mine, send_sem=send_sems.at[j], recv_sem=recv_sems.at[j],
                device_id=(x, y, 1 - c), device_id_type=MESH))
            recvs.append(pltpu.make_async_remote_copy(
                src_ref=other, dst_ref=other, send_sem=send_sems.at[j], recv_sem=recv_sems.at[j],
                device_id=(x, y, c), device_id_type=MESH))
        return sends, recvs

    def start(ins, outs, sems):
        for cp in copies(outs, sems)[0]:
            cp.start()

    def finish(ins, outs, sems):
        sends, recvs = copies(outs, sems)
        for cp in recvs:
            cp.wait_recv()
        for cp in sends:
            cp.wait_send()

    return _Carry([land], [jax.ShapeDtypeStruct(land.shape, land.dtype)], {0: 0}, [sem, sem], start, finish)


def _swap_carry(dws):
    n = len(dws)
    sem = pltpu.SemaphoreType.DMA((n,))

    def copies(ins, outs, sems):
        send_sems, recv_sems = sems
        x, y, c = _mesh_pos()
        cps = []
        for w in range(n):
            h = dws[w].shape[1] // 2
            cps.append(pltpu.make_async_remote_copy(
                src_ref=ins[w].at[:, pl.ds((1 - c) * h, h), :], dst_ref=outs[w],
                send_sem=send_sems.at[w], recv_sem=recv_sems.at[w],
                device_id=(x, y, 1 - c), device_id_type=MESH))
        return cps

    def start(ins, outs, sems):
        for cp in copies(ins, outs, sems):
            cp.start()

    def finish(ins, outs, sems):
        for cp in copies(ins, outs, sems):
            cp.wait()

    shapes = [jax.ShapeDtypeStruct((s.shape[0], s.shape[1] // 2, s.shape[2]), s.dtype) for s in dws]
    return _Carry(dws, shapes, {}, [sem, sem], start, finish)


def _merge_carries(carries):
    if len(carries) == 1:
        return carries[0]
    inputs, out_shapes, sem_shapes, aliases, spans = [], [], [], {}, []
    for cy in carries:
        i0, o0, s0 = len(inputs), len(out_shapes), len(sem_shapes)
        aliases.update({i0 + i: o0 + o for i, o in cy.aliases.items()})
        inputs += cy.inputs
        out_shapes += cy.out_shapes
        sem_shapes += cy.sem_shapes
        spans.append((slice(i0, len(inputs)), slice(o0, len(out_shapes)), slice(s0, len(sem_shapes))))

    def start(ins, outs, sems):
        for cy, (si, so, ss) in zip(carries, spans):
            cy.start(ins[si], outs[so], sems[ss])

    def finish(ins, outs, sems):
        for cy, (si, so, ss) in zip(carries, spans):
            cy.finish(ins[si], outs[so], sems[ss])

    return _Carry(inputs, out_shapes, aliases, sem_shapes, start, finish)


ALL_CHIPS = (0, 1, 2)


class _MeshComm:
    GATHER_AT = {}
    FORWARD_AT = {
        "conv_fwd": ["w_conv_out", "w_glu_a", "w_glu_b", "w_out"],
        "mm_out": ["w_ff1"],
        "mm_ff1": ["w_ff2"],
    }
    SWAP_AT = {
        "mm_d_ff2": ["w_ff2"],
        "mm_d_ff1": ["w_ff1"],
        "conv_bwd": ["w_out", "w_glu_a", "w_glu_b", "w_conv_out"],
    }
    EXCHANGE_AT = {}
    EARLY_AT = "mm_dw_in"

    def __init__(self, shards, pos, chip, my_c):
        self.pos = pos
        self.chip = chip
        self.my_c = my_c
        self.shards = shards
        self.w_in_own, self.w_in_rel = _cast_bf16(shards["w_in"], name="cast_w_in")
        self.raw = {}
        self.flights = []
        self.parts = {}
        self.landing = {}
        self.halves = {}
        self.pending = {}
        self.last_site = {}
        for site, items in self.EXCHANGE_AT.items():
            for it in items:
                self.last_site[it[0]] = site

    def weight(self, name):
        g = self.bufs[name]
        return g.reshape(g.shape[0] * g.shape[1], g.shape[2]) if name in ROW_SHARDED else g

    def _slot_ids(self):
        x, y, _ = self.pos
        ids = [2 * x + y] + [2 * cx + cy for cx, cy in _other_chips(x, y)]
        return jnp.stack(ids).astype(jnp.int32)

    def start_w_in(self, after):
        *self.w_in_flight, token = _w_in_send(self.w_in_own, self.w_in_rel, after)
        order = [n for names in self.FORWARD_AT.values() for n in names]
        casts = [_cast_into_slot(self.shards[n], self.chip, token, name="cast_" + n) for n in order]
        sems, bufs = _gather_send(casts, name="gather_send")
        self.bufs = dict(zip(order, bufs))
        self.gather_sems = dict(zip(order, sems))
        return token

    def mm_in(self, u, afters):
        ids = self._slot_ids()
        sems, own, land = self.w_in_flight
        proj = _mm_slots(u, own[None], ids[0:1], None, name="mm_in_own")
        own, land = _w_in_wait(sems, own, land, [proj] + list(self.bufs.values()) + list(afters))
        land, = _run_carry(_forward_carry(land), name="forward_w_in")
        proj = _mm_slots(u, land, ids[1:4], proj, name="mm_in_rest", first=1)
        self.w_in_rel = land
        return proj

    def _add_and_send(self, names, landed, site, through):
        parts = [_add_half(self.raw.pop(n), l1, self.my_c, name="add_half_" + n) for n, l1 in zip(names, landed)]
        sems, parts, lands, through = _exchange_send(parts, through, name="exchange_send_" + site)
        self.flights.append((names, sems, parts, lands))
        return through

    def mm_d_in(self, dproj):
        landed = _run_carry(_swap_carry([self.raw["w_in"]]), name="swap_halves_w_in")
        dproj = self._add_and_send(["w_in"], landed, "w_in", dproj)
        return _mm(dproj, self.w_in_rel, mode="nt", out_dtype=F32, name="mm_d_in", a_slots=self._slot_ids())

    def early_grads(self, early):
        self.early = early

    def carry(self, site, args=()):
        jobs = []
        if site in self.FORWARD_AT:
            names = self.FORWARD_AT[site]
            landed = _gather_wait([self.gather_sems.pop(n) for n in names], [self.bufs[n] for n in names],
                                  [args[0]], name="gather_wait_" + site)
            jobs.append(("gather", [(n,) for n in names], _forward_abs_carry(landed)))
        if site == self.EARLY_AT:
            flat, self.early_offs = _pack(list(self.early.values()))
            jobs.append(("early", None, _gather_rows_carry(flat.reshape(-1, PACK_COLS))))
        if site in self.GATHER_AT:
            items = self.GATHER_AT[site]
            jobs.append(("gather", items, _gather_carry([(self.bufs[it[0]],) + tuple(it[1:]) for it in items])))
        if site in self.EXCHANGE_AT:
            items = self.EXCHANGE_AT[site]
            jobs.append(("exchange", items, _exchange_carry(
                [(self.parts[it[0]], self.landing.get(it[0])) + tuple(it[1:]) for it in items])))
        if site in self.SWAP_AT:
            names = self.SWAP_AT[site]
            jobs.append(("swap", names, _swap_carry([self.raw[n] for n in names])))
        if not jobs:
            return None
        self.pending[site] = jobs
        return _merge_carries([job[2] for job in jobs])

    def done(self, site, carried, out=None):
        pos = 0
        for kind, items, carry in self.pending.pop(site):
            outs = carried[pos:pos + len(carry.out_shapes)]
            pos += len(carry.out_shapes)
            if kind == "early":
                self.early_all = outs[0]
            elif kind == "gather":
                self.bufs.update(zip([it[0] for it in items], outs))
            elif kind == "swap":
                if isinstance(out, (list, tuple)):
                    out = [self._add_and_send(items, outs, site, out[0])] + list(out[1:])
                else:
                    out = self._add_and_send(items, outs, site, out)
            else:
                for it, landed in zip(items, outs):
                    n = it[0]
                    self.landing[n] = landed
                    if self.last_site[n] == site:
                        self.halves[n] = _sum_into_half(self.parts.pop(n), self.landing.pop(n), self.chip,
                                                        self.my_c, name="sum_chips_" + n)
        return out

    def grad(self, name, dw):
        if name in ROW_SHARDED:
            dw = dw.reshape(N_CHIPS, dw.shape[0] // N_CHIPS, dw.shape[1])
        self.raw[name] = dw

    def join_start(self, names, afters):
        for i, (group, sems, parts, lands) in enumerate(self.flights):
            parts, lands = _exchange_wait(sems, parts, lands, afters, name="exchange_wait_%d" % i)
            for n, part, land in zip(group, parts, lands):
                self.halves[n] = _sum_into_half(part, land, self.chip, self.my_c, name="sum_chips_" + n)
        self.flights = []
        sems, fulls, token = _join_send([self.halves.pop(n) for n in names], name="join_send")
        self.join_flight = (names, sems, fulls)
        return token

    def join_finish(self, afters):
        names, sems, fulls = self.join_flight
        return dict(zip(names, _join_wait(sems, fulls, afters, name="join_wait")))


def _local_step(x, target, mod, small, comm):
    rows, d = x.shape
    cw = d // 2
    shift1, scale1, gate1, shift2, scale2, gate2 = mod
    _, _, bbr, bbi = small["s5_disc"]
    b_in, c_out, b_out, c_in, mults = _s5_operands(*small["s5_loglam"], bbr, bbi, small["c_re"], small["c_im"])
    wt = comm.weight

    def riding(site, fn, *args, **kwargs):
        carry = comm.carry(site, args)
        if carry is None:
            return fn(*args, **kwargs)
        out, carried = fn(*args, carry=carry, **kwargs)
        return comm.done(site, carried, out)

    u = _norm_mod(x, small["norm1_g"], scale1, shift1, name="norm1_fwd")
    proj = comm.mm_in(u, [*b_in, *c_out, *b_out, *c_in, *mults])
    sl, cv = riding("conv_fwd", _conv_fwd, proj, small["w_dw"], small["b_dw"], small["ln_g"], small["ln_b"], cw=cw)
    y_conv = _mm(sl, wt("w_conv_out"), mode="nn", out_dtype=BF16, name="mm_conv_out")
    yg, st_re, st_im = riding("s5_fwd", _s5_fwd, proj, small["d_skip"], b_in, c_out, mults, col0=2 * cw // LANES)
    ya = riding("mm_glu_a", _mm, yg, wt("w_glu_a"), mode="nn", out_dtype=BF16, name="mm_glu_a")
    yb = riding("mm_glu_b", _mm, yg, wt("w_glu_b"), mode="nn", out_dtype=BF16, name="mm_glu_b")
    merged = _merge_fwd(proj, y_conv, ya, yb, cw=cw)
    mo = riding("mm_out", _mm, merged, wt("w_out"), mode="nn", out_dtype=BF16, name="mm_out")
    h1, z = _res_norm(x, mo, gate1, small["norm2_g"], scale2, shift2)
    f1 = riding("mm_ff1", _mm, z, wt("w_ff1"), mode="nn", out_dtype=BF16, name="mm_ff1")
    ff = _mm(f1, wt("w_ff2"), mode="nn", out_dtype=BF16, name="mm_ff2", a_fn=_relu2_bf16)
    dh2, dff, loss, d_final_g, d_gate2 = _final_fwd_bwd(h1, ff, gate2, small["final_g"], target)

    comm.grad("w_ff2", _mm(f1, dff, mode="tn", out_dtype=BF16, name="mm_dw_ff2", a_fn=_relu2_bf16))
    df1 = riding("mm_d_ff2", _mm, dff, wt("w_ff2"), mode="nt", out_dtype=BF16, name="mm_d_ff2", extra=f1,
                 epi=lambda acc, f: acc * (2.0 * jnp.maximum(f.astype(F32), 0.0)))
    comm.grad("w_ff1", riding("mm_dw_ff1", _mm, z, df1, mode="tn", out_dtype=BF16, name="mm_dw_ff1",
                              out_gathered=True))
    dz = riding("mm_d_ff1", _mm, df1, wt("w_ff1"), mode="nt", out_dtype=F32, name="mm_d_ff1")
    dh1, d_shift2, d_scale2, d_norm2_g, dmo, d_gate1 = riding(
        "norm2_bwd", _norm_mod_bwd, dz, h1, dh2, small["norm2_g"], scale2, gate1, mo, name="norm2_bwd")
    comm.grad("w_out", riding("mm_dw_out", _mm, merged, dmo, mode="tn", out_dtype=BF16, name="mm_dw_out"))
    dmerged = riding("mm_d_out", _mm, dmo, wt("w_out"), mode="nt", out_dtype=BF16, name="mm_d_out")
    dproj, dy_conv, dya, dyb = riding("merge_bwd", _merge_bwd, dmerged, proj, y_conv, ya, yb, cw=cw)
    comm.grad("w_glu_a", _mm(yg, dya, mode="tn", out_dtype=BF16, name="mm_dw_glu_a", out_gathered=True))
    comm.grad("w_glu_b", _mm(yg, dyb, mode="tn", out_dtype=BF16, name="mm_dw_glu_b", out_gathered=True))
    dyg_a = _mm(dya, wt("w_glu_a"), mode="nt", out_dtype=F32, name="mm_d_glu_a")
    dyg = _mm(dyb, wt("w_glu_b"), mode="nt", out_dtype=F32, name="mm_d_glu_b", extra=dyg_a,
              epi=lambda acc, e: acc + e)
    comm.grad("w_conv_out", _mm(sl, dy_conv, mode="tn", out_dtype=BF16, name="mm_dw_conv_out", out_gathered=True))
    dsl = _mm(dy_conv, wt("w_conv_out"), mode="nt", out_dtype=F32, name="mm_d_conv_out")
    dcv, d_ln_g, d_ln_b = _ln_bwd(dsl, cv, small["ln_g"], small["ln_b"])
    dproj, d_w_dw, d_b_dw = riding("conv_bwd", _conv_bwd, dcv, proj, small["w_dw"], dproj, cw=cw)
    dproj, d_d_skip, dbr, dbi, dcr, dci, dlr, dli = riding(
        "s5_bwd", _s5_bwd, proj, dyg, small["d_skip"], (st_re, st_im), c_out, b_out, c_in, mults, dproj,
        col0=2 * cw // LANES)
    sw = lambda m: jnp.swapaxes(m, 1, 2)
    early = {
        "dmod_tail": jnp.concatenate([d_gate1, d_shift2, d_scale2, d_gate2], axis=1), "loss": loss[:, 0:1],
        "w_dw": d_w_dw, "b_dw": d_b_dw, "ln_g": d_ln_g, "ln_b": d_ln_b,
        "lam_re": dlr.reshape(-1, SSM_STATE), "lam_im": dli.reshape(-1, SSM_STATE),
        "bb_re": sw(_block_diag_extract(dbr, SSM_GROUP, SSM_STATE)),
        "bb_im": sw(_block_diag_extract(dbi, SSM_GROUP, SSM_STATE)),
        "c_re": sw(_block_diag_extract(dcr, SSM_STATE, SSM_GROUP)),
        "c_im": sw(_block_diag_extract(dci, SSM_STATE, SSM_GROUP)),
        "d_skip": d_d_skip, "norm2_g": d_norm2_g, "final_g": d_final_g,
    }
    comm.early_grads(early)
    comm.grad("w_in", riding("mm_dw_in", _mm, u, dproj, mode="tn", out_dtype=BF16, name="mm_dw_in",
                             out_gathered=True))
    du = comm.mm_d_in(dproj)
    grad_x, d_shift1, d_scale1, d_norm1_g = riding(
        "norm1_bwd", _norm_mod_bwd, du, x, dh1, small["norm1_g"], scale1, None, None, name="norm1_bwd")
    late ={"dmod_head": jnp.concatenate([d_shift1, d_scale1], axis=1), "norm1_g": d_norm1_g}
    return grad_x, early, late


WEIGHT_NAMES = ["w_ada", "b_ada", "norm1_g", "w_in", "w_dw", "b_dw", "ln_g", "ln_b", "w_conv_out", "a_re", "a_im",
                "log_dt", "b_re", "b_im", "c_re", "c_im", "d_skip", "w_glu_a", "w_glu_b", "w_out", "norm2_g",
                "w_ff1", "w_ff2", "final_g"]
BIG_NAMES = ["w_in", "w_conv_out", "w_glu_a", "w_glu_b", "w_out", "w_ff1", "w_ff2"]
ROW_SHARDED = ("w_out", "w_ff2")
PACK_COLS = 1024
PACK_TILE = SUBLANES * PACK_COLS


def _pack(arrays):
    flats = [a.reshape(-1) for a in arrays]
    offs = []
    total = 0
    for f in flats:
        offs.append(total)
        total += f.shape[0]
    pad = (-total) % PACK_TILE
    if pad:
        flats.append(jnp.zeros((pad,), F32))
    return jnp.concatenate(flats), offs


def _unpack(flat, offs, like):
    return [flat[o:o + a.size].reshape(a.shape) for o, a in zip(offs, like)]


def _gather_w_dw(w_shard):
    k, n = w_shard.shape
    padded = jnp.pad(w_shard, ((0, HALO - k), (0, 0)))
    allw = _gather_small(padded, name="gather_w_dw").reshape(N_CHIPS, 2, HALO, n)[:, 0, :k]
    return jnp.moveaxis(allw, 0, 1).reshape(k, N_CHIPS * n)


def kernel(x, c, w_ada, b_ada, norm1_g, w_in, w_dw, b_dw, ln_g, ln_b, w_conv_out, a_re, a_im, log_dt, b_re, b_im, c_re, c_im, d_skip, w_glu_a, w_glu_b, w_out, norm2_g, w_ff1, w_ff2, final_g, loss_target, m_w_ada, m_b_ada, m_norm1_g, m_w_in, m_w_dw, m_b_dw, m_ln_g, m_ln_b, m_w_conv_out, m_a_re, m_a_im, m_log_dt, m_b_re, m_b_im, m_c_re, m_c_im, m_d_skip, m_w_glu_a, m_w_glu_b, m_w_out, m_norm2_g, m_w_ff1, m_w_ff2, m_final_g, v_w_ada, v_b_ada, v_norm1_g, v_w_in, v_w_dw, v_b_dw, v_ln_g, v_ln_b, v_w_conv_out, v_a_re, v_a_im, v_log_dt, v_b_re, v_b_im, v_c_re, v_c_im, v_d_skip, v_w_glu_a, v_w_glu_b, v_w_out, v_norm2_g, v_w_ff1, v_w_ff2, v_final_g):
    given = dict(locals())
    w = {n: given[n] for n in WEIGHT_NAMES}
    m = {n: given["m_" + n] for n in WEIGHT_NAMES}
    v = {n: given["v_" + n] for n in WEIGHT_NAMES}
    d = x.shape[2]
    xi, yi, ci = _mesh_pos()
    chip = 2 * xi + yi
    dev = 4 * xi + 2 * yi + ci
    my_c = jnp.reshape(ci, (1,)).astype(jnp.int32)
    chip_arr = jnp.reshape(chip, (1,)).astype(jnp.int32)

    comm = _MeshComm({n: w[n][0] for n in BIG_NAMES}, (xi, yi, ci), chip_arr, my_c)

    ndw = w_dw.shape[2]
    assert d // SUBLANES == ndw
    first = jnp.concatenate([c.reshape(SUBLANES, ndw), jnp.pad(w_dw[0], ((0, HALO - CONV_KERNEL), (0, 0)))])
    first_all = _gather_small(first, name="gather_c_w_dw").reshape(N_DEV, SUBLANES + HALO, ndw)
    c_all = first_all[:, :SUBLANES].reshape(N_DEV, d)
    taps = first_all.reshape(N_CHIPS, 2, SUBLANES + HALO, ndw)[:, 0, SUBLANES:SUBLANES + CONV_KERNEL]
    w_dw_full = jnp.moveaxis(taps, 0, 1).reshape(CONV_KERNEL, N_CHIPS * ndw)

    nmod = w_ada.shape[2]
    b_cols = lax.dynamic_slice(b_ada, (0, chip * nmod), (1, nmod))
    mod_part = _ada_fwd(c_all, w_ada[0], b_cols)
    mod_all = _gather_small(mod_part, name="gather_mod").reshape(N_CHIPS, 2, N_DEV, nmod)[:, 0]
    mod_full = jnp.moveaxis(mod_all, 0, 1).reshape(N_DEV, N_CHIPS * nmod)
    mod_row = lax.dynamic_slice(mod_full, (dev, 0), (1, N_CHIPS * nmod))
    mod = [mod_row[:, i * d:(i + 1) * d] for i in range(6)]

    token = comm.start_w_in(mod_row)
    log_dt_0 = log_dt[0] + token[0, 0]

    disc_in = (a_re[0], a_im[0], log_dt_0, b_re[0], b_im[0])
    disc, disc_vjp = jax.vjp(_s5_discretise, *disc_in)
    dt = jnp.exp(log_dt_0)[:, None]
    small = {"norm1_g": norm1_g, "w_dw": w_dw_full, "b_dw": b_dw, "ln_g": ln_g, "ln_b": ln_b,
             "c_re": c_re[0], "c_im": c_im[0], "d_skip": d_skip, "norm2_g": norm2_g,
             "final_g": final_g[None, :], "s5_disc": disc, "s5_loglam": (a_re[0] * dt, a_im[0] * dt)}

    grad_x, early, late = _local_step(x[0], loss_target[0], mod, small, comm)
    grads = {}

    early_all = comm.early_all.reshape(N_DEV, -1, PACK_COLS)
    early_sum = _sum_leading(early_all, name="sum_small_grads").reshape(-1)
    summed = dict(zip(early, _unpack(early_sum, comm.early_offs, list(early.values()))))
    flat, late_offs = _pack(list(late.values()))
    late_all = _gather_small(flat.reshape(-1, PACK_COLS), name="gather_late_grads").reshape(N_DEV, -1, PACK_COLS)
    late_sum = _sum_leading(late_all, name="sum_late_grads").reshape(-1)
    summed.update(zip(late, _unpack(late_sum, late_offs, list(late.values()))))
    head = late_all[:, :2 * d // PACK_COLS].reshape(N_DEV, 2 * d)
    tail = early_all[:, :4 * d // PACK_COLS].reshape(N_DEV, 4 * d)
    dmod_all = jnp.concatenate([head, tail], axis=1)

    grads["w_ada"] = _ada_bwd(c_all, lax.dynamic_slice(dmod_all, (0, chip * nmod), (N_DEV, nmod)))
    grads["b_ada"] = _sum_leading(dmod_all.reshape(N_DEV, SUBLANES, 6 * d // SUBLANES),
                                  name="sum_b_ada").reshape(1, 6 * d)
    da_re, da_im, dlog_dt, db_re, db_im = disc_vjp(
        (summed["lam_re"], summed["lam_im"], summed["bb_re"], summed["bb_im"]))
    grads.update({
        "norm1_g": summed["norm1_g"], "w_dw": lax.dynamic_slice(summed["w_dw"], (0, chip * ndw), (CONV_KERNEL, ndw)),
        "b_dw": summed["b_dw"], "ln_g": summed["ln_g"], "ln_b": summed["ln_b"],
        "a_re": da_re, "a_im": da_im, "log_dt": dlog_dt, "b_re": db_re, "b_im": db_im,
        "c_re": summed["c_re"], "c_im": summed["c_im"], "d_skip": summed["d_skip"],
        "norm2_g": summed["norm2_g"], "final_g": summed["final_g"],
    })

    delta, new_m, new_v = {}, {}, {}

    def adam_big(n, after=None):
        shp = w[n].shape
        two_d = lambda a: a.reshape(shp[1], shp[2])
        res = _adamw(two_d(w[n]), two_d(grads[n]), two_d(m[n]), two_d(v[n]), name="adamw_" + n, after=after,
                     with_grad=n in BIG_NAMES)
        delta[n], new_m[n], new_v[n] = [r.reshape(shp) for r in res[:3]]
        if n in BIG_NAMES:
            grads[n] = res[3]

    token = comm.join_start(BIG_NAMES, [late_all])
    adam_big("w_ada", token)
    grads.update(comm.join_finish([delta["w_ada"]]))
    for n in BIG_NAMES:
        adam_big(n)
    grads = {n: grads[n].reshape(w[n].shape) for n in WEIGHT_NAMES}
    rest = [n for n in WEIGHT_NAMES if n not in delta]
    as_2d = lambda a: a.reshape(1, -1) if a.ndim == 1 else a
    outs = _adamw_many(*[[as_2d(src[n]) for n in rest] for src in (w, grads, m, v)], name="adamw_small")
    for dst, arrays in zip((delta, new_m, new_v), outs):
        for n, a in zip(rest, arrays):
            dst[n] = a.reshape(w[n].shape)

    return (summed["loss"].reshape(()), grad_x[None], *[grads[n] for n in WEIGHT_NAMES],
            *[delta[n] for n in WEIGHT_NAMES], *[new_m[n] for n in WEIGHT_NAMES],
            *[new_v[n] for n in WEIGHT_NAMES])
```

```python
import math

import jax
import jax.numpy as jnp
from jax import lax
from jax.experimental import pallas as pl
from jax.experimental.pallas import tpu as pltpu

F32 = jnp.float32
BF16 = jnp.bfloat16
EPS = 1e-6
CONV_KERNEL = 31
SSM_GROUP = 16
SSM_STATE = 64
ADAM_LR = 0.001
ADAM_B1 = 0.9
ADAM_B2 = 0.999
ADAM_EPS = 1e-08
ADAM_WD = 0.01
ADAM_STEP = 10

N_CHIPS = 4
N_DEV = 8
VMEM_LIMIT_BYTES = 56 * 1024 * 1024
LANES = 128
SUBLANES = 8
HALO = 32
GROUPS_PER_BLOCK = LANES // SSM_GROUP
STATE_LANES = GROUPS_PER_BLOCK * SSM_STATE
MESH = pl.DeviceIdType.MESH


def _cparams(sem):
    return pltpu.CompilerParams(dimension_semantics=sem, vmem_limit_bytes=VMEM_LIMIT_BYTES)


def _pick(n, pref, mult=LANES):
    if n <= pref:
        return n
    best = None
    for d in range(mult, pref + 1, mult):
        if n % d == 0:
            best = d
    assert best is not None, (n, pref)
    return best


def _sigmoid(v):
    return 1.0 / (1.0 + jnp.exp(-v))


def _gelu_parts(v):
    k0 = math.sqrt(2.0 / math.pi)
    inner = k0 * (v + 0.044715 * v * v * v)
    t = jnp.tanh(inner)
    return k0, t


def _gelu(v):
    _, t = _gelu_parts(v)
    return 0.5 * v * (1.0 + t)


def _gelu_grad(v):
    k0, t = _gelu_parts(v)
    return 0.5 * (1.0 + t) + 0.5 * v * (1.0 - t * t) * k0 * (1.0 + 3.0 * 0.044715 * v * v)


def _relu2_bf16(a):
    t = jnp.maximum(a.astype(F32), 0.0)
    return (t * t).astype(BF16)


class _Carry:
    def __init__(self, inputs, out_shapes, aliases, sem_shapes, start, finish):
        self.inputs = list(inputs)
        self.out_shapes = list(out_shapes)
        self.aliases = dict(aliases)
        self.sem_shapes = list(sem_shapes)
        self.start = start
        self.finish = finish


def _call(body, *, grid, in_specs, out_specs, out_shape, scratch_shapes, semantics, name, args, carry=None,
          prefetch=(), aliases=None):
    n_in, n_out, n_scr, n_pf = len(in_specs), len(out_specs), len(scratch_shapes), len(prefetch)
    own_aliases = {n_pf + i: o for i, o in (aliases or {}).items()}
    if carry is None:
        gs = pltpu.PrefetchScalarGridSpec(
            num_scalar_prefetch=n_pf, grid=grid, in_specs=in_specs, out_specs=out_specs,
            scratch_shapes=scratch_shapes)
        outs = pl.pallas_call(
            body, grid_spec=gs, out_shape=out_shape, input_output_aliases=own_aliases,
            compiler_params=_cparams(semantics), name=name)(*prefetch, *args)
        return list(outs), []
    ci, co = len(carry.inputs), len(carry.out_shapes)

    def wrapped(*refs):
        pf, refs = refs[:n_pf], refs[n_pf:]
        ins, cins = refs[:n_in], refs[n_in:n_in + ci]
        p = n_in + ci
        outs, couts = refs[p:p + n_out], refs[p + n_out:p + n_out + co]
        p += n_out + co
        scr, csems = refs[p:p + n_scr], refs[p + n_scr:]
        first = pl.program_id(0) == 0
        last = pl.program_id(0) == grid[0] - 1
        for ax in range(1, len(grid)):
            first = jnp.logical_and(first, pl.program_id(ax) == 0)
            last = jnp.logical_and(last, pl.program_id(ax) == grid[ax] - 1)

        @pl.when(first)
        def _():
            carry.start(cins, couts, csems)

        body(*pf, *ins, *outs, *scr)

        @pl.when(last)
        def _():
            carry.finish(cins, couts, csems)

    any_spec = pl.BlockSpec(memory_space=pl.ANY)
    gs = pltpu.PrefetchScalarGridSpec(
        num_scalar_prefetch=n_pf, grid=grid, in_specs=list(in_specs) + [any_spec] * ci,
        out_specs=list(out_specs) + [any_spec] * co, scratch_shapes=list(scratch_shapes) + carry.sem_shapes)
    all_aliases = dict(own_aliases)
    all_aliases.update({n_pf + n_in + i: n_out + o for i, o in carry.aliases.items()})
    outs = pl.pallas_call(
        wrapped, grid_spec=gs, out_shape=list(out_shape) + carry.out_shapes, input_output_aliases=all_aliases,
        compiler_params=_cparams(("arbitrary",) * len(grid)), name=name)(*prefetch, *args, *carry.inputs)
    return list(outs[:n_out]), list(outs[n_out:])


def _run_carry(carry, *, name):
    ci = len(carry.inputs)

    def body(*refs):
        cins, couts, csems = refs[:ci], refs[ci:ci + len(carry.out_shapes)], refs[ci + len(carry.out_shapes):]
        carry.start(cins, couts, csems)
        carry.finish(cins, couts, csems)

    any_spec = pl.BlockSpec(memory_space=pl.ANY)
    outs = pl.pallas_call(
        body, in_specs=[any_spec] * ci, out_specs=[any_spec] * len(carry.out_shapes), out_shape=carry.out_shapes,
        scratch_shapes=carry.sem_shapes, input_output_aliases=carry.aliases, name=name)(*carry.inputs)
    return list(outs)


def _mm(a, b, *, mode, out_dtype, name, out_gathered=False, a_fn=None, epi=None, extra=None,
        bm_pref=1024, bn_pref=1024, bk_pref=2048, carry=None, a_slots=None):
    gathered = (b.ndim == 3)
    if mode == "nn":
        m, kdim = a.shape
        ns = b.shape[-1]
        n = ns * (N_CHIPS if gathered else 1)
        bm, bn, bk = _pick(m, bm_pref), _pick(ns, bn_pref), _pick(kdim, bk_pref)
        npb = ns // bn
        grid = (m // bm, n // bn, kdim // bk)
        a_spec = pl.BlockSpec((bm, bk), lambda i, j, k: (i, k))
        if gathered:
            b_spec = pl.BlockSpec((None, bk, bn), lambda i, j, k: (j // npb, k, j % npb))
        else:
            b_spec = pl.BlockSpec((bk, bn), lambda i, j, k: (k, j))
        o_spec = pl.BlockSpec((bm, bn), lambda i, j, k: (i, j))
        e_spec = pl.BlockSpec((bm, bn), lambda i, j, k: (i, j))
        out_shape = (m, n)
        acc_shape = (bm, bn)
        dims = (((1,), (0,)), ((), ()))
    elif mode == "nt":
        m = a.shape[0]
        kdim, ns = b.shape[-2], b.shape[-1]
        n = ns * (N_CHIPS if gathered else 1)
        assert a.shape[1] == n
        bm, bko, bnr = _pick(m, bm_pref), _pick(kdim, bn_pref), _pick(ns, bk_pref)
        npb = ns // bnr
        grid = (m // bm, kdim // bko, n // bnr)
        a_spec = pl.BlockSpec((bm, bnr), lambda i, j, k: (i, k))
        if gathered:
            b_spec = pl.BlockSpec((None, bko, bnr), lambda i, j, k: (k // npb, j, k % npb))
        else:
            b_spec = pl.BlockSpec((bko, bnr), lambda i, j, k: (j, k))
        o_spec = pl.BlockSpec((bm, bko), lambda i, j, k: (i, j))
        e_spec = pl.BlockSpec((bm, bko), lambda i, j, k: (i, j))
        if a_slots is not None:
            assert gathered and extra is None
            a_spec = pl.BlockSpec((bm, bnr), lambda i, j, k, s_ref: (i, s_ref[k // npb] * npb + k % npb))
            b_spec = pl.BlockSpec((None, bko, bnr), lambda i, j, k, s_ref: (k // npb, j, k % npb))
            o_spec = pl.BlockSpec((bm, bko), lambda i, j, k, s_ref: (i, j))
        out_shape = (m, kdim)
        acc_shape = (bm, bko)
        dims = (((1,), (1,)), ((), ()))
    else:
        m, kdim = a.shape
        n = b.shape[1]
        ns = n // N_CHIPS if out_gathered else n
        bmr, bko, bn = _pick(m, bk_pref), _pick(kdim, bm_pref), _pick(ns, bn_pref)
        npb = ns // bn
        grid = (kdim // bko, n // bn, m // bmr)
        a_spec = pl.BlockSpec((bmr, bko), lambda i, j, k: (k, i))
        b_spec = pl.BlockSpec((bmr, bn), lambda i, j, k: (k, j))
        if out_gathered:
            o_spec = pl.BlockSpec((None, bko, bn), lambda i, j, k: (j // npb, i, j % npb))
            out_shape = (N_CHIPS, kdim, ns)
        else:
            o_spec = pl.BlockSpec((bko, bn), lambda i, j, k: (i, j))
            out_shape = (kdim, n)
        e_spec = None
        acc_shape = (bko, bn)
        dims = (((0,), (0,)), ((), ()))
    nk = grid[2]

    def body(*refs):
        if a_slots is not None:
            refs = refs[1:]
        if extra is not None:
            a_ref, b_ref, e_ref, o_ref, acc = refs
        else:
            a_ref, b_ref, o_ref, acc = refs
            e_ref = None
        k = pl.program_id(2)
        av = a_ref[...]
        if a_fn is not None:
            av = a_fn(av)
        part = lax.dot_general(av, b_ref[...], dims, preferred_element_type=F32)

        def finish(r):
            if epi is not None:
                r = epi(r, e_ref[...])
            o_ref[...] = r.astype(o_ref.dtype)

        if nk == 1:
            finish(part)
            return

        @pl.when(k == 0)
        def _():
            acc[...] = part

        @pl.when(jnp.logical_and(k > 0, k < nk - 1))
        def _():
            acc[...] += part

        @pl.when(k == nk - 1)
        def _():
            finish(acc[...] + part)

    in_specs = [a_spec, b_spec]
    args = [a, b]
    if extra is not None:
        in_specs.append(e_spec)
        args.append(extra)
    outs, carried = _call(body, grid=grid, in_specs=in_specs, out_specs=[o_spec],
                          out_shape=[jax.ShapeDtypeStruct(out_shape, out_dtype)],
                          scratch_shapes=[pltpu.VMEM(acc_shape, F32)],
                          semantics=("parallel", "parallel", "arbitrary"), name=name, args=args, carry=carry,
                          prefetch=() if a_slots is None else (a_slots,))
    return outs[0] if carry is None else (outs[0], carried)


def _mm_slots(a, wbuf, slots, prev, *, name, carry=None, first=0):
    m, kdim = a.shape
    ns = wbuf.shape[2]
    bm, bn = _pick(m, 1024), _pick(ns, 1024)
    npb = ns // bn
    grid = (m // bm, slots.shape[0], npb)

    def body(s_ref, a_ref, b_ref, *rest):
        o_ref = rest[-1]
        o_ref[...] = _dot(a_ref[...], b_ref[...]).astype(o_ref.dtype)

    in_specs = [pl.BlockSpec((bm, kdim), lambda i, s, j, s_ref: (i, 0)),
                pl.BlockSpec((None, kdim, bn), lambda i, s, j, s_ref: (first + s, 0, j))]
    args = [a, wbuf]
    aliases = None
    if prev is not None:
        in_specs.append(pl.BlockSpec(memory_space=pl.ANY))
        args.append(prev)
        aliases = {2: 0}
    outs, carried = _call(
        body, grid=grid, in_specs=in_specs,
        out_specs=[pl.BlockSpec((bm, bn), lambda i, s, j, s_ref: (i, s_ref[s] * npb + j))],
        out_shape=[jax.ShapeDtypeStruct((m, N_CHIPS * ns), BF16)], scratch_shapes=[],
        semantics=("parallel", "arbitrary", "arbitrary"), name=name, args=args, carry=carry,
        prefetch=(slots,), aliases=aliases)
    return outs[0] if carry is None else (outs[0], carried)


def _row_tile(rows, cols, n_arrays):
    budget = VMEM_LIMIT_BYTES // 2
    cap = min(512, budget // (n_arrays * 2 * cols * 4))
    for t in range(cap - cap % SUBLANES, 0, -SUBLANES):
        if rows % t == 0:
            return t
    return rows


def _norm_mod(x, g, scale, shift, *, name):
    rows, d = x.shape
    tr = _row_tile(rows, d, 3)

    def body(x_ref, g_ref, sc_ref, sh_ref, o_ref):
        xv = x_ref[...]
        r = lax.rsqrt(jnp.mean(xv * xv, axis=-1, keepdims=True) + EPS)
        o_ref[...] = ((xv * r * g_ref[...]) * (1.0 + sc_ref[...]) + sh_ref[...]).astype(o_ref.dtype)

    row = pl.BlockSpec((tr, d), lambda i: (i, 0))
    vec = pl.BlockSpec((1, d), lambda i: (0, 0))
    return pl.pallas_call(
        body, grid=(rows // tr,), in_specs=[row, vec, vec, vec], out_specs=row,
        out_shape=jax.ShapeDtypeStruct((rows, d), BF16),
        compiler_params=_cparams(("parallel",)), name=name)(x, g, scale, shift)


CONV_CHUNK = 2 * SUBLANES


def _shifted_copies(buf, n):
    for r in range(1, SUBLANES):
        buf[r, pl.ds(0, n - SUBLANES), :] = buf[0, pl.ds(r, n - SUBLANES), :]


def _conv_fwd(proj, w_dw, b_dw, ln_g, ln_b, *, cw, carry=None):
    rows = proj.shape[0]
    tt = _pick(rows, 256, HALO)
    hb = tt // HALO

    def body(a_ref, g_ref, ha_ref, hg_ref, w_ref, b_ref, lg_ref, lb_ref, sl_ref, cv_ref, vs):
        i = pl.program_id(0)
        hv = ha_ref[...].astype(F32) * _sigmoid(hg_ref[...].astype(F32))
        vs[0, pl.ds(0, HALO), :] = jnp.where(i == 0, 0.0, hv)
        vs[0, pl.ds(HALO, tt), :] = a_ref[...].astype(F32) * _sigmoid(g_ref[...].astype(F32))
        _shifted_copies(vs, HALO + tt)

        def chunk(ci, carry):
            r0 = pl.multiple_of(ci * CONV_CHUNK, CONV_CHUNK)
            acc = jnp.broadcast_to(b_ref[...], (CONV_CHUNK, cw))
            for k in range(CONV_KERNEL):
                q, r = divmod(HALO - (CONV_KERNEL - 1) + k, SUBLANES)
                acc = acc + w_ref[pl.ds(k, 1), :] * vs[r, pl.ds(r0 + q * SUBLANES, CONV_CHUNK), :]
            cv_ref[pl.ds(r0, CONV_CHUNK), :] = acc
            return carry

        lax.fori_loop(0, tt // CONV_CHUNK, chunk, 0)
        acc = cv_ref[...]
        mu = jnp.mean(acc, axis=-1, keepdims=True)
        xc = acc - mu
        rstd = lax.rsqrt(jnp.mean(xc * xc, axis=-1, keepdims=True) + EPS)
        ln = xc * rstd * lg_ref[...] + lb_ref[...]
        sl_ref[...] = (ln * _sigmoid(ln)).astype(sl_ref.dtype)

    tile = lambda c: pl.BlockSpec((tt, cw), lambda i, c=c: (i, c))
    halo = lambda c: pl.BlockSpec((HALO, cw), lambda i, c=c: (jnp.maximum(i * hb - 1, 0), c))
    vec = pl.BlockSpec((1, cw), lambda i: (0, 0))
    outs, carried = _call(
        body, grid=(rows // tt,),
        in_specs=[tile(0), tile(1), halo(0), halo(1),
                  pl.BlockSpec((CONV_KERNEL, cw), lambda i: (0, 0)), vec, vec, vec],
        out_specs=[pl.BlockSpec((tt, cw), lambda i: (i, 0)), pl.BlockSpec((tt, cw), lambda i: (i, 0))],
        out_shape=[jax.ShapeDtypeStruct((rows, cw), BF16), jax.ShapeDtypeStruct((rows, cw), F32)],
        scratch_shapes=[pltpu.VMEM((SUBLANES, HALO + tt, cw), F32)],
        semantics=("parallel",), name="conv_fwd", args=[proj, proj, proj, proj, w_dw, b_dw, ln_g, ln_b],
        carry=carry)
    return outs if carry is None else (outs, carried)


def _ln_bwd(dsl, cv, ln_g, ln_b):
    rows, cw = cv.shape
    tr = _row_tile(rows, cw, 3)

    def body(d_ref, cv_ref, lg_ref, lb_ref, o_ref, dg_ref, db_ref):
        i = pl.program_id(0)

        @pl.when(i == 0)
        def _():
            dg_ref[...] = jnp.zeros_like(dg_ref)
            db_ref[...] = jnp.zeros_like(db_ref)

        x = cv_ref[...]
        mu = jnp.mean(x, axis=-1, keepdims=True)
        xc = x - mu
        rstd = lax.rsqrt(jnp.mean(xc * xc, axis=-1, keepdims=True) + EPS)
        xh = xc * rstd
        ln = xh * lg_ref[...] + lb_ref[...]
        s = _sigmoid(ln)
        dln = d_ref[...].astype(F32) * (s * (1.0 + ln * (1.0 - s)))
        dg_ref[...] += jnp.sum(dln * xh, axis=0, keepdims=True)
        db_ref[...] += jnp.sum(dln, axis=0, keepdims=True)
        dxh = dln * lg_ref[...]
        m1 = jnp.mean(dxh, axis=-1, keepdims=True)
        m2 = jnp.mean(dxh * xh, axis=-1, keepdims=True)
        o_ref[...] = rstd * (dxh - m1 - xh * m2)

    row = pl.BlockSpec((tr, cw), lambda i: (i, 0))
    vec = pl.BlockSpec((1, cw), lambda i: (0, 0))
    return pl.pallas_call(
        body, grid=(rows // tr,), in_specs=[row, row, vec, vec], out_specs=[row, vec, vec],
        out_shape=[jax.ShapeDtypeStruct((rows, cw), F32), jax.ShapeDtypeStruct((1, cw), F32),
                   jax.ShapeDtypeStruct((1, cw), F32)],
        compiler_params=_cparams(("arbitrary",)), name="ln_bwd")(dsl, cv, ln_g, ln_b)


def _conv_bwd(dcv, proj, w_dw, dproj, *, cw, carry=None):
    rows = proj.shape[0]
    tt = _pick(rows, 256, HALO)
    hb = tt // HALO
    nt = rows // tt
    taps = CONV_KERNEL

    def body(d_ref, dn_ref, a_ref, g_ref, ha_ref, hg_ref, w_ref, dproj_ref, o_ref, dw_ref, db_ref, vs, ds):
        i = pl.program_id(0)

        @pl.when(i == 0)
        def _():
            dw_ref[...] = jnp.zeros_like(dw_ref)
            db_ref[...] = jnp.zeros_like(db_ref)

        hv = ha_ref[...].astype(F32) * _sigmoid(hg_ref[...].astype(F32))
        vs[0, pl.ds(0, HALO), :] = jnp.where(i == 0, 0.0, hv)
        vs[0, pl.ds(HALO, tt), :] = a_ref[...].astype(F32) * _sigmoid(g_ref[...].astype(F32))
        _shifted_copies(vs, HALO + tt)
        ds[0, pl.ds(0, tt), :] = d_ref[...]
        ds[0, pl.ds(tt, HALO), :] = jnp.where(i == nt - 1, 0.0, dn_ref[...])
        _shifted_copies(ds, tt + HALO)
        db_ref[...] += jnp.sum(d_ref[...], axis=0, keepdims=True)
        for k in range(taps):
            q, r = divmod(HALO - (taps - 1) + k, SUBLANES)
            dw_ref[pl.ds(k, 1), :] += jnp.sum(d_ref[...] * vs[r, pl.ds(q * SUBLANES, tt), :], axis=0, keepdims=True)

        def chunk(ci, carry):
            r0 = pl.multiple_of(ci * CONV_CHUNK, CONV_CHUNK)
            dv = jnp.zeros((CONV_CHUNK, cw), F32)
            for k in range(taps):
                q, r = divmod(taps - 1 - k, SUBLANES)
                dv = dv + w_ref[pl.ds(k, 1), :] * ds[r, pl.ds(r0 + q * SUBLANES, CONV_CHUNK), :]
            av = a_ref[pl.ds(r0, CONV_CHUNK), :].astype(F32)
            sg = _sigmoid(g_ref[pl.ds(r0, CONV_CHUNK), :].astype(F32))
            o_ref[pl.ds(r0, CONV_CHUNK), pl.ds(0, cw)] = (dv * sg).astype(o_ref.dtype)
            o_ref[pl.ds(r0, CONV_CHUNK), pl.ds(cw, cw)] = (dv * av * sg * (1.0 - sg)).astype(o_ref.dtype)
            return carry

        lax.fori_loop(0, tt // CONV_CHUNK, chunk, 0)

    tile = lambda c: pl.BlockSpec((tt, cw), lambda i, c=c: (i, c))
    halo = lambda c: pl.BlockSpec((HALO, cw), lambda i, c=c: (jnp.maximum(i * hb - 1, 0), c))
    nxt = pl.BlockSpec((HALO, cw), lambda i: (jnp.minimum((i + 1) * hb, nt * hb - 1), 0))
    outs, carried = _call(
        body, grid=(nt,),
        in_specs=[pl.BlockSpec((tt, cw), lambda i: (i, 0)), nxt, tile(0), tile(1), halo(0), halo(1),
                  pl.BlockSpec((taps, cw), lambda i: (0, 0)), pl.BlockSpec(memory_space=pl.ANY)],
        out_specs=[pl.BlockSpec((tt, 2 * cw), lambda i: (i, 0)),
                   pl.BlockSpec((taps, cw), lambda i: (0, 0)), pl.BlockSpec((1, cw), lambda i: (0, 0))],
        out_shape=[jax.ShapeDtypeStruct(dproj.shape, dproj.dtype), jax.ShapeDtypeStruct((taps, cw), F32),
                   jax.ShapeDtypeStruct((1, cw), F32)],
        scratch_shapes=[pltpu.VMEM((SUBLANES, HALO + tt, cw), F32), pltpu.VMEM((SUBLANES, tt + HALO, cw), F32)],
        semantics=("arbitrary",), name="conv_bwd", args=[dcv, dcv, proj, proj, proj, proj, w_dw, dproj],
        carry=carry, aliases={7: 0})
    return outs if carry is None else (outs, carried)


def _merge_fwd(proj, y_conv, ya, yb, *, cw):
    rows = proj.shape[0]
    tr = _row_tile(rows, cw, 4)

    def body(gc_ref, gs_ref, yc_ref, ya_ref, yb_ref, o_ref):
        ys = ya_ref[...].astype(F32) * _sigmoid(yb_ref[...].astype(F32))
        o_ref[...] = (_sigmoid(gc_ref[...].astype(F32)) * yc_ref[...].astype(F32)
                      + _sigmoid(gs_ref[...].astype(F32)) * ys).astype(o_ref.dtype)

    blk = lambda off: pl.BlockSpec((tr, cw), lambda i, h, off=off: (i, off + h))
    return pl.pallas_call(
        body, grid=(rows // tr, 2), in_specs=[blk(3), blk(5), blk(0), blk(0), blk(0)], out_specs=blk(0),
        out_shape=jax.ShapeDtypeStruct((rows, 2 * cw), BF16),
        compiler_params=_cparams(("parallel", "parallel")), name="merge_fwd")(proj, proj, y_conv, ya, yb)


def _merge_bwd(dmerged, proj, y_conv, ya, yb, *, cw, carry=None):
    rows = proj.shape[0]
    tr = _row_tile(rows, cw, 3)

    def body(d_ref, g_ref, yc_ref, ya_ref, yb_ref, dg_ref, dyc_ref, dya_ref, dyb_ref):
        q = pl.program_id(1)
        d = d_ref[...].astype(F32)
        sg = _sigmoid(g_ref[...].astype(F32))

        @pl.when(q < 2)
        def _():
            dg_ref[...] = (d * yc_ref[...].astype(F32) * sg * (1.0 - sg)).astype(dg_ref.dtype)
            dyc_ref[...] = (d * sg).astype(dyc_ref.dtype)

        @pl.when(q >= 2)
        def _():
            sb = _sigmoid(yb_ref[...].astype(F32))
            yav = ya_ref[...].astype(F32)
            dg_ref[...] = (d * (yav * sb) * sg * (1.0 - sg)).astype(dg_ref.dtype)
            dys = d * sg
            dya_ref[...] = (dys * sb).astype(dya_ref.dtype)
            dyb_ref[...] = (dys * yav * sb * (1.0 - sb)).astype(dyb_ref.dtype)

    spec = lambda f: pl.BlockSpec((tr, cw), lambda i, q, f=f: (i, f(q)))
    conv_half = spec(lambda q: jnp.minimum(q, 1))
    ssm_half = spec(lambda q: jnp.maximum(q - 2, 0))
    o2 = jax.ShapeDtypeStruct((rows, 2 * cw), BF16)
    outs, carried = _call(
        body, grid=(rows // tr, 4),
        in_specs=[spec(lambda q: q % 2), spec(lambda q: 3 + q), conv_half, ssm_half, ssm_half],
        out_specs=[spec(lambda q: 3 + q), conv_half, ssm_half, ssm_half],
        out_shape=[jax.ShapeDtypeStruct((rows, 7 * cw), BF16), o2, o2, o2], scratch_shapes=[],
        semantics=("parallel", "arbitrary"), name="merge_bwd", args=[dmerged, proj, y_conv, ya, yb],
        carry=carry)
    return outs if carry is None else (outs, carried)


def _res_norm(x, mo, gate, g, scale, shift):
    rows, d = x.shape
    tr = _row_tile(rows, d, 4)

    def body(x_ref, mo_ref, gt_ref, g_ref, sc_ref, sh_ref, h_ref, z_ref):
        h = x_ref[...] + gt_ref[...] * mo_ref[...].astype(F32)
        h_ref[...] = h
        r = lax.rsqrt(jnp.mean(h * h, axis=-1, keepdims=True) + EPS)
        z_ref[...] = ((h * r * g_ref[...]) * (1.0 + sc_ref[...]) + sh_ref[...]).astype(z_ref.dtype)

    row = pl.BlockSpec((tr, d), lambda i: (i, 0))
    vec = pl.BlockSpec((1, d), lambda i: (0, 0))
    return pl.pallas_call(
        body, grid=(rows // tr,), in_specs=[row, row, vec, vec, vec, vec], out_specs=[row, row],
        out_shape=[jax.ShapeDtypeStruct((rows, d), F32), jax.ShapeDtypeStruct((rows, d), BF16)],
        compiler_params=_cparams(("parallel",)), name="res_norm")(x, mo, gate, g, scale, shift)


def _final_fwd_bwd(h1, ff, gate2, final_g, target):
    rows, d = h1.shape
    tr = _row_tile(rows, d, 5)

    def body(h_ref, ff_ref, gt_ref, fg_ref, t_ref, dh_ref, dff_ref, loss_ref, dfg_ref, dgt_ref):
        i = pl.program_id(0)

        @pl.when(i == 0)
        def _():
            loss_ref[...] = jnp.zeros_like(loss_ref)
            dfg_ref[...] = jnp.zeros_like(dfg_ref)
            dgt_ref[...] = jnp.zeros_like(dgt_ref)

        ffv = ff_ref[...].astype(F32)
        h2 = h_ref[...] + gt_ref[...] * ffv
        r = lax.rsqrt(jnp.mean(h2 * h2, axis=-1, keepdims=True) + EPS)
        y = h2 * r
        e = y * fg_ref[...] - t_ref[...]
        loss_ref[...] += 0.5 * jnp.sum(jnp.mean(e * e, axis=-1, keepdims=True))
        dout = e * (1.0 / d)
        dfg_ref[...] += jnp.sum(dout * y, axis=0, keepdims=True)
        dy = dout * fg_ref[...]
        dh2 = r * (dy - y * jnp.mean(dy * y, axis=-1, keepdims=True))
        dh_ref[...] = dh2
        dgt_ref[...] += jnp.sum(dh2 * ffv, axis=0, keepdims=True)
        dff_ref[...] = (dh2 * gt_ref[...]).astype(dff_ref.dtype)

    row = pl.BlockSpec((tr, d), lambda i: (i, 0))
    vec = pl.BlockSpec((1, d), lambda i: (0, 0))
    return pl.pallas_call(
        body, grid=(rows // tr,), in_specs=[row, row, vec, vec, row],
        out_specs=[row, row, pl.BlockSpec((1, LANES), lambda i: (0, 0)), vec, vec],
        out_shape=[jax.ShapeDtypeStruct((rows, d), F32), jax.ShapeDtypeStruct((rows, d), BF16),
                   jax.ShapeDtypeStruct((1, LANES), F32), jax.ShapeDtypeStruct((1, d), F32),
                   jax.ShapeDtypeStruct((1, d), F32)],
        compiler_params=_cparams(("arbitrary",)), name="final_fwd_bwd")(h1, ff, gate2, final_g, target)


def _norm_mod_bwd(dz, hin, dres, g, scale, gate, mo, *, name, carry=None):
    rows, d = hin.shape
    with_gate = gate is not None
    tr = _row_tile(rows, d, 6)

    def body(*refs):
        if with_gate:
            (dz_ref, h_ref, dr_ref, g_ref, sc_ref, gt_ref, mo_ref,
             dh_ref, dsh_ref, dsc_ref, dg_ref, dmo_ref, dgt_ref) = refs
        else:
            dz_ref, h_ref, dr_ref, g_ref, sc_ref, dh_ref, dsh_ref, dsc_ref, dg_ref = refs
        i = pl.program_id(0)

        @pl.when(i == 0)
        def _():
            dsh_ref[...] = jnp.zeros_like(dsh_ref)
            dsc_ref[...] = jnp.zeros_like(dsc_ref)
            dg_ref[...] = jnp.zeros_like(dg_ref)
            if with_gate:
                dgt_ref[...] = jnp.zeros_like(dgt_ref)

        dzv = dz_ref[...].astype(F32)
        h = h_ref[...]
        r = lax.rsqrt(jnp.mean(h * h, axis=-1, keepdims=True) + EPS)
        y = h * r
        dsh_ref[...] += jnp.sum(dzv, axis=0, keepdims=True)
        dsc_ref[...] += jnp.sum(dzv * (y * g_ref[...]), axis=0, keepdims=True)
        dn = dzv * (1.0 + sc_ref[...])
        dg_ref[...] += jnp.sum(dn * y, axis=0, keepdims=True)
        dy = dn * g_ref[...]
        dh = dr_ref[...] + r * (dy - y * jnp.mean(dy * y, axis=-1, keepdims=True))
        dh_ref[...] = dh
        if with_gate:
            dmo_ref[...] = (dh * gt_ref[...]).astype(dmo_ref.dtype)
            dgt_ref[...] += jnp.sum(dh * mo_ref[...].astype(F32), axis=0, keepdims=True)

    row = pl.BlockSpec((tr, d), lambda i: (i, 0))
    vec = pl.BlockSpec((1, d), lambda i: (0, 0))
    vshape = jax.ShapeDtypeStruct((1, d), F32)
    in_specs = [row, row, row, vec, vec]
    args = [dz, hin, dres, g, scale]
    out_specs = [row, vec, vec, vec]
    out_shape = [jax.ShapeDtypeStruct((rows, d), F32), vshape, vshape, vshape]
    if with_gate:
        in_specs += [vec, row]
        args += [gate, mo]
        out_specs += [row, vec]
        out_shape += [jax.ShapeDtypeStruct((rows, d), BF16), vshape]
    outs, carried = _call(
        body, grid=(rows // tr,), in_specs=in_specs, out_specs=out_specs, out_shape=out_shape,
        scratch_shapes=[], semantics=("arbitrary",), name=name, args=args, carry=carry)
    return outs if carry is None else (outs, carried)


def _s5_discretise(a_re, a_im, log_dt, b_re, b_im):
    dt = jnp.exp(log_dt)[:, None]
    er = jnp.exp(a_re * dt)
    lr = er * jnp.cos(a_im * dt)
    li = er * jnp.sin(a_im * dt)
    den = a_re * a_re + a_im * a_im
    cr = ((lr - 1.0) * a_re + li * a_im) / den
    ci = (li * a_re - (lr - 1.0) * a_im) / den
    bbr = cr[..., None] * b_re - ci[..., None] * b_im
    bbi = cr[..., None] * b_im + ci[..., None] * b_re
    return lr, li, bbr, bbi


def _block_diag(w):
    g, r, c = w.shape
    nb = g // GROUPS_PER_BLOCK
    eye = jnp.eye(GROUPS_PER_BLOCK, dtype=w.dtype)
    w5 = w.reshape(nb, GROUPS_PER_BLOCK, r, 1, c) * eye[None, :, None, :, None]
    return w5.reshape(nb, GROUPS_PER_BLOCK * r, GROUPS_PER_BLOCK * c)


def _block_diag_extract(m, r, c):
    nb = m.shape[0]
    m5 = m.reshape(nb, GROUPS_PER_BLOCK, r, GROUPS_PER_BLOCK, c)
    idx = jnp.arange(GROUPS_PER_BLOCK)
    d = m5[:, idx, :, idx, :]
    return jnp.moveaxis(d, 0, 1).reshape(nb * GROUPS_PER_BLOCK, r, c)


def _scan_multipliers(lr, li):
    power = jnp.arange(1, SUBLANES + 1, dtype=F32)[None, :, None]
    er = jnp.exp(power * lr)
    pr = er * jnp.cos(power * li)
    pi = er * jnp.sin(power * li)
    rows = jnp.arange(SUBLANES)[None, :, None]
    fr, fi, rr, ri = [], [], [], []
    for s in (1, 2, 4):
        mf = (rows >= s).astype(F32)
        mr = (rows <= SUBLANES - 1 - s).astype(F32)
        fr.append(mf * pr[:, s - 1:s, :])
        fi.append(mf * pi[:, s - 1:s, :])
        rr.append(mr * pr[:, s - 1:s, :])
        ri.append(mr * pi[:, s - 1:s, :])
    fr.append(pr)
    fi.append(pi)
    rr.append(pr[:, ::-1, :])
    ri.append(pi[:, ::-1, :])
    st = lambda xs: jnp.stack(xs, axis=1)
    return st(fr), st(fi), st(rr), st(ri)


def _scan_rows(sre, sim, mul_r, mul_i, n_groups, reverse):
    sgn = -1.0 if reverse else 1.0
    lanes = sre.shape[1]

    def step(k, carry):
        cr, ci = carry
        kk = (n_groups - 1 - k) if reverse else k
        r0 = pl.multiple_of(kk * SUBLANES, SUBLANES)
        xr = sre[pl.ds(r0, SUBLANES), :]
        xi = sim[pl.ds(r0, SUBLANES), :]
        for lvl, s in enumerate((1, 2, 4)):
            sh = (SUBLANES - s) if reverse else s
            nr = pltpu.roll(xr, sh, 0)
            ni = pltpu.roll(xi, sh, 0)
            mr = mul_r[lvl]
            mi = mul_i[lvl] * sgn
            xr, xi = xr + mr * nr - mi * ni, xi + mr * ni + mi * nr
        mr = mul_r[3]
        mi = mul_i[3] * sgn
        xr, xi = xr + mr * cr - mi * ci, xi + mr * ci + mi * cr
        sre[pl.ds(r0, SUBLANES), :] = xr
        sim[pl.ds(r0, SUBLANES), :] = xi
        edge = 0 if reverse else SUBLANES - 1
        ncr = jnp.broadcast_to(xr[edge:edge + 1, :], (SUBLANES, lanes))
        nci = jnp.broadcast_to(xi[edge:edge + 1, :], (SUBLANES, lanes))
        return ncr, nci

    zero = jnp.zeros((SUBLANES, lanes), F32)
    lax.fori_loop(0, n_groups, step, (zero, zero))


def _dot(a, b):
    return jnp.dot(a, b, preferred_element_type=F32)


def _dotf(a, b):
    return _dot(a.astype(BF16), b)


def _s5_operands(lr, li, bbr, bbi, c_re, c_im):
    g = lr.shape[0]
    nb = g // GROUPS_PER_BLOCK
    tb = lambda w: jnp.swapaxes(w, 1, 2)
    b_in = [_block_diag(tb(bbr)), _block_diag(tb(bbi))]
    c_out = [_block_diag(tb(c_re)), _block_diag(tb(c_im))]
    b_out = [_block_diag(bbr), _block_diag(bbi)]
    c_in = [_block_diag(c_re), _block_diag(c_im)]
    lam_r = lr.reshape(nb, 1, STATE_LANES)
    lam_i = li.reshape(nb, 1, STATE_LANES)
    mults = _scan_multipliers(lam_r, lam_i)
    cast = lambda ws: [w.astype(BF16) for w in ws]
    return cast(b_in), cast(c_out), cast(b_out), cast(c_in), mults


def _s5_fwd(proj, d_skip, b_in, c_out, mults, *, col0, carry=None):
    rows = proj.shape[0]
    nb = b_in[0].shape[0]
    tm = _pick(rows, 512, SUBLANES)
    n_tiles = rows // tm
    s_l = STATE_LANES

    def body(u_ref, dk_ref, br, bi, cr, ci, fr_ref, fi_ref, o_ref, sr_ref, si_ref, sre, sim):
        for t in range(n_tiles):
            rs = pl.ds(t * tm, tm)
            ub = u_ref[rs, :]
            sre[rs, :] = _dot(ub, br[...])
            sim[rs, :] = _dot(ub, bi[...])
        _scan_rows(sre, sim, fr_ref, fi_ref, rows // SUBLANES, False)
        for t in range(n_tiles):
            rs = pl.ds(t * tm, tm)
            srb = sre[rs, :].astype(BF16)
            sib = sim[rs, :].astype(BF16)
            sr_ref[rs, :] = srb
            si_ref[rs, :] = sib
            y0 = _dot(srb, cr[...]) - _dot(sib, ci[...])
            y1 = y0 + dk_ref[...] * u_ref[rs, :].astype(F32)
            o_ref[rs, :] = _gelu(y1).astype(o_ref.dtype)

    mat_in = pl.BlockSpec((None, LANES, s_l), lambda g: (g, 0, 0))
    mat_out = pl.BlockSpec((None, s_l, LANES), lambda g: (g, 0, 0))
    mul = pl.BlockSpec((None, 4, SUBLANES, s_l), lambda g: (g, 0, 0, 0))
    state = pl.BlockSpec((rows, s_l), lambda g: (0, g))
    outs, carried = _call(
        body, grid=(nb,),
        in_specs=[pl.BlockSpec((rows, LANES), lambda g: (0, col0 + g)), pl.BlockSpec((1, LANES), lambda g: (0, g))]
        + [mat_in] * 2 + [mat_out] * 2 + [mul] * 2,
        out_specs=[pl.BlockSpec((rows, LANES), lambda g: (0, g)), state, state],
        out_shape=[jax.ShapeDtypeStruct((rows, nb * LANES), BF16), jax.ShapeDtypeStruct((rows, nb * s_l), BF16),
                   jax.ShapeDtypeStruct((rows, nb * s_l), BF16)],
        scratch_shapes=[pltpu.VMEM((rows, s_l), F32), pltpu.VMEM((rows, s_l), F32)],
        semantics=("parallel",), name="s5_fwd", args=[proj, d_skip, *b_in, *c_out, mults[0], mults[1]], carry=carry)
    return outs if carry is None else (outs, carried)


def _s5_bwd(proj, dyg, d_skip, states, c_out, b_out, c_in, mults, dproj, *, col0, carry=None):
    rows = proj.shape[0]
    nb = c_out[0].shape[0]
    tm = _pick(rows, 512, SUBLANES)
    n_tiles = rows // tm
    s_l = STATE_LANES
    n_groups = rows // SUBLANES
    tn = (((0,), (0,)), ((), ()))

    def body(u_ref, dy_ref, dk_ref, sr_ref, si_ref, cr, ci, bor, boi, cir, cii, rr_ref, ri_ref, dproj_ref,
             du_ref, ddk_ref, dbr_ref, dbi_ref, dcr_ref, dci_ref, dlr_ref, dli_ref,
             gre, gim, dy1):
        ddk = jnp.zeros((1, LANES), F32)
        dcr = jnp.zeros((s_l, LANES), F32)
        dci = jnp.zeros((s_l, LANES), F32)
        for t in range(n_tiles):
            rs = pl.ds(t * tm, tm)
            srb = sr_ref[rs, :]
            sib = si_ref[rs, :]
            uf = u_ref[rs, :].astype(F32)
            y0 = _dot(srb, cr[...]) - _dot(sib, ci[...])
            y1 = y0 + dk_ref[...] * uf
            d1 = dy_ref[rs, :].astype(F32) * _gelu_grad(y1)
            dy1[rs, :] = d1
            ddk = ddk + jnp.sum(d1 * uf, axis=0, keepdims=True)
            d1b = d1.astype(BF16)
            dcr = dcr + lax.dot_general(srb, d1b, tn, preferred_element_type=F32)
            dci = dci - lax.dot_general(sib, d1b, tn, preferred_element_type=F32)
            gre[rs, :] = _dot(d1b, cir[...])
            gim[rs, :] = -_dot(d1b, cii[...])
        ddk_ref[...] = ddk
        dcr_ref[...] = dcr
        dci_ref[...] = dci

        last_row = lax.broadcasted_iota(jnp.int32, (SUBLANES, s_l), 0) == SUBLANES - 1

        def group(r0, s_r, s_i, carry):
            cr_, ci_, ar, ai = carry
            xr = gre[pl.ds(r0, SUBLANES), :]
            xi = gim[pl.ds(r0, SUBLANES), :]
            for lvl, s in enumerate((1, 2, 4)):
                nr = pltpu.roll(xr, SUBLANES - s, 0)
                ni = pltpu.roll(xi, SUBLANES - s, 0)
                mr = rr_ref[lvl]
                mi = ri_ref[lvl]
                xr, xi = xr + mr * nr + mi * ni, xi + mr * ni - mi * nr
            mr = rr_ref[3]
            mi = ri_ref[3]
            xr, xi = xr + mr * cr_ + mi * ci_, xi + mr * ci_ - mi * cr_
            gre[pl.ds(r0, SUBLANES), :] = xr
            gim[pl.ds(r0, SUBLANES), :] = xi
            nxt_r = jnp.where(last_row, cr_, pltpu.roll(xr, SUBLANES - 1, 0))
            nxt_i = jnp.where(last_row, ci_, pltpu.roll(xi, SUBLANES - 1, 0))
            ncr = jnp.broadcast_to(xr[0:1, :], (SUBLANES, s_l))
            nci = jnp.broadcast_to(xi[0:1, :], (SUBLANES, s_l))
            return ncr, nci, ar + nxt_r * s_r + nxt_i * s_i, ai + nxt_i * s_r - nxt_r * s_i

        def rev_step(k, carry):
            r0 = pl.multiple_of((n_groups // 2 - 1 - k) * 2 * SUBLANES, 2 * SUBLANES)
            s_r = sr_ref[pl.ds(r0, 2 * SUBLANES), :].astype(F32)
            s_i = si_ref[pl.ds(r0, 2 * SUBLANES), :].astype(F32)
            carry = group(r0 + SUBLANES, s_r[SUBLANES:], s_i[SUBLANES:], carry)
            return group(r0, s_r[:SUBLANES], s_i[:SUBLANES], carry)

        zero = jnp.zeros((SUBLANES, s_l), F32)
        _, _, ar, ai = lax.fori_loop(0, n_groups // 2, rev_step, (zero, zero, zero, zero))
        dlr_ref[...] = jnp.sum(ar, axis=0, keepdims=True)
        dli_ref[...] = jnp.sum(ai, axis=0, keepdims=True)

        dbr = jnp.zeros((LANES, s_l), F32)
        dbi = jnp.zeros((LANES, s_l), F32)
        for t in range(n_tiles):
            rs = pl.ds(t * tm, tm)
            gr = gre[rs, :]
            gi = gim[rs, :]
            grb = gr.astype(BF16)
            gib = gi.astype(BF16)
            du = _dot(grb, bor[...]) + _dot(gib, boi[...]) + dy1[rs, :] * dk_ref[...]
            du_ref[rs, :] = du.astype(du_ref.dtype)
            ub = u_ref[rs, :]
            dbr = dbr + lax.dot_general(ub, grb, tn, preferred_element_type=F32)
            dbi = dbi + lax.dot_general(ub, gib, tn, preferred_element_type=F32)
        dbr_ref[...] = dbr
        dbi_ref[...] = dbi

    mat_in = pl.BlockSpec((None, LANES, s_l), lambda g: (g, 0, 0))
    mat_out = pl.BlockSpec((None, s_l, LANES), lambda g: (g, 0, 0))
    mul = pl.BlockSpec((None, 4, SUBLANES, s_l), lambda g: (g, 0, 0, 0))
    lam = pl.BlockSpec((None, 1, s_l), lambda g: (g, 0, 0))
    col = pl.BlockSpec((rows, LANES), lambda g: (0, g))
    vec = pl.BlockSpec((1, LANES), lambda g: (0, g))
    state = pl.BlockSpec((rows, s_l), lambda g: (0, g))
    outs, carried = _call(
        body, grid=(nb,),
        in_specs=[pl.BlockSpec((rows, LANES), lambda g: (0, col0 + g)), col, vec]
        + [state] * 2 + [mat_out] * 2 + [mat_out] * 2 + [mat_in] * 2 + [mul] * 2
        + [pl.BlockSpec(memory_space=pl.ANY)],
        out_specs=[pl.BlockSpec((rows, LANES), lambda g: (0, col0 + g)), vec, mat_in, mat_in, mat_out, mat_out,
                   lam, lam],
        out_shape=[jax.ShapeDtypeStruct(dproj.shape, dproj.dtype), jax.ShapeDtypeStruct((1, nb * LANES), F32),
                   jax.ShapeDtypeStruct((nb, LANES, s_l), F32), jax.ShapeDtypeStruct((nb, LANES, s_l), F32),
                   jax.ShapeDtypeStruct((nb, s_l, LANES), F32), jax.ShapeDtypeStruct((nb, s_l, LANES), F32),
                   jax.ShapeDtypeStruct((nb, 1, s_l), F32), jax.ShapeDtypeStruct((nb, 1, s_l), F32)],
        scratch_shapes=[pltpu.VMEM((rows, s_l), F32)] * 2 + [pltpu.VMEM((rows, LANES), F32)],
        semantics=("parallel",), name="s5_bwd",
        args=[proj, dyg, d_skip, *states, *c_out, *b_out, *c_in, mults[2], mults[3], dproj], carry=carry,
        aliases={13: 0})
    return outs if carry is None else (outs, carried)


def _silu(v):
    return v * _sigmoid(v)


def _ada_fwd(c_all, w_shard, b_cols):
    d, n = w_shard.shape
    bn = _pick(n, 512)

    def body(c_ref, w_ref, b_ref, o_ref):
        ca = _silu(c_ref[...]).astype(BF16)
        o_ref[...] = _dot(ca, w_ref[...].astype(BF16)) + b_ref[...]

    return pl.pallas_call(
        body, grid=(n // bn,),
        in_specs=[pl.BlockSpec((N_DEV, d), lambda j: (0, 0)), pl.BlockSpec((d, bn), lambda j: (0, j)),
                  pl.BlockSpec((1, bn), lambda j: (0, j))],
        out_specs=pl.BlockSpec((N_DEV, bn), lambda j: (0, j)),
        out_shape=jax.ShapeDtypeStruct((N_DEV, n), F32),
        compiler_params=_cparams(("parallel",)), name="ada_fwd")(c_all, w_shard, b_cols)


def _ada_bwd(c_all, dmod_cols):
    d = c_all.shape[1]
    n = dmod_cols.shape[1]
    bn = _pick(n, 512)

    def body(c_ref, g_ref, o_ref):
        ca = _silu(c_ref[...]).astype(BF16)
        o_ref[...] = lax.dot_general(ca, g_ref[...].astype(BF16), (((0,), (0,)), ((), ())),
                                     preferred_element_type=F32)

    return pl.pallas_call(
        body, grid=(n // bn,),
        in_specs=[pl.BlockSpec((N_DEV, d), lambda j: (0, 0)), pl.BlockSpec((N_DEV, bn), lambda j: (0, j))],
        out_specs=pl.BlockSpec((d, bn), lambda j: (0, j)),
        out_shape=jax.ShapeDtypeStruct((d, n), F32),
        compiler_params=_cparams(("parallel",)), name="ada_bwd")(c_all, dmod_cols)


def _cast_bf16(w, *, name):
    rows, cols = w.shape
    tr = _row_tile(rows, cols, 3)

    def body(w_ref, o_ref, slot_ref):
        o_ref[...] = w_ref[...].astype(BF16)
        slot_ref[...] = w_ref[...].astype(BF16)

    row = pl.BlockSpec((tr, cols), lambda i: (i, 0))
    return pl.pallas_call(
        body, grid=(rows // tr,), in_specs=[row],
        out_specs=[row, pl.BlockSpec((None, tr, cols), lambda i: (0, i, 0))],
        out_shape=[jax.ShapeDtypeStruct((rows, cols), BF16), jax.ShapeDtypeStruct((N_CHIPS, rows, cols), BF16)],
        compiler_params=_cparams(("parallel",)), name=name)(w)


def _adamw_update(w_ref, g_ref, m_ref, v_ref, d_ref, nm_ref, nv_ref):
    c1 = 1.0 / (1.0 - ADAM_B1 ** ADAM_STEP)
    c2 = 1.0 / (1.0 - ADAM_B2 ** ADAM_STEP)
    gv = g_ref[...]
    nm = ADAM_B1 * m_ref[...] + (1.0 - ADAM_B1) * gv
    nv = ADAM_B2 * v_ref[...] + (1.0 - ADAM_B2) * (gv * gv)
    nm_ref[...] = nm
    nv_ref[...] = nv
    d_ref[...] = -ADAM_LR * ((nm * c1) / (jnp.sqrt(nv * c2) + ADAM_EPS) + ADAM_WD * w_ref[...])


def _adamw_many(ws, gs, ms, vs, *, name):
    n = len(ws)

    def body(*refs):
        for i in range(n):
            _adamw_update(*[refs[k * n + i] for k in range(7)])

    vm = pl.BlockSpec(memory_space=pltpu.VMEM)
    shapes = [jax.ShapeDtypeStruct(a.shape, F32) for a in ws]
    outs = pl.pallas_call(
        body, in_specs=[vm] * (4 * n), out_specs=[vm] * (3 * n), out_shape=shapes * 3,
        compiler_params=pltpu.CompilerParams(vmem_limit_bytes=VMEM_LIMIT_BYTES), name=name)(*ws, *gs, *ms, *vs)
    return list(outs[:n]), list(outs[n:2 * n]), list(outs[2 * n:])


def _adamw(w, g, m, v, *, name, after=None, with_grad=False):
    rows, cols = w.shape
    tr = _row_tile(rows, cols, 8)
    n_out = 4 if with_grad else 3

    def body(w_ref, g_ref, m_ref, v_ref, *rest):
        outs = rest[-n_out:]
        _adamw_update(w_ref, g_ref, m_ref, v_ref, *outs[:3])
        if with_grad:
            outs[3][...] = g_ref[...]

    row = pl.BlockSpec((tr, cols), lambda i: (i, 0))
    shp = jax.ShapeDtypeStruct((rows, cols), F32)
    extra = [] if after is None else [after]
    outs, _ = _call(
        body, grid=(rows // tr,), in_specs=[row] * 4 + [pl.BlockSpec(memory_space=pl.ANY)] * len(extra),
        out_specs=[row] * n_out, out_shape=[shp] * n_out, scratch_shapes=[], semantics=("parallel",), name=name,
        args=[w, g, m, v] + extra)
    return outs


def _sum_leading(a, *, name, out_dtype=F32):
    n, rows, cols = a.shape
    tr = _row_tile(rows, cols, n + 1)

    def body(a_ref, o_ref):
        acc = a_ref[0].astype(F32)
        for i in range(1, n):
            acc = acc + a_ref[i].astype(F32)
        o_ref[...] = acc.astype(o_ref.dtype)

    return pl.pallas_call(
        body, grid=(rows // tr,), in_specs=[pl.BlockSpec((n, tr, cols), lambda i: (0, i, 0))],
        out_specs=pl.BlockSpec((tr, cols), lambda i: (i, 0)),
        out_shape=jax.ShapeDtypeStruct((rows, cols), out_dtype),
        compiler_params=_cparams(("parallel",)), name=name)(a)


def _add_half(dw, land, my_c, *, name):
    n, r, cols = dw.shape
    h = r // 2
    tr = _row_tile(h, cols, 3)
    hb = h // tr

    def body(c_ref, a_ref, b_ref, o_ref):
        o_ref[...] = (a_ref[...].astype(F32) + b_ref[...].astype(F32)).astype(o_ref.dtype)

    gs = pltpu.PrefetchScalarGridSpec(
        num_scalar_prefetch=1, grid=(n, hb),
        in_specs=[pl.BlockSpec((None, tr, cols), lambda s, i, c_ref: (s, c_ref[0] * hb + i, 0)),
                  pl.BlockSpec((None, tr, cols), lambda s, i, c_ref: (s, i, 0))],
        out_specs=pl.BlockSpec((None, tr, cols), lambda s, i, c_ref: (s, i, 0)))
    return pl.pallas_call(
        body, grid_spec=gs, out_shape=jax.ShapeDtypeStruct((n, h, cols), BF16),
        compiler_params=_cparams(("parallel", "parallel")), name=name)(my_c, dw, land)


def _mesh_pos():
    return lax.axis_index("x"), lax.axis_index("y"), lax.axis_index("c")


def _other_chips(x, y):
    return [(1 - x, y), (x, 1 - y), (1 - x, 1 - y)]


def _gather_small(blk, *, name):
    m_per, n = blk.shape

    def body(x_ref, out_ref, send_sems, recv_sems, local_sem):
        x, y, c = _mesh_pos()
        me, sibling = (x, y, c), (x, y, 1 - c)
        chips = _other_chips(x, y)

        def rows(px, py, pc):
            return out_ref.at[pl.ds((4 * px + 2 * py + pc) * m_per, m_per), :]

        def copy(k, block, to, src=None):
            return pltpu.make_async_remote_copy(
                src_ref=rows(*block) if src is None else src, dst_ref=rows(*block),
                send_sem=send_sems.at[k], recv_sem=recv_sems.at[k], device_id=to, device_id_type=MESH)

        mine = pltpu.make_async_copy(x_ref, rows(*me), local_sem)
        mine.start()
        first = [copy(0, me, sibling, src=x_ref)]
        first += [copy(1 + j, me, (*chip, c), src=x_ref) for j, chip in enumerate(chips)]
        for cp in first:
            cp.start()
        passed = [copy(4 + j, (*chip, c), sibling) for j, chip in enumerate(chips)]
        for j, chip in enumerate(chips):
            copy(1 + j, (*chip, c), me).wait_recv()
            passed[j].start()
        copy(0, sibling, me).wait_recv()
        for j, chip in enumerate(chips):
            copy(4 + j, (*chip, 1 - c), me).wait_recv()
        for cp in first + passed:
            cp.wait_send()
        mine.wait()

    return pl.pallas_call(
        body, out_shape=jax.ShapeDtypeStruct((N_DEV * m_per, n), blk.dtype),
        in_specs=[pl.BlockSpec(memory_space=pltpu.VMEM)], out_specs=pl.BlockSpec(memory_space=pltpu.VMEM),
        scratch_shapes=[pltpu.SemaphoreType.DMA((7,)), pltpu.SemaphoreType.DMA((7,)), pltpu.SemaphoreType.DMA],
        compiler_params=pltpu.CompilerParams(vmem_limit_bytes=VMEM_LIMIT_BYTES), name=name)(blk)


def _cast_into_slot(w, chip, after, *, name):
    rows, cols = w.shape
    tr = _row_tile(rows, cols, 2)

    def body(chip_ref, w_ref, after_ref, o_ref):
        o_ref[...] = w_ref[...].astype(BF16)

    gs = pltpu.PrefetchScalarGridSpec(
        num_scalar_prefetch=1, grid=(rows // tr,),
        in_specs=[pl.BlockSpec((tr, cols), lambda i, chip_ref: (i, 0)), pl.BlockSpec(memory_space=pl.ANY)],
        out_specs=pl.BlockSpec((None, tr, cols), lambda i, chip_ref: (chip_ref[0], i, 0)))
    return pl.pallas_call(
        body, grid_spec=gs, out_shape=jax.ShapeDtypeStruct((N_CHIPS, rows, cols), BF16),
        compiler_params=_cparams(("parallel",)), name=name)(chip, w, after)


def _sum_into_half(part, landed, chip, my_c, *, name):
    _, h, cols = part.shape
    tr = _row_tile(h, cols, 5)
    hb = h // tr

    def body(chip_ref, c_ref, p_ref, l_ref, o_ref):
        acc = p_ref[...].astype(F32)
        for j in range(3):
            acc = acc + l_ref[j].astype(F32)
        o_ref[...] = acc

    gs = pltpu.PrefetchScalarGridSpec(
        num_scalar_prefetch=2, grid=(hb,),
        in_specs=[pl.BlockSpec((None, tr, cols), lambda i, chip_ref, c_ref: (chip_ref[0], i, 0)),
                  pl.BlockSpec((3, tr, cols), lambda i, chip_ref, c_ref: (0, i, 0))],
        out_specs=pl.BlockSpec((tr, cols), lambda i, chip_ref, c_ref: (c_ref[0] * hb + i, 0)))
    return pl.pallas_call(
        body, grid_spec=gs, out_shape=jax.ShapeDtypeStruct((2 * h, cols), F32),
        compiler_params=_cparams(("parallel",)), name=name)(chip, my_c, part, landed)


class _NoComm:
    def __init__(self, big):
        self.big = big
        self.grads = {}

    def weight(self, name):
        return self.big[name]

    def mm_in(self, u, afters):
        return _mm(u, self.big["w_in"], mode="nn", out_dtype=BF16, name="mm_in")

    def mm_d_in(self, dproj):
        return _mm(dproj, self.big["w_in"], mode="nt", out_dtype=F32, name="mm_d_in")

    def carry(self, site, args=()):
        return None

    def done(self, site, carried, out=None):
        return out

    def grad(self, name, dw):
        self.grads[name] = dw

    def early_grads(self, early):
        self.early = early


def _gather_rows_carry(blk):
    m_per = blk.shape[0]
    sem = pltpu.SemaphoreType.DMA((7,))

    def copies(ins, outs, sems):
        send_sems, recv_sems, local_sem = sems
        x, y, c = _mesh_pos()
        me, sibling = (x, y, c), (x, y, 1 - c)
        chips = _other_chips(x, y)

        def rows(px, py, pc):
            return outs[0].at[pl.ds((4 * px + 2 * py + pc) * m_per, m_per), :]

        def copy(k, block, to, src=None):
            return pltpu.make_async_remote_copy(
                src_ref=rows(*block) if src is None else src, dst_ref=rows(*block),
                send_sem=send_sems.at[k], recv_sem=recv_sems.at[k], device_id=to, device_id_type=MESH)

        mine = pltpu.make_async_copy(ins[0], rows(*me), local_sem.at[0])
        first = [copy(0, me, sibling, src=ins[0])]
        first += [copy(1 + j, me, (*chip, c), src=ins[0]) for j, chip in enumerate(chips)]
        passed = [copy(4 + j, (*chip, c), sibling) for j, chip in enumerate(chips)]
        landed = [copy(1 + j, (*chip, c), me) for j, chip in enumerate(chips)]
        from_sibling = [copy(0, sibling, me)] + [copy(4 + j, (*chip, 1 - c), me) for j, chip in enumerate(chips)]
        return mine, first, passed, landed, from_sibling

    def start(ins, outs, sems):
        mine, first, _, _, _ = copies(ins, outs, sems)
        mine.start()
        for cp in first:
            cp.start()

    def finish(ins, outs, sems):
        mine, first, passed, landed, from_sibling = copies(ins, outs, sems)
        for arrived, onward in zip(landed, passed):
            arrived.wait_recv()
            onward.start()
        for arrived in from_sibling:
            arrived.wait_recv()
        for cp in first + passed:
            cp.wait_send()
        mine.wait()

    shape = jax.ShapeDtypeStruct((N_DEV * m_per, blk.shape[1]), blk.dtype)
    return _Carry([blk], [shape], {}, [sem, sem, pltpu.SemaphoreType.DMA((1,))], start, finish)


def _w_in_copies(own_ref, land_ref, send_sems, recv_sems):
    x, y, c = _mesh_pos()
    h = own_ref.shape[0] // 2
    return [pltpu.make_async_remote_copy(
        src_ref=own_ref.at[pl.ds(c * h, h), :], dst_ref=land_ref.at[1 + j, pl.ds(c * h, h), :],
        send_sem=send_sems[j], recv_sem=recv_sems[j], device_id=(*chip, c), device_id_type=MESH)
        for j, chip in enumerate(_other_chips(x, y))]


def _w_in_send(own, land, after):
    hbm = pl.BlockSpec(memory_space=pltpu.HBM)
    sem = pl.BlockSpec(memory_space=pltpu.SEMAPHORE)
    land_shape = land.shape

    def body(own_ref, land_ref, after_ref, s0, s1, s2, r0, r1, r2, own_thru, land_thru, token):
        for cp in _w_in_copies(own_ref, land_ref, (s0, s1, s2), (r0, r1, r2)):
            cp.start()
        token[...] = jnp.zeros_like(token)

    outs = pl.pallas_call(
        body, name="w_in_send",
        out_shape=(pltpu.SemaphoreType.DMA(()),) * 6 + (
            pltpu.HBM(own.shape, own.dtype), pltpu.HBM(land_shape, own.dtype), jax.ShapeDtypeStruct((8, LANES), F32)),
        in_specs=(hbm, hbm, pl.BlockSpec(memory_space=pl.ANY)),
        out_specs=(sem,) * 6 + (hbm, hbm, pl.BlockSpec(memory_space=pltpu.VMEM)),
        input_output_aliases={0: 6, 1: 7},
        compiler_params=pltpu.CompilerParams(has_side_effects=pltpu.SideEffectType.DATAFLOW_SIDE_EFFECTING),
    )(pltpu.with_memory_space_constraint(own, pltpu.HBM), pltpu.with_memory_space_constraint(land, pltpu.HBM), after)
    return outs[:6], outs[6], outs[7], outs[8]


def _w_in_wait(sems, own, land, afters):
    hbm = pl.BlockSpec(memory_space=pltpu.HBM)
    sem = pl.BlockSpec(memory_space=pltpu.SEMAPHORE)
    n_after = len(afters)

    def body(own_ref, land_ref, s0, s1, s2, r0, r1, r2, *rest):
        for cp in _w_in_copies(own_ref, land_ref, (s0, s1, s2), (r0, r1, r2)):
            cp.wait_send()
            cp.wait_recv()

    return pl.pallas_call(
        body, name="w_in_wait", out_shape=(pltpu.HBM(own.shape, own.dtype), pltpu.HBM(land.shape, land.dtype)),
        in_specs=(hbm, hbm) + (sem,) * 6 + (pl.BlockSpec(memory_space=pl.ANY),) * n_after, out_specs=(hbm, hbm),
        input_output_aliases={0: 0, 1: 1},
        compiler_params=pltpu.CompilerParams(has_side_effects=pltpu.SideEffectType.DATAFLOW_SIDE_EFFECTING),
    )(own, land, *sems, *afters)


def _exchange_copies(part_refs, land_refs, send_sems, recv_sems):
    x, y, c = _mesh_pos()
    cps = []
    for w, (part, land) in enumerate(zip(part_refs, land_refs)):
        for j, chip in enumerate(_other_chips(x, y)):
            cps.append(pltpu.make_async_remote_copy(
                src_ref=part.at[2 * chip[0] + chip[1]], dst_ref=land.at[j],
                send_sem=send_sems[3 * w + j], recv_sem=recv_sems[3 * w + j],
                device_id=(*chip, c), device_id_type=MESH))
    return cps


def _exchange_send(parts, through, *, name):
    n = len(parts)
    hbm = pl.BlockSpec(memory_space=pltpu.HBM)
    sem = pl.BlockSpec(memory_space=pltpu.SEMAPHORE)
    any_spec = pl.BlockSpec(memory_space=pl.ANY)
    land_shapes = [(3,) + p.shape[1:] for p in parts]

    def body(*refs):
        part_refs, land_refs = refs[:n], refs[n:2 * n]
        sems = refs[2 * n + 1:8 * n + 1]
        for cp in _exchange_copies(part_refs, land_refs, sems[:3 * n], sems[3 * n:]):
            cp.start()

    outs = pl.pallas_call(
        body, name=name,
        out_shape=(pltpu.SemaphoreType.DMA(()),) * (6 * n)
        + tuple(pltpu.HBM(p.shape, p.dtype) for p in parts)
        + tuple(pltpu.HBM(s, p.dtype) for s, p in zip(land_shapes, parts))
        + (jax.ShapeDtypeStruct(through.shape, through.dtype),),
        in_specs=(hbm,) * (2 * n) + (any_spec,), out_specs=(sem,) * (6 * n) + (hbm,) * (2 * n) + (any_spec,),
        input_output_aliases={i: 6 * n + i for i in range(2 * n + 1)},
        compiler_params=pltpu.CompilerParams(has_side_effects=pltpu.SideEffectType.DATAFLOW_SIDE_EFFECTING),
    )(*[pltpu.with_memory_space_constraint(p, pltpu.HBM) for p in parts],
      *[pltpu.with_memory_space_constraint(lax.empty(s, p.dtype), pltpu.HBM) for s, p in zip(land_shapes, parts)],
      through)
    return outs[:6 * n], outs[6 * n:7 * n], outs[7 * n:8 * n], outs[8 * n]


def _exchange_wait(sems, parts, lands, afters, *, name):
    n = len(parts)
    hbm = pl.BlockSpec(memory_space=pltpu.HBM)
    sem = pl.BlockSpec(memory_space=pltpu.SEMAPHORE)

    def body(*refs):
        part_refs, land_refs = refs[:n], refs[n:2 * n]
        sem_refs = refs[2 * n:8 * n]
        for cp in _exchange_copies(part_refs, land_refs, sem_refs[:3 * n], sem_refs[3 * n:]):
            cp.wait_send()
            cp.wait_recv()

    outs = pl.pallas_call(
        body, name=name,
        out_shape=tuple(pltpu.HBM(p.shape, p.dtype) for p in parts) + tuple(pltpu.HBM(l.shape, l.dtype) for l in lands),
        in_specs=(hbm,) * (2 * n) + (sem,) * (6 * n) + (pl.BlockSpec(memory_space=pl.ANY),) * len(afters),
        out_specs=(hbm,) * (2 * n), input_output_aliases={i: i for i in range(2 * n)},
        compiler_params=pltpu.CompilerParams(has_side_effects=pltpu.SideEffectType.DATAFLOW_SIDE_EFFECTING),
    )(*parts, *lands, *sems, *afters)
    return outs[:n], outs[n:]


def _join_copies(full_refs, send_sems, recv_sems):
    x, y, c = _mesh_pos()
    cps = []
    for w, full in enumerate(full_refs):
        h = full.shape[0] // 2
        mine = full.at[pl.ds(c * h, h), :]
        cps.append(pltpu.make_async_remote_copy(
            src_ref=mine, dst_ref=mine, send_sem=send_sems[w], recv_sem=recv_sems[w],
            device_id=(x, y, 1 - c), device_id_type=MESH))
    return cps


def _join_send(fulls, *, name):
    n = len(fulls)
    hbm = pl.BlockSpec(memory_space=pltpu.HBM)
    sem = pl.BlockSpec(memory_space=pltpu.SEMAPHORE)

    def body(*refs):
        sems = refs[n:3 * n]
        for cp in _join_copies(refs[:n], sems[:n], sems[n:]):
            cp.start()
        token = refs[-1]
        token[...] = jnp.zeros_like(token)

    outs = pl.pallas_call(
        body, name=name,
        out_shape=(pltpu.SemaphoreType.DMA(()),) * (2 * n) + tuple(pltpu.HBM(f.shape, f.dtype) for f in fulls)
        + (jax.ShapeDtypeStruct((SUBLANES, LANES), F32),),
        in_specs=(hbm,) * n,
        out_specs=(sem,) * (2 * n) + (hbm,) * n + (pl.BlockSpec(memory_space=pltpu.VMEM),),
        input_output_aliases={i: 2 * n + i for i in range(n)},
        compiler_params=pltpu.CompilerParams(has_side_effects=pltpu.SideEffectType.DATAFLOW_SIDE_EFFECTING),
    )(*[pltpu.with_memory_space_constraint(f, pltpu.HBM) for f in fulls])
    return outs[:2 * n], list(outs[2 * n:3 * n]), outs[3 * n]


def _join_wait(sems, fulls, afters, *, name):
    n = len(fulls)
    hbm = pl.BlockSpec(memory_space=pltpu.HBM)
    sem = pl.BlockSpec(memory_space=pltpu.SEMAPHORE)

    def body(*refs):
        sem_refs = refs[n:3 * n]
        for cp in _join_copies(refs[:n], sem_refs[:n], sem_refs[n:]):
            cp.wait_send()
            cp.wait_recv()

    outs = pl.pallas_call(
        body, name=name, out_shape=tuple(pltpu.HBM(f.shape, f.dtype) for f in fulls),
        in_specs=(hbm,) * n + (sem,) * (2 * n) + (pl.BlockSpec(memory_space=pl.ANY),) * len(afters),
        out_specs=(hbm,) * n, input_output_aliases={i: i for i in range(n)},
        compiler_params=pltpu.CompilerParams(has_side_effects=pltpu.SideEffectType.DATAFLOW_SIDE_EFFECTING),
    )(*fulls, *sems, *afters)
    return list(outs)


def _gather_ici_copies(buf_refs, send_sems, recv_sems):
    x, y, c = _mesh_pos()
    me_chip = 2 * x + y
    cps = []
    for w, buf in enumerate(buf_refs):
        h = buf.shape[1] // 2
        ref = buf.at[me_chip, pl.ds(c * h, h), :]
        for j, chip in enumerate(_other_chips(x, y)):
            cps.append(pltpu.make_async_remote_copy(
                src_ref=ref, dst_ref=ref, send_sem=send_sems[3 * w + j], recv_sem=recv_sems[3 * w + j],
                device_id=(*chip, c), device_id_type=MESH))
    return cps


def _gather_send(bufs, *, name):
    n = len(bufs)
    hbm = pl.BlockSpec(memory_space=pltpu.HBM)
    sem = pl.BlockSpec(memory_space=pltpu.SEMAPHORE)

    def body(*refs):
        sems = refs[n:7 * n]
        for cp in _gather_ici_copies(refs[:n], sems[:3 * n], sems[3 * n:]):
            cp.start()

    outs = pl.pallas_call(
        body, name=name,
        out_shape=(pltpu.SemaphoreType.DMA(()),) * (6 * n) + tuple(pltpu.HBM(b.shape, b.dtype) for b in bufs),
        in_specs=(hbm,) * n, out_specs=(sem,) * (6 * n) + (hbm,) * n,
        input_output_aliases={i: 6 * n + i for i in range(n)},
        compiler_params=pltpu.CompilerParams(has_side_effects=pltpu.SideEffectType.DATAFLOW_SIDE_EFFECTING),
    )(*[pltpu.with_memory_space_constraint(b, pltpu.HBM) for b in bufs])
    send_sems, recv_sems = outs[:3 * n], outs[3 * n:6 * n]
    per_buf = [tuple(send_sems[3 * w:3 * w + 3]) + tuple(recv_sems[3 * w:3 * w + 3]) for w in range(n)]
    return per_buf, list(outs[6 * n:])


def _gather_wait(sems, bufs, afters, *, name):
    n = len(bufs)
    hbm = pl.BlockSpec(memory_space=pltpu.HBM)
    sem = pl.BlockSpec(memory_space=pltpu.SEMAPHORE)
    flat = [s for six in sems for s in six[:3]] + [s for six in sems for s in six[3:]]

    def body(*refs):
        sem_refs = refs[n:7 * n]
        for cp in _gather_ici_copies(refs[:n], sem_refs[:3 * n], sem_refs[3 * n:]):
            cp.wait_send()
            cp.wait_recv()

    outs = pl.pallas_call(
        body, name=name, out_shape=tuple(pltpu.HBM(b.shape, b.dtype) for b in bufs),
        in_specs=(hbm,) * n + (sem,) * (6 * n) + (pl.BlockSpec(memory_space=pl.ANY),) * len(afters),
        out_specs=(hbm,) * n, input_output_aliases={i: i for i in range(n)},
        compiler_params=pltpu.CompilerParams(has_side_effects=pltpu.SideEffectType.DATAFLOW_SIDE_EFFECTING),
    )(*bufs, *flat, *afters)
    return list(outs)


def _forward_abs_carry(bufs):
    n = len(bufs)
    sem = pltpu.SemaphoreType.DMA((3 * n,))

    def copies(outs, sems):
        send_sems, recv_sems = sems
        x, y, c = _mesh_pos()
        sends, recvs = [], []
        for w in range(n):
            h = bufs[w].shape[1] // 2
            for j, chip in enumerate(_other_chips(x, y)):
                slot = 2 * chip[0] + chip[1]
                mine = outs[w].at[slot, pl.ds(c * h, h), :]
                other = outs[w].at[slot, pl.ds((1 - c) * h, h), :]
                sends.append(pltpu.make_async_remote_copy(
                    src_ref=mine, dst_ref=mine, send_sem=send_sems.at[3 * w + j], recv_sem=recv_sems.at[3 * w + j],
                    device_id=(x, y, 1 - c), device_id_type=MESH))
                recvs.append(pltpu.make_async_remote_copy(
                    src_ref=other, dst_ref=other, send_sem=send_sems.at[3 * w + j], recv_sem=recv_sems.at[3 * w + j],
                    device_id=(x, y, c), device_id_type=MESH))
        return sends, recvs

    def start(ins, outs, sems):
        for cp in copies(outs, sems)[0]:
            cp.start()

    def finish(ins, outs, sems):
        sends, recvs = copies(outs, sems)
        for cp in recvs:
            cp.wait_recv()
        for cp in sends:
            cp.wait_send()

    shapes = [jax.ShapeDtypeStruct(b.shape, b.dtype) for b in bufs]
    return _Carry(bufs, shapes, {i: i for i in range(n)}, [sem, sem], start, finish)


def _forward_carry(land):
    n = land.shape[0] - 1
    h = land.shape[1] // 2
    sem = pltpu.SemaphoreType.DMA((n,))

    def copies(outs, sems):
        send_sems, recv_sems = sems
        x, y, c = _mesh_pos()
        sends, recvs = [], []
        for j in range(n):
            mine = outs[0].at[1 + j, pl.ds(c * h, h), :]
            other = outs[0].at[1 + j, pl.ds((1 - c) * h, h), :]
            sends.append(pltpu.make_async_remote_copy(
                src_ref=mine, dst_ref=mine, send_sem=send_sems.at[j], recv_sem=recv_sems.at[j],
                device_id=(x, y, 1 - c), device_id_type=MESH))
            recvs.append(pltpu.make_async_remote_copy(
                src_ref=other, dst_ref=other, send_sem=send_sems.at[j], recv_sem=recv_sems.at[j],
                device_id=(x, y, c), device_id_type=MESH))
        return sends, recvs

    def start(ins, outs, sems):
        for cp in copies(outs, sems)[0]:
            cp.start()

    def finish(ins, outs, sems):
        sends, recvs = copies(outs, sems)
        for cp in recvs:
            cp.wait_recv()
        for cp in sends:
            cp.wait_send()

    return _Carry([land], [jax.ShapeDtypeStruct(land.shape, land.dtype)], {0: 0}, [sem, sem], start, finish)


def _swap_carry(dws):
    n = len(dws)
    sem = pltpu.SemaphoreType.DMA((n,))

    def copies(ins, outs, sems):
        send_sems, recv_sems = sems
        x, y, c = _mesh_pos()
        cps = []
        for w in range(n):
            h = dws[w].shape[1] // 2
            cps.append(pltpu.make_async_remote_copy(
                src_ref=ins[w].at[:, pl.ds((1 - c) * h, h), :], dst_ref=outs[w],
                send_sem=send_sems.at[w], recv_sem=recv_sems.at[w],
                device_id=(x, y, 1 - c), device_id_type=MESH))
        return cps

    def start(ins, outs, sems):
        for cp in copies(ins, outs, sems):
            cp.start()

    def finish(ins, outs, sems):
        for cp in copies(ins, outs, sems):
            cp.wait()

    shapes = [jax.ShapeDtypeStruct((s.shape[0], s.shape[1] // 2, s.shape[2]), s.dtype) for s in dws]
    return _Carry(dws, shapes, {}, [sem, sem], start, finish)


def _merge_carries(carries):
    if len(carries) == 1:
        return carries[0]
    inputs, out_shapes, sem_shapes, aliases, spans = [], [], [], {}, []
    for cy in carries:
        i0, o0, s0 = len(inputs), len(out_shapes), len(sem_shapes)
        aliases.update({i0 + i: o0 + o for i, o in cy.aliases.items()})
        inputs += cy.inputs
        out_shapes += cy.out_shapes
        sem_shapes += cy.sem_shapes
        spans.append((slice(i0, len(inputs)), slice(o0, len(out_shapes)), slice(s0, len(sem_shapes))))

    def start(ins, outs, sems):
        for cy, (si, so, ss) in zip(carries, spans):
            cy.start(ins[si], outs[so], sems[ss])

    def finish(ins, outs, sems):
        for cy, (si, so, ss) in zip(carries, spans):
            cy.finish(ins[si], outs[so], sems[ss])

    return _Carry(inputs, out_shapes, aliases, sem_shapes, start, finish)


class _MeshComm:
    FORWARD_AT = {
        "conv_fwd": ["w_conv_out", "w_glu_a", "w_glu_b", "w_out"],
        "mm_out": ["w_ff1"],
        "mm_ff1": ["w_ff2"],
    }
    SWAP_AT = {
        "mm_d_ff2": ["w_ff2"],
        "mm_d_ff1": ["w_ff1"],
        "conv_bwd": ["w_out", "w_glu_a", "w_glu_b", "w_conv_out"],
    }
    EARLY_AT = "mm_dw_in"

    def __init__(self, shards, pos, chip, my_c):
        self.pos = pos
        self.chip = chip
        self.my_c = my_c
        self.shards = shards
        self.w_in_own, self.w_in_rel = _cast_bf16(shards["w_in"], name="cast_w_in")
        self.raw = {}
        self.flights = []
        self.halves = {}
        self.pending = {}

    def weight(self, name):
        g = self.bufs[name]
        return g.reshape(g.shape[0] * g.shape[1], g.shape[2]) if name in ROW_SHARDED else g

    def _slot_ids(self):
        x, y, _ = self.pos
        ids = [2 * x + y] + [2 * cx + cy for cx, cy in _other_chips(x, y)]
        return jnp.stack(ids).astype(jnp.int32)

    def start_w_in(self, after):
        *self.w_in_flight, token = _w_in_send(self.w_in_own, self.w_in_rel, after)
        order = [n for names in self.FORWARD_AT.values() for n in names]
        casts = [_cast_into_slot(self.shards[n], self.chip, token, name="cast_" + n) for n in order]
        sems, bufs = _gather_send(casts, name="gather_send")
        self.bufs = dict(zip(order, bufs))
        self.gather_sems = dict(zip(order, sems))
        return token

    def mm_in(self, u, afters):
        ids = self._slot_ids()
        sems, own, land = self.w_in_flight
        proj = _mm_slots(u, own[None], ids[0:1], None, name="mm_in_own")
        own, land = _w_in_wait(sems, own, land, [proj] + list(self.bufs.values()) + list(afters))
        land, = _run_carry(_forward_carry(land), name="forward_w_in")
        proj = _mm_slots(u, land, ids[1:4], proj, name="mm_in_rest", first=1)
        self.w_in_rel = land
        return proj

    def _add_and_send(self, names, landed, site, through):
        parts = [_add_half(self.raw.pop(n), l1, self.my_c, name="add_half_" + n) for n, l1 in zip(names, landed)]
        sems, parts, lands, through = _exchange_send(parts, through, name="exchange_send_" + site)
        self.flights.append((names, sems, parts, lands))
        return through

    def mm_d_in(self, dproj):
        landed = _run_carry(_swap_carry([self.raw["w_in"]]), name="swap_halves_w_in")
        dproj = self._add_and_send(["w_in"], landed, "w_in", dproj)
        return _mm(dproj, self.w_in_rel, mode="nt", out_dtype=F32, name="mm_d_in", a_slots=self._slot_ids())

    def early_grads(self, early):
        self.early = early

    def carry(self, site, args=()):
        jobs = []
        if site in self.FORWARD_AT:
            names = self.FORWARD_AT[site]
            landed = _gather_wait([self.gather_sems.pop(n) for n in names], [self.bufs[n] for n in names],
                                  [args[0]], name="gather_wait_" + site)
            jobs.append(("forward", names, _forward_abs_carry(landed)))
        if site == self.EARLY_AT:
            flat, self.early_offs = _pack(list(self.early.values()))
            jobs.append(("early", None, _gather_rows_carry(flat.reshape(-1, PACK_COLS))))
        if site in self.SWAP_AT:
            names = self.SWAP_AT[site]
            jobs.append(("swap", names, _swap_carry([self.raw[n] for n in names])))
        if not jobs:
            return None
        self.pending[site] = jobs
        return _merge_carries([job[2] for job in jobs])

    def done(self, site, carried, out=None):
        pos = 0
        for kind, items, carry in self.pending.pop(site):
            outs = carried[pos:pos + len(carry.out_shapes)]
            pos += len(carry.out_shapes)
            if kind == "early":
                self.early_all = outs[0]
            elif kind == "forward":
                self.bufs.update(zip(items, outs))
            elif isinstance(out, (list, tuple)):
                out = [self._add_and_send(items, outs, site, out[0])] + list(out[1:])
            else:
                out = self._add_and_send(items, outs, site, out)
        return out

    def grad(self, name, dw):
        if name in ROW_SHARDED:
            dw = dw.reshape(N_CHIPS, dw.shape[0] // N_CHIPS, dw.shape[1])
        self.raw[name] = dw

    def join_start(self, names, afters):
        for i, (group, sems, parts, lands) in enumerate(self.flights):
            parts, lands = _exchange_wait(sems, parts, lands, afters, name="exchange_wait_%d" % i)
            for n, part, land in zip(group, parts, lands):
                self.halves[n] = _sum_into_half(part, land, self.chip, self.my_c, name="sum_chips_" + n)
        self.flights = []
        sems, fulls, token = _join_send([self.halves.pop(n) for n in names], name="join_send")
        self.join_flight = (names, sems, fulls)
        return token

    def join_finish(self, afters):
        names, sems, fulls = self.join_flight
        return dict(zip(names, _join_wait(sems, fulls, afters, name="join_wait")))


def _local_step(x, target, mod, small, comm):
    rows, d = x.shape
    cw = d // 2
    shift1, scale1, gate1, shift2, scale2, gate2 = mod
    _, _, bbr, bbi = small["s5_disc"]
    b_in, c_out, b_out, c_in, mults = _s5_operands(*small["s5_loglam"], bbr, bbi, small["c_re"], small["c_im"])
    wt = comm.weight

    def riding(site, fn, *args, **kwargs):
        carry = comm.carry(site, args)
        if carry is None:
            return fn(*args, **kwargs)
        out, carried = fn(*args, carry=carry, **kwargs)
        return comm.done(site, carried, out)

    u = _norm_mod(x, small["norm1_g"], scale1, shift1, name="norm1_fwd")
    proj = comm.mm_in(u, [*b_in, *c_out, *b_out, *c_in, *mults])
    sl, cv = riding("conv_fwd", _conv_fwd, proj, small["w_dw"], small["b_dw"], small["ln_g"], small["ln_b"], cw=cw)
    y_conv = _mm(sl, wt("w_conv_out"), mode="nn", out_dtype=BF16, name="mm_conv_out")
    yg, st_re, st_im = riding("s5_fwd", _s5_fwd, proj, small["d_skip"], b_in, c_out, mults, col0=2 * cw // LANES)
    ya = riding("mm_glu_a", _mm, yg, wt("w_glu_a"), mode="nn", out_dtype=BF16, name="mm_glu_a")
    yb = riding("mm_glu_b", _mm, yg, wt("w_glu_b"), mode="nn", out_dtype=BF16, name="mm_glu_b")
    merged = _merge_fwd(proj, y_conv, ya, yb, cw=cw)
    mo = riding("mm_out", _mm, merged, wt("w_out"), mode="nn", out_dtype=BF16, name="mm_out")
    h1, z = _res_norm(x, mo, gate1, small["norm2_g"], scale2, shift2)
    f1 = riding("mm_ff1", _mm, z, wt("w_ff1"), mode="nn", out_dtype=BF16, name="mm_ff1")
    ff = _mm(f1, wt("w_ff2"), mode="nn", out_dtype=BF16, name="mm_ff2", a_fn=_relu2_bf16)
    dh2, dff, loss, d_final_g, d_gate2 = _final_fwd_bwd(h1, ff, gate2, small["final_g"], target)

    comm.grad("w_ff2", _mm(f1, dff, mode="tn", out_dtype=BF16, name="mm_dw_ff2", a_fn=_relu2_bf16))
    df1 = riding("mm_d_ff2", _mm, dff, wt("w_ff2"), mode="nt", out_dtype=BF16, name="mm_d_ff2", extra=f1,
                 epi=lambda acc, f: acc * (2.0 * jnp.maximum(f.astype(F32), 0.0)))
    comm.grad("w_ff1", riding("mm_dw_ff1", _mm, z, df1, mode="tn", out_dtype=BF16, name="mm_dw_ff1",
                              out_gathered=True))
    dz = riding("mm_d_ff1", _mm, df1, wt("w_ff1"), mode="nt", out_dtype=F32, name="mm_d_ff1")
    dh1, d_shift2, d_scale2, d_norm2_g, dmo, d_gate1 = riding(
        "norm2_bwd", _norm_mod_bwd, dz, h1, dh2, small["norm2_g"], scale2, gate1, mo, name="norm2_bwd")
    comm.grad("w_out", riding("mm_dw_out", _mm, merged, dmo, mode="tn", out_dtype=BF16, name="mm_dw_out"))
    dmerged = riding("mm_d_out", _mm, dmo, wt("w_out"), mode="nt", out_dtype=BF16, name="mm_d_out")
    dproj, dy_conv, dya, dyb = riding("merge_bwd", _merge_bwd, dmerged, proj, y_conv, ya, yb, cw=cw)
    comm.grad("w_glu_a", _mm(yg, dya, mode="tn", out_dtype=BF16, name="mm_dw_glu_a", out_gathered=True))
    comm.grad("w_glu_b", _mm(yg, dyb, mode="tn", out_dtype=BF16, name="mm_dw_glu_b", out_gathered=True))
    dyg_a = _mm(dya, wt("w_glu_a"), mode="nt", out_dtype=F32, name="mm_d_glu_a")
    dyg = _mm(dyb, wt("w_glu_b"), mode="nt", out_dtype=F32, name="mm_d_glu_b", extra=dyg_a,
              epi=lambda acc, e: acc + e)
    comm.grad("w_conv_out", _mm(sl, dy_conv, mode="tn", out_dtype=BF16, name="mm_dw_conv_out", out_gathered=True))
    dsl = _mm(dy_conv, wt("w_conv_out"), mode="nt", out_dtype=F32, name="mm_d_conv_out")
    dcv, d_ln_g, d_ln_b = _ln_bwd(dsl, cv, small["ln_g"], small["ln_b"])
    dproj, d_w_dw, d_b_dw = riding("conv_bwd", _conv_bwd, dcv, proj, small["w_dw"], dproj, cw=cw)
    dproj, d_d_skip, dbr, dbi, dcr, dci, dlr, dli = riding(
        "s5_bwd", _s5_bwd, proj, dyg, small["d_skip"], (st_re, st_im), c_out, b_out, c_in, mults, dproj,
        col0=2 * cw // LANES)
    sw = lambda m: jnp.swapaxes(m, 1, 2)
    early = {
        "dmod_tail": jnp.concatenate([d_gate1, d_shift2, d_scale2, d_gate2], axis=1), "loss": loss[:, 0:1],
        "w_dw": d_w_dw, "b_dw": d_b_dw, "ln_g": d_ln_g, "ln_b": d_ln_b,
        "lam_re": dlr.reshape(-1, SSM_STATE), "lam_im": dli.reshape(-1, SSM_STATE),
        "bb_re": sw(_block_diag_extract(dbr, SSM_GROUP, SSM_STATE)),
        "bb_im": sw(_block_diag_extract(dbi, SSM_GROUP, SSM_STATE)),
        "c_re": sw(_block_diag_extract(dcr, SSM_STATE, SSM_GROUP)),
        "c_im": sw(_block_diag_extract(dci, SSM_STATE, SSM_GROUP)),
        "d_skip": d_d_skip, "norm2_g": d_norm2_g, "final_g": d_final_g,
    }
    comm.early_grads(early)
    comm.grad("w_in", riding("mm_dw_in", _mm, u, dproj, mode="tn", out_dtype=BF16, name="mm_dw_in",
                             out_gathered=True))
    du = comm.mm_d_in(dproj)
    grad_x, d_shift1, d_scale1, d_norm1_g = riding(
        "norm1_bwd", _norm_mod_bwd, du, x, dh1, small["norm1_g"], scale1, None, None, name="norm1_bwd")
    late ={"dmod_head": jnp.concatenate([d_shift1, d_scale1], axis=1), "norm1_g": d_norm1_g}
    return grad_x, early, late


WEIGHT_NAMES = ["w_ada", "b_ada", "norm1_g", "w_in", "w_dw", "b_dw", "ln_g", "ln_b", "w_conv_out", "a_re", "a_im",
                "log_dt", "b_re", "b_im", "c_re", "c_im", "d_skip", "w_glu_a", "w_glu_b", "w_out", "norm2_g",
                "w_ff1", "w_ff2", "final_g"]
BIG_NAMES = ["w_in", "w_conv_out", "w_glu_a", "w_glu_b", "w_out", "w_ff1", "w_ff2"]
ROW_SHARDED = ("w_out", "w_ff2")
PACK_COLS = 1024
PACK_TILE = SUBLANES * PACK_COLS


def _pack(arrays):
    flats = [a.reshape(-1) for a in arrays]
    offs = []
    total = 0
    for f in flats:
        offs.append(total)
        total += f.shape[0]
    pad = (-total) % PACK_TILE
    if pad:
        flats.append(jnp.zeros((pad,), F32))
    return jnp.concatenate(flats), offs


def _unpack(flat, offs, like):
    return [flat[o:o + a.size].reshape(a.shape) for o, a in zip(offs, like)]


def kernel(x, c, w_ada, b_ada, norm1_g, w_in, w_dw, b_dw, ln_g, ln_b, w_conv_out, a_re, a_im, log_dt, b_re, b_im, c_re, c_im, d_skip, w_glu_a, w_glu_b, w_out, norm2_g, w_ff1, w_ff2, final_g, loss_target, m_w_ada, m_b_ada, m_norm1_g, m_w_in, m_w_dw, m_b_dw, m_ln_g, m_ln_b, m_w_conv_out, m_a_re, m_a_im, m_log_dt, m_b_re, m_b_im, m_c_re, m_c_im, m_d_skip, m_w_glu_a, m_w_glu_b, m_w_out, m_norm2_g, m_w_ff1, m_w_ff2, m_final_g, v_w_ada, v_b_ada, v_norm1_g, v_w_in, v_w_dw, v_b_dw, v_ln_g, v_ln_b, v_w_conv_out, v_a_re, v_a_im, v_log_dt, v_b_re, v_b_im, v_c_re, v_c_im, v_d_skip, v_w_glu_a, v_w_glu_b, v_w_out, v_norm2_g, v_w_ff1, v_w_ff2, v_final_g):
    given = dict(locals())
    w = {n: given[n] for n in WEIGHT_NAMES}
    m = {n: given["m_" + n] for n in WEIGHT_NAMES}
    v = {n: given["v_" + n] for n in WEIGHT_NAMES}
    d = x.shape[2]
    xi, yi, ci = _mesh_pos()
    chip = 2 * xi + yi
    dev = 4 * xi + 2 * yi + ci
    my_c = jnp.reshape(ci, (1,)).astype(jnp.int32)
    chip_arr = jnp.reshape(chip, (1,)).astype(jnp.int32)

    comm = _MeshComm({n: w[n][0] for n in BIG_NAMES}, (xi, yi, ci), chip_arr, my_c)

    ndw = w_dw.shape[2]
    assert d // SUBLANES == ndw
    first = jnp.concatenate([c.reshape(SUBLANES, ndw), jnp.pad(w_dw[0], ((0, HALO - CONV_KERNEL), (0, 0)))])
    first_all = _gather_small(first, name="gather_c_w_dw").reshape(N_DEV, SUBLANES + HALO, ndw)
    c_all = first_all[:, :SUBLANES].reshape(N_DEV, d)
    taps = first_all.reshape(N_CHIPS, 2, SUBLANES + HALO, ndw)[:, 0, SUBLANES:SUBLANES + CONV_KERNEL]
    w_dw_full = jnp.moveaxis(taps, 0, 1).reshape(CONV_KERNEL, N_CHIPS * ndw)

    nmod = w_ada.shape[2]
    b_cols = lax.dynamic_slice(b_ada, (0, chip * nmod), (1, nmod))
    mod_part = _ada_fwd(c_all, w_ada[0], b_cols)
    mod_all = _gather_small(mod_part, name="gather_mod").reshape(N_CHIPS, 2, N_DEV, nmod)[:, 0]
    mod_full = jnp.moveaxis(mod_all, 0, 1).reshape(N_DEV, N_CHIPS * nmod)
    mod_row = lax.dynamic_slice(mod_full, (dev, 0), (1, N_CHIPS * nmod))
    mod = [mod_row[:, i * d:(i + 1) * d] for i in range(6)]

    token = comm.start_w_in(mod_row)
    log_dt_0 = log_dt[0] + token[0, 0]

    disc_in = (a_re[0], a_im[0], log_dt_0, b_re[0], b_im[0])
    disc, disc_vjp = jax.vjp(_s5_discretise, *disc_in)
    dt = jnp.exp(log_dt_0)[:, None]
    small = {"norm1_g": norm1_g, "w_dw": w_dw_full, "b_dw": b_dw, "ln_g": ln_g, "ln_b": ln_b,
             "c_re": c_re[0], "c_im": c_im[0], "d_skip": d_skip, "norm2_g": norm2_g,
             "final_g": final_g[None, :], "s5_disc": disc, "s5_loglam": (a_re[0] * dt, a_im[0] * dt)}

    grad_x, early, late = _local_step(x[0], loss_target[0], mod, small, comm)
    grads = {}

    early_all = comm.early_all.reshape(N_DEV, -1, PACK_COLS)
    early_sum = _sum_leading(early_all, name="sum_small_grads").reshape(-1)
    summed = dict(zip(early, _unpack(early_sum, comm.early_offs, list(early.values()))))
    flat, late_offs = _pack(list(late.values()))
    late_all = _gather_small(flat.reshape(-1, PACK_COLS), name="gather_late_grads").reshape(N_DEV, -1, PACK_COLS)
    late_sum = _sum_leading(late_all, name="sum_late_grads").reshape(-1)
    summed.update(zip(late, _unpack(late_sum, late_offs, list(late.values()))))
    head = late_all[:, :2 * d // PACK_COLS].reshape(N_DEV, 2 * d)
    tail = early_all[:, :4 * d // PACK_COLS].reshape(N_DEV, 4 * d)
    dmod_all = jnp.concatenate([head, tail], axis=1)

    grads["w_ada"] = _ada_bwd(c_all, lax.dynamic_slice(dmod_all, (0, chip * nmod), (N_DEV, nmod)))
    grads["b_ada"] = _sum_leading(dmod_all.reshape(N_DEV, SUBLANES, 6 * d // SUBLANES),
                                  name="sum_b_ada").reshape(1, 6 * d)
    da_re, da_im, dlog_dt, db_re, db_im = disc_vjp(
        (summed["lam_re"], summed["lam_im"], summed["bb_re"], summed["bb_im"]))
    grads.update({
        "norm1_g": summed["norm1_g"], "w_dw": lax.dynamic_slice(summed["w_dw"], (0, chip * ndw), (CONV_KERNEL, ndw)),
        "b_dw": summed["b_dw"], "ln_g": summed["ln_g"], "ln_b": summed["ln_b"],
        "a_re": da_re, "a_im": da_im, "log_dt": dlog_dt, "b_re": db_re, "b_im": db_im,
        "c_re": summed["c_re"], "c_im": summed["c_im"], "d_skip": summed["d_skip"],
        "norm2_g": summed["norm2_g"], "final_g": summed["final_g"],
    })

    delta, new_m, new_v = {}, {}, {}

    def adam_big(n, after=None):
        shp = w[n].shape
        two_d = lambda a: a.reshape(shp[1], shp[2])
        res = _adamw(two_d(w[n]), two_d(grads[n]), two_d(m[n]), two_d(v[n]), name="adamw_" + n, after=after,
                     with_grad=n in BIG_NAMES)
        delta[n], new_m[n], new_v[n] = [r.reshape(shp) for r in res[:3]]
        if n in BIG_NAMES:
            grads[n] = res[3]

    token = comm.join_start(BIG_NAMES, [late_all])
    adam_big("w_ada", token)
    grads.update(comm.join_finish([delta["w_ada"]]))
    for n in BIG_NAMES:
        adam_big(n)
    grads = {n: grads[n].reshape(w[n].shape) for n in WEIGHT_NAMES}
    rest = [n for n in WEIGHT_NAMES if n not in delta]
    as_2d = lambda a: a.reshape(1, -1) if a.ndim == 1 else a
    outs = _adamw_many(*[[as_2d(src[n]) for n in rest] for src in (w, grads, m, v)], name="adamw_small")
    for dst, arrays in zip((delta, new_m, new_v), outs):
        for n, a in zip(rest, arrays):
            dst[n] = a.reshape(w[n].shape)

    return (summed["loss"].reshape(()), grad_x[None], *[grads[n] for n in WEIGHT_NAMES],
            *[delta[n] for n in WEIGHT_NAMES], *[new_m[n] for n in WEIGHT_NAMES],
            *[new_v[n] for n in WEIGHT_NAMES])
```

```python
import math

import jax
import jax.numpy as jnp
from jax import lax
from jax.experimental import pallas as pl
from jax.experimental.pallas import tpu as pltpu

F32 = jnp.float32
BF16 = jnp.bfloat16
EPS = 1e-6
CONV_KERNEL = 31
SSM_GROUP = 16
SSM_STATE = 64
ADAM_LR = 0.001
ADAM_B1 = 0.9
ADAM_B2 = 0.999
ADAM_EPS = 1e-08
ADAM_WD = 0.01
ADAM_STEP = 10

N_CHIPS = 4
N_DEV = 8
VMEM_LIMIT_BYTES = 56 * 1024 * 1024
LANES = 128
SUBLANES = 8
HALO = 32
GROUPS_PER_BLOCK = LANES // SSM_GROUP
STATE_LANES = GROUPS_PER_BLOCK * SSM_STATE
MESH = pl.DeviceIdType.MESH


def _cparams(sem):
    return pltpu.CompilerParams(dimension_semantics=sem, vmem_limit_bytes=VMEM_LIMIT_BYTES)


def _pick(n, pref, mult=LANES):
    if n <= pref:
        return n
    best = None
    for d in range(mult, pref + 1, mult):
        if n % d == 0:
            best = d
    assert best is not None, (n, pref)
    return best


def _sigmoid(v):
    return 1.0 / (1.0 + jnp.exp(-v))


def _gelu_parts(v):
    k0 = math.sqrt(2.0 / math.pi)
    inner = k0 * (v + 0.044715 * v * v * v)
    t = jnp.tanh(inner)
    return k0, t


def _gelu(v):
    _, t = _gelu_parts(v)
    return 0.5 * v * (1.0 + t)


def _gelu_grad(v):
    k0, t = _gelu_parts(v)
    return 0.5 * (1.0 + t) + 0.5 * v * (1.0 - t * t) * k0 * (1.0 + 3.0 * 0.044715 * v * v)


def _relu2_bf16(a):
    t = jnp.maximum(a.astype(F32), 0.0)
    return (t * t).astype(BF16)


class _Carry:
    def __init__(self, inputs, out_shapes, aliases, sem_shapes, start, finish):
        self.inputs = list(inputs)
        self.out_shapes = list(out_shapes)
        self.aliases = dict(aliases)
        self.sem_shapes = list(sem_shapes)
        self.start = start
        self.finish = finish


def _call(body, *, grid, in_specs, out_specs, out_shape, scratch_shapes, semantics, name, args, carry=None,
          prefetch=(), aliases=None):
    n_in, n_out, n_scr, n_pf = len(in_specs), len(out_specs), len(scratch_shapes), len(prefetch)
    own_aliases = {n_pf + i: o for i, o in (aliases or {}).items()}
    if carry is None:
        gs = pltpu.PrefetchScalarGridSpec(
            num_scalar_prefetch=n_pf, grid=grid, in_specs=in_specs, out_specs=out_specs,
            scratch_shapes=scratch_shapes)
        outs = pl.pallas_call(
            body, grid_spec=gs, out_shape=out_shape, input_output_aliases=own_aliases,
            compiler_params=_cparams(semantics), name=name)(*prefetch, *args)
        return list(outs), []
    ci, co = len(carry.inputs), len(carry.out_shapes)

    def wrapped(*refs):
        pf, refs = refs[:n_pf], refs[n_pf:]
        ins, cins = refs[:n_in], refs[n_in:n_in + ci]
        p = n_in + ci
        outs, couts = refs[p:p + n_out], refs[p + n_out:p + n_out + co]
        p += n_out + co
        scr, csems = refs[p:p + n_scr], refs[p + n_scr:]
        first = pl.program_id(0) == 0
        last = pl.program_id(0) == grid[0] - 1
        for ax in range(1, len(grid)):
            first = jnp.logical_and(first, pl.program_id(ax) == 0)
            last = jnp.logical_and(last, pl.program_id(ax) == grid[ax] - 1)

        @pl.when(first)
        def _():
            carry.start(cins, couts, csems)

        body(*pf, *ins, *outs, *scr)

        @pl.when(last)
        def _():
            carry.finish(cins, couts, csems)

    any_spec = pl.BlockSpec(memory_space=pl.ANY)
    gs = pltpu.PrefetchScalarGridSpec(
        num_scalar_prefetch=n_pf, grid=grid, in_specs=list(in_specs) + [any_spec] * ci,
        out_specs=list(out_specs) + [any_spec] * co, scratch_shapes=list(scratch_shapes) + carry.sem_shapes)
    all_aliases = dict(own_aliases)
    all_aliases.update({n_pf + n_in + i: n_out + o for i, o in carry.aliases.items()})
    outs = pl.pallas_call(
        wrapped, grid_spec=gs, out_shape=list(out_shape) + carry.out_shapes, input_output_aliases=all_aliases,
        compiler_params=_cparams(("arbitrary",) * len(grid)), name=name)(*prefetch, *args, *carry.inputs)
    return list(outs[:n_out]), list(outs[n_out:])


def _run_carry(carry, *, name):
    ci = len(carry.inputs)

    def body(*refs):
        cins, couts, csems = refs[:ci], refs[ci:ci + len(carry.out_shapes)], refs[ci + len(carry.out_shapes):]
        carry.start(cins, couts, csems)
        carry.finish(cins, couts, csems)

    any_spec = pl.BlockSpec(memory_space=pl.ANY)
    outs = pl.pallas_call(
        body, in_specs=[any_spec] * ci, out_specs=[any_spec] * len(carry.out_shapes), out_shape=carry.out_shapes,
        scratch_shapes=carry.sem_shapes, input_output_aliases=carry.aliases, name=name)(*carry.inputs)
    return list(outs)


def _mm(a, b, *, mode, out_dtype, name, out_gathered=False, a_fn=None, epi=None, extra=None,
        bm_pref=1024, bn_pref=1024, bk_pref=2048, carry=None, a_slots=None):
    gathered = (b.ndim == 3)
    if mode == "nn":
        m, kdim = a.shape
        ns = b.shape[-1]
        n = ns * (N_CHIPS if gathered else 1)
        bm, bn, bk = _pick(m, bm_pref), _pick(ns, bn_pref), _pick(kdim, bk_pref)
        npb = ns // bn
        grid = (m // bm, n // bn, kdim // bk)
        a_spec = pl.BlockSpec((bm, bk), lambda i, j, k: (i, k))
        if gathered:
            b_spec = pl.BlockSpec((None, bk, bn), lambda i, j, k: (j // npb, k, j % npb))
        else:
            b_spec = pl.BlockSpec((bk, bn), lambda i, j, k: (k, j))
        o_spec = pl.BlockSpec((bm, bn), lambda i, j, k: (i, j))
        e_spec = pl.BlockSpec((bm, bn), lambda i, j, k: (i, j))
        out_shape = (m, n)
        acc_shape = (bm, bn)
        dims = (((1,), (0,)), ((), ()))
    elif mode == "nt":
        m = a.shape[0]
        kdim, ns = b.shape[-2], b.shape[-1]
        n = ns * (N_CHIPS if gathered else 1)
        assert a.shape[1] == n
        bm, bko, bnr = _pick(m, bm_pref), _pick(kdim, bn_pref), _pick(ns, bk_pref)
        npb = ns // bnr
        grid = (m // bm, kdim // bko, n // bnr)
        a_spec = pl.BlockSpec((bm, bnr), lambda i, j, k: (i, k))
        if gathered:
            b_spec = pl.BlockSpec((None, bko, bnr), lambda i, j, k: (k // npb, j, k % npb))
        else:
            b_spec = pl.BlockSpec((bko, bnr), lambda i, j, k: (j, k))
        o_spec = pl.BlockSpec((bm, bko), lambda i, j, k: (i, j))
        e_spec = pl.BlockSpec((bm, bko), lambda i, j, k: (i, j))
        if a_slots is not None:
            assert gathered and extra is None
            a_spec = pl.BlockSpec((bm, bnr), lambda i, j, k, s_ref: (i, s_ref[k // npb] * npb + k % npb))
            b_spec = pl.BlockSpec((None, bko, bnr), lambda i, j, k, s_ref: (k // npb, j, k % npb))
            o_spec = pl.BlockSpec((bm, bko), lambda i, j, k, s_ref: (i, j))
        out_shape = (m, kdim)
        acc_shape = (bm, bko)
        dims = (((1,), (1,)), ((), ()))
    else:
        m, kdim = a.shape
        n = b.shape[1]
        ns = n // N_CHIPS if out_gathered else n
        bmr, bko, bn = _pick(m, bk_pref), _pick(kdim, bm_pref), _pick(ns, bn_pref)
        npb = ns // bn
        grid = (kdim // bko, n // bn, m // bmr)
        a_spec = pl.BlockSpec((bmr, bko), lambda i, j, k: (k, i))
        b_spec = pl.BlockSpec((bmr, bn), lambda i, j, k: (k, j))
        if out_gathered:
            o_spec = pl.BlockSpec((None, bko, bn), lambda i, j, k: (j // npb, i, j % npb))
            out_shape = (N_CHIPS, kdim, ns)
        else:
            o_spec = pl.BlockSpec((bko, bn), lambda i, j, k: (i, j))
            out_shape = (kdim, n)
        e_spec = None
        acc_shape = (bko, bn)
        dims = (((0,), (0,)), ((), ()))
    nk = grid[2]

    def body(*refs):
        if a_slots is not None:
            refs = refs[1:]
        if extra is not None:
            a_ref, b_ref, e_ref, o_ref, acc = refs
        else:
            a_ref, b_ref, o_ref, acc = refs
            e_ref = None
        k = pl.program_id(2)
        av = a_ref[...]
        if a_fn is not None:
            av = a_fn(av)
        part = lax.dot_general(av, b_ref[...], dims, preferred_element_type=F32)

        def finish(r):
            if epi is not None:
                r = epi(r, e_ref[...])
            o_ref[...] = r.astype(o_ref.dtype)

        if nk == 1:
            finish(part)
            return

        @pl.when(k == 0)
        def _():
            acc[...] = part

        @pl.when(jnp.logical_and(k > 0, k < nk - 1))
        def _():
            acc[...] += part

        @pl.when(k == nk - 1)
        def _():
            finish(acc[...] + part)

    in_specs = [a_spec, b_spec]
    args = [a, b]
    if extra is not None:
        in_specs.append(e_spec)
        args.append(extra)
    outs, carried = _call(body, grid=grid, in_specs=in_specs, out_specs=[o_spec],
                          out_shape=[jax.ShapeDtypeStruct(out_shape, out_dtype)],
                          scratch_shapes=[pltpu.VMEM(acc_shape, F32)],
                          semantics=("parallel", "parallel", "arbitrary"), name=name, args=args, carry=carry,
                          prefetch=() if a_slots is None else (a_slots,))
    return outs[0] if carry is None else (outs[0], carried)


def _mm_slots(a, wbuf, slots, prev, *, name, carry=None, first=0):
    m, kdim = a.shape
    ns = wbuf.shape[2]
    bm, bn = _pick(m, 1024), _pick(ns, 1024)
    npb = ns // bn
    grid = (m // bm, slots.shape[0], npb)

    def body(s_ref, a_ref, b_ref, *rest):
        o_ref = rest[-1]
        o_ref[...] = _dot(a_ref[...], b_ref[...]).astype(o_ref.dtype)

    in_specs = [pl.BlockSpec((bm, kdim), lambda i, s, j, s_ref: (i, 0)),
                pl.BlockSpec((None, kdim, bn), lambda i, s, j, s_ref: (first + s, 0, j))]
    args = [a, wbuf]
    aliases = None
    if prev is not None:
        in_specs.append(pl.BlockSpec(memory_space=pl.ANY))
        args.append(prev)
        aliases = {2: 0}
    outs, carried = _call(
        body, grid=grid, in_specs=in_specs,
        out_specs=[pl.BlockSpec((bm, bn), lambda i, s, j, s_ref: (i, s_ref[s] * npb + j))],
        out_shape=[jax.ShapeDtypeStruct((m, N_CHIPS * ns), BF16)], scratch_shapes=[],
        semantics=("parallel", "arbitrary", "arbitrary"), name=name, args=args, carry=carry,
        prefetch=(slots,), aliases=aliases)
    return outs[0] if carry is None else (outs[0], carried)


def _row_tile(rows, cols, n_arrays):
    budget = VMEM_LIMIT_BYTES // 2
    cap = min(512, budget // (n_arrays * 2 * cols * 4))
    for t in range(cap - cap % SUBLANES, 0, -SUBLANES):
        if rows % t == 0:
            return t
    return rows


def _norm_mod(x, g, scale, shift, *, name):
    rows, d = x.shape
    tr = _row_tile(rows, d, 3)

    def body(x_ref, g_ref, sc_ref, sh_ref, o_ref):
        xv = x_ref[...]
        r = lax.rsqrt(jnp.mean(xv * xv, axis=-1, keepdims=True) + EPS)
        o_ref[...] = ((xv * r * g_ref[...]) * (1.0 + sc_ref[...]) + sh_ref[...]).astype(o_ref.dtype)

    row = pl.BlockSpec((tr, d), lambda i: (i, 0))
    vec = pl.BlockSpec((1, d), lambda i: (0, 0))
    return pl.pallas_call(
        body, grid=(rows // tr,), in_specs=[row, vec, vec, vec], out_specs=row,
        out_shape=jax.ShapeDtypeStruct((rows, d), BF16),
        compiler_params=_cparams(("parallel",)), name=name)(x, g, scale, shift)


CONV_CHUNK = 4 * SUBLANES


def _shifted_copies(buf, n):
    for r in range(1, SUBLANES):
        buf[r, pl.ds(0, n - SUBLANES), :] = buf[0, pl.ds(r, n - SUBLANES), :]


def _conv_fwd(proj, w_dw, b_dw, ln_g, ln_b, *, cw, carry=None):
    rows = proj.shape[0]
    tt = _pick(rows, 256, HALO)
    hb = tt // HALO

    def body(a_ref, g_ref, ha_ref, hg_ref, w_ref, b_ref, lg_ref, lb_ref, sl_ref, cv_ref, vs):
        i = pl.program_id(0)
        hv = ha_ref[...].astype(F32) * _sigmoid(hg_ref[...].astype(F32))
        vs[0, pl.ds(0, HALO), :] = jnp.where(i == 0, 0.0, hv)
        vs[0, pl.ds(HALO, tt), :] = a_ref[...].astype(F32) * _sigmoid(g_ref[...].astype(F32))
        _shifted_copies(vs, HALO + tt)

        def chunk(ci, carry):
            r0 = pl.multiple_of(ci * CONV_CHUNK, CONV_CHUNK)
            acc = jnp.broadcast_to(b_ref[...], (CONV_CHUNK, cw))
            for k in range(CONV_KERNEL):
                q, r = divmod(HALO - (CONV_KERNEL - 1) + k, SUBLANES)
                acc = acc + w_ref[pl.ds(k, 1), :] * vs[r, pl.ds(r0 + q * SUBLANES, CONV_CHUNK), :]
            cv_ref[pl.ds(r0, CONV_CHUNK), :] = acc
            return carry

        lax.fori_loop(0, tt // CONV_CHUNK, chunk, 0)
        acc = cv_ref[...]
        mu = jnp.mean(acc, axis=-1, keepdims=True)
        xc = acc - mu
        rstd = lax.rsqrt(jnp.mean(xc * xc, axis=-1, keepdims=True) + EPS)
        ln = xc * rstd * lg_ref[...] + lb_ref[...]
        sl_ref[...] = (ln * _sigmoid(ln)).astype(sl_ref.dtype)

    tile = lambda c: pl.BlockSpec((tt, cw), lambda i, c=c: (i, c))
    halo = lambda c: pl.BlockSpec((HALO, cw), lambda i, c=c: (jnp.maximum(i * hb - 1, 0), c))
    vec = pl.BlockSpec((1, cw), lambda i: (0, 0))
    outs, carried = _call(
        body, grid=(rows // tt,),
        in_specs=[tile(0), tile(1), halo(0), halo(1),
                  pl.BlockSpec((CONV_KERNEL, cw), lambda i: (0, 0)), vec, vec, vec],
        out_specs=[pl.BlockSpec((tt, cw), lambda i: (i, 0)), pl.BlockSpec((tt, cw), lambda i: (i, 0))],
        out_shape=[jax.ShapeDtypeStruct((rows, cw), BF16), jax.ShapeDtypeStruct((rows, cw), F32)],
        scratch_shapes=[pltpu.VMEM((SUBLANES, HALO + tt, cw), F32)],
        semantics=("parallel",), name="conv_fwd", args=[proj, proj, proj, proj, w_dw, b_dw, ln_g, ln_b],
        carry=carry)
    return outs if carry is None else (outs, carried)


def _ln_bwd(dsl, cv, ln_g, ln_b):
    rows, cw = cv.shape
    tr = _row_tile(rows, cw, 3)

    def body(d_ref, cv_ref, lg_ref, lb_ref, o_ref, dg_ref, db_ref):
        i = pl.program_id(0)

        @pl.when(i == 0)
        def _():
            dg_ref[...] = jnp.zeros_like(dg_ref)
            db_ref[...] = jnp.zeros_like(db_ref)

        x = cv_ref[...]
        mu = jnp.mean(x, axis=-1, keepdims=True)
        xc = x - mu
        rstd = lax.rsqrt(jnp.mean(xc * xc, axis=-1, keepdims=True) + EPS)
        xh = xc * rstd
        ln = xh * lg_ref[...] + lb_ref[...]
        s = _sigmoid(ln)
        dln = d_ref[...].astype(F32) * (s * (1.0 + ln * (1.0 - s)))
        dg_ref[...] += jnp.sum(dln * xh, axis=0, keepdims=True)
        db_ref[...] += jnp.sum(dln, axis=0, keepdims=True)
        dxh = dln * lg_ref[...]
        m1 = jnp.mean(dxh, axis=-1, keepdims=True)
        m2 = jnp.mean(dxh * xh, axis=-1, keepdims=True)
        o_ref[...] = rstd * (dxh - m1 - xh * m2)

    row = pl.BlockSpec((tr, cw), lambda i: (i, 0))
    vec = pl.BlockSpec((1, cw), lambda i: (0, 0))
    return pl.pallas_call(
        body, grid=(rows // tr,), in_specs=[row, row, vec, vec], out_specs=[row, vec, vec],
        out_shape=[jax.ShapeDtypeStruct((rows, cw), F32), jax.ShapeDtypeStruct((1, cw), F32),
                   jax.ShapeDtypeStruct((1, cw), F32)],
        compiler_params=_cparams(("arbitrary",)), name="ln_bwd")(dsl, cv, ln_g, ln_b)


def _conv_bwd(dcv, proj, w_dw, dproj, *, cw, carry=None):
    rows = proj.shape[0]
    tt = _pick(rows, 256, HALO)
    hb = tt // HALO
    nt = rows // tt
    taps = CONV_KERNEL

    def body(d_ref, dn_ref, a_ref, g_ref, ha_ref, hg_ref, w_ref, dproj_ref, o_ref, dw_ref, db_ref, vs, ds):
        i = pl.program_id(0)

        @pl.when(i == 0)
        def _():
            dw_ref[...] = jnp.zeros_like(dw_ref)
            db_ref[...] = jnp.zeros_like(db_ref)

        hv = ha_ref[...].astype(F32) * _sigmoid(hg_ref[...].astype(F32))
        vs[0, pl.ds(0, HALO), :] = jnp.where(i == 0, 0.0, hv)
        vs[0, pl.ds(HALO, tt), :] = a_ref[...].astype(F32) * _sigmoid(g_ref[...].astype(F32))
        _shifted_copies(vs, HALO + tt)
        ds[0, pl.ds(0, tt), :] = d_ref[...]
        ds[0, pl.ds(tt, HALO), :] = jnp.where(i == nt - 1, 0.0, dn_ref[...])
        _shifted_copies(ds, tt + HALO)
        db_ref[...] += jnp.sum(d_ref[...], axis=0, keepdims=True)
        for k in range(taps):
            q, r = divmod(HALO - (taps - 1) + k, SUBLANES)
            dw_ref[pl.ds(k, 1), :] += jnp.sum(d_ref[...] * vs[r, pl.ds(q * SUBLANES, tt), :], axis=0, keepdims=True)

        def chunk(ci, carry):
            r0 = pl.multiple_of(ci * CONV_CHUNK, CONV_CHUNK)
            dv = jnp.zeros((CONV_CHUNK, cw), F32)
            for k in range(taps):
                q, r = divmod(taps - 1 - k, SUBLANES)
                dv = dv + w_ref[pl.ds(k, 1), :] * ds[r, pl.ds(r0 + q * SUBLANES, CONV_CHUNK), :]
            av = a_ref[pl.ds(r0, CONV_CHUNK), :].astype(F32)
            sg = _sigmoid(g_ref[pl.ds(r0, CONV_CHUNK), :].astype(F32))
            o_ref[pl.ds(r0, CONV_CHUNK), pl.ds(0, cw)] = (dv * sg).astype(o_ref.dtype)
            o_ref[pl.ds(r0, CONV_CHUNK), pl.ds(cw, cw)] = (dv * av * sg * (1.0 - sg)).astype(o_ref.dtype)
            return carry

        lax.fori_loop(0, tt // CONV_CHUNK, chunk, 0)

    tile = lambda c: pl.BlockSpec((tt, cw), lambda i, c=c: (i, c))
    halo = lambda c: pl.BlockSpec((HALO, cw), lambda i, c=c: (jnp.maximum(i * hb - 1, 0), c))
    nxt = pl.BlockSpec((HALO, cw), lambda i: (jnp.minimum((i + 1) * hb, nt * hb - 1), 0))
    outs, carried = _call(
        body, grid=(nt,),
        in_specs=[pl.BlockSpec((tt, cw), lambda i: (i, 0)), nxt, tile(0), tile(1), halo(0), halo(1),
                  pl.BlockSpec((taps, cw), lambda i: (0, 0)), pl.BlockSpec(memory_space=pl.ANY)],
        out_specs=[pl.BlockSpec((tt, 2 * cw), lambda i: (i, 0)),
                   pl.BlockSpec((taps, cw), lambda i: (0, 0)), pl.BlockSpec((1, cw), lambda i: (0, 0))],
        out_shape=[jax.ShapeDtypeStruct(dproj.shape, dproj.dtype), jax.ShapeDtypeStruct((taps, cw), F32),
                   jax.ShapeDtypeStruct((1, cw), F32)],
        scratch_shapes=[pltpu.VMEM((SUBLANES, HALO + tt, cw), F32), pltpu.VMEM((SUBLANES, tt + HALO, cw), F32)],
        semantics=("arbitrary",), name="conv_bwd", args=[dcv, dcv, proj, proj, proj, proj, w_dw, dproj],
        carry=carry, aliases={7: 0})
    return outs if carry is None else (outs, carried)


def _merge_fwd(proj, y_conv, ya, yb, *, cw):
    rows = proj.shape[0]
    tr = _row_tile(rows, cw, 4)

    def body(gc_ref, gs_ref, yc_ref, ya_ref, yb_ref, o_ref):
        ys = ya_ref[...].astype(F32) * _sigmoid(yb_ref[...].astype(F32))
        o_ref[...] = (_sigmoid(gc_ref[...].astype(F32)) * yc_ref[...].astype(F32)
                      + _sigmoid(gs_ref[...].astype(F32)) * ys).astype(o_ref.dtype)

    blk = lambda off: pl.BlockSpec((tr, cw), lambda i, h, off=off: (i, off + h))
    return pl.pallas_call(
        body, grid=(rows // tr, 2), in_specs=[blk(3), blk(5), blk(0), blk(0), blk(0)], out_specs=blk(0),
        out_shape=jax.ShapeDtypeStruct((rows, 2 * cw), BF16),
        compiler_params=_cparams(("parallel", "parallel")), name="merge_fwd")(proj, proj, y_conv, ya, yb)


def _merge_bwd(dmerged, proj, y_conv, ya, yb, *, cw, carry=None):
    rows = proj.shape[0]
    tr = _row_tile(rows, cw, 3)

    def body(d_ref, g_ref, yc_ref, ya_ref, yb_ref, dg_ref, dyc_ref, dya_ref, dyb_ref):
        q = pl.program_id(1)
        d = d_ref[...].astype(F32)
        sg = _sigmoid(g_ref[...].astype(F32))

        @pl.when(q < 2)
        def _():
            dg_ref[...] = (d * yc_ref[...].astype(F32) * sg * (1.0 - sg)).astype(dg_ref.dtype)
            dyc_ref[...] = (d * sg).astype(dyc_ref.dtype)

        @pl.when(q >= 2)
        def _():
            sb = _sigmoid(yb_ref[...].astype(F32))
            yav = ya_ref[...].astype(F32)
            dg_ref[...] = (d * (yav * sb) * sg * (1.0 - sg)).astype(dg_ref.dtype)
            dys = d * sg
            dya_ref[...] = (dys * sb).astype(dya_ref.dtype)
            dyb_ref[...] = (dys * yav * sb * (1.0 - sb)).astype(dyb_ref.dtype)

    spec = lambda f: pl.BlockSpec((tr, cw), lambda i, q, f=f: (i, f(q)))
    conv_half = spec(lambda q: jnp.minimum(q, 1))
    ssm_half = spec(lambda q: jnp.maximum(q - 2, 0))
    o2 = jax.ShapeDtypeStruct((rows, 2 * cw), BF16)
    outs, carried = _call(
        body, grid=(rows // tr, 4),
        in_specs=[spec(lambda q: q % 2), spec(lambda q: 3 + q), conv_half, ssm_half, ssm_half],
        out_specs=[spec(lambda q: 3 + q), conv_half, ssm_half, ssm_half],
        out_shape=[jax.ShapeDtypeStruct((rows, 7 * cw), BF16), o2, o2, o2], scratch_shapes=[],
        semantics=("parallel", "arbitrary"), name="merge_bwd", args=[dmerged, proj, y_conv, ya, yb],
        carry=carry)
    return outs if carry is None else (outs, carried)


def _res_norm(x, mo, gate, g, scale, shift):
    rows, d = x.shape
    tr = _row_tile(rows, d, 4)

    def body(x_ref, mo_ref, gt_ref, g_ref, sc_ref, sh_ref, h_ref, z_ref):
        h = x_ref[...] + gt_ref[...] * mo_ref[...].astype(F32)
        h_ref[...] = h
        r = lax.rsqrt(jnp.mean(h * h, axis=-1, keepdims=True) + EPS)
        z_ref[...] = ((h * r * g_ref[...]) * (1.0 + sc_ref[...]) + sh_ref[...]).astype(z_ref.dtype)

    row = pl.BlockSpec((tr, d), lambda i: (i, 0))
    vec = pl.BlockSpec((1, d), lambda i: (0, 0))
    return pl.pallas_call(
        body, grid=(rows // tr,), in_specs=[row, row, vec, vec, vec, vec], out_specs=[row, row],
        out_shape=[jax.ShapeDtypeStruct((rows, d), F32), jax.ShapeDtypeStruct((rows, d), BF16)],
        compiler_params=_cparams(("parallel",)), name="res_norm")(x, mo, gate, g, scale, shift)


def _final_fwd_bwd(h1, ff, gate2, final_g, target):
    rows, d = h1.shape
    tr = _row_tile(rows, d, 5)

    def body(h_ref, ff_ref, gt_ref, fg_ref, t_ref, dh_ref, dff_ref, loss_ref, dfg_ref, dgt_ref):
        i = pl.program_id(0)

        @pl.when(i == 0)
        def _():
            loss_ref[...] = jnp.zeros_like(loss_ref)
            dfg_ref[...] = jnp.zeros_like(dfg_ref)
            dgt_ref[...] = jnp.zeros_like(dgt_ref)

        ffv = ff_ref[...].astype(F32)
        h2 = h_ref[...] + gt_ref[...] * ffv
        r = lax.rsqrt(jnp.mean(h2 * h2, axis=-1, keepdims=True) + EPS)
        y = h2 * r
        e = y * fg_ref[...] - t_ref[...]
        loss_ref[...] += 0.5 * jnp.sum(jnp.mean(e * e, axis=-1, keepdims=True))
        dout = e * (1.0 / d)
        dfg_ref[...] += jnp.sum(dout * y, axis=0, keepdims=True)
        dy = dout * fg_ref[...]
        dh2 = r * (dy - y * jnp.mean(dy * y, axis=-1, keepdims=True))
        dh_ref[...] = dh2
        dgt_ref[...] += jnp.sum(dh2 * ffv, axis=0, keepdims=True)
        dff_ref[...] = (dh2 * gt_ref[...]).astype(dff_ref.dtype)

    row = pl.BlockSpec((tr, d), lambda i: (i, 0))
    vec = pl.BlockSpec((1, d), lambda i: (0, 0))
    return pl.pallas_call(
        body, grid=(rows // tr,), in_specs=[row, row, vec, vec, row],
        out_specs=[row, row, pl.BlockSpec((1, LANES), lambda i: (0, 0)), vec, vec],
        out_shape=[jax.ShapeDtypeStruct((rows, d), F32), jax.ShapeDtypeStruct((rows, d), BF16),
                   jax.ShapeDtypeStruct((1, LANES), F32), jax.ShapeDtypeStruct((1, d), F32),
                   jax.ShapeDtypeStruct((1, d), F32)],
        compiler_params=_cparams(("arbitrary",)), name="final_fwd_bwd")(h1, ff, gate2, final_g, target)


def _norm_mod_bwd(dz, hin, dres, g, scale, gate, mo, *, name, carry=None):
    rows, d = hin.shape
    with_gate = gate is not None
    tr = _row_tile(rows, d, 6)

    def body(*refs):
        if with_gate:
            (dz_ref, h_ref, dr_ref, g_ref, sc_ref, gt_ref, mo_ref,
             dh_ref, dsh_ref, dsc_ref, dg_ref, dmo_ref, dgt_ref) = refs
        else:
            dz_ref, h_ref, dr_ref, g_ref, sc_ref, dh_ref, dsh_ref, dsc_ref, dg_ref = refs
        i = pl.program_id(0)

        @pl.when(i == 0)
        def _():
            dsh_ref[...] = jnp.zeros_like(dsh_ref)
            dsc_ref[...] = jnp.zeros_like(dsc_ref)
            dg_ref[...] = jnp.zeros_like(dg_ref)
            if with_gate:
                dgt_ref[...] = jnp.zeros_like(dgt_ref)

        dzv = dz_ref[...].astype(F32)
        h = h_ref[...]
        r = lax.rsqrt(jnp.mean(h * h, axis=-1, keepdims=True) + EPS)
        y = h * r
        dsh_ref[...] += jnp.sum(dzv, axis=0, keepdims=True)
        dsc_ref[...] += jnp.sum(dzv * (y * g_ref[...]), axis=0, keepdims=True)
        dn = dzv * (1.0 + sc_ref[...])
        dg_ref[...] += jnp.sum(dn * y, axis=0, keepdims=True)
        dy = dn * g_ref[...]
        dh = dr_ref[...] + r * (dy - y * jnp.mean(dy * y, axis=-1, keepdims=True))
        dh_ref[...] = dh
        if with_gate:
            dmo_ref[...] = (dh * gt_ref[...]).astype(dmo_ref.dtype)
            dgt_ref[...] += jnp.sum(dh * mo_ref[...].astype(F32), axis=0, keepdims=True)

    row = pl.BlockSpec((tr, d), lambda i: (i, 0))
    vec = pl.BlockSpec((1, d), lambda i: (0, 0))
    vshape = jax.ShapeDtypeStruct((1, d), F32)
    in_specs = [row, row, row, vec, vec]
    args = [dz, hin, dres, g, scale]
    out_specs = [row, vec, vec, vec]
    out_shape = [jax.ShapeDtypeStruct((rows, d), F32), vshape, vshape, vshape]
    if with_gate:
        in_specs += [vec, row]
        args += [gate, mo]
        out_specs += [row, vec]
        out_shape += [jax.ShapeDtypeStruct((rows, d), BF16), vshape]
    outs, carried = _call(
        body, grid=(rows // tr,), in_specs=in_specs, out_specs=out_specs, out_shape=out_shape,
        scratch_shapes=[], semantics=("arbitrary",), name=name, args=args, carry=carry)
    return outs if carry is None else (outs, carried)


def _s5_discretise(a_re, a_im, log_dt, b_re, b_im):
    dt = jnp.exp(log_dt)[:, None]
    er = jnp.exp(a_re * dt)
    lr = er * jnp.cos(a_im * dt)
    li = er * jnp.sin(a_im * dt)
    den = a_re * a_re + a_im * a_im
    cr = ((lr - 1.0) * a_re + li * a_im) / den
    ci = (li * a_re - (lr - 1.0) * a_im) / den
    bbr = cr[..., None] * b_re - ci[..., None] * b_im
    bbi = cr[..., None] * b_im + ci[..., None] * b_re
    return lr, li, bbr, bbi


def _block_diag(w):
    g, r, c = w.shape
    nb = g // GROUPS_PER_BLOCK
    eye = jnp.eye(GROUPS_PER_BLOCK, dtype=w.dtype)
    w5 = w.reshape(nb, GROUPS_PER_BLOCK, r, 1, c) * eye[None, :, None, :, None]
    return w5.reshape(nb, GROUPS_PER_BLOCK * r, GROUPS_PER_BLOCK * c)


def _block_diag_extract(m, r, c):
    nb = m.shape[0]
    m5 = m.reshape(nb, GROUPS_PER_BLOCK, r, GROUPS_PER_BLOCK, c)
    idx = jnp.arange(GROUPS_PER_BLOCK)
    d = m5[:, idx, :, idx, :]
    return jnp.moveaxis(d, 0, 1).reshape(nb * GROUPS_PER_BLOCK, r, c)


def _scan_multipliers(lr, li):
    power = jnp.arange(1, SUBLANES + 1, dtype=F32)[None, :, None]
    er = jnp.exp(power * lr)
    pr = er * jnp.cos(power * li)
    pi = er * jnp.sin(power * li)
    rows = jnp.arange(SUBLANES)[None, :, None]
    fr, fi, rr, ri = [], [], [], []
    for s in (1, 2, 4):
        mf = (rows >= s).astype(F32)
        mr = (rows <= SUBLANES - 1 - s).astype(F32)
        fr.append(mf * pr[:, s - 1:s, :])
        fi.append(mf * pi[:, s - 1:s, :])
        rr.append(mr * pr[:, s - 1:s, :])
        ri.append(mr * pi[:, s - 1:s, :])
    fr.append(pr)
    fi.append(pi)
    rr.append(pr[:, ::-1, :])
    ri.append(pi[:, ::-1, :])
    st = lambda xs: jnp.stack(xs, axis=1)
    return st(fr), st(fi), st(rr), st(ri)


def _scan_rows(sre, sim, mul_r, mul_i, n_groups, reverse):
    sgn = -1.0 if reverse else 1.0
    lanes = sre.shape[1]

    def step(k, carry):
        cr, ci = carry
        kk = (n_groups - 1 - k) if reverse else k
        r0 = pl.multiple_of(kk * SUBLANES, SUBLANES)
        xr = sre[pl.ds(r0, SUBLANES), :]
        xi = sim[pl.ds(r0, SUBLANES), :]
        for lvl, s in enumerate((1, 2, 4)):
            sh = (SUBLANES - s) if reverse else s
            nr = pltpu.roll(xr, sh, 0)
            ni = pltpu.roll(xi, sh, 0)
            mr = mul_r[lvl]
            mi = mul_i[lvl] * sgn
            xr, xi = xr + mr * nr - mi * ni, xi + mr * ni + mi * nr
        mr = mul_r[3]
        mi = mul_i[3] * sgn
        xr, xi = xr + mr * cr - mi * ci, xi + mr * ci + mi * cr
        sre[pl.ds(r0, SUBLANES), :] = xr
        sim[pl.ds(r0, SUBLANES), :] = xi
        edge = 0 if reverse else SUBLANES - 1
        ncr = jnp.broadcast_to(xr[edge:edge + 1, :], (SUBLANES, lanes))
        nci = jnp.broadcast_to(xi[edge:edge + 1, :], (SUBLANES, lanes))
        return ncr, nci

    zero = jnp.zeros((SUBLANES, lanes), F32)
    lax.fori_loop(0, n_groups, step, (zero, zero))


def _dot(a, b):
    return jnp.dot(a, b, preferred_element_type=F32)


def _dotf(a, b):
    return _dot(a.astype(BF16), b)


def _s5_operands(lr, li, bbr, bbi, c_re, c_im):
    g = lr.shape[0]
    nb = g // GROUPS_PER_BLOCK
    tb = lambda w: jnp.swapaxes(w, 1, 2)
    b_in = [_block_diag(tb(bbr)), _block_diag(tb(bbi))]
    c_out = [_block_diag(tb(c_re)), _block_diag(tb(c_im))]
    b_out = [_block_diag(bbr), _block_diag(bbi)]
    c_in = [_block_diag(c_re), _block_diag(c_im)]
    lam_r = lr.reshape(nb, 1, STATE_LANES)
    lam_i = li.reshape(nb, 1, STATE_LANES)
    mults = _scan_multipliers(lam_r, lam_i)
    cast = lambda ws: [w.astype(BF16) for w in ws]
    return cast(b_in), cast(c_out), cast(b_out), cast(c_in), mults


def _s5_fwd(proj, d_skip, b_in, c_out, mults, *, col0, carry=None):
    rows = proj.shape[0]
    nb = b_in[0].shape[0]
    tm = _pick(rows, 512, SUBLANES)
    n_tiles = rows // tm
    s_l = STATE_LANES

    def body(u_ref, dk_ref, br, bi, cr, ci, fr_ref, fi_ref, o_ref, sr_ref, si_ref, sre, sim):
        for t in range(n_tiles):
            rs = pl.ds(t * tm, tm)
            ub = u_ref[rs, :]
            sre[rs, :] = _dot(ub, br[...])
            sim[rs, :] = _dot(ub, bi[...])
        _scan_rows(sre, sim, fr_ref, fi_ref, rows // SUBLANES, False)
        for t in range(n_tiles):
            rs = pl.ds(t * tm, tm)
            srb = sre[rs, :].astype(BF16)
            sib = sim[rs, :].astype(BF16)
            sr_ref[rs, :] = srb
            si_ref[rs, :] = sib
            y0 = _dot(srb, cr[...]) - _dot(sib, ci[...])
            y1 = y0 + dk_ref[...] * u_ref[rs, :].astype(F32)
            o_ref[rs, :] = _gelu(y1).astype(o_ref.dtype)

    mat_in = pl.BlockSpec((None, LANES, s_l), lambda g: (g, 0, 0))
    mat_out = pl.BlockSpec((None, s_l, LANES), lambda g: (g, 0, 0))
    mul = pl.BlockSpec((None, 4, SUBLANES, s_l), lambda g: (g, 0, 0, 0))
    state = pl.BlockSpec((rows, s_l), lambda g: (0, g))
    outs, carried = _call(
        body, grid=(nb,),
        in_specs=[pl.BlockSpec((rows, LANES), lambda g: (0, col0 + g)), pl.BlockSpec((1, LANES), lambda g: (0, g))]
        + [mat_in] * 2 + [mat_out] * 2 + [mul] * 2,
        out_specs=[pl.BlockSpec((rows, LANES), lambda g: (0, g)), state, state],
        out_shape=[jax.ShapeDtypeStruct((rows, nb * LANES), BF16), jax.ShapeDtypeStruct((rows, nb * s_l), BF16),
                   jax.ShapeDtypeStruct((rows, nb * s_l), BF16)],
        scratch_shapes=[pltpu.VMEM((rows, s_l), F32), pltpu.VMEM((rows, s_l), F32)],
        semantics=("parallel",), name="s5_fwd", args=[proj, d_skip, *b_in, *c_out, mults[0], mults[1]], carry=carry)
    return outs if carry is None else (outs, carried)


def _s5_bwd(proj, dyg, d_skip, states, c_out, b_out, c_in, mults, dproj, *, col0, carry=None):
    rows = proj.shape[0]
    nb = c_out[0].shape[0]
    tm = _pick(rows, 512, SUBLANES)
    n_tiles = rows // tm
    s_l = STATE_LANES
    n_groups = rows // SUBLANES
    tn = (((0,), (0,)), ((), ()))

    def body(u_ref, dy_ref, dk_ref, sr_ref, si_ref, cr, ci, bor, boi, cir, cii, rr_ref, ri_ref, dproj_ref,
             du_ref, ddk_ref, dbr_ref, dbi_ref, dcr_ref, dci_ref, dlr_ref, dli_ref,
             gre, gim, dy1):
        ddk = jnp.zeros((1, LANES), F32)
        dcr = jnp.zeros((s_l, LANES), F32)
        dci = jnp.zeros((s_l, LANES), F32)
        for t in range(n_tiles):
            rs = pl.ds(t * tm, tm)
            srb = sr_ref[rs, :]
            sib = si_ref[rs, :]
            uf = u_ref[rs, :].astype(F32)
            y0 = _dot(srb, cr[...]) - _dot(sib, ci[...])
            y1 = y0 + dk_ref[...] * uf
            d1 = dy_ref[rs, :].astype(F32) * _gelu_grad(y1)
            dy1[rs, :] = d1
            ddk = ddk + jnp.sum(d1 * uf, axis=0, keepdims=True)
            d1b = d1.astype(BF16)
            dcr = dcr + lax.dot_general(srb, d1b, tn, preferred_element_type=F32)
            dci = dci - lax.dot_general(sib, d1b, tn, preferred_element_type=F32)
            gre[rs, :] = _dot(d1b, cir[...])
            gim[rs, :] = -_dot(d1b, cii[...])
        ddk_ref[...] = ddk
        dcr_ref[...] = dcr
        dci_ref[...] = dci

        last_row = lax.broadcasted_iota(jnp.int32, (SUBLANES, s_l), 0) == SUBLANES - 1

        def group(r0, s_r, s_i, carry):
            cr_, ci_, ar, ai = carry
            xr = gre[pl.ds(r0, SUBLANES), :]
            xi = gim[pl.ds(r0, SUBLANES), :]
            for lvl, s in enumerate((1, 2, 4)):
                nr = pltpu.roll(xr, SUBLANES - s, 0)
                ni = pltpu.roll(xi, SUBLANES - s, 0)
                mr = rr_ref[lvl]
                mi = ri_ref[lvl]
                xr, xi = xr + mr * nr + mi * ni, xi + mr * ni - mi * nr
            mr = rr_ref[3]
            mi = ri_ref[3]
            xr, xi = xr + mr * cr_ + mi * ci_, xi + mr * ci_ - mi * cr_
            gre[pl.ds(r0, SUBLANES), :] = xr
            gim[pl.ds(r0, SUBLANES), :] = xi
            nxt_r = jnp.where(last_row, cr_, pltpu.roll(xr, SUBLANES - 1, 0))
            nxt_i = jnp.where(last_row, ci_, pltpu.roll(xi, SUBLANES - 1, 0))
            ncr = jnp.broadcast_to(xr[0:1, :], (SUBLANES, s_l))
            nci = jnp.broadcast_to(xi[0:1, :], (SUBLANES, s_l))
            return ncr, nci, ar + nxt_r * s_r + nxt_i * s_i, ai + nxt_i * s_r - nxt_r * s_i

        def rev_step(k, carry):
            r0 = pl.multiple_of((n_groups // 2 - 1 - k) * 2 * SUBLANES, 2 * SUBLANES)
            s_r = sr_ref[pl.ds(r0, 2 * SUBLANES), :].astype(F32)
            s_i = si_ref[pl.ds(r0, 2 * SUBLANES), :].astype(F32)
            carry = group(r0 + SUBLANES, s_r[SUBLANES:], s_i[SUBLANES:], carry)
            return group(r0, s_r[:SUBLANES], s_i[:SUBLANES], carry)

        zero = jnp.zeros((SUBLANES, s_l), F32)
        _, _, ar, ai = lax.fori_loop(0, n_groups // 2, rev_step, (zero, zero, zero, zero))
        dlr_ref[...] = jnp.sum(ar, axis=0, keepdims=True)
        dli_ref[...] = jnp.sum(ai, axis=0, keepdims=True)

        dbr = jnp.zeros((LANES, s_l), F32)
        dbi = jnp.zeros((LANES, s_l), F32)
        for t in range(n_tiles):
            rs = pl.ds(t * tm, tm)
            gr = gre[rs, :]
            gi = gim[rs, :]
            grb = gr.astype(BF16)
            gib = gi.astype(BF16)
            du = _dot(grb, bor[...]) + _dot(gib, boi[...]) + dy1[rs, :] * dk_ref[...]
            du_ref[rs, :] = du.astype(du_ref.dtype)
            ub = u_ref[rs, :]
            dbr = dbr + lax.dot_general(ub, grb, tn, preferred_element_type=F32)
            dbi = dbi + lax.dot_general(ub, gib, tn, preferred_element_type=F32)
        dbr_ref[...] = dbr
        dbi_ref[...] = dbi

    mat_in = pl.BlockSpec((None, LANES, s_l), lambda g: (g, 0, 0))
    mat_out = pl.BlockSpec((None, s_l, LANES), lambda g: (g, 0, 0))
    mul = pl.BlockSpec((None, 4, SUBLANES, s_l), lambda g: (g, 0, 0, 0))
    lam = pl.BlockSpec((None, 1, s_l), lambda g: (g, 0, 0))
    col = pl.BlockSpec((rows, LANES), lambda g: (0, g))
    vec = pl.BlockSpec((1, LANES), lambda g: (0, g))
    state = pl.BlockSpec((rows, s_l), lambda g: (0, g))
    outs, carried = _call(
        body, grid=(nb,),
        in_specs=[pl.BlockSpec((rows, LANES), lambda g: (0, col0 + g)), col, vec]
        + [state] * 2 + [mat_out] * 2 + [mat_out] * 2 + [mat_in] * 2 + [mul] * 2
        + [pl.BlockSpec(memory_space=pl.ANY)],
        out_specs=[pl.BlockSpec((rows, LANES), lambda g: (0, col0 + g)), vec, mat_in, mat_in, mat_out, mat_out,
                   lam, lam],
        out_shape=[jax.ShapeDtypeStruct(dproj.shape, dproj.dtype), jax.ShapeDtypeStruct((1, nb * LANES), F32),
                   jax.ShapeDtypeStruct((nb, LANES, s_l), F32), jax.ShapeDtypeStruct((nb, LANES, s_l), F32),
                   jax.ShapeDtypeStruct((nb, s_l, LANES), F32), jax.ShapeDtypeStruct((nb, s_l, LANES), F32),
                   jax.ShapeDtypeStruct((nb, 1, s_l), F32), jax.ShapeDtypeStruct((nb, 1, s_l), F32)],
        scratch_shapes=[pltpu.VMEM((rows, s_l), F32)] * 2 + [pltpu.VMEM((rows, LANES), F32)],
        semantics=("parallel",), name="s5_bwd",
        args=[proj, dyg, d_skip, *states, *c_out, *b_out, *c_in, mults[2], mults[3], dproj], carry=carry,
        aliases={13: 0})
    return outs if carry is None else (outs, carried)


def _silu(v):
    return v * _sigmoid(v)


def _ada_fwd(c_all, w_shard, b_cols):
    d, n = w_shard.shape
    bn = _pick(n, 512)

    def body(c_ref, w_ref, b_ref, o_ref):
        ca = _silu(c_ref[...]).astype(BF16)
        o_ref[...] = _dot(ca, w_ref[...].astype(BF16)) + b_ref[...]

    return pl.pallas_call(
        body, grid=(n // bn,),
        in_specs=[pl.BlockSpec((N_DEV, d), lambda j: (0, 0)), pl.BlockSpec((d, bn), lambda j: (0, j)),
                  pl.BlockSpec((1, bn), lambda j: (0, j))],
        out_specs=pl.BlockSpec((N_DEV, bn), lambda j: (0, j)),
        out_shape=jax.ShapeDtypeStruct((N_DEV, n), F32),
        compiler_params=_cparams(("parallel",)), name="ada_fwd")(c_all, w_shard, b_cols)


def _ada_bwd(c_all, dmod_cols):
    d = c_all.shape[1]
    n = dmod_cols.shape[1]
    bn = _pick(n, 512)

    def body(c_ref, g_ref, o_ref):
        ca = _silu(c_ref[...]).astype(BF16)
        o_ref[...] = lax.dot_general(ca, g_ref[...].astype(BF16), (((0,), (0,)), ((), ())),
                                     preferred_element_type=F32)

    return pl.pallas_call(
        body, grid=(n // bn,),
        in_specs=[pl.BlockSpec((N_DEV, d), lambda j: (0, 0)), pl.BlockSpec((N_DEV, bn), lambda j: (0, j))],
        out_specs=pl.BlockSpec((d, bn), lambda j: (0, j)),
        out_shape=jax.ShapeDtypeStruct((d, n), F32),
        compiler_params=_cparams(("parallel",)), name="ada_bwd")(c_all, dmod_cols)


def _cast_bf16(w, *, name):
    rows, cols = w.shape
    tr = _row_tile(rows, cols, 3)

    def body(w_ref, o_ref, slot_ref):
        o_ref[...] = w_ref[...].astype(BF16)
        slot_ref[...] = w_ref[...].astype(BF16)

    row = pl.BlockSpec((tr, cols), lambda i: (i, 0))
    return pl.pallas_call(
        body, grid=(rows // tr,), in_specs=[row],
        out_specs=[row, pl.BlockSpec((None, tr, cols), lambda i: (0, i, 0))],
        out_shape=[jax.ShapeDtypeStruct((rows, cols), BF16), jax.ShapeDtypeStruct((N_CHIPS, rows, cols), BF16)],
        compiler_params=_cparams(("parallel",)), name=name)(w)


def _adamw_update(w_ref, g_ref, m_ref, v_ref, d_ref, nm_ref, nv_ref):
    c1 = 1.0 / (1.0 - ADAM_B1 ** ADAM_STEP)
    c2 = 1.0 / (1.0 - ADAM_B2 ** ADAM_STEP)
    gv = g_ref[...]
    nm = ADAM_B1 * m_ref[...] + (1.0 - ADAM_B1) * gv
    nv = ADAM_B2 * v_ref[...] + (1.0 - ADAM_B2) * (gv * gv)
    nm_ref[...] = nm
    nv_ref[...] = nv
    d_ref[...] = -ADAM_LR * ((nm * c1) / (jnp.sqrt(nv * c2) + ADAM_EPS) + ADAM_WD * w_ref[...])


def _adamw_many(ws, gs, ms, vs, *, name):
    n = len(ws)

    def body(*refs):
        for i in range(n):
            _adamw_update(*[refs[k * n + i] for k in range(7)])

    vm = pl.BlockSpec(memory_space=pltpu.VMEM)
    shapes = [jax.ShapeDtypeStruct(a.shape, F32) for a in ws]
    outs = pl.pallas_call(
        body, in_specs=[vm] * (4 * n), out_specs=[vm] * (3 * n), out_shape=shapes * 3,
        compiler_params=pltpu.CompilerParams(vmem_limit_bytes=VMEM_LIMIT_BYTES), name=name)(*ws, *gs, *ms, *vs)
    return list(outs[:n]), list(outs[n:2 * n]), list(outs[2 * n:])


def _adamw(w, g, m, v, *, name, after=None, with_grad=False):
    rows, cols = w.shape
    tr = _row_tile(rows, cols, 8)
    n_out = 4 if with_grad else 3

    def body(w_ref, g_ref, m_ref, v_ref, *rest):
        outs = rest[-n_out:]
        _adamw_update(w_ref, g_ref, m_ref, v_ref, *outs[:3])
        if with_grad:
            outs[3][...] = g_ref[...]

    row = pl.BlockSpec((tr, cols), lambda i: (i, 0))
    shp = jax.ShapeDtypeStruct((rows, cols), F32)
    extra = [] if after is None else [after]
    outs, _ = _call(
        body, grid=(rows // tr,), in_specs=[row] * 4 + [pl.BlockSpec(memory_space=pl.ANY)] * len(extra),
        out_specs=[row] * n_out, out_shape=[shp] * n_out, scratch_shapes=[], semantics=("parallel",), name=name,
        args=[w, g, m, v] + extra)
    return outs


def _sum_leading(a, *, name, out_dtype=F32):
    n, rows, cols = a.shape
    tr = _row_tile(rows, cols, n + 1)

    def body(a_ref, o_ref):
        acc = a_ref[0].astype(F32)
        for i in range(1, n):
            acc = acc + a_ref[i].astype(F32)
        o_ref[...] = acc.astype(o_ref.dtype)

    return pl.pallas_call(
        body, grid=(rows // tr,), in_specs=[pl.BlockSpec((n, tr, cols), lambda i: (0, i, 0))],
        out_specs=pl.BlockSpec((tr, cols), lambda i: (i, 0)),
        out_shape=jax.ShapeDtypeStruct((rows, cols), out_dtype),
        compiler_params=_cparams(("parallel",)), name=name)(a)


def _add_half(dw, land, my_c, *, name):
    n, r, cols = dw.shape
    h = r // 2
    tr = _row_tile(h, cols, 3)
    hb = h // tr

    def body(c_ref, a_ref, b_ref, o_ref):
        o_ref[...] = (a_ref[...].astype(F32) + b_ref[...].astype(F32)).astype(o_ref.dtype)

    gs = pltpu.PrefetchScalarGridSpec(
        num_scalar_prefetch=1, grid=(n, hb),
        in_specs=[pl.BlockSpec((None, tr, cols), lambda s, i, c_ref: (s, c_ref[0] * hb + i, 0)),
                  pl.BlockSpec((None, tr, cols), lambda s, i, c_ref: (s, i, 0))],
        out_specs=pl.BlockSpec((None, tr, cols), lambda s, i, c_ref: (s, i, 0)))
    return pl.pallas_call(
        body, grid_spec=gs, out_shape=jax.ShapeDtypeStruct((n, h, cols), BF16),
        compiler_params=_cparams(("parallel", "parallel")), name=name)(my_c, dw, land)


def _mesh_pos():
    return lax.axis_index("x"), lax.axis_index("y"), lax.axis_index("c")


def _other_chips(x, y):
    return [(1 - x, y), (x, 1 - y), (1 - x, 1 - y)]


def _gather_small(blk, *, name):
    m_per, n = blk.shape

    def body(x_ref, out_ref, send_sems, recv_sems, local_sem):
        x, y, c = _mesh_pos()
        me, sibling = (x, y, c), (x, y, 1 - c)
        chips = _other_chips(x, y)

        def rows(px, py, pc):
            return out_ref.at[pl.ds((4 * px + 2 * py + pc) * m_per, m_per), :]

        def copy(k, block, to, src=None):
            return pltpu.make_async_remote_copy(
                src_ref=rows(*block) if src is None else src, dst_ref=rows(*block),
                send_sem=send_sems.at[k], recv_sem=recv_sems.at[k], device_id=to, device_id_type=MESH)

        mine = pltpu.make_async_copy(x_ref, rows(*me), local_sem)
        mine.start()
        first = [copy(0, me, sibling, src=x_ref)]
        first += [copy(1 + j, me, (*chip, c), src=x_ref) for j, chip in enumerate(chips)]
        for cp in first:
            cp.start()
        passed = [copy(4 + j, (*chip, c), sibling) for j, chip in enumerate(chips)]
        for j, chip in enumerate(chips):
            copy(1 + j, (*chip, c), me).wait_recv()
            passed[j].start()
        copy(0, sibling, me).wait_recv()
        for j, chip in enumerate(chips):
            copy(4 + j, (*chip, 1 - c), me).wait_recv()
        for cp in first + passed:
            cp.wait_send()
        mine.wait()

    return pl.pallas_call(
        body, out_shape=jax.ShapeDtypeStruct((N_DEV * m_per, n), blk.dtype),
        in_specs=[pl.BlockSpec(memory_space=pltpu.VMEM)], out_specs=pl.BlockSpec(memory_space=pltpu.VMEM),
        scratch_shapes=[pltpu.SemaphoreType.DMA((7,)), pltpu.SemaphoreType.DMA((7,)), pltpu.SemaphoreType.DMA],
        compiler_params=pltpu.CompilerParams(vmem_limit_bytes=VMEM_LIMIT_BYTES), name=name)(blk)


def _cast_into_slot(w, chip, after, *, name):
    rows, cols = w.shape
    tr = _row_tile(rows, cols, 2)

    def body(chip_ref, w_ref, after_ref, o_ref):
        o_ref[...] = w_ref[...].astype(BF16)

    gs = pltpu.PrefetchScalarGridSpec(
        num_scalar_prefetch=1, grid=(rows // tr,),
        in_specs=[pl.BlockSpec((tr, cols), lambda i, chip_ref: (i, 0)), pl.BlockSpec(memory_space=pl.ANY)],
        out_specs=pl.BlockSpec((None, tr, cols), lambda i, chip_ref: (chip_ref[0], i, 0)))
    return pl.pallas_call(
        body, grid_spec=gs, out_shape=jax.ShapeDtypeStruct((N_CHIPS, rows, cols), BF16),
        compiler_params=_cparams(("parallel",)), name=name)(chip, w, after)


def _sum_into_half(part, landed, chip, my_c, *, name):
    _, h, cols = part.shape
    tr = _row_tile(h, cols, 5)
    hb = h // tr

    def body(chip_ref, c_ref, p_ref, l_ref, o_ref):
        acc = p_ref[...].astype(F32)
        for j in range(3):
            acc = acc + l_ref[j].astype(F32)
        o_ref[...] = acc

    gs = pltpu.PrefetchScalarGridSpec(
        num_scalar_prefetch=2, grid=(hb,),
        in_specs=[pl.BlockSpec((None, tr, cols), lambda i, chip_ref, c_ref: (chip_ref[0], i, 0)),
                  pl.BlockSpec((3, tr, cols), lambda i, chip_ref, c_ref: (0, i, 0))],
        out_specs=pl.BlockSpec((tr, cols), lambda i, chip_ref, c_ref: (c_ref[0] * hb + i, 0)))
    return pl.pallas_call(
        body, grid_spec=gs, out_shape=jax.ShapeDtypeStruct((2 * h, cols), F32),
        compiler_params=_cparams(("parallel",)), name=name)(chip, my_c, part, landed)


class _NoComm:
    def __init__(self, big):
        self.big = big
        self.grads = {}

    def weight(self, name):
        return self.big[name]

    def mm_in(self, u, afters):
        return _mm(u, self.big["w_in"], mode="nn", out_dtype=BF16, name="mm_in")

    def mm_d_in(self, dproj):
        return _mm(dproj, self.big["w_in"], mode="nt", out_dtype=F32, name="mm_d_in")

    def carry(self, site, args=()):
        return None

    def done(self, site, carried, out=None):
        return out

    def grad(self, name, dw):
        self.grads[name] = dw

    def early_grads(self, early):
        self.early = early


def _gather_rows_carry(blk):
    m_per = blk.shape[0]
    sem = pltpu.SemaphoreType.DMA((7,))

    def copies(ins, outs, sems):
        send_sems, recv_sems, local_sem = sems
        x, y, c = _mesh_pos()
        me, sibling = (x, y, c), (x, y, 1 - c)
        chips = _other_chips(x, y)

        def rows(px, py, pc):
            return outs[0].at[pl.ds((4 * px + 2 * py + pc) * m_per, m_per), :]

        def copy(k, block, to, src=None):
            return pltpu.make_async_remote_copy(
                src_ref=rows(*block) if src is None else src, dst_ref=rows(*block),
                send_sem=send_sems.at[k], recv_sem=recv_sems.at[k], device_id=to, device_id_type=MESH)

        mine = pltpu.make_async_copy(ins[0], rows(*me), local_sem.at[0])
        first = [copy(0, me, sibling, src=ins[0])]
        first += [copy(1 + j, me, (*chip, c), src=ins[0]) for j, chip in enumerate(chips)]
        passed = [copy(4 + j, (*chip, c), sibling) for j, chip in enumerate(chips)]
        landed = [copy(1 + j, (*chip, c), me) for j, chip in enumerate(chips)]
        from_sibling = [copy(0, sibling, me)] + [copy(4 + j, (*chip, 1 - c), me) for j, chip in enumerate(chips)]
        return mine, first, passed, landed, from_sibling

    def start(ins, outs, sems):
        mine, first, _, _, _ = copies(ins, outs, sems)
        mine.start()
        for cp in first:
            cp.start()

    def finish(ins, outs, sems):
        mine, first, passed, landed, from_sibling = copies(ins, outs, sems)
        for arrived, onward in zip(landed, passed):
            arrived.wait_recv()
            onward.start()
        for arrived in from_sibling:
            arrived.wait_recv()
        for cp in first + passed:
            cp.wait_send()
        mine.wait()

    shape = jax.ShapeDtypeStruct((N_DEV * m_per, blk.shape[1]), blk.dtype)
    return _Carry([blk], [shape], {}, [sem, sem, pltpu.SemaphoreType.DMA((1,))], start, finish)


def _w_in_copies(own_ref, land_ref, send_sems, recv_sems):
    x, y, c = _mesh_pos()
    h = own_ref.shape[0] // 2
    return [pltpu.make_async_remote_copy(
        src_ref=own_ref.at[pl.ds(c * h, h), :], dst_ref=land_ref.at[1 + j, pl.ds(c * h, h), :],
        send_sem=send_sems[j], recv_sem=recv_sems[j], device_id=(*chip, c), device_id_type=MESH)
        for j, chip in enumerate(_other_chips(x, y))]


def _w_in_send(own, land, after):
    hbm = pl.BlockSpec(memory_space=pltpu.HBM)
    sem = pl.BlockSpec(memory_space=pltpu.SEMAPHORE)
    land_shape = land.shape

    def body(own_ref, land_ref, after_ref, s0, s1, s2, r0, r1, r2, own_thru, land_thru, token):
        for cp in _w_in_copies(own_ref, land_ref, (s0, s1, s2), (r0, r1, r2)):
            cp.start()
        token[...] = jnp.zeros_like(token)

    outs = pl.pallas_call(
        body, name="w_in_send",
        out_shape=(pltpu.SemaphoreType.DMA(()),) * 6 + (
            pltpu.HBM(own.shape, own.dtype), pltpu.HBM(land_shape, own.dtype), jax.ShapeDtypeStruct((8, LANES), F32)),
        in_specs=(hbm, hbm, pl.BlockSpec(memory_space=pl.ANY)),
        out_specs=(sem,) * 6 + (hbm, hbm, pl.BlockSpec(memory_space=pltpu.VMEM)),
        input_output_aliases={0: 6, 1: 7},
        compiler_params=pltpu.CompilerParams(has_side_effects=pltpu.SideEffectType.DATAFLOW_SIDE_EFFECTING),
    )(pltpu.with_memory_space_constraint(own, pltpu.HBM), pltpu.with_memory_space_constraint(land, pltpu.HBM), after)
    return outs[:6], outs[6], outs[7], outs[8]


def _w_in_wait(sems, own, land, afters):
    hbm = pl.BlockSpec(memory_space=pltpu.HBM)
    sem = pl.BlockSpec(memory_space=pltpu.SEMAPHORE)
    n_after = len(afters)

    def body(own_ref, land_ref, s0, s1, s2, r0, r1, r2, *rest):
        for cp in _w_in_copies(own_ref, land_ref, (s0, s1, s2), (r0, r1, r2)):
            cp.wait_send()
            cp.wait_recv()

    return pl.pallas_call(
        body, name="w_in_wait", out_shape=(pltpu.HBM(own.shape, own.dtype), pltpu.HBM(land.shape, land.dtype)),
        in_specs=(hbm, hbm) + (sem,) * 6 + (pl.BlockSpec(memory_space=pl.ANY),) * n_after, out_specs=(hbm, hbm),
        input_output_aliases={0: 0, 1: 1},
        compiler_params=pltpu.CompilerParams(has_side_effects=pltpu.SideEffectType.DATAFLOW_SIDE_EFFECTING),
    )(own, land, *sems, *afters)


def _exchange_copies(part_refs, land_refs, send_sems, recv_sems):
    x, y, c = _mesh_pos()
    cps = []
    for w, (part, land) in enumerate(zip(part_refs, land_refs)):
        for j, chip in enumerate(_other_chips(x, y)):
            cps.append(pltpu.make_async_remote_copy(
                src_ref=part.at[2 * chip[0] + chip[1]], dst_ref=land.at[j],
                send_sem=send_sems[3 * w + j], recv_sem=recv_sems[3 * w + j],
                device_id=(*chip, c), device_id_type=MESH))
    return cps


def _exchange_send(parts, through, *, name):
    n = len(parts)
    hbm = pl.BlockSpec(memory_space=pltpu.HBM)
    sem = pl.BlockSpec(memory_space=pltpu.SEMAPHORE)
    any_spec = pl.BlockSpec(memory_space=pl.ANY)
    land_shapes = [(3,) + p.shape[1:] for p in parts]

    def body(*refs):
        part_refs, land_refs = refs[:n], refs[n:2 * n]
        sems = refs[2 * n + 1:8 * n + 1]
        for cp in _exchange_copies(part_refs, land_refs, sems[:3 * n], sems[3 * n:]):
            cp.start()

    outs = pl.pallas_call(
        body, name=name,
        out_shape=(pltpu.SemaphoreType.DMA(()),) * (6 * n)
        + tuple(pltpu.HBM(p.shape, p.dtype) for p in parts)
        + tuple(pltpu.HBM(s, p.dtype) for s, p in zip(land_shapes, parts))
        + (jax.ShapeDtypeStruct(through.shape, through.dtype),),
        in_specs=(hbm,) * (2 * n) + (any_spec,), out_specs=(sem,) * (6 * n) + (hbm,) * (2 * n) + (any_spec,),
        input_output_aliases={i: 6 * n + i for i in range(2 * n + 1)},
        compiler_params=pltpu.CompilerParams(has_side_effects=pltpu.SideEffectType.DATAFLOW_SIDE_EFFECTING),
    )(*[pltpu.with_memory_space_constraint(p, pltpu.HBM) for p in parts],
      *[pltpu.with_memory_space_constraint(lax.empty(s, p.dtype), pltpu.HBM) for s, p in zip(land_shapes, parts)],
      through)
    return outs[:6 * n], outs[6 * n:7 * n], outs[7 * n:8 * n], outs[8 * n]


def _exchange_wait(sems, parts, lands, afters, *, name):
    n = len(parts)
    hbm = pl.BlockSpec(memory_space=pltpu.HBM)
    sem = pl.BlockSpec(memory_space=pltpu.SEMAPHORE)

    def body(*refs):
        part_refs, land_refs = refs[:n], refs[n:2 * n]
        sem_refs = refs[2 * n:8 * n]
        for cp in _exchange_copies(part_refs, land_refs, sem_refs[:3 * n], sem_refs[3 * n:]):
            cp.wait_send()
            cp.wait_recv()

    outs = pl.pallas_call(
        body, name=name,
        out_shape=tuple(pltpu.HBM(p.shape, p.dtype) for p in parts) + tuple(pltpu.HBM(l.shape, l.dtype) for l in lands),
        in_specs=(hbm,) * (2 * n) + (sem,) * (6 * n) + (pl.BlockSpec(memory_space=pl.ANY),) * len(afters),
        out_specs=(hbm,) * (2 * n), input_output_aliases={i: i for i in range(2 * n)},
        compiler_params=pltpu.CompilerParams(has_side_effects=pltpu.SideEffectType.DATAFLOW_SIDE_EFFECTING),
    )(*parts, *lands, *sems, *afters)
    return outs[:n], outs[n:]


def _join_copies(full_refs, send_sems, recv_sems):
    x, y, c = _mesh_pos()
    cps = []
    for w, full in enumerate(full_refs):
        h = full.shape[0] // 2
        mine = full.at[pl.ds(c * h, h), :]
        cps.append(pltpu.make_async_remote_copy(
            src_ref=mine, dst_ref=mine, send_sem=send_sems[w], recv_sem=recv_sems[w],
            device_id=(x, y, 1 - c), device_id_type=MESH))
    return cps


def _join_send(fulls, *, name):
    n = len(fulls)
    hbm = pl.BlockSpec(memory_space=pltpu.HBM)
    sem = pl.BlockSpec(memory_space=pltpu.SEMAPHORE)

    def body(*refs):
        sems = refs[n:3 * n]
        for cp in _join_copies(refs[:n], sems[:n], sems[n:]):
            cp.start()
        token = refs[-1]
        token[...] = jnp.zeros_like(token)

    outs = pl.pallas_call(
        body, name=name,
        out_shape=(pltpu.SemaphoreType.DMA(()),) * (2 * n) + tuple(pltpu.HBM(f.shape, f.dtype) for f in fulls)
        + (jax.ShapeDtypeStruct((SUBLANES, LANES), F32),),
        in_specs=(hbm,) * n,
        out_specs=(sem,) * (2 * n) + (hbm,) * n + (pl.BlockSpec(memory_space=pltpu.VMEM),),
        input_output_aliases={i: 2 * n + i for i in range(n)},
        compiler_params=pltpu.CompilerParams(has_side_effects=pltpu.SideEffectType.DATAFLOW_SIDE_EFFECTING),
    )(*[pltpu.with_memory_space_constraint(f, pltpu.HBM) for f in fulls])
    return outs[:2 * n], list(outs[2 * n:3 * n]), outs[3 * n]


def _join_wait(sems, fulls, afters, *, name):
    n = len(fulls)
    hbm = pl.BlockSpec(memory_space=pltpu.HBM)
    sem = pl.BlockSpec(memory_space=pltpu.SEMAPHORE)

    def body(*refs):
        sem_refs = refs[n:3 * n]
        for cp in _join_copies(refs[:n], sem_refs[:n], sem_refs[n:]):
            cp.wait_send()
            cp.wait_recv()

    outs = pl.pallas_call(
        body, name=name, out_shape=tuple(pltpu.HBM(f.shape, f.dtype) for f in fulls),
        in_specs=(hbm,) * n + (sem,) * (2 * n) + (pl.BlockSpec(memory_space=pl.ANY),) * len(afters),
        out_specs=(hbm,) * n, input_output_aliases={i: i for i in range(n)},
        compiler_params=pltpu.CompilerParams(has_side_effects=pltpu.SideEffectType.DATAFLOW_SIDE_EFFECTING),
    )(*fulls, *sems, *afters)
    return list(outs)


def _gather_ici_copies(buf_refs, send_sems, recv_sems):
    x, y, c = _mesh_pos()
    me_chip = 2 * x + y
    cps = []
    for w, buf in enumerate(buf_refs):
        h = buf.shape[1] // 2
        ref = buf.at[me_chip, pl.ds(c * h, h), :]
        for j, chip in enumerate(_other_chips(x, y)):
            cps.append(pltpu.make_async_remote_copy(
                src_ref=ref, dst_ref=ref, send_sem=send_sems[3 * w + j], recv_sem=recv_sems[3 * w + j],
                device_id=(*chip, c), device_id_type=MESH))
    return cps


def _gather_send(bufs, *, name):
    n = len(bufs)
    hbm = pl.BlockSpec(memory_space=pltpu.HBM)
    sem = pl.BlockSpec(memory_space=pltpu.SEMAPHORE)

    def body(*refs):
        sems = refs[n:7 * n]
        for cp in _gather_ici_copies(refs[:n], sems[:3 * n], sems[3 * n:]):
            cp.start()

    outs = pl.pallas_call(
        body, name=name,
        out_shape=(pltpu.SemaphoreType.DMA(()),) * (6 * n) + tuple(pltpu.HBM(b.shape, b.dtype) for b in bufs),
        in_specs=(hbm,) * n, out_specs=(sem,) * (6 * n) + (hbm,) * n,
        input_output_aliases={i: 6 * n + i for i in range(n)},
        compiler_params=pltpu.CompilerParams(has_side_effects=pltpu.SideEffectType.DATAFLOW_SIDE_EFFECTING),
    )(*[pltpu.with_memory_space_constraint(b, pltpu.HBM) for b in bufs])
    send_sems, recv_sems = outs[:3 * n], outs[3 * n:6 * n]
    per_buf = [tuple(send_sems[3 * w:3 * w + 3]) + tuple(recv_sems[3 * w:3 * w + 3]) for w in range(n)]
    return per_buf, list(outs[6 * n:])


def _gather_wait(sems, bufs, afters, *, name):
    n = len(bufs)
    hbm = pl.BlockSpec(memory_space=pltpu.HBM)
    sem = pl.BlockSpec(memory_space=pltpu.SEMAPHORE)
    flat = [s for six in sems for s in six[:3]] + [s for six in sems for s in six[3:]]

    def body(*refs):
        sem_refs = refs[n:7 * n]
        for cp in _gather_ici_copies(refs[:n], sem_refs[:3 * n], sem_refs[3 * n:]):
            cp.wait_send()
            cp.wait_recv()

    outs = pl.pallas_call(
        body, name=name, out_shape=tuple(pltpu.HBM(b.shape, b.dtype) for b in bufs),
        in_specs=(hbm,) * n + (sem,) * (6 * n) + (pl.BlockSpec(memory_space=pl.ANY),) * len(afters),
        out_specs=(hbm,) * n, input_output_aliases={i: i for i in range(n)},
        compiler_params=pltpu.CompilerParams(has_side_effects=pltpu.SideEffectType.DATAFLOW_SIDE_EFFECTING),
    )(*bufs, *flat, *afters)
    return list(outs)


def _forward_abs_carry(bufs):
    n = len(bufs)
    sem = pltpu.SemaphoreType.DMA((3 * n,))

    def copies(outs, sems):
        send_sems, recv_sems = sems
        x, y, c = _mesh_pos()
        sends, recvs = [], []
        for w in range(n):
            h = bufs[w].shape[1] // 2
            for j, chip in enumerate(_other_chips(x, y)):
                slot = 2 * chip[0] + chip[1]
                mine = outs[w].at[slot, pl.ds(c * h, h), :]
                other = outs[w].at[slot, pl.ds((1 - c) * h, h), :]
                sends.append(pltpu.make_async_remote_copy(
                    src_ref=mine, dst_ref=mine, send_sem=send_sems.at[3 * w + j], recv_sem=recv_sems.at[3 * w + j],
                    device_id=(x, y, 1 - c), device_id_type=MESH))
                recvs.append(pltpu.make_async_remote_copy(
                    src_ref=other, dst_ref=other, send_sem=send_sems.at[3 * w + j], recv_sem=recv_sems.at[3 * w + j],
                    device_id=(x, y, c), device_id_type=MESH))
        return sends, recvs

    def start(ins, outs, sems):
        for cp in copies(outs, sems)[0]:
            cp.start()

    def finish(ins, outs, sems):
        sends, recvs = copies(outs, sems)
        for cp in recvs:
            cp.wait_recv()
        for cp in sends:
            cp.wait_send()

    shapes = [jax.ShapeDtypeStruct(b.shape, b.dtype) for b in bufs]
    return _Carry(bufs, shapes, {i: i for i in range(n)}, [sem, sem], start, finish)


def _forward_carry(land):
    n = land.shape[0] - 1
    h = land.shape[1] // 2
    sem = pltpu.SemaphoreType.DMA((n,))

    def copies(outs, sems):
        send_sems, recv_sems = sems
        x, y, c = _mesh_pos()
        sends, recvs = [], []
        for j in range(n):
            mine = outs[0].at[1 + j, pl.ds(c * h, h), :]
            other = outs[0].at[1 + j, pl.ds((1 - c) * h, h), :]
            sends.append(pltpu.make_async_remote_copy(
                src_ref=mine, dst_ref=mine, send_sem=send_sems.at[j], recv_sem=recv_sems.at[j],
                device_id=(x, y, 1 - c), device_id_type=MESH))
            recvs.append(pltpu.make_async_remote_copy(
                src_ref=other, dst_ref=other, send_sem=send_sems.at[j], recv_sem=recv_sems.at[j],
                device_id=(x, y, c), device_id_type=MESH))
        return sends, recvs

    def start(ins, outs, sems):
        for cp in copies(outs, sems)[0]:
            cp.start()

    def finish(ins, outs, sems):
        sends, recvs = copies(outs, sems)
        for cp in recvs:
            cp.wait_recv()
        for cp in sends:
            cp.wait_send()

    return _Carry([land], [jax.ShapeDtypeStruct(land.shape, land.dtype)], {0: 0}, [sem, sem], start, finish)


def _swap_carry(dws):
    n = len(dws)
    sem = pltpu.SemaphoreType.DMA((n,))

    def copies(ins, outs, sems):
        send_sems, recv_sems = sems
        x, y, c = _mesh_pos()
        cps = []
        for w in range(n):
            h = dws[w].shape[1] // 2
            cps.append(pltpu.make_async_remote_copy(
                src_ref=ins[w].at[:, pl.ds((1 - c) * h, h), :], dst_ref=outs[w],
                send_sem=send_sems.at[w], recv_sem=recv_sems.at[w],
                device_id=(x, y, 1 - c), device_id_type=MESH))
        return cps

    def start(ins, outs, sems):
        for cp in copies(ins, outs, sems):
            cp.start()

    def finish(ins, outs, sems):
        for cp in copies(ins, outs, sems):
            cp.wait()

    shapes = [jax.ShapeDtypeStruct((s.shape[0], s.shape[1] // 2, s.shape[2]), s.dtype) for s in dws]
    return _Carry(dws, shapes, {}, [sem, sem], start, finish)


def _merge_carries(carries):
    if len(carries) == 1:
        return carries[0]
    inputs, out_shapes, sem_shapes, aliases, spans = [], [], [], {}, []
    for cy in carries:
        i0, o0, s0 = len(inputs), len(out_shapes), len(sem_shapes)
        aliases.update({i0 + i: o0 + o for i, o in cy.aliases.items()})
        inputs += cy.inputs
        out_shapes += cy.out_shapes
        sem_shapes += cy.sem_shapes
        spans.append((slice(i0, len(inputs)), slice(o0, len(out_shapes)), slice(s0, len(sem_shapes))))

    def start(ins, outs, sems):
        for cy, (si, so, ss) in zip(carries, spans):
            cy.start(ins[si], outs[so], sems[ss])

    def finish(ins, outs, sems):
        for cy, (si, so, ss) in zip(carries, spans):
            cy.finish(ins[si], outs[so], sems[ss])

    return _Carry(inputs, out_shapes, aliases, sem_shapes, start, finish)


class _MeshComm:
    FORWARD_AT = {
        "conv_fwd": ["w_conv_out", "w_glu_a", "w_glu_b", "w_out"],
        "mm_out": ["w_ff1"],
        "mm_ff1": ["w_ff2"],
    }
    SWAP_AT = {
        "mm_d_ff2": ["w_ff2"],
        "mm_d_ff1": ["w_ff1"],
        "conv_bwd": ["w_out", "w_glu_a", "w_glu_b", "w_conv_out"],
    }
    EARLY_AT = "mm_dw_in"

    def __init__(self, shards, pos, chip, my_c):
        self.pos = pos
        self.chip = chip
        self.my_c = my_c
        self.shards = shards
        self.w_in_own, self.w_in_rel = _cast_bf16(shards["w_in"], name="cast_w_in")
        self.raw = {}
        self.flights = []
        self.halves = {}
        self.pending = {}

    def weight(self, name):
        g = self.bufs[name]
        return g.reshape(g.shape[0] * g.shape[1], g.shape[2]) if name in ROW_SHARDED else g

    def _slot_ids(self):
        x, y, _ = self.pos
        ids = [2 * x + y] + [2 * cx + cy for cx, cy in _other_chips(x, y)]
        return jnp.stack(ids).astype(jnp.int32)

    def start_w_in(self, after):
        *self.w_in_flight, token = _w_in_send(self.w_in_own, self.w_in_rel, after)
        order = [n for names in self.FORWARD_AT.values() for n in names]
        casts = [_cast_into_slot(self.shards[n], self.chip, token, name="cast_" + n) for n in order]
        sems, bufs = _gather_send(casts, name="gather_send")
        self.bufs = dict(zip(order, bufs))
        self.gather_sems = dict(zip(order, sems))
        return token

    def mm_in(self, u, afters):
        ids = self._slot_ids()
        sems, own, land = self.w_in_flight
        proj = _mm_slots(u, own[None], ids[0:1], None, name="mm_in_own")
        own, land = _w_in_wait(sems, own, land, [proj] + list(self.bufs.values()) + list(afters))
        land, = _run_carry(_forward_carry(land), name="forward_w_in")
        proj = _mm_slots(u, land, ids[1:4], proj, name="mm_in_rest", first=1)
        self.w_in_rel = land
        return proj

    def _add_and_send(self, names, landed, site, through):
        parts = [_add_half(self.raw.pop(n), l1, self.my_c, name="add_half_" + n) for n, l1 in zip(names, landed)]
        sems, parts, lands, through = _exchange_send(parts, through, name="exchange_send_" + site)
        self.flights.append((names, sems, parts, lands))
        return through

    def mm_d_in(self, dproj):
        landed = _run_carry(_swap_carry([self.raw["w_in"]]), name="swap_halves_w_in")
        dproj = self._add_and_send(["w_in"], landed, "w_in", dproj)
        return _mm(dproj, self.w_in_rel, mode="nt", out_dtype=F32, name="mm_d_in", a_slots=self._slot_ids())

    def early_grads(self, early):
        self.early = early

    def carry(self, site, args=()):
        jobs = []
        if site in self.FORWARD_AT:
            names = self.FORWARD_AT[site]
            landed = _gather_wait([self.gather_sems.pop(n) for n in names], [self.bufs[n] for n in names],
                                  [args[0]], name="gather_wait_" + site)
            jobs.append(("forward", names, _forward_abs_carry(landed)))
        if site == self.EARLY_AT:
            flat, self.early_offs = _pack(list(self.early.values()))
            jobs.append(("early", None, _gather_rows_carry(flat.reshape(-1, PACK_COLS))))
        if site in self.SWAP_AT:
            names = self.SWAP_AT[site]
            jobs.append(("swap", names, _swap_carry([self.raw[n] for n in names])))
        if not jobs:
            return None
        self.pending[site] = jobs
        return _merge_carries([job[2] for job in jobs])

    def done(self, site, carried, out=None):
        pos = 0
        for kind, items, carry in self.pending.pop(site):
            outs = carried[pos:pos + len(carry.out_shapes)]
            pos += len(carry.out_shapes)
            if kind == "early":
                self.early_all = outs[0]
            elif kind == "forward":
                self.bufs.update(zip(items, outs))
            elif isinstance(out, (list, tuple)):
                out = [self._add_and_send(items, outs, site, out[0])] + list(out[1:])
            else:
                out = self._add_and_send(items, outs, site, out)
        return out

    def grad(self, name, dw):
        if name in ROW_SHARDED:
            dw = dw.reshape(N_CHIPS, dw.shape[0] // N_CHIPS, dw.shape[1])
        self.raw[name] = dw

    def join_start(self, names, afters):
        for i, (group, sems, parts, lands) in enumerate(self.flights):
            parts, lands = _exchange_wait(sems, parts, lands, afters, name="exchange_wait_%d" % i)
            for n, part, land in zip(group, parts, lands):
                self.halves[n] = _sum_into_half(part, land, self.chip, self.my_c, name="sum_chips_" + n)
        self.flights = []
        sems, fulls, token = _join_send([self.halves.pop(n) for n in names], name="join_send")
        self.join_flight = (names, sems, fulls)
        return token

    def join_finish(self, afters):
        names, sems, fulls = self.join_flight
        return dict(zip(names, _join_wait(sems, fulls, afters, name="join_wait")))


def _local_step(x, target, mod, small, comm):
    rows, d = x.shape
    cw = d // 2
    shift1, scale1, gate1, shift2, scale2, gate2 = mod
    _, _, bbr, bbi = small["s5_disc"]
    b_in, c_out, b_out, c_in, mults = _s5_operands(*small["s5_loglam"], bbr, bbi, small["c_re"], small["c_im"])
    wt = comm.weight

    def riding(site, fn, *args, **kwargs):
        carry = comm.carry(site, args)
        if carry is None:
            return fn(*args, **kwargs)
        out, carried = fn(*args, carry=carry, **kwargs)
        return comm.done(site, carried, out)

    u = _norm_mod(x, small["norm1_g"], scale1, shift1, name="norm1_fwd")
    proj = comm.mm_in(u, [*b_in, *c_out, *b_out, *c_in, *mults])
    sl, cv = riding("conv_fwd", _conv_fwd, proj, small["w_dw"], small["b_dw"], small["ln_g"], small["ln_b"], cw=cw)
    y_conv = _mm(sl, wt("w_conv_out"), mode="nn", out_dtype=BF16, name="mm_conv_out")
    yg, st_re, st_im = riding("s5_fwd", _s5_fwd, proj, small["d_skip"], b_in, c_out, mults, col0=2 * cw // LANES)
    ya = riding("mm_glu_a", _mm, yg, wt("w_glu_a"), mode="nn", out_dtype=BF16, name="mm_glu_a")
    yb = riding("mm_glu_b", _mm, yg, wt("w_glu_b"), mode="nn", out_dtype=BF16, name="mm_glu_b")
    merged = _merge_fwd(proj, y_conv, ya, yb, cw=cw)
    mo = riding("mm_out", _mm, merged, wt("w_out"), mode="nn", out_dtype=BF16, name="mm_out")
    h1, z = _res_norm(x, mo, gate1, small["norm2_g"], scale2, shift2)
    f1 = riding("mm_ff1", _mm, z, wt("w_ff1"), mode="nn", out_dtype=BF16, name="mm_ff1")
    ff = _mm(f1, wt("w_ff2"), mode="nn", out_dtype=BF16, name="mm_ff2", a_fn=_relu2_bf16)
    dh2, dff, loss, d_final_g, d_gate2 = _final_fwd_bwd(h1, ff, gate2, small["final_g"], target)

    comm.grad("w_ff2", _mm(f1, dff, mode="tn", out_dtype=BF16, name="mm_dw_ff2", a_fn=_relu2_bf16))
    df1 = riding("mm_d_ff2", _mm, dff, wt("w_ff2"), mode="nt", out_dtype=BF16, name="mm_d_ff2", extra=f1,
                 epi=lambda acc, f: acc * (2.0 * jnp.maximum(f.astype(F32), 0.0)))
    comm.grad("w_ff1", riding("mm_dw_ff1", _mm, z, df1, mode="tn", out_dtype=BF16, name="mm_dw_ff1",
                              out_gathered=True))
    dz = riding("mm_d_ff1", _mm, df1, wt("w_ff1"), mode="nt", out_dtype=F32, name="mm_d_ff1")
    dh1, d_shift2, d_scale2, d_norm2_g, dmo, d_gate1 = riding(
        "norm2_bwd", _norm_mod_bwd, dz, h1, dh2, small["norm2_g"], scale2, gate1, mo, name="norm2_bwd")
    comm.grad("w_out", riding("mm_dw_out", _mm, merged, dmo, mode="tn", out_dtype=BF16, name="mm_dw_out"))
    dmerged = riding("mm_d_out", _mm, dmo, wt("w_out"), mode="nt", out_dtype=BF16, name="mm_d_out")
    dproj, dy_conv, dya, dyb = riding("merge_bwd", _merge_bwd, dmerged, proj, y_conv, ya, yb, cw=cw)
    comm.grad("w_glu_a", _mm(yg, dya, mode="tn", out_dtype=BF16, name="mm_dw_glu_a", out_gathered=True))
    comm.grad("w_glu_b", _mm(yg, dyb, mode="tn", out_dtype=BF16, name="mm_dw_glu_b", out_gathered=True))
    dyg_a = _mm(dya, wt("w_glu_a"), mode="nt", out_dtype=F32, name="mm_d_glu_a")
    dyg = _mm(dyb, wt("w_glu_b"), mode="nt", out_dtype=F32, name="mm_d_glu_b", extra=dyg_a,
              epi=lambda acc, e: acc + e)
    comm.grad("w_conv_out", _mm(sl, dy_conv, mode="tn", out_dtype=BF16, name="mm_dw_conv_out", out_gathered=True))
    dsl = _mm(dy_conv, wt("w_conv_out"), mode="nt", out_dtype=F32, name="mm_d_conv_out")
    dcv, d_ln_g, d_ln_b = _ln_bwd(dsl, cv, small["ln_g"], small["ln_b"])
    dproj, d_w_dw, d_b_dw = riding("conv_bwd", _conv_bwd, dcv, proj, small["w_dw"], dproj, cw=cw)
    dproj, d_d_skip, dbr, dbi, dcr, dci, dlr, dli = riding(
        "s5_bwd", _s5_bwd, proj, dyg, small["d_skip"], (st_re, st_im), c_out, b_out, c_in, mults, dproj,
        col0=2 * cw // LANES)
    sw = lambda m: jnp.swapaxes(m, 1, 2)
    early = {
        "dmod_tail": jnp.concatenate([d_gate1, d_shift2, d_scale2, d_gate2], axis=1), "loss": loss[:, 0:1],
        "w_dw": d_w_dw, "b_dw": d_b_dw, "ln_g": d_ln_g, "ln_b": d_ln_b,
        "lam_re": dlr.reshape(-1, SSM_STATE), "lam_im": dli.reshape(-1, SSM_STATE),
        "bb_re": sw(_block_diag_extract(dbr, SSM_GROUP, SSM_STATE)),
        "bb_im": sw(_block_diag_extract(dbi, SSM_GROUP, SSM_STATE)),
        "c_re": sw(_block_diag_extract(dcr, SSM_STATE, SSM_GROUP)),
        "c_im": sw(_block_diag_extract(dci, SSM_STATE, SSM_GROUP)),
        "d_skip": d_d_skip, "norm2_g": d_norm2_g, "final_g": d_final_g,
    }
    comm.early_grads(early)
    comm.grad("w_in", riding("mm_dw_in", _mm, u, dproj, mode="tn", out_dtype=BF16, name="mm_dw_in",
                             out_gathered=True))
    du = comm.mm_d_in(dproj)
    grad_x, d_shift1, d_scale1, d_norm1_g = riding(
        "norm1_bwd", _norm_mod_bwd, du, x, dh1, small["norm1_g"], scale1, None, None, name="norm1_bwd")
    late ={"dmod_head": jnp.concatenate([d_shift1, d_scale1], axis=1), "norm1_g": d_norm1_g}
    return grad_x, early, late


WEIGHT_NAMES = ["w_ada", "b_ada", "norm1_g", "w_in", "w_dw", "b_dw", "ln_g", "ln_b", "w_conv_out", "a_re", "a_im",
                "log_dt", "b_re", "b_im", "c_re", "c_im", "d_skip", "w_glu_a", "w_glu_b", "w_out", "norm2_g",
                "w_ff1", "w_ff2", "final_g"]
BIG_NAMES = ["w_in", "w_conv_out", "w_glu_a", "w_glu_b", "w_out", "w_ff1", "w_ff2"]
ROW_SHARDED = ("w_out", "w_ff2")
PACK_COLS = 1024
PACK_TILE = SUBLANES * PACK_COLS


def _pack(arrays):
    flats = [a.reshape(-1) for a in arrays]
    offs = []
    total = 0
    for f in flats:
        offs.append(total)
        total += f.shape[0]
    pad = (-total) % PACK_TILE
    if pad:
        flats.append(jnp.zeros((pad,), F32))
    return jnp.concatenate(flats), offs


def _unpack(flat, offs, like):
    return [flat[o:o + a.size].reshape(a.shape) for o, a in zip(offs, like)]


def kernel(x, c, w_ada, b_ada, norm1_g, w_in, w_dw, b_dw, ln_g, ln_b, w_conv_out, a_re, a_im, log_dt, b_re, b_im, c_re, c_im, d_skip, w_glu_a, w_glu_b, w_out, norm2_g, w_ff1, w_ff2, final_g, loss_target, m_w_ada, m_b_ada, m_norm1_g, m_w_in, m_w_dw, m_b_dw, m_ln_g, m_ln_b, m_w_conv_out, m_a_re, m_a_im, m_log_dt, m_b_re, m_b_im, m_c_re, m_c_im, m_d_skip, m_w_glu_a, m_w_glu_b, m_w_out, m_norm2_g, m_w_ff1, m_w_ff2, m_final_g, v_w_ada, v_b_ada, v_norm1_g, v_w_in, v_w_dw, v_b_dw, v_ln_g, v_ln_b, v_w_conv_out, v_a_re, v_a_im, v_log_dt, v_b_re, v_b_im, v_c_re, v_c_im, v_d_skip, v_w_glu_a, v_w_glu_b, v_w_out, v_norm2_g, v_w_ff1, v_w_ff2, v_final_g):
    given = dict(locals())
    w = {n: given[n] for n in WEIGHT_NAMES}
    m = {n: given["m_" + n] for n in WEIGHT_NAMES}
    v = {n: given["v_" + n] for n in WEIGHT_NAMES}
    d = x.shape[2]
    xi, yi, ci = _mesh_pos()
    chip = 2 * xi + yi
    dev = 4 * xi + 2 * yi + ci
    my_c = jnp.reshape(ci, (1,)).astype(jnp.int32)
    chip_arr = jnp.reshape(chip, (1,)).astype(jnp.int32)

    comm = _MeshComm({n: w[n][0] for n in BIG_NAMES}, (xi, yi, ci), chip_arr, my_c)

    ndw = w_dw.shape[2]
    assert d // SUBLANES == ndw
    first = jnp.concatenate([c.reshape(SUBLANES, ndw), jnp.pad(w_dw[0], ((0, HALO - CONV_KERNEL), (0, 0)))])
    first_all = _gather_small(first, name="gather_c_w_dw").reshape(N_DEV, SUBLANES + HALO, ndw)
    c_all = first_all[:, :SUBLANES].reshape(N_DEV, d)
    taps = first_all.reshape(N_CHIPS, 2, SUBLANES + HALO, ndw)[:, 0, SUBLANES:SUBLANES + CONV_KERNEL]
    w_dw_full = jnp.moveaxis(taps, 0, 1).reshape(CONV_KERNEL, N_CHIPS * ndw)

    nmod = w_ada.shape[2]
    b_cols = lax.dynamic_slice(b_ada, (0, chip * nmod), (1, nmod))
    mod_part = _ada_fwd(c_all, w_ada[0], b_cols)
    mod_all = _gather_small(mod_part, name="gather_mod").reshape(N_CHIPS, 2, N_DEV, nmod)[:, 0]
    mod_full = jnp.moveaxis(mod_all, 0, 1).reshape(N_DEV, N_CHIPS * nmod)
    mod_row = lax.dynamic_slice(mod_full, (dev, 0), (1, N_CHIPS * nmod))
    mod = [mod_row[:, i * d:(i + 1) * d] for i in range(6)]

    token = comm.start_w_in(mod_row)
    log_dt_0 = log_dt[0] + token[0, 0]

    disc_in = (a_re[0], a_im[0], log_dt_0, b_re[0], b_im[0])
    disc, disc_vjp = jax.vjp(_s5_discretise, *disc_in)
    dt = jnp.exp(log_dt_0)[:, None]
    small = {"norm1_g": norm1_g, "w_dw": w_dw_full, "b_dw": b_dw, "ln_g": ln_g, "ln_b": ln_b,
             "c_re": c_re[0], "c_im": c_im[0], "d_skip": d_skip, "norm2_g": norm2_g,
             "final_g": final_g[None, :], "s5_disc": disc, "s5_loglam": (a_re[0] * dt, a_im[0] * dt)}

    grad_x, early, late = _local_step(x[0], loss_target[0], mod, small, comm)
    grads = {}

    early_all = comm.early_all.reshape(N_DEV, -1, PACK_COLS)
    early_sum = _sum_leading(early_all, name="sum_small_grads").reshape(-1)
    summed = dict(zip(early, _unpack(early_sum, comm.early_offs, list(early.values()))))
    flat, late_offs = _pack(list(late.values()))
    late_all = _gather_small(flat.reshape(-1, PACK_COLS), name="gather_late_grads").reshape(N_DEV, -1, PACK_COLS)
    late_sum = _sum_leading(late_all, name="sum_late_grads").reshape(-1)
    summed.update(zip(late, _unpack(late_sum, late_offs, list(late.values()))))
    head = late_all[:, :2 * d // PACK_COLS].reshape(N_DEV, 2 * d)
    tail = early_all[:, :4 * d // PACK_COLS].reshape(N_DEV, 4 * d)
    dmod_all = jnp.concatenate([head, tail], axis=1)

    grads["w_ada"] = _ada_bwd(c_all, lax.dynamic_slice(dmod_all, (0, chip * nmod), (N_DEV, nmod)))
    grads["b_ada"] = _sum_leading(dmod_all.reshape(N_DEV, SUBLANES, 6 * d // SUBLANES),
                                  name="sum_b_ada").reshape(1, 6 * d)
    da_re, da_im, dlog_dt, db_re, db_im = disc_vjp(
        (summed["lam_re"], summed["lam_im"], summed["bb_re"], summed["bb_im"]))
    grads.update({
        "norm1_g": summed["norm1_g"], "w_dw": lax.dynamic_slice(summed["w_dw"], (0, chip * ndw), (CONV_KERNEL, ndw)),
        "b_dw": summed["b_dw"], "ln_g": summed["ln_g"], "ln_b": summed["ln_b"],
        "a_re": da_re, "a_im": da_im, "log_dt": dlog_dt, "b_re": db_re, "b_im": db_im,
        "c_re": summed["c_re"], "c_im": summed["c_im"], "d_skip": summed["d_skip"],
        "norm2_g": summed["norm2_g"], "final_g": summed["final_g"],
    })

    delta, new_m, new_v = {}, {}, {}

    def adam_big(n, after=None):
        shp = w[n].shape
        two_d = lambda a: a.reshape(shp[1], shp[2])
        res = _adamw(two_d(w[n]), two_d(grads[n]), two_d(m[n]), two_d(v[n]), name="adamw_" + n, after=after,
                     with_grad=n in BIG_NAMES)
        delta[n], new_m[n], new_v[n] = [r.reshape(shp) for r in res[:3]]
        if n in BIG_NAMES:
            grads[n] = res[3]

    token = comm.join_start(BIG_NAMES, [late_all])
    adam_big("w_ada", token)
    grads.update(comm.join_finish([delta["w_ada"]]))
    for n in BIG_NAMES:
        adam_big(n)
    grads = {n: grads[n].reshape(w[n].shape) for n in WEIGHT_NAMES}
    rest = [n for n in WEIGHT_NAMES if n not in delta]
    as_2d = lambda a: a.reshape(1, -1) if a.ndim == 1 else a
    outs = _adamw_many(*[[as_2d(src[n]) for n in rest] for src in (w, grads, m, v)], name="adamw_small")
    for dst, arrays in zip((delta, new_m, new_v), outs):
        for n, a in zip(rest, arrays):
            dst[n] = a.reshape(w[n].shape)

    return (summed["loss"].reshape(()), grad_x[None], *[grads[n] for n in WEIGHT_NAMES],
            *[delta[n] for n in WEIGHT_NAMES], *[new_m[n] for n in WEIGHT_NAMES],
            *[new_v[n] for n in WEIGHT_NAMES])
```

```python
import math

import jax
import jax.numpy as jnp
from jax import lax
from jax.experimental import pallas as pl
from jax.experimental.pallas import tpu as pltpu

F32 = jnp.float32
BF16 = jnp.bfloat16
EPS = 1e-6
CONV_KERNEL = 31
SSM_GROUP = 16
SSM_STATE = 64
ADAM_LR = 0.001
ADAM_B1 = 0.9
ADAM_B2 = 0.999
ADAM_EPS = 1e-08
ADAM_WD = 0.01
ADAM_STEP = 10

N_CHIPS = 4
N_DEV = 8
VMEM_LIMIT_BYTES = 56 * 1024 * 1024
LANES = 128
SUBLANES = 8
HALO = 32
GROUPS_PER_BLOCK = LANES // SSM_GROUP
STATE_LANES = GROUPS_PER_BLOCK * SSM_STATE
MESH = pl.DeviceIdType.MESH


def _cparams(sem):
    return pltpu.CompilerParams(dimension_semantics=sem, vmem_limit_bytes=VMEM_LIMIT_BYTES)


def _pick(n, pref, mult=LANES):
    if n <= pref:
        return n
    best = None
    for d in range(mult, pref + 1, mult):
        if n % d == 0:
            best = d
    assert best is not None, (n, pref)
    return best


def _sigmoid(v):
    return 1.0 / (1.0 + jnp.exp(-v))


def _gelu_parts(v):
    k0 = math.sqrt(2.0 / math.pi)
    inner = k0 * (v + 0.044715 * v * v * v)
    t = jnp.tanh(inner)
    return k0, t


def _gelu(v):
    _, t = _gelu_parts(v)
    return 0.5 * v * (1.0 + t)


def _gelu_grad(v):
    k0, t = _gelu_parts(v)
    return 0.5 * (1.0 + t) + 0.5 * v * (1.0 - t * t) * k0 * (1.0 + 3.0 * 0.044715 * v * v)


def _relu2_bf16(a):
    t = jnp.maximum(a.astype(F32), 0.0)
    return (t * t).astype(BF16)


class _Carry:
    def __init__(self, inputs, out_shapes, aliases, sem_shapes, start, finish):
        self.inputs = list(inputs)
        self.out_shapes = list(out_shapes)
        self.aliases = dict(aliases)
        self.sem_shapes = list(sem_shapes)
        self.start = start
        self.finish = finish


def _call(body, *, grid, in_specs, out_specs, out_shape, scratch_shapes, semantics, name, args, carry=None,
          prefetch=(), aliases=None):
    n_in, n_out, n_scr, n_pf = len(in_specs), len(out_specs), len(scratch_shapes), len(prefetch)
    own_aliases = {n_pf + i: o for i, o in (aliases or {}).items()}
    if carry is None:
        gs = pltpu.PrefetchScalarGridSpec(
            num_scalar_prefetch=n_pf, grid=grid, in_specs=in_specs, out_specs=out_specs,
            scratch_shapes=scratch_shapes)
        outs = pl.pallas_call(
            body, grid_spec=gs, out_shape=out_shape, input_output_aliases=own_aliases,
            compiler_params=_cparams(semantics), name=name)(*prefetch, *args)
        return list(outs), []
    ci, co = len(carry.inputs), len(carry.out_shapes)

    def wrapped(*refs):
        pf, refs = refs[:n_pf], refs[n_pf:]
        ins, cins = refs[:n_in], refs[n_in:n_in + ci]
        p = n_in + ci
        outs, couts = refs[p:p + n_out], refs[p + n_out:p + n_out + co]
        p += n_out + co
        scr, csems = refs[p:p + n_scr], refs[p + n_scr:]
        first = pl.program_id(0) == 0
        last = pl.program_id(0) == grid[0] - 1
        for ax in range(1, len(grid)):
            first = jnp.logical_and(first, pl.program_id(ax) == 0)
            last = jnp.logical_and(last, pl.program_id(ax) == grid[ax] - 1)

        @pl.when(first)
        def _():
            carry.start(cins, couts, csems)

        body(*pf, *ins, *outs, *scr)

        @pl.when(last)
        def _():
            carry.finish(cins, couts, csems)

    any_spec = pl.BlockSpec(memory_space=pl.ANY)
    gs = pltpu.PrefetchScalarGridSpec(
        num_scalar_prefetch=n_pf, grid=grid, in_specs=list(in_specs) + [any_spec] * ci,
        out_specs=list(out_specs) + [any_spec] * co, scratch_shapes=list(scratch_shapes) + carry.sem_shapes)
    all_aliases = dict(own_aliases)
    all_aliases.update({n_pf + n_in + i: n_out + o for i, o in carry.aliases.items()})
    outs = pl.pallas_call(
        wrapped, grid_spec=gs, out_shape=list(out_shape) + carry.out_shapes, input_output_aliases=all_aliases,
        compiler_params=_cparams(("arbitrary",) * len(grid)), name=name)(*prefetch, *args, *carry.inputs)
    return list(outs[:n_out]), list(outs[n_out:])


def _run_carry(carry, *, name):
    ci = len(carry.inputs)

    def body(*refs):
        cins, couts, csems = refs[:ci], refs[ci:ci + len(carry.out_shapes)], refs[ci + len(carry.out_shapes):]
        carry.start(cins, couts, csems)
        carry.finish(cins, couts, csems)

    any_spec = pl.BlockSpec(memory_space=pl.ANY)
    outs = pl.pallas_call(
        body, in_specs=[any_spec] * ci, out_specs=[any_spec] * len(carry.out_shapes), out_shape=carry.out_shapes,
        scratch_shapes=carry.sem_shapes, input_output_aliases=carry.aliases, name=name)(*carry.inputs)
    return list(outs)


def _mm(a, b, *, mode, out_dtype, name, out_gathered=False, a_fn=None, epi=None, extra=None,
        bm_pref=1024, bn_pref=1024, bk_pref=2048, carry=None, a_slots=None):
    gathered = (b.ndim == 3)
    if mode == "nn":
        m, kdim = a.shape
        ns = b.shape[-1]
        n = ns * (N_CHIPS if gathered else 1)
        bm, bn, bk = _pick(m, bm_pref), _pick(ns, bn_pref), _pick(kdim, bk_pref)
        npb = ns // bn
        grid = (m // bm, n // bn, kdim // bk)
        a_spec = pl.BlockSpec((bm, bk), lambda i, j, k: (i, k))
        if gathered:
            b_spec = pl.BlockSpec((None, bk, bn), lambda i, j, k: (j // npb, k, j % npb))
        else:
            b_spec = pl.BlockSpec((bk, bn), lambda i, j, k: (k, j))
        o_spec = pl.BlockSpec((bm, bn), lambda i, j, k: (i, j))
        e_spec = pl.BlockSpec((bm, bn), lambda i, j, k: (i, j))
        out_shape = (m, n)
        acc_shape = (bm, bn)
        dims = (((1,), (0,)), ((), ()))
    elif mode == "nt":
        m = a.shape[0]
        kdim, ns = b.shape[-2], b.shape[-1]
        n = ns * (N_CHIPS if gathered else 1)
        assert a.shape[1] == n
        bm, bko, bnr = _pick(m, bm_pref), _pick(kdim, bn_pref), _pick(ns, bk_pref)
        npb = ns // bnr
        grid = (m // bm, kdim // bko, n // bnr)
        a_spec = pl.BlockSpec((bm, bnr), lambda i, j, k: (i, k))
        if gathered:
            b_spec = pl.BlockSpec((None, bko, bnr), lambda i, j, k: (k // npb, j, k % npb))
        else:
            b_spec = pl.BlockSpec((bko, bnr), lambda i, j, k: (j, k))
        o_spec = pl.BlockSpec((bm, bko), lambda i, j, k: (i, j))
        e_spec = pl.BlockSpec((bm, bko), lambda i, j, k: (i, j))
        if a_slots is not None:
            assert gathered and extra is None
            a_spec = pl.BlockSpec((bm, bnr), lambda i, j, k, s_ref: (i, s_ref[k // npb] * npb + k % npb))
            b_spec = pl.BlockSpec((None, bko, bnr), lambda i, j, k, s_ref: (k // npb, j, k % npb))
            o_spec = pl.BlockSpec((bm, bko), lambda i, j, k, s_ref: (i, j))
        out_shape = (m, kdim)
        acc_shape = (bm, bko)
        dims = (((1,), (1,)), ((), ()))
    else:
        m, kdim = a.shape
        n = b.shape[1]
        ns = n // N_CHIPS if out_gathered else n
        bmr, bko, bn = _pick(m, bk_pref), _pick(kdim, bm_pref), _pick(ns, bn_pref)
        npb = ns // bn
        grid = (kdim // bko, n // bn, m // bmr)
        a_spec = pl.BlockSpec((bmr, bko), lambda i, j, k: (k, i))
        b_spec = pl.BlockSpec((bmr, bn), lambda i, j, k: (k, j))
        if out_gathered:
            o_spec = pl.BlockSpec((None, bko, bn), lambda i, j, k: (j // npb, i, j % npb))
            out_shape = (N_CHIPS, kdim, ns)
        else:
            o_spec = pl.BlockSpec((bko, bn), lambda i, j, k: (i, j))
            out_shape = (kdim, n)
        e_spec = None
        acc_shape = (bko, bn)
        dims = (((0,), (0,)), ((), ()))
    nk = grid[2]

    def body(*refs):
        if a_slots is not None:
            refs = refs[1:]
        if extra is not None:
            a_ref, b_ref, e_ref, o_ref, acc = refs
        else:
            a_ref, b_ref, o_ref, acc = refs
            e_ref = None
        k = pl.program_id(2)
        av = a_ref[...]
        if a_fn is not None:
            av = a_fn(av)
        part = lax.dot_general(av, b_ref[...], dims, preferred_element_type=F32)

        def finish(r):
            if epi is not None:
                r = epi(r, e_ref[...])
            o_ref[...] = r.astype(o_ref.dtype)

        if nk == 1:
            finish(part)
            return

        @pl.when(k == 0)
        def _():
            acc[...] = part

        @pl.when(jnp.logical_and(k > 0, k < nk - 1))
        def _():
            acc[...] += part

        @pl.when(k == nk - 1)
        def _():
            finish(acc[...] + part)

    in_specs = [a_spec, b_spec]
    args = [a, b]
    if extra is not None:
        in_specs.append(e_spec)
        args.append(extra)
    outs, carried = _call(body, grid=grid, in_specs=in_specs, out_specs=[o_spec],
                          out_shape=[jax.ShapeDtypeStruct(out_shape, out_dtype)],
                          scratch_shapes=[pltpu.VMEM(acc_shape, F32)],
                          semantics=("parallel", "parallel", "arbitrary"), name=name, args=args, carry=carry,
                          prefetch=() if a_slots is None else (a_slots,))
    return outs[0] if carry is None else (outs[0], carried)


def _mm_slots(a, wbuf, slots, prev, *, name, carry=None, first=0):
    m, kdim = a.shape
    ns = wbuf.shape[2]
    bm, bn = _pick(m, 1024), _pick(ns, 1024)
    npb = ns // bn
    grid = (m // bm, slots.shape[0], npb)

    def body(s_ref, a_ref, b_ref, *rest):
        o_ref = rest[-1]
        o_ref[...] = _dot(a_ref[...], b_ref[...]).astype(o_ref.dtype)

    in_specs = [pl.BlockSpec((bm, kdim), lambda i, s, j, s_ref: (i, 0)),
                pl.BlockSpec((None, kdim, bn), lambda i, s, j, s_ref: (first + s, 0, j))]
    args = [a, wbuf]
    aliases = None
    if prev is not None:
        in_specs.append(pl.BlockSpec(memory_space=pl.ANY))
        args.append(prev)
        aliases = {2: 0}
    outs, carried = _call(
        body, grid=grid, in_specs=in_specs,
        out_specs=[pl.BlockSpec((bm, bn), lambda i, s, j, s_ref: (i, s_ref[s] * npb + j))],
        out_shape=[jax.ShapeDtypeStruct((m, N_CHIPS * ns), BF16)], scratch_shapes=[],
        semantics=("parallel", "arbitrary", "arbitrary"), name=name, args=args, carry=carry,
        prefetch=(slots,), aliases=aliases)
    return outs[0] if carry is None else (outs[0], carried)


def _row_tile(rows, cols, n_arrays):
    budget = VMEM_LIMIT_BYTES // 2
    cap = min(512, budget // (n_arrays * 2 * cols * 4))
    for t in range(cap - cap % SUBLANES, 0, -SUBLANES):
        if rows % t == 0:
            return t
    return rows


def _norm_mod(x, g, scale, shift, *, name):
    rows, d = x.shape
    tr = _row_tile(rows, d, 3)

    def body(x_ref, g_ref, sc_ref, sh_ref, o_ref):
        xv = x_ref[...]
        r = lax.rsqrt(jnp.mean(xv * xv, axis=-1, keepdims=True) + EPS)
        o_ref[...] = ((xv * r * g_ref[...]) * (1.0 + sc_ref[...]) + sh_ref[...]).astype(o_ref.dtype)

    row = pl.BlockSpec((tr, d), lambda i: (i, 0))
    vec = pl.BlockSpec((1, d), lambda i: (0, 0))
    return pl.pallas_call(
        body, grid=(rows // tr,), in_specs=[row, vec, vec, vec], out_specs=row,
        out_shape=jax.ShapeDtypeStruct((rows, d), BF16),
        compiler_params=_cparams(("parallel",)), name=name)(x, g, scale, shift)


CONV_CHUNK = 4 * SUBLANES


def _shifted_copies(buf, n):
    for r in range(1, SUBLANES):
        buf[r, pl.ds(0, n - SUBLANES), :] = buf[0, pl.ds(r, n - SUBLANES), :]


def _conv_fwd(proj, w_dw, b_dw, ln_g, ln_b, *, cw, carry=None):
    rows = proj.shape[0]
    tt = _pick(rows, 256, HALO)
    hb = tt // HALO

    def body(a_ref, g_ref, ha_ref, hg_ref, w_ref, b_ref, lg_ref, lb_ref, sl_ref, cv_ref, vs):
        i = pl.program_id(0)
        hv = ha_ref[...].astype(F32) * _sigmoid(hg_ref[...].astype(F32))
        vs[0, pl.ds(0, HALO), :] = jnp.where(i == 0, 0.0, hv)
        vs[0, pl.ds(HALO, tt), :] = a_ref[...].astype(F32) * _sigmoid(g_ref[...].astype(F32))
        _shifted_copies(vs, HALO + tt)

        def chunk(ci, carry):
            r0 = pl.multiple_of(ci * CONV_CHUNK, CONV_CHUNK)
            acc = jnp.broadcast_to(b_ref[...], (CONV_CHUNK, cw))
            for k in range(CONV_KERNEL):
                q, r = divmod(HALO - (CONV_KERNEL - 1) + k, SUBLANES)
                acc = acc + w_ref[pl.ds(k, 1), :] * vs[r, pl.ds(r0 + q * SUBLANES, CONV_CHUNK), :]
            cv_ref[pl.ds(r0, CONV_CHUNK), :] = acc
            return carry

        lax.fori_loop(0, tt // CONV_CHUNK, chunk, 0)
        acc = cv_ref[...]
        mu = jnp.mean(acc, axis=-1, keepdims=True)
        xc = acc - mu
        rstd = lax.rsqrt(jnp.mean(xc * xc, axis=-1, keepdims=True) + EPS)
        ln = xc * rstd * lg_ref[...] + lb_ref[...]
        sl_ref[...] = (ln * _sigmoid(ln)).astype(sl_ref.dtype)

    tile = lambda c: pl.BlockSpec((tt, cw), lambda i, c=c: (i, c))
    halo = lambda c: pl.BlockSpec((HALO, cw), lambda i, c=c: (jnp.maximum(i * hb - 1, 0), c))
    vec = pl.BlockSpec((1, cw), lambda i: (0, 0))
    outs, carried = _call(
        body, grid=(rows // tt,),
        in_specs=[tile(0), tile(1), halo(0), halo(1),
                  pl.BlockSpec((CONV_KERNEL, cw), lambda i: (0, 0)), vec, vec, vec],
        out_specs=[pl.BlockSpec((tt, cw), lambda i: (i, 0)), pl.BlockSpec((tt, cw), lambda i: (i, 0))],
        out_shape=[jax.ShapeDtypeStruct((rows, cw), BF16), jax.ShapeDtypeStruct((rows, cw), F32)],
        scratch_shapes=[pltpu.VMEM((SUBLANES, HALO + tt, cw), F32)],
        semantics=("parallel",), name="conv_fwd", args=[proj, proj, proj, proj, w_dw, b_dw, ln_g, ln_b],
        carry=carry)
    return outs if carry is None else (outs, carried)


def _ln_bwd(dsl, cv, ln_g, ln_b):
    rows, cw = cv.shape
    tr = _row_tile(rows, cw, 3)

    def body(d_ref, cv_ref, lg_ref, lb_ref, o_ref, dg_ref, db_ref):
        i = pl.program_id(0)

        @pl.when(i == 0)
        def _():
            dg_ref[...] = jnp.zeros_like(dg_ref)
            db_ref[...] = jnp.zeros_like(db_ref)

        x = cv_ref[...]
        mu = jnp.mean(x, axis=-1, keepdims=True)
        xc = x - mu
        rstd = lax.rsqrt(jnp.mean(xc * xc, axis=-1, keepdims=True) + EPS)
        xh = xc * rstd
        ln = xh * lg_ref[...] + lb_ref[...]
        s = _sigmoid(ln)
        dln = d_ref[...].astype(F32) * (s * (1.0 + ln * (1.0 - s)))
        dg_ref[...] += jnp.sum(dln * xh, axis=0, keepdims=True)
        db_ref[...] += jnp.sum(dln, axis=0, keepdims=True)
        dxh = dln * lg_ref[...]
        m1 = jnp.mean(dxh, axis=-1, keepdims=True)
        m2 = jnp.mean(dxh * xh, axis=-1, keepdims=True)
        o_ref[...] = rstd * (dxh - m1 - xh * m2)

    row = pl.BlockSpec((tr, cw), lambda i: (i, 0))
    vec = pl.BlockSpec((1, cw), lambda i: (0, 0))
    return pl.pallas_call(
        body, grid=(rows // tr,), in_specs=[row, row, vec, vec], out_specs=[row, vec, vec],
        out_shape=[jax.ShapeDtypeStruct((rows, cw), F32), jax.ShapeDtypeStruct((1, cw), F32),
                   jax.ShapeDtypeStruct((1, cw), F32)],
        compiler_params=_cparams(("arbitrary",)), name="ln_bwd")(dsl, cv, ln_g, ln_b)


def _conv_bwd(dcv, proj, w_dw, dproj, *, cw, carry=None):
    rows = proj.shape[0]
    tt = _pick(rows, 256, HALO)
    hb = tt // HALO
    nt = rows // tt
    taps = CONV_KERNEL

    def body(d_ref, dn_ref, a_ref, g_ref, ha_ref, hg_ref, w_ref, dproj_ref, o_ref, dw_ref, db_ref, vs, ds):
        i = pl.program_id(0)

        @pl.when(i == 0)
        def _():
            dw_ref[...] = jnp.zeros_like(dw_ref)
            db_ref[...] = jnp.zeros_like(db_ref)

        hv = ha_ref[...].astype(F32) * _sigmoid(hg_ref[...].astype(F32))
        vs[0, pl.ds(0, HALO), :] = jnp.where(i == 0, 0.0, hv)
        vs[0, pl.ds(HALO, tt), :] = a_ref[...].astype(F32) * _sigmoid(g_ref[...].astype(F32))
        _shifted_copies(vs, HALO + tt)
        ds[0, pl.ds(0, tt), :] = d_ref[...]
        ds[0, pl.ds(tt, HALO), :] = jnp.where(i == nt - 1, 0.0, dn_ref[...])
        _shifted_copies(ds, tt + HALO)
        db_ref[...] += jnp.sum(d_ref[...], axis=0, keepdims=True)
        for k in range(taps):
            q, r = divmod(HALO - (taps - 1) + k, SUBLANES)
            dw_ref[pl.ds(k, 1), :] += jnp.sum(d_ref[...] * vs[r, pl.ds(q * SUBLANES, tt), :], axis=0, keepdims=True)

        def chunk(ci, carry):
            r0 = pl.multiple_of(ci * CONV_CHUNK, CONV_CHUNK)
            dv = jnp.zeros((CONV_CHUNK, cw), F32)
            for k in range(taps):
                q, r = divmod(taps - 1 - k, SUBLANES)
                dv = dv + w_ref[pl.ds(k, 1), :] * ds[r, pl.ds(r0 + q * SUBLANES, CONV_CHUNK), :]
            av = a_ref[pl.ds(r0, CONV_CHUNK), :].astype(F32)
            sg = _sigmoid(g_ref[pl.ds(r0, CONV_CHUNK), :].astype(F32))
            o_ref[pl.ds(r0, CONV_CHUNK), pl.ds(0, cw)] = (dv * sg).astype(o_ref.dtype)
            o_ref[pl.ds(r0, CONV_CHUNK), pl.ds(cw, cw)] = (dv * av * sg * (1.0 - sg)).astype(o_ref.dtype)
            return carry

        lax.fori_loop(0, tt // CONV_CHUNK, chunk, 0)

    tile = lambda c: pl.BlockSpec((tt, cw), lambda i, c=c: (i, c))
    halo = lambda c: pl.BlockSpec((HALO, cw), lambda i, c=c: (jnp.maximum(i * hb - 1, 0), c))
    nxt = pl.BlockSpec((HALO, cw), lambda i: (jnp.minimum((i + 1) * hb, nt * hb - 1), 0))
    outs, carried = _call(
        body, grid=(nt,),
        in_specs=[pl.BlockSpec((tt, cw), lambda i: (i, 0)), nxt, tile(0), tile(1), halo(0), halo(1),
                  pl.BlockSpec((taps, cw), lambda i: (0, 0)), pl.BlockSpec(memory_space=pl.ANY)],
        out_specs=[pl.BlockSpec((tt, 2 * cw), lambda i: (i, 0)),
                   pl.BlockSpec((taps, cw), lambda i: (0, 0)), pl.BlockSpec((1, cw), lambda i: (0, 0))],
        out_shape=[jax.ShapeDtypeStruct(dproj.shape, dproj.dtype), jax.ShapeDtypeStruct((taps, cw), F32),
                   jax.ShapeDtypeStruct((1, cw), F32)],
        scratch_shapes=[pltpu.VMEM((SUBLANES, HALO + tt, cw), F32), pltpu.VMEM((SUBLANES, tt + HALO, cw), F32)],
        semantics=("arbitrary",), name="conv_bwd", args=[dcv, dcv, proj, proj, proj, proj, w_dw, dproj],
        carry=carry, aliases={7: 0})
    return outs if carry is None else (outs, carried)


def _merge_fwd(proj, y_conv, ya, yb, *, cw):
    rows = proj.shape[0]
    tr = _row_tile(rows, cw, 4)

    def body(gc_ref, gs_ref, yc_ref, ya_ref, yb_ref, o_ref):
        ys = ya_ref[...].astype(F32) * _sigmoid(yb_ref[...].astype(F32))
        o_ref[...] = (_sigmoid(gc_ref[...].astype(F32)) * yc_ref[...].astype(F32)
                      + _sigmoid(gs_ref[...].astype(F32)) * ys).astype(o_ref.dtype)

    blk = lambda off: pl.BlockSpec((tr, cw), lambda i, h, off=off: (i, off + h))
    return pl.pallas_call(
        body, grid=(rows // tr, 2), in_specs=[blk(3), blk(5), blk(0), blk(0), blk(0)], out_specs=blk(0),
        out_shape=jax.ShapeDtypeStruct((rows, 2 * cw), BF16),
        compiler_params=_cparams(("parallel", "parallel")), name="merge_fwd")(proj, proj, y_conv, ya, yb)


def _merge_bwd(dmerged, proj, y_conv, ya, yb, *, cw, carry=None):
    rows = proj.shape[0]
    tr = _row_tile(rows, cw, 3)

    def body(d_ref, g_ref, yc_ref, ya_ref, yb_ref, dg_ref, dyc_ref, dya_ref, dyb_ref):
        q = pl.program_id(1)
        d = d_ref[...].astype(F32)
        sg = _sigmoid(g_ref[...].astype(F32))

        @pl.when(q < 2)
        def _():
            dg_ref[...] = (d * yc_ref[...].astype(F32) * sg * (1.0 - sg)).astype(dg_ref.dtype)
            dyc_ref[...] = (d * sg).astype(dyc_ref.dtype)

        @pl.when(q >= 2)
        def _():
            sb = _sigmoid(yb_ref[...].astype(F32))
            yav = ya_ref[...].astype(F32)
            dg_ref[...] = (d * (yav * sb) * sg * (1.0 - sg)).astype(dg_ref.dtype)
            dys = d * sg
            dya_ref[...] = (dys * sb).astype(dya_ref.dtype)
            dyb_ref[...] = (dys * yav * sb * (1.0 - sb)).astype(dyb_ref.dtype)

    spec = lambda f: pl.BlockSpec((tr, cw), lambda i, q, f=f: (i, f(q)))
    conv_half = spec(lambda q: jnp.minimum(q, 1))
    ssm_half = spec(lambda q: jnp.maximum(q - 2, 0))
    o2 = jax.ShapeDtypeStruct((rows, 2 * cw), BF16)
    outs, carried = _call(
        body, grid=(rows // tr, 4),
        in_specs=[spec(lambda q: q % 2), spec(lambda q: 3 + q), conv_half, ssm_half, ssm_half],
        out_specs=[spec(lambda q: 3 + q), conv_half, ssm_half, ssm_half],
        out_shape=[jax.ShapeDtypeStruct((rows, 7 * cw), BF16), o2, o2, o2], scratch_shapes=[],
        semantics=("parallel", "arbitrary"), name="merge_bwd", args=[dmerged, proj, y_conv, ya, yb],
        carry=carry)
    return outs if carry is None else (outs, carried)


def _res_norm(x, mo, gate, g, scale, shift):
    rows, d = x.shape
    tr = _row_tile(rows, d, 4)

    def body(x_ref, mo_ref, gt_ref, g_ref, sc_ref, sh_ref, h_ref, z_ref):
        h = x_ref[...] + gt_ref[...] * mo_ref[...].astype(F32)
        h_ref[...] = h
        r = lax.rsqrt(jnp.mean(h * h, axis=-1, keepdims=True) + EPS)
        z_ref[...] = ((h * r * g_ref[...]) * (1.0 + sc_ref[...]) + sh_ref[...]).astype(z_ref.dtype)

    row = pl.BlockSpec((tr, d), lambda i: (i, 0))
    vec = pl.BlockSpec((1, d), lambda i: (0, 0))
    return pl.pallas_call(
        body, grid=(rows // tr,), in_specs=[row, row, vec, vec, vec, vec], out_specs=[row, row],
        out_shape=[jax.ShapeDtypeStruct((rows, d), F32), jax.ShapeDtypeStruct((rows, d), BF16)],
        compiler_params=_cparams(("parallel",)), name="res_norm")(x, mo, gate, g, scale, shift)


def _final_fwd_bwd(h1, ff, gate2, final_g, target):
    rows, d = h1.shape
    tr = _row_tile(rows, d, 5)

    def body(h_ref, ff_ref, gt_ref, fg_ref, t_ref, dh_ref, dff_ref, loss_ref, dfg_ref, dgt_ref):
        i = pl.program_id(0)

        @pl.when(i == 0)
        def _():
            loss_ref[...] = jnp.zeros_like(loss_ref)
            dfg_ref[...] = jnp.zeros_like(dfg_ref)
            dgt_ref[...] = jnp.zeros_like(dgt_ref)

        ffv = ff_ref[...].astype(F32)
        h2 = h_ref[...] + gt_ref[...] * ffv
        r = lax.rsqrt(jnp.mean(h2 * h2, axis=-1, keepdims=True) + EPS)
        y = h2 * r
        e = y * fg_ref[...] - t_ref[...]
        loss_ref[...] += 0.5 * jnp.sum(jnp.mean(e * e, axis=-1, keepdims=True))
        dout = e * (1.0 / d)
        dfg_ref[...] += jnp.sum(dout * y, axis=0, keepdims=True)
        dy = dout * fg_ref[...]
        dh2 = r * (dy - y * jnp.mean(dy * y, axis=-1, keepdims=True))
        dh_ref[...] = dh2
        dgt_ref[...] += jnp.sum(dh2 * ffv, axis=0, keepdims=True)
        dff_ref[...] = (dh2 * gt_ref[...]).astype(dff_ref.dtype)

    row = pl.BlockSpec((tr, d), lambda i: (i, 0))
    vec = pl.BlockSpec((1, d), lambda i: (0, 0))
    return pl.pallas_call(
        body, grid=(rows // tr,), in_specs=[row, row, vec, vec, row],
        out_specs=[row, row, pl.BlockSpec((1, LANES), lambda i: (0, 0)), vec, vec],
        out_shape=[jax.ShapeDtypeStruct((rows, d), F32), jax.ShapeDtypeStruct((rows, d), BF16),
                   jax.ShapeDtypeStruct((1, LANES), F32), jax.ShapeDtypeStruct((1, d), F32),
                   jax.ShapeDtypeStruct((1, d), F32)],
        compiler_params=_cparams(("arbitrary",)), name="final_fwd_bwd")(h1, ff, gate2, final_g, target)


def _norm_mod_bwd(dz, hin, dres, g, scale, gate, mo, *, name, carry=None):
    rows, d = hin.shape
    with_gate = gate is not None
    tr = _row_tile(rows, d, 6)

    def body(*refs):
        if with_gate:
            (dz_ref, h_ref, dr_ref, g_ref, sc_ref, gt_ref, mo_ref,
             dh_ref, dsh_ref, dsc_ref, dg_ref, dmo_ref, dgt_ref) = refs
        else:
            dz_ref, h_ref, dr_ref, g_ref, sc_ref, dh_ref, dsh_ref, dsc_ref, dg_ref = refs
        i = pl.program_id(0)

        @pl.when(i == 0)
        def _():
            dsh_ref[...] = jnp.zeros_like(dsh_ref)
            dsc_ref[...] = jnp.zeros_like(dsc_ref)
            dg_ref[...] = jnp.zeros_like(dg_ref)
            if with_gate:
                dgt_ref[...] = jnp.zeros_like(dgt_ref)

        dzv = dz_ref[...].astype(F32)
        h = h_ref[...]
        r = lax.rsqrt(jnp.mean(h * h, axis=-1, keepdims=True) + EPS)
        y = h * r
        dsh_ref[...] += jnp.sum(dzv, axis=0, keepdims=True)
        dsc_ref[...] += jnp.sum(dzv * (y * g_ref[...]), axis=0, keepdims=True)
        dn = dzv * (1.0 + sc_ref[...])
        dg_ref[...] += jnp.sum(dn * y, axis=0, keepdims=True)
        dy = dn * g_ref[...]
        dh = dr_ref[...] + r * (dy - y * jnp.mean(dy * y, axis=-1, keepdims=True))
        dh_ref[...] = dh
        if with_gate:
            dmo_ref[...] = (dh * gt_ref[...]).astype(dmo_ref.dtype)
            dgt_ref[...] += jnp.sum(dh * mo_ref[...].astype(F32), axis=0, keepdims=True)

    row = pl.BlockSpec((tr, d), lambda i: (i, 0))
    vec = pl.BlockSpec((1, d), lambda i: (0, 0))
    vshape = jax.ShapeDtypeStruct((1, d), F32)
    in_specs = [row, row, row, vec, vec]
    args = [dz, hin, dres, g, scale]
    out_specs = [row, vec, vec, vec]
    out_shape = [jax.ShapeDtypeStruct((rows, d), F32), vshape, vshape, vshape]
    if with_gate:
        in_specs += [vec, row]
        args += [gate, mo]
        out_specs += [row, vec]
        out_shape += [jax.ShapeDtypeStruct((rows, d), BF16), vshape]
    outs, carried = _call(
        body, grid=(rows // tr,), in_specs=in_specs, out_specs=out_specs, out_shape=out_shape,
        scratch_shapes=[], semantics=("arbitrary",), name=name, args=args, carry=carry)
    return outs if carry is None else (outs, carried)


def _s5_discretise(a_re, a_im, log_dt, b_re, b_im):
    dt = jnp.exp(log_dt)[:, None]
    er = jnp.exp(a_re * dt)
    lr = er * jnp.cos(a_im * dt)
    li = er * jnp.sin(a_im * dt)
    den = a_re * a_re + a_im * a_im
    cr = ((lr - 1.0) * a_re + li * a_im) / den
    ci = (li * a_re - (lr - 1.0) * a_im) / den
    bbr = cr[..., None] * b_re - ci[..., None] * b_im
    bbi = cr[..., None] * b_im + ci[..., None] * b_re
    return lr, li, bbr, bbi


def _block_diag(w):
    g, r, c = w.shape
    nb = g // GROUPS_PER_BLOCK
    eye = jnp.eye(GROUPS_PER_BLOCK, dtype=w.dtype)
    w5 = w.reshape(nb, GROUPS_PER_BLOCK, r, 1, c) * eye[None, :, None, :, None]
    return w5.reshape(nb, GROUPS_PER_BLOCK * r, GROUPS_PER_BLOCK * c)


def _block_diag_extract(m, r, c):
    nb = m.shape[0]
    m5 = m.reshape(nb, GROUPS_PER_BLOCK, r, GROUPS_PER_BLOCK, c)
    idx = jnp.arange(GROUPS_PER_BLOCK)
    d = m5[:, idx, :, idx, :]
    return jnp.moveaxis(d, 0, 1).reshape(nb * GROUPS_PER_BLOCK, r, c)


def _scan_multipliers(lr, li):
    power = jnp.arange(1, SUBLANES + 1, dtype=F32)[None, :, None]
    er = jnp.exp(power * lr)
    pr = er * jnp.cos(power * li)
    pi = er * jnp.sin(power * li)
    rows = jnp.arange(SUBLANES)[None, :, None]
    fr, fi, rr, ri = [], [], [], []
    for s in (1, 2, 4):
        mf = (rows >= s).astype(F32)
        mr = (rows <= SUBLANES - 1 - s).astype(F32)
        fr.append(mf * pr[:, s - 1:s, :])
        fi.append(mf * pi[:, s - 1:s, :])
        rr.append(mr * pr[:, s - 1:s, :])
        ri.append(mr * pi[:, s - 1:s, :])
    fr.append(pr)
    fi.append(pi)
    rr.append(pr[:, ::-1, :])
    ri.append(pi[:, ::-1, :])
    st = lambda xs: jnp.stack(xs, axis=1)
    return st(fr), st(fi), st(rr), st(ri)


def _scan_rows(sre, sim, mul_r, mul_i, n_groups, reverse):
    sgn = -1.0 if reverse else 1.0
    lanes = sre.shape[1]

    def step(k, carry):
        cr, ci = carry
        kk = (n_groups - 1 - k) if reverse else k
        r0 = pl.multiple_of(kk * SUBLANES, SUBLANES)
        xr = sre[pl.ds(r0, SUBLANES), :]
        xi = sim[pl.ds(r0, SUBLANES), :]
        for lvl, s in enumerate((1, 2, 4)):
            sh = (SUBLANES - s) if reverse else s
            nr = pltpu.roll(xr, sh, 0)
            ni = pltpu.roll(xi, sh, 0)
            mr = mul_r[lvl]
            mi = mul_i[lvl] * sgn
            xr, xi = xr + mr * nr - mi * ni, xi + mr * ni + mi * nr
        mr = mul_r[3]
        mi = mul_i[3] * sgn
        xr, xi = xr + mr * cr - mi * ci, xi + mr * ci + mi * cr
        sre[pl.ds(r0, SUBLANES), :] = xr
        sim[pl.ds(r0, SUBLANES), :] = xi
        edge = 0 if reverse else SUBLANES - 1
        ncr = jnp.broadcast_to(xr[edge:edge + 1, :], (SUBLANES, lanes))
        nci = jnp.broadcast_to(xi[edge:edge + 1, :], (SUBLANES, lanes))
        return ncr, nci

    zero = jnp.zeros((SUBLANES, lanes), F32)
    lax.fori_loop(0, n_groups, step, (zero, zero))


def _dot(a, b):
    return jnp.dot(a, b, preferred_element_type=F32)


def _dotf(a, b):
    return _dot(a.astype(BF16), b)


def _s5_operands(lr, li, bbr, bbi, c_re, c_im):
    g = lr.shape[0]
    nb = g // GROUPS_PER_BLOCK
    tb = lambda w: jnp.swapaxes(w, 1, 2)
    b_in = [_block_diag(tb(bbr)), _block_diag(tb(bbi))]
    c_out = [_block_diag(tb(c_re)), _block_diag(tb(c_im))]
    b_out = [_block_diag(bbr), _block_diag(bbi)]
    c_in = [_block_diag(c_re), _block_diag(c_im)]
    lam_r = lr.reshape(nb, 1, STATE_LANES)
    lam_i = li.reshape(nb, 1, STATE_LANES)
    mults = _scan_multipliers(lam_r, lam_i)
    cast = lambda ws: [w.astype(BF16) for w in ws]
    return cast(b_in), cast(c_out), cast(b_out), cast(c_in), mults


def _s5_fwd(proj, d_skip, b_in, c_out, mults, *, col0, carry=None):
    rows = proj.shape[0]
    nb = b_in[0].shape[0]
    tm = _pick(rows, 512, SUBLANES)
    n_tiles = rows // tm
    s_l = STATE_LANES

    def body(u_ref, dk_ref, br, bi, cr, ci, fr_ref, fi_ref, o_ref, sr_ref, si_ref, sre, sim):
        for t in range(n_tiles):
            rs = pl.ds(t * tm, tm)
            ub = u_ref[rs, :]
            sre[rs, :] = _dot(ub, br[...])
            sim[rs, :] = _dot(ub, bi[...])
        _scan_rows(sre, sim, fr_ref, fi_ref, rows // SUBLANES, False)
        for t in range(n_tiles):
            rs = pl.ds(t * tm, tm)
            srb = sre[rs, :].astype(BF16)
            sib = sim[rs, :].astype(BF16)
            sr_ref[rs, :] = srb
            si_ref[rs, :] = sib
            y0 = _dot(srb, cr[...]) - _dot(sib, ci[...])
            y1 = y0 + dk_ref[...] * u_ref[rs, :].astype(F32)
            o_ref[rs, :] = _gelu(y1).astype(o_ref.dtype)

    mat_in = pl.BlockSpec((None, LANES, s_l), lambda g: (g, 0, 0))
    mat_out = pl.BlockSpec((None, s_l, LANES), lambda g: (g, 0, 0))
    mul = pl.BlockSpec((None, 4, SUBLANES, s_l), lambda g: (g, 0, 0, 0))
    state = pl.BlockSpec((rows, s_l), lambda g: (0, g))
    outs, carried = _call(
        body, grid=(nb,),
        in_specs=[pl.BlockSpec((rows, LANES), lambda g: (0, col0 + g)), pl.BlockSpec((1, LANES), lambda g: (0, g))]
        + [mat_in] * 2 + [mat_out] * 2 + [mul] * 2,
        out_specs=[pl.BlockSpec((rows, LANES), lambda g: (0, g)), state, state],
        out_shape=[jax.ShapeDtypeStruct((rows, nb * LANES), BF16), jax.ShapeDtypeStruct((rows, nb * s_l), BF16),
                   jax.ShapeDtypeStruct((rows, nb * s_l), BF16)],
        scratch_shapes=[pltpu.VMEM((rows, s_l), F32), pltpu.VMEM((rows, s_l), F32)],
        semantics=("parallel",), name="s5_fwd", args=[proj, d_skip, *b_in, *c_out, mults[0], mults[1]], carry=carry)
    return outs if carry is None else (outs, carried)


def _s5_bwd(proj, dyg, d_skip, states, c_out, b_out, c_in, mults, dproj, *, col0, carry=None):
    rows = proj.shape[0]
    nb = c_out[0].shape[0]
    tm = _pick(rows, 512, SUBLANES)
    n_tiles = rows // tm
    s_l = STATE_LANES
    n_groups = rows // SUBLANES
    tn = (((0,), (0,)), ((), ()))

    def body(u_ref, dy_ref, dk_ref, sr_ref, si_ref, cr, ci, bor, boi, cir, cii, rr_ref, ri_ref, dproj_ref,
             du_ref, ddk_ref, dbr_ref, dbi_ref, dcr_ref, dci_ref, dlr_ref, dli_ref,
             gre, gim, dy1):
        ddk = jnp.zeros((1, LANES), F32)
        dcr = jnp.zeros((s_l, LANES), F32)
        dci = jnp.zeros((s_l, LANES), F32)
        for t in range(n_tiles):
            rs = pl.ds(t * tm, tm)
            srb = sr_ref[rs, :]
            sib = si_ref[rs, :]
            uf = u_ref[rs, :].astype(F32)
            y0 = _dot(srb, cr[...]) - _dot(sib, ci[...])
            y1 = y0 + dk_ref[...] * uf
            d1 = dy_ref[rs, :].astype(F32) * _gelu_grad(y1)
            dy1[rs, :] = d1
            ddk = ddk + jnp.sum(d1 * uf, axis=0, keepdims=True)
            d1b = d1.astype(BF16)
            dcr = dcr + lax.dot_general(srb, d1b, tn, preferred_element_type=F32)
            dci = dci - lax.dot_general(sib, d1b, tn, preferred_element_type=F32)
            gre[rs, :] = _dot(d1b, cir[...])
            gim[rs, :] = -_dot(d1b, cii[...])
        ddk_ref[...] = ddk
        dcr_ref[...] = dcr
        dci_ref[...] = dci

        last_row = lax.broadcasted_iota(jnp.int32, (SUBLANES, s_l), 0) == SUBLANES - 1

        def group(r0, s_r, s_i, carry):
            cr_, ci_, ar, ai = carry
            xr = gre[pl.ds(r0, SUBLANES), :]
            xi = gim[pl.ds(r0, SUBLANES), :]
            for lvl, s in enumerate((1, 2, 4)):
                nr = pltpu.roll(xr, SUBLANES - s, 0)
                ni = pltpu.roll(xi, SUBLANES - s, 0)
                mr = rr_ref[lvl]
                mi = ri_ref[lvl]
                xr, xi = xr + mr * nr + mi * ni, xi + mr * ni - mi * nr
            mr = rr_ref[3]
            mi = ri_ref[3]
            xr, xi = xr + mr * cr_ + mi * ci_, xi + mr * ci_ - mi * cr_
            gre[pl.ds(r0, SUBLANES), :] = xr
            gim[pl.ds(r0, SUBLANES), :] = xi
            nxt_r = jnp.where(last_row, cr_, pltpu.roll(xr, SUBLANES - 1, 0))
            nxt_i = jnp.where(last_row, ci_, pltpu.roll(xi, SUBLANES - 1, 0))
            ncr = jnp.broadcast_to(xr[0:1, :], (SUBLANES, s_l))
            nci = jnp.broadcast_to(xi[0:1, :], (SUBLANES, s_l))
            return ncr, nci, ar + nxt_r * s_r + nxt_i * s_i, ai + nxt_i * s_r - nxt_r * s_i

        def rev_step(k, carry):
            r0 = pl.multiple_of((n_groups // 2 - 1 - k) * 2 * SUBLANES, 2 * SUBLANES)
            s_r = sr_ref[pl.ds(r0, 2 * SUBLANES), :].astype(F32)
            s_i = si_ref[pl.ds(r0, 2 * SUBLANES), :].astype(F32)
            carry = group(r0 + SUBLANES, s_r[SUBLANES:], s_i[SUBLANES:], carry)
            return group(r0, s_r[:SUBLANES], s_i[:SUBLANES], carry)

        zero = jnp.zeros((SUBLANES, s_l), F32)
        _, _, ar, ai = lax.fori_loop(0, n_groups // 2, rev_step, (zero, zero, zero, zero))
        dlr_ref[...] = jnp.sum(ar, axis=0, keepdims=True)
        dli_ref[...] = jnp.sum(ai, axis=0, keepdims=True)

        dbr = jnp.zeros((LANES, s_l), F32)
        dbi = jnp.zeros((LANES, s_l), F32)
        for t in range(n_tiles):
            rs = pl.ds(t * tm, tm)
            gr = gre[rs, :]
            gi = gim[rs, :]
            grb = gr.astype(BF16)
            gib = gi.astype(BF16)
            du = _dot(grb, bor[...]) + _dot(gib, boi[...]) + dy1[rs, :] * dk_ref[...]
            du_ref[rs, :] = du.astype(du_ref.dtype)
            ub = u_ref[rs, :]
            dbr = dbr + lax.dot_general(ub, grb, tn, preferred_element_type=F32)
            dbi = dbi + lax.dot_general(ub, gib, tn, preferred_element_type=F32)
        dbr_ref[...] = dbr
        dbi_ref[...] = dbi

    mat_in = pl.BlockSpec((None, LANES, s_l), lambda g: (g, 0, 0))
    mat_out = pl.BlockSpec((None, s_l, LANES), lambda g: (g, 0, 0))
    mul = pl.BlockSpec((None, 4, SUBLANES, s_l), lambda g: (g, 0, 0, 0))
    lam = pl.BlockSpec((None, 1, s_l), lambda g: (g, 0, 0))
    col = pl.BlockSpec((rows, LANES), lambda g: (0, g))
    vec = pl.BlockSpec((1, LANES), lambda g: (0, g))
    state = pl.BlockSpec((rows, s_l), lambda g: (0, g))
    outs, carried = _call(
        body, grid=(nb,),
        in_specs=[pl.BlockSpec((rows, LANES), lambda g: (0, col0 + g)), col, vec]
        + [state] * 2 + [mat_out] * 2 + [mat_out] * 2 + [mat_in] * 2 + [mul] * 2
        + [pl.BlockSpec(memory_space=pl.ANY)],
        out_specs=[pl.BlockSpec((rows, LANES), lambda g: (0, col0 + g)), vec, mat_in, mat_in, mat_out, mat_out,
                   lam, lam],
        out_shape=[jax.ShapeDtypeStruct(dproj.shape, dproj.dtype), jax.ShapeDtypeStruct((1, nb * LANES), F32),
                   jax.ShapeDtypeStruct((nb, LANES, s_l), F32), jax.ShapeDtypeStruct((nb, LANES, s_l), F32),
                   jax.ShapeDtypeStruct((nb, s_l, LANES), F32), jax.ShapeDtypeStruct((nb, s_l, LANES), F32),
                   jax.ShapeDtypeStruct((nb, 1, s_l), F32), jax.ShapeDtypeStruct((nb, 1, s_l), F32)],
        scratch_shapes=[pltpu.VMEM((rows, s_l), F32)] * 2 + [pltpu.VMEM((rows, LANES), F32)],
        semantics=("parallel",), name="s5_bwd",
        args=[proj, dyg, d_skip, *states, *c_out, *b_out, *c_in, mults[2], mults[3], dproj], carry=carry,
        aliases={13: 0})
    return outs if carry is None else (outs, carried)


def _silu(v):
    return v * _sigmoid(v)


def _ada_fwd(c_all, w_shard, b_cols):
    d, n = w_shard.shape
    bn = _pick(n, 512)

    def body(c_ref, w_ref, b_ref, o_ref):
        ca = _silu(c_ref[...]).astype(BF16)
        o_ref[...] = _dot(ca, w_ref[...].astype(BF16)) + b_ref[...]

    return pl.pallas_call(
        body, grid=(n // bn,),
        in_specs=[pl.BlockSpec((N_DEV, d), lambda j: (0, 0)), pl.BlockSpec((d, bn), lambda j: (0, j)),
                  pl.BlockSpec((1, bn), lambda j: (0, j))],
        out_specs=pl.BlockSpec((N_DEV, bn), lambda j: (0, j)),
        out_shape=jax.ShapeDtypeStruct((N_DEV, n), F32),
        compiler_params=_cparams(("parallel",)), name="ada_fwd")(c_all, w_shard, b_cols)


def _ada_bwd(c_all, dmod_cols):
    d = c_all.shape[1]
    n = dmod_cols.shape[1]
    bn = _pick(n, 512)

    def body(c_ref, g_ref, o_ref):
        ca = _silu(c_ref[...]).astype(BF16)
        o_ref[...] = lax.dot_general(ca, g_ref[...].astype(BF16), (((0,), (0,)), ((), ())),
                                     preferred_element_type=F32)

    return pl.pallas_call(
        body, grid=(n // bn,),
        in_specs=[pl.BlockSpec((N_DEV, d), lambda j: (0, 0)), pl.BlockSpec((N_DEV, bn), lambda j: (0, j))],
        out_specs=pl.BlockSpec((d, bn), lambda j: (0, j)),
        out_shape=jax.ShapeDtypeStruct((d, n), F32),
        compiler_params=_cparams(("parallel",)), name="ada_bwd")(c_all, dmod_cols)


def _cast_bf16(w, *, name):
    rows, cols = w.shape
    tr = _row_tile(rows, cols, 3)

    def body(w_ref, o_ref, slot_ref):
        o_ref[...] = w_ref[...].astype(BF16)
        slot_ref[...] = w_ref[...].astype(BF16)

    row = pl.BlockSpec((tr, cols), lambda i: (i, 0))
    return pl.pallas_call(
        body, grid=(rows // tr,), in_specs=[row],
        out_specs=[row, pl.BlockSpec((None, tr, cols), lambda i: (0, i, 0))],
        out_shape=[jax.ShapeDtypeStruct((rows, cols), BF16), jax.ShapeDtypeStruct((N_CHIPS, rows, cols), BF16)],
        compiler_params=_cparams(("parallel",)), name=name)(w)


def _adamw_update(w_ref, g_ref, m_ref, v_ref, d_ref, nm_ref, nv_ref):
    c1 = 1.0 / (1.0 - ADAM_B1 ** ADAM_STEP)
    c2 = 1.0 / (1.0 - ADAM_B2 ** ADAM_STEP)
    gv = g_ref[...]
    nm = ADAM_B1 * m_ref[...] + (1.0 - ADAM_B1) * gv
    nv = ADAM_B2 * v_ref[...] + (1.0 - ADAM_B2) * (gv * gv)
    nm_ref[...] = nm
    nv_ref[...] = nv
    d_ref[...] = -ADAM_LR * ((nm * c1) / (jnp.sqrt(nv * c2) + ADAM_EPS) + ADAM_WD * w_ref[...])


def _adamw_many(ws, gs, ms, vs, *, name):
    n = len(ws)

    def body(*refs):
        for i in range(n):
            _adamw_update(*[refs[k * n + i] for k in range(7)])

    vm = pl.BlockSpec(memory_space=pltpu.VMEM)
    shapes = [jax.ShapeDtypeStruct(a.shape, F32) for a in ws]
    outs = pl.pallas_call(
        body, in_specs=[vm] * (4 * n), out_specs=[vm] * (3 * n), out_shape=shapes * 3,
        compiler_params=pltpu.CompilerParams(vmem_limit_bytes=VMEM_LIMIT_BYTES), name=name)(*ws, *gs, *ms, *vs)
    return list(outs[:n]), list(outs[n:2 * n]), list(outs[2 * n:])


def _adamw(w, g, m, v, *, name, after=None, with_grad=False):
    rows, cols = w.shape
    tr = _row_tile(rows, cols, 8)
    n_out = 4 if with_grad else 3

    def body(w_ref, g_ref, m_ref, v_ref, *rest):
        outs = rest[-n_out:]
        _adamw_update(w_ref, g_ref, m_ref, v_ref, *outs[:3])
        if with_grad:
            outs[3][...] = g_ref[...]

    row = pl.BlockSpec((tr, cols), lambda i: (i, 0))
    shp = jax.ShapeDtypeStruct((rows, cols), F32)
    extra = [] if after is None else [after]
    outs, _ = _call(
        body, grid=(rows // tr,), in_specs=[row] * 4 + [pl.BlockSpec(memory_space=pl.ANY)] * len(extra),
        out_specs=[row] * n_out, out_shape=[shp] * n_out, scratch_shapes=[], semantics=("parallel",), name=name,
        args=[w, g, m, v] + extra)
    return outs


def _sum_leading(a, *, name, out_dtype=F32):
    n, rows, cols = a.shape
    tr = _row_tile(rows, cols, n + 1)

    def body(a_ref, o_ref):
        acc = a_ref[0].astype(F32)
        for i in range(1, n):
            acc = acc + a_ref[i].astype(F32)
        o_ref[...] = acc.astype(o_ref.dtype)

    return pl.pallas_call(
        body, grid=(rows // tr,), in_specs=[pl.BlockSpec((n, tr, cols), lambda i: (0, i, 0))],
        out_specs=pl.BlockSpec((tr, cols), lambda i: (i, 0)),
        out_shape=jax.ShapeDtypeStruct((rows, cols), out_dtype),
        compiler_params=_cparams(("parallel",)), name=name)(a)


def _add_half(dw, land, my_c, *, name):
    n, r, cols = dw.shape
    h = r // 2
    tr = _row_tile(h, cols, 3)
    hb = h // tr

    def body(c_ref, a_ref, b_ref, o_ref):
        o_ref[...] = (a_ref[...].astype(F32) + b_ref[...].astype(F32)).astype(o_ref.dtype)

    gs = pltpu.PrefetchScalarGridSpec(
        num_scalar_prefetch=1, grid=(n, hb),
        in_specs=[pl.BlockSpec((None, tr, cols), lambda s, i, c_ref: (s, c_ref[0] * hb + i, 0)),
                  pl.BlockSpec((None, tr, cols), lambda s, i, c_ref: (s, i, 0))],
        out_specs=pl.BlockSpec((None, tr, cols), lambda s, i, c_ref: (s, i, 0)))
    return pl.pallas_call(
        body, grid_spec=gs, out_shape=jax.ShapeDtypeStruct((n, h, cols), BF16),
        compiler_params=_cparams(("parallel", "parallel")), name=name)(my_c, dw, land)


def _mesh_pos():
    return lax.axis_index("x"), lax.axis_index("y"), lax.axis_index("c")


def _other_chips(x, y):
    return [(1 - x, y), (x, 1 - y), (1 - x, 1 - y)]


def _gather_small(blk, *, name):
    m_per, n = blk.shape

    def body(x_ref, out_ref, send_sems, recv_sems, local_sem):
        x, y, c = _mesh_pos()
        me, sibling = (x, y, c), (x, y, 1 - c)
        chips = _other_chips(x, y)

        def rows(px, py, pc):
            return out_ref.at[pl.ds((4 * px + 2 * py + pc) * m_per, m_per), :]

        def copy(k, block, to, src=None):
            return pltpu.make_async_remote_copy(
                src_ref=rows(*block) if src is None else src, dst_ref=rows(*block),
                send_sem=send_sems.at[k], recv_sem=recv_sems.at[k], device_id=to, device_id_type=MESH)

        mine = pltpu.make_async_copy(x_ref, rows(*me), local_sem)
        mine.start()
        first = [copy(0, me, sibling, src=x_ref)]
        first += [copy(1 + j, me, (*chip, c), src=x_ref) for j, chip in enumerate(chips)]
        for cp in first:
            cp.start()
        passed = [copy(4 + j, (*chip, c), sibling) for j, chip in enumerate(chips)]
        for j, chip in enumerate(chips):
            copy(1 + j, (*chip, c), me).wait_recv()
            passed[j].start()
        copy(0, sibling, me).wait_recv()
        for j, chip in enumerate(chips):
            copy(4 + j, (*chip, 1 - c), me).wait_recv()
        for cp in first + passed:
            cp.wait_send()
        mine.wait()

    return pl.pallas_call(
        body, out_shape=jax.ShapeDtypeStruct((N_DEV * m_per, n), blk.dtype),
        in_specs=[pl.BlockSpec(memory_space=pltpu.VMEM)], out_specs=pl.BlockSpec(memory_space=pltpu.VMEM),
        scratch_shapes=[pltpu.SemaphoreType.DMA((7,)), pltpu.SemaphoreType.DMA((7,)), pltpu.SemaphoreType.DMA],
        compiler_params=pltpu.CompilerParams(vmem_limit_bytes=VMEM_LIMIT_BYTES), name=name)(blk)


def _cast_into_slot(w, chip, after, *, name):
    rows, cols = w.shape
    tr = _row_tile(rows, cols, 2)

    def body(chip_ref, w_ref, after_ref, o_ref):
        o_ref[...] = w_ref[...].astype(BF16)

    gs = pltpu.PrefetchScalarGridSpec(
        num_scalar_prefetch=1, grid=(rows // tr,),
        in_specs=[pl.BlockSpec((tr, cols), lambda i, chip_ref: (i, 0)), pl.BlockSpec(memory_space=pl.ANY)],
        out_specs=pl.BlockSpec((None, tr, cols), lambda i, chip_ref: (chip_ref[0], i, 0)))
    return pl.pallas_call(
        body, grid_spec=gs, out_shape=jax.ShapeDtypeStruct((N_CHIPS, rows, cols), BF16),
        compiler_params=_cparams(("parallel",)), name=name)(chip, w, after)


def _sum_into_half(part, landed, chip, my_c, *, name):
    _, h, cols = part.shape
    tr = _row_tile(h, cols, 5)
    hb = h // tr

    def body(chip_ref, c_ref, p_ref, l_ref, o_ref):
        acc = p_ref[...].astype(F32)
        for j in range(3):
            acc = acc + l_ref[j].astype(F32)
        o_ref[...] = acc

    gs = pltpu.PrefetchScalarGridSpec(
        num_scalar_prefetch=2, grid=(hb,),
        in_specs=[pl.BlockSpec((None, tr, cols), lambda i, chip_ref, c_ref: (chip_ref[0], i, 0)),
                  pl.BlockSpec((3, tr, cols), lambda i, chip_ref, c_ref: (0, i, 0))],
        out_specs=pl.BlockSpec((tr, cols), lambda i, chip_ref, c_ref: (c_ref[0] * hb + i, 0)))
    return pl.pallas_call(
        body, grid_spec=gs, out_shape=jax.ShapeDtypeStruct((2 * h, cols), F32),
        compiler_params=_cparams(("parallel",)), name=name)(chip, my_c, part, landed)


class _NoComm:
    def __init__(self, big):
        self.big = big
        self.grads = {}

    def weight(self, name):
        return self.big[name]

    def mm_in(self, u, afters):
        return _mm(u, self.big["w_in"], mode="nn", out_dtype=BF16, name="mm_in")

    def mm_d_in(self, dproj):
        return _mm(dproj, self.big["w_in"], mode="nt", out_dtype=F32, name="mm_d_in")

    def carry(self, site, args=()):
        return None

    def done(self, site, carried, out=None):
        return out

    def grad(self, name, dw):
        self.grads[name] = dw

    def early_grads(self, early):
        self.early = early


def _gather_rows_carry(blk):
    m_per = blk.shape[0]
    sem = pltpu.SemaphoreType.DMA((7,))

    def copies(ins, outs, sems):
        send_sems, recv_sems, local_sem = sems
        x, y, c = _mesh_pos()
        me, sibling = (x, y, c), (x, y, 1 - c)
        chips = _other_chips(x, y)

        def rows(px, py, pc):
            return outs[0].at[pl.ds((4 * px + 2 * py + pc) * m_per, m_per), :]

        def copy(k, block, to, src=None):
            return pltpu.make_async_remote_copy(
                src_ref=rows(*block) if src is None else src, dst_ref=rows(*block),
                send_sem=send_sems.at[k], recv_sem=recv_sems.at[k], device_id=to, device_id_type=MESH)

        mine = pltpu.make_async_copy(ins[0], rows(*me), local_sem.at[0])
        first = [copy(0, me, sibling, src=ins[0])]
        first += [copy(1 + j, me, (*chip, c), src=ins[0]) for j, chip in enumerate(chips)]
        passed = [copy(4 + j, (*chip, c), sibling) for j, chip in enumerate(chips)]
        landed = [copy(1 + j, (*chip, c), me) for j, chip in enumerate(chips)]
        from_sibling = [copy(0, sibling, me)] + [copy(4 + j, (*chip, 1 - c), me) for j, chip in enumerate(chips)]
        return mine, first, passed, landed, from_sibling

    def start(ins, outs, sems):
        mine, first, _, _, _ = copies(ins, outs, sems)
        mine.start()
        for cp in first:
            cp.start()

    def finish(ins, outs, sems):
        mine, first, passed, landed, from_sibling = copies(ins, outs, sems)
        for arrived, onward in zip(landed, passed):
            arrived.wait_recv()
            onward.start()
        for arrived in from_sibling:
            arrived.wait_recv()
        for cp in first + passed:
            cp.wait_send()
        mine.wait()

    shape = jax.ShapeDtypeStruct((N_DEV * m_per, blk.shape[1]), blk.dtype)
    return _Carry([blk], [shape], {}, [sem, sem, pltpu.SemaphoreType.DMA((1,))], start, finish)


def _w_in_copies(own_ref, land_ref, send_sems, recv_sems):
    x, y, c = _mesh_pos()
    h = own_ref.shape[0] // 2
    return [pltpu.make_async_remote_copy(
        src_ref=own_ref.at[pl.ds(c * h, h), :], dst_ref=land_ref.at[1 + j, pl.ds(c * h, h), :],
        send_sem=send_sems[j], recv_sem=recv_sems[j], device_id=(*chip, c), device_id_type=MESH)
        for j, chip in enumerate(_other_chips(x, y))]


def _w_in_send(own, land, after):
    hbm = pl.BlockSpec(memory_space=pltpu.HBM)
    sem = pl.BlockSpec(memory_space=pltpu.SEMAPHORE)
    land_shape = land.shape

    def body(own_ref, land_ref, after_ref, s0, s1, s2, r0, r1, r2, own_thru, land_thru, token):
        for cp in _w_in_copies(own_ref, land_ref, (s0, s1, s2), (r0, r1, r2)):
            cp.start()
        token[...] = jnp.zeros_like(token)

    outs = pl.pallas_call(
        body, name="w_in_send",
        out_shape=(pltpu.SemaphoreType.DMA(()),) * 6 + (
            pltpu.HBM(own.shape, own.dtype), pltpu.HBM(land_shape, own.dtype), jax.ShapeDtypeStruct((8, LANES), F32)),
        in_specs=(hbm, hbm, pl.BlockSpec(memory_space=pl.ANY)),
        out_specs=(sem,) * 6 + (hbm, hbm, pl.BlockSpec(memory_space=pltpu.VMEM)),
        input_output_aliases={0: 6, 1: 7},
        compiler_params=pltpu.CompilerParams(has_side_effects=pltpu.SideEffectType.DATAFLOW_SIDE_EFFECTING),
    )(pltpu.with_memory_space_constraint(own, pltpu.HBM), pltpu.with_memory_space_constraint(land, pltpu.HBM), after)
    return outs[:6], outs[6], outs[7], outs[8]


def _w_in_wait(sems, own, land, afters):
    hbm = pl.BlockSpec(memory_space=pltpu.HBM)
    sem = pl.BlockSpec(memory_space=pltpu.SEMAPHORE)
    n_after = len(afters)

    def body(own_ref, land_ref, s0, s1, s2, r0, r1, r2, *rest):
        for cp in _w_in_copies(own_ref, land_ref, (s0, s1, s2), (r0, r1, r2)):
            cp.wait_send()
            cp.wait_recv()

    return pl.pallas_call(
        body, name="w_in_wait", out_shape=(pltpu.HBM(own.shape, own.dtype), pltpu.HBM(land.shape, land.dtype)),
        in_specs=(hbm, hbm) + (sem,) * 6 + (pl.BlockSpec(memory_space=pl.ANY),) * n_after, out_specs=(hbm, hbm),
        input_output_aliases={0: 0, 1: 1},
        compiler_params=pltpu.CompilerParams(has_side_effects=pltpu.SideEffectType.DATAFLOW_SIDE_EFFECTING),
    )(own, land, *sems, *afters)


def _exchange_copies(part_refs, land_refs, send_sems, recv_sems):
    x, y, c = _mesh_pos()
    cps = []
    for w, (part, land) in enumerate(zip(part_refs, land_refs)):
        for j, chip in enumerate(_other_chips(x, y)):
            cps.append(pltpu.make_async_remote_copy(
                src_ref=part.at[2 * chip[0] + chip[1]], dst_ref=land.at[j],
                send_sem=send_sems[3 * w + j], recv_sem=recv_sems[3 * w + j],
                device_id=(*chip, c), device_id_type=MESH))
    return cps


def _exchange_send(parts, through, *, name):
    n = len(parts)
    hbm = pl.BlockSpec(memory_space=pltpu.HBM)
    sem = pl.BlockSpec(memory_space=pltpu.SEMAPHORE)
    any_spec = pl.BlockSpec(memory_space=pl.ANY)
    land_shapes = [(3,) + p.shape[1:] for p in parts]

    def body(*refs):
        part_refs, land_refs = refs[:n], refs[n:2 * n]
        sems = refs[2 * n + 1:8 * n + 1]
        for cp in _exchange_copies(part_refs, land_refs, sems[:3 * n], sems[3 * n:]):
            cp.start()

    outs = pl.pallas_call(
        body, name=name,
        out_shape=(pltpu.SemaphoreType.DMA(()),) * (6 * n)
        + tuple(pltpu.HBM(p.shape, p.dtype) for p in parts)
        + tuple(pltpu.HBM(s, p.dtype) for s, p in zip(land_shapes, parts))
        + (jax.ShapeDtypeStruct(through.shape, through.dtype),),
        in_specs=(hbm,) * (2 * n) + (any_spec,), out_specs=(sem,) * (6 * n) + (hbm,) * (2 * n) + (any_spec,),
        input_output_aliases={i: 6 * n + i for i in range(2 * n + 1)},
        compiler_params=pltpu.CompilerParams(has_side_effects=pltpu.SideEffectType.DATAFLOW_SIDE_EFFECTING),
    )(*[pltpu.with_memory_space_constraint(p, pltpu.HBM) for p in parts],
      *[pltpu.with_memory_space_constraint(lax.empty(s, p.dtype), pltpu.HBM) for s, p in zip(land_shapes, parts)],
      through)
    return outs[:6 * n], outs[6 * n:7 * n], outs[7 * n:8 * n], outs[8 * n]


def _exchange_wait(sems, parts, lands, afters, *, name):
    n = len(parts)
    hbm = pl.BlockSpec(memory_space=pltpu.HBM)
    sem = pl.BlockSpec(memory_space=pltpu.SEMAPHORE)

    def body(*refs):
        part_refs, land_refs = refs[:n], refs[n:2 * n]
        sem_refs = refs[2 * n:8 * n]
        for cp in _exchange_copies(part_refs, land_refs, sem_refs[:3 * n], sem_refs[3 * n:]):
            cp.wait_send()
            cp.wait_recv()

    outs = pl.pallas_call(
        body, name=name,
        out_shape=tuple(pltpu.HBM(p.shape, p.dtype) for p in parts) + tuple(pltpu.HBM(l.shape, l.dtype) for l in lands),
        in_specs=(hbm,) * (2 * n) + (sem,) * (6 * n) + (pl.BlockSpec(memory_space=pl.ANY),) * len(afters),
        out_specs=(hbm,) * (2 * n), input_output_aliases={i: i for i in range(2 * n)},
        compiler_params=pltpu.CompilerParams(has_side_effects=pltpu.SideEffectType.DATAFLOW_SIDE_EFFECTING),
    )(*parts, *lands, *sems, *afters)
    return outs[:n], outs[n:]


def _join_copies(full_refs, send_sems, recv_sems):
    x, y, c = _mesh_pos()
    cps = []
    for w, full in enumerate(full_refs):
        h = full.shape[0] // 2
        mine = full.at[pl.ds(c * h, h), :]
        cps.append(pltpu.make_async_remote_copy(
            src_ref=mine, dst_ref=mine, send_sem=send_sems[w], recv_sem=recv_sems[w],
            device_id=(x, y, 1 - c), device_id_type=MESH))
    return cps


def _join_send(fulls, *, name):
    n = len(fulls)
    hbm = pl.BlockSpec(memory_space=pltpu.HBM)
    sem = pl.BlockSpec(memory_space=pltpu.SEMAPHORE)

    def body(*refs):
        sems = refs[n:3 * n]
        for cp in _join_copies(refs[:n], sems[:n], sems[n:]):
            cp.start()
        token = refs[-1]
        token[...] = jnp.zeros_like(token)

    outs = pl.pallas_call(
        body, name=name,
        out_shape=(pltpu.SemaphoreType.DMA(()),) * (2 * n) + tuple(pltpu.HBM(f.shape, f.dtype) for f in fulls)
        + (jax.ShapeDtypeStruct((SUBLANES, LANES), F32),),
        in_specs=(hbm,) * n,
        out_specs=(sem,) * (2 * n) + (hbm,) * n + (pl.BlockSpec(memory_space=pltpu.VMEM),),
        input_output_aliases={i: 2 * n + i for i in range(n)},
        compiler_params=pltpu.CompilerParams(has_side_effects=pltpu.SideEffectType.DATAFLOW_SIDE_EFFECTING),
    )(*[pltpu.with_memory_space_constraint(f, pltpu.HBM) for f in fulls])
    return outs[:2 * n], list(outs[2 * n:3 * n]), outs[3 * n]


def _join_wait(sems, fulls, afters, *, name):
    n = len(fulls)
    hbm = pl.BlockSpec(memory_space=pltpu.HBM)
    sem = pl.BlockSpec(memory_space=pltpu.SEMAPHORE)

    def body(*refs):
        sem_refs = refs[n:3 * n]
        for cp in _join_copies(refs[:n], sem_refs[:n], sem_refs[n:]):
            cp.wait_send()
            cp.wait_recv()

    outs = pl.pallas_call(
        body, name=name, out_shape=tuple(pltpu.HBM(f.shape, f.dtype) for f in fulls),
        in_specs=(hbm,) * n + (sem,) * (2 * n) + (pl.BlockSpec(memory_space=pl.ANY),) * len(afters),
        out_specs=(hbm,) * n, input_output_aliases={i: i for i in range(n)},
        compiler_params=pltpu.CompilerParams(has_side_effects=pltpu.SideEffectType.DATAFLOW_SIDE_EFFECTING),
    )(*fulls, *sems, *afters)
    return list(outs)


def _gather_ici_copies(buf_refs, send_sems, recv_sems):
    x, y, c = _mesh_pos()
    me_chip = 2 * x + y
    cps = []
    for w, buf in enumerate(buf_refs):
        h = buf.shape[1] // 2
        ref = buf.at[me_chip, pl.ds(c * h, h), :]
        for j, chip in enumerate(_other_chips(x, y)):
            cps.append(pltpu.make_async_remote_copy(
                src_ref=ref, dst_ref=ref, send_sem=send_sems[3 * w + j], recv_sem=recv_sems[3 * w + j],
                device_id=(*chip, c), device_id_type=MESH))
    return cps


def _gather_send(bufs, *, name):
    n = len(bufs)
    hbm = pl.BlockSpec(memory_space=pltpu.HBM)
    sem = pl.BlockSpec(memory_space=pltpu.SEMAPHORE)

    def body(*refs):
        sems = refs[n:7 * n]
        for cp in _gather_ici_copies(refs[:n], sems[:3 * n], sems[3 * n:]):
            cp.start()

    outs = pl.pallas_call(
        body, name=name,
        out_shape=(pltpu.SemaphoreType.DMA(()),) * (6 * n) + tuple(pltpu.HBM(b.shape, b.dtype) for b in bufs),
        in_specs=(hbm,) * n, out_specs=(sem,) * (6 * n) + (hbm,) * n,
        input_output_aliases={i: 6 * n + i for i in range(n)},
        compiler_params=pltpu.CompilerParams(has_side_effects=pltpu.SideEffectType.DATAFLOW_SIDE_EFFECTING),
    )(*[pltpu.with_memory_space_constraint(b, pltpu.HBM) for b in bufs])
    send_sems, recv_sems = outs[:3 * n], outs[3 * n:6 * n]
    per_buf = [tuple(send_sems[3 * w:3 * w + 3]) + tuple(recv_sems[3 * w:3 * w + 3]) for w in range(n)]
    return per_buf, list(outs[6 * n:])


def _gather_wait(sems, bufs, afters, *, name):
    n = len(bufs)
    hbm = pl.BlockSpec(memory_space=pltpu.HBM)
    sem = pl.BlockSpec(memory_space=pltpu.SEMAPHORE)
    flat = [s for six in sems for s in six[:3]] + [s for six in sems for s in six[3:]]

    def body(*refs):
        sem_refs = refs[n:7 * n]
        for cp in _gather_ici_copies(refs[:n], sem_refs[:3 * n], sem_refs[3 * n:]):
            cp.wait_send()
            cp.wait_recv()

    outs = pl.pallas_call(
        body, name=name, out_shape=tuple(pltpu.HBM(b.shape, b.dtype) for b in bufs),
        in_specs=(hbm,) * n + (sem,) * (6 * n) + (pl.BlockSpec(memory_space=pl.ANY),) * len(afters),
        out_specs=(hbm,) * n, input_output_aliases={i: i for i in range(n)},
        compiler_params=pltpu.CompilerParams(has_side_effects=pltpu.SideEffectType.DATAFLOW_SIDE_EFFECTING),
    )(*bufs, *flat, *afters)
    return list(outs)


def _forward_abs_carry(bufs):
    n = len(bufs)
    sem = pltpu.SemaphoreType.DMA((3 * n,))

    def copies(outs, sems):
        send_sems, recv_sems = sems
        x, y, c = _mesh_pos()
        sends, recvs = [], []
        for w in range(n):
            h = bufs[w].shape[1] // 2
            for j, chip in enumerate(_other_chips(x, y)):
                slot = 2 * chip[0] + chip[1]
                mine = outs[w].at[slot, pl.ds(c * h, h), :]
                other = outs[w].at[slot, pl.ds((1 - c) * h, h), :]
                sends.append(pltpu.make_async_remote_copy(
                    src_ref=mine, dst_ref=mine, send_sem=send_sems.at[3 * w + j], recv_sem=recv_sems.at[3 * w + j],
                    device_id=(x, y, 1 - c), device_id_type=MESH))
                recvs.append(pltpu.make_async_remote_copy(
                    src_ref=other, dst_ref=other, send_sem=send_sems.at[3 * w + j], recv_sem=recv_sems.at[3 * w + j],
                    device_id=(x, y, c), device_id_type=MESH))
        return sends, recvs

    def start(ins, outs, sems):
        for cp in copies(outs, sems)[0]:
            cp.start()

    def finish(ins, outs, sems):
        sends, recvs = copies(outs, sems)
        for cp in recvs:
            cp.wait_recv()
        for cp in sends:
            cp.wait_send()

    shapes = [jax.ShapeDtypeStruct(b.shape, b.dtype) for b in bufs]
    return _Carry(bufs, shapes, {i: i for i in range(n)}, [sem, sem], start, finish)


def _forward_carry(land):
    n = land.shape[0] - 1
    h = land.shape[1] // 2
    sem = pltpu.SemaphoreType.DMA((n,))

    def copies(outs, sems):
        send_sems, recv_sems = sems
        x, y, c = _mesh_pos()
        sends, recvs = [], []
        for j in range(n):
            mine = outs[0].at[1 + j, pl.ds(c * h, h), :]
            other = outs[0].at[1 + j, pl.ds((1 - c) * h, h), :]
            sends.append(pltpu.make_async_remote_copy(
                src_ref=mine, dst_ref=mine, send_sem=send_sems.at[j], recv_sem=recv_sems.at[j],
                device_id=(x, y, 1 - c), device_id_type=MESH))
            recvs.append(pltpu.make_async_remote_copy(
                src_ref=other, dst_ref=other, send_sem=send_sems.at[j], recv_sem=recv_sems.at[j],
                device_id=(x, y, c), device_id_type=MESH))
        return sends, recvs

    def start(ins, outs, sems):
        for cp in copies(outs, sems)[0]:
            cp.start()

    def finish(ins, outs, sems):
        sends, recvs = copies(outs, sems)
        for cp in recvs:
            cp.wait_recv()
        for cp in sends:
            cp.wait_send()

    return _Carry([land], [jax.ShapeDtypeStruct(land.shape, land.dtype)], {0: 0}, [sem, sem], start, finish)


def _swap_carry(dws):
    n = len(dws)
    sem = pltpu.SemaphoreType.DMA((n,))

    def copies(ins, outs, sems):
        send_sems, recv_sems = sems
        x, y, c = _mesh_pos()
        cps = []
        for w in range(n):
            h = dws[w].shape[1] // 2
            cps.append(pltpu.make_async_remote_copy(
                src_ref=ins[w].at[:, pl.ds((1 - c) * h, h), :], dst_ref=outs[w],
                send_sem=send_sems.at[w], recv_sem=recv_sems.at[w],
                device_id=(x, y, 1 - c), device_id_type=MESH))
        return cps

    def start(ins, outs, sems):
        for cp in copies(ins, outs, sems):
            cp.start()

    def finish(ins, outs, sems):
        for cp in copies(ins, outs, sems):
            cp.wait()

    shapes = [jax.ShapeDtypeStruct((s.shape[0], s.shape[1] // 2, s.shape[2]), s.dtype) for s in dws]
    return _Carry(dws, shapes, {}, [sem, sem], start, finish)


def _merge_carries(carries):
    if len(carries) == 1:
        return carries[0]
    inputs, out_shapes, sem_shapes, aliases, spans = [], [], [], {}, []
    for cy in carries:
        i0, o0, s0 = len(inputs), len(out_shapes), len(sem_shapes)
        aliases.update({i0 + i: o0 + o for i, o in cy.aliases.items()})
        inputs += cy.inputs
        out_shapes += cy.out_shapes
        sem_shapes += cy.sem_shapes
        spans.append((slice(i0, len(inputs)), slice(o0, len(out_shapes)), slice(s0, len(sem_shapes))))

    def start(ins, outs, sems):
        for cy, (si, so, ss) in zip(carries, spans):
            cy.start(ins[si], outs[so], sems[ss])

    def finish(ins, outs, sems):
        for cy, (si, so, ss) in zip(carries, spans):
            cy.finish(ins[si], outs[so], sems[ss])

    return _Carry(inputs, out_shapes, aliases, sem_shapes, start, finish)


class _MeshComm:
    FORWARD_AT = {
        "conv_fwd": ["w_conv_out", "w_glu_a", "w_glu_b", "w_out"],
        "mm_out": ["w_ff1"],
        "mm_ff1": ["w_ff2"],
    }
    SWAP_AT = {
        "mm_d_ff2": ["w_ff2"],
        "mm_d_ff1": ["w_ff1"],
        "conv_bwd": ["w_out", "w_glu_a", "w_glu_b", "w_conv_out"],
        "mm_d_in": ["w_in"],
    }
    EARLY_AT = "mm_dw_in"

    def __init__(self, shards, pos, chip, my_c):
        self.pos = pos
        self.chip = chip
        self.my_c = my_c
        self.shards = shards
        self.w_in_own, self.w_in_rel = _cast_bf16(shards["w_in"], name="cast_w_in")
        self.raw = {}
        self.flights = []
        self.halves = {}
        self.pending = {}

    def weight(self, name):
        g = self.bufs[name]
        return g.reshape(g.shape[0] * g.shape[1], g.shape[2]) if name in ROW_SHARDED else g

    def _slot_ids(self):
        x, y, _ = self.pos
        ids = [2 * x + y] + [2 * cx + cy for cx, cy in _other_chips(x, y)]
        return jnp.stack(ids).astype(jnp.int32)

    def start_w_in(self, after):
        *self.w_in_flight, token = _w_in_send(self.w_in_own, self.w_in_rel, after)
        order = [n for names in self.FORWARD_AT.values() for n in names]
        casts = [_cast_into_slot(self.shards[n], self.chip, token, name="cast_" + n) for n in order]
        sems, bufs = _gather_send(casts, name="gather_send")
        self.bufs = dict(zip(order, bufs))
        self.gather_sems = dict(zip(order, sems))
        return token

    def mm_in(self, u, afters):
        ids = self._slot_ids()
        sems, own, land = self.w_in_flight
        proj = _mm_slots(u, own[None], ids[0:1], None, name="mm_in_own")
        own, land = _w_in_wait(sems, own, land, [proj] + list(self.bufs.values()) + list(afters))
        land, = _run_carry(_forward_carry(land), name="forward_w_in")
        proj = _mm_slots(u, land, ids[1:4], proj, name="mm_in_rest", first=1)
        self.w_in_rel = land
        return proj

    def _add_and_send(self, names, landed, site, through):
        parts = [_add_half(self.raw.pop(n), l1, self.my_c, name="add_half_" + n) for n, l1 in zip(names, landed)]
        sems, parts, lands, through = _exchange_send(parts, through, name="exchange_send_" + site)
        self.flights.append((names, sems, parts, lands))
        return through

    def mm_d_in(self, dproj):
        du, carried = _mm(dproj, self.w_in_rel, mode="nt", out_dtype=F32, name="mm_d_in",
                          a_slots=self._slot_ids(), carry=self.carry("mm_d_in"))
        return self.done("mm_d_in", carried, du)

    def early_grads(self, early):
        self.early = early

    def carry(self, site, args=()):
        jobs = []
        if site in self.FORWARD_AT:
            names = self.FORWARD_AT[site]
            landed = _gather_wait([self.gather_sems.pop(n) for n in names], [self.bufs[n] for n in names],
                                  [args[0]], name="gather_wait_" + site)
            jobs.append(("forward", names, _forward_abs_carry(landed)))
        if site == self.EARLY_AT:
            flat, self.early_offs = _pack(list(self.early.values()))
            jobs.append(("early", None, _gather_rows_carry(flat.reshape(-1, PACK_COLS))))
        if site in self.SWAP_AT:
            names = self.SWAP_AT[site]
            jobs.append(("swap", names, _swap_carry([self.raw[n] for n in names])))
        if not jobs:
            return None
        self.pending[site] = jobs
        return _merge_carries([job[2] for job in jobs])

    def done(self, site, carried, out=None):
        pos = 0
        for kind, items, carry in self.pending.pop(site):
            outs = carried[pos:pos + len(carry.out_shapes)]
            pos += len(carry.out_shapes)
            if kind == "early":
                self.early_all = outs[0]
            elif kind == "forward":
                self.bufs.update(zip(items, outs))
            elif isinstance(out, (list, tuple)):
                out = [self._add_and_send(items, outs, site, out[0])] + list(out[1:])
            else:
                out = self._add_and_send(items, outs, site, out)
        return out

    def grad(self, name, dw):
        if name in ROW_SHARDED:
            dw = dw.reshape(N_CHIPS, dw.shape[0] // N_CHIPS, dw.shape[1])
        self.raw[name] = dw

    def join_start(self, names, afters):
        for i, (group, sems, parts, lands) in enumerate(self.flights):
            parts, lands = _exchange_wait(sems, parts, lands, afters, name="exchange_wait_%d" % i)
            for n, part, land in zip(group, parts, lands):
                self.halves[n] = _sum_into_half(part, land, self.chip, self.my_c, name="sum_chips_" + n)
        self.flights = []
        sems, fulls, token = _join_send([self.halves.pop(n) for n in names], name="join_send")
        self.join_flight = (names, sems, fulls)
        return token

    def join_finish(self, afters):
        names, sems, fulls = self.join_flight
        return dict(zip(names, _join_wait(sems, fulls, afters, name="join_wait")))


def _local_step(x, target, mod, small, comm):
    rows, d = x.shape
    cw = d // 2
    shift1, scale1, gate1, shift2, scale2, gate2 = mod
    _, _, bbr, bbi = small["s5_disc"]
    b_in, c_out, b_out, c_in, mults = _s5_operands(*small["s5_loglam"], bbr, bbi, small["c_re"], small["c_im"])
    wt = comm.weight

    def riding(site, fn, *args, **kwargs):
        carry = comm.carry(site, args)
        if carry is None:
            return fn(*args, **kwargs)
        out, carried = fn(*args, carry=carry, **kwargs)
        return comm.done(site, carried, out)

    u = _norm_mod(x, small["norm1_g"], scale1, shift1, name="norm1_fwd")
    proj = comm.mm_in(u, [*b_in, *c_out, *b_out, *c_in, *mults])
    sl, cv = riding("conv_fwd", _conv_fwd, proj, small["w_dw"], small["b_dw"], small["ln_g"], small["ln_b"], cw=cw)
    y_conv = _mm(sl, wt("w_conv_out"), mode="nn", out_dtype=BF16, name="mm_conv_out")
    yg, st_re, st_im = riding("s5_fwd", _s5_fwd, proj, small["d_skip"], b_in, c_out, mults, col0=2 * cw // LANES)
    ya = riding("mm_glu_a", _mm, yg, wt("w_glu_a"), mode="nn", out_dtype=BF16, name="mm_glu_a")
    yb = riding("mm_glu_b", _mm, yg, wt("w_glu_b"), mode="nn", out_dtype=BF16, name="mm_glu_b")
    merged = _merge_fwd(proj, y_conv, ya, yb, cw=cw)
    mo = riding("mm_out", _mm, merged, wt("w_out"), mode="nn", out_dtype=BF16, name="mm_out")
    h1, z = _res_norm(x, mo, gate1, small["norm2_g"], scale2, shift2)
    f1 = riding("mm_ff1", _mm, z, wt("w_ff1"), mode="nn", out_dtype=BF16, name="mm_ff1")
    ff = _mm(f1, wt("w_ff2"), mode="nn", out_dtype=BF16, name="mm_ff2", a_fn=_relu2_bf16)
    dh2, dff, loss, d_final_g, d_gate2 = _final_fwd_bwd(h1, ff, gate2, small["final_g"], target)

    comm.grad("w_ff2", _mm(f1, dff, mode="tn", out_dtype=BF16, name="mm_dw_ff2", a_fn=_relu2_bf16))
    df1 = riding("mm_d_ff2", _mm, dff, wt("w_ff2"), mode="nt", out_dtype=BF16, name="mm_d_ff2", extra=f1,
                 epi=lambda acc, f: acc * (2.0 * jnp.maximum(f.astype(F32), 0.0)))
    comm.grad("w_ff1", riding("mm_dw_ff1", _mm, z, df1, mode="tn", out_dtype=BF16, name="mm_dw_ff1",
                              out_gathered=True))
    dz = riding("mm_d_ff1", _mm, df1, wt("w_ff1"), mode="nt", out_dtype=F32, name="mm_d_ff1")
    dh1, d_shift2, d_scale2, d_norm2_g, dmo, d_gate1 = riding(
        "norm2_bwd", _norm_mod_bwd, dz, h1, dh2, small["norm2_g"], scale2, gate1, mo, name="norm2_bwd")
    comm.grad("w_out", riding("mm_dw_out", _mm, merged, dmo, mode="tn", out_dtype=BF16, name="mm_dw_out"))
    dmerged = riding("mm_d_out", _mm, dmo, wt("w_out"), mode="nt", out_dtype=BF16, name="mm_d_out")
    dproj, dy_conv, dya, dyb = riding("merge_bwd", _merge_bwd, dmerged, proj, y_conv, ya, yb, cw=cw)
    comm.grad("w_glu_a", _mm(yg, dya, mode="tn", out_dtype=BF16, name="mm_dw_glu_a", out_gathered=True))
    comm.grad("w_glu_b", _mm(yg, dyb, mode="tn", out_dtype=BF16, name="mm_dw_glu_b", out_gathered=True))
    dyg_a = _mm(dya, wt("w_glu_a"), mode="nt", out_dtype=F32, name="mm_d_glu_a")
    dyg = _mm(dyb, wt("w_glu_b"), mode="nt", out_dtype=F32, name="mm_d_glu_b", extra=dyg_a,
              epi=lambda acc, e: acc + e)
    comm.grad("w_conv_out", _mm(sl, dy_conv, mode="tn", out_dtype=BF16, name="mm_dw_conv_out", out_gathered=True))
    dsl = _mm(dy_conv, wt("w_conv_out"), mode="nt", out_dtype=F32, name="mm_d_conv_out")
    dcv, d_ln_g, d_ln_b = _ln_bwd(dsl, cv, small["ln_g"], small["ln_b"])
    dproj, d_w_dw, d_b_dw = riding("conv_bwd", _conv_bwd, dcv, proj, small["w_dw"], dproj, cw=cw)
    dproj, d_d_skip, dbr, dbi, dcr, dci, dlr, dli = riding(
        "s5_bwd", _s5_bwd, proj, dyg, small["d_skip"], (st_re, st_im), c_out, b_out, c_in, mults, dproj,
        col0=2 * cw // LANES)
    sw = lambda m: jnp.swapaxes(m, 1, 2)
    early = {
        "dmod_tail": jnp.concatenate([d_gate1, d_shift2, d_scale2, d_gate2], axis=1), "loss": loss[:, 0:1],
        "w_dw": d_w_dw, "b_dw": d_b_dw, "ln_g": d_ln_g, "ln_b": d_ln_b,
        "lam_re": dlr.reshape(-1, SSM_STATE), "lam_im": dli.reshape(-1, SSM_STATE),
        "bb_re": sw(_block_diag_extract(dbr, SSM_GROUP, SSM_STATE)),
        "bb_im": sw(_block_diag_extract(dbi, SSM_GROUP, SSM_STATE)),
        "c_re": sw(_block_diag_extract(dcr, SSM_STATE, SSM_GROUP)),
        "c_im": sw(_block_diag_extract(dci, SSM_STATE, SSM_GROUP)),
        "d_skip": d_d_skip, "norm2_g": d_norm2_g, "final_g": d_final_g,
    }
    comm.early_grads(early)
    comm.grad("w_in", riding("mm_dw_in", _mm, u, dproj, mode="tn", out_dtype=BF16, name="mm_dw_in",
                             out_gathered=True))
    du = comm.mm_d_in(dproj)
    grad_x, d_shift1, d_scale1, d_norm1_g = riding(
        "norm1_bwd", _norm_mod_bwd, du, x, dh1, small["norm1_g"], scale1, None, None, name="norm1_bwd")
    late ={"dmod_head": jnp.concatenate([d_shift1, d_scale1], axis=1), "norm1_g": d_norm1_g}
    return grad_x, early, late


WEIGHT_NAMES = ["w_ada", "b_ada", "norm1_g", "w_in", "w_dw", "b_dw", "ln_g", "ln_b", "w_conv_out", "a_re", "a_im",
                "log_dt", "b_re", "b_im", "c_re", "c_im", "d_skip", "w_glu_a", "w_glu_b", "w_out", "norm2_g",
                "w_ff1", "w_ff2", "final_g"]
BIG_NAMES = ["w_in", "w_conv_out", "w_glu_a", "w_glu_b", "w_out", "w_ff1", "w_ff2"]
ROW_SHARDED = ("w_out", "w_ff2")
PACK_COLS = 1024
PACK_TILE = SUBLANES * PACK_COLS


def _pack(arrays):
    flats = [a.reshape(-1) for a in arrays]
    offs = []
    total = 0
    for f in flats:
        offs.append(total)
        total += f.shape[0]
    pad = (-total) % PACK_TILE
    if pad:
        flats.append(jnp.zeros((pad,), F32))
    return jnp.concatenate(flats), offs


def _unpack(flat, offs, like):
    return [flat[o:o + a.size].reshape(a.shape) for o, a in zip(offs, like)]


def kernel(x, c, w_ada, b_ada, norm1_g, w_in, w_dw, b_dw, ln_g, ln_b, w_conv_out, a_re, a_im, log_dt, b_re, b_im, c_re, c_im, d_skip, w_glu_a, w_glu_b, w_out, norm2_g, w_ff1, w_ff2, final_g, loss_target, m_w_ada, m_b_ada, m_norm1_g, m_w_in, m_w_dw, m_b_dw, m_ln_g, m_ln_b, m_w_conv_out, m_a_re, m_a_im, m_log_dt, m_b_re, m_b_im, m_c_re, m_c_im, m_d_skip, m_w_glu_a, m_w_glu_b, m_w_out, m_norm2_g, m_w_ff1, m_w_ff2, m_final_g, v_w_ada, v_b_ada, v_norm1_g, v_w_in, v_w_dw, v_b_dw, v_ln_g, v_ln_b, v_w_conv_out, v_a_re, v_a_im, v_log_dt, v_b_re, v_b_im, v_c_re, v_c_im, v_d_skip, v_w_glu_a, v_w_glu_b, v_w_out, v_norm2_g, v_w_ff1, v_w_ff2, v_final_g):
    given = dict(locals())
    w = {n: given[n] for n in WEIGHT_NAMES}
    m = {n: given["m_" + n] for n in WEIGHT_NAMES}
    v = {n: given["v_" + n] for n in WEIGHT_NAMES}
    d = x.shape[2]
    xi, yi, ci = _mesh_pos()
    chip = 2 * xi + yi
    dev = 4 * xi + 2 * yi + ci
    my_c = jnp.reshape(ci, (1,)).astype(jnp.int32)
    chip_arr = jnp.reshape(chip, (1,)).astype(jnp.int32)

    comm = _MeshComm({n: w[n][0] for n in BIG_NAMES}, (xi, yi, ci), chip_arr, my_c)

    ndw = w_dw.shape[2]
    assert d // SUBLANES == ndw
    first = jnp.concatenate([c.reshape(SUBLANES, ndw), jnp.pad(w_dw[0], ((0, HALO - CONV_KERNEL), (0, 0)))])
    first_all = _gather_small(first, name="gather_c_w_dw").reshape(N_DEV, SUBLANES + HALO, ndw)
    c_all = first_all[:, :SUBLANES].reshape(N_DEV, d)
    taps = first_all.reshape(N_CHIPS, 2, SUBLANES + HALO, ndw)[:, 0, SUBLANES:SUBLANES + CONV_KERNEL]
    w_dw_full = jnp.moveaxis(taps, 0, 1).reshape(CONV_KERNEL, N_CHIPS * ndw)

    nmod = w_ada.shape[2]
    b_cols = lax.dynamic_slice(b_ada, (0, chip * nmod), (1, nmod))
    mod_part = _ada_fwd(c_all, w_ada[0], b_cols)
    mod_all = _gather_small(mod_part, name="gather_mod").reshape(N_CHIPS, 2, N_DEV, nmod)[:, 0]
    mod_full = jnp.moveaxis(mod_all, 0, 1).reshape(N_DEV, N_CHIPS * nmod)
    mod_row = lax.dynamic_slice(mod_full, (dev, 0), (1, N_CHIPS * nmod))
    mod = [mod_row[:, i * d:(i + 1) * d] for i in range(6)]

    token = comm.start_w_in(mod_row)
    log_dt_0 = log_dt[0] + token[0, 0]

    disc_in = (a_re[0], a_im[0], log_dt_0, b_re[0], b_im[0])
    disc, disc_vjp = jax.vjp(_s5_discretise, *disc_in)
    dt = jnp.exp(log_dt_0)[:, None]
    small = {"norm1_g": norm1_g, "w_dw": w_dw_full, "b_dw": b_dw, "ln_g": ln_g, "ln_b": ln_b,
             "c_re": c_re[0], "c_im": c_im[0], "d_skip": d_skip, "norm2_g": norm2_g,
             "final_g": final_g[None, :], "s5_disc": disc, "s5_loglam": (a_re[0] * dt, a_im[0] * dt)}

    grad_x, early, late = _local_step(x[0], loss_target[0], mod, small, comm)
    grads = {}

    early_all = comm.early_all.reshape(N_DEV, -1, PACK_COLS)
    early_sum = _sum_leading(early_all, name="sum_small_grads").reshape(-1)
    summed = dict(zip(early, _unpack(early_sum, comm.early_offs, list(early.values()))))
    flat, late_offs = _pack(list(late.values()))
    late_all = _gather_small(flat.reshape(-1, PACK_COLS), name="gather_late_grads").reshape(N_DEV, -1, PACK_COLS)
    late_sum = _sum_leading(late_all, name="sum_late_grads").reshape(-1)
    summed.update(zip(late, _unpack(late_sum, late_offs, list(late.values()))))
    head = late_all[:, :2 * d // PACK_COLS].reshape(N_DEV, 2 * d)
    tail = early_all[:, :4 * d // PACK_COLS].reshape(N_DEV, 4 * d)
    dmod_all = jnp.concatenate([head, tail], axis=1)

    grads["w_ada"] = _ada_bwd(c_all, lax.dynamic_slice(dmod_all, (0, chip * nmod), (N_DEV, nmod)))
    grads["b_ada"] = _sum_leading(dmod_all.reshape(N_DEV, SUBLANES, 6 * d // SUBLANES),
                                  name="sum_b_ada").reshape(1, 6 * d)
    da_re, da_im, dlog_dt, db_re, db_im = disc_vjp(
        (summed["lam_re"], summed["lam_im"], summed["bb_re"], summed["bb_im"]))
    grads.update({
        "norm1_g": summed["norm1_g"], "w_dw": lax.dynamic_slice(summed["w_dw"], (0, chip * ndw), (CONV_KERNEL, ndw)),
        "b_dw": summed["b_dw"], "ln_g": summed["ln_g"], "ln_b": summed["ln_b"],
        "a_re": da_re, "a_im": da_im, "log_dt": dlog_dt, "b_re": db_re, "b_im": db_im,
        "c_re": summed["c_re"], "c_im": summed["c_im"], "d_skip": summed["d_skip"],
        "norm2_g": summed["norm2_g"], "final_g": summed["final_g"],
    })

    delta, new_m, new_v = {}, {}, {}

    def adam_big(n, after=None):
        shp = w[n].shape
        two_d = lambda a: a.reshape(shp[1], shp[2])
        res = _adamw(two_d(w[n]), two_d(grads[n]), two_d(m[n]), two_d(v[n]), name="adamw_" + n, after=after,
                     with_grad=n in BIG_NAMES)
        delta[n], new_m[n], new_v[n] = [r.reshape(shp) for r in res[:3]]
        if n in BIG_NAMES:
            grads[n] = res[3]

    token = comm.join_start(BIG_NAMES, [late_all])
    adam_big("w_ada", token)
    grads.update(comm.join_finish([delta["w_ada"]]))
    for n in BIG_NAMES:
        adam_big(n)
    grads = {n: grads[n].reshape(w[n].shape) for n in WEIGHT_NAMES}
    rest = [n for n in WEIGHT_NAMES if n not in delta]
    as_2d = lambda a: a.reshape(1, -1) if a.ndim == 1 else a
    outs = _adamw_many(*[[as_2d(src[n]) for n in rest] for src in (w, grads, m, v)], name="adamw_small")
    for dst, arrays in zip((delta, new_m, new_v), outs):
        for n, a in zip(rest, arrays):
            dst[n] = a.reshape(w[n].shape)

    return (summed["loss"].reshape(()), grad_x[None], *[grads[n] for n in WEIGHT_NAMES],
            *[delta[n] for n in WEIGHT_NAMES], *[new_m[n] for n in WEIGHT_NAMES],
            *[new_v[n] for n in WEIGHT_NAMES])
```

```python
import math

import jax
import jax.numpy as jnp
from jax import lax
from jax.experimental import pallas as pl
from jax.experimental.pallas import tpu as pltpu

F32 = jnp.float32
BF16 = jnp.bfloat16
EPS = 1e-6
CONV_KERNEL = 31
SSM_GROUP = 16
SSM_STATE = 64
ADAM_LR = 0.001
ADAM_B1 = 0.9
ADAM_B2 = 0.999
ADAM_EPS = 1e-08
ADAM_WD = 0.01
ADAM_STEP = 10

N_CHIPS = 4
N_DEV = 8
VMEM_LIMIT_BYTES = 56 * 1024 * 1024
LANES = 128
SUBLANES = 8
HALO = 32
GROUPS_PER_BLOCK = LANES // SSM_GROUP
STATE_LANES = GROUPS_PER_BLOCK * SSM_STATE
MESH = pl.DeviceIdType.MESH


def _cparams(sem):
    return pltpu.CompilerParams(dimension_semantics=sem, vmem_limit_bytes=VMEM_LIMIT_BYTES)


def _pick(n, pref, mult=LANES):
    if n <= pref:
        return n
    best = None
    for d in range(mult, pref + 1, mult):
        if n % d == 0:
            best = d
    assert best is not None, (n, pref)
    return best


def _sigmoid(v):
    return 0.5 * jnp.tanh(0.5 * v) + 0.5


def _gelu_parts(v):
    k0 = math.sqrt(2.0 / math.pi)
    inner = k0 * (v + 0.044715 * v * v * v)
    t = jnp.tanh(inner)
    return k0, t


def _gelu(v):
    _, t = _gelu_parts(v)
    return 0.5 * v * (1.0 + t)


def _gelu_grad(v):
    k0, t = _gelu_parts(v)
    return 0.5 * (1.0 + t) + 0.5 * v * (1.0 - t * t) * k0 * (1.0 + 3.0 * 0.044715 * v * v)


def _relu2_bf16(a):
    t = jnp.maximum(a.astype(F32), 0.0)
    return (t * t).astype(BF16)


class _Carry:
    def __init__(self, inputs, out_shapes, aliases, sem_shapes, start, finish):
        self.inputs = list(inputs)
        self.out_shapes = list(out_shapes)
        self.aliases = dict(aliases)
        self.sem_shapes = list(sem_shapes)
        self.start = start
        self.finish = finish


def _call(body, *, grid, in_specs, out_specs, out_shape, scratch_shapes, semantics, name, args, carry=None,
          prefetch=(), aliases=None):
    n_in, n_out, n_scr, n_pf = len(in_specs), len(out_specs), len(scratch_shapes), len(prefetch)
    own_aliases = {n_pf + i: o for i, o in (aliases or {}).items()}
    if carry is None:
        gs = pltpu.PrefetchScalarGridSpec(
            num_scalar_prefetch=n_pf, grid=grid, in_specs=in_specs, out_specs=out_specs,
            scratch_shapes=scratch_shapes)
        outs = pl.pallas_call(
            body, grid_spec=gs, out_shape=out_shape, input_output_aliases=own_aliases,
            compiler_params=_cparams(semantics), name=name)(*prefetch, *args)
        return list(outs), []
    ci, co = len(carry.inputs), len(carry.out_shapes)

    def wrapped(*refs):
        pf, refs = refs[:n_pf], refs[n_pf:]
        ins, cins = refs[:n_in], refs[n_in:n_in + ci]
        p = n_in + ci
        outs, couts = refs[p:p + n_out], refs[p + n_out:p + n_out + co]
        p += n_out + co
        scr, csems = refs[p:p + n_scr], refs[p + n_scr:]
        first = pl.program_id(0) == 0
        last = pl.program_id(0) == grid[0] - 1
        for ax in range(1, len(grid)):
            first = jnp.logical_and(first, pl.program_id(ax) == 0)
            last = jnp.logical_and(last, pl.program_id(ax) == grid[ax] - 1)

        @pl.when(first)
        def _():
            carry.start(cins, couts, csems)

        body(*pf, *ins, *outs, *scr)

        @pl.when(last)
        def _():
            carry.finish(cins, couts, csems)

    any_spec = pl.BlockSpec(memory_space=pl.ANY)
    gs = pltpu.PrefetchScalarGridSpec(
        num_scalar_prefetch=n_pf, grid=grid, in_specs=list(in_specs) + [any_spec] * ci,
        out_specs=list(out_specs) + [any_spec] * co, scratch_shapes=list(scratch_shapes) + carry.sem_shapes)
    all_aliases = dict(own_aliases)
    all_aliases.update({n_pf + n_in + i: n_out + o for i, o in carry.aliases.items()})
    outs = pl.pallas_call(
        wrapped, grid_spec=gs, out_shape=list(out_shape) + carry.out_shapes, input_output_aliases=all_aliases,
        compiler_params=_cparams(("arbitrary",) * len(grid)), name=name)(*prefetch, *args, *carry.inputs)
    return list(outs[:n_out]), list(outs[n_out:])


def _run_carry(carry, *, name):
    ci = len(carry.inputs)

    def body(*refs):
        cins, couts, csems = refs[:ci], refs[ci:ci + len(carry.out_shapes)], refs[ci + len(carry.out_shapes):]
        carry.start(cins, couts, csems)
        carry.finish(cins, couts, csems)

    any_spec = pl.BlockSpec(memory_space=pl.ANY)
    outs = pl.pallas_call(
        body, in_specs=[any_spec] * ci, out_specs=[any_spec] * len(carry.out_shapes), out_shape=carry.out_shapes,
        scratch_shapes=carry.sem_shapes, input_output_aliases=carry.aliases, name=name)(*carry.inputs)
    return list(outs)


def _mm(a, b, *, mode, out_dtype, name, out_gathered=False, a_fn=None, epi=None, extra=None,
        bm_pref=1024, bn_pref=1024, bk_pref=2048, carry=None, a_slots=None):
    gathered = (b.ndim == 3)
    if mode == "nn":
        m, kdim = a.shape
        ns = b.shape[-1]
        n = ns * (N_CHIPS if gathered else 1)
        bm, bn, bk = _pick(m, bm_pref), _pick(ns, bn_pref), _pick(kdim, bk_pref)
        npb = ns // bn
        grid = (m // bm, n // bn, kdim // bk)
        a_spec = pl.BlockSpec((bm, bk), lambda i, j, k: (i, k))
        if gathered:
            b_spec = pl.BlockSpec((None, bk, bn), lambda i, j, k: (j // npb, k, j % npb))
        else:
            b_spec = pl.BlockSpec((bk, bn), lambda i, j, k: (k, j))
        o_spec = pl.BlockSpec((bm, bn), lambda i, j, k: (i, j))
        e_spec = pl.BlockSpec((bm, bn), lambda i, j, k: (i, j))
        out_shape = (m, n)
        acc_shape = (bm, bn)
        dims = (((1,), (0,)), ((), ()))
    elif mode == "nt":
        m = a.shape[0]
        kdim, ns = b.shape[-2], b.shape[-1]
        n = ns * (N_CHIPS if gathered else 1)
        assert a.shape[1] == n
        bm, bko, bnr = _pick(m, bm_pref), _pick(kdim, bn_pref), _pick(ns, bk_pref)
        npb = ns // bnr
        grid = (m // bm, kdim // bko, n // bnr)
        a_spec = pl.BlockSpec((bm, bnr), lambda i, j, k: (i, k))
        if gathered:
            b_spec = pl.BlockSpec((None, bko, bnr), lambda i, j, k: (k // npb, j, k % npb))
        else:
            b_spec = pl.BlockSpec((bko, bnr), lambda i, j, k: (j, k))
        o_spec = pl.BlockSpec((bm, bko), lambda i, j, k: (i, j))
        e_spec = pl.BlockSpec((bm, bko), lambda i, j, k: (i, j))
        if a_slots is not None:
            assert gathered and extra is None
            a_spec = pl.BlockSpec((bm, bnr), lambda i, j, k, s_ref: (i, s_ref[k // npb] * npb + k % npb))
            b_spec = pl.BlockSpec((None, bko, bnr), lambda i, j, k, s_ref: (k // npb, j, k % npb))
            o_spec = pl.BlockSpec((bm, bko), lambda i, j, k, s_ref: (i, j))
        out_shape = (m, kdim)
        acc_shape = (bm, bko)
        dims = (((1,), (1,)), ((), ()))
    else:
        m, kdim = a.shape
        n = b.shape[1]
        ns = n // N_CHIPS if out_gathered else n
        bmr, bko, bn = _pick(m, bk_pref), _pick(kdim, bm_pref), _pick(ns, bn_pref)
        npb = ns // bn
        grid = (kdim // bko, n // bn, m // bmr)
        a_spec = pl.BlockSpec((bmr, bko), lambda i, j, k: (k, i))
        b_spec = pl.BlockSpec((bmr, bn), lambda i, j, k: (k, j))
        if out_gathered:
            o_spec = pl.BlockSpec((None, bko, bn), lambda i, j, k: (j // npb, i, j % npb))
            out_shape = (N_CHIPS, kdim, ns)
        else:
            o_spec = pl.BlockSpec((bko, bn), lambda i, j, k: (i, j))
            out_shape = (kdim, n)
        e_spec = None
        acc_shape = (bko, bn)
        dims = (((0,), (0,)), ((), ()))
    nk = grid[2]

    def body(*refs):
        if a_slots is not None:
            refs = refs[1:]
        if extra is not None:
            a_ref, b_ref, e_ref, o_ref, acc = refs
        else:
            a_ref, b_ref, o_ref, acc = refs
            e_ref = None
        k = pl.program_id(2)
        av = a_ref[...]
        if a_fn is not None:
            av = a_fn(av)
        part = lax.dot_general(av, b_ref[...], dims, preferred_element_type=F32)

        def finish(r):
            if epi is not None:
                r = epi(r, e_ref[...])
            o_ref[...] = r.astype(o_ref.dtype)

        if nk == 1:
            finish(part)
            return

        @pl.when(k == 0)
        def _():
            acc[...] = part

        @pl.when(jnp.logical_and(k > 0, k < nk - 1))
        def _():
            acc[...] += part

        @pl.when(k == nk - 1)
        def _():
            finish(acc[...] + part)

    in_specs = [a_spec, b_spec]
    args = [a, b]
    if extra is not None:
        in_specs.append(e_spec)
        args.append(extra)
    outs, carried = _call(body, grid=grid, in_specs=in_specs, out_specs=[o_spec],
                          out_shape=[jax.ShapeDtypeStruct(out_shape, out_dtype)],
                          scratch_shapes=[pltpu.VMEM(acc_shape, F32)],
                          semantics=("parallel", "parallel", "arbitrary"), name=name, args=args, carry=carry,
                          prefetch=() if a_slots is None else (a_slots,))
    return outs[0] if carry is None else (outs[0], carried)


def _mm_slots(a, wbuf, slots, prev, *, name, carry=None, first=0):
    m, kdim = a.shape
    ns = wbuf.shape[2]
    bm, bn = _pick(m, 1024), _pick(ns, 1024)
    npb = ns // bn
    grid = (m // bm, slots.shape[0], npb)

    def body(s_ref, a_ref, b_ref, *rest):
        o_ref = rest[-1]
        o_ref[...] = _dot(a_ref[...], b_ref[...]).astype(o_ref.dtype)

    in_specs = [pl.BlockSpec((bm, kdim), lambda i, s, j, s_ref: (i, 0)),
                pl.BlockSpec((None, kdim, bn), lambda i, s, j, s_ref: (first + s, 0, j))]
    args = [a, wbuf]
    aliases = None
    if prev is not None:
        in_specs.append(pl.BlockSpec(memory_space=pl.ANY))
        args.append(prev)
        aliases = {2: 0}
    outs, carried = _call(
        body, grid=grid, in_specs=in_specs,
        out_specs=[pl.BlockSpec((bm, bn), lambda i, s, j, s_ref: (i, s_ref[s] * npb + j))],
        out_shape=[jax.ShapeDtypeStruct((m, N_CHIPS * ns), BF16)], scratch_shapes=[],
        semantics=("parallel", "arbitrary", "arbitrary"), name=name, args=args, carry=carry,
        prefetch=(slots,), aliases=aliases)
    return outs[0] if carry is None else (outs[0], carried)


def _row_tile(rows, cols, n_arrays):
    budget = VMEM_LIMIT_BYTES // 2
    cap = min(512, budget // (n_arrays * 2 * cols * 4))
    for t in range(cap - cap % SUBLANES, 0, -SUBLANES):
        if rows % t == 0:
            return t
    return rows


def _norm_mod(x, g, scale, shift, *, name):
    rows, d = x.shape
    tr = _row_tile(rows, d, 3)

    def body(x_ref, g_ref, sc_ref, sh_ref, o_ref):
        xv = x_ref[...]
        r = lax.rsqrt(jnp.mean(xv * xv, axis=-1, keepdims=True) + EPS)
        o_ref[...] = ((xv * r * g_ref[...]) * (1.0 + sc_ref[...]) + sh_ref[...]).astype(o_ref.dtype)

    row = pl.BlockSpec((tr, d), lambda i: (i, 0))
    vec = pl.BlockSpec((1, d), lambda i: (0, 0))
    return pl.pallas_call(
        body, grid=(rows // tr,), in_specs=[row, vec, vec, vec], out_specs=row,
        out_shape=jax.ShapeDtypeStruct((rows, d), BF16),
        compiler_params=_cparams(("parallel",)), name=name)(x, g, scale, shift)


CONV_CHUNK = 4 * SUBLANES


def _shifted_copies(buf, n):
    for r in range(1, SUBLANES):
        buf[r, pl.ds(0, n - SUBLANES), :] = buf[0, pl.ds(r, n - SUBLANES), :]


def _conv_fwd(proj, w_dw, b_dw, ln_g, ln_b, *, cw, carry=None):
    rows = proj.shape[0]
    tt = _pick(rows, 256, HALO)
    hb = tt // HALO

    def body(a_ref, g_ref, ha_ref, hg_ref, w_ref, b_ref, lg_ref, lb_ref, sl_ref, cv_ref, vs):
        i = pl.program_id(0)
        hv = ha_ref[...].astype(F32) * _sigmoid(hg_ref[...].astype(F32))
        vs[0, pl.ds(0, HALO), :] = jnp.where(i == 0, 0.0, hv)
        vs[0, pl.ds(HALO, tt), :] = a_ref[...].astype(F32) * _sigmoid(g_ref[...].astype(F32))
        _shifted_copies(vs, HALO + tt)

        def chunk(ci, carry):
            r0 = pl.multiple_of(ci * CONV_CHUNK, CONV_CHUNK)
            acc = jnp.broadcast_to(b_ref[...], (CONV_CHUNK, cw))
            for k in range(CONV_KERNEL):
                q, r = divmod(HALO - (CONV_KERNEL - 1) + k, SUBLANES)
                acc = acc + w_ref[pl.ds(k, 1), :] * vs[r, pl.ds(r0 + q * SUBLANES, CONV_CHUNK), :]
            cv_ref[pl.ds(r0, CONV_CHUNK), :] = acc
            return carry

        lax.fori_loop(0, tt // CONV_CHUNK, chunk, 0)
        acc = cv_ref[...]
        mu = jnp.mean(acc, axis=-1, keepdims=True)
        xc = acc - mu
        rstd = lax.rsqrt(jnp.mean(xc * xc, axis=-1, keepdims=True) + EPS)
        ln = xc * rstd * lg_ref[...] + lb_ref[...]
        sl_ref[...] = (ln * _sigmoid(ln)).astype(sl_ref.dtype)

    tile = lambda c: pl.BlockSpec((tt, cw), lambda i, c=c: (i, c))
    halo = lambda c: pl.BlockSpec((HALO, cw), lambda i, c=c: (jnp.maximum(i * hb - 1, 0), c))
    vec = pl.BlockSpec((1, cw), lambda i: (0, 0))
    outs, carried = _call(
        body, grid=(rows // tt,),
        in_specs=[tile(0), tile(1), halo(0), halo(1),
                  pl.BlockSpec((CONV_KERNEL, cw), lambda i: (0, 0)), vec, vec, vec],
        out_specs=[pl.BlockSpec((tt, cw), lambda i: (i, 0)), pl.BlockSpec((tt, cw), lambda i: (i, 0))],
        out_shape=[jax.ShapeDtypeStruct((rows, cw), BF16), jax.ShapeDtypeStruct((rows, cw), F32)],
        scratch_shapes=[pltpu.VMEM((SUBLANES, HALO + tt, cw), F32)],
        semantics=("parallel",), name="conv_fwd", args=[proj, proj, proj, proj, w_dw, b_dw, ln_g, ln_b],
        carry=carry)
    return outs if carry is None else (outs, carried)


def _ln_bwd(dsl, cv, ln_g, ln_b):
    rows, cw = cv.shape
    tr = _row_tile(rows, cw, 3)

    def body(d_ref, cv_ref, lg_ref, lb_ref, o_ref, dg_ref, db_ref):
        i = pl.program_id(0)

        @pl.when(i == 0)
        def _():
            dg_ref[...] = jnp.zeros_like(dg_ref)
            db_ref[...] = jnp.zeros_like(db_ref)

        x = cv_ref[...]
        mu = jnp.mean(x, axis=-1, keepdims=True)
        xc = x - mu
        rstd = lax.rsqrt(jnp.mean(xc * xc, axis=-1, keepdims=True) + EPS)
        xh = xc * rstd
        ln = xh * lg_ref[...] + lb_ref[...]
        s = _sigmoid(ln)
        dln = d_ref[...].astype(F32) * (s * (1.0 + ln * (1.0 - s)))
        dg_ref[...] += jnp.sum(dln * xh, axis=0, keepdims=True)
        db_ref[...] += jnp.sum(dln, axis=0, keepdims=True)
        dxh = dln * lg_ref[...]
        m1 = jnp.mean(dxh, axis=-1, keepdims=True)
        m2 = jnp.mean(dxh * xh, axis=-1, keepdims=True)
        o_ref[...] = rstd * (dxh - m1 - xh * m2)

    row = pl.BlockSpec((tr, cw), lambda i: (i, 0))
    vec = pl.BlockSpec((1, cw), lambda i: (0, 0))
    return pl.pallas_call(
        body, grid=(rows // tr,), in_specs=[row, row, vec, vec], out_specs=[row, vec, vec],
        out_shape=[jax.ShapeDtypeStruct((rows, cw), F32), jax.ShapeDtypeStruct((1, cw), F32),
                   jax.ShapeDtypeStruct((1, cw), F32)],
        compiler_params=_cparams(("arbitrary",)), name="ln_bwd")(dsl, cv, ln_g, ln_b)


def _conv_bwd(dcv, proj, w_dw, dproj, *, cw, carry=None):
    rows = proj.shape[0]
    tt = _pick(rows, 256, HALO)
    hb = tt // HALO
    nt = rows // tt
    taps = CONV_KERNEL

    def body(d_ref, dn_ref, a_ref, g_ref, ha_ref, hg_ref, w_ref, dproj_ref, o_ref, dw_ref, db_ref, vs, ds):
        i = pl.program_id(0)

        @pl.when(i == 0)
        def _():
            dw_ref[...] = jnp.zeros_like(dw_ref)
            db_ref[...] = jnp.zeros_like(db_ref)

        hv = ha_ref[...].astype(F32) * _sigmoid(hg_ref[...].astype(F32))
        vs[0, pl.ds(0, HALO), :] = jnp.where(i == 0, 0.0, hv)
        vs[0, pl.ds(HALO, tt), :] = a_ref[...].astype(F32) * _sigmoid(g_ref[...].astype(F32))
        _shifted_copies(vs, HALO + tt)
        ds[0, pl.ds(0, tt), :] = d_ref[...]
        ds[0, pl.ds(tt, HALO), :] = jnp.where(i == nt - 1, 0.0, dn_ref[...])
        _shifted_copies(ds, tt + HALO)
        db_ref[...] += jnp.sum(d_ref[...], axis=0, keepdims=True)
        for k in range(taps):
            q, r = divmod(HALO - (taps - 1) + k, SUBLANES)
            dw_ref[pl.ds(k, 1), :] += jnp.sum(d_ref[...] * vs[r, pl.ds(q * SUBLANES, tt), :], axis=0, keepdims=True)

        def chunk(ci, carry):
            r0 = pl.multiple_of(ci * CONV_CHUNK, CONV_CHUNK)
            dv = jnp.zeros((CONV_CHUNK, cw), F32)
            for k in range(taps):
                q, r = divmod(taps - 1 - k, SUBLANES)
                dv = dv + w_ref[pl.ds(k, 1), :] * ds[r, pl.ds(r0 + q * SUBLANES, CONV_CHUNK), :]
            av = a_ref[pl.ds(r0, CONV_CHUNK), :].astype(F32)
            sg = _sigmoid(g_ref[pl.ds(r0, CONV_CHUNK), :].astype(F32))
            o_ref[pl.ds(r0, CONV_CHUNK), pl.ds(0, cw)] = (dv * sg).astype(o_ref.dtype)
            o_ref[pl.ds(r0, CONV_CHUNK), pl.ds(cw, cw)] = (dv * av * sg * (1.0 - sg)).astype(o_ref.dtype)
            return carry

        lax.fori_loop(0, tt // CONV_CHUNK, chunk, 0)

    tile = lambda c: pl.BlockSpec((tt, cw), lambda i, c=c: (i, c))
    halo = lambda c: pl.BlockSpec((HALO, cw), lambda i, c=c: (jnp.maximum(i * hb - 1, 0), c))
    nxt = pl.BlockSpec((HALO, cw), lambda i: (jnp.minimum((i + 1) * hb, nt * hb - 1), 0))
    outs, carried = _call(
        body, grid=(nt,),
        in_specs=[pl.BlockSpec((tt, cw), lambda i: (i, 0)), nxt, tile(0), tile(1), halo(0), halo(1),
                  pl.BlockSpec((taps, cw), lambda i: (0, 0)), pl.BlockSpec(memory_space=pl.ANY)],
        out_specs=[pl.BlockSpec((tt, 2 * cw), lambda i: (i, 0)),
                   pl.BlockSpec((taps, cw), lambda i: (0, 0)), pl.BlockSpec((1, cw), lambda i: (0, 0))],
        out_shape=[jax.ShapeDtypeStruct(dproj.shape, dproj.dtype), jax.ShapeDtypeStruct((taps, cw), F32),
                   jax.ShapeDtypeStruct((1, cw), F32)],
        scratch_shapes=[pltpu.VMEM((SUBLANES, HALO + tt, cw), F32), pltpu.VMEM((SUBLANES, tt + HALO, cw), F32)],
        semantics=("arbitrary",), name="conv_bwd", args=[dcv, dcv, proj, proj, proj, proj, w_dw, dproj],
        carry=carry, aliases={7: 0})
    return outs if carry is None else (outs, carried)


def _merge_fwd(proj, y_conv, ya, yb, *, cw):
    rows = proj.shape[0]
    tr = _row_tile(rows, cw, 4)

    def body(gc_ref, gs_ref, yc_ref, ya_ref, yb_ref, o_ref):
        ys = ya_ref[...].astype(F32) * _sigmoid(yb_ref[...].astype(F32))
        o_ref[...] = (_sigmoid(gc_ref[...].astype(F32)) * yc_ref[...].astype(F32)
                      + _sigmoid(gs_ref[...].astype(F32)) * ys).astype(o_ref.dtype)

    blk = lambda off: pl.BlockSpec((tr, cw), lambda i, h, off=off: (i, off + h))
    return pl.pallas_call(
        body, grid=(rows // tr, 2), in_specs=[blk(3), blk(5), blk(0), blk(0), blk(0)], out_specs=blk(0),
        out_shape=jax.ShapeDtypeStruct((rows, 2 * cw), BF16),
        compiler_params=_cparams(("parallel", "parallel")), name="merge_fwd")(proj, proj, y_conv, ya, yb)


def _merge_bwd(dmerged, proj, y_conv, ya, yb, *, cw, carry=None):
    rows = proj.shape[0]
    tr = _row_tile(rows, cw, 3)

    def body(d_ref, g_ref, yc_ref, ya_ref, yb_ref, dg_ref, dyc_ref, dya_ref, dyb_ref):
        q = pl.program_id(1)
        d = d_ref[...].astype(F32)
        sg = _sigmoid(g_ref[...].astype(F32))

        @pl.when(q < 2)
        def _():
            dg_ref[...] = (d * yc_ref[...].astype(F32) * sg * (1.0 - sg)).astype(dg_ref.dtype)
            dyc_ref[...] = (d * sg).astype(dyc_ref.dtype)

        @pl.when(q >= 2)
        def _():
            sb = _sigmoid(yb_ref[...].astype(F32))
            yav = ya_ref[...].astype(F32)
            dg_ref[...] = (d * (yav * sb) * sg * (1.0 - sg)).astype(dg_ref.dtype)
            dys = d * sg
            dya_ref[...] = (dys * sb).astype(dya_ref.dtype)
            dyb_ref[...] = (dys * yav * sb * (1.0 - sb)).astype(dyb_ref.dtype)

    spec = lambda f: pl.BlockSpec((tr, cw), lambda i, q, f=f: (i, f(q)))
    conv_half = spec(lambda q: jnp.minimum(q, 1))
    ssm_half = spec(lambda q: jnp.maximum(q - 2, 0))
    o2 = jax.ShapeDtypeStruct((rows, 2 * cw), BF16)
    outs, carried = _call(
        body, grid=(rows // tr, 4),
        in_specs=[spec(lambda q: q % 2), spec(lambda q: 3 + q), conv_half, ssm_half, ssm_half],
        out_specs=[spec(lambda q: 3 + q), conv_half, ssm_half, ssm_half],
        out_shape=[jax.ShapeDtypeStruct((rows, 7 * cw), BF16), o2, o2, o2], scratch_shapes=[],
        semantics=("parallel", "arbitrary"), name="merge_bwd", args=[dmerged, proj, y_conv, ya, yb],
        carry=carry)
    return outs if carry is None else (outs, carried)


def _res_norm(x, mo, gate, g, scale, shift):
    rows, d = x.shape
    tr = _row_tile(rows, d, 4)

    def body(x_ref, mo_ref, gt_ref, g_ref, sc_ref, sh_ref, h_ref, z_ref):
        h = x_ref[...] + gt_ref[...] * mo_ref[...].astype(F32)
        h_ref[...] = h
        r = lax.rsqrt(jnp.mean(h * h, axis=-1, keepdims=True) + EPS)
        z_ref[...] = ((h * r * g_ref[...]) * (1.0 + sc_ref[...]) + sh_ref[...]).astype(z_ref.dtype)

    row = pl.BlockSpec((tr, d), lambda i: (i, 0))
    vec = pl.BlockSpec((1, d), lambda i: (0, 0))
    return pl.pallas_call(
        body, grid=(rows // tr,), in_specs=[row, row, vec, vec, vec, vec], out_specs=[row, row],
        out_shape=[jax.ShapeDtypeStruct((rows, d), F32), jax.ShapeDtypeStruct((rows, d), BF16)],
        compiler_params=_cparams(("parallel",)), name="res_norm")(x, mo, gate, g, scale, shift)


def _final_fwd_bwd(h1, ff, gate2, final_g, target):
    rows, d = h1.shape
    tr = _row_tile(rows, d, 5)

    def body(h_ref, ff_ref, gt_ref, fg_ref, t_ref, dh_ref, dff_ref, loss_ref, dfg_ref, dgt_ref):
        i = pl.program_id(0)

        @pl.when(i == 0)
        def _():
            loss_ref[...] = jnp.zeros_like(loss_ref)
            dfg_ref[...] = jnp.zeros_like(dfg_ref)
            dgt_ref[...] = jnp.zeros_like(dgt_ref)

        ffv = ff_ref[...].astype(F32)
        h2 = h_ref[...] + gt_ref[...] * ffv
        r = lax.rsqrt(jnp.mean(h2 * h2, axis=-1, keepdims=True) + EPS)
        y = h2 * r
        e = y * fg_ref[...] - t_ref[...]
        loss_ref[...] += 0.5 * jnp.sum(jnp.mean(e * e, axis=-1, keepdims=True))
        dout = e * (1.0 / d)
        dfg_ref[...] += jnp.sum(dout * y, axis=0, keepdims=True)
        dy = dout * fg_ref[...]
        dh2 = r * (dy - y * jnp.mean(dy * y, axis=-1, keepdims=True))
        dh_ref[...] = dh2
        dgt_ref[...] += jnp.sum(dh2 * ffv, axis=0, keepdims=True)
        dff_ref[...] = (dh2 * gt_ref[...]).astype(dff_ref.dtype)

    row = pl.BlockSpec((tr, d), lambda i: (i, 0))
    vec = pl.BlockSpec((1, d), lambda i: (0, 0))
    return pl.pallas_call(
        body, grid=(rows // tr,), in_specs=[row, row, vec, vec, row],
        out_specs=[row, row, pl.BlockSpec((1, LANES), lambda i: (0, 0)), vec, vec],
        out_shape=[jax.ShapeDtypeStruct((rows, d), F32), jax.ShapeDtypeStruct((rows, d), BF16),
                   jax.ShapeDtypeStruct((1, LANES), F32), jax.ShapeDtypeStruct((1, d), F32),
                   jax.ShapeDtypeStruct((1, d), F32)],
        compiler_params=_cparams(("arbitrary",)), name="final_fwd_bwd")(h1, ff, gate2, final_g, target)


def _norm_mod_bwd(dz, hin, dres, g, scale, gate, mo, *, name, carry=None):
    rows, d = hin.shape
    with_gate = gate is not None
    tr = _row_tile(rows, d, 6)

    def body(*refs):
        if with_gate:
            (dz_ref, h_ref, dr_ref, g_ref, sc_ref, gt_ref, mo_ref,
             dh_ref, dsh_ref, dsc_ref, dg_ref, dmo_ref, dgt_ref) = refs
        else:
            dz_ref, h_ref, dr_ref, g_ref, sc_ref, dh_ref, dsh_ref, dsc_ref, dg_ref = refs
        i = pl.program_id(0)

        @pl.when(i == 0)
        def _():
            dsh_ref[...] = jnp.zeros_like(dsh_ref)
            dsc_ref[...] = jnp.zeros_like(dsc_ref)
            dg_ref[...] = jnp.zeros_like(dg_ref)
            if with_gate:
                dgt_ref[...] = jnp.zeros_like(dgt_ref)

        dzv = dz_ref[...].astype(F32)
        h = h_ref[...]
        r = lax.rsqrt(jnp.mean(h * h, axis=-1, keepdims=True) + EPS)
        y = h * r
        dsh_ref[...] += jnp.sum(dzv, axis=0, keepdims=True)
        dsc_ref[...] += jnp.sum(dzv * (y * g_ref[...]), axis=0, keepdims=True)
        dn = dzv * (1.0 + sc_ref[...])
        dg_ref[...] += jnp.sum(dn * y, axis=0, keepdims=True)
        dy = dn * g_ref[...]
        dh = dr_ref[...] + r * (dy - y * jnp.mean(dy * y, axis=-1, keepdims=True))
        dh_ref[...] = dh
        if with_gate:
            dmo_ref[...] = (dh * gt_ref[...]).astype(dmo_ref.dtype)
            dgt_ref[...] += jnp.sum(dh * mo_ref[...].astype(F32), axis=0, keepdims=True)

    row = pl.BlockSpec((tr, d), lambda i: (i, 0))
    vec = pl.BlockSpec((1, d), lambda i: (0, 0))
    vshape = jax.ShapeDtypeStruct((1, d), F32)
    in_specs = [row, row, row, vec, vec]
    args = [dz, hin, dres, g, scale]
    out_specs = [row, vec, vec, vec]
    out_shape = [jax.ShapeDtypeStruct((rows, d), F32), vshape, vshape, vshape]
    if with_gate:
        in_specs += [vec, row]
        args += [gate, mo]
        out_specs += [row, vec]
        out_shape += [jax.ShapeDtypeStruct((rows, d), BF16), vshape]
    outs, carried = _call(
        body, grid=(rows // tr,), in_specs=in_specs, out_specs=out_specs, out_shape=out_shape,
        scratch_shapes=[], semantics=("arbitrary",), name=name, args=args, carry=carry)
    return outs if carry is None else (outs, carried)


def _s5_discretise(a_re, a_im, log_dt, b_re, b_im):
    dt = jnp.exp(log_dt)[:, None]
    er = jnp.exp(a_re * dt)
    lr = er * jnp.cos(a_im * dt)
    li = er * jnp.sin(a_im * dt)
    den = a_re * a_re + a_im * a_im
    cr = ((lr - 1.0) * a_re + li * a_im) / den
    ci = (li * a_re - (lr - 1.0) * a_im) / den
    bbr = cr[..., None] * b_re - ci[..., None] * b_im
    bbi = cr[..., None] * b_im + ci[..., None] * b_re
    return lr, li, bbr, bbi


def _block_diag(w):
    g, r, c = w.shape
    nb = g // GROUPS_PER_BLOCK
    eye = jnp.eye(GROUPS_PER_BLOCK, dtype=w.dtype)
    w5 = w.reshape(nb, GROUPS_PER_BLOCK, r, 1, c) * eye[None, :, None, :, None]
    return w5.reshape(nb, GROUPS_PER_BLOCK * r, GROUPS_PER_BLOCK * c)


def _block_diag_extract(m, r, c):
    nb = m.shape[0]
    m5 = m.reshape(nb, GROUPS_PER_BLOCK, r, GROUPS_PER_BLOCK, c)
    idx = jnp.arange(GROUPS_PER_BLOCK)
    d = m5[:, idx, :, idx, :]
    return jnp.moveaxis(d, 0, 1).reshape(nb * GROUPS_PER_BLOCK, r, c)


def _scan_multipliers(lr, li):
    power = jnp.arange(1, SUBLANES + 1, dtype=F32)[None, :, None]
    er = jnp.exp(power * lr)
    pr = er * jnp.cos(power * li)
    pi = er * jnp.sin(power * li)
    rows = jnp.arange(SUBLANES)[None, :, None]
    fr, fi, rr, ri = [], [], [], []
    for s in (1, 2, 4):
        mf = (rows >= s).astype(F32)
        mr = (rows <= SUBLANES - 1 - s).astype(F32)
        fr.append(mf * pr[:, s - 1:s, :])
        fi.append(mf * pi[:, s - 1:s, :])
        rr.append(mr * pr[:, s - 1:s, :])
        ri.append(mr * pi[:, s - 1:s, :])
    fr.append(pr)
    fi.append(pi)
    rr.append(pr[:, ::-1, :])
    ri.append(pi[:, ::-1, :])
    st = lambda xs: jnp.stack(xs, axis=1)
    return st(fr), st(fi), st(rr), st(ri)


def _scan_rows(sre, sim, mul_r, mul_i, n_groups, reverse):
    sgn = -1.0 if reverse else 1.0
    lanes = sre.shape[1]

    def step(k, carry):
        cr, ci = carry
        kk = (n_groups - 1 - k) if reverse else k
        r0 = pl.multiple_of(kk * SUBLANES, SUBLANES)
        xr = sre[pl.ds(r0, SUBLANES), :]
        xi = sim[pl.ds(r0, SUBLANES), :]
        for lvl, s in enumerate((1, 2, 4)):
            sh = (SUBLANES - s) if reverse else s
            nr = pltpu.roll(xr, sh, 0)
            ni = pltpu.roll(xi, sh, 0)
            mr = mul_r[lvl]
            mi = mul_i[lvl] * sgn
            xr, xi = xr + mr * nr - mi * ni, xi + mr * ni + mi * nr
        mr = mul_r[3]
        mi = mul_i[3] * sgn
        xr, xi = xr + mr * cr - mi * ci, xi + mr * ci + mi * cr
        sre[pl.ds(r0, SUBLANES), :] = xr
        sim[pl.ds(r0, SUBLANES), :] = xi
        edge = 0 if reverse else SUBLANES - 1
        ncr = jnp.broadcast_to(xr[edge:edge + 1, :], (SUBLANES, lanes))
        nci = jnp.broadcast_to(xi[edge:edge + 1, :], (SUBLANES, lanes))
        return ncr, nci

    zero = jnp.zeros((SUBLANES, lanes), F32)
    lax.fori_loop(0, n_groups, step, (zero, zero))


def _dot(a, b):
    return jnp.dot(a, b, preferred_element_type=F32)


def _dotf(a, b):
    return _dot(a.astype(BF16), b)


def _s5_operands(lr, li, bbr, bbi, c_re, c_im):
    g = lr.shape[0]
    nb = g // GROUPS_PER_BLOCK
    tb = lambda w: jnp.swapaxes(w, 1, 2)
    b_in = [_block_diag(tb(bbr)), _block_diag(tb(bbi))]
    c_out = [_block_diag(tb(c_re)), _block_diag(tb(c_im))]
    b_out = [_block_diag(bbr), _block_diag(bbi)]
    c_in = [_block_diag(c_re), _block_diag(c_im)]
    lam_r = lr.reshape(nb, 1, STATE_LANES)
    lam_i = li.reshape(nb, 1, STATE_LANES)
    mults = _scan_multipliers(lam_r, lam_i)
    cast = lambda ws: [w.astype(BF16) for w in ws]
    return cast(b_in), cast(c_out), cast(b_out), cast(c_in), mults


def _s5_fwd(proj, d_skip, b_in, c_out, mults, *, col0, carry=None):
    rows = proj.shape[0]
    nb = b_in[0].shape[0]
    tm = _pick(rows, 512, SUBLANES)
    n_tiles = rows // tm
    s_l = STATE_LANES

    def body(u_ref, dk_ref, br, bi, cr, ci, fr_ref, fi_ref, o_ref, sr_ref, si_ref, sre, sim):
        for t in range(n_tiles):
            rs = pl.ds(t * tm, tm)
            ub = u_ref[rs, :]
            sre[rs, :] = _dot(ub, br[...])
            sim[rs, :] = _dot(ub, bi[...])
        _scan_rows(sre, sim, fr_ref, fi_ref, rows // SUBLANES, False)
        for t in range(n_tiles):
            rs = pl.ds(t * tm, tm)
            srb = sre[rs, :].astype(BF16)
            sib = sim[rs, :].astype(BF16)
            sr_ref[rs, :] = srb
            si_ref[rs, :] = sib
            y0 = _dot(srb, cr[...]) - _dot(sib, ci[...])
            y1 = y0 + dk_ref[...] * u_ref[rs, :].astype(F32)
            o_ref[rs, :] = _gelu(y1).astype(o_ref.dtype)

    mat_in = pl.BlockSpec((None, LANES, s_l), lambda g: (g, 0, 0))
    mat_out = pl.BlockSpec((None, s_l, LANES), lambda g: (g, 0, 0))
    mul = pl.BlockSpec((None, 4, SUBLANES, s_l), lambda g: (g, 0, 0, 0))
    state = pl.BlockSpec((rows, s_l), lambda g: (0, g))
    outs, carried = _call(
        body, grid=(nb,),
        in_specs=[pl.BlockSpec((rows, LANES), lambda g: (0, col0 + g)), pl.BlockSpec((1, LANES), lambda g: (0, g))]
        + [mat_in] * 2 + [mat_out] * 2 + [mul] * 2,
        out_specs=[pl.BlockSpec((rows, LANES), lambda g: (0, g)), state, state],
        out_shape=[jax.ShapeDtypeStruct((rows, nb * LANES), BF16), jax.ShapeDtypeStruct((rows, nb * s_l), BF16),
                   jax.ShapeDtypeStruct((rows, nb * s_l), BF16)],
        scratch_shapes=[pltpu.VMEM((rows, s_l), F32), pltpu.VMEM((rows, s_l), F32)],
        semantics=("parallel",), name="s5_fwd", args=[proj, d_skip, *b_in, *c_out, mults[0], mults[1]], carry=carry)
    return outs if carry is None else (outs, carried)


def _s5_bwd(proj, dyg, d_skip, states, c_out, b_out, c_in, mults, dproj, *, col0, carry=None):
    rows = proj.shape[0]
    nb = c_out[0].shape[0]
    tm = _pick(rows, 512, SUBLANES)
    n_tiles = rows // tm
    s_l = STATE_LANES
    n_groups = rows // SUBLANES
    tn = (((0,), (0,)), ((), ()))

    def body(u_ref, dy_ref, dk_ref, sr_ref, si_ref, cr, ci, bor, boi, cir, cii, rr_ref, ri_ref, dproj_ref,
             du_ref, ddk_ref, dbr_ref, dbi_ref, dcr_ref, dci_ref, dlr_ref, dli_ref,
             gre, gim, dy1):
        ddk = jnp.zeros((1, LANES), F32)
        dcr = jnp.zeros((s_l, LANES), F32)
        dci = jnp.zeros((s_l, LANES), F32)
        for t in range(n_tiles):
            rs = pl.ds(t * tm, tm)
            srb = sr_ref[rs, :]
            sib = si_ref[rs, :]
            uf = u_ref[rs, :].astype(F32)
            y0 = _dot(srb, cr[...]) - _dot(sib, ci[...])
            y1 = y0 + dk_ref[...] * uf
            d1 = dy_ref[rs, :].astype(F32) * _gelu_grad(y1)
            dy1[rs, :] = d1
            ddk = ddk + jnp.sum(d1 * uf, axis=0, keepdims=True)
            d1b = d1.astype(BF16)
            dcr = dcr + lax.dot_general(srb, d1b, tn, preferred_element_type=F32)
            dci = dci - lax.dot_general(sib, d1b, tn, preferred_element_type=F32)
            gre[rs, :] = _dot(d1b, cir[...])
            gim[rs, :] = -_dot(d1b, cii[...])
        ddk_ref[...] = ddk
        dcr_ref[...] = dcr
        dci_ref[...] = dci

        last_row = lax.broadcasted_iota(jnp.int32, (SUBLANES, s_l), 0) == SUBLANES - 1

        def group(r0, s_r, s_i, carry):
            cr_, ci_, ar, ai = carry
            xr = gre[pl.ds(r0, SUBLANES), :]
            xi = gim[pl.ds(r0, SUBLANES), :]
            for lvl, s in enumerate((1, 2, 4)):
                nr = pltpu.roll(xr, SUBLANES - s, 0)
                ni = pltpu.roll(xi, SUBLANES - s, 0)
                mr = rr_ref[lvl]
                mi = ri_ref[lvl]
                xr, xi = xr + mr * nr + mi * ni, xi + mr * ni - mi * nr
            mr = rr_ref[3]
            mi = ri_ref[3]
            xr, xi = xr + mr * cr_ + mi * ci_, xi + mr * ci_ - mi * cr_
            gre[pl.ds(r0, SUBLANES), :] = xr
            gim[pl.ds(r0, SUBLANES), :] = xi
            nxt_r = jnp.where(last_row, cr_, pltpu.roll(xr, SUBLANES - 1, 0))
            nxt_i = jnp.where(last_row, ci_, pltpu.roll(xi, SUBLANES - 1, 0))
            ncr = jnp.broadcast_to(xr[0:1, :], (SUBLANES, s_l))
            nci = jnp.broadcast_to(xi[0:1, :], (SUBLANES, s_l))
            return ncr, nci, ar + nxt_r * s_r + nxt_i * s_i, ai + nxt_i * s_r - nxt_r * s_i

        def rev_step(k, carry):
            r0 = pl.multiple_of((n_groups // 2 - 1 - k) * 2 * SUBLANES, 2 * SUBLANES)
            s_r = sr_ref[pl.ds(r0, 2 * SUBLANES), :].astype(F32)
            s_i = si_ref[pl.ds(r0, 2 * SUBLANES), :].astype(F32)
            carry = group(r0 + SUBLANES, s_r[SUBLANES:], s_i[SUBLANES:], carry)
            return group(r0, s_r[:SUBLANES], s_i[:SUBLANES], carry)

        zero = jnp.zeros((SUBLANES, s_l), F32)
        _, _, ar, ai = lax.fori_loop(0, n_groups // 2, rev_step, (zero, zero, zero, zero))
        dlr_ref[...] = jnp.sum(ar, axis=0, keepdims=True)
        dli_ref[...] = jnp.sum(ai, axis=0, keepdims=True)

        dbr = jnp.zeros((LANES, s_l), F32)
        dbi = jnp.zeros((LANES, s_l), F32)
        for t in range(n_tiles):
            rs = pl.ds(t * tm, tm)
            gr = gre[rs, :]
            gi = gim[rs, :]
            grb = gr.astype(BF16)
            gib = gi.astype(BF16)
            du = _dot(grb, bor[...]) + _dot(gib, boi[...]) + dy1[rs, :] * dk_ref[...]
            du_ref[rs, :] = du.astype(du_ref.dtype)
            ub = u_ref[rs, :]
            dbr = dbr + lax.dot_general(ub, grb, tn, preferred_element_type=F32)
            dbi = dbi + lax.dot_general(ub, gib, tn, preferred_element_type=F32)
        dbr_ref[...] = dbr
        dbi_ref[...] = dbi

    mat_in = pl.BlockSpec((None, LANES, s_l), lambda g: (g, 0, 0))
    mat_out = pl.BlockSpec((None, s_l, LANES), lambda g: (g, 0, 0))
    mul = pl.BlockSpec((None, 4, SUBLANES, s_l), lambda g: (g, 0, 0, 0))
    lam = pl.BlockSpec((None, 1, s_l), lambda g: (g, 0, 0))
    col = pl.BlockSpec((rows, LANES), lambda g: (0, g))
    vec = pl.BlockSpec((1, LANES), lambda g: (0, g))
    state = pl.BlockSpec((rows, s_l), lambda g: (0, g))
    outs, carried = _call(
        body, grid=(nb,),
        in_specs=[pl.BlockSpec((rows, LANES), lambda g: (0, col0 + g)), col, vec]
        + [state] * 2 + [mat_out] * 2 + [mat_out] * 2 + [mat_in] * 2 + [mul] * 2
        + [pl.BlockSpec(memory_space=pl.ANY)],
        out_specs=[pl.BlockSpec((rows, LANES), lambda g: (0, col0 + g)), vec, mat_in, mat_in, mat_out, mat_out,
                   lam, lam],
        out_shape=[jax.ShapeDtypeStruct(dproj.shape, dproj.dtype), jax.ShapeDtypeStruct((1, nb * LANES), F32),
                   jax.ShapeDtypeStruct((nb, LANES, s_l), F32), jax.ShapeDtypeStruct((nb, LANES, s_l), F32),
                   jax.ShapeDtypeStruct((nb, s_l, LANES), F32), jax.ShapeDtypeStruct((nb, s_l, LANES), F32),
                   jax.ShapeDtypeStruct((nb, 1, s_l), F32), jax.ShapeDtypeStruct((nb, 1, s_l), F32)],
        scratch_shapes=[pltpu.VMEM((rows, s_l), F32)] * 2 + [pltpu.VMEM((rows, LANES), F32)],
        semantics=("parallel",), name="s5_bwd",
        args=[proj, dyg, d_skip, *states, *c_out, *b_out, *c_in, mults[2], mults[3], dproj], carry=carry,
        aliases={13: 0})
    return outs if carry is None else (outs, carried)


def _silu(v):
    return v * _sigmoid(v)


def _ada_fwd(c_all, w_shard, b_cols):
    d, n = w_shard.shape
    bn = _pick(n, 512)

    def body(c_ref, w_ref, b_ref, o_ref):
        ca = _silu(c_ref[...]).astype(BF16)
        o_ref[...] = _dot(ca, w_ref[...].astype(BF16)) + b_ref[...]

    return pl.pallas_call(
        body, grid=(n // bn,),
        in_specs=[pl.BlockSpec((N_DEV, d), lambda j: (0, 0)), pl.BlockSpec((d, bn), lambda j: (0, j)),
                  pl.BlockSpec((1, bn), lambda j: (0, j))],
        out_specs=pl.BlockSpec((N_DEV, bn), lambda j: (0, j)),
        out_shape=jax.ShapeDtypeStruct((N_DEV, n), F32),
        compiler_params=_cparams(("parallel",)), name="ada_fwd")(c_all, w_shard, b_cols)


def _ada_bwd(c_all, dmod_cols):
    d = c_all.shape[1]
    n = dmod_cols.shape[1]
    bn = _pick(n, 512)

    def body(c_ref, g_ref, o_ref):
        ca = _silu(c_ref[...]).astype(BF16)
        o_ref[...] = lax.dot_general(ca, g_ref[...].astype(BF16), (((0,), (0,)), ((), ())),
                                     preferred_element_type=F32)

    return pl.pallas_call(
        body, grid=(n // bn,),
        in_specs=[pl.BlockSpec((N_DEV, d), lambda j: (0, 0)), pl.BlockSpec((N_DEV, bn), lambda j: (0, j))],
        out_specs=pl.BlockSpec((d, bn), lambda j: (0, j)),
        out_shape=jax.ShapeDtypeStruct((d, n), F32),
        compiler_params=_cparams(("parallel",)), name="ada_bwd")(c_all, dmod_cols)


def _cast_bf16(w, *, name):
    rows, cols = w.shape
    tr = _row_tile(rows, cols, 3)

    def body(w_ref, o_ref, slot_ref):
        o_ref[...] = w_ref[...].astype(BF16)
        slot_ref[...] = w_ref[...].astype(BF16)

    row = pl.BlockSpec((tr, cols), lambda i: (i, 0))
    return pl.pallas_call(
        body, grid=(rows // tr,), in_specs=[row],
        out_specs=[row, pl.BlockSpec((None, tr, cols), lambda i: (0, i, 0))],
        out_shape=[jax.ShapeDtypeStruct((rows, cols), BF16), jax.ShapeDtypeStruct((N_CHIPS, rows, cols), BF16)],
        compiler_params=_cparams(("parallel",)), name=name)(w)


def _adamw_update(w_ref, g_ref, m_ref, v_ref, d_ref, nm_ref, nv_ref):
    c1 = 1.0 / (1.0 - ADAM_B1 ** ADAM_STEP)
    c2 = 1.0 / (1.0 - ADAM_B2 ** ADAM_STEP)
    gv = g_ref[...]
    nm = ADAM_B1 * m_ref[...] + (1.0 - ADAM_B1) * gv
    nv = ADAM_B2 * v_ref[...] + (1.0 - ADAM_B2) * (gv * gv)
    nm_ref[...] = nm
    nv_ref[...] = nv
    d_ref[...] = -ADAM_LR * ((nm * c1) / (jnp.sqrt(nv * c2) + ADAM_EPS) + ADAM_WD * w_ref[...])


def _adamw_many(ws, gs, ms, vs, *, name):
    n = len(ws)

    def body(*refs):
        for i in range(n):
            _adamw_update(*[refs[k * n + i] for k in range(7)])

    vm = pl.BlockSpec(memory_space=pltpu.VMEM)
    shapes = [jax.ShapeDtypeStruct(a.shape, F32) for a in ws]
    outs = pl.pallas_call(
        body, in_specs=[vm] * (4 * n), out_specs=[vm] * (3 * n), out_shape=shapes * 3,
        compiler_params=pltpu.CompilerParams(vmem_limit_bytes=VMEM_LIMIT_BYTES), name=name)(*ws, *gs, *ms, *vs)
    return list(outs[:n]), list(outs[n:2 * n]), list(outs[2 * n:])


def _adamw(w, g, m, v, *, name, after=None, with_grad=False):
    rows, cols = w.shape
    tr = _row_tile(rows, cols, 8)
    n_out = 4 if with_grad else 3

    def body(w_ref, g_ref, m_ref, v_ref, *rest):
        outs = rest[-n_out:]
        _adamw_update(w_ref, g_ref, m_ref, v_ref, *outs[:3])
        if with_grad:
            outs[3][...] = g_ref[...]

    row = pl.BlockSpec((tr, cols), lambda i: (i, 0))
    shp = jax.ShapeDtypeStruct((rows, cols), F32)
    extra = [] if after is None else [after]
    outs, _ = _call(
        body, grid=(rows // tr,), in_specs=[row] * 4 + [pl.BlockSpec(memory_space=pl.ANY)] * len(extra),
        out_specs=[row] * n_out, out_shape=[shp] * n_out, scratch_shapes=[], semantics=("parallel",), name=name,
        args=[w, g, m, v] + extra)
    return outs


def _sum_leading(a, *, name, out_dtype=F32):
    n, rows, cols = a.shape
    tr = _row_tile(rows, cols, n + 1)

    def body(a_ref, o_ref):
        acc = a_ref[0].astype(F32)
        for i in range(1, n):
            acc = acc + a_ref[i].astype(F32)
        o_ref[...] = acc.astype(o_ref.dtype)

    return pl.pallas_call(
        body, grid=(rows // tr,), in_specs=[pl.BlockSpec((n, tr, cols), lambda i: (0, i, 0))],
        out_specs=pl.BlockSpec((tr, cols), lambda i: (i, 0)),
        out_shape=jax.ShapeDtypeStruct((rows, cols), out_dtype),
        compiler_params=_cparams(("parallel",)), name=name)(a)


def _add_half(dw, land, my_c, *, name):
    n, r, cols = dw.shape
    h = r // 2
    tr = _row_tile(h, cols, 3)
    hb = h // tr

    def body(c_ref, a_ref, b_ref, o_ref):
        o_ref[...] = (a_ref[...].astype(F32) + b_ref[...].astype(F32)).astype(o_ref.dtype)

    gs = pltpu.PrefetchScalarGridSpec(
        num_scalar_prefetch=1, grid=(n, hb),
        in_specs=[pl.BlockSpec((None, tr, cols), lambda s, i, c_ref: (s, c_ref[0] * hb + i, 0)),
                  pl.BlockSpec((None, tr, cols), lambda s, i, c_ref: (s, i, 0))],
        out_specs=pl.BlockSpec((None, tr, cols), lambda s, i, c_ref: (s, i, 0)))
    return pl.pallas_call(
        body, grid_spec=gs, out_shape=jax.ShapeDtypeStruct((n, h, cols), BF16),
        compiler_params=_cparams(("parallel", "parallel")), name=name)(my_c, dw, land)


def _mesh_pos():
    return lax.axis_index("x"), lax.axis_index("y"), lax.axis_index("c")


def _other_chips(x, y):
    return [(1 - x, y), (x, 1 - y), (1 - x, 1 - y)]


def _gather_small(blk, *, name):
    m_per, n = blk.shape

    def body(x_ref, out_ref, send_sems, recv_sems, local_sem):
        x, y, c = _mesh_pos()
        me, sibling = (x, y, c), (x, y, 1 - c)
        chips = _other_chips(x, y)

        def rows(px, py, pc):
            return out_ref.at[pl.ds((4 * px + 2 * py + pc) * m_per, m_per), :]

        def copy(k, block, to, src=None):
            return pltpu.make_async_remote_copy(
                src_ref=rows(*block) if src is None else src, dst_ref=rows(*block),
                send_sem=send_sems.at[k], recv_sem=recv_sems.at[k], device_id=to, device_id_type=MESH)

        mine = pltpu.make_async_copy(x_ref, rows(*me), local_sem)
        mine.start()
        first = [copy(0, me, sibling, src=x_ref)]
        first += [copy(1 + j, me, (*chip, c), src=x_ref) for j, chip in enumerate(chips)]
        for cp in first:
            cp.start()
        passed = [copy(4 + j, (*chip, c), sibling) for j, chip in enumerate(chips)]
        for j, chip in enumerate(chips):
            copy(1 + j, (*chip, c), me).wait_recv()
            passed[j].start()
        copy(0, sibling, me).wait_recv()
        for j, chip in enumerate(chips):
            copy(4 + j, (*chip, 1 - c), me).wait_recv()
        for cp in first + passed:
            cp.wait_send()
        mine.wait()

    return pl.pallas_call(
        body, out_shape=jax.ShapeDtypeStruct((N_DEV * m_per, n), blk.dtype),
        in_specs=[pl.BlockSpec(memory_space=pltpu.VMEM)], out_specs=pl.BlockSpec(memory_space=pltpu.VMEM),
        scratch_shapes=[pltpu.SemaphoreType.DMA((7,)), pltpu.SemaphoreType.DMA((7,)), pltpu.SemaphoreType.DMA],
        compiler_params=pltpu.CompilerParams(vmem_limit_bytes=VMEM_LIMIT_BYTES), name=name)(blk)


def _cast_into_slot(w, chip, after, *, name):
    rows, cols = w.shape
    tr = _row_tile(rows, cols, 2)

    def body(chip_ref, w_ref, after_ref, o_ref):
        o_ref[...] = w_ref[...].astype(BF16)

    gs = pltpu.PrefetchScalarGridSpec(
        num_scalar_prefetch=1, grid=(rows // tr,),
        in_specs=[pl.BlockSpec((tr, cols), lambda i, chip_ref: (i, 0)), pl.BlockSpec(memory_space=pl.ANY)],
        out_specs=pl.BlockSpec((None, tr, cols), lambda i, chip_ref: (chip_ref[0], i, 0)))
    return pl.pallas_call(
        body, grid_spec=gs, out_shape=jax.ShapeDtypeStruct((N_CHIPS, rows, cols), BF16),
        compiler_params=_cparams(("parallel",)), name=name)(chip, w, after)


def _sum_into_half(part, landed, chip, my_c, *, name):
    _, h, cols = part.shape
    tr = _row_tile(h, cols, 5)
    hb = h // tr

    def body(chip_ref, c_ref, p_ref, l_ref, o_ref):
        acc = p_ref[...].astype(F32)
        for j in range(3):
            acc = acc + l_ref[j].astype(F32)
        o_ref[...] = acc

    gs = pltpu.PrefetchScalarGridSpec(
        num_scalar_prefetch=2, grid=(hb,),
        in_specs=[pl.BlockSpec((None, tr, cols), lambda i, chip_ref, c_ref: (chip_ref[0], i, 0)),
                  pl.BlockSpec((3, tr, cols), lambda i, chip_ref, c_ref: (0, i, 0))],
        out_specs=pl.BlockSpec((tr, cols), lambda i, chip_ref, c_ref: (c_ref[0] * hb + i, 0)))
    return pl.pallas_call(
        body, grid_spec=gs, out_shape=jax.ShapeDtypeStruct((2 * h, cols), F32),
        compiler_params=_cparams(("parallel",)), name=name)(chip, my_c, part, landed)


class _NoComm:
    def __init__(self, big):
        self.big = big
        self.grads = {}

    def weight(self, name):
        return self.big[name]

    def mm_in(self, u, afters):
        return _mm(u, self.big["w_in"], mode="nn", out_dtype=BF16, name="mm_in")

    def mm_d_in(self, dproj):
        return _mm(dproj, self.big["w_in"], mode="nt", out_dtype=F32, name="mm_d_in")

    def carry(self, site, args=()):
        return None

    def done(self, site, carried, out=None):
        return out

    def grad(self, name, dw):
        self.grads[name] = dw

    def early_grads(self, early):
        self.early = early


def _gather_rows_carry(blk):
    m_per = blk.shape[0]
    sem = pltpu.SemaphoreType.DMA((7,))

    def copies(ins, outs, sems):
        send_sems, recv_sems, local_sem = sems
        x, y, c = _mesh_pos()
        me, sibling = (x, y, c), (x, y, 1 - c)
        chips = _other_chips(x, y)

        def rows(px, py, pc):
            return outs[0].at[pl.ds((4 * px + 2 * py + pc) * m_per, m_per), :]

        def copy(k, block, to, src=None):
            return pltpu.make_async_remote_copy(
                src_ref=rows(*block) if src is None else src, dst_ref=rows(*block),
                send_sem=send_sems.at[k], recv_sem=recv_sems.at[k], device_id=to, device_id_type=MESH)

        mine = pltpu.make_async_copy(ins[0], rows(*me), local_sem.at[0])
        first = [copy(0, me, sibling, src=ins[0])]
        first += [copy(1 + j, me, (*chip, c), src=ins[0]) for j, chip in enumerate(chips)]
        passed = [copy(4 + j, (*chip, c), sibling) for j, chip in enumerate(chips)]
        landed = [copy(1 + j, (*chip, c), me) for j, chip in enumerate(chips)]
        from_sibling = [copy(0, sibling, me)] + [copy(4 + j, (*chip, 1 - c), me) for j, chip in enumerate(chips)]
        return mine, first, passed, landed, from_sibling

    def start(ins, outs, sems):
        mine, first, _, _, _ = copies(ins, outs, sems)
        mine.start()
        for cp in first:
            cp.start()

    def finish(ins, outs, sems):
        mine, first, passed, landed, from_sibling = copies(ins, outs, sems)
        for arrived, onward in zip(landed, passed):
            arrived.wait_recv()
            onward.start()
        for arrived in from_sibling:
            arrived.wait_recv()
        for cp in first + passed:
            cp.wait_send()
        mine.wait()

    shape = jax.ShapeDtypeStruct((N_DEV * m_per, blk.shape[1]), blk.dtype)
    return _Carry([blk], [shape], {}, [sem, sem, pltpu.SemaphoreType.DMA((1,))], start, finish)


def _w_in_copies(own_ref, land_ref, send_sems, recv_sems):
    x, y, c = _mesh_pos()
    h = own_ref.shape[0] // 2
    return [pltpu.make_async_remote_copy(
        src_ref=own_ref.at[pl.ds(c * h, h), :], dst_ref=land_ref.at[1 + j, pl.ds(c * h, h), :],
        send_sem=send_sems[j], recv_sem=recv_sems[j], device_id=(*chip, c), device_id_type=MESH)
        for j, chip in enumerate(_other_chips(x, y))]


def _w_in_send(own, land, after):
    hbm = pl.BlockSpec(memory_space=pltpu.HBM)
    sem = pl.BlockSpec(memory_space=pltpu.SEMAPHORE)
    land_shape = land.shape

    def body(own_ref, land_ref, after_ref, s0, s1, s2, r0, r1, r2, own_thru, land_thru, token):
        for cp in _w_in_copies(own_ref, land_ref, (s0, s1, s2), (r0, r1, r2)):
            cp.start()
        token[...] = jnp.zeros_like(token)

    outs = pl.pallas_call(
        body, name="w_in_send",
        out_shape=(pltpu.SemaphoreType.DMA(()),) * 6 + (
            pltpu.HBM(own.shape, own.dtype), pltpu.HBM(land_shape, own.dtype), jax.ShapeDtypeStruct((8, LANES), F32)),
        in_specs=(hbm, hbm, pl.BlockSpec(memory_space=pl.ANY)),
        out_specs=(sem,) * 6 + (hbm, hbm, pl.BlockSpec(memory_space=pltpu.VMEM)),
        input_output_aliases={0: 6, 1: 7},
        compiler_params=pltpu.CompilerParams(has_side_effects=pltpu.SideEffectType.DATAFLOW_SIDE_EFFECTING),
    )(pltpu.with_memory_space_constraint(own, pltpu.HBM), pltpu.with_memory_space_constraint(land, pltpu.HBM), after)
    return outs[:6], outs[6], outs[7], outs[8]


def _w_in_wait(sems, own, land, afters):
    hbm = pl.BlockSpec(memory_space=pltpu.HBM)
    sem = pl.BlockSpec(memory_space=pltpu.SEMAPHORE)
    n_after = len(afters)

    def body(own_ref, land_ref, s0, s1, s2, r0, r1, r2, *rest):
        for cp in _w_in_copies(own_ref, land_ref, (s0, s1, s2), (r0, r1, r2)):
            cp.wait_send()
            cp.wait_recv()

    return pl.pallas_call(
        body, name="w_in_wait", out_shape=(pltpu.HBM(own.shape, own.dtype), pltpu.HBM(land.shape, land.dtype)),
        in_specs=(hbm, hbm) + (sem,) * 6 + (pl.BlockSpec(memory_space=pl.ANY),) * n_after, out_specs=(hbm, hbm),
        input_output_aliases={0: 0, 1: 1},
        compiler_params=pltpu.CompilerParams(has_side_effects=pltpu.SideEffectType.DATAFLOW_SIDE_EFFECTING),
    )(own, land, *sems, *afters)


def _exchange_copies(part_refs, land_refs, send_sems, recv_sems):
    x, y, c = _mesh_pos()
    cps = []
    for w, (part, land) in enumerate(zip(part_refs, land_refs)):
        for j, chip in enumerate(_other_chips(x, y)):
            cps.append(pltpu.make_async_remote_copy(
                src_ref=part.at[2 * chip[0] + chip[1]], dst_ref=land.at[j],
                send_sem=send_sems[3 * w + j], recv_sem=recv_sems[3 * w + j],
                device_id=(*chip, c), device_id_type=MESH))
    return cps


def _exchange_send(parts, through, *, name):
    n = len(parts)
    hbm = pl.BlockSpec(memory_space=pltpu.HBM)
    sem = pl.BlockSpec(memory_space=pltpu.SEMAPHORE)
    any_spec = pl.BlockSpec(memory_space=pl.ANY)
    land_shapes = [(3,) + p.shape[1:] for p in parts]

    def body(*refs):
        part_refs, land_refs = refs[:n], refs[n:2 * n]
        sems = refs[2 * n + 1:8 * n + 1]
        for cp in _exchange_copies(part_refs, land_refs, sems[:3 * n], sems[3 * n:]):
            cp.start()

    outs = pl.pallas_call(
        body, name=name,
        out_shape=(pltpu.SemaphoreType.DMA(()),) * (6 * n)
        + tuple(pltpu.HBM(p.shape, p.dtype) for p in parts)
        + tuple(pltpu.HBM(s, p.dtype) for s, p in zip(land_shapes, parts))
        + (jax.ShapeDtypeStruct(through.shape, through.dtype),),
        in_specs=(hbm,) * (2 * n) + (any_spec,), out_specs=(sem,) * (6 * n) + (hbm,) * (2 * n) + (any_spec,),
        input_output_aliases={i: 6 * n + i for i in range(2 * n + 1)},
        compiler_params=pltpu.CompilerParams(has_side_effects=pltpu.SideEffectType.DATAFLOW_SIDE_EFFECTING),
    )(*[pltpu.with_memory_space_constraint(p, pltpu.HBM) for p in parts],
      *[pltpu.with_memory_space_constraint(lax.empty(s, p.dtype), pltpu.HBM) for s, p in zip(land_shapes, parts)],
      through)
    return outs[:6 * n], outs[6 * n:7 * n], outs[7 * n:8 * n], outs[8 * n]


def _exchange_wait(sems, parts, lands, afters, *, name):
    n = len(parts)
    hbm = pl.BlockSpec(memory_space=pltpu.HBM)
    sem = pl.BlockSpec(memory_space=pltpu.SEMAPHORE)

    def body(*refs):
        part_refs, land_refs = refs[:n], refs[n:2 * n]
        sem_refs = refs[2 * n:8 * n]
        for cp in _exchange_copies(part_refs, land_refs, sem_refs[:3 * n], sem_refs[3 * n:]):
            cp.wait_send()
            cp.wait_recv()

    outs = pl.pallas_call(
        body, name=name,
        out_shape=tuple(pltpu.HBM(p.shape, p.dtype) for p in parts) + tuple(pltpu.HBM(l.shape, l.dtype) for l in lands),
        in_specs=(hbm,) * (2 * n) + (sem,) * (6 * n) + (pl.BlockSpec(memory_space=pl.ANY),) * len(afters),
        out_specs=(hbm,) * (2 * n), input_output_aliases={i: i for i in range(2 * n)},
        compiler_params=pltpu.CompilerParams(has_side_effects=pltpu.SideEffectType.DATAFLOW_SIDE_EFFECTING),
    )(*parts, *lands, *sems, *afters)
    return outs[:n], outs[n:]


def _join_copies(full_refs, send_sems, recv_sems):
    x, y, c = _mesh_pos()
    cps = []
    for w, full in enumerate(full_refs):
        h = full.shape[0] // 2
        mine = full.at[pl.ds(c * h, h), :]
        cps.append(pltpu.make_async_remote_copy(
            src_ref=mine, dst_ref=mine, send_sem=send_sems[w], recv_sem=recv_sems[w],
            device_id=(x, y, 1 - c), device_id_type=MESH))
    return cps


def _join_send(fulls, *, name):
    n = len(fulls)
    hbm = pl.BlockSpec(memory_space=pltpu.HBM)
    sem = pl.BlockSpec(memory_space=pltpu.SEMAPHORE)

    def body(*refs):
        sems = refs[n:3 * n]
        for cp in _join_copies(refs[:n], sems[:n], sems[n:]):
            cp.start()
        token = refs[-1]
        token[...] = jnp.zeros_like(token)

    outs = pl.pallas_call(
        body, name=name,
        out_shape=(pltpu.SemaphoreType.DMA(()),) * (2 * n) + tuple(pltpu.HBM(f.shape, f.dtype) for f in fulls)
        + (jax.ShapeDtypeStruct((SUBLANES, LANES), F32),),
        in_specs=(hbm,) * n,
        out_specs=(sem,) * (2 * n) + (hbm,) * n + (pl.BlockSpec(memory_space=pltpu.VMEM),),
        input_output_aliases={i: 2 * n + i for i in range(n)},
        compiler_params=pltpu.CompilerParams(has_side_effects=pltpu.SideEffectType.DATAFLOW_SIDE_EFFECTING),
    )(*[pltpu.with_memory_space_constraint(f, pltpu.HBM) for f in fulls])
    return outs[:2 * n], list(outs[2 * n:3 * n]), outs[3 * n]


def _join_wait(sems, fulls, afters, *, name):
    n = len(fulls)
    hbm = pl.BlockSpec(memory_space=pltpu.HBM)
    sem = pl.BlockSpec(memory_space=pltpu.SEMAPHORE)

    def body(*refs):
        sem_refs = refs[n:3 * n]
        for cp in _join_copies(refs[:n], sem_refs[:n], sem_refs[n:]):
            cp.wait_send()
            cp.wait_recv()

    outs = pl.pallas_call(
        body, name=name, out_shape=tuple(pltpu.HBM(f.shape, f.dtype) for f in fulls),
        in_specs=(hbm,) * n + (sem,) * (2 * n) + (pl.BlockSpec(memory_space=pl.ANY),) * len(afters),
        out_specs=(hbm,) * n, input_output_aliases={i: i for i in range(n)},
        compiler_params=pltpu.CompilerParams(has_side_effects=pltpu.SideEffectType.DATAFLOW_SIDE_EFFECTING),
    )(*fulls, *sems, *afters)
    return list(outs)


def _gather_ici_copies(buf_refs, send_sems, recv_sems):
    x, y, c = _mesh_pos()
    me_chip = 2 * x + y
    cps = []
    for w, buf in enumerate(buf_refs):
        h = buf.shape[1] // 2
        ref = buf.at[me_chip, pl.ds(c * h, h), :]
        for j, chip in enumerate(_other_chips(x, y)):
            cps.append(pltpu.make_async_remote_copy(
                src_ref=ref, dst_ref=ref, send_sem=send_sems[3 * w + j], recv_sem=recv_sems[3 * w + j],
                device_id=(*chip, c), device_id_type=MESH))
    return cps


def _gather_send(bufs, *, name):
    n = len(bufs)
    hbm = pl.BlockSpec(memory_space=pltpu.HBM)
    sem = pl.BlockSpec(memory_space=pltpu.SEMAPHORE)

    def body(*refs):
        sems = refs[n:7 * n]
        for cp in _gather_ici_copies(refs[:n], sems[:3 * n], sems[3 * n:]):
            cp.start()

    outs = pl.pallas_call(
        body, name=name,
        out_shape=(pltpu.SemaphoreType.DMA(()),) * (6 * n) + tuple(pltpu.HBM(b.shape, b.dtype) for b in bufs),
        in_specs=(hbm,) * n, out_specs=(sem,) * (6 * n) + (hbm,) * n,
        input_output_aliases={i: 6 * n + i for i in range(n)},
        compiler_params=pltpu.CompilerParams(has_side_effects=pltpu.SideEffectType.DATAFLOW_SIDE_EFFECTING),
    )(*[pltpu.with_memory_space_constraint(b, pltpu.HBM) for b in bufs])
    send_sems, recv_sems = outs[:3 * n], outs[3 * n:6 * n]
    per_buf = [tuple(send_sems[3 * w:3 * w + 3]) + tuple(recv_sems[3 * w:3 * w + 3]) for w in range(n)]
    return per_buf, list(outs[6 * n:])


def _gather_wait(sems, bufs, afters, *, name):
    n = len(bufs)
    hbm = pl.BlockSpec(memory_space=pltpu.HBM)
    sem = pl.BlockSpec(memory_space=pltpu.SEMAPHORE)
    flat = [s for six in sems for s in six[:3]] + [s for six in sems for s in six[3:]]

    def body(*refs):
        sem_refs = refs[n:7 * n]
        for cp in _gather_ici_copies(refs[:n], sem_refs[:3 * n], sem_refs[3 * n:]):
            cp.wait_send()
            cp.wait_recv()

    outs = pl.pallas_call(
        body, name=name, out_shape=tuple(pltpu.HBM(b.shape, b.dtype) for b in bufs),
        in_specs=(hbm,) * n + (sem,) * (6 * n) + (pl.BlockSpec(memory_space=pl.ANY),) * len(afters),
        out_specs=(hbm,) * n, input_output_aliases={i: i for i in range(n)},
        compiler_params=pltpu.CompilerParams(has_side_effects=pltpu.SideEffectType.DATAFLOW_SIDE_EFFECTING),
    )(*bufs, *flat, *afters)
    return list(outs)


def _forward_abs_carry(bufs):
    n = len(bufs)
    sem = pltpu.SemaphoreType.DMA((3 * n,))

    def copies(outs, sems):
        send_sems, recv_sems = sems
        x, y, c = _mesh_pos()
        sends, recvs = [], []
        for w in range(n):
            h = bufs[w].shape[1] // 2
            for j, chip in enumerate(_other_chips(x, y)):
                slot = 2 * chip[0] + chip[1]
                mine = outs[w].at[slot, pl.ds(c * h, h), :]
                other = outs[w].at[slot, pl.ds((1 - c) * h, h), :]
                sends.append(pltpu.make_async_remote_copy(
                    src_ref=mine, dst_ref=mine, send_sem=send_sems.at[3 * w + j], recv_sem=recv_sems.at[3 * w + j],
                    device_id=(x, y, 1 - c), device_id_type=MESH))
                recvs.append(pltpu.make_async_remote_copy(
                    src_ref=other, dst_ref=other, send_sem=send_sems.at[3 * w + j], recv_sem=recv_sems.at[3 * w + j],
                    device_id=(x, y, c), device_id_type=MESH))
        return sends, recvs

    def start(ins, outs, sems):
        for cp in copies(outs, sems)[0]:
            cp.start()

    def finish(ins, outs, sems):
        sends, recvs = copies(outs, sems)
        for cp in recvs:
            cp.wait_recv()
        for cp in sends:
            cp.wait_send()

    shapes = [jax.ShapeDtypeStruct(b.shape, b.dtype) for b in bufs]
    return _Carry(bufs, shapes, {i: i for i in range(n)}, [sem, sem], start, finish)


def _forward_carry(land):
    n = land.shape[0] - 1
    h = land.shape[1] // 2
    sem = pltpu.SemaphoreType.DMA((n,))

    def copies(outs, sems):
        send_sems, recv_sems = sems
        x, y, c = _mesh_pos()
        sends, recvs = [], []
        for j in range(n):
            mine = outs[0].at[1 + j, pl.ds(c * h, h), :]
            other = outs[0].at[1 + j, pl.ds((1 - c) * h, h), :]
            sends.append(pltpu.make_async_remote_copy(
                src_ref=mine, dst_ref=mine, send_sem=send_sems.at[j], recv_sem=recv_sems.at[j],
                device_id=(x, y, 1 - c), device_id_type=MESH))
            recvs.append(pltpu.make_async_remote_copy(
                src_ref=other, dst_ref=other, send_sem=send_sems.at[j], recv_sem=recv_sems.at[j],
                device_id=(x, y, c), device_id_type=MESH))
        return sends, recvs

    def start(ins, outs, sems):
        for cp in copies(outs, sems)[0]:
            cp.start()

    def finish(ins, outs, sems):
        sends, recvs = copies(outs, sems)
        for cp in recvs:
            cp.wait_recv()
        for cp in sends:
            cp.wait_send()

    return _Carry([land], [jax.ShapeDtypeStruct(land.shape, land.dtype)], {0: 0}, [sem, sem], start, finish)


def _swap_carry(dws):
    n = len(dws)
    sem = pltpu.SemaphoreType.DMA((n,))

    def copies(ins, outs, sems):
        send_sems, recv_sems = sems
        x, y, c = _mesh_pos()
        cps = []
        for w in range(n):
            h = dws[w].shape[1] // 2
            cps.append(pltpu.make_async_remote_copy(
                src_ref=ins[w].at[:, pl.ds((1 - c) * h, h), :], dst_ref=outs[w],
                send_sem=send_sems.at[w], recv_sem=recv_sems.at[w],
                device_id=(x, y, 1 - c), device_id_type=MESH))
        return cps

    def start(ins, outs, sems):
        for cp in copies(ins, outs, sems):
            cp.start()

    def finish(ins, outs, sems):
        for cp in copies(ins, outs, sems):
            cp.wait()

    shapes = [jax.ShapeDtypeStruct((s.shape[0], s.shape[1] // 2, s.shape[2]), s.dtype) for s in dws]
    return _Carry(dws, shapes, {}, [sem, sem], start, finish)


def _merge_carries(carries):
    if len(carries) == 1:
        return carries[0]
    inputs, out_shapes, sem_shapes, aliases, spans = [], [], [], {}, []
    for cy in carries:
        i0, o0, s0 = len(inputs), len(out_shapes), len(sem_shapes)
        aliases.update({i0 + i: o0 + o for i, o in cy.aliases.items()})
        inputs += cy.inputs
        out_shapes += cy.out_shapes
        sem_shapes += cy.sem_shapes
        spans.append((slice(i0, len(inputs)), slice(o0, len(out_shapes)), slice(s0, len(sem_shapes))))

    def start(ins, outs, sems):
        for cy, (si, so, ss) in zip(carries, spans):
            cy.start(ins[si], outs[so], sems[ss])

    def finish(ins, outs, sems):
        for cy, (si, so, ss) in zip(carries, spans):
            cy.finish(ins[si], outs[so], sems[ss])

    return _Carry(inputs, out_shapes, aliases, sem_shapes, start, finish)


class _MeshComm:
    FORWARD_AT = {
        "conv_fwd": ["w_conv_out", "w_glu_a", "w_glu_b", "w_out"],
        "mm_out": ["w_ff1"],
        "mm_ff1": ["w_ff2"],
    }
    SWAP_AT = {
        "mm_d_ff2": ["w_ff2"],
        "mm_d_ff1": ["w_ff1"],
        "conv_bwd": ["w_out", "w_glu_a", "w_glu_b", "w_conv_out"],
    }
    EARLY_AT = "mm_dw_in"

    def __init__(self, shards, pos, chip, my_c):
        self.pos = pos
        self.chip = chip
        self.my_c = my_c
        self.shards = shards
        self.w_in_own, self.w_in_rel = _cast_bf16(shards["w_in"], name="cast_w_in")
        self.raw = {}
        self.flights = []
        self.halves = {}
        self.pending = {}

    def weight(self, name):
        g = self.bufs[name]
        return g.reshape(g.shape[0] * g.shape[1], g.shape[2]) if name in ROW_SHARDED else g

    def _slot_ids(self):
        x, y, _ = self.pos
        ids = [2 * x + y] + [2 * cx + cy for cx, cy in _other_chips(x, y)]
        return jnp.stack(ids).astype(jnp.int32)

    def start_w_in(self, after):
        *self.w_in_flight, token = _w_in_send(self.w_in_own, self.w_in_rel, after)
        order = [n for names in self.FORWARD_AT.values() for n in names]
        casts = [_cast_into_slot(self.shards[n], self.chip, token, name="cast_" + n) for n in order]
        sems, bufs = _gather_send(casts, name="gather_send")
        self.bufs = dict(zip(order, bufs))
        self.gather_sems = dict(zip(order, sems))
        return token

    def mm_in(self, u, afters):
        ids = self._slot_ids()
        sems, own, land = self.w_in_flight
        proj = _mm_slots(u, own[None], ids[0:1], None, name="mm_in_own")
        own, land = _w_in_wait(sems, own, land, [proj] + list(self.bufs.values()) + list(afters))
        land, = _run_carry(_forward_carry(land), name="forward_w_in")
        proj = _mm_slots(u, land, ids[1:4], proj, name="mm_in_rest", first=1)
        self.w_in_rel = land
        return proj

    def _add_and_send(self, names, landed, site, through):
        parts = [_add_half(self.raw.pop(n), l1, self.my_c, name="add_half_" + n) for n, l1 in zip(names, landed)]
        sems, parts, lands, through = _exchange_send(parts, through, name="exchange_send_" + site)
        self.flights.append((names, sems, parts, lands))
        return through

    def mm_d_in(self, dproj):
        landed = _run_carry(_swap_carry([self.raw["w_in"]]), name="swap_halves_w_in")
        dproj = self._add_and_send(["w_in"], landed, "w_in", dproj)
        return _mm(dproj, self.w_in_rel, mode="nt", out_dtype=F32, name="mm_d_in", a_slots=self._slot_ids())

    def early_grads(self, early):
        self.early = early

    def carry(self, site, args=()):
        jobs = []
        if site in self.FORWARD_AT:
            names = self.FORWARD_AT[site]
            landed = _gather_wait([self.gather_sems.pop(n) for n in names], [self.bufs[n] for n in names],
                                  [args[0]], name="gather_wait_" + site)
            jobs.append(("forward", names, _forward_abs_carry(landed)))
        if site == self.EARLY_AT:
            flat, self.early_offs = _pack(list(self.early.values()))
            jobs.append(("early", None, _gather_rows_carry(flat.reshape(-1, PACK_COLS))))
        if site in self.SWAP_AT:
            names = self.SWAP_AT[site]
            jobs.append(("swap", names, _swap_carry([self.raw[n] for n in names])))
        if not jobs:
            return None
        self.pending[site] = jobs
        return _merge_carries([job[2] for job in jobs])

    def done(self, site, carried, out=None):
        pos = 0
        for kind, items, carry in self.pending.pop(site):
            outs = carried[pos:pos + len(carry.out_shapes)]
            pos += len(carry.out_shapes)
            if kind == "early":
                self.early_all = outs[0]
            elif kind == "forward":
                self.bufs.update(zip(items, outs))
            elif isinstance(out, (list, tuple)):
                out = [self._add_and_send(items, outs, site, out[0])] + list(out[1:])
            else:
                out = self._add_and_send(items, outs, site, out)
        return out

    def grad(self, name, dw):
        if name in ROW_SHARDED:
            dw = dw.reshape(N_CHIPS, dw.shape[0] // N_CHIPS, dw.shape[1])
        self.raw[name] = dw

    def join_start(self, names, afters):
        for i, (group, sems, parts, lands) in enumerate(self.flights):
            parts, lands = _exchange_wait(sems, parts, lands, afters, name="exchange_wait_%d" % i)
            for n, part, land in zip(group, parts, lands):
                self.halves[n] = _sum_into_half(part, land, self.chip, self.my_c, name="sum_chips_" + n)
        self.flights = []
        sems, fulls, token = _join_send([self.halves.pop(n) for n in names], name="join_send")
        self.join_flight = (names, sems, fulls)
        return token

    def join_finish(self, afters):
        names, sems, fulls = self.join_flight
        return dict(zip(names, _join_wait(sems, fulls, afters, name="join_wait")))


def _local_step(x, target, mod, small, comm):
    rows, d = x.shape
    cw = d // 2
    shift1, scale1, gate1, shift2, scale2, gate2 = mod
    _, _, bbr, bbi = small["s5_disc"]
    b_in, c_out, b_out, c_in, mults = _s5_operands(*small["s5_loglam"], bbr, bbi, small["c_re"], small["c_im"])
    wt = comm.weight

    def riding(site, fn, *args, **kwargs):
        carry = comm.carry(site, args)
        if carry is None:
            return fn(*args, **kwargs)
        out, carried = fn(*args, carry=carry, **kwargs)
        return comm.done(site, carried, out)

    u = _norm_mod(x, small["norm1_g"], scale1, shift1, name="norm1_fwd")
    proj = comm.mm_in(u, [*b_in, *c_out, *b_out, *c_in, *mults])
    sl, cv = riding("conv_fwd", _conv_fwd, proj, small["w_dw"], small["b_dw"], small["ln_g"], small["ln_b"], cw=cw)
    y_conv = _mm(sl, wt("w_conv_out"), mode="nn", out_dtype=BF16, name="mm_conv_out")
    yg, st_re, st_im = riding("s5_fwd", _s5_fwd, proj, small["d_skip"], b_in, c_out, mults, col0=2 * cw // LANES)
    ya = riding("mm_glu_a", _mm, yg, wt("w_glu_a"), mode="nn", out_dtype=BF16, name="mm_glu_a")
    yb = riding("mm_glu_b", _mm, yg, wt("w_glu_b"), mode="nn", out_dtype=BF16, name="mm_glu_b")
    merged = _merge_fwd(proj, y_conv, ya, yb, cw=cw)
    mo = riding("mm_out", _mm, merged, wt("w_out"), mode="nn", out_dtype=BF16, name="mm_out")
    h1, z = _res_norm(x, mo, gate1, small["norm2_g"], scale2, shift2)
    f1 = riding("mm_ff1", _mm, z, wt("w_ff1"), mode="nn", out_dtype=BF16, name="mm_ff1")
    ff = _mm(f1, wt("w_ff2"), mode="nn", out_dtype=BF16, name="mm_ff2", a_fn=_relu2_bf16)
    dh2, dff, loss, d_final_g, d_gate2 = _final_fwd_bwd(h1, ff, gate2, small["final_g"], target)

    comm.grad("w_ff2", _mm(f1, dff, mode="tn", out_dtype=BF16, name="mm_dw_ff2", a_fn=_relu2_bf16))
    df1 = riding("mm_d_ff2", _mm, dff, wt("w_ff2"), mode="nt", out_dtype=BF16, name="mm_d_ff2", extra=f1,
                 epi=lambda acc, f: acc * (2.0 * jnp.maximum(f.astype(F32), 0.0)))
    comm.grad("w_ff1", riding("mm_dw_ff1", _mm, z, df1, mode="tn", out_dtype=BF16, name="mm_dw_ff1",
                              out_gathered=True))
    dz = riding("mm_d_ff1", _mm, df1, wt("w_ff1"), mode="nt", out_dtype=F32, name="mm_d_ff1")
    dh1, d_shift2, d_scale2, d_norm2_g, dmo, d_gate1 = riding(
        "norm2_bwd", _norm_mod_bwd, dz, h1, dh2, small["norm2_g"], scale2, gate1, mo, name="norm2_bwd")
    comm.grad("w_out", riding("mm_dw_out", _mm, merged, dmo, mode="tn", out_dtype=BF16, name="mm_dw_out"))
    dmerged = riding("mm_d_out", _mm, dmo, wt("w_out"), mode="nt", out_dtype=BF16, name="mm_d_out")
    dproj, dy_conv, dya, dyb = riding("merge_bwd", _merge_bwd, dmerged, proj, y_conv, ya, yb, cw=cw)
    comm.grad("w_glu_a", _mm(yg, dya, mode="tn", out_dtype=BF16, name="mm_dw_glu_a", out_gathered=True))
    comm.grad("w_glu_b", _mm(yg, dyb, mode="tn", out_dtype=BF16, name="mm_dw_glu_b", out_gathered=True))
    dyg_a = _mm(dya, wt("w_glu_a"), mode="nt", out_dtype=F32, name="mm_d_glu_a")
    dyg = _mm(dyb, wt("w_glu_b"), mode="nt", out_dtype=F32, name="mm_d_glu_b", extra=dyg_a,
              epi=lambda acc, e: acc + e)
    comm.grad("w_conv_out", _mm(sl, dy_conv, mode="tn", out_dtype=BF16, name="mm_dw_conv_out", out_gathered=True))
    dsl = _mm(dy_conv, wt("w_conv_out"), mode="nt", out_dtype=F32, name="mm_d_conv_out")
    dcv, d_ln_g, d_ln_b = _ln_bwd(dsl, cv, small["ln_g"], small["ln_b"])
    dproj, d_w_dw, d_b_dw = riding("conv_bwd", _conv_bwd, dcv, proj, small["w_dw"], dproj, cw=cw)
    dproj, d_d_skip, dbr, dbi, dcr, dci, dlr, dli = riding(
        "s5_bwd", _s5_bwd, proj, dyg, small["d_skip"], (st_re, st_im), c_out, b_out, c_in, mults, dproj,
        col0=2 * cw // LANES)
    sw = lambda m: jnp.swapaxes(m, 1, 2)
    early = {
        "dmod_tail": jnp.concatenate([d_gate1, d_shift2, d_scale2, d_gate2], axis=1), "loss": loss[:, 0:1],
        "w_dw": d_w_dw, "b_dw": d_b_dw, "ln_g": d_ln_g, "ln_b": d_ln_b,
        "lam_re": dlr.reshape(-1, SSM_STATE), "lam_im": dli.reshape(-1, SSM_STATE),
        "bb_re": sw(_block_diag_extract(dbr, SSM_GROUP, SSM_STATE)),
        "bb_im": sw(_block_diag_extract(dbi, SSM_GROUP, SSM_STATE)),
        "c_re": sw(_block_diag_extract(dcr, SSM_STATE, SSM_GROUP)),
        "c_im": sw(_block_diag_extract(dci, SSM_STATE, SSM_GROUP)),
        "d_skip": d_d_skip, "norm2_g": d_norm2_g, "final_g": d_final_g,
    }
    comm.early_grads(early)
    comm.grad("w_in", riding("mm_dw_in", _mm, u, dproj, mode="tn", out_dtype=BF16, name="mm_dw_in",
                             out_gathered=True))
    du = comm.mm_d_in(dproj)
    grad_x, d_shift1, d_scale1, d_norm1_g = riding(
        "norm1_bwd", _norm_mod_bwd, du, x, dh1, small["norm1_g"], scale1, None, None, name="norm1_bwd")
    late ={"dmod_head": jnp.concatenate([d_shift1, d_scale1], axis=1), "norm1_g": d_norm1_g}
    return grad_x, early, late


WEIGHT_NAMES = ["w_ada", "b_ada", "norm1_g", "w_in", "w_dw", "b_dw", "ln_g", "ln_b", "w_conv_out", "a_re", "a_im",
                "log_dt", "b_re", "b_im", "c_re", "c_im", "d_skip", "w_glu_a", "w_glu_b", "w_out", "norm2_g",
                "w_ff1", "w_ff2", "final_g"]
BIG_NAMES = ["w_in", "w_conv_out", "w_glu_a", "w_glu_b", "w_out", "w_ff1", "w_ff2"]
ROW_SHARDED = ("w_out", "w_ff2")
PACK_COLS = 1024
PACK_TILE = SUBLANES * PACK_COLS


def _pack(arrays):
    flats = [a.reshape(-1) for a in arrays]
    offs = []
    total = 0
    for f in flats:
        offs.append(total)
        total += f.shape[0]
    pad = (-total) % PACK_TILE
    if pad:
        flats.append(jnp.zeros((pad,), F32))
    return jnp.concatenate(flats), offs


def _unpack(flat, offs, like):
    return [flat[o:o + a.size].reshape(a.shape) for o, a in zip(offs, like)]


def kernel(x, c, w_ada, b_ada, norm1_g, w_in, w_dw, b_dw, ln_g, ln_b, w_conv_out, a_re, a_im, log_dt, b_re, b_im, c_re, c_im, d_skip, w_glu_a, w_glu_b, w_out, norm2_g, w_ff1, w_ff2, final_g, loss_target, m_w_ada, m_b_ada, m_norm1_g, m_w_in, m_w_dw, m_b_dw, m_ln_g, m_ln_b, m_w_conv_out, m_a_re, m_a_im, m_log_dt, m_b_re, m_b_im, m_c_re, m_c_im, m_d_skip, m_w_glu_a, m_w_glu_b, m_w_out, m_norm2_g, m_w_ff1, m_w_ff2, m_final_g, v_w_ada, v_b_ada, v_norm1_g, v_w_in, v_w_dw, v_b_dw, v_ln_g, v_ln_b, v_w_conv_out, v_a_re, v_a_im, v_log_dt, v_b_re, v_b_im, v_c_re, v_c_im, v_d_skip, v_w_glu_a, v_w_glu_b, v_w_out, v_norm2_g, v_w_ff1, v_w_ff2, v_final_g):
    given = dict(locals())
    w = {n: given[n] for n in WEIGHT_NAMES}
    m = {n: given["m_" + n] for n in WEIGHT_NAMES}
    v = {n: given["v_" + n] for n in WEIGHT_NAMES}
    d = x.shape[2]
    xi, yi, ci = _mesh_pos()
    chip = 2 * xi + yi
    dev = 4 * xi + 2 * yi + ci
    my_c = jnp.reshape(ci, (1,)).astype(jnp.int32)
    chip_arr = jnp.reshape(chip, (1,)).astype(jnp.int32)

    comm = _MeshComm({n: w[n][0] for n in BIG_NAMES}, (xi, yi, ci), chip_arr, my_c)

    ndw = w_dw.shape[2]
    assert d // SUBLANES == ndw
    first = jnp.concatenate([c.reshape(SUBLANES, ndw), jnp.pad(w_dw[0], ((0, HALO - CONV_KERNEL), (0, 0)))])
    first_all = _gather_small(first, name="gather_c_w_dw").reshape(N_DEV, SUBLANES + HALO, ndw)
    c_all = first_all[:, :SUBLANES].reshape(N_DEV, d)
    taps = first_all.reshape(N_CHIPS, 2, SUBLANES + HALO, ndw)[:, 0, SUBLANES:SUBLANES + CONV_KERNEL]
    w_dw_full = jnp.moveaxis(taps, 0, 1).reshape(CONV_KERNEL, N_CHIPS * ndw)

    nmod = w_ada.shape[2]
    b_cols = lax.dynamic_slice(b_ada, (0, chip * nmod), (1, nmod))
    mod_part = _ada_fwd(c_all, w_ada[0], b_cols)
    mod_all = _gather_small(mod_part, name="gather_mod").reshape(N_CHIPS, 2, N_DEV, nmod)[:, 0]
    mod_full = jnp.moveaxis(mod_all, 0, 1).reshape(N_DEV, N_CHIPS * nmod)
    mod_row = lax.dynamic_slice(mod_full, (dev, 0), (1, N_CHIPS * nmod))
    mod = [mod_row[:, i * d:(i + 1) * d] for i in range(6)]

    token = comm.start_w_in(mod_row)
    log_dt_0 = log_dt[0] + token[0, 0]

    disc_in = (a_re[0], a_im[0], log_dt_0, b_re[0], b_im[0])
    disc, disc_vjp = jax.vjp(_s5_discretise, *disc_in)
    dt = jnp.exp(log_dt_0)[:, None]
    small = {"norm1_g": norm1_g, "w_dw": w_dw_full, "b_dw": b_dw, "ln_g": ln_g, "ln_b": ln_b,
             "c_re": c_re[0], "c_im": c_im[0], "d_skip": d_skip, "norm2_g": norm2_g,
             "final_g": final_g[None, :], "s5_disc": disc, "s5_loglam": (a_re[0] * dt, a_im[0] * dt)}

    grad_x, early, late = _local_step(x[0], loss_target[0], mod, small, comm)
    grads = {}

    early_all = comm.early_all.reshape(N_DEV, -1, PACK_COLS)
    early_sum = _sum_leading(early_all, name="sum_small_grads").reshape(-1)
    summed = dict(zip(early, _unpack(early_sum, comm.early_offs, list(early.values()))))
    flat, late_offs = _pack(list(late.values()))
    late_all = _gather_small(flat.reshape(-1, PACK_COLS), name="gather_late_grads").reshape(N_DEV, -1, PACK_COLS)
    late_sum = _sum_leading(late_all, name="sum_late_grads").reshape(-1)
    summed.update(zip(late, _unpack(late_sum, late_offs, list(late.values()))))
    head = late_all[:, :2 * d // PACK_COLS].reshape(N_DEV, 2 * d)
    tail = early_all[:, :4 * d // PACK_COLS].reshape(N_DEV, 4 * d)
    dmod_all = jnp.concatenate([head, tail], axis=1)

    grads["w_ada"] = _ada_bwd(c_all, lax.dynamic_slice(dmod_all, (0, chip * nmod), (N_DEV, nmod)))
    grads["b_ada"] = _sum_leading(dmod_all.reshape(N_DEV, SUBLANES, 6 * d // SUBLANES),
                                  name="sum_b_ada").reshape(1, 6 * d)
    da_re, da_im, dlog_dt, db_re, db_im = disc_vjp(
        (summed["lam_re"], summed["lam_im"], summed["bb_re"], summed["bb_im"]))
    grads.update({
        "norm1_g": summed["norm1_g"], "w_dw": lax.dynamic_slice(summed["w_dw"], (0, chip * ndw), (CONV_KERNEL, ndw)),
        "b_dw": summed["b_dw"], "ln_g": summed["ln_g"], "ln_b": summed["ln_b"],
        "a_re": da_re, "a_im": da_im, "log_dt": dlog_dt, "b_re": db_re, "b_im": db_im,
        "c_re": summed["c_re"], "c_im": summed["c_im"], "d_skip": summed["d_skip"],
        "norm2_g": summed["norm2_g"], "final_g": summed["final_g"],
    })

    delta, new_m, new_v = {}, {}, {}

    def adam_big(n, after=None):
        shp = w[n].shape
        two_d = lambda a: a.reshape(shp[1], shp[2])
        res = _adamw(two_d(w[n]), two_d(grads[n]), two_d(m[n]), two_d(v[n]), name="adamw_" + n, after=after,
                     with_grad=n in BIG_NAMES)
        delta[n], new_m[n], new_v[n] = [r.reshape(shp) for r in res[:3]]
        if n in BIG_NAMES:
            grads[n] = res[3]

    token = comm.join_start(BIG_NAMES, [late_all])
    adam_big("w_ada", token)
    grads.update(comm.join_finish([delta["w_ada"]]))
    for n in BIG_NAMES:
        adam_big(n)
    grads = {n: grads[n].reshape(w[n].shape) for n in WEIGHT_NAMES}
    rest = [n for n in WEIGHT_NAMES if n not in delta]
    as_2d = lambda a: a.reshape(1, -1) if a.ndim == 1 else a
    outs = _adamw_many(*[[as_2d(src[n]) for n in rest] for src in (w, grads, m, v)], name="adamw_small")
    for dst, arrays in zip((delta, new_m, new_v), outs):
        for n, a in zip(rest, arrays):
            dst[n] = a.reshape(w[n].shape)

    return (summed["loss"].reshape(()), grad_x[None], *[grads[n] for n in WEIGHT_NAMES],
            *[delta[n] for n in WEIGHT_NAMES], *[new_m[n] for n in WEIGHT_NAMES],
            *[new_v[n] for n in WEIGHT_NAMES])
```
